```python
import jax, jax.numpy as jnp
from jax import lax
import numpy as np

D_MODEL = 1024
BATCH = 8
SEQ = 2048
DEPTH = 1

D_POOL = D_MODEL
POOL_WINDOWS = (2, 4, 8, 16)
N_POOL_GROUPS = len(POOL_WINDOWS)
POOL_GROUP = D_POOL // N_POOL_GROUPS
D_REC = D_MODEL
HEAD_DIM = 128
N_REC_HEADS = D_REC // HEAD_DIM
CHUNK = 64
D_MIX = D_POOL + D_REC
D_IN_PROJ = 2 * D_POOL + 4 * D_REC
EPS = 1e-6

kernel_name = "hybrid_pool_hgrn2_parallel_heads"


def rmsnorm(x, g):
    xf = x.astype(jnp.float32)
    y = xf * lax.rsqrt(jnp.mean(xf * xf, axis=-1, keepdims=True) + EPS)
    return y.astype(x.dtype) * g


def causal_multiscale_pool(u, pool_w, pool_scale):
    B, T, _ = u.shape
    ug = u.reshape(B, T, N_POOL_GROUPS, POOL_GROUP).astype(jnp.float32)
    cs0 = jnp.pad(jnp.cumsum(ug, axis=1), ((0, 0), (1, 0), (0, 0), (0, 0)))
    pos = jnp.arange(T, dtype=jnp.float32)
    pooled = []
    for gi, w in enumerate(POOL_WINDOWS):
        c = cs0[:, :, gi]
        hi = c[:, 1:]
        lo = jnp.pad(c[:, : T + 1 - w], ((0, 0), (w - 1, 0), (0, 0)))
        count = jnp.minimum(pos + 1.0, float(w))[None, :, None]
        pooled.append((hi - lo) / count)
    pooled = jnp.stack(pooled, axis=2)
    mixed = jnp.einsum('btgc,gcd->btgd', pooled - ug, pool_w.astype(jnp.float32))
    return mixed.reshape(B, T, D_POOL).astype(u.dtype) * pool_scale


def hgrn2_chunked(q, k, v, log_f):
    B, H, T, Dk = q.shape
    Dv = v.shape[-1]
    n = T // CHUNK

    def to_chunks(a):
        return a.reshape(B, H, n, CHUNK, a.shape[-1]).transpose(2, 0, 1, 3, 4)

    qc, kc, vc, gc = to_chunks(q), to_chunks(k), to_chunks(v), to_chunks(log_f)
    causal = jnp.tril(jnp.ones((CHUNK, CHUNK), dtype=bool))[:, :, None]

    def step(S, inp):
        qb, kb, vb, gb = inp
        G = jnp.cumsum(gb, axis=2)
        diff = G[:, :, :, None, :] - G[:, :, None, :, :]
        decay = jnp.exp(jnp.where(causal, diff, -jnp.inf))
        A = jnp.einsum('bhtd,bhsd,bhtsd->bhts', qb, kb, decay)
        o = (jnp.einsum('bhts,bhsv->bhtv', A, vb)
             + jnp.einsum('bhtd,bhdv->bhtv', qb * jnp.exp(G), S))
        G_last = G[:, :, -1]
        k_dec = kb * jnp.exp(G_last[:, :, None, :] - G)
        S = jnp.exp(G_last)[..., None] * S + jnp.einsum('bhsd,bhsv->bhdv', k_dec, vb)
        return S, o

    S0 = jnp.zeros((B, H, Dk, Dv), jnp.float32)
    _, o = lax.scan(step, S0, (qc, kc, vc, gc))
    return o.transpose(1, 2, 0, 3, 4).reshape(B, H, T, Dv)


def hybrid_layer(x, norm_g, w_in, pool_w, pool_scale, lb, rec_norm_g, w_out):
    B, T, _ = x.shape
    h = rmsnorm(x, norm_g)
    proj = jnp.einsum('btd,de->bte', h, w_in)
    o = 0
    pool_in = proj[..., o:o + D_POOL]; o += D_POOL
    pool_gate = proj[..., o:o + D_POOL]; o += D_POOL
    q = proj[..., o:o + D_REC]; o += D_REC
    f_logit = proj[..., o:o + D_REC]; o += D_REC
    i_in = proj[..., o:o + D_REC]; o += D_REC
    rec_gate = proj[..., o:o + D_REC]

    y_pool = causal_multiscale_pool(pool_in, pool_w, pool_scale) * jax.nn.silu(pool_gate)

    def heads(a):
        return a.astype(jnp.float32).reshape(B, T, N_REC_HEADS, HEAD_DIM).transpose(0, 2, 1, 3)

    f = lb + (1.0 - lb) * jax.nn.sigmoid(f_logit.astype(jnp.float32))
    rec = hgrn2_chunked(heads(jax.nn.silu(q)), heads(1.0 - f), heads(i_in), heads(jnp.log(f)))
    rec = rec.transpose(0, 2, 1, 3)
    rec = rec * lax.rsqrt(jnp.mean(rec * rec, axis=-1, keepdims=True) + EPS)
    rec = rec.reshape(B, T, D_REC).astype(x.dtype) * rec_norm_g
    y_rec = rec * jax.nn.silu(rec_gate)

    y = jnp.concatenate([y_pool, y_rec], axis=-1)
    return x + jnp.einsum('bte,ed->btd', y, w_out)


def _fwd_setup_inputs(seed: int = 0) -> dict:
    key = jax.random.key(seed)
    ks = jax.random.split(key, 9)
    f32 = jnp.float32
    return {
        "x": jax.random.normal(ks[0], (BATCH, SEQ, D_MODEL), f32),
        "norm1_g": 1.0 + 0.02 * jax.random.normal(ks[1], (DEPTH, D_MODEL), f32),
        "w_in": jax.random.normal(ks[2], (DEPTH, D_MODEL, D_IN_PROJ), f32) * D_MODEL ** -0.5,
        "pool_w": jax.random.normal(ks[3], (DEPTH, N_POOL_GROUPS, POOL_GROUP, POOL_GROUP), f32) * POOL_GROUP ** -0.5,
        "pool_scale": 1.0 + 0.02 * jax.random.normal(ks[4], (DEPTH, D_POOL), f32),
        "lb_logits": 0.5 * jax.random.normal(ks[5], (DEPTH + 1, D_REC), f32),
        "rec_norm_g": 1.0 + 0.02 * jax.random.normal(ks[6], (DEPTH, D_REC), f32),
        "w_out": jax.random.normal(ks[7], (DEPTH, D_MIX, D_MODEL), f32) * D_MIX ** -0.5,
        "final_norm_g": 1.0 + 0.02 * jax.random.normal(ks[8], (D_MODEL,), f32),
    }


def _fwd_reference(x, norm1_g, w_in, pool_w, pool_scale, lb_logits, rec_norm_g, w_out, final_norm_g):
    lb_all = jnp.cumsum(jax.nn.softmax(lb_logits.astype(jnp.float32), axis=0), axis=0)
    for layer in range(DEPTH):
        x = hybrid_layer(x, norm1_g[layer], w_in[layer], pool_w[layer], pool_scale[layer],
                         lb_all[layer], rec_norm_g[layer], w_out[layer])
    return rmsnorm(x, final_norm_g)


import jax as _jax
import jax.numpy as _jnp

TWIN_FORMAT = 'train_step'
FWD_PARAMS = ['x', 'norm1_g', 'w_in', 'pool_w', 'pool_scale', 'lb_logits', 'rec_norm_g', 'w_out', 'final_norm_g']
TWIN_WEIGHTS = ['norm1_g', 'w_in', 'pool_w', 'pool_scale', 'lb_logits', 'rec_norm_g', 'w_out', 'final_norm_g']
TWIN_DIFF_INPUT = 'x'
TWIN_INPUTS = ['x', 'norm1_g', 'w_in', 'pool_w', 'pool_scale', 'lb_logits', 'rec_norm_g', 'w_out', 'final_norm_g', 'loss_target', 'm_norm1_g', 'm_w_in', 'm_pool_w', 'm_pool_scale', 'm_lb_logits', 'm_rec_norm_g', 'm_w_out', 'm_final_norm_g', 'v_norm1_g', 'v_w_in', 'v_pool_w', 'v_pool_scale', 'v_lb_logits', 'v_rec_norm_g', 'v_w_out', 'v_final_norm_g']
TWIN_OUTPUTS = ['loss', 'grad_x', 'grad_norm1_g', 'grad_w_in', 'grad_pool_w', 'grad_pool_scale', 'grad_lb_logits', 'grad_rec_norm_g', 'grad_w_out', 'grad_final_norm_g', 'delta_norm1_g', 'delta_w_in', 'delta_pool_w', 'delta_pool_scale', 'delta_lb_logits', 'delta_rec_norm_g', 'delta_w_out', 'delta_final_norm_g', 'new_m_norm1_g', 'new_m_w_in', 'new_m_pool_w', 'new_m_pool_scale', 'new_m_lb_logits', 'new_m_rec_norm_g', 'new_m_w_out', 'new_m_final_norm_g', 'new_v_norm1_g', 'new_v_w_in', 'new_v_pool_w', 'new_v_pool_scale', 'new_v_lb_logits', 'new_v_rec_norm_g', 'new_v_w_out', 'new_v_final_norm_g']
TWIN_LEAF_KINDS = {'loss': 'loss', 'grad_x': 'grad_x', 'grad_norm1_g': 'grad_w', 'grad_w_in': 'grad_w', 'grad_pool_w': 'grad_w', 'grad_pool_scale': 'grad_w', 'grad_lb_logits': 'grad_w', 'grad_rec_norm_g': 'grad_w', 'grad_w_out': 'grad_w', 'grad_final_norm_g': 'grad_w', 'delta_norm1_g': 'delta_w', 'delta_w_in': 'delta_w', 'delta_pool_w': 'delta_w', 'delta_pool_scale': 'delta_w', 'delta_lb_logits': 'delta_w', 'delta_rec_norm_g': 'delta_w', 'delta_w_out': 'delta_w', 'delta_final_norm_g': 'delta_w', 'new_m_norm1_g': 'new_m', 'new_m_w_in': 'new_m', 'new_m_pool_w': 'new_m', 'new_m_pool_scale': 'new_m', 'new_m_lb_logits': 'new_m', 'new_m_rec_norm_g': 'new_m', 'new_m_w_out': 'new_m', 'new_m_final_norm_g': 'new_m', 'new_v_norm1_g': 'new_v', 'new_v_w_in': 'new_v', 'new_v_pool_w': 'new_v', 'new_v_pool_scale': 'new_v', 'new_v_lb_logits': 'new_v', 'new_v_rec_norm_g': 'new_v', 'new_v_w_out': 'new_v', 'new_v_final_norm_g': 'new_v'}


def _forward(args):
    return _fwd_reference(*[args[k] for k in FWD_PARAMS])


def _output_shape():
    out = _jax.eval_shape(lambda: _forward(_fwd_setup_inputs(0)))
    return out.shape, out.dtype

N_MICROBATCH = 1
ADAM_LR = 0.001
ADAM_B1 = 0.9
ADAM_B2 = 0.999
ADAM_EPS = 1e-08
ADAM_WD = 0.01
ADAM_STEP = 10
PER_EXAMPLE_BATCH_AXIS = {'x': 0, 'loss_target': 0}
SHARED_INPUTS = []
_WEIGHT_DTYPES = {'norm1_g': _jnp.float32, 'w_in': _jnp.float32, 'pool_w': _jnp.float32, 'pool_scale': _jnp.float32, 'lb_logits': _jnp.float32, 'rec_norm_g': _jnp.float32, 'w_out': _jnp.float32, 'final_norm_g': _jnp.float32}
MOMENT_SCALE = {'norm1_g': 8.858182e-02, 'w_in': 3.604176e-02, 'pool_w': 4.042643e-02, 'pool_scale': 4.146428e-02, 'lb_logits': 4.058581e-03, 'rec_norm_g': 4.909875e-02, 'w_out': 6.123138e-02, 'final_norm_g': 1.599697e+01}


def _to_microbatches(a, axis):
    t = _jnp.moveaxis(a, axis, 0)
    t = t.reshape((N_MICROBATCH, t.shape[0] // N_MICROBATCH) + t.shape[1:])
    return _jnp.moveaxis(t, 1, axis + 1)


def setup_inputs(seed: int = 0) -> dict:
    inp = _fwd_setup_inputs(seed)
    key = _jax.random.fold_in(_jax.random.key(seed), 7919)
    shape, _ = _output_shape()
    out = dict(inp)
    out["loss_target"] = _jax.random.normal(_jax.random.fold_in(key, 0), shape, _jnp.float32)
    for i, name in enumerate(TWIN_WEIGHTS):
        w = inp[name].astype(_jnp.float32)
        if MOMENT_SCALE is None:
            s = _jnp.sqrt(_jnp.mean(_jnp.square(w)) + 1e-30)
        else:
            s = MOMENT_SCALE[name]
        km, kv = _jax.random.split(_jax.random.fold_in(key, i + 1))
        out[name] = w
        out["m_" + name] = s * _jax.random.normal(km, w.shape, _jnp.float32)
        out["v_" + name] = (s * s) * _jax.random.uniform(kv, w.shape, _jnp.float32, 0.5, 1.5)
    if N_MICROBATCH > 1:
        for name, axis in PER_EXAMPLE_BATCH_AXIS.items():
            out[name] = _to_microbatches(out[name], axis)
    return {'x': out['x'], 'norm1_g': out['norm1_g'], 'w_in': out['w_in'], 'pool_w': out['pool_w'], 'pool_scale': out['pool_scale'], 'lb_logits': out['lb_logits'], 'rec_norm_g': out['rec_norm_g'], 'w_out': out['w_out'], 'final_norm_g': out['final_norm_g'], 'loss_target': out['loss_target'], 'm_norm1_g': out['m_norm1_g'], 'm_w_in': out['m_w_in'], 'm_pool_w': out['m_pool_w'], 'm_pool_scale': out['m_pool_scale'], 'm_lb_logits': out['m_lb_logits'], 'm_rec_norm_g': out['m_rec_norm_g'], 'm_w_out': out['m_w_out'], 'm_final_norm_g': out['m_final_norm_g'], 'v_norm1_g': out['v_norm1_g'], 'v_w_in': out['v_w_in'], 'v_pool_w': out['v_pool_w'], 'v_pool_scale': out['v_pool_scale'], 'v_lb_logits': out['v_lb_logits'], 'v_rec_norm_g': out['v_rec_norm_g'], 'v_w_out': out['v_w_out'], 'v_final_norm_g': out['v_final_norm_g']}


def _loss(weights, diff, rest, loss_target):
    with _jax.named_scope("forward"):
        args = {**rest, TWIN_DIFF_INPUT: diff, **{k: w.astype(_WEIGHT_DTYPES[k]) for k, w in weights.items()}}
        y = _forward(args)
    with _jax.named_scope("loss_head"):
        err = _jnp.square(y.astype(_jnp.float32) - loss_target)
        return 0.5 * _jnp.sum(_jnp.mean(err, axis=-1)) if err.ndim else 0.5 * err


def _adamw(w, g, m, v):
    m = ADAM_B1 * m + (1.0 - ADAM_B1) * g
    v = ADAM_B2 * v + (1.0 - ADAM_B2) * _jnp.square(g)
    m_hat = m / (1.0 - ADAM_B1 ** ADAM_STEP)
    v_hat = v / (1.0 - ADAM_B2 ** ADAM_STEP)
    delta = -ADAM_LR * (m_hat / (_jnp.sqrt(v_hat) + ADAM_EPS) + ADAM_WD * w)
    return delta, m, v


def reference(x, norm1_g, w_in, pool_w, pool_scale, lb_logits, rec_norm_g, w_out, final_norm_g, loss_target, m_norm1_g, m_w_in, m_pool_w, m_pool_scale, m_lb_logits, m_rec_norm_g, m_w_out, m_final_norm_g, v_norm1_g, v_w_in, v_pool_w, v_pool_scale, v_lb_logits, v_rec_norm_g, v_w_out, v_final_norm_g):
    given = dict(x=x, norm1_g=norm1_g, w_in=w_in, pool_w=pool_w, pool_scale=pool_scale, lb_logits=lb_logits, rec_norm_g=rec_norm_g, w_out=w_out, final_norm_g=final_norm_g, loss_target=loss_target, m_norm1_g=m_norm1_g, m_w_in=m_w_in, m_pool_w=m_pool_w, m_pool_scale=m_pool_scale, m_lb_logits=m_lb_logits, m_rec_norm_g=m_rec_norm_g, m_w_out=m_w_out, m_final_norm_g=m_final_norm_g, v_norm1_g=v_norm1_g, v_w_in=v_w_in, v_pool_w=v_pool_w, v_pool_scale=v_pool_scale, v_lb_logits=v_lb_logits, v_rec_norm_g=v_rec_norm_g, v_w_out=v_w_out, v_final_norm_g=v_final_norm_g)
    weights = {n: given[n] for n in TWIN_WEIGHTS}
    shared = {n: given[n] for n in SHARED_INPUTS}
    per_example = {n: given[n] for n in ['x']}
    grad_fn = _jax.value_and_grad(_loss, argnums=(0, 1))

    def one_microbatch(ex, loss_target):
        ex = dict(ex)
        diff = ex.pop(TWIN_DIFF_INPUT)
        return grad_fn(weights, diff, {**shared, **ex}, loss_target)

    if N_MICROBATCH == 1:
        loss, (grad_w, grad_x) = one_microbatch(per_example, given["loss_target"])
    else:
        def body(carry, xs):
            loss_sum, grad_sum = carry
            l_k, (gw_k, gx_k) = one_microbatch(xs[0], xs[1])
            with _jax.named_scope("update"):
                return (loss_sum + l_k, _jax.tree.map(_jnp.add, grad_sum, gw_k)), gx_k

        init = (_jnp.zeros((), _jnp.float32), _jax.tree.map(_jnp.zeros_like, weights))
        (loss, grad_w), grad_x = _jax.lax.scan(body, init, (per_example, given["loss_target"]))
    with _jax.named_scope("update"):
        delta_w, new_m, new_v = {}, {}, {}
        for n in TWIN_WEIGHTS:
            delta_w[n], new_m[n], new_v[n] = _adamw(weights[n], grad_w[n], given["m_" + n], given["v_" + n])
    return (loss, grad_x, *[grad_w[n] for n in TWIN_WEIGHTS], *[delta_w[n] for n in TWIN_WEIGHTS],
            *[new_m[n] for n in TWIN_WEIGHTS], *[new_v[n] for n in TWIN_WEIGHTS])
```

```python
import functools

import jax
import jax.numpy as jnp
from jax import lax
from jax.experimental import pallas as pl
from jax.experimental.pallas import tpu as pltpu

T = 2048
D = 1024
NSEG = 6
NTILE = 24
TILE = 256
DMIX = 2048
NDEV = 8
HEAD = 128
NHEAD = 8
CHUNK = 64
NCHUNK = T // CHUNK
NGROUP = 4
GROUP = 256
EPS = 1e-6
EXP_CAP = 80.0
MESH = pl.DeviceIdType.MESH
AXES = ("x", "y", "c")

ADAM_LR = 0.001
ADAM_B1 = 0.9
ADAM_B2 = 0.999
ADAM_EPS = 1e-08
ADAM_WD = 0.01
ADAM_STEP = 10
BC1 = 1.0 - ADAM_B1 ** ADAM_STEP
BC2 = 1.0 - ADAM_B2 ** ADAM_STEP

MIB = 1 << 20
HI = lax.Precision.HIGHEST


def _params(sem=None, vmem_mib=48):
    return pltpu.CompilerParams(dimension_semantics=sem, vmem_limit_bytes=vmem_mib * MIB)


def _sigmoid(v):
    return 1.0 / (1.0 + jnp.exp(-v))


def _dot(a, b, ca, cb, precision=None):
    return lax.dot_general(a, b, (((ca,), (cb,)), ((), ())), precision=precision,
                           preferred_element_type=jnp.float32)


def _bf(v):
    return v.astype(jnp.bfloat16)


def _place():
    x, y, c = lax.axis_index("x"), lax.axis_index("y"), lax.axis_index("c")
    return x, y, c, 4 * x + 2 * y + c


def _peer(x, y, c, r):
    return (x ^ ((r >> 2) & 1), y ^ ((r >> 1) & 1), c ^ (r & 1))


def _gather_weights(w_t3, w_out_b, pool_b):
    def body(win_ref, wout_ref, pool_ref, win_o, wout_o, pool_o, send_sems, recv_sems, loc_sems):
        x, y, c, me = _place()
        srcs = (win_ref, wout_ref, pool_ref)

        def dst(w, idx):
            if w == 0:
                return win_o.at[pl.ds(3 * idx, 3)]
            return (wout_o if w == 1 else pool_o).at[idx]

        locs = [pltpu.make_async_copy(srcs[w], dst(w, me), loc_sems.at[w]) for w in range(3)]
        for cp in locs:
            cp.start()

        def copy(r, w, slot):
            k = (r - 1) * 3 + w
            return pltpu.make_async_remote_copy(
                src_ref=srcs[w], dst_ref=dst(w, slot), send_sem=send_sems.at[k], recv_sem=recv_sems.at[k],
                device_id=_peer(x, y, c, r), device_id_type=MESH)

        sends = [copy(r, w, me) for r in range(1, NDEV) for w in range(3)]
        for cp in sends:
            cp.start()
        for r in range(1, NDEV):
            for w in range(3):
                copy(r, w, me ^ r).wait_recv()
        for cp in sends:
            cp.wait_send()
        for cp in locs:
            cp.wait()

    any_spec = pl.BlockSpec(memory_space=pl.ANY)
    return pl.pallas_call(
        body, name="gather_weights",
        out_shape=(jax.ShapeDtypeStruct((NTILE, D, TILE), jnp.bfloat16),
                   jax.ShapeDtypeStruct((NDEV, DMIX // NDEV, D), jnp.bfloat16),
                   jax.ShapeDtypeStruct((NDEV, NGROUP, GROUP // NDEV, GROUP), jnp.bfloat16)),
        in_specs=[any_spec] * 3, out_specs=(any_spec,) * 3,
        scratch_shapes=[pltpu.SemaphoreType.DMA((21,)), pltpu.SemaphoreType.DMA((21,)),
                        pltpu.SemaphoreType.DMA((3,))],
    )(w_t3, w_out_b, pool_b)


def _scatter_grads(gw_f, gw_b, gwout_f, gwout_b, gpool_f, small):
    def body(gwf, gwb, gof, gob, gpf, sm, own_in, r_in, own_out, r_out, own_pool, r_pool, r_small,
             send_sems, recv_sems, loc_sems):
        x, y, c, me = _place()
        locs = [pltpu.make_async_copy(gwf.at[pl.ds(3 * me, 3)], own_in, loc_sems.at[0]),
                pltpu.make_async_copy(gof.at[me], own_out, loc_sems.at[1]),
                pltpu.make_async_copy(gpf.at[me], own_pool, loc_sems.at[2]),
                pltpu.make_async_copy(sm, r_small.at[me], loc_sems.at[3])]
        for cp in locs:
            cp.start()

        def copy(r, w, to_idx, src_of_small):
            k = (r - 1) * 4 + w
            if w == 0:
                src, dstr = gwb.at[pl.ds(3 * to_idx, 3)], r_in.at[r - 1]
            elif w == 1:
                src, dstr = gob.at[to_idx], r_out.at[r - 1]
            elif w == 2:
                src, dstr = gpf.at[to_idx], r_pool.at[r - 1]
            else:
                src, dstr = sm, r_small.at[src_of_small]
            return pltpu.make_async_remote_copy(
                src_ref=src, dst_ref=dstr, send_sem=send_sems.at[k], recv_sem=recv_sems.at[k],
                device_id=_peer(x, y, c, r), device_id_type=MESH)

        sends = [copy(r, w, me ^ r, me) for r in range(1, NDEV) for w in range(4)]
        for cp in sends:
            cp.start()
        for r in range(1, NDEV):
            for w in range(4):
                copy(r, w, me, me ^ r).wait_recv()
        for cp in sends:
            cp.wait_send()
        for cp in locs:
            cp.wait()

    any_spec = pl.BlockSpec(memory_space=pl.ANY)
    f32, bf16 = jnp.float32, jnp.bfloat16
    return pl.pallas_call(
        body, name="scatter_grads",
        out_shape=(jax.ShapeDtypeStruct((3, D, TILE), f32), jax.ShapeDtypeStruct((NDEV - 1, 3, D, TILE), bf16),
                   jax.ShapeDtypeStruct((DMIX // NDEV, D), f32), jax.ShapeDtypeStruct((NDEV - 1, DMIX // NDEV, D), bf16),
                   jax.ShapeDtypeStruct((NGROUP, GROUP // NDEV, GROUP), f32),
                   jax.ShapeDtypeStruct((NDEV - 1, NGROUP, GROUP // NDEV, GROUP), f32),
                   jax.ShapeDtypeStruct((NDEV, 8, D), f32)),
        in_specs=[any_spec] * 6, out_specs=(any_spec,) * 7,
        scratch_shapes=[pltpu.SemaphoreType.DMA((28,)), pltpu.SemaphoreType.DMA((28,)),
                        pltpu.SemaphoreType.DMA((4,))],
    )(gw_f, gw_b, gwout_f, gwout_b, gpool_f, small)


def _norm1(x, g1):
    rows = 512

    def body(x_ref, g_ref, h_ref):
        xv = x_ref[...]
        r = lax.rsqrt(jnp.mean(xv * xv, axis=-1, keepdims=True) + EPS)
        h_ref[...] = _bf(xv * r * g_ref[...])

    return pl.pallas_call(
        body, name="norm1", grid=(T // rows,),
        in_specs=[pl.BlockSpec((rows, D), lambda i: (i, 0)), pl.BlockSpec((1, D), lambda i: (0, 0))],
        out_specs=pl.BlockSpec((rows, D), lambda i: (i, 0)),
        out_shape=jax.ShapeDtypeStruct((T, D), jnp.bfloat16),
        compiler_params=_params(("parallel",)),
    )(x, g1)


def _seg_tiles(s):
    return (s + 2) % NSEG


def _proj(h, w_t):
    tm = 1024

    def body(h_ref, w_ref, o_ref):
        hv = h_ref[...]
        for i in range(4):
            o_ref[:, i * TILE:(i + 1) * TILE] = _dot(hv, w_ref[i], 1, 0)

    return pl.pallas_call(
        body, name="proj", grid=(T // tm, NSEG),
        in_specs=[pl.BlockSpec((tm, D), lambda m, s: (m, 0)),
                  pl.BlockSpec((4, D, TILE), lambda m, s: (_seg_tiles(s), 0, 0))],
        out_specs=pl.BlockSpec((None, tm, D), lambda m, s: (s, m, 0)),
        out_shape=jax.ShapeDtypeStruct((NSEG, T, D), jnp.float32),
        compiler_params=_params(("parallel", "parallel")),
    )(h, w_t)


def _row_ids(shape):
    return lax.broadcasted_iota(jnp.int32, shape, 0)


def _shift_down(a, k, rows):
    return jnp.where(rows >= k, pltpu.roll(a, k, 0), 0.0)


def _shift_up(a, k, rows):
    return jnp.where(rows < T - k, pltpu.roll(a, T - k, 0), 0.0)


def _window_sum(u, gidx, rows, shift):
    s2 = u + shift(u, 1, rows)
    s4 = s2 + shift(s2, 2, rows)
    s8 = s4 + shift(s4, 4, rows)
    s16 = s8 + shift(s8, 8, rows)
    return jnp.where(gidx == 0, s2, jnp.where(gidx == 1, s4, jnp.where(gidx == 2, s8, s16)))


def _pool_count(gidx, rows):
    width = lax.shift_left(jnp.int32(2), gidx)
    return jnp.minimum(rows + 1, width).astype(jnp.float32)


def _pool_fwd(proj, pool_w, pool_scale):
    def body(p_ref, w_ref, sc_ref, y_ref):
        gidx = pl.program_id(0)
        u, pg = p_ref[0], p_ref[1]
        rows = _row_ids(u.shape)
        d = _window_sum(u, gidx, rows, _shift_down) / _pool_count(gidx, rows) - u
        mixed = _dot(_bf(d), w_ref[...], 1, 0)
        y_ref[...] = _bf(mixed * sc_ref[...] * (pg * _sigmoid(pg)))

    return pl.pallas_call(
        body, name="pool_fwd", grid=(NGROUP,),
        in_specs=[pl.BlockSpec((2, T, GROUP), lambda g: (2, 0, g)),
                  pl.BlockSpec((None, GROUP, GROUP), lambda g: (g, 0, 0)),
                  pl.BlockSpec((1, GROUP), lambda g: (0, g))],
        out_specs=pl.BlockSpec((T, GROUP), lambda g: (0, g)),
        out_shape=jax.ShapeDtypeStruct((T, DMIX), jnp.bfloat16),
        compiler_params=_params(("parallel",)),
    )(proj, pool_w, pool_scale)


def _tri(lower):
    r = lax.broadcasted_iota(jnp.int32, (CHUNK, CHUNK), 0)
    c = lax.broadcasted_iota(jnp.int32, (CHUNK, CHUNK), 1)
    return (r >= c) if lower else (r <= c)


def _chunk_terms(qv, fl, lb):
    sq = _sigmoid(qv)
    qs = qv * sq
    sg = _sigmoid(fl)
    f = lb + (1.0 - lb) * sg
    kk = 1.0 - f
    g = jnp.log(f)
    ltri = _tri(True).astype(jnp.float32)
    big_g = _dot(ltri, g, 1, 0, precision=HI)
    g_last = jnp.sum(g, axis=0, keepdims=True)
    half = _row_ids(g.shape) < CHUNK // 2
    g_mid = jnp.sum(jnp.where(half, g, 0.0), axis=0, keepdims=True)
    e_q = jnp.exp(big_g)
    e_k = jnp.exp(g_last - big_g)
    e_qm = jnp.exp(jnp.minimum(big_g - g_mid, EXP_CAP))
    e_km = jnp.exp(jnp.minimum(g_mid - big_g, EXP_CAP))
    return dict(sq=sq, qs=qs, sg=sg, f=f, kk=kk, g_last=g_last, e_q=e_q, e_k=e_k, e_qm=e_qm, e_km=e_km)


def _lower_bound(lb_ref):
    return _sigmoid(lb_ref[0:1, :] - lb_ref[1:2, :])


def _hgrn_fwd(proj, lb_logits, rec_g, y_in):
    def body(p_ref, lb_ref, rg_ref, y_any, y_ref, o_ref, st_ref):
        del y_any
        lb = _lower_bound(lb_ref)
        causal = _tri(True)

        def chunk(ci, st):
            r0 = pl.multiple_of(ci * CHUNK, CHUNK)
            rows = pl.ds(r0, CHUNK)
            t = _chunk_terms(p_ref[0, rows, :], p_ref[1, rows, :], lb)
            v = _bf(p_ref[2, rows, :])
            st_ref[ci] = st
            q_e, k_e = _bf(t["qs"] * t["e_q"]), _bf(t["kk"] * t["e_k"])
            q_m, k_m = _bf(t["qs"] * t["e_qm"]), _bf(t["kk"] * t["e_km"])
            a = jnp.where(causal, _dot(q_m, k_m, 1, 1), 0.0)
            o_ref[rows, :] = _dot(_bf(a), v, 1, 0) + _dot(q_e, _bf(st), 1, 1)
            return st * jnp.exp(t["g_last"]) + _dot(v, k_e, 0, 0)

        lax.fori_loop(0, NCHUNK, chunk, jnp.zeros((HEAD, HEAD), jnp.float32))
        o = o_ref[...]
        rn = o * lax.rsqrt(jnp.mean(o * o, axis=-1, keepdims=True) + EPS)
        gate = p_ref[3]
        y_ref[...] = _bf(rn * rg_ref[...] * (gate * _sigmoid(gate)))

    return pl.pallas_call(
        body, name="hgrn_fwd", grid=(NHEAD,),
        in_specs=[pl.BlockSpec((4, T, HEAD), lambda h: (0, 0, h)),
                  pl.BlockSpec((2, HEAD), lambda h: (0, h)),
                  pl.BlockSpec((1, HEAD), lambda h: (0, h)),
                  pl.BlockSpec(memory_space=pl.ANY)],
        out_specs=(pl.BlockSpec((T, HEAD), lambda h: (0, NHEAD + h)),
                   pl.BlockSpec((T, HEAD), lambda h: (0, h)),
                   pl.BlockSpec((None, NCHUNK, HEAD, HEAD), lambda h: (h, 0, 0, 0))),
        out_shape=(jax.ShapeDtypeStruct((T, DMIX), jnp.bfloat16),
                   jax.ShapeDtypeStruct((T, D), jnp.float32),
                   jax.ShapeDtypeStruct((NHEAD, NCHUNK, HEAD, HEAD), jnp.float32)),
        input_output_aliases={3: 0},
        compiler_params=_params(("parallel",)),
    )(proj, lb_logits, rec_g, y_in)


def _out_proj(y, w_out):
    tm = 512

    def body(y_ref, w_ref, o_ref):
        o_ref[...] = _dot(y_ref[...], w_ref[...], 1, 0)

    return pl.pallas_call(
        body, name="out_proj", grid=(T // tm,),
        in_specs=[pl.BlockSpec((tm, DMIX), lambda m: (m, 0)), pl.BlockSpec((DMIX, D), lambda m: (0, 0))],
        out_specs=pl.BlockSpec((tm, D), lambda m: (m, 0)),
        out_shape=jax.ShapeDtypeStruct((T, D), jnp.float32),
        compiler_params=_params(("parallel",)),
    )(y, w_out)


def _loss_head(x, ymm, target, gf):
    rows = 256

    def body(x_ref, y_ref, t_ref, g_ref, dz_ref, dzb_ref, sq_ref, dg_ref):
        z = x_ref[...] + y_ref[...]
        r = lax.rsqrt(jnp.mean(z * z, axis=-1, keepdims=True) + EPS)
        zhat = z * r
        err = zhat * g_ref[...] - t_ref[...]
        dy = err * (1.0 / D)
        gdy = dy * g_ref[...]
        dz = r * (gdy - zhat * jnp.mean(zhat * gdy, axis=-1, keepdims=True))
        dz_ref[...] = dz
        dzb_ref[...] = _bf(dz)
        sq = jnp.sum(err * err, axis=0, keepdims=True)
        dg = jnp.sum(zhat * dy, axis=0, keepdims=True)

        @pl.when(pl.program_id(0) == 0)
        def _():
            sq_ref[...] = sq
            dg_ref[...] = dg

        @pl.when(pl.program_id(0) != 0)
        def _():
            sq_ref[...] += sq
            dg_ref[...] += dg

    tile = pl.BlockSpec((rows, D), lambda i: (i, 0))
    vec = pl.BlockSpec((1, D), lambda i: (0, 0))
    return pl.pallas_call(
        body, name="loss_head", grid=(T // rows,),
        in_specs=[tile, tile, tile, vec], out_specs=(tile, tile, vec, vec),
        out_shape=(jax.ShapeDtypeStruct((T, D), jnp.float32), jax.ShapeDtypeStruct((T, D), jnp.bfloat16),
                   jax.ShapeDtypeStruct((1, D), jnp.float32), jax.ShapeDtypeStruct((1, D), jnp.float32)),
        compiler_params=_params(("arbitrary",)),
    )(x, ymm, target, gf)


def _out_proj_bwd(dzb, w_out, y):
    tn = 512

    def body(dz_ref, w_ref, y_ref, dy_ref, gw_ref, gwb_ref):
        dz = dz_ref[...]
        dy_ref[...] = _dot(dz, w_ref[...], 1, 1)
        gw = _dot(y_ref[...], dz, 0, 0)
        gw_ref[...] = gw
        gwb_ref[...] = _bf(gw)

    return pl.pallas_call(
        body, name="out_proj_bwd", grid=(DMIX // tn,),
        in_specs=[pl.BlockSpec((T, D), lambda n: (0, 0)), pl.BlockSpec((tn, D), lambda n: (n, 0)),
                  pl.BlockSpec((T, tn), lambda n: (0, n))],
        out_specs=(pl.BlockSpec((T, tn), lambda n: (0, n)), pl.BlockSpec((tn, D), lambda n: (n, 0)),
                   pl.BlockSpec((tn, D), lambda n: (n, 0))),
        out_shape=(jax.ShapeDtypeStruct((T, DMIX), jnp.float32), jax.ShapeDtypeStruct((DMIX, D), jnp.float32),
                   jax.ShapeDtypeStruct((DMIX, D), jnp.bfloat16)),
        compiler_params=_params(("parallel",)),
    )(dzb, w_out, y)


def _hgrn_bwd(proj, lb_logits, rec_g, o, states, dymix):
    def body(p_ref, lb_ref, rg_ref, o_ref, st_ref, dy_ref, dp_ref, drg_ref, dlb_ref, do_ref):
        lb = _lower_bound(lb_ref)
        causal = _tri(True)
        utri = _tri(False).astype(jnp.float32)

        o = o_ref[...]
        rs = lax.rsqrt(jnp.mean(o * o, axis=-1, keepdims=True) + EPS)
        rn = o * rs
        gate = p_ref[3]
        sgate = _sigmoid(gate)
        dyv = dy_ref[...]
        d_r = dyv * (gate * sgate)
        dp_ref[3] = _bf(dyv * (rn * rg_ref[...]) * (sgate * (1.0 + gate * (1.0 - sgate))))
        drg_ref[...] = jnp.sum(d_r * rn, axis=0, keepdims=True)
        drn = d_r * rg_ref[...]
        do_ref[...] = rs * (drn - rn * jnp.mean(rn * drn, axis=-1, keepdims=True))

        def chunk(i, carry):
            dst, dlb = carry
            ci = NCHUNK - 1 - i
            r0 = pl.multiple_of(ci * CHUNK, CHUNK)
            rows = pl.ds(r0, CHUNK)
            qv, fl = p_ref[0, rows, :], p_ref[1, rows, :]
            t = _chunk_terms(qv, fl, lb)
            v = _bf(p_ref[2, rows, :])
            st = st_ref[ci]
            do = _bf(do_ref[rows, :])
            qe_f, ke_f = t["qs"] * t["e_q"], t["kk"] * t["e_k"]
            qm_f, km_f = t["qs"] * t["e_qm"], t["kk"] * t["e_km"]
            q_e, k_e, q_m, k_m = _bf(qe_f), _bf(ke_f), _bf(qm_f), _bf(km_f)
            dst_b = _bf(dst)
            a = _bf(jnp.where(causal, _dot(q_m, k_m, 1, 1), 0.0))
            da = _bf(jnp.where(causal, _dot(do, v, 1, 1), 0.0))
            dqm = _dot(da, k_m, 1, 0)
            dkm = _dot(da, q_m, 0, 0)
            dqe = _dot(do, _bf(st), 1, 0)
            dke = _dot(v, dst_b, 1, 0)
            dp_ref[2, rows, :] = _bf(_dot(a, do, 0, 0) + _dot(k_e, dst_b, 1, 1))
            decay = jnp.exp(t["g_last"])
            dqs = dqm * t["e_qm"] + dqe * t["e_q"]
            dkk = dkm * t["e_km"] + dke * t["e_k"]
            kdk = ke_f * dke
            d_big_g = (q_m.astype(jnp.float32) * dqm - k_m.astype(jnp.float32) * dkm) + (qe_f * dqe - kdk)
            tail = (jnp.sum(kdk, axis=0, keepdims=True)
                    + decay * jnp.sum(dst * st, axis=0, keepdims=True))
            dg = _dot(utri, d_big_g, 1, 0, precision=HI) + tail
            df = dg / t["f"] - dkk
            dp_ref[1, rows, :] = _bf(df * (1.0 - lb) * (t["sg"] * (1.0 - t["sg"])))
            dp_ref[0, rows, :] = _bf(dqs * (t["sq"] * (1.0 + qv * (1.0 - t["sq"]))))
            dlb = dlb + jnp.sum(df * (1.0 - t["sg"]), axis=0, keepdims=True)
            return dst * decay + _dot(do, q_e, 0, 0), dlb

        _, dlb = lax.fori_loop(0, NCHUNK, chunk, (jnp.zeros((HEAD, HEAD), jnp.float32),
                                                  jnp.zeros((1, HEAD), jnp.float32)))
        dlb_ref[...] = dlb

    vec = pl.BlockSpec((1, HEAD), lambda h: (0, h))
    return pl.pallas_call(
        body, name="hgrn_bwd", grid=(NHEAD,),
        in_specs=[pl.BlockSpec((4, T, HEAD), lambda h: (0, 0, h)),
                  pl.BlockSpec((2, HEAD), lambda h: (0, h)), vec,
                  pl.BlockSpec((T, HEAD), lambda h: (0, h)),
                  pl.BlockSpec((None, NCHUNK, HEAD, HEAD), lambda h: (h, 0, 0, 0)),
                  pl.BlockSpec((T, HEAD), lambda h: (0, NHEAD + h))],
        out_specs=(pl.BlockSpec((4, T, HEAD), lambda h: (0, 0, h)), vec, vec),
        out_shape=(jax.ShapeDtypeStruct((NSEG, T, D), jnp.bfloat16),
                   jax.ShapeDtypeStruct((1, D), jnp.float32), jax.ShapeDtypeStruct((1, D), jnp.float32)),
        scratch_shapes=[pltpu.VMEM((T, HEAD), jnp.float32)],
        compiler_params=_params(("parallel",)),
    )(proj, lb_logits, rec_g, o, states, dymix)


def _pool_bwd(proj, pool_w, pool_scale, dymix, dproj_in):
    def body(p_ref, w_ref, sc_ref, dy_ref, dp_any, dp_ref, gw_ref, gs_ref):
        del dp_any
        gidx = pl.program_id(0)
        u, pg = p_ref[0], p_ref[1]
        rows = _row_ids(u.shape)
        count = _pool_count(gidx, rows)
        d = _bf(_window_sum(u, gidx, rows, _shift_down) / count - u)
        mixed = _dot(d, w_ref[...], 1, 0)
        spg = _sigmoid(pg)
        dyv = dy_ref[...]
        d_p = dyv * (pg * spg)
        dp_ref[1] = _bf(dyv * (mixed * sc_ref[...]) * (spg * (1.0 + pg * (1.0 - spg))))
        gs_ref[...] = jnp.sum(d_p * mixed, axis=0, keepdims=True)
        dmixed = _bf(d_p * sc_ref[...])
        gw_ref[...] = _dot(d, dmixed, 0, 0)
        dd = _dot(dmixed, w_ref[...], 1, 1)
        dp_ref[0] = _bf(_window_sum(dd / count, gidx, rows, _shift_up) - dd)

    return pl.pallas_call(
        body, name="pool_bwd", grid=(NGROUP,),
        in_specs=[pl.BlockSpec((2, T, GROUP), lambda g: (2, 0, g)),
                  pl.BlockSpec((None, GROUP, GROUP), lambda g: (g, 0, 0)),
                  pl.BlockSpec((1, GROUP), lambda g: (0, g)),
                  pl.BlockSpec((T, GROUP), lambda g: (0, g)),
                  pl.BlockSpec(memory_space=pl.ANY)],
        out_specs=(pl.BlockSpec((2, T, GROUP), lambda g: (2, 0, g)),
                   pl.BlockSpec((None, GROUP, GROUP), lambda g: (g, 0, 0)),
                   pl.BlockSpec((1, GROUP), lambda g: (0, g))),
        out_shape=(jax.ShapeDtypeStruct((NSEG, T, D), jnp.bfloat16),
                   jax.ShapeDtypeStruct((NGROUP, GROUP, GROUP), jnp.float32),
                   jax.ShapeDtypeStruct((1, D), jnp.float32)),
        input_output_aliases={4: 0},
        compiler_params=_params(("parallel",)),
    )(proj, pool_w, pool_scale, dymix, dproj_in)


def _proj_bwd_w(h, dproj):
    def body(h_ref, dp_ref, gw_ref, gwb_ref):
        gw = _dot(h_ref[...], dp_ref[...], 0, 0)
        for i in range(4):
            gw_ref[i] = gw[:, i * TILE:(i + 1) * TILE]
            gwb_ref[i] = _bf(gw[:, i * TILE:(i + 1) * TILE])

    out = pl.BlockSpec((4, D, TILE), lambda s: (_seg_tiles(s), 0, 0))
    return pl.pallas_call(
        body, name="proj_bwd_w", grid=(NSEG,),
        in_specs=[pl.BlockSpec((T, D), lambda s: (0, 0)), pl.BlockSpec((None, T, D), lambda s: (s, 0, 0))],
        out_specs=(out, out),
        out_shape=(jax.ShapeDtypeStruct((NTILE, D, TILE), jnp.float32),
                   jax.ShapeDtypeStruct((NTILE, D, TILE), jnp.bfloat16)),
        compiler_params=_params(("parallel",)),
    )(h, dproj)


def _proj_bwd_x(dproj, w_t):
    tm = 1024

    def body(dp_ref, w_ref, o_ref, wcat):
        for i in range(4):
            wcat[:, i * TILE:(i + 1) * TILE] = w_ref[i]
        r = _dot(dp_ref[...], wcat[...], 1, 1)

        @pl.when(pl.program_id(1) == 0)
        def _():
            o_ref[...] = r

        @pl.when(pl.program_id(1) != 0)
        def _():
            o_ref[...] += r

    return pl.pallas_call(
        body, name="proj_bwd_x", grid=(T // tm, NSEG),
        in_specs=[pl.BlockSpec((None, tm, D), lambda m, s: (s, m, 0)),
                  pl.BlockSpec((4, D, TILE), lambda m, s: (_seg_tiles(s), 0, 0))],
        out_specs=pl.BlockSpec((tm, D), lambda m, s: (m, 0)),
        out_shape=jax.ShapeDtypeStruct((T, D), jnp.float32),
        scratch_shapes=[pltpu.VMEM((D, D), jnp.bfloat16)],
        compiler_params=_params(("parallel", "arbitrary")),
    )(dproj, w_t)


def _norm1_bwd(x, g1, dh, dz):
    rows = 256

    def body(x_ref, g_ref, dh_ref, dz_ref, dx_ref, dg_ref):
        xv = x_ref[...]
        r = lax.rsqrt(jnp.mean(xv * xv, axis=-1, keepdims=True) + EPS)
        xhat = xv * r
        dhv = dh_ref[...]
        gdh = dhv * g_ref[...]
        dx_ref[...] = dz_ref[...] + r * (gdh - xhat * jnp.mean(xhat * gdh, axis=-1, keepdims=True))
        dg = jnp.sum(xhat * dhv, axis=0, keepdims=True)

        @pl.when(pl.program_id(0) == 0)
        def _():
            dg_ref[...] = dg

        @pl.when(pl.program_id(0) != 0)
        def _():
            dg_ref[...] += dg

    tile = pl.BlockSpec((rows, D), lambda i: (i, 0))
    vec = pl.BlockSpec((1, D), lambda i: (0, 0))
    return pl.pallas_call(
        body, name="norm1_bwd", grid=(T // rows,),
        in_specs=[tile, vec, tile, tile], out_specs=(tile, vec),
        out_shape=(jax.ShapeDtypeStruct((T, D), jnp.float32), jax.ShapeDtypeStruct((1, D), jnp.float32)),
        compiler_params=_params(("arbitrary",)),
    )(x, g1, dh, dz)


def _adamw(w, g, m, v):
    m_new = ADAM_B1 * m + (1.0 - ADAM_B1) * g
    v_new = ADAM_B2 * v + (1.0 - ADAM_B2) * (g * g)
    delta = -ADAM_LR * ((m_new / BC1) / (jnp.sqrt(v_new / BC2) + ADAM_EPS) + ADAM_WD * w)
    return delta, m_new, v_new


def _reduce_adam(name, own, recv, w, m, v, grid, own_spec, recv_spec, w_spec):
    def body(own_ref, recv_ref, w_ref, m_ref, v_ref, g_ref, d_ref, mo_ref, vo_ref):
        g = own_ref[...]
        for r in range(NDEV - 1):
            g = g + recv_ref[r].astype(jnp.float32)
        delta, m_new, v_new = _adamw(w_ref[...], g, m_ref[...], v_ref[...])
        g_ref[...] = g
        d_ref[...] = delta
        mo_ref[...] = m_new
        vo_ref[...] = v_new

    shape = jax.ShapeDtypeStruct(w.shape, jnp.float32)
    return pl.pallas_call(
        body, name=name, grid=grid,
        in_specs=[own_spec, recv_spec, w_spec, w_spec, w_spec], out_specs=(w_spec,) * 4,
        out_shape=(shape,) * 4,
        compiler_params=_params(("parallel",)),
    )(own, recv, w, m, v)


def _small_adam(parts, w, m, v):
    def body(p_ref, w_ref, m_ref, v_ref, g_ref, d_ref, mo_ref, vo_ref):
        g = p_ref[0]
        for s in range(1, NDEV):
            g = g + p_ref[s]
        wv = w_ref[...]
        rows = _row_ids(wv.shape)
        other = jnp.where(rows == 2, pltpu.roll(wv, 7, 0), jnp.where(rows == 3, pltpu.roll(wv, 1, 0), 0.0))
        lbv = _sigmoid(wv - other)
        sign = jnp.where(rows == 2, 1.0, -1.0)
        g = jnp.where((rows == 2) | (rows == 3), sign * g * lbv * (1.0 - lbv), g)
        delta, m_new, v_new = _adamw(wv, g, m_ref[...], v_ref[...])
        g_ref[...] = g
        d_ref[...] = delta
        mo_ref[...] = m_new
        vo_ref[...] = v_new

    shape = jax.ShapeDtypeStruct((8, D), jnp.float32)
    return pl.pallas_call(body, name="small_adam", out_shape=(shape,) * 4)(parts, w, m, v)


def _rows8(*vecs):
    rows = [a.reshape(-1, D) for a in vecs]
    n = sum(r.shape[0] for r in rows)
    return jnp.concatenate(rows + [jnp.zeros((8 - n, D), jnp.float32)], axis=0)


def kernel(x, norm1_g, w_in, pool_w, pool_scale, lb_logits, rec_norm_g, w_out, final_norm_g, loss_target, m_norm1_g, m_w_in, m_pool_w, m_pool_scale, m_lb_logits, m_rec_norm_g, m_w_out, m_final_norm_g, v_norm1_g, v_w_in, v_pool_w, v_pool_scale, v_lb_logits, v_rec_norm_g, v_w_out, v_final_norm_g):
    xs = x[0]
    target = loss_target[0]
    gf = final_norm_g.reshape(1, D)

    w_t3 = _bf(w_in[0]).reshape(D, 3, TILE).transpose(1, 0, 2)
    w_t, w_out_g, pool_g = _gather_weights(w_t3, _bf(w_out[0]), _bf(pool_w[0]))
    w_out_full = w_out_g.reshape(DMIX, D)
    pool_full = pool_g.transpose(1, 0, 2, 3).reshape(NGROUP, GROUP, GROUP)

    h = _norm1(xs, norm1_g)
    proj = _proj(h, w_t)
    y = _pool_fwd(proj, pool_full, pool_scale)
    y, o, states = _hgrn_fwd(proj, lb_logits, rec_norm_g, y)
    ymm = _out_proj(y, w_out_full)
    dz, dzb, sq, dgf = _loss_head(xs, ymm, target, gf)
    loss = lax.psum(0.5 * jnp.sum(sq) / D, AXES)

    dymix, gwout_f, gwout_b = _out_proj_bwd(dzb, w_out_full, y)
    dproj, drecg, dlb = _hgrn_bwd(proj, lb_logits, rec_norm_g, o, states, dymix)
    dproj, gpool, dscale = _pool_bwd(proj, pool_full, pool_scale, dymix, dproj)
    gw_f, gw_b = _proj_bwd_w(h, dproj)
    dh = _proj_bwd_x(dproj, w_t)
    grad_x, dg1 = _norm1_bwd(xs, norm1_g, dh, dz)

    gpool_s = gpool.reshape(NGROUP, NDEV, GROUP // NDEV, GROUP).transpose(1, 0, 2, 3)
    small = _rows8(dg1, dscale, dlb, dlb, drecg, dgf)
    own_in, r_in, own_out, r_out, own_pool, r_pool, r_small = _scatter_grads(
        gw_f, gw_b, gwout_f.reshape(NDEV, DMIX // NDEV, D), gwout_b.reshape(NDEV, DMIX // NDEV, D), gpool_s, small)

    col = lambda i: (0, 0, i)
    g_win, d_win, m_win, v_win = _reduce_adam(
        "adam_w_in", own_in, r_in, w_in, m_w_in, v_w_in, (3,),
        pl.BlockSpec((None, D, TILE), lambda i: (i, 0, 0)),
        pl.BlockSpec((NDEV - 1, None, D, TILE), lambda i: (0, i, 0, 0)),
        pl.BlockSpec((None, D, TILE), col))
    g_wout, d_wout, m_wout, v_wout = _reduce_adam(
        "adam_w_out", own_out, r_out, w_out, m_w_out, v_w_out, (1,),
        pl.BlockSpec((DMIX // NDEV, D), lambda i: (0, 0)),
        pl.BlockSpec((NDEV - 1, DMIX // NDEV, D), lambda i: (0, 0, 0)),
        pl.BlockSpec((None, DMIX // NDEV, D), lambda i: (0, 0, 0)))
    g_pool, d_pool, m_pool, v_pool = _reduce_adam(
        "adam_pool_w", own_pool, r_pool, pool_w, m_pool_w, v_pool_w, (1,),
        pl.BlockSpec((NGROUP, GROUP // NDEV, GROUP), lambda i: (0, 0, 0)),
        pl.BlockSpec((NDEV - 1, NGROUP, GROUP // NDEV, GROUP), lambda i: (0, 0, 0, 0)),
        pl.BlockSpec((None, NGROUP, GROUP // NDEV, GROUP), lambda i: (0, 0, 0, 0)))

    g_s, d_s, m_s, v_s = _small_adam(
        r_small,
        _rows8(norm1_g, pool_scale, lb_logits, rec_norm_g, final_norm_g),
        _rows8(m_norm1_g, m_pool_scale, m_lb_logits, m_rec_norm_g, m_final_norm_g),
        _rows8(v_norm1_g, v_pool_scale, v_lb_logits, v_rec_norm_g, v_final_norm_g))

    def small_outs(a):
        return a[0:1], a[1:2], a[2:4], a[4:5], a[5]

    def outs(small_a, win, pool, wout):
        n1, ps, lbl, rg, fg = small_outs(small_a)
        return n1, win, pool, ps, lbl, rg, wout, fg

    return (loss, grad_x[None],
            *outs(g_s, g_win, g_pool, g_wout), *outs(d_s, d_win, d_pool, d_wout),
            *outs(m_s, m_win, m_pool, m_wout), *outs(v_s, v_win, v_pool, v_wout))
```

```python
import functools

import jax
import jax.numpy as jnp
from jax import lax
from jax.experimental import pallas as pl
from jax.experimental.pallas import tpu as pltpu

T = 2048
D = 1024
NSEG = 6
NTILE = 24
TILE = 256
DMIX = 2048
NDEV = 8
HEAD = 128
NHEAD = 8
CHUNK = 64
NCHUNK = T // CHUNK
NB = 8
NGRP = NCHUNK // NB
NGROUP = 4
GROUP = 256
EPS = 1e-6
EXP_CAP = 80.0
MESH = pl.DeviceIdType.MESH
AXES = ("x", "y", "c")

ADAM_LR = 0.001
ADAM_B1 = 0.9
ADAM_B2 = 0.999
ADAM_EPS = 1e-08
ADAM_WD = 0.01
ADAM_STEP = 10
BC1 = 1.0 - ADAM_B1 ** ADAM_STEP
BC2 = 1.0 - ADAM_B2 ** ADAM_STEP

MIB = 1 << 20


def _params(sem=None, vmem_mib=48):
    return pltpu.CompilerParams(dimension_semantics=sem, vmem_limit_bytes=vmem_mib * MIB)


def _sigmoid(v):
    return 1.0 / (1.0 + jnp.exp(-v))


def _dot(a, b, ca, cb, precision=None):
    return lax.dot_general(a, b, (((ca,), (cb,)), ((), ())), precision=precision,
                           preferred_element_type=jnp.float32)


def _bf(v):
    return v.astype(jnp.bfloat16)


def _place():
    x, y, c = lax.axis_index("x"), lax.axis_index("y"), lax.axis_index("c")
    return x, y, c, 4 * x + 2 * y + c


def _peer(x, y, c, r):
    return (x ^ ((r >> 2) & 1), y ^ ((r >> 1) & 1), c ^ (r & 1))


def _gather_weights(w_t3, w_out_b, pool_b):
    def body(win_ref, wout_ref, pool_ref, win_o, wout_o, pool_o, send_sems, recv_sems, loc_sems):
        x, y, c, me = _place()
        srcs = (win_ref, wout_ref, pool_ref)

        def dst(w, idx):
            if w == 0:
                return win_o.at[pl.ds(3 * idx, 3)]
            return (wout_o if w == 1 else pool_o).at[idx]

        locs = [pltpu.make_async_copy(srcs[w], dst(w, me), loc_sems.at[w]) for w in range(3)]
        for cp in locs:
            cp.start()

        def copy(r, w, slot):
            k = (r - 1) * 3 + w
            return pltpu.make_async_remote_copy(
                src_ref=srcs[w], dst_ref=dst(w, slot), send_sem=send_sems.at[k], recv_sem=recv_sems.at[k],
                device_id=_peer(x, y, c, r), device_id_type=MESH)

        sends = [copy(r, w, me) for r in range(1, NDEV) for w in range(3)]
        for cp in sends:
            cp.start()
        for r in range(1, NDEV):
            for w in range(3):
                copy(r, w, me ^ r).wait_recv()
        for cp in sends:
            cp.wait_send()
        for cp in locs:
            cp.wait()

    any_spec = pl.BlockSpec(memory_space=pl.ANY)
    return pl.pallas_call(
        body, name="gather_weights",
        out_shape=(jax.ShapeDtypeStruct((NTILE, D, TILE), jnp.bfloat16),
                   jax.ShapeDtypeStruct((NDEV, DMIX // NDEV, D), jnp.bfloat16),
                   jax.ShapeDtypeStruct((NDEV, NGROUP, GROUP // NDEV, GROUP), jnp.bfloat16)),
        in_specs=[any_spec] * 3, out_specs=(any_spec,) * 3,
        scratch_shapes=[pltpu.SemaphoreType.DMA((21,)), pltpu.SemaphoreType.DMA((21,)),
                        pltpu.SemaphoreType.DMA((3,))],
    )(w_t3, w_out_b, pool_b)


def _scatter_grads(gw_f, gw_b, gwout_f, gwout_b, gpool_f, small):
    def body(gwf, gwb, gof, gob, gpf, sm, own_in, r_in, own_out, r_out, own_pool, r_pool, r_small,
             send_sems, recv_sems, loc_sems):
        x, y, c, me = _place()
        locs = [pltpu.make_async_copy(gwf.at[pl.ds(3 * me, 3)], own_in, loc_sems.at[0]),
                pltpu.make_async_copy(gof.at[me], own_out, loc_sems.at[1]),
                pltpu.make_async_copy(gpf.at[me], own_pool, loc_sems.at[2]),
                pltpu.make_async_copy(sm, r_small.at[me], loc_sems.at[3])]
        for cp in locs:
            cp.start()

        def copy(r, w, to_idx, src_of_small):
            k = (r - 1) * 4 + w
            if w == 0:
                src, dstr = gwb.at[pl.ds(3 * to_idx, 3)], r_in.at[r - 1]
            elif w == 1:
                src, dstr = gob.at[to_idx], r_out.at[r - 1]
            elif w == 2:
                src, dstr = gpf.at[to_idx], r_pool.at[r - 1]
            else:
                src, dstr = sm, r_small.at[src_of_small]
            return pltpu.make_async_remote_copy(
                src_ref=src, dst_ref=dstr, send_sem=send_sems.at[k], recv_sem=recv_sems.at[k],
                device_id=_peer(x, y, c, r), device_id_type=MESH)

        sends = [copy(r, w, me ^ r, me) for r in range(1, NDEV) for w in range(4)]
        for cp in sends:
            cp.start()
        for r in range(1, NDEV):
            for w in range(4):
                copy(r, w, me, me ^ r).wait_recv()
        for cp in sends:
            cp.wait_send()
        for cp in locs:
            cp.wait()

    any_spec = pl.BlockSpec(memory_space=pl.ANY)
    f32, bf16 = jnp.float32, jnp.bfloat16
    return pl.pallas_call(
        body, name="scatter_grads",
        out_shape=(jax.ShapeDtypeStruct((3, D, TILE), f32), jax.ShapeDtypeStruct((NDEV - 1, 3, D, TILE), bf16),
                   jax.ShapeDtypeStruct((DMIX // NDEV, D), f32), jax.ShapeDtypeStruct((NDEV - 1, DMIX // NDEV, D), bf16),
                   jax.ShapeDtypeStruct((NGROUP, GROUP // NDEV, GROUP), f32),
                   jax.ShapeDtypeStruct((NDEV - 1, NGROUP, GROUP // NDEV, GROUP), f32),
                   jax.ShapeDtypeStruct((NDEV, 8, D), f32)),
        in_specs=[any_spec] * 6, out_specs=(any_spec,) * 7,
        scratch_shapes=[pltpu.SemaphoreType.DMA((28,)), pltpu.SemaphoreType.DMA((28,)),
                        pltpu.SemaphoreType.DMA((4,))],
    )(gw_f, gw_b, gwout_f, gwout_b, gpool_f, small)


def _norm1(x, g1):
    rows = 512

    def body(x_ref, g_ref, h_ref):
        xv = x_ref[...]
        r = lax.rsqrt(jnp.mean(xv * xv, axis=-1, keepdims=True) + EPS)
        h_ref[...] = _bf(xv * r * g_ref[...])

    return pl.pallas_call(
        body, name="norm1", grid=(T // rows,),
        in_specs=[pl.BlockSpec((rows, D), lambda i: (i, 0)), pl.BlockSpec((1, D), lambda i: (0, 0))],
        out_specs=pl.BlockSpec((rows, D), lambda i: (i, 0)),
        out_shape=jax.ShapeDtypeStruct((T, D), jnp.bfloat16),
        compiler_params=_params(("parallel",)),
    )(x, g1)


def _seg_tiles(s):
    return (s + 2) % NSEG


def _proj(h, w_t):
    tm = 1024

    def body(h_ref, w_ref, o_ref):
        hv = h_ref[...]
        for i in range(4):
            o_ref[:, i * TILE:(i + 1) * TILE] = _dot(hv, w_ref[i], 1, 0)

    return pl.pallas_call(
        body, name="proj", grid=(T // tm, NSEG),
        in_specs=[pl.BlockSpec((tm, D), lambda m, s: (m, 0)),
                  pl.BlockSpec((4, D, TILE), lambda m, s: (_seg_tiles(s), 0, 0))],
        out_specs=pl.BlockSpec((None, tm, D), lambda m, s: (s, m, 0)),
        out_shape=jax.ShapeDtypeStruct((NSEG, T, D), jnp.float32),
        compiler_params=_params(("parallel", "parallel")),
    )(h, w_t)


def _row_ids(shape):
    return lax.broadcasted_iota(jnp.int32, shape, 0)


def _shift_down(a, k, rows):
    return jnp.where(rows >= k, pltpu.roll(a, k, 0), 0.0)


def _shift_up(a, k, rows):
    return jnp.where(rows < T - k, pltpu.roll(a, T - k, 0), 0.0)


def _window_sum(u, gidx, rows, shift):
    s2 = u + shift(u, 1, rows)
    s4 = s2 + shift(s2, 2, rows)
    s8 = s4 + shift(s4, 4, rows)
    s16 = s8 + shift(s8, 8, rows)
    return jnp.where(gidx == 0, s2, jnp.where(gidx == 1, s4, jnp.where(gidx == 2, s8, s16)))


def _pool_count(gidx, rows):
    width = lax.shift_left(jnp.int32(2), gidx)
    return jnp.minimum(rows + 1, width).astype(jnp.float32)


def _pool_fwd(proj, pool_w, pool_scale):
    def body(p_ref, w_ref, sc_ref, y_ref):
        gidx = pl.program_id(0)
        u, pg = p_ref[0], p_ref[1]
        rows = _row_ids(u.shape)
        d = _window_sum(u, gidx, rows, _shift_down) / _pool_count(gidx, rows) - u
        mixed = _dot(_bf(d), w_ref[...], 1, 0)
        y_ref[...] = _bf(mixed * sc_ref[...] * (pg * _sigmoid(pg)))

    return pl.pallas_call(
        body, name="pool_fwd", grid=(NGROUP,),
        in_specs=[pl.BlockSpec((2, T, GROUP), lambda g: (2, 0, g)),
                  pl.BlockSpec((None, GROUP, GROUP), lambda g: (g, 0, 0)),
                  pl.BlockSpec((1, GROUP), lambda g: (0, g))],
        out_specs=pl.BlockSpec((T, GROUP), lambda g: (0, g)),
        out_shape=jax.ShapeDtypeStruct((T, DMIX), jnp.bfloat16),
        compiler_params=_params(("parallel",)),
    )(proj, pool_w, pool_scale)


def _tri(lower):
    r = lax.broadcasted_iota(jnp.int32, (CHUNK, CHUNK), 0)
    c = lax.broadcasted_iota(jnp.int32, (CHUNK, CHUNK), 1)
    return (r >= c) if lower else (r <= c)


def _sum_rows_matrix():
    shape = (CHUNK + 16, CHUNK)
    r, c = lax.broadcasted_iota(jnp.int32, shape, 0), lax.broadcasted_iota(jnp.int32, shape, 1)
    run = jnp.where(c <= r, 1.0, 0.0)
    half = jnp.where(c < CHUNK // 2, 1.0, 0.0)
    return _bf(jnp.where(r < CHUNK, run, jnp.where(r < CHUNK + 8, 1.0, half)))


def _rev_sum_matrix():
    shape = (CHUNK, 2 * CHUNK)
    r, c = lax.broadcasted_iota(jnp.int32, shape, 0), lax.broadcasted_iota(jnp.int32, shape, 1)
    return _bf(jnp.where(c < CHUNK, jnp.where(c >= r, 1.0, 0.0), jnp.where(c - CHUNK < r, 1.0, 0.0)))


def _split3(a):
    a1 = _bf(a)
    r1 = a - a1.astype(jnp.float32)
    a2 = _bf(r1)
    return [a1, a2, _bf(r1 - a2.astype(jnp.float32))]


def _exact_sums(mat, pieces):
    x = jnp.concatenate([s for p in pieces for s in _split3(p)], axis=1)
    r = _dot(mat, x, 1, 0)
    return [r[:, 3 * j * HEAD:(3 * j + 1) * HEAD] + r[:, (3 * j + 1) * HEAD:(3 * j + 2) * HEAD]
            + r[:, (3 * j + 2) * HEAD:(3 * j + 3) * HEAD] for j in range(len(pieces))]


def _gates(qv, fl, lb):
    sq = _sigmoid(qv)
    sg = _sigmoid(fl)
    f = lb + (1.0 - lb) * sg
    return dict(sq=sq, qs=qv * sq, sg=sg, f=f, kk=1.0 - f, g=jnp.log(f))


def _decays(sums):
    big_g = sums[:CHUNK]
    total = sums[CHUNK:CHUNK + 8]
    g_last = jnp.tile(total, (CHUNK // 8, 1))
    g_mid = jnp.tile(sums[CHUNK + 8:], (CHUNK // 8, 1))
    return dict(
        e_q=jnp.exp(big_g),
        e_k=jnp.exp(g_last - big_g),
        e_qm=jnp.exp(jnp.minimum(big_g - g_mid, EXP_CAP)),
        e_km=jnp.exp(jnp.minimum(g_mid - big_g, EXP_CAP)),
        total8=jnp.exp(total),
        state=jnp.exp(jnp.tile(total, (HEAD // 8, 1))))


def _group_rows(gi):
    return [pl.ds(pl.multiple_of((gi * NB + j) * CHUNK, CHUNK), CHUNK) for j in range(NB)]


def _lower_bound(lb_ref):
    return _sigmoid(lb_ref[0:1, :] - lb_ref[1:2, :])


def _hgrn_fwd(proj, lb_logits, rec_g, y_in):
    def body(p_ref, lb_ref, rg_ref, y_any, y_ref, o_ref, st_ref):
        del y_any
        lb = _lower_bound(lb_ref)
        causal = _tri(True)
        smat = _sum_rows_matrix()

        def group(gi, st):
            rows = _group_rows(gi)
            ts = [_gates(p_ref[0, r, :], p_ref[1, r, :], lb) for r in rows]
            ds = [_decays(s) for s in _exact_sums(smat, [t["g"] for t in ts])]
            vs = [_bf(p_ref[2, r, :]) for r in rows]
            q_m = [_bf(t["qs"] * d["e_qm"]) for t, d in zip(ts, ds)]
            k_m = [_bf(t["kk"] * d["e_km"]) for t, d in zip(ts, ds)]
            q_e = [_bf(t["qs"] * d["e_q"]) for t, d in zip(ts, ds)]
            k_e = [_bf(t["kk"] * d["e_k"]) for t, d in zip(ts, ds)]
            a = [_bf(jnp.where(causal, _dot(q_m[j], k_m[j], 1, 1), 0.0)) for j in range(NB)]
            intra = [_dot(a[j], vs[j], 1, 0) for j in range(NB)]
            upd = [_dot(vs[j], k_e[j], 0, 0) for j in range(NB)]
            for j in range(NB):
                st_ref[gi * NB + j] = st
                o_ref[rows[j], :] = intra[j] + _dot(q_e[j], _bf(st), 1, 1)
                st = st * ds[j]["state"] + upd[j]
            return st

        lax.fori_loop(0, NGRP, group, jnp.zeros((HEAD, HEAD), jnp.float32))
        o = o_ref[...]
        rn = o * lax.rsqrt(jnp.mean(o * o, axis=-1, keepdims=True) + EPS)
        gate = p_ref[3]
        y_ref[...] = _bf(rn * rg_ref[...] * (gate * _sigmoid(gate)))

    return pl.pallas_call(
        body, name="hgrn_fwd", grid=(NHEAD,),
        in_specs=[pl.BlockSpec((4, T, HEAD), lambda h: (0, 0, h)),
                  pl.BlockSpec((2, HEAD), lambda h: (0, h)),
                  pl.BlockSpec((1, HEAD), lambda h: (0, h)),
                  pl.BlockSpec(memory_space=pl.ANY)],
        out_specs=(pl.BlockSpec((T, HEAD), lambda h: (0, NHEAD + h)),
                   pl.BlockSpec((T, HEAD), lambda h: (0, h)),
                   pl.BlockSpec((None, NCHUNK, HEAD, HEAD), lambda h: (h, 0, 0, 0))),
        out_shape=(jax.ShapeDtypeStruct((T, DMIX), jnp.bfloat16),
                   jax.ShapeDtypeStruct((T, D), jnp.float32),
                   jax.ShapeDtypeStruct((NHEAD, NCHUNK, HEAD, HEAD), jnp.float32)),
        input_output_aliases={3: 0},
        compiler_params=_params(("parallel",)),
    )(proj, lb_logits, rec_g, y_in)


def _out_proj(y, w_out):
    tm = 512

    def body(y_ref, w_ref, o_ref):
        o_ref[...] = _dot(y_ref[...], w_ref[...], 1, 0)

    return pl.pallas_call(
        body, name="out_proj", grid=(T // tm,),
        in_specs=[pl.BlockSpec((tm, DMIX), lambda m: (m, 0)), pl.BlockSpec((DMIX, D), lambda m: (0, 0))],
        out_specs=pl.BlockSpec((tm, D), lambda m: (m, 0)),
        out_shape=jax.ShapeDtypeStruct((T, D), jnp.float32),
        compiler_params=_params(("parallel",)),
    )(y, w_out)


def _loss_head(x, ymm, target, gf):
    rows = 256

    def body(x_ref, y_ref, t_ref, g_ref, dz_ref, dzb_ref, sq_ref, dg_ref):
        z = x_ref[...] + y_ref[...]
        r = lax.rsqrt(jnp.mean(z * z, axis=-1, keepdims=True) + EPS)
        zhat = z * r
        err = zhat * g_ref[...] - t_ref[...]
        dy = err * (1.0 / D)
        gdy = dy * g_ref[...]
        dz = r * (gdy - zhat * jnp.mean(zhat * gdy, axis=-1, keepdims=True))
        dz_ref[...] = dz
        dzb_ref[...] = _bf(dz)
        sq = jnp.sum(err * err, axis=0, keepdims=True)
        dg = jnp.sum(zhat * dy, axis=0, keepdims=True)

        @pl.when(pl.program_id(0) == 0)
        def _():
            sq_ref[...] = sq
            dg_ref[...] = dg

        @pl.when(pl.program_id(0) != 0)
        def _():
            sq_ref[...] += sq
            dg_ref[...] += dg

    tile = pl.BlockSpec((rows, D), lambda i: (i, 0))
    vec = pl.BlockSpec((1, D), lambda i: (0, 0))
    return pl.pallas_call(
        body, name="loss_head", grid=(T // rows,),
        in_specs=[tile, tile, tile, vec], out_specs=(tile, tile, vec, vec),
        out_shape=(jax.ShapeDtypeStruct((T, D), jnp.float32), jax.ShapeDtypeStruct((T, D), jnp.bfloat16),
                   jax.ShapeDtypeStruct((1, D), jnp.float32), jax.ShapeDtypeStruct((1, D), jnp.float32)),
        compiler_params=_params(("arbitrary",)),
    )(x, ymm, target, gf)


def _out_proj_bwd(dzb, w_out, y):
    tn = 512

    def body(dz_ref, w_ref, y_ref, dy_ref, gw_ref, gwb_ref):
        dz = dz_ref[...]
        dy_ref[...] = _dot(dz, w_ref[...], 1, 1)
        gw = _dot(y_ref[...], dz, 0, 0)
        gw_ref[...] = gw
        gwb_ref[...] = _bf(gw)

    return pl.pallas_call(
        body, name="out_proj_bwd", grid=(DMIX // tn,),
        in_specs=[pl.BlockSpec((T, D), lambda n: (0, 0)), pl.BlockSpec((tn, D), lambda n: (n, 0)),
                  pl.BlockSpec((T, tn), lambda n: (0, n))],
        out_specs=(pl.BlockSpec((T, tn), lambda n: (0, n)), pl.BlockSpec((tn, D), lambda n: (n, 0)),
                   pl.BlockSpec((tn, D), lambda n: (n, 0))),
        out_shape=(jax.ShapeDtypeStruct((T, DMIX), jnp.float32), jax.ShapeDtypeStruct((DMIX, D), jnp.float32),
                   jax.ShapeDtypeStruct((DMIX, D), jnp.bfloat16)),
        compiler_params=_params(("parallel",)),
    )(dzb, w_out, y)


def _hgrn_bwd(proj, lb_logits, rec_g, o, states, dymix):
    def body(p_ref, lb_ref, rg_ref, o_ref, st_ref, dy_ref, dp_ref, drg_ref, dlb_ref, do_ref):
        lb = _lower_bound(lb_ref)
        causal = _tri(True)
        smat, rmat = _sum_rows_matrix(), _rev_sum_matrix()

        o = o_ref[...]
        rs = lax.rsqrt(jnp.mean(o * o, axis=-1, keepdims=True) + EPS)
        rn = o * rs
        gate = p_ref[3]
        sgate = _sigmoid(gate)
        dyv = dy_ref[...]
        d_r = dyv * (gate * sgate)
        dp_ref[3] = _bf(dyv * (rn * rg_ref[...]) * (sgate * (1.0 + gate * (1.0 - sgate))))
        drg_ref[...] = jnp.sum(d_r * rn, axis=0, keepdims=True)
        drn = d_r * rg_ref[...]
        do_ref[...] = rs * (drn - rn * jnp.mean(rn * drn, axis=-1, keepdims=True))

        def group(i, carry):
            dst, dlb = carry
            gi = NGRP - 1 - i
            rows = _group_rows(gi)
            span = range(NB)
            qvs = [p_ref[0, r, :] for r in rows]
            ts = [_gates(qv, p_ref[1, r, :], lb) for qv, r in zip(qvs, rows)]
            ds = [_decays(s) for s in _exact_sums(smat, [t["g"] for t in ts])]
            vs = [_bf(p_ref[2, r, :]) for r in rows]
            dos = [_bf(do_ref[r, :]) for r in rows]
            sts = [st_ref[gi * NB + j] for j in span]
            qe_f = [t["qs"] * d["e_q"] for t, d in zip(ts, ds)]
            ke_f = [t["kk"] * d["e_k"] for t, d in zip(ts, ds)]
            q_e, k_e = [_bf(a) for a in qe_f], [_bf(a) for a in ke_f]
            q_m = [_bf(t["qs"] * d["e_qm"]) for t, d in zip(ts, ds)]
            k_m = [_bf(t["kk"] * d["e_km"]) for t, d in zip(ts, ds)]
            a = [_bf(jnp.where(causal, _dot(q_m[j], k_m[j], 1, 1), 0.0)) for j in span]
            da = [_bf(jnp.where(causal, _dot(dos[j], vs[j], 1, 1), 0.0)) for j in span]
            dqm = [_dot(da[j], k_m[j], 1, 0) for j in span]
            dkm = [_dot(da[j], q_m[j], 0, 0) for j in span]
            dv_in = [_dot(a[j], dos[j], 0, 0) for j in span]
            dqe = [_dot(dos[j], _bf(sts[j]), 1, 0) for j in span]
            grow = [_dot(dos[j], q_e[j], 0, 0) for j in span]
            dke, carried = [None] * NB, [None] * NB
            for j in reversed(span):
                dst_b = _bf(dst)
                dke[j] = _dot(vs[j], dst_b, 1, 0)
                dp_ref[2, rows[j], :] = _bf(dv_in[j] + _dot(k_e[j], dst_b, 1, 1))
                carried[j] = ds[j]["total8"] * jnp.sum(dst * sts[j], axis=0, keepdims=True)
                dst = dst * ds[j]["state"] + grow[j]
            kdk = [ke_f[j] * dke[j] for j in span]
            pos = [(q_m[j].astype(jnp.float32) * dqm[j] - k_m[j].astype(jnp.float32) * dkm[j]) + qe_f[j] * dqe[j]
                   for j in span]
            dgs = _exact_sums(rmat, [jnp.concatenate([pos[j], kdk[j]], axis=0) for j in span])
            for j in span:
                t, d = ts[j], ds[j]
                dg = dgs[j] + jnp.tile(carried[j], (CHUNK // 8, 1))
                dqs = dqm[j] * d["e_qm"] + dqe[j] * d["e_q"]
                dkk = dkm[j] * d["e_km"] + dke[j] * d["e_k"]
                df = dg / t["f"] - dkk
                dp_ref[1, rows[j], :] = _bf(df * (1.0 - lb) * (t["sg"] * (1.0 - t["sg"])))
                dp_ref[0, rows[j], :] = _bf(dqs * (t["sq"] * (1.0 + qvs[j] * (1.0 - t["sq"]))))
                dlb = dlb + df * (1.0 - t["sg"])
            return dst, dlb

        _, dlb = lax.fori_loop(0, NGRP, group, (jnp.zeros((HEAD, HEAD), jnp.float32),
                                                jnp.zeros((CHUNK, HEAD), jnp.float32)))
        dlb_ref[...] = jnp.sum(dlb, axis=0, keepdims=True)

    vec = pl.BlockSpec((1, HEAD), lambda h: (0, h))
    return pl.pallas_call(
        body, name="hgrn_bwd", grid=(NHEAD,),
        in_specs=[pl.BlockSpec((4, T, HEAD), lambda h: (0, 0, h)),
                  pl.BlockSpec((2, HEAD), lambda h: (0, h)), vec,
                  pl.BlockSpec((T, HEAD), lambda h: (0, h)),
                  pl.BlockSpec((None, NCHUNK, HEAD, HEAD), lambda h: (h, 0, 0, 0)),
                  pl.BlockSpec((T, HEAD), lambda h: (0, NHEAD + h))],
        out_specs=(pl.BlockSpec((4, T, HEAD), lambda h: (0, 0, h)), vec, vec),
        out_shape=(jax.ShapeDtypeStruct((NSEG, T, D), jnp.bfloat16),
                   jax.ShapeDtypeStruct((1, D), jnp.float32), jax.ShapeDtypeStruct((1, D), jnp.float32)),
        scratch_shapes=[pltpu.VMEM((T, HEAD), jnp.float32)],
        compiler_params=_params(("parallel",)),
    )(proj, lb_logits, rec_g, o, states, dymix)


def _pool_bwd(proj, pool_w, pool_scale, dymix, dproj_in):
    def body(p_ref, w_ref, sc_ref, dy_ref, dp_any, dp_ref, gw_ref, gs_ref):
        del dp_any
        gidx = pl.program_id(0)
        u, pg = p_ref[0], p_ref[1]
        rows = _row_ids(u.shape)
        count = _pool_count(gidx, rows)
        d = _bf(_window_sum(u, gidx, rows, _shift_down) / count - u)
        mixed = _dot(d, w_ref[...], 1, 0)
        spg = _sigmoid(pg)
        dyv = dy_ref[...]
        d_p = dyv * (pg * spg)
        dp_ref[1] = _bf(dyv * (mixed * sc_ref[...]) * (spg * (1.0 + pg * (1.0 - spg))))
        gs_ref[...] = jnp.sum(d_p * mixed, axis=0, keepdims=True)
        dmixed = _bf(d_p * sc_ref[...])
        gw_ref[...] = _dot(d, dmixed, 0, 0)
        dd = _dot(dmixed, w_ref[...], 1, 1)
        dp_ref[0] = _bf(_window_sum(dd / count, gidx, rows, _shift_up) - dd)

    return pl.pallas_call(
        body, name="pool_bwd", grid=(NGROUP,),
        in_specs=[pl.BlockSpec((2, T, GROUP), lambda g: (2, 0, g)),
                  pl.BlockSpec((None, GROUP, GROUP), lambda g: (g, 0, 0)),
                  pl.BlockSpec((1, GROUP), lambda g: (0, g)),
                  pl.BlockSpec((T, GROUP), lambda g: (0, g)),
                  pl.BlockSpec(memory_space=pl.ANY)],
        out_specs=(pl.BlockSpec((2, T, GROUP), lambda g: (2, 0, g)),
                   pl.BlockSpec((None, GROUP, GROUP), lambda g: (g, 0, 0)),
                   pl.BlockSpec((1, GROUP), lambda g: (0, g))),
        out_shape=(jax.ShapeDtypeStruct((NSEG, T, D), jnp.bfloat16),
                   jax.ShapeDtypeStruct((NGROUP, GROUP, GROUP), jnp.float32),
                   jax.ShapeDtypeStruct((1, D), jnp.float32)),
        input_output_aliases={4: 0},
        compiler_params=_params(("parallel",)),
    )(proj, pool_w, pool_scale, dymix, dproj_in)


def _proj_bwd_w(h, dproj):
    def body(h_ref, dp_ref, gw_ref, gwb_ref):
        gw = _dot(h_ref[...], dp_ref[...], 0, 0)
        for i in range(4):
            gw_ref[i] = gw[:, i * TILE:(i + 1) * TILE]
            gwb_ref[i] = _bf(gw[:, i * TILE:(i + 1) * TILE])

    out = pl.BlockSpec((4, D, TILE), lambda s: (_seg_tiles(s), 0, 0))
    return pl.pallas_call(
        body, name="proj_bwd_w", grid=(NSEG,),
        in_specs=[pl.BlockSpec((T, D), lambda s: (0, 0)), pl.BlockSpec((None, T, D), lambda s: (s, 0, 0))],
        out_specs=(out, out),
        out_shape=(jax.ShapeDtypeStruct((NTILE, D, TILE), jnp.float32),
                   jax.ShapeDtypeStruct((NTILE, D, TILE), jnp.bfloat16)),
        compiler_params=_params(("parallel",)),
    )(h, dproj)


def _proj_bwd_x(dproj, w_t):
    tm = 1024

    def body(dp_ref, w_ref, o_ref, wcat):
        for i in range(4):
            wcat[:, i * TILE:(i + 1) * TILE] = w_ref[i]
        r = _dot(dp_ref[...], wcat[...], 1, 1)

        @pl.when(pl.program_id(1) == 0)
        def _():
            o_ref[...] = r

        @pl.when(pl.program_id(1) != 0)
        def _():
            o_ref[...] += r

    return pl.pallas_call(
        body, name="proj_bwd_x", grid=(T // tm, NSEG),
        in_specs=[pl.BlockSpec((None, tm, D), lambda m, s: (s, m, 0)),
                  pl.BlockSpec((4, D, TILE), lambda m, s: (_seg_tiles(s), 0, 0))],
        out_specs=pl.BlockSpec((tm, D), lambda m, s: (m, 0)),
        out_shape=jax.ShapeDtypeStruct((T, D), jnp.float32),
        scratch_shapes=[pltpu.VMEM((D, D), jnp.bfloat16)],
        compiler_params=_params(("parallel", "arbitrary")),
    )(dproj, w_t)


def _norm1_bwd(x, g1, dh, dz):
    rows = 256

    def body(x_ref, g_ref, dh_ref, dz_ref, dx_ref, dg_ref):
        xv = x_ref[...]
        r = lax.rsqrt(jnp.mean(xv * xv, axis=-1, keepdims=True) + EPS)
        xhat = xv * r
        dhv = dh_ref[...]
        gdh = dhv * g_ref[...]
        dx_ref[...] = dz_ref[...] + r * (gdh - xhat * jnp.mean(xhat * gdh, axis=-1, keepdims=True))
        dg = jnp.sum(xhat * dhv, axis=0, keepdims=True)

        @pl.when(pl.program_id(0) == 0)
        def _():
            dg_ref[...] = dg

        @pl.when(pl.program_id(0) != 0)
        def _():
            dg_ref[...] += dg

    tile = pl.BlockSpec((rows, D), lambda i: (i, 0))
    vec = pl.BlockSpec((1, D), lambda i: (0, 0))
    return pl.pallas_call(
        body, name="norm1_bwd", grid=(T // rows,),
        in_specs=[tile, vec, tile, tile], out_specs=(tile, vec),
        out_shape=(jax.ShapeDtypeStruct((T, D), jnp.float32), jax.ShapeDtypeStruct((1, D), jnp.float32)),
        compiler_params=_params(("arbitrary",)),
    )(x, g1, dh, dz)


def _adamw(w, g, m, v):
    m_new = ADAM_B1 * m + (1.0 - ADAM_B1) * g
    v_new = ADAM_B2 * v + (1.0 - ADAM_B2) * (g * g)
    delta = -ADAM_LR * ((m_new / BC1) / (jnp.sqrt(v_new / BC2) + ADAM_EPS) + ADAM_WD * w)
    return delta, m_new, v_new


def _reduce_adam(name, own, recv, w, m, v, grid, own_spec, recv_spec, w_spec):
    def body(own_ref, recv_ref, w_ref, m_ref, v_ref, g_ref, d_ref, mo_ref, vo_ref):
        g = own_ref[...]
        for r in range(NDEV - 1):
            g = g + recv_ref[r].astype(jnp.float32)
        delta, m_new, v_new = _adamw(w_ref[...], g, m_ref[...], v_ref[...])
        g_ref[...] = g
        d_ref[...] = delta
        mo_ref[...] = m_new
        vo_ref[...] = v_new

    shape = jax.ShapeDtypeStruct(w.shape, jnp.float32)
    return pl.pallas_call(
        body, name=name, grid=grid,
        in_specs=[own_spec, recv_spec, w_spec, w_spec, w_spec], out_specs=(w_spec,) * 4,
        out_shape=(shape,) * 4,
        compiler_params=_params(("parallel",)),
    )(own, recv, w, m, v)


def _small_adam(parts, w, m, v):
    def body(p_ref, w_ref, m_ref, v_ref, g_ref, d_ref, mo_ref, vo_ref):
        g = p_ref[0]
        for s in range(1, NDEV):
            g = g + p_ref[s]
        wv = w_ref[...]
        rows = _row_ids(wv.shape)
        other = jnp.where(rows == 2, pltpu.roll(wv, 7, 0), jnp.where(rows == 3, pltpu.roll(wv, 1, 0), 0.0))
        lbv = _sigmoid(wv - other)
        sign = jnp.where(rows == 2, 1.0, -1.0)
        g = jnp.where((rows == 2) | (rows == 3), sign * g * lbv * (1.0 - lbv), g)
        delta, m_new, v_new = _adamw(wv, g, m_ref[...], v_ref[...])
        g_ref[...] = g
        d_ref[...] = delta
        mo_ref[...] = m_new
        vo_ref[...] = v_new

    shape = jax.ShapeDtypeStruct((8, D), jnp.float32)
    return pl.pallas_call(body, name="small_adam", out_shape=(shape,) * 4)(parts, w, m, v)


def _rows8(*vecs):
    rows = [a.reshape(-1, D) for a in vecs]
    n = sum(r.shape[0] for r in rows)
    return jnp.concatenate(rows + [jnp.zeros((8 - n, D), jnp.float32)], axis=0)


def kernel(x, norm1_g, w_in, pool_w, pool_scale, lb_logits, rec_norm_g, w_out, final_norm_g, loss_target, m_norm1_g, m_w_in, m_pool_w, m_pool_scale, m_lb_logits, m_rec_norm_g, m_w_out, m_final_norm_g, v_norm1_g, v_w_in, v_pool_w, v_pool_scale, v_lb_logits, v_rec_norm_g, v_w_out, v_final_norm_g):
    xs = x[0]
    target = loss_target[0]
    gf = final_norm_g.reshape(1, D)

    w_t3 = _bf(w_in[0]).reshape(D, 3, TILE).transpose(1, 0, 2)
    w_t, w_out_g, pool_g = _gather_weights(w_t3, _bf(w_out[0]), _bf(pool_w[0]))
    w_out_full = w_out_g.reshape(DMIX, D)
    pool_full = pool_g.transpose(1, 0, 2, 3).reshape(NGROUP, GROUP, GROUP)

    h = _norm1(xs, norm1_g)
    proj = _proj(h, w_t)
    y = _pool_fwd(proj, pool_full, pool_scale)
    y, o, states = _hgrn_fwd(proj, lb_logits, rec_norm_g, y)
    ymm = _out_proj(y, w_out_full)
    dz, dzb, sq, dgf = _loss_head(xs, ymm, target, gf)
    loss = lax.psum(0.5 * jnp.sum(sq) / D, AXES)

    dymix, gwout_f, gwout_b = _out_proj_bwd(dzb, w_out_full, y)
    dproj, drecg, dlb = _hgrn_bwd(proj, lb_logits, rec_norm_g, o, states, dymix)
    dproj, gpool, dscale = _pool_bwd(proj, pool_full, pool_scale, dymix, dproj)
    gw_f, gw_b = _proj_bwd_w(h, dproj)
    dh = _proj_bwd_x(dproj, w_t)
    grad_x, dg1 = _norm1_bwd(xs, norm1_g, dh, dz)

    gpool_s = gpool.reshape(NGROUP, NDEV, GROUP // NDEV, GROUP).transpose(1, 0, 2, 3)
    small = _rows8(dg1, dscale, dlb, dlb, drecg, dgf)
    own_in, r_in, own_out, r_out, own_pool, r_pool, r_small = _scatter_grads(
        gw_f, gw_b, gwout_f.reshape(NDEV, DMIX // NDEV, D), gwout_b.reshape(NDEV, DMIX // NDEV, D), gpool_s, small)

    col = lambda i: (0, 0, i)
    g_win, d_win, m_win, v_win = _reduce_adam(
        "adam_w_in", own_in, r_in, w_in, m_w_in, v_w_in, (3,),
        pl.BlockSpec((None, D, TILE), lambda i: (i, 0, 0)),
        pl.BlockSpec((NDEV - 1, None, D, TILE), lambda i: (0, i, 0, 0)),
        pl.BlockSpec((None, D, TILE), col))
    g_wout, d_wout, m_wout, v_wout = _reduce_adam(
        "adam_w_out", own_out, r_out, w_out, m_w_out, v_w_out, (1,),
        pl.BlockSpec((DMIX // NDEV, D), lambda i: (0, 0)),
        pl.BlockSpec((NDEV - 1, DMIX // NDEV, D), lambda i: (0, 0, 0)),
        pl.BlockSpec((None, DMIX // NDEV, D), lambda i: (0, 0, 0)))
    g_pool, d_pool, m_pool, v_pool = _reduce_adam(
        "adam_pool_w", own_pool, r_pool, pool_w, m_pool_w, v_pool_w, (1,),
        pl.BlockSpec((NGROUP, GROUP // NDEV, GROUP), lambda i: (0, 0, 0)),
        pl.BlockSpec((NDEV - 1, NGROUP, GROUP // NDEV, GROUP), lambda i: (0, 0, 0, 0)),
        pl.BlockSpec((None, NGROUP, GROUP // NDEV, GROUP), lambda i: (0, 0, 0, 0)))

    g_s, d_s, m_s, v_s = _small_adam(
        r_small,
        _rows8(norm1_g, pool_scale, lb_logits, rec_norm_g, final_norm_g),
        _rows8(m_norm1_g, m_pool_scale, m_lb_logits, m_rec_norm_g, m_final_norm_g),
        _rows8(v_norm1_g, v_pool_scale, v_lb_logits, v_rec_norm_g, v_final_norm_g))

    def small_outs(a):
        return a[0:1], a[1:2], a[2:4], a[4:5], a[5]

    def outs(small_a, win, pool, wout):
        n1, ps, lbl, rg, fg = small_outs(small_a)
        return n1, win, pool, ps, lbl, rg, wout, fg

    return (loss, grad_x[None],
            *outs(g_s, g_win, g_pool, g_wout), *outs(d_s, d_win, d_pool, d_wout),
            *outs(m_s, m_win, m_pool, m_wout), *outs(v_s, v_win, v_pool, v_wout))
```

```python
import functools

import jax
import jax.numpy as jnp
from jax import lax
from jax.experimental import pallas as pl
from jax.experimental.pallas import tpu as pltpu

T = 2048
D = 1024
NSEG = 6
NTILE = 24
TILE = 256
DMIX = 2048
NDEV = 8
HEAD = 128
NHEAD = 8
CHUNK = 64
NCHUNK = T // CHUNK
NB = 8
NGRP = NCHUNK // NB
NGROUP = 4
GROUP = 256
EPS = 1e-6
EXP_CAP = 80.0
MESH = pl.DeviceIdType.MESH
AXES = ("x", "y", "c")
ANY = pl.BlockSpec(memory_space=pl.ANY)
HBM = pl.BlockSpec(memory_space=pltpu.HBM)
SEM = pl.BlockSpec(memory_space=pltpu.SEMAPHORE)
EFFECT = pltpu.SideEffectType.DATAFLOW_SIDE_EFFECTING

ADAM_LR = 0.001
ADAM_B1 = 0.9
ADAM_B2 = 0.999
ADAM_EPS = 1e-08
ADAM_WD = 0.01
ADAM_STEP = 10
BC1 = 1.0 - ADAM_B1 ** ADAM_STEP
BC2 = 1.0 - ADAM_B2 ** ADAM_STEP

MIB = 1 << 20


def _params(sem=None, vmem_mib=48):
    return pltpu.CompilerParams(dimension_semantics=sem, vmem_limit_bytes=vmem_mib * MIB)


def _sigmoid(v):
    return 1.0 / (1.0 + jnp.exp(-v))


def _dot(a, b, ca, cb, precision=None):
    return lax.dot_general(a, b, (((ca,), (cb,)), ((), ())), precision=precision,
                           preferred_element_type=jnp.float32)


def _bf(v):
    return v.astype(jnp.bfloat16)


def _place():
    x, y, c = lax.axis_index("x"), lax.axis_index("y"), lax.axis_index("c")
    return x, y, c, 4 * x + 2 * y + c


def _peer(x, y, c, r):
    return (x ^ ((r >> 2) & 1), y ^ ((r >> 1) & 1), c ^ (r & 1))


def _gather_weights(w_t3, w_out_b, pool_b):
    order = (0, 2, 1)

    def body(win_ref, wout_ref, pool_ref, win_o, wout_o, pool_o, send_sems, recv_sems, loc_sems):
        x, y, c, _ = _place()
        srcs = (win_ref, wout_ref, pool_ref)
        me, sibling = (x, y, c), (x, y, 1 - c)
        chips = [(1 - x, y), (x, 1 - y), (1 - x, 1 - y)]

        def slot(w, px, py, pc):
            idx = 4 * px + 2 * py + pc
            if w == 0:
                return win_o.at[pl.ds(3 * idx, 3)]
            return (wout_o if w == 1 else pool_o).at[idx]

        def copy(k, w, block, to, src=None):
            return pltpu.make_async_remote_copy(
                src_ref=slot(w, *block) if src is None else src, dst_ref=slot(w, *block),
                send_sem=send_sems.at[3 * k + w], recv_sem=recv_sems.at[3 * k + w],
                device_id=to, device_id_type=MESH)

        locs = [pltpu.make_async_copy(srcs[w], slot(w, *me), loc_sems.at[w]) for w in range(3)]
        for cp in locs:
            cp.start()
        first = []
        for w in order:
            first += [copy(1 + j, w, me, (*chip, c), src=srcs[w]) for j, chip in enumerate(chips)]
            first.append(copy(0, w, me, sibling, src=srcs[w]))
        for cp in first:
            cp.start()
        passed = []
        for w in order:
            for j, chip in enumerate(chips):
                copy(1 + j, w, (*chip, c), me).wait_recv()
                passed.append(copy(4 + j, w, (*chip, c), sibling))
                passed[-1].start()
        for w in order:
            copy(0, w, sibling, me).wait_recv()
            for j, chip in enumerate(chips):
                copy(4 + j, w, (*chip, 1 - c), me).wait_recv()
        for cp in first + passed:
            cp.wait_send()
        for cp in locs:
            cp.wait()

    return pl.pallas_call(
        body, name="gather_weights",
        out_shape=(jax.ShapeDtypeStruct((NTILE, D, TILE), jnp.bfloat16),
                   jax.ShapeDtypeStruct((NDEV, DMIX // NDEV, D), jnp.bfloat16),
                   jax.ShapeDtypeStruct((NDEV, NGROUP, GROUP // NDEV, GROUP), jnp.bfloat16)),
        in_specs=[ANY] * 3, out_specs=(ANY,) * 3,
        scratch_shapes=[pltpu.SemaphoreType.DMA((21,)), pltpu.SemaphoreType.DMA((21,)),
                        pltpu.SemaphoreType.DMA((3,))],
    )(w_t3, w_out_b, pool_b)


def _split_start(name, arrays, n_copies, plan):
    k = len(arrays)

    def body(*refs):
        send_sems, recv_sems, token = refs[k], refs[k + 1], refs[-1]
        for i, (src, dst, to) in enumerate(plan(refs[:k])):
            pltpu.make_async_remote_copy(src_ref=src, dst_ref=dst, send_sem=send_sems.at[i],
                                         recv_sem=recv_sems.at[i], device_id=to, device_id_type=MESH).start()
        token[...] = jnp.zeros_like(token)

    out = pl.pallas_call(
        body, name=name,
        out_shape=(pltpu.SemaphoreType.DMA((n_copies,)), pltpu.SemaphoreType.DMA((n_copies,)),
                   *[pltpu.HBM(a.shape, a.dtype) for a in arrays], jax.ShapeDtypeStruct((8, 128), jnp.float32)),
        in_specs=[HBM] * k, out_specs=(SEM, SEM, *[HBM] * k, pl.BlockSpec(memory_space=pltpu.VMEM)),
        input_output_aliases={i: 2 + i for i in range(k)},
        compiler_params=pltpu.CompilerParams(has_side_effects=EFFECT),
    )(*[pltpu.with_memory_space_constraint(a, pltpu.HBM) for a in arrays])
    return out[0], out[1], out[2:2 + k], out[-1]


def _split_wait(name, arrays, send_sems, recv_sems, plan, after):
    k = len(arrays)

    def body(*refs):
        sends, recvs = refs[k], refs[k + 1]
        for i, (src, dst, to) in enumerate(plan(refs[:k])):
            cp = pltpu.make_async_remote_copy(src_ref=src, dst_ref=dst, send_sem=sends.at[i], recv_sem=recvs.at[i],
                                              device_id=to, device_id_type=MESH)
            cp.wait_send()
            cp.wait_recv()

    return pl.pallas_call(
        body, name=name,
        out_shape=tuple(pltpu.HBM(a.shape, a.dtype) for a in arrays),
        in_specs=[HBM] * k + [SEM, SEM, ANY], out_specs=(HBM,) * k,
        input_output_aliases={i: i for i in range(k)},
        compiler_params=pltpu.CompilerParams(has_side_effects=EFFECT),
    )(*arrays, send_sems, recv_sems, after)


def _plan_rest(refs):
    gob, gpf, r_out, r_pool = refs
    x, y, c, me = _place()
    plan = []
    for r in range(1, NDEV):
        plan.append((gob.at[me ^ r], r_out.at[r - 1], _peer(x, y, c, r)))
        plan.append((gpf.at[me ^ r], r_pool.at[r - 1], _peer(x, y, c, r)))
    return plan


def _plan_in(refs):
    sum_b, r_in = refs
    x, y, c, _ = _place()
    plan = []
    for j, (dx, dy) in enumerate(((1, 0), (0, 1), (1, 1))):
        px, py = x ^ dx, y ^ dy
        plan.append((sum_b.at[2 * px + py], r_in.at[j], (px, py, c)))
    return plan


def _pair_exchange(gw_b5):
    def body(g_ref, r_ref, send_sem, recv_sem):
        x, y, c, _ = _place()
        cp = pltpu.make_async_remote_copy(src_ref=g_ref.at[:, pl.ds(1 - c, 1)], dst_ref=r_ref, send_sem=send_sem,
                                          recv_sem=recv_sem, device_id=(x, y, 1 - c), device_id_type=MESH)
        cp.start()
        cp.wait()

    return pl.pallas_call(
        body, name="pair_exchange",
        out_shape=jax.ShapeDtypeStruct((4, 1, 3, D, TILE), jnp.bfloat16),
        in_specs=[ANY], out_specs=ANY,
        scratch_shapes=[pltpu.SemaphoreType.DMA, pltpu.SemaphoreType.DMA],
    )(gw_b5)


def _pair_sum(place, gw_f5, recv):
    def body(place_ref, g_ref, r_ref, f_ref, b_ref):
        del place_ref
        s = g_ref[...] + r_ref[...].astype(jnp.float32)
        f_ref[...] = s
        b_ref[...] = _bf(s)

    tile = pl.BlockSpec((None, None, D, TILE), lambda q, p, pr: (q, p, 0, 0))
    return pl.pallas_call(
        body, name="pair_sum",
        grid_spec=pltpu.PrefetchScalarGridSpec(
            num_scalar_prefetch=1, grid=(4, 3),
            in_specs=[pl.BlockSpec((None, None, None, D, TILE), lambda q, p, pr: (q, pr[2], p, 0, 0)),
                      pl.BlockSpec((None, None, None, D, TILE), lambda q, p, pr: (q, 0, p, 0, 0))],
            out_specs=(tile, tile)),
        out_shape=(jax.ShapeDtypeStruct((4, 3, D, TILE), jnp.float32),
                   jax.ShapeDtypeStruct((4, 3, D, TILE), jnp.bfloat16)),
        compiler_params=_params(("parallel", "parallel")),
    )(place, gw_f5, recv)


def _gather_small(small):
    def body(sm, r_small, send_sems, recv_sems, loc_sem):
        x, y, c, me = _place()
        loc = pltpu.make_async_copy(sm, r_small.at[me], loc_sem)
        loc.start()

        def copy(r, src_idx):
            return pltpu.make_async_remote_copy(
                src_ref=sm, dst_ref=r_small.at[src_idx], send_sem=send_sems.at[r - 1], recv_sem=recv_sems.at[r - 1],
                device_id=_peer(x, y, c, r), device_id_type=MESH)

        sends = [copy(r, me) for r in range(1, NDEV)]
        for cp in sends:
            cp.start()
        for r in range(1, NDEV):
            copy(r, me ^ r).wait_recv()
        for cp in sends:
            cp.wait_send()
        loc.wait()

    return pl.pallas_call(
        body, name="gather_small",
        out_shape=jax.ShapeDtypeStruct((NDEV, 8, D), jnp.float32),
        in_specs=[ANY], out_specs=ANY,
        scratch_shapes=[pltpu.SemaphoreType.DMA((NDEV - 1,)), pltpu.SemaphoreType.DMA((NDEV - 1,)),
                        pltpu.SemaphoreType.DMA],
    )(small)


def _norm1(x, g1):
    rows = 512

    def body(x_ref, g_ref, h_ref):
        xv = x_ref[...]
        r = lax.rsqrt(jnp.mean(xv * xv, axis=-1, keepdims=True) + EPS)
        h_ref[...] = _bf(xv * r * g_ref[...])

    return pl.pallas_call(
        body, name="norm1", grid=(T // rows,),
        in_specs=[pl.BlockSpec((rows, D), lambda i: (i, 0)), pl.BlockSpec((1, D), lambda i: (0, 0))],
        out_specs=pl.BlockSpec((rows, D), lambda i: (i, 0)),
        out_shape=jax.ShapeDtypeStruct((T, D), jnp.bfloat16),
        compiler_params=_params(("parallel",)),
    )(x, g1)


def _seg_tiles(s):
    return (s + 2) % NSEG


def _proj(h, w_t):
    tm = 1024

    def body(h_ref, w_ref, o_ref):
        hv = h_ref[...]
        for i in range(4):
            o_ref[:, i * TILE:(i + 1) * TILE] = _dot(hv, w_ref[i], 1, 0)

    return pl.pallas_call(
        body, name="proj", grid=(T // tm, NSEG),
        in_specs=[pl.BlockSpec((tm, D), lambda m, s: (m, 0)),
                  pl.BlockSpec((4, D, TILE), lambda m, s: (_seg_tiles(s), 0, 0))],
        out_specs=pl.BlockSpec((None, tm, D), lambda m, s: (s, m, 0)),
        out_shape=jax.ShapeDtypeStruct((NSEG, T, D), jnp.float32),
        compiler_params=_params(("parallel", "parallel")),
    )(h, w_t)


def _row_ids(shape):
    return lax.broadcasted_iota(jnp.int32, shape, 0)


def _shift_down(a, k, rows):
    return jnp.where(rows >= k, pltpu.roll(a, k, 0), 0.0)


def _shift_up(a, k, rows):
    return jnp.where(rows < T - k, pltpu.roll(a, T - k, 0), 0.0)


def _window_sum(u, gidx, rows, shift):
    s2 = u + shift(u, 1, rows)
    s4 = s2 + shift(s2, 2, rows)
    s8 = s4 + shift(s4, 4, rows)
    s16 = s8 + shift(s8, 8, rows)
    return jnp.where(gidx == 0, s2, jnp.where(gidx == 1, s4, jnp.where(gidx == 2, s8, s16)))


def _pool_count(gidx, rows):
    width = lax.shift_left(jnp.int32(2), gidx)
    return jnp.minimum(rows + 1, width).astype(jnp.float32)


def _pool_fwd(proj, pool_w, pool_scale):
    def body(p_ref, w_ref, sc_ref, y_ref):
        gidx = pl.program_id(0)
        u, pg = p_ref[0], p_ref[1]
        rows = _row_ids(u.shape)
        d = _window_sum(u, gidx, rows, _shift_down) / _pool_count(gidx, rows) - u
        mixed = _dot(_bf(d), w_ref[...], 1, 0)
        y_ref[...] = _bf(mixed * sc_ref[...] * (pg * _sigmoid(pg)))

    return pl.pallas_call(
        body, name="pool_fwd", grid=(NGROUP,),
        in_specs=[pl.BlockSpec((2, T, GROUP), lambda g: (2, 0, g)),
                  pl.BlockSpec((None, GROUP, GROUP), lambda g: (g, 0, 0)),
                  pl.BlockSpec((1, GROUP), lambda g: (0, g))],
        out_specs=pl.BlockSpec((T, GROUP), lambda g: (0, g)),
        out_shape=jax.ShapeDtypeStruct((T, DMIX), jnp.bfloat16),
        compiler_params=_params(("parallel",)),
    )(proj, pool_w, pool_scale)


def _tri(lower):
    r = lax.broadcasted_iota(jnp.int32, (CHUNK, CHUNK), 0)
    c = lax.broadcasted_iota(jnp.int32, (CHUNK, CHUNK), 1)
    return (r >= c) if lower else (r <= c)


def _sum_rows_matrix():
    shape = (CHUNK + 16, CHUNK)
    r, c = lax.broadcasted_iota(jnp.int32, shape, 0), lax.broadcasted_iota(jnp.int32, shape, 1)
    run = jnp.where(c <= r, 1.0, 0.0)
    half = jnp.where(c < CHUNK // 2, 1.0, 0.0)
    return _bf(jnp.where(r < CHUNK, run, jnp.where(r < CHUNK + 8, 1.0, half)))


def _rev_sum_matrix():
    shape = (CHUNK, 2 * CHUNK)
    r, c = lax.broadcasted_iota(jnp.int32, shape, 0), lax.broadcasted_iota(jnp.int32, shape, 1)
    return _bf(jnp.where(c < CHUNK, jnp.where(c >= r, 1.0, 0.0), jnp.where(c - CHUNK < r, 1.0, 0.0)))


def _split3(a):
    a1 = _bf(a)
    r1 = a - a1.astype(jnp.float32)
    a2 = _bf(r1)
    return [a1, a2, _bf(r1 - a2.astype(jnp.float32))]


def _exact_sums(mat, pieces):
    x = jnp.concatenate([s for p in pieces for s in _split3(p)], axis=1)
    r = _dot(mat, x, 1, 0)
    return [r[:, 3 * j * HEAD:(3 * j + 1) * HEAD] + r[:, (3 * j + 1) * HEAD:(3 * j + 2) * HEAD]
            + r[:, (3 * j + 2) * HEAD:(3 * j + 3) * HEAD] for j in range(len(pieces))]


def _gates(qv, fl, lb):
    sq = _sigmoid(qv)
    sg = _sigmoid(fl)
    f = lb + (1.0 - lb) * sg
    return dict(sq=sq, qs=qv * sq, sg=sg, f=f, kk=1.0 - f, g=jnp.log(f))


def _decays(sums):
    big_g = sums[:CHUNK]
    total = sums[CHUNK:CHUNK + 8]
    g_last = jnp.tile(total, (CHUNK // 8, 1))
    g_mid = jnp.tile(sums[CHUNK + 8:], (CHUNK // 8, 1))
    return dict(
        e_q=jnp.exp(big_g),
        e_k=jnp.exp(g_last - big_g),
        e_qm=jnp.exp(jnp.minimum(big_g - g_mid, EXP_CAP)),
        e_km=jnp.exp(jnp.minimum(g_mid - big_g, EXP_CAP)),
        total8=jnp.exp(total),
        state=jnp.exp(jnp.tile(total, (HEAD // 8, 1))))


def _group_rows(gi):
    return [pl.ds(pl.multiple_of((gi * NB + j) * CHUNK, CHUNK), CHUNK) for j in range(NB)]


def _lower_bound(lb_ref):
    return _sigmoid(lb_ref[0:1, :] - lb_ref[1:2, :])


def _hgrn_fwd(proj, lb_logits, rec_g, y_in):
    def body(p_ref, lb_ref, rg_ref, y_any, y_ref, o_ref, st_ref):
        del y_any
        lb = _lower_bound(lb_ref)
        causal = _tri(True)
        smat = _sum_rows_matrix()

        def group(gi, st):
            rows = _group_rows(gi)
            ts = [_gates(p_ref[0, r, :], p_ref[1, r, :], lb) for r in rows]
            ds = [_decays(s) for s in _exact_sums(smat, [t["g"] for t in ts])]
            vs = [_bf(p_ref[2, r, :]) for r in rows]
            q_m = [_bf(t["qs"] * d["e_qm"]) for t, d in zip(ts, ds)]
            k_m = [_bf(t["kk"] * d["e_km"]) for t, d in zip(ts, ds)]
            q_e = [_bf(t["qs"] * d["e_q"]) for t, d in zip(ts, ds)]
            k_e = [_bf(t["kk"] * d["e_k"]) for t, d in zip(ts, ds)]
            a = [_bf(jnp.where(causal, _dot(q_m[j], k_m[j], 1, 1), 0.0)) for j in range(NB)]
            intra = [_dot(a[j], vs[j], 1, 0) for j in range(NB)]
            upd = [_dot(vs[j], k_e[j], 0, 0) for j in range(NB)]
            for j in range(NB):
                st_ref[gi * NB + j] = st
                o_ref[rows[j], :] = intra[j] + _dot(q_e[j], _bf(st), 1, 1)
                st = st * ds[j]["state"] + upd[j]
            return st

        lax.fori_loop(0, NGRP, group, jnp.zeros((HEAD, HEAD), jnp.float32))
        o = o_ref[...]
        rn = o * lax.rsqrt(jnp.mean(o * o, axis=-1, keepdims=True) + EPS)
        gate = p_ref[3]
        y_ref[...] = _bf(rn * rg_ref[...] * (gate * _sigmoid(gate)))

    return pl.pallas_call(
        body, name="hgrn_fwd", grid=(NHEAD,),
        in_specs=[pl.BlockSpec((4, T, HEAD), lambda h: (0, 0, h)),
                  pl.BlockSpec((2, HEAD), lambda h: (0, h)),
                  pl.BlockSpec((1, HEAD), lambda h: (0, h)),
                  pl.BlockSpec(memory_space=pl.ANY)],
        out_specs=(pl.BlockSpec((T, HEAD), lambda h: (0, NHEAD + h)),
                   pl.BlockSpec((T, HEAD), lambda h: (0, h)),
                   pl.BlockSpec((None, NCHUNK, HEAD, HEAD), lambda h: (h, 0, 0, 0))),
        out_shape=(jax.ShapeDtypeStruct((T, DMIX), jnp.bfloat16),
                   jax.ShapeDtypeStruct((T, D), jnp.float32),
                   jax.ShapeDtypeStruct((NHEAD, NCHUNK, HEAD, HEAD), jnp.float32)),
        input_output_aliases={3: 0},
        compiler_params=_params(("parallel",)),
    )(proj, lb_logits, rec_g, y_in)


def _out_proj(y, w_out):
    tm = 512

    def body(y_ref, w_ref, o_ref):
        o_ref[...] = _dot(y_ref[...], w_ref[...], 1, 0)

    return pl.pallas_call(
        body, name="out_proj", grid=(T // tm,),
        in_specs=[pl.BlockSpec((tm, DMIX), lambda m: (m, 0)), pl.BlockSpec((DMIX, D), lambda m: (0, 0))],
        out_specs=pl.BlockSpec((tm, D), lambda m: (m, 0)),
        out_shape=jax.ShapeDtypeStruct((T, D), jnp.float32),
        compiler_params=_params(("parallel",)),
    )(y, w_out)


def _loss_head(x, ymm, target, gf):
    rows = 256

    def body(x_ref, y_ref, t_ref, g_ref, dz_ref, dzb_ref, sq_ref, dg_ref):
        z = x_ref[...] + y_ref[...]
        r = lax.rsqrt(jnp.mean(z * z, axis=-1, keepdims=True) + EPS)
        zhat = z * r
        err = zhat * g_ref[...] - t_ref[...]
        dy = err * (1.0 / D)
        gdy = dy * g_ref[...]
        dz = r * (gdy - zhat * jnp.mean(zhat * gdy, axis=-1, keepdims=True))
        dz_ref[...] = dz
        dzb_ref[...] = _bf(dz)
        sq = jnp.sum(err * err, axis=0, keepdims=True)
        dg = jnp.sum(zhat * dy, axis=0, keepdims=True)

        @pl.when(pl.program_id(0) == 0)
        def _():
            sq_ref[...] = sq
            dg_ref[...] = dg

        @pl.when(pl.program_id(0) != 0)
        def _():
            sq_ref[...] += sq
            dg_ref[...] += dg

    tile = pl.BlockSpec((rows, D), lambda i: (i, 0))
    vec = pl.BlockSpec((1, D), lambda i: (0, 0))
    return pl.pallas_call(
        body, name="loss_head", grid=(T // rows,),
        in_specs=[tile, tile, tile, vec], out_specs=(tile, tile, vec, vec),
        out_shape=(jax.ShapeDtypeStruct((T, D), jnp.float32), jax.ShapeDtypeStruct((T, D), jnp.bfloat16),
                   jax.ShapeDtypeStruct((1, D), jnp.float32), jax.ShapeDtypeStruct((1, D), jnp.float32)),
        compiler_params=_params(("arbitrary",)),
    )(x, ymm, target, gf)


def _out_proj_bwd(dzb, w_out, y):
    tn = 512

    def body(dz_ref, w_ref, y_ref, dy_ref, gw_ref, gwb_ref):
        dz = dz_ref[...]
        dy_ref[...] = _dot(dz, w_ref[...], 1, 1)
        gw = _dot(y_ref[...], dz, 0, 0)
        gw_ref[...] = gw
        gwb_ref[...] = _bf(gw)

    return pl.pallas_call(
        body, name="out_proj_bwd", grid=(DMIX // tn,),
        in_specs=[pl.BlockSpec((T, D), lambda n: (0, 0)), pl.BlockSpec((tn, D), lambda n: (n, 0)),
                  pl.BlockSpec((T, tn), lambda n: (0, n))],
        out_specs=(pl.BlockSpec((T, tn), lambda n: (0, n)), pl.BlockSpec((tn, D), lambda n: (n, 0)),
                   pl.BlockSpec((tn, D), lambda n: (n, 0))),
        out_shape=(jax.ShapeDtypeStruct((T, DMIX), jnp.float32), jax.ShapeDtypeStruct((DMIX, D), jnp.float32),
                   jax.ShapeDtypeStruct((DMIX, D), jnp.bfloat16)),
        compiler_params=_params(("parallel",)),
    )(dzb, w_out, y)


def _hgrn_bwd(proj, lb_logits, rec_g, o, states, dymix, dproj_in, token):
    def body(p_ref, lb_ref, rg_ref, o_ref, st_ref, dy_ref, dp_any, token_any, dp_ref, drg_ref, dlb_ref, do_ref):
        del dp_any, token_any
        lb = _lower_bound(lb_ref)
        causal = _tri(True)
        smat, rmat = _sum_rows_matrix(), _rev_sum_matrix()

        o = o_ref[...]
        rs = lax.rsqrt(jnp.mean(o * o, axis=-1, keepdims=True) + EPS)
        rn = o * rs
        gate = p_ref[3]
        sgate = _sigmoid(gate)
        dyv = dy_ref[...]
        d_r = dyv * (gate * sgate)
        dp_ref[3] = _bf(dyv * (rn * rg_ref[...]) * (sgate * (1.0 + gate * (1.0 - sgate))))
        drg_ref[...] = jnp.sum(d_r * rn, axis=0, keepdims=True)
        drn = d_r * rg_ref[...]
        do_ref[...] = rs * (drn - rn * jnp.mean(rn * drn, axis=-1, keepdims=True))

        def group(i, carry):
            dst, dlb = carry
            gi = NGRP - 1 - i
            rows = _group_rows(gi)
            span = range(NB)
            qvs = [p_ref[0, r, :] for r in rows]
            ts = [_gates(qv, p_ref[1, r, :], lb) for qv, r in zip(qvs, rows)]
            ds = [_decays(s) for s in _exact_sums(smat, [t["g"] for t in ts])]
            vs = [_bf(p_ref[2, r, :]) for r in rows]
            dos = [_bf(do_ref[r, :]) for r in rows]
            sts = [st_ref[gi * NB + j] for j in span]
            qe_f = [t["qs"] * d["e_q"] for t, d in zip(ts, ds)]
            ke_f = [t["kk"] * d["e_k"] for t, d in zip(ts, ds)]
            q_e, k_e = [_bf(a) for a in qe_f], [_bf(a) for a in ke_f]
            q_m = [_bf(t["qs"] * d["e_qm"]) for t, d in zip(ts, ds)]
            k_m = [_bf(t["kk"] * d["e_km"]) for t, d in zip(ts, ds)]
            a = [_bf(jnp.where(causal, _dot(q_m[j], k_m[j], 1, 1), 0.0)) for j in span]
            da = [_bf(jnp.where(causal, _dot(dos[j], vs[j], 1, 1), 0.0)) for j in span]
            dqm = [_dot(da[j], k_m[j], 1, 0) for j in span]
            dkm = [_dot(da[j], q_m[j], 0, 0) for j in span]
            dv_in = [_dot(a[j], dos[j], 0, 0) for j in span]
            dqe = [_dot(dos[j], _bf(sts[j]), 1, 0) for j in span]
            grow = [_dot(dos[j], q_e[j], 0, 0) for j in span]
            dke, carried = [None] * NB, [None] * NB
            for j in reversed(span):
                dst_b = _bf(dst)
                dke[j] = _dot(vs[j], dst_b, 1, 0)
                dp_ref[2, rows[j], :] = _bf(dv_in[j] + _dot(k_e[j], dst_b, 1, 1))
                carried[j] = ds[j]["total8"] * jnp.sum(dst * sts[j], axis=0, keepdims=True)
                dst = dst * ds[j]["state"] + grow[j]
            kdk = [ke_f[j] * dke[j] for j in span]
            pos = [(q_m[j].astype(jnp.float32) * dqm[j] - k_m[j].astype(jnp.float32) * dkm[j]) + qe_f[j] * dqe[j]
                   for j in span]
            dgs = _exact_sums(rmat, [jnp.concatenate([pos[j], kdk[j]], axis=0) for j in span])
            for j in span:
                t, d = ts[j], ds[j]
                dg = dgs[j] + jnp.tile(carried[j], (CHUNK // 8, 1))
                dqs = dqm[j] * d["e_qm"] + dqe[j] * d["e_q"]
                dkk = dkm[j] * d["e_km"] + dke[j] * d["e_k"]
                df = dg / t["f"] - dkk
                dp_ref[1, rows[j], :] = _bf(df * (1.0 - lb) * (t["sg"] * (1.0 - t["sg"])))
                dp_ref[0, rows[j], :] = _bf(dqs * (t["sq"] * (1.0 + qvs[j] * (1.0 - t["sq"]))))
                dlb = dlb + df * (1.0 - t["sg"])
            return dst, dlb

        _, dlb = lax.fori_loop(0, NGRP, group, (jnp.zeros((HEAD, HEAD), jnp.float32),
                                                jnp.zeros((CHUNK, HEAD), jnp.float32)))
        dlb_ref[...] = jnp.sum(dlb, axis=0, keepdims=True)

    vec = pl.BlockSpec((1, HEAD), lambda h: (0, h))
    return pl.pallas_call(
        body, name="hgrn_bwd", grid=(NHEAD,),
        in_specs=[pl.BlockSpec((4, T, HEAD), lambda h: (0, 0, h)),
                  pl.BlockSpec((2, HEAD), lambda h: (0, h)), vec,
                  pl.BlockSpec((T, HEAD), lambda h: (0, h)),
                  pl.BlockSpec((None, NCHUNK, HEAD, HEAD), lambda h: (h, 0, 0, 0)),
                  pl.BlockSpec((T, HEAD), lambda h: (0, NHEAD + h)), ANY, ANY],
        out_specs=(pl.BlockSpec((4, T, HEAD), lambda h: (0, 0, h)), vec, vec),
        out_shape=(jax.ShapeDtypeStruct((NSEG, T, D), jnp.bfloat16),
                   jax.ShapeDtypeStruct((1, D), jnp.float32), jax.ShapeDtypeStruct((1, D), jnp.float32)),
        scratch_shapes=[pltpu.VMEM((T, HEAD), jnp.float32)],
        input_output_aliases={6: 0},
        compiler_params=_params(("parallel",)),
    )(proj, lb_logits, rec_g, o, states, dymix, dproj_in, token)


def _pool_bwd(proj, pool_w, pool_scale, dymix):
    def body(p_ref, w_ref, sc_ref, dy_ref, dp_ref, gw_ref, gs_ref):
        gidx = pl.program_id(0)
        u, pg = p_ref[0], p_ref[1]
        rows = _row_ids(u.shape)
        count = _pool_count(gidx, rows)
        d = _bf(_window_sum(u, gidx, rows, _shift_down) / count - u)
        mixed = _dot(d, w_ref[...], 1, 0)
        spg = _sigmoid(pg)
        dyv = dy_ref[...]
        d_p = dyv * (pg * spg)
        dp_ref[1] = _bf(dyv * (mixed * sc_ref[...]) * (spg * (1.0 + pg * (1.0 - spg))))
        gs_ref[...] = jnp.sum(d_p * mixed, axis=0, keepdims=True)
        dmixed = _bf(d_p * sc_ref[...])
        gw_ref[...] = _dot(d, dmixed, 0, 0)
        dd = _dot(dmixed, w_ref[...], 1, 1)
        dp_ref[0] = _bf(_window_sum(dd / count, gidx, rows, _shift_up) - dd)

    return pl.pallas_call(
        body, name="pool_bwd", grid=(NGROUP,),
        in_specs=[pl.BlockSpec((2, T, GROUP), lambda g: (2, 0, g)),
                  pl.BlockSpec((None, GROUP, GROUP), lambda g: (g, 0, 0)),
                  pl.BlockSpec((1, GROUP), lambda g: (0, g)),
                  pl.BlockSpec((T, GROUP), lambda g: (0, g))],
        out_specs=(pl.BlockSpec((2, T, GROUP), lambda g: (2, 0, g)),
                   pl.BlockSpec((None, GROUP, GROUP), lambda g: (g, 0, 0)),
                   pl.BlockSpec((1, GROUP), lambda g: (0, g))),
        out_shape=(jax.ShapeDtypeStruct((NSEG, T, D), jnp.bfloat16),
                   jax.ShapeDtypeStruct((NGROUP, GROUP, GROUP), jnp.float32),
                   jax.ShapeDtypeStruct((1, D), jnp.float32)),
        compiler_params=_params(("parallel",)),
    )(proj, pool_w, pool_scale, dymix)


def _proj_bwd_w(h, dproj):
    def body(h_ref, dp_ref, gw_ref, gwb_ref):
        gw = _dot(h_ref[...], dp_ref[...], 0, 0)
        for i in range(4):
            gw_ref[i] = gw[:, i * TILE:(i + 1) * TILE]
            gwb_ref[i] = _bf(gw[:, i * TILE:(i + 1) * TILE])

    out = pl.BlockSpec((4, D, TILE), lambda s: (_seg_tiles(s), 0, 0))
    return pl.pallas_call(
        body, name="proj_bwd_w", grid=(NSEG,),
        in_specs=[pl.BlockSpec((T, D), lambda s: (0, 0)), pl.BlockSpec((None, T, D), lambda s: (s, 0, 0))],
        out_specs=(out, out),
        out_shape=(jax.ShapeDtypeStruct((NTILE, D, TILE), jnp.float32),
                   jax.ShapeDtypeStruct((NTILE, D, TILE), jnp.bfloat16)),
        compiler_params=_params(("parallel",)),
    )(h, dproj)


def _proj_bwd_x(dproj, w_t, token):
    tm = 1024

    def body(dp_ref, w_ref, token_any, o_ref, wcat):
        del token_any
        for i in range(4):
            wcat[:, i * TILE:(i + 1) * TILE] = w_ref[i]
        r = _dot(dp_ref[...], wcat[...], 1, 1)

        @pl.when(pl.program_id(1) == 0)
        def _():
            o_ref[...] = r

        @pl.when(pl.program_id(1) != 0)
        def _():
            o_ref[...] += r

    return pl.pallas_call(
        body, name="proj_bwd_x", grid=(T // tm, NSEG),
        in_specs=[pl.BlockSpec((None, tm, D), lambda m, s: (s, m, 0)),
                  pl.BlockSpec((4, D, TILE), lambda m, s: (_seg_tiles(s), 0, 0)), ANY],
        out_specs=pl.BlockSpec((tm, D), lambda m, s: (m, 0)),
        out_shape=jax.ShapeDtypeStruct((T, D), jnp.float32),
        scratch_shapes=[pltpu.VMEM((D, D), jnp.bfloat16)],
        compiler_params=_params(("parallel", "arbitrary")),
    )(dproj, w_t, token)


def _norm1_bwd(x, g1, dh, dz):
    rows = 256

    def body(x_ref, g_ref, dh_ref, dz_ref, dx_ref, dg_ref):
        xv = x_ref[...]
        r = lax.rsqrt(jnp.mean(xv * xv, axis=-1, keepdims=True) + EPS)
        xhat = xv * r
        dhv = dh_ref[...]
        gdh = dhv * g_ref[...]
        dx_ref[...] = dz_ref[...] + r * (gdh - xhat * jnp.mean(xhat * gdh, axis=-1, keepdims=True))
        dg = jnp.sum(xhat * dhv, axis=0, keepdims=True)

        @pl.when(pl.program_id(0) == 0)
        def _():
            dg_ref[...] = dg

        @pl.when(pl.program_id(0) != 0)
        def _():
            dg_ref[...] += dg

    tile = pl.BlockSpec((rows, D), lambda i: (i, 0))
    vec = pl.BlockSpec((1, D), lambda i: (0, 0))
    return pl.pallas_call(
        body, name="norm1_bwd", grid=(T // rows,),
        in_specs=[tile, vec, tile, tile], out_specs=(tile, vec),
        out_shape=(jax.ShapeDtypeStruct((T, D), jnp.float32), jax.ShapeDtypeStruct((1, D), jnp.float32)),
        compiler_params=_params(("arbitrary",)),
    )(x, g1, dh, dz)


def _adamw(w, g, m, v):
    m_new = ADAM_B1 * m + (1.0 - ADAM_B1) * g
    v_new = ADAM_B2 * v + (1.0 - ADAM_B2) * (g * g)
    delta = -ADAM_LR * ((m_new / BC1) / (jnp.sqrt(v_new / BC2) + ADAM_EPS) + ADAM_WD * w)
    return delta, m_new, v_new


def _reduce_adam(name, place, own, recv, w, m, v, grid, own_spec, recv_spec, w_spec):
    def body(place_ref, own_ref, recv_ref, w_ref, m_ref, v_ref, g_ref, d_ref, mo_ref, vo_ref):
        del place_ref
        g = own_ref[...]
        for r in range(recv_ref.shape[0]):
            g = g + recv_ref[r].astype(jnp.float32)
        delta, m_new, v_new = _adamw(w_ref[...], g, m_ref[...], v_ref[...])
        g_ref[...] = g
        d_ref[...] = delta
        mo_ref[...] = m_new
        vo_ref[...] = v_new

    shape = jax.ShapeDtypeStruct(w.shape, jnp.float32)
    return pl.pallas_call(
        body, name=name,
        grid_spec=pltpu.PrefetchScalarGridSpec(
            num_scalar_prefetch=1, grid=grid,
            in_specs=[own_spec, recv_spec, w_spec, w_spec, w_spec], out_specs=(w_spec,) * 4),
        out_shape=(shape,) * 4,
        compiler_params=_params(("parallel",)),
    )(place, own, recv, w, m, v)


def _small_adam(parts, w, m, v):
    def body(p_ref, w_ref, m_ref, v_ref, g_ref, d_ref, mo_ref, vo_ref):
        g = p_ref[0]
        for s in range(1, NDEV):
            g = g + p_ref[s]
        wv = w_ref[...]
        rows = _row_ids(wv.shape)
        other = jnp.where(rows == 2, pltpu.roll(wv, 7, 0), jnp.where(rows == 3, pltpu.roll(wv, 1, 0), 0.0))
        lbv = _sigmoid(wv - other)
        sign = jnp.where(rows == 2, 1.0, -1.0)
        g = jnp.where((rows == 2) | (rows == 3), sign * g * lbv * (1.0 - lbv), g)
        delta, m_new, v_new = _adamw(wv, g, m_ref[...], v_ref[...])
        g_ref[...] = g
        d_ref[...] = delta
        mo_ref[...] = m_new
        vo_ref[...] = v_new

    shape = jax.ShapeDtypeStruct((8, D), jnp.float32)
    return pl.pallas_call(body, name="small_adam", out_shape=(shape,) * 4)(parts, w, m, v)


def _rows8(*vecs):
    rows = [a.reshape(-1, D) for a in vecs]
    n = sum(r.shape[0] for r in rows)
    return jnp.concatenate(rows + [jnp.zeros((8 - n, D), jnp.float32)], axis=0)


def kernel(x, norm1_g, w_in, pool_w, pool_scale, lb_logits, rec_norm_g, w_out, final_norm_g, loss_target, m_norm1_g, m_w_in, m_pool_w, m_pool_scale, m_lb_logits, m_rec_norm_g, m_w_out, m_final_norm_g, v_norm1_g, v_w_in, v_pool_w, v_pool_scale, v_lb_logits, v_rec_norm_g, v_w_out, v_final_norm_g):
    xs = x[0]
    target = loss_target[0]
    ix, iy, ic = lax.axis_index("x"), lax.axis_index("y"), lax.axis_index("c")
    place = jnp.stack([4 * ix + 2 * iy + ic, 2 * ix + iy, ic]).astype(jnp.int32)
    gf = final_norm_g.reshape(1, D)

    w_t3 = _bf(w_in[0]).reshape(D, 3, TILE).transpose(1, 0, 2)
    w_t, w_out_g, pool_g = _gather_weights(w_t3, _bf(w_out[0]), _bf(pool_w[0]))
    w_out_full = w_out_g.reshape(DMIX, D)
    pool_full = pool_g.transpose(1, 0, 2, 3).reshape(NGROUP, GROUP, GROUP)

    h = _norm1(xs, norm1_g)
    proj = _proj(h, w_t)
    y = _pool_fwd(proj, pool_full, pool_scale)
    y, o, states = _hgrn_fwd(proj, lb_logits, rec_norm_g, y)
    ymm = _out_proj(y, w_out_full)
    dz, dzb, sq, dgf = _loss_head(xs, ymm, target, gf)
    loss = lax.psum(0.5 * jnp.sum(sq) / D, AXES)

    dymix, gwout_f, gwout_b = _out_proj_bwd(dzb, w_out_full, y)
    dproj, gpool, dscale = _pool_bwd(proj, pool_full, pool_scale, dymix)

    blk_out = (NDEV, DMIX // NDEV, D)
    blk_pool = (NDEV, NGROUP, GROUP // NDEV, GROUP)
    gpool_s = gpool.reshape(NGROUP, NDEV, GROUP // NDEV, GROUP).transpose(1, 0, 2, 3)
    rest = [gwout_b.reshape(blk_out), gpool_s,
            lax.empty((NDEV - 1,) + blk_out[1:], jnp.bfloat16), lax.empty((NDEV - 1,) + blk_pool[1:], jnp.float32)]
    rest_send, rest_recv, rest, rest_token = _split_start("scatter_rest_start", rest, 2 * (NDEV - 1), _plan_rest)

    dproj, drecg, dlb = _hgrn_bwd(proj, lb_logits, rec_norm_g, o, states, dymix, dproj, rest_token)
    gw_f, gw_b = _proj_bwd_w(h, dproj)

    by_owner = (4, 2, 3, D, TILE)
    sum_f, sum_b = _pair_sum(place, gw_f.reshape(by_owner), _pair_exchange(gw_b.reshape(by_owner)))
    win = [sum_b, lax.empty((3, 3, D, TILE), jnp.bfloat16)]
    win_send, win_recv, win, win_token = _split_start("scatter_win_start", win, 3, _plan_in)

    dh = _proj_bwd_x(dproj, w_t, win_token)
    grad_x, dg1 = _norm1_bwd(xs, norm1_g, dh, dz)
    r_small = _gather_small(_rows8(dg1, dscale, dlb, dlb, drecg, dgf))

    _, gpool_own, r_out, r_pool = _split_wait("scatter_rest_wait", rest, rest_send, rest_recv, _plan_rest, r_small)
    g_wout, d_wout, m_wout, v_wout = _reduce_adam(
        "adam_w_out", place, gwout_f.reshape(blk_out), r_out, w_out, m_w_out, v_w_out, (1,),
        pl.BlockSpec((None,) + blk_out[1:], lambda i, pr: (pr[0], 0, 0)),
        pl.BlockSpec((NDEV - 1,) + blk_out[1:], lambda i, pr: (0, 0, 0)),
        pl.BlockSpec((None,) + blk_out[1:], lambda i, pr: (0, 0, 0)))
    g_pool, d_pool, m_pool, v_pool = _reduce_adam(
        "adam_pool_w", place, gpool_own, r_pool, pool_w, m_pool_w, v_pool_w, (1,),
        pl.BlockSpec((None,) + blk_pool[1:], lambda i, pr: (pr[0], 0, 0, 0)),
        pl.BlockSpec((NDEV - 1,) + blk_pool[1:], lambda i, pr: (0, 0, 0, 0)),
        pl.BlockSpec((None,) + blk_pool[1:], lambda i, pr: (0, 0, 0, 0)))

    g_s, d_s, m_s, v_s = _small_adam(
        r_small,
        _rows8(norm1_g, pool_scale, lb_logits, rec_norm_g, final_norm_g),
        _rows8(m_norm1_g, m_pool_scale, m_lb_logits, m_rec_norm_g, m_final_norm_g),
        _rows8(v_norm1_g, v_pool_scale, v_lb_logits, v_rec_norm_g, v_final_norm_g))

    _, r_in = _split_wait("scatter_win_wait", win, win_send, win_recv, _plan_in, v_s)
    g_win, d_win, m_win, v_win = _reduce_adam(
        "adam_w_in", place, sum_f, r_in, w_in, m_w_in, v_w_in, (3,),
        pl.BlockSpec((None, None, D, TILE), lambda i, pr: (pr[1], i, 0, 0)),
        pl.BlockSpec((3, None, D, TILE), lambda i, pr: (0, i, 0, 0)),
        pl.BlockSpec((None, D, TILE), lambda i, pr: (0, 0, i)))

    def small_outs(a):
        return a[0:1], a[1:2], a[2:4], a[4:5], a[5]

    def outs(small_a, win, pool, wout):
        n1, ps, lbl, rg, fg = small_outs(small_a)
        return n1, win, pool, ps, lbl, rg, wout, fg

    return (loss, grad_x[None],
            *outs(g_s, g_win, g_pool, g_wout), *outs(d_s, d_win, d_pool, d_wout),
            *outs(m_s, m_win, m_pool, m_wout), *outs(v_s, v_win, v_pool, v_wout))
```

```python
import functools

import jax
import jax.numpy as jnp
from jax import lax
from jax.experimental import pallas as pl
from jax.experimental.pallas import tpu as pltpu

T = 2048
D = 1024
NSEG = 6
NTILE = 24
TILE = 256
DMIX = 2048
NDEV = 8
HEAD = 128
NHEAD = 8
CHUNK = 64
NCHUNK = T // CHUNK
NB = 8
NGRP = NCHUNK // NB
NGROUP = 4
GROUP = 256
EPS = 1e-6
EXP_CAP = 80.0
MESH = pl.DeviceIdType.MESH
AXES = ("x", "y", "c")
ANY = pl.BlockSpec(memory_space=pl.ANY)
HBM = pl.BlockSpec(memory_space=pltpu.HBM)
SEM = pl.BlockSpec(memory_space=pltpu.SEMAPHORE)
EFFECT = pltpu.SideEffectType.DATAFLOW_SIDE_EFFECTING

ADAM_LR = 0.001
ADAM_B1 = 0.9
ADAM_B2 = 0.999
ADAM_EPS = 1e-08
ADAM_WD = 0.01
ADAM_STEP = 10
BC1 = 1.0 - ADAM_B1 ** ADAM_STEP
BC2 = 1.0 - ADAM_B2 ** ADAM_STEP

MIB = 1 << 20


def _params(sem=None, vmem_mib=48):
    return pltpu.CompilerParams(dimension_semantics=sem, vmem_limit_bytes=vmem_mib * MIB)


def _sigmoid(v):
    return 1.0 / (1.0 + jnp.exp(-v))


def _dot(a, b, ca, cb, precision=None):
    return lax.dot_general(a, b, (((ca,), (cb,)), ((), ())), precision=precision,
                           preferred_element_type=jnp.float32)


def _bf(v):
    return v.astype(jnp.bfloat16)


def _place():
    x, y, c = lax.axis_index("x"), lax.axis_index("y"), lax.axis_index("c")
    return x, y, c, 4 * x + 2 * y + c


def _peer(x, y, c, r):
    return (x ^ ((r >> 2) & 1), y ^ ((r >> 1) & 1), c ^ (r & 1))


def _gather_weights(w_t3, w_out_b, pool_b):
    order = (0, 2)

    def body(win_ref, wout_ref, pool_ref, win_o, wout_o, pool_o, send_sems, recv_sems, loc_sems):
        x, y, c, _ = _place()
        srcs = (win_ref, wout_ref, pool_ref)
        me, sibling = (x, y, c), (x, y, 1 - c)
        chips = [(1 - x, y), (x, 1 - y), (1 - x, 1 - y)]

        def slot(w, px, py, pc):
            idx = 4 * px + 2 * py + pc
            if w == 0:
                return win_o.at[pl.ds(3 * idx, 3)]
            return (wout_o if w == 1 else pool_o).at[idx]

        def copy(k, w, block, to, src=None):
            return pltpu.make_async_remote_copy(
                src_ref=slot(w, *block) if src is None else src, dst_ref=slot(w, *block),
                send_sem=send_sems.at[3 * k + w], recv_sem=recv_sems.at[3 * k + w],
                device_id=to, device_id_type=MESH)

        locs = [pltpu.make_async_copy(srcs[w], slot(w, *me), loc_sems.at[w]) for w in range(3)]
        for cp in locs:
            cp.start()
        first = []
        for w in order:
            first += [copy(1 + j, w, me, (*chip, c), src=srcs[w]) for j, chip in enumerate(chips)]
            first.append(copy(0, w, me, sibling, src=srcs[w]))
        for cp in first:
            cp.start()
        passed = []
        for w in order:
            for j, chip in enumerate(chips):
                copy(1 + j, w, (*chip, c), me).wait_recv()
                passed.append(copy(4 + j, w, (*chip, c), sibling))
                passed[-1].start()
        for w in order:
            copy(0, w, sibling, me).wait_recv()
            for j, chip in enumerate(chips):
                copy(4 + j, w, (*chip, 1 - c), me).wait_recv()
        for cp in first + passed:
            cp.wait_send()
        for cp in locs:
            cp.wait()

    return pl.pallas_call(
        body, name="gather_weights",
        out_shape=(jax.ShapeDtypeStruct((NTILE, D, TILE), jnp.bfloat16),
                   jax.ShapeDtypeStruct((NDEV, DMIX // NDEV, D), jnp.bfloat16),
                   jax.ShapeDtypeStruct((NDEV, NGROUP, GROUP // NDEV, GROUP), jnp.bfloat16)),
        in_specs=[ANY] * 3, out_specs=(ANY,) * 3,
        scratch_shapes=[pltpu.SemaphoreType.DMA((21,)), pltpu.SemaphoreType.DMA((21,)),
                        pltpu.SemaphoreType.DMA((3,))],
    )(w_t3, w_out_b, pool_b)


def _split_start(name, arrays, n_copies, plan):
    k = len(arrays)

    def body(*refs):
        send_sems, recv_sems, token = refs[k], refs[k + 1], refs[-1]
        for i, (src, dst, to) in enumerate(plan(refs[:k])):
            pltpu.make_async_remote_copy(src_ref=src, dst_ref=dst, send_sem=send_sems.at[i],
                                         recv_sem=recv_sems.at[i], device_id=to, device_id_type=MESH).start()
        token[...] = jnp.zeros_like(token)

    out = pl.pallas_call(
        body, name=name,
        out_shape=(pltpu.SemaphoreType.DMA((n_copies,)), pltpu.SemaphoreType.DMA((n_copies,)),
                   *[pltpu.HBM(a.shape, a.dtype) for a in arrays], jax.ShapeDtypeStruct((8, 128), jnp.float32)),
        in_specs=[HBM] * k, out_specs=(SEM, SEM, *[HBM] * k, pl.BlockSpec(memory_space=pltpu.VMEM)),
        input_output_aliases={i: 2 + i for i in range(k)},
        compiler_params=pltpu.CompilerParams(has_side_effects=EFFECT),
    )(*[pltpu.with_memory_space_constraint(a, pltpu.HBM) for a in arrays])
    return out[0], out[1], out[2:2 + k], out[-1]


def _split_wait(name, arrays, send_sems, recv_sems, plan, after):
    k = len(arrays)

    def body(*refs):
        sends, recvs = refs[k], refs[k + 1]
        for i, (src, dst, to) in enumerate(plan(refs[:k])):
            cp = pltpu.make_async_remote_copy(src_ref=src, dst_ref=dst, send_sem=sends.at[i], recv_sem=recvs.at[i],
                                              device_id=to, device_id_type=MESH)
            cp.wait_send()
            cp.wait_recv()

    return pl.pallas_call(
        body, name=name,
        out_shape=tuple(pltpu.HBM(a.shape, a.dtype) for a in arrays),
        in_specs=[HBM] * k + [SEM, SEM, ANY], out_specs=(HBM,) * k,
        input_output_aliases={i: i for i in range(k)},
        compiler_params=pltpu.CompilerParams(has_side_effects=EFFECT),
    )(*arrays, send_sems, recv_sems, after)


def _plan_wout(refs):
    src, land = refs
    x, y, c, me = _place()
    return [(src, land.at[me], _peer(x, y, c, r)) for r in range(1, NDEV)]


def _plan_rest(refs):
    gob, gpf, r_out, r_pool = refs
    x, y, c, me = _place()
    plan = []
    for r in range(1, NDEV):
        plan.append((gob.at[me ^ r], r_out.at[r - 1], _peer(x, y, c, r)))
        plan.append((gpf.at[me ^ r], r_pool.at[r - 1], _peer(x, y, c, r)))
    return plan


def _plan_in(refs):
    sum_b, r_in = refs
    x, y, c, _ = _place()
    plan = []
    for j, (dx, dy) in enumerate(((1, 0), (0, 1), (1, 1))):
        px, py = x ^ dx, y ^ dy
        plan.append((sum_b.at[2 * px + py], r_in.at[j], (px, py, c)))
    return plan


def _pair_exchange(gw_b5):
    def body(g_ref, r_ref, send_sem, recv_sem):
        x, y, c, _ = _place()
        cp = pltpu.make_async_remote_copy(src_ref=g_ref.at[:, pl.ds(1 - c, 1)], dst_ref=r_ref, send_sem=send_sem,
                                          recv_sem=recv_sem, device_id=(x, y, 1 - c), device_id_type=MESH)
        cp.start()
        cp.wait()

    return pl.pallas_call(
        body, name="pair_exchange",
        out_shape=jax.ShapeDtypeStruct((4, 1, 3, D, TILE), jnp.bfloat16),
        in_specs=[ANY], out_specs=ANY,
        scratch_shapes=[pltpu.SemaphoreType.DMA, pltpu.SemaphoreType.DMA],
    )(gw_b5)


def _pair_sum(place, gw_f5, recv):
    def body(place_ref, g_ref, r_ref, b_ref):
        del place_ref
        b_ref[...] = _bf(g_ref[...] + r_ref[...].astype(jnp.float32))

    return pl.pallas_call(
        body, name="pair_sum",
        grid_spec=pltpu.PrefetchScalarGridSpec(
            num_scalar_prefetch=1, grid=(4, 3),
            in_specs=[pl.BlockSpec((None, None, None, D, TILE), lambda q, p, pr: (q, pr[2], p, 0, 0)),
                      pl.BlockSpec((None, None, None, D, TILE), lambda q, p, pr: (q, 0, p, 0, 0))],
            out_specs=pl.BlockSpec((None, None, D, TILE), lambda q, p, pr: (q, p, 0, 0))),
        out_shape=jax.ShapeDtypeStruct((4, 3, D, TILE), jnp.bfloat16),
        compiler_params=_params(("parallel", "parallel")),
    )(place, gw_f5, recv)


def _gather_small(small, *after):
    def body(sm, *refs):
        r_small, send_sems, recv_sems, loc_sem = refs[len(after):]
        x, y, c, me = _place()
        loc = pltpu.make_async_copy(sm, r_small.at[me], loc_sem)
        loc.start()

        def copy(r, src_idx):
            return pltpu.make_async_remote_copy(
                src_ref=sm, dst_ref=r_small.at[src_idx], send_sem=send_sems.at[r - 1], recv_sem=recv_sems.at[r - 1],
                device_id=_peer(x, y, c, r), device_id_type=MESH)

        sends = [copy(r, me) for r in range(1, NDEV)]
        for cp in sends:
            cp.start()
        for r in range(1, NDEV):
            copy(r, me ^ r).wait_recv()
        for cp in sends:
            cp.wait_send()
        loc.wait()

    return pl.pallas_call(
        body, name="gather_small",
        out_shape=jax.ShapeDtypeStruct((NDEV, 8, D), jnp.float32),
        in_specs=[ANY] * (1 + len(after)), out_specs=ANY,
        scratch_shapes=[pltpu.SemaphoreType.DMA((NDEV - 1,)), pltpu.SemaphoreType.DMA((NDEV - 1,)),
                        pltpu.SemaphoreType.DMA],
    )(small, *after)


def _norm1(x, g1):
    rows = 512

    def body(x_ref, g_ref, h_ref):
        xv = x_ref[...]
        r = lax.rsqrt(jnp.mean(xv * xv, axis=-1, keepdims=True) + EPS)
        h_ref[...] = _bf(xv * r * g_ref[...])

    return pl.pallas_call(
        body, name="norm1", grid=(T // rows,),
        in_specs=[pl.BlockSpec((rows, D), lambda i: (i, 0)), pl.BlockSpec((1, D), lambda i: (0, 0))],
        out_specs=pl.BlockSpec((rows, D), lambda i: (i, 0)),
        out_shape=jax.ShapeDtypeStruct((T, D), jnp.bfloat16),
        compiler_params=_params(("parallel",)),
    )(x, g1)


def _seg_tiles(s):
    return (s + 2) % NSEG


def _proj(h, w_t, token):
    tm = 1024

    def body(h_ref, w_ref, token_any, o_ref):
        del token_any
        hv = h_ref[...]
        for i in range(4):
            o_ref[:, i * TILE:(i + 1) * TILE] = _dot(hv, w_ref[i], 1, 0)

    return pl.pallas_call(
        body, name="proj", grid=(T // tm, NSEG),
        in_specs=[pl.BlockSpec((tm, D), lambda m, s: (m, 0)),
                  pl.BlockSpec((4, D, TILE), lambda m, s: (_seg_tiles(s), 0, 0)), ANY],
        out_specs=pl.BlockSpec((None, tm, D), lambda m, s: (s, m, 0)),
        out_shape=jax.ShapeDtypeStruct((NSEG, T, D), jnp.float32),
        compiler_params=_params(("parallel", "parallel")),
    )(h, w_t, token)


def _row_ids(shape):
    return lax.broadcasted_iota(jnp.int32, shape, 0)


def _shift_down(a, k, rows):
    return jnp.where(rows >= k, pltpu.roll(a, k, 0), 0.0)


def _shift_up(a, k, rows):
    return jnp.where(rows < T - k, pltpu.roll(a, T - k, 0), 0.0)


def _window_sum(u, gidx, rows, shift):
    s2 = u + shift(u, 1, rows)
    s4 = s2 + shift(s2, 2, rows)
    s8 = s4 + shift(s4, 4, rows)
    s16 = s8 + shift(s8, 8, rows)
    return jnp.where(gidx == 0, s2, jnp.where(gidx == 1, s4, jnp.where(gidx == 2, s8, s16)))


def _pool_count(gidx, rows):
    width = lax.shift_left(jnp.int32(2), gidx)
    return jnp.minimum(rows + 1, width).astype(jnp.float32)


def _pool_fwd(proj, pool_w, pool_scale):
    def body(p_ref, w_ref, sc_ref, y_ref):
        gidx = pl.program_id(0)
        u, pg = p_ref[0], p_ref[1]
        rows = _row_ids(u.shape)
        d = _window_sum(u, gidx, rows, _shift_down) / _pool_count(gidx, rows) - u
        mixed = _dot(_bf(d), w_ref[...], 1, 0)
        y_ref[...] = _bf(mixed * sc_ref[...] * (pg * _sigmoid(pg)))

    return pl.pallas_call(
        body, name="pool_fwd", grid=(NGROUP,),
        in_specs=[pl.BlockSpec((2, T, GROUP), lambda g: (2, 0, g)),
                  pl.BlockSpec((None, GROUP, GROUP), lambda g: (g, 0, 0)),
                  pl.BlockSpec((1, GROUP), lambda g: (0, g))],
        out_specs=pl.BlockSpec((T, GROUP), lambda g: (0, g)),
        out_shape=jax.ShapeDtypeStruct((T, DMIX), jnp.bfloat16),
        compiler_params=_params(("parallel",)),
    )(proj, pool_w, pool_scale)


def _tri(lower):
    r = lax.broadcasted_iota(jnp.int32, (CHUNK, CHUNK), 0)
    c = lax.broadcasted_iota(jnp.int32, (CHUNK, CHUNK), 1)
    return (r >= c) if lower else (r <= c)


def _sum_rows_matrix():
    shape = (CHUNK + 16, CHUNK)
    r, c = lax.broadcasted_iota(jnp.int32, shape, 0), lax.broadcasted_iota(jnp.int32, shape, 1)
    run = jnp.where(c <= r, 1.0, 0.0)
    half = jnp.where(c < CHUNK // 2, 1.0, 0.0)
    return _bf(jnp.where(r < CHUNK, run, jnp.where(r < CHUNK + 8, 1.0, half)))


def _rev_sum_matrix():
    shape = (CHUNK, 2 * CHUNK)
    r, c = lax.broadcasted_iota(jnp.int32, shape, 0), lax.broadcasted_iota(jnp.int32, shape, 1)
    return _bf(jnp.where(c < CHUNK, jnp.where(c >= r, 1.0, 0.0), jnp.where(c - CHUNK < r, 1.0, 0.0)))


def _split3(a):
    a1 = _bf(a)
    r1 = a - a1.astype(jnp.float32)
    a2 = _bf(r1)
    return [a1, a2, _bf(r1 - a2.astype(jnp.float32))]


def _exact_sums(mat, pieces):
    x = jnp.concatenate([s for p in pieces for s in _split3(p)], axis=1)
    r = _dot(mat, x, 1, 0)
    return [r[:, 3 * j * HEAD:(3 * j + 1) * HEAD] + r[:, (3 * j + 1) * HEAD:(3 * j + 2) * HEAD]
            + r[:, (3 * j + 2) * HEAD:(3 * j + 3) * HEAD] for j in range(len(pieces))]


def _gates(qv, fl, lb):
    sq = _sigmoid(qv)
    sg = _sigmoid(fl)
    f = lb + (1.0 - lb) * sg
    return dict(sq=sq, qs=qv * sq, sg=sg, f=f, kk=1.0 - f, g=jnp.log(f))


def _decays(sums):
    big_g = sums[:CHUNK]
    total = sums[CHUNK:CHUNK + 8]
    g_last = jnp.tile(total, (CHUNK // 8, 1))
    g_mid = jnp.tile(sums[CHUNK + 8:], (CHUNK // 8, 1))
    return dict(
        e_q=jnp.exp(big_g),
        e_k=jnp.exp(g_last - big_g),
        e_qm=jnp.exp(jnp.minimum(big_g - g_mid, EXP_CAP)),
        e_km=jnp.exp(jnp.minimum(g_mid - big_g, EXP_CAP)),
        total8=jnp.exp(total),
        state=jnp.exp(jnp.tile(total, (HEAD // 8, 1))))


def _group_rows(gi):
    return [pl.ds(pl.multiple_of((gi * NB + j) * CHUNK, CHUNK), CHUNK) for j in range(NB)]


def _lower_bound(lb_ref):
    return _sigmoid(lb_ref[0:1, :] - lb_ref[1:2, :])


def _hgrn_fwd(proj, lb_logits, rec_g, y_in):
    def body(p_ref, lb_ref, rg_ref, y_any, y_ref, o_ref, st_ref):
        del y_any
        lb = _lower_bound(lb_ref)
        causal = _tri(True)
        smat = _sum_rows_matrix()

        def group(gi, st):
            rows = _group_rows(gi)
            ts = [_gates(p_ref[0, r, :], p_ref[1, r, :], lb) for r in rows]
            ds = [_decays(s) for s in _exact_sums(smat, [t["g"] for t in ts])]
            vs = [_bf(p_ref[2, r, :]) for r in rows]
            q_m = [_bf(t["qs"] * d["e_qm"]) for t, d in zip(ts, ds)]
            k_m = [_bf(t["kk"] * d["e_km"]) for t, d in zip(ts, ds)]
            q_e = [_bf(t["qs"] * d["e_q"]) for t, d in zip(ts, ds)]
            k_e = [_bf(t["kk"] * d["e_k"]) for t, d in zip(ts, ds)]
            a = [_bf(jnp.where(causal, _dot(q_m[j], k_m[j], 1, 1), 0.0)) for j in range(NB)]
            intra = [_dot(a[j], vs[j], 1, 0) for j in range(NB)]
            upd = [_dot(vs[j], k_e[j], 0, 0) for j in range(NB)]
            for j in range(NB):
                st_ref[gi * NB + j] = st
                o_ref[rows[j], :] = intra[j] + _dot(q_e[j], _bf(st), 1, 1)
                st = st * ds[j]["state"] + upd[j]
            return st

        lax.fori_loop(0, NGRP, group, jnp.zeros((HEAD, HEAD), jnp.float32))
        o = o_ref[...]
        rn = o * lax.rsqrt(jnp.mean(o * o, axis=-1, keepdims=True) + EPS)
        gate = p_ref[3]
        y_ref[...] = _bf(rn * rg_ref[...] * (gate * _sigmoid(gate)))

    return pl.pallas_call(
        body, name="hgrn_fwd", grid=(NHEAD,),
        in_specs=[pl.BlockSpec((4, T, HEAD), lambda h: (0, 0, h)),
                  pl.BlockSpec((2, HEAD), lambda h: (0, h)),
                  pl.BlockSpec((1, HEAD), lambda h: (0, h)),
                  pl.BlockSpec(memory_space=pl.ANY)],
        out_specs=(pl.BlockSpec((T, HEAD), lambda h: (0, NHEAD + h)),
                   pl.BlockSpec((T, HEAD), lambda h: (0, h)),
                   pl.BlockSpec((None, NCHUNK, HEAD, HEAD), lambda h: (h, 0, 0, 0))),
        out_shape=(jax.ShapeDtypeStruct((T, DMIX), jnp.bfloat16),
                   jax.ShapeDtypeStruct((T, D), jnp.float32),
                   jax.ShapeDtypeStruct((NHEAD, NCHUNK, HEAD, HEAD), jnp.float32)),
        input_output_aliases={3: 0},
        compiler_params=_params(("parallel",)),
    )(proj, lb_logits, rec_g, y_in)


def _out_proj(y, w_out):
    tm = 512

    def body(y_ref, w_ref, o_ref):
        o_ref[...] = _dot(y_ref[...], w_ref[...], 1, 0)

    return pl.pallas_call(
        body, name="out_proj", grid=(T // tm,),
        in_specs=[pl.BlockSpec((tm, DMIX), lambda m: (m, 0)), pl.BlockSpec((DMIX, D), lambda m: (0, 0))],
        out_specs=pl.BlockSpec((tm, D), lambda m: (m, 0)),
        out_shape=jax.ShapeDtypeStruct((T, D), jnp.float32),
        compiler_params=_params(("parallel",)),
    )(y, w_out)


def _loss_head(x, ymm, target, gf):
    rows = 256

    def body(x_ref, y_ref, t_ref, g_ref, dz_ref, dzb_ref, sq_ref, dg_ref):
        z = x_ref[...] + y_ref[...]
        r = lax.rsqrt(jnp.mean(z * z, axis=-1, keepdims=True) + EPS)
        zhat = z * r
        err = zhat * g_ref[...] - t_ref[...]
        dy = err * (1.0 / D)
        gdy = dy * g_ref[...]
        dz = r * (gdy - zhat * jnp.mean(zhat * gdy, axis=-1, keepdims=True))
        dz_ref[...] = dz
        dzb_ref[...] = _bf(dz)
        sq = jnp.sum(err * err, axis=0, keepdims=True)
        dg = jnp.sum(zhat * dy, axis=0, keepdims=True)

        @pl.when(pl.program_id(0) == 0)
        def _():
            sq_ref[...] = sq
            dg_ref[...] = dg

        @pl.when(pl.program_id(0) != 0)
        def _():
            sq_ref[...] += sq
            dg_ref[...] += dg

    tile = pl.BlockSpec((rows, D), lambda i: (i, 0))
    vec = pl.BlockSpec((1, D), lambda i: (0, 0))
    return pl.pallas_call(
        body, name="loss_head", grid=(T // rows,),
        in_specs=[tile, tile, tile, vec], out_specs=(tile, tile, vec, vec),
        out_shape=(jax.ShapeDtypeStruct((T, D), jnp.float32), jax.ShapeDtypeStruct((T, D), jnp.bfloat16),
                   jax.ShapeDtypeStruct((1, D), jnp.float32), jax.ShapeDtypeStruct((1, D), jnp.float32)),
        compiler_params=_params(("arbitrary",)),
    )(x, ymm, target, gf)


def _out_proj_bwd(dzb, w_out, y):
    tn = 512

    def body(dz_ref, w_ref, y_ref, dy_ref, gw_ref, gwb_ref):
        dz = dz_ref[...]
        dy_ref[...] = _dot(dz, w_ref[...], 1, 1)
        gw = _dot(y_ref[...], dz, 0, 0)
        gw_ref[...] = gw
        gwb_ref[...] = _bf(gw)

    return pl.pallas_call(
        body, name="out_proj_bwd", grid=(DMIX // tn,),
        in_specs=[pl.BlockSpec((T, D), lambda n: (0, 0)), pl.BlockSpec((tn, D), lambda n: (n, 0)),
                  pl.BlockSpec((T, tn), lambda n: (0, n))],
        out_specs=(pl.BlockSpec((T, tn), lambda n: (0, n)), pl.BlockSpec((tn, D), lambda n: (n, 0)),
                   pl.BlockSpec((tn, D), lambda n: (n, 0))),
        out_shape=(jax.ShapeDtypeStruct((T, DMIX), jnp.float32), jax.ShapeDtypeStruct((DMIX, D), jnp.float32),
                   jax.ShapeDtypeStruct((DMIX, D), jnp.bfloat16)),
        compiler_params=_params(("parallel",)),
    )(dzb, w_out, y)


def _hgrn_bwd(proj, lb_logits, rec_g, o, states, dymix, dproj_in, token):
    def body(p_ref, lb_ref, rg_ref, o_ref, st_ref, dy_ref, dp_any, token_any, dp_ref, drg_ref, dlb_ref, do_ref):
        del dp_any, token_any
        lb = _lower_bound(lb_ref)
        causal = _tri(True)
        smat, rmat = _sum_rows_matrix(), _rev_sum_matrix()

        o = o_ref[...]
        rs = lax.rsqrt(jnp.mean(o * o, axis=-1, keepdims=True) + EPS)
        rn = o * rs
        gate = p_ref[3]
        sgate = _sigmoid(gate)
        dyv = dy_ref[...]
        d_r = dyv * (gate * sgate)
        dp_ref[3] = _bf(dyv * (rn * rg_ref[...]) * (sgate * (1.0 + gate * (1.0 - sgate))))
        drg_ref[...] = jnp.sum(d_r * rn, axis=0, keepdims=True)
        drn = d_r * rg_ref[...]
        do_ref[...] = rs * (drn - rn * jnp.mean(rn * drn, axis=-1, keepdims=True))

        def group(i, carry):
            dst, dlb = carry
            gi = NGRP - 1 - i
            rows = _group_rows(gi)
            span = range(NB)
            qvs = [p_ref[0, r, :] for r in rows]
            ts = [_gates(qv, p_ref[1, r, :], lb) for qv, r in zip(qvs, rows)]
            ds = [_decays(s) for s in _exact_sums(smat, [t["g"] for t in ts])]
            vs = [_bf(p_ref[2, r, :]) for r in rows]
            dos = [_bf(do_ref[r, :]) for r in rows]
            sts = [st_ref[gi * NB + j] for j in span]
            qe_f = [t["qs"] * d["e_q"] for t, d in zip(ts, ds)]
            ke_f = [t["kk"] * d["e_k"] for t, d in zip(ts, ds)]
            q_e, k_e = [_bf(a) for a in qe_f], [_bf(a) for a in ke_f]
            q_m = [_bf(t["qs"] * d["e_qm"]) for t, d in zip(ts, ds)]
            k_m = [_bf(t["kk"] * d["e_km"]) for t, d in zip(ts, ds)]
            a = [_bf(jnp.where(causal, _dot(q_m[j], k_m[j], 1, 1), 0.0)) for j in span]
            da = [_bf(jnp.where(causal, _dot(dos[j], vs[j], 1, 1), 0.0)) for j in span]
            dqm = [_dot(da[j], k_m[j], 1, 0) for j in span]
            dkm = [_dot(da[j], q_m[j], 0, 0) for j in span]
            dv_in = [_dot(a[j], dos[j], 0, 0) for j in span]
            dqe = [_dot(dos[j], _bf(sts[j]), 1, 0) for j in span]
            grow = [_dot(dos[j], q_e[j], 0, 0) for j in span]
            dke, carried = [None] * NB, [None] * NB
            for j in reversed(span):
                dst_b = _bf(dst)
                dke[j] = _dot(vs[j], dst_b, 1, 0)
                dp_ref[2, rows[j], :] = _bf(dv_in[j] + _dot(k_e[j], dst_b, 1, 1))
                carried[j] = ds[j]["total8"] * jnp.sum(dst * sts[j], axis=0, keepdims=True)
                dst = dst * ds[j]["state"] + grow[j]
            kdk = [ke_f[j] * dke[j] for j in span]
            pos = [(q_m[j].astype(jnp.float32) * dqm[j] - k_m[j].astype(jnp.float32) * dkm[j]) + qe_f[j] * dqe[j]
                   for j in span]
            dgs = _exact_sums(rmat, [jnp.concatenate([pos[j], kdk[j]], axis=0) for j in span])
            for j in span:
                t, d = ts[j], ds[j]
                dg = dgs[j] + jnp.tile(carried[j], (CHUNK // 8, 1))
                dqs = dqm[j] * d["e_qm"] + dqe[j] * d["e_q"]
                dkk = dkm[j] * d["e_km"] + dke[j] * d["e_k"]
                df = dg / t["f"] - dkk
                dp_ref[1, rows[j], :] = _bf(df * (1.0 - lb) * (t["sg"] * (1.0 - t["sg"])))
                dp_ref[0, rows[j], :] = _bf(dqs * (t["sq"] * (1.0 + qvs[j] * (1.0 - t["sq"]))))
                dlb = dlb + df * (1.0 - t["sg"])
            return dst, dlb

        _, dlb = lax.fori_loop(0, NGRP, group, (jnp.zeros((HEAD, HEAD), jnp.float32),
                                                jnp.zeros((CHUNK, HEAD), jnp.float32)))
        dlb_ref[...] = jnp.sum(dlb, axis=0, keepdims=True)

    vec = pl.BlockSpec((1, HEAD), lambda h: (0, h))
    return pl.pallas_call(
        body, name="hgrn_bwd", grid=(NHEAD,),
        in_specs=[pl.BlockSpec((4, T, HEAD), lambda h: (0, 0, h)),
                  pl.BlockSpec((2, HEAD), lambda h: (0, h)), vec,
                  pl.BlockSpec((T, HEAD), lambda h: (0, h)),
                  pl.BlockSpec((None, NCHUNK, HEAD, HEAD), lambda h: (h, 0, 0, 0)),
                  pl.BlockSpec((T, HEAD), lambda h: (0, NHEAD + h)), ANY, ANY],
        out_specs=(pl.BlockSpec((4, T, HEAD), lambda h: (0, 0, h)), vec, vec),
        out_shape=(jax.ShapeDtypeStruct((NSEG, T, D), jnp.bfloat16),
                   jax.ShapeDtypeStruct((1, D), jnp.float32), jax.ShapeDtypeStruct((1, D), jnp.float32)),
        scratch_shapes=[pltpu.VMEM((T, HEAD), jnp.float32)],
        input_output_aliases={6: 0},
        compiler_params=_params(("parallel",)),
    )(proj, lb_logits, rec_g, o, states, dymix, dproj_in, token)


def _pool_bwd(proj, pool_w, pool_scale, dymix):
    def body(p_ref, w_ref, sc_ref, dy_ref, dp_ref, gw_ref, gs_ref):
        gidx = pl.program_id(0)
        u, pg = p_ref[0], p_ref[1]
        rows = _row_ids(u.shape)
        count = _pool_count(gidx, rows)
        d = _bf(_window_sum(u, gidx, rows, _shift_down) / count - u)
        mixed = _dot(d, w_ref[...], 1, 0)
        spg = _sigmoid(pg)
        dyv = dy_ref[...]
        d_p = dyv * (pg * spg)
        dp_ref[1] = _bf(dyv * (mixed * sc_ref[...]) * (spg * (1.0 + pg * (1.0 - spg))))
        gs_ref[...] = jnp.sum(d_p * mixed, axis=0, keepdims=True)
        dmixed = _bf(d_p * sc_ref[...])
        gw_ref[...] = _dot(d, dmixed, 0, 0)
        dd = _dot(dmixed, w_ref[...], 1, 1)
        dp_ref[0] = _bf(_window_sum(dd / count, gidx, rows, _shift_up) - dd)

    return pl.pallas_call(
        body, name="pool_bwd", grid=(NGROUP,),
        in_specs=[pl.BlockSpec((2, T, GROUP), lambda g: (2, 0, g)),
                  pl.BlockSpec((None, GROUP, GROUP), lambda g: (g, 0, 0)),
                  pl.BlockSpec((1, GROUP), lambda g: (0, g)),
                  pl.BlockSpec((T, GROUP), lambda g: (0, g))],
        out_specs=(pl.BlockSpec((2, T, GROUP), lambda g: (2, 0, g)),
                   pl.BlockSpec((None, GROUP, GROUP), lambda g: (g, 0, 0)),
                   pl.BlockSpec((1, GROUP), lambda g: (0, g))),
        out_shape=(jax.ShapeDtypeStruct((NSEG, T, D), jnp.bfloat16),
                   jax.ShapeDtypeStruct((NGROUP, GROUP, GROUP), jnp.float32),
                   jax.ShapeDtypeStruct((1, D), jnp.float32)),
        compiler_params=_params(("parallel",)),
    )(proj, pool_w, pool_scale, dymix)


def _proj_bwd_w(h, dproj):
    def body(h_ref, dp_ref, gw_ref, gwb_ref):
        gw = _dot(h_ref[...], dp_ref[...], 0, 0)
        for i in range(4):
            gw_ref[i] = gw[:, i * TILE:(i + 1) * TILE]
            gwb_ref[i] = _bf(gw[:, i * TILE:(i + 1) * TILE])

    out = pl.BlockSpec((4, D, TILE), lambda s: (_seg_tiles(s), 0, 0))
    return pl.pallas_call(
        body, name="proj_bwd_w", grid=(NSEG,),
        in_specs=[pl.BlockSpec((T, D), lambda s: (0, 0)), pl.BlockSpec((None, T, D), lambda s: (s, 0, 0))],
        out_specs=(out, out),
        out_shape=(jax.ShapeDtypeStruct((NTILE, D, TILE), jnp.float32),
                   jax.ShapeDtypeStruct((NTILE, D, TILE), jnp.bfloat16)),
        compiler_params=_params(("parallel",)),
    )(h, dproj)


def _proj_bwd_x(dproj, w_t, token):
    tm = 1024

    def body(dp_ref, w_ref, token_any, o_ref, wcat):
        del token_any
        for i in range(4):
            wcat[:, i * TILE:(i + 1) * TILE] = w_ref[i]
        r = _dot(dp_ref[...], wcat[...], 1, 1)

        @pl.when(pl.program_id(1) == 0)
        def _():
            o_ref[...] = r

        @pl.when(pl.program_id(1) != 0)
        def _():
            o_ref[...] += r

    return pl.pallas_call(
        body, name="proj_bwd_x", grid=(T // tm, NSEG),
        in_specs=[pl.BlockSpec((None, tm, D), lambda m, s: (s, m, 0)),
                  pl.BlockSpec((4, D, TILE), lambda m, s: (_seg_tiles(s), 0, 0)), ANY],
        out_specs=pl.BlockSpec((tm, D), lambda m, s: (m, 0)),
        out_shape=jax.ShapeDtypeStruct((T, D), jnp.float32),
        scratch_shapes=[pltpu.VMEM((D, D), jnp.bfloat16)],
        compiler_params=_params(("parallel", "arbitrary")),
    )(dproj, w_t, token)


def _norm1_bwd(x, g1, dh, dz):
    rows = 256

    def body(x_ref, g_ref, dh_ref, dz_ref, dx_ref, dg_ref):
        xv = x_ref[...]
        r = lax.rsqrt(jnp.mean(xv * xv, axis=-1, keepdims=True) + EPS)
        xhat = xv * r
        dhv = dh_ref[...]
        gdh = dhv * g_ref[...]
        dx_ref[...] = dz_ref[...] + r * (gdh - xhat * jnp.mean(xhat * gdh, axis=-1, keepdims=True))
        dg = jnp.sum(xhat * dhv, axis=0, keepdims=True)

        @pl.when(pl.program_id(0) == 0)
        def _():
            dg_ref[...] = dg

        @pl.when(pl.program_id(0) != 0)
        def _():
            dg_ref[...] += dg

    tile = pl.BlockSpec((rows, D), lambda i: (i, 0))
    vec = pl.BlockSpec((1, D), lambda i: (0, 0))
    return pl.pallas_call(
        body, name="norm1_bwd", grid=(T // rows,),
        in_specs=[tile, vec, tile, tile], out_specs=(tile, vec),
        out_shape=(jax.ShapeDtypeStruct((T, D), jnp.float32), jax.ShapeDtypeStruct((1, D), jnp.float32)),
        compiler_params=_params(("arbitrary",)),
    )(x, g1, dh, dz)


def _adamw(w, g, m, v):
    m_new = ADAM_B1 * m + (1.0 - ADAM_B1) * g
    v_new = ADAM_B2 * v + (1.0 - ADAM_B2) * (g * g)
    delta = -ADAM_LR * ((m_new / BC1) / (jnp.sqrt(v_new / BC2) + ADAM_EPS) + ADAM_WD * w)
    return delta, m_new, v_new


def _reduce_adam(name, place, parts, w, m, v, grid, w_spec):
    n = len(parts)

    def body(place_ref, *refs):
        del place_ref
        w_ref, m_ref, v_ref, g_ref, d_ref, mo_ref, vo_ref = refs[n:]
        g = None
        for ref, (_, _, stacked) in zip(refs[:n], parts):
            terms = [ref[r] for r in range(ref.shape[0])] if stacked else [ref[...]]
            for t in terms:
                g = t.astype(jnp.float32) if g is None else g + t.astype(jnp.float32)
        delta, m_new, v_new = _adamw(w_ref[...], g, m_ref[...], v_ref[...])
        g_ref[...] = g
        d_ref[...] = delta
        mo_ref[...] = m_new
        vo_ref[...] = v_new

    shape = jax.ShapeDtypeStruct(w.shape, jnp.float32)
    return pl.pallas_call(
        body, name=name,
        grid_spec=pltpu.PrefetchScalarGridSpec(
            num_scalar_prefetch=1, grid=grid,
            in_specs=[spec for _, spec, _ in parts] + [w_spec] * 3, out_specs=(w_spec,) * 4),
        out_shape=(shape,) * 4,
        compiler_params=_params(("parallel",)),
    )(place, *[a for a, _, _ in parts], w, m, v)


def _small_adam(parts, w, m, v):
    def body(p_ref, w_ref, m_ref, v_ref, g_ref, d_ref, mo_ref, vo_ref):
        g = p_ref[0]
        for s in range(1, NDEV):
            g = g + p_ref[s]
        wv = w_ref[...]
        rows = _row_ids(wv.shape)
        other = jnp.where(rows == 2, pltpu.roll(wv, 7, 0), jnp.where(rows == 3, pltpu.roll(wv, 1, 0), 0.0))
        lbv = _sigmoid(wv - other)
        sign = jnp.where(rows == 2, 1.0, -1.0)
        g = jnp.where((rows == 2) | (rows == 3), sign * g * lbv * (1.0 - lbv), g)
        delta, m_new, v_new = _adamw(wv, g, m_ref[...], v_ref[...])
        g_ref[...] = g
        d_ref[...] = delta
        mo_ref[...] = m_new
        vo_ref[...] = v_new

    shape = jax.ShapeDtypeStruct((8, D), jnp.float32)
    return pl.pallas_call(body, name="small_adam", out_shape=(shape,) * 4)(parts, w, m, v)


def _rows8(*vecs):
    rows = [a.reshape(-1, D) for a in vecs]
    n = sum(r.shape[0] for r in rows)
    return jnp.concatenate(rows + [jnp.zeros((8 - n, D), jnp.float32)], axis=0)


def kernel(x, norm1_g, w_in, pool_w, pool_scale, lb_logits, rec_norm_g, w_out, final_norm_g, loss_target, m_norm1_g, m_w_in, m_pool_w, m_pool_scale, m_lb_logits, m_rec_norm_g, m_w_out, m_final_norm_g, v_norm1_g, v_w_in, v_pool_w, v_pool_scale, v_lb_logits, v_rec_norm_g, v_w_out, v_final_norm_g):
    xs = x[0]
    target = loss_target[0]
    ix, iy, ic = lax.axis_index("x"), lax.axis_index("y"), lax.axis_index("c")
    place = jnp.stack([4 * ix + 2 * iy + ic, 2 * ix + iy, ic]).astype(jnp.int32)
    gf = final_norm_g.reshape(1, D)

    w_t3 = _bf(w_in[0]).reshape(D, 3, TILE).transpose(1, 0, 2)
    w_out_b = _bf(w_out[0])
    w_t, w_out_g, pool_g = _gather_weights(w_t3, w_out_b, _bf(pool_w[0]))
    pool_full = pool_g.transpose(1, 0, 2, 3).reshape(NGROUP, GROUP, GROUP)
    wout = [w_out_b, w_out_g]
    wout_send, wout_recv, wout, wout_token = _split_start("gather_wout_start", wout, NDEV - 1, _plan_wout)

    h = _norm1(xs, norm1_g)
    proj = _proj(h, w_t, wout_token)
    y = _pool_fwd(proj, pool_full, pool_scale)
    y, o, states = _hgrn_fwd(proj, lb_logits, rec_norm_g, y)
    _, w_out_g = _split_wait("gather_wout_wait", wout, wout_send, wout_recv, _plan_wout, o)
    w_out_full = w_out_g.reshape(DMIX, D)
    ymm = _out_proj(y, w_out_full)
    dz, dzb, sq, dgf = _loss_head(xs, ymm, target, gf)

    dymix, gwout_f, gwout_b = _out_proj_bwd(dzb, w_out_full, y)
    dproj, gpool, dscale = _pool_bwd(proj, pool_full, pool_scale, dymix)

    blk_out = (NDEV, DMIX // NDEV, D)
    blk_pool = (NDEV, NGROUP, GROUP // NDEV, GROUP)
    gpool_s = gpool.reshape(NGROUP, NDEV, GROUP // NDEV, GROUP).transpose(1, 0, 2, 3)
    rest = [gwout_b.reshape(blk_out), gpool_s,
            lax.empty((NDEV - 1,) + blk_out[1:], jnp.bfloat16), lax.empty((NDEV - 1,) + blk_pool[1:], jnp.float32)]
    rest_send, rest_recv, rest, rest_token = _split_start("scatter_rest_start", rest, 2 * (NDEV - 1), _plan_rest)

    dproj, drecg, dlb = _hgrn_bwd(proj, lb_logits, rec_norm_g, o, states, dymix, dproj, rest_token)
    gw_f, gw_b = _proj_bwd_w(h, dproj)

    by_owner = (4, 2, 3, D, TILE)
    gw_f5 = gw_f.reshape(by_owner)
    from_sibling = _pair_exchange(gw_b.reshape(by_owner))
    win = [_pair_sum(place, gw_f5, from_sibling), lax.empty((3, 3, D, TILE), jnp.bfloat16)]
    win_send, win_recv, win, win_token = _split_start("scatter_win_start", win, 3, _plan_in)

    dh = _proj_bwd_x(dproj, w_t, win_token)
    grad_x, dg1 = _norm1_bwd(xs, norm1_g, dh, dz)

    _, gpool_own, r_out, r_pool = _split_wait("scatter_rest_wait", rest, rest_send, rest_recv, _plan_rest, grad_x)
    g_wout, d_wout, m_wout, v_wout = _reduce_adam(
        "adam_w_out", place,
        [(gwout_f.reshape(blk_out), pl.BlockSpec((None,) + blk_out[1:], lambda i, pr: (pr[0], 0, 0)), False),
         (r_out, pl.BlockSpec((NDEV - 1,) + blk_out[1:], lambda i, pr: (0, 0, 0)), True)],
        w_out, m_w_out, v_w_out, (1,), pl.BlockSpec((None,) + blk_out[1:], lambda i, pr: (0, 0, 0)))
    g_pool, d_pool, m_pool, v_pool = _reduce_adam(
        "adam_pool_w", place,
        [(gpool_own, pl.BlockSpec((None,) + blk_pool[1:], lambda i, pr: (pr[0], 0, 0, 0)), False),
         (r_pool, pl.BlockSpec((NDEV - 1,) + blk_pool[1:], lambda i, pr: (0, 0, 0, 0)), True)],
        pool_w, m_pool_w, v_pool_w, (1,), pl.BlockSpec((None,) + blk_pool[1:], lambda i, pr: (0, 0, 0, 0)))

    r_small = _gather_small(_rows8(dg1, dscale, dlb, dlb, drecg, dgf, sq), d_wout, d_pool)
    loss = jnp.sum(r_small[:, 6, :]) * (0.5 / D)
    g_s, d_s, m_s, v_s = _small_adam(
        r_small,
        _rows8(norm1_g, pool_scale, lb_logits, rec_norm_g, final_norm_g),
        _rows8(m_norm1_g, m_pool_scale, m_lb_logits, m_rec_norm_g, m_final_norm_g),
        _rows8(v_norm1_g, v_pool_scale, v_lb_logits, v_rec_norm_g, v_final_norm_g))

    _, r_in = _split_wait("scatter_win_wait", win, win_send, win_recv, _plan_in, v_s)
    g_win, d_win, m_win, v_win = _reduce_adam(
        "adam_w_in", place,
        [(gw_f5, pl.BlockSpec((None, None, None, D, TILE), lambda i, pr: (pr[1], pr[2], i, 0, 0)), False),
         (from_sibling, pl.BlockSpec((None, None, None, D, TILE), lambda i, pr: (pr[1], 0, i, 0, 0)), False),
         (r_in, pl.BlockSpec((3, None, D, TILE), lambda i, pr: (0, i, 0, 0)), True)],
        w_in, m_w_in, v_w_in, (3,), pl.BlockSpec((None, D, TILE), lambda i, pr: (0, 0, i)))

    def small_outs(a):
        return a[0:1], a[1:2], a[2:4], a[4:5], a[5]

    def outs(small_a, win, pool, wout):
        n1, ps, lbl, rg, fg = small_outs(small_a)
        return n1, win, pool, ps, lbl, rg, wout, fg

    return (loss, grad_x[None],
            *outs(g_s, g_win, g_pool, g_wout), *outs(d_s, d_win, d_pool, d_wout),
            *outs(m_s, m_win, m_pool, m_wout), *outs(v_s, v_win, v_pool, v_wout))
```

```python
import functools

import jax
import jax.numpy as jnp
from jax import lax
from jax.experimental import pallas as pl
from jax.experimental.pallas import tpu as pltpu

T = 2048
D = 1024
NSEG = 6
NTILE = 24
TILE = 256
DMIX = 2048
NDEV = 8
HEAD = 128
NHEAD = 8
CHUNK = 64
NCHUNK = T // CHUNK
NB = 8
NGRP = NCHUNK // NB
NGROUP = 4
GROUP = 256
EPS = 1e-6
EXP_CAP = 80.0
MESH = pl.DeviceIdType.MESH
AXES = ("x", "y", "c")
ANY = pl.BlockSpec(memory_space=pl.ANY)
HBM = pl.BlockSpec(memory_space=pltpu.HBM)
SEM = pl.BlockSpec(memory_space=pltpu.SEMAPHORE)
EFFECT = pltpu.SideEffectType.DATAFLOW_SIDE_EFFECTING

ADAM_LR = 0.001
ADAM_B1 = 0.9
ADAM_B2 = 0.999
ADAM_EPS = 1e-08
ADAM_WD = 0.01
ADAM_STEP = 10
BC1 = 1.0 - ADAM_B1 ** ADAM_STEP
BC2 = 1.0 - ADAM_B2 ** ADAM_STEP

MIB = 1 << 20


def _params(sem=None, vmem_mib=48):
    return pltpu.CompilerParams(dimension_semantics=sem, vmem_limit_bytes=vmem_mib * MIB)


def _sigmoid(v):
    return 1.0 / (1.0 + jnp.exp(-v))


def _dot(a, b, ca, cb, precision=None):
    return lax.dot_general(a, b, (((ca,), (cb,)), ((), ())), precision=precision,
                           preferred_element_type=jnp.float32)


def _bf(v):
    return v.astype(jnp.bfloat16)


def _place():
    x, y, c = lax.axis_index("x"), lax.axis_index("y"), lax.axis_index("c")
    return x, y, c, 4 * x + 2 * y + c


def _peer(x, y, c, r):
    return (x ^ ((r >> 2) & 1), y ^ ((r >> 1) & 1), c ^ (r & 1))


def _gather_proj(x, g1, w_t3, w_out_b, pool_b):
    def body(x_ref, g_ref, win_ref, wout_ref, pool_ref, h_ref, wt_o, wout_o, pool_o, proj_o,
             wv, stage, send_sems, recv_sems, loc_sems, out_sems):
        px, py, c, my_idx = _place()
        me, sibling = (px, py, c), (px, py, 1 - c)
        chips = [(1 - px, py), (px, 1 - py), (1 - px, 1 - py)]
        srcs = (win_ref, pool_ref)

        def index(bx, by, bc):
            return 4 * bx + 2 * by + bc

        def slot(w, block):
            return wv.at[pl.ds(3 * index(*block), 3)] if w == 0 else pool_o.at[index(*block)]

        def copy(k, w, block, to, src=None):
            return pltpu.make_async_remote_copy(
                src_ref=slot(w, block) if src is None else src, dst_ref=slot(w, block),
                send_sem=send_sems.at[2 * k + w], recv_sem=recv_sems.at[2 * k + w],
                device_id=to, device_id_type=MESH)

        locs = [pltpu.make_async_copy(win_ref, slot(0, me), loc_sems.at[0]),
                pltpu.make_async_copy(pool_ref, slot(1, me), loc_sems.at[1]),
                pltpu.make_async_copy(wout_ref, wout_o.at[my_idx], loc_sems.at[2])]
        for cp in locs:
            cp.start()
        first = []
        for w in (0, 1):
            first += [copy(1 + j, w, me, (*chip, c), src=srcs[w]) for j, chip in enumerate(chips)]
            first.append(copy(0, w, me, sibling, src=srcs[w]))
        for cp in first:
            cp.start()

        xv = x_ref[...]
        h_ref[...] = _bf(xv * lax.rsqrt(jnp.mean(xv * xv, axis=-1, keepdims=True) + EPS) * g_ref[...])

        def out_copy(p, j):
            return pltpu.make_async_copy(stage.at[p], proj_o.at[j], out_sems.at[p])

        def project(nth, block):
            base = 3 * index(*block)

            def tile(p, carry):
                if nth > 0:
                    out_copy(p, base + p).wait()
                stage[p] = _dot(h_ref[...], wv[base + p], 1, 0)
                out_copy(p, base + p).start()
                return carry

            lax.fori_loop(0, 3, tile, 0)

        locs[0].wait()
        project(0, me)
        copy(0, 0, sibling, me).wait_recv()
        project(1, sibling)
        passed = []
        for j, chip in enumerate(chips):
            copy(1 + j, 0, (*chip, c), me).wait_recv()
            passed.append(copy(4 + j, 0, (*chip, c), sibling))
            passed[-1].start()
            project(2 + j, (*chip, c))
        for j, chip in enumerate(chips):
            copy(1 + j, 1, (*chip, c), me).wait_recv()
            passed.append(copy(4 + j, 1, (*chip, c), sibling))
            passed[-1].start()
        for j, chip in enumerate(chips):
            copy(4 + j, 0, (*chip, 1 - c), me).wait_recv()
            project(5 + j, (*chip, 1 - c))
        copy(0, 1, sibling, me).wait_recv()
        for j, chip in enumerate(chips):
            copy(4 + j, 1, (*chip, 1 - c), me).wait_recv()
        keep = pltpu.make_async_copy(wv, wt_o, loc_sems.at[3])
        keep.start()
        for p in range(3):
            out_copy(p, p).wait()
        for cp in first + passed:
            cp.wait_send()
        keep.wait()
        for cp in locs[1:]:
            cp.wait()

    vmem = pl.BlockSpec(memory_space=pltpu.VMEM)
    return pl.pallas_call(
        body, name="gather_proj",
        out_shape=(jax.ShapeDtypeStruct((T, D), jnp.bfloat16),
                   jax.ShapeDtypeStruct((NTILE, D, TILE), jnp.bfloat16),
                   jax.ShapeDtypeStruct((NDEV, DMIX // NDEV, D), jnp.bfloat16),
                   jax.ShapeDtypeStruct((NDEV, NGROUP, GROUP // NDEV, GROUP), jnp.bfloat16),
                   jax.ShapeDtypeStruct((NTILE, T, TILE), jnp.float32)),
        in_specs=[vmem, vmem, ANY, ANY, ANY], out_specs=(vmem, ANY, ANY, ANY, ANY),
        scratch_shapes=[pltpu.VMEM((NTILE, D, TILE), jnp.bfloat16), pltpu.VMEM((3, T, TILE), jnp.float32),
                        pltpu.SemaphoreType.DMA((14,)), pltpu.SemaphoreType.DMA((14,)),
                        pltpu.SemaphoreType.DMA((4,)), pltpu.SemaphoreType.DMA((3,))],
        compiler_params=_params(vmem_mib=56),
    )(x, g1, w_t3, w_out_b, pool_b)


def _split_start(name, arrays, n_copies, plan):
    k = len(arrays)

    def body(*refs):
        send_sems, recv_sems, token = refs[k], refs[k + 1], refs[-1]
        for i, (src, dst, to) in enumerate(plan(refs[:k])):
            pltpu.make_async_remote_copy(src_ref=src, dst_ref=dst, send_sem=send_sems.at[i],
                                         recv_sem=recv_sems.at[i], device_id=to, device_id_type=MESH).start()
        token[...] = jnp.zeros_like(token)

    out = pl.pallas_call(
        body, name=name,
        out_shape=(pltpu.SemaphoreType.DMA((n_copies,)), pltpu.SemaphoreType.DMA((n_copies,)),
                   *[pltpu.HBM(a.shape, a.dtype) for a in arrays], jax.ShapeDtypeStruct((8, 128), jnp.float32)),
        in_specs=[HBM] * k, out_specs=(SEM, SEM, *[HBM] * k, pl.BlockSpec(memory_space=pltpu.VMEM)),
        input_output_aliases={i: 2 + i for i in range(k)},
        compiler_params=pltpu.CompilerParams(has_side_effects=EFFECT),
    )(*[pltpu.with_memory_space_constraint(a, pltpu.HBM) for a in arrays])
    return out[0], out[1], out[2:2 + k], out[-1]


def _split_wait(name, arrays, send_sems, recv_sems, plan, after):
    k = len(arrays)

    def body(*refs):
        sends, recvs = refs[k], refs[k + 1]
        for i, (src, dst, to) in enumerate(plan(refs[:k])):
            cp = pltpu.make_async_remote_copy(src_ref=src, dst_ref=dst, send_sem=sends.at[i], recv_sem=recvs.at[i],
                                              device_id=to, device_id_type=MESH)
            cp.wait_send()
            cp.wait_recv()

    return pl.pallas_call(
        body, name=name,
        out_shape=tuple(pltpu.HBM(a.shape, a.dtype) for a in arrays),
        in_specs=[HBM] * k + [SEM, SEM, ANY], out_specs=(HBM,) * k,
        input_output_aliases={i: i for i in range(k)},
        compiler_params=pltpu.CompilerParams(has_side_effects=EFFECT),
    )(*arrays, send_sems, recv_sems, after)


def _plan_wout(refs):
    src, land = refs
    x, y, c, me = _place()
    return [(src, land.at[me], _peer(x, y, c, r)) for r in range(1, NDEV)]


def _plan_rest(refs):
    gob, gpf, r_out, r_pool = refs
    x, y, c, me = _place()
    plan = []
    for r in range(1, NDEV):
        plan.append((gob.at[me ^ r], r_out.at[r - 1], _peer(x, y, c, r)))
        plan.append((gpf.at[me ^ r], r_pool.at[r - 1], _peer(x, y, c, r)))
    return plan


def _plan_in(refs):
    sum_b, r_in = refs
    x, y, c, _ = _place()
    plan = []
    for j, (dx, dy) in enumerate(((1, 0), (0, 1), (1, 1))):
        px, py = x ^ dx, y ^ dy
        plan.append((sum_b.at[2 * px + py], r_in.at[j], (px, py, c)))
    return plan


def _pair_exchange(gw_b5):
    def body(g_ref, r_ref, send_sem, recv_sem):
        x, y, c, _ = _place()
        cp = pltpu.make_async_remote_copy(src_ref=g_ref.at[:, pl.ds(1 - c, 1)], dst_ref=r_ref, send_sem=send_sem,
                                          recv_sem=recv_sem, device_id=(x, y, 1 - c), device_id_type=MESH)
        cp.start()
        cp.wait()

    return pl.pallas_call(
        body, name="pair_exchange",
        out_shape=jax.ShapeDtypeStruct((4, 1, 3, D, TILE), jnp.bfloat16),
        in_specs=[ANY], out_specs=ANY,
        scratch_shapes=[pltpu.SemaphoreType.DMA, pltpu.SemaphoreType.DMA],
    )(gw_b5)


def _pair_sum(place, gw_f5, recv):
    def body(place_ref, g_ref, r_ref, b_ref):
        del place_ref
        b_ref[...] = _bf(g_ref[...] + r_ref[...].astype(jnp.float32))

    return pl.pallas_call(
        body, name="pair_sum",
        grid_spec=pltpu.PrefetchScalarGridSpec(
            num_scalar_prefetch=1, grid=(4, 3),
            in_specs=[pl.BlockSpec((None, None, None, D, TILE), lambda q, p, pr: (q, pr[2], p, 0, 0)),
                      pl.BlockSpec((None, None, None, D, TILE), lambda q, p, pr: (q, 0, p, 0, 0))],
            out_specs=pl.BlockSpec((None, None, D, TILE), lambda q, p, pr: (q, p, 0, 0))),
        out_shape=jax.ShapeDtypeStruct((4, 3, D, TILE), jnp.bfloat16),
        compiler_params=_params(("parallel", "parallel")),
    )(place, gw_f5, recv)


def _gather_small(small, *after):
    def body(sm, *refs):
        r_small, send_sems, recv_sems, loc_sem = refs[len(after):]
        x, y, c, me = _place()
        loc = pltpu.make_async_copy(sm, r_small.at[me], loc_sem)
        loc.start()

        def copy(r, src_idx):
            return pltpu.make_async_remote_copy(
                src_ref=sm, dst_ref=r_small.at[src_idx], send_sem=send_sems.at[r - 1], recv_sem=recv_sems.at[r - 1],
                device_id=_peer(x, y, c, r), device_id_type=MESH)

        sends = [copy(r, me) for r in range(1, NDEV)]
        for cp in sends:
            cp.start()
        for r in range(1, NDEV):
            copy(r, me ^ r).wait_recv()
        for cp in sends:
            cp.wait_send()
        loc.wait()

    return pl.pallas_call(
        body, name="gather_small",
        out_shape=jax.ShapeDtypeStruct((NDEV, 8, D), jnp.float32),
        in_specs=[ANY] * (1 + len(after)), out_specs=ANY,
        scratch_shapes=[pltpu.SemaphoreType.DMA((NDEV - 1,)), pltpu.SemaphoreType.DMA((NDEV - 1,)),
                        pltpu.SemaphoreType.DMA],
    )(small, *after)


def _seg_tiles(s):
    return (s + 2) % NSEG


_POOL_SPECS = [pl.BlockSpec((None, T, GROUP), lambda g, base=base: (base + g, 0, 0)) for base in (0, 4)]
_HEAD_SPECS = [pl.BlockSpec((None, T, HEAD), lambda h, base=base: (base + h // 2, 0, h % 2))
               for base in (8, 12, 16, 20)]


def _row_ids(shape):
    return lax.broadcasted_iota(jnp.int32, shape, 0)


def _shift_down(a, k, rows):
    return jnp.where(rows >= k, pltpu.roll(a, k, 0), 0.0)


def _shift_up(a, k, rows):
    return jnp.where(rows < T - k, pltpu.roll(a, T - k, 0), 0.0)


def _window_sum(u, gidx, rows, shift):
    s2 = u + shift(u, 1, rows)
    s4 = s2 + shift(s2, 2, rows)
    s8 = s4 + shift(s4, 4, rows)
    s16 = s8 + shift(s8, 8, rows)
    return jnp.where(gidx == 0, s2, jnp.where(gidx == 1, s4, jnp.where(gidx == 2, s8, s16)))


def _pool_count(gidx, rows):
    width = lax.shift_left(jnp.int32(2), gidx)
    return jnp.minimum(rows + 1, width).astype(jnp.float32)


def _pool_fwd(proj, pool_w, pool_scale, token):
    def body(u_ref, pg_ref, w_ref, sc_ref, token_any, y_ref):
        del token_any
        gidx = pl.program_id(0)
        u, pg = u_ref[...], pg_ref[...]
        rows = _row_ids(u.shape)
        d = _window_sum(u, gidx, rows, _shift_down) / _pool_count(gidx, rows) - u
        mixed = _dot(_bf(d), w_ref[...], 1, 0)
        y_ref[...] = _bf(mixed * sc_ref[...] * (pg * _sigmoid(pg)))

    return pl.pallas_call(
        body, name="pool_fwd", grid=(NGROUP,),
        in_specs=[*_POOL_SPECS,
                  pl.BlockSpec((None, GROUP, GROUP), lambda g: (g, 0, 0)),
                  pl.BlockSpec((1, GROUP), lambda g: (0, g)), ANY],
        out_specs=pl.BlockSpec((T, GROUP), lambda g: (0, g)),
        out_shape=jax.ShapeDtypeStruct((T, DMIX), jnp.bfloat16),
        compiler_params=_params(("parallel",)),
    )(proj, proj, pool_w, pool_scale, token)


def _tri(lower):
    r = lax.broadcasted_iota(jnp.int32, (CHUNK, CHUNK), 0)
    c = lax.broadcasted_iota(jnp.int32, (CHUNK, CHUNK), 1)
    return (r >= c) if lower else (r <= c)


def _sum_rows_matrix():
    shape = (CHUNK + 16, CHUNK)
    r, c = lax.broadcasted_iota(jnp.int32, shape, 0), lax.broadcasted_iota(jnp.int32, shape, 1)
    run = jnp.where(c <= r, 1.0, 0.0)
    half = jnp.where(c < CHUNK // 2, 1.0, 0.0)
    return _bf(jnp.where(r < CHUNK, run, jnp.where(r < CHUNK + 8, 1.0, half)))


def _rev_sum_matrix():
    shape = (CHUNK, 2 * CHUNK)
    r, c = lax.broadcasted_iota(jnp.int32, shape, 0), lax.broadcasted_iota(jnp.int32, shape, 1)
    return _bf(jnp.where(c < CHUNK, jnp.where(c >= r, 1.0, 0.0), jnp.where(c - CHUNK < r, 1.0, 0.0)))


def _split3(a):
    a1 = _bf(a)
    r1 = a - a1.astype(jnp.float32)
    a2 = _bf(r1)
    return [a1, a2, _bf(r1 - a2.astype(jnp.float32))]


def _exact_sums(mat, pieces):
    x = jnp.concatenate([s for p in pieces for s in _split3(p)], axis=1)
    r = _dot(mat, x, 1, 0)
    return [r[:, 3 * j * HEAD:(3 * j + 1) * HEAD] + r[:, (3 * j + 1) * HEAD:(3 * j + 2) * HEAD]
            + r[:, (3 * j + 2) * HEAD:(3 * j + 3) * HEAD] for j in range(len(pieces))]


def _gates(qv, fl, lb):
    sq = _sigmoid(qv)
    sg = _sigmoid(fl)
    f = lb + (1.0 - lb) * sg
    return dict(sq=sq, qs=qv * sq, sg=sg, f=f, kk=1.0 - f, g=jnp.log(f))


def _decays(sums):
    big_g = sums[:CHUNK]
    total = sums[CHUNK:CHUNK + 8]
    g_last = jnp.tile(total, (CHUNK // 8, 1))
    g_mid = jnp.tile(sums[CHUNK + 8:], (CHUNK // 8, 1))
    return dict(
        e_q=jnp.exp(big_g),
        e_k=jnp.exp(g_last - big_g),
        e_qm=jnp.exp(jnp.minimum(big_g - g_mid, EXP_CAP)),
        e_km=jnp.exp(jnp.minimum(g_mid - big_g, EXP_CAP)),
        total8=jnp.exp(total),
        state=jnp.exp(jnp.tile(total, (HEAD // 8, 1))))


def _group_rows(gi):
    return [pl.ds(pl.multiple_of((gi * NB + j) * CHUNK, CHUNK), CHUNK) for j in range(NB)]


def _lower_bound(lb_ref):
    return _sigmoid(lb_ref[0:1, :] - lb_ref[1:2, :])


def _hgrn_fwd(proj, lb_logits, rec_g, y_in):
    def body(q_ref, f_ref, i_ref, gate_ref, lb_ref, rg_ref, y_any, y_ref, o_ref, st_ref):
        del y_any
        lb = _lower_bound(lb_ref)
        causal = _tri(True)
        smat = _sum_rows_matrix()

        def group(gi, st):
            rows = _group_rows(gi)
            ts = [_gates(q_ref[r, :], f_ref[r, :], lb) for r in rows]
            ds = [_decays(s) for s in _exact_sums(smat, [t["g"] for t in ts])]
            vs = [_bf(i_ref[r, :]) for r in rows]
            q_m = [_bf(t["qs"] * d["e_qm"]) for t, d in zip(ts, ds)]
            k_m = [_bf(t["kk"] * d["e_km"]) for t, d in zip(ts, ds)]
            q_e = [_bf(t["qs"] * d["e_q"]) for t, d in zip(ts, ds)]
            k_e = [_bf(t["kk"] * d["e_k"]) for t, d in zip(ts, ds)]
            a = [_bf(jnp.where(causal, _dot(q_m[j], k_m[j], 1, 1), 0.0)) for j in range(NB)]
            intra = [_dot(a[j], vs[j], 1, 0) for j in range(NB)]
            upd = [_dot(vs[j], k_e[j], 0, 0) for j in range(NB)]
            for j in range(NB):
                st_ref[gi * NB + j] = st
                o_ref[rows[j], :] = intra[j] + _dot(q_e[j], _bf(st), 1, 1)
                st = st * ds[j]["state"] + upd[j]
            return st

        lax.fori_loop(0, NGRP, group, jnp.zeros((HEAD, HEAD), jnp.float32))
        o = o_ref[...]
        rn = o * lax.rsqrt(jnp.mean(o * o, axis=-1, keepdims=True) + EPS)
        gate = gate_ref[...]
        y_ref[...] = _bf(rn * rg_ref[...] * (gate * _sigmoid(gate)))

    return pl.pallas_call(
        body, name="hgrn_fwd", grid=(NHEAD,),
        in_specs=[*_HEAD_SPECS,
                  pl.BlockSpec((2, HEAD), lambda h: (0, h)),
                  pl.BlockSpec((1, HEAD), lambda h: (0, h)),
                  pl.BlockSpec(memory_space=pl.ANY)],
        out_specs=(pl.BlockSpec((T, HEAD), lambda h: (0, NHEAD + h)),
                   pl.BlockSpec((T, HEAD), lambda h: (0, h)),
                   pl.BlockSpec((None, NCHUNK, HEAD, HEAD), lambda h: (h, 0, 0, 0))),
        out_shape=(jax.ShapeDtypeStruct((T, DMIX), jnp.bfloat16),
                   jax.ShapeDtypeStruct((T, D), jnp.float32),
                   jax.ShapeDtypeStruct((NHEAD, NCHUNK, HEAD, HEAD), jnp.float32)),
        input_output_aliases={6: 0},
        compiler_params=_params(("parallel",)),
    )(proj, proj, proj, proj, lb_logits, rec_g, y_in)


def _out_proj(y, w_out):
    tm = 512

    def body(y_ref, w_ref, o_ref):
        o_ref[...] = _dot(y_ref[...], w_ref[...], 1, 0)

    return pl.pallas_call(
        body, name="out_proj", grid=(T // tm,),
        in_specs=[pl.BlockSpec((tm, DMIX), lambda m: (m, 0)), pl.BlockSpec((DMIX, D), lambda m: (0, 0))],
        out_specs=pl.BlockSpec((tm, D), lambda m: (m, 0)),
        out_shape=jax.ShapeDtypeStruct((T, D), jnp.float32),
        compiler_params=_params(("parallel",)),
    )(y, w_out)


def _loss_head(x, ymm, target, gf):
    rows = 256

    def body(x_ref, y_ref, t_ref, g_ref, dz_ref, dzb_ref, sq_ref, dg_ref):
        z = x_ref[...] + y_ref[...]
        r = lax.rsqrt(jnp.mean(z * z, axis=-1, keepdims=True) + EPS)
        zhat = z * r
        err = zhat * g_ref[...] - t_ref[...]
        dy = err * (1.0 / D)
        gdy = dy * g_ref[...]
        dz = r * (gdy - zhat * jnp.mean(zhat * gdy, axis=-1, keepdims=True))
        dz_ref[...] = dz
        dzb_ref[...] = _bf(dz)
        sq = jnp.sum(err * err, axis=0, keepdims=True)
        dg = jnp.sum(zhat * dy, axis=0, keepdims=True)

        @pl.when(pl.program_id(0) == 0)
        def _():
            sq_ref[...] = sq
            dg_ref[...] = dg

        @pl.when(pl.program_id(0) != 0)
        def _():
            sq_ref[...] += sq
            dg_ref[...] += dg

    tile = pl.BlockSpec((rows, D), lambda i: (i, 0))
    vec = pl.BlockSpec((1, D), lambda i: (0, 0))
    return pl.pallas_call(
        body, name="loss_head", grid=(T // rows,),
        in_specs=[tile, tile, tile, vec], out_specs=(tile, tile, vec, vec),
        out_shape=(jax.ShapeDtypeStruct((T, D), jnp.float32), jax.ShapeDtypeStruct((T, D), jnp.bfloat16),
                   jax.ShapeDtypeStruct((1, D), jnp.float32), jax.ShapeDtypeStruct((1, D), jnp.float32)),
        compiler_params=_params(("arbitrary",)),
    )(x, ymm, target, gf)


def _out_proj_bwd(dzb, w_out, y):
    tn = 512

    def body(dz_ref, w_ref, y_ref, dy_ref, gw_ref, gwb_ref):
        dz = dz_ref[...]
        dy_ref[...] = _dot(dz, w_ref[...], 1, 1)
        gw = _dot(y_ref[...], dz, 0, 0)
        gw_ref[...] = gw
        gwb_ref[...] = _bf(gw)

    return pl.pallas_call(
        body, name="out_proj_bwd", grid=(DMIX // tn,),
        in_specs=[pl.BlockSpec((T, D), lambda n: (0, 0)), pl.BlockSpec((tn, D), lambda n: (n, 0)),
                  pl.BlockSpec((T, tn), lambda n: (0, n))],
        out_specs=(pl.BlockSpec((T, tn), lambda n: (0, n)), pl.BlockSpec((tn, D), lambda n: (n, 0)),
                   pl.BlockSpec((tn, D), lambda n: (n, 0))),
        out_shape=(jax.ShapeDtypeStruct((T, DMIX), jnp.float32), jax.ShapeDtypeStruct((DMIX, D), jnp.float32),
                   jax.ShapeDtypeStruct((DMIX, D), jnp.bfloat16)),
        compiler_params=_params(("parallel",)),
    )(dzb, w_out, y)


def _hgrn_bwd(proj, lb_logits, rec_g, o, states, dymix, dproj_in, token):
    def body(q_ref, f_ref, i_ref, gate_ref, lb_ref, rg_ref, o_ref, st_ref, dy_ref, dp_any, token_any,
             dp_ref, drg_ref, dlb_ref, do_ref):
        del dp_any, token_any
        lb = _lower_bound(lb_ref)
        causal = _tri(True)
        smat, rmat = _sum_rows_matrix(), _rev_sum_matrix()

        o = o_ref[...]
        rs = lax.rsqrt(jnp.mean(o * o, axis=-1, keepdims=True) + EPS)
        rn = o * rs
        gate = gate_ref[...]
        sgate = _sigmoid(gate)
        dyv = dy_ref[...]
        d_r = dyv * (gate * sgate)
        dp_ref[3] = _bf(dyv * (rn * rg_ref[...]) * (sgate * (1.0 + gate * (1.0 - sgate))))
        drg_ref[...] = jnp.sum(d_r * rn, axis=0, keepdims=True)
        drn = d_r * rg_ref[...]
        do_ref[...] = rs * (drn - rn * jnp.mean(rn * drn, axis=-1, keepdims=True))

        def group(i, carry):
            dst, dlb = carry
            gi = NGRP - 1 - i
            rows = _group_rows(gi)
            span = range(NB)
            qvs = [q_ref[r, :] for r in rows]
            ts = [_gates(qv, f_ref[r, :], lb) for qv, r in zip(qvs, rows)]
            ds = [_decays(s) for s in _exact_sums(smat, [t["g"] for t in ts])]
            vs = [_bf(i_ref[r, :]) for r in rows]
            dos = [_bf(do_ref[r, :]) for r in rows]
            sts = [st_ref[gi * NB + j] for j in span]
            qe_f = [t["qs"] * d["e_q"] for t, d in zip(ts, ds)]
            ke_f = [t["kk"] * d["e_k"] for t, d in zip(ts, ds)]
            q_e, k_e = [_bf(a) for a in qe_f], [_bf(a) for a in ke_f]
            q_m = [_bf(t["qs"] * d["e_qm"]) for t, d in zip(ts, ds)]
            k_m = [_bf(t["kk"] * d["e_km"]) for t, d in zip(ts, ds)]
            a = [_bf(jnp.where(causal, _dot(q_m[j], k_m[j], 1, 1), 0.0)) for j in span]
            da = [_bf(jnp.where(causal, _dot(dos[j], vs[j], 1, 1), 0.0)) for j in span]
            dqm = [_dot(da[j], k_m[j], 1, 0) for j in span]
            dkm = [_dot(da[j], q_m[j], 0, 0) for j in span]
            dv_in = [_dot(a[j], dos[j], 0, 0) for j in span]
            dqe = [_dot(dos[j], _bf(sts[j]), 1, 0) for j in span]
            grow = [_dot(dos[j], q_e[j], 0, 0) for j in span]
            dke, carried = [None] * NB, [None] * NB
            for j in reversed(span):
                dst_b = _bf(dst)
                dke[j] = _dot(vs[j], dst_b, 1, 0)
                dp_ref[2, rows[j], :] = _bf(dv_in[j] + _dot(k_e[j], dst_b, 1, 1))
                carried[j] = ds[j]["total8"] * jnp.sum(dst * sts[j], axis=0, keepdims=True)
                dst = dst * ds[j]["state"] + grow[j]
            kdk = [ke_f[j] * dke[j] for j in span]
            pos = [(q_m[j].astype(jnp.float32) * dqm[j] - k_m[j].astype(jnp.float32) * dkm[j]) + qe_f[j] * dqe[j]
                   for j in span]
            dgs = _exact_sums(rmat, [jnp.concatenate([pos[j], kdk[j]], axis=0) for j in span])
            for j in span:
                t, d = ts[j], ds[j]
                dg = dgs[j] + jnp.tile(carried[j], (CHUNK // 8, 1))
                dqs = dqm[j] * d["e_qm"] + dqe[j] * d["e_q"]
                dkk = dkm[j] * d["e_km"] + dke[j] * d["e_k"]
                df = dg / t["f"] - dkk
                dp_ref[1, rows[j], :] = _bf(df * (1.0 - lb) * (t["sg"] * (1.0 - t["sg"])))
                dp_ref[0, rows[j], :] = _bf(dqs * (t["sq"] * (1.0 + qvs[j] * (1.0 - t["sq"]))))
                dlb = dlb + df * (1.0 - t["sg"])
            return dst, dlb

        _, dlb = lax.fori_loop(0, NGRP, group, (jnp.zeros((HEAD, HEAD), jnp.float32),
                                                jnp.zeros((CHUNK, HEAD), jnp.float32)))
        dlb_ref[...] = jnp.sum(dlb, axis=0, keepdims=True)

    vec = pl.BlockSpec((1, HEAD), lambda h: (0, h))
    return pl.pallas_call(
        body, name="hgrn_bwd", grid=(NHEAD,),
        in_specs=[*_HEAD_SPECS,
                  pl.BlockSpec((2, HEAD), lambda h: (0, h)), vec,
                  pl.BlockSpec((T, HEAD), lambda h: (0, h)),
                  pl.BlockSpec((None, NCHUNK, HEAD, HEAD), lambda h: (h, 0, 0, 0)),
                  pl.BlockSpec((T, HEAD), lambda h: (0, NHEAD + h)), ANY, ANY],
        out_specs=(pl.BlockSpec((4, T, HEAD), lambda h: (0, 0, h)), vec, vec),
        out_shape=(jax.ShapeDtypeStruct((NSEG, T, D), jnp.bfloat16),
                   jax.ShapeDtypeStruct((1, D), jnp.float32), jax.ShapeDtypeStruct((1, D), jnp.float32)),
        scratch_shapes=[pltpu.VMEM((T, HEAD), jnp.float32)],
        input_output_aliases={9: 0},
        compiler_params=_params(("parallel",)),
    )(proj, proj, proj, proj, lb_logits, rec_g, o, states, dymix, dproj_in, token)


def _pool_bwd(proj, pool_w, pool_scale, dymix):
    def body(u_ref, pg_ref, w_ref, sc_ref, dy_ref, dp_ref, gw_ref, gs_ref):
        gidx = pl.program_id(0)
        u, pg = u_ref[...], pg_ref[...]
        rows = _row_ids(u.shape)
        count = _pool_count(gidx, rows)
        d = _bf(_window_sum(u, gidx, rows, _shift_down) / count - u)
        mixed = _dot(d, w_ref[...], 1, 0)
        spg = _sigmoid(pg)
        dyv = dy_ref[...]
        d_p = dyv * (pg * spg)
        dp_ref[1] = _bf(dyv * (mixed * sc_ref[...]) * (spg * (1.0 + pg * (1.0 - spg))))
        gs_ref[...] = jnp.sum(d_p * mixed, axis=0, keepdims=True)
        dmixed = _bf(d_p * sc_ref[...])
        gw_ref[...] = _dot(d, dmixed, 0, 0)
        dd = _dot(dmixed, w_ref[...], 1, 1)
        dp_ref[0] = _bf(_window_sum(dd / count, gidx, rows, _shift_up) - dd)

    return pl.pallas_call(
        body, name="pool_bwd", grid=(NGROUP,),
        in_specs=[*_POOL_SPECS,
                  pl.BlockSpec((None, GROUP, GROUP), lambda g: (g, 0, 0)),
                  pl.BlockSpec((1, GROUP), lambda g: (0, g)),
                  pl.BlockSpec((T, GROUP), lambda g: (0, g))],
        out_specs=(pl.BlockSpec((2, T, GROUP), lambda g: (2, 0, g)),
                   pl.BlockSpec((None, GROUP, GROUP), lambda g: (g, 0, 0)),
                   pl.BlockSpec((1, GROUP), lambda g: (0, g))),
        out_shape=(jax.ShapeDtypeStruct((NSEG, T, D), jnp.bfloat16),
                   jax.ShapeDtypeStruct((NGROUP, GROUP, GROUP), jnp.float32),
                   jax.ShapeDtypeStruct((1, D), jnp.float32)),
        compiler_params=_params(("parallel",)),
    )(proj, proj, pool_w, pool_scale, dymix)


def _proj_bwd_w(h, dproj):
    def body(h_ref, dp_ref, gw_ref, gwb_ref):
        gw = _dot(h_ref[...], dp_ref[...], 0, 0)
        for i in range(4):
            gw_ref[i] = gw[:, i * TILE:(i + 1) * TILE]
            gwb_ref[i] = _bf(gw[:, i * TILE:(i + 1) * TILE])

    out = pl.BlockSpec((4, D, TILE), lambda s: (_seg_tiles(s), 0, 0))
    return pl.pallas_call(
        body, name="proj_bwd_w", grid=(NSEG,),
        in_specs=[pl.BlockSpec((T, D), lambda s: (0, 0)), pl.BlockSpec((None, T, D), lambda s: (s, 0, 0))],
        out_specs=(out, out),
        out_shape=(jax.ShapeDtypeStruct((NTILE, D, TILE), jnp.float32),
                   jax.ShapeDtypeStruct((NTILE, D, TILE), jnp.bfloat16)),
        compiler_params=_params(("parallel",)),
    )(h, dproj)


def _proj_bwd_x(dproj, w_t, token):
    tm = 1024

    def body(dp_ref, w_ref, token_any, o_ref, wcat):
        del token_any
        for i in range(4):
            wcat[:, i * TILE:(i + 1) * TILE] = w_ref[i]
        r = _dot(dp_ref[...], wcat[...], 1, 1)

        @pl.when(pl.program_id(1) == 0)
        def _():
            o_ref[...] = r

        @pl.when(pl.program_id(1) != 0)
        def _():
            o_ref[...] += r

    return pl.pallas_call(
        body, name="proj_bwd_x", grid=(T // tm, NSEG),
        in_specs=[pl.BlockSpec((None, tm, D), lambda m, s: (s, m, 0)),
                  pl.BlockSpec((4, D, TILE), lambda m, s: (_seg_tiles(s), 0, 0)), ANY],
        out_specs=pl.BlockSpec((tm, D), lambda m, s: (m, 0)),
        out_shape=jax.ShapeDtypeStruct((T, D), jnp.float32),
        scratch_shapes=[pltpu.VMEM((D, D), jnp.bfloat16)],
        compiler_params=_params(("parallel", "arbitrary")),
    )(dproj, w_t, token)


def _norm1_bwd(x, g1, dh, dz):
    rows = 256

    def body(x_ref, g_ref, dh_ref, dz_ref, dx_ref, dg_ref):
        xv = x_ref[...]
        r = lax.rsqrt(jnp.mean(xv * xv, axis=-1, keepdims=True) + EPS)
        xhat = xv * r
        dhv = dh_ref[...]
        gdh = dhv * g_ref[...]
        dx_ref[...] = dz_ref[...] + r * (gdh - xhat * jnp.mean(xhat * gdh, axis=-1, keepdims=True))
        dg = jnp.sum(xhat * dhv, axis=0, keepdims=True)

        @pl.when(pl.program_id(0) == 0)
        def _():
            dg_ref[...] = dg

        @pl.when(pl.program_id(0) != 0)
        def _():
            dg_ref[...] += dg

    tile = pl.BlockSpec((rows, D), lambda i: (i, 0))
    vec = pl.BlockSpec((1, D), lambda i: (0, 0))
    return pl.pallas_call(
        body, name="norm1_bwd", grid=(T // rows,),
        in_specs=[tile, vec, tile, tile], out_specs=(tile, vec),
        out_shape=(jax.ShapeDtypeStruct((T, D), jnp.float32), jax.ShapeDtypeStruct((1, D), jnp.float32)),
        compiler_params=_params(("arbitrary",)),
    )(x, g1, dh, dz)


def _adamw(w, g, m, v):
    m_new = ADAM_B1 * m + (1.0 - ADAM_B1) * g
    v_new = ADAM_B2 * v + (1.0 - ADAM_B2) * (g * g)
    delta = -ADAM_LR * ((m_new / BC1) / (jnp.sqrt(v_new / BC2) + ADAM_EPS) + ADAM_WD * w)
    return delta, m_new, v_new


def _reduce_adam(name, place, parts, w, m, v, grid, w_spec):
    n = len(parts)

    def body(place_ref, *refs):
        del place_ref
        w_ref, m_ref, v_ref, g_ref, d_ref, mo_ref, vo_ref = refs[n:]
        g = None
        for ref, (_, _, stacked) in zip(refs[:n], parts):
            terms = [ref[r] for r in range(ref.shape[0])] if stacked else [ref[...]]
            for t in terms:
                g = t.astype(jnp.float32) if g is None else g + t.astype(jnp.float32)
        delta, m_new, v_new = _adamw(w_ref[...], g, m_ref[...], v_ref[...])
        g_ref[...] = g
        d_ref[...] = delta
        mo_ref[...] = m_new
        vo_ref[...] = v_new

    shape = jax.ShapeDtypeStruct(w.shape, jnp.float32)
    return pl.pallas_call(
        body, name=name,
        grid_spec=pltpu.PrefetchScalarGridSpec(
            num_scalar_prefetch=1, grid=grid,
            in_specs=[spec for _, spec, _ in parts] + [w_spec] * 3, out_specs=(w_spec,) * 4),
        out_shape=(shape,) * 4,
        compiler_params=_params(("parallel",)),
    )(place, *[a for a, _, _ in parts], w, m, v)


def _small_adam(parts, w, m, v):
    def body(p_ref, w_ref, m_ref, v_ref, g_ref, d_ref, mo_ref, vo_ref):
        g = p_ref[0]
        for s in range(1, NDEV):
            g = g + p_ref[s]
        wv = w_ref[...]
        rows = _row_ids(wv.shape)
        other = jnp.where(rows == 2, pltpu.roll(wv, 7, 0), jnp.where(rows == 3, pltpu.roll(wv, 1, 0), 0.0))
        lbv = _sigmoid(wv - other)
        sign = jnp.where(rows == 2, 1.0, -1.0)
        g = jnp.where((rows == 2) | (rows == 3), sign * g * lbv * (1.0 - lbv), g)
        delta, m_new, v_new = _adamw(wv, g, m_ref[...], v_ref[...])
        g_ref[...] = g
        d_ref[...] = delta
        mo_ref[...] = m_new
        vo_ref[...] = v_new

    shape = jax.ShapeDtypeStruct((8, D), jnp.float32)
    return pl.pallas_call(body, name="small_adam", out_shape=(shape,) * 4)(parts, w, m, v)


def _rows8(*vecs):
    rows = [a.reshape(-1, D) for a in vecs]
    n = sum(r.shape[0] for r in rows)
    return jnp.concatenate(rows + [jnp.zeros((8 - n, D), jnp.float32)], axis=0)


def kernel(x, norm1_g, w_in, pool_w, pool_scale, lb_logits, rec_norm_g, w_out, final_norm_g, loss_target, m_norm1_g, m_w_in, m_pool_w, m_pool_scale, m_lb_logits, m_rec_norm_g, m_w_out, m_final_norm_g, v_norm1_g, v_w_in, v_pool_w, v_pool_scale, v_lb_logits, v_rec_norm_g, v_w_out, v_final_norm_g):
    xs = x[0]
    target = loss_target[0]
    ix, iy, ic = lax.axis_index("x"), lax.axis_index("y"), lax.axis_index("c")
    place = jnp.stack([4 * ix + 2 * iy + ic, 2 * ix + iy, ic]).astype(jnp.int32)
    gf = final_norm_g.reshape(1, D)

    w_t3 = _bf(w_in[0]).reshape(D, 3, TILE).transpose(1, 0, 2)
    w_out_b = _bf(w_out[0])
    h, w_t, w_out_g, pool_g, proj = _gather_proj(xs, norm1_g, w_t3, w_out_b, _bf(pool_w[0]))
    pool_full = pool_g.transpose(1, 0, 2, 3).reshape(NGROUP, GROUP, GROUP)
    wout = [w_out_b, w_out_g]
    wout_send, wout_recv, wout, wout_token = _split_start("gather_wout_start", wout, NDEV - 1, _plan_wout)

    y = _pool_fwd(proj, pool_full, pool_scale, wout_token)
    y, o, states = _hgrn_fwd(proj, lb_logits, rec_norm_g, y)
    _, w_out_g = _split_wait("gather_wout_wait", wout, wout_send, wout_recv, _plan_wout, o)
    w_out_full = w_out_g.reshape(DMIX, D)
    ymm = _out_proj(y, w_out_full)
    dz, dzb, sq, dgf = _loss_head(xs, ymm, target, gf)

    dymix, gwout_f, gwout_b = _out_proj_bwd(dzb, w_out_full, y)
    dproj, gpool, dscale = _pool_bwd(proj, pool_full, pool_scale, dymix)

    blk_out = (NDEV, DMIX // NDEV, D)
    blk_pool = (NDEV, NGROUP, GROUP // NDEV, GROUP)
    gpool_s = gpool.reshape(NGROUP, NDEV, GROUP // NDEV, GROUP).transpose(1, 0, 2, 3)
    rest = [gwout_b.reshape(blk_out), gpool_s,
            lax.empty((NDEV - 1,) + blk_out[1:], jnp.bfloat16), lax.empty((NDEV - 1,) + blk_pool[1:], jnp.float32)]
    rest_send, rest_recv, rest, rest_token = _split_start("scatter_rest_start", rest, 2 * (NDEV - 1), _plan_rest)

    dproj, drecg, dlb = _hgrn_bwd(proj, lb_logits, rec_norm_g, o, states, dymix, dproj, rest_token)
    gw_f, gw_b = _proj_bwd_w(h, dproj)

    by_owner = (4, 2, 3, D, TILE)
    gw_f5 = gw_f.reshape(by_owner)
    from_sibling = _pair_exchange(gw_b.reshape(by_owner))
    win = [_pair_sum(place, gw_f5, from_sibling), lax.empty((3, 3, D, TILE), jnp.bfloat16)]
    win_send, win_recv, win, win_token = _split_start("scatter_win_start", win, 3, _plan_in)

    dh = _proj_bwd_x(dproj, w_t, win_token)
    grad_x, dg1 = _norm1_bwd(xs, norm1_g, dh, dz)

    _, gpool_own, r_out, r_pool = _split_wait("scatter_rest_wait", rest, rest_send, rest_recv, _plan_rest, grad_x)
    g_wout, d_wout, m_wout, v_wout = _reduce_adam(
        "adam_w_out", place,
        [(gwout_f.reshape(blk_out), pl.BlockSpec((None,) + blk_out[1:], lambda i, pr: (pr[0], 0, 0)), False),
         (r_out, pl.BlockSpec((NDEV - 1,) + blk_out[1:], lambda i, pr: (0, 0, 0)), True)],
        w_out, m_w_out, v_w_out, (1,), pl.BlockSpec((None,) + blk_out[1:], lambda i, pr: (0, 0, 0)))
    g_pool, d_pool, m_pool, v_pool = _reduce_adam(
        "adam_pool_w", place,
        [(gpool_own, pl.BlockSpec((None,) + blk_pool[1:], lambda i, pr: (pr[0], 0, 0, 0)), False),
         (r_pool, pl.BlockSpec((NDEV - 1,) + blk_pool[1:], lambda i, pr: (0, 0, 0, 0)), True)],
        pool_w, m_pool_w, v_pool_w, (1,), pl.BlockSpec((None,) + blk_pool[1:], lambda i, pr: (0, 0, 0, 0)))

    r_small = _gather_small(_rows8(dg1, dscale, dlb, dlb, drecg, dgf, sq), d_wout, d_pool)
    loss = jnp.sum(r_small[:, 6, :]) * (0.5 / D)
    g_s, d_s, m_s, v_s = _small_adam(
        r_small,
        _rows8(norm1_g, pool_scale, lb_logits, rec_norm_g, final_norm_g),
        _rows8(m_norm1_g, m_pool_scale, m_lb_logits, m_rec_norm_g, m_final_norm_g),
        _rows8(v_norm1_g, v_pool_scale, v_lb_logits, v_rec_norm_g, v_final_norm_g))

    _, r_in = _split_wait("scatter_win_wait", win, win_send, win_recv, _plan_in, v_s)
    g_win, d_win, m_win, v_win = _reduce_adam(
        "adam_w_in", place,
        [(gw_f5, pl.BlockSpec((None, None, None, D, TILE), lambda i, pr: (pr[1], pr[2], i, 0, 0)), False),
         (from_sibling, pl.BlockSpec((None, None, None, D, TILE), lambda i, pr: (pr[1], 0, i, 0, 0)), False),
         (r_in, pl.BlockSpec((3, None, D, TILE), lambda i, pr: (0, i, 0, 0)), True)],
        w_in, m_w_in, v_w_in, (3,), pl.BlockSpec((None, D, TILE), lambda i, pr: (0, 0, i)))

    def small_outs(a):
        return a[0:1], a[1:2], a[2:4], a[4:5], a[5]

    def outs(small_a, win, pool, wout):
        n1, ps, lbl, rg, fg = small_outs(small_a)
        return n1, win, pool, ps, lbl, rg, wout, fg

    return (loss, grad_x[None],
            *outs(g_s, g_win, g_pool, g_wout), *outs(d_s, d_win, d_pool, d_wout),
            *outs(m_s, m_win, m_pool, m_wout), *outs(v_s, v_win, v_pool, v_wout))
```

```python
import functools

import jax
import jax.numpy as jnp
from jax import lax
from jax.experimental import pallas as pl
from jax.experimental.pallas import tpu as pltpu

T = 2048
D = 1024
NSEG = 6
NTILE = 24
TILE = 256
DMIX = 2048
NDEV = 8
HEAD = 128
NHEAD = 8
CHUNK = 64
NCHUNK = T // CHUNK
NB = 8
NGRP = NCHUNK // NB
NGROUP = 4
GROUP = 256
EPS = 1e-6
EXP_CAP = 80.0
MESH = pl.DeviceIdType.MESH
AXES = ("x", "y", "c")
ANY = pl.BlockSpec(memory_space=pl.ANY)
HBM = pl.BlockSpec(memory_space=pltpu.HBM)
SEM = pl.BlockSpec(memory_space=pltpu.SEMAPHORE)
EFFECT = pltpu.SideEffectType.DATAFLOW_SIDE_EFFECTING

ADAM_LR = 0.001
ADAM_B1 = 0.9
ADAM_B2 = 0.999
ADAM_EPS = 1e-08
ADAM_WD = 0.01
ADAM_STEP = 10
BC1 = 1.0 - ADAM_B1 ** ADAM_STEP
BC2 = 1.0 - ADAM_B2 ** ADAM_STEP

MIB = 1 << 20


def _params(sem=None, vmem_mib=48):
    return pltpu.CompilerParams(dimension_semantics=sem, vmem_limit_bytes=vmem_mib * MIB)


def _sigmoid(v):
    return 1.0 / (1.0 + jnp.exp(-v))


def _dot(a, b, ca, cb, precision=None):
    return lax.dot_general(a, b, (((ca,), (cb,)), ((), ())), precision=precision,
                           preferred_element_type=jnp.float32)


def _bf(v):
    return v.astype(jnp.bfloat16)


def _place():
    x, y, c = lax.axis_index("x"), lax.axis_index("y"), lax.axis_index("c")
    return x, y, c, 4 * x + 2 * y + c


def _peer(x, y, c, r):
    return (x ^ ((r >> 2) & 1), y ^ ((r >> 1) & 1), c ^ (r & 1))


def _gather_proj(x, g1, w_t3, w_out_b, pool_b):
    def body(x_ref, g_ref, win_ref, wout_ref, pool_ref, h_ref, wt_o, wout_o, pool_o, proj_o,
             wv, stage, send_sems, recv_sems, loc_sems, out_sems):
        px, py, c, my_idx = _place()
        me, sibling = (px, py, c), (px, py, 1 - c)
        chips = [(1 - px, py), (px, 1 - py), (1 - px, 1 - py)]
        srcs = (win_ref, pool_ref)

        def index(bx, by, bc):
            return 4 * bx + 2 * by + bc

        def slot(w, block):
            return wv.at[pl.ds(3 * index(*block), 3)] if w == 0 else pool_o.at[index(*block)]

        def copy(k, w, block, to, src=None):
            return pltpu.make_async_remote_copy(
                src_ref=slot(w, block) if src is None else src, dst_ref=slot(w, block),
                send_sem=send_sems.at[2 * k + w], recv_sem=recv_sems.at[2 * k + w],
                device_id=to, device_id_type=MESH)

        locs = [pltpu.make_async_copy(win_ref, slot(0, me), loc_sems.at[0]),
                pltpu.make_async_copy(pool_ref, slot(1, me), loc_sems.at[1]),
                pltpu.make_async_copy(wout_ref, wout_o.at[my_idx], loc_sems.at[2])]
        for cp in locs:
            cp.start()
        first = []
        for w in (0, 1):
            first += [copy(1 + j, w, me, (*chip, c), src=srcs[w]) for j, chip in enumerate(chips)]
            first.append(copy(0, w, me, sibling, src=srcs[w]))
        for cp in first:
            cp.start()

        xv = x_ref[...]
        h_ref[...] = _bf(xv * lax.rsqrt(jnp.mean(xv * xv, axis=-1, keepdims=True) + EPS) * g_ref[...])

        def out_copy(p, j):
            return pltpu.make_async_copy(stage.at[p], proj_o.at[j], out_sems.at[p])

        def project(nth, block):
            base = 3 * index(*block)

            def tile(p, carry):
                if nth > 0:
                    out_copy(p, base + p).wait()
                stage[p] = _dot(h_ref[...], wv[base + p], 1, 0)
                out_copy(p, base + p).start()
                return carry

            lax.fori_loop(0, 3, tile, 0)

        locs[0].wait()
        project(0, me)
        copy(0, 0, sibling, me).wait_recv()
        project(1, sibling)
        passed = []

        def arrived(nth, w, j):
            chip = chips[j]
            copy(1 + j, w, (*chip, c), me).wait_recv()
            passed.append(copy(4 + j, w, (*chip, c), sibling))
            passed[-1].start()
            if w == 0:
                project(nth, (*chip, c))

        def handed(nth, j):
            copy(4 + j, 0, (*chips[j], 1 - c), me).wait_recv()
            project(nth, (*chips[j], 1 - c))

        arrived(2, 0, 0)
        arrived(3, 0, 1)
        handed(4, 0)
        handed(5, 1)
        arrived(6, 0, 2)
        handed(7, 2)
        for j in range(3):
            arrived(8, 1, j)
        copy(0, 1, sibling, me).wait_recv()
        for j, chip in enumerate(chips):
            copy(4 + j, 1, (*chip, 1 - c), me).wait_recv()
        keep = pltpu.make_async_copy(wv, wt_o, loc_sems.at[3])
        keep.start()
        for p in range(3):
            out_copy(p, p).wait()
        for cp in first + passed:
            cp.wait_send()
        keep.wait()
        for cp in locs[1:]:
            cp.wait()

    vmem = pl.BlockSpec(memory_space=pltpu.VMEM)
    return pl.pallas_call(
        body, name="gather_proj",
        out_shape=(jax.ShapeDtypeStruct((T, D), jnp.bfloat16),
                   jax.ShapeDtypeStruct((NTILE, D, TILE), jnp.bfloat16),
                   jax.ShapeDtypeStruct((NDEV, DMIX // NDEV, D), jnp.bfloat16),
                   jax.ShapeDtypeStruct((NDEV, NGROUP, GROUP // NDEV, GROUP), jnp.bfloat16),
                   jax.ShapeDtypeStruct((NTILE, T, TILE), jnp.float32)),
        in_specs=[vmem, vmem, ANY, ANY, ANY], out_specs=(vmem, ANY, ANY, ANY, ANY),
        scratch_shapes=[pltpu.VMEM((NTILE, D, TILE), jnp.bfloat16), pltpu.VMEM((3, T, TILE), jnp.float32),
                        pltpu.SemaphoreType.DMA((14,)), pltpu.SemaphoreType.DMA((14,)),
                        pltpu.SemaphoreType.DMA((4,)), pltpu.SemaphoreType.DMA((3,))],
        compiler_params=_params(vmem_mib=56),
    )(x, g1, w_t3, w_out_b, pool_b)


def _split_start(name, arrays, n_copies, plan):
    k = len(arrays)

    def body(*refs):
        send_sems, recv_sems, token = refs[k], refs[k + 1], refs[-1]
        for i, (src, dst, to) in enumerate(plan(refs[:k])):
            pltpu.make_async_remote_copy(src_ref=src, dst_ref=dst, send_sem=send_sems.at[i],
                                         recv_sem=recv_sems.at[i], device_id=to, device_id_type=MESH).start()
        token[...] = jnp.zeros_like(token)

    out = pl.pallas_call(
        body, name=name,
        out_shape=(pltpu.SemaphoreType.DMA((n_copies,)), pltpu.SemaphoreType.DMA((n_copies,)),
                   *[pltpu.HBM(a.shape, a.dtype) for a in arrays], jax.ShapeDtypeStruct((8, 128), jnp.float32)),
        in_specs=[HBM] * k, out_specs=(SEM, SEM, *[HBM] * k, pl.BlockSpec(memory_space=pltpu.VMEM)),
        input_output_aliases={i: 2 + i for i in range(k)},
        compiler_params=pltpu.CompilerParams(has_side_effects=EFFECT),
    )(*[pltpu.with_memory_space_constraint(a, pltpu.HBM) for a in arrays])
    return out[0], out[1], out[2:2 + k], out[-1]


def _split_wait(name, arrays, send_sems, recv_sems, plan, after):
    k = len(arrays)

    def body(*refs):
        sends, recvs = refs[k], refs[k + 1]
        for i, (src, dst, to) in enumerate(plan(refs[:k])):
            cp = pltpu.make_async_remote_copy(src_ref=src, dst_ref=dst, send_sem=sends.at[i], recv_sem=recvs.at[i],
                                              device_id=to, device_id_type=MESH)
            cp.wait_send()
            cp.wait_recv()

    return pl.pallas_call(
        body, name=name,
        out_shape=tuple(pltpu.HBM(a.shape, a.dtype) for a in arrays),
        in_specs=[HBM] * k + [SEM, SEM, ANY], out_specs=(HBM,) * k,
        input_output_aliases={i: i for i in range(k)},
        compiler_params=pltpu.CompilerParams(has_side_effects=EFFECT),
    )(*arrays, send_sems, recv_sems, after)


def _plan_wout(refs):
    src, land = refs
    x, y, c, me = _place()
    return [(src, land.at[me], _peer(x, y, c, r)) for r in range(1, NDEV)]


def _plan_rest(refs):
    gob, gpf, r_out, r_pool = refs
    x, y, c, me = _place()
    plan = []
    for r in range(1, NDEV):
        plan.append((gob.at[me ^ r], r_out.at[r - 1], _peer(x, y, c, r)))
        plan.append((gpf.at[me ^ r], r_pool.at[r - 1], _peer(x, y, c, r)))
    return plan


def _plan_in(refs):
    sum_b, r_in = refs
    x, y, c, _ = _place()
    plan = []
    for j, (dx, dy) in enumerate(((1, 0), (0, 1), (1, 1))):
        px, py = x ^ dx, y ^ dy
        plan.append((sum_b.at[2 * px + py], r_in.at[j], (px, py, c)))
    return plan


def _gather_small(small, *after):
    def body(sm, *refs):
        r_small, send_sems, recv_sems, loc_sem = refs[len(after):]
        x, y, c, me = _place()
        loc = pltpu.make_async_copy(sm, r_small.at[me], loc_sem)
        loc.start()

        def copy(r, src_idx):
            return pltpu.make_async_remote_copy(
                src_ref=sm, dst_ref=r_small.at[src_idx], send_sem=send_sems.at[r - 1], recv_sem=recv_sems.at[r - 1],
                device_id=_peer(x, y, c, r), device_id_type=MESH)

        sends = [copy(r, me) for r in range(1, NDEV)]
        for cp in sends:
            cp.start()
        for r in range(1, NDEV):
            copy(r, me ^ r).wait_recv()
        for cp in sends:
            cp.wait_send()
        loc.wait()

    return pl.pallas_call(
        body, name="gather_small",
        out_shape=jax.ShapeDtypeStruct((NDEV, 8, D), jnp.float32),
        in_specs=[ANY] * (1 + len(after)), out_specs=ANY,
        scratch_shapes=[pltpu.SemaphoreType.DMA((NDEV - 1,)), pltpu.SemaphoreType.DMA((NDEV - 1,)),
                        pltpu.SemaphoreType.DMA],
    )(small, *after)


def _seg_tiles(s):
    return (s + 2) % NSEG


_POOL_SPECS = [pl.BlockSpec((None, T, GROUP), lambda g, base=base: (base + g, 0, 0)) for base in (0, 4)]
_HEAD_SPECS = [pl.BlockSpec((None, T, HEAD), lambda h, base=base: (base + h // 2, 0, h % 2))
               for base in (8, 12, 16, 20)]


def _row_ids(shape):
    return lax.broadcasted_iota(jnp.int32, shape, 0)


def _shift_down(a, k, rows):
    return jnp.where(rows >= k, pltpu.roll(a, k, 0), 0.0)


def _shift_up(a, k, rows):
    return jnp.where(rows < T - k, pltpu.roll(a, T - k, 0), 0.0)


def _window_sum(u, gidx, rows, shift):
    s2 = u + shift(u, 1, rows)
    s4 = s2 + shift(s2, 2, rows)
    s8 = s4 + shift(s4, 4, rows)
    s16 = s8 + shift(s8, 8, rows)
    return jnp.where(gidx == 0, s2, jnp.where(gidx == 1, s4, jnp.where(gidx == 2, s8, s16)))


def _pool_count(gidx, rows):
    width = lax.shift_left(jnp.int32(2), gidx)
    return jnp.minimum(rows + 1, width).astype(jnp.float32)


def _pool_fwd(proj, pool_w, pool_scale, token):
    def body(u_ref, pg_ref, w_ref, sc_ref, token_any, y_ref):
        del token_any
        gidx = pl.program_id(0)
        u, pg = u_ref[...], pg_ref[...]
        rows = _row_ids(u.shape)
        d = _window_sum(u, gidx, rows, _shift_down) / _pool_count(gidx, rows) - u
        mixed = _dot(_bf(d), w_ref[...], 1, 0)
        y_ref[...] = _bf(mixed * sc_ref[...] * (pg * _sigmoid(pg)))

    return pl.pallas_call(
        body, name="pool_fwd", grid=(NGROUP,),
        in_specs=[*_POOL_SPECS,
                  pl.BlockSpec((None, GROUP, GROUP), lambda g: (g, 0, 0)),
                  pl.BlockSpec((1, GROUP), lambda g: (0, g)), ANY],
        out_specs=pl.BlockSpec((T, GROUP), lambda g: (0, g)),
        out_shape=jax.ShapeDtypeStruct((T, DMIX), jnp.bfloat16),
        compiler_params=_params(("parallel",)),
    )(proj, proj, pool_w, pool_scale, token)


def _tri(lower):
    r = lax.broadcasted_iota(jnp.int32, (CHUNK, CHUNK), 0)
    c = lax.broadcasted_iota(jnp.int32, (CHUNK, CHUNK), 1)
    return (r >= c) if lower else (r <= c)


def _sum_rows_matrix():
    shape = (CHUNK + 16, CHUNK)
    r, c = lax.broadcasted_iota(jnp.int32, shape, 0), lax.broadcasted_iota(jnp.int32, shape, 1)
    run = jnp.where(c <= r, 1.0, 0.0)
    half = jnp.where(c < CHUNK // 2, 1.0, 0.0)
    return _bf(jnp.where(r < CHUNK, run, jnp.where(r < CHUNK + 8, 1.0, half)))


def _rev_sum_matrix():
    shape = (CHUNK, 2 * CHUNK)
    r, c = lax.broadcasted_iota(jnp.int32, shape, 0), lax.broadcasted_iota(jnp.int32, shape, 1)
    return _bf(jnp.where(c < CHUNK, jnp.where(c >= r, 1.0, 0.0), jnp.where(c - CHUNK < r, 1.0, 0.0)))


def _split3(a):
    a1 = _bf(a)
    r1 = a - a1.astype(jnp.float32)
    a2 = _bf(r1)
    return [a1, a2, _bf(r1 - a2.astype(jnp.float32))]


def _exact_sums(mat, pieces):
    x = jnp.concatenate([s for p in pieces for s in _split3(p)], axis=1)
    r = _dot(mat, x, 1, 0)
    return [r[:, 3 * j * HEAD:(3 * j + 1) * HEAD] + r[:, (3 * j + 1) * HEAD:(3 * j + 2) * HEAD]
            + r[:, (3 * j + 2) * HEAD:(3 * j + 3) * HEAD] for j in range(len(pieces))]


def _gates(qv, fl, lb):
    sq = _sigmoid(qv)
    sg = _sigmoid(fl)
    f = lb + (1.0 - lb) * sg
    return dict(sq=sq, qs=qv * sq, sg=sg, f=f, kk=1.0 - f, g=jnp.log(f))


def _decays(sums):
    big_g = sums[:CHUNK]
    total = sums[CHUNK:CHUNK + 8]
    g_last = jnp.tile(total, (CHUNK // 8, 1))
    g_mid = jnp.tile(sums[CHUNK + 8:], (CHUNK // 8, 1))
    return dict(
        e_q=jnp.exp(big_g),
        e_k=jnp.exp(g_last - big_g),
        e_qm=jnp.exp(jnp.minimum(big_g - g_mid, EXP_CAP)),
        e_km=jnp.exp(jnp.minimum(g_mid - big_g, EXP_CAP)),
        total8=jnp.exp(total),
        state=jnp.exp(jnp.tile(total, (HEAD // 8, 1))))


def _group_rows(gi):
    return [pl.ds(pl.multiple_of((gi * NB + j) * CHUNK, CHUNK), CHUNK) for j in range(NB)]


def _lower_bound(lb_ref):
    return _sigmoid(lb_ref[0:1, :] - lb_ref[1:2, :])


def _hgrn_fwd(proj, lb_logits, rec_g, y_in):
    def body(q_ref, f_ref, i_ref, gate_ref, lb_ref, rg_ref, y_any, y_ref, o_ref, st_ref):
        del y_any
        lb = _lower_bound(lb_ref)
        causal = _tri(True)
        smat = _sum_rows_matrix()

        def group(gi, st):
            rows = _group_rows(gi)
            ts = [_gates(q_ref[r, :], f_ref[r, :], lb) for r in rows]
            ds = [_decays(s) for s in _exact_sums(smat, [t["g"] for t in ts])]
            vs = [_bf(i_ref[r, :]) for r in rows]
            q_m = [_bf(t["qs"] * d["e_qm"]) for t, d in zip(ts, ds)]
            k_m = [_bf(t["kk"] * d["e_km"]) for t, d in zip(ts, ds)]
            q_e = [_bf(t["qs"] * d["e_q"]) for t, d in zip(ts, ds)]
            k_e = [_bf(t["kk"] * d["e_k"]) for t, d in zip(ts, ds)]
            a = [_bf(jnp.where(causal, _dot(q_m[j], k_m[j], 1, 1), 0.0)) for j in range(NB)]
            intra = [_dot(a[j], vs[j], 1, 0) for j in range(NB)]
            upd = [_dot(vs[j], k_e[j], 0, 0) for j in range(NB)]
            for j in range(NB):
                st_ref[gi * NB + j] = st
                o_ref[rows[j], :] = intra[j] + _dot(q_e[j], _bf(st), 1, 1)
                st = st * ds[j]["state"] + upd[j]
            return st

        lax.fori_loop(0, NGRP, group, jnp.zeros((HEAD, HEAD), jnp.float32))
        o = o_ref[...]
        rn = o * lax.rsqrt(jnp.mean(o * o, axis=-1, keepdims=True) + EPS)
        gate = gate_ref[...]
        y_ref[...] = _bf(rn * rg_ref[...] * (gate * _sigmoid(gate)))

    return pl.pallas_call(
        body, name="hgrn_fwd", grid=(NHEAD,),
        in_specs=[*_HEAD_SPECS,
                  pl.BlockSpec((2, HEAD), lambda h: (0, h)),
                  pl.BlockSpec((1, HEAD), lambda h: (0, h)),
                  pl.BlockSpec(memory_space=pl.ANY)],
        out_specs=(pl.BlockSpec((T, HEAD), lambda h: (0, NHEAD + h)),
                   pl.BlockSpec((T, HEAD), lambda h: (0, h)),
                   pl.BlockSpec((None, NCHUNK, HEAD, HEAD), lambda h: (h, 0, 0, 0))),
        out_shape=(jax.ShapeDtypeStruct((T, DMIX), jnp.bfloat16),
                   jax.ShapeDtypeStruct((T, D), jnp.float32),
                   jax.ShapeDtypeStruct((NHEAD, NCHUNK, HEAD, HEAD), jnp.float32)),
        input_output_aliases={6: 0},
        compiler_params=_params(("parallel",)),
    )(proj, proj, proj, proj, lb_logits, rec_g, y_in)


def _out_proj(y, w_out):
    tm = 512

    def body(y_ref, w_ref, o_ref):
        o_ref[...] = _dot(y_ref[...], w_ref[...], 1, 0)

    return pl.pallas_call(
        body, name="out_proj", grid=(T // tm,),
        in_specs=[pl.BlockSpec((tm, DMIX), lambda m: (m, 0)), pl.BlockSpec((DMIX, D), lambda m: (0, 0))],
        out_specs=pl.BlockSpec((tm, D), lambda m: (m, 0)),
        out_shape=jax.ShapeDtypeStruct((T, D), jnp.float32),
        compiler_params=_params(("parallel",)),
    )(y, w_out)


def _loss_head(x, ymm, target, gf):
    rows = 256

    def body(x_ref, y_ref, t_ref, g_ref, dz_ref, dzb_ref, sq_ref, dg_ref):
        z = x_ref[...] + y_ref[...]
        r = lax.rsqrt(jnp.mean(z * z, axis=-1, keepdims=True) + EPS)
        zhat = z * r
        err = zhat * g_ref[...] - t_ref[...]
        dy = err * (1.0 / D)
        gdy = dy * g_ref[...]
        dz = r * (gdy - zhat * jnp.mean(zhat * gdy, axis=-1, keepdims=True))
        dz_ref[...] = dz
        dzb_ref[...] = _bf(dz)
        sq = jnp.sum(err * err, axis=0, keepdims=True)
        dg = jnp.sum(zhat * dy, axis=0, keepdims=True)

        @pl.when(pl.program_id(0) == 0)
        def _():
            sq_ref[...] = sq
            dg_ref[...] = dg

        @pl.when(pl.program_id(0) != 0)
        def _():
            sq_ref[...] += sq
            dg_ref[...] += dg

    tile = pl.BlockSpec((rows, D), lambda i: (i, 0))
    vec = pl.BlockSpec((1, D), lambda i: (0, 0))
    return pl.pallas_call(
        body, name="loss_head", grid=(T // rows,),
        in_specs=[tile, tile, tile, vec], out_specs=(tile, tile, vec, vec),
        out_shape=(jax.ShapeDtypeStruct((T, D), jnp.float32), jax.ShapeDtypeStruct((T, D), jnp.bfloat16),
                   jax.ShapeDtypeStruct((1, D), jnp.float32), jax.ShapeDtypeStruct((1, D), jnp.float32)),
        compiler_params=_params(("arbitrary",)),
    )(x, ymm, target, gf)


def _out_proj_bwd(dzb, w_out, y):
    tn = 512

    def body(dz_ref, w_ref, y_ref, dy_ref, gw_ref, gwb_ref):
        dz = dz_ref[...]
        dy_ref[...] = _dot(dz, w_ref[...], 1, 1)
        gw = _dot(y_ref[...], dz, 0, 0)
        gw_ref[...] = gw
        gwb_ref[...] = _bf(gw)

    return pl.pallas_call(
        body, name="out_proj_bwd", grid=(DMIX // tn,),
        in_specs=[pl.BlockSpec((T, D), lambda n: (0, 0)), pl.BlockSpec((tn, D), lambda n: (n, 0)),
                  pl.BlockSpec((T, tn), lambda n: (0, n))],
        out_specs=(pl.BlockSpec((T, tn), lambda n: (0, n)), pl.BlockSpec((tn, D), lambda n: (n, 0)),
                   pl.BlockSpec((tn, D), lambda n: (n, 0))),
        out_shape=(jax.ShapeDtypeStruct((T, DMIX), jnp.float32), jax.ShapeDtypeStruct((DMIX, D), jnp.float32),
                   jax.ShapeDtypeStruct((DMIX, D), jnp.bfloat16)),
        compiler_params=_params(("parallel",)),
    )(dzb, w_out, y)


def _hgrn_bwd(proj, lb_logits, rec_g, o, states, dymix, dproj_in, token):
    def body(q_ref, f_ref, i_ref, gate_ref, lb_ref, rg_ref, o_ref, st_ref, dy_ref, dp_any, token_any,
             dp_ref, drg_ref, dlb_ref, do_ref):
        del dp_any, token_any
        lb = _lower_bound(lb_ref)
        causal = _tri(True)
        smat, rmat = _sum_rows_matrix(), _rev_sum_matrix()

        o = o_ref[...]
        rs = lax.rsqrt(jnp.mean(o * o, axis=-1, keepdims=True) + EPS)
        rn = o * rs
        gate = gate_ref[...]
        sgate = _sigmoid(gate)
        dyv = dy_ref[...]
        d_r = dyv * (gate * sgate)
        dp_ref[3] = _bf(dyv * (rn * rg_ref[...]) * (sgate * (1.0 + gate * (1.0 - sgate))))
        drg_ref[...] = jnp.sum(d_r * rn, axis=0, keepdims=True)
        drn = d_r * rg_ref[...]
        do_ref[...] = rs * (drn - rn * jnp.mean(rn * drn, axis=-1, keepdims=True))

        def group(i, carry):
            dst, dlb = carry
            gi = NGRP - 1 - i
            rows = _group_rows(gi)
            span = range(NB)
            qvs = [q_ref[r, :] for r in rows]
            ts = [_gates(qv, f_ref[r, :], lb) for qv, r in zip(qvs, rows)]
            ds = [_decays(s) for s in _exact_sums(smat, [t["g"] for t in ts])]
            vs = [_bf(i_ref[r, :]) for r in rows]
            dos = [_bf(do_ref[r, :]) for r in rows]
            sts = [st_ref[gi * NB + j] for j in span]
            qe_f = [t["qs"] * d["e_q"] for t, d in zip(ts, ds)]
            ke_f = [t["kk"] * d["e_k"] for t, d in zip(ts, ds)]
            q_e, k_e = [_bf(a) for a in qe_f], [_bf(a) for a in ke_f]
            q_m = [_bf(t["qs"] * d["e_qm"]) for t, d in zip(ts, ds)]
            k_m = [_bf(t["kk"] * d["e_km"]) for t, d in zip(ts, ds)]
            a = [_bf(jnp.where(causal, _dot(q_m[j], k_m[j], 1, 1), 0.0)) for j in span]
            da = [_bf(jnp.where(causal, _dot(dos[j], vs[j], 1, 1), 0.0)) for j in span]
            dqm = [_dot(da[j], k_m[j], 1, 0) for j in span]
            dkm = [_dot(da[j], q_m[j], 0, 0) for j in span]
            dv_in = [_dot(a[j], dos[j], 0, 0) for j in span]
            dqe = [_dot(dos[j], _bf(sts[j]), 1, 0) for j in span]
            grow = [_dot(dos[j], q_e[j], 0, 0) for j in span]
            dke, carried = [None] * NB, [None] * NB
            for j in reversed(span):
                dst_b = _bf(dst)
                dke[j] = _dot(vs[j], dst_b, 1, 0)
                dp_ref[2, rows[j], :] = _bf(dv_in[j] + _dot(k_e[j], dst_b, 1, 1))
                carried[j] = ds[j]["total8"] * jnp.sum(dst * sts[j], axis=0, keepdims=True)
                dst = dst * ds[j]["state"] + grow[j]
            kdk = [ke_f[j] * dke[j] for j in span]
            pos = [(q_m[j].astype(jnp.float32) * dqm[j] - k_m[j].astype(jnp.float32) * dkm[j]) + qe_f[j] * dqe[j]
                   for j in span]
            dgs = _exact_sums(rmat, [jnp.concatenate([pos[j], kdk[j]], axis=0) for j in span])
            for j in span:
                t, d = ts[j], ds[j]
                dg = dgs[j] + jnp.tile(carried[j], (CHUNK // 8, 1))
                dqs = dqm[j] * d["e_qm"] + dqe[j] * d["e_q"]
                dkk = dkm[j] * d["e_km"] + dke[j] * d["e_k"]
                df = dg / t["f"] - dkk
                dp_ref[1, rows[j], :] = _bf(df * (1.0 - lb) * (t["sg"] * (1.0 - t["sg"])))
                dp_ref[0, rows[j], :] = _bf(dqs * (t["sq"] * (1.0 + qvs[j] * (1.0 - t["sq"]))))
                dlb = dlb + df * (1.0 - t["sg"])
            return dst, dlb

        _, dlb = lax.fori_loop(0, NGRP, group, (jnp.zeros((HEAD, HEAD), jnp.float32),
                                                jnp.zeros((CHUNK, HEAD), jnp.float32)))
        dlb_ref[...] = jnp.sum(dlb, axis=0, keepdims=True)

    vec = pl.BlockSpec((1, HEAD), lambda h: (0, h))
    return pl.pallas_call(
        body, name="hgrn_bwd", grid=(NHEAD,),
        in_specs=[*_HEAD_SPECS,
                  pl.BlockSpec((2, HEAD), lambda h: (0, h)), vec,
                  pl.BlockSpec((T, HEAD), lambda h: (0, h)),
                  pl.BlockSpec((None, NCHUNK, HEAD, HEAD), lambda h: (h, 0, 0, 0)),
                  pl.BlockSpec((T, HEAD), lambda h: (0, NHEAD + h)), ANY, ANY],
        out_specs=(pl.BlockSpec((4, T, HEAD), lambda h: (0, 0, h)), vec, vec),
        out_shape=(jax.ShapeDtypeStruct((NSEG, T, D), jnp.bfloat16),
                   jax.ShapeDtypeStruct((1, D), jnp.float32), jax.ShapeDtypeStruct((1, D), jnp.float32)),
        scratch_shapes=[pltpu.VMEM((T, HEAD), jnp.float32)],
        input_output_aliases={9: 0},
        compiler_params=_params(("parallel",)),
    )(proj, proj, proj, proj, lb_logits, rec_g, o, states, dymix, dproj_in, token)


def _pool_bwd(proj, pool_w, pool_scale, dymix):
    def body(u_ref, pg_ref, w_ref, sc_ref, dy_ref, dp_ref, gw_ref, gs_ref):
        gidx = pl.program_id(0)
        u, pg = u_ref[...], pg_ref[...]
        rows = _row_ids(u.shape)
        count = _pool_count(gidx, rows)
        d = _bf(_window_sum(u, gidx, rows, _shift_down) / count - u)
        mixed = _dot(d, w_ref[...], 1, 0)
        spg = _sigmoid(pg)
        dyv = dy_ref[...]
        d_p = dyv * (pg * spg)
        dp_ref[1] = _bf(dyv * (mixed * sc_ref[...]) * (spg * (1.0 + pg * (1.0 - spg))))
        gs_ref[...] = jnp.sum(d_p * mixed, axis=0, keepdims=True)
        dmixed = _bf(d_p * sc_ref[...])
        gw_ref[...] = _dot(d, dmixed, 0, 0)
        dd = _dot(dmixed, w_ref[...], 1, 1)
        dp_ref[0] = _bf(_window_sum(dd / count, gidx, rows, _shift_up) - dd)

    return pl.pallas_call(
        body, name="pool_bwd", grid=(NGROUP,),
        in_specs=[*_POOL_SPECS,
                  pl.BlockSpec((None, GROUP, GROUP), lambda g: (g, 0, 0)),
                  pl.BlockSpec((1, GROUP), lambda g: (0, g)),
                  pl.BlockSpec((T, GROUP), lambda g: (0, g))],
        out_specs=(pl.BlockSpec((2, T, GROUP), lambda g: (2, 0, g)),
                   pl.BlockSpec((None, GROUP, GROUP), lambda g: (g, 0, 0)),
                   pl.BlockSpec((1, GROUP), lambda g: (0, g))),
        out_shape=(jax.ShapeDtypeStruct((NSEG, T, D), jnp.bfloat16),
                   jax.ShapeDtypeStruct((NGROUP, GROUP, GROUP), jnp.float32),
                   jax.ShapeDtypeStruct((1, D), jnp.float32)),
        compiler_params=_params(("parallel",)),
    )(proj, proj, pool_w, pool_scale, dymix)


def _proj_bwd_w(place, h, dproj):
    half = NTILE // 2

    def owner_chip(i, pr):
        return jnp.where(i < half, i // 3, (pr[1] + 1 + (i - half) // 3) % 4)

    def tile_of(i, pr):
        side = jnp.where(i < half, 1 - pr[2], pr[2])
        return 6 * owner_chip(i, pr) + 3 * side + i % 3

    def dproj_block(i, pr):
        j = tile_of(i, pr)
        return ((j // 4 + 4) % NSEG, 0, j % 4)

    def mine(i):
        return jnp.maximum(i, half)

    def body(place_ref, h_ref, dp_ref, sum_ref, own_ref, sendbuf, recvbuf, send_sems, recv_sems):
        i = pl.program_id(0)
        px, py, c, _ = _place()
        gw = _dot(h_ref[...], dp_ref[...], 0, 0)

        def to_sibling(slot):
            return pltpu.make_async_remote_copy(
                src_ref=sendbuf.at[slot], dst_ref=recvbuf.at[slot], send_sem=send_sems.at[slot],
                recv_sem=recv_sems.at[slot], device_id=(px, py, 1 - c), device_id_type=MESH)

        @pl.when(i < half)
        def _():
            sendbuf[i] = _bf(gw)
            to_sibling(i).start()

        @pl.when(i >= half)
        def _():
            slot = 3 * owner_chip(i, place_ref) + i % 3
            to_sibling(slot).wait_recv()
            total = gw + recvbuf[slot].astype(jnp.float32)
            sum_ref[...] = _bf(total)
            own_ref[...] = total

        @pl.when(i == NTILE - 1)
        def _():
            for slot in range(half):
                to_sibling(slot).wait_send()

    return pl.pallas_call(
        body, name="proj_bwd_w",
        grid_spec=pltpu.PrefetchScalarGridSpec(
            num_scalar_prefetch=1, grid=(NTILE,),
            in_specs=[pl.BlockSpec((T, D), lambda i, pr: (0, 0)),
                      pl.BlockSpec((None, T, TILE), lambda i, pr: dproj_block(i, pr))],
            out_specs=(pl.BlockSpec((None, None, D, TILE), lambda i, pr: (owner_chip(mine(i), pr), mine(i) % 3, 0, 0)),
                       pl.BlockSpec((None, D, TILE), lambda i, pr: (jnp.where(i < NTILE - 3, 0, i % 3), 0, 0))),
            scratch_shapes=[pltpu.VMEM((half, D, TILE), jnp.bfloat16), pltpu.VMEM((half, D, TILE), jnp.bfloat16),
                            pltpu.SemaphoreType.DMA((half,)), pltpu.SemaphoreType.DMA((half,))]),
        out_shape=(jax.ShapeDtypeStruct((4, 3, D, TILE), jnp.bfloat16),
                   jax.ShapeDtypeStruct((3, D, TILE), jnp.float32)),
        compiler_params=_params(("arbitrary",)),
    )(place, h, dproj)


def _proj_bwd_x(dproj, w_t, token):
    tm = 1024

    def body(dp_ref, w_ref, token_any, o_ref, wcat):
        del token_any
        for i in range(4):
            wcat[:, i * TILE:(i + 1) * TILE] = w_ref[i]
        r = _dot(dp_ref[...], wcat[...], 1, 1)

        @pl.when(pl.program_id(1) == 0)
        def _():
            o_ref[...] = r

        @pl.when(pl.program_id(1) != 0)
        def _():
            o_ref[...] += r

    return pl.pallas_call(
        body, name="proj_bwd_x", grid=(T // tm, NSEG),
        in_specs=[pl.BlockSpec((None, tm, D), lambda m, s: (s, m, 0)),
                  pl.BlockSpec((4, D, TILE), lambda m, s: (_seg_tiles(s), 0, 0)), ANY],
        out_specs=pl.BlockSpec((tm, D), lambda m, s: (m, 0)),
        out_shape=jax.ShapeDtypeStruct((T, D), jnp.float32),
        scratch_shapes=[pltpu.VMEM((D, D), jnp.bfloat16)],
        compiler_params=_params(("parallel", "arbitrary")),
    )(dproj, w_t, token)


def _norm1_bwd(x, g1, dh, dz):
    rows = 256

    def body(x_ref, g_ref, dh_ref, dz_ref, dx_ref, dg_ref):
        xv = x_ref[...]
        r = lax.rsqrt(jnp.mean(xv * xv, axis=-1, keepdims=True) + EPS)
        xhat = xv * r
        dhv = dh_ref[...]
        gdh = dhv * g_ref[...]
        dx_ref[...] = dz_ref[...] + r * (gdh - xhat * jnp.mean(xhat * gdh, axis=-1, keepdims=True))
        dg = jnp.sum(xhat * dhv, axis=0, keepdims=True)

        @pl.when(pl.program_id(0) == 0)
        def _():
            dg_ref[...] = dg

        @pl.when(pl.program_id(0) != 0)
        def _():
            dg_ref[...] += dg

    tile = pl.BlockSpec((rows, D), lambda i: (i, 0))
    vec = pl.BlockSpec((1, D), lambda i: (0, 0))
    return pl.pallas_call(
        body, name="norm1_bwd", grid=(T // rows,),
        in_specs=[tile, vec, tile, tile], out_specs=(tile, vec),
        out_shape=(jax.ShapeDtypeStruct((T, D), jnp.float32), jax.ShapeDtypeStruct((1, D), jnp.float32)),
        compiler_params=_params(("arbitrary",)),
    )(x, g1, dh, dz)


def _adamw(w, g, m, v):
    m_new = ADAM_B1 * m + (1.0 - ADAM_B1) * g
    v_new = ADAM_B2 * v + (1.0 - ADAM_B2) * (g * g)
    delta = -ADAM_LR * ((m_new / BC1) / (jnp.sqrt(v_new / BC2) + ADAM_EPS) + ADAM_WD * w)
    return delta, m_new, v_new


def _reduce_adam(name, place, parts, w, m, v, grid, w_spec):
    n = len(parts)

    def body(place_ref, *refs):
        del place_ref
        w_ref, m_ref, v_ref, g_ref, d_ref, mo_ref, vo_ref = refs[n:]
        g = None
        for ref, (_, _, stacked) in zip(refs[:n], parts):
            terms = [ref[r] for r in range(ref.shape[0])] if stacked else [ref[...]]
            for t in terms:
                g = t.astype(jnp.float32) if g is None else g + t.astype(jnp.float32)
        delta, m_new, v_new = _adamw(w_ref[...], g, m_ref[...], v_ref[...])
        g_ref[...] = g
        d_ref[...] = delta
        mo_ref[...] = m_new
        vo_ref[...] = v_new

    shape = jax.ShapeDtypeStruct(w.shape, jnp.float32)
    return pl.pallas_call(
        body, name=name,
        grid_spec=pltpu.PrefetchScalarGridSpec(
            num_scalar_prefetch=1, grid=grid,
            in_specs=[spec for _, spec, _ in parts] + [w_spec] * 3, out_specs=(w_spec,) * 4),
        out_shape=(shape,) * 4,
        compiler_params=_params(("parallel",)),
    )(place, *[a for a, _, _ in parts], w, m, v)


def _small_adam(parts, w, m, v):
    def body(p_ref, w_ref, m_ref, v_ref, g_ref, d_ref, mo_ref, vo_ref):
        g = p_ref[0]
        for s in range(1, NDEV):
            g = g + p_ref[s]
        wv = w_ref[...]
        rows = _row_ids(wv.shape)
        other = jnp.where(rows == 2, pltpu.roll(wv, 7, 0), jnp.where(rows == 3, pltpu.roll(wv, 1, 0), 0.0))
        lbv = _sigmoid(wv - other)
        sign = jnp.where(rows == 2, 1.0, -1.0)
        g = jnp.where((rows == 2) | (rows == 3), sign * g * lbv * (1.0 - lbv), g)
        delta, m_new, v_new = _adamw(wv, g, m_ref[...], v_ref[...])
        g_ref[...] = g
        d_ref[...] = delta
        mo_ref[...] = m_new
        vo_ref[...] = v_new

    shape = jax.ShapeDtypeStruct((8, D), jnp.float32)
    return pl.pallas_call(body, name="small_adam", out_shape=(shape,) * 4)(parts, w, m, v)


def _rows8(*vecs):
    rows = [a.reshape(-1, D) for a in vecs]
    n = sum(r.shape[0] for r in rows)
    return jnp.concatenate(rows + [jnp.zeros((8 - n, D), jnp.float32)], axis=0)


def kernel(x, norm1_g, w_in, pool_w, pool_scale, lb_logits, rec_norm_g, w_out, final_norm_g, loss_target, m_norm1_g, m_w_in, m_pool_w, m_pool_scale, m_lb_logits, m_rec_norm_g, m_w_out, m_final_norm_g, v_norm1_g, v_w_in, v_pool_w, v_pool_scale, v_lb_logits, v_rec_norm_g, v_w_out, v_final_norm_g):
    xs = x[0]
    target = loss_target[0]
    ix, iy, ic = lax.axis_index("x"), lax.axis_index("y"), lax.axis_index("c")
    place = jnp.stack([4 * ix + 2 * iy + ic, 2 * ix + iy, ic]).astype(jnp.int32)
    gf = final_norm_g.reshape(1, D)

    w_t3 = _bf(w_in[0]).reshape(D, 3, TILE).transpose(1, 0, 2)
    w_out_b = _bf(w_out[0])
    h, w_t, w_out_g, pool_g, proj = _gather_proj(xs, norm1_g, w_t3, w_out_b, _bf(pool_w[0]))
    pool_full = pool_g.transpose(1, 0, 2, 3).reshape(NGROUP, GROUP, GROUP)
    wout = [w_out_b, w_out_g]
    wout_send, wout_recv, wout, wout_token = _split_start("gather_wout_start", wout, NDEV - 1, _plan_wout)

    y = _pool_fwd(proj, pool_full, pool_scale, wout_token)
    y, o, states = _hgrn_fwd(proj, lb_logits, rec_norm_g, y)
    _, w_out_g = _split_wait("gather_wout_wait", wout, wout_send, wout_recv, _plan_wout, o)
    w_out_full = w_out_g.reshape(DMIX, D)
    ymm = _out_proj(y, w_out_full)
    dz, dzb, sq, dgf = _loss_head(xs, ymm, target, gf)

    dymix, gwout_f, gwout_b = _out_proj_bwd(dzb, w_out_full, y)
    dproj, gpool, dscale = _pool_bwd(proj, pool_full, pool_scale, dymix)

    blk_out = (NDEV, DMIX // NDEV, D)
    blk_pool = (NDEV, NGROUP, GROUP // NDEV, GROUP)
    gpool_s = gpool.reshape(NGROUP, NDEV, GROUP // NDEV, GROUP).transpose(1, 0, 2, 3)
    rest = [gwout_b.reshape(blk_out), gpool_s,
            lax.empty((NDEV - 1,) + blk_out[1:], jnp.bfloat16), lax.empty((NDEV - 1,) + blk_pool[1:], jnp.float32)]
    rest_send, rest_recv, rest, rest_token = _split_start("scatter_rest_start", rest, 2 * (NDEV - 1), _plan_rest)

    dproj, drecg, dlb = _hgrn_bwd(proj, lb_logits, rec_norm_g, o, states, dymix, dproj, rest_token)
    chip_sums, own_sum = _proj_bwd_w(place, h, dproj)
    win = [chip_sums, lax.empty((3, 3, D, TILE), jnp.bfloat16)]
    win_send, win_recv, win, win_token = _split_start("scatter_win_start", win, 3, _plan_in)

    dh = _proj_bwd_x(dproj, w_t, win_token)
    grad_x, dg1 = _norm1_bwd(xs, norm1_g, dh, dz)

    _, gpool_own, r_out, r_pool = _split_wait("scatter_rest_wait", rest, rest_send, rest_recv, _plan_rest, grad_x)
    g_wout, d_wout, m_wout, v_wout = _reduce_adam(
        "adam_w_out", place,
        [(gwout_f.reshape(blk_out), pl.BlockSpec((None,) + blk_out[1:], lambda i, pr: (pr[0], 0, 0)), False),
         (r_out, pl.BlockSpec((NDEV - 1,) + blk_out[1:], lambda i, pr: (0, 0, 0)), True)],
        w_out, m_w_out, v_w_out, (1,), pl.BlockSpec((None,) + blk_out[1:], lambda i, pr: (0, 0, 0)))
    g_pool, d_pool, m_pool, v_pool = _reduce_adam(
        "adam_pool_w", place,
        [(gpool_own, pl.BlockSpec((None,) + blk_pool[1:], lambda i, pr: (pr[0], 0, 0, 0)), False),
         (r_pool, pl.BlockSpec((NDEV - 1,) + blk_pool[1:], lambda i, pr: (0, 0, 0, 0)), True)],
        pool_w, m_pool_w, v_pool_w, (1,), pl.BlockSpec((None,) + blk_pool[1:], lambda i, pr: (0, 0, 0, 0)))

    r_small = _gather_small(_rows8(dg1, dscale, dlb, dlb, drecg, dgf, sq), d_wout, d_pool)
    loss = jnp.sum(r_small[:, 6, :]) * (0.5 / D)
    g_s, d_s, m_s, v_s = _small_adam(
        r_small,
        _rows8(norm1_g, pool_scale, lb_logits, rec_norm_g, final_norm_g),
        _rows8(m_norm1_g, m_pool_scale, m_lb_logits, m_rec_norm_g, m_final_norm_g),
        _rows8(v_norm1_g, v_pool_scale, v_lb_logits, v_rec_norm_g, v_final_norm_g))

    _, r_in = _split_wait("scatter_win_wait", win, win_send, win_recv, _plan_in, v_s)
    g_win, d_win, m_win, v_win = _reduce_adam(
        "adam_w_in", place,
        [(own_sum, pl.BlockSpec((None, D, TILE), lambda i, pr: (i, 0, 0)), False),
         (r_in, pl.BlockSpec((3, None, D, TILE), lambda i, pr: (0, i, 0, 0)), True)],
        w_in, m_w_in, v_w_in, (3,), pl.BlockSpec((None, D, TILE), lambda i, pr: (0, 0, i)))

    def small_outs(a):
        return a[0:1], a[1:2], a[2:4], a[4:5], a[5]

    def outs(small_a, win, pool, wout):
        n1, ps, lbl, rg, fg = small_outs(small_a)
        return n1, win, pool, ps, lbl, rg, wout, fg

    return (loss, grad_x[None],
            *outs(g_s, g_win, g_pool, g_wout), *outs(d_s, d_win, d_pool, d_wout),
            *outs(m_s, m_win, m_pool, m_wout), *outs(v_s, v_win, v_pool, v_wout))
```

```python
import functools

import jax
import jax.numpy as jnp
from jax import lax
from jax.experimental import pallas as pl
from jax.experimental.pallas import tpu as pltpu

T = 2048
D = 1024
NSEG = 6
NTILE = 24
TILE = 256
DMIX = 2048
NDEV = 8
HEAD = 128
NHEAD = 8
CHUNK = 64
NCHUNK = T // CHUNK
NB = 8
NGRP = NCHUNK // NB
NGROUP = 4
GROUP = 256
EPS = 1e-6
EXP_CAP = 80.0
MESH = pl.DeviceIdType.MESH
AXES = ("x", "y", "c")
ANY = pl.BlockSpec(memory_space=pl.ANY)
HBM = pl.BlockSpec(memory_space=pltpu.HBM)
SEM = pl.BlockSpec(memory_space=pltpu.SEMAPHORE)
EFFECT = pltpu.SideEffectType.DATAFLOW_SIDE_EFFECTING

ADAM_LR = 0.001
ADAM_B1 = 0.9
ADAM_B2 = 0.999
ADAM_EPS = 1e-08
ADAM_WD = 0.01
ADAM_STEP = 10
BC1 = 1.0 - ADAM_B1 ** ADAM_STEP
BC2 = 1.0 - ADAM_B2 ** ADAM_STEP

MIB = 1 << 20


def _params(sem=None, vmem_mib=48):
    return pltpu.CompilerParams(dimension_semantics=sem, vmem_limit_bytes=vmem_mib * MIB)


def _sigmoid(v):
    return 1.0 / (1.0 + jnp.exp(-v))


def _dot(a, b, ca, cb, precision=None):
    return lax.dot_general(a, b, (((ca,), (cb,)), ((), ())), precision=precision,
                           preferred_element_type=jnp.float32)


def _bf(v):
    return v.astype(jnp.bfloat16)


def _in_hbm(a):
    return pltpu.with_memory_space_constraint(a, pltpu.HBM)


def _place():
    x, y, c = lax.axis_index("x"), lax.axis_index("y"), lax.axis_index("c")
    return x, y, c, 4 * x + 2 * y + c


def _peer(x, y, c, r):
    return (x ^ ((r >> 2) & 1), y ^ ((r >> 1) & 1), c ^ (r & 1))


def _gather_proj(x, g1, w_in, w_out, pool_w):
    def body(x_ref, g_ref, win_ref, wout_ref, pool_ref, h_o, wt_o, woutb_o, wout_o, pool_o, proj_o,
             hv, wv, wob, pb, stage, send_sems, recv_sems, loc_sems, out_sems):
        px, py, c, my_idx = _place()
        me, sibling = (px, py, c), (px, py, 1 - c)
        chips = [(1 - px, py), (px, 1 - py), (1 - px, 1 - py)]
        for p in range(3):
            wv[3 * my_idx + p] = _bf(win_ref[0, :, p * TILE:(p + 1) * TILE])
        pb[...] = _bf(pool_ref[0])
        wob[...] = _bf(wout_ref[0])

        def index(bx, by, bc):
            return 4 * bx + 2 * by + bc

        def slot(w, block):
            return wv.at[pl.ds(3 * index(*block), 3)] if w == 0 else pool_o.at[index(*block)]

        def copy(k, w, block, to, src=None):
            return pltpu.make_async_remote_copy(
                src_ref=slot(w, block) if src is None else src, dst_ref=slot(w, block),
                send_sem=send_sems.at[2 * k + w], recv_sem=recv_sems.at[2 * k + w],
                device_id=to, device_id_type=MESH)

        srcs = (slot(0, me), pb)
        first = []
        for w in (0, 1):
            first += [copy(1 + j, w, me, (*chip, c), src=srcs[w]) for j, chip in enumerate(chips)]
            first.append(copy(0, w, me, sibling, src=srcs[w]))
        for cp in first:
            cp.start()
        locs = [pltpu.make_async_copy(pb, slot(1, me), loc_sems.at[0]),
                pltpu.make_async_copy(wob, wout_o.at[my_idx], loc_sems.at[1]),
                pltpu.make_async_copy(wob, woutb_o, loc_sems.at[2])]
        for cp in locs:
            cp.start()

        xv = x_ref[...]
        hv[...] = _bf(xv * lax.rsqrt(jnp.mean(xv * xv, axis=-1, keepdims=True) + EPS) * g_ref[...])
        locs.append(pltpu.make_async_copy(hv, h_o, loc_sems.at[3]))
        locs[-1].start()

        def out_copy(p, j):
            return pltpu.make_async_copy(stage.at[p], proj_o.at[j], out_sems.at[p])

        def project(nth, block):
            base = 3 * index(*block)

            def tile(p, carry):
                if nth > 0:
                    out_copy(p, base + p).wait()
                stage[p] = _dot(hv[...], wv[base + p], 1, 0)
                out_copy(p, base + p).start()
                return carry

            lax.fori_loop(0, 3, tile, 0)

        project(0, me)
        copy(0, 0, sibling, me).wait_recv()
        project(1, sibling)
        passed = []

        def arrived(nth, w, j):
            chip = chips[j]
            copy(1 + j, w, (*chip, c), me).wait_recv()
            passed.append(copy(4 + j, w, (*chip, c), sibling))
            passed[-1].start()
            if w == 0:
                project(nth, (*chip, c))

        def handed(nth, j):
            copy(4 + j, 0, (*chips[j], 1 - c), me).wait_recv()
            project(nth, (*chips[j], 1 - c))

        arrived(2, 0, 0)
        arrived(3, 0, 1)
        handed(4, 0)
        handed(5, 1)
        arrived(6, 0, 2)
        handed(7, 2)
        for j in range(3):
            arrived(8, 1, j)
        copy(0, 1, sibling, me).wait_recv()
        for j, chip in enumerate(chips):
            copy(4 + j, 1, (*chip, 1 - c), me).wait_recv()
        keep = pltpu.make_async_copy(wv, wt_o, loc_sems.at[4])
        keep.start()
        for p in range(3):
            out_copy(p, p).wait()
        for cp in first + passed:
            cp.wait_send()
        keep.wait()
        for cp in locs:
            cp.wait()

    vmem = pl.BlockSpec(memory_space=pltpu.VMEM)
    bf16 = jnp.bfloat16
    return pl.pallas_call(
        body, name="gather_proj",
        out_shape=(pltpu.HBM((T, D), bf16), pltpu.HBM((NTILE, D, TILE), bf16),
                   pltpu.HBM((DMIX // NDEV, D), bf16), pltpu.HBM((NDEV, DMIX // NDEV, D), bf16),
                   pltpu.HBM((NDEV, NGROUP, GROUP // NDEV, GROUP), bf16), pltpu.HBM((NTILE, T, TILE), jnp.float32)),
        in_specs=[vmem] * 5, out_specs=(ANY,) * 6,
        scratch_shapes=[pltpu.VMEM((T, D), bf16), pltpu.VMEM((NTILE, D, TILE), bf16),
                        pltpu.VMEM((DMIX // NDEV, D), bf16), pltpu.VMEM((NGROUP, GROUP // NDEV, GROUP), bf16),
                        pltpu.VMEM((3, T, TILE), jnp.float32),
                        pltpu.SemaphoreType.DMA((14,)), pltpu.SemaphoreType.DMA((14,)),
                        pltpu.SemaphoreType.DMA((5,)), pltpu.SemaphoreType.DMA((3,))],
        compiler_params=_params(vmem_mib=56),
    )(x, g1, w_in, w_out, pool_w)


def _split_start(name, arrays, n_copies, plan):
    k = len(arrays)

    def body(*refs):
        send_sems, recv_sems, token = refs[k], refs[k + 1], refs[-1]
        for i, (src, dst, to) in enumerate(plan(refs[:k])):
            pltpu.make_async_remote_copy(src_ref=src, dst_ref=dst, send_sem=send_sems.at[i],
                                         recv_sem=recv_sems.at[i], device_id=to, device_id_type=MESH).start()
        token[...] = jnp.zeros_like(token)

    out = pl.pallas_call(
        body, name=name,
        out_shape=(pltpu.SemaphoreType.DMA((n_copies,)), pltpu.SemaphoreType.DMA((n_copies,)),
                   *[pltpu.HBM(a.shape, a.dtype) for a in arrays], jax.ShapeDtypeStruct((8, 128), jnp.float32)),
        in_specs=[HBM] * k, out_specs=(SEM, SEM, *[HBM] * k, pl.BlockSpec(memory_space=pltpu.VMEM)),
        input_output_aliases={i: 2 + i for i in range(k)},
        compiler_params=pltpu.CompilerParams(has_side_effects=EFFECT),
    )(*[pltpu.with_memory_space_constraint(a, pltpu.HBM) for a in arrays])
    return out[0], out[1], out[2:2 + k], out[-1]


def _split_wait(name, arrays, send_sems, recv_sems, plan, after):
    k = len(arrays)

    def body(*refs):
        sends, recvs = refs[k], refs[k + 1]
        for i, (src, dst, to) in enumerate(plan(refs[:k])):
            cp = pltpu.make_async_remote_copy(src_ref=src, dst_ref=dst, send_sem=sends.at[i], recv_sem=recvs.at[i],
                                              device_id=to, device_id_type=MESH)
            cp.wait_send()
            cp.wait_recv()

    return pl.pallas_call(
        body, name=name,
        out_shape=tuple(pltpu.HBM(a.shape, a.dtype) for a in arrays),
        in_specs=[HBM] * k + [SEM, SEM, ANY], out_specs=(HBM,) * k,
        input_output_aliases={i: i for i in range(k)},
        compiler_params=pltpu.CompilerParams(has_side_effects=EFFECT),
    )(*arrays, send_sems, recv_sems, after)


def _plan_wout(refs):
    src, land = refs
    x, y, c, me = _place()
    return [(src, land.at[me], _peer(x, y, c, r)) for r in range(1, NDEV)]


def _plan_rest(refs):
    gob, gpf, r_out, r_pool = refs
    x, y, c, me = _place()
    plan = []
    for r in range(1, NDEV):
        plan.append((gob.at[me ^ r], r_out.at[r - 1], _peer(x, y, c, r)))
        plan.append((gpf.at[me ^ r], r_pool.at[r - 1], _peer(x, y, c, r)))
    return plan


def _plan_in(refs):
    sum_b, r_in = refs
    x, y, c, _ = _place()
    plan = []
    for j, (dx, dy) in enumerate(((1, 0), (0, 1), (1, 1))):
        px, py = x ^ dx, y ^ dy
        plan.append((sum_b.at[2 * px + py], r_in.at[j], (px, py, c)))
    return plan


def _gather_small(small, *after):
    def body(sm, *refs):
        r_small, send_sems, recv_sems, loc_sem = refs[len(after):]
        x, y, c, me = _place()
        loc = pltpu.make_async_copy(sm, r_small.at[me], loc_sem)
        loc.start()

        def copy(r, src_idx):
            return pltpu.make_async_remote_copy(
                src_ref=sm, dst_ref=r_small.at[src_idx], send_sem=send_sems.at[r - 1], recv_sem=recv_sems.at[r - 1],
                device_id=_peer(x, y, c, r), device_id_type=MESH)

        sends = [copy(r, me) for r in range(1, NDEV)]
        for cp in sends:
            cp.start()
        for r in range(1, NDEV):
            copy(r, me ^ r).wait_recv()
        for cp in sends:
            cp.wait_send()
        loc.wait()

    return pl.pallas_call(
        body, name="gather_small",
        out_shape=jax.ShapeDtypeStruct((NDEV, 8, D), jnp.float32),
        in_specs=[ANY] * (1 + len(after)), out_specs=ANY,
        scratch_shapes=[pltpu.SemaphoreType.DMA((NDEV - 1,)), pltpu.SemaphoreType.DMA((NDEV - 1,)),
                        pltpu.SemaphoreType.DMA],
    )(small, *after)


def _seg_tiles(s):
    return (s + 2) % NSEG


_POOL_SPECS = [pl.BlockSpec((None, T, GROUP), lambda g, base=base: (base + g, 0, 0)) for base in (0, 4)]
_HEAD_SPECS = [pl.BlockSpec((None, T, HEAD), lambda h, base=base: (base + h // 2, 0, h % 2))
               for base in (8, 12, 16, 20)]


def _row_ids(shape):
    return lax.broadcasted_iota(jnp.int32, shape, 0)


def _shift_down(a, k, rows):
    return jnp.where(rows >= k, pltpu.roll(a, k, 0), 0.0)


def _shift_up(a, k, rows):
    return jnp.where(rows < T - k, pltpu.roll(a, T - k, 0), 0.0)


def _window_sum(u, gidx, rows, shift):
    s2 = u + shift(u, 1, rows)
    s4 = s2 + shift(s2, 2, rows)
    s8 = s4 + shift(s4, 4, rows)
    s16 = s8 + shift(s8, 8, rows)
    return jnp.where(gidx == 0, s2, jnp.where(gidx == 1, s4, jnp.where(gidx == 2, s8, s16)))


def _pool_count(gidx, rows):
    width = lax.shift_left(jnp.int32(2), gidx)
    return jnp.minimum(rows + 1, width).astype(jnp.float32)


def _pool_fwd(proj, pool_w, pool_scale, token):
    def body(u_ref, pg_ref, w_ref, sc_ref, token_any, y_ref):
        del token_any
        gidx = pl.program_id(0)
        u, pg = u_ref[...], pg_ref[...]
        rows = _row_ids(u.shape)
        d = _window_sum(u, gidx, rows, _shift_down) / _pool_count(gidx, rows) - u
        mixed = _dot(_bf(d), w_ref[...], 1, 0)
        y_ref[...] = _bf(mixed * sc_ref[...] * (pg * _sigmoid(pg)))

    return pl.pallas_call(
        body, name="pool_fwd", grid=(NGROUP,),
        in_specs=[*_POOL_SPECS,
                  pl.BlockSpec((None, GROUP, GROUP), lambda g: (g, 0, 0)),
                  pl.BlockSpec((1, GROUP), lambda g: (0, g)), ANY],
        out_specs=pl.BlockSpec((T, GROUP), lambda g: (0, g)),
        out_shape=pltpu.HBM((T, DMIX), jnp.bfloat16),
        compiler_params=_params(("parallel",)),
    )(proj, proj, pool_w, pool_scale, token)


def _tri(lower):
    r = lax.broadcasted_iota(jnp.int32, (CHUNK, CHUNK), 0)
    c = lax.broadcasted_iota(jnp.int32, (CHUNK, CHUNK), 1)
    return (r >= c) if lower else (r <= c)


def _sum_rows_matrix():
    shape = (CHUNK + 16, CHUNK)
    r, c = lax.broadcasted_iota(jnp.int32, shape, 0), lax.broadcasted_iota(jnp.int32, shape, 1)
    run = jnp.where(c <= r, 1.0, 0.0)
    half = jnp.where(c < CHUNK // 2, 1.0, 0.0)
    return _bf(jnp.where(r < CHUNK, run, jnp.where(r < CHUNK + 8, 1.0, half)))


def _rev_sum_matrix():
    shape = (CHUNK, 2 * CHUNK)
    r, c = lax.broadcasted_iota(jnp.int32, shape, 0), lax.broadcasted_iota(jnp.int32, shape, 1)
    return _bf(jnp.where(c < CHUNK, jnp.where(c >= r, 1.0, 0.0), jnp.where(c - CHUNK < r, 1.0, 0.0)))


def _split3(a):
    a1 = _bf(a)
    r1 = a - a1.astype(jnp.float32)
    a2 = _bf(r1)
    return [a1, a2, _bf(r1 - a2.astype(jnp.float32))]


def _exact_sums(mat, pieces):
    x = jnp.concatenate([s for p in pieces for s in _split3(p)], axis=1)
    r = _dot(mat, x, 1, 0)
    return [r[:, 3 * j * HEAD:(3 * j + 1) * HEAD] + r[:, (3 * j + 1) * HEAD:(3 * j + 2) * HEAD]
            + r[:, (3 * j + 2) * HEAD:(3 * j + 3) * HEAD] for j in range(len(pieces))]


def _gates(qv, fl, lb):
    sq = _sigmoid(qv)
    sg = _sigmoid(fl)
    f = lb + (1.0 - lb) * sg
    return dict(sq=sq, qs=qv * sq, sg=sg, f=f, kk=1.0 - f, g=jnp.log(f))


def _decays(sums):
    big_g = sums[:CHUNK]
    total = sums[CHUNK:CHUNK + 8]
    g_last = jnp.tile(total, (CHUNK // 8, 1))
    g_mid = jnp.tile(sums[CHUNK + 8:], (CHUNK // 8, 1))
    return dict(
        e_q=jnp.exp(big_g),
        e_k=jnp.exp(g_last - big_g),
        e_qm=jnp.exp(jnp.minimum(big_g - g_mid, EXP_CAP)),
        e_km=jnp.exp(jnp.minimum(g_mid - big_g, EXP_CAP)),
        total8=jnp.exp(total),
        state=jnp.exp(jnp.tile(total, (HEAD // 8, 1))))


def _group_rows(gi):
    return [pl.ds(pl.multiple_of((gi * NB + j) * CHUNK, CHUNK), CHUNK) for j in range(NB)]


def _lower_bound(lb_ref):
    return _sigmoid(lb_ref[0:1, :] - lb_ref[1:2, :])


def _hgrn_fwd(proj, lb_logits, rec_g, y_in):
    def body(q_ref, f_ref, i_ref, gate_ref, lb_ref, rg_ref, y_any, y_ref, o_ref, st_ref):
        del y_any
        lb = _lower_bound(lb_ref)
        causal = _tri(True)
        smat = _sum_rows_matrix()

        def group(gi, st):
            rows = _group_rows(gi)
            ts = [_gates(q_ref[r, :], f_ref[r, :], lb) for r in rows]
            ds = [_decays(s) for s in _exact_sums(smat, [t["g"] for t in ts])]
            vs = [_bf(i_ref[r, :]) for r in rows]
            q_m = [_bf(t["qs"] * d["e_qm"]) for t, d in zip(ts, ds)]
            k_m = [_bf(t["kk"] * d["e_km"]) for t, d in zip(ts, ds)]
            q_e = [_bf(t["qs"] * d["e_q"]) for t, d in zip(ts, ds)]
            k_e = [_bf(t["kk"] * d["e_k"]) for t, d in zip(ts, ds)]
            a = [_bf(jnp.where(causal, _dot(q_m[j], k_m[j], 1, 1), 0.0)) for j in range(NB)]
            intra = [_dot(a[j], vs[j], 1, 0) for j in range(NB)]
            upd = [_dot(vs[j], k_e[j], 0, 0) for j in range(NB)]
            for j in range(NB):
                st_ref[gi * NB + j] = st
                o_ref[rows[j], :] = intra[j] + _dot(q_e[j], _bf(st), 1, 1)
                st = st * ds[j]["state"] + upd[j]
            return st

        lax.fori_loop(0, NGRP, group, jnp.zeros((HEAD, HEAD), jnp.float32))
        o = o_ref[...]
        rn = o * lax.rsqrt(jnp.mean(o * o, axis=-1, keepdims=True) + EPS)
        gate = gate_ref[...]
        y_ref[...] = _bf(rn * rg_ref[...] * (gate * _sigmoid(gate)))

    return pl.pallas_call(
        body, name="hgrn_fwd", grid=(NHEAD,),
        in_specs=[*_HEAD_SPECS,
                  pl.BlockSpec((2, HEAD), lambda h: (0, h)),
                  pl.BlockSpec((1, HEAD), lambda h: (0, h)),
                  pl.BlockSpec(memory_space=pl.ANY)],
        out_specs=(pl.BlockSpec((T, HEAD), lambda h: (0, NHEAD + h)),
                   pl.BlockSpec((T, HEAD), lambda h: (0, h)),
                   pl.BlockSpec((None, NCHUNK, HEAD, HEAD), lambda h: (h, 0, 0, 0))),
        out_shape=(pltpu.HBM((T, DMIX), jnp.bfloat16), pltpu.HBM((T, D), jnp.float32),
                   pltpu.HBM((NHEAD, NCHUNK, HEAD, HEAD), jnp.float32)),
        input_output_aliases={6: 0},
        compiler_params=_params(("parallel",)),
    )(proj, proj, proj, proj, lb_logits, rec_g, y_in)


def _out_proj_loss(x, y, w_out, target, gf):
    rows = 256

    def body(x_ref, y_ref, w_ref, t_ref, g_ref, dz_ref, dzb_ref, sq_ref, dg_ref):
        z = x_ref[...] + _dot(y_ref[...], w_ref[...], 1, 0)
        r = lax.rsqrt(jnp.mean(z * z, axis=-1, keepdims=True) + EPS)
        zhat = z * r
        err = zhat * g_ref[...] - t_ref[...]
        dy = err * (1.0 / D)
        gdy = dy * g_ref[...]
        dz = r * (gdy - zhat * jnp.mean(zhat * gdy, axis=-1, keepdims=True))
        dz_ref[...] = dz
        dzb_ref[...] = _bf(dz)
        sq = jnp.sum(err * err, axis=0, keepdims=True)
        dg = jnp.sum(zhat * dy, axis=0, keepdims=True)

        @pl.when(pl.program_id(0) == 0)
        def _():
            sq_ref[...] = sq
            dg_ref[...] = dg

        @pl.when(pl.program_id(0) != 0)
        def _():
            sq_ref[...] += sq
            dg_ref[...] += dg

    tile = pl.BlockSpec((rows, D), lambda i: (i, 0))
    vec = pl.BlockSpec((1, D), lambda i: (0, 0))
    return pl.pallas_call(
        body, name="out_proj_loss", grid=(T // rows,),
        in_specs=[tile, pl.BlockSpec((rows, DMIX), lambda i: (i, 0)), pl.BlockSpec((DMIX, D), lambda i: (0, 0)),
                  tile, vec],
        out_specs=(tile, tile, vec, vec),
        out_shape=(pltpu.HBM((T, D), jnp.float32), pltpu.HBM((T, D), jnp.bfloat16),
                   jax.ShapeDtypeStruct((1, D), jnp.float32), jax.ShapeDtypeStruct((1, D), jnp.float32)),
        compiler_params=_params(("arbitrary",)),
    )(x, y, w_out, target, gf)


def _out_proj_bwd(dzb, w_out, y):
    tn = 512

    def body(dz_ref, w_ref, y_ref, dy_ref, gw_ref, gwb_ref):
        dz = dz_ref[...]
        dy_ref[...] = _dot(dz, w_ref[...], 1, 1)
        gw = _dot(y_ref[...], dz, 0, 0)
        gw_ref[...] = gw
        gwb_ref[...] = _bf(gw)

    return pl.pallas_call(
        body, name="out_proj_bwd", grid=(DMIX // tn,),
        in_specs=[pl.BlockSpec((T, D), lambda n: (0, 0)), pl.BlockSpec((tn, D), lambda n: (n, 0)),
                  pl.BlockSpec((T, tn), lambda n: (0, n))],
        out_specs=(pl.BlockSpec((T, tn), lambda n: (0, n)), pl.BlockSpec((tn, D), lambda n: (n, 0)),
                   pl.BlockSpec((tn, D), lambda n: (n, 0))),
        out_shape=(pltpu.HBM((T, DMIX), jnp.float32), pltpu.HBM((DMIX, D), jnp.float32),
                   pltpu.HBM((DMIX, D), jnp.bfloat16)),
        compiler_params=_params(("parallel",)),
    )(dzb, w_out, y)


def _hgrn_bwd(proj, lb_logits, rec_g, o, states, dymix, dproj_in, token):
    def body(q_ref, f_ref, i_ref, gate_ref, lb_ref, rg_ref, o_ref, st_ref, dy_ref, dp_any, token_any,
             dp_ref, drg_ref, dlb_ref, do_ref):
        del dp_any, token_any
        lb = _lower_bound(lb_ref)
        causal = _tri(True)
        smat, rmat = _sum_rows_matrix(), _rev_sum_matrix()

        o = o_ref[...]
        rs = lax.rsqrt(jnp.mean(o * o, axis=-1, keepdims=True) + EPS)
        rn = o * rs
        gate = gate_ref[...]
        sgate = _sigmoid(gate)
        dyv = dy_ref[...]
        d_r = dyv * (gate * sgate)
        dp_ref[3] = _bf(dyv * (rn * rg_ref[...]) * (sgate * (1.0 + gate * (1.0 - sgate))))
        drg_ref[...] = jnp.sum(d_r * rn, axis=0, keepdims=True)
        drn = d_r * rg_ref[...]
        do_ref[...] = rs * (drn - rn * jnp.mean(rn * drn, axis=-1, keepdims=True))

        def group(i, carry):
            dst, dlb = carry
            gi = NGRP - 1 - i
            rows = _group_rows(gi)
            span = range(NB)
            qvs = [q_ref[r, :] for r in rows]
            ts = [_gates(qv, f_ref[r, :], lb) for qv, r in zip(qvs, rows)]
            ds = [_decays(s) for s in _exact_sums(smat, [t["g"] for t in ts])]
            vs = [_bf(i_ref[r, :]) for r in rows]
            dos = [_bf(do_ref[r, :]) for r in rows]
            sts = [st_ref[gi * NB + j] for j in span]
            qe_f = [t["qs"] * d["e_q"] for t, d in zip(ts, ds)]
            ke_f = [t["kk"] * d["e_k"] for t, d in zip(ts, ds)]
            q_e, k_e = [_bf(a) for a in qe_f], [_bf(a) for a in ke_f]
            q_m = [_bf(t["qs"] * d["e_qm"]) for t, d in zip(ts, ds)]
            k_m = [_bf(t["kk"] * d["e_km"]) for t, d in zip(ts, ds)]
            a = [_bf(jnp.where(causal, _dot(q_m[j], k_m[j], 1, 1), 0.0)) for j in span]
            da = [_bf(jnp.where(causal, _dot(dos[j], vs[j], 1, 1), 0.0)) for j in span]
            dqm = [_dot(da[j], k_m[j], 1, 0) for j in span]
            dkm = [_dot(da[j], q_m[j], 0, 0) for j in span]
            dv_in = [_dot(a[j], dos[j], 0, 0) for j in span]
            dqe = [_dot(dos[j], _bf(sts[j]), 1, 0) for j in span]
            grow = [_dot(dos[j], q_e[j], 0, 0) for j in span]
            dke, carried = [None] * NB, [None] * NB
            for j in reversed(span):
                dst_b = _bf(dst)
                dke[j] = _dot(vs[j], dst_b, 1, 0)
                dp_ref[2, rows[j], :] = _bf(dv_in[j] + _dot(k_e[j], dst_b, 1, 1))
                carried[j] = ds[j]["total8"] * jnp.sum(dst * sts[j], axis=0, keepdims=True)
                dst = dst * ds[j]["state"] + grow[j]
            kdk = [ke_f[j] * dke[j] for j in span]
            pos = [(q_m[j].astype(jnp.float32) * dqm[j] - k_m[j].astype(jnp.float32) * dkm[j]) + qe_f[j] * dqe[j]
                   for j in span]
            dgs = _exact_sums(rmat, [jnp.concatenate([pos[j], kdk[j]], axis=0) for j in span])
            for j in span:
                t, d = ts[j], ds[j]
                dg = dgs[j] + jnp.tile(carried[j], (CHUNK // 8, 1))
                dqs = dqm[j] * d["e_qm"] + dqe[j] * d["e_q"]
                dkk = dkm[j] * d["e_km"] + dke[j] * d["e_k"]
                df = dg / t["f"] - dkk
                dp_ref[1, rows[j], :] = _bf(df * (1.0 - lb) * (t["sg"] * (1.0 - t["sg"])))
                dp_ref[0, rows[j], :] = _bf(dqs * (t["sq"] * (1.0 + qvs[j] * (1.0 - t["sq"]))))
                dlb = dlb + df * (1.0 - t["sg"])
            return dst, dlb

        _, dlb = lax.fori_loop(0, NGRP, group, (jnp.zeros((HEAD, HEAD), jnp.float32),
                                                jnp.zeros((CHUNK, HEAD), jnp.float32)))
        dlb_ref[...] = jnp.sum(dlb, axis=0, keepdims=True)

    vec = pl.BlockSpec((1, HEAD), lambda h: (0, h))
    return pl.pallas_call(
        body, name="hgrn_bwd", grid=(NHEAD,),
        in_specs=[*_HEAD_SPECS,
                  pl.BlockSpec((2, HEAD), lambda h: (0, h)), vec,
                  pl.BlockSpec((T, HEAD), lambda h: (0, h)),
                  pl.BlockSpec((None, NCHUNK, HEAD, HEAD), lambda h: (h, 0, 0, 0)),
                  pl.BlockSpec((T, HEAD), lambda h: (0, NHEAD + h)), ANY, ANY],
        out_specs=(pl.BlockSpec((4, T, HEAD), lambda h: (0, 0, h)), vec, vec),
        out_shape=(pltpu.HBM((NSEG, T, D), jnp.bfloat16),
                   jax.ShapeDtypeStruct((1, D), jnp.float32), jax.ShapeDtypeStruct((1, D), jnp.float32)),
        scratch_shapes=[pltpu.VMEM((T, HEAD), jnp.float32)],
        input_output_aliases={9: 0},
        compiler_params=_params(("parallel",)),
    )(proj, proj, proj, proj, lb_logits, rec_g, o, states, dymix, dproj_in, token)


def _pool_bwd(proj, pool_w, pool_scale, dymix):
    def body(u_ref, pg_ref, w_ref, sc_ref, dy_ref, dp_ref, gw_ref, gs_ref):
        gidx = pl.program_id(0)
        u, pg = u_ref[...], pg_ref[...]
        rows = _row_ids(u.shape)
        count = _pool_count(gidx, rows)
        d = _bf(_window_sum(u, gidx, rows, _shift_down) / count - u)
        mixed = _dot(d, w_ref[...], 1, 0)
        spg = _sigmoid(pg)
        dyv = dy_ref[...]
        d_p = dyv * (pg * spg)
        dp_ref[1] = _bf(dyv * (mixed * sc_ref[...]) * (spg * (1.0 + pg * (1.0 - spg))))
        gs_ref[...] = jnp.sum(d_p * mixed, axis=0, keepdims=True)
        dmixed = _bf(d_p * sc_ref[...])
        gw_ref[...] = _dot(d, dmixed, 0, 0)
        dd = _dot(dmixed, w_ref[...], 1, 1)
        dp_ref[0] = _bf(_window_sum(dd / count, gidx, rows, _shift_up) - dd)

    return pl.pallas_call(
        body, name="pool_bwd", grid=(NGROUP,),
        in_specs=[*_POOL_SPECS,
                  pl.BlockSpec((None, GROUP, GROUP), lambda g: (g, 0, 0)),
                  pl.BlockSpec((1, GROUP), lambda g: (0, g)),
                  pl.BlockSpec((T, GROUP), lambda g: (0, g))],
        out_specs=(pl.BlockSpec((2, T, GROUP), lambda g: (2, 0, g)),
                   pl.BlockSpec((None, GROUP, GROUP), lambda g: (g, 0, 0)),
                   pl.BlockSpec((1, GROUP), lambda g: (0, g))),
        out_shape=(pltpu.HBM((NSEG, T, D), jnp.bfloat16),
                   jax.ShapeDtypeStruct((NGROUP, GROUP, GROUP), jnp.float32),
                   jax.ShapeDtypeStruct((1, D), jnp.float32)),
        compiler_params=_params(("parallel",)),
    )(proj, proj, pool_w, pool_scale, dymix)


def _proj_bwd_w(place, h, dproj):
    half = NTILE // 2

    def owner_chip(i, pr):
        return jnp.where(i < half, i // 3, (pr[1] + 1 + (i - half) // 3) % 4)

    def tile_of(i, pr):
        side = jnp.where(i < half, 1 - pr[2], pr[2])
        return 6 * owner_chip(i, pr) + 3 * side + i % 3

    def dproj_block(i, pr):
        j = tile_of(i, pr)
        return ((j // 4 + 4) % NSEG, 0, j % 4)

    def mine(i):
        return jnp.maximum(i, half)

    def body(place_ref, h_ref, dp_ref, sum_ref, own_ref, sendbuf, recvbuf, send_sems, recv_sems):
        i = pl.program_id(0)
        px, py, c, _ = _place()
        gw = _dot(h_ref[...], dp_ref[...], 0, 0)

        def to_sibling(slot):
            return pltpu.make_async_remote_copy(
                src_ref=sendbuf.at[slot], dst_ref=recvbuf.at[slot], send_sem=send_sems.at[slot],
                recv_sem=recv_sems.at[slot], device_id=(px, py, 1 - c), device_id_type=MESH)

        @pl.when(i < half)
        def _():
            sendbuf[i] = _bf(gw)
            to_sibling(i).start()

        @pl.when(i >= half)
        def _():
            slot = 3 * owner_chip(i, place_ref) + i % 3
            to_sibling(slot).wait_recv()
            total = gw + recvbuf[slot].astype(jnp.float32)
            sum_ref[...] = _bf(total)
            own_ref[...] = total

        @pl.when(i == NTILE - 1)
        def _():
            for slot in range(half):
                to_sibling(slot).wait_send()

    return pl.pallas_call(
        body, name="proj_bwd_w",
        grid_spec=pltpu.PrefetchScalarGridSpec(
            num_scalar_prefetch=1, grid=(NTILE,),
            in_specs=[pl.BlockSpec((T, D), lambda i, pr: (0, 0)),
                      pl.BlockSpec((None, T, TILE), lambda i, pr: dproj_block(i, pr))],
            out_specs=(pl.BlockSpec((None, None, D, TILE), lambda i, pr: (owner_chip(mine(i), pr), mine(i) % 3, 0, 0)),
                       pl.BlockSpec((None, D, TILE), lambda i, pr: (jnp.where(i < NTILE - 3, 0, i % 3), 0, 0))),
            scratch_shapes=[pltpu.VMEM((half, D, TILE), jnp.bfloat16), pltpu.VMEM((half, D, TILE), jnp.bfloat16),
                            pltpu.SemaphoreType.DMA((half,)), pltpu.SemaphoreType.DMA((half,))]),
        out_shape=(pltpu.HBM((4, 3, D, TILE), jnp.bfloat16), pltpu.HBM((3, D, TILE), jnp.float32)),
        compiler_params=_params(("arbitrary",)),
    )(place, h, dproj)


def _proj_bwd_x(dproj, w_t, x, g1, dz, token):
    tm = 1024

    def body(dp_ref, w_ref, x_ref, g_ref, dz_ref, token_any, dx_ref, dg_ref, wcat, acc):
        del token_any
        m, s = pl.program_id(0), pl.program_id(1)
        for i in range(4):
            wcat[:, i * TILE:(i + 1) * TILE] = w_ref[i]
        r = _dot(dp_ref[...], wcat[...], 1, 1)

        @pl.when(s == 0)
        def _():
            acc[...] = r

        @pl.when(s != 0)
        def _():
            acc[...] += r

        @pl.when(s == NSEG - 1)
        def _():
            xv = x_ref[...]
            rs = lax.rsqrt(jnp.mean(xv * xv, axis=-1, keepdims=True) + EPS)
            xhat = xv * rs
            dhv = acc[...]
            gdh = dhv * g_ref[...]
            dx_ref[...] = dz_ref[...] + rs * (gdh - xhat * jnp.mean(xhat * gdh, axis=-1, keepdims=True))
            dg = jnp.sum(xhat * dhv, axis=0, keepdims=True)

            @pl.when(m == 0)
            def _():
                dg_ref[...] = dg

            @pl.when(m != 0)
            def _():
                dg_ref[...] += dg

    half = pl.BlockSpec((tm, D), lambda m, s: (m, 0))
    vec = pl.BlockSpec((1, D), lambda m, s: (0, 0))
    return pl.pallas_call(
        body, name="proj_bwd_x", grid=(T // tm, NSEG),
        in_specs=[pl.BlockSpec((None, tm, D), lambda m, s: (s, m, 0)),
                  pl.BlockSpec((4, D, TILE), lambda m, s: (_seg_tiles(s), 0, 0)), half, vec, half, ANY],
        out_specs=(half, vec),
        out_shape=(jax.ShapeDtypeStruct((T, D), jnp.float32), jax.ShapeDtypeStruct((1, D), jnp.float32)),
        scratch_shapes=[pltpu.VMEM((D, D), jnp.bfloat16), pltpu.VMEM((tm, D), jnp.float32)],
        compiler_params=_params(("arbitrary", "arbitrary"), vmem_mib=56),
    )(dproj, w_t, x, g1, dz, token)


def _adamw(w, g, m, v):
    m_new = ADAM_B1 * m + (1.0 - ADAM_B1) * g
    v_new = ADAM_B2 * v + (1.0 - ADAM_B2) * (g * g)
    delta = -ADAM_LR * ((m_new / BC1) / (jnp.sqrt(v_new / BC2) + ADAM_EPS) + ADAM_WD * w)
    return delta, m_new, v_new


def _reduce_adam(name, place, parts, w, m, v, grid, w_spec):
    n = len(parts)

    def body(place_ref, *refs):
        del place_ref
        w_ref, m_ref, v_ref, g_ref, d_ref, mo_ref, vo_ref = refs[n:]
        g = None
        for ref, (_, _, stacked) in zip(refs[:n], parts):
            terms = [ref[r] for r in range(ref.shape[0])] if stacked else [ref[...]]
            for t in terms:
                g = t.astype(jnp.float32) if g is None else g + t.astype(jnp.float32)
        delta, m_new, v_new = _adamw(w_ref[...], g, m_ref[...], v_ref[...])
        g_ref[...] = g
        d_ref[...] = delta
        mo_ref[...] = m_new
        vo_ref[...] = v_new

    shape = jax.ShapeDtypeStruct(w.shape, jnp.float32)
    return pl.pallas_call(
        body, name=name,
        grid_spec=pltpu.PrefetchScalarGridSpec(
            num_scalar_prefetch=1, grid=grid,
            in_specs=[spec for _, spec, _ in parts] + [w_spec] * 3, out_specs=(w_spec,) * 4),
        out_shape=(shape,) * 4,
        compiler_params=_params(("parallel",)),
    )(place, *[_in_hbm(a) for a in [a for a, _, _ in parts] + [w, m, v]])


def _small_adam(parts, w, m, v):
    def body(p_ref, w_ref, m_ref, v_ref, g_ref, d_ref, mo_ref, vo_ref):
        g = p_ref[0]
        for s in range(1, NDEV):
            g = g + p_ref[s]
        wv = w_ref[...]
        rows = _row_ids(wv.shape)
        other = jnp.where(rows == 2, pltpu.roll(wv, 7, 0), jnp.where(rows == 3, pltpu.roll(wv, 1, 0), 0.0))
        lbv = _sigmoid(wv - other)
        sign = jnp.where(rows == 2, 1.0, -1.0)
        g = jnp.where((rows == 2) | (rows == 3), sign * g * lbv * (1.0 - lbv), g)
        delta, m_new, v_new = _adamw(wv, g, m_ref[...], v_ref[...])
        g_ref[...] = g
        d_ref[...] = delta
        mo_ref[...] = m_new
        vo_ref[...] = v_new

    shape = jax.ShapeDtypeStruct((8, D), jnp.float32)
    return pl.pallas_call(body, name="small_adam", out_shape=(shape,) * 4)(parts, w, m, v)


def _rows8(*vecs):
    rows = [a.reshape(-1, D) for a in vecs]
    n = sum(r.shape[0] for r in rows)
    return jnp.concatenate(rows + [jnp.zeros((8 - n, D), jnp.float32)], axis=0)


def kernel(x, norm1_g, w_in, pool_w, pool_scale, lb_logits, rec_norm_g, w_out, final_norm_g, loss_target, m_norm1_g, m_w_in, m_pool_w, m_pool_scale, m_lb_logits, m_rec_norm_g, m_w_out, m_final_norm_g, v_norm1_g, v_w_in, v_pool_w, v_pool_scale, v_lb_logits, v_rec_norm_g, v_w_out, v_final_norm_g):
    xs = x[0]
    target = loss_target[0]
    ix, iy, ic = lax.axis_index("x"), lax.axis_index("y"), lax.axis_index("c")
    place = jnp.stack([4 * ix + 2 * iy + ic, 2 * ix + iy, ic]).astype(jnp.int32)
    gf = final_norm_g.reshape(1, D)

    h, w_t, w_out_b, w_out_g, pool_g, proj = _gather_proj(xs, norm1_g, w_in, w_out, pool_w)
    pool_full = pool_g.transpose(1, 0, 2, 3).reshape(NGROUP, GROUP, GROUP)
    wout = [w_out_b, w_out_g]
    wout_send, wout_recv, wout, wout_token = _split_start("gather_wout_start", wout, NDEV - 1, _plan_wout)

    y = _pool_fwd(proj, pool_full, pool_scale, wout_token)
    y, o, states = _hgrn_fwd(proj, lb_logits, rec_norm_g, y)
    _, w_out_g = _split_wait("gather_wout_wait", wout, wout_send, wout_recv, _plan_wout, o)
    w_out_full = _in_hbm(w_out_g.reshape(DMIX, D))
    dz, dzb, sq, dgf = _out_proj_loss(xs, y, w_out_full, target, gf)

    dymix, gwout_f, gwout_b = _out_proj_bwd(dzb, w_out_full, y)
    dproj, gpool, dscale = _pool_bwd(proj, pool_full, pool_scale, dymix)

    blk_out = (NDEV, DMIX // NDEV, D)
    blk_pool = (NDEV, NGROUP, GROUP // NDEV, GROUP)
    gpool_s = gpool.reshape(NGROUP, NDEV, GROUP // NDEV, GROUP).transpose(1, 0, 2, 3)
    rest = [gwout_b.reshape(blk_out), gpool_s,
            lax.empty((NDEV - 1,) + blk_out[1:], jnp.bfloat16), lax.empty((NDEV - 1,) + blk_pool[1:], jnp.float32)]
    rest_send, rest_recv, rest, rest_token = _split_start("scatter_rest_start", rest, 2 * (NDEV - 1), _plan_rest)

    dproj, drecg, dlb = _hgrn_bwd(proj, lb_logits, rec_norm_g, o, states, dymix, dproj, rest_token)
    chip_sums, own_sum = _proj_bwd_w(place, h, dproj)
    win = [chip_sums, lax.empty((3, 3, D, TILE), jnp.bfloat16)]
    win_send, win_recv, win, win_token = _split_start("scatter_win_start", win, 3, _plan_in)

    grad_x, dg1 = _proj_bwd_x(dproj, w_t, xs, norm1_g, dz, win_token)

    _, gpool_own, r_out, r_pool = _split_wait("scatter_rest_wait", rest, rest_send, rest_recv, _plan_rest, grad_x)
    g_wout, d_wout, m_wout, v_wout = _reduce_adam(
        "adam_w_out", place,
        [(gwout_f.reshape(blk_out), pl.BlockSpec((None,) + blk_out[1:], lambda i, pr: (pr[0], 0, 0)), False),
         (r_out, pl.BlockSpec((NDEV - 1,) + blk_out[1:], lambda i, pr: (0, 0, 0)), True)],
        w_out, m_w_out, v_w_out, (1,), pl.BlockSpec((None,) + blk_out[1:], lambda i, pr: (0, 0, 0)))
    g_pool, d_pool, m_pool, v_pool = _reduce_adam(
        "adam_pool_w", place,
        [(gpool_own, pl.BlockSpec((None,) + blk_pool[1:], lambda i, pr: (pr[0], 0, 0, 0)), False),
         (r_pool, pl.BlockSpec((NDEV - 1,) + blk_pool[1:], lambda i, pr: (0, 0, 0, 0)), True)],
        pool_w, m_pool_w, v_pool_w, (1,), pl.BlockSpec((None,) + blk_pool[1:], lambda i, pr: (0, 0, 0, 0)))

    r_small = _gather_small(_rows8(dg1, dscale, dlb, dlb, drecg, dgf, sq), d_wout, d_pool)
    loss = jnp.sum(r_small[:, 6, :]) * (0.5 / D)
    g_s, d_s, m_s, v_s = _small_adam(
        r_small,
        _rows8(norm1_g, pool_scale, lb_logits, rec_norm_g, final_norm_g),
        _rows8(m_norm1_g, m_pool_scale, m_lb_logits, m_rec_norm_g, m_final_norm_g),
        _rows8(v_norm1_g, v_pool_scale, v_lb_logits, v_rec_norm_g, v_final_norm_g))

    _, r_in = _split_wait("scatter_win_wait", win, win_send, win_recv, _plan_in, v_s)
    g_win, d_win, m_win, v_win = _reduce_adam(
        "adam_w_in", place,
        [(own_sum, pl.BlockSpec((None, D, TILE), lambda i, pr: (i, 0, 0)), False),
         (r_in, pl.BlockSpec((3, None, D, TILE), lambda i, pr: (0, i, 0, 0)), True)],
        w_in, m_w_in, v_w_in, (3,), pl.BlockSpec((None, D, TILE), lambda i, pr: (0, 0, i)))

    def small_outs(a):
        return a[0:1], a[1:2], a[2:4], a[4:5], a[5]

    def outs(small_a, win, pool, wout):
        n1, ps, lbl, rg, fg = small_outs(small_a)
        return n1, win, pool, ps, lbl, rg, wout, fg

    return (loss, grad_x[None],
            *outs(g_s, g_win, g_pool, g_wout), *outs(d_s, d_win, d_pool, d_wout),
            *outs(m_s, m_win, m_pool, m_wout), *outs(v_s, v_win, v_pool, v_wout))
```

```python
import functools

import jax
import jax.numpy as jnp
from jax import lax
from jax.experimental import pallas as pl
from jax.experimental.pallas import tpu as pltpu

T = 2048
D = 1024
NSEG = 6
NTILE = 24
TILE = 256
DMIX = 2048
NDEV = 8
HEAD = 128
NHEAD = 8
CHUNK = 64
NCHUNK = T // CHUNK
NB = 8
NGRP = NCHUNK // NB
NGROUP = 4
GROUP = 256
EPS = 1e-6
EXP_CAP = 80.0
MESH = pl.DeviceIdType.MESH
AXES = ("x", "y", "c")
ANY = pl.BlockSpec(memory_space=pl.ANY)
HBM = pl.BlockSpec(memory_space=pltpu.HBM)
SEM = pl.BlockSpec(memory_space=pltpu.SEMAPHORE)
EFFECT = pltpu.SideEffectType.DATAFLOW_SIDE_EFFECTING

ADAM_LR = 0.001
ADAM_B1 = 0.9
ADAM_B2 = 0.999
ADAM_EPS = 1e-08
ADAM_WD = 0.01
ADAM_STEP = 10
BC1 = 1.0 - ADAM_B1 ** ADAM_STEP
BC2 = 1.0 - ADAM_B2 ** ADAM_STEP

MIB = 1 << 20


def _params(sem=None, vmem_mib=48):
    return pltpu.CompilerParams(dimension_semantics=sem, vmem_limit_bytes=vmem_mib * MIB)


def _sigmoid(v):
    return 1.0 / (1.0 + jnp.exp(-v))


def _dot(a, b, ca, cb, precision=None):
    return lax.dot_general(a, b, (((ca,), (cb,)), ((), ())), precision=precision,
                           preferred_element_type=jnp.float32)


def _bf(v):
    return v.astype(jnp.bfloat16)


def _in_hbm(a):
    return pltpu.with_memory_space_constraint(a, pltpu.HBM)


def _place():
    x, y, c = lax.axis_index("x"), lax.axis_index("y"), lax.axis_index("c")
    return x, y, c, 4 * x + 2 * y + c


def _peer(x, y, c, r):
    return (x ^ ((r >> 2) & 1), y ^ ((r >> 1) & 1), c ^ (r & 1))


def _gather_proj(x, g1, w_in, w_out, pool_w):
    def body(x_ref, g_ref, win_ref, wout_ref, pool_ref, ht_o, wt_o, woutb_o, wout_o, pool_o, proj_o,
             hv, htv, wv, wob, pb, stage, send_sems, recv_sems, loc_sems, out_sems):
        px, py, c, my_idx = _place()
        me, sibling = (px, py, c), (px, py, 1 - c)
        chips = [(1 - px, py), (px, 1 - py), (1 - px, 1 - py)]
        for p in range(3):
            wv[3 * my_idx + p] = _bf(win_ref[0, :, p * TILE:(p + 1) * TILE])
        pb[...] = _bf(pool_ref[0])
        wob[...] = _bf(wout_ref[0])

        def index(bx, by, bc):
            return 4 * bx + 2 * by + bc

        def slot(w, block):
            return wv.at[pl.ds(3 * index(*block), 3)] if w == 0 else pool_o.at[index(*block)]

        def copy(k, w, block, to, src=None):
            return pltpu.make_async_remote_copy(
                src_ref=slot(w, block) if src is None else src, dst_ref=slot(w, block),
                send_sem=send_sems.at[2 * k + w], recv_sem=recv_sems.at[2 * k + w],
                device_id=to, device_id_type=MESH)

        srcs = (slot(0, me), pb)
        first = []
        for w in (0, 1):
            first += [copy(1 + j, w, me, (*chip, c), src=srcs[w]) for j, chip in enumerate(chips)]
            first.append(copy(0, w, me, sibling, src=srcs[w]))
        for cp in first:
            cp.start()
        locs = [pltpu.make_async_copy(pb, slot(1, me), loc_sems.at[0]),
                pltpu.make_async_copy(wob, wout_o.at[my_idx], loc_sems.at[1]),
                pltpu.make_async_copy(wob, woutb_o, loc_sems.at[2])]
        for cp in locs:
            cp.start()

        xv = x_ref[...]
        hv[...] = _bf(xv * lax.rsqrt(jnp.mean(xv * xv, axis=-1, keepdims=True) + EPS) * g_ref[...])
        rows = 256
        for r0 in range(0, T, rows):
            htv[:, r0:r0 + rows] = hv[r0:r0 + rows, :].T
        locs.append(pltpu.make_async_copy(htv, ht_o, loc_sems.at[3]))
        locs[-1].start()

        def out_copy(p, j):
            return pltpu.make_async_copy(stage.at[p], proj_o.at[j], out_sems.at[p])

        def project(nth, block):
            base = 3 * index(*block)

            def tile(p, carry):
                if nth > 0:
                    out_copy(p, base + p).wait()
                stage[p] = _dot(hv[...], wv[base + p], 1, 0)
                out_copy(p, base + p).start()
                return carry

            lax.fori_loop(0, 3, tile, 0)

        project(0, me)
        copy(0, 0, sibling, me).wait_recv()
        project(1, sibling)
        passed = []

        def arrived(nth, w, j):
            chip = chips[j]
            copy(1 + j, w, (*chip, c), me).wait_recv()
            passed.append(copy(4 + j, w, (*chip, c), sibling))
            passed[-1].start()
            if w == 0:
                project(nth, (*chip, c))

        def handed(nth, j):
            copy(4 + j, 0, (*chips[j], 1 - c), me).wait_recv()
            project(nth, (*chips[j], 1 - c))

        arrived(2, 0, 0)
        arrived(3, 0, 1)
        handed(4, 0)
        handed(5, 1)
        arrived(6, 0, 2)
        handed(7, 2)
        for j in range(3):
            arrived(8, 1, j)
        copy(0, 1, sibling, me).wait_recv()
        for j, chip in enumerate(chips):
            copy(4 + j, 1, (*chip, 1 - c), me).wait_recv()
        keep = pltpu.make_async_copy(wv, wt_o, loc_sems.at[4])
        keep.start()
        for p in range(3):
            out_copy(p, p).wait()
        for cp in first + passed:
            cp.wait_send()
        keep.wait()
        for cp in locs:
            cp.wait()

    vmem = pl.BlockSpec(memory_space=pltpu.VMEM)
    bf16 = jnp.bfloat16
    return pl.pallas_call(
        body, name="gather_proj",
        out_shape=(pltpu.HBM((D, T), bf16), pltpu.HBM((NTILE, D, TILE), bf16),
                   pltpu.HBM((DMIX // NDEV, D), bf16), pltpu.HBM((NDEV, DMIX // NDEV, D), bf16),
                   pltpu.HBM((NDEV, NGROUP, GROUP // NDEV, GROUP), bf16), pltpu.HBM((NTILE, T, TILE), jnp.float32)),
        in_specs=[vmem] * 5, out_specs=(ANY,) * 6,
        scratch_shapes=[pltpu.VMEM((T, D), bf16), pltpu.VMEM((D, T), bf16), pltpu.VMEM((NTILE, D, TILE), bf16),
                        pltpu.VMEM((DMIX // NDEV, D), bf16), pltpu.VMEM((NGROUP, GROUP // NDEV, GROUP), bf16),
                        pltpu.VMEM((3, T, TILE), jnp.float32),
                        pltpu.SemaphoreType.DMA((14,)), pltpu.SemaphoreType.DMA((14,)),
                        pltpu.SemaphoreType.DMA((5,)), pltpu.SemaphoreType.DMA((3,))],
        compiler_params=_params(vmem_mib=56),
    )(x, g1, w_in, w_out, pool_w)


def _split_start(name, arrays, n_copies, plan):
    k = len(arrays)

    def body(*refs):
        send_sems, recv_sems, token = refs[k], refs[k + 1], refs[-1]
        for i, (src, dst, to) in enumerate(plan(refs[:k])):
            pltpu.make_async_remote_copy(src_ref=src, dst_ref=dst, send_sem=send_sems.at[i],
                                         recv_sem=recv_sems.at[i], device_id=to, device_id_type=MESH).start()
        token[...] = jnp.zeros_like(token)

    out = pl.pallas_call(
        body, name=name,
        out_shape=(pltpu.SemaphoreType.DMA((n_copies,)), pltpu.SemaphoreType.DMA((n_copies,)),
                   *[pltpu.HBM(a.shape, a.dtype) for a in arrays], jax.ShapeDtypeStruct((8, 128), jnp.float32)),
        in_specs=[HBM] * k, out_specs=(SEM, SEM, *[HBM] * k, pl.BlockSpec(memory_space=pltpu.VMEM)),
        input_output_aliases={i: 2 + i for i in range(k)},
        compiler_params=pltpu.CompilerParams(has_side_effects=EFFECT),
    )(*[pltpu.with_memory_space_constraint(a, pltpu.HBM) for a in arrays])
    return out[0], out[1], out[2:2 + k], out[-1]


def _split_wait(name, arrays, send_sems, recv_sems, plan, after):
    k = len(arrays)

    def body(*refs):
        sends, recvs = refs[k], refs[k + 1]
        for i, (src, dst, to) in enumerate(plan(refs[:k])):
            cp = pltpu.make_async_remote_copy(src_ref=src, dst_ref=dst, send_sem=sends.at[i], recv_sem=recvs.at[i],
                                              device_id=to, device_id_type=MESH)
            cp.wait_send()
            cp.wait_recv()

    return pl.pallas_call(
        body, name=name,
        out_shape=tuple(pltpu.HBM(a.shape, a.dtype) for a in arrays),
        in_specs=[HBM] * k + [SEM, SEM, ANY], out_specs=(HBM,) * k,
        input_output_aliases={i: i for i in range(k)},
        compiler_params=pltpu.CompilerParams(has_side_effects=EFFECT),
    )(*arrays, send_sems, recv_sems, after)


def _plan_wout(refs):
    src, land = refs
    x, y, c, me = _place()
    return [(src, land.at[me], _peer(x, y, c, r)) for r in range(1, NDEV)]


def _plan_rest(refs):
    gob, gpf, r_out, r_pool = refs
    x, y, c, me = _place()
    plan = []
    for r in range(1, NDEV):
        plan.append((gob.at[me ^ r], r_out.at[r - 1], _peer(x, y, c, r)))
        plan.append((gpf.at[me ^ r], r_pool.at[r - 1], _peer(x, y, c, r)))
    return plan


def _plan_in(refs):
    sum_b, r_in = refs
    x, y, c, _ = _place()
    plan = []
    for j, (dx, dy) in enumerate(((1, 0), (0, 1), (1, 1))):
        px, py = x ^ dx, y ^ dy
        plan.append((sum_b.at[2 * px + py], r_in.at[j], (px, py, c)))
    return plan


def _gather_small(small, *after):
    def body(sm, *refs):
        r_small, send_sems, recv_sems, loc_sem = refs[len(after):]
        x, y, c, me = _place()
        loc = pltpu.make_async_copy(sm, r_small.at[me], loc_sem)
        loc.start()

        def copy(r, src_idx):
            return pltpu.make_async_remote_copy(
                src_ref=sm, dst_ref=r_small.at[src_idx], send_sem=send_sems.at[r - 1], recv_sem=recv_sems.at[r - 1],
                device_id=_peer(x, y, c, r), device_id_type=MESH)

        sends = [copy(r, me) for r in range(1, NDEV)]
        for cp in sends:
            cp.start()
        for r in range(1, NDEV):
            copy(r, me ^ r).wait_recv()
        for cp in sends:
            cp.wait_send()
        loc.wait()

    return pl.pallas_call(
        body, name="gather_small",
        out_shape=jax.ShapeDtypeStruct((NDEV, 8, D), jnp.float32),
        in_specs=[ANY] * (1 + len(after)), out_specs=ANY,
        scratch_shapes=[pltpu.SemaphoreType.DMA((NDEV - 1,)), pltpu.SemaphoreType.DMA((NDEV - 1,)),
                        pltpu.SemaphoreType.DMA],
    )(small, *after)


def _seg_tiles(s):
    return (s + 2) % NSEG


_POOL_SPECS = [pl.BlockSpec((None, T, GROUP), lambda g, base=base: (base + g, 0, 0)) for base in (0, 4)]
_HEAD_SPECS = [pl.BlockSpec((None, T, HEAD), lambda h, base=base: (base + h // 2, 0, h % 2))
               for base in (8, 12, 16, 20)]


def _row_ids(shape):
    return lax.broadcasted_iota(jnp.int32, shape, 0)


EDGE = 8


def _shift_down(a, k):
    r = pltpu.roll(a, k, 0)
    edge = _row_ids((EDGE, a.shape[1]))
    return jnp.concatenate([jnp.where(edge >= k, r[:EDGE], 0.0), r[EDGE:]], axis=0)


def _shift_up(a, k):
    r = pltpu.roll(a, T - k, 0)
    edge = _row_ids((EDGE, a.shape[1]))
    return jnp.concatenate([r[:T - EDGE], jnp.where(edge < EDGE - k, r[T - EDGE:], 0.0)], axis=0)


def _window_sum(u, gidx, shift):
    s2 = u + shift(u, 1)
    s4 = s2 + shift(s2, 2)
    s8 = s4 + shift(s4, 4)
    s16 = s8 + shift(s8, 8)
    return jnp.where(gidx == 0, s2, jnp.where(gidx == 1, s4, jnp.where(gidx == 2, s8, s16)))


def _window_mean(s, gidx):
    inv = jnp.where(gidx == 0, 0.5, jnp.where(gidx == 1, 0.25, jnp.where(gidx == 2, 0.125, 0.0625)))
    width = lax.shift_left(jnp.int32(2), gidx)
    head = s[:16] / jnp.minimum(_row_ids((16, s.shape[1])) + 1, width).astype(jnp.float32)
    return jnp.concatenate([head, s[16:] * inv], axis=0)


def _pool_fwd(proj, pool_w, pool_scale, token):
    def body(u_ref, pg_ref, w_ref, sc_ref, token_any, y_ref):
        del token_any
        gidx = pl.program_id(0)
        u, pg = u_ref[...], pg_ref[...]
        d = _window_mean(_window_sum(u, gidx, _shift_down), gidx) - u
        mixed = _dot(_bf(d), w_ref[...], 1, 0)
        y_ref[...] = _bf(mixed * sc_ref[...] * (pg * _sigmoid(pg)))

    return pl.pallas_call(
        body, name="pool_fwd", grid=(NGROUP,),
        in_specs=[*_POOL_SPECS,
                  pl.BlockSpec((None, GROUP, GROUP), lambda g: (g, 0, 0)),
                  pl.BlockSpec((1, GROUP), lambda g: (0, g)), ANY],
        out_specs=pl.BlockSpec((T, GROUP), lambda g: (0, g)),
        out_shape=pltpu.HBM((T, DMIX), jnp.bfloat16),
        compiler_params=_params(("parallel",)),
    )(proj, proj, pool_w, pool_scale, token)


def _tri(lower):
    r = lax.broadcasted_iota(jnp.int32, (CHUNK, CHUNK), 0)
    c = lax.broadcasted_iota(jnp.int32, (CHUNK, CHUNK), 1)
    return (r >= c) if lower else (r <= c)


def _sum_rows_matrix():
    shape = (CHUNK + 16, CHUNK)
    r, c = lax.broadcasted_iota(jnp.int32, shape, 0), lax.broadcasted_iota(jnp.int32, shape, 1)
    run = jnp.where(c <= r, 1.0, 0.0)
    half = jnp.where(c < CHUNK // 2, 1.0, 0.0)
    return _bf(jnp.where(r < CHUNK, run, jnp.where(r < CHUNK + 8, 1.0, half)))


def _rev_sum_matrix():
    shape = (CHUNK, 2 * CHUNK)
    r, c = lax.broadcasted_iota(jnp.int32, shape, 0), lax.broadcasted_iota(jnp.int32, shape, 1)
    return _bf(jnp.where(c < CHUNK, jnp.where(c >= r, 1.0, 0.0), jnp.where(c - CHUNK < r, 1.0, 0.0)))


def _split2(a):
    hi = _bf(a)
    return [hi, _bf(a - hi.astype(jnp.float32))]


def _exact_sums(mat, pieces):
    x = jnp.concatenate([s for p in pieces for s in _split2(p)], axis=1)
    r = _dot(mat, x, 1, 0)
    return [r[:, 2 * j * HEAD:(2 * j + 1) * HEAD] + r[:, (2 * j + 1) * HEAD:(2 * j + 2) * HEAD]
            for j in range(len(pieces))]


def _gates(qv, fl, lb):
    sq = _sigmoid(qv)
    sg = _sigmoid(fl)
    f = lb + (1.0 - lb) * sg
    return dict(sq=sq, qs=qv * sq, sg=sg, f=f, kk=1.0 - f, g=jnp.log(f))


def _decays(sums):
    big_g = sums[:CHUNK]
    total = sums[CHUNK:CHUNK + 8]
    g_last = jnp.tile(total, (CHUNK // 8, 1))
    g_mid = jnp.tile(sums[CHUNK + 8:], (CHUNK // 8, 1))
    return dict(
        e_q=jnp.exp(big_g),
        e_k=jnp.exp(g_last - big_g),
        e_qm=jnp.exp(jnp.minimum(big_g - g_mid, EXP_CAP)),
        e_km=jnp.exp(jnp.minimum(g_mid - big_g, EXP_CAP)),
        total8=jnp.exp(total),
        state=jnp.exp(jnp.tile(total, (HEAD // 8, 1))))


def _group_rows(gi):
    return [pl.ds(pl.multiple_of((gi * NB + j) * CHUNK, CHUNK), CHUNK) for j in range(NB)]


def _lower_bound(lb_ref):
    return _sigmoid(lb_ref[0:1, :] - lb_ref[1:2, :])


def _hgrn_fwd(proj, lb_logits, rec_g, y_in):
    def body(q_ref, f_ref, i_ref, gate_ref, lb_ref, rg_ref, y_any, y_ref, o_ref, st_ref):
        del y_any
        lb = _lower_bound(lb_ref)
        causal = _tri(True)
        smat = _sum_rows_matrix()

        def group(gi, st):
            rows = _group_rows(gi)
            ts = [_gates(q_ref[r, :], f_ref[r, :], lb) for r in rows]
            ds = [_decays(s) for s in _exact_sums(smat, [t["g"] for t in ts])]
            vs = [_bf(i_ref[r, :]) for r in rows]
            q_m = [_bf(t["qs"] * d["e_qm"]) for t, d in zip(ts, ds)]
            k_m = [_bf(t["kk"] * d["e_km"]) for t, d in zip(ts, ds)]
            q_e = [_bf(t["qs"] * d["e_q"]) for t, d in zip(ts, ds)]
            k_e = [_bf(t["kk"] * d["e_k"]) for t, d in zip(ts, ds)]
            a = [_bf(jnp.where(causal, _dot(q_m[j], k_m[j], 1, 1), 0.0)) for j in range(NB)]
            intra = [_dot(a[j], vs[j], 1, 0) for j in range(NB)]
            upd = [_dot(vs[j], k_e[j], 0, 0) for j in range(NB)]
            for j in range(NB):
                st_ref[gi * NB + j] = st
                o_ref[rows[j], :] = intra[j] + _dot(q_e[j], _bf(st), 1, 1)
                st = st * ds[j]["state"] + upd[j]
            return st

        lax.fori_loop(0, NGRP, group, jnp.zeros((HEAD, HEAD), jnp.float32))
        o = o_ref[...]
        rn = o * lax.rsqrt(jnp.mean(o * o, axis=-1, keepdims=True) + EPS)
        gate = gate_ref[...]
        y_ref[...] = _bf(rn * rg_ref[...] * (gate * _sigmoid(gate)))

    return pl.pallas_call(
        body, name="hgrn_fwd", grid=(NHEAD,),
        in_specs=[*_HEAD_SPECS,
                  pl.BlockSpec((2, HEAD), lambda h: (0, h)),
                  pl.BlockSpec((1, HEAD), lambda h: (0, h)),
                  pl.BlockSpec(memory_space=pl.ANY)],
        out_specs=(pl.BlockSpec((T, HEAD), lambda h: (0, NHEAD + h)),
                   pl.BlockSpec((T, HEAD), lambda h: (0, h)),
                   pl.BlockSpec((None, NCHUNK, HEAD, HEAD), lambda h: (h, 0, 0, 0))),
        out_shape=(pltpu.HBM((T, DMIX), jnp.bfloat16), pltpu.HBM((T, D), jnp.float32),
                   pltpu.HBM((NHEAD, NCHUNK, HEAD, HEAD), jnp.float32)),
        input_output_aliases={6: 0},
        compiler_params=_params(("parallel",)),
    )(proj, proj, proj, proj, lb_logits, rec_g, y_in)


def _out_proj_loss(x, y, w_out, target, gf):
    rows = 256

    def body(x_ref, y_ref, w_ref, t_ref, g_ref, dz_ref, dzb_ref, sq_ref, dg_ref):
        z = x_ref[...] + _dot(y_ref[...], w_ref[...], 1, 0)
        r = lax.rsqrt(jnp.mean(z * z, axis=-1, keepdims=True) + EPS)
        zhat = z * r
        err = zhat * g_ref[...] - t_ref[...]
        dy = err * (1.0 / D)
        gdy = dy * g_ref[...]
        dz = r * (gdy - zhat * jnp.mean(zhat * gdy, axis=-1, keepdims=True))
        dz_ref[...] = dz
        dzb_ref[...] = _bf(dz)
        sq = jnp.sum(err * err, axis=0, keepdims=True)
        dg = jnp.sum(zhat * dy, axis=0, keepdims=True)

        @pl.when(pl.program_id(0) == 0)
        def _():
            sq_ref[...] = sq
            dg_ref[...] = dg

        @pl.when(pl.program_id(0) != 0)
        def _():
            sq_ref[...] += sq
            dg_ref[...] += dg

    tile = pl.BlockSpec((rows, D), lambda i: (i, 0))
    vec = pl.BlockSpec((1, D), lambda i: (0, 0))
    return pl.pallas_call(
        body, name="out_proj_loss", grid=(T // rows,),
        in_specs=[tile, pl.BlockSpec((rows, DMIX), lambda i: (i, 0)), pl.BlockSpec((DMIX, D), lambda i: (0, 0)),
                  tile, vec],
        out_specs=(tile, tile, vec, vec),
        out_shape=(pltpu.HBM((T, D), jnp.float32), pltpu.HBM((T, D), jnp.bfloat16),
                   jax.ShapeDtypeStruct((1, D), jnp.float32), jax.ShapeDtypeStruct((1, D), jnp.float32)),
        compiler_params=_params(("arbitrary",)),
    )(x, y, w_out, target, gf)


def _out_proj_bwd(dzb, w_out, y):
    tn = 512

    def body(dz_ref, w_ref, y_ref, dy_ref, gw_ref, gwb_ref):
        dz = dz_ref[...]
        dy_ref[...] = _dot(dz, w_ref[...], 1, 1)
        gw = _dot(y_ref[...], dz, 0, 0)
        gw_ref[...] = gw
        gwb_ref[...] = _bf(gw)

    return pl.pallas_call(
        body, name="out_proj_bwd", grid=(DMIX // tn,),
        in_specs=[pl.BlockSpec((T, D), lambda n: (0, 0)), pl.BlockSpec((tn, D), lambda n: (n, 0)),
                  pl.BlockSpec((T, tn), lambda n: (0, n))],
        out_specs=(pl.BlockSpec((T, tn), lambda n: (0, n)), pl.BlockSpec((tn, D), lambda n: (n, 0)),
                   pl.BlockSpec((tn, D), lambda n: (n, 0))),
        out_shape=(pltpu.HBM((T, DMIX), jnp.float32), pltpu.HBM((DMIX, D), jnp.float32),
                   pltpu.HBM((DMIX, D), jnp.bfloat16)),
        compiler_params=_params(("parallel",)),
    )(dzb, w_out, y)


def _hgrn_bwd(proj, lb_logits, rec_g, o, states, dymix, dproj_in, token):
    def body(q_ref, f_ref, i_ref, gate_ref, lb_ref, rg_ref, o_ref, st_ref, dy_ref, dp_any, token_any,
             dp_ref, drg_ref, dlb_ref, do_ref):
        del dp_any, token_any
        lb = _lower_bound(lb_ref)
        causal = _tri(True)
        smat, rmat = _sum_rows_matrix(), _rev_sum_matrix()

        o = o_ref[...]
        rs = lax.rsqrt(jnp.mean(o * o, axis=-1, keepdims=True) + EPS)
        rn = o * rs
        gate = gate_ref[...]
        sgate = _sigmoid(gate)
        dyv = dy_ref[...]
        d_r = dyv * (gate * sgate)
        dp_ref[3] = _bf(dyv * (rn * rg_ref[...]) * (sgate * (1.0 + gate * (1.0 - sgate))))
        drg_ref[...] = jnp.sum(d_r * rn, axis=0, keepdims=True)
        drn = d_r * rg_ref[...]
        do_ref[...] = rs * (drn - rn * jnp.mean(rn * drn, axis=-1, keepdims=True))

        def group(i, carry):
            dst, dlb = carry
            gi = NGRP - 1 - i
            rows = _group_rows(gi)
            span = range(NB)
            qvs = [q_ref[r, :] for r in rows]
            ts = [_gates(qv, f_ref[r, :], lb) for qv, r in zip(qvs, rows)]
            ds = [_decays(s) for s in _exact_sums(smat, [t["g"] for t in ts])]
            vs = [_bf(i_ref[r, :]) for r in rows]
            dos = [_bf(do_ref[r, :]) for r in rows]
            sts = [st_ref[gi * NB + j] for j in span]
            qe_f = [t["qs"] * d["e_q"] for t, d in zip(ts, ds)]
            ke_f = [t["kk"] * d["e_k"] for t, d in zip(ts, ds)]
            q_e, k_e = [_bf(a) for a in qe_f], [_bf(a) for a in ke_f]
            q_m = [_bf(t["qs"] * d["e_qm"]) for t, d in zip(ts, ds)]
            k_m = [_bf(t["kk"] * d["e_km"]) for t, d in zip(ts, ds)]
            a = [_bf(jnp.where(causal, _dot(q_m[j], k_m[j], 1, 1), 0.0)) for j in span]
            da = [_bf(jnp.where(causal, _dot(dos[j], vs[j], 1, 1), 0.0)) for j in span]
            dqm = [_dot(da[j], k_m[j], 1, 0) for j in span]
            dkm = [_dot(da[j], q_m[j], 0, 0) for j in span]
            dv_in = [_dot(a[j], dos[j], 0, 0) for j in span]
            dqe = [_dot(dos[j], _bf(sts[j]), 1, 0) for j in span]
            grow = [_dot(dos[j], q_e[j], 0, 0) for j in span]
            dke, carried = [None] * NB, [None] * NB
            for j in reversed(span):
                dst_b = _bf(dst)
                dke[j] = _dot(vs[j], dst_b, 1, 0)
                dp_ref[2, rows[j], :] = _bf(dv_in[j] + _dot(k_e[j], dst_b, 1, 1))
                carried[j] = ds[j]["total8"] * jnp.sum(dst * sts[j], axis=0, keepdims=True)
                dst = dst * ds[j]["state"] + grow[j]
            kdk = [ke_f[j] * dke[j] for j in span]
            pos = [(q_m[j].astype(jnp.float32) * dqm[j] - k_m[j].astype(jnp.float32) * dkm[j]) + qe_f[j] * dqe[j]
                   for j in span]
            dgs = _exact_sums(rmat, [jnp.concatenate([pos[j], kdk[j]], axis=0) for j in span])
            for j in span:
                t, d = ts[j], ds[j]
                dg = dgs[j] + jnp.tile(carried[j], (CHUNK // 8, 1))
                dqs = dqm[j] * d["e_qm"] + dqe[j] * d["e_q"]
                dkk = dkm[j] * d["e_km"] + dke[j] * d["e_k"]
                df = dg / t["f"] - dkk
                dp_ref[1, rows[j], :] = _bf(df * (1.0 - lb) * (t["sg"] * (1.0 - t["sg"])))
                dp_ref[0, rows[j], :] = _bf(dqs * (t["sq"] * (1.0 + qvs[j] * (1.0 - t["sq"]))))
                dlb = dlb + df * (1.0 - t["sg"])
            return dst, dlb

        _, dlb = lax.fori_loop(0, NGRP, group, (jnp.zeros((HEAD, HEAD), jnp.float32),
                                                jnp.zeros((CHUNK, HEAD), jnp.float32)))
        dlb_ref[...] = jnp.sum(dlb, axis=0, keepdims=True)

    vec = pl.BlockSpec((1, HEAD), lambda h: (0, h))
    return pl.pallas_call(
        body, name="hgrn_bwd", grid=(NHEAD,),
        in_specs=[*_HEAD_SPECS,
                  pl.BlockSpec((2, HEAD), lambda h: (0, h)), vec,
                  pl.BlockSpec((T, HEAD), lambda h: (0, h)),
                  pl.BlockSpec((None, NCHUNK, HEAD, HEAD), lambda h: (h, 0, 0, 0)),
                  pl.BlockSpec((T, HEAD), lambda h: (0, NHEAD + h)), ANY, ANY],
        out_specs=(pl.BlockSpec((4, T, HEAD), lambda h: (0, 0, h)), vec, vec),
        out_shape=(pltpu.HBM((NSEG, T, D), jnp.bfloat16),
                   jax.ShapeDtypeStruct((1, D), jnp.float32), jax.ShapeDtypeStruct((1, D), jnp.float32)),
        scratch_shapes=[pltpu.VMEM((T, HEAD), jnp.float32)],
        input_output_aliases={9: 0},
        compiler_params=_params(("parallel",)),
    )(proj, proj, proj, proj, lb_logits, rec_g, o, states, dymix, dproj_in, token)


def _pool_bwd(proj, pool_w, pool_scale, dymix):
    def body(u_ref, pg_ref, w_ref, sc_ref, dy_ref, dp_ref, gw_ref, gs_ref):
        gidx = pl.program_id(0)
        u, pg = u_ref[...], pg_ref[...]
        d = _bf(_window_mean(_window_sum(u, gidx, _shift_down), gidx) - u)
        mixed = _dot(d, w_ref[...], 1, 0)
        spg = _sigmoid(pg)
        dyv = dy_ref[...]
        d_p = dyv * (pg * spg)
        dp_ref[1] = _bf(dyv * (mixed * sc_ref[...]) * (spg * (1.0 + pg * (1.0 - spg))))
        gs_ref[...] = jnp.sum(d_p * mixed, axis=0, keepdims=True)
        dmixed = _bf(d_p * sc_ref[...])
        gw_ref[...] = _dot(d, dmixed, 0, 0)
        dd = _dot(dmixed, w_ref[...], 1, 1)
        dp_ref[0] = _bf(_window_sum(_window_mean(dd, gidx), gidx, _shift_up) - dd)

    return pl.pallas_call(
        body, name="pool_bwd", grid=(NGROUP,),
        in_specs=[*_POOL_SPECS,
                  pl.BlockSpec((None, GROUP, GROUP), lambda g: (g, 0, 0)),
                  pl.BlockSpec((1, GROUP), lambda g: (0, g)),
                  pl.BlockSpec((T, GROUP), lambda g: (0, g))],
        out_specs=(pl.BlockSpec((2, T, GROUP), lambda g: (2, 0, g)),
                   pl.BlockSpec((None, GROUP, GROUP), lambda g: (g, 0, 0)),
                   pl.BlockSpec((1, GROUP), lambda g: (0, g))),
        out_shape=(pltpu.HBM((NSEG, T, D), jnp.bfloat16),
                   jax.ShapeDtypeStruct((NGROUP, GROUP, GROUP), jnp.float32),
                   jax.ShapeDtypeStruct((1, D), jnp.float32)),
        compiler_params=_params(("parallel",)),
    )(proj, proj, pool_w, pool_scale, dymix)


def _proj_bwd_w(place, ht, dproj):
    half = NTILE // 2

    def owner_chip(i, pr):
        return jnp.where(i < half, i // 3, (pr[1] + 1 + (i - half) // 3) % 4)

    def tile_of(i, pr):
        side = jnp.where(i < half, 1 - pr[2], pr[2])
        return 6 * owner_chip(i, pr) + 3 * side + i % 3

    def dproj_block(i, pr):
        j = tile_of(i, pr)
        return ((j // 4 + 4) % NSEG, 0, j % 4)

    def mine(i):
        return jnp.maximum(i, half)

    def body(place_ref, h_ref, dp_ref, sum_ref, own_ref, sendbuf, recvbuf, send_sems, recv_sems):
        i = pl.program_id(0)
        px, py, c, _ = _place()
        gw = _dot(h_ref[...], dp_ref[...], 1, 0)

        def to_sibling(slot):
            return pltpu.make_async_remote_copy(
                src_ref=sendbuf.at[slot], dst_ref=recvbuf.at[slot], send_sem=send_sems.at[slot],
                recv_sem=recv_sems.at[slot], device_id=(px, py, 1 - c), device_id_type=MESH)

        @pl.when(i < half)
        def _():
            sendbuf[i] = _bf(gw)
            to_sibling(i).start()

        @pl.when(i >= half)
        def _():
            slot = 3 * owner_chip(i, place_ref) + i % 3
            to_sibling(slot).wait_recv()
            total = gw + recvbuf[slot].astype(jnp.float32)
            sum_ref[...] = _bf(total)
            own_ref[...] = total

        @pl.when(i == NTILE - 1)
        def _():
            for slot in range(half):
                to_sibling(slot).wait_send()

    return pl.pallas_call(
        body, name="proj_bwd_w",
        grid_spec=pltpu.PrefetchScalarGridSpec(
            num_scalar_prefetch=1, grid=(NTILE,),
            in_specs=[pl.BlockSpec((D, T), lambda i, pr: (0, 0)),
                      pl.BlockSpec((None, T, TILE), lambda i, pr: dproj_block(i, pr))],
            out_specs=(pl.BlockSpec((None, None, D, TILE), lambda i, pr: (owner_chip(mine(i), pr), mine(i) % 3, 0, 0)),
                       pl.BlockSpec((None, D, TILE), lambda i, pr: (jnp.where(i < NTILE - 3, 0, i % 3), 0, 0))),
            scratch_shapes=[pltpu.VMEM((half, D, TILE), jnp.bfloat16), pltpu.VMEM((half, D, TILE), jnp.bfloat16),
                            pltpu.SemaphoreType.DMA((half,)), pltpu.SemaphoreType.DMA((half,))]),
        out_shape=(pltpu.HBM((4, 3, D, TILE), jnp.bfloat16), pltpu.HBM((3, D, TILE), jnp.float32)),
        compiler_params=_params(("arbitrary",)),
    )(place, ht, dproj)


def _proj_bwd_x(dproj, w_t, x, g1, dz, token):
    tm = 1024

    def body(dp_ref, w_ref, x_ref, g_ref, dz_ref, token_any, dx_ref, dg_ref, wcat, acc):
        del token_any
        m, s = pl.program_id(0), pl.program_id(1)
        for i in range(4):
            wcat[:, i * TILE:(i + 1) * TILE] = w_ref[i]
        r = _dot(dp_ref[...], wcat[...], 1, 1)

        @pl.when(s == 0)
        def _():
            acc[...] = r

        @pl.when(s != 0)
        def _():
            acc[...] += r

        @pl.when(s == NSEG - 1)
        def _():
            xv = x_ref[...]
            rs = lax.rsqrt(jnp.mean(xv * xv, axis=-1, keepdims=True) + EPS)
            xhat = xv * rs
            dhv = acc[...]
            gdh = dhv * g_ref[...]
            dx_ref[...] = dz_ref[...] + rs * (gdh - xhat * jnp.mean(xhat * gdh, axis=-1, keepdims=True))
            dg = jnp.sum(xhat * dhv, axis=0, keepdims=True)

            @pl.when(m == 0)
            def _():
                dg_ref[...] = dg

            @pl.when(m != 0)
            def _():
                dg_ref[...] += dg

    half = pl.BlockSpec((tm, D), lambda m, s: (m, 0))
    vec = pl.BlockSpec((1, D), lambda m, s: (0, 0))
    return pl.pallas_call(
        body, name="proj_bwd_x", grid=(T // tm, NSEG),
        in_specs=[pl.BlockSpec((None, tm, D), lambda m, s: (s, m, 0)),
                  pl.BlockSpec((4, D, TILE), lambda m, s: (_seg_tiles(s), 0, 0)), half, vec, half, ANY],
        out_specs=(half, vec),
        out_shape=(jax.ShapeDtypeStruct((T, D), jnp.float32), jax.ShapeDtypeStruct((1, D), jnp.float32)),
        scratch_shapes=[pltpu.VMEM((D, D), jnp.bfloat16), pltpu.VMEM((tm, D), jnp.float32)],
        compiler_params=_params(("arbitrary", "arbitrary"), vmem_mib=56),
    )(dproj, w_t, x, g1, dz, token)


def _adamw(w, g, m, v):
    m_new = ADAM_B1 * m + (1.0 - ADAM_B1) * g
    v_new = ADAM_B2 * v + (1.0 - ADAM_B2) * (g * g)
    delta = -ADAM_LR * ((m_new / BC1) / (jnp.sqrt(v_new / BC2) + ADAM_EPS) + ADAM_WD * w)
    return delta, m_new, v_new


def _reduce_adam(name, place, parts, w, m, v, grid, w_spec):
    n = len(parts)

    def body(place_ref, *refs):
        del place_ref
        w_ref, m_ref, v_ref, g_ref, d_ref, mo_ref, vo_ref = refs[n:]
        g = None
        for ref, (_, _, stacked) in zip(refs[:n], parts):
            terms = [ref[r] for r in range(ref.shape[0])] if stacked else [ref[...]]
            for t in terms:
                g = t.astype(jnp.float32) if g is None else g + t.astype(jnp.float32)
        delta, m_new, v_new = _adamw(w_ref[...], g, m_ref[...], v_ref[...])
        g_ref[...] = g
        d_ref[...] = delta
        mo_ref[...] = m_new
        vo_ref[...] = v_new

    shape = jax.ShapeDtypeStruct(w.shape, jnp.float32)
    return pl.pallas_call(
        body, name=name,
        grid_spec=pltpu.PrefetchScalarGridSpec(
            num_scalar_prefetch=1, grid=grid,
            in_specs=[spec for _, spec, _ in parts] + [w_spec] * 3, out_specs=(w_spec,) * 4),
        out_shape=(shape,) * 4,
        compiler_params=_params(("parallel",)),
    )(place, *[_in_hbm(a) for a in [a for a, _, _ in parts] + [w, m, v]])


def _small_adam(parts, w, m, v):
    def body(p_ref, w_ref, m_ref, v_ref, g_ref, d_ref, mo_ref, vo_ref):
        g = p_ref[0]
        for s in range(1, NDEV):
            g = g + p_ref[s]
        wv = w_ref[...]
        rows = _row_ids(wv.shape)
        other = jnp.where(rows == 2, pltpu.roll(wv, 7, 0), jnp.where(rows == 3, pltpu.roll(wv, 1, 0), 0.0))
        lbv = _sigmoid(wv - other)
        sign = jnp.where(rows == 2, 1.0, -1.0)
        g = jnp.where((rows == 2) | (rows == 3), sign * g * lbv * (1.0 - lbv), g)
        delta, m_new, v_new = _adamw(wv, g, m_ref[...], v_ref[...])
        g_ref[...] = g
        d_ref[...] = delta
        mo_ref[...] = m_new
        vo_ref[...] = v_new

    shape = jax.ShapeDtypeStruct((8, D), jnp.float32)
    return pl.pallas_call(body, name="small_adam", out_shape=(shape,) * 4)(parts, w, m, v)


def _rows8(*vecs):
    rows = [a.reshape(-1, D) for a in vecs]
    n = sum(r.shape[0] for r in rows)
    return jnp.concatenate(rows + [jnp.zeros((8 - n, D), jnp.float32)], axis=0)


def kernel(x, norm1_g, w_in, pool_w, pool_scale, lb_logits, rec_norm_g, w_out, final_norm_g, loss_target, m_norm1_g, m_w_in, m_pool_w, m_pool_scale, m_lb_logits, m_rec_norm_g, m_w_out, m_final_norm_g, v_norm1_g, v_w_in, v_pool_w, v_pool_scale, v_lb_logits, v_rec_norm_g, v_w_out, v_final_norm_g):
    xs = x[0]
    target = loss_target[0]
    ix, iy, ic = lax.axis_index("x"), lax.axis_index("y"), lax.axis_index("c")
    place = jnp.stack([4 * ix + 2 * iy + ic, 2 * ix + iy, ic]).astype(jnp.int32)
    gf = final_norm_g.reshape(1, D)

    ht, w_t, w_out_b, w_out_g, pool_g, proj = _gather_proj(xs, norm1_g, w_in, w_out, pool_w)
    pool_full = pool_g.transpose(1, 0, 2, 3).reshape(NGROUP, GROUP, GROUP)
    wout = [w_out_b, w_out_g]
    wout_send, wout_recv, wout, wout_token = _split_start("gather_wout_start", wout, NDEV - 1, _plan_wout)

    y = _pool_fwd(proj, pool_full, pool_scale, wout_token)
    y, o, states = _hgrn_fwd(proj, lb_logits, rec_norm_g, y)
    _, w_out_g = _split_wait("gather_wout_wait", wout, wout_send, wout_recv, _plan_wout, o)
    w_out_full = _in_hbm(w_out_g.reshape(DMIX, D))
    dz, dzb, sq, dgf = _out_proj_loss(xs, y, w_out_full, target, gf)

    dymix, gwout_f, gwout_b = _out_proj_bwd(dzb, w_out_full, y)
    dproj, gpool, dscale = _pool_bwd(proj, pool_full, pool_scale, dymix)

    blk_out = (NDEV, DMIX // NDEV, D)
    blk_pool = (NDEV, NGROUP, GROUP // NDEV, GROUP)
    gpool_s = gpool.reshape(NGROUP, NDEV, GROUP // NDEV, GROUP).transpose(1, 0, 2, 3)
    rest = [gwout_b.reshape(blk_out), gpool_s,
            lax.empty((NDEV - 1,) + blk_out[1:], jnp.bfloat16), lax.empty((NDEV - 1,) + blk_pool[1:], jnp.float32)]
    rest_send, rest_recv, rest, rest_token = _split_start("scatter_rest_start", rest, 2 * (NDEV - 1), _plan_rest)

    dproj, drecg, dlb = _hgrn_bwd(proj, lb_logits, rec_norm_g, o, states, dymix, dproj, rest_token)
    chip_sums, own_sum = _proj_bwd_w(place, ht, dproj)
    win = [chip_sums, lax.empty((3, 3, D, TILE), jnp.bfloat16)]
    win_send, win_recv, win, win_token = _split_start("scatter_win_start", win, 3, _plan_in)

    grad_x, dg1 = _proj_bwd_x(dproj, w_t, xs, norm1_g, dz, win_token)

    _, gpool_own, r_out, r_pool = _split_wait("scatter_rest_wait", rest, rest_send, rest_recv, _plan_rest, grad_x)
    g_wout, d_wout, m_wout, v_wout = _reduce_adam(
        "adam_w_out", place,
        [(gwout_f.reshape(blk_out), pl.BlockSpec((None,) + blk_out[1:], lambda i, pr: (pr[0], 0, 0)), False),
         (r_out, pl.BlockSpec((NDEV - 1,) + blk_out[1:], lambda i, pr: (0, 0, 0)), True)],
        w_out, m_w_out, v_w_out, (1,), pl.BlockSpec((None,) + blk_out[1:], lambda i, pr: (0, 0, 0)))
    g_pool, d_pool, m_pool, v_pool = _reduce_adam(
        "adam_pool_w", place,
        [(gpool_own, pl.BlockSpec((None,) + blk_pool[1:], lambda i, pr: (pr[0], 0, 0, 0)), False),
         (r_pool, pl.BlockSpec((NDEV - 1,) + blk_pool[1:], lambda i, pr: (0, 0, 0, 0)), True)],
        pool_w, m_pool_w, v_pool_w, (1,), pl.BlockSpec((None,) + blk_pool[1:], lambda i, pr: (0, 0, 0, 0)))

    r_small = _gather_small(_rows8(dg1, dscale, dlb, dlb, drecg, dgf, sq), d_wout, d_pool)
    loss = jnp.sum(r_small[:, 6, :]) * (0.5 / D)
    g_s, d_s, m_s, v_s = _small_adam(
        r_small,
        _rows8(norm1_g, pool_scale, lb_logits, rec_norm_g, final_norm_g),
        _rows8(m_norm1_g, m_pool_scale, m_lb_logits, m_rec_norm_g, m_final_norm_g),
        _rows8(v_norm1_g, v_pool_scale, v_lb_logits, v_rec_norm_g, v_final_norm_g))

    _, r_in = _split_wait("scatter_win_wait", win, win_send, win_recv, _plan_in, v_s)
    g_win, d_win, m_win, v_win = _reduce_adam(
        "adam_w_in", place,
        [(own_sum, pl.BlockSpec((None, D, TILE), lambda i, pr: (i, 0, 0)), False),
         (r_in, pl.BlockSpec((3, None, D, TILE), lambda i, pr: (0, i, 0, 0)), True)],
        w_in, m_w_in, v_w_in, (3,), pl.BlockSpec((None, D, TILE), lambda i, pr: (0, 0, i)))

    def small_outs(a):
        return a[0:1], a[1:2], a[2:4], a[4:5], a[5]

    def outs(small_a, win, pool, wout):
        n1, ps, lbl, rg, fg = small_outs(small_a)
        return n1, win, pool, ps, lbl, rg, wout, fg

    return (loss, grad_x[None],
            *outs(g_s, g_win, g_pool, g_wout), *outs(d_s, d_win, d_pool, d_wout),
            *outs(m_s, m_win, m_pool, m_wout), *outs(v_s, v_win, v_pool, v_wout))
```

```python
import functools

import jax
import jax.numpy as jnp
from jax import lax
from jax.experimental import pallas as pl
from jax.experimental.pallas import tpu as pltpu

T = 2048
D = 1024
NSEG = 6
NTILE = 24
TILE = 256
DMIX = 2048
NDEV = 8
HEAD = 128
NHEAD = 8
CHUNK = 64
NCHUNK = T // CHUNK
NB = 32
NGRP = NCHUNK // NB
NGROUP = 4
GROUP = 256
EPS = 1e-6
EXP_CAP = 80.0
MESH = pl.DeviceIdType.MESH
AXES = ("x", "y", "c")
ANY = pl.BlockSpec(memory_space=pl.ANY)
HBM = pl.BlockSpec(memory_space=pltpu.HBM)
SEM = pl.BlockSpec(memory_space=pltpu.SEMAPHORE)
EFFECT = pltpu.SideEffectType.DATAFLOW_SIDE_EFFECTING

ADAM_LR = 0.001
ADAM_B1 = 0.9
ADAM_B2 = 0.999
ADAM_EPS = 1e-08
ADAM_WD = 0.01
ADAM_STEP = 10
BC1 = 1.0 - ADAM_B1 ** ADAM_STEP
BC2 = 1.0 - ADAM_B2 ** ADAM_STEP

MIB = 1 << 20


def _params(sem=None, vmem_mib=48):
    return pltpu.CompilerParams(dimension_semantics=sem, vmem_limit_bytes=vmem_mib * MIB)


def _sigmoid(v):
    return 1.0 / (1.0 + jnp.exp(-v))


def _dot(a, b, ca, cb, precision=None):
    return lax.dot_general(a, b, (((ca,), (cb,)), ((), ())), precision=precision,
                           preferred_element_type=jnp.float32)


def _bf(v):
    return v.astype(jnp.bfloat16)


def _in_hbm(a):
    return pltpu.with_memory_space_constraint(a, pltpu.HBM)


def _place():
    x, y, c = lax.axis_index("x"), lax.axis_index("y"), lax.axis_index("c")
    return x, y, c, 4 * x + 2 * y + c


def _peer(x, y, c, r):
    return (x ^ ((r >> 2) & 1), y ^ ((r >> 1) & 1), c ^ (r & 1))


def _gather_proj(x, g1, w_in, w_out, pool_w):
    def body(x_ref, g_ref, win_ref, wout_ref, pool_ref, ht_o, wt_o, woutb_o, wout_o, pool_o, proj_o,
             hv, htv, wv, wob, pb, stage, send_sems, recv_sems, loc_sems, out_sems):
        px, py, c, my_idx = _place()
        me, sibling = (px, py, c), (px, py, 1 - c)
        chips = [(1 - px, py), (px, 1 - py), (1 - px, 1 - py)]
        for p in range(3):
            wv[3 * my_idx + p] = _bf(win_ref[0, :, p * TILE:(p + 1) * TILE])
        pb[...] = _bf(pool_ref[0])
        wob[...] = _bf(wout_ref[0])

        def index(bx, by, bc):
            return 4 * bx + 2 * by + bc

        def slot(w, block):
            return wv.at[pl.ds(3 * index(*block), 3)] if w == 0 else pool_o.at[index(*block)]

        def copy(k, w, block, to, src=None):
            return pltpu.make_async_remote_copy(
                src_ref=slot(w, block) if src is None else src, dst_ref=slot(w, block),
                send_sem=send_sems.at[2 * k + w], recv_sem=recv_sems.at[2 * k + w],
                device_id=to, device_id_type=MESH)

        srcs = (slot(0, me), pb)
        first = []
        for w in (0, 1):
            first += [copy(1 + j, w, me, (*chip, c), src=srcs[w]) for j, chip in enumerate(chips)]
            first.append(copy(0, w, me, sibling, src=srcs[w]))
        for cp in first:
            cp.start()
        locs = [pltpu.make_async_copy(pb, slot(1, me), loc_sems.at[0]),
                pltpu.make_async_copy(wob, wout_o.at[my_idx], loc_sems.at[1]),
                pltpu.make_async_copy(wob, woutb_o, loc_sems.at[2])]
        for cp in locs:
            cp.start()

        xv = x_ref[...]
        hv[...] = _bf(xv * lax.rsqrt(jnp.mean(xv * xv, axis=-1, keepdims=True) + EPS) * g_ref[...])
        rows = 256
        for r0 in range(0, T, rows):
            htv[:, r0:r0 + rows] = hv[r0:r0 + rows, :].T
        locs.append(pltpu.make_async_copy(htv, ht_o, loc_sems.at[3]))
        locs[-1].start()

        def out_copy(p, j):
            return pltpu.make_async_copy(stage.at[p], proj_o.at[j], out_sems.at[p])

        def project(nth, block):
            base = 3 * index(*block)

            def tile(p, carry):
                if nth > 0:
                    out_copy(p, base + p).wait()
                stage[p] = _dot(hv[...], wv[base + p], 1, 0)
                out_copy(p, base + p).start()
                return carry

            lax.fori_loop(0, 3, tile, 0)

        project(0, me)
        copy(0, 0, sibling, me).wait_recv()
        project(1, sibling)
        passed = []

        def arrived(nth, w, j):
            chip = chips[j]
            copy(1 + j, w, (*chip, c), me).wait_recv()
            passed.append(copy(4 + j, w, (*chip, c), sibling))
            passed[-1].start()
            if w == 0:
                project(nth, (*chip, c))

        def handed(nth, j):
            copy(4 + j, 0, (*chips[j], 1 - c), me).wait_recv()
            project(nth, (*chips[j], 1 - c))

        arrived(2, 0, 0)
        arrived(3, 0, 1)
        handed(4, 0)
        handed(5, 1)
        arrived(6, 0, 2)
        handed(7, 2)
        for j in range(3):
            arrived(8, 1, j)
        copy(0, 1, sibling, me).wait_recv()
        for j, chip in enumerate(chips):
            copy(4 + j, 1, (*chip, 1 - c), me).wait_recv()
        keep = pltpu.make_async_copy(wv, wt_o, loc_sems.at[4])
        keep.start()
        for p in range(3):
            out_copy(p, p).wait()
        for cp in first + passed:
            cp.wait_send()
        keep.wait()
        for cp in locs:
            cp.wait()

    vmem = pl.BlockSpec(memory_space=pltpu.VMEM)
    bf16 = jnp.bfloat16
    return pl.pallas_call(
        body, name="gather_proj",
        out_shape=(pltpu.HBM((D, T), bf16), pltpu.HBM((NTILE, D, TILE), bf16),
                   pltpu.HBM((DMIX // NDEV, D), bf16), pltpu.HBM((NDEV, DMIX // NDEV, D), bf16),
                   pltpu.HBM((NDEV, NGROUP, GROUP // NDEV, GROUP), bf16), pltpu.HBM((NTILE, T, TILE), jnp.float32)),
        in_specs=[vmem] * 5, out_specs=(ANY,) * 6,
        scratch_shapes=[pltpu.VMEM((T, D), bf16), pltpu.VMEM((D, T), bf16), pltpu.VMEM((NTILE, D, TILE), bf16),
                        pltpu.VMEM((DMIX // NDEV, D), bf16), pltpu.VMEM((NGROUP, GROUP // NDEV, GROUP), bf16),
                        pltpu.VMEM((3, T, TILE), jnp.float32),
                        pltpu.SemaphoreType.DMA((14,)), pltpu.SemaphoreType.DMA((14,)),
                        pltpu.SemaphoreType.DMA((5,)), pltpu.SemaphoreType.DMA((3,))],
        compiler_params=_params(vmem_mib=56),
    )(x, g1, w_in, w_out, pool_w)


def _split_start(name, arrays, n_copies, plan):
    k = len(arrays)

    def body(*refs):
        send_sems, recv_sems, token = refs[k], refs[k + 1], refs[-1]
        for i, (src, dst, to) in enumerate(plan(refs[:k])):
            pltpu.make_async_remote_copy(src_ref=src, dst_ref=dst, send_sem=send_sems.at[i],
                                         recv_sem=recv_sems.at[i], device_id=to, device_id_type=MESH).start()
        token[...] = jnp.zeros_like(token)

    out = pl.pallas_call(
        body, name=name,
        out_shape=(pltpu.SemaphoreType.DMA((n_copies,)), pltpu.SemaphoreType.DMA((n_copies,)),
                   *[pltpu.HBM(a.shape, a.dtype) for a in arrays], jax.ShapeDtypeStruct((8, 128), jnp.float32)),
        in_specs=[HBM] * k, out_specs=(SEM, SEM, *[HBM] * k, pl.BlockSpec(memory_space=pltpu.VMEM)),
        input_output_aliases={i: 2 + i for i in range(k)},
        compiler_params=pltpu.CompilerParams(has_side_effects=EFFECT),
    )(*[pltpu.with_memory_space_constraint(a, pltpu.HBM) for a in arrays])
    return out[0], out[1], out[2:2 + k], out[-1]


def _split_wait(name, arrays, send_sems, recv_sems, plan, after):
    k = len(arrays)

    def body(*refs):
        sends, recvs = refs[k], refs[k + 1]
        for i, (src, dst, to) in enumerate(plan(refs[:k])):
            cp = pltpu.make_async_remote_copy(src_ref=src, dst_ref=dst, send_sem=sends.at[i], recv_sem=recvs.at[i],
                                              device_id=to, device_id_type=MESH)
            cp.wait_send()
            cp.wait_recv()

    return pl.pallas_call(
        body, name=name,
        out_shape=tuple(pltpu.HBM(a.shape, a.dtype) for a in arrays),
        in_specs=[HBM] * k + [SEM, SEM, ANY], out_specs=(HBM,) * k,
        input_output_aliases={i: i for i in range(k)},
        compiler_params=pltpu.CompilerParams(has_side_effects=EFFECT),
    )(*arrays, send_sems, recv_sems, after)


def _plan_wout(refs):
    src, land = refs
    x, y, c, me = _place()
    return [(src, land.at[me], _peer(x, y, c, r)) for r in range(1, NDEV)]


def _plan_rest(refs):
    gob, gpf, r_out, r_pool = refs
    x, y, c, me = _place()
    plan = []
    for r in range(1, NDEV):
        plan.append((gob.at[me ^ r], r_out.at[r - 1], _peer(x, y, c, r)))
        plan.append((gpf.at[me ^ r], r_pool.at[r - 1], _peer(x, y, c, r)))
    return plan


def _plan_in(refs):
    sum_b, r_in = refs
    x, y, c, _ = _place()
    plan = []
    for j, (dx, dy) in enumerate(((1, 0), (0, 1), (1, 1))):
        px, py = x ^ dx, y ^ dy
        plan.append((sum_b.at[2 * px + py], r_in.at[j], (px, py, c)))
    return plan


def _gather_small(small, *after):
    def body(sm, *refs):
        r_small, send_sems, recv_sems, loc_sem = refs[len(after):]
        x, y, c, me = _place()
        loc = pltpu.make_async_copy(sm, r_small.at[me], loc_sem)
        loc.start()

        def copy(r, src_idx):
            return pltpu.make_async_remote_copy(
                src_ref=sm, dst_ref=r_small.at[src_idx], send_sem=send_sems.at[r - 1], recv_sem=recv_sems.at[r - 1],
                device_id=_peer(x, y, c, r), device_id_type=MESH)

        sends = [copy(r, me) for r in range(1, NDEV)]
        for cp in sends:
            cp.start()
        for r in range(1, NDEV):
            copy(r, me ^ r).wait_recv()
        for cp in sends:
            cp.wait_send()
        loc.wait()

    return pl.pallas_call(
        body, name="gather_small",
        out_shape=jax.ShapeDtypeStruct((NDEV, 8, D), jnp.float32),
        in_specs=[ANY] * (1 + len(after)), out_specs=ANY,
        scratch_shapes=[pltpu.SemaphoreType.DMA((NDEV - 1,)), pltpu.SemaphoreType.DMA((NDEV - 1,)),
                        pltpu.SemaphoreType.DMA],
    )(small, *after)


def _seg_tiles(s):
    return (s + 2) % NSEG


_POOL_SPECS = [pl.BlockSpec((None, T, GROUP), lambda g, base=base: (base + g, 0, 0)) for base in (0, 4)]
_HEAD_SPECS = [pl.BlockSpec((None, T, HEAD), lambda h, base=base: (base + h // 2, 0, h % 2))
               for base in (8, 12, 16, 20)]


def _row_ids(shape):
    return lax.broadcasted_iota(jnp.int32, shape, 0)


EDGE = 8


def _shift_down(a, k):
    r = pltpu.roll(a, k, 0)
    edge = _row_ids((EDGE, a.shape[1]))
    return jnp.concatenate([jnp.where(edge >= k, r[:EDGE], 0.0), r[EDGE:]], axis=0)


def _shift_up(a, k):
    r = pltpu.roll(a, T - k, 0)
    edge = _row_ids((EDGE, a.shape[1]))
    return jnp.concatenate([r[:T - EDGE], jnp.where(edge < EDGE - k, r[T - EDGE:], 0.0)], axis=0)


def _window_sum(u, gidx, shift):
    s2 = u + shift(u, 1)
    s4 = s2 + shift(s2, 2)
    s8 = s4 + shift(s4, 4)
    s16 = s8 + shift(s8, 8)
    return jnp.where(gidx == 0, s2, jnp.where(gidx == 1, s4, jnp.where(gidx == 2, s8, s16)))


def _window_mean(s, gidx):
    inv = jnp.where(gidx == 0, 0.5, jnp.where(gidx == 1, 0.25, jnp.where(gidx == 2, 0.125, 0.0625)))
    width = lax.shift_left(jnp.int32(2), gidx)
    head = s[:16] / jnp.minimum(_row_ids((16, s.shape[1])) + 1, width).astype(jnp.float32)
    return jnp.concatenate([head, s[16:] * inv], axis=0)


def _pool_fwd(proj, pool_w, pool_scale, token):
    def body(u_ref, pg_ref, w_ref, sc_ref, token_any, y_ref):
        del token_any
        gidx = pl.program_id(0)
        u, pg = u_ref[...], pg_ref[...]
        d = _window_mean(_window_sum(u, gidx, _shift_down), gidx) - u
        mixed = _dot(_bf(d), w_ref[...], 1, 0)
        y_ref[...] = _bf(mixed * sc_ref[...] * (pg * _sigmoid(pg)))

    return pl.pallas_call(
        body, name="pool_fwd", grid=(NGROUP,),
        in_specs=[*_POOL_SPECS,
                  pl.BlockSpec((None, GROUP, GROUP), lambda g: (g, 0, 0)),
                  pl.BlockSpec((1, GROUP), lambda g: (0, g)), ANY],
        out_specs=pl.BlockSpec((T, GROUP), lambda g: (0, g)),
        out_shape=pltpu.HBM((T, DMIX), jnp.bfloat16),
        compiler_params=_params(("parallel",)),
    )(proj, proj, pool_w, pool_scale, token)


def _tri(lower):
    r = lax.broadcasted_iota(jnp.int32, (CHUNK, CHUNK), 0)
    c = lax.broadcasted_iota(jnp.int32, (CHUNK, CHUNK), 1)
    return (r >= c) if lower else (r <= c)


def _sum_rows_matrix():
    shape = (CHUNK + 16, CHUNK)
    r, c = lax.broadcasted_iota(jnp.int32, shape, 0), lax.broadcasted_iota(jnp.int32, shape, 1)
    run = jnp.where(c <= r, 1.0, 0.0)
    half = jnp.where(c < CHUNK // 2, 1.0, 0.0)
    return _bf(jnp.where(r < CHUNK, run, jnp.where(r < CHUNK + 8, 1.0, half)))


def _rev_sum_matrix():
    shape = (CHUNK, 2 * CHUNK)
    r, c = lax.broadcasted_iota(jnp.int32, shape, 0), lax.broadcasted_iota(jnp.int32, shape, 1)
    return _bf(jnp.where(c < CHUNK, jnp.where(c >= r, 1.0, 0.0), jnp.where(c - CHUNK < r, 1.0, 0.0)))


def _split2(a):
    hi = _bf(a)
    return [hi, _bf(a - hi.astype(jnp.float32))]


def _exact_sums(mat, pieces):
    x = jnp.concatenate([s for p in pieces for s in _split2(p)], axis=1)
    r = _dot(mat, x, 1, 0)
    return [r[:, 2 * j * HEAD:(2 * j + 1) * HEAD] + r[:, (2 * j + 1) * HEAD:(2 * j + 2) * HEAD]
            for j in range(len(pieces))]


def _gates(qv, fl, lb):
    sq = _sigmoid(qv)
    sg = _sigmoid(fl)
    f = lb + (1.0 - lb) * sg
    return dict(sq=sq, qs=qv * sq, sg=sg, f=f, kk=1.0 - f, g=jnp.log(f))


def _decays(sums):
    big_g = sums[:CHUNK]
    total = sums[CHUNK:CHUNK + 8]
    g_last = jnp.tile(total, (CHUNK // 8, 1))
    g_mid = jnp.tile(sums[CHUNK + 8:], (CHUNK // 8, 1))
    return dict(
        e_q=jnp.exp(big_g),
        e_k=jnp.exp(g_last - big_g),
        e_qm=jnp.exp(jnp.minimum(big_g - g_mid, EXP_CAP)),
        e_km=jnp.exp(jnp.minimum(g_mid - big_g, EXP_CAP)),
        total8=jnp.exp(total),
        state=jnp.exp(jnp.tile(total, (HEAD // 8, 1))))


def _group_rows(gi):
    return [pl.ds(pl.multiple_of((gi * NB + j) * CHUNK, CHUNK), CHUNK) for j in range(NB)]


def _lower_bound(lb_ref):
    return _sigmoid(lb_ref[0:1, :] - lb_ref[1:2, :])


def _hgrn_fwd(proj, lb_logits, rec_g, y_in):
    def body(q_ref, f_ref, i_ref, gate_ref, lb_ref, rg_ref, y_any, y_ref, o_ref, st_ref):
        del y_any
        lb = _lower_bound(lb_ref)
        causal = _tri(True)
        smat = _sum_rows_matrix()

        def group(gi, st):
            rows = _group_rows(gi)
            ts = [_gates(q_ref[r, :], f_ref[r, :], lb) for r in rows]
            ds = [_decays(s) for s in _exact_sums(smat, [t["g"] for t in ts])]
            vs = [_bf(i_ref[r, :]) for r in rows]
            q_m = [_bf(t["qs"] * d["e_qm"]) for t, d in zip(ts, ds)]
            k_m = [_bf(t["kk"] * d["e_km"]) for t, d in zip(ts, ds)]
            q_e = [_bf(t["qs"] * d["e_q"]) for t, d in zip(ts, ds)]
            k_e = [_bf(t["kk"] * d["e_k"]) for t, d in zip(ts, ds)]
            a = [_bf(jnp.where(causal, _dot(q_m[j], k_m[j], 1, 1), 0.0)) for j in range(NB)]
            intra = [_dot(a[j], vs[j], 1, 0) for j in range(NB)]
            upd = [_dot(vs[j], k_e[j], 0, 0) for j in range(NB)]
            for j in range(NB):
                st_ref[gi * NB + j] = st
                o_ref[rows[j], :] = intra[j] + _dot(q_e[j], _bf(st), 1, 1)
                st = st * ds[j]["state"] + upd[j]
            return st

        lax.fori_loop(0, NGRP, group, jnp.zeros((HEAD, HEAD), jnp.float32))
        o = o_ref[...]
        rn = o * lax.rsqrt(jnp.mean(o * o, axis=-1, keepdims=True) + EPS)
        gate = gate_ref[...]
        y_ref[...] = _bf(rn * rg_ref[...] * (gate * _sigmoid(gate)))

    return pl.pallas_call(
        body, name="hgrn_fwd", grid=(NHEAD,),
        in_specs=[*_HEAD_SPECS,
                  pl.BlockSpec((2, HEAD), lambda h: (0, h)),
                  pl.BlockSpec((1, HEAD), lambda h: (0, h)),
                  pl.BlockSpec(memory_space=pl.ANY)],
        out_specs=(pl.BlockSpec((T, HEAD), lambda h: (0, NHEAD + h)),
                   pl.BlockSpec((T, HEAD), lambda h: (0, h)),
                   pl.BlockSpec((None, NCHUNK, HEAD, HEAD), lambda h: (h, 0, 0, 0))),
        out_shape=(pltpu.HBM((T, DMIX), jnp.bfloat16), pltpu.HBM((T, D), jnp.float32),
                   pltpu.HBM((NHEAD, NCHUNK, HEAD, HEAD), jnp.float32)),
        input_output_aliases={6: 0},
        compiler_params=_params(("parallel",)),
    )(proj, proj, proj, proj, lb_logits, rec_g, y_in)


def _out_proj_loss(x, y, w_out, target, gf):
    rows = 256

    def body(x_ref, y_ref, w_ref, t_ref, g_ref, dz_ref, dzb_ref, sq_ref, dg_ref):
        z = x_ref[...] + _dot(y_ref[...], w_ref[...], 1, 0)
        r = lax.rsqrt(jnp.mean(z * z, axis=-1, keepdims=True) + EPS)
        zhat = z * r
        err = zhat * g_ref[...] - t_ref[...]
        dy = err * (1.0 / D)
        gdy = dy * g_ref[...]
        dz = r * (gdy - zhat * jnp.mean(zhat * gdy, axis=-1, keepdims=True))
        dz_ref[...] = dz
        dzb_ref[...] = _bf(dz)
        sq = jnp.sum(err * err, axis=0, keepdims=True)
        dg = jnp.sum(zhat * dy, axis=0, keepdims=True)

        @pl.when(pl.program_id(0) == 0)
        def _():
            sq_ref[...] = sq
            dg_ref[...] = dg

        @pl.when(pl.program_id(0) != 0)
        def _():
            sq_ref[...] += sq
            dg_ref[...] += dg

    tile = pl.BlockSpec((rows, D), lambda i: (i, 0))
    vec = pl.BlockSpec((1, D), lambda i: (0, 0))
    return pl.pallas_call(
        body, name="out_proj_loss", grid=(T // rows,),
        in_specs=[tile, pl.BlockSpec((rows, DMIX), lambda i: (i, 0)), pl.BlockSpec((DMIX, D), lambda i: (0, 0)),
                  tile, vec],
        out_specs=(tile, tile, vec, vec),
        out_shape=(pltpu.HBM((T, D), jnp.float32), pltpu.HBM((T, D), jnp.bfloat16),
                   jax.ShapeDtypeStruct((1, D), jnp.float32), jax.ShapeDtypeStruct((1, D), jnp.float32)),
        compiler_params=_params(("arbitrary",)),
    )(x, y, w_out, target, gf)


def _out_proj_bwd(dzb, w_out, y):
    tn = 512

    def body(dz_ref, w_ref, y_ref, dy_ref, gw_ref, gwb_ref):
        dz = dz_ref[...]
        dy_ref[...] = _dot(dz, w_ref[...], 1, 1)
        gw = _dot(y_ref[...], dz, 0, 0)
        gw_ref[...] = gw
        gwb_ref[...] = _bf(gw)

    return pl.pallas_call(
        body, name="out_proj_bwd", grid=(DMIX // tn,),
        in_specs=[pl.BlockSpec((T, D), lambda n: (0, 0)), pl.BlockSpec((tn, D), lambda n: (n, 0)),
                  pl.BlockSpec((T, tn), lambda n: (0, n))],
        out_specs=(pl.BlockSpec((T, tn), lambda n: (0, n)), pl.BlockSpec((tn, D), lambda n: (n, 0)),
                   pl.BlockSpec((tn, D), lambda n: (n, 0))),
        out_shape=(pltpu.HBM((T, DMIX), jnp.float32), pltpu.HBM((DMIX, D), jnp.float32),
                   pltpu.HBM((DMIX, D), jnp.bfloat16)),
        compiler_params=_params(("parallel",)),
    )(dzb, w_out, y)


def _hgrn_bwd(proj, lb_logits, rec_g, o, states, dymix, dproj_in, token):
    def body(q_ref, f_ref, i_ref, gate_ref, lb_ref, rg_ref, o_ref, st_ref, dy_ref, dp_any, token_any,
             dp_ref, drg_ref, dlb_ref, do_ref):
        del dp_any, token_any
        lb = _lower_bound(lb_ref)
        causal = _tri(True)
        smat, rmat = _sum_rows_matrix(), _rev_sum_matrix()

        o = o_ref[...]
        rs = lax.rsqrt(jnp.mean(o * o, axis=-1, keepdims=True) + EPS)
        rn = o * rs
        gate = gate_ref[...]
        sgate = _sigmoid(gate)
        dyv = dy_ref[...]
        d_r = dyv * (gate * sgate)
        dp_ref[3] = _bf(dyv * (rn * rg_ref[...]) * (sgate * (1.0 + gate * (1.0 - sgate))))
        drg_ref[...] = jnp.sum(d_r * rn, axis=0, keepdims=True)
        drn = d_r * rg_ref[...]
        do_ref[...] = rs * (drn - rn * jnp.mean(rn * drn, axis=-1, keepdims=True))

        def group(i, carry):
            dst, dlb = carry
            gi = NGRP - 1 - i
            rows = _group_rows(gi)
            span = range(NB)
            qvs = [q_ref[r, :] for r in rows]
            ts = [_gates(qv, f_ref[r, :], lb) for qv, r in zip(qvs, rows)]
            ds = [_decays(s) for s in _exact_sums(smat, [t["g"] for t in ts])]
            vs = [_bf(i_ref[r, :]) for r in rows]
            dos = [_bf(do_ref[r, :]) for r in rows]
            sts = [st_ref[gi * NB + j] for j in span]
            qe_f = [t["qs"] * d["e_q"] for t, d in zip(ts, ds)]
            ke_f = [t["kk"] * d["e_k"] for t, d in zip(ts, ds)]
            q_e, k_e = [_bf(a) for a in qe_f], [_bf(a) for a in ke_f]
            q_m = [_bf(t["qs"] * d["e_qm"]) for t, d in zip(ts, ds)]
            k_m = [_bf(t["kk"] * d["e_km"]) for t, d in zip(ts, ds)]
            a = [_bf(jnp.where(causal, _dot(q_m[j], k_m[j], 1, 1), 0.0)) for j in span]
            da = [_bf(jnp.where(causal, _dot(dos[j], vs[j], 1, 1), 0.0)) for j in span]
            dqm = [_dot(da[j], k_m[j], 1, 0) for j in span]
            dkm = [_dot(da[j], q_m[j], 0, 0) for j in span]
            dv_in = [_dot(a[j], dos[j], 0, 0) for j in span]
            dqe = [_dot(dos[j], _bf(sts[j]), 1, 0) for j in span]
            grow = [_dot(dos[j], q_e[j], 0, 0) for j in span]
            dke, carried = [None] * NB, [None] * NB
            for j in reversed(span):
                dst_b = _bf(dst)
                dke[j] = _dot(vs[j], dst_b, 1, 0)
                dp_ref[2, rows[j], :] = _bf(dv_in[j] + _dot(k_e[j], dst_b, 1, 1))
                carried[j] = ds[j]["total8"] * jnp.sum(dst * sts[j], axis=0, keepdims=True)
                dst = dst * ds[j]["state"] + grow[j]
            kdk = [ke_f[j] * dke[j] for j in span]
            pos = [(q_m[j].astype(jnp.float32) * dqm[j] - k_m[j].astype(jnp.float32) * dkm[j]) + qe_f[j] * dqe[j]
                   for j in span]
            dgs = _exact_sums(rmat, [jnp.concatenate([pos[j], kdk[j]], axis=0) for j in span])
            for j in span:
                t, d = ts[j], ds[j]
                dg = dgs[j] + jnp.tile(carried[j], (CHUNK // 8, 1))
                dqs = dqm[j] * d["e_qm"] + dqe[j] * d["e_q"]
                dkk = dkm[j] * d["e_km"] + dke[j] * d["e_k"]
                df = dg / t["f"] - dkk
                dp_ref[1, rows[j], :] = _bf(df * (1.0 - lb) * (t["sg"] * (1.0 - t["sg"])))
                dp_ref[0, rows[j], :] = _bf(dqs * (t["sq"] * (1.0 + qvs[j] * (1.0 - t["sq"]))))
                dlb = dlb + df * (1.0 - t["sg"])
            return dst, dlb

        _, dlb = lax.fori_loop(0, NGRP, group, (jnp.zeros((HEAD, HEAD), jnp.float32),
                                                jnp.zeros((CHUNK, HEAD), jnp.float32)))
        dlb_ref[...] = jnp.sum(dlb, axis=0, keepdims=True)

    vec = pl.BlockSpec((1, HEAD), lambda h: (0, h))
    return pl.pallas_call(
        body, name="hgrn_bwd", grid=(NHEAD,),
        in_specs=[*_HEAD_SPECS,
                  pl.BlockSpec((2, HEAD), lambda h: (0, h)), vec,
                  pl.BlockSpec((T, HEAD), lambda h: (0, h)),
                  pl.BlockSpec((None, NCHUNK, HEAD, HEAD), lambda h: (h, 0, 0, 0)),
                  pl.BlockSpec((T, HEAD), lambda h: (0, NHEAD + h)), ANY, ANY],
        out_specs=(pl.BlockSpec((4, T, HEAD), lambda h: (0, 0, h)), vec, vec),
        out_shape=(pltpu.HBM((NSEG, T, D), jnp.bfloat16),
                   jax.ShapeDtypeStruct((1, D), jnp.float32), jax.ShapeDtypeStruct((1, D), jnp.float32)),
        scratch_shapes=[pltpu.VMEM((T, HEAD), jnp.float32)],
        input_output_aliases={9: 0},
        compiler_params=_params(("parallel",)),
    )(proj, proj, proj, proj, lb_logits, rec_g, o, states, dymix, dproj_in, token)


def _pool_bwd(proj, pool_w, pool_scale, dymix):
    def body(u_ref, pg_ref, w_ref, sc_ref, dy_ref, dp_ref, gw_ref, gs_ref):
        gidx = pl.program_id(0)
        u, pg = u_ref[...], pg_ref[...]
        d = _bf(_window_mean(_window_sum(u, gidx, _shift_down), gidx) - u)
        mixed = _dot(d, w_ref[...], 1, 0)
        spg = _sigmoid(pg)
        dyv = dy_ref[...]
        d_p = dyv * (pg * spg)
        dp_ref[1] = _bf(dyv * (mixed * sc_ref[...]) * (spg * (1.0 + pg * (1.0 - spg))))
        gs_ref[...] = jnp.sum(d_p * mixed, axis=0, keepdims=True)
        dmixed = _bf(d_p * sc_ref[...])
        gw_ref[...] = _dot(d, dmixed, 0, 0)
        dd = _dot(dmixed, w_ref[...], 1, 1)
        dp_ref[0] = _bf(_window_sum(_window_mean(dd, gidx), gidx, _shift_up) - dd)

    return pl.pallas_call(
        body, name="pool_bwd", grid=(NGROUP,),
        in_specs=[*_POOL_SPECS,
                  pl.BlockSpec((None, GROUP, GROUP), lambda g: (g, 0, 0)),
                  pl.BlockSpec((1, GROUP), lambda g: (0, g)),
                  pl.BlockSpec((T, GROUP), lambda g: (0, g))],
        out_specs=(pl.BlockSpec((2, T, GROUP), lambda g: (2, 0, g)),
                   pl.BlockSpec((None, GROUP, GROUP), lambda g: (g, 0, 0)),
                   pl.BlockSpec((1, GROUP), lambda g: (0, g))),
        out_shape=(pltpu.HBM((NSEG, T, D), jnp.bfloat16),
                   jax.ShapeDtypeStruct((NGROUP, GROUP, GROUP), jnp.float32),
                   jax.ShapeDtypeStruct((1, D), jnp.float32)),
        compiler_params=_params(("parallel",)),
    )(proj, proj, pool_w, pool_scale, dymix)


def _proj_bwd_w(place, ht, dproj):
    half = NTILE // 2

    def owner_chip(i, pr):
        return jnp.where(i < half, i // 3, (pr[1] + 1 + (i - half) // 3) % 4)

    def tile_of(i, pr):
        side = jnp.where(i < half, 1 - pr[2], pr[2])
        return 6 * owner_chip(i, pr) + 3 * side + i % 3

    def dproj_block(i, pr):
        j = tile_of(i, pr)
        return ((j // 4 + 4) % NSEG, 0, j % 4)

    def mine(i):
        return jnp.maximum(i, half)

    def body(place_ref, h_ref, dp_ref, sum_ref, own_ref, sendbuf, recvbuf, send_sems, recv_sems):
        i = pl.program_id(0)
        px, py, c, _ = _place()
        gw = _dot(h_ref[...], dp_ref[...], 1, 0)

        def to_sibling(slot):
            return pltpu.make_async_remote_copy(
                src_ref=sendbuf.at[slot], dst_ref=recvbuf.at[slot], send_sem=send_sems.at[slot],
                recv_sem=recv_sems.at[slot], device_id=(px, py, 1 - c), device_id_type=MESH)

        @pl.when(i < half)
        def _():
            sendbuf[i] = _bf(gw)
            to_sibling(i).start()

        @pl.when(i >= half)
        def _():
            slot = 3 * owner_chip(i, place_ref) + i % 3
            to_sibling(slot).wait_recv()
            total = gw + recvbuf[slot].astype(jnp.float32)
            sum_ref[...] = _bf(total)
            own_ref[...] = total

        @pl.when(i == NTILE - 1)
        def _():
            for slot in range(half):
                to_sibling(slot).wait_send()

    return pl.pallas_call(
        body, name="proj_bwd_w",
        grid_spec=pltpu.PrefetchScalarGridSpec(
            num_scalar_prefetch=1, grid=(NTILE,),
            in_specs=[pl.BlockSpec((D, T), lambda i, pr: (0, 0)),
                      pl.BlockSpec((None, T, TILE), lambda i, pr: dproj_block(i, pr))],
            out_specs=(pl.BlockSpec((None, None, D, TILE), lambda i, pr: (owner_chip(mine(i), pr), mine(i) % 3, 0, 0)),
                       pl.BlockSpec((None, D, TILE), lambda i, pr: (jnp.where(i < NTILE - 3, 0, i % 3), 0, 0))),
            scratch_shapes=[pltpu.VMEM((half, D, TILE), jnp.bfloat16), pltpu.VMEM((half, D, TILE), jnp.bfloat16),
                            pltpu.SemaphoreType.DMA((half,)), pltpu.SemaphoreType.DMA((half,))]),
        out_shape=(pltpu.HBM((4, 3, D, TILE), jnp.bfloat16), pltpu.HBM((3, D, TILE), jnp.float32)),
        compiler_params=_params(("arbitrary",)),
    )(place, ht, dproj)


def _proj_bwd_x(dproj, w_t, x, g1, dz, token):
    tm = 1024

    def body(dp_ref, w_ref, x_ref, g_ref, dz_ref, token_any, dx_ref, dg_ref, wcat, acc):
        del token_any
        m, s = pl.program_id(0), pl.program_id(1)
        for i in range(4):
            wcat[:, i * TILE:(i + 1) * TILE] = w_ref[i]
        r = _dot(dp_ref[...], wcat[...], 1, 1)

        @pl.when(s == 0)
        def _():
            acc[...] = r

        @pl.when(s != 0)
        def _():
            acc[...] += r

        @pl.when(s == NSEG - 1)
        def _():
            xv = x_ref[...]
            rs = lax.rsqrt(jnp.mean(xv * xv, axis=-1, keepdims=True) + EPS)
            xhat = xv * rs
            dhv = acc[...]
            gdh = dhv * g_ref[...]
            dx_ref[...] = dz_ref[...] + rs * (gdh - xhat * jnp.mean(xhat * gdh, axis=-1, keepdims=True))
            dg = jnp.sum(xhat * dhv, axis=0, keepdims=True)

            @pl.when(m == 0)
            def _():
                dg_ref[...] = dg

            @pl.when(m != 0)
            def _():
                dg_ref[...] += dg

    half = pl.BlockSpec((tm, D), lambda m, s: (m, 0))
    vec = pl.BlockSpec((1, D), lambda m, s: (0, 0))
    return pl.pallas_call(
        body, name="proj_bwd_x", grid=(T // tm, NSEG),
        in_specs=[pl.BlockSpec((None, tm, D), lambda m, s: (s, m, 0)),
                  pl.BlockSpec((4, D, TILE), lambda m, s: (_seg_tiles(s), 0, 0)), half, vec, half, ANY],
        out_specs=(half, vec),
        out_shape=(jax.ShapeDtypeStruct((T, D), jnp.float32), jax.ShapeDtypeStruct((1, D), jnp.float32)),
        scratch_shapes=[pltpu.VMEM((D, D), jnp.bfloat16), pltpu.VMEM((tm, D), jnp.float32)],
        compiler_params=_params(("arbitrary", "arbitrary"), vmem_mib=56),
    )(dproj, w_t, x, g1, dz, token)


def _adamw(w, g, m, v):
    m_new = ADAM_B1 * m + (1.0 - ADAM_B1) * g
    v_new = ADAM_B2 * v + (1.0 - ADAM_B2) * (g * g)
    delta = -ADAM_LR * ((m_new / BC1) / (jnp.sqrt(v_new / BC2) + ADAM_EPS) + ADAM_WD * w)
    return delta, m_new, v_new


def _reduce_adam(name, place, parts, w, m, v, grid, w_spec):
    n = len(parts)

    def body(place_ref, *refs):
        del place_ref
        w_ref, m_ref, v_ref, g_ref, d_ref, mo_ref, vo_ref = refs[n:]
        g = None
        for ref, (_, _, stacked) in zip(refs[:n], parts):
            terms = [ref[r] for r in range(ref.shape[0])] if stacked else [ref[...]]
            for t in terms:
                g = t.astype(jnp.float32) if g is None else g + t.astype(jnp.float32)
        delta, m_new, v_new = _adamw(w_ref[...], g, m_ref[...], v_ref[...])
        g_ref[...] = g
        d_ref[...] = delta
        mo_ref[...] = m_new
        vo_ref[...] = v_new

    shape = jax.ShapeDtypeStruct(w.shape, jnp.float32)
    return pl.pallas_call(
        body, name=name,
        grid_spec=pltpu.PrefetchScalarGridSpec(
            num_scalar_prefetch=1, grid=grid,
            in_specs=[spec for _, spec, _ in parts] + [w_spec] * 3, out_specs=(w_spec,) * 4),
        out_shape=(shape,) * 4,
        compiler_params=_params(("parallel",)),
    )(place, *[_in_hbm(a) for a in [a for a, _, _ in parts] + [w, m, v]])


def _small_adam(parts, w, m, v):
    def body(p_ref, w_ref, m_ref, v_ref, g_ref, d_ref, mo_ref, vo_ref):
        g = p_ref[0]
        for s in range(1, NDEV):
            g = g + p_ref[s]
        wv = w_ref[...]
        rows = _row_ids(wv.shape)
        other = jnp.where(rows == 2, pltpu.roll(wv, 7, 0), jnp.where(rows == 3, pltpu.roll(wv, 1, 0), 0.0))
        lbv = _sigmoid(wv - other)
        sign = jnp.where(rows == 2, 1.0, -1.0)
        g = jnp.where((rows == 2) | (rows == 3), sign * g * lbv * (1.0 - lbv), g)
        delta, m_new, v_new = _adamw(wv, g, m_ref[...], v_ref[...])
        g_ref[...] = g
        d_ref[...] = delta
        mo_ref[...] = m_new
        vo_ref[...] = v_new

    shape = jax.ShapeDtypeStruct((8, D), jnp.float32)
    return pl.pallas_call(body, name="small_adam", out_shape=(shape,) * 4)(parts, w, m, v)


def _rows8(*vecs):
    rows = [a.reshape(-1, D) for a in vecs]
    n = sum(r.shape[0] for r in rows)
    return jnp.concatenate(rows + [jnp.zeros((8 - n, D), jnp.float32)], axis=0)


def kernel(x, norm1_g, w_in, pool_w, pool_scale, lb_logits, rec_norm_g, w_out, final_norm_g, loss_target, m_norm1_g, m_w_in, m_pool_w, m_pool_scale, m_lb_logits, m_rec_norm_g, m_w_out, m_final_norm_g, v_norm1_g, v_w_in, v_pool_w, v_pool_scale, v_lb_logits, v_rec_norm_g, v_w_out, v_final_norm_g):
    xs = x[0]
    target = loss_target[0]
    ix, iy, ic = lax.axis_index("x"), lax.axis_index("y"), lax.axis_index("c")
    place = jnp.stack([4 * ix + 2 * iy + ic, 2 * ix + iy, ic]).astype(jnp.int32)
    gf = final_norm_g.reshape(1, D)

    ht, w_t, w_out_b, w_out_g, pool_g, proj = _gather_proj(xs, norm1_g, w_in, w_out, pool_w)
    pool_full = pool_g.transpose(1, 0, 2, 3).reshape(NGROUP, GROUP, GROUP)
    wout = [w_out_b, w_out_g]
    wout_send, wout_recv, wout, wout_token = _split_start("gather_wout_start", wout, NDEV - 1, _plan_wout)

    y = _pool_fwd(proj, pool_full, pool_scale, wout_token)
    y, o, states = _hgrn_fwd(proj, lb_logits, rec_norm_g, y)
    _, w_out_g = _split_wait("gather_wout_wait", wout, wout_send, wout_recv, _plan_wout, o)
    w_out_full = _in_hbm(w_out_g.reshape(DMIX, D))
    dz, dzb, sq, dgf = _out_proj_loss(xs, y, w_out_full, target, gf)

    dymix, gwout_f, gwout_b = _out_proj_bwd(dzb, w_out_full, y)
    dproj, gpool, dscale = _pool_bwd(proj, pool_full, pool_scale, dymix)

    blk_out = (NDEV, DMIX // NDEV, D)
    blk_pool = (NDEV, NGROUP, GROUP // NDEV, GROUP)
    gpool_s = gpool.reshape(NGROUP, NDEV, GROUP // NDEV, GROUP).transpose(1, 0, 2, 3)
    rest = [gwout_b.reshape(blk_out), gpool_s,
            lax.empty((NDEV - 1,) + blk_out[1:], jnp.bfloat16), lax.empty((NDEV - 1,) + blk_pool[1:], jnp.float32)]
    rest_send, rest_recv, rest, rest_token = _split_start("scatter_rest_start", rest, 2 * (NDEV - 1), _plan_rest)

    dproj, drecg, dlb = _hgrn_bwd(proj, lb_logits, rec_norm_g, o, states, dymix, dproj, rest_token)
    chip_sums, own_sum = _proj_bwd_w(place, ht, dproj)
    win = [chip_sums, lax.empty((3, 3, D, TILE), jnp.bfloat16)]
    win_send, win_recv, win, win_token = _split_start("scatter_win_start", win, 3, _plan_in)

    grad_x, dg1 = _proj_bwd_x(dproj, w_t, xs, norm1_g, dz, win_token)

    _, gpool_own, r_out, r_pool = _split_wait("scatter_rest_wait", rest, rest_send, rest_recv, _plan_rest, grad_x)
    g_wout, d_wout, m_wout, v_wout = _reduce_adam(
        "adam_w_out", place,
        [(gwout_f.reshape(blk_out), pl.BlockSpec((None,) + blk_out[1:], lambda i, pr: (pr[0], 0, 0)), False),
         (r_out, pl.BlockSpec((NDEV - 1,) + blk_out[1:], lambda i, pr: (0, 0, 0)), True)],
        w_out, m_w_out, v_w_out, (1,), pl.BlockSpec((None,) + blk_out[1:], lambda i, pr: (0, 0, 0)))
    g_pool, d_pool, m_pool, v_pool = _reduce_adam(
        "adam_pool_w", place,
        [(gpool_own, pl.BlockSpec((None,) + blk_pool[1:], lambda i, pr: (pr[0], 0, 0, 0)), False),
         (r_pool, pl.BlockSpec((NDEV - 1,) + blk_pool[1:], lambda i, pr: (0, 0, 0, 0)), True)],
        pool_w, m_pool_w, v_pool_w, (1,), pl.BlockSpec((None,) + blk_pool[1:], lambda i, pr: (0, 0, 0, 0)))

    r_small = _gather_small(_rows8(dg1, dscale, dlb, dlb, drecg, dgf, sq), d_wout, d_pool)
    loss = jnp.sum(r_small[:, 6, :]) * (0.5 / D)
    g_s, d_s, m_s, v_s = _small_adam(
        r_small,
        _rows8(norm1_g, pool_scale, lb_logits, rec_norm_g, final_norm_g),
        _rows8(m_norm1_g, m_pool_scale, m_lb_logits, m_rec_norm_g, m_final_norm_g),
        _rows8(v_norm1_g, v_pool_scale, v_lb_logits, v_rec_norm_g, v_final_norm_g))

    _, r_in = _split_wait("scatter_win_wait", win, win_send, win_recv, _plan_in, v_s)
    g_win, d_win, m_win, v_win = _reduce_adam(
        "adam_w_in", place,
        [(own_sum, pl.BlockSpec((None, D, TILE), lambda i, pr: (i, 0, 0)), False),
         (r_in, pl.BlockSpec((3, None, D, TILE), lambda i, pr: (0, i, 0, 0)), True)],
        w_in, m_w_in, v_w_in, (3,), pl.BlockSpec((None, D, TILE), lambda i, pr: (0, 0, i)))

    def small_outs(a):
        return a[0:1], a[1:2], a[2:4], a[4:5], a[5]

    def outs(small_a, win, pool, wout):
        n1, ps, lbl, rg, fg = small_outs(small_a)
        return n1, win, pool, ps, lbl, rg, wout, fg

    return (loss, grad_x[None],
            *outs(g_s, g_win, g_pool, g_wout), *outs(d_s, d_win, d_pool, d_wout),
            *outs(m_s, m_win, m_pool, m_wout), *outs(v_s, v_win, v_pool, v_wout))
```

```python
import functools

import jax
import jax.numpy as jnp
from jax import lax
from jax.experimental import pallas as pl
from jax.experimental.pallas import tpu as pltpu

T = 2048
D = 1024
NSEG = 6
NTILE = 24
TILE = 256
DMIX = 2048
NDEV = 8
HEAD = 128
NHEAD = 8
CHUNK = 64
NCHUNK = T // CHUNK
NB = 32
NGRP = NCHUNK // NB
NGROUP = 4
GROUP = 256
EPS = 1e-6
EXP_CAP = 80.0
MESH = pl.DeviceIdType.MESH
AXES = ("x", "y", "c")
ANY = pl.BlockSpec(memory_space=pl.ANY)
HBM = pl.BlockSpec(memory_space=pltpu.HBM)
SEM = pl.BlockSpec(memory_space=pltpu.SEMAPHORE)
EFFECT = pltpu.SideEffectType.DATAFLOW_SIDE_EFFECTING

ADAM_LR = 0.001
ADAM_B1 = 0.9
ADAM_B2 = 0.999
ADAM_EPS = 1e-08
ADAM_WD = 0.01
ADAM_STEP = 10
BC1 = 1.0 - ADAM_B1 ** ADAM_STEP
BC2 = 1.0 - ADAM_B2 ** ADAM_STEP

MIB = 1 << 20


def _params(sem=None, vmem_mib=48):
    return pltpu.CompilerParams(dimension_semantics=sem, vmem_limit_bytes=vmem_mib * MIB)


def _sigmoid(v):
    return 1.0 / (1.0 + jnp.exp(-v))


def _dot(a, b, ca, cb, precision=None):
    return lax.dot_general(a, b, (((ca,), (cb,)), ((), ())), precision=precision,
                           preferred_element_type=jnp.float32)


def _bf(v):
    return v.astype(jnp.bfloat16)


def _in_hbm(a):
    return pltpu.with_memory_space_constraint(a, pltpu.HBM)


def _place():
    x, y, c = lax.axis_index("x"), lax.axis_index("y"), lax.axis_index("c")
    return x, y, c, 4 * x + 2 * y + c


def _peer(x, y, c, r):
    return (x ^ ((r >> 2) & 1), y ^ ((r >> 1) & 1), c ^ (r & 1))


def _gather_proj(x, g1, w_in, w_out, pool_w):
    def body(x_ref, g_ref, win_ref, wout_ref, pool_ref, ht_o, wt_o, woutb_o, wout_o, pool_o, proj_o,
             hv, htv, wv, wob, pb, stage, send_sems, recv_sems, loc_sems, out_sems):
        px, py, c, my_idx = _place()
        me, sibling = (px, py, c), (px, py, 1 - c)
        chips = [(1 - px, py), (px, 1 - py), (1 - px, 1 - py)]
        for p in range(3):
            wv[3 * my_idx + p] = _bf(win_ref[0, :, p * TILE:(p + 1) * TILE])
        pb[...] = _bf(pool_ref[0])
        wob[...] = _bf(wout_ref[0])

        def index(bx, by, bc):
            return 4 * bx + 2 * by + bc

        def slot(w, block):
            return wv.at[pl.ds(3 * index(*block), 3)] if w == 0 else pool_o.at[index(*block)]

        def copy(k, w, block, to, src=None):
            return pltpu.make_async_remote_copy(
                src_ref=slot(w, block) if src is None else src, dst_ref=slot(w, block),
                send_sem=send_sems.at[2 * k + w], recv_sem=recv_sems.at[2 * k + w],
                device_id=to, device_id_type=MESH)

        def save(block):
            at = pl.ds(3 * index(*block), 3)
            pltpu.make_async_copy(wv.at[at], wt_o.at[at], loc_sems.at[4]).start()

        srcs = (slot(0, me), pb)
        first = []
        for w in (0, 1):
            first += [copy(1 + j, w, me, (*chip, c), src=srcs[w]) for j, chip in enumerate(chips[:2])]
            first.append(copy(0, w, me, sibling, src=srcs[w]))
        for cp in first:
            cp.start()
        save(me)
        locs = [pltpu.make_async_copy(pb, slot(1, me), loc_sems.at[0]),
                pltpu.make_async_copy(wob, wout_o.at[my_idx], loc_sems.at[1]),
                pltpu.make_async_copy(wob, woutb_o, loc_sems.at[2])]
        for cp in locs:
            cp.start()

        xv = x_ref[...]
        hv[...] = _bf(xv * lax.rsqrt(jnp.mean(xv * xv, axis=-1, keepdims=True) + EPS) * g_ref[...])
        rows = 256
        for r0 in range(0, T, rows):
            htv[:, r0:r0 + rows] = hv[r0:r0 + rows, :].T
        locs.append(pltpu.make_async_copy(htv, ht_o, loc_sems.at[3]))
        locs[-1].start()

        def out_copy(p, j):
            return pltpu.make_async_copy(stage.at[p], proj_o.at[j], out_sems.at[p])

        def project(nth, block):
            base = 3 * index(*block)

            def tile(p, carry):
                if nth > 0:
                    out_copy(p, base + p).wait()
                stage[p] = _dot(hv[...], wv[base + p], 1, 0)
                out_copy(p, base + p).start()
                return carry

            lax.fori_loop(0, 3, tile, 0)

        project(0, me)
        copy(0, 0, sibling, me).wait_recv()
        save(sibling)
        project(1, sibling)
        passed = []
        relay_from = (px ^ (1 - c), py ^ c, c)
        relay_to = (px ^ c, py ^ (1 - c), c)

        def arrived(w, j):
            copy(1 + j, w, (*chips[j], c), me).wait_recv()
            passed.append(copy(4 + j, w, (*chips[j], c), sibling))
            passed[-1].start()

        def relay(w):
            passed.append(copy(3, w, relay_from, relay_to))
            passed[-1].start()

        def handed(nth, j):
            copy(4 + j, 0, (*chips[j], 1 - c), me).wait_recv()
            save((*chips[j], 1 - c))
            project(nth, (*chips[j], 1 - c))

        arrived(0, 0)
        arrived(0, 1)
        relay(0)
        for j in range(2):
            save((*chips[j], c))
            project(2 + j, (*chips[j], c))
        handed(4, 0)
        handed(5, 1)
        arrived(1, 0)
        arrived(1, 1)
        relay(1)
        arrived(0, 2)
        save((*chips[2], c))
        project(6, (*chips[2], c))
        handed(7, 2)
        arrived(1, 2)
        copy(0, 1, sibling, me).wait_recv()
        for j, chip in enumerate(chips):
            copy(4 + j, 1, (*chip, 1 - c), me).wait_recv()
        keep = pltpu.make_async_copy(wv, wt_o, loc_sems.at[4])
        for p in range(3):
            out_copy(p, p).wait()
        for cp in first + passed:
            cp.wait_send()
        keep.wait()
        for cp in locs:
            cp.wait()

    vmem = pl.BlockSpec(memory_space=pltpu.VMEM)
    bf16 = jnp.bfloat16
    return pl.pallas_call(
        body, name="gather_proj",
        out_shape=(pltpu.HBM((D, T), bf16), pltpu.HBM((NTILE, D, TILE), bf16),
                   pltpu.HBM((DMIX // NDEV, D), bf16), pltpu.HBM((NDEV, DMIX // NDEV, D), bf16),
                   pltpu.HBM((NDEV, NGROUP, GROUP // NDEV, GROUP), bf16), pltpu.HBM((NTILE, T, TILE), jnp.float32)),
        in_specs=[vmem] * 5, out_specs=(ANY,) * 6,
        scratch_shapes=[pltpu.VMEM((T, D), bf16), pltpu.VMEM((D, T), bf16), pltpu.VMEM((NTILE, D, TILE), bf16),
                        pltpu.VMEM((DMIX // NDEV, D), bf16), pltpu.VMEM((NGROUP, GROUP // NDEV, GROUP), bf16),
                        pltpu.VMEM((3, T, TILE), jnp.float32),
                        pltpu.SemaphoreType.DMA((14,)), pltpu.SemaphoreType.DMA((14,)),
                        pltpu.SemaphoreType.DMA((5,)), pltpu.SemaphoreType.DMA((3,))],
        compiler_params=_params(vmem_mib=56),
    )(x, g1, w_in, w_out, pool_w)


def _split_start(name, arrays, n_copies, plan):
    k = len(arrays)

    def body(*refs):
        send_sems, recv_sems, token = refs[k], refs[k + 1], refs[-1]
        for i, (src, dst, to) in enumerate(plan(refs[:k])):
            pltpu.make_async_remote_copy(src_ref=src, dst_ref=dst, send_sem=send_sems.at[i],
                                         recv_sem=recv_sems.at[i], device_id=to, device_id_type=MESH).start()
        token[...] = jnp.zeros_like(token)

    out = pl.pallas_call(
        body, name=name,
        out_shape=(pltpu.SemaphoreType.DMA((n_copies,)), pltpu.SemaphoreType.DMA((n_copies,)),
                   *[pltpu.HBM(a.shape, a.dtype) for a in arrays], jax.ShapeDtypeStruct((8, 128), jnp.float32)),
        in_specs=[HBM] * k, out_specs=(SEM, SEM, *[HBM] * k, pl.BlockSpec(memory_space=pltpu.VMEM)),
        input_output_aliases={i: 2 + i for i in range(k)},
        compiler_params=pltpu.CompilerParams(has_side_effects=EFFECT),
    )(*[pltpu.with_memory_space_constraint(a, pltpu.HBM) for a in arrays])
    return out[0], out[1], out[2:2 + k], out[-1]


def _split_wait(name, arrays, send_sems, recv_sems, plan, after):
    k = len(arrays)

    def body(*refs):
        sends, recvs = refs[k], refs[k + 1]
        for i, (src, dst, to) in enumerate(plan(refs[:k])):
            cp = pltpu.make_async_remote_copy(src_ref=src, dst_ref=dst, send_sem=sends.at[i], recv_sem=recvs.at[i],
                                              device_id=to, device_id_type=MESH)
            cp.wait_send()
            cp.wait_recv()

    return pl.pallas_call(
        body, name=name,
        out_shape=tuple(pltpu.HBM(a.shape, a.dtype) for a in arrays),
        in_specs=[HBM] * k + [SEM, SEM, ANY], out_specs=(HBM,) * k,
        input_output_aliases={i: i for i in range(k)},
        compiler_params=pltpu.CompilerParams(has_side_effects=EFFECT),
    )(*arrays, send_sems, recv_sems, after)


def _plan_wout(refs):
    src, land = refs
    x, y, c, me = _place()
    return [(src, land.at[me], _peer(x, y, c, r)) for r in range(1, NDEV)]


def _plan_rest(refs):
    gob, gpf, r_out, r_pool = refs
    x, y, c, me = _place()
    plan = []
    for r in range(1, NDEV):
        plan.append((gob.at[me ^ r], r_out.at[r - 1], _peer(x, y, c, r)))
        plan.append((gpf.at[me ^ r], r_pool.at[r - 1], _peer(x, y, c, r)))
    return plan


def _plan_in(refs):
    sum_b, r_in = refs
    x, y, c, _ = _place()
    plan = []
    for j, (dx, dy) in enumerate(((1, 0), (0, 1), (1, 1))):
        px, py = x ^ dx, y ^ dy
        plan.append((sum_b.at[2 * px + py], r_in.at[j], (px, py, c)))
    return plan


def _gather_small(small, *after):
    def body(sm, *refs):
        r_small, send_sems, recv_sems, loc_sem = refs[len(after):]
        x, y, c, me = _place()
        loc = pltpu.make_async_copy(sm, r_small.at[me], loc_sem)
        loc.start()

        def copy(r, src_idx):
            return pltpu.make_async_remote_copy(
                src_ref=sm, dst_ref=r_small.at[src_idx], send_sem=send_sems.at[r - 1], recv_sem=recv_sems.at[r - 1],
                device_id=_peer(x, y, c, r), device_id_type=MESH)

        sends = [copy(r, me) for r in range(1, NDEV)]
        for cp in sends:
            cp.start()
        for r in range(1, NDEV):
            copy(r, me ^ r).wait_recv()
        for cp in sends:
            cp.wait_send()
        loc.wait()

    return pl.pallas_call(
        body, name="gather_small",
        out_shape=jax.ShapeDtypeStruct((NDEV, 8, D), jnp.float32),
        in_specs=[ANY] * (1 + len(after)), out_specs=ANY,
        scratch_shapes=[pltpu.SemaphoreType.DMA((NDEV - 1,)), pltpu.SemaphoreType.DMA((NDEV - 1,)),
                        pltpu.SemaphoreType.DMA],
    )(small, *after)


def _seg_tiles(s):
    return (s + 2) % NSEG


_POOL_SPECS = [pl.BlockSpec((None, T, GROUP), lambda g, base=base: (base + g, 0, 0)) for base in (0, 4)]
_HEAD_SPECS = [pl.BlockSpec((None, T, HEAD), lambda h, base=base: (base + h // 2, 0, h % 2))
               for base in (8, 12, 16, 20)]


def _row_ids(shape):
    return lax.broadcasted_iota(jnp.int32, shape, 0)


EDGE = 8


def _shift_down(a, k):
    r = pltpu.roll(a, k, 0)
    edge = _row_ids((EDGE, a.shape[1]))
    return jnp.concatenate([jnp.where(edge >= k, r[:EDGE], 0.0), r[EDGE:]], axis=0)


def _shift_up(a, k):
    r = pltpu.roll(a, T - k, 0)
    edge = _row_ids((EDGE, a.shape[1]))
    return jnp.concatenate([r[:T - EDGE], jnp.where(edge < EDGE - k, r[T - EDGE:], 0.0)], axis=0)


def _window_sum(u, gidx, shift):
    s2 = u + shift(u, 1)
    s4 = s2 + shift(s2, 2)
    s8 = s4 + shift(s4, 4)
    s16 = s8 + shift(s8, 8)
    return jnp.where(gidx == 0, s2, jnp.where(gidx == 1, s4, jnp.where(gidx == 2, s8, s16)))


def _window_mean(s, gidx):
    inv = jnp.where(gidx == 0, 0.5, jnp.where(gidx == 1, 0.25, jnp.where(gidx == 2, 0.125, 0.0625)))
    width = lax.shift_left(jnp.int32(2), gidx)
    head = s[:16] / jnp.minimum(_row_ids((16, s.shape[1])) + 1, width).astype(jnp.float32)
    return jnp.concatenate([head, s[16:] * inv], axis=0)


def _pool_fwd(proj, pool_w, pool_scale, token):
    def body(u_ref, pg_ref, w_ref, sc_ref, token_any, y_ref):
        del token_any
        gidx = pl.program_id(0)
        u, pg = u_ref[...], pg_ref[...]
        d = _window_mean(_window_sum(u, gidx, _shift_down), gidx) - u
        mixed = _dot(_bf(d), w_ref[...], 1, 0)
        y_ref[...] = _bf(mixed * sc_ref[...] * (pg * _sigmoid(pg)))

    return pl.pallas_call(
        body, name="pool_fwd", grid=(NGROUP,),
        in_specs=[*_POOL_SPECS,
                  pl.BlockSpec((None, GROUP, GROUP), lambda g: (g, 0, 0)),
                  pl.BlockSpec((1, GROUP), lambda g: (0, g)), ANY],
        out_specs=pl.BlockSpec((T, GROUP), lambda g: (0, g)),
        out_shape=pltpu.HBM((T, DMIX), jnp.bfloat16),
        compiler_params=_params(("parallel",)),
    )(proj, proj, pool_w, pool_scale, token)


def _tri(lower):
    r = lax.broadcasted_iota(jnp.int32, (CHUNK, CHUNK), 0)
    c = lax.broadcasted_iota(jnp.int32, (CHUNK, CHUNK), 1)
    return (r >= c) if lower else (r <= c)


def _sum_rows_matrix():
    shape = (CHUNK + 16, CHUNK)
    r, c = lax.broadcasted_iota(jnp.int32, shape, 0), lax.broadcasted_iota(jnp.int32, shape, 1)
    run = jnp.where(c <= r, 1.0, 0.0)
    half = jnp.where(c < CHUNK // 2, 1.0, 0.0)
    return _bf(jnp.where(r < CHUNK, run, jnp.where(r < CHUNK + 8, 1.0, half)))


def _rev_sum_matrix():
    shape = (CHUNK, 2 * CHUNK)
    r, c = lax.broadcasted_iota(jnp.int32, shape, 0), lax.broadcasted_iota(jnp.int32, shape, 1)
    return _bf(jnp.where(c < CHUNK, jnp.where(c >= r, 1.0, 0.0), jnp.where(c - CHUNK < r, 1.0, 0.0)))


def _split2(a):
    hi = _bf(a)
    return [hi, _bf(a - hi.astype(jnp.float32))]


def _exact_sums(mat, pieces):
    x = jnp.concatenate([s for p in pieces for s in _split2(p)], axis=1)
    r = _dot(mat, x, 1, 0)
    return [r[:, 2 * j * HEAD:(2 * j + 1) * HEAD] + r[:, (2 * j + 1) * HEAD:(2 * j + 2) * HEAD]
            for j in range(len(pieces))]


def _gates(qv, fl, lb):
    sq = _sigmoid(qv)
    sg = _sigmoid(fl)
    f = lb + (1.0 - lb) * sg
    return dict(sq=sq, qs=qv * sq, sg=sg, f=f, kk=1.0 - f, g=jnp.log(f))


def _decays(sums):
    big_g = sums[:CHUNK]
    total = sums[CHUNK:CHUNK + 8]
    g_last = jnp.tile(total, (CHUNK // 8, 1))
    g_mid = jnp.tile(sums[CHUNK + 8:], (CHUNK // 8, 1))
    return dict(
        e_q=jnp.exp(big_g),
        e_k=jnp.exp(g_last - big_g),
        e_qm=jnp.exp(jnp.minimum(big_g - g_mid, EXP_CAP)),
        e_km=jnp.exp(jnp.minimum(g_mid - big_g, EXP_CAP)),
        total8=jnp.exp(total),
        state=jnp.exp(jnp.tile(total, (HEAD // 8, 1))))


def _group_rows(gi):
    return [pl.ds(pl.multiple_of((gi * NB + j) * CHUNK, CHUNK), CHUNK) for j in range(NB)]


def _lower_bound(lb_ref):
    return _sigmoid(lb_ref[0:1, :] - lb_ref[1:2, :])


def _hgrn_fwd(proj, lb_logits, rec_g, y_in):
    def body(q_ref, f_ref, i_ref, gate_ref, lb_ref, rg_ref, y_any, y_ref, o_ref, st_ref):
        del y_any
        lb = _lower_bound(lb_ref)
        causal = _tri(True)
        smat = _sum_rows_matrix()

        def group(gi, st):
            rows = _group_rows(gi)
            ts = [_gates(q_ref[r, :], f_ref[r, :], lb) for r in rows]
            ds = [_decays(s) for s in _exact_sums(smat, [t["g"] for t in ts])]
            vs = [_bf(i_ref[r, :]) for r in rows]
            q_m = [_bf(t["qs"] * d["e_qm"]) for t, d in zip(ts, ds)]
            k_m = [_bf(t["kk"] * d["e_km"]) for t, d in zip(ts, ds)]
            q_e = [_bf(t["qs"] * d["e_q"]) for t, d in zip(ts, ds)]
            k_e = [_bf(t["kk"] * d["e_k"]) for t, d in zip(ts, ds)]
            a = [_bf(jnp.where(causal, _dot(q_m[j], k_m[j], 1, 1), 0.0)) for j in range(NB)]
            intra = [_dot(a[j], vs[j], 1, 0) for j in range(NB)]
            upd = [_dot(vs[j], k_e[j], 0, 0) for j in range(NB)]
            for j in range(NB):
                st_ref[gi * NB + j] = st
                o_ref[rows[j], :] = intra[j] + _dot(q_e[j], _bf(st), 1, 1)
                st = st * ds[j]["state"] + upd[j]
            return st

        lax.fori_loop(0, NGRP, group, jnp.zeros((HEAD, HEAD), jnp.float32))
        o = o_ref[...]
        rn = o * lax.rsqrt(jnp.mean(o * o, axis=-1, keepdims=True) + EPS)
        gate = gate_ref[...]
        y_ref[...] = _bf(rn * rg_ref[...] * (gate * _sigmoid(gate)))

    return pl.pallas_call(
        body, name="hgrn_fwd", grid=(NHEAD,),
        in_specs=[*_HEAD_SPECS,
                  pl.BlockSpec((2, HEAD), lambda h: (0, h)),
                  pl.BlockSpec((1, HEAD), lambda h: (0, h)),
                  pl.BlockSpec(memory_space=pl.ANY)],
        out_specs=(pl.BlockSpec((T, HEAD), lambda h: (0, NHEAD + h)),
                   pl.BlockSpec((T, HEAD), lambda h: (0, h)),
                   pl.BlockSpec((None, NCHUNK, HEAD, HEAD), lambda h: (h, 0, 0, 0))),
        out_shape=(pltpu.HBM((T, DMIX), jnp.bfloat16), pltpu.HBM((T, D), jnp.float32),
                   pltpu.HBM((NHEAD, NCHUNK, HEAD, HEAD), jnp.float32)),
        input_output_aliases={6: 0},
        compiler_params=_params(("parallel",)),
    )(proj, proj, proj, proj, lb_logits, rec_g, y_in)


def _out_proj_loss(x, y, w_out, target, gf):
    rows = 256

    def body(x_ref, y_ref, w_ref, t_ref, g_ref, dz_ref, dzb_ref, sq_ref, dg_ref):
        z = x_ref[...] + _dot(y_ref[...], w_ref[...], 1, 0)
        r = lax.rsqrt(jnp.mean(z * z, axis=-1, keepdims=True) + EPS)
        zhat = z * r
        err = zhat * g_ref[...] - t_ref[...]
        dy = err * (1.0 / D)
        gdy = dy * g_ref[...]
        dz = r * (gdy - zhat * jnp.mean(zhat * gdy, axis=-1, keepdims=True))
        dz_ref[...] = dz
        dzb_ref[...] = _bf(dz)
        sq = jnp.sum(err * err, axis=0, keepdims=True)
        dg = jnp.sum(zhat * dy, axis=0, keepdims=True)

        @pl.when(pl.program_id(0) == 0)
        def _():
            sq_ref[...] = sq
            dg_ref[...] = dg

        @pl.when(pl.program_id(0) != 0)
        def _():
            sq_ref[...] += sq
            dg_ref[...] += dg

    tile = pl.BlockSpec((rows, D), lambda i: (i, 0))
    vec = pl.BlockSpec((1, D), lambda i: (0, 0))
    return pl.pallas_call(
        body, name="out_proj_loss", grid=(T // rows,),
        in_specs=[tile, pl.BlockSpec((rows, DMIX), lambda i: (i, 0)), pl.BlockSpec((DMIX, D), lambda i: (0, 0)),
                  tile, vec],
        out_specs=(tile, tile, vec, vec),
        out_shape=(pltpu.HBM((T, D), jnp.float32), pltpu.HBM((T, D), jnp.bfloat16),
                   jax.ShapeDtypeStruct((1, D), jnp.float32), jax.ShapeDtypeStruct((1, D), jnp.float32)),
        compiler_params=_params(("arbitrary",)),
    )(x, y, w_out, target, gf)


def _out_proj_bwd(dzb, w_out, y):
    tn = 512

    def body(dz_ref, w_ref, y_ref, dy_ref, gw_ref, gwb_ref):
        dz = dz_ref[...]
        dy_ref[...] = _dot(dz, w_ref[...], 1, 1)
        gw = _dot(y_ref[...], dz, 0, 0)
        gw_ref[...] = gw
        gwb_ref[...] = _bf(gw)

    return pl.pallas_call(
        body, name="out_proj_bwd", grid=(DMIX // tn,),
        in_specs=[pl.BlockSpec((T, D), lambda n: (0, 0)), pl.BlockSpec((tn, D), lambda n: (n, 0)),
                  pl.BlockSpec((T, tn), lambda n: (0, n))],
        out_specs=(pl.BlockSpec((T, tn), lambda n: (0, n)), pl.BlockSpec((tn, D), lambda n: (n, 0)),
                   pl.BlockSpec((tn, D), lambda n: (n, 0))),
        out_shape=(pltpu.HBM((T, DMIX), jnp.float32), pltpu.HBM((DMIX, D), jnp.float32),
                   pltpu.HBM((DMIX, D), jnp.bfloat16)),
        compiler_params=_params(("parallel",)),
    )(dzb, w_out, y)


def _hgrn_bwd(proj, lb_logits, rec_g, o, states, dymix, dproj_in, token):
    def body(q_ref, f_ref, i_ref, gate_ref, lb_ref, rg_ref, o_ref, st_ref, dy_ref, dp_any, token_any,
             dp_ref, drg_ref, dlb_ref, do_ref):
        del dp_any, token_any
        lb = _lower_bound(lb_ref)
        causal = _tri(True)
        smat, rmat = _sum_rows_matrix(), _rev_sum_matrix()

        o = o_ref[...]
        rs = lax.rsqrt(jnp.mean(o * o, axis=-1, keepdims=True) + EPS)
        rn = o * rs
        gate = gate_ref[...]
        sgate = _sigmoid(gate)
        dyv = dy_ref[...]
        d_r = dyv * (gate * sgate)
        dp_ref[3] = _bf(dyv * (rn * rg_ref[...]) * (sgate * (1.0 + gate * (1.0 - sgate))))
        drg_ref[...] = jnp.sum(d_r * rn, axis=0, keepdims=True)
        drn = d_r * rg_ref[...]
        do_ref[...] = rs * (drn - rn * jnp.mean(rn * drn, axis=-1, keepdims=True))

        def group(i, carry):
            dst, dlb = carry
            gi = NGRP - 1 - i
            rows = _group_rows(gi)
            span = range(NB)
            qvs = [q_ref[r, :] for r in rows]
            ts = [_gates(qv, f_ref[r, :], lb) for qv, r in zip(qvs, rows)]
            ds = [_decays(s) for s in _exact_sums(smat, [t["g"] for t in ts])]
            vs = [_bf(i_ref[r, :]) for r in rows]
            dos = [_bf(do_ref[r, :]) for r in rows]
            sts = [st_ref[gi * NB + j] for j in span]
            qe_f = [t["qs"] * d["e_q"] for t, d in zip(ts, ds)]
            ke_f = [t["kk"] * d["e_k"] for t, d in zip(ts, ds)]
            q_e, k_e = [_bf(a) for a in qe_f], [_bf(a) for a in ke_f]
            q_m = [_bf(t["qs"] * d["e_qm"]) for t, d in zip(ts, ds)]
            k_m = [_bf(t["kk"] * d["e_km"]) for t, d in zip(ts, ds)]
            a = [_bf(jnp.where(causal, _dot(q_m[j], k_m[j], 1, 1), 0.0)) for j in span]
            da = [_bf(jnp.where(causal, _dot(dos[j], vs[j], 1, 1), 0.0)) for j in span]
            dqm = [_dot(da[j], k_m[j], 1, 0) for j in span]
            dkm = [_dot(da[j], q_m[j], 0, 0) for j in span]
            dv_in = [_dot(a[j], dos[j], 0, 0) for j in span]
            dqe = [_dot(dos[j], _bf(sts[j]), 1, 0) for j in span]
            grow = [_dot(dos[j], q_e[j], 0, 0) for j in span]
            dke, carried = [None] * NB, [None] * NB
            for j in reversed(span):
                dst_b = _bf(dst)
                dke[j] = _dot(vs[j], dst_b, 1, 0)
                dp_ref[2, rows[j], :] = _bf(dv_in[j] + _dot(k_e[j], dst_b, 1, 1))
                carried[j] = ds[j]["total8"] * jnp.sum(dst * sts[j], axis=0, keepdims=True)
                dst = dst * ds[j]["state"] + grow[j]
            kdk = [ke_f[j] * dke[j] for j in span]
            pos = [(q_m[j].astype(jnp.float32) * dqm[j] - k_m[j].astype(jnp.float32) * dkm[j]) + qe_f[j] * dqe[j]
                   for j in span]
            dgs = _exact_sums(rmat, [jnp.concatenate([pos[j], kdk[j]], axis=0) for j in span])
            for j in span:
                t, d = ts[j], ds[j]
                dg = dgs[j] + jnp.tile(carried[j], (CHUNK // 8, 1))
                dqs = dqm[j] * d["e_qm"] + dqe[j] * d["e_q"]
                dkk = dkm[j] * d["e_km"] + dke[j] * d["e_k"]
                df = dg / t["f"] - dkk
                dp_ref[1, rows[j], :] = _bf(df * (1.0 - lb) * (t["sg"] * (1.0 - t["sg"])))
                dp_ref[0, rows[j], :] = _bf(dqs * (t["sq"] * (1.0 + qvs[j] * (1.0 - t["sq"]))))
                dlb = dlb + df * (1.0 - t["sg"])
            return dst, dlb

        _, dlb = lax.fori_loop(0, NGRP, group, (jnp.zeros((HEAD, HEAD), jnp.float32),
                                                jnp.zeros((CHUNK, HEAD), jnp.float32)))
        dlb_ref[...] = jnp.sum(dlb, axis=0, keepdims=True)

    vec = pl.BlockSpec((1, HEAD), lambda h: (0, h))
    return pl.pallas_call(
        body, name="hgrn_bwd", grid=(NHEAD,),
        in_specs=[*_HEAD_SPECS,
                  pl.BlockSpec((2, HEAD), lambda h: (0, h)), vec,
                  pl.BlockSpec((T, HEAD), lambda h: (0, h)),
                  pl.BlockSpec((None, NCHUNK, HEAD, HEAD), lambda h: (h, 0, 0, 0)),
                  pl.BlockSpec((T, HEAD), lambda h: (0, NHEAD + h)), ANY, ANY],
        out_specs=(pl.BlockSpec((4, T, HEAD), lambda h: (0, 0, h)), vec, vec),
        out_shape=(pltpu.HBM((NSEG, T, D), jnp.bfloat16),
                   jax.ShapeDtypeStruct((1, D), jnp.float32), jax.ShapeDtypeStruct((1, D), jnp.float32)),
        scratch_shapes=[pltpu.VMEM((T, HEAD), jnp.float32)],
        input_output_aliases={9: 0},
        compiler_params=_params(("parallel",)),
    )(proj, proj, proj, proj, lb_logits, rec_g, o, states, dymix, dproj_in, token)


def _pool_bwd(proj, pool_w, pool_scale, dymix):
    def body(u_ref, pg_ref, w_ref, sc_ref, dy_ref, dp_ref, gw_ref, gs_ref):
        gidx = pl.program_id(0)
        u, pg = u_ref[...], pg_ref[...]
        d = _bf(_window_mean(_window_sum(u, gidx, _shift_down), gidx) - u)
        mixed = _dot(d, w_ref[...], 1, 0)
        spg = _sigmoid(pg)
        dyv = dy_ref[...]
        d_p = dyv * (pg * spg)
        dp_ref[1] = _bf(dyv * (mixed * sc_ref[...]) * (spg * (1.0 + pg * (1.0 - spg))))
        gs_ref[...] = jnp.sum(d_p * mixed, axis=0, keepdims=True)
        dmixed = _bf(d_p * sc_ref[...])
        gw_ref[...] = _dot(d, dmixed, 0, 0)
        dd = _dot(dmixed, w_ref[...], 1, 1)
        dp_ref[0] = _bf(_window_sum(_window_mean(dd, gidx), gidx, _shift_up) - dd)

    return pl.pallas_call(
        body, name="pool_bwd", grid=(NGROUP,),
        in_specs=[*_POOL_SPECS,
                  pl.BlockSpec((None, GROUP, GROUP), lambda g: (g, 0, 0)),
                  pl.BlockSpec((1, GROUP), lambda g: (0, g)),
                  pl.BlockSpec((T, GROUP), lambda g: (0, g))],
        out_specs=(pl.BlockSpec((2, T, GROUP), lambda g: (2, 0, g)),
                   pl.BlockSpec((None, GROUP, GROUP), lambda g: (g, 0, 0)),
                   pl.BlockSpec((1, GROUP), lambda g: (0, g))),
        out_shape=(pltpu.HBM((NSEG, T, D), jnp.bfloat16),
                   jax.ShapeDtypeStruct((NGROUP, GROUP, GROUP), jnp.float32),
                   jax.ShapeDtypeStruct((1, D), jnp.float32)),
        compiler_params=_params(("parallel",)),
    )(proj, proj, pool_w, pool_scale, dymix)


def _proj_bwd_w(place, ht, dproj):
    half = NTILE // 2

    def owner_chip(i, pr):
        return jnp.where(i < half, i // 3, (pr[1] + 1 + (i - half) // 3) % 4)

    def tile_of(i, pr):
        side = jnp.where(i < half, 1 - pr[2], pr[2])
        return 6 * owner_chip(i, pr) + 3 * side + i % 3

    def dproj_block(i, pr):
        j = tile_of(i, pr)
        return ((j // 4 + 4) % NSEG, 0, j % 4)

    def mine(i):
        return jnp.maximum(i, half)

    def body(place_ref, h_ref, dp_ref, sum_ref, own_ref, sendbuf, recvbuf, send_sems, recv_sems):
        i = pl.program_id(0)
        px, py, c, _ = _place()
        gw = _dot(h_ref[...], dp_ref[...], 1, 0)

        def to_sibling(slot):
            return pltpu.make_async_remote_copy(
                src_ref=sendbuf.at[slot], dst_ref=recvbuf.at[slot], send_sem=send_sems.at[slot],
                recv_sem=recv_sems.at[slot], device_id=(px, py, 1 - c), device_id_type=MESH)

        @pl.when(i < half)
        def _():
            sendbuf[i] = _bf(gw)
            to_sibling(i).start()

        @pl.when(i >= half)
        def _():
            slot = 3 * owner_chip(i, place_ref) + i % 3
            to_sibling(slot).wait_recv()
            total = gw + recvbuf[slot].astype(jnp.float32)
            sum_ref[...] = _bf(total)
            own_ref[...] = total

        @pl.when(i == NTILE - 1)
        def _():
            for slot in range(half):
                to_sibling(slot).wait_send()

    return pl.pallas_call(
        body, name="proj_bwd_w",
        grid_spec=pltpu.PrefetchScalarGridSpec(
            num_scalar_prefetch=1, grid=(NTILE,),
            in_specs=[pl.BlockSpec((D, T), lambda i, pr: (0, 0)),
                      pl.BlockSpec((None, T, TILE), lambda i, pr: dproj_block(i, pr))],
            out_specs=(pl.BlockSpec((None, None, D, TILE), lambda i, pr: (owner_chip(mine(i), pr), mine(i) % 3, 0, 0)),
                       pl.BlockSpec((None, D, TILE), lambda i, pr: (jnp.where(i < NTILE - 3, 0, i % 3), 0, 0))),
            scratch_shapes=[pltpu.VMEM((half, D, TILE), jnp.bfloat16), pltpu.VMEM((half, D, TILE), jnp.bfloat16),
                            pltpu.SemaphoreType.DMA((half,)), pltpu.SemaphoreType.DMA((half,))]),
        out_shape=(pltpu.HBM((4, 3, D, TILE), jnp.bfloat16), pltpu.HBM((3, D, TILE), jnp.float32)),
        compiler_params=_params(("arbitrary",)),
    )(place, ht, dproj)


def _proj_bwd_x(dproj, w_t, x, g1, dz, token):
    tm = 1024

    def body(dp_ref, w_ref, x_ref, g_ref, dz_ref, token_any, dx_ref, dg_ref, wcat, acc):
        del token_any
        m, s = pl.program_id(0), pl.program_id(1)
        for i in range(4):
            wcat[:, i * TILE:(i + 1) * TILE] = w_ref[i]
        r = _dot(dp_ref[...], wcat[...], 1, 1)

        @pl.when(s == 0)
        def _():
            acc[...] = r

        @pl.when(s != 0)
        def _():
            acc[...] += r

        @pl.when(s == NSEG - 1)
        def _():
            xv = x_ref[...]
            rs = lax.rsqrt(jnp.mean(xv * xv, axis=-1, keepdims=True) + EPS)
            xhat = xv * rs
            dhv = acc[...]
            gdh = dhv * g_ref[...]
            dx_ref[...] = dz_ref[...] + rs * (gdh - xhat * jnp.mean(xhat * gdh, axis=-1, keepdims=True))
            dg = jnp.sum(xhat * dhv, axis=0, keepdims=True)

            @pl.when(m == 0)
            def _():
                dg_ref[...] = dg

            @pl.when(m != 0)
            def _():
                dg_ref[...] += dg

    half = pl.BlockSpec((tm, D), lambda m, s: (m, 0))
    vec = pl.BlockSpec((1, D), lambda m, s: (0, 0))
    return pl.pallas_call(
        body, name="proj_bwd_x", grid=(T // tm, NSEG),
        in_specs=[pl.BlockSpec((None, tm, D), lambda m, s: (s, m, 0)),
                  pl.BlockSpec((4, D, TILE), lambda m, s: (_seg_tiles(s), 0, 0)), half, vec, half, ANY],
        out_specs=(half, vec),
        out_shape=(jax.ShapeDtypeStruct((T, D), jnp.float32), jax.ShapeDtypeStruct((1, D), jnp.float32)),
        scratch_shapes=[pltpu.VMEM((D, D), jnp.bfloat16), pltpu.VMEM((tm, D), jnp.float32)],
        compiler_params=_params(("arbitrary", "arbitrary"), vmem_mib=56),
    )(dproj, w_t, x, g1, dz, token)


def _adamw(w, g, m, v):
    m_new = ADAM_B1 * m + (1.0 - ADAM_B1) * g
    v_new = ADAM_B2 * v + (1.0 - ADAM_B2) * (g * g)
    delta = -ADAM_LR * ((m_new / BC1) / (jnp.sqrt(v_new / BC2) + ADAM_EPS) + ADAM_WD * w)
    return delta, m_new, v_new


def _reduce_adam(name, place, parts, w, m, v, grid, w_spec):
    n = len(parts)

    def body(place_ref, *refs):
        del place_ref
        w_ref, m_ref, v_ref, g_ref, d_ref, mo_ref, vo_ref = refs[n:]
        g = None
        for ref, (_, _, stacked) in zip(refs[:n], parts):
            terms = [ref[r] for r in range(ref.shape[0])] if stacked else [ref[...]]
            for t in terms:
                g = t.astype(jnp.float32) if g is None else g + t.astype(jnp.float32)
        delta, m_new, v_new = _adamw(w_ref[...], g, m_ref[...], v_ref[...])
        g_ref[...] = g
        d_ref[...] = delta
        mo_ref[...] = m_new
        vo_ref[...] = v_new

    shape = jax.ShapeDtypeStruct(w.shape, jnp.float32)
    return pl.pallas_call(
        body, name=name,
        grid_spec=pltpu.PrefetchScalarGridSpec(
            num_scalar_prefetch=1, grid=grid,
            in_specs=[spec for _, spec, _ in parts] + [w_spec] * 3, out_specs=(w_spec,) * 4),
        out_shape=(shape,) * 4,
        compiler_params=_params(("parallel",)),
    )(place, *[_in_hbm(a) for a in [a for a, _, _ in parts] + [w, m, v]])


def _small_adam(parts, w, m, v):
    def body(p_ref, w_ref, m_ref, v_ref, g_ref, d_ref, mo_ref, vo_ref):
        g = p_ref[0]
        for s in range(1, NDEV):
            g = g + p_ref[s]
        wv = w_ref[...]
        rows = _row_ids(wv.shape)
        other = jnp.where(rows == 2, pltpu.roll(wv, 7, 0), jnp.where(rows == 3, pltpu.roll(wv, 1, 0), 0.0))
        lbv = _sigmoid(wv - other)
        sign = jnp.where(rows == 2, 1.0, -1.0)
        g = jnp.where((rows == 2) | (rows == 3), sign * g * lbv * (1.0 - lbv), g)
        delta, m_new, v_new = _adamw(wv, g, m_ref[...], v_ref[...])
        g_ref[...] = g
        d_ref[...] = delta
        mo_ref[...] = m_new
        vo_ref[...] = v_new

    shape = jax.ShapeDtypeStruct((8, D), jnp.float32)
    return pl.pallas_call(body, name="small_adam", out_shape=(shape,) * 4)(parts, w, m, v)


def _rows8(*vecs):
    rows = [a.reshape(-1, D) for a in vecs]
    n = sum(r.shape[0] for r in rows)
    return jnp.concatenate(rows + [jnp.zeros((8 - n, D), jnp.float32)], axis=0)


def kernel(x, norm1_g, w_in, pool_w, pool_scale, lb_logits, rec_norm_g, w_out, final_norm_g, loss_target, m_norm1_g, m_w_in, m_pool_w, m_pool_scale, m_lb_logits, m_rec_norm_g, m_w_out, m_final_norm_g, v_norm1_g, v_w_in, v_pool_w, v_pool_scale, v_lb_logits, v_rec_norm_g, v_w_out, v_final_norm_g):
    xs = x[0]
    target = loss_target[0]
    ix, iy, ic = lax.axis_index("x"), lax.axis_index("y"), lax.axis_index("c")
    place = jnp.stack([4 * ix + 2 * iy + ic, 2 * ix + iy, ic]).astype(jnp.int32)
    gf = final_norm_g.reshape(1, D)

    ht, w_t, w_out_b, w_out_g, pool_g, proj = _gather_proj(xs, norm1_g, w_in, w_out, pool_w)
    pool_full = pool_g.transpose(1, 0, 2, 3).reshape(NGROUP, GROUP, GROUP)
    wout = [w_out_b, w_out_g]
    wout_send, wout_recv, wout, wout_token = _split_start("gather_wout_start", wout, NDEV - 1, _plan_wout)

    y = _pool_fwd(proj, pool_full, pool_scale, wout_token)
    y, o, states = _hgrn_fwd(proj, lb_logits, rec_norm_g, y)
    _, w_out_g = _split_wait("gather_wout_wait", wout, wout_send, wout_recv, _plan_wout, o)
    w_out_full = _in_hbm(w_out_g.reshape(DMIX, D))
    dz, dzb, sq, dgf = _out_proj_loss(xs, y, w_out_full, target, gf)

    dymix, gwout_f, gwout_b = _out_proj_bwd(dzb, w_out_full, y)
    dproj, gpool, dscale = _pool_bwd(proj, pool_full, pool_scale, dymix)

    blk_out = (NDEV, DMIX // NDEV, D)
    blk_pool = (NDEV, NGROUP, GROUP // NDEV, GROUP)
    gpool_s = gpool.reshape(NGROUP, NDEV, GROUP // NDEV, GROUP).transpose(1, 0, 2, 3)
    rest = [gwout_b.reshape(blk_out), gpool_s,
            lax.empty((NDEV - 1,) + blk_out[1:], jnp.bfloat16), lax.empty((NDEV - 1,) + blk_pool[1:], jnp.float32)]
    rest_send, rest_recv, rest, rest_token = _split_start("scatter_rest_start", rest, 2 * (NDEV - 1), _plan_rest)

    dproj, drecg, dlb = _hgrn_bwd(proj, lb_logits, rec_norm_g, o, states, dymix, dproj, rest_token)
    chip_sums, own_sum = _proj_bwd_w(place, ht, dproj)
    win = [chip_sums, lax.empty((3, 3, D, TILE), jnp.bfloat16)]
    win_send, win_recv, win, win_token = _split_start("scatter_win_start", win, 3, _plan_in)

    grad_x, dg1 = _proj_bwd_x(dproj, w_t, xs, norm1_g, dz, win_token)

    _, gpool_own, r_out, r_pool = _split_wait("scatter_rest_wait", rest, rest_send, rest_recv, _plan_rest, grad_x)
    g_wout, d_wout, m_wout, v_wout = _reduce_adam(
        "adam_w_out", place,
        [(gwout_f.reshape(blk_out), pl.BlockSpec((None,) + blk_out[1:], lambda i, pr: (pr[0], 0, 0)), False),
         (r_out, pl.BlockSpec((NDEV - 1,) + blk_out[1:], lambda i, pr: (0, 0, 0)), True)],
        w_out, m_w_out, v_w_out, (1,), pl.BlockSpec((None,) + blk_out[1:], lambda i, pr: (0, 0, 0)))
    g_pool, d_pool, m_pool, v_pool = _reduce_adam(
        "adam_pool_w", place,
        [(gpool_own, pl.BlockSpec((None,) + blk_pool[1:], lambda i, pr: (pr[0], 0, 0, 0)), False),
         (r_pool, pl.BlockSpec((NDEV - 1,) + blk_pool[1:], lambda i, pr: (0, 0, 0, 0)), True)],
        pool_w, m_pool_w, v_pool_w, (1,), pl.BlockSpec((None,) + blk_pool[1:], lambda i, pr: (0, 0, 0, 0)))

    r_small = _gather_small(_rows8(dg1, dscale, dlb, dlb, drecg, dgf, sq), d_wout, d_pool)
    loss = jnp.sum(r_small[:, 6, :]) * (0.5 / D)
    g_s, d_s, m_s, v_s = _small_adam(
        r_small,
        _rows8(norm1_g, pool_scale, lb_logits, rec_norm_g, final_norm_g),
        _rows8(m_norm1_g, m_pool_scale, m_lb_logits, m_rec_norm_g, m_final_norm_g),
        _rows8(v_norm1_g, v_pool_scale, v_lb_logits, v_rec_norm_g, v_final_norm_g))

    _, r_in = _split_wait("scatter_win_wait", win, win_send, win_recv, _plan_in, v_s)
    g_win, d_win, m_win, v_win = _reduce_adam(
        "adam_w_in", place,
        [(own_sum, pl.BlockSpec((None, D, TILE), lambda i, pr: (i, 0, 0)), False),
         (r_in, pl.BlockSpec((3, None, D, TILE), lambda i, pr: (0, i, 0, 0)), True)],
        w_in, m_w_in, v_w_in, (3,), pl.BlockSpec((None, D, TILE), lambda i, pr: (0, 0, i)))

    def small_outs(a):
        return a[0:1], a[1:2], a[2:4], a[4:5], a[5]

    def outs(small_a, win, pool, wout):
        n1, ps, lbl, rg, fg = small_outs(small_a)
        return n1, win, pool, ps, lbl, rg, wout, fg

    return (loss, grad_x[None],
            *outs(g_s, g_win, g_pool, g_wout), *outs(d_s, d_win, d_pool, d_wout),
            *outs(m_s, m_win, m_pool, m_wout), *outs(v_s, v_win, v_pool, v_wout))
```

```python
import functools

import jax
import jax.numpy as jnp
from jax import lax
from jax.experimental import pallas as pl
from jax.experimental.pallas import tpu as pltpu

T = 2048
D = 1024
NSEG = 6
NTILE = 24
TILE = 256
DMIX = 2048
NDEV = 8
HEAD = 128
NHEAD = 8
CHUNK = 64
NCHUNK = T // CHUNK
NB = 32
NGRP = NCHUNK // NB
NGROUP = 4
GROUP = 256
EPS = 1e-6
EXP_CAP = 80.0
MESH = pl.DeviceIdType.MESH
AXES = ("x", "y", "c")
ANY = pl.BlockSpec(memory_space=pl.ANY)
HBM = pl.BlockSpec(memory_space=pltpu.HBM)
SEM = pl.BlockSpec(memory_space=pltpu.SEMAPHORE)
EFFECT = pltpu.SideEffectType.DATAFLOW_SIDE_EFFECTING

ADAM_LR = 0.001
ADAM_B1 = 0.9
ADAM_B2 = 0.999
ADAM_EPS = 1e-08
ADAM_WD = 0.01
ADAM_STEP = 10
BC1 = 1.0 - ADAM_B1 ** ADAM_STEP
BC2 = 1.0 - ADAM_B2 ** ADAM_STEP

MIB = 1 << 20


def _params(sem=None, vmem_mib=48):
    return pltpu.CompilerParams(dimension_semantics=sem, vmem_limit_bytes=vmem_mib * MIB)


def _sigmoid(v):
    return 1.0 / (1.0 + jnp.exp(-v))


def _dot(a, b, ca, cb, precision=None):
    return lax.dot_general(a, b, (((ca,), (cb,)), ((), ())), precision=precision,
                           preferred_element_type=jnp.float32)


def _bf(v):
    return v.astype(jnp.bfloat16)


def _in_hbm(a):
    return pltpu.with_memory_space_constraint(a, pltpu.HBM)


def _place():
    x, y, c = lax.axis_index("x"), lax.axis_index("y"), lax.axis_index("c")
    return x, y, c, 4 * x + 2 * y + c


def _peer(x, y, c, r):
    return (x ^ ((r >> 2) & 1), y ^ ((r >> 1) & 1), c ^ (r & 1))


def _gather_proj(x, g1, w_in, w_out, pool_w):
    def body(x_ref, g_ref, win_ref, wout_ref, pool_ref, ht_o, wt_o, woutb_o, wout_o, pool_o, proj_o,
             hv, htv, wv, wob, pb, stage, send_sems, recv_sems, loc_sems, out_sems):
        px, py, c, my_idx = _place()
        me, sibling = (px, py, c), (px, py, 1 - c)
        chips = [(1 - px, py), (px, 1 - py), (1 - px, 1 - py)]
        for p in range(3):
            wv[3 * my_idx + p] = _bf(win_ref[0, :, p * TILE:(p + 1) * TILE])
        pb[...] = _bf(pool_ref[0])
        wob[...] = _bf(wout_ref[0])

        def index(bx, by, bc):
            return 4 * bx + 2 * by + bc

        def slot(w, block):
            return wv.at[pl.ds(3 * index(*block), 3)] if w == 0 else pool_o.at[index(*block)]

        def copy(k, w, block, to, src=None):
            return pltpu.make_async_remote_copy(
                src_ref=slot(w, block) if src is None else src, dst_ref=slot(w, block),
                send_sem=send_sems.at[2 * k + w], recv_sem=recv_sems.at[2 * k + w],
                device_id=to, device_id_type=MESH)

        def save(block):
            at = pl.ds(3 * index(*block), 3)
            pltpu.make_async_copy(wv.at[at], wt_o.at[at], loc_sems.at[4]).start()

        srcs = (slot(0, me), pb)
        first = []
        for w in (0, 1):
            first += [copy(1 + j, w, me, (*chip, c), src=srcs[w]) for j, chip in enumerate(chips[:2])]
            first.append(copy(0, w, me, sibling, src=srcs[w]))
        for cp in first:
            cp.start()
        save(me)
        locs = [pltpu.make_async_copy(pb, slot(1, me), loc_sems.at[0]),
                pltpu.make_async_copy(wob, wout_o.at[my_idx], loc_sems.at[1]),
                pltpu.make_async_copy(wob, woutb_o, loc_sems.at[2])]
        for cp in locs:
            cp.start()

        xv = x_ref[...]
        hv[...] = _bf(xv * lax.rsqrt(jnp.mean(xv * xv, axis=-1, keepdims=True) + EPS) * g_ref[...])
        rows = 256
        for r0 in range(0, T, rows):
            htv[:, r0:r0 + rows] = hv[r0:r0 + rows, :].T
        locs.append(pltpu.make_async_copy(htv, ht_o, loc_sems.at[3]))
        locs[-1].start()

        def out_copy(p, j):
            return pltpu.make_async_copy(stage.at[p], proj_o.at[j], out_sems.at[p])

        def project(nth, block):
            base = 3 * index(*block)

            def tile(p, carry):
                if nth > 0:
                    out_copy(p, base + p).wait()
                stage[p] = _dot(hv[...], wv[base + p], 1, 0)
                out_copy(p, base + p).start()
                return carry

            lax.fori_loop(0, 3, tile, 0)

        project(0, me)
        copy(0, 0, sibling, me).wait_recv()
        save(sibling)
        project(1, sibling)
        passed = []
        relay_from = (px ^ (1 - c), py ^ c, c)
        relay_to = (px ^ c, py ^ (1 - c), c)

        def arrived(w, j):
            copy(1 + j, w, (*chips[j], c), me).wait_recv()
            passed.append(copy(4 + j, w, (*chips[j], c), sibling))
            passed[-1].start()

        def relay(w):
            passed.append(copy(3, w, relay_from, relay_to))
            passed[-1].start()

        def handed(nth, j):
            copy(4 + j, 0, (*chips[j], 1 - c), me).wait_recv()
            save((*chips[j], 1 - c))
            project(nth, (*chips[j], 1 - c))

        arrived(0, 0)
        arrived(0, 1)
        relay(0)
        for j in range(2):
            save((*chips[j], c))
            project(2 + j, (*chips[j], c))
        handed(4, 0)
        handed(5, 1)
        arrived(1, 0)
        arrived(1, 1)
        relay(1)
        arrived(0, 2)
        save((*chips[2], c))
        project(6, (*chips[2], c))
        handed(7, 2)
        arrived(1, 2)
        copy(0, 1, sibling, me).wait_recv()
        for j, chip in enumerate(chips):
            copy(4 + j, 1, (*chip, 1 - c), me).wait_recv()
        keep = pltpu.make_async_copy(wv, wt_o, loc_sems.at[4])
        for p in range(3):
            out_copy(p, p).wait()
        for cp in first + passed:
            cp.wait_send()
        keep.wait()
        for cp in locs:
            cp.wait()

    vmem = pl.BlockSpec(memory_space=pltpu.VMEM)
    bf16 = jnp.bfloat16
    return pl.pallas_call(
        body, name="gather_proj",
        out_shape=(pltpu.HBM((D, T), bf16), pltpu.HBM((NTILE, D, TILE), bf16),
                   pltpu.HBM((DMIX // NDEV, D), bf16), pltpu.HBM((NDEV, DMIX // NDEV, D), bf16),
                   pltpu.HBM((NDEV, NGROUP, GROUP // NDEV, GROUP), bf16), pltpu.HBM((NTILE, T, TILE), jnp.float32)),
        in_specs=[vmem] * 5, out_specs=(ANY,) * 6,
        scratch_shapes=[pltpu.VMEM((T, D), bf16), pltpu.VMEM((D, T), bf16), pltpu.VMEM((NTILE, D, TILE), bf16),
                        pltpu.VMEM((DMIX // NDEV, D), bf16), pltpu.VMEM((NGROUP, GROUP // NDEV, GROUP), bf16),
                        pltpu.VMEM((3, T, TILE), jnp.float32),
                        pltpu.SemaphoreType.DMA((14,)), pltpu.SemaphoreType.DMA((14,)),
                        pltpu.SemaphoreType.DMA((5,)), pltpu.SemaphoreType.DMA((3,))],
        compiler_params=_params(vmem_mib=56),
    )(x, g1, w_in, w_out, pool_w)


def _split_start(name, arrays, n_copies, plan):
    k = len(arrays)

    def body(*refs):
        send_sems, recv_sems, token = refs[k], refs[k + 1], refs[-1]
        for i, (src, dst, to) in enumerate(plan(refs[:k])):
            pltpu.make_async_remote_copy(src_ref=src, dst_ref=dst, send_sem=send_sems.at[i],
                                         recv_sem=recv_sems.at[i], device_id=to, device_id_type=MESH).start()
        token[...] = jnp.zeros_like(token)

    out = pl.pallas_call(
        body, name=name,
        out_shape=(pltpu.SemaphoreType.DMA((n_copies,)), pltpu.SemaphoreType.DMA((n_copies,)),
                   *[pltpu.HBM(a.shape, a.dtype) for a in arrays], jax.ShapeDtypeStruct((8, 128), jnp.float32)),
        in_specs=[HBM] * k, out_specs=(SEM, SEM, *[HBM] * k, pl.BlockSpec(memory_space=pltpu.VMEM)),
        input_output_aliases={i: 2 + i for i in range(k)},
        compiler_params=pltpu.CompilerParams(has_side_effects=EFFECT),
    )(*[pltpu.with_memory_space_constraint(a, pltpu.HBM) for a in arrays])
    return out[0], out[1], out[2:2 + k], out[-1]


def _split_wait(name, arrays, send_sems, recv_sems, plan, after):
    k = len(arrays)

    def body(*refs):
        sends, recvs = refs[k], refs[k + 1]
        for i, (src, dst, to) in enumerate(plan(refs[:k])):
            cp = pltpu.make_async_remote_copy(src_ref=src, dst_ref=dst, send_sem=sends.at[i], recv_sem=recvs.at[i],
                                              device_id=to, device_id_type=MESH)
            cp.wait_send()
            cp.wait_recv()

    return pl.pallas_call(
        body, name=name,
        out_shape=tuple(pltpu.HBM(a.shape, a.dtype) for a in arrays),
        in_specs=[HBM] * k + [SEM, SEM, ANY], out_specs=(HBM,) * k,
        input_output_aliases={i: i for i in range(k)},
        compiler_params=pltpu.CompilerParams(has_side_effects=EFFECT),
    )(*arrays, send_sems, recv_sems, after)


def _plan_wout(refs):
    src, land = refs
    x, y, c, me = _place()
    return [(src, land.at[me], _peer(x, y, c, r)) for r in range(1, NDEV)]


def _plan_rest(refs):
    gob, gpf, r_out, r_pool = refs
    x, y, c, me = _place()
    plan = []
    for r in range(1, NDEV):
        plan.append((gob.at[me ^ r], r_out.at[r - 1], _peer(x, y, c, r)))
        plan.append((gpf.at[me ^ r], r_pool.at[r - 1], _peer(x, y, c, r)))
    return plan


def _plan_in(refs):
    sum_b, r_in = refs
    x, y, c, _ = _place()
    plan = []
    for j, (dx, dy) in enumerate(((1, 0), (0, 1), (1, 1))):
        px, py = x ^ dx, y ^ dy
        plan.append((sum_b.at[2 * px + py], r_in.at[j], (px, py, c)))
    return plan


def _gather_small(small, *after):
    def body(sm, *refs):
        r_small, send_sems, recv_sems, loc_sem = refs[len(after):]
        x, y, c, me = _place()
        loc = pltpu.make_async_copy(sm, r_small.at[me], loc_sem)
        loc.start()

        def copy(r, src_idx):
            return pltpu.make_async_remote_copy(
                src_ref=sm, dst_ref=r_small.at[src_idx], send_sem=send_sems.at[r - 1], recv_sem=recv_sems.at[r - 1],
                device_id=_peer(x, y, c, r), device_id_type=MESH)

        sends = [copy(r, me) for r in range(1, NDEV)]
        for cp in sends:
            cp.start()
        for r in range(1, NDEV):
            copy(r, me ^ r).wait_recv()
        for cp in sends:
            cp.wait_send()
        loc.wait()

    return pl.pallas_call(
        body, name="gather_small",
        out_shape=jax.ShapeDtypeStruct((NDEV, 8, D), jnp.float32),
        in_specs=[ANY] * (1 + len(after)), out_specs=ANY,
        scratch_shapes=[pltpu.SemaphoreType.DMA((NDEV - 1,)), pltpu.SemaphoreType.DMA((NDEV - 1,)),
                        pltpu.SemaphoreType.DMA],
    )(small, *after)


def _seg_tiles(s):
    return (s + 2) % NSEG


_POOL_SPECS = [pl.BlockSpec((None, T, GROUP), lambda g, base=base: (base + g, 0, 0)) for base in (0, 4)]
_HEAD_SPECS = [pl.BlockSpec((None, T, HEAD), lambda h, base=base: (base + h // 2, 0, h % 2))
               for base in (8, 12, 16, 20)]


def _row_ids(shape):
    return lax.broadcasted_iota(jnp.int32, shape, 0)


EDGE = 8


def _shift_down(a, k):
    r = pltpu.roll(a, k, 0)
    edge = _row_ids((EDGE, a.shape[1]))
    return jnp.concatenate([jnp.where(edge >= k, r[:EDGE], 0.0), r[EDGE:]], axis=0)


def _shift_up(a, k):
    r = pltpu.roll(a, T - k, 0)
    edge = _row_ids((EDGE, a.shape[1]))
    return jnp.concatenate([r[:T - EDGE], jnp.where(edge < EDGE - k, r[T - EDGE:], 0.0)], axis=0)


def _window_sum(u, gidx, shift):
    s2 = u + shift(u, 1)
    s4 = s2 + shift(s2, 2)
    s8 = s4 + shift(s4, 4)
    s16 = s8 + shift(s8, 8)
    return jnp.where(gidx == 0, s2, jnp.where(gidx == 1, s4, jnp.where(gidx == 2, s8, s16)))


def _window_mean(s, gidx):
    inv = jnp.where(gidx == 0, 0.5, jnp.where(gidx == 1, 0.25, jnp.where(gidx == 2, 0.125, 0.0625)))
    width = lax.shift_left(jnp.int32(2), gidx)
    head = s[:16] / jnp.minimum(_row_ids((16, s.shape[1])) + 1, width).astype(jnp.float32)
    return jnp.concatenate([head, s[16:] * inv], axis=0)


def _pool_fwd(proj, pool_w, pool_scale, token):
    def body(u_ref, pg_ref, w_ref, sc_ref, token_any, y_ref):
        del token_any
        gidx = pl.program_id(0)
        u, pg = u_ref[...], pg_ref[...]
        d = _window_mean(_window_sum(u, gidx, _shift_down), gidx) - u
        mixed = _dot(_bf(d), w_ref[...], 1, 0)
        y_ref[...] = _bf(mixed * sc_ref[...] * (pg * _sigmoid(pg)))

    return pl.pallas_call(
        body, name="pool_fwd", grid=(NGROUP,),
        in_specs=[*_POOL_SPECS,
                  pl.BlockSpec((None, GROUP, GROUP), lambda g: (g, 0, 0)),
                  pl.BlockSpec((1, GROUP), lambda g: (0, g)), ANY],
        out_specs=pl.BlockSpec((T, GROUP), lambda g: (0, g)),
        out_shape=pltpu.HBM((T, DMIX), jnp.bfloat16),
        compiler_params=_params(("parallel",)),
    )(proj, proj, pool_w, pool_scale, token)


def _tri(lower):
    r = lax.broadcasted_iota(jnp.int32, (CHUNK, CHUNK), 0)
    c = lax.broadcasted_iota(jnp.int32, (CHUNK, CHUNK), 1)
    return (r >= c) if lower else (r <= c)


def _sum_rows_matrix():
    shape = (CHUNK + 16, CHUNK)
    r, c = lax.broadcasted_iota(jnp.int32, shape, 0), lax.broadcasted_iota(jnp.int32, shape, 1)
    run = jnp.where(c <= r, 1.0, 0.0)
    half = jnp.where(c < CHUNK // 2, 1.0, 0.0)
    return _bf(jnp.where(r < CHUNK, run, jnp.where(r < CHUNK + 8, 1.0, half)))


def _rev_sum_matrix():
    shape = (CHUNK, 2 * CHUNK)
    r, c = lax.broadcasted_iota(jnp.int32, shape, 0), lax.broadcasted_iota(jnp.int32, shape, 1)
    return _bf(jnp.where(c < CHUNK, jnp.where(c >= r, 1.0, 0.0), jnp.where(c - CHUNK < r, 1.0, 0.0)))


def _split2(a):
    hi = _bf(a)
    return [hi, _bf(a - hi.astype(jnp.float32))]


def _exact_sums(mat, pieces):
    x = jnp.concatenate([s for p in pieces for s in _split2(p)], axis=1)
    r = _dot(mat, x, 1, 0)
    return [r[:, 2 * j * HEAD:(2 * j + 1) * HEAD] + r[:, (2 * j + 1) * HEAD:(2 * j + 2) * HEAD]
            for j in range(len(pieces))]


def _gates(qv, fl, lb):
    sq = _sigmoid(qv)
    sg = _sigmoid(fl)
    f = lb + (1.0 - lb) * sg
    return dict(sq=sq, qs=qv * sq, sg=sg, f=f, kk=1.0 - f, g=jnp.log(f))


def _decays(sums):
    big_g = sums[:CHUNK]
    total = sums[CHUNK:CHUNK + 8]
    g_last = jnp.tile(total, (CHUNK // 8, 1))
    g_mid = jnp.tile(sums[CHUNK + 8:], (CHUNK // 8, 1))
    return dict(
        e_q=jnp.exp(big_g),
        e_k=jnp.exp(g_last - big_g),
        e_qm=jnp.exp(jnp.minimum(big_g - g_mid, EXP_CAP)),
        e_km=jnp.exp(jnp.minimum(g_mid - big_g, EXP_CAP)),
        total8=jnp.exp(total),
        state=jnp.exp(jnp.tile(total, (HEAD // 8, 1))))


def _group_rows(gi):
    return [pl.ds(pl.multiple_of((gi * NB + j) * CHUNK, CHUNK), CHUNK) for j in range(NB)]


def _lower_bound(lb_ref):
    return _sigmoid(lb_ref[0:1, :] - lb_ref[1:2, :])


def _hgrn_fwd(proj, lb_logits, rec_g, y_in):
    def body(q_ref, f_ref, i_ref, gate_ref, lb_ref, rg_ref, y_any, y_ref, o_ref, st_ref):
        del y_any
        lb = _lower_bound(lb_ref)
        causal = _tri(True)
        smat = _sum_rows_matrix()

        def group(gi, st):
            rows = _group_rows(gi)
            ts = [_gates(q_ref[r, :], f_ref[r, :], lb) for r in rows]
            ds = [_decays(s) for s in _exact_sums(smat, [t["g"] for t in ts])]
            vs = [_bf(i_ref[r, :]) for r in rows]
            q_m = [_bf(t["qs"] * d["e_qm"]) for t, d in zip(ts, ds)]
            k_m = [_bf(t["kk"] * d["e_km"]) for t, d in zip(ts, ds)]
            q_e = [_bf(t["qs"] * d["e_q"]) for t, d in zip(ts, ds)]
            k_e = [_bf(t["kk"] * d["e_k"]) for t, d in zip(ts, ds)]
            a = [_bf(jnp.where(causal, _dot(q_m[j], k_m[j], 1, 1), 0.0)) for j in range(NB)]
            intra = [_dot(a[j], vs[j], 1, 0) for j in range(NB)]
            upd = [_dot(vs[j], k_e[j], 0, 0) for j in range(NB)]
            for j in range(NB):
                st_ref[gi * NB + j] = st
                o_ref[rows[j], :] = intra[j] + _dot(q_e[j], _bf(st), 1, 1)
                st = st * ds[j]["state"] + upd[j]
            return st

        lax.fori_loop(0, NGRP, group, jnp.zeros((HEAD, HEAD), jnp.float32))
        o = o_ref[...]
        rn = o * lax.rsqrt(jnp.mean(o * o, axis=-1, keepdims=True) + EPS)
        gate = gate_ref[...]
        y_ref[...] = _bf(rn * rg_ref[...] * (gate * _sigmoid(gate)))

    return pl.pallas_call(
        body, name="hgrn_fwd", grid=(NHEAD,),
        in_specs=[*_HEAD_SPECS,
                  pl.BlockSpec((2, HEAD), lambda h: (0, h)),
                  pl.BlockSpec((1, HEAD), lambda h: (0, h)),
                  pl.BlockSpec(memory_space=pl.ANY)],
        out_specs=(pl.BlockSpec((T, HEAD), lambda h: (0, NHEAD + h)),
                   pl.BlockSpec((T, HEAD), lambda h: (0, h)),
                   pl.BlockSpec((None, NCHUNK, HEAD, HEAD), lambda h: (h, 0, 0, 0))),
        out_shape=(pltpu.HBM((T, DMIX), jnp.bfloat16), pltpu.HBM((T, D), jnp.float32),
                   pltpu.HBM((NHEAD, NCHUNK, HEAD, HEAD), jnp.float32)),
        input_output_aliases={6: 0},
        compiler_params=_params(("parallel",)),
    )(proj, proj, proj, proj, lb_logits, rec_g, y_in)


def _out_proj_loss(x, y, w_out, target, gf):
    rows = 256

    def body(x_ref, y_ref, w_ref, t_ref, g_ref, dz_ref, dzb_ref, sq_ref, dg_ref):
        z = x_ref[...] + _dot(y_ref[...], w_ref[...], 1, 0)
        r = lax.rsqrt(jnp.mean(z * z, axis=-1, keepdims=True) + EPS)
        zhat = z * r
        err = zhat * g_ref[...] - t_ref[...]
        dy = err * (1.0 / D)
        gdy = dy * g_ref[...]
        dz = r * (gdy - zhat * jnp.mean(zhat * gdy, axis=-1, keepdims=True))
        dz_ref[...] = dz
        dzb_ref[...] = _bf(dz)
        sq = jnp.sum(err * err, axis=0, keepdims=True)
        dg = jnp.sum(zhat * dy, axis=0, keepdims=True)

        @pl.when(pl.program_id(0) == 0)
        def _():
            sq_ref[...] = sq
            dg_ref[...] = dg

        @pl.when(pl.program_id(0) != 0)
        def _():
            sq_ref[...] += sq
            dg_ref[...] += dg

    tile = pl.BlockSpec((rows, D), lambda i: (i, 0))
    vec = pl.BlockSpec((1, D), lambda i: (0, 0))
    return pl.pallas_call(
        body, name="out_proj_loss", grid=(T // rows,),
        in_specs=[tile, pl.BlockSpec((rows, DMIX), lambda i: (i, 0)), pl.BlockSpec((DMIX, D), lambda i: (0, 0)),
                  tile, vec],
        out_specs=(tile, tile, vec, vec),
        out_shape=(pltpu.HBM((T, D), jnp.float32), pltpu.HBM((T, D), jnp.bfloat16),
                   jax.ShapeDtypeStruct((1, D), jnp.float32), jax.ShapeDtypeStruct((1, D), jnp.float32)),
        compiler_params=_params(("arbitrary",)),
    )(x, y, w_out, target, gf)


def _out_proj_bwd(dzb, w_out, y):
    tn = 512

    def body(dz_ref, w_ref, y_ref, dy_ref, gw_ref, gwb_ref):
        dz = dz_ref[...]
        dy_ref[...] = _dot(dz, w_ref[...], 1, 1)
        gw = _dot(y_ref[...], dz, 0, 0)
        gw_ref[...] = gw
        gwb_ref[...] = _bf(gw)

    return pl.pallas_call(
        body, name="out_proj_bwd", grid=(DMIX // tn,),
        in_specs=[pl.BlockSpec((T, D), lambda n: (0, 0)), pl.BlockSpec((tn, D), lambda n: (n, 0)),
                  pl.BlockSpec((T, tn), lambda n: (0, n))],
        out_specs=(pl.BlockSpec((T, tn), lambda n: (0, n)), pl.BlockSpec((tn, D), lambda n: (n, 0)),
                   pl.BlockSpec((tn, D), lambda n: (n, 0))),
        out_shape=(pltpu.HBM((T, DMIX), jnp.float32), pltpu.HBM((DMIX, D), jnp.float32),
                   pltpu.HBM((DMIX, D), jnp.bfloat16)),
        compiler_params=_params(("parallel",)),
    )(dzb, w_out, y)


def _hgrn_bwd(proj, lb_logits, rec_g, o, states, dymix, dproj_in, token):
    def body(q_ref, f_ref, i_ref, gate_ref, lb_ref, rg_ref, o_ref, st_ref, dy_ref, dp_any, token_any,
             dp_ref, drg_ref, dlb_ref, do_ref):
        del dp_any, token_any
        lb = _lower_bound(lb_ref)
        causal = _tri(True)
        smat, rmat = _sum_rows_matrix(), _rev_sum_matrix()

        o = o_ref[...]
        rs = lax.rsqrt(jnp.mean(o * o, axis=-1, keepdims=True) + EPS)
        rn = o * rs
        gate = gate_ref[...]
        sgate = _sigmoid(gate)
        dyv = dy_ref[...]
        d_r = dyv * (gate * sgate)
        dp_ref[3] = _bf(dyv * (rn * rg_ref[...]) * (sgate * (1.0 + gate * (1.0 - sgate))))
        drg_ref[...] = jnp.sum(d_r * rn, axis=0, keepdims=True)
        drn = d_r * rg_ref[...]
        do_ref[...] = rs * (drn - rn * jnp.mean(rn * drn, axis=-1, keepdims=True))

        def group(i, carry):
            dst, dlb = carry
            gi = NGRP - 1 - i
            rows = _group_rows(gi)
            span = range(NB)
            qvs = [q_ref[r, :] for r in rows]
            ts = [_gates(qv, f_ref[r, :], lb) for qv, r in zip(qvs, rows)]
            ds = [_decays(s) for s in _exact_sums(smat, [t["g"] for t in ts])]
            vs = [_bf(i_ref[r, :]) for r in rows]
            dos = [_bf(do_ref[r, :]) for r in rows]
            sts = [st_ref[gi * NB + j] for j in span]
            qe_f = [t["qs"] * d["e_q"] for t, d in zip(ts, ds)]
            ke_f = [t["kk"] * d["e_k"] for t, d in zip(ts, ds)]
            q_e, k_e = [_bf(a) for a in qe_f], [_bf(a) for a in ke_f]
            q_m = [_bf(t["qs"] * d["e_qm"]) for t, d in zip(ts, ds)]
            k_m = [_bf(t["kk"] * d["e_km"]) for t, d in zip(ts, ds)]
            a = [_bf(jnp.where(causal, _dot(q_m[j], k_m[j], 1, 1), 0.0)) for j in span]
            da = [_bf(jnp.where(causal, _dot(dos[j], vs[j], 1, 1), 0.0)) for j in span]
            dqm = [_dot(da[j], k_m[j], 1, 0) for j in span]
            dkm = [_dot(da[j], q_m[j], 0, 0) for j in span]
            dv_in = [_dot(a[j], dos[j], 0, 0) for j in span]
            dqe = [_dot(dos[j], _bf(sts[j]), 1, 0) for j in span]
            grow = [_dot(dos[j], q_e[j], 0, 0) for j in span]
            dke, carried = [None] * NB, [None] * NB
            for j in reversed(span):
                dst_b = _bf(dst)
                dke[j] = _dot(vs[j], dst_b, 1, 0)
                dp_ref[2, rows[j], :] = _bf(dv_in[j] + _dot(k_e[j], dst_b, 1, 1))
                carried[j] = ds[j]["total8"] * jnp.sum(dst * sts[j], axis=0, keepdims=True)
                dst = dst * ds[j]["state"] + grow[j]
            kdk = [ke_f[j] * dke[j] for j in span]
            pos = [(q_m[j].astype(jnp.float32) * dqm[j] - k_m[j].astype(jnp.float32) * dkm[j]) + qe_f[j] * dqe[j]
                   for j in span]
            dgs = _exact_sums(rmat, [jnp.concatenate([pos[j], kdk[j]], axis=0) for j in span])
            for j in span:
                t, d = ts[j], ds[j]
                dg = dgs[j] + jnp.tile(carried[j], (CHUNK // 8, 1))
                dqs = dqm[j] * d["e_qm"] + dqe[j] * d["e_q"]
                dkk = dkm[j] * d["e_km"] + dke[j] * d["e_k"]
                df = dg / t["f"] - dkk
                dp_ref[1, rows[j], :] = _bf(df * (1.0 - lb) * (t["sg"] * (1.0 - t["sg"])))
                dp_ref[0, rows[j], :] = _bf(dqs * (t["sq"] * (1.0 + qvs[j] * (1.0 - t["sq"]))))
                dlb = dlb + df * (1.0 - t["sg"])
            return dst, dlb

        _, dlb = lax.fori_loop(0, NGRP, group, (jnp.zeros((HEAD, HEAD), jnp.float32),
                                                jnp.zeros((CHUNK, HEAD), jnp.float32)))
        dlb_ref[...] = jnp.sum(dlb, axis=0, keepdims=True)

    vec = pl.BlockSpec((1, HEAD), lambda h: (0, h))
    return pl.pallas_call(
        body, name="hgrn_bwd", grid=(NHEAD,),
        in_specs=[*_HEAD_SPECS,
                  pl.BlockSpec((2, HEAD), lambda h: (0, h)), vec,
                  pl.BlockSpec((T, HEAD), lambda h: (0, h)),
                  pl.BlockSpec((None, NCHUNK, HEAD, HEAD), lambda h: (h, 0, 0, 0)),
                  pl.BlockSpec((T, HEAD), lambda h: (0, NHEAD + h)), ANY, ANY],
        out_specs=(pl.BlockSpec((4, T, HEAD), lambda h: (0, 0, h)), vec, vec),
        out_shape=(pltpu.HBM((NSEG, T, D), jnp.bfloat16),
                   jax.ShapeDtypeStruct((1, D), jnp.float32), jax.ShapeDtypeStruct((1, D), jnp.float32)),
        scratch_shapes=[pltpu.VMEM((T, HEAD), jnp.float32)],
        input_output_aliases={9: 0},
        compiler_params=_params(("parallel",)),
    )(proj, proj, proj, proj, lb_logits, rec_g, o, states, dymix, dproj_in, token)


def _pool_bwd(proj, pool_w, pool_scale, dymix):
    def body(u_ref, pg_ref, w_ref, sc_ref, dy_ref, dp_ref, gw_ref, gs_ref):
        gidx = pl.program_id(0)
        u, pg = u_ref[...], pg_ref[...]
        d = _bf(_window_mean(_window_sum(u, gidx, _shift_down), gidx) - u)
        mixed = _dot(d, w_ref[...], 1, 0)
        spg = _sigmoid(pg)
        dyv = dy_ref[...]
        d_p = dyv * (pg * spg)
        dp_ref[1] = _bf(dyv * (mixed * sc_ref[...]) * (spg * (1.0 + pg * (1.0 - spg))))
        gs_ref[...] = jnp.sum(d_p * mixed, axis=0, keepdims=True)
        dmixed = _bf(d_p * sc_ref[...])
        gw_ref[...] = _dot(d, dmixed, 0, 0)
        dd = _dot(dmixed, w_ref[...], 1, 1)
        dp_ref[0] = _bf(_window_sum(_window_mean(dd, gidx), gidx, _shift_up) - dd)

    return pl.pallas_call(
        body, name="pool_bwd", grid=(NGROUP,),
        in_specs=[*_POOL_SPECS,
                  pl.BlockSpec((None, GROUP, GROUP), lambda g: (g, 0, 0)),
                  pl.BlockSpec((1, GROUP), lambda g: (0, g)),
                  pl.BlockSpec((T, GROUP), lambda g: (0, g))],
        out_specs=(pl.BlockSpec((2, T, GROUP), lambda g: (2, 0, g)),
                   pl.BlockSpec((None, GROUP, GROUP), lambda g: (g, 0, 0)),
                   pl.BlockSpec((1, GROUP), lambda g: (0, g))),
        out_shape=(pltpu.HBM((NSEG, T, D), jnp.bfloat16),
                   jax.ShapeDtypeStruct((NGROUP, GROUP, GROUP), jnp.float32),
                   jax.ShapeDtypeStruct((1, D), jnp.float32)),
        compiler_params=_params(("parallel",)),
    )(proj, proj, pool_w, pool_scale, dymix)


def _proj_bwd_w(place, ht, dproj):
    half = NTILE // 2

    def owner_chip(i, pr):
        return jnp.where(i < half, i // 3, (pr[1] + 1 + (i - half) // 3) % 4)

    def tile_of(i, pr):
        side = jnp.where(i < half, 1 - pr[2], pr[2])
        return 6 * owner_chip(i, pr) + 3 * side + i % 3

    def dproj_block(i, pr):
        j = tile_of(i, pr)
        return ((j // 4 + 4) % NSEG, 0, j % 4)

    def mine(i):
        return jnp.maximum(i, half)

    def body(place_ref, h_ref, dp_ref, sum_ref, own_ref, sendbuf, recvbuf, send_sems, recv_sems):
        i = pl.program_id(0)
        px, py, c, _ = _place()
        gw = _dot(h_ref[...], dp_ref[...], 1, 0)

        def to_sibling(slot):
            return pltpu.make_async_remote_copy(
                src_ref=sendbuf.at[slot], dst_ref=recvbuf.at[slot], send_sem=send_sems.at[slot],
                recv_sem=recv_sems.at[slot], device_id=(px, py, 1 - c), device_id_type=MESH)

        @pl.when(i < half)
        def _():
            sendbuf[i] = _bf(gw)
            to_sibling(i).start()

        @pl.when(i >= half)
        def _():
            slot = 3 * owner_chip(i, place_ref) + i % 3
            to_sibling(slot).wait_recv()
            total = gw + recvbuf[slot].astype(jnp.float32)
            sum_ref[...] = _bf(total)
            own_ref[...] = total

        @pl.when(i == NTILE - 1)
        def _():
            for slot in range(half):
                to_sibling(slot).wait_send()

    return pl.pallas_call(
        body, name="proj_bwd_w",
        grid_spec=pltpu.PrefetchScalarGridSpec(
            num_scalar_prefetch=1, grid=(NTILE,),
            in_specs=[pl.BlockSpec((D, T), lambda i, pr: (0, 0)),
                      pl.BlockSpec((None, T, TILE), lambda i, pr: dproj_block(i, pr))],
            out_specs=(pl.BlockSpec((None, None, D, TILE), lambda i, pr: (owner_chip(mine(i), pr), mine(i) % 3, 0, 0)),
                       pl.BlockSpec((None, D, TILE), lambda i, pr: (jnp.where(i < NTILE - 3, 0, i % 3), 0, 0))),
            scratch_shapes=[pltpu.VMEM((half, D, TILE), jnp.bfloat16), pltpu.VMEM((half, D, TILE), jnp.bfloat16),
                            pltpu.SemaphoreType.DMA((half,)), pltpu.SemaphoreType.DMA((half,))]),
        out_shape=(pltpu.HBM((4, 3, D, TILE), jnp.bfloat16), pltpu.HBM((3, D, TILE), jnp.float32)),
        compiler_params=_params(("arbitrary",)),
    )(place, ht, dproj)


def _proj_bwd_x(dproj, w_t, x, g1, dz, token):
    tm = 512
    pairs = NSEG // 2

    def body(dp_ref, w_ref, x_ref, g_ref, dz_ref, token_any, dx_ref, dg_ref, wcat, acc):
        del token_any
        m, s = pl.program_id(0), pl.program_id(1)
        r = None
        for k in range(2):
            for i in range(4):
                wcat[k, :, i * TILE:(i + 1) * TILE] = w_ref[4 * k + i]
            part = _dot(dp_ref[k], wcat[k], 1, 1)
            r = part if r is None else r + part

        @pl.when(s == 0)
        def _():
            acc[...] = r

        @pl.when(s != 0)
        def _():
            acc[...] += r

        @pl.when(s == pairs - 1)
        def _():
            xv = x_ref[...]
            rs = lax.rsqrt(jnp.mean(xv * xv, axis=-1, keepdims=True) + EPS)
            xhat = xv * rs
            dhv = acc[...]
            gdh = dhv * g_ref[...]
            dx_ref[...] = dz_ref[...] + rs * (gdh - xhat * jnp.mean(xhat * gdh, axis=-1, keepdims=True))
            dg = jnp.sum(xhat * dhv, axis=0, keepdims=True)

            @pl.when(m == 0)
            def _():
                dg_ref[...] = dg

            @pl.when(m != 0)
            def _():
                dg_ref[...] += dg

    rows = pl.BlockSpec((tm, D), lambda m, s: (m, 0))
    vec = pl.BlockSpec((1, D), lambda m, s: (0, 0))
    return pl.pallas_call(
        body, name="proj_bwd_x", grid=(T // tm, pairs),
        in_specs=[pl.BlockSpec((2, tm, D), lambda m, s: (s, m, 0)),
                  pl.BlockSpec((8, D, TILE), lambda m, s: ((s + 1) % pairs, 0, 0)), rows, vec, rows, ANY],
        out_specs=(rows, vec),
        out_shape=(jax.ShapeDtypeStruct((T, D), jnp.float32), jax.ShapeDtypeStruct((1, D), jnp.float32)),
        scratch_shapes=[pltpu.VMEM((2, D, D), jnp.bfloat16), pltpu.VMEM((tm, D), jnp.float32)],
        compiler_params=_params(("arbitrary", "arbitrary"), vmem_mib=56),
    )(dproj, w_t, x, g1, dz, token)


def _adamw(w, g, m, v):
    m_new = ADAM_B1 * m + (1.0 - ADAM_B1) * g
    v_new = ADAM_B2 * v + (1.0 - ADAM_B2) * (g * g)
    delta = -ADAM_LR * ((m_new / BC1) / (jnp.sqrt(v_new / BC2) + ADAM_EPS) + ADAM_WD * w)
    return delta, m_new, v_new


def _reduce_adam(name, place, parts, w, m, v, grid, w_spec):
    n = len(parts)

    def body(place_ref, *refs):
        del place_ref
        w_ref, m_ref, v_ref, g_ref, d_ref, mo_ref, vo_ref = refs[n:]
        g = None
        for ref, (_, _, stacked) in zip(refs[:n], parts):
            terms = [ref[r] for r in range(ref.shape[0])] if stacked else [ref[...]]
            for t in terms:
                g = t.astype(jnp.float32) if g is None else g + t.astype(jnp.float32)
        delta, m_new, v_new = _adamw(w_ref[...], g, m_ref[...], v_ref[...])
        g_ref[...] = g
        d_ref[...] = delta
        mo_ref[...] = m_new
        vo_ref[...] = v_new

    shape = jax.ShapeDtypeStruct(w.shape, jnp.float32)
    return pl.pallas_call(
        body, name=name,
        grid_spec=pltpu.PrefetchScalarGridSpec(
            num_scalar_prefetch=1, grid=grid,
            in_specs=[spec for _, spec, _ in parts] + [w_spec] * 3, out_specs=(w_spec,) * 4),
        out_shape=(shape,) * 4,
        compiler_params=_params(("parallel",)),
    )(place, *[_in_hbm(a) for a in [a for a, _, _ in parts] + [w, m, v]])


def _small_adam(parts, w, m, v):
    def body(p_ref, w_ref, m_ref, v_ref, g_ref, d_ref, mo_ref, vo_ref):
        g = p_ref[0]
        for s in range(1, NDEV):
            g = g + p_ref[s]
        wv = w_ref[...]
        rows = _row_ids(wv.shape)
        other = jnp.where(rows == 2, pltpu.roll(wv, 7, 0), jnp.where(rows == 3, pltpu.roll(wv, 1, 0), 0.0))
        lbv = _sigmoid(wv - other)
        sign = jnp.where(rows == 2, 1.0, -1.0)
        g = jnp.where((rows == 2) | (rows == 3), sign * g * lbv * (1.0 - lbv), g)
        delta, m_new, v_new = _adamw(wv, g, m_ref[...], v_ref[...])
        g_ref[...] = g
        d_ref[...] = delta
        mo_ref[...] = m_new
        vo_ref[...] = v_new

    shape = jax.ShapeDtypeStruct((8, D), jnp.float32)
    return pl.pallas_call(body, name="small_adam", out_shape=(shape,) * 4)(parts, w, m, v)


def _rows8(*vecs):
    rows = [a.reshape(-1, D) for a in vecs]
    n = sum(r.shape[0] for r in rows)
    return jnp.concatenate(rows + [jnp.zeros((8 - n, D), jnp.float32)], axis=0)


def kernel(x, norm1_g, w_in, pool_w, pool_scale, lb_logits, rec_norm_g, w_out, final_norm_g, loss_target, m_norm1_g, m_w_in, m_pool_w, m_pool_scale, m_lb_logits, m_rec_norm_g, m_w_out, m_final_norm_g, v_norm1_g, v_w_in, v_pool_w, v_pool_scale, v_lb_logits, v_rec_norm_g, v_w_out, v_final_norm_g):
    xs = x[0]
    target = loss_target[0]
    ix, iy, ic = lax.axis_index("x"), lax.axis_index("y"), lax.axis_index("c")
    place = jnp.stack([4 * ix + 2 * iy + ic, 2 * ix + iy, ic]).astype(jnp.int32)
    gf = final_norm_g.reshape(1, D)

    ht, w_t, w_out_b, w_out_g, pool_g, proj = _gather_proj(xs, norm1_g, w_in, w_out, pool_w)
    pool_full = pool_g.transpose(1, 0, 2, 3).reshape(NGROUP, GROUP, GROUP)
    wout = [w_out_b, w_out_g]
    wout_send, wout_recv, wout, wout_token = _split_start("gather_wout_start", wout, NDEV - 1, _plan_wout)

    y = _pool_fwd(proj, pool_full, pool_scale, wout_token)
    y, o, states = _hgrn_fwd(proj, lb_logits, rec_norm_g, y)
    _, w_out_g = _split_wait("gather_wout_wait", wout, wout_send, wout_recv, _plan_wout, o)
    w_out_full = _in_hbm(w_out_g.reshape(DMIX, D))
    dz, dzb, sq, dgf = _out_proj_loss(xs, y, w_out_full, target, gf)

    dymix, gwout_f, gwout_b = _out_proj_bwd(dzb, w_out_full, y)
    dproj, gpool, dscale = _pool_bwd(proj, pool_full, pool_scale, dymix)

    blk_out = (NDEV, DMIX // NDEV, D)
    blk_pool = (NDEV, NGROUP, GROUP // NDEV, GROUP)
    gpool_s = gpool.reshape(NGROUP, NDEV, GROUP // NDEV, GROUP).transpose(1, 0, 2, 3)
    rest = [gwout_b.reshape(blk_out), gpool_s,
            lax.empty((NDEV - 1,) + blk_out[1:], jnp.bfloat16), lax.empty((NDEV - 1,) + blk_pool[1:], jnp.float32)]
    rest_send, rest_recv, rest, rest_token = _split_start("scatter_rest_start", rest, 2 * (NDEV - 1), _plan_rest)

    dproj, drecg, dlb = _hgrn_bwd(proj, lb_logits, rec_norm_g, o, states, dymix, dproj, rest_token)
    chip_sums, own_sum = _proj_bwd_w(place, ht, dproj)
    win = [chip_sums, lax.empty((3, 3, D, TILE), jnp.bfloat16)]
    win_send, win_recv, win, win_token = _split_start("scatter_win_start", win, 3, _plan_in)

    grad_x, dg1 = _proj_bwd_x(dproj, w_t, xs, norm1_g, dz, win_token)

    _, gpool_own, r_out, r_pool = _split_wait("scatter_rest_wait", rest, rest_send, rest_recv, _plan_rest, grad_x)
    g_wout, d_wout, m_wout, v_wout = _reduce_adam(
        "adam_w_out", place,
        [(gwout_f.reshape(blk_out), pl.BlockSpec((None,) + blk_out[1:], lambda i, pr: (pr[0], 0, 0)), False),
         (r_out, pl.BlockSpec((NDEV - 1,) + blk_out[1:], lambda i, pr: (0, 0, 0)), True)],
        w_out, m_w_out, v_w_out, (1,), pl.BlockSpec((None,) + blk_out[1:], lambda i, pr: (0, 0, 0)))
    g_pool, d_pool, m_pool, v_pool = _reduce_adam(
        "adam_pool_w", place,
        [(gpool_own, pl.BlockSpec((None,) + blk_pool[1:], lambda i, pr: (pr[0], 0, 0, 0)), False),
         (r_pool, pl.BlockSpec((NDEV - 1,) + blk_pool[1:], lambda i, pr: (0, 0, 0, 0)), True)],
        pool_w, m_pool_w, v_pool_w, (1,), pl.BlockSpec((None,) + blk_pool[1:], lambda i, pr: (0, 0, 0, 0)))

    r_small = _gather_small(_rows8(dg1, dscale, dlb, dlb, drecg, dgf, sq), d_wout, d_pool)
    loss = jnp.sum(r_small[:, 6, :]) * (0.5 / D)
    g_s, d_s, m_s, v_s = _small_adam(
        r_small,
        _rows8(norm1_g, pool_scale, lb_logits, rec_norm_g, final_norm_g),
        _rows8(m_norm1_g, m_pool_scale, m_lb_logits, m_rec_norm_g, m_final_norm_g),
        _rows8(v_norm1_g, v_pool_scale, v_lb_logits, v_rec_norm_g, v_final_norm_g))

    _, r_in = _split_wait("scatter_win_wait", win, win_send, win_recv, _plan_in, v_s)
    g_win, d_win, m_win, v_win = _reduce_adam(
        "adam_w_in", place,
        [(own_sum, pl.BlockSpec((None, D, TILE), lambda i, pr: (i, 0, 0)), False),
         (r_in, pl.BlockSpec((3, None, D, TILE), lambda i, pr: (0, i, 0, 0)), True)],
        w_in, m_w_in, v_w_in, (3,), pl.BlockSpec((None, D, TILE), lambda i, pr: (0, 0, i)))

    def small_outs(a):
        return a[0:1], a[1:2], a[2:4], a[4:5], a[5]

    def outs(small_a, win, pool, wout):
        n1, ps, lbl, rg, fg = small_outs(small_a)
        return n1, win, pool, ps, lbl, rg, wout, fg

    return (loss, grad_x[None],
            *outs(g_s, g_win, g_pool, g_wout), *outs(d_s, d_win, d_pool, d_wout),
            *outs(m_s, m_win, m_pool, m_wout), *outs(v_s, v_win, v_pool, v_wout))
```

```python
import functools

import jax
import jax.numpy as jnp
from jax import lax
from jax.experimental import pallas as pl
from jax.experimental.pallas import tpu as pltpu

T = 2048
D = 1024
NSEG = 6
NTILE = 24
TILE = 256
DMIX = 2048
NDEV = 8
HEAD = 128
NHEAD = 8
CHUNK = 64
NCHUNK = T // CHUNK
NB = 32
NGRP = NCHUNK // NB
NGROUP = 4
GROUP = 256
EPS = 1e-6
EXP_CAP = 80.0
MESH = pl.DeviceIdType.MESH
AXES = ("x", "y", "c")
ANY = pl.BlockSpec(memory_space=pl.ANY)
HBM = pl.BlockSpec(memory_space=pltpu.HBM)
SEM = pl.BlockSpec(memory_space=pltpu.SEMAPHORE)
EFFECT = pltpu.SideEffectType.DATAFLOW_SIDE_EFFECTING

ADAM_LR = 0.001
ADAM_B1 = 0.9
ADAM_B2 = 0.999
ADAM_EPS = 1e-08
ADAM_WD = 0.01
ADAM_STEP = 10
BC1 = 1.0 - ADAM_B1 ** ADAM_STEP
BC2 = 1.0 - ADAM_B2 ** ADAM_STEP

MIB = 1 << 20


def _params(sem=None, vmem_mib=48):
    return pltpu.CompilerParams(dimension_semantics=sem, vmem_limit_bytes=vmem_mib * MIB)


def _sigmoid(v):
    return 1.0 / (1.0 + jnp.exp(-v))


def _dot(a, b, ca, cb, precision=None):
    return lax.dot_general(a, b, (((ca,), (cb,)), ((), ())), precision=precision,
                           preferred_element_type=jnp.float32)


def _bf(v):
    return v.astype(jnp.bfloat16)


def _in_hbm(a):
    return pltpu.with_memory_space_constraint(a, pltpu.HBM)


def _place():
    x, y, c = lax.axis_index("x"), lax.axis_index("y"), lax.axis_index("c")
    return x, y, c, 4 * x + 2 * y + c


def _peer(x, y, c, r):
    return (x ^ ((r >> 2) & 1), y ^ ((r >> 1) & 1), c ^ (r & 1))


def _gather_proj(x, g1, w_in, w_out, pool_w):
    def body(x_ref, g_ref, win_ref, wout_ref, pool_ref, ht_o, wt_o, woutb_o, wout_o, pool_o, proj_o,
             hv, htv, wv, wob, pb, stage, send_sems, recv_sems, loc_sems, out_sems):
        px, py, c, my_idx = _place()
        me, sibling = (px, py, c), (px, py, 1 - c)
        chips = [(1 - px, py), (px, 1 - py), (1 - px, 1 - py)]
        for p in range(3):
            wv[3 * my_idx + p] = _bf(win_ref[0, :, p * TILE:(p + 1) * TILE])

        def index(bx, by, bc):
            return 4 * bx + 2 * by + bc

        def slot(w, block):
            return wv.at[pl.ds(3 * index(*block), 3)] if w == 0 else pool_o.at[index(*block)]

        def copy(k, w, block, to, src=None):
            return pltpu.make_async_remote_copy(
                src_ref=slot(w, block) if src is None else src, dst_ref=slot(w, block),
                send_sem=send_sems.at[2 * k + w], recv_sem=recv_sems.at[2 * k + w],
                device_id=to, device_id_type=MESH)

        def save(block):
            at = pl.ds(3 * index(*block), 3)
            pltpu.make_async_copy(wv.at[at], wt_o.at[at], loc_sems.at[4]).start()

        srcs = (slot(0, me), pb)
        first = []
        for w in (0, 1):
            if w == 1:
                pb[...] = _bf(pool_ref[0])
                wob[...] = _bf(wout_ref[0])
            group = [copy(1 + j, w, me, (*chip, c), src=srcs[w]) for j, chip in enumerate(chips[:2])]
            group.append(copy(0, w, me, sibling, src=srcs[w]))
            for cp in group:
                cp.start()
            first += group
        save(me)
        locs = [pltpu.make_async_copy(pb, slot(1, me), loc_sems.at[0]),
                pltpu.make_async_copy(wob, wout_o.at[my_idx], loc_sems.at[1]),
                pltpu.make_async_copy(wob, woutb_o, loc_sems.at[2])]
        for cp in locs:
            cp.start()

        xv = x_ref[...]
        hv[...] = _bf(xv * lax.rsqrt(jnp.mean(xv * xv, axis=-1, keepdims=True) + EPS) * g_ref[...])
        rows = 256
        for r0 in range(0, T, rows):
            htv[:, r0:r0 + rows] = hv[r0:r0 + rows, :].T
        locs.append(pltpu.make_async_copy(htv, ht_o, loc_sems.at[3]))
        locs[-1].start()

        def out_copy(p, j):
            return pltpu.make_async_copy(stage.at[p], proj_o.at[j], out_sems.at[p])

        def project(nth, block):
            base = 3 * index(*block)

            def tile(p, carry):
                if nth > 0:
                    out_copy(p, base + p).wait()
                stage[p] = _dot(hv[...], wv[base + p], 1, 0)
                out_copy(p, base + p).start()
                return carry

            lax.fori_loop(0, 3, tile, 0)

        project(0, me)
        copy(0, 0, sibling, me).wait_recv()
        save(sibling)
        project(1, sibling)
        passed = []
        relay_from = (px ^ (1 - c), py ^ c, c)
        relay_to = (px ^ c, py ^ (1 - c), c)

        def arrived(w, j):
            copy(1 + j, w, (*chips[j], c), me).wait_recv()
            passed.append(copy(4 + j, w, (*chips[j], c), sibling))
            passed[-1].start()

        def relay(w):
            passed.append(copy(3, w, relay_from, relay_to))
            passed[-1].start()

        def handed(nth, j):
            copy(4 + j, 0, (*chips[j], 1 - c), me).wait_recv()
            save((*chips[j], 1 - c))
            project(nth, (*chips[j], 1 - c))

        arrived(0, 0)
        arrived(0, 1)
        relay(0)
        for j in range(2):
            save((*chips[j], c))
            project(2 + j, (*chips[j], c))
        handed(4, 0)
        handed(5, 1)
        arrived(1, 0)
        arrived(1, 1)
        relay(1)
        arrived(0, 2)
        save((*chips[2], c))
        project(6, (*chips[2], c))
        handed(7, 2)
        arrived(1, 2)
        copy(0, 1, sibling, me).wait_recv()
        for j, chip in enumerate(chips):
            copy(4 + j, 1, (*chip, 1 - c), me).wait_recv()
        keep = pltpu.make_async_copy(wv, wt_o, loc_sems.at[4])
        for p in range(3):
            out_copy(p, p).wait()
        for cp in first + passed:
            cp.wait_send()
        keep.wait()
        for cp in locs:
            cp.wait()

    vmem = pl.BlockSpec(memory_space=pltpu.VMEM)
    bf16 = jnp.bfloat16
    return pl.pallas_call(
        body, name="gather_proj",
        out_shape=(pltpu.HBM((D, T), bf16), pltpu.HBM((NTILE, D, TILE), bf16),
                   pltpu.HBM((DMIX // NDEV, D), bf16), pltpu.HBM((NDEV, DMIX // NDEV, D), bf16),
                   pltpu.HBM((NDEV, NGROUP, GROUP // NDEV, GROUP), bf16), pltpu.HBM((NTILE, T, TILE), jnp.float32)),
        in_specs=[vmem] * 5, out_specs=(ANY,) * 6,
        scratch_shapes=[pltpu.VMEM((T, D), bf16), pltpu.VMEM((D, T), bf16), pltpu.VMEM((NTILE, D, TILE), bf16),
                        pltpu.VMEM((DMIX // NDEV, D), bf16), pltpu.VMEM((NGROUP, GROUP // NDEV, GROUP), bf16),
                        pltpu.VMEM((3, T, TILE), jnp.float32),
                        pltpu.SemaphoreType.DMA((14,)), pltpu.SemaphoreType.DMA((14,)),
                        pltpu.SemaphoreType.DMA((5,)), pltpu.SemaphoreType.DMA((3,))],
        compiler_params=_params(vmem_mib=56),
    )(x, g1, w_in, w_out, pool_w)


def _split_start(name, arrays, n_copies, plan):
    k = len(arrays)

    def body(*refs):
        send_sems, recv_sems, token = refs[k], refs[k + 1], refs[-1]
        for i, (src, dst, to) in enumerate(plan(refs[:k])):
            pltpu.make_async_remote_copy(src_ref=src, dst_ref=dst, send_sem=send_sems.at[i],
                                         recv_sem=recv_sems.at[i], device_id=to, device_id_type=MESH).start()
        token[...] = jnp.zeros_like(token)

    out = pl.pallas_call(
        body, name=name,
        out_shape=(pltpu.SemaphoreType.DMA((n_copies,)), pltpu.SemaphoreType.DMA((n_copies,)),
                   *[pltpu.HBM(a.shape, a.dtype) for a in arrays], jax.ShapeDtypeStruct((8, 128), jnp.float32)),
        in_specs=[HBM] * k, out_specs=(SEM, SEM, *[HBM] * k, pl.BlockSpec(memory_space=pltpu.VMEM)),
        input_output_aliases={i: 2 + i for i in range(k)},
        compiler_params=pltpu.CompilerParams(has_side_effects=EFFECT),
    )(*[pltpu.with_memory_space_constraint(a, pltpu.HBM) for a in arrays])
    return out[0], out[1], out[2:2 + k], out[-1]


def _split_wait(name, arrays, send_sems, recv_sems, plan, after):
    k = len(arrays)

    def body(*refs):
        sends, recvs = refs[k], refs[k + 1]
        for i, (src, dst, to) in enumerate(plan(refs[:k])):
            cp = pltpu.make_async_remote_copy(src_ref=src, dst_ref=dst, send_sem=sends.at[i], recv_sem=recvs.at[i],
                                              device_id=to, device_id_type=MESH)
            cp.wait_send()
            cp.wait_recv()

    return pl.pallas_call(
        body, name=name,
        out_shape=tuple(pltpu.HBM(a.shape, a.dtype) for a in arrays),
        in_specs=[HBM] * k + [SEM, SEM, ANY], out_specs=(HBM,) * k,
        input_output_aliases={i: i for i in range(k)},
        compiler_params=pltpu.CompilerParams(has_side_effects=EFFECT),
    )(*arrays, send_sems, recv_sems, after)


def _plan_wout(refs):
    src, land = refs
    x, y, c, me = _place()
    return [(src, land.at[me], _peer(x, y, c, r)) for r in range(1, NDEV)]


def _plan_rest(refs):
    gob, gpf, r_out, r_pool = refs
    x, y, c, me = _place()
    plan = []
    for r in range(1, NDEV):
        plan.append((gob.at[me ^ r], r_out.at[r - 1], _peer(x, y, c, r)))
        plan.append((gpf.at[me ^ r], r_pool.at[r - 1], _peer(x, y, c, r)))
    return plan


def _plan_in(refs):
    sum_b, r_in = refs
    x, y, c, _ = _place()
    plan = []
    for j, (dx, dy) in enumerate(((1, 0), (0, 1), (1, 1))):
        px, py = x ^ dx, y ^ dy
        plan.append((sum_b.at[2 * px + py], r_in.at[j], (px, py, c)))
    return plan


def _plan_small(refs):
    small, land = refs
    x, y, c, me = _place()
    return [(small, land.at[me], _peer(x, y, c, r)) for r in range(1, NDEV)]


def _seg_tiles(s):
    return (s + 2) % NSEG


_POOL_SPECS = [pl.BlockSpec((None, T, GROUP), lambda g, base=base: (base + g, 0, 0)) for base in (0, 4)]
_HEAD_SPECS = [pl.BlockSpec((None, T, HEAD), lambda h, base=base: (base + h // 2, 0, h % 2))
               for base in (8, 12, 16, 20)]


def _row_ids(shape):
    return lax.broadcasted_iota(jnp.int32, shape, 0)


EDGE = 8


def _shift_down(a, k):
    r = pltpu.roll(a, k, 0)
    edge = _row_ids((EDGE, a.shape[1]))
    return jnp.concatenate([jnp.where(edge >= k, r[:EDGE], 0.0), r[EDGE:]], axis=0)


def _shift_up(a, k):
    r = pltpu.roll(a, T - k, 0)
    edge = _row_ids((EDGE, a.shape[1]))
    return jnp.concatenate([r[:T - EDGE], jnp.where(edge < EDGE - k, r[T - EDGE:], 0.0)], axis=0)


def _window_sum(u, gidx, shift):
    s2 = u + shift(u, 1)
    s4 = s2 + shift(s2, 2)
    s8 = s4 + shift(s4, 4)
    s16 = s8 + shift(s8, 8)
    return jnp.where(gidx == 0, s2, jnp.where(gidx == 1, s4, jnp.where(gidx == 2, s8, s16)))


def _window_mean(s, gidx):
    inv = jnp.where(gidx == 0, 0.5, jnp.where(gidx == 1, 0.25, jnp.where(gidx == 2, 0.125, 0.0625)))
    width = lax.shift_left(jnp.int32(2), gidx)
    head = s[:16] / jnp.minimum(_row_ids((16, s.shape[1])) + 1, width).astype(jnp.float32)
    return jnp.concatenate([head, s[16:] * inv], axis=0)


def _pool_fwd(proj, pool_w, pool_scale, token):
    def body(u_ref, pg_ref, w_ref, sc_ref, token_any, y_ref):
        del token_any
        gidx = pl.program_id(0)
        u, pg = u_ref[...], pg_ref[...]
        d = _window_mean(_window_sum(u, gidx, _shift_down), gidx) - u
        mixed = _dot(_bf(d), w_ref[...], 1, 0)
        y_ref[...] = _bf(mixed * sc_ref[...] * (pg * _sigmoid(pg)))

    return pl.pallas_call(
        body, name="pool_fwd", grid=(NGROUP,),
        in_specs=[*_POOL_SPECS,
                  pl.BlockSpec((None, GROUP, GROUP), lambda g: (g, 0, 0)),
                  pl.BlockSpec((1, GROUP), lambda g: (0, g)), ANY],
        out_specs=pl.BlockSpec((T, GROUP), lambda g: (0, g)),
        out_shape=pltpu.HBM((T, DMIX), jnp.bfloat16),
        compiler_params=_params(("parallel",)),
    )(proj, proj, pool_w, pool_scale, token)


def _tri(lower):
    r = lax.broadcasted_iota(jnp.int32, (CHUNK, CHUNK), 0)
    c = lax.broadcasted_iota(jnp.int32, (CHUNK, CHUNK), 1)
    return (r >= c) if lower else (r <= c)


def _sum_rows_matrix():
    shape = (CHUNK + 16, CHUNK)
    r, c = lax.broadcasted_iota(jnp.int32, shape, 0), lax.broadcasted_iota(jnp.int32, shape, 1)
    run = jnp.where(c <= r, 1.0, 0.0)
    half = jnp.where(c < CHUNK // 2, 1.0, 0.0)
    return _bf(jnp.where(r < CHUNK, run, jnp.where(r < CHUNK + 8, 1.0, half)))


def _rev_sum_matrix():
    shape = (CHUNK, 2 * CHUNK)
    r, c = lax.broadcasted_iota(jnp.int32, shape, 0), lax.broadcasted_iota(jnp.int32, shape, 1)
    return _bf(jnp.where(c < CHUNK, jnp.where(c >= r, 1.0, 0.0), jnp.where(c - CHUNK < r, 1.0, 0.0)))


def _split2(a):
    hi = _bf(a)
    return [hi, _bf(a - hi.astype(jnp.float32))]


def _exact_sums(mat, pieces):
    x = jnp.concatenate([s for p in pieces for s in _split2(p)], axis=1)
    r = _dot(mat, x, 1, 0)
    return [r[:, 2 * j * HEAD:(2 * j + 1) * HEAD] + r[:, (2 * j + 1) * HEAD:(2 * j + 2) * HEAD]
            for j in range(len(pieces))]


def _gates(qv, fl, lb):
    sq = _sigmoid(qv)
    sg = _sigmoid(fl)
    f = lb + (1.0 - lb) * sg
    return dict(sq=sq, qs=qv * sq, sg=sg, f=f, kk=1.0 - f, g=jnp.log(f))


def _decays(sums):
    big_g = sums[:CHUNK]
    total = sums[CHUNK:CHUNK + 8]
    g_last = jnp.tile(total, (CHUNK // 8, 1))
    g_mid = jnp.tile(sums[CHUNK + 8:], (CHUNK // 8, 1))
    return dict(
        e_q=jnp.exp(big_g),
        e_k=jnp.exp(g_last - big_g),
        e_qm=jnp.exp(jnp.minimum(big_g - g_mid, EXP_CAP)),
        e_km=jnp.exp(jnp.minimum(g_mid - big_g, EXP_CAP)),
        total8=jnp.exp(total),
        state=jnp.exp(jnp.tile(total, (HEAD // 8, 1))))


def _group_rows(gi):
    return [pl.ds(pl.multiple_of((gi * NB + j) * CHUNK, CHUNK), CHUNK) for j in range(NB)]


def _lower_bound(lb_ref):
    return _sigmoid(lb_ref[0:1, :] - lb_ref[1:2, :])


def _hgrn_fwd(proj, lb_logits, rec_g, y_in):
    def body(q_ref, f_ref, i_ref, gate_ref, lb_ref, rg_ref, y_any, y_ref, o_ref, st_ref):
        del y_any
        lb = _lower_bound(lb_ref)
        causal = _tri(True)
        smat = _sum_rows_matrix()

        def group(gi, st):
            rows = _group_rows(gi)
            ts = [_gates(q_ref[r, :], f_ref[r, :], lb) for r in rows]
            ds = [_decays(s) for s in _exact_sums(smat, [t["g"] for t in ts])]
            vs = [_bf(i_ref[r, :]) for r in rows]
            q_m = [_bf(t["qs"] * d["e_qm"]) for t, d in zip(ts, ds)]
            k_m = [_bf(t["kk"] * d["e_km"]) for t, d in zip(ts, ds)]
            q_e = [_bf(t["qs"] * d["e_q"]) for t, d in zip(ts, ds)]
            k_e = [_bf(t["kk"] * d["e_k"]) for t, d in zip(ts, ds)]
            a = [_bf(jnp.where(causal, _dot(q_m[j], k_m[j], 1, 1), 0.0)) for j in range(NB)]
            intra = [_dot(a[j], vs[j], 1, 0) for j in range(NB)]
            upd = [_dot(vs[j], k_e[j], 0, 0) for j in range(NB)]
            for j in range(NB):
                st_ref[gi * NB + j] = st
                o_ref[rows[j], :] = intra[j] + _dot(q_e[j], _bf(st), 1, 1)
                st = st * ds[j]["state"] + upd[j]
            return st

        lax.fori_loop(0, NGRP, group, jnp.zeros((HEAD, HEAD), jnp.float32))
        o = o_ref[...]
        rn = o * lax.rsqrt(jnp.mean(o * o, axis=-1, keepdims=True) + EPS)
        gate = gate_ref[...]
        y_ref[...] = _bf(rn * rg_ref[...] * (gate * _sigmoid(gate)))

    return pl.pallas_call(
        body, name="hgrn_fwd", grid=(NHEAD,),
        in_specs=[*_HEAD_SPECS,
                  pl.BlockSpec((2, HEAD), lambda h: (0, h)),
                  pl.BlockSpec((1, HEAD), lambda h: (0, h)),
                  pl.BlockSpec(memory_space=pl.ANY)],
        out_specs=(pl.BlockSpec((T, HEAD), lambda h: (0, NHEAD + h)),
                   pl.BlockSpec((T, HEAD), lambda h: (0, h)),
                   pl.BlockSpec((None, NCHUNK, HEAD, HEAD), lambda h: (h, 0, 0, 0))),
        out_shape=(pltpu.HBM((T, DMIX), jnp.bfloat16), pltpu.HBM((T, D), jnp.float32),
                   pltpu.HBM((NHEAD, NCHUNK, HEAD, HEAD), jnp.float32)),
        input_output_aliases={6: 0},
        compiler_params=_params(("parallel",)),
    )(proj, proj, proj, proj, lb_logits, rec_g, y_in)


def _out_proj_loss(x, y, w_out, target, gf):
    rows = 512
    parts = [slice(k * rows // 2, (k + 1) * rows // 2) for k in range(2)]

    def body(x_ref, y_ref, w_ref, t_ref, g_ref, dz_ref, dzb_ref, sq_ref, dg_ref):
        zs = [x_ref[p, :] + _dot(y_ref[p, :], w_ref[...], 1, 0) for p in parts]
        sq = dg = 0.0
        for p, z in zip(parts, zs):
            r = lax.rsqrt(jnp.mean(z * z, axis=-1, keepdims=True) + EPS)
            zhat = z * r
            err = zhat * g_ref[...] - t_ref[p, :]
            dy = err * (1.0 / D)
            gdy = dy * g_ref[...]
            dz = r * (gdy - zhat * jnp.mean(zhat * gdy, axis=-1, keepdims=True))
            dz_ref[p, :] = dz
            dzb_ref[p, :] = _bf(dz)
            sq = sq + jnp.sum(err * err, axis=0, keepdims=True)
            dg = dg + jnp.sum(zhat * dy, axis=0, keepdims=True)

        @pl.when(pl.program_id(0) == 0)
        def _():
            sq_ref[...] = sq
            dg_ref[...] = dg

        @pl.when(pl.program_id(0) != 0)
        def _():
            sq_ref[...] += sq
            dg_ref[...] += dg

    tile = pl.BlockSpec((rows, D), lambda i: (i, 0))
    vec = pl.BlockSpec((1, D), lambda i: (0, 0))
    return pl.pallas_call(
        body, name="out_proj_loss", grid=(T // rows,),
        in_specs=[tile, pl.BlockSpec((rows, DMIX), lambda i: (i, 0)), pl.BlockSpec((DMIX, D), lambda i: (0, 0)),
                  tile, vec],
        out_specs=(tile, tile, vec, vec),
        out_shape=(pltpu.HBM((T, D), jnp.float32), pltpu.HBM((T, D), jnp.bfloat16),
                   jax.ShapeDtypeStruct((1, D), jnp.float32), jax.ShapeDtypeStruct((1, D), jnp.float32)),
        compiler_params=_params(("arbitrary",)),
    )(x, y, w_out, target, gf)


def _out_proj_bwd(dzb, w_out, y):
    tn = 512

    def body(dz_ref, w_ref, y_ref, dy_ref, gw_ref, gwb_ref):
        dz = dz_ref[...]
        dy_ref[...] = _dot(dz, w_ref[...], 1, 1)
        gw = _dot(y_ref[...], dz, 0, 0)
        gw_ref[...] = gw
        gwb_ref[...] = _bf(gw)

    return pl.pallas_call(
        body, name="out_proj_bwd", grid=(DMIX // tn,),
        in_specs=[pl.BlockSpec((T, D), lambda n: (0, 0)), pl.BlockSpec((tn, D), lambda n: (n, 0)),
                  pl.BlockSpec((T, tn), lambda n: (0, n))],
        out_specs=(pl.BlockSpec((T, tn), lambda n: (0, n)), pl.BlockSpec((tn, D), lambda n: (n, 0)),
                   pl.BlockSpec((tn, D), lambda n: (n, 0))),
        out_shape=(pltpu.HBM((T, DMIX), jnp.float32), pltpu.HBM((DMIX, D), jnp.float32),
                   pltpu.HBM((DMIX, D), jnp.bfloat16)),
        compiler_params=_params(("parallel",)),
    )(dzb, w_out, y)


def _hgrn_bwd(proj, lb_logits, rec_g, o, states, dymix, dproj_in, token):
    def body(q_ref, f_ref, i_ref, gate_ref, lb_ref, rg_ref, o_ref, st_ref, dy_ref, dp_any, token_any,
             dp_ref, drg_ref, dlb_ref, do_ref):
        del dp_any, token_any
        lb = _lower_bound(lb_ref)
        causal = _tri(True)
        smat, rmat = _sum_rows_matrix(), _rev_sum_matrix()

        o = o_ref[...]
        rs = lax.rsqrt(jnp.mean(o * o, axis=-1, keepdims=True) + EPS)
        rn = o * rs
        gate = gate_ref[...]
        sgate = _sigmoid(gate)
        dyv = dy_ref[...]
        d_r = dyv * (gate * sgate)
        dp_ref[3] = _bf(dyv * (rn * rg_ref[...]) * (sgate * (1.0 + gate * (1.0 - sgate))))
        drg_ref[...] = jnp.sum(d_r * rn, axis=0, keepdims=True)
        drn = d_r * rg_ref[...]
        do_ref[...] = rs * (drn - rn * jnp.mean(rn * drn, axis=-1, keepdims=True))

        def group(i, carry):
            dst, dlb = carry
            gi = NGRP - 1 - i
            rows = _group_rows(gi)
            span = range(NB)
            qvs = [q_ref[r, :] for r in rows]
            ts = [_gates(qv, f_ref[r, :], lb) for qv, r in zip(qvs, rows)]
            ds = [_decays(s) for s in _exact_sums(smat, [t["g"] for t in ts])]
            vs = [_bf(i_ref[r, :]) for r in rows]
            dos = [_bf(do_ref[r, :]) for r in rows]
            sts = [st_ref[gi * NB + j] for j in span]
            qe_f = [t["qs"] * d["e_q"] for t, d in zip(ts, ds)]
            ke_f = [t["kk"] * d["e_k"] for t, d in zip(ts, ds)]
            q_e, k_e = [_bf(a) for a in qe_f], [_bf(a) for a in ke_f]
            q_m = [_bf(t["qs"] * d["e_qm"]) for t, d in zip(ts, ds)]
            k_m = [_bf(t["kk"] * d["e_km"]) for t, d in zip(ts, ds)]
            a = [_bf(jnp.where(causal, _dot(q_m[j], k_m[j], 1, 1), 0.0)) for j in span]
            da = [_bf(jnp.where(causal, _dot(dos[j], vs[j], 1, 1), 0.0)) for j in span]
            dqm = [_dot(da[j], k_m[j], 1, 0) for j in span]
            dkm = [_dot(da[j], q_m[j], 0, 0) for j in span]
            dv_in = [_dot(a[j], dos[j], 0, 0) for j in span]
            dqe = [_dot(dos[j], _bf(sts[j]), 1, 0) for j in span]
            grow = [_dot(dos[j], q_e[j], 0, 0) for j in span]
            dke, carried = [None] * NB, [None] * NB
            for j in reversed(span):
                dst_b = _bf(dst)
                dke[j] = _dot(vs[j], dst_b, 1, 0)
                dp_ref[2, rows[j], :] = _bf(dv_in[j] + _dot(k_e[j], dst_b, 1, 1))
                carried[j] = ds[j]["total8"] * jnp.sum(dst * sts[j], axis=0, keepdims=True)
                dst = dst * ds[j]["state"] + grow[j]
            kdk = [ke_f[j] * dke[j] for j in span]
            pos = [(q_m[j].astype(jnp.float32) * dqm[j] - k_m[j].astype(jnp.float32) * dkm[j]) + qe_f[j] * dqe[j]
                   for j in span]
            dgs = _exact_sums(rmat, [jnp.concatenate([pos[j], kdk[j]], axis=0) for j in span])
            for j in span:
                t, d = ts[j], ds[j]
                dg = dgs[j] + jnp.tile(carried[j], (CHUNK // 8, 1))
                dqs = dqm[j] * d["e_qm"] + dqe[j] * d["e_q"]
                dkk = dkm[j] * d["e_km"] + dke[j] * d["e_k"]
                df = dg / t["f"] - dkk
                dp_ref[1, rows[j], :] = _bf(df * (1.0 - lb) * (t["sg"] * (1.0 - t["sg"])))
                dp_ref[0, rows[j], :] = _bf(dqs * (t["sq"] * (1.0 + qvs[j] * (1.0 - t["sq"]))))
                dlb = dlb + df * (1.0 - t["sg"])
            return dst, dlb

        _, dlb = lax.fori_loop(0, NGRP, group, (jnp.zeros((HEAD, HEAD), jnp.float32),
                                                jnp.zeros((CHUNK, HEAD), jnp.float32)))
        dlb_ref[...] = jnp.sum(dlb, axis=0, keepdims=True)

    vec = pl.BlockSpec((1, HEAD), lambda h: (0, h))
    return pl.pallas_call(
        body, name="hgrn_bwd", grid=(NHEAD,),
        in_specs=[*_HEAD_SPECS,
                  pl.BlockSpec((2, HEAD), lambda h: (0, h)), vec,
                  pl.BlockSpec((T, HEAD), lambda h: (0, h)),
                  pl.BlockSpec((None, NCHUNK, HEAD, HEAD), lambda h: (h, 0, 0, 0)),
                  pl.BlockSpec((T, HEAD), lambda h: (0, NHEAD + h)), ANY, ANY],
        out_specs=(pl.BlockSpec((4, T, HEAD), lambda h: (0, 0, h)), vec, vec),
        out_shape=(pltpu.HBM((NSEG, T, D), jnp.bfloat16),
                   jax.ShapeDtypeStruct((1, D), jnp.float32), jax.ShapeDtypeStruct((1, D), jnp.float32)),
        scratch_shapes=[pltpu.VMEM((T, HEAD), jnp.float32)],
        input_output_aliases={9: 0},
        compiler_params=_params(("parallel",)),
    )(proj, proj, proj, proj, lb_logits, rec_g, o, states, dymix, dproj_in, token)


def _pool_bwd(proj, pool_w, pool_scale, dymix):
    def body(u_ref, pg_ref, w_ref, sc_ref, dy_ref, dp_ref, gw_ref, gs_ref):
        gidx = pl.program_id(0)
        u, pg = u_ref[...], pg_ref[...]
        d = _bf(_window_mean(_window_sum(u, gidx, _shift_down), gidx) - u)
        mixed = _dot(d, w_ref[...], 1, 0)
        spg = _sigmoid(pg)
        dyv = dy_ref[...]
        d_p = dyv * (pg * spg)
        dp_ref[1] = _bf(dyv * (mixed * sc_ref[...]) * (spg * (1.0 + pg * (1.0 - spg))))
        gs_ref[...] = jnp.sum(d_p * mixed, axis=0, keepdims=True)
        dmixed = _bf(d_p * sc_ref[...])
        gw_ref[...] = _dot(d, dmixed, 0, 0)
        dd = _dot(dmixed, w_ref[...], 1, 1)
        dp_ref[0] = _bf(_window_sum(_window_mean(dd, gidx), gidx, _shift_up) - dd)

    return pl.pallas_call(
        body, name="pool_bwd", grid=(NGROUP,),
        in_specs=[*_POOL_SPECS,
                  pl.BlockSpec((None, GROUP, GROUP), lambda g: (g, 0, 0)),
                  pl.BlockSpec((1, GROUP), lambda g: (0, g)),
                  pl.BlockSpec((T, GROUP), lambda g: (0, g))],
        out_specs=(pl.BlockSpec((2, T, GROUP), lambda g: (2, 0, g)),
                   pl.BlockSpec((None, GROUP, GROUP), lambda g: (g, 0, 0)),
                   pl.BlockSpec((1, GROUP), lambda g: (0, g))),
        out_shape=(pltpu.HBM((NSEG, T, D), jnp.bfloat16),
                   jax.ShapeDtypeStruct((NGROUP, GROUP, GROUP), jnp.float32),
                   jax.ShapeDtypeStruct((1, D), jnp.float32)),
        compiler_params=_params(("parallel",)),
    )(proj, proj, pool_w, pool_scale, dymix)


def _proj_bwd_w(place, ht, dproj):
    half = NTILE // 2

    def owner_chip(i, pr):
        return jnp.where(i < half, i // 3, (pr[1] + 1 + (i - half) // 3) % 4)

    def tile_of(i, pr):
        side = jnp.where(i < half, 1 - pr[2], pr[2])
        return 6 * owner_chip(i, pr) + 3 * side + i % 3

    def dproj_block(i, pr):
        j = tile_of(i, pr)
        return ((j // 4 + 4) % NSEG, 0, j % 4)

    def mine(i):
        return jnp.maximum(i, half)

    def body(place_ref, h_ref, dp_ref, sum_ref, own_ref, sendbuf, recvbuf, send_sems, recv_sems):
        i = pl.program_id(0)
        px, py, c, _ = _place()
        gw = _dot(h_ref[...], dp_ref[...], 1, 0)

        def to_sibling(slot):
            return pltpu.make_async_remote_copy(
                src_ref=sendbuf.at[slot], dst_ref=recvbuf.at[slot], send_sem=send_sems.at[slot],
                recv_sem=recv_sems.at[slot], device_id=(px, py, 1 - c), device_id_type=MESH)

        @pl.when(i < half)
        def _():
            sendbuf[i] = _bf(gw)
            to_sibling(i).start()

        @pl.when(i >= half)
        def _():
            slot = 3 * owner_chip(i, place_ref) + i % 3
            to_sibling(slot).wait_recv()
            total = gw + recvbuf[slot].astype(jnp.float32)
            sum_ref[...] = _bf(total)
            own_ref[...] = total

        @pl.when(i == NTILE - 1)
        def _():
            for slot in range(half):
                to_sibling(slot).wait_send()

    return pl.pallas_call(
        body, name="proj_bwd_w",
        grid_spec=pltpu.PrefetchScalarGridSpec(
            num_scalar_prefetch=1, grid=(NTILE,),
            in_specs=[pl.BlockSpec((D, T), lambda i, pr: (0, 0)),
                      pl.BlockSpec((None, T, TILE), lambda i, pr: dproj_block(i, pr))],
            out_specs=(pl.BlockSpec((None, None, D, TILE), lambda i, pr: (owner_chip(mine(i), pr), mine(i) % 3, 0, 0)),
                       pl.BlockSpec((None, D, TILE), lambda i, pr: (jnp.where(i < NTILE - 3, 0, i % 3), 0, 0))),
            scratch_shapes=[pltpu.VMEM((half, D, TILE), jnp.bfloat16), pltpu.VMEM((half, D, TILE), jnp.bfloat16),
                            pltpu.SemaphoreType.DMA((half,)), pltpu.SemaphoreType.DMA((half,))]),
        out_shape=(pltpu.HBM((4, 3, D, TILE), jnp.bfloat16), pltpu.HBM((3, D, TILE), jnp.float32)),
        compiler_params=_params(("arbitrary",)),
    )(place, ht, dproj)


def _proj_bwd_x(dproj, w_t, x, g1, dz, token):
    tm = 512
    pairs = NSEG // 2

    def body(dp_ref, w_ref, x_ref, g_ref, dz_ref, token_any, dx_ref, dg_ref, wcat, acc):
        del token_any
        m, s = pl.program_id(0), pl.program_id(1)
        r = None
        for k in range(2):
            for i in range(4):
                wcat[k, :, i * TILE:(i + 1) * TILE] = w_ref[4 * k + i]
            part = _dot(dp_ref[k], wcat[k], 1, 1)
            r = part if r is None else r + part

        @pl.when(s == 0)
        def _():
            acc[...] = r

        @pl.when(s != 0)
        def _():
            acc[...] += r

        @pl.when(s == pairs - 1)
        def _():
            xv = x_ref[...]
            rs = lax.rsqrt(jnp.mean(xv * xv, axis=-1, keepdims=True) + EPS)
            xhat = xv * rs
            dhv = acc[...]
            gdh = dhv * g_ref[...]
            dx_ref[...] = dz_ref[...] + rs * (gdh - xhat * jnp.mean(xhat * gdh, axis=-1, keepdims=True))
            dg = jnp.sum(xhat * dhv, axis=0, keepdims=True)

            @pl.when(m == 0)
            def _():
                dg_ref[...] = dg

            @pl.when(m != 0)
            def _():
                dg_ref[...] += dg

    rows = pl.BlockSpec((tm, D), lambda m, s: (m, 0))
    vec = pl.BlockSpec((1, D), lambda m, s: (0, 0))
    return pl.pallas_call(
        body, name="proj_bwd_x", grid=(T // tm, pairs),
        in_specs=[pl.BlockSpec((2, tm, D), lambda m, s: (s, m, 0)),
                  pl.BlockSpec((8, D, TILE), lambda m, s: ((s + 1) % pairs, 0, 0)), rows, vec, rows, ANY],
        out_specs=(rows, vec),
        out_shape=(jax.ShapeDtypeStruct((T, D), jnp.float32), jax.ShapeDtypeStruct((1, D), jnp.float32)),
        scratch_shapes=[pltpu.VMEM((2, D, D), jnp.bfloat16), pltpu.VMEM((tm, D), jnp.float32)],
        compiler_params=_params(("arbitrary", "arbitrary"), vmem_mib=56),
    )(dproj, w_t, x, g1, dz, token)


def _adamw(w, g, m, v):
    m_new = ADAM_B1 * m + (1.0 - ADAM_B1) * g
    v_new = ADAM_B2 * v + (1.0 - ADAM_B2) * (g * g)
    delta = -ADAM_LR * ((m_new / BC1) / (jnp.sqrt(v_new / BC2) + ADAM_EPS) + ADAM_WD * w)
    return delta, m_new, v_new


def _reduce_adam(name, place, parts, w, m, v, grid, w_spec):
    n = len(parts)

    def body(place_ref, *refs):
        del place_ref
        w_ref, m_ref, v_ref, g_ref, d_ref, mo_ref, vo_ref = refs[n:]
        g = None
        for ref, (_, _, stacked) in zip(refs[:n], parts):
            terms = [ref[r] for r in range(ref.shape[0])] if stacked else [ref[...]]
            for t in terms:
                g = t.astype(jnp.float32) if g is None else g + t.astype(jnp.float32)
        delta, m_new, v_new = _adamw(w_ref[...], g, m_ref[...], v_ref[...])
        g_ref[...] = g
        d_ref[...] = delta
        mo_ref[...] = m_new
        vo_ref[...] = v_new

    shape = jax.ShapeDtypeStruct(w.shape, jnp.float32)
    return pl.pallas_call(
        body, name=name,
        grid_spec=pltpu.PrefetchScalarGridSpec(
            num_scalar_prefetch=1, grid=grid,
            in_specs=[spec for _, spec, _ in parts] + [w_spec] * 3, out_specs=(w_spec,) * 4),
        out_shape=(shape,) * 4,
        compiler_params=_params(("parallel",)),
    )(place, *[_in_hbm(a) for a in [a for a, _, _ in parts] + [w, m, v]])


def _small_adam(place, own, parts, w, m, v):
    def body(place_ref, own_ref, p_ref, w_ref, m_ref, v_ref, g_ref, d_ref, mo_ref, vo_ref):
        me = place_ref[0]
        g = None
        for s in range(NDEV):
            term = jnp.where(me == s, own_ref[...], p_ref[s])
            g = term if g is None else g + term
        wv = w_ref[...]
        rows = _row_ids(wv.shape)
        other = jnp.where(rows == 2, pltpu.roll(wv, 7, 0), jnp.where(rows == 3, pltpu.roll(wv, 1, 0), 0.0))
        lbv = _sigmoid(wv - other)
        sign = jnp.where(rows == 2, 1.0, -1.0)
        g = jnp.where((rows == 2) | (rows == 3), sign * g * lbv * (1.0 - lbv), g)
        delta, m_new, v_new = _adamw(wv, g, m_ref[...], v_ref[...])
        g_ref[...] = g
        d_ref[...] = delta
        mo_ref[...] = m_new
        vo_ref[...] = v_new

    shape = jax.ShapeDtypeStruct((8, D), jnp.float32)
    vmem = pl.BlockSpec(memory_space=pltpu.VMEM)
    return pl.pallas_call(
        body, name="small_adam", out_shape=(shape,) * 4,
        in_specs=[pl.BlockSpec(memory_space=pltpu.SMEM)] + [vmem] * 5, out_specs=(vmem,) * 4,
    )(place, own, parts, w, m, v)


def _rows8(*vecs):
    rows = [a.reshape(-1, D) for a in vecs]
    n = sum(r.shape[0] for r in rows)
    return jnp.concatenate(rows + [jnp.zeros((8 - n, D), jnp.float32)], axis=0)


def kernel(x, norm1_g, w_in, pool_w, pool_scale, lb_logits, rec_norm_g, w_out, final_norm_g, loss_target, m_norm1_g, m_w_in, m_pool_w, m_pool_scale, m_lb_logits, m_rec_norm_g, m_w_out, m_final_norm_g, v_norm1_g, v_w_in, v_pool_w, v_pool_scale, v_lb_logits, v_rec_norm_g, v_w_out, v_final_norm_g):
    xs = x[0]
    target = loss_target[0]
    ix, iy, ic = lax.axis_index("x"), lax.axis_index("y"), lax.axis_index("c")
    place = jnp.stack([4 * ix + 2 * iy + ic, 2 * ix + iy, ic]).astype(jnp.int32)
    gf = final_norm_g.reshape(1, D)

    ht, w_t, w_out_b, w_out_g, pool_g, proj = _gather_proj(xs, norm1_g, w_in, w_out, pool_w)
    pool_full = pool_g.transpose(1, 0, 2, 3).reshape(NGROUP, GROUP, GROUP)
    wout = [w_out_b, w_out_g]
    wout_send, wout_recv, wout, wout_token = _split_start("gather_wout_start", wout, NDEV - 1, _plan_wout)

    y = _pool_fwd(proj, pool_full, pool_scale, wout_token)
    y, o, states = _hgrn_fwd(proj, lb_logits, rec_norm_g, y)
    _, w_out_g = _split_wait("gather_wout_wait", wout, wout_send, wout_recv, _plan_wout, o)
    w_out_full = _in_hbm(w_out_g.reshape(DMIX, D))
    dz, dzb, sq, dgf = _out_proj_loss(xs, y, w_out_full, target, gf)

    dymix, gwout_f, gwout_b = _out_proj_bwd(dzb, w_out_full, y)
    dproj, gpool, dscale = _pool_bwd(proj, pool_full, pool_scale, dymix)

    blk_out = (NDEV, DMIX // NDEV, D)
    blk_pool = (NDEV, NGROUP, GROUP // NDEV, GROUP)
    gpool_s = gpool.reshape(NGROUP, NDEV, GROUP // NDEV, GROUP).transpose(1, 0, 2, 3)
    rest = [gwout_b.reshape(blk_out), gpool_s,
            lax.empty((NDEV - 1,) + blk_out[1:], jnp.bfloat16), lax.empty((NDEV - 1,) + blk_pool[1:], jnp.float32)]
    rest_send, rest_recv, rest, rest_token = _split_start("scatter_rest_start", rest, 2 * (NDEV - 1), _plan_rest)

    dproj, drecg, dlb = _hgrn_bwd(proj, lb_logits, rec_norm_g, o, states, dymix, dproj, rest_token)
    chip_sums, own_sum = _proj_bwd_w(place, ht, dproj)
    win = [chip_sums, lax.empty((3, 3, D, TILE), jnp.bfloat16)]
    win_send, win_recv, win, win_token = _split_start("scatter_win_start", win, 3, _plan_in)

    grad_x, dg1 = _proj_bwd_x(dproj, w_t, xs, norm1_g, dz, win_token)

    small = [_rows8(dg1, dscale, dlb, dlb, drecg, dgf, sq), lax.empty((NDEV, 8, D), jnp.float32)]
    small_send, small_recv, small, small_token = _split_start("gather_small_start", small, NDEV - 1, _plan_small)

    _, gpool_own, r_out, r_pool = _split_wait("scatter_rest_wait", rest, rest_send, rest_recv, _plan_rest,
                                              small_token)
    g_wout, d_wout, m_wout, v_wout = _reduce_adam(
        "adam_w_out", place,
        [(gwout_f.reshape(blk_out), pl.BlockSpec((None,) + blk_out[1:], lambda i, pr: (pr[0], 0, 0)), False),
         (r_out, pl.BlockSpec((NDEV - 1,) + blk_out[1:], lambda i, pr: (0, 0, 0)), True)],
        w_out, m_w_out, v_w_out, (1,), pl.BlockSpec((None,) + blk_out[1:], lambda i, pr: (0, 0, 0)))
    g_pool, d_pool, m_pool, v_pool = _reduce_adam(
        "adam_pool_w", place,
        [(gpool_own, pl.BlockSpec((None,) + blk_pool[1:], lambda i, pr: (pr[0], 0, 0, 0)), False),
         (r_pool, pl.BlockSpec((NDEV - 1,) + blk_pool[1:], lambda i, pr: (0, 0, 0, 0)), True)],
        pool_w, m_pool_w, v_pool_w, (1,), pl.BlockSpec((None,) + blk_pool[1:], lambda i, pr: (0, 0, 0, 0)))

    _, r_in = _split_wait("scatter_win_wait", win, win_send, win_recv, _plan_in, d_wout)
    g_win, d_win, m_win, v_win = _reduce_adam(
        "adam_w_in", place,
        [(own_sum, pl.BlockSpec((None, D, TILE), lambda i, pr: (i, 0, 0)), False),
         (r_in, pl.BlockSpec((3, None, D, TILE), lambda i, pr: (0, i, 0, 0)), True)],
        w_in, m_w_in, v_w_in, (3,), pl.BlockSpec((None, D, TILE), lambda i, pr: (0, 0, i)))

    own_small, r_small = _split_wait("gather_small_wait", small, small_send, small_recv, _plan_small, d_win)
    g_s, d_s, m_s, v_s = _small_adam(
        place, own_small, r_small,
        _rows8(norm1_g, pool_scale, lb_logits, rec_norm_g, final_norm_g),
        _rows8(m_norm1_g, m_pool_scale, m_lb_logits, m_rec_norm_g, m_final_norm_g),
        _rows8(v_norm1_g, v_pool_scale, v_lb_logits, v_rec_norm_g, v_final_norm_g))
    loss = jnp.sum(g_s[6]) * (0.5 / D)

    def small_outs(a):
        return a[0:1], a[1:2], a[2:4], a[4:5], a[5]

    def outs(small_a, win, pool, wout):
        n1, ps, lbl, rg, fg = small_outs(small_a)
        return n1, win, pool, ps, lbl, rg, wout, fg

    return (loss, grad_x[None],
            *outs(g_s, g_win, g_pool, g_wout), *outs(d_s, d_win, d_pool, d_wout),
            *outs(m_s, m_win, m_pool, m_wout), *outs(v_s, v_win, v_pool, v_wout))
```

```python
import functools

import jax
import jax.numpy as jnp
from jax import lax
from jax.experimental import pallas as pl
from jax.experimental.pallas import tpu as pltpu

T = 2048
D = 1024
NSEG = 6
NTILE = 24
TILE = 256
DMIX = 2048
NDEV = 8
HEAD = 128
NHEAD = 8
CHUNK = 64
NCHUNK = T // CHUNK
NB = 32
NGRP = NCHUNK // NB
NGROUP = 4
GROUP = 256
EPS = 1e-6
EXP_CAP = 80.0
MESH = pl.DeviceIdType.MESH
AXES = ("x", "y", "c")
ANY = pl.BlockSpec(memory_space=pl.ANY)
HBM = pl.BlockSpec(memory_space=pltpu.HBM)
SEM = pl.BlockSpec(memory_space=pltpu.SEMAPHORE)
EFFECT = pltpu.SideEffectType.DATAFLOW_SIDE_EFFECTING

ADAM_LR = 0.001
ADAM_B1 = 0.9
ADAM_B2 = 0.999
ADAM_EPS = 1e-08
ADAM_WD = 0.01
ADAM_STEP = 10
BC1 = 1.0 - ADAM_B1 ** ADAM_STEP
BC2 = 1.0 - ADAM_B2 ** ADAM_STEP

MIB = 1 << 20


def _params(sem=None, vmem_mib=48):
    return pltpu.CompilerParams(dimension_semantics=sem, vmem_limit_bytes=vmem_mib * MIB)


def _sigmoid(v):
    return 1.0 / (1.0 + jnp.exp(-v))


def _dot(a, b, ca, cb, precision=None):
    return lax.dot_general(a, b, (((ca,), (cb,)), ((), ())), precision=precision,
                           preferred_element_type=jnp.float32)


def _bf(v):
    return v.astype(jnp.bfloat16)


def _in_hbm(a):
    return pltpu.with_memory_space_constraint(a, pltpu.HBM)


def _place():
    x, y, c = lax.axis_index("x"), lax.axis_index("y"), lax.axis_index("c")
    return x, y, c, 4 * x + 2 * y + c


def _peer(x, y, c, r):
    return (x ^ ((r >> 2) & 1), y ^ ((r >> 1) & 1), c ^ (r & 1))


def _gather_proj(x, g1, w_in, w_out, pool_w):
    def body(x_ref, g_ref, win_ref, wout_ref, pool_ref, ht_o, wt_o, woutb_o, wout_o, pool_o, proj_o,
             hv, htv, wv, wob, pb, stage, send_sems, recv_sems, loc_sems, out_sems):
        px, py, c, my_idx = _place()
        me, sibling = (px, py, c), (px, py, 1 - c)
        chips = [(1 - px, py), (px, 1 - py), (1 - px, 1 - py)]
        for p in range(3):
            wv[3 * my_idx + p] = _bf(win_ref[0, :, p * TILE:(p + 1) * TILE])

        def index(bx, by, bc):
            return 4 * bx + 2 * by + bc

        def slot(w, block):
            return wv.at[pl.ds(3 * index(*block), 3)] if w == 0 else pool_o.at[index(*block)]

        def copy(k, w, block, to, src=None):
            return pltpu.make_async_remote_copy(
                src_ref=slot(w, block) if src is None else src, dst_ref=slot(w, block),
                send_sem=send_sems.at[2 * k + w], recv_sem=recv_sems.at[2 * k + w],
                device_id=to, device_id_type=MESH)

        def save(block):
            at = pl.ds(3 * index(*block), 3)
            pltpu.make_async_copy(wv.at[at], wt_o.at[at], loc_sems.at[4]).start()

        srcs = (slot(0, me), pb)
        first = []
        for w in (0, 1):
            if w == 1:
                pb[...] = _bf(pool_ref[0])
                wob[...] = _bf(wout_ref[0])
            group = [copy(1 + j, w, me, (*chip, c), src=srcs[w]) for j, chip in enumerate(chips[:2])]
            group.append(copy(0, w, me, sibling, src=srcs[w]))
            for cp in group:
                cp.start()
            first += group
        save(me)
        locs = [pltpu.make_async_copy(pb, slot(1, me), loc_sems.at[0]),
                pltpu.make_async_copy(wob, wout_o.at[my_idx], loc_sems.at[1]),
                pltpu.make_async_copy(wob, woutb_o, loc_sems.at[2])]
        for cp in locs:
            cp.start()

        xv = x_ref[...]
        hv[...] = _bf(xv * lax.rsqrt(jnp.mean(xv * xv, axis=-1, keepdims=True) + EPS) * g_ref[...])
        rows = 256
        for r0 in range(0, T, rows):
            htv[:, r0:r0 + rows] = hv[r0:r0 + rows, :].T
        locs.append(pltpu.make_async_copy(htv, ht_o, loc_sems.at[3]))
        locs[-1].start()

        def out_copy(p, j):
            return pltpu.make_async_copy(stage.at[p], proj_o.at[j], out_sems.at[p])

        def project(nth, block):
            base = 3 * index(*block)

            def tile(p, carry):
                if nth > 0:
                    out_copy(p, base + p).wait()
                stage[p] = _dot(hv[...], wv[base + p], 1, 0)
                out_copy(p, base + p).start()
                return carry

            lax.fori_loop(0, 3, tile, 0)

        project(0, me)
        copy(0, 0, sibling, me).wait_recv()
        save(sibling)
        project(1, sibling)
        passed = []
        relay_from = (px ^ (1 - c), py ^ c, c)
        relay_to = (px ^ c, py ^ (1 - c), c)

        def arrived(w, j):
            copy(1 + j, w, (*chips[j], c), me).wait_recv()
            passed.append(copy(4 + j, w, (*chips[j], c), sibling))
            passed[-1].start()

        def relay(w):
            passed.append(copy(3, w, relay_from, relay_to))
            passed[-1].start()

        def handed(nth, j):
            copy(4 + j, 0, (*chips[j], 1 - c), me).wait_recv()
            save((*chips[j], 1 - c))
            project(nth, (*chips[j], 1 - c))

        arrived(0, 0)
        arrived(0, 1)
        relay(0)
        for j in range(2):
            save((*chips[j], c))
            project(2 + j, (*chips[j], c))
        handed(4, 0)
        handed(5, 1)
        arrived(1, 0)
        arrived(1, 1)
        relay(1)
        arrived(0, 2)
        save((*chips[2], c))
        project(6, (*chips[2], c))
        handed(7, 2)
        arrived(1, 2)
        copy(0, 1, sibling, me).wait_recv()
        for j, chip in enumerate(chips):
            copy(4 + j, 1, (*chip, 1 - c), me).wait_recv()
        keep = pltpu.make_async_copy(wv, wt_o, loc_sems.at[4])
        for p in range(3):
            out_copy(p, p).wait()
        for cp in first + passed:
            cp.wait_send()
        keep.wait()
        for cp in locs:
            cp.wait()

    vmem = pl.BlockSpec(memory_space=pltpu.VMEM)
    bf16 = jnp.bfloat16
    return pl.pallas_call(
        body, name="gather_proj",
        out_shape=(pltpu.HBM((D, T), bf16), pltpu.HBM((NTILE, D, TILE), bf16),
                   pltpu.HBM((DMIX // NDEV, D), bf16), pltpu.HBM((NDEV, DMIX // NDEV, D), bf16),
                   pltpu.HBM((NDEV, NGROUP, GROUP // NDEV, GROUP), bf16), pltpu.HBM((NTILE, T, TILE), jnp.float32)),
        in_specs=[vmem] * 5, out_specs=(ANY,) * 6,
        scratch_shapes=[pltpu.VMEM((T, D), bf16), pltpu.VMEM((D, T), bf16), pltpu.VMEM((NTILE, D, TILE), bf16),
                        pltpu.VMEM((DMIX // NDEV, D), bf16), pltpu.VMEM((NGROUP, GROUP // NDEV, GROUP), bf16),
                        pltpu.VMEM((3, T, TILE), jnp.float32),
                        pltpu.SemaphoreType.DMA((14,)), pltpu.SemaphoreType.DMA((14,)),
                        pltpu.SemaphoreType.DMA((5,)), pltpu.SemaphoreType.DMA((3,))],
        compiler_params=_params(vmem_mib=56),
    )(x, g1, w_in, w_out, pool_w)


def _split_start(name, arrays, n_copies, plan):
    k = len(arrays)

    def body(*refs):
        send_sems, recv_sems, token = refs[k], refs[k + 1], refs[-1]
        for i, (src, dst, to) in enumerate(plan(refs[:k])):
            pltpu.make_async_remote_copy(src_ref=src, dst_ref=dst, send_sem=send_sems.at[i],
                                         recv_sem=recv_sems.at[i], device_id=to, device_id_type=MESH).start()
        token[...] = jnp.zeros_like(token)

    out = pl.pallas_call(
        body, name=name,
        out_shape=(pltpu.SemaphoreType.DMA((n_copies,)), pltpu.SemaphoreType.DMA((n_copies,)),
                   *[pltpu.HBM(a.shape, a.dtype) for a in arrays], jax.ShapeDtypeStruct((8, 128), jnp.float32)),
        in_specs=[HBM] * k, out_specs=(SEM, SEM, *[HBM] * k, pl.BlockSpec(memory_space=pltpu.VMEM)),
        input_output_aliases={i: 2 + i for i in range(k)},
        compiler_params=pltpu.CompilerParams(has_side_effects=EFFECT),
    )(*[pltpu.with_memory_space_constraint(a, pltpu.HBM) for a in arrays])
    return out[0], out[1], out[2:2 + k], out[-1]


def _split_wait(name, arrays, send_sems, recv_sems, plan, after):
    k = len(arrays)

    def body(*refs):
        sends, recvs = refs[k], refs[k + 1]
        for i, (src, dst, to) in enumerate(plan(refs[:k])):
            cp = pltpu.make_async_remote_copy(src_ref=src, dst_ref=dst, send_sem=sends.at[i], recv_sem=recvs.at[i],
                                              device_id=to, device_id_type=MESH)
            cp.wait_send()
            cp.wait_recv()

    return pl.pallas_call(
        body, name=name,
        out_shape=tuple(pltpu.HBM(a.shape, a.dtype) for a in arrays),
        in_specs=[HBM] * k + [SEM, SEM, ANY], out_specs=(HBM,) * k,
        input_output_aliases={i: i for i in range(k)},
        compiler_params=pltpu.CompilerParams(has_side_effects=EFFECT),
    )(*arrays, send_sems, recv_sems, after)


def _plan_wout(refs):
    src, land = refs
    x, y, c, me = _place()
    return [(src, land.at[me], _peer(x, y, c, r)) for r in range(1, NDEV)]


def _plan_rest(refs):
    gob, gpf, r_out, r_pool = refs
    x, y, c, me = _place()
    plan = []
    for r in range(1, NDEV):
        plan.append((gob.at[me ^ r], r_out.at[r - 1], _peer(x, y, c, r)))
        plan.append((gpf.at[me ^ r], r_pool.at[r - 1], _peer(x, y, c, r)))
    return plan


def _plan_in(refs):
    sum_b, r_in = refs
    x, y, c, _ = _place()
    plan = []
    for j, (dx, dy) in enumerate(((1, 0), (0, 1), (1, 1))):
        px, py = x ^ dx, y ^ dy
        plan.append((sum_b.at[2 * px + py], r_in.at[j], (px, py, c)))
    return plan


def _gather_small(small, *after):
    def body(sm, *refs):
        r_small, send_sems, recv_sems, loc_sem = refs[len(after):]
        x, y, c, me = _place()
        loc = pltpu.make_async_copy(sm, r_small.at[me], loc_sem)
        loc.start()

        def copy(r, src_idx):
            return pltpu.make_async_remote_copy(
                src_ref=sm, dst_ref=r_small.at[src_idx], send_sem=send_sems.at[r - 1], recv_sem=recv_sems.at[r - 1],
                device_id=_peer(x, y, c, r), device_id_type=MESH)

        sends = [copy(r, me) for r in range(1, NDEV)]
        for cp in sends:
            cp.start()
        for r in range(1, NDEV):
            copy(r, me ^ r).wait_recv()
        for cp in sends:
            cp.wait_send()
        loc.wait()

    return pl.pallas_call(
        body, name="gather_small",
        out_shape=jax.ShapeDtypeStruct((NDEV, 8, D), jnp.float32),
        in_specs=[ANY] * (1 + len(after)), out_specs=ANY,
        scratch_shapes=[pltpu.SemaphoreType.DMA((NDEV - 1,)), pltpu.SemaphoreType.DMA((NDEV - 1,)),
                        pltpu.SemaphoreType.DMA],
    )(small, *after)


def _seg_tiles(s):
    return (s + 2) % NSEG


_POOL_SPECS = [pl.BlockSpec((None, T, GROUP), lambda g, base=base: (base + g, 0, 0)) for base in (0, 4)]
_HEAD_SPECS = [pl.BlockSpec((None, T, HEAD), lambda h, base=base: (base + h // 2, 0, h % 2))
               for base in (8, 12, 16, 20)]


def _row_ids(shape):
    return lax.broadcasted_iota(jnp.int32, shape, 0)


EDGE = 8


def _shift_down(a, k):
    r = pltpu.roll(a, k, 0)
    edge = _row_ids((EDGE, a.shape[1]))
    return jnp.concatenate([jnp.where(edge >= k, r[:EDGE], 0.0), r[EDGE:]], axis=0)


def _shift_up(a, k):
    r = pltpu.roll(a, T - k, 0)
    edge = _row_ids((EDGE, a.shape[1]))
    return jnp.concatenate([r[:T - EDGE], jnp.where(edge < EDGE - k, r[T - EDGE:], 0.0)], axis=0)


def _window_sum(u, gidx, shift):
    s2 = u + shift(u, 1)
    s4 = s2 + shift(s2, 2)
    s8 = s4 + shift(s4, 4)
    s16 = s8 + shift(s8, 8)
    return jnp.where(gidx == 0, s2, jnp.where(gidx == 1, s4, jnp.where(gidx == 2, s8, s16)))


def _window_mean(s, gidx):
    inv = jnp.where(gidx == 0, 0.5, jnp.where(gidx == 1, 0.25, jnp.where(gidx == 2, 0.125, 0.0625)))
    width = lax.shift_left(jnp.int32(2), gidx)
    head = s[:16] / jnp.minimum(_row_ids((16, s.shape[1])) + 1, width).astype(jnp.float32)
    return jnp.concatenate([head, s[16:] * inv], axis=0)


def _pool_fwd(proj, pool_w, pool_scale, token):
    def body(u_ref, pg_ref, w_ref, sc_ref, token_any, y_ref):
        del token_any
        gidx = pl.program_id(0)
        u, pg = u_ref[...], pg_ref[...]
        d = _window_mean(_window_sum(u, gidx, _shift_down), gidx) - u
        mixed = _dot(_bf(d), w_ref[...], 1, 0)
        y_ref[...] = _bf(mixed * sc_ref[...] * (pg * _sigmoid(pg)))

    return pl.pallas_call(
        body, name="pool_fwd", grid=(NGROUP,),
        in_specs=[*_POOL_SPECS,
                  pl.BlockSpec((None, GROUP, GROUP), lambda g: (g, 0, 0)),
                  pl.BlockSpec((1, GROUP), lambda g: (0, g)), ANY],
        out_specs=pl.BlockSpec((T, GROUP), lambda g: (0, g)),
        out_shape=pltpu.HBM((T, DMIX), jnp.bfloat16),
        compiler_params=_params(("parallel",)),
    )(proj, proj, pool_w, pool_scale, token)


def _tri(lower):
    r = lax.broadcasted_iota(jnp.int32, (CHUNK, CHUNK), 0)
    c = lax.broadcasted_iota(jnp.int32, (CHUNK, CHUNK), 1)
    return (r >= c) if lower else (r <= c)


def _sum_rows_matrix():
    shape = (CHUNK + 16, CHUNK)
    r, c = lax.broadcasted_iota(jnp.int32, shape, 0), lax.broadcasted_iota(jnp.int32, shape, 1)
    run = jnp.where(c <= r, 1.0, 0.0)
    half = jnp.where(c < CHUNK // 2, 1.0, 0.0)
    return _bf(jnp.where(r < CHUNK, run, jnp.where(r < CHUNK + 8, 1.0, half)))


def _rev_sum_matrix():
    shape = (CHUNK, 2 * CHUNK)
    r, c = lax.broadcasted_iota(jnp.int32, shape, 0), lax.broadcasted_iota(jnp.int32, shape, 1)
    return _bf(jnp.where(c < CHUNK, jnp.where(c >= r, 1.0, 0.0), jnp.where(c - CHUNK < r, 1.0, 0.0)))


def _split2(a):
    hi = _bf(a)
    return [hi, _bf(a - hi.astype(jnp.float32))]


def _exact_sums(mat, pieces):
    x = jnp.concatenate([s for p in pieces for s in _split2(p)], axis=1)
    r = _dot(mat, x, 1, 0)
    return [r[:, 2 * j * HEAD:(2 * j + 1) * HEAD] + r[:, (2 * j + 1) * HEAD:(2 * j + 2) * HEAD]
            for j in range(len(pieces))]


def _gates(qv, fl, lb):
    sq = _sigmoid(qv)
    sg = _sigmoid(fl)
    f = lb + (1.0 - lb) * sg
    return dict(sq=sq, qs=qv * sq, sg=sg, f=f, kk=1.0 - f, g=jnp.log(f))


def _decays(sums):
    big_g = sums[:CHUNK]
    total = sums[CHUNK:CHUNK + 8]
    g_last = jnp.tile(total, (CHUNK // 8, 1))
    g_mid = jnp.tile(sums[CHUNK + 8:], (CHUNK // 8, 1))
    return dict(
        e_q=jnp.exp(big_g),
        e_k=jnp.exp(g_last - big_g),
        e_qm=jnp.exp(jnp.minimum(big_g - g_mid, EXP_CAP)),
        e_km=jnp.exp(jnp.minimum(g_mid - big_g, EXP_CAP)),
        total8=jnp.exp(total),
        state=jnp.exp(jnp.tile(total, (HEAD // 8, 1))))


def _group_rows(gi):
    return [pl.ds(pl.multiple_of((gi * NB + j) * CHUNK, CHUNK), CHUNK) for j in range(NB)]


def _lower_bound(lb_ref):
    return _sigmoid(lb_ref[0:1, :] - lb_ref[1:2, :])


def _hgrn_fwd(proj, lb_logits, rec_g, y_in):
    def body(q_ref, f_ref, i_ref, gate_ref, lb_ref, rg_ref, y_any, y_ref, o_ref, st_ref):
        del y_any
        lb = _lower_bound(lb_ref)
        causal = _tri(True)
        smat = _sum_rows_matrix()

        def group(gi, st):
            rows = _group_rows(gi)
            ts = [_gates(q_ref[r, :], f_ref[r, :], lb) for r in rows]
            ds = [_decays(s) for s in _exact_sums(smat, [t["g"] for t in ts])]
            vs = [_bf(i_ref[r, :]) for r in rows]
            q_m = [_bf(t["qs"] * d["e_qm"]) for t, d in zip(ts, ds)]
            k_m = [_bf(t["kk"] * d["e_km"]) for t, d in zip(ts, ds)]
            q_e = [_bf(t["qs"] * d["e_q"]) for t, d in zip(ts, ds)]
            k_e = [_bf(t["kk"] * d["e_k"]) for t, d in zip(ts, ds)]
            a = [_bf(jnp.where(causal, _dot(q_m[j], k_m[j], 1, 1), 0.0)) for j in range(NB)]
            intra = [_dot(a[j], vs[j], 1, 0) for j in range(NB)]
            upd = [_dot(vs[j], k_e[j], 0, 0) for j in range(NB)]
            for j in range(NB):
                st_ref[gi * NB + j] = st
                o_ref[rows[j], :] = intra[j] + _dot(q_e[j], _bf(st), 1, 1)
                st = st * ds[j]["state"] + upd[j]
            return st

        lax.fori_loop(0, NGRP, group, jnp.zeros((HEAD, HEAD), jnp.float32))
        o = o_ref[...]
        rn = o * lax.rsqrt(jnp.mean(o * o, axis=-1, keepdims=True) + EPS)
        gate = gate_ref[...]
        y_ref[...] = _bf(rn * rg_ref[...] * (gate * _sigmoid(gate)))

    return pl.pallas_call(
        body, name="hgrn_fwd", grid=(NHEAD,),
        in_specs=[*_HEAD_SPECS,
                  pl.BlockSpec((2, HEAD), lambda h: (0, h)),
                  pl.BlockSpec((1, HEAD), lambda h: (0, h)),
                  pl.BlockSpec(memory_space=pl.ANY)],
        out_specs=(pl.BlockSpec((T, HEAD), lambda h: (0, NHEAD + h)),
                   pl.BlockSpec((T, HEAD), lambda h: (0, h)),
                   pl.BlockSpec((None, NCHUNK, HEAD, HEAD), lambda h: (h, 0, 0, 0))),
        out_shape=(pltpu.HBM((T, DMIX), jnp.bfloat16), pltpu.HBM((T, D), jnp.float32),
                   pltpu.HBM((NHEAD, NCHUNK, HEAD, HEAD), jnp.float32)),
        input_output_aliases={6: 0},
        compiler_params=_params(("parallel",)),
    )(proj, proj, proj, proj, lb_logits, rec_g, y_in)


def _out_proj_loss(x, y, w_out, target, gf):
    rows = 512
    parts = [slice(k * rows // 2, (k + 1) * rows // 2) for k in range(2)]

    def body(x_ref, y_ref, w_ref, t_ref, g_ref, dz_ref, dzb_ref, sq_ref, dg_ref):
        zs = [x_ref[p, :] + _dot(y_ref[p, :], w_ref[...], 1, 0) for p in parts]
        sq = dg = 0.0
        for p, z in zip(parts, zs):
            r = lax.rsqrt(jnp.mean(z * z, axis=-1, keepdims=True) + EPS)
            zhat = z * r
            err = zhat * g_ref[...] - t_ref[p, :]
            dy = err * (1.0 / D)
            gdy = dy * g_ref[...]
            dz = r * (gdy - zhat * jnp.mean(zhat * gdy, axis=-1, keepdims=True))
            dz_ref[p, :] = dz
            dzb_ref[p, :] = _bf(dz)
            sq = sq + jnp.sum(err * err, axis=0, keepdims=True)
            dg = dg + jnp.sum(zhat * dy, axis=0, keepdims=True)

        @pl.when(pl.program_id(0) == 0)
        def _():
            sq_ref[...] = sq
            dg_ref[...] = dg

        @pl.when(pl.program_id(0) != 0)
        def _():
            sq_ref[...] += sq
            dg_ref[...] += dg

    tile = pl.BlockSpec((rows, D), lambda i: (i, 0))
    vec = pl.BlockSpec((1, D), lambda i: (0, 0))
    return pl.pallas_call(
        body, name="out_proj_loss", grid=(T // rows,),
        in_specs=[tile, pl.BlockSpec((rows, DMIX), lambda i: (i, 0)), pl.BlockSpec((DMIX, D), lambda i: (0, 0)),
                  tile, vec],
        out_specs=(tile, tile, vec, vec),
        out_shape=(pltpu.HBM((T, D), jnp.float32), pltpu.HBM((T, D), jnp.bfloat16),
                   jax.ShapeDtypeStruct((1, D), jnp.float32), jax.ShapeDtypeStruct((1, D), jnp.float32)),
        compiler_params=_params(("arbitrary",)),
    )(x, y, w_out, target, gf)


def _out_proj_bwd(dzb, w_out, y):
    tn = 512

    def body(dz_ref, w_ref, y_ref, dy_ref, gw_ref, gwb_ref):
        dz = dz_ref[...]
        dy_ref[...] = _dot(dz, w_ref[...], 1, 1)
        gw = _dot(y_ref[...], dz, 0, 0)
        gw_ref[...] = gw
        gwb_ref[...] = _bf(gw)

    return pl.pallas_call(
        body, name="out_proj_bwd", grid=(DMIX // tn,),
        in_specs=[pl.BlockSpec((T, D), lambda n: (0, 0)), pl.BlockSpec((tn, D), lambda n: (n, 0)),
                  pl.BlockSpec((T, tn), lambda n: (0, n))],
        out_specs=(pl.BlockSpec((T, tn), lambda n: (0, n)), pl.BlockSpec((tn, D), lambda n: (n, 0)),
                   pl.BlockSpec((tn, D), lambda n: (n, 0))),
        out_shape=(pltpu.HBM((T, DMIX), jnp.float32), pltpu.HBM((DMIX, D), jnp.float32),
                   pltpu.HBM((DMIX, D), jnp.bfloat16)),
        compiler_params=_params(("parallel",)),
    )(dzb, w_out, y)


def _hgrn_bwd(proj, lb_logits, rec_g, o, states, dymix, dproj_in, token):
    def body(q_ref, f_ref, i_ref, gate_ref, lb_ref, rg_ref, o_ref, st_ref, dy_ref, dp_any, token_any,
             dp_ref, drg_ref, dlb_ref, do_ref):
        del dp_any, token_any
        lb = _lower_bound(lb_ref)
        causal = _tri(True)
        smat, rmat = _sum_rows_matrix(), _rev_sum_matrix()

        o = o_ref[...]
        rs = lax.rsqrt(jnp.mean(o * o, axis=-1, keepdims=True) + EPS)
        rn = o * rs
        gate = gate_ref[...]
        sgate = _sigmoid(gate)
        dyv = dy_ref[...]
        d_r = dyv * (gate * sgate)
        dp_ref[3] = _bf(dyv * (rn * rg_ref[...]) * (sgate * (1.0 + gate * (1.0 - sgate))))
        drg_ref[...] = jnp.sum(d_r * rn, axis=0, keepdims=True)
        drn = d_r * rg_ref[...]
        do_ref[...] = rs * (drn - rn * jnp.mean(rn * drn, axis=-1, keepdims=True))

        def group(i, carry):
            dst, dlb = carry
            gi = NGRP - 1 - i
            rows = _group_rows(gi)
            span = range(NB)
            qvs = [q_ref[r, :] for r in rows]
            ts = [_gates(qv, f_ref[r, :], lb) for qv, r in zip(qvs, rows)]
            ds = [_decays(s) for s in _exact_sums(smat, [t["g"] for t in ts])]
            vs = [_bf(i_ref[r, :]) for r in rows]
            dos = [_bf(do_ref[r, :]) for r in rows]
            sts = [st_ref[gi * NB + j] for j in span]
            qe_f = [t["qs"] * d["e_q"] for t, d in zip(ts, ds)]
            ke_f = [t["kk"] * d["e_k"] for t, d in zip(ts, ds)]
            q_e, k_e = [_bf(a) for a in qe_f], [_bf(a) for a in ke_f]
            q_m = [_bf(t["qs"] * d["e_qm"]) for t, d in zip(ts, ds)]
            k_m = [_bf(t["kk"] * d["e_km"]) for t, d in zip(ts, ds)]
            a = [_bf(jnp.where(causal, _dot(q_m[j], k_m[j], 1, 1), 0.0)) for j in span]
            da = [_bf(jnp.where(causal, _dot(dos[j], vs[j], 1, 1), 0.0)) for j in span]
            dqm = [_dot(da[j], k_m[j], 1, 0) for j in span]
            dkm = [_dot(da[j], q_m[j], 0, 0) for j in span]
            dv_in = [_dot(a[j], dos[j], 0, 0) for j in span]
            dqe = [_dot(dos[j], _bf(sts[j]), 1, 0) for j in span]
            grow = [_dot(dos[j], q_e[j], 0, 0) for j in span]
            dke, carried = [None] * NB, [None] * NB
            for j in reversed(span):
                dst_b = _bf(dst)
                dke[j] = _dot(vs[j], dst_b, 1, 0)
                dp_ref[2, rows[j], :] = _bf(dv_in[j] + _dot(k_e[j], dst_b, 1, 1))
                carried[j] = ds[j]["total8"] * jnp.sum(dst * sts[j], axis=0, keepdims=True)
                dst = dst * ds[j]["state"] + grow[j]
            kdk = [ke_f[j] * dke[j] for j in span]
            pos = [(q_m[j].astype(jnp.float32) * dqm[j] - k_m[j].astype(jnp.float32) * dkm[j]) + qe_f[j] * dqe[j]
                   for j in span]
            dgs = _exact_sums(rmat, [jnp.concatenate([pos[j], kdk[j]], axis=0) for j in span])
            for j in span:
                t, d = ts[j], ds[j]
                dg = dgs[j] + jnp.tile(carried[j], (CHUNK // 8, 1))
                dqs = dqm[j] * d["e_qm"] + dqe[j] * d["e_q"]
                dkk = dkm[j] * d["e_km"] + dke[j] * d["e_k"]
                df = dg / t["f"] - dkk
                dp_ref[1, rows[j], :] = _bf(df * (1.0 - lb) * (t["sg"] * (1.0 - t["sg"])))
                dp_ref[0, rows[j], :] = _bf(dqs * (t["sq"] * (1.0 + qvs[j] * (1.0 - t["sq"]))))
                dlb = dlb + df * (1.0 - t["sg"])
            return dst, dlb

        _, dlb = lax.fori_loop(0, NGRP, group, (jnp.zeros((HEAD, HEAD), jnp.float32),
                                                jnp.zeros((CHUNK, HEAD), jnp.float32)))
        dlb_ref[...] = jnp.sum(dlb, axis=0, keepdims=True)

    vec = pl.BlockSpec((1, HEAD), lambda h: (0, h))
    return pl.pallas_call(
        body, name="hgrn_bwd", grid=(NHEAD,),
        in_specs=[*_HEAD_SPECS,
                  pl.BlockSpec((2, HEAD), lambda h: (0, h)), vec,
                  pl.BlockSpec((T, HEAD), lambda h: (0, h)),
                  pl.BlockSpec((None, NCHUNK, HEAD, HEAD), lambda h: (h, 0, 0, 0)),
                  pl.BlockSpec((T, HEAD), lambda h: (0, NHEAD + h)), ANY, ANY],
        out_specs=(pl.BlockSpec((4, T, HEAD), lambda h: (0, 0, h)), vec, vec),
        out_shape=(pltpu.HBM((NSEG, T, D), jnp.bfloat16),
                   jax.ShapeDtypeStruct((1, D), jnp.float32), jax.ShapeDtypeStruct((1, D), jnp.float32)),
        scratch_shapes=[pltpu.VMEM((T, HEAD), jnp.float32)],
        input_output_aliases={9: 0},
        compiler_params=_params(("parallel",)),
    )(proj, proj, proj, proj, lb_logits, rec_g, o, states, dymix, dproj_in, token)


def _pool_bwd(proj, pool_w, pool_scale, dymix):
    def body(u_ref, pg_ref, w_ref, sc_ref, dy_ref, dp_ref, gw_ref, gs_ref):
        gidx = pl.program_id(0)
        u, pg = u_ref[...], pg_ref[...]
        d = _bf(_window_mean(_window_sum(u, gidx, _shift_down), gidx) - u)
        mixed = _dot(d, w_ref[...], 1, 0)
        spg = _sigmoid(pg)
        dyv = dy_ref[...]
        d_p = dyv * (pg * spg)
        dp_ref[1] = _bf(dyv * (mixed * sc_ref[...]) * (spg * (1.0 + pg * (1.0 - spg))))
        gs_ref[...] = jnp.sum(d_p * mixed, axis=0, keepdims=True)
        dmixed = _bf(d_p * sc_ref[...])
        gw_ref[...] = _dot(d, dmixed, 0, 0)
        dd = _dot(dmixed, w_ref[...], 1, 1)
        dp_ref[0] = _bf(_window_sum(_window_mean(dd, gidx), gidx, _shift_up) - dd)

    return pl.pallas_call(
        body, name="pool_bwd", grid=(NGROUP,),
        in_specs=[*_POOL_SPECS,
                  pl.BlockSpec((None, GROUP, GROUP), lambda g: (g, 0, 0)),
                  pl.BlockSpec((1, GROUP), lambda g: (0, g)),
                  pl.BlockSpec((T, GROUP), lambda g: (0, g))],
        out_specs=(pl.BlockSpec((2, T, GROUP), lambda g: (2, 0, g)),
                   pl.BlockSpec((None, GROUP, GROUP), lambda g: (g, 0, 0)),
                   pl.BlockSpec((1, GROUP), lambda g: (0, g))),
        out_shape=(pltpu.HBM((NSEG, T, D), jnp.bfloat16),
                   jax.ShapeDtypeStruct((NGROUP, GROUP, GROUP), jnp.float32),
                   jax.ShapeDtypeStruct((1, D), jnp.float32)),
        compiler_params=_params(("parallel",)),
    )(proj, proj, pool_w, pool_scale, dymix)


def _proj_bwd_w(place, ht, dproj):
    half = NTILE // 2

    def owner_chip(i, pr):
        return jnp.where(i < half, i // 3, (pr[1] + 1 + (i - half) // 3) % 4)

    def tile_of(i, pr):
        side = jnp.where(i < half, 1 - pr[2], pr[2])
        return 6 * owner_chip(i, pr) + 3 * side + i % 3

    def dproj_block(i, pr):
        j = tile_of(i, pr)
        return ((j // 4 + 4) % NSEG, 0, j % 4)

    def mine(i):
        return jnp.maximum(i, half)

    def body(place_ref, h_ref, dp_ref, sum_ref, own_ref, sendbuf, recvbuf, send_sems, recv_sems):
        i = pl.program_id(0)
        px, py, c, _ = _place()
        gw = _dot(h_ref[...], dp_ref[...], 1, 0)

        def to_sibling(slot):
            return pltpu.make_async_remote_copy(
                src_ref=sendbuf.at[slot], dst_ref=recvbuf.at[slot], send_sem=send_sems.at[slot],
                recv_sem=recv_sems.at[slot], device_id=(px, py, 1 - c), device_id_type=MESH)

        @pl.when(i < half)
        def _():
            sendbuf[i] = _bf(gw)
            to_sibling(i).start()

        @pl.when(i >= half)
        def _():
            slot = 3 * owner_chip(i, place_ref) + i % 3
            to_sibling(slot).wait_recv()
            total = gw + recvbuf[slot].astype(jnp.float32)
            sum_ref[...] = _bf(total)
            own_ref[...] = total

        @pl.when(i == NTILE - 1)
        def _():
            for slot in range(half):
                to_sibling(slot).wait_send()

    return pl.pallas_call(
        body, name="proj_bwd_w",
        grid_spec=pltpu.PrefetchScalarGridSpec(
            num_scalar_prefetch=1, grid=(NTILE,),
            in_specs=[pl.BlockSpec((D, T), lambda i, pr: (0, 0)),
                      pl.BlockSpec((None, T, TILE), lambda i, pr: dproj_block(i, pr))],
            out_specs=(pl.BlockSpec((None, None, D, TILE), lambda i, pr: (owner_chip(mine(i), pr), mine(i) % 3, 0, 0)),
                       pl.BlockSpec((None, D, TILE), lambda i, pr: (jnp.where(i < NTILE - 3, 0, i % 3), 0, 0))),
            scratch_shapes=[pltpu.VMEM((half, D, TILE), jnp.bfloat16), pltpu.VMEM((half, D, TILE), jnp.bfloat16),
                            pltpu.SemaphoreType.DMA((half,)), pltpu.SemaphoreType.DMA((half,))]),
        out_shape=(pltpu.HBM((4, 3, D, TILE), jnp.bfloat16), pltpu.HBM((3, D, TILE), jnp.float32)),
        compiler_params=_params(("arbitrary",)),
    )(place, ht, dproj)


def _proj_bwd_x(dproj, w_t, x, g1, dz, token):
    tm = 512
    pairs = NSEG // 2

    def body(dp_ref, w_ref, x_ref, g_ref, dz_ref, token_any, dx_ref, dg_ref, wcat, acc):
        del token_any
        m, s = pl.program_id(0), pl.program_id(1)
        r = None
        for k in range(2):
            for i in range(4):
                wcat[k, :, i * TILE:(i + 1) * TILE] = w_ref[4 * k + i]
            part = _dot(dp_ref[k], wcat[k], 1, 1)
            r = part if r is None else r + part

        @pl.when(s == 0)
        def _():
            acc[...] = r

        @pl.when(s != 0)
        def _():
            acc[...] += r

        @pl.when(s == pairs - 1)
        def _():
            xv = x_ref[...]
            rs = lax.rsqrt(jnp.mean(xv * xv, axis=-1, keepdims=True) + EPS)
            xhat = xv * rs
            dhv = acc[...]
            gdh = dhv * g_ref[...]
            dx_ref[...] = dz_ref[...] + rs * (gdh - xhat * jnp.mean(xhat * gdh, axis=-1, keepdims=True))
            dg = jnp.sum(xhat * dhv, axis=0, keepdims=True)

            @pl.when(m == 0)
            def _():
                dg_ref[...] = dg

            @pl.when(m != 0)
            def _():
                dg_ref[...] += dg

    rows = pl.BlockSpec((tm, D), lambda m, s: (m, 0))
    vec = pl.BlockSpec((1, D), lambda m, s: (0, 0))
    return pl.pallas_call(
        body, name="proj_bwd_x", grid=(T // tm, pairs),
        in_specs=[pl.BlockSpec((2, tm, D), lambda m, s: (s, m, 0)),
                  pl.BlockSpec((8, D, TILE), lambda m, s: ((s + 1) % pairs, 0, 0)), rows, vec, rows, ANY],
        out_specs=(rows, vec),
        out_shape=(jax.ShapeDtypeStruct((T, D), jnp.float32), jax.ShapeDtypeStruct((1, D), jnp.float32)),
        scratch_shapes=[pltpu.VMEM((2, D, D), jnp.bfloat16), pltpu.VMEM((tm, D), jnp.float32)],
        compiler_params=_params(("arbitrary", "arbitrary"), vmem_mib=56),
    )(dproj, w_t, x, g1, dz, token)


def _adamw(w, g, m, v):
    m_new = ADAM_B1 * m + (1.0 - ADAM_B1) * g
    v_new = ADAM_B2 * v + (1.0 - ADAM_B2) * (g * g)
    delta = -ADAM_LR * ((m_new / BC1) / (jnp.sqrt(v_new / BC2) + ADAM_EPS) + ADAM_WD * w)
    return delta, m_new, v_new


def _reduce_adam(name, place, parts, w, m, v, grid, w_spec):
    n = len(parts)

    def body(place_ref, *refs):
        del place_ref
        w_ref, m_ref, v_ref, g_ref, d_ref, mo_ref, vo_ref = refs[n:]
        g = None
        for ref, (_, _, stacked) in zip(refs[:n], parts):
            terms = [ref[r] for r in range(ref.shape[0])] if stacked else [ref[...]]
            for t in terms:
                g = t.astype(jnp.float32) if g is None else g + t.astype(jnp.float32)
        delta, m_new, v_new = _adamw(w_ref[...], g, m_ref[...], v_ref[...])
        g_ref[...] = g
        d_ref[...] = delta
        mo_ref[...] = m_new
        vo_ref[...] = v_new

    shape = jax.ShapeDtypeStruct(w.shape, jnp.float32)
    return pl.pallas_call(
        body, name=name,
        grid_spec=pltpu.PrefetchScalarGridSpec(
            num_scalar_prefetch=1, grid=grid,
            in_specs=[spec for _, spec, _ in parts] + [w_spec] * 3, out_specs=(w_spec,) * 4),
        out_shape=(shape,) * 4,
        compiler_params=_params(("parallel",)),
    )(place, *[_in_hbm(a) for a in [a for a, _, _ in parts] + [w, m, v]])


def _small_adam(parts, w, m, v):
    def body(p_ref, w_ref, m_ref, v_ref, g_ref, d_ref, mo_ref, vo_ref):
        g = p_ref[0]
        for s in range(1, NDEV):
            g = g + p_ref[s]
        wv = w_ref[...]
        rows = _row_ids(wv.shape)
        other = jnp.where(rows == 2, pltpu.roll(wv, 7, 0), jnp.where(rows == 3, pltpu.roll(wv, 1, 0), 0.0))
        lbv = _sigmoid(wv - other)
        sign = jnp.where(rows == 2, 1.0, -1.0)
        g = jnp.where((rows == 2) | (rows == 3), sign * g * lbv * (1.0 - lbv), g)
        delta, m_new, v_new = _adamw(wv, g, m_ref[...], v_ref[...])
        g_ref[...] = g
        d_ref[...] = delta
        mo_ref[...] = m_new
        vo_ref[...] = v_new

    shape = jax.ShapeDtypeStruct((8, D), jnp.float32)
    return pl.pallas_call(body, name="small_adam", out_shape=(shape,) * 4)(parts, w, m, v)


def _rows8(*vecs):
    rows = [a.reshape(-1, D) for a in vecs]
    n = sum(r.shape[0] for r in rows)
    return jnp.concatenate(rows + [jnp.zeros((8 - n, D), jnp.float32)], axis=0)


def kernel(x, norm1_g, w_in, pool_w, pool_scale, lb_logits, rec_norm_g, w_out, final_norm_g, loss_target, m_norm1_g, m_w_in, m_pool_w, m_pool_scale, m_lb_logits, m_rec_norm_g, m_w_out, m_final_norm_g, v_norm1_g, v_w_in, v_pool_w, v_pool_scale, v_lb_logits, v_rec_norm_g, v_w_out, v_final_norm_g):
    xs = x[0]
    target = loss_target[0]
    ix, iy, ic = lax.axis_index("x"), lax.axis_index("y"), lax.axis_index("c")
    place = jnp.stack([4 * ix + 2 * iy + ic, 2 * ix + iy, ic]).astype(jnp.int32)
    gf = final_norm_g.reshape(1, D)

    ht, w_t, w_out_b, w_out_g, pool_g, proj = _gather_proj(xs, norm1_g, w_in, w_out, pool_w)
    pool_full = pool_g.transpose(1, 0, 2, 3).reshape(NGROUP, GROUP, GROUP)
    wout = [w_out_b, w_out_g]
    wout_send, wout_recv, wout, wout_token = _split_start("gather_wout_start", wout, NDEV - 1, _plan_wout)

    y = _pool_fwd(proj, pool_full, pool_scale, wout_token)
    y, o, states = _hgrn_fwd(proj, lb_logits, rec_norm_g, y)
    _, w_out_g = _split_wait("gather_wout_wait", wout, wout_send, wout_recv, _plan_wout, o)
    w_out_full = _in_hbm(w_out_g.reshape(DMIX, D))
    dz, dzb, sq, dgf = _out_proj_loss(xs, y, w_out_full, target, gf)

    dymix, gwout_f, gwout_b = _out_proj_bwd(dzb, w_out_full, y)
    dproj, gpool, dscale = _pool_bwd(proj, pool_full, pool_scale, dymix)

    blk_out = (NDEV, DMIX // NDEV, D)
    blk_pool = (NDEV, NGROUP, GROUP // NDEV, GROUP)
    gpool_s = gpool.reshape(NGROUP, NDEV, GROUP // NDEV, GROUP).transpose(1, 0, 2, 3)
    rest = [gwout_b.reshape(blk_out), gpool_s,
            lax.empty((NDEV - 1,) + blk_out[1:], jnp.bfloat16), lax.empty((NDEV - 1,) + blk_pool[1:], jnp.float32)]
    rest_send, rest_recv, rest, rest_token = _split_start("scatter_rest_start", rest, 2 * (NDEV - 1), _plan_rest)

    dproj, drecg, dlb = _hgrn_bwd(proj, lb_logits, rec_norm_g, o, states, dymix, dproj, rest_token)
    chip_sums, own_sum = _proj_bwd_w(place, ht, dproj)
    win = [chip_sums, lax.empty((3, 3, D, TILE), jnp.bfloat16)]
    win_send, win_recv, win, win_token = _split_start("scatter_win_start", win, 3, _plan_in)

    grad_x, dg1 = _proj_bwd_x(dproj, w_t, xs, norm1_g, dz, win_token)

    _, gpool_own, r_out, r_pool = _split_wait("scatter_rest_wait", rest, rest_send, rest_recv, _plan_rest, grad_x)
    g_wout, d_wout, m_wout, v_wout = _reduce_adam(
        "adam_w_out", place,
        [(gwout_f.reshape(blk_out), pl.BlockSpec((None,) + blk_out[1:], lambda i, pr: (pr[0], 0, 0)), False),
         (r_out, pl.BlockSpec((NDEV - 1,) + blk_out[1:], lambda i, pr: (0, 0, 0)), True)],
        w_out, m_w_out, v_w_out, (1,), pl.BlockSpec((None,) + blk_out[1:], lambda i, pr: (0, 0, 0)))
    g_pool, d_pool, m_pool, v_pool = _reduce_adam(
        "adam_pool_w", place,
        [(gpool_own, pl.BlockSpec((None,) + blk_pool[1:], lambda i, pr: (pr[0], 0, 0, 0)), False),
         (r_pool, pl.BlockSpec((NDEV - 1,) + blk_pool[1:], lambda i, pr: (0, 0, 0, 0)), True)],
        pool_w, m_pool_w, v_pool_w, (1,), pl.BlockSpec((None,) + blk_pool[1:], lambda i, pr: (0, 0, 0, 0)))

    r_small = _gather_small(_rows8(dg1, dscale, dlb, dlb, drecg, dgf, sq), d_wout, d_pool)
    g_s, d_s, m_s, v_s = _small_adam(
        r_small,
        _rows8(norm1_g, pool_scale, lb_logits, rec_norm_g, final_norm_g),
        _rows8(m_norm1_g, m_pool_scale, m_lb_logits, m_rec_norm_g, m_final_norm_g),
        _rows8(v_norm1_g, v_pool_scale, v_lb_logits, v_rec_norm_g, v_final_norm_g))
    loss = jnp.sum(g_s[6]) * (0.5 / D)

    _, r_in = _split_wait("scatter_win_wait", win, win_send, win_recv, _plan_in, v_s)
    g_win, d_win, m_win, v_win = _reduce_adam(
        "adam_w_in", place,
        [(own_sum, pl.BlockSpec((None, D, TILE), lambda i, pr: (i, 0, 0)), False),
         (r_in, pl.BlockSpec((3, None, D, TILE), lambda i, pr: (0, i, 0, 0)), True)],
        w_in, m_w_in, v_w_in, (3,), pl.BlockSpec((None, D, TILE), lambda i, pr: (0, 0, i)))

    def small_outs(a):
        return a[0:1], a[1:2], a[2:4], a[4:5], a[5]

    def outs(small_a, win, pool, wout):
        n1, ps, lbl, rg, fg = small_outs(small_a)
        return n1, win, pool, ps, lbl, rg, wout, fg

    return (loss, grad_x[None],
            *outs(g_s, g_win, g_pool, g_wout), *outs(d_s, d_win, d_pool, d_wout),
            *outs(m_s, m_win, m_pool, m_wout), *outs(v_s, v_win, v_pool, v_wout))
```

```python
import functools

import jax
import jax.numpy as jnp
from jax import lax
from jax.experimental import pallas as pl
from jax.experimental.pallas import tpu as pltpu

T = 2048
D = 1024
NSEG = 6
NTILE = 24
TILE = 256
DMIX = 2048
NDEV = 8
HEAD = 128
NHEAD = 8
CHUNK = 64
NCHUNK = T // CHUNK
NB = 32
NGRP = NCHUNK // NB
NGROUP = 4
GROUP = 256
EPS = 1e-6
EXP_CAP = 80.0
MESH = pl.DeviceIdType.MESH
AXES = ("x", "y", "c")
ANY = pl.BlockSpec(memory_space=pl.ANY)
HBM = pl.BlockSpec(memory_space=pltpu.HBM)
SEM = pl.BlockSpec(memory_space=pltpu.SEMAPHORE)
EFFECT = pltpu.SideEffectType.DATAFLOW_SIDE_EFFECTING

ADAM_LR = 0.001
ADAM_B1 = 0.9
ADAM_B2 = 0.999
ADAM_EPS = 1e-08
ADAM_WD = 0.01
ADAM_STEP = 10
BC1 = 1.0 - ADAM_B1 ** ADAM_STEP
BC2 = 1.0 - ADAM_B2 ** ADAM_STEP

MIB = 1 << 20


def _params(sem=None, vmem_mib=48):
    return pltpu.CompilerParams(dimension_semantics=sem, vmem_limit_bytes=vmem_mib * MIB)


def _sigmoid(v):
    return 1.0 / (1.0 + jnp.exp(-v))


def _dot(a, b, ca, cb, precision=None):
    return lax.dot_general(a, b, (((ca,), (cb,)), ((), ())), precision=precision,
                           preferred_element_type=jnp.float32)


def _bf(v):
    return v.astype(jnp.bfloat16)


def _in_hbm(a):
    return pltpu.with_memory_space_constraint(a, pltpu.HBM)


def _place():
    x, y, c = lax.axis_index("x"), lax.axis_index("y"), lax.axis_index("c")
    return x, y, c, 4 * x + 2 * y + c


def _peer(x, y, c, r):
    return (x ^ ((r >> 2) & 1), y ^ ((r >> 1) & 1), c ^ (r & 1))


def _gather_proj(x, g1, w_in, w_out, pool_w):
    def body(x_ref, g_ref, win_ref, wout_ref, pool_ref, ht_o, wt_o, woutb_o, wout_o, pool_o, proj_o,
             hv, htv, wv, wob, pb, stage, send_sems, recv_sems, loc_sems, out_sems):
        px, py, c, my_idx = _place()
        me, sibling = (px, py, c), (px, py, 1 - c)
        chips = [(1 - px, py), (px, 1 - py), (1 - px, 1 - py)]
        for p in range(3):
            wv[3 * my_idx + p] = _bf(win_ref[0, :, p * TILE:(p + 1) * TILE])

        def index(bx, by, bc):
            return 4 * bx + 2 * by + bc

        def slot(w, block):
            return wv.at[pl.ds(3 * index(*block), 3)] if w == 0 else pool_o.at[index(*block)]

        def copy(k, w, block, to, src=None):
            return pltpu.make_async_remote_copy(
                src_ref=slot(w, block) if src is None else src, dst_ref=slot(w, block),
                send_sem=send_sems.at[2 * k + w], recv_sem=recv_sems.at[2 * k + w],
                device_id=to, device_id_type=MESH)

        def save(block):
            at = pl.ds(3 * index(*block), 3)
            pltpu.make_async_copy(wv.at[at], wt_o.at[at], loc_sems.at[4]).start()

        srcs = (slot(0, me), pb)
        first = []
        for w in (0, 1):
            if w == 1:
                pb[...] = _bf(pool_ref[0])
                wob[...] = _bf(wout_ref[0])
            group = [copy(1 + j, w, me, (*chip, c), src=srcs[w]) for j, chip in enumerate(chips[:2])]
            group.append(copy(0, w, me, sibling, src=srcs[w]))
            for cp in group:
                cp.start()
            first += group
        save(me)
        locs = [pltpu.make_async_copy(pb, slot(1, me), loc_sems.at[0]),
                pltpu.make_async_copy(wob, wout_o.at[my_idx], loc_sems.at[1]),
                pltpu.make_async_copy(wob, woutb_o, loc_sems.at[2])]
        for cp in locs:
            cp.start()

        xv = x_ref[...]
        hv[...] = _bf(xv * lax.rsqrt(jnp.mean(xv * xv, axis=-1, keepdims=True) + EPS) * g_ref[...])
        rows = 256
        for r0 in range(0, T, rows):
            htv[:, r0:r0 + rows] = hv[r0:r0 + rows, :].T
        locs.append(pltpu.make_async_copy(htv, ht_o, loc_sems.at[3]))
        locs[-1].start()

        def out_copy(p, j):
            return pltpu.make_async_copy(stage.at[p], proj_o.at[j], out_sems.at[p])

        def project(nth, block):
            base = 3 * index(*block)

            def tile(p, carry):
                if nth > 0:
                    out_copy(p, base + p).wait()
                stage[p] = _dot(hv[...], wv[base + p], 1, 0)
                out_copy(p, base + p).start()
                return carry

            lax.fori_loop(0, 3, tile, 0)

        project(0, me)
        copy(0, 0, sibling, me).wait_recv()
        save(sibling)
        project(1, sibling)
        passed = []
        relay_from = (px ^ (1 - c), py ^ c, c)
        relay_to = (px ^ c, py ^ (1 - c), c)

        def arrived(w, j):
            copy(1 + j, w, (*chips[j], c), me).wait_recv()
            passed.append(copy(4 + j, w, (*chips[j], c), sibling))
            passed[-1].start()

        def relay(w):
            passed.append(copy(3, w, relay_from, relay_to))
            passed[-1].start()

        def handed(nth, j):
            copy(4 + j, 0, (*chips[j], 1 - c), me).wait_recv()
            save((*chips[j], 1 - c))
            project(nth, (*chips[j], 1 - c))

        arrived(0, 0)
        arrived(0, 1)
        relay(0)
        for j in range(2):
            save((*chips[j], c))
            project(2 + j, (*chips[j], c))
        handed(4, 0)
        handed(5, 1)
        arrived(1, 0)
        arrived(1, 1)
        relay(1)
        arrived(0, 2)
        save((*chips[2], c))
        project(6, (*chips[2], c))
        handed(7, 2)
        arrived(1, 2)
        copy(0, 1, sibling, me).wait_recv()
        for j, chip in enumerate(chips):
            copy(4 + j, 1, (*chip, 1 - c), me).wait_recv()
        keep = pltpu.make_async_copy(wv, wt_o, loc_sems.at[4])
        for p in range(3):
            out_copy(p, p).wait()
        for cp in first + passed:
            cp.wait_send()
        keep.wait()
        for cp in locs:
            cp.wait()

    vmem = pl.BlockSpec(memory_space=pltpu.VMEM)
    bf16 = jnp.bfloat16
    return pl.pallas_call(
        body, name="gather_proj",
        out_shape=(pltpu.HBM((D, T), bf16), pltpu.HBM((NTILE, D, TILE), bf16),
                   pltpu.HBM((DMIX // NDEV, D), bf16), pltpu.HBM((NDEV, DMIX // NDEV, D), bf16),
                   pltpu.HBM((NDEV, NGROUP, GROUP // NDEV, GROUP), bf16), pltpu.HBM((NTILE, T, TILE), jnp.float32)),
        in_specs=[vmem] * 5, out_specs=(ANY,) * 6,
        scratch_shapes=[pltpu.VMEM((T, D), bf16), pltpu.VMEM((D, T), bf16), pltpu.VMEM((NTILE, D, TILE), bf16),
                        pltpu.VMEM((DMIX // NDEV, D), bf16), pltpu.VMEM((NGROUP, GROUP // NDEV, GROUP), bf16),
                        pltpu.VMEM((3, T, TILE), jnp.float32),
                        pltpu.SemaphoreType.DMA((14,)), pltpu.SemaphoreType.DMA((14,)),
                        pltpu.SemaphoreType.DMA((5,)), pltpu.SemaphoreType.DMA((3,))],
        compiler_params=_params(vmem_mib=56),
    )(x, g1, w_in, w_out, pool_w)


def _split_start(name, arrays, n_copies, plan):
    k = len(arrays)

    def body(*refs):
        send_sems, recv_sems, token = refs[k], refs[k + 1], refs[-1]
        for i, (src, dst, to) in enumerate(plan(refs[:k])):
            pltpu.make_async_remote_copy(src_ref=src, dst_ref=dst, send_sem=send_sems.at[i],
                                         recv_sem=recv_sems.at[i], device_id=to, device_id_type=MESH).start()
        token[...] = jnp.zeros_like(token)

    out = pl.pallas_call(
        body, name=name,
        out_shape=(pltpu.SemaphoreType.DMA((n_copies,)), pltpu.SemaphoreType.DMA((n_copies,)),
                   *[pltpu.HBM(a.shape, a.dtype) for a in arrays], jax.ShapeDtypeStruct((8, 128), jnp.float32)),
        in_specs=[HBM] * k, out_specs=(SEM, SEM, *[HBM] * k, pl.BlockSpec(memory_space=pltpu.VMEM)),
        input_output_aliases={i: 2 + i for i in range(k)},
        compiler_params=pltpu.CompilerParams(has_side_effects=EFFECT),
    )(*[pltpu.with_memory_space_constraint(a, pltpu.HBM) for a in arrays])
    return out[0], out[1], out[2:2 + k], out[-1]


def _split_wait(name, arrays, send_sems, recv_sems, plan, after):
    k = len(arrays)

    def body(*refs):
        sends, recvs = refs[k], refs[k + 1]
        for i, (src, dst, to) in enumerate(plan(refs[:k])):
            cp = pltpu.make_async_remote_copy(src_ref=src, dst_ref=dst, send_sem=sends.at[i], recv_sem=recvs.at[i],
                                              device_id=to, device_id_type=MESH)
            cp.wait_send()
            cp.wait_recv()

    return pl.pallas_call(
        body, name=name,
        out_shape=tuple(pltpu.HBM(a.shape, a.dtype) for a in arrays),
        in_specs=[HBM] * k + [SEM, SEM, ANY], out_specs=(HBM,) * k,
        input_output_aliases={i: i for i in range(k)},
        compiler_params=pltpu.CompilerParams(has_side_effects=EFFECT),
    )(*arrays, send_sems, recv_sems, after)


def _plan_wout(refs):
    src, land = refs
    x, y, c, me = _place()
    return [(src, land.at[me], _peer(x, y, c, r)) for r in range(1, NDEV)]


def _plan_rest(refs):
    gob, gpf, r_out, r_pool = refs
    x, y, c, me = _place()
    plan = []
    for r in range(1, NDEV):
        plan.append((gob.at[me ^ r], r_out.at[r - 1], _peer(x, y, c, r)))
        plan.append((gpf.at[me ^ r], r_pool.at[r - 1], _peer(x, y, c, r)))
    return plan


def _plan_in(refs):
    sum_b, r_in = refs
    x, y, c, _ = _place()
    plan = []
    for j, (dx, dy) in enumerate(((1, 0), (0, 1), (1, 1))):
        px, py = x ^ dx, y ^ dy
        plan.append((sum_b.at[2 * px + py], r_in.at[j], (px, py, c)))
    return plan


def _gather_small(small, *after):
    def body(sm, *refs):
        r_small, send_sems, recv_sems, loc_sem = refs[len(after):]
        x, y, c, me = _place()
        loc = pltpu.make_async_copy(sm, r_small.at[me], loc_sem)
        loc.start()

        def copy(r, src_idx):
            return pltpu.make_async_remote_copy(
                src_ref=sm, dst_ref=r_small.at[src_idx], send_sem=send_sems.at[r - 1], recv_sem=recv_sems.at[r - 1],
                device_id=_peer(x, y, c, r), device_id_type=MESH)

        sends = [copy(r, me) for r in range(1, NDEV)]
        for cp in sends:
            cp.start()
        for r in range(1, NDEV):
            copy(r, me ^ r).wait_recv()
        for cp in sends:
            cp.wait_send()
        loc.wait()

    return pl.pallas_call(
        body, name="gather_small",
        out_shape=jax.ShapeDtypeStruct((NDEV, 8, D), jnp.float32),
        in_specs=[ANY] * (1 + len(after)), out_specs=ANY,
        scratch_shapes=[pltpu.SemaphoreType.DMA((NDEV - 1,)), pltpu.SemaphoreType.DMA((NDEV - 1,)),
                        pltpu.SemaphoreType.DMA],
    )(small, *after)


def _seg_tiles(s):
    return (s + 2) % NSEG


_POOL_SPECS = [pl.BlockSpec((None, T, GROUP), lambda g, base=base: (base + g, 0, 0)) for base in (0, 4)]
_HEAD_SPECS = [pl.BlockSpec((None, T, HEAD), lambda h, base=base: (base + h // 2, 0, h % 2))
               for base in (8, 12, 16, 20)]


def _row_ids(shape):
    return lax.broadcasted_iota(jnp.int32, shape, 0)


BAND_ROWS = 128
HALO = 16


def _window_sum(a, gidx, lead):
    width = lax.shift_left(jnp.int32(2), gidx)
    shape = (BAND_ROWS, BAND_ROWS + HALO)
    t, j = lax.broadcasted_iota(jnp.int32, shape, 0), lax.broadcasted_iota(jnp.int32, shape, 1)
    first = t if lead else t + HALO - width + 1
    band = _bf(jnp.where(j >= first, jnp.where(j < first + width, 1.0, 0.0), 0.0))
    zeros = jnp.zeros((HALO, a.shape[1]), jnp.bfloat16)
    padded = [jnp.concatenate([p, zeros] if lead else [zeros, p], axis=0) for p in _split2(a)]
    out = []
    for r0 in range(0, T, BAND_ROWS):
        slab = jnp.concatenate([p[r0:r0 + BAND_ROWS + HALO] for p in padded], axis=1)
        r = _dot(band, slab, 1, 0)
        out.append(r[:, :a.shape[1]] + r[:, a.shape[1]:])
    return jnp.concatenate(out, axis=0)


def _window_mean(s, gidx):
    inv = jnp.where(gidx == 0, 0.5, jnp.where(gidx == 1, 0.25, jnp.where(gidx == 2, 0.125, 0.0625)))
    width = lax.shift_left(jnp.int32(2), gidx)
    head = s[:16] / jnp.minimum(_row_ids((16, s.shape[1])) + 1, width).astype(jnp.float32)
    return jnp.concatenate([head, s[16:] * inv], axis=0)


def _pool_fwd(proj, pool_w, pool_scale, token):
    def body(u_ref, pg_ref, w_ref, sc_ref, token_any, y_ref):
        del token_any
        gidx = pl.program_id(0)
        u, pg = u_ref[...], pg_ref[...]
        d = _window_mean(_window_sum(u, gidx, False), gidx) - u
        mixed = _dot(_bf(d), w_ref[...], 1, 0)
        y_ref[...] = _bf(mixed * sc_ref[...] * (pg * _sigmoid(pg)))

    return pl.pallas_call(
        body, name="pool_fwd", grid=(NGROUP,),
        in_specs=[*_POOL_SPECS,
                  pl.BlockSpec((None, GROUP, GROUP), lambda g: (g, 0, 0)),
                  pl.BlockSpec((1, GROUP), lambda g: (0, g)), ANY],
        out_specs=pl.BlockSpec((T, GROUP), lambda g: (0, g)),
        out_shape=pltpu.HBM((T, DMIX), jnp.bfloat16),
        compiler_params=_params(("parallel",)),
    )(proj, proj, pool_w, pool_scale, token)


def _tri(lower):
    r = lax.broadcasted_iota(jnp.int32, (CHUNK, CHUNK), 0)
    c = lax.broadcasted_iota(jnp.int32, (CHUNK, CHUNK), 1)
    return (r >= c) if lower else (r <= c)


def _sum_rows_matrix():
    shape = (CHUNK + 16, CHUNK)
    r, c = lax.broadcasted_iota(jnp.int32, shape, 0), lax.broadcasted_iota(jnp.int32, shape, 1)
    run = jnp.where(c <= r, 1.0, 0.0)
    half = jnp.where(c < CHUNK // 2, 1.0, 0.0)
    return _bf(jnp.where(r < CHUNK, run, jnp.where(r < CHUNK + 8, 1.0, half)))


def _rev_sum_matrix():
    shape = (CHUNK, 2 * CHUNK)
    r, c = lax.broadcasted_iota(jnp.int32, shape, 0), lax.broadcasted_iota(jnp.int32, shape, 1)
    return _bf(jnp.where(c < CHUNK, jnp.where(c >= r, 1.0, 0.0), jnp.where(c - CHUNK < r, 1.0, 0.0)))


def _split2(a):
    hi = _bf(a)
    return [hi, _bf(a - hi.astype(jnp.float32))]


def _exact_sums(mat, pieces):
    x = jnp.concatenate([s for p in pieces for s in _split2(p)], axis=1)
    r = _dot(mat, x, 1, 0)
    return [r[:, 2 * j * HEAD:(2 * j + 1) * HEAD] + r[:, (2 * j + 1) * HEAD:(2 * j + 2) * HEAD]
            for j in range(len(pieces))]


def _gates(qv, fl, lb):
    sq = _sigmoid(qv)
    sg = _sigmoid(fl)
    f = lb + (1.0 - lb) * sg
    return dict(sq=sq, qs=qv * sq, sg=sg, f=f, kk=1.0 - f, g=jnp.log(f))


def _decays(sums):
    big_g = sums[:CHUNK]
    total = sums[CHUNK:CHUNK + 8]
    g_last = jnp.tile(total, (CHUNK // 8, 1))
    g_mid = jnp.tile(sums[CHUNK + 8:], (CHUNK // 8, 1))
    return dict(
        e_q=jnp.exp(big_g),
        e_k=jnp.exp(g_last - big_g),
        e_qm=jnp.exp(jnp.minimum(big_g - g_mid, EXP_CAP)),
        e_km=jnp.exp(jnp.minimum(g_mid - big_g, EXP_CAP)),
        total8=jnp.exp(total),
        state=jnp.exp(jnp.tile(total, (HEAD // 8, 1))))


def _group_rows(gi):
    return [pl.ds(pl.multiple_of((gi * NB + j) * CHUNK, CHUNK), CHUNK) for j in range(NB)]


def _lower_bound(lb_ref):
    return _sigmoid(lb_ref[0:1, :] - lb_ref[1:2, :])


def _hgrn_fwd(proj, lb_logits, rec_g, y_in):
    def body(q_ref, f_ref, i_ref, gate_ref, lb_ref, rg_ref, y_any, y_ref, o_ref, st_ref):
        del y_any
        lb = _lower_bound(lb_ref)
        causal = _tri(True)
        smat = _sum_rows_matrix()

        def group(gi, st):
            rows = _group_rows(gi)
            ts = [_gates(q_ref[r, :], f_ref[r, :], lb) for r in rows]
            ds = [_decays(s) for s in _exact_sums(smat, [t["g"] for t in ts])]
            vs = [_bf(i_ref[r, :]) for r in rows]
            q_m = [_bf(t["qs"] * d["e_qm"]) for t, d in zip(ts, ds)]
            k_m = [_bf(t["kk"] * d["e_km"]) for t, d in zip(ts, ds)]
            q_e = [_bf(t["qs"] * d["e_q"]) for t, d in zip(ts, ds)]
            k_e = [_bf(t["kk"] * d["e_k"]) for t, d in zip(ts, ds)]
            a = [_bf(jnp.where(causal, _dot(q_m[j], k_m[j], 1, 1), 0.0)) for j in range(NB)]
            intra = [_dot(a[j], vs[j], 1, 0) for j in range(NB)]
            upd = [_dot(vs[j], k_e[j], 0, 0) for j in range(NB)]
            for j in range(NB):
                st_ref[gi * NB + j] = st
                o_ref[rows[j], :] = intra[j] + _dot(q_e[j], _bf(st), 1, 1)
                st = st * ds[j]["state"] + upd[j]
            return st

        lax.fori_loop(0, NGRP, group, jnp.zeros((HEAD, HEAD), jnp.float32))
        o = o_ref[...]
        rn = o * lax.rsqrt(jnp.mean(o * o, axis=-1, keepdims=True) + EPS)
        gate = gate_ref[...]
        y_ref[...] = _bf(rn * rg_ref[...] * (gate * _sigmoid(gate)))

    return pl.pallas_call(
        body, name="hgrn_fwd", grid=(NHEAD,),
        in_specs=[*_HEAD_SPECS,
                  pl.BlockSpec((2, HEAD), lambda h: (0, h)),
                  pl.BlockSpec((1, HEAD), lambda h: (0, h)),
                  pl.BlockSpec(memory_space=pl.ANY)],
        out_specs=(pl.BlockSpec((T, HEAD), lambda h: (0, NHEAD + h)),
                   pl.BlockSpec((T, HEAD), lambda h: (0, h)),
                   pl.BlockSpec((None, NCHUNK, HEAD, HEAD), lambda h: (h, 0, 0, 0))),
        out_shape=(pltpu.HBM((T, DMIX), jnp.bfloat16), pltpu.HBM((T, D), jnp.float32),
                   pltpu.HBM((NHEAD, NCHUNK, HEAD, HEAD), jnp.float32)),
        input_output_aliases={6: 0},
        compiler_params=_params(("parallel",)),
    )(proj, proj, proj, proj, lb_logits, rec_g, y_in)


def _out_proj_loss(x, y, w_out, target, gf):
    rows = 512
    parts = [slice(k * rows // 2, (k + 1) * rows // 2) for k in range(2)]

    def body(x_ref, y_ref, w_ref, t_ref, g_ref, dz_ref, dzb_ref, sq_ref, dg_ref):
        zs = [x_ref[p, :] + _dot(y_ref[p, :], w_ref[...], 1, 0) for p in parts]
        sq = dg = 0.0
        for p, z in zip(parts, zs):
            r = lax.rsqrt(jnp.mean(z * z, axis=-1, keepdims=True) + EPS)
            zhat = z * r
            err = zhat * g_ref[...] - t_ref[p, :]
            dy = err * (1.0 / D)
            gdy = dy * g_ref[...]
            dz = r * (gdy - zhat * jnp.mean(zhat * gdy, axis=-1, keepdims=True))
            dz_ref[p, :] = dz
            dzb_ref[p, :] = _bf(dz)
            sq = sq + jnp.sum(err * err, axis=0, keepdims=True)
            dg = dg + jnp.sum(zhat * dy, axis=0, keepdims=True)

        @pl.when(pl.program_id(0) == 0)
        def _():
            sq_ref[...] = sq
            dg_ref[...] = dg

        @pl.when(pl.program_id(0) != 0)
        def _():
            sq_ref[...] += sq
            dg_ref[...] += dg

    tile = pl.BlockSpec((rows, D), lambda i: (i, 0))
    vec = pl.BlockSpec((1, D), lambda i: (0, 0))
    return pl.pallas_call(
        body, name="out_proj_loss", grid=(T // rows,),
        in_specs=[tile, pl.BlockSpec((rows, DMIX), lambda i: (i, 0)), pl.BlockSpec((DMIX, D), lambda i: (0, 0)),
                  tile, vec],
        out_specs=(tile, tile, vec, vec),
        out_shape=(pltpu.HBM((T, D), jnp.float32), pltpu.HBM((T, D), jnp.bfloat16),
                   jax.ShapeDtypeStruct((1, D), jnp.float32), jax.ShapeDtypeStruct((1, D), jnp.float32)),
        compiler_params=_params(("arbitrary",)),
    )(x, y, w_out, target, gf)


def _out_proj_bwd(dzb, w_out, y):
    tn = 512

    def body(dz_ref, w_ref, y_ref, dy_ref, gw_ref, gwb_ref):
        dz = dz_ref[...]
        dy_ref[...] = _dot(dz, w_ref[...], 1, 1)
        gw = _dot(y_ref[...], dz, 0, 0)
        gw_ref[...] = gw
        gwb_ref[...] = _bf(gw)

    return pl.pallas_call(
        body, name="out_proj_bwd", grid=(DMIX // tn,),
        in_specs=[pl.BlockSpec((T, D), lambda n: (0, 0)), pl.BlockSpec((tn, D), lambda n: (n, 0)),
                  pl.BlockSpec((T, tn), lambda n: (0, n))],
        out_specs=(pl.BlockSpec((T, tn), lambda n: (0, n)), pl.BlockSpec((tn, D), lambda n: (n, 0)),
                   pl.BlockSpec((tn, D), lambda n: (n, 0))),
        out_shape=(pltpu.HBM((T, DMIX), jnp.float32), pltpu.HBM((DMIX, D), jnp.float32),
                   pltpu.HBM((DMIX, D), jnp.bfloat16)),
        compiler_params=_params(("parallel",)),
    )(dzb, w_out, y)


def _hgrn_bwd(proj, lb_logits, rec_g, o, states, dymix, dproj_in, token):
    def body(q_ref, f_ref, i_ref, gate_ref, lb_ref, rg_ref, o_ref, st_ref, dy_ref, dp_any, token_any,
             dp_ref, drg_ref, dlb_ref, do_ref):
        del dp_any, token_any
        lb = _lower_bound(lb_ref)
        causal = _tri(True)
        smat, rmat = _sum_rows_matrix(), _rev_sum_matrix()

        o = o_ref[...]
        rs = lax.rsqrt(jnp.mean(o * o, axis=-1, keepdims=True) + EPS)
        rn = o * rs
        gate = gate_ref[...]
        sgate = _sigmoid(gate)
        dyv = dy_ref[...]
        d_r = dyv * (gate * sgate)
        dp_ref[3] = _bf(dyv * (rn * rg_ref[...]) * (sgate * (1.0 + gate * (1.0 - sgate))))
        drg_ref[...] = jnp.sum(d_r * rn, axis=0, keepdims=True)
        drn = d_r * rg_ref[...]
        do_ref[...] = rs * (drn - rn * jnp.mean(rn * drn, axis=-1, keepdims=True))

        def group(i, carry):
            dst, dlb = carry
            gi = NGRP - 1 - i
            rows = _group_rows(gi)
            span = range(NB)
            qvs = [q_ref[r, :] for r in rows]
            ts = [_gates(qv, f_ref[r, :], lb) for qv, r in zip(qvs, rows)]
            ds = [_decays(s) for s in _exact_sums(smat, [t["g"] for t in ts])]
            vs = [_bf(i_ref[r, :]) for r in rows]
            dos = [_bf(do_ref[r, :]) for r in rows]
            sts = [st_ref[gi * NB + j] for j in span]
            qe_f = [t["qs"] * d["e_q"] for t, d in zip(ts, ds)]
            ke_f = [t["kk"] * d["e_k"] for t, d in zip(ts, ds)]
            q_e, k_e = [_bf(a) for a in qe_f], [_bf(a) for a in ke_f]
            q_m = [_bf(t["qs"] * d["e_qm"]) for t, d in zip(ts, ds)]
            k_m = [_bf(t["kk"] * d["e_km"]) for t, d in zip(ts, ds)]
            a = [_bf(jnp.where(causal, _dot(q_m[j], k_m[j], 1, 1), 0.0)) for j in span]
            da = [_bf(jnp.where(causal, _dot(dos[j], vs[j], 1, 1), 0.0)) for j in span]
            dqm = [_dot(da[j], k_m[j], 1, 0) for j in span]
            dkm = [_dot(da[j], q_m[j], 0, 0) for j in span]
            dv_in = [_dot(a[j], dos[j], 0, 0) for j in span]
            dqe = [_dot(dos[j], _bf(sts[j]), 1, 0) for j in span]
            grow = [_dot(dos[j], q_e[j], 0, 0) for j in span]
            dke, carried = [None] * NB, [None] * NB
            for j in reversed(span):
                dst_b = _bf(dst)
                dke[j] = _dot(vs[j], dst_b, 1, 0)
                dp_ref[2, rows[j], :] = _bf(dv_in[j] + _dot(k_e[j], dst_b, 1, 1))
                carried[j] = ds[j]["total8"] * jnp.sum(dst * sts[j], axis=0, keepdims=True)
                dst = dst * ds[j]["state"] + grow[j]
            kdk = [ke_f[j] * dke[j] for j in span]
            pos = [(q_m[j].astype(jnp.float32) * dqm[j] - k_m[j].astype(jnp.float32) * dkm[j]) + qe_f[j] * dqe[j]
                   for j in span]
            dgs = _exact_sums(rmat, [jnp.concatenate([pos[j], kdk[j]], axis=0) for j in span])
            for j in span:
                t, d = ts[j], ds[j]
                dg = dgs[j] + jnp.tile(carried[j], (CHUNK // 8, 1))
                dqs = dqm[j] * d["e_qm"] + dqe[j] * d["e_q"]
                dkk = dkm[j] * d["e_km"] + dke[j] * d["e_k"]
                df = dg / t["f"] - dkk
                dp_ref[1, rows[j], :] = _bf(df * (1.0 - lb) * (t["sg"] * (1.0 - t["sg"])))
                dp_ref[0, rows[j], :] = _bf(dqs * (t["sq"] * (1.0 + qvs[j] * (1.0 - t["sq"]))))
                dlb = dlb + df * (1.0 - t["sg"])
            return dst, dlb

        _, dlb = lax.fori_loop(0, NGRP, group, (jnp.zeros((HEAD, HEAD), jnp.float32),
                                                jnp.zeros((CHUNK, HEAD), jnp.float32)))
        dlb_ref[...] = jnp.sum(dlb, axis=0, keepdims=True)

    vec = pl.BlockSpec((1, HEAD), lambda h: (0, h))
    return pl.pallas_call(
        body, name="hgrn_bwd", grid=(NHEAD,),
        in_specs=[*_HEAD_SPECS,
                  pl.BlockSpec((2, HEAD), lambda h: (0, h)), vec,
                  pl.BlockSpec((T, HEAD), lambda h: (0, h)),
                  pl.BlockSpec((None, NCHUNK, HEAD, HEAD), lambda h: (h, 0, 0, 0)),
                  pl.BlockSpec((T, HEAD), lambda h: (0, NHEAD + h)), ANY, ANY],
        out_specs=(pl.BlockSpec((4, T, HEAD), lambda h: (0, 0, h)), vec, vec),
        out_shape=(pltpu.HBM((NSEG, T, D), jnp.bfloat16),
                   jax.ShapeDtypeStruct((1, D), jnp.float32), jax.ShapeDtypeStruct((1, D), jnp.float32)),
        scratch_shapes=[pltpu.VMEM((T, HEAD), jnp.float32)],
        input_output_aliases={9: 0},
        compiler_params=_params(("parallel",)),
    )(proj, proj, proj, proj, lb_logits, rec_g, o, states, dymix, dproj_in, token)


def _pool_bwd(proj, pool_w, pool_scale, dymix):
    def body(u_ref, pg_ref, w_ref, sc_ref, dy_ref, dp_ref, gw_ref, gs_ref):
        gidx = pl.program_id(0)
        u, pg = u_ref[...], pg_ref[...]
        d = _bf(_window_mean(_window_sum(u, gidx, False), gidx) - u)
        mixed = _dot(d, w_ref[...], 1, 0)
        spg = _sigmoid(pg)
        dyv = dy_ref[...]
        d_p = dyv * (pg * spg)
        dp_ref[1] = _bf(dyv * (mixed * sc_ref[...]) * (spg * (1.0 + pg * (1.0 - spg))))
        gs_ref[...] = jnp.sum(d_p * mixed, axis=0, keepdims=True)
        dmixed = _bf(d_p * sc_ref[...])
        gw_ref[...] = _dot(d, dmixed, 0, 0)
        dd = _dot(dmixed, w_ref[...], 1, 1)
        dp_ref[0] = _bf(_window_sum(_window_mean(dd, gidx), gidx, True) - dd)

    return pl.pallas_call(
        body, name="pool_bwd", grid=(NGROUP,),
        in_specs=[*_POOL_SPECS,
                  pl.BlockSpec((None, GROUP, GROUP), lambda g: (g, 0, 0)),
                  pl.BlockSpec((1, GROUP), lambda g: (0, g)),
                  pl.BlockSpec((T, GROUP), lambda g: (0, g))],
        out_specs=(pl.BlockSpec((2, T, GROUP), lambda g: (2, 0, g)),
                   pl.BlockSpec((None, GROUP, GROUP), lambda g: (g, 0, 0)),
                   pl.BlockSpec((1, GROUP), lambda g: (0, g))),
        out_shape=(pltpu.HBM((NSEG, T, D), jnp.bfloat16),
                   jax.ShapeDtypeStruct((NGROUP, GROUP, GROUP), jnp.float32),
                   jax.ShapeDtypeStruct((1, D), jnp.float32)),
        compiler_params=_params(("parallel",)),
    )(proj, proj, pool_w, pool_scale, dymix)


def _proj_bwd_w(place, ht, dproj):
    half = NTILE // 2

    def owner_chip(i, pr):
        return jnp.where(i < half, i // 3, (pr[1] + 1 + (i - half) // 3) % 4)

    def tile_of(i, pr):
        side = jnp.where(i < half, 1 - pr[2], pr[2])
        return 6 * owner_chip(i, pr) + 3 * side + i % 3

    def dproj_block(i, pr):
        j = tile_of(i, pr)
        return ((j // 4 + 4) % NSEG, 0, j % 4)

    def mine(i):
        return jnp.maximum(i, half)

    def body(place_ref, h_ref, dp_ref, sum_ref, own_ref, sendbuf, recvbuf, send_sems, recv_sems):
        i = pl.program_id(0)
        px, py, c, _ = _place()
        gw = _dot(h_ref[...], dp_ref[...], 1, 0)

        def to_sibling(slot):
            return pltpu.make_async_remote_copy(
                src_ref=sendbuf.at[slot], dst_ref=recvbuf.at[slot], send_sem=send_sems.at[slot],
                recv_sem=recv_sems.at[slot], device_id=(px, py, 1 - c), device_id_type=MESH)

        @pl.when(i < half)
        def _():
            sendbuf[i] = _bf(gw)
            to_sibling(i).start()

        @pl.when(i >= half)
        def _():
            slot = 3 * owner_chip(i, place_ref) + i % 3
            to_sibling(slot).wait_recv()
            total = gw + recvbuf[slot].astype(jnp.float32)
            sum_ref[...] = _bf(total)
            own_ref[...] = total

        @pl.when(i == NTILE - 1)
        def _():
            for slot in range(half):
                to_sibling(slot).wait_send()

    return pl.pallas_call(
        body, name="proj_bwd_w",
        grid_spec=pltpu.PrefetchScalarGridSpec(
            num_scalar_prefetch=1, grid=(NTILE,),
            in_specs=[pl.BlockSpec((D, T), lambda i, pr: (0, 0)),
                      pl.BlockSpec((None, T, TILE), lambda i, pr: dproj_block(i, pr))],
            out_specs=(pl.BlockSpec((None, None, D, TILE), lambda i, pr: (owner_chip(mine(i), pr), mine(i) % 3, 0, 0)),
                       pl.BlockSpec((None, D, TILE), lambda i, pr: (jnp.where(i < NTILE - 3, 0, i % 3), 0, 0))),
            scratch_shapes=[pltpu.VMEM((half, D, TILE), jnp.bfloat16), pltpu.VMEM((half, D, TILE), jnp.bfloat16),
                            pltpu.SemaphoreType.DMA((half,)), pltpu.SemaphoreType.DMA((half,))]),
        out_shape=(pltpu.HBM((4, 3, D, TILE), jnp.bfloat16), pltpu.HBM((3, D, TILE), jnp.float32)),
        compiler_params=_params(("arbitrary",)),
    )(place, ht, dproj)


def _proj_bwd_x(dproj, w_t, x, g1, dz, token):
    tm = 512
    pairs = NSEG // 2

    def body(dp_ref, w_ref, x_ref, g_ref, dz_ref, token_any, dx_ref, dg_ref, wcat, acc):
        del token_any
        m, s = pl.program_id(0), pl.program_id(1)
        r = None
        for k in range(2):
            for i in range(4):
                wcat[k, :, i * TILE:(i + 1) * TILE] = w_ref[4 * k + i]
            part = _dot(dp_ref[k], wcat[k], 1, 1)
            r = part if r is None else r + part

        @pl.when(s == 0)
        def _():
            acc[...] = r

        @pl.when(s != 0)
        def _():
            acc[...] += r

        @pl.when(s == pairs - 1)
        def _():
            xv = x_ref[...]
            rs = lax.rsqrt(jnp.mean(xv * xv, axis=-1, keepdims=True) + EPS)
            xhat = xv * rs
            dhv = acc[...]
            gdh = dhv * g_ref[...]
            dx_ref[...] = dz_ref[...] + rs * (gdh - xhat * jnp.mean(xhat * gdh, axis=-1, keepdims=True))
            dg = jnp.sum(xhat * dhv, axis=0, keepdims=True)

            @pl.when(m == 0)
            def _():
                dg_ref[...] = dg

            @pl.when(m != 0)
            def _():
                dg_ref[...] += dg

    rows = pl.BlockSpec((tm, D), lambda m, s: (m, 0))
    vec = pl.BlockSpec((1, D), lambda m, s: (0, 0))
    return pl.pallas_call(
        body, name="proj_bwd_x", grid=(T // tm, pairs),
        in_specs=[pl.BlockSpec((2, tm, D), lambda m, s: (s, m, 0)),
                  pl.BlockSpec((8, D, TILE), lambda m, s: ((s + 1) % pairs, 0, 0)), rows, vec, rows, ANY],
        out_specs=(rows, vec),
        out_shape=(jax.ShapeDtypeStruct((T, D), jnp.float32), jax.ShapeDtypeStruct((1, D), jnp.float32)),
        scratch_shapes=[pltpu.VMEM((2, D, D), jnp.bfloat16), pltpu.VMEM((tm, D), jnp.float32)],
        compiler_params=_params(("arbitrary", "arbitrary"), vmem_mib=56),
    )(dproj, w_t, x, g1, dz, token)


def _adamw(w, g, m, v):
    m_new = ADAM_B1 * m + (1.0 - ADAM_B1) * g
    v_new = ADAM_B2 * v + (1.0 - ADAM_B2) * (g * g)
    delta = -ADAM_LR * ((m_new / BC1) / (jnp.sqrt(v_new / BC2) + ADAM_EPS) + ADAM_WD * w)
    return delta, m_new, v_new


def _reduce_adam(name, place, parts, w, m, v, grid, w_spec):
    n = len(parts)

    def body(place_ref, *refs):
        del place_ref
        w_ref, m_ref, v_ref, g_ref, d_ref, mo_ref, vo_ref = refs[n:]
        g = None
        for ref, (_, _, stacked) in zip(refs[:n], parts):
            terms = [ref[r] for r in range(ref.shape[0])] if stacked else [ref[...]]
            for t in terms:
                g = t.astype(jnp.float32) if g is None else g + t.astype(jnp.float32)
        delta, m_new, v_new = _adamw(w_ref[...], g, m_ref[...], v_ref[...])
        g_ref[...] = g
        d_ref[...] = delta
        mo_ref[...] = m_new
        vo_ref[...] = v_new

    shape = jax.ShapeDtypeStruct(w.shape, jnp.float32)
    return pl.pallas_call(
        body, name=name,
        grid_spec=pltpu.PrefetchScalarGridSpec(
            num_scalar_prefetch=1, grid=grid,
            in_specs=[spec for _, spec, _ in parts] + [w_spec] * 3, out_specs=(w_spec,) * 4),
        out_shape=(shape,) * 4,
        compiler_params=_params(("parallel",)),
    )(place, *[_in_hbm(a) for a in [a for a, _, _ in parts] + [w, m, v]])


def _small_adam(parts, w, m, v):
    def body(p_ref, w_ref, m_ref, v_ref, g_ref, d_ref, mo_ref, vo_ref):
        g = p_ref[0]
        for s in range(1, NDEV):
            g = g + p_ref[s]
        wv = w_ref[...]
        rows = _row_ids(wv.shape)
        other = jnp.where(rows == 2, pltpu.roll(wv, 7, 0), jnp.where(rows == 3, pltpu.roll(wv, 1, 0), 0.0))
        lbv = _sigmoid(wv - other)
        sign = jnp.where(rows == 2, 1.0, -1.0)
        g = jnp.where((rows == 2) | (rows == 3), sign * g * lbv * (1.0 - lbv), g)
        delta, m_new, v_new = _adamw(wv, g, m_ref[...], v_ref[...])
        g_ref[...] = g
        d_ref[...] = delta
        mo_ref[...] = m_new
        vo_ref[...] = v_new

    shape = jax.ShapeDtypeStruct((8, D), jnp.float32)
    return pl.pallas_call(body, name="small_adam", out_shape=(shape,) * 4)(parts, w, m, v)


def _rows8(*vecs):
    rows = [a.reshape(-1, D) for a in vecs]
    n = sum(r.shape[0] for r in rows)
    return jnp.concatenate(rows + [jnp.zeros((8 - n, D), jnp.float32)], axis=0)


def kernel(x, norm1_g, w_in, pool_w, pool_scale, lb_logits, rec_norm_g, w_out, final_norm_g, loss_target, m_norm1_g, m_w_in, m_pool_w, m_pool_scale, m_lb_logits, m_rec_norm_g, m_w_out, m_final_norm_g, v_norm1_g, v_w_in, v_pool_w, v_pool_scale, v_lb_logits, v_rec_norm_g, v_w_out, v_final_norm_g):
    xs = x[0]
    target = loss_target[0]
    ix, iy, ic = lax.axis_index("x"), lax.axis_index("y"), lax.axis_index("c")
    place = jnp.stack([4 * ix + 2 * iy + ic, 2 * ix + iy, ic]).astype(jnp.int32)
    gf = final_norm_g.reshape(1, D)

    ht, w_t, w_out_b, w_out_g, pool_g, proj = _gather_proj(xs, norm1_g, w_in, w_out, pool_w)
    pool_full = pool_g.transpose(1, 0, 2, 3).reshape(NGROUP, GROUP, GROUP)
    wout = [w_out_b, w_out_g]
    wout_send, wout_recv, wout, wout_token = _split_start("gather_wout_start", wout, NDEV - 1, _plan_wout)

    y = _pool_fwd(proj, pool_full, pool_scale, wout_token)
    y, o, states = _hgrn_fwd(proj, lb_logits, rec_norm_g, y)
    _, w_out_g = _split_wait("gather_wout_wait", wout, wout_send, wout_recv, _plan_wout, o)
    w_out_full = _in_hbm(w_out_g.reshape(DMIX, D))
    dz, dzb, sq, dgf = _out_proj_loss(xs, y, w_out_full, target, gf)

    dymix, gwout_f, gwout_b = _out_proj_bwd(dzb, w_out_full, y)
    dproj, gpool, dscale = _pool_bwd(proj, pool_full, pool_scale, dymix)

    blk_out = (NDEV, DMIX // NDEV, D)
    blk_pool = (NDEV, NGROUP, GROUP // NDEV, GROUP)
    gpool_s = gpool.reshape(NGROUP, NDEV, GROUP // NDEV, GROUP).transpose(1, 0, 2, 3)
    rest = [gwout_b.reshape(blk_out), gpool_s,
            lax.empty((NDEV - 1,) + blk_out[1:], jnp.bfloat16), lax.empty((NDEV - 1,) + blk_pool[1:], jnp.float32)]
    rest_send, rest_recv, rest, rest_token = _split_start("scatter_rest_start", rest, 2 * (NDEV - 1), _plan_rest)

    dproj, drecg, dlb = _hgrn_bwd(proj, lb_logits, rec_norm_g, o, states, dymix, dproj, rest_token)
    chip_sums, own_sum = _proj_bwd_w(place, ht, dproj)
    win = [chip_sums, lax.empty((3, 3, D, TILE), jnp.bfloat16)]
    win_send, win_recv, win, win_token = _split_start("scatter_win_start", win, 3, _plan_in)

    grad_x, dg1 = _proj_bwd_x(dproj, w_t, xs, norm1_g, dz, win_token)

    _, gpool_own, r_out, r_pool = _split_wait("scatter_rest_wait", rest, rest_send, rest_recv, _plan_rest, grad_x)
    g_wout, d_wout, m_wout, v_wout = _reduce_adam(
        "adam_w_out", place,
        [(gwout_f.reshape(blk_out), pl.BlockSpec((None,) + blk_out[1:], lambda i, pr: (pr[0], 0, 0)), False),
         (r_out, pl.BlockSpec((NDEV - 1,) + blk_out[1:], lambda i, pr: (0, 0, 0)), True)],
        w_out, m_w_out, v_w_out, (1,), pl.BlockSpec((None,) + blk_out[1:], lambda i, pr: (0, 0, 0)))
    g_pool, d_pool, m_pool, v_pool = _reduce_adam(
        "adam_pool_w", place,
        [(gpool_own, pl.BlockSpec((None,) + blk_pool[1:], lambda i, pr: (pr[0], 0, 0, 0)), False),
         (r_pool, pl.BlockSpec((NDEV - 1,) + blk_pool[1:], lambda i, pr: (0, 0, 0, 0)), True)],
        pool_w, m_pool_w, v_pool_w, (1,), pl.BlockSpec((None,) + blk_pool[1:], lambda i, pr: (0, 0, 0, 0)))

    r_small = _gather_small(_rows8(dg1, dscale, dlb, dlb, drecg, dgf, sq), d_wout, d_pool)
    g_s, d_s, m_s, v_s = _small_adam(
        r_small,
        _rows8(norm1_g, pool_scale, lb_logits, rec_norm_g, final_norm_g),
        _rows8(m_norm1_g, m_pool_scale, m_lb_logits, m_rec_norm_g, m_final_norm_g),
        _rows8(v_norm1_g, v_pool_scale, v_lb_logits, v_rec_norm_g, v_final_norm_g))
    loss = jnp.sum(g_s[6]) * (0.5 / D)

    _, r_in = _split_wait("scatter_win_wait", win, win_send, win_recv, _plan_in, v_s)
    g_win, d_win, m_win, v_win = _reduce_adam(
        "adam_w_in", place,
        [(own_sum, pl.BlockSpec((None, D, TILE), lambda i, pr: (i, 0, 0)), False),
         (r_in, pl.BlockSpec((3, None, D, TILE), lambda i, pr: (0, i, 0, 0)), True)],
        w_in, m_w_in, v_w_in, (3,), pl.BlockSpec((None, D, TILE), lambda i, pr: (0, 0, i)))

    def small_outs(a):
        return a[0:1], a[1:2], a[2:4], a[4:5], a[5]

    def outs(small_a, win, pool, wout):
        n1, ps, lbl, rg, fg = small_outs(small_a)
        return n1, win, pool, ps, lbl, rg, wout, fg

    return (loss, grad_x[None],
            *outs(g_s, g_win, g_pool, g_wout), *outs(d_s, d_win, d_pool, d_wout),
            *outs(m_s, m_win, m_pool, m_wout), *outs(v_s, v_win, v_pool, v_wout))
```

```python
import functools

import jax
import jax.numpy as jnp
from jax import lax
from jax.experimental import pallas as pl
from jax.experimental.pallas import tpu as pltpu

T = 2048
D = 1024
NSEG = 6
NTILE = 24
TILE = 256
DMIX = 2048
NDEV = 8
HEAD = 128
NHEAD = 8
CHUNK = 64
NCHUNK = T // CHUNK
NB = 32
NGRP = NCHUNK // NB
NGROUP = 4
GROUP = 256
EPS = 1e-6
EXP_CAP = 80.0
MESH = pl.DeviceIdType.MESH
AXES = ("x", "y", "c")
ANY = pl.BlockSpec(memory_space=pl.ANY)
HBM = pl.BlockSpec(memory_space=pltpu.HBM)
SEM = pl.BlockSpec(memory_space=pltpu.SEMAPHORE)
EFFECT = pltpu.SideEffectType.DATAFLOW_SIDE_EFFECTING

ADAM_LR = 0.001
ADAM_B1 = 0.9
ADAM_B2 = 0.999
ADAM_EPS = 1e-08
ADAM_WD = 0.01
ADAM_STEP = 10
BC1 = 1.0 - ADAM_B1 ** ADAM_STEP
BC2 = 1.0 - ADAM_B2 ** ADAM_STEP

MIB = 1 << 20


def _params(sem=None, vmem_mib=48):
    return pltpu.CompilerParams(dimension_semantics=sem, vmem_limit_bytes=vmem_mib * MIB)


def _sigmoid(v):
    return 1.0 / (1.0 + jnp.exp(-v))


def _dot(a, b, ca, cb, precision=None):
    return lax.dot_general(a, b, (((ca,), (cb,)), ((), ())), precision=precision,
                           preferred_element_type=jnp.float32)


def _bf(v):
    return v.astype(jnp.bfloat16)


def _in_hbm(a):
    return pltpu.with_memory_space_constraint(a, pltpu.HBM)


def _place():
    x, y, c = lax.axis_index("x"), lax.axis_index("y"), lax.axis_index("c")
    return x, y, c, 4 * x + 2 * y + c


def _peer(x, y, c, r):
    return (x ^ ((r >> 2) & 1), y ^ ((r >> 1) & 1), c ^ (r & 1))


def _gather_proj(x, g1, w_in, w_out, pool_w):
    def body(x_ref, g_ref, win_ref, wout_ref, pool_ref, ht_o, wt_o, woutb_o, wout_o, pool_o, proj_o,
             hv, htv, wv, wob, pb, stage, send_sems, recv_sems, loc_sems, out_sems):
        px, py, c, my_idx = _place()
        me, sibling = (px, py, c), (px, py, 1 - c)
        chips = [(1 - px, py), (px, 1 - py), (1 - px, 1 - py)]
        for p in range(3):
            wv[3 * my_idx + p] = _bf(win_ref[0, :, p * TILE:(p + 1) * TILE])

        def index(bx, by, bc):
            return 4 * bx + 2 * by + bc

        def slot(w, block):
            return wv.at[pl.ds(3 * index(*block), 3)] if w == 0 else pool_o.at[index(*block)]

        def copy(k, w, block, to, src=None):
            return pltpu.make_async_remote_copy(
                src_ref=slot(w, block) if src is None else src, dst_ref=slot(w, block),
                send_sem=send_sems.at[2 * k + w], recv_sem=recv_sems.at[2 * k + w],
                device_id=to, device_id_type=MESH)

        def save(block):
            at = pl.ds(3 * index(*block), 3)
            pltpu.make_async_copy(wv.at[at], wt_o.at[at], loc_sems.at[4]).start()

        srcs = (slot(0, me), pb)
        first = []
        for w in (0, 1):
            if w == 1:
                pb[...] = _bf(pool_ref[0])
                wob[...] = _bf(wout_ref[0])
            group = [copy(1 + j, w, me, (*chip, c), src=srcs[w]) for j, chip in enumerate(chips[:2])]
            group.append(copy(0, w, me, sibling, src=srcs[w]))
            for cp in group:
                cp.start()
            first += group
        save(me)
        locs = [pltpu.make_async_copy(pb, slot(1, me), loc_sems.at[0]),
                pltpu.make_async_copy(wob, wout_o.at[my_idx], loc_sems.at[1]),
                pltpu.make_async_copy(wob, woutb_o, loc_sems.at[2])]
        for cp in locs:
            cp.start()

        xv = x_ref[...]
        hv[...] = _bf(xv * lax.rsqrt(jnp.mean(xv * xv, axis=-1, keepdims=True) + EPS) * g_ref[...])
        rows = 256
        for r0 in range(0, T, rows):
            htv[:, r0:r0 + rows] = hv[r0:r0 + rows, :].T
        locs.append(pltpu.make_async_copy(htv, ht_o, loc_sems.at[3]))
        locs[-1].start()

        def out_copy(p, j):
            return pltpu.make_async_copy(stage.at[p], proj_o.at[j], out_sems.at[p])

        def project(nth, block):
            base = 3 * index(*block)

            def tile(p, carry):
                if nth > 0:
                    out_copy(p, base + p).wait()
                stage[p] = _dot(hv[...], wv[base + p], 1, 0)
                out_copy(p, base + p).start()
                return carry

            lax.fori_loop(0, 3, tile, 0)

        project(0, me)
        copy(0, 0, sibling, me).wait_recv()
        save(sibling)
        project(1, sibling)
        passed = []
        relay_from = (px ^ (1 - c), py ^ c, c)
        relay_to = (px ^ c, py ^ (1 - c), c)

        def arrived(w, j):
            copy(1 + j, w, (*chips[j], c), me).wait_recv()
            passed.append(copy(4 + j, w, (*chips[j], c), sibling))
            passed[-1].start()

        def relay(w):
            passed.append(copy(3, w, relay_from, relay_to))
            passed[-1].start()

        def handed(nth, j):
            copy(4 + j, 0, (*chips[j], 1 - c), me).wait_recv()
            save((*chips[j], 1 - c))
            project(nth, (*chips[j], 1 - c))

        arrived(0, 0)
        arrived(0, 1)
        relay(0)
        for j in range(2):
            save((*chips[j], c))
            project(2 + j, (*chips[j], c))
        handed(4, 0)
        handed(5, 1)
        arrived(1, 0)
        arrived(1, 1)
        relay(1)
        arrived(0, 2)
        save((*chips[2], c))
        project(6, (*chips[2], c))
        handed(7, 2)
        arrived(1, 2)
        copy(0, 1, sibling, me).wait_recv()
        for j, chip in enumerate(chips):
            copy(4 + j, 1, (*chip, 1 - c), me).wait_recv()
        keep = pltpu.make_async_copy(wv, wt_o, loc_sems.at[4])
        for p in range(3):
            out_copy(p, p).wait()
        for cp in first + passed:
            cp.wait_send()
        keep.wait()
        for cp in locs:
            cp.wait()

    vmem = pl.BlockSpec(memory_space=pltpu.VMEM)
    bf16 = jnp.bfloat16
    return pl.pallas_call(
        body, name="gather_proj",
        out_shape=(pltpu.HBM((D, T), bf16), pltpu.HBM((NTILE, D, TILE), bf16),
                   pltpu.HBM((DMIX // NDEV, D), bf16), pltpu.HBM((NDEV, DMIX // NDEV, D), bf16),
                   pltpu.HBM((NDEV, NGROUP, GROUP // NDEV, GROUP), bf16), pltpu.HBM((NTILE, T, TILE), jnp.float32)),
        in_specs=[vmem] * 5, out_specs=(ANY,) * 6,
        scratch_shapes=[pltpu.VMEM((T, D), bf16), pltpu.VMEM((D, T), bf16), pltpu.VMEM((NTILE, D, TILE), bf16),
                        pltpu.VMEM((DMIX // NDEV, D), bf16), pltpu.VMEM((NGROUP, GROUP // NDEV, GROUP), bf16),
                        pltpu.VMEM((3, T, TILE), jnp.float32),
                        pltpu.SemaphoreType.DMA((14,)), pltpu.SemaphoreType.DMA((14,)),
                        pltpu.SemaphoreType.DMA((5,)), pltpu.SemaphoreType.DMA((3,))],
        compiler_params=_params(vmem_mib=56),
    )(x, g1, w_in, w_out, pool_w)


def _split_start(name, arrays, n_copies, plan):
    k = len(arrays)

    def body(*refs):
        send_sems, recv_sems, token = refs[k], refs[k + 1], refs[-1]
        for i, (src, dst, to) in enumerate(plan(refs[:k])):
            pltpu.make_async_remote_copy(src_ref=src, dst_ref=dst, send_sem=send_sems.at[i],
                                         recv_sem=recv_sems.at[i], device_id=to, device_id_type=MESH).start()
        token[...] = jnp.zeros_like(token)

    out = pl.pallas_call(
        body, name=name,
        out_shape=(pltpu.SemaphoreType.DMA((n_copies,)), pltpu.SemaphoreType.DMA((n_copies,)),
                   *[pltpu.HBM(a.shape, a.dtype) for a in arrays], jax.ShapeDtypeStruct((8, 128), jnp.float32)),
        in_specs=[HBM] * k, out_specs=(SEM, SEM, *[HBM] * k, pl.BlockSpec(memory_space=pltpu.VMEM)),
        input_output_aliases={i: 2 + i for i in range(k)},
        compiler_params=pltpu.CompilerParams(has_side_effects=EFFECT),
    )(*[pltpu.with_memory_space_constraint(a, pltpu.HBM) for a in arrays])
    return out[0], out[1], out[2:2 + k], out[-1]


def _split_wait(name, arrays, send_sems, recv_sems, plan, after):
    k = len(arrays)

    def body(*refs):
        sends, recvs = refs[k], refs[k + 1]
        for i, (src, dst, to) in enumerate(plan(refs[:k])):
            cp = pltpu.make_async_remote_copy(src_ref=src, dst_ref=dst, send_sem=sends.at[i], recv_sem=recvs.at[i],
                                              device_id=to, device_id_type=MESH)
            cp.wait_send()
            cp.wait_recv()

    return pl.pallas_call(
        body, name=name,
        out_shape=tuple(pltpu.HBM(a.shape, a.dtype) for a in arrays),
        in_specs=[HBM] * k + [SEM, SEM, ANY], out_specs=(HBM,) * k,
        input_output_aliases={i: i for i in range(k)},
        compiler_params=pltpu.CompilerParams(has_side_effects=EFFECT),
    )(*arrays, send_sems, recv_sems, after)


def _plan_wout(refs):
    src, land = refs
    x, y, c, me = _place()
    return [(src, land.at[me], _peer(x, y, c, r)) for r in range(1, NDEV)]


def _plan_rest(refs):
    gob, gpf, r_out, r_pool = refs
    x, y, c, me = _place()
    plan = []
    for r in range(1, NDEV):
        plan.append((gob.at[me ^ r], r_out.at[r - 1], _peer(x, y, c, r)))
        plan.append((gpf.at[me ^ r], r_pool.at[r - 1], _peer(x, y, c, r)))
    return plan


def _plan_in(refs):
    sum_b, r_in = refs
    x, y, c, _ = _place()
    plan = []
    for j, (dx, dy) in enumerate(((1, 0), (0, 1), (1, 1))):
        px, py = x ^ dx, y ^ dy
        plan.append((sum_b.at[2 * px + py], r_in.at[j], (px, py, c)))
    return plan


def _gather_small(small, *after):
    def body(sm, *refs):
        r_small, send_sems, recv_sems, loc_sem = refs[len(after):]
        x, y, c, me = _place()
        loc = pltpu.make_async_copy(sm, r_small.at[me], loc_sem)
        loc.start()

        def copy(r, src_idx):
            return pltpu.make_async_remote_copy(
                src_ref=sm, dst_ref=r_small.at[src_idx], send_sem=send_sems.at[r - 1], recv_sem=recv_sems.at[r - 1],
                device_id=_peer(x, y, c, r), device_id_type=MESH)

        sends = [copy(r, me) for r in range(1, NDEV)]
        for cp in sends:
            cp.start()
        for r in range(1, NDEV):
            copy(r, me ^ r).wait_recv()
        for cp in sends:
            cp.wait_send()
        loc.wait()

    return pl.pallas_call(
        body, name="gather_small",
        out_shape=jax.ShapeDtypeStruct((NDEV, 8, D), jnp.float32),
        in_specs=[ANY] * (1 + len(after)), out_specs=ANY,
        scratch_shapes=[pltpu.SemaphoreType.DMA((NDEV - 1,)), pltpu.SemaphoreType.DMA((NDEV - 1,)),
                        pltpu.SemaphoreType.DMA],
    )(small, *after)


def _seg_tiles(s):
    return (s + 2) % NSEG


_POOL_SPECS = [pl.BlockSpec((None, T, GROUP), lambda g, base=base: (base + g, 0, 0)) for base in (0, 4)]
_HEAD_SPECS = [pl.BlockSpec((None, T, HEAD), lambda h, base=base: (base + h // 2, 0, h % 2))
               for base in (8, 12, 16, 20)]


def _row_ids(shape):
    return lax.broadcasted_iota(jnp.int32, shape, 0)


BAND_ROWS = 128
HALO = 16


def _window_sum(a, gidx, lead):
    width = lax.shift_left(jnp.int32(2), gidx)
    shape = (BAND_ROWS, BAND_ROWS + HALO)
    t, j = lax.broadcasted_iota(jnp.int32, shape, 0), lax.broadcasted_iota(jnp.int32, shape, 1)
    first = t if lead else t + HALO - width + 1
    band = _bf(jnp.where(j >= first, jnp.where(j < first + width, 1.0, 0.0), 0.0))
    zeros = jnp.zeros((HALO, a.shape[1]), jnp.bfloat16)
    padded = [jnp.concatenate([p, zeros] if lead else [zeros, p], axis=0) for p in _split2(a)]
    out = []
    for r0 in range(0, T, BAND_ROWS):
        slab = jnp.concatenate([p[r0:r0 + BAND_ROWS + HALO] for p in padded], axis=1)
        r = _dot(band, slab, 1, 0)
        out.append(r[:, :a.shape[1]] + r[:, a.shape[1]:])
    return jnp.concatenate(out, axis=0)


def _window_mean(s, gidx):
    inv = jnp.where(gidx == 0, 0.5, jnp.where(gidx == 1, 0.25, jnp.where(gidx == 2, 0.125, 0.0625)))
    width = lax.shift_left(jnp.int32(2), gidx)
    head = s[:16] / jnp.minimum(_row_ids((16, s.shape[1])) + 1, width).astype(jnp.float32)
    return jnp.concatenate([head, s[16:] * inv], axis=0)


def _pool_fwd(proj, pool_w, pool_scale, token):
    def body(u_ref, pg_ref, w_ref, sc_ref, token_any, y_ref):
        del token_any
        gidx = pl.program_id(0)
        u, pg = u_ref[...], pg_ref[...]
        d = _window_mean(_window_sum(u, gidx, False), gidx) - u
        mixed = _dot(_bf(d), w_ref[...], 1, 0)
        y_ref[...] = _bf(mixed * sc_ref[...] * (pg * _sigmoid(pg)))

    return pl.pallas_call(
        body, name="pool_fwd", grid=(NGROUP,),
        in_specs=[*_POOL_SPECS,
                  pl.BlockSpec((None, GROUP, GROUP), lambda g: (g, 0, 0)),
                  pl.BlockSpec((1, GROUP), lambda g: (0, g)), ANY],
        out_specs=pl.BlockSpec((T, GROUP), lambda g: (0, g)),
        out_shape=pltpu.HBM((T, DMIX), jnp.bfloat16),
        compiler_params=_params(("parallel",)),
    )(proj, proj, pool_w, pool_scale, token)


def _tri(lower):
    r = lax.broadcasted_iota(jnp.int32, (CHUNK, CHUNK), 0)
    c = lax.broadcasted_iota(jnp.int32, (CHUNK, CHUNK), 1)
    return (r >= c) if lower else (r <= c)


def _sum_rows_matrix():
    shape = (CHUNK + 16, CHUNK)
    r, c = lax.broadcasted_iota(jnp.int32, shape, 0), lax.broadcasted_iota(jnp.int32, shape, 1)
    run = jnp.where(c <= r, 1.0, 0.0)
    half = jnp.where(c < CHUNK // 2, 1.0, 0.0)
    return _bf(jnp.where(r < CHUNK, run, jnp.where(r < CHUNK + 8, 1.0, half)))


def _rev_sum_matrix():
    shape = (CHUNK, 2 * CHUNK)
    r, c = lax.broadcasted_iota(jnp.int32, shape, 0), lax.broadcasted_iota(jnp.int32, shape, 1)
    return _bf(jnp.where(c < CHUNK, jnp.where(c >= r, 1.0, 0.0), jnp.where(c - CHUNK < r, 1.0, 0.0)))


def _split2(a):
    hi = _bf(a)
    return [hi, _bf(a - hi.astype(jnp.float32))]


def _exact_sums(mat, pieces):
    x = jnp.concatenate([s for p in pieces for s in _split2(p)], axis=1)
    r = _dot(mat, x, 1, 0)
    return [r[:, 2 * j * HEAD:(2 * j + 1) * HEAD] + r[:, (2 * j + 1) * HEAD:(2 * j + 2) * HEAD]
            for j in range(len(pieces))]


def _gates(qv, fl, lb):
    sq = _sigmoid(qv)
    sg = _sigmoid(fl)
    f = lb + (1.0 - lb) * sg
    return dict(sq=sq, qs=qv * sq, sg=sg, f=f, kk=1.0 - f, g=jnp.log(f))


def _decays(sums):
    big_g = sums[:CHUNK]
    total = sums[CHUNK:CHUNK + 8]
    g_last = jnp.tile(total, (CHUNK // 8, 1))
    g_mid = jnp.tile(sums[CHUNK + 8:], (CHUNK // 8, 1))
    return dict(
        e_q=jnp.exp(big_g),
        e_k=jnp.exp(g_last - big_g),
        e_qm=jnp.exp(jnp.minimum(big_g - g_mid, EXP_CAP)),
        e_km=jnp.exp(jnp.minimum(g_mid - big_g, EXP_CAP)),
        total8=jnp.exp(total),
        state=jnp.exp(jnp.tile(total, (HEAD // 8, 1))))


def _group_rows(gi):
    return [pl.ds(pl.multiple_of((gi * NB + j) * CHUNK, CHUNK), CHUNK) for j in range(NB)]


def _lower_bound(lb_ref):
    return _sigmoid(lb_ref[0:1, :] - lb_ref[1:2, :])


def _hgrn_fwd(proj, lb_logits, rec_g, y_in):
    def body(q_ref, f_ref, i_ref, gate_ref, lb_ref, rg_ref, y_any, y_ref, o_ref, st_ref):
        del y_any
        lb = _lower_bound(lb_ref)
        causal = _tri(True)
        smat = _sum_rows_matrix()

        def group(gi, st):
            rows = _group_rows(gi)
            ts = [_gates(q_ref[r, :], f_ref[r, :], lb) for r in rows]
            ds = [_decays(s) for s in _exact_sums(smat, [t["g"] for t in ts])]
            vs = [_bf(i_ref[r, :]) for r in rows]
            q_m = [_bf(t["qs"] * d["e_qm"]) for t, d in zip(ts, ds)]
            k_m = [_bf(t["kk"] * d["e_km"]) for t, d in zip(ts, ds)]
            q_e = [_bf(t["qs"] * d["e_q"]) for t, d in zip(ts, ds)]
            k_e = [_bf(t["kk"] * d["e_k"]) for t, d in zip(ts, ds)]
            a = [_bf(jnp.where(causal, _dot(q_m[j], k_m[j], 1, 1), 0.0)) for j in range(NB)]
            intra = [_dot(a[j], vs[j], 1, 0) for j in range(NB)]
            upd = [_dot(vs[j], k_e[j], 0, 0) for j in range(NB)]
            for j in range(NB):
                st_ref[gi * NB + j] = st
                o_ref[rows[j], :] = intra[j] + _dot(q_e[j], _bf(st), 1, 1)
                st = st * ds[j]["state"] + upd[j]
            return st

        lax.fori_loop(0, NGRP, group, jnp.zeros((HEAD, HEAD), jnp.float32))
        o = o_ref[...]
        rn = o * lax.rsqrt(jnp.mean(o * o, axis=-1, keepdims=True) + EPS)
        gate = gate_ref[...]
        y_ref[...] = _bf(rn * rg_ref[...] * (gate * _sigmoid(gate)))

    return pl.pallas_call(
        body, name="hgrn_fwd", grid=(NHEAD,),
        in_specs=[*_HEAD_SPECS,
                  pl.BlockSpec((2, HEAD), lambda h: (0, h)),
                  pl.BlockSpec((1, HEAD), lambda h: (0, h)),
                  pl.BlockSpec(memory_space=pl.ANY)],
        out_specs=(pl.BlockSpec((T, HEAD), lambda h: (0, NHEAD + h)),
                   pl.BlockSpec((T, HEAD), lambda h: (0, h)),
                   pl.BlockSpec((None, NCHUNK, HEAD, HEAD), lambda h: (h, 0, 0, 0))),
        out_shape=(pltpu.HBM((T, DMIX), jnp.bfloat16), pltpu.HBM((T, D), jnp.float32),
                   pltpu.HBM((NHEAD, NCHUNK, HEAD, HEAD), jnp.float32)),
        input_output_aliases={6: 0},
        compiler_params=_params(("parallel",)),
    )(proj, proj, proj, proj, lb_logits, rec_g, y_in)


def _out_proj_loss(x, y, w_out, target, gf):
    rows = 512
    parts = [slice(k * rows // 2, (k + 1) * rows // 2) for k in range(2)]

    def body(x_ref, y_ref, w_ref, t_ref, g_ref, dz_ref, dzb_ref, sq_ref, dg_ref):
        zs = [x_ref[p, :] + _dot(y_ref[p, :], w_ref[...], 1, 0) for p in parts]
        sq = dg = 0.0
        for p, z in zip(parts, zs):
            r = lax.rsqrt(jnp.mean(z * z, axis=-1, keepdims=True) + EPS)
            zhat = z * r
            err = zhat * g_ref[...] - t_ref[p, :]
            dy = err * (1.0 / D)
            gdy = dy * g_ref[...]
            dz = r * (gdy - zhat * jnp.mean(zhat * gdy, axis=-1, keepdims=True))
            dz_ref[p, :] = dz
            dzb_ref[p, :] = _bf(dz)
            sq = sq + jnp.sum(err * err, axis=0, keepdims=True)
            dg = dg + jnp.sum(zhat * dy, axis=0, keepdims=True)

        @pl.when(pl.program_id(0) == 0)
        def _():
            sq_ref[...] = sq
            dg_ref[...] = dg

        @pl.when(pl.program_id(0) != 0)
        def _():
            sq_ref[...] += sq
            dg_ref[...] += dg

    tile = pl.BlockSpec((rows, D), lambda i: (i, 0))
    vec = pl.BlockSpec((1, D), lambda i: (0, 0))
    return pl.pallas_call(
        body, name="out_proj_loss", grid=(T // rows,),
        in_specs=[tile, pl.BlockSpec((rows, DMIX), lambda i: (i, 0)), pl.BlockSpec((DMIX, D), lambda i: (0, 0)),
                  tile, vec],
        out_specs=(tile, tile, vec, vec),
        out_shape=(pltpu.HBM((T, D), jnp.float32), pltpu.HBM((T, D), jnp.bfloat16),
                   jax.ShapeDtypeStruct((1, D), jnp.float32), jax.ShapeDtypeStruct((1, D), jnp.float32)),
        compiler_params=_params(("arbitrary",)),
    )(x, y, w_out, target, gf)


def _out_proj_bwd(dzb, w_out, y):
    tn = 512

    def body(dz_ref, w_ref, y_ref, dy_ref, gw_ref, gwb_ref):
        dz = dz_ref[...]
        dy_ref[...] = _dot(dz, w_ref[...], 1, 1)
        gw = _dot(y_ref[...], dz, 0, 0)
        gw_ref[...] = gw
        gwb_ref[...] = _bf(gw)

    return pl.pallas_call(
        body, name="out_proj_bwd", grid=(DMIX // tn,),
        in_specs=[pl.BlockSpec((T, D), lambda n: (0, 0)), pl.BlockSpec((tn, D), lambda n: (n, 0)),
                  pl.BlockSpec((T, tn), lambda n: (0, n))],
        out_specs=(pl.BlockSpec((T, tn), lambda n: (0, n)), pl.BlockSpec((tn, D), lambda n: (n, 0)),
                   pl.BlockSpec((tn, D), lambda n: (n, 0))),
        out_shape=(pltpu.HBM((T, DMIX), jnp.float32), pltpu.HBM((DMIX, D), jnp.float32),
                   pltpu.HBM((DMIX, D), jnp.bfloat16)),
        compiler_params=_params(("parallel",)),
    )(dzb, w_out, y)


def _hgrn_bwd(proj, lb_logits, rec_g, o, states, dymix, dproj_in, token):
    def body(q_ref, f_ref, i_ref, gate_ref, lb_ref, rg_ref, o_ref, st_ref, dy_ref, dp_any, token_any,
             dp_ref, drg_ref, dlb_ref, do_ref):
        del dp_any, token_any
        lb = _lower_bound(lb_ref)
        causal = _tri(True)
        smat, rmat = _sum_rows_matrix(), _rev_sum_matrix()

        o = o_ref[...]
        rs = lax.rsqrt(jnp.mean(o * o, axis=-1, keepdims=True) + EPS)
        rn = o * rs
        gate = gate_ref[...]
        sgate = _sigmoid(gate)
        dyv = dy_ref[...]
        d_r = dyv * (gate * sgate)
        dp_ref[3] = _bf(dyv * (rn * rg_ref[...]) * (sgate * (1.0 + gate * (1.0 - sgate))))
        drg_ref[...] = jnp.sum(d_r * rn, axis=0, keepdims=True)
        drn = d_r * rg_ref[...]
        do_ref[...] = rs * (drn - rn * jnp.mean(rn * drn, axis=-1, keepdims=True))

        def group(i, carry):
            dst, dlb = carry
            gi = NGRP - 1 - i
            rows = _group_rows(gi)
            span = range(NB)
            qvs = [q_ref[r, :] for r in rows]
            ts = [_gates(qv, f_ref[r, :], lb) for qv, r in zip(qvs, rows)]
            ds = [_decays(s) for s in _exact_sums(smat, [t["g"] for t in ts])]
            vs = [_bf(i_ref[r, :]) for r in rows]
            dos = [_bf(do_ref[r, :]) for r in rows]
            sts = [st_ref[gi * NB + j] for j in span]
            qe_f = [t["qs"] * d["e_q"] for t, d in zip(ts, ds)]
            ke_f = [t["kk"] * d["e_k"] for t, d in zip(ts, ds)]
            q_e, k_e = [_bf(a) for a in qe_f], [_bf(a) for a in ke_f]
            q_m = [_bf(t["qs"] * d["e_qm"]) for t, d in zip(ts, ds)]
            k_m = [_bf(t["kk"] * d["e_km"]) for t, d in zip(ts, ds)]
            a = [_bf(jnp.where(causal, _dot(q_m[j], k_m[j], 1, 1), 0.0)) for j in span]
            da = [_bf(jnp.where(causal, _dot(dos[j], vs[j], 1, 1), 0.0)) for j in span]
            dqm = [_dot(da[j], k_m[j], 1, 0) for j in span]
            dkm = [_dot(da[j], q_m[j], 0, 0) for j in span]
            dv_in = [_dot(a[j], dos[j], 0, 0) for j in span]
            dqe = [_dot(dos[j], _bf(sts[j]), 1, 0) for j in span]
            grow = [_dot(dos[j], q_e[j], 0, 0) for j in span]
            dke, carried = [None] * NB, [None] * NB
            for j in reversed(span):
                dst_b = _bf(dst)
                dke[j] = _dot(vs[j], dst_b, 1, 0)
                dp_ref[2, rows[j], :] = _bf(dv_in[j] + _dot(k_e[j], dst_b, 1, 1))
                carried[j] = ds[j]["total8"] * jnp.sum(dst * sts[j], axis=0, keepdims=True)
                dst = dst * ds[j]["state"] + grow[j]
            kdk = [ke_f[j] * dke[j] for j in span]
            pos = [(q_m[j].astype(jnp.float32) * dqm[j] - k_m[j].astype(jnp.float32) * dkm[j]) + qe_f[j] * dqe[j]
                   for j in span]
            dgs = _exact_sums(rmat, [jnp.concatenate([pos[j], kdk[j]], axis=0) for j in span])
            for j in span:
                t, d = ts[j], ds[j]
                dg = dgs[j] + jnp.tile(carried[j], (CHUNK // 8, 1))
                dqs = dqm[j] * d["e_qm"] + dqe[j] * d["e_q"]
                dkk = dkm[j] * d["e_km"] + dke[j] * d["e_k"]
                df = dg / t["f"] - dkk
                dp_ref[1, rows[j], :] = _bf(df * (1.0 - lb) * (t["sg"] * (1.0 - t["sg"])))
                dp_ref[0, rows[j], :] = _bf(dqs * (t["sq"] * (1.0 + qvs[j] * (1.0 - t["sq"]))))
                dlb = dlb + df * (1.0 - t["sg"])
            return dst, dlb

        _, dlb = lax.fori_loop(0, NGRP, group, (jnp.zeros((HEAD, HEAD), jnp.float32),
                                                jnp.zeros((CHUNK, HEAD), jnp.float32)))
        dlb_ref[...] = jnp.sum(dlb, axis=0, keepdims=True)

    vec = pl.BlockSpec((1, HEAD), lambda h: (0, h))
    return pl.pallas_call(
        body, name="hgrn_bwd", grid=(NHEAD,),
        in_specs=[*_HEAD_SPECS,
                  pl.BlockSpec((2, HEAD), lambda h: (0, h)), vec,
                  pl.BlockSpec((T, HEAD), lambda h: (0, h)),
                  pl.BlockSpec((None, NCHUNK, HEAD, HEAD), lambda h: (h, 0, 0, 0)),
                  pl.BlockSpec((T, HEAD), lambda h: (0, NHEAD + h)), ANY, ANY],
        out_specs=(pl.BlockSpec((4, T, HEAD), lambda h: (0, 0, h)), vec, vec),
        out_shape=(pltpu.HBM((NSEG, T, D), jnp.bfloat16),
                   jax.ShapeDtypeStruct((1, D), jnp.float32), jax.ShapeDtypeStruct((1, D), jnp.float32)),
        scratch_shapes=[pltpu.VMEM((T, HEAD), jnp.float32)],
        input_output_aliases={9: 0},
        compiler_params=_params(("parallel",)),
    )(proj, proj, proj, proj, lb_logits, rec_g, o, states, dymix, dproj_in, token)


def _pool_bwd(proj, pool_w, pool_scale, dymix, ht):
    def body(u_ref, pg_ref, w_ref, sc_ref, dy_ref, ht_ref, dp_ref, gw_ref, gs_ref, gwin_ref):
        gidx = pl.program_id(0)
        u, pg = u_ref[...], pg_ref[...]
        d = _bf(_window_mean(_window_sum(u, gidx, False), gidx) - u)
        mixed = _dot(d, w_ref[...], 1, 0)
        spg = _sigmoid(pg)
        dyv = dy_ref[...]
        d_p = dyv * (pg * spg)
        dp_ref[1] = _bf(dyv * (mixed * sc_ref[...]) * (spg * (1.0 + pg * (1.0 - spg))))
        gs_ref[...] = jnp.sum(d_p * mixed, axis=0, keepdims=True)
        dmixed = _bf(d_p * sc_ref[...])
        gw_ref[...] = _dot(d, dmixed, 0, 0)
        dd = _dot(dmixed, w_ref[...], 1, 1)
        dp_ref[0] = _bf(_window_sum(_window_mean(dd, gidx), gidx, True) - dd)
        for kind in range(2):
            gwin_ref[kind] = _dot(ht_ref[...], dp_ref[kind], 1, 0)

    return pl.pallas_call(
        body, name="pool_bwd", grid=(NGROUP,),
        in_specs=[*_POOL_SPECS,
                  pl.BlockSpec((None, GROUP, GROUP), lambda g: (g, 0, 0)),
                  pl.BlockSpec((1, GROUP), lambda g: (0, g)),
                  pl.BlockSpec((T, GROUP), lambda g: (0, g)),
                  pl.BlockSpec((D, T), lambda g: (0, 0))],
        out_specs=(pl.BlockSpec((2, T, GROUP), lambda g: (2, 0, g)),
                   pl.BlockSpec((None, GROUP, GROUP), lambda g: (g, 0, 0)),
                   pl.BlockSpec((1, GROUP), lambda g: (0, g)),
                   pl.BlockSpec((2, None, D, TILE), lambda g: (0, g, 0, 0))),
        out_shape=(pltpu.HBM((NSEG, T, D), jnp.bfloat16),
                   jax.ShapeDtypeStruct((NGROUP, GROUP, GROUP), jnp.float32),
                   jax.ShapeDtypeStruct((1, D), jnp.float32),
                   pltpu.HBM((2, NGROUP, D, TILE), jnp.float32)),
        compiler_params=_params(("parallel",), vmem_mib=56),
    )(proj, proj, pool_w, pool_scale, dymix, ht)


def _proj_bwd_w(place, ht, dproj, pool_tiles):
    half = NTILE // 2

    def owner_chip(i, pr):
        return jnp.where(i < half, i // 3, (pr[1] + 1 + (i - half) // 3) % 4)

    def tile_of(i, pr):
        side = jnp.where(i < half, 1 - pr[2], pr[2])
        return 6 * owner_chip(i, pr) + 3 * side + i % 3

    def dproj_block(i, pr):
        j = tile_of(i, pr)
        return ((j // 4 + 4) % NSEG, 0, j % 4)

    def pool_tile(i, pr):
        j = jnp.minimum(tile_of(i, pr), 2 * NGROUP - 1)
        return (j // NGROUP, j % NGROUP, 0, 0)

    def mine(i):
        return jnp.maximum(i, half)

    def body(place_ref, h_ref, dp_ref, pre_ref, sum_ref, own_ref, gw, sendbuf, recvbuf, send_sems, recv_sems):
        i = pl.program_id(0)
        px, py, c, _ = _place()
        taken = tile_of(i, place_ref) < 2 * NGROUP

        @pl.when(taken)
        def _():
            gw[...] = pre_ref[...]

        @pl.when(jnp.logical_not(taken))
        def _():
            gw[...] = _dot(h_ref[...], dp_ref[...], 1, 0)

        def to_sibling(slot):
            return pltpu.make_async_remote_copy(
                src_ref=sendbuf.at[slot], dst_ref=recvbuf.at[slot], send_sem=send_sems.at[slot],
                recv_sem=recv_sems.at[slot], device_id=(px, py, 1 - c), device_id_type=MESH)

        @pl.when(i < half)
        def _():
            sendbuf[i] = _bf(gw[...])
            to_sibling(i).start()

        @pl.when(i >= half)
        def _():
            slot = 3 * owner_chip(i, place_ref) + i % 3
            to_sibling(slot).wait_recv()
            total = gw[...] + recvbuf[slot].astype(jnp.float32)
            sum_ref[...] = _bf(total)
            own_ref[...] = total

        @pl.when(i == NTILE - 1)
        def _():
            for slot in range(half):
                to_sibling(slot).wait_send()

    return pl.pallas_call(
        body, name="proj_bwd_w",
        grid_spec=pltpu.PrefetchScalarGridSpec(
            num_scalar_prefetch=1, grid=(NTILE,),
            in_specs=[pl.BlockSpec((D, T), lambda i, pr: (0, 0)),
                      pl.BlockSpec((None, T, TILE), lambda i, pr: dproj_block(i, pr)),
                      pl.BlockSpec((None, None, D, TILE), lambda i, pr: pool_tile(i, pr))],
            out_specs=(pl.BlockSpec((None, None, D, TILE), lambda i, pr: (owner_chip(mine(i), pr), mine(i) % 3, 0, 0)),
                       pl.BlockSpec((None, D, TILE), lambda i, pr: (jnp.where(i < NTILE - 3, 0, i % 3), 0, 0))),
            scratch_shapes=[pltpu.VMEM((D, TILE), jnp.float32),
                            pltpu.VMEM((half, D, TILE), jnp.bfloat16), pltpu.VMEM((half, D, TILE), jnp.bfloat16),
                            pltpu.SemaphoreType.DMA((half,)), pltpu.SemaphoreType.DMA((half,))]),
        out_shape=(pltpu.HBM((4, 3, D, TILE), jnp.bfloat16), pltpu.HBM((3, D, TILE), jnp.float32)),
        compiler_params=_params(("arbitrary",)),
    )(place, ht, dproj, pool_tiles)


def _proj_bwd_x(dproj, w_t, x, g1, dz, token):
    tm = 512
    pairs = NSEG // 2

    def body(dp_ref, w_ref, x_ref, g_ref, dz_ref, token_any, dx_ref, dg_ref, wcat, acc):
        del token_any
        m, s = pl.program_id(0), pl.program_id(1)
        r = None
        for k in range(2):
            for i in range(4):
                wcat[k, :, i * TILE:(i + 1) * TILE] = w_ref[4 * k + i]
            part = _dot(dp_ref[k], wcat[k], 1, 1)
            r = part if r is None else r + part

        @pl.when(s == 0)
        def _():
            acc[...] = r

        @pl.when(s != 0)
        def _():
            acc[...] += r

        @pl.when(s == pairs - 1)
        def _():
            xv = x_ref[...]
            rs = lax.rsqrt(jnp.mean(xv * xv, axis=-1, keepdims=True) + EPS)
            xhat = xv * rs
            dhv = acc[...]
            gdh = dhv * g_ref[...]
            dx_ref[...] = dz_ref[...] + rs * (gdh - xhat * jnp.mean(xhat * gdh, axis=-1, keepdims=True))
            dg = jnp.sum(xhat * dhv, axis=0, keepdims=True)

            @pl.when(m == 0)
            def _():
                dg_ref[...] = dg

            @pl.when(m != 0)
            def _():
                dg_ref[...] += dg

    rows = pl.BlockSpec((tm, D), lambda m, s: (m, 0))
    vec = pl.BlockSpec((1, D), lambda m, s: (0, 0))
    return pl.pallas_call(
        body, name="proj_bwd_x", grid=(T // tm, pairs),
        in_specs=[pl.BlockSpec((2, tm, D), lambda m, s: (s, m, 0)),
                  pl.BlockSpec((8, D, TILE), lambda m, s: ((s + 1) % pairs, 0, 0)), rows, vec, rows, ANY],
        out_specs=(rows, vec),
        out_shape=(jax.ShapeDtypeStruct((T, D), jnp.float32), jax.ShapeDtypeStruct((1, D), jnp.float32)),
        scratch_shapes=[pltpu.VMEM((2, D, D), jnp.bfloat16), pltpu.VMEM((tm, D), jnp.float32)],
        compiler_params=_params(("arbitrary", "arbitrary"), vmem_mib=56),
    )(dproj, w_t, x, g1, dz, token)


def _adamw(w, g, m, v):
    m_new = ADAM_B1 * m + (1.0 - ADAM_B1) * g
    v_new = ADAM_B2 * v + (1.0 - ADAM_B2) * (g * g)
    delta = -ADAM_LR * ((m_new / BC1) / (jnp.sqrt(v_new / BC2) + ADAM_EPS) + ADAM_WD * w)
    return delta, m_new, v_new


def _reduce_adam(name, place, parts, w, m, v, grid, w_spec):
    n = len(parts)

    def body(place_ref, *refs):
        del place_ref
        w_ref, m_ref, v_ref, g_ref, d_ref, mo_ref, vo_ref = refs[n:]
        g = None
        for ref, (_, _, stacked) in zip(refs[:n], parts):
            terms = [ref[r] for r in range(ref.shape[0])] if stacked else [ref[...]]
            for t in terms:
                g = t.astype(jnp.float32) if g is None else g + t.astype(jnp.float32)
        delta, m_new, v_new = _adamw(w_ref[...], g, m_ref[...], v_ref[...])
        g_ref[...] = g
        d_ref[...] = delta
        mo_ref[...] = m_new
        vo_ref[...] = v_new

    shape = jax.ShapeDtypeStruct(w.shape, jnp.float32)
    return pl.pallas_call(
        body, name=name,
        grid_spec=pltpu.PrefetchScalarGridSpec(
            num_scalar_prefetch=1, grid=grid,
            in_specs=[spec for _, spec, _ in parts] + [w_spec] * 3, out_specs=(w_spec,) * 4),
        out_shape=(shape,) * 4,
        compiler_params=_params(("parallel",)),
    )(place, *[_in_hbm(a) for a in [a for a, _, _ in parts] + [w, m, v]])


def _small_adam(parts, w, m, v):
    def body(p_ref, w_ref, m_ref, v_ref, g_ref, d_ref, mo_ref, vo_ref):
        g = p_ref[0]
        for s in range(1, NDEV):
            g = g + p_ref[s]
        wv = w_ref[...]
        rows = _row_ids(wv.shape)
        other = jnp.where(rows == 2, pltpu.roll(wv, 7, 0), jnp.where(rows == 3, pltpu.roll(wv, 1, 0), 0.0))
        lbv = _sigmoid(wv - other)
        sign = jnp.where(rows == 2, 1.0, -1.0)
        g = jnp.where((rows == 2) | (rows == 3), sign * g * lbv * (1.0 - lbv), g)
        delta, m_new, v_new = _adamw(wv, g, m_ref[...], v_ref[...])
        g_ref[...] = g
        d_ref[...] = delta
        mo_ref[...] = m_new
        vo_ref[...] = v_new

    shape = jax.ShapeDtypeStruct((8, D), jnp.float32)
    return pl.pallas_call(body, name="small_adam", out_shape=(shape,) * 4)(parts, w, m, v)


def _rows8(*vecs):
    rows = [a.reshape(-1, D) for a in vecs]
    n = sum(r.shape[0] for r in rows)
    return jnp.concatenate(rows + [jnp.zeros((8 - n, D), jnp.float32)], axis=0)


def kernel(x, norm1_g, w_in, pool_w, pool_scale, lb_logits, rec_norm_g, w_out, final_norm_g, loss_target, m_norm1_g, m_w_in, m_pool_w, m_pool_scale, m_lb_logits, m_rec_norm_g, m_w_out, m_final_norm_g, v_norm1_g, v_w_in, v_pool_w, v_pool_scale, v_lb_logits, v_rec_norm_g, v_w_out, v_final_norm_g):
    xs = x[0]
    target = loss_target[0]
    ix, iy, ic = lax.axis_index("x"), lax.axis_index("y"), lax.axis_index("c")
    place = jnp.stack([4 * ix + 2 * iy + ic, 2 * ix + iy, ic]).astype(jnp.int32)
    gf = final_norm_g.reshape(1, D)

    ht, w_t, w_out_b, w_out_g, pool_g, proj = _gather_proj(xs, norm1_g, w_in, w_out, pool_w)
    pool_full = pool_g.transpose(1, 0, 2, 3).reshape(NGROUP, GROUP, GROUP)
    wout = [w_out_b, w_out_g]
    wout_send, wout_recv, wout, wout_token = _split_start("gather_wout_start", wout, NDEV - 1, _plan_wout)

    y = _pool_fwd(proj, pool_full, pool_scale, wout_token)
    y, o, states = _hgrn_fwd(proj, lb_logits, rec_norm_g, y)
    _, w_out_g = _split_wait("gather_wout_wait", wout, wout_send, wout_recv, _plan_wout, o)
    w_out_full = _in_hbm(w_out_g.reshape(DMIX, D))
    dz, dzb, sq, dgf = _out_proj_loss(xs, y, w_out_full, target, gf)

    dymix, gwout_f, gwout_b = _out_proj_bwd(dzb, w_out_full, y)
    dproj, gpool, dscale, pool_tiles = _pool_bwd(proj, pool_full, pool_scale, dymix, ht)

    blk_out = (NDEV, DMIX // NDEV, D)
    blk_pool = (NDEV, NGROUP, GROUP // NDEV, GROUP)
    gpool_s = gpool.reshape(NGROUP, NDEV, GROUP // NDEV, GROUP).transpose(1, 0, 2, 3)
    rest = [gwout_b.reshape(blk_out), gpool_s,
            lax.empty((NDEV - 1,) + blk_out[1:], jnp.bfloat16), lax.empty((NDEV - 1,) + blk_pool[1:], jnp.float32)]
    rest_send, rest_recv, rest, rest_token = _split_start("scatter_rest_start", rest, 2 * (NDEV - 1), _plan_rest)

    dproj, drecg, dlb = _hgrn_bwd(proj, lb_logits, rec_norm_g, o, states, dymix, dproj, rest_token)
    chip_sums, own_sum = _proj_bwd_w(place, ht, dproj, pool_tiles)
    win = [chip_sums, lax.empty((3, 3, D, TILE), jnp.bfloat16)]
    win_send, win_recv, win, win_token = _split_start("scatter_win_start", win, 3, _plan_in)

    grad_x, dg1 = _proj_bwd_x(dproj, w_t, xs, norm1_g, dz, win_token)

    _, gpool_own, r_out, r_pool = _split_wait("scatter_rest_wait", rest, rest_send, rest_recv, _plan_rest, grad_x)
    g_wout, d_wout, m_wout, v_wout = _reduce_adam(
        "adam_w_out", place,
        [(gwout_f.reshape(blk_out), pl.BlockSpec((None,) + blk_out[1:], lambda i, pr: (pr[0], 0, 0)), False),
         (r_out, pl.BlockSpec((NDEV - 1,) + blk_out[1:], lambda i, pr: (0, 0, 0)), True)],
        w_out, m_w_out, v_w_out, (1,), pl.BlockSpec((None,) + blk_out[1:], lambda i, pr: (0, 0, 0)))
    g_pool, d_pool, m_pool, v_pool = _reduce_adam(
        "adam_pool_w", place,
        [(gpool_own, pl.BlockSpec((None,) + blk_pool[1:], lambda i, pr: (pr[0], 0, 0, 0)), False),
         (r_pool, pl.BlockSpec((NDEV - 1,) + blk_pool[1:], lambda i, pr: (0, 0, 0, 0)), True)],
        pool_w, m_pool_w, v_pool_w, (1,), pl.BlockSpec((None,) + blk_pool[1:], lambda i, pr: (0, 0, 0, 0)))

    r_small = _gather_small(_rows8(dg1, dscale, dlb, dlb, drecg, dgf, sq), d_wout, d_pool)
    g_s, d_s, m_s, v_s = _small_adam(
        r_small,
        _rows8(norm1_g, pool_scale, lb_logits, rec_norm_g, final_norm_g),
        _rows8(m_norm1_g, m_pool_scale, m_lb_logits, m_rec_norm_g, m_final_norm_g),
        _rows8(v_norm1_g, v_pool_scale, v_lb_logits, v_rec_norm_g, v_final_norm_g))
    loss = jnp.sum(g_s[6]) * (0.5 / D)

    _, r_in = _split_wait("scatter_win_wait", win, win_send, win_recv, _plan_in, v_s)
    g_win, d_win, m_win, v_win = _reduce_adam(
        "adam_w_in", place,
        [(own_sum, pl.BlockSpec((None, D, TILE), lambda i, pr: (i, 0, 0)), False),
         (r_in, pl.BlockSpec((3, None, D, TILE), lambda i, pr: (0, i, 0, 0)), True)],
        w_in, m_w_in, v_w_in, (3,), pl.BlockSpec((None, D, TILE), lambda i, pr: (0, 0, i)))

    def small_outs(a):
        return a[0:1], a[1:2], a[2:4], a[4:5], a[5]

    def outs(small_a, win, pool, wout):
        n1, ps, lbl, rg, fg = small_outs(small_a)
        return n1, win, pool, ps, lbl, rg, wout, fg

    return (loss, grad_x[None],
            *outs(g_s, g_win, g_pool, g_wout), *outs(d_s, d_win, d_pool, d_wout),
            *outs(m_s, m_win, m_pool, m_wout), *outs(v_s, v_win, v_pool, v_wout))
```

```python
import functools

import jax
import jax.numpy as jnp
from jax import lax
from jax.experimental import pallas as pl
from jax.experimental.pallas import tpu as pltpu

T = 2048
D = 1024
NSEG = 6
NTILE = 24
TILE = 256
DMIX = 2048
NDEV = 8
HEAD = 128
NHEAD = 8
CHUNK = 64
NCHUNK = T // CHUNK
NB = 32
NGRP = NCHUNK // NB
NGROUP = 4
GROUP = 256
EPS = 1e-6
EXP_CAP = 80.0
MESH = pl.DeviceIdType.MESH
AXES = ("x", "y", "c")
ANY = pl.BlockSpec(memory_space=pl.ANY)
HBM = pl.BlockSpec(memory_space=pltpu.HBM)
SEM = pl.BlockSpec(memory_space=pltpu.SEMAPHORE)
EFFECT = pltpu.SideEffectType.DATAFLOW_SIDE_EFFECTING

ADAM_LR = 0.001
ADAM_B1 = 0.9
ADAM_B2 = 0.999
ADAM_EPS = 1e-08
ADAM_WD = 0.01
ADAM_STEP = 10
BC1 = 1.0 - ADAM_B1 ** ADAM_STEP
BC2 = 1.0 - ADAM_B2 ** ADAM_STEP

MIB = 1 << 20


def _params(sem=None, vmem_mib=48):
    return pltpu.CompilerParams(dimension_semantics=sem, vmem_limit_bytes=vmem_mib * MIB)


def _sigmoid(v):
    return 1.0 / (1.0 + jnp.exp(-v))


def _dot(a, b, ca, cb, precision=None):
    return lax.dot_general(a, b, (((ca,), (cb,)), ((), ())), precision=precision,
                           preferred_element_type=jnp.float32)


def _bf(v):
    return v.astype(jnp.bfloat16)


def _in_hbm(a):
    return pltpu.with_memory_space_constraint(a, pltpu.HBM)


def _place():
    x, y, c = lax.axis_index("x"), lax.axis_index("y"), lax.axis_index("c")
    return x, y, c, 4 * x + 2 * y + c


def _peer(x, y, c, r):
    return (x ^ ((r >> 2) & 1), y ^ ((r >> 1) & 1), c ^ (r & 1))


def _gather_proj(x, g1, w_in, w_out, pool_w):
    def body(x_ref, g_ref, win_ref, wout_ref, pool_ref, ht_o, wt_o, woutb_o, wout_o, pool_o, proj_o,
             xbuf, hv, htv, wv, wob, pb, stage, send_sems, recv_sems, loc_sems, out_sems):
        px, py, c, my_idx = _place()
        fetch_x = pltpu.make_async_copy(x_ref, xbuf, loc_sems.at[5])
        fetch_x.start()
        me, sibling = (px, py, c), (px, py, 1 - c)
        chips = [(1 - px, py), (px, 1 - py), (1 - px, 1 - py)]
        for p in range(3):
            wv[3 * my_idx + p] = _bf(win_ref[0, :, p * TILE:(p + 1) * TILE])

        def index(bx, by, bc):
            return 4 * bx + 2 * by + bc

        def slot(w, block):
            return wv.at[pl.ds(3 * index(*block), 3)] if w == 0 else pool_o.at[index(*block)]

        def copy(k, w, block, to, src=None):
            return pltpu.make_async_remote_copy(
                src_ref=slot(w, block) if src is None else src, dst_ref=slot(w, block),
                send_sem=send_sems.at[2 * k + w], recv_sem=recv_sems.at[2 * k + w],
                device_id=to, device_id_type=MESH)

        def save(block):
            at = pl.ds(3 * index(*block), 3)
            pltpu.make_async_copy(wv.at[at], wt_o.at[at], loc_sems.at[4]).start()

        srcs = (slot(0, me), pb)
        first = []
        for w in (0, 1):
            if w == 1:
                pb[...] = _bf(pool_ref[0])
                wob[...] = _bf(wout_ref[0])
            group = [copy(1 + j, w, me, (*chip, c), src=srcs[w]) for j, chip in enumerate(chips[:2])]
            group.append(copy(0, w, me, sibling, src=srcs[w]))
            for cp in group:
                cp.start()
            first += group
        save(me)
        locs = [pltpu.make_async_copy(pb, slot(1, me), loc_sems.at[0]),
                pltpu.make_async_copy(wob, wout_o.at[my_idx], loc_sems.at[1]),
                pltpu.make_async_copy(wob, woutb_o, loc_sems.at[2])]
        for cp in locs:
            cp.start()

        fetch_x.wait()
        xv = xbuf[...]
        hv[...] = _bf(xv * lax.rsqrt(jnp.mean(xv * xv, axis=-1, keepdims=True) + EPS) * g_ref[...])
        rows = 256
        for r0 in range(0, T, rows):
            htv[:, r0:r0 + rows] = hv[r0:r0 + rows, :].T
        locs.append(pltpu.make_async_copy(htv, ht_o, loc_sems.at[3]))
        locs[-1].start()

        def out_copy(p, j):
            return pltpu.make_async_copy(stage.at[p], proj_o.at[j], out_sems.at[p])

        def project(nth, block):
            base = 3 * index(*block)

            def tile(p, carry):
                if nth > 0:
                    out_copy(p, base + p).wait()
                stage[p] = _dot(hv[...], wv[base + p], 1, 0)
                out_copy(p, base + p).start()
                return carry

            lax.fori_loop(0, 3, tile, 0)

        project(0, me)
        copy(0, 0, sibling, me).wait_recv()
        save(sibling)
        project(1, sibling)
        passed = []
        relay_from = (px ^ (1 - c), py ^ c, c)
        relay_to = (px ^ c, py ^ (1 - c), c)

        def arrived(w, j):
            copy(1 + j, w, (*chips[j], c), me).wait_recv()
            passed.append(copy(4 + j, w, (*chips[j], c), sibling))
            passed[-1].start()

        def relay(w):
            passed.append(copy(3, w, relay_from, relay_to))
            passed[-1].start()

        def handed(nth, j):
            copy(4 + j, 0, (*chips[j], 1 - c), me).wait_recv()
            save((*chips[j], 1 - c))
            project(nth, (*chips[j], 1 - c))

        arrived(0, 0)
        arrived(0, 1)
        relay(0)
        for j in range(2):
            save((*chips[j], c))
            project(2 + j, (*chips[j], c))
        handed(4, 0)
        handed(5, 1)
        arrived(1, 0)
        arrived(1, 1)
        relay(1)
        arrived(0, 2)
        save((*chips[2], c))
        project(6, (*chips[2], c))
        handed(7, 2)
        arrived(1, 2)
        copy(0, 1, sibling, me).wait_recv()
        for j, chip in enumerate(chips):
            copy(4 + j, 1, (*chip, 1 - c), me).wait_recv()
        keep = pltpu.make_async_copy(wv, wt_o, loc_sems.at[4])
        for p in range(3):
            out_copy(p, p).wait()
        for cp in first + passed:
            cp.wait_send()
        keep.wait()
        for cp in locs:
            cp.wait()

    vmem = pl.BlockSpec(memory_space=pltpu.VMEM)
    bf16 = jnp.bfloat16
    return pl.pallas_call(
        body, name="gather_proj",
        out_shape=(pltpu.HBM((D, T), bf16), pltpu.HBM((NTILE, D, TILE), bf16),
                   pltpu.HBM((DMIX // NDEV, D), bf16), pltpu.HBM((NDEV, DMIX // NDEV, D), bf16),
                   pltpu.HBM((NDEV, NGROUP, GROUP // NDEV, GROUP), bf16), pltpu.HBM((NTILE, T, TILE), jnp.float32)),
        in_specs=[ANY] + [vmem] * 4, out_specs=(ANY,) * 6,
        scratch_shapes=[pltpu.VMEM((T, D), jnp.float32),
                        pltpu.VMEM((T, D), bf16), pltpu.VMEM((D, T), bf16), pltpu.VMEM((NTILE, D, TILE), bf16),
                        pltpu.VMEM((DMIX // NDEV, D), bf16), pltpu.VMEM((NGROUP, GROUP // NDEV, GROUP), bf16),
                        pltpu.VMEM((3, T, TILE), jnp.float32),
                        pltpu.SemaphoreType.DMA((14,)), pltpu.SemaphoreType.DMA((14,)),
                        pltpu.SemaphoreType.DMA((6,)), pltpu.SemaphoreType.DMA((3,))],
        compiler_params=_params(vmem_mib=56),
    )(x, g1, w_in, w_out, pool_w)


def _split_start(name, arrays, n_copies, plan):
    k = len(arrays)

    def body(*refs):
        send_sems, recv_sems, token = refs[k], refs[k + 1], refs[-1]
        for i, (src, dst, to) in enumerate(plan(refs[:k])):
            pltpu.make_async_remote_copy(src_ref=src, dst_ref=dst, send_sem=send_sems.at[i],
                                         recv_sem=recv_sems.at[i], device_id=to, device_id_type=MESH).start()
        token[...] = jnp.zeros_like(token)

    out = pl.pallas_call(
        body, name=name,
        out_shape=(pltpu.SemaphoreType.DMA((n_copies,)), pltpu.SemaphoreType.DMA((n_copies,)),
                   *[pltpu.HBM(a.shape, a.dtype) for a in arrays], jax.ShapeDtypeStruct((8, 128), jnp.float32)),
        in_specs=[HBM] * k, out_specs=(SEM, SEM, *[HBM] * k, pl.BlockSpec(memory_space=pltpu.VMEM)),
        input_output_aliases={i: 2 + i for i in range(k)},
        compiler_params=pltpu.CompilerParams(has_side_effects=EFFECT),
    )(*[pltpu.with_memory_space_constraint(a, pltpu.HBM) for a in arrays])
    return out[0], out[1], out[2:2 + k], out[-1]


def _split_wait(name, arrays, send_sems, recv_sems, plan, after):
    k = len(arrays)

    def body(*refs):
        sends, recvs = refs[k], refs[k + 1]
        for i, (src, dst, to) in enumerate(plan(refs[:k])):
            cp = pltpu.make_async_remote_copy(src_ref=src, dst_ref=dst, send_sem=sends.at[i], recv_sem=recvs.at[i],
                                              device_id=to, device_id_type=MESH)
            cp.wait_send()
            cp.wait_recv()

    return pl.pallas_call(
        body, name=name,
        out_shape=tuple(pltpu.HBM(a.shape, a.dtype) for a in arrays),
        in_specs=[HBM] * k + [SEM, SEM, ANY], out_specs=(HBM,) * k,
        input_output_aliases={i: i for i in range(k)},
        compiler_params=pltpu.CompilerParams(has_side_effects=EFFECT),
    )(*arrays, send_sems, recv_sems, after)


def _plan_wout(refs):
    src, land = refs
    x, y, c, me = _place()
    return [(src, land.at[me], _peer(x, y, c, r)) for r in range(1, NDEV)]


def _plan_rest(refs):
    gob, gpf, r_out, r_pool = refs
    x, y, c, me = _place()
    plan = []
    for r in range(1, NDEV):
        plan.append((gob.at[me ^ r], r_out.at[r - 1], _peer(x, y, c, r)))
        plan.append((gpf.at[me ^ r], r_pool.at[r - 1], _peer(x, y, c, r)))
    return plan


def _plan_in(refs):
    sum_b, r_in = refs
    x, y, c, _ = _place()
    plan = []
    for j, (dx, dy) in enumerate(((1, 0), (0, 1), (1, 1))):
        px, py = x ^ dx, y ^ dy
        plan.append((sum_b.at[2 * px + py], r_in.at[j], (px, py, c)))
    return plan


def _gather_small(small, *after):
    def body(sm, *refs):
        r_small, send_sems, recv_sems, loc_sem = refs[len(after):]
        x, y, c, me = _place()
        loc = pltpu.make_async_copy(sm, r_small.at[me], loc_sem)
        loc.start()

        def copy(r, src_idx):
            return pltpu.make_async_remote_copy(
                src_ref=sm, dst_ref=r_small.at[src_idx], send_sem=send_sems.at[r - 1], recv_sem=recv_sems.at[r - 1],
                device_id=_peer(x, y, c, r), device_id_type=MESH)

        sends = [copy(r, me) for r in range(1, NDEV)]
        for cp in sends:
            cp.start()
        for r in range(1, NDEV):
            copy(r, me ^ r).wait_recv()
        for cp in sends:
            cp.wait_send()
        loc.wait()

    return pl.pallas_call(
        body, name="gather_small",
        out_shape=jax.ShapeDtypeStruct((NDEV, 8, D), jnp.float32),
        in_specs=[ANY] * (1 + len(after)), out_specs=ANY,
        scratch_shapes=[pltpu.SemaphoreType.DMA((NDEV - 1,)), pltpu.SemaphoreType.DMA((NDEV - 1,)),
                        pltpu.SemaphoreType.DMA],
    )(small, *after)


def _seg_tiles(s):
    return (s + 2) % NSEG


_POOL_SPECS = [pl.BlockSpec((None, T, GROUP), lambda g, base=base: (base + g, 0, 0)) for base in (0, 4)]
_HEAD_SPECS = [pl.BlockSpec((None, T, HEAD), lambda h, base=base: (base + h // 2, 0, h % 2))
               for base in (8, 12, 16, 20)]


def _row_ids(shape):
    return lax.broadcasted_iota(jnp.int32, shape, 0)


BAND_ROWS = 128
HALO = 16


def _window_sum(a, gidx, lead):
    width = lax.shift_left(jnp.int32(2), gidx)
    shape = (BAND_ROWS, BAND_ROWS + HALO)
    t, j = lax.broadcasted_iota(jnp.int32, shape, 0), lax.broadcasted_iota(jnp.int32, shape, 1)
    first = t if lead else t + HALO - width + 1
    band = _bf(jnp.where(j >= first, jnp.where(j < first + width, 1.0, 0.0), 0.0))
    zeros = jnp.zeros((HALO, a.shape[1]), jnp.bfloat16)
    padded = [jnp.concatenate([p, zeros] if lead else [zeros, p], axis=0) for p in _split2(a)]
    out = []
    for r0 in range(0, T, BAND_ROWS):
        slab = jnp.concatenate([p[r0:r0 + BAND_ROWS + HALO] for p in padded], axis=1)
        r = _dot(band, slab, 1, 0)
        out.append(r[:, :a.shape[1]] + r[:, a.shape[1]:])
    return jnp.concatenate(out, axis=0)


def _window_mean(s, gidx):
    inv = jnp.where(gidx == 0, 0.5, jnp.where(gidx == 1, 0.25, jnp.where(gidx == 2, 0.125, 0.0625)))
    width = lax.shift_left(jnp.int32(2), gidx)
    head = s[:16] / jnp.minimum(_row_ids((16, s.shape[1])) + 1, width).astype(jnp.float32)
    return jnp.concatenate([head, s[16:] * inv], axis=0)


def _pool_fwd(proj, pool_w, pool_scale, token):
    def body(u_ref, pg_ref, w_ref, sc_ref, token_any, y_ref):
        del token_any
        gidx = pl.program_id(0)
        u, pg = u_ref[...], pg_ref[...]
        d = _window_mean(_window_sum(u, gidx, False), gidx) - u
        mixed = _dot(_bf(d), w_ref[...], 1, 0)
        y_ref[...] = _bf(mixed * sc_ref[...] * (pg * _sigmoid(pg)))

    return pl.pallas_call(
        body, name="pool_fwd", grid=(NGROUP,),
        in_specs=[*_POOL_SPECS,
                  pl.BlockSpec((None, GROUP, GROUP), lambda g: (g, 0, 0)),
                  pl.BlockSpec((1, GROUP), lambda g: (0, g)), ANY],
        out_specs=pl.BlockSpec((T, GROUP), lambda g: (0, g)),
        out_shape=pltpu.HBM((T, DMIX), jnp.bfloat16),
        compiler_params=_params(("parallel",)),
    )(proj, proj, pool_w, pool_scale, token)


def _tri(lower):
    r = lax.broadcasted_iota(jnp.int32, (CHUNK, CHUNK), 0)
    c = lax.broadcasted_iota(jnp.int32, (CHUNK, CHUNK), 1)
    return (r >= c) if lower else (r <= c)


def _sum_rows_matrix():
    shape = (CHUNK + 16, CHUNK)
    r, c = lax.broadcasted_iota(jnp.int32, shape, 0), lax.broadcasted_iota(jnp.int32, shape, 1)
    run = jnp.where(c <= r, 1.0, 0.0)
    half = jnp.where(c < CHUNK // 2, 1.0, 0.0)
    return _bf(jnp.where(r < CHUNK, run, jnp.where(r < CHUNK + 8, 1.0, half)))


def _rev_sum_matrix():
    shape = (CHUNK, 2 * CHUNK)
    r, c = lax.broadcasted_iota(jnp.int32, shape, 0), lax.broadcasted_iota(jnp.int32, shape, 1)
    return _bf(jnp.where(c < CHUNK, jnp.where(c >= r, 1.0, 0.0), jnp.where(c - CHUNK < r, 1.0, 0.0)))


def _split2(a):
    hi = _bf(a)
    return [hi, _bf(a - hi.astype(jnp.float32))]


def _exact_sums(mat, pieces):
    x = jnp.concatenate([s for p in pieces for s in _split2(p)], axis=1)
    r = _dot(mat, x, 1, 0)
    return [r[:, 2 * j * HEAD:(2 * j + 1) * HEAD] + r[:, (2 * j + 1) * HEAD:(2 * j + 2) * HEAD]
            for j in range(len(pieces))]


def _gates(qv, fl, lb):
    sq = _sigmoid(qv)
    sg = _sigmoid(fl)
    f = lb + (1.0 - lb) * sg
    return dict(sq=sq, qs=qv * sq, sg=sg, f=f, kk=1.0 - f, g=jnp.log(f))


def _decays(sums):
    big_g = sums[:CHUNK]
    total = sums[CHUNK:CHUNK + 8]
    g_last = jnp.tile(total, (CHUNK // 8, 1))
    g_mid = jnp.tile(sums[CHUNK + 8:], (CHUNK // 8, 1))
    return dict(
        e_q=jnp.exp(big_g),
        e_k=jnp.exp(g_last - big_g),
        e_qm=jnp.exp(jnp.minimum(big_g - g_mid, EXP_CAP)),
        e_km=jnp.exp(jnp.minimum(g_mid - big_g, EXP_CAP)),
        total8=jnp.exp(total),
        state=jnp.exp(jnp.tile(total, (HEAD // 8, 1))))


def _group_rows(gi):
    return [pl.ds(pl.multiple_of((gi * NB + j) * CHUNK, CHUNK), CHUNK) for j in range(NB)]


def _lower_bound(lb_ref):
    return _sigmoid(lb_ref[0:1, :] - lb_ref[1:2, :])


def _hgrn_fwd(proj, lb_logits, rec_g, y_in):
    def body(q_ref, f_ref, i_ref, gate_ref, lb_ref, rg_ref, y_any, y_ref, o_ref, st_ref):
        del y_any
        lb = _lower_bound(lb_ref)
        causal = _tri(True)
        smat = _sum_rows_matrix()

        def group(gi, st):
            rows = _group_rows(gi)
            ts = [_gates(q_ref[r, :], f_ref[r, :], lb) for r in rows]
            ds = [_decays(s) for s in _exact_sums(smat, [t["g"] for t in ts])]
            vs = [_bf(i_ref[r, :]) for r in rows]
            q_m = [_bf(t["qs"] * d["e_qm"]) for t, d in zip(ts, ds)]
            k_m = [_bf(t["kk"] * d["e_km"]) for t, d in zip(ts, ds)]
            q_e = [_bf(t["qs"] * d["e_q"]) for t, d in zip(ts, ds)]
            k_e = [_bf(t["kk"] * d["e_k"]) for t, d in zip(ts, ds)]
            a = [_bf(jnp.where(causal, _dot(q_m[j], k_m[j], 1, 1), 0.0)) for j in range(NB)]
            intra = [_dot(a[j], vs[j], 1, 0) for j in range(NB)]
            upd = [_dot(vs[j], k_e[j], 0, 0) for j in range(NB)]
            for j in range(NB):
                st_ref[gi * NB + j] = st
                o_ref[rows[j], :] = intra[j] + _dot(q_e[j], _bf(st), 1, 1)
                st = st * ds[j]["state"] + upd[j]
            return st

        lax.fori_loop(0, NGRP, group, jnp.zeros((HEAD, HEAD), jnp.float32))
        o = o_ref[...]
        rn = o * lax.rsqrt(jnp.mean(o * o, axis=-1, keepdims=True) + EPS)
        gate = gate_ref[...]
        y_ref[...] = _bf(rn * rg_ref[...] * (gate * _sigmoid(gate)))

    return pl.pallas_call(
        body, name="hgrn_fwd", grid=(NHEAD,),
        in_specs=[*_HEAD_SPECS,
                  pl.BlockSpec((2, HEAD), lambda h: (0, h)),
                  pl.BlockSpec((1, HEAD), lambda h: (0, h)),
                  pl.BlockSpec(memory_space=pl.ANY)],
        out_specs=(pl.BlockSpec((T, HEAD), lambda h: (0, NHEAD + h)),
                   pl.BlockSpec((T, HEAD), lambda h: (0, h)),
                   pl.BlockSpec((None, NCHUNK, HEAD, HEAD), lambda h: (h, 0, 0, 0))),
        out_shape=(pltpu.HBM((T, DMIX), jnp.bfloat16), pltpu.HBM((T, D), jnp.float32),
                   pltpu.HBM((NHEAD, NCHUNK, HEAD, HEAD), jnp.float32)),
        input_output_aliases={6: 0},
        compiler_params=_params(("parallel",)),
    )(proj, proj, proj, proj, lb_logits, rec_g, y_in)


def _out_proj_loss(x, y, w_out, target, gf):
    rows = 512
    parts = [slice(k * rows // 2, (k + 1) * rows // 2) for k in range(2)]

    def body(x_ref, y_ref, w_ref, t_ref, g_ref, dz_ref, dzb_ref, sq_ref, dg_ref):
        zs = [x_ref[p, :] + _dot(y_ref[p, :], w_ref[...], 1, 0) for p in parts]
        sq = dg = 0.0
        for p, z in zip(parts, zs):
            r = lax.rsqrt(jnp.mean(z * z, axis=-1, keepdims=True) + EPS)
            zhat = z * r
            err = zhat * g_ref[...] - t_ref[p, :]
            dy = err * (1.0 / D)
            gdy = dy * g_ref[...]
            dz = r * (gdy - zhat * jnp.mean(zhat * gdy, axis=-1, keepdims=True))
            dz_ref[p, :] = dz
            dzb_ref[p, :] = _bf(dz)
            sq = sq + jnp.sum(err * err, axis=0, keepdims=True)
            dg = dg + jnp.sum(zhat * dy, axis=0, keepdims=True)

        @pl.when(pl.program_id(0) == 0)
        def _():
            sq_ref[...] = sq
            dg_ref[...] = dg

        @pl.when(pl.program_id(0) != 0)
        def _():
            sq_ref[...] += sq
            dg_ref[...] += dg

    tile = pl.BlockSpec((rows, D), lambda i: (i, 0))
    vec = pl.BlockSpec((1, D), lambda i: (0, 0))
    return pl.pallas_call(
        body, name="out_proj_loss", grid=(T // rows,),
        in_specs=[tile, pl.BlockSpec((rows, DMIX), lambda i: (i, 0)), pl.BlockSpec((DMIX, D), lambda i: (0, 0)),
                  tile, vec],
        out_specs=(tile, tile, vec, vec),
        out_shape=(pltpu.HBM((T, D), jnp.float32), pltpu.HBM((T, D), jnp.bfloat16),
                   jax.ShapeDtypeStruct((1, D), jnp.float32), jax.ShapeDtypeStruct((1, D), jnp.float32)),
        compiler_params=_params(("arbitrary",)),
    )(x, y, w_out, target, gf)


def _out_proj_bwd(dzb, w_out, y):
    tn = 512

    def body(dz_ref, w_ref, y_ref, dy_ref, gw_ref, gwb_ref):
        dz = dz_ref[...]
        dy_ref[...] = _dot(dz, w_ref[...], 1, 1)
        gw = _dot(y_ref[...], dz, 0, 0)
        gw_ref[...] = gw
        gwb_ref[...] = _bf(gw)

    return pl.pallas_call(
        body, name="out_proj_bwd", grid=(DMIX // tn,),
        in_specs=[pl.BlockSpec((T, D), lambda n: (0, 0)), pl.BlockSpec((tn, D), lambda n: (n, 0)),
                  pl.BlockSpec((T, tn), lambda n: (0, n))],
        out_specs=(pl.BlockSpec((T, tn), lambda n: (0, n)), pl.BlockSpec((tn, D), lambda n: (n, 0)),
                   pl.BlockSpec((tn, D), lambda n: (n, 0))),
        out_shape=(pltpu.HBM((T, DMIX), jnp.float32), pltpu.HBM((DMIX, D), jnp.float32),
                   pltpu.HBM((DMIX, D), jnp.bfloat16)),
        compiler_params=_params(("parallel",)),
    )(dzb, w_out, y)


def _hgrn_bwd(proj, lb_logits, rec_g, o, states, dymix, dproj_in, token):
    def body(q_ref, f_ref, i_ref, gate_ref, lb_ref, rg_ref, o_ref, st_ref, dy_ref, dp_any, token_any,
             dp_ref, drg_ref, dlb_ref, do_ref):
        del dp_any, token_any
        lb = _lower_bound(lb_ref)
        causal = _tri(True)
        smat, rmat = _sum_rows_matrix(), _rev_sum_matrix()

        o = o_ref[...]
        rs = lax.rsqrt(jnp.mean(o * o, axis=-1, keepdims=True) + EPS)
        rn = o * rs
        gate = gate_ref[...]
        sgate = _sigmoid(gate)
        dyv = dy_ref[...]
        d_r = dyv * (gate * sgate)
        dp_ref[3] = _bf(dyv * (rn * rg_ref[...]) * (sgate * (1.0 + gate * (1.0 - sgate))))
        drg_ref[...] = jnp.sum(d_r * rn, axis=0, keepdims=True)
        drn = d_r * rg_ref[...]
        do_ref[...] = rs * (drn - rn * jnp.mean(rn * drn, axis=-1, keepdims=True))

        def group(i, carry):
            dst, dlb = carry
            gi = NGRP - 1 - i
            rows = _group_rows(gi)
            span = range(NB)
            qvs = [q_ref[r, :] for r in rows]
            ts = [_gates(qv, f_ref[r, :], lb) for qv, r in zip(qvs, rows)]
            ds = [_decays(s) for s in _exact_sums(smat, [t["g"] for t in ts])]
            vs = [_bf(i_ref[r, :]) for r in rows]
            dos = [_bf(do_ref[r, :]) for r in rows]
            sts = [st_ref[gi * NB + j] for j in span]
            qe_f = [t["qs"] * d["e_q"] for t, d in zip(ts, ds)]
            ke_f = [t["kk"] * d["e_k"] for t, d in zip(ts, ds)]
            q_e, k_e = [_bf(a) for a in qe_f], [_bf(a) for a in ke_f]
            q_m = [_bf(t["qs"] * d["e_qm"]) for t, d in zip(ts, ds)]
            k_m = [_bf(t["kk"] * d["e_km"]) for t, d in zip(ts, ds)]
            a = [_bf(jnp.where(causal, _dot(q_m[j], k_m[j], 1, 1), 0.0)) for j in span]
            da = [_bf(jnp.where(causal, _dot(dos[j], vs[j], 1, 1), 0.0)) for j in span]
            dqm = [_dot(da[j], k_m[j], 1, 0) for j in span]
            dkm = [_dot(da[j], q_m[j], 0, 0) for j in span]
            dv_in = [_dot(a[j], dos[j], 0, 0) for j in span]
            dqe = [_dot(dos[j], _bf(sts[j]), 1, 0) for j in span]
            grow = [_dot(dos[j], q_e[j], 0, 0) for j in span]
            dke, carried = [None] * NB, [None] * NB
            for j in reversed(span):
                dst_b = _bf(dst)
                dke[j] = _dot(vs[j], dst_b, 1, 0)
                dp_ref[2, rows[j], :] = _bf(dv_in[j] + _dot(k_e[j], dst_b, 1, 1))
                carried[j] = ds[j]["total8"] * jnp.sum(dst * sts[j], axis=0, keepdims=True)
                dst = dst * ds[j]["state"] + grow[j]
            kdk = [ke_f[j] * dke[j] for j in span]
            pos = [(q_m[j].astype(jnp.float32) * dqm[j] - k_m[j].astype(jnp.float32) * dkm[j]) + qe_f[j] * dqe[j]
                   for j in span]
            dgs = _exact_sums(rmat, [jnp.concatenate([pos[j], kdk[j]], axis=0) for j in span])
            for j in span:
                t, d = ts[j], ds[j]
                dg = dgs[j] + jnp.tile(carried[j], (CHUNK // 8, 1))
                dqs = dqm[j] * d["e_qm"] + dqe[j] * d["e_q"]
                dkk = dkm[j] * d["e_km"] + dke[j] * d["e_k"]
                df = dg / t["f"] - dkk
                dp_ref[1, rows[j], :] = _bf(df * (1.0 - lb) * (t["sg"] * (1.0 - t["sg"])))
                dp_ref[0, rows[j], :] = _bf(dqs * (t["sq"] * (1.0 + qvs[j] * (1.0 - t["sq"]))))
                dlb = dlb + df * (1.0 - t["sg"])
            return dst, dlb

        _, dlb = lax.fori_loop(0, NGRP, group, (jnp.zeros((HEAD, HEAD), jnp.float32),
                                                jnp.zeros((CHUNK, HEAD), jnp.float32)))
        dlb_ref[...] = jnp.sum(dlb, axis=0, keepdims=True)

    vec = pl.BlockSpec((1, HEAD), lambda h: (0, h))
    return pl.pallas_call(
        body, name="hgrn_bwd", grid=(NHEAD,),
        in_specs=[*_HEAD_SPECS,
                  pl.BlockSpec((2, HEAD), lambda h: (0, h)), vec,
                  pl.BlockSpec((T, HEAD), lambda h: (0, h)),
                  pl.BlockSpec((None, NCHUNK, HEAD, HEAD), lambda h: (h, 0, 0, 0)),
                  pl.BlockSpec((T, HEAD), lambda h: (0, NHEAD + h)), ANY, ANY],
        out_specs=(pl.BlockSpec((4, T, HEAD), lambda h: (0, 0, h)), vec, vec),
        out_shape=(pltpu.HBM((NSEG, T, D), jnp.bfloat16),
                   jax.ShapeDtypeStruct((1, D), jnp.float32), jax.ShapeDtypeStruct((1, D), jnp.float32)),
        scratch_shapes=[pltpu.VMEM((T, HEAD), jnp.float32)],
        input_output_aliases={9: 0},
        compiler_params=_params(("parallel",)),
    )(proj, proj, proj, proj, lb_logits, rec_g, o, states, dymix, dproj_in, token)


def _pool_bwd(proj, pool_w, pool_scale, dymix):
    def body(u_ref, pg_ref, w_ref, sc_ref, dy_ref, dp_ref, gw_ref, gs_ref):
        gidx = pl.program_id(0)
        u, pg = u_ref[...], pg_ref[...]
        d = _bf(_window_mean(_window_sum(u, gidx, False), gidx) - u)
        mixed = _dot(d, w_ref[...], 1, 0)
        spg = _sigmoid(pg)
        dyv = dy_ref[...]
        d_p = dyv * (pg * spg)
        dp_ref[1] = _bf(dyv * (mixed * sc_ref[...]) * (spg * (1.0 + pg * (1.0 - spg))))
        gs_ref[...] = jnp.sum(d_p * mixed, axis=0, keepdims=True)
        dmixed = _bf(d_p * sc_ref[...])
        gw_ref[...] = _dot(d, dmixed, 0, 0)
        dd = _dot(dmixed, w_ref[...], 1, 1)
        dp_ref[0] = _bf(_window_sum(_window_mean(dd, gidx), gidx, True) - dd)

    return pl.pallas_call(
        body, name="pool_bwd", grid=(NGROUP,),
        in_specs=[*_POOL_SPECS,
                  pl.BlockSpec((None, GROUP, GROUP), lambda g: (g, 0, 0)),
                  pl.BlockSpec((1, GROUP), lambda g: (0, g)),
                  pl.BlockSpec((T, GROUP), lambda g: (0, g))],
        out_specs=(pl.BlockSpec((2, T, GROUP), lambda g: (2, 0, g)),
                   pl.BlockSpec((None, GROUP, GROUP), lambda g: (g, 0, 0)),
                   pl.BlockSpec((1, GROUP), lambda g: (0, g))),
        out_shape=(pltpu.HBM((NSEG, T, D), jnp.bfloat16),
                   jax.ShapeDtypeStruct((NGROUP, GROUP, GROUP), jnp.float32),
                   jax.ShapeDtypeStruct((1, D), jnp.float32)),
        compiler_params=_params(("parallel",)),
    )(proj, proj, pool_w, pool_scale, dymix)


def _proj_bwd_w(place, ht, dproj):
    half = NTILE // 2

    def owner_chip(i, pr):
        return jnp.where(i < half, i // 3, (pr[1] + 1 + (i - half) // 3) % 4)

    def tile_of(i, pr):
        side = jnp.where(i < half, 1 - pr[2], pr[2])
        return 6 * owner_chip(i, pr) + 3 * side + i % 3

    def dproj_block(i, pr):
        j = tile_of(i, pr)
        return ((j // 4 + 4) % NSEG, 0, j % 4)

    def mine(i):
        return jnp.maximum(i, half)

    def body(place_ref, h_ref, dp_ref, sum_ref, own_ref, sendbuf, recvbuf, send_sems, recv_sems):
        i = pl.program_id(0)
        px, py, c, _ = _place()
        gw = _dot(h_ref[...], dp_ref[...], 1, 0)

        def to_sibling(slot):
            return pltpu.make_async_remote_copy(
                src_ref=sendbuf.at[slot], dst_ref=recvbuf.at[slot], send_sem=send_sems.at[slot],
                recv_sem=recv_sems.at[slot], device_id=(px, py, 1 - c), device_id_type=MESH)

        @pl.when(i < half)
        def _():
            sendbuf[i] = _bf(gw)
            to_sibling(i).start()

        @pl.when(i >= half)
        def _():
            slot = 3 * owner_chip(i, place_ref) + i % 3
            to_sibling(slot).wait_recv()
            total = gw + recvbuf[slot].astype(jnp.float32)
            sum_ref[...] = _bf(total)
            own_ref[...] = total

        @pl.when(i == NTILE - 1)
        def _():
            for slot in range(half):
                to_sibling(slot).wait_send()

    return pl.pallas_call(
        body, name="proj_bwd_w",
        grid_spec=pltpu.PrefetchScalarGridSpec(
            num_scalar_prefetch=1, grid=(NTILE,),
            in_specs=[pl.BlockSpec((D, T), lambda i, pr: (0, 0)),
                      pl.BlockSpec((None, T, TILE), lambda i, pr: dproj_block(i, pr))],
            out_specs=(pl.BlockSpec((None, None, D, TILE), lambda i, pr: (owner_chip(mine(i), pr), mine(i) % 3, 0, 0)),
                       pl.BlockSpec((None, D, TILE), lambda i, pr: (jnp.where(i < NTILE - 3, 0, i % 3), 0, 0))),
            scratch_shapes=[pltpu.VMEM((half, D, TILE), jnp.bfloat16), pltpu.VMEM((half, D, TILE), jnp.bfloat16),
                            pltpu.SemaphoreType.DMA((half,)), pltpu.SemaphoreType.DMA((half,))]),
        out_shape=(pltpu.HBM((4, 3, D, TILE), jnp.bfloat16), pltpu.HBM((3, D, TILE), jnp.float32)),
        compiler_params=_params(("arbitrary",)),
    )(place, ht, dproj)


def _proj_bwd_x(dproj, w_t, x, g1, dz, token):
    tm = 512
    pairs = NSEG // 2

    def body(dp_ref, w_ref, x_ref, g_ref, dz_ref, token_any, dx_ref, dg_ref, wcat, acc):
        del token_any
        m, s = pl.program_id(0), pl.program_id(1)
        r = None
        for k in range(2):
            for i in range(4):
                wcat[k, :, i * TILE:(i + 1) * TILE] = w_ref[4 * k + i]
            part = _dot(dp_ref[k], wcat[k], 1, 1)
            r = part if r is None else r + part

        @pl.when(s == 0)
        def _():
            acc[...] = r

        @pl.when(s != 0)
        def _():
            acc[...] += r

        @pl.when(s == pairs - 1)
        def _():
            xv = x_ref[...]
            rs = lax.rsqrt(jnp.mean(xv * xv, axis=-1, keepdims=True) + EPS)
            xhat = xv * rs
            dhv = acc[...]
            gdh = dhv * g_ref[...]
            dx_ref[...] = dz_ref[...] + rs * (gdh - xhat * jnp.mean(xhat * gdh, axis=-1, keepdims=True))
            dg = jnp.sum(xhat * dhv, axis=0, keepdims=True)

            @pl.when(m == 0)
            def _():
                dg_ref[...] = dg

            @pl.when(m != 0)
            def _():
                dg_ref[...] += dg

    rows = pl.BlockSpec((tm, D), lambda m, s: (m, 0))
    vec = pl.BlockSpec((1, D), lambda m, s: (0, 0))
    return pl.pallas_call(
        body, name="proj_bwd_x", grid=(T // tm, pairs),
        in_specs=[pl.BlockSpec((2, tm, D), lambda m, s: (s, m, 0)),
                  pl.BlockSpec((8, D, TILE), lambda m, s: ((s + 1) % pairs, 0, 0)), rows, vec, rows, ANY],
        out_specs=(rows, vec),
        out_shape=(jax.ShapeDtypeStruct((T, D), jnp.float32), jax.ShapeDtypeStruct((1, D), jnp.float32)),
        scratch_shapes=[pltpu.VMEM((2, D, D), jnp.bfloat16), pltpu.VMEM((tm, D), jnp.float32)],
        compiler_params=_params(("arbitrary", "arbitrary"), vmem_mib=56),
    )(dproj, w_t, x, g1, dz, token)


def _adamw(w, g, m, v):
    m_new = ADAM_B1 * m + (1.0 - ADAM_B1) * g
    v_new = ADAM_B2 * v + (1.0 - ADAM_B2) * (g * g)
    delta = -ADAM_LR * ((m_new / BC1) / (jnp.sqrt(v_new / BC2) + ADAM_EPS) + ADAM_WD * w)
    return delta, m_new, v_new


def _reduce_adam(name, place, parts, w, m, v, grid, w_spec):
    n = len(parts)

    def body(place_ref, *refs):
        del place_ref
        w_ref, m_ref, v_ref, g_ref, d_ref, mo_ref, vo_ref = refs[n:]
        g = None
        for ref, (_, _, stacked) in zip(refs[:n], parts):
            terms = [ref[r] for r in range(ref.shape[0])] if stacked else [ref[...]]
            for t in terms:
                g = t.astype(jnp.float32) if g is None else g + t.astype(jnp.float32)
        delta, m_new, v_new = _adamw(w_ref[...], g, m_ref[...], v_ref[...])
        g_ref[...] = g
        d_ref[...] = delta
        mo_ref[...] = m_new
        vo_ref[...] = v_new

    shape = jax.ShapeDtypeStruct(w.shape, jnp.float32)
    return pl.pallas_call(
        body, name=name,
        grid_spec=pltpu.PrefetchScalarGridSpec(
            num_scalar_prefetch=1, grid=grid,
            in_specs=[spec for _, spec, _ in parts] + [w_spec] * 3, out_specs=(w_spec,) * 4),
        out_shape=(shape,) * 4,
        compiler_params=_params(("parallel",)),
    )(place, *[_in_hbm(a) for a in [a for a, _, _ in parts] + [w, m, v]])


def _small_adam(parts, w, m, v):
    def body(p_ref, w_ref, m_ref, v_ref, g_ref, d_ref, mo_ref, vo_ref):
        g = p_ref[0]
        for s in range(1, NDEV):
            g = g + p_ref[s]
        wv = w_ref[...]
        rows = _row_ids(wv.shape)
        other = jnp.where(rows == 2, pltpu.roll(wv, 7, 0), jnp.where(rows == 3, pltpu.roll(wv, 1, 0), 0.0))
        lbv = _sigmoid(wv - other)
        sign = jnp.where(rows == 2, 1.0, -1.0)
        g = jnp.where((rows == 2) | (rows == 3), sign * g * lbv * (1.0 - lbv), g)
        delta, m_new, v_new = _adamw(wv, g, m_ref[...], v_ref[...])
        g_ref[...] = g
        d_ref[...] = delta
        mo_ref[...] = m_new
        vo_ref[...] = v_new

    shape = jax.ShapeDtypeStruct((8, D), jnp.float32)
    return pl.pallas_call(body, name="small_adam", out_shape=(shape,) * 4)(parts, w, m, v)


def _rows8(*vecs):
    rows = [a.reshape(-1, D) for a in vecs]
    n = sum(r.shape[0] for r in rows)
    return jnp.concatenate(rows + [jnp.zeros((8 - n, D), jnp.float32)], axis=0)


def kernel(x, norm1_g, w_in, pool_w, pool_scale, lb_logits, rec_norm_g, w_out, final_norm_g, loss_target, m_norm1_g, m_w_in, m_pool_w, m_pool_scale, m_lb_logits, m_rec_norm_g, m_w_out, m_final_norm_g, v_norm1_g, v_w_in, v_pool_w, v_pool_scale, v_lb_logits, v_rec_norm_g, v_w_out, v_final_norm_g):
    xs = x[0]
    target = loss_target[0]
    ix, iy, ic = lax.axis_index("x"), lax.axis_index("y"), lax.axis_index("c")
    place = jnp.stack([4 * ix + 2 * iy + ic, 2 * ix + iy, ic]).astype(jnp.int32)
    gf = final_norm_g.reshape(1, D)

    ht, w_t, w_out_b, w_out_g, pool_g, proj = _gather_proj(xs, norm1_g, w_in, w_out, pool_w)
    pool_full = pool_g.transpose(1, 0, 2, 3).reshape(NGROUP, GROUP, GROUP)
    wout = [w_out_b, w_out_g]
    wout_send, wout_recv, wout, wout_token = _split_start("gather_wout_start", wout, NDEV - 1, _plan_wout)

    y = _pool_fwd(proj, pool_full, pool_scale, wout_token)
    y, o, states = _hgrn_fwd(proj, lb_logits, rec_norm_g, y)
    _, w_out_g = _split_wait("gather_wout_wait", wout, wout_send, wout_recv, _plan_wout, o)
    w_out_full = _in_hbm(w_out_g.reshape(DMIX, D))
    dz, dzb, sq, dgf = _out_proj_loss(xs, y, w_out_full, target, gf)

    dymix, gwout_f, gwout_b = _out_proj_bwd(dzb, w_out_full, y)
    dproj, gpool, dscale = _pool_bwd(proj, pool_full, pool_scale, dymix)

    blk_out = (NDEV, DMIX // NDEV, D)
    blk_pool = (NDEV, NGROUP, GROUP // NDEV, GROUP)
    gpool_s = gpool.reshape(NGROUP, NDEV, GROUP // NDEV, GROUP).transpose(1, 0, 2, 3)
    rest = [gwout_b.reshape(blk_out), gpool_s,
            lax.empty((NDEV - 1,) + blk_out[1:], jnp.bfloat16), lax.empty((NDEV - 1,) + blk_pool[1:], jnp.float32)]
    rest_send, rest_recv, rest, rest_token = _split_start("scatter_rest_start", rest, 2 * (NDEV - 1), _plan_rest)

    dproj, drecg, dlb = _hgrn_bwd(proj, lb_logits, rec_norm_g, o, states, dymix, dproj, rest_token)
    chip_sums, own_sum = _proj_bwd_w(place, ht, dproj)
    win = [chip_sums, lax.empty((3, 3, D, TILE), jnp.bfloat16)]
    win_send, win_recv, win, win_token = _split_start("scatter_win_start", win, 3, _plan_in)

    grad_x, dg1 = _proj_bwd_x(dproj, w_t, xs, norm1_g, dz, win_token)

    _, gpool_own, r_out, r_pool = _split_wait("scatter_rest_wait", rest, rest_send, rest_recv, _plan_rest, grad_x)
    g_wout, d_wout, m_wout, v_wout = _reduce_adam(
        "adam_w_out", place,
        [(gwout_f.reshape(blk_out), pl.BlockSpec((None,) + blk_out[1:], lambda i, pr: (pr[0], 0, 0)), False),
         (r_out, pl.BlockSpec((NDEV - 1,) + blk_out[1:], lambda i, pr: (0, 0, 0)), True)],
        w_out, m_w_out, v_w_out, (1,), pl.BlockSpec((None,) + blk_out[1:], lambda i, pr: (0, 0, 0)))
    g_pool, d_pool, m_pool, v_pool = _reduce_adam(
        "adam_pool_w", place,
        [(gpool_own, pl.BlockSpec((None,) + blk_pool[1:], lambda i, pr: (pr[0], 0, 0, 0)), False),
         (r_pool, pl.BlockSpec((NDEV - 1,) + blk_pool[1:], lambda i, pr: (0, 0, 0, 0)), True)],
        pool_w, m_pool_w, v_pool_w, (1,), pl.BlockSpec((None,) + blk_pool[1:], lambda i, pr: (0, 0, 0, 0)))

    r_small = _gather_small(_rows8(dg1, dscale, dlb, dlb, drecg, dgf, sq), d_wout, d_pool)
    g_s, d_s, m_s, v_s = _small_adam(
        r_small,
        _rows8(norm1_g, pool_scale, lb_logits, rec_norm_g, final_norm_g),
        _rows8(m_norm1_g, m_pool_scale, m_lb_logits, m_rec_norm_g, m_final_norm_g),
        _rows8(v_norm1_g, v_pool_scale, v_lb_logits, v_rec_norm_g, v_final_norm_g))
    loss = jnp.sum(g_s[6]) * (0.5 / D)

    _, r_in = _split_wait("scatter_win_wait", win, win_send, win_recv, _plan_in, v_s)
    g_win, d_win, m_win, v_win = _reduce_adam(
        "adam_w_in", place,
        [(own_sum, pl.BlockSpec((None, D // 2, TILE), lambda i, pr: (i // 2, i % 2, 0)), False),
         (r_in, pl.BlockSpec((3, None, D // 2, TILE), lambda i, pr: (0, i // 2, i % 2, 0)), True)],
        w_in, m_w_in, v_w_in, (6,), pl.BlockSpec((None, D // 2, TILE), lambda i, pr: (0, i % 2, i // 2)))

    def small_outs(a):
        return a[0:1], a[1:2], a[2:4], a[4:5], a[5]

    def outs(small_a, win, pool, wout):
        n1, ps, lbl, rg, fg = small_outs(small_a)
        return n1, win, pool, ps, lbl, rg, wout, fg

    return (loss, grad_x[None],
            *outs(g_s, g_win, g_pool, g_wout), *outs(d_s, d_win, d_pool, d_wout),
            *outs(m_s, m_win, m_pool, m_wout), *outs(v_s, v_win, v_pool, v_wout))
```

```python
import functools

import jax
import jax.numpy as jnp
from jax import lax
from jax.experimental import pallas as pl
from jax.experimental.pallas import tpu as pltpu

T = 2048
D = 1024
NSEG = 6
NTILE = 24
TILE = 256
DMIX = 2048
NDEV = 8
HEAD = 128
NHEAD = 8
CHUNK = 64
NCHUNK = T // CHUNK
NB = 32
NGRP = NCHUNK // NB
NGROUP = 4
GROUP = 256
EPS = 1e-6
EXP_CAP = 80.0
MESH = pl.DeviceIdType.MESH
AXES = ("x", "y", "c")
ANY = pl.BlockSpec(memory_space=pl.ANY)
HBM = pl.BlockSpec(memory_space=pltpu.HBM)
SEM = pl.BlockSpec(memory_space=pltpu.SEMAPHORE)
EFFECT = pltpu.SideEffectType.DATAFLOW_SIDE_EFFECTING

ADAM_LR = 0.001
ADAM_B1 = 0.9
ADAM_B2 = 0.999
ADAM_EPS = 1e-08
ADAM_WD = 0.01
ADAM_STEP = 10
BC1 = 1.0 - ADAM_B1 ** ADAM_STEP
BC2 = 1.0 - ADAM_B2 ** ADAM_STEP

MIB = 1 << 20


def _params(sem=None, vmem_mib=48):
    return pltpu.CompilerParams(dimension_semantics=sem, vmem_limit_bytes=vmem_mib * MIB)


def _sigmoid(v):
    return 1.0 / (1.0 + jnp.exp(-v))


def _dot(a, b, ca, cb, precision=None):
    return lax.dot_general(a, b, (((ca,), (cb,)), ((), ())), precision=precision,
                           preferred_element_type=jnp.float32)


def _bf(v):
    return v.astype(jnp.bfloat16)


def _in_hbm(a):
    return pltpu.with_memory_space_constraint(a, pltpu.HBM)


def _place():
    x, y, c = lax.axis_index("x"), lax.axis_index("y"), lax.axis_index("c")
    return x, y, c, 4 * x + 2 * y + c


def _peer(x, y, c, r):
    return (x ^ ((r >> 2) & 1), y ^ ((r >> 1) & 1), c ^ (r & 1))


def _gather_proj(x, g1, w_in, w_out, pool_w):
    def body(x_ref, g_ref, win_ref, wout_ref, pool_ref, ht_o, wt_o, woutb_o, wout_o, pool_o, proj_o,
             xbuf, hv, htv, wv, wob, pb, stage, send_sems, recv_sems, loc_sems, out_sems):
        px, py, c, my_idx = _place()
        fetch_x = pltpu.make_async_copy(x_ref, xbuf, loc_sems.at[5])
        fetch_x.start()
        me, sibling = (px, py, c), (px, py, 1 - c)
        chips = [(1 - px, py), (px, 1 - py), (1 - px, 1 - py)]
        for p in range(3):
            wv[3 * my_idx + p] = _bf(win_ref[0, :, p * TILE:(p + 1) * TILE])

        def index(bx, by, bc):
            return 4 * bx + 2 * by + bc

        def slot(w, block):
            return wv.at[pl.ds(3 * index(*block), 3)] if w == 0 else pool_o.at[index(*block)]

        def copy(k, w, block, to, src=None):
            return pltpu.make_async_remote_copy(
                src_ref=slot(w, block) if src is None else src, dst_ref=slot(w, block),
                send_sem=send_sems.at[2 * k + w], recv_sem=recv_sems.at[2 * k + w],
                device_id=to, device_id_type=MESH)

        def tile_copy(k, p, block, to):
            at = wv.at[pl.ds(3 * index(*block) + p, 1)]
            sem = 2 * k if p == 0 else 14 + (k // 6) * 2 + p - 1
            return pltpu.make_async_remote_copy(
                src_ref=at, dst_ref=at, send_sem=send_sems.at[sem], recv_sem=recv_sems.at[sem],
                device_id=to, device_id_type=MESH)

        def save(block):
            at = pl.ds(3 * index(*block), 3)
            pltpu.make_async_copy(wv.at[at], wt_o.at[at], loc_sems.at[4]).start()

        srcs = (slot(0, me), pb)
        first = []
        for w in (0, 1):
            if w == 1:
                pb[...] = _bf(pool_ref[0])
                wob[...] = _bf(wout_ref[0])
            group = [copy(1 + j, w, me, (*chip, c), src=srcs[w]) for j, chip in enumerate(chips[:2])]
            group.append(copy(0, w, me, sibling, src=srcs[w]))
            for cp in group:
                cp.start()
            first += group
        save(me)
        locs = [pltpu.make_async_copy(pb, slot(1, me), loc_sems.at[0]),
                pltpu.make_async_copy(wob, wout_o.at[my_idx], loc_sems.at[1]),
                pltpu.make_async_copy(wob, woutb_o, loc_sems.at[2])]
        for cp in locs:
            cp.start()

        fetch_x.wait()
        xv = xbuf[...]
        hv[...] = _bf(xv * lax.rsqrt(jnp.mean(xv * xv, axis=-1, keepdims=True) + EPS) * g_ref[...])
        rows = 256
        for r0 in range(0, T, rows):
            htv[:, r0:r0 + rows] = hv[r0:r0 + rows, :].T
        locs.append(pltpu.make_async_copy(htv, ht_o, loc_sems.at[3]))
        locs[-1].start()

        def out_copy(p, j):
            return pltpu.make_async_copy(stage.at[p], proj_o.at[j], out_sems.at[p])

        def project_tile(nth, block, p):
            j = 3 * index(*block) + p
            if nth > 0:
                out_copy(p, j).wait()
            stage[p] = _dot(hv[...], wv[j], 1, 0)
            out_copy(p, j).start()

        def project(nth, block):
            def tile(p, carry):
                project_tile(nth, block, p)
                return carry

            lax.fori_loop(0, 3, tile, 0)

        project(0, me)
        copy(0, 0, sibling, me).wait_recv()
        save(sibling)
        project(1, sibling)
        passed = []
        relay_from = (px ^ (1 - c), py ^ c, c)
        relay_to = (px ^ c, py ^ (1 - c), c)

        def arrived(w, j):
            copy(1 + j, w, (*chips[j], c), me).wait_recv()
            passed.append(copy(4 + j, w, (*chips[j], c), sibling))
            passed[-1].start()

        def relay(w):
            if w == 0:
                passed.extend(tile_copy(3, p, relay_from, relay_to) for p in range(3))
            else:
                passed.append(copy(3, w, relay_from, relay_to))
            for cp in passed[-3 if w == 0 else -1:]:
                cp.start()

        def handed(nth, j):
            copy(4 + j, 0, (*chips[j], 1 - c), me).wait_recv()
            save((*chips[j], 1 - c))
            project(nth, (*chips[j], 1 - c))

        arrived(0, 0)
        arrived(0, 1)
        relay(0)
        for j in range(2):
            save((*chips[j], c))
            project(2 + j, (*chips[j], c))
        handed(4, 0)
        handed(5, 1)
        arrived(1, 0)
        arrived(1, 1)
        relay(1)
        far, far_sibling = (*chips[2], c), (*chips[2], 1 - c)
        for p in range(3):
            tile_copy(3, p, far, me).wait_recv()
            passed.append(tile_copy(6, p, far, sibling))
            passed[-1].start()
            project_tile(6, far, p)
        save(far)
        for p in range(3):
            tile_copy(6, p, far_sibling, me).wait_recv()
            project_tile(7, far_sibling, p)
        save(far_sibling)
        arrived(1, 2)
        copy(0, 1, sibling, me).wait_recv()
        for j, chip in enumerate(chips):
            copy(4 + j, 1, (*chip, 1 - c), me).wait_recv()
        keep = pltpu.make_async_copy(wv, wt_o, loc_sems.at[4])
        for p in range(3):
            out_copy(p, p).wait()
        for cp in first + passed:
            cp.wait_send()
        keep.wait()
        for cp in locs:
            cp.wait()

    vmem = pl.BlockSpec(memory_space=pltpu.VMEM)
    bf16 = jnp.bfloat16
    return pl.pallas_call(
        body, name="gather_proj",
        out_shape=(pltpu.HBM((D, T), bf16), pltpu.HBM((NTILE, D, TILE), bf16),
                   pltpu.HBM((DMIX // NDEV, D), bf16), pltpu.HBM((NDEV, DMIX // NDEV, D), bf16),
                   pltpu.HBM((NDEV, NGROUP, GROUP // NDEV, GROUP), bf16), pltpu.HBM((NTILE, T, TILE), jnp.float32)),
        in_specs=[ANY] + [vmem] * 4, out_specs=(ANY,) * 6,
        scratch_shapes=[pltpu.VMEM((T, D), jnp.float32),
                        pltpu.VMEM((T, D), bf16), pltpu.VMEM((D, T), bf16), pltpu.VMEM((NTILE, D, TILE), bf16),
                        pltpu.VMEM((DMIX // NDEV, D), bf16), pltpu.VMEM((NGROUP, GROUP // NDEV, GROUP), bf16),
                        pltpu.VMEM((3, T, TILE), jnp.float32),
                        pltpu.SemaphoreType.DMA((18,)), pltpu.SemaphoreType.DMA((18,)),
                        pltpu.SemaphoreType.DMA((6,)), pltpu.SemaphoreType.DMA((3,))],
        compiler_params=_params(vmem_mib=56),
    )(x, g1, w_in, w_out, pool_w)


def _split_start(name, arrays, n_copies, plan):
    k = len(arrays)

    def body(*refs):
        send_sems, recv_sems, token = refs[k], refs[k + 1], refs[-1]
        for i, (src, dst, to) in enumerate(plan(refs[:k])):
            pltpu.make_async_remote_copy(src_ref=src, dst_ref=dst, send_sem=send_sems.at[i],
                                         recv_sem=recv_sems.at[i], device_id=to, device_id_type=MESH).start()
        token[...] = jnp.zeros_like(token)

    out = pl.pallas_call(
        body, name=name,
        out_shape=(pltpu.SemaphoreType.DMA((n_copies,)), pltpu.SemaphoreType.DMA((n_copies,)),
                   *[pltpu.HBM(a.shape, a.dtype) for a in arrays], jax.ShapeDtypeStruct((8, 128), jnp.float32)),
        in_specs=[HBM] * k, out_specs=(SEM, SEM, *[HBM] * k, pl.BlockSpec(memory_space=pltpu.VMEM)),
        input_output_aliases={i: 2 + i for i in range(k)},
        compiler_params=pltpu.CompilerParams(has_side_effects=EFFECT),
    )(*[pltpu.with_memory_space_constraint(a, pltpu.HBM) for a in arrays])
    return out[0], out[1], out[2:2 + k], out[-1]


def _split_wait(name, arrays, send_sems, recv_sems, plan, after):
    k = len(arrays)

    def body(*refs):
        sends, recvs = refs[k], refs[k + 1]
        for i, (src, dst, to) in enumerate(plan(refs[:k])):
            cp = pltpu.make_async_remote_copy(src_ref=src, dst_ref=dst, send_sem=sends.at[i], recv_sem=recvs.at[i],
                                              device_id=to, device_id_type=MESH)
            cp.wait_send()
            cp.wait_recv()

    return pl.pallas_call(
        body, name=name,
        out_shape=tuple(pltpu.HBM(a.shape, a.dtype) for a in arrays),
        in_specs=[HBM] * k + [SEM, SEM, ANY], out_specs=(HBM,) * k,
        input_output_aliases={i: i for i in range(k)},
        compiler_params=pltpu.CompilerParams(has_side_effects=EFFECT),
    )(*arrays, send_sems, recv_sems, after)


def _plan_wout(refs):
    src, land = refs
    x, y, c, me = _place()
    return [(src, land.at[me], _peer(x, y, c, r)) for r in range(1, NDEV)]


def _plan_rest(refs):
    gob, gpf, r_out, r_pool = refs
    x, y, c, me = _place()
    plan = []
    for r in range(1, NDEV):
        plan.append((gob.at[me ^ r], r_out.at[r - 1], _peer(x, y, c, r)))
        plan.append((gpf.at[me ^ r], r_pool.at[r - 1], _peer(x, y, c, r)))
    return plan


def _plan_in(refs):
    sum_b, r_in = refs
    x, y, c, _ = _place()
    plan = []
    for j, (dx, dy) in enumerate(((1, 0), (0, 1), (1, 1))):
        px, py = x ^ dx, y ^ dy
        plan.append((sum_b.at[2 * px + py], r_in.at[j], (px, py, c)))
    return plan


def _gather_small(small, *after):
    def body(sm, *refs):
        r_small, send_sems, recv_sems, loc_sem = refs[len(after):]
        x, y, c, me = _place()
        loc = pltpu.make_async_copy(sm, r_small.at[me], loc_sem)
        loc.start()

        def copy(r, src_idx):
            return pltpu.make_async_remote_copy(
                src_ref=sm, dst_ref=r_small.at[src_idx], send_sem=send_sems.at[r - 1], recv_sem=recv_sems.at[r - 1],
                device_id=_peer(x, y, c, r), device_id_type=MESH)

        sends = [copy(r, me) for r in range(1, NDEV)]
        for cp in sends:
            cp.start()
        for r in range(1, NDEV):
            copy(r, me ^ r).wait_recv()
        for cp in sends:
            cp.wait_send()
        loc.wait()

    return pl.pallas_call(
        body, name="gather_small",
        out_shape=jax.ShapeDtypeStruct((NDEV, 8, D), jnp.float32),
        in_specs=[ANY] * (1 + len(after)), out_specs=ANY,
        scratch_shapes=[pltpu.SemaphoreType.DMA((NDEV - 1,)), pltpu.SemaphoreType.DMA((NDEV - 1,)),
                        pltpu.SemaphoreType.DMA],
    )(small, *after)


def _seg_tiles(s):
    return (s + 2) % NSEG


_POOL_SPECS = [pl.BlockSpec((None, T, GROUP), lambda g, base=base: (base + g, 0, 0)) for base in (0, 4)]
_HEAD_SPECS = [pl.BlockSpec((None, T, HEAD), lambda h, base=base: (base + h // 2, 0, h % 2))
               for base in (8, 12, 16, 20)]


def _row_ids(shape):
    return lax.broadcasted_iota(jnp.int32, shape, 0)


BAND_ROWS = 128
HALO = 16


def _window_sum(a, gidx, lead):
    width = lax.shift_left(jnp.int32(2), gidx)
    shape = (BAND_ROWS, BAND_ROWS + HALO)
    t, j = lax.broadcasted_iota(jnp.int32, shape, 0), lax.broadcasted_iota(jnp.int32, shape, 1)
    first = t if lead else t + HALO - width + 1
    band = _bf(jnp.where(j >= first, jnp.where(j < first + width, 1.0, 0.0), 0.0))
    zeros = jnp.zeros((HALO, a.shape[1]), jnp.bfloat16)
    padded = [jnp.concatenate([p, zeros] if lead else [zeros, p], axis=0) for p in _split2(a)]
    out = []
    for r0 in range(0, T, BAND_ROWS):
        slab = jnp.concatenate([p[r0:r0 + BAND_ROWS + HALO] for p in padded], axis=1)
        r = _dot(band, slab, 1, 0)
        out.append(r[:, :a.shape[1]] + r[:, a.shape[1]:])
    return jnp.concatenate(out, axis=0)


def _window_mean(s, gidx):
    inv = jnp.where(gidx == 0, 0.5, jnp.where(gidx == 1, 0.25, jnp.where(gidx == 2, 0.125, 0.0625)))
    width = lax.shift_left(jnp.int32(2), gidx)
    head = s[:16] / jnp.minimum(_row_ids((16, s.shape[1])) + 1, width).astype(jnp.float32)
    return jnp.concatenate([head, s[16:] * inv], axis=0)


def _pool_fwd(proj, pool_w, pool_scale, token):
    def body(u_ref, pg_ref, w_ref, sc_ref, token_any, y_ref):
        del token_any
        gidx = pl.program_id(0)
        u, pg = u_ref[...], pg_ref[...]
        d = _window_mean(_window_sum(u, gidx, False), gidx) - u
        mixed = _dot(_bf(d), w_ref[...], 1, 0)
        y_ref[...] = _bf(mixed * sc_ref[...] * (pg * _sigmoid(pg)))

    return pl.pallas_call(
        body, name="pool_fwd", grid=(NGROUP,),
        in_specs=[*_POOL_SPECS,
                  pl.BlockSpec((None, GROUP, GROUP), lambda g: (g, 0, 0)),
                  pl.BlockSpec((1, GROUP), lambda g: (0, g)), ANY],
        out_specs=pl.BlockSpec((T, GROUP), lambda g: (0, g)),
        out_shape=pltpu.HBM((T, DMIX), jnp.bfloat16),
        compiler_params=_params(("parallel",)),
    )(proj, proj, pool_w, pool_scale, token)


def _tri(lower):
    r = lax.broadcasted_iota(jnp.int32, (CHUNK, CHUNK), 0)
    c = lax.broadcasted_iota(jnp.int32, (CHUNK, CHUNK), 1)
    return (r >= c) if lower else (r <= c)


def _sum_rows_matrix():
    shape = (CHUNK + 16, CHUNK)
    r, c = lax.broadcasted_iota(jnp.int32, shape, 0), lax.broadcasted_iota(jnp.int32, shape, 1)
    run = jnp.where(c <= r, 1.0, 0.0)
    half = jnp.where(c < CHUNK // 2, 1.0, 0.0)
    return _bf(jnp.where(r < CHUNK, run, jnp.where(r < CHUNK + 8, 1.0, half)))


def _rev_sum_matrix():
    shape = (CHUNK, 2 * CHUNK)
    r, c = lax.broadcasted_iota(jnp.int32, shape, 0), lax.broadcasted_iota(jnp.int32, shape, 1)
    return _bf(jnp.where(c < CHUNK, jnp.where(c >= r, 1.0, 0.0), jnp.where(c - CHUNK < r, 1.0, 0.0)))


def _split2(a):
    hi = _bf(a)
    return [hi, _bf(a - hi.astype(jnp.float32))]


def _exact_sums(mat, pieces):
    x = jnp.concatenate([s for p in pieces for s in _split2(p)], axis=1)
    r = _dot(mat, x, 1, 0)
    return [r[:, 2 * j * HEAD:(2 * j + 1) * HEAD] + r[:, (2 * j + 1) * HEAD:(2 * j + 2) * HEAD]
            for j in range(len(pieces))]


def _gates(qv, fl, lb):
    sq = _sigmoid(qv)
    sg = _sigmoid(fl)
    f = lb + (1.0 - lb) * sg
    return dict(sq=sq, qs=qv * sq, sg=sg, f=f, kk=1.0 - f, g=jnp.log(f))


def _decays(sums):
    big_g = sums[:CHUNK]
    total = sums[CHUNK:CHUNK + 8]
    g_last = jnp.tile(total, (CHUNK // 8, 1))
    g_mid = jnp.tile(sums[CHUNK + 8:], (CHUNK // 8, 1))
    return dict(
        e_q=jnp.exp(big_g),
        e_k=jnp.exp(g_last - big_g),
        e_qm=jnp.exp(jnp.minimum(big_g - g_mid, EXP_CAP)),
        e_km=jnp.exp(jnp.minimum(g_mid - big_g, EXP_CAP)),
        total8=jnp.exp(total),
        state=jnp.exp(jnp.tile(total, (HEAD // 8, 1))))


def _group_rows(gi):
    return [pl.ds(pl.multiple_of((gi * NB + j) * CHUNK, CHUNK), CHUNK) for j in range(NB)]


def _lower_bound(lb_ref):
    return _sigmoid(lb_ref[0:1, :] - lb_ref[1:2, :])


def _hgrn_fwd(proj, lb_logits, rec_g, y_in):
    def body(q_ref, f_ref, i_ref, gate_ref, lb_ref, rg_ref, y_any, y_ref, o_ref, st_ref):
        del y_any
        lb = _lower_bound(lb_ref)
        causal = _tri(True)
        smat = _sum_rows_matrix()

        def group(gi, st):
            rows = _group_rows(gi)
            ts = [_gates(q_ref[r, :], f_ref[r, :], lb) for r in rows]
            ds = [_decays(s) for s in _exact_sums(smat, [t["g"] for t in ts])]
            vs = [_bf(i_ref[r, :]) for r in rows]
            q_m = [_bf(t["qs"] * d["e_qm"]) for t, d in zip(ts, ds)]
            k_m = [_bf(t["kk"] * d["e_km"]) for t, d in zip(ts, ds)]
            q_e = [_bf(t["qs"] * d["e_q"]) for t, d in zip(ts, ds)]
            k_e = [_bf(t["kk"] * d["e_k"]) for t, d in zip(ts, ds)]
            a = [_bf(jnp.where(causal, _dot(q_m[j], k_m[j], 1, 1), 0.0)) for j in range(NB)]
            intra = [_dot(a[j], vs[j], 1, 0) for j in range(NB)]
            upd = [_dot(vs[j], k_e[j], 0, 0) for j in range(NB)]
            for j in range(NB):
                st_ref[gi * NB + j] = st
                o_ref[rows[j], :] = intra[j] + _dot(q_e[j], _bf(st), 1, 1)
                st = st * ds[j]["state"] + upd[j]
            return st

        lax.fori_loop(0, NGRP, group, jnp.zeros((HEAD, HEAD), jnp.float32))
        o = o_ref[...]
        rn = o * lax.rsqrt(jnp.mean(o * o, axis=-1, keepdims=True) + EPS)
        gate = gate_ref[...]
        y_ref[...] = _bf(rn * rg_ref[...] * (gate * _sigmoid(gate)))

    return pl.pallas_call(
        body, name="hgrn_fwd", grid=(NHEAD,),
        in_specs=[*_HEAD_SPECS,
                  pl.BlockSpec((2, HEAD), lambda h: (0, h)),
                  pl.BlockSpec((1, HEAD), lambda h: (0, h)),
                  pl.BlockSpec(memory_space=pl.ANY)],
        out_specs=(pl.BlockSpec((T, HEAD), lambda h: (0, NHEAD + h)),
                   pl.BlockSpec((T, HEAD), lambda h: (0, h)),
                   pl.BlockSpec((None, NCHUNK, HEAD, HEAD), lambda h: (h, 0, 0, 0))),
        out_shape=(pltpu.HBM((T, DMIX), jnp.bfloat16), pltpu.HBM((T, D), jnp.float32),
                   pltpu.HBM((NHEAD, NCHUNK, HEAD, HEAD), jnp.float32)),
        input_output_aliases={6: 0},
        compiler_params=_params(("parallel",)),
    )(proj, proj, proj, proj, lb_logits, rec_g, y_in)


def _out_proj_loss(x, y, w_out, target, gf):
    rows = 512
    parts = [slice(k * rows // 2, (k + 1) * rows // 2) for k in range(2)]

    def body(x_ref, y_ref, w_ref, t_ref, g_ref, dz_ref, dzb_ref, sq_ref, dg_ref):
        zs = [x_ref[p, :] + _dot(y_ref[p, :], w_ref[...], 1, 0) for p in parts]
        sq = dg = 0.0
        for p, z in zip(parts, zs):
            r = lax.rsqrt(jnp.mean(z * z, axis=-1, keepdims=True) + EPS)
            zhat = z * r
            err = zhat * g_ref[...] - t_ref[p, :]
            dy = err * (1.0 / D)
            gdy = dy * g_ref[...]
            dz = r * (gdy - zhat * jnp.mean(zhat * gdy, axis=-1, keepdims=True))
            dz_ref[p, :] = dz
            dzb_ref[p, :] = _bf(dz)
            sq = sq + jnp.sum(err * err, axis=0, keepdims=True)
            dg = dg + jnp.sum(zhat * dy, axis=0, keepdims=True)

        @pl.when(pl.program_id(0) == 0)
        def _():
            sq_ref[...] = sq
            dg_ref[...] = dg

        @pl.when(pl.program_id(0) != 0)
        def _():
            sq_ref[...] += sq
            dg_ref[...] += dg

    tile = pl.BlockSpec((rows, D), lambda i: (i, 0))
    vec = pl.BlockSpec((1, D), lambda i: (0, 0))
    return pl.pallas_call(
        body, name="out_proj_loss", grid=(T // rows,),
        in_specs=[tile, pl.BlockSpec((rows, DMIX), lambda i: (i, 0)), pl.BlockSpec((DMIX, D), lambda i: (0, 0)),
                  tile, vec],
        out_specs=(tile, tile, vec, vec),
        out_shape=(pltpu.HBM((T, D), jnp.float32), pltpu.HBM((T, D), jnp.bfloat16),
                   jax.ShapeDtypeStruct((1, D), jnp.float32), jax.ShapeDtypeStruct((1, D), jnp.float32)),
        compiler_params=_params(("arbitrary",)),
    )(x, y, w_out, target, gf)


def _out_proj_bwd(dzb, w_out, y):
    tn = 512

    def body(dz_ref, w_ref, y_ref, dy_ref, gw_ref, gwb_ref):
        dz = dz_ref[...]
        dy_ref[...] = _dot(dz, w_ref[...], 1, 1)
        gw = _dot(y_ref[...], dz, 0, 0)
        gw_ref[...] = gw
        gwb_ref[...] = _bf(gw)

    return pl.pallas_call(
        body, name="out_proj_bwd", grid=(DMIX // tn,),
        in_specs=[pl.BlockSpec((T, D), lambda n: (0, 0)), pl.BlockSpec((tn, D), lambda n: (n, 0)),
                  pl.BlockSpec((T, tn), lambda n: (0, n))],
        out_specs=(pl.BlockSpec((T, tn), lambda n: (0, n)), pl.BlockSpec((tn, D), lambda n: (n, 0)),
                   pl.BlockSpec((tn, D), lambda n: (n, 0))),
        out_shape=(pltpu.HBM((T, DMIX), jnp.float32), pltpu.HBM((DMIX, D), jnp.float32),
                   pltpu.HBM((DMIX, D), jnp.bfloat16)),
        compiler_params=_params(("parallel",)),
    )(dzb, w_out, y)


def _hgrn_bwd(proj, lb_logits, rec_g, o, states, dymix, dproj_in, token):
    def body(q_ref, f_ref, i_ref, gate_ref, lb_ref, rg_ref, o_ref, st_ref, dy_ref, dp_any, token_any,
             dp_ref, drg_ref, dlb_ref, do_ref):
        del dp_any, token_any
        lb = _lower_bound(lb_ref)
        causal = _tri(True)
        smat, rmat = _sum_rows_matrix(), _rev_sum_matrix()

        o = o_ref[...]
        rs = lax.rsqrt(jnp.mean(o * o, axis=-1, keepdims=True) + EPS)
        rn = o * rs
        gate = gate_ref[...]
        sgate = _sigmoid(gate)
        dyv = dy_ref[...]
        d_r = dyv * (gate * sgate)
        dp_ref[3] = _bf(dyv * (rn * rg_ref[...]) * (sgate * (1.0 + gate * (1.0 - sgate))))
        drg_ref[...] = jnp.sum(d_r * rn, axis=0, keepdims=True)
        drn = d_r * rg_ref[...]
        do_ref[...] = rs * (drn - rn * jnp.mean(rn * drn, axis=-1, keepdims=True))

        def group(i, carry):
            dst, dlb = carry
            gi = NGRP - 1 - i
            rows = _group_rows(gi)
            span = range(NB)
            qvs = [q_ref[r, :] for r in rows]
            ts = [_gates(qv, f_ref[r, :], lb) for qv, r in zip(qvs, rows)]
            ds = [_decays(s) for s in _exact_sums(smat, [t["g"] for t in ts])]
            vs = [_bf(i_ref[r, :]) for r in rows]
            dos = [_bf(do_ref[r, :]) for r in rows]
            sts = [st_ref[gi * NB + j] for j in span]
            qe_f = [t["qs"] * d["e_q"] for t, d in zip(ts, ds)]
            ke_f = [t["kk"] * d["e_k"] for t, d in zip(ts, ds)]
            q_e, k_e = [_bf(a) for a in qe_f], [_bf(a) for a in ke_f]
            q_m = [_bf(t["qs"] * d["e_qm"]) for t, d in zip(ts, ds)]
            k_m = [_bf(t["kk"] * d["e_km"]) for t, d in zip(ts, ds)]
            a = [_bf(jnp.where(causal, _dot(q_m[j], k_m[j], 1, 1), 0.0)) for j in span]
            da = [_bf(jnp.where(causal, _dot(dos[j], vs[j], 1, 1), 0.0)) for j in span]
            dqm = [_dot(da[j], k_m[j], 1, 0) for j in span]
            dkm = [_dot(da[j], q_m[j], 0, 0) for j in span]
            dv_in = [_dot(a[j], dos[j], 0, 0) for j in span]
            dqe = [_dot(dos[j], _bf(sts[j]), 1, 0) for j in span]
            grow = [_dot(dos[j], q_e[j], 0, 0) for j in span]
            dke, carried = [None] * NB, [None] * NB
            for j in reversed(span):
                dst_b = _bf(dst)
                dke[j] = _dot(vs[j], dst_b, 1, 0)
                dp_ref[2, rows[j], :] = _bf(dv_in[j] + _dot(k_e[j], dst_b, 1, 1))
                carried[j] = ds[j]["total8"] * jnp.sum(dst * sts[j], axis=0, keepdims=True)
                dst = dst * ds[j]["state"] + grow[j]
            kdk = [ke_f[j] * dke[j] for j in span]
            pos = [(q_m[j].astype(jnp.float32) * dqm[j] - k_m[j].astype(jnp.float32) * dkm[j]) + qe_f[j] * dqe[j]
                   for j in span]
            dgs = _exact_sums(rmat, [jnp.concatenate([pos[j], kdk[j]], axis=0) for j in span])
            for j in span:
                t, d = ts[j], ds[j]
                dg = dgs[j] + jnp.tile(carried[j], (CHUNK // 8, 1))
                dqs = dqm[j] * d["e_qm"] + dqe[j] * d["e_q"]
                dkk = dkm[j] * d["e_km"] + dke[j] * d["e_k"]
                df = dg / t["f"] - dkk
                dp_ref[1, rows[j], :] = _bf(df * (1.0 - lb) * (t["sg"] * (1.0 - t["sg"])))
                dp_ref[0, rows[j], :] = _bf(dqs * (t["sq"] * (1.0 + qvs[j] * (1.0 - t["sq"]))))
                dlb = dlb + df * (1.0 - t["sg"])
            return dst, dlb

        _, dlb = lax.fori_loop(0, NGRP, group, (jnp.zeros((HEAD, HEAD), jnp.float32),
                                                jnp.zeros((CHUNK, HEAD), jnp.float32)))
        dlb_ref[...] = jnp.sum(dlb, axis=0, keepdims=True)

    vec = pl.BlockSpec((1, HEAD), lambda h: (0, h))
    return pl.pallas_call(
        body, name="hgrn_bwd", grid=(NHEAD,),
        in_specs=[*_HEAD_SPECS,
                  pl.BlockSpec((2, HEAD), lambda h: (0, h)), vec,
                  pl.BlockSpec((T, HEAD), lambda h: (0, h)),
                  pl.BlockSpec((None, NCHUNK, HEAD, HEAD), lambda h: (h, 0, 0, 0)),
                  pl.BlockSpec((T, HEAD), lambda h: (0, NHEAD + h)), ANY, ANY],
        out_specs=(pl.BlockSpec((4, T, HEAD), lambda h: (0, 0, h)), vec, vec),
        out_shape=(pltpu.HBM((NSEG, T, D), jnp.bfloat16),
                   jax.ShapeDtypeStruct((1, D), jnp.float32), jax.ShapeDtypeStruct((1, D), jnp.float32)),
        scratch_shapes=[pltpu.VMEM((T, HEAD), jnp.float32)],
        input_output_aliases={9: 0},
        compiler_params=_params(("parallel",)),
    )(proj, proj, proj, proj, lb_logits, rec_g, o, states, dymix, dproj_in, token)


def _pool_bwd(proj, pool_w, pool_scale, dymix):
    def body(u_ref, pg_ref, w_ref, sc_ref, dy_ref, dp_ref, gw_ref, gs_ref):
        gidx = pl.program_id(0)
        u, pg = u_ref[...], pg_ref[...]
        d = _bf(_window_mean(_window_sum(u, gidx, False), gidx) - u)
        mixed = _dot(d, w_ref[...], 1, 0)
        spg = _sigmoid(pg)
        dyv = dy_ref[...]
        d_p = dyv * (pg * spg)
        dp_ref[1] = _bf(dyv * (mixed * sc_ref[...]) * (spg * (1.0 + pg * (1.0 - spg))))
        gs_ref[...] = jnp.sum(d_p * mixed, axis=0, keepdims=True)
        dmixed = _bf(d_p * sc_ref[...])
        gw_ref[...] = _dot(d, dmixed, 0, 0)
        dd = _dot(dmixed, w_ref[...], 1, 1)
        dp_ref[0] = _bf(_window_sum(_window_mean(dd, gidx), gidx, True) - dd)

    return pl.pallas_call(
        body, name="pool_bwd", grid=(NGROUP,),
        in_specs=[*_POOL_SPECS,
                  pl.BlockSpec((None, GROUP, GROUP), lambda g: (g, 0, 0)),
                  pl.BlockSpec((1, GROUP), lambda g: (0, g)),
                  pl.BlockSpec((T, GROUP), lambda g: (0, g))],
        out_specs=(pl.BlockSpec((2, T, GROUP), lambda g: (2, 0, g)),
                   pl.BlockSpec((None, GROUP, GROUP), lambda g: (g, 0, 0)),
                   pl.BlockSpec((1, GROUP), lambda g: (0, g))),
        out_shape=(pltpu.HBM((NSEG, T, D), jnp.bfloat16),
                   jax.ShapeDtypeStruct((NGROUP, GROUP, GROUP), jnp.float32),
                   jax.ShapeDtypeStruct((1, D), jnp.float32)),
        compiler_params=_params(("parallel",)),
    )(proj, proj, pool_w, pool_scale, dymix)


def _proj_bwd_w(place, ht, dproj):
    half = NTILE // 2

    def owner_chip(i, pr):
        return jnp.where(i < half, i // 3, (pr[1] + 1 + (i - half) // 3) % 4)

    def tile_of(i, pr):
        side = jnp.where(i < half, 1 - pr[2], pr[2])
        return 6 * owner_chip(i, pr) + 3 * side + i % 3

    def dproj_block(i, pr):
        j = tile_of(i, pr)
        return ((j // 4 + 4) % NSEG, 0, j % 4)

    def mine(i):
        return jnp.maximum(i, half)

    def body(place_ref, h_ref, dp_ref, sum_ref, own_ref, sendbuf, recvbuf, send_sems, recv_sems):
        i = pl.program_id(0)
        px, py, c, _ = _place()
        gw = _dot(h_ref[...], dp_ref[...], 1, 0)

        def to_sibling(slot):
            return pltpu.make_async_remote_copy(
                src_ref=sendbuf.at[slot], dst_ref=recvbuf.at[slot], send_sem=send_sems.at[slot],
                recv_sem=recv_sems.at[slot], device_id=(px, py, 1 - c), device_id_type=MESH)

        @pl.when(i < half)
        def _():
            sendbuf[i] = _bf(gw)
            to_sibling(i).start()

        @pl.when(i >= half)
        def _():
            slot = 3 * owner_chip(i, place_ref) + i % 3
            to_sibling(slot).wait_recv()
            total = gw + recvbuf[slot].astype(jnp.float32)
            sum_ref[...] = _bf(total)
            own_ref[...] = total

        @pl.when(i == NTILE - 1)
        def _():
            for slot in range(half):
                to_sibling(slot).wait_send()

    return pl.pallas_call(
        body, name="proj_bwd_w",
        grid_spec=pltpu.PrefetchScalarGridSpec(
            num_scalar_prefetch=1, grid=(NTILE,),
            in_specs=[pl.BlockSpec((D, T), lambda i, pr: (0, 0)),
                      pl.BlockSpec((None, T, TILE), lambda i, pr: dproj_block(i, pr))],
            out_specs=(pl.BlockSpec((None, None, D, TILE), lambda i, pr: (owner_chip(mine(i), pr), mine(i) % 3, 0, 0)),
                       pl.BlockSpec((None, D, TILE), lambda i, pr: (jnp.where(i < NTILE - 3, 0, i % 3), 0, 0))),
            scratch_shapes=[pltpu.VMEM((half, D, TILE), jnp.bfloat16), pltpu.VMEM((half, D, TILE), jnp.bfloat16),
                            pltpu.SemaphoreType.DMA((half,)), pltpu.SemaphoreType.DMA((half,))]),
        out_shape=(pltpu.HBM((4, 3, D, TILE), jnp.bfloat16), pltpu.HBM((3, D, TILE), jnp.float32)),
        compiler_params=_params(("arbitrary",)),
    )(place, ht, dproj)


def _proj_bwd_x(dproj, w_t, x, g1, dz, token):
    tm = 512
    pairs = NSEG // 2

    def body(dp_ref, w_ref, x_ref, g_ref, dz_ref, token_any, dx_ref, dg_ref, wcat, acc):
        del token_any
        m, s = pl.program_id(0), pl.program_id(1)
        r = None
        for k in range(2):
            for i in range(4):
                wcat[k, :, i * TILE:(i + 1) * TILE] = w_ref[4 * k + i]
            part = _dot(dp_ref[k], wcat[k], 1, 1)
            r = part if r is None else r + part

        @pl.when(s == 0)
        def _():
            acc[...] = r

        @pl.when(s != 0)
        def _():
            acc[...] += r

        @pl.when(s == pairs - 1)
        def _():
            xv = x_ref[...]
            rs = lax.rsqrt(jnp.mean(xv * xv, axis=-1, keepdims=True) + EPS)
            xhat = xv * rs
            dhv = acc[...]
            gdh = dhv * g_ref[...]
            dx_ref[...] = dz_ref[...] + rs * (gdh - xhat * jnp.mean(xhat * gdh, axis=-1, keepdims=True))
            dg = jnp.sum(xhat * dhv, axis=0, keepdims=True)

            @pl.when(m == 0)
            def _():
                dg_ref[...] = dg

            @pl.when(m != 0)
            def _():
                dg_ref[...] += dg

    rows = pl.BlockSpec((tm, D), lambda m, s: (m, 0))
    vec = pl.BlockSpec((1, D), lambda m, s: (0, 0))
    return pl.pallas_call(
        body, name="proj_bwd_x", grid=(T // tm, pairs),
        in_specs=[pl.BlockSpec((2, tm, D), lambda m, s: (s, m, 0)),
                  pl.BlockSpec((8, D, TILE), lambda m, s: ((s + 1) % pairs, 0, 0)), rows, vec, rows, ANY],
        out_specs=(rows, vec),
        out_shape=(jax.ShapeDtypeStruct((T, D), jnp.float32), jax.ShapeDtypeStruct((1, D), jnp.float32)),
        scratch_shapes=[pltpu.VMEM((2, D, D), jnp.bfloat16), pltpu.VMEM((tm, D), jnp.float32)],
        compiler_params=_params(("arbitrary", "arbitrary"), vmem_mib=56),
    )(dproj, w_t, x, g1, dz, token)


def _adamw(w, g, m, v):
    m_new = ADAM_B1 * m + (1.0 - ADAM_B1) * g
    v_new = ADAM_B2 * v + (1.0 - ADAM_B2) * (g * g)
    delta = -ADAM_LR * ((m_new / BC1) / (jnp.sqrt(v_new / BC2) + ADAM_EPS) + ADAM_WD * w)
    return delta, m_new, v_new


def _reduce_adam(name, place, parts, w, m, v, grid, w_spec):
    n = len(parts)

    def body(place_ref, *refs):
        del place_ref
        w_ref, m_ref, v_ref, g_ref, d_ref, mo_ref, vo_ref = refs[n:]
        g = None
        for ref, (_, _, stacked) in zip(refs[:n], parts):
            terms = [ref[r] for r in range(ref.shape[0])] if stacked else [ref[...]]
            for t in terms:
                g = t.astype(jnp.float32) if g is None else g + t.astype(jnp.float32)
        delta, m_new, v_new = _adamw(w_ref[...], g, m_ref[...], v_ref[...])
        g_ref[...] = g
        d_ref[...] = delta
        mo_ref[...] = m_new
        vo_ref[...] = v_new

    shape = jax.ShapeDtypeStruct(w.shape, jnp.float32)
    return pl.pallas_call(
        body, name=name,
        grid_spec=pltpu.PrefetchScalarGridSpec(
            num_scalar_prefetch=1, grid=grid,
            in_specs=[spec for _, spec, _ in parts] + [w_spec] * 3, out_specs=(w_spec,) * 4),
        out_shape=(shape,) * 4,
        compiler_params=_params(("parallel",)),
    )(place, *[_in_hbm(a) for a in [a for a, _, _ in parts] + [w, m, v]])


def _small_adam(parts, w, m, v):
    def body(p_ref, w_ref, m_ref, v_ref, g_ref, d_ref, mo_ref, vo_ref):
        g = p_ref[0]
        for s in range(1, NDEV):
            g = g + p_ref[s]
        wv = w_ref[...]
        rows = _row_ids(wv.shape)
        other = jnp.where(rows == 2, pltpu.roll(wv, 7, 0), jnp.where(rows == 3, pltpu.roll(wv, 1, 0), 0.0))
        lbv = _sigmoid(wv - other)
        sign = jnp.where(rows == 2, 1.0, -1.0)
        g = jnp.where((rows == 2) | (rows == 3), sign * g * lbv * (1.0 - lbv), g)
        delta, m_new, v_new = _adamw(wv, g, m_ref[...], v_ref[...])
        g_ref[...] = g
        d_ref[...] = delta
        mo_ref[...] = m_new
        vo_ref[...] = v_new

    shape = jax.ShapeDtypeStruct((8, D), jnp.float32)
    return pl.pallas_call(body, name="small_adam", out_shape=(shape,) * 4)(parts, w, m, v)


def _rows8(*vecs):
    rows = [a.reshape(-1, D) for a in vecs]
    n = sum(r.shape[0] for r in rows)
    return jnp.concatenate(rows + [jnp.zeros((8 - n, D), jnp.float32)], axis=0)


def kernel(x, norm1_g, w_in, pool_w, pool_scale, lb_logits, rec_norm_g, w_out, final_norm_g, loss_target, m_norm1_g, m_w_in, m_pool_w, m_pool_scale, m_lb_logits, m_rec_norm_g, m_w_out, m_final_norm_g, v_norm1_g, v_w_in, v_pool_w, v_pool_scale, v_lb_logits, v_rec_norm_g, v_w_out, v_final_norm_g):
    xs = x[0]
    target = loss_target[0]
    ix, iy, ic = lax.axis_index("x"), lax.axis_index("y"), lax.axis_index("c")
    place = jnp.stack([4 * ix + 2 * iy + ic, 2 * ix + iy, ic]).astype(jnp.int32)
    gf = final_norm_g.reshape(1, D)

    ht, w_t, w_out_b, w_out_g, pool_g, proj = _gather_proj(xs, norm1_g, w_in, w_out, pool_w)
    pool_full = pool_g.transpose(1, 0, 2, 3).reshape(NGROUP, GROUP, GROUP)
    wout = [w_out_b, w_out_g]
    wout_send, wout_recv, wout, wout_token = _split_start("gather_wout_start", wout, NDEV - 1, _plan_wout)

    y = _pool_fwd(proj, pool_full, pool_scale, wout_token)
    y, o, states = _hgrn_fwd(proj, lb_logits, rec_norm_g, y)
    _, w_out_g = _split_wait("gather_wout_wait", wout, wout_send, wout_recv, _plan_wout, o)
    w_out_full = _in_hbm(w_out_g.reshape(DMIX, D))
    dz, dzb, sq, dgf = _out_proj_loss(xs, y, w_out_full, target, gf)

    dymix, gwout_f, gwout_b = _out_proj_bwd(dzb, w_out_full, y)
    dproj, gpool, dscale = _pool_bwd(proj, pool_full, pool_scale, dymix)

    blk_out = (NDEV, DMIX // NDEV, D)
    blk_pool = (NDEV, NGROUP, GROUP // NDEV, GROUP)
    gpool_s = gpool.reshape(NGROUP, NDEV, GROUP // NDEV, GROUP).transpose(1, 0, 2, 3)
    rest = [gwout_b.reshape(blk_out), gpool_s,
            lax.empty((NDEV - 1,) + blk_out[1:], jnp.bfloat16), lax.empty((NDEV - 1,) + blk_pool[1:], jnp.float32)]
    rest_send, rest_recv, rest, rest_token = _split_start("scatter_rest_start", rest, 2 * (NDEV - 1), _plan_rest)

    dproj, drecg, dlb = _hgrn_bwd(proj, lb_logits, rec_norm_g, o, states, dymix, dproj, rest_token)
    chip_sums, own_sum = _proj_bwd_w(place, ht, dproj)
    win = [chip_sums, lax.empty((3, 3, D, TILE), jnp.bfloat16)]
    win_send, win_recv, win, win_token = _split_start("scatter_win_start", win, 3, _plan_in)

    grad_x, dg1 = _proj_bwd_x(dproj, w_t, xs, norm1_g, dz, win_token)

    _, gpool_own, r_out, r_pool = _split_wait("scatter_rest_wait", rest, rest_send, rest_recv, _plan_rest, grad_x)
    g_wout, d_wout, m_wout, v_wout = _reduce_adam(
        "adam_w_out", place,
        [(gwout_f.reshape(blk_out), pl.BlockSpec((None,) + blk_out[1:], lambda i, pr: (pr[0], 0, 0)), False),
         (r_out, pl.BlockSpec((NDEV - 1,) + blk_out[1:], lambda i, pr: (0, 0, 0)), True)],
        w_out, m_w_out, v_w_out, (1,), pl.BlockSpec((None,) + blk_out[1:], lambda i, pr: (0, 0, 0)))
    g_pool, d_pool, m_pool, v_pool = _reduce_adam(
        "adam_pool_w", place,
        [(gpool_own, pl.BlockSpec((None,) + blk_pool[1:], lambda i, pr: (pr[0], 0, 0, 0)), False),
         (r_pool, pl.BlockSpec((NDEV - 1,) + blk_pool[1:], lambda i, pr: (0, 0, 0, 0)), True)],
        pool_w, m_pool_w, v_pool_w, (1,), pl.BlockSpec((None,) + blk_pool[1:], lambda i, pr: (0, 0, 0, 0)))

    r_small = _gather_small(_rows8(dg1, dscale, dlb, dlb, drecg, dgf, sq), d_wout, d_pool)
    g_s, d_s, m_s, v_s = _small_adam(
        r_small,
        _rows8(norm1_g, pool_scale, lb_logits, rec_norm_g, final_norm_g),
        _rows8(m_norm1_g, m_pool_scale, m_lb_logits, m_rec_norm_g, m_final_norm_g),
        _rows8(v_norm1_g, v_pool_scale, v_lb_logits, v_rec_norm_g, v_final_norm_g))
    loss = jnp.sum(g_s[6]) * (0.5 / D)

    _, r_in = _split_wait("scatter_win_wait", win, win_send, win_recv, _plan_in, v_s)
    g_win, d_win, m_win, v_win = _reduce_adam(
        "adam_w_in", place,
        [(own_sum, pl.BlockSpec((None, D // 2, TILE), lambda i, pr: (i // 2, i % 2, 0)), False),
         (r_in, pl.BlockSpec((3, None, D // 2, TILE), lambda i, pr: (0, i // 2, i % 2, 0)), True)],
        w_in, m_w_in, v_w_in, (6,), pl.BlockSpec((None, D // 2, TILE), lambda i, pr: (0, i % 2, i // 2)))

    def small_outs(a):
        return a[0:1], a[1:2], a[2:4], a[4:5], a[5]

    def outs(small_a, win, pool, wout):
        n1, ps, lbl, rg, fg = small_outs(small_a)
        return n1, win, pool, ps, lbl, rg, wout, fg

    return (loss, grad_x[None],
            *outs(g_s, g_win, g_pool, g_wout), *outs(d_s, d_win, d_pool, d_wout),
            *outs(m_s, m_win, m_pool, m_wout), *outs(v_s, v_win, v_pool, v_wout))
```

```python
import functools

import jax
import jax.numpy as jnp
from jax import lax
from jax.experimental import pallas as pl
from jax.experimental.pallas import tpu as pltpu

T = 2048
D = 1024
NSEG = 6
NTILE = 24
TILE = 256
DMIX = 2048
NDEV = 8
HEAD = 128
NHEAD = 8
CHUNK = 64
NCHUNK = T // CHUNK
NB = 32
NGRP = NCHUNK // NB
NGROUP = 4
GROUP = 256
EPS = 1e-6
EXP_CAP = 80.0
MESH = pl.DeviceIdType.MESH
AXES = ("x", "y", "c")
ANY = pl.BlockSpec(memory_space=pl.ANY)
HBM = pl.BlockSpec(memory_space=pltpu.HBM)
SEM = pl.BlockSpec(memory_space=pltpu.SEMAPHORE)
EFFECT = pltpu.SideEffectType.DATAFLOW_SIDE_EFFECTING

ADAM_LR = 0.001
ADAM_B1 = 0.9
ADAM_B2 = 0.999
ADAM_EPS = 1e-08
ADAM_WD = 0.01
ADAM_STEP = 10
BC1 = 1.0 - ADAM_B1 ** ADAM_STEP
BC2 = 1.0 - ADAM_B2 ** ADAM_STEP

MIB = 1 << 20


def _params(sem=None, vmem_mib=48):
    return pltpu.CompilerParams(dimension_semantics=sem, vmem_limit_bytes=vmem_mib * MIB)


def _sigmoid(v):
    return 1.0 / (1.0 + jnp.exp(-v))


def _dot(a, b, ca, cb, precision=None):
    return lax.dot_general(a, b, (((ca,), (cb,)), ((), ())), precision=precision,
                           preferred_element_type=jnp.float32)


def _bf(v):
    return v.astype(jnp.bfloat16)


def _in_hbm(a):
    return pltpu.with_memory_space_constraint(a, pltpu.HBM)


def _place():
    x, y, c = lax.axis_index("x"), lax.axis_index("y"), lax.axis_index("c")
    return x, y, c, 4 * x + 2 * y + c


def _peer(x, y, c, r):
    return (x ^ ((r >> 2) & 1), y ^ ((r >> 1) & 1), c ^ (r & 1))


def _gather_proj(x, g1, w_in, w_out, pool_w):
    def body(x_ref, g_ref, win_ref, wout_ref, pool_ref, ht_o, wt_o, woutb_o, wout_o, pool_o, proj_o,
             xbuf, hv, htv, wv, wob, pb, stage, send_sems, recv_sems, loc_sems, out_sems):
        px, py, c, my_idx = _place()
        fetch_x = pltpu.make_async_copy(x_ref, xbuf, loc_sems.at[5])
        fetch_x.start()
        me, sibling = (px, py, c), (px, py, 1 - c)
        chips = [(1 - px, py), (px, 1 - py), (1 - px, 1 - py)]
        for p in range(3):
            wv[3 * my_idx + p] = _bf(win_ref[0, :, p * TILE:(p + 1) * TILE])

        def index(bx, by, bc):
            return 4 * bx + 2 * by + bc

        def slot(w, block):
            return wv.at[pl.ds(3 * index(*block), 3)] if w == 0 else pool_o.at[index(*block)]

        def copy(k, w, block, to, src=None):
            return pltpu.make_async_remote_copy(
                src_ref=slot(w, block) if src is None else src, dst_ref=slot(w, block),
                send_sem=send_sems.at[2 * k + w], recv_sem=recv_sems.at[2 * k + w],
                device_id=to, device_id_type=MESH)

        def save(block):
            at = pl.ds(3 * index(*block), 3)
            pltpu.make_async_copy(wv.at[at], wt_o.at[at], loc_sems.at[4]).start()

        srcs = (slot(0, me), pb)
        first = []
        for w in (0, 1):
            if w == 1:
                pb[...] = _bf(pool_ref[0])
                wob[...] = _bf(wout_ref[0])
            group = [copy(1 + j, w, me, (*chip, c), src=srcs[w]) for j, chip in enumerate(chips[:2])]
            group.append(copy(0, w, me, sibling, src=srcs[w]))
            for cp in group:
                cp.start()
            first += group
        save(me)
        locs = [pltpu.make_async_copy(pb, slot(1, me), loc_sems.at[0]),
                pltpu.make_async_copy(wob, wout_o.at[my_idx], loc_sems.at[1]),
                pltpu.make_async_copy(wob, woutb_o, loc_sems.at[2])]
        for cp in locs:
            cp.start()

        fetch_x.wait()
        xv = xbuf[...]
        hv[...] = _bf(xv * lax.rsqrt(jnp.mean(xv * xv, axis=-1, keepdims=True) + EPS) * g_ref[...])
        rows = 256
        for r0 in range(0, T, rows):
            htv[:, r0:r0 + rows] = hv[r0:r0 + rows, :].T
        locs.append(pltpu.make_async_copy(htv, ht_o, loc_sems.at[3]))
        locs[-1].start()

        def out_copy(p, j):
            return pltpu.make_async_copy(stage.at[p], proj_o.at[j], out_sems.at[p])

        def project(nth, block):
            base = 3 * index(*block)

            def tile(p, carry):
                if nth > 0:
                    out_copy(p, base + p).wait()
                stage[p] = _dot(hv[...], wv[base + p], 1, 0)
                out_copy(p, base + p).start()
                return carry

            lax.fori_loop(0, 3, tile, 0)

        project(0, me)
        copy(0, 0, sibling, me).wait_recv()
        save(sibling)
        project(1, sibling)
        passed = []
        relay_from = (px ^ (1 - c), py ^ c, c)
        relay_to = (px ^ c, py ^ (1 - c), c)

        def arrived(w, j):
            copy(1 + j, w, (*chips[j], c), me).wait_recv()
            passed.append(copy(4 + j, w, (*chips[j], c), sibling))
            passed[-1].start()

        def relay(w):
            passed.append(copy(3, w, relay_from, relay_to))
            passed[-1].start()

        def handed(nth, j):
            copy(4 + j, 0, (*chips[j], 1 - c), me).wait_recv()
            save((*chips[j], 1 - c))
            project(nth, (*chips[j], 1 - c))

        arrived(0, 0)
        arrived(0, 1)
        relay(0)
        for j in range(2):
            save((*chips[j], c))
            project(2 + j, (*chips[j], c))
        handed(4, 0)
        handed(5, 1)
        arrived(1, 0)
        arrived(1, 1)
        relay(1)
        arrived(0, 2)
        save((*chips[2], c))
        project(6, (*chips[2], c))
        handed(7, 2)
        arrived(1, 2)
        copy(0, 1, sibling, me).wait_recv()
        for j, chip in enumerate(chips):
            copy(4 + j, 1, (*chip, 1 - c), me).wait_recv()
        keep = pltpu.make_async_copy(wv, wt_o, loc_sems.at[4])
        for p in range(3):
            out_copy(p, p).wait()
        for cp in first + passed:
            cp.wait_send()
        keep.wait()
        for cp in locs:
            cp.wait()

    vmem = pl.BlockSpec(memory_space=pltpu.VMEM)
    bf16 = jnp.bfloat16
    return pl.pallas_call(
        body, name="gather_proj",
        out_shape=(pltpu.HBM((D, T), bf16), pltpu.HBM((NTILE, D, TILE), bf16),
                   pltpu.HBM((DMIX // NDEV, D), bf16), pltpu.HBM((NDEV, DMIX // NDEV, D), bf16),
                   pltpu.HBM((NDEV, NGROUP, GROUP // NDEV, GROUP), bf16), pltpu.HBM((NTILE, T, TILE), jnp.float32)),
        in_specs=[ANY] + [vmem] * 4, out_specs=(ANY,) * 6,
        scratch_shapes=[pltpu.VMEM((T, D), jnp.float32),
                        pltpu.VMEM((T, D), bf16), pltpu.VMEM((D, T), bf16), pltpu.VMEM((NTILE, D, TILE), bf16),
                        pltpu.VMEM((DMIX // NDEV, D), bf16), pltpu.VMEM((NGROUP, GROUP // NDEV, GROUP), bf16),
                        pltpu.VMEM((3, T, TILE), jnp.float32),
                        pltpu.SemaphoreType.DMA((14,)), pltpu.SemaphoreType.DMA((14,)),
                        pltpu.SemaphoreType.DMA((6,)), pltpu.SemaphoreType.DMA((3,))],
        compiler_params=_params(vmem_mib=56),
    )(x, g1, w_in, w_out, pool_w)


def _split_start(name, arrays, n_copies, plan):
    k = len(arrays)

    def body(*refs):
        send_sems, recv_sems, token = refs[k], refs[k + 1], refs[-1]
        for i, (src, dst, to) in enumerate(plan(refs[:k])):
            pltpu.make_async_remote_copy(src_ref=src, dst_ref=dst, send_sem=send_sems.at[i],
                                         recv_sem=recv_sems.at[i], device_id=to, device_id_type=MESH).start()
        token[...] = jnp.zeros_like(token)

    out = pl.pallas_call(
        body, name=name,
        out_shape=(pltpu.SemaphoreType.DMA((n_copies,)), pltpu.SemaphoreType.DMA((n_copies,)),
                   *[pltpu.HBM(a.shape, a.dtype) for a in arrays], jax.ShapeDtypeStruct((8, 128), jnp.float32)),
        in_specs=[HBM] * k, out_specs=(SEM, SEM, *[HBM] * k, pl.BlockSpec(memory_space=pltpu.VMEM)),
        input_output_aliases={i: 2 + i for i in range(k)},
        compiler_params=pltpu.CompilerParams(has_side_effects=EFFECT),
    )(*[pltpu.with_memory_space_constraint(a, pltpu.HBM) for a in arrays])
    return out[0], out[1], out[2:2 + k], out[-1]


def _split_wait(name, arrays, send_sems, recv_sems, plan, after):
    k = len(arrays)

    def body(*refs):
        sends, recvs = refs[k], refs[k + 1]
        for i, (src, dst, to) in enumerate(plan(refs[:k])):
            cp = pltpu.make_async_remote_copy(src_ref=src, dst_ref=dst, send_sem=sends.at[i], recv_sem=recvs.at[i],
                                              device_id=to, device_id_type=MESH)
            cp.wait_send()
            cp.wait_recv()

    return pl.pallas_call(
        body, name=name,
        out_shape=tuple(pltpu.HBM(a.shape, a.dtype) for a in arrays),
        in_specs=[HBM] * k + [SEM, SEM, ANY], out_specs=(HBM,) * k,
        input_output_aliases={i: i for i in range(k)},
        compiler_params=pltpu.CompilerParams(has_side_effects=EFFECT),
    )(*arrays, send_sems, recv_sems, after)


def _plan_wout(refs):
    src, land = refs
    x, y, c, me = _place()
    return [(src, land.at[me], _peer(x, y, c, r)) for r in range(1, NDEV)]


def _plan_rest(refs):
    gob, gpf, r_out, r_pool = refs
    x, y, c, me = _place()
    plan = []
    for r in range(1, NDEV):
        plan.append((gob.at[me ^ r], r_out.at[r - 1], _peer(x, y, c, r)))
        plan.append((gpf.at[me ^ r], r_pool.at[r - 1], _peer(x, y, c, r)))
    return plan


def _plan_in(refs):
    sum_b, r_in = refs
    x, y, c, _ = _place()
    plan = []
    for j, (dx, dy) in enumerate(((1, 0), (0, 1), (1, 1))):
        px, py = x ^ dx, y ^ dy
        plan.append((sum_b.at[2 * px + py], r_in.at[j], (px, py, c)))
    return plan


def _gather_small(small, *after):
    def body(sm, *refs):
        r_small, send_sems, recv_sems, loc_sem = refs[len(after):]
        x, y, c, me = _place()
        loc = pltpu.make_async_copy(sm, r_small.at[me], loc_sem)
        loc.start()

        def copy(r, src_idx):
            return pltpu.make_async_remote_copy(
                src_ref=sm, dst_ref=r_small.at[src_idx], send_sem=send_sems.at[r - 1], recv_sem=recv_sems.at[r - 1],
                device_id=_peer(x, y, c, r), device_id_type=MESH)

        sends = [copy(r, me) for r in range(1, NDEV)]
        for cp in sends:
            cp.start()
        for r in range(1, NDEV):
            copy(r, me ^ r).wait_recv()
        for cp in sends:
            cp.wait_send()
        loc.wait()

    return pl.pallas_call(
        body, name="gather_small",
        out_shape=jax.ShapeDtypeStruct((NDEV, 8, D), jnp.float32),
        in_specs=[ANY] * (1 + len(after)), out_specs=ANY,
        scratch_shapes=[pltpu.SemaphoreType.DMA((NDEV - 1,)), pltpu.SemaphoreType.DMA((NDEV - 1,)),
                        pltpu.SemaphoreType.DMA],
    )(small, *after)


def _seg_tiles(s):
    return (s + 2) % NSEG


_POOL_SPECS = [pl.BlockSpec((None, T, GROUP), lambda g, base=base: (base + g, 0, 0)) for base in (0, 4)]
_HEAD_SPECS = [pl.BlockSpec((None, T, HEAD), lambda h, base=base: (base + h // 2, 0, h % 2))
               for base in (8, 12, 16, 20)]


def _row_ids(shape):
    return lax.broadcasted_iota(jnp.int32, shape, 0)


BAND_ROWS = 128
HALO = 16


def _window_sum(a, gidx, lead):
    width = lax.shift_left(jnp.int32(2), gidx)
    shape = (BAND_ROWS, BAND_ROWS + HALO)
    t, j = lax.broadcasted_iota(jnp.int32, shape, 0), lax.broadcasted_iota(jnp.int32, shape, 1)
    first = t if lead else t + HALO - width + 1
    band = _bf(jnp.where(j >= first, jnp.where(j < first + width, 1.0, 0.0), 0.0))
    zeros = jnp.zeros((HALO, a.shape[1]), jnp.bfloat16)
    padded = [jnp.concatenate([p, zeros] if lead else [zeros, p], axis=0) for p in _split2(a)]
    out = []
    for r0 in range(0, T, BAND_ROWS):
        slab = jnp.concatenate([p[r0:r0 + BAND_ROWS + HALO] for p in padded], axis=1)
        r = _dot(band, slab, 1, 0)
        out.append(r[:, :a.shape[1]] + r[:, a.shape[1]:])
    return jnp.concatenate(out, axis=0)


def _window_mean(s, gidx):
    inv = jnp.where(gidx == 0, 0.5, jnp.where(gidx == 1, 0.25, jnp.where(gidx == 2, 0.125, 0.0625)))
    width = lax.shift_left(jnp.int32(2), gidx)
    head = s[:16] / jnp.minimum(_row_ids((16, s.shape[1])) + 1, width).astype(jnp.float32)
    return jnp.concatenate([head, s[16:] * inv], axis=0)


def _pool_fwd(proj, pool_w, pool_scale, token):
    def body(u_ref, pg_ref, w_ref, sc_ref, token_any, y_ref):
        del token_any
        gidx = pl.program_id(0)
        u, pg = u_ref[...], pg_ref[...]
        d = _window_mean(_window_sum(u, gidx, False), gidx) - u
        mixed = _dot(_bf(d), w_ref[...], 1, 0)
        y_ref[...] = _bf(mixed * sc_ref[...] * (pg * _sigmoid(pg)))

    return pl.pallas_call(
        body, name="pool_fwd", grid=(NGROUP,),
        in_specs=[*_POOL_SPECS,
                  pl.BlockSpec((None, GROUP, GROUP), lambda g: (g, 0, 0)),
                  pl.BlockSpec((1, GROUP), lambda g: (0, g)), ANY],
        out_specs=pl.BlockSpec((T, GROUP), lambda g: (0, g)),
        out_shape=pltpu.HBM((T, DMIX), jnp.bfloat16),
        compiler_params=_params(("parallel",)),
    )(proj, proj, pool_w, pool_scale, token)


def _tri(lower):
    r = lax.broadcasted_iota(jnp.int32, (CHUNK, CHUNK), 0)
    c = lax.broadcasted_iota(jnp.int32, (CHUNK, CHUNK), 1)
    return (r >= c) if lower else (r <= c)


def _sum_rows_matrix():
    shape = (CHUNK + 16, CHUNK)
    r, c = lax.broadcasted_iota(jnp.int32, shape, 0), lax.broadcasted_iota(jnp.int32, shape, 1)
    run = jnp.where(c <= r, 1.0, 0.0)
    half = jnp.where(c < CHUNK // 2, 1.0, 0.0)
    return _bf(jnp.where(r < CHUNK, run, jnp.where(r < CHUNK + 8, 1.0, half)))


def _rev_sum_matrix():
    shape = (CHUNK, 2 * CHUNK)
    r, c = lax.broadcasted_iota(jnp.int32, shape, 0), lax.broadcasted_iota(jnp.int32, shape, 1)
    return _bf(jnp.where(c < CHUNK, jnp.where(c >= r, 1.0, 0.0), jnp.where(c - CHUNK < r, 1.0, 0.0)))


def _split2(a):
    hi = _bf(a)
    return [hi, _bf(a - hi.astype(jnp.float32))]


def _exact_sums(mat, pieces):
    x = jnp.concatenate([s for p in pieces for s in _split2(p)], axis=1)
    r = _dot(mat, x, 1, 0)
    return [r[:, 2 * j * HEAD:(2 * j + 1) * HEAD] + r[:, (2 * j + 1) * HEAD:(2 * j + 2) * HEAD]
            for j in range(len(pieces))]


def _gates(qv, fl, lb):
    sq = _sigmoid(qv)
    sg = _sigmoid(fl)
    f = lb + (1.0 - lb) * sg
    return dict(sq=sq, qs=qv * sq, sg=sg, f=f, kk=1.0 - f, g=jnp.log(f))


def _decays(sums):
    big_g = sums[:CHUNK]
    total = sums[CHUNK:CHUNK + 8]
    g_last = jnp.tile(total, (CHUNK // 8, 1))
    g_mid = jnp.tile(sums[CHUNK + 8:], (CHUNK // 8, 1))
    return dict(
        e_q=jnp.exp(big_g),
        e_k=jnp.exp(g_last - big_g),
        e_qm=jnp.exp(jnp.minimum(big_g - g_mid, EXP_CAP)),
        e_km=jnp.exp(jnp.minimum(g_mid - big_g, EXP_CAP)),
        total8=jnp.exp(total),
        state=jnp.exp(jnp.tile(total, (HEAD // 8, 1))))


def _group_rows(gi):
    return [pl.ds(pl.multiple_of((gi * NB + j) * CHUNK, CHUNK), CHUNK) for j in range(NB)]


def _lower_bound(lb_ref):
    return _sigmoid(lb_ref[0:1, :] - lb_ref[1:2, :])


def _hgrn_fwd(proj, lb_logits, rec_g, y_in):
    def body(q_ref, f_ref, i_ref, gate_ref, lb_ref, rg_ref, y_any, y_ref, o_ref, st_ref):
        del y_any
        lb = _lower_bound(lb_ref)
        causal = _tri(True)
        smat = _sum_rows_matrix()

        def group(gi, st):
            rows = _group_rows(gi)
            ts = [_gates(q_ref[r, :], f_ref[r, :], lb) for r in rows]
            ds = [_decays(s) for s in _exact_sums(smat, [t["g"] for t in ts])]
            vs = [_bf(i_ref[r, :]) for r in rows]
            q_m = [_bf(t["qs"] * d["e_qm"]) for t, d in zip(ts, ds)]
            k_m = [_bf(t["kk"] * d["e_km"]) for t, d in zip(ts, ds)]
            q_e = [_bf(t["qs"] * d["e_q"]) for t, d in zip(ts, ds)]
            k_e = [_bf(t["kk"] * d["e_k"]) for t, d in zip(ts, ds)]
            a = [_bf(jnp.where(causal, _dot(q_m[j], k_m[j], 1, 1), 0.0)) for j in range(NB)]
            intra = [_dot(a[j], vs[j], 1, 0) for j in range(NB)]
            upd = [_dot(vs[j], k_e[j], 0, 0) for j in range(NB)]
            for j in range(NB):
                st_ref[gi * NB + j] = st
                o_ref[rows[j], :] = intra[j] + _dot(q_e[j], _bf(st), 1, 1)
                st = st * ds[j]["state"] + upd[j]
            return st

        lax.fori_loop(0, NGRP, group, jnp.zeros((HEAD, HEAD), jnp.float32))
        o = o_ref[...]
        rn = o * lax.rsqrt(jnp.mean(o * o, axis=-1, keepdims=True) + EPS)
        gate = gate_ref[...]
        y_ref[...] = _bf(rn * rg_ref[...] * (gate * _sigmoid(gate)))

    return pl.pallas_call(
        body, name="hgrn_fwd", grid=(NHEAD,),
        in_specs=[*_HEAD_SPECS,
                  pl.BlockSpec((2, HEAD), lambda h: (0, h)),
                  pl.BlockSpec((1, HEAD), lambda h: (0, h)),
                  pl.BlockSpec(memory_space=pl.ANY)],
        out_specs=(pl.BlockSpec((T, HEAD), lambda h: (0, NHEAD + h)),
                   pl.BlockSpec((T, HEAD), lambda h: (0, h)),
                   pl.BlockSpec((None, NCHUNK, HEAD, HEAD), lambda h: (h, 0, 0, 0))),
        out_shape=(pltpu.HBM((T, DMIX), jnp.bfloat16), pltpu.HBM((T, D), jnp.float32),
                   pltpu.HBM((NHEAD, NCHUNK, HEAD, HEAD), jnp.float32)),
        input_output_aliases={6: 0},
        compiler_params=_params(("parallel",)),
    )(proj, proj, proj, proj, lb_logits, rec_g, y_in)


def _out_proj_loss(x, y, w_out, target, gf):
    rows = 512
    parts = [slice(k * rows // 2, (k + 1) * rows // 2) for k in range(2)]

    def body(x_ref, y_ref, w_ref, t_ref, g_ref, dz_ref, dzb_ref, sq_ref, dg_ref):
        zs = [x_ref[p, :] + _dot(y_ref[p, :], w_ref[...], 1, 0) for p in parts]
        sq = dg = 0.0
        for p, z in zip(parts, zs):
            r = lax.rsqrt(jnp.mean(z * z, axis=-1, keepdims=True) + EPS)
            zhat = z * r
            err = zhat * g_ref[...] - t_ref[p, :]
            dy = err * (1.0 / D)
            gdy = dy * g_ref[...]
            dz = r * (gdy - zhat * jnp.mean(zhat * gdy, axis=-1, keepdims=True))
            dz_ref[p, :] = dz
            dzb_ref[p, :] = _bf(dz)
            sq = sq + jnp.sum(err * err, axis=0, keepdims=True)
            dg = dg + jnp.sum(zhat * dy, axis=0, keepdims=True)

        @pl.when(pl.program_id(0) == 0)
        def _():
            sq_ref[...] = sq
            dg_ref[...] = dg

        @pl.when(pl.program_id(0) != 0)
        def _():
            sq_ref[...] += sq
            dg_ref[...] += dg

    tile = pl.BlockSpec((rows, D), lambda i: (i, 0))
    vec = pl.BlockSpec((1, D), lambda i: (0, 0))
    return pl.pallas_call(
        body, name="out_proj_loss", grid=(T // rows,),
        in_specs=[tile, pl.BlockSpec((rows, DMIX), lambda i: (i, 0)), pl.BlockSpec((DMIX, D), lambda i: (0, 0)),
                  tile, vec],
        out_specs=(tile, tile, vec, vec),
        out_shape=(pltpu.HBM((T, D), jnp.float32), pltpu.HBM((T, D), jnp.bfloat16),
                   jax.ShapeDtypeStruct((1, D), jnp.float32), jax.ShapeDtypeStruct((1, D), jnp.float32)),
        compiler_params=_params(("arbitrary",)),
    )(x, y, w_out, target, gf)


def _out_proj_bwd(dzb, w_out, y):
    tn = 512

    def body(dz_ref, w_ref, y_ref, dy_ref, gw_ref, gwb_ref):
        dz = dz_ref[...]
        dy_ref[...] = _dot(dz, w_ref[...], 1, 1)
        gw = _dot(y_ref[...], dz, 0, 0)
        gw_ref[...] = gw
        gwb_ref[...] = _bf(gw)

    return pl.pallas_call(
        body, name="out_proj_bwd", grid=(DMIX // tn,),
        in_specs=[pl.BlockSpec((T, D), lambda n: (0, 0)), pl.BlockSpec((tn, D), lambda n: (n, 0)),
                  pl.BlockSpec((T, tn), lambda n: (0, n))],
        out_specs=(pl.BlockSpec((T, tn), lambda n: (0, n)), pl.BlockSpec((tn, D), lambda n: (n, 0)),
                   pl.BlockSpec((tn, D), lambda n: (n, 0))),
        out_shape=(pltpu.HBM((T, DMIX), jnp.float32), pltpu.HBM((DMIX, D), jnp.float32),
                   pltpu.HBM((DMIX, D), jnp.bfloat16)),
        compiler_params=_params(("parallel",)),
    )(dzb, w_out, y)


def _hgrn_bwd(proj, lb_logits, rec_g, o, states, dymix, dproj_in, token):
    def body(q_ref, f_ref, i_ref, gate_ref, lb_ref, rg_ref, o_ref, st_ref, dy_ref, dp_any, token_any,
             dp_ref, drg_ref, dlb_ref, do_ref):
        del dp_any, token_any
        lb = _lower_bound(lb_ref)
        causal = _tri(True)
        smat, rmat = _sum_rows_matrix(), _rev_sum_matrix()

        o = o_ref[...]
        rs = lax.rsqrt(jnp.mean(o * o, axis=-1, keepdims=True) + EPS)
        rn = o * rs
        gate = gate_ref[...]
        sgate = _sigmoid(gate)
        dyv = dy_ref[...]
        d_r = dyv * (gate * sgate)
        dp_ref[3] = _bf(dyv * (rn * rg_ref[...]) * (sgate * (1.0 + gate * (1.0 - sgate))))
        drg_ref[...] = jnp.sum(d_r * rn, axis=0, keepdims=True)
        drn = d_r * rg_ref[...]
        do_ref[...] = rs * (drn - rn * jnp.mean(rn * drn, axis=-1, keepdims=True))

        def group(i, carry):
            dst, dlb = carry
            gi = NGRP - 1 - i
            rows = _group_rows(gi)
            span = range(NB)
            qvs = [q_ref[r, :] for r in rows]
            ts = [_gates(qv, f_ref[r, :], lb) for qv, r in zip(qvs, rows)]
            ds = [_decays(s) for s in _exact_sums(smat, [t["g"] for t in ts])]
            vs = [_bf(i_ref[r, :]) for r in rows]
            dos = [_bf(do_ref[r, :]) for r in rows]
            sts = [st_ref[gi * NB + j] for j in span]
            qe_f = [t["qs"] * d["e_q"] for t, d in zip(ts, ds)]
            ke_f = [t["kk"] * d["e_k"] for t, d in zip(ts, ds)]
            q_e, k_e = [_bf(a) for a in qe_f], [_bf(a) for a in ke_f]
            q_m = [_bf(t["qs"] * d["e_qm"]) for t, d in zip(ts, ds)]
            k_m = [_bf(t["kk"] * d["e_km"]) for t, d in zip(ts, ds)]
            a = [_bf(jnp.where(causal, _dot(q_m[j], k_m[j], 1, 1), 0.0)) for j in span]
            da = [_bf(jnp.where(causal, _dot(dos[j], vs[j], 1, 1), 0.0)) for j in span]
            dqm = [_dot(da[j], k_m[j], 1, 0) for j in span]
            dkm = [_dot(da[j], q_m[j], 0, 0) for j in span]
            dv_in = [_dot(a[j], dos[j], 0, 0) for j in span]
            dqe = [_dot(dos[j], _bf(sts[j]), 1, 0) for j in span]
            grow = [_dot(dos[j], q_e[j], 0, 0) for j in span]
            dke, carried = [None] * NB, [None] * NB
            for j in reversed(span):
                dst_b = _bf(dst)
                dke[j] = _dot(vs[j], dst_b, 1, 0)
                dp_ref[2, rows[j], :] = _bf(dv_in[j] + _dot(k_e[j], dst_b, 1, 1))
                carried[j] = ds[j]["total8"] * jnp.sum(dst * sts[j], axis=0, keepdims=True)
                dst = dst * ds[j]["state"] + grow[j]
            kdk = [ke_f[j] * dke[j] for j in span]
            pos = [(q_m[j].astype(jnp.float32) * dqm[j] - k_m[j].astype(jnp.float32) * dkm[j]) + qe_f[j] * dqe[j]
                   for j in span]
            dgs = _exact_sums(rmat, [jnp.concatenate([pos[j], kdk[j]], axis=0) for j in span])
            for j in span:
                t, d = ts[j], ds[j]
                dg = dgs[j] + jnp.tile(carried[j], (CHUNK // 8, 1))
                dqs = dqm[j] * d["e_qm"] + dqe[j] * d["e_q"]
                dkk = dkm[j] * d["e_km"] + dke[j] * d["e_k"]
                df = dg / t["f"] - dkk
                dp_ref[1, rows[j], :] = _bf(df * (1.0 - lb) * (t["sg"] * (1.0 - t["sg"])))
                dp_ref[0, rows[j], :] = _bf(dqs * (t["sq"] * (1.0 + qvs[j] * (1.0 - t["sq"]))))
                dlb = dlb + df * (1.0 - t["sg"])
            return dst, dlb

        _, dlb = lax.fori_loop(0, NGRP, group, (jnp.zeros((HEAD, HEAD), jnp.float32),
                                                jnp.zeros((CHUNK, HEAD), jnp.float32)))
        dlb_ref[...] = jnp.sum(dlb, axis=0, keepdims=True)

    vec = pl.BlockSpec((1, HEAD), lambda h: (0, h))
    return pl.pallas_call(
        body, name="hgrn_bwd", grid=(NHEAD,),
        in_specs=[*_HEAD_SPECS,
                  pl.BlockSpec((2, HEAD), lambda h: (0, h)), vec,
                  pl.BlockSpec((T, HEAD), lambda h: (0, h)),
                  pl.BlockSpec((None, NCHUNK, HEAD, HEAD), lambda h: (h, 0, 0, 0)),
                  pl.BlockSpec((T, HEAD), lambda h: (0, NHEAD + h)), ANY, ANY],
        out_specs=(pl.BlockSpec((4, T, HEAD), lambda h: (0, 0, h)), vec, vec),
        out_shape=(pltpu.HBM((NSEG, T, D), jnp.bfloat16),
                   jax.ShapeDtypeStruct((1, D), jnp.float32), jax.ShapeDtypeStruct((1, D), jnp.float32)),
        scratch_shapes=[pltpu.VMEM((T, HEAD), jnp.float32)],
        input_output_aliases={9: 0},
        compiler_params=_params(("parallel",)),
    )(proj, proj, proj, proj, lb_logits, rec_g, o, states, dymix, dproj_in, token)


def _pool_bwd(proj, pool_w, pool_scale, dymix):
    def body(u_ref, pg_ref, w_ref, sc_ref, dy_ref, dp_ref, gw_ref, gs_ref):
        gidx = pl.program_id(0)
        u, pg = u_ref[...], pg_ref[...]
        d = _bf(_window_mean(_window_sum(u, gidx, False), gidx) - u)
        mixed = _dot(d, w_ref[...], 1, 0)
        spg = _sigmoid(pg)
        dyv = dy_ref[...]
        d_p = dyv * (pg * spg)
        dp_ref[1] = _bf(dyv * (mixed * sc_ref[...]) * (spg * (1.0 + pg * (1.0 - spg))))
        gs_ref[...] = jnp.sum(d_p * mixed, axis=0, keepdims=True)
        dmixed = _bf(d_p * sc_ref[...])
        gw_ref[...] = _dot(d, dmixed, 0, 0)
        dd = _dot(dmixed, w_ref[...], 1, 1)
        dp_ref[0] = _bf(_window_sum(_window_mean(dd, gidx), gidx, True) - dd)

    return pl.pallas_call(
        body, name="pool_bwd", grid=(NGROUP,),
        in_specs=[*_POOL_SPECS,
                  pl.BlockSpec((None, GROUP, GROUP), lambda g: (g, 0, 0)),
                  pl.BlockSpec((1, GROUP), lambda g: (0, g)),
                  pl.BlockSpec((T, GROUP), lambda g: (0, g))],
        out_specs=(pl.BlockSpec((2, T, GROUP), lambda g: (2, 0, g)),
                   pl.BlockSpec((None, GROUP, GROUP), lambda g: (g, 0, 0)),
                   pl.BlockSpec((1, GROUP), lambda g: (0, g))),
        out_shape=(pltpu.HBM((NSEG, T, D), jnp.bfloat16),
                   jax.ShapeDtypeStruct((NGROUP, GROUP, GROUP), jnp.float32),
                   jax.ShapeDtypeStruct((1, D), jnp.float32)),
        compiler_params=_params(("parallel",)),
    )(proj, proj, pool_w, pool_scale, dymix)


def _proj_bwd_w(place, ht, dproj):
    half = NTILE // 2

    def owner_chip(i, pr):
        return jnp.where(i < half, i // 3, (pr[1] + 1 + (i - half) // 3) % 4)

    def tile_of(i, pr):
        side = jnp.where(i < half, 1 - pr[2], pr[2])
        return 6 * owner_chip(i, pr) + 3 * side + i % 3

    def dproj_block(i, pr):
        j = tile_of(i, pr)
        return ((j // 4 + 4) % NSEG, 0, j % 4)

    def mine(i):
        return jnp.maximum(i, half)

    def body(place_ref, h_ref, dp_ref, sum_ref, own_ref, sendbuf, recvbuf, send_sems, recv_sems):
        i = pl.program_id(0)
        px, py, c, _ = _place()
        gw = _dot(h_ref[...], dp_ref[...], 1, 0)

        def to_sibling(slot):
            return pltpu.make_async_remote_copy(
                src_ref=sendbuf.at[slot], dst_ref=recvbuf.at[slot], send_sem=send_sems.at[slot],
                recv_sem=recv_sems.at[slot], device_id=(px, py, 1 - c), device_id_type=MESH)

        @pl.when(i < half)
        def _():
            sendbuf[i] = _bf(gw)
            to_sibling(i).start()

        @pl.when(i >= half)
        def _():
            slot = 3 * owner_chip(i, place_ref) + i % 3
            to_sibling(slot).wait_recv()
            total = gw + recvbuf[slot].astype(jnp.float32)
            sum_ref[...] = _bf(total)
            own_ref[...] = total

        @pl.when(i == NTILE - 1)
        def _():
            for slot in range(half):
                to_sibling(slot).wait_send()

    return pl.pallas_call(
        body, name="proj_bwd_w",
        grid_spec=pltpu.PrefetchScalarGridSpec(
            num_scalar_prefetch=1, grid=(NTILE,),
            in_specs=[pl.BlockSpec((D, T), lambda i, pr: (0, 0)),
                      pl.BlockSpec((None, T, TILE), lambda i, pr: dproj_block(i, pr))],
            out_specs=(pl.BlockSpec((None, None, D, TILE), lambda i, pr: (owner_chip(mine(i), pr), mine(i) % 3, 0, 0)),
                       pl.BlockSpec((None, D, TILE), lambda i, pr: (jnp.where(i < NTILE - 3, 0, i % 3), 0, 0))),
            scratch_shapes=[pltpu.VMEM((half, D, TILE), jnp.bfloat16), pltpu.VMEM((half, D, TILE), jnp.bfloat16),
                            pltpu.SemaphoreType.DMA((half,)), pltpu.SemaphoreType.DMA((half,))]),
        out_shape=(pltpu.HBM((4, 3, D, TILE), jnp.bfloat16), pltpu.HBM((3, D, TILE), jnp.float32)),
        compiler_params=_params(("arbitrary",)),
    )(place, ht, dproj)


def _proj_bwd_x(dproj, w_t, x, g1, dz, token):
    tm = 512
    pairs = NSEG // 2

    def body(dp_ref, w_ref, x_ref, g_ref, dz_ref, token_any, dx_ref, dg_ref, wcat, acc):
        del token_any
        m, s = pl.program_id(0), pl.program_id(1)
        r = None
        for k in range(2):
            for i in range(4):
                wcat[k, :, i * TILE:(i + 1) * TILE] = w_ref[4 * k + i]
            part = _dot(dp_ref[k], wcat[k], 1, 1)
            r = part if r is None else r + part

        @pl.when(s == 0)
        def _():
            acc[...] = r

        @pl.when(s != 0)
        def _():
            acc[...] += r

        @pl.when(s == pairs - 1)
        def _():
            xv = x_ref[...]
            rs = lax.rsqrt(jnp.mean(xv * xv, axis=-1, keepdims=True) + EPS)
            xhat = xv * rs
            dhv = acc[...]
            gdh = dhv * g_ref[...]
            dx_ref[...] = dz_ref[...] + rs * (gdh - xhat * jnp.mean(xhat * gdh, axis=-1, keepdims=True))
            dg = jnp.sum(xhat * dhv, axis=0, keepdims=True)

            @pl.when(m == 0)
            def _():
                dg_ref[...] = dg

            @pl.when(m != 0)
            def _():
                dg_ref[...] += dg

    rows = pl.BlockSpec((tm, D), lambda m, s: (m, 0))
    vec = pl.BlockSpec((1, D), lambda m, s: (0, 0))
    return pl.pallas_call(
        body, name="proj_bwd_x", grid=(T // tm, pairs),
        in_specs=[pl.BlockSpec((2, tm, D), lambda m, s: (s, m, 0)),
                  pl.BlockSpec((8, D, TILE), lambda m, s: ((s + 1) % pairs, 0, 0)), rows, vec, rows, ANY],
        out_specs=(rows, vec),
        out_shape=(jax.ShapeDtypeStruct((T, D), jnp.float32), jax.ShapeDtypeStruct((1, D), jnp.float32)),
        scratch_shapes=[pltpu.VMEM((2, D, D), jnp.bfloat16), pltpu.VMEM((tm, D), jnp.float32)],
        compiler_params=_params(("arbitrary", "arbitrary"), vmem_mib=56),
    )(dproj, w_t, x, g1, dz, token)


def _adamw(w, g, m, v):
    m_new = ADAM_B1 * m + (1.0 - ADAM_B1) * g
    v_new = ADAM_B2 * v + (1.0 - ADAM_B2) * (g * g)
    delta = -ADAM_LR * ((m_new / BC1) / (jnp.sqrt(v_new / BC2) + ADAM_EPS) + ADAM_WD * w)
    return delta, m_new, v_new


def _reduce_adam(name, place, parts, w, m, v, grid, w_spec):
    n = len(parts)

    def body(place_ref, *refs):
        del place_ref
        w_ref, m_ref, v_ref, g_ref, d_ref, mo_ref, vo_ref = refs[n:]
        g = None
        for ref, (_, _, stacked) in zip(refs[:n], parts):
            terms = [ref[r] for r in range(ref.shape[0])] if stacked else [ref[...]]
            for t in terms:
                if t.shape[-1] != w_ref.shape[-1]:
                    t = jnp.concatenate([t[p] for p in range(t.shape[0])], axis=1)
                g = t.astype(jnp.float32) if g is None else g + t.astype(jnp.float32)
        delta, m_new, v_new = _adamw(w_ref[...], g, m_ref[...], v_ref[...])
        g_ref[...] = g
        d_ref[...] = delta
        mo_ref[...] = m_new
        vo_ref[...] = v_new

    shape = jax.ShapeDtypeStruct(w.shape, jnp.float32)
    return pl.pallas_call(
        body, name=name,
        grid_spec=pltpu.PrefetchScalarGridSpec(
            num_scalar_prefetch=1, grid=grid,
            in_specs=[spec for _, spec, _ in parts] + [w_spec] * 3, out_specs=(w_spec,) * 4),
        out_shape=(shape,) * 4,
        compiler_params=_params(("parallel",)),
    )(place, *[_in_hbm(a) for a in [a for a, _, _ in parts] + [w, m, v]])


def _small_adam(parts, w, m, v):
    def body(p_ref, w_ref, m_ref, v_ref, g_ref, d_ref, mo_ref, vo_ref):
        g = p_ref[0]
        for s in range(1, NDEV):
            g = g + p_ref[s]
        wv = w_ref[...]
        rows = _row_ids(wv.shape)
        other = jnp.where(rows == 2, pltpu.roll(wv, 7, 0), jnp.where(rows == 3, pltpu.roll(wv, 1, 0), 0.0))
        lbv = _sigmoid(wv - other)
        sign = jnp.where(rows == 2, 1.0, -1.0)
        g = jnp.where((rows == 2) | (rows == 3), sign * g * lbv * (1.0 - lbv), g)
        delta, m_new, v_new = _adamw(wv, g, m_ref[...], v_ref[...])
        g_ref[...] = g
        d_ref[...] = delta
        mo_ref[...] = m_new
        vo_ref[...] = v_new

    shape = jax.ShapeDtypeStruct((8, D), jnp.float32)
    return pl.pallas_call(body, name="small_adam", out_shape=(shape,) * 4)(parts, w, m, v)


def _rows8(*vecs):
    rows = [a.reshape(-1, D) for a in vecs]
    n = sum(r.shape[0] for r in rows)
    return jnp.concatenate(rows + [jnp.zeros((8 - n, D), jnp.float32)], axis=0)


def kernel(x, norm1_g, w_in, pool_w, pool_scale, lb_logits, rec_norm_g, w_out, final_norm_g, loss_target, m_norm1_g, m_w_in, m_pool_w, m_pool_scale, m_lb_logits, m_rec_norm_g, m_w_out, m_final_norm_g, v_norm1_g, v_w_in, v_pool_w, v_pool_scale, v_lb_logits, v_rec_norm_g, v_w_out, v_final_norm_g):
    xs = x[0]
    target = loss_target[0]
    ix, iy, ic = lax.axis_index("x"), lax.axis_index("y"), lax.axis_index("c")
    place = jnp.stack([4 * ix + 2 * iy + ic, 2 * ix + iy, ic]).astype(jnp.int32)
    gf = final_norm_g.reshape(1, D)

    ht, w_t, w_out_b, w_out_g, pool_g, proj = _gather_proj(xs, norm1_g, w_in, w_out, pool_w)
    pool_full = pool_g.transpose(1, 0, 2, 3).reshape(NGROUP, GROUP, GROUP)
    wout = [w_out_b, w_out_g]
    wout_send, wout_recv, wout, wout_token = _split_start("gather_wout_start", wout, NDEV - 1, _plan_wout)

    y = _pool_fwd(proj, pool_full, pool_scale, wout_token)
    y, o, states = _hgrn_fwd(proj, lb_logits, rec_norm_g, y)
    _, w_out_g = _split_wait("gather_wout_wait", wout, wout_send, wout_recv, _plan_wout, o)
    w_out_full = _in_hbm(w_out_g.reshape(DMIX, D))
    dz, dzb, sq, dgf = _out_proj_loss(xs, y, w_out_full, target, gf)

    dymix, gwout_f, gwout_b = _out_proj_bwd(dzb, w_out_full, y)
    dproj, gpool, dscale = _pool_bwd(proj, pool_full, pool_scale, dymix)

    blk_out = (NDEV, DMIX // NDEV, D)
    blk_pool = (NDEV, NGROUP, GROUP // NDEV, GROUP)
    gpool_s = gpool.reshape(NGROUP, NDEV, GROUP // NDEV, GROUP).transpose(1, 0, 2, 3)
    rest = [gwout_b.reshape(blk_out), gpool_s,
            lax.empty((NDEV - 1,) + blk_out[1:], jnp.bfloat16), lax.empty((NDEV - 1,) + blk_pool[1:], jnp.float32)]
    rest_send, rest_recv, rest, rest_token = _split_start("scatter_rest_start", rest, 2 * (NDEV - 1), _plan_rest)

    dproj, drecg, dlb = _hgrn_bwd(proj, lb_logits, rec_norm_g, o, states, dymix, dproj, rest_token)
    chip_sums, own_sum = _proj_bwd_w(place, ht, dproj)
    win = [chip_sums, lax.empty((3, 3, D, TILE), jnp.bfloat16)]
    win_send, win_recv, win, win_token = _split_start("scatter_win_start", win, 3, _plan_in)

    grad_x, dg1 = _proj_bwd_x(dproj, w_t, xs, norm1_g, dz, win_token)

    _, gpool_own, r_out, r_pool = _split_wait("scatter_rest_wait", rest, rest_send, rest_recv, _plan_rest, grad_x)
    g_wout, d_wout, m_wout, v_wout = _reduce_adam(
        "adam_w_out", place,
        [(gwout_f.reshape(blk_out), pl.BlockSpec((None,) + blk_out[1:], lambda i, pr: (pr[0], 0, 0)), False),
         (r_out, pl.BlockSpec((NDEV - 1,) + blk_out[1:], lambda i, pr: (0, 0, 0)), True)],
        w_out, m_w_out, v_w_out, (1,), pl.BlockSpec((None,) + blk_out[1:], lambda i, pr: (0, 0, 0)))
    g_pool, d_pool, m_pool, v_pool = _reduce_adam(
        "adam_pool_w", place,
        [(gpool_own, pl.BlockSpec((None,) + blk_pool[1:], lambda i, pr: (pr[0], 0, 0, 0)), False),
         (r_pool, pl.BlockSpec((NDEV - 1,) + blk_pool[1:], lambda i, pr: (0, 0, 0, 0)), True)],
        pool_w, m_pool_w, v_pool_w, (1,), pl.BlockSpec((None,) + blk_pool[1:], lambda i, pr: (0, 0, 0, 0)))

    r_small = _gather_small(_rows8(dg1, dscale, dlb, dlb, drecg, dgf, sq), d_wout, d_pool)
    g_s, d_s, m_s, v_s = _small_adam(
        r_small,
        _rows8(norm1_g, pool_scale, lb_logits, rec_norm_g, final_norm_g),
        _rows8(m_norm1_g, m_pool_scale, m_lb_logits, m_rec_norm_g, m_final_norm_g),
        _rows8(v_norm1_g, v_pool_scale, v_lb_logits, v_rec_norm_g, v_final_norm_g))
    loss = jnp.sum(g_s[6]) * (0.5 / D)

    _, r_in = _split_wait("scatter_win_wait", win, win_send, win_recv, _plan_in, v_s)
    g_win, d_win, m_win, v_win = _reduce_adam(
        "adam_w_in", place,
        [(own_sum, pl.BlockSpec((3, D // 8, TILE), lambda i, pr: (0, i, 0)), False),
         (r_in, pl.BlockSpec((3, 3, D // 8, TILE), lambda i, pr: (0, 0, i, 0)), True)],
        w_in, m_w_in, v_w_in, (8,), pl.BlockSpec((None, D // 8, 3 * TILE), lambda i, pr: (0, i, 0)))

    def small_outs(a):
        return a[0:1], a[1:2], a[2:4], a[4:5], a[5]

    def outs(small_a, win, pool, wout):
        n1, ps, lbl, rg, fg = small_outs(small_a)
        return n1, win, pool, ps, lbl, rg, wout, fg

    return (loss, grad_x[None],
            *outs(g_s, g_win, g_pool, g_wout), *outs(d_s, d_win, d_pool, d_wout),
            *outs(m_s, m_win, m_pool, m_wout), *outs(v_s, v_win, v_pool, v_wout))
```

```python
import functools

import jax
import jax.numpy as jnp
from jax import lax
from jax.experimental import pallas as pl
from jax.experimental.pallas import tpu as pltpu

T = 2048
D = 1024
NSEG = 6
NTILE = 24
TILE = 256
DMIX = 2048
NDEV = 8
HEAD = 128
NHEAD = 8
CHUNK = 64
NCHUNK = T // CHUNK
NB = 32
NGRP = NCHUNK // NB
NGROUP = 4
GROUP = 256
EPS = 1e-6
EXP_CAP = 80.0
MESH = pl.DeviceIdType.MESH
AXES = ("x", "y", "c")
ANY = pl.BlockSpec(memory_space=pl.ANY)
HBM = pl.BlockSpec(memory_space=pltpu.HBM)
SEM = pl.BlockSpec(memory_space=pltpu.SEMAPHORE)
EFFECT = pltpu.SideEffectType.DATAFLOW_SIDE_EFFECTING

ADAM_LR = 0.001
ADAM_B1 = 0.9
ADAM_B2 = 0.999
ADAM_EPS = 1e-08
ADAM_WD = 0.01
ADAM_STEP = 10
BC1 = 1.0 - ADAM_B1 ** ADAM_STEP
BC2 = 1.0 - ADAM_B2 ** ADAM_STEP

MIB = 1 << 20


def _params(sem=None, vmem_mib=48):
    return pltpu.CompilerParams(dimension_semantics=sem, vmem_limit_bytes=vmem_mib * MIB)


def _sigmoid(v):
    return 1.0 / (1.0 + jnp.exp(-v))


def _dot(a, b, ca, cb, precision=None):
    return lax.dot_general(a, b, (((ca,), (cb,)), ((), ())), precision=precision,
                           preferred_element_type=jnp.float32)


def _bf(v):
    return v.astype(jnp.bfloat16)


def _in_hbm(a):
    return pltpu.with_memory_space_constraint(a, pltpu.HBM)


def _place():
    x, y, c = lax.axis_index("x"), lax.axis_index("y"), lax.axis_index("c")
    return x, y, c, 4 * x + 2 * y + c


def _peer(x, y, c, r):
    return (x ^ ((r >> 2) & 1), y ^ ((r >> 1) & 1), c ^ (r & 1))


def _gather_proj(x, g1, w_in, w_out, pool_w):
    def body(x_ref, g_ref, win_ref, wout_ref, pool_ref, ht_o, wt_o, woutb_o, wout_o, pool_o, proj_o,
             xbuf, hv, htv, wv, wob, pb, stage, send_sems, recv_sems, loc_sems, out_sems):
        px, py, c, my_idx = _place()
        fetch_x = pltpu.make_async_copy(x_ref, xbuf, loc_sems.at[5])
        fetch_x.start()
        me, sibling = (px, py, c), (px, py, 1 - c)
        chips = [(1 - px, py), (px, 1 - py), (1 - px, 1 - py)]
        for p in range(3):
            wv[3 * my_idx + p] = _bf(win_ref[0, :, p * TILE:(p + 1) * TILE])

        def index(bx, by, bc):
            return 4 * bx + 2 * by + bc

        def slot(w, block):
            return wv.at[pl.ds(3 * index(*block), 3)] if w == 0 else pool_o.at[index(*block)]

        def copy(k, w, block, to, src=None):
            return pltpu.make_async_remote_copy(
                src_ref=slot(w, block) if src is None else src, dst_ref=slot(w, block),
                send_sem=send_sems.at[2 * k + w], recv_sem=recv_sems.at[2 * k + w],
                device_id=to, device_id_type=MESH)

        def save(block):
            at = pl.ds(3 * index(*block), 3)
            pltpu.make_async_copy(wv.at[at], wt_o.at[at], loc_sems.at[4]).start()

        srcs = (slot(0, me), pb)
        first = []
        for w in (0, 1):
            if w == 1:
                pb[...] = _bf(pool_ref[0])
                wob[...] = _bf(wout_ref[0])
            group = [copy(1 + j, w, me, (*chip, c), src=srcs[w]) for j, chip in enumerate(chips[:2])]
            group.append(copy(0, w, me, sibling, src=srcs[w]))
            for cp in group:
                cp.start()
            first += group
        save(me)
        locs = [pltpu.make_async_copy(pb, slot(1, me), loc_sems.at[0]),
                pltpu.make_async_copy(wob, wout_o.at[my_idx], loc_sems.at[1]),
                pltpu.make_async_copy(wob, woutb_o, loc_sems.at[2])]
        for cp in locs:
            cp.start()

        fetch_x.wait()
        xv = xbuf[...]
        hv[...] = _bf(xv * lax.rsqrt(jnp.mean(xv * xv, axis=-1, keepdims=True) + EPS) * g_ref[...])
        rows = 256
        for r0 in range(0, T, rows):
            htv[:, r0:r0 + rows] = hv[r0:r0 + rows, :].T
        locs.append(pltpu.make_async_copy(htv, ht_o, loc_sems.at[3]))
        locs[-1].start()

        def out_copy(p, j):
            return pltpu.make_async_copy(stage.at[p], proj_o.at[j], out_sems.at[p])

        def project(nth, block):
            base = 3 * index(*block)

            def tile(p, carry):
                if nth > 0:
                    out_copy(p, base + p).wait()
                stage[p] = _dot(hv[...], wv[base + p], 1, 0)
                out_copy(p, base + p).start()
                return carry

            lax.fori_loop(0, 3, tile, 0)

        project(0, me)
        copy(0, 0, sibling, me).wait_recv()
        save(sibling)
        project(1, sibling)
        passed = []
        relay_from = (px ^ (1 - c), py ^ c, c)
        relay_to = (px ^ c, py ^ (1 - c), c)

        def arrived(w, j):
            copy(1 + j, w, (*chips[j], c), me).wait_recv()
            passed.append(copy(4 + j, w, (*chips[j], c), sibling))
            passed[-1].start()

        def relay(w):
            passed.append(copy(3, w, relay_from, relay_to))
            passed[-1].start()

        def handed(nth, j):
            copy(4 + j, 0, (*chips[j], 1 - c), me).wait_recv()
            save((*chips[j], 1 - c))
            project(nth, (*chips[j], 1 - c))

        arrived(0, 0)
        arrived(0, 1)
        relay(0)
        for j in range(2):
            save((*chips[j], c))
            project(2 + j, (*chips[j], c))
        handed(4, 0)
        handed(5, 1)
        arrived(1, 0)
        arrived(1, 1)
        relay(1)
        arrived(0, 2)
        save((*chips[2], c))
        project(6, (*chips[2], c))
        handed(7, 2)
        arrived(1, 2)
        copy(0, 1, sibling, me).wait_recv()
        for j, chip in enumerate(chips):
            copy(4 + j, 1, (*chip, 1 - c), me).wait_recv()
        keep = pltpu.make_async_copy(wv, wt_o, loc_sems.at[4])
        for p in range(3):
            out_copy(p, p).wait()
        for cp in first + passed:
            cp.wait_send()
        keep.wait()
        for cp in locs:
            cp.wait()

    vmem = pl.BlockSpec(memory_space=pltpu.VMEM)
    bf16 = jnp.bfloat16
    return pl.pallas_call(
        body, name="gather_proj",
        out_shape=(pltpu.HBM((D, T), bf16), pltpu.HBM((NTILE, D, TILE), bf16),
                   pltpu.HBM((DMIX // NDEV, D), bf16), pltpu.HBM((NDEV, DMIX // NDEV, D), bf16),
                   pltpu.HBM((NDEV, NGROUP, GROUP // NDEV, GROUP), bf16), pltpu.HBM((NTILE, T, TILE), jnp.float32)),
        in_specs=[ANY] + [vmem] * 4, out_specs=(ANY,) * 6,
        scratch_shapes=[pltpu.VMEM((T, D), jnp.float32),
                        pltpu.VMEM((T, D), bf16), pltpu.VMEM((D, T), bf16), pltpu.VMEM((NTILE, D, TILE), bf16),
                        pltpu.VMEM((DMIX // NDEV, D), bf16), pltpu.VMEM((NGROUP, GROUP // NDEV, GROUP), bf16),
                        pltpu.VMEM((3, T, TILE), jnp.float32),
                        pltpu.SemaphoreType.DMA((14,)), pltpu.SemaphoreType.DMA((14,)),
                        pltpu.SemaphoreType.DMA((6,)), pltpu.SemaphoreType.DMA((3,))],
        compiler_params=_params(vmem_mib=56),
    )(x, g1, w_in, w_out, pool_w)


def _split_start(name, arrays, n_copies, plan):
    k = len(arrays)

    def body(*refs):
        send_sems, recv_sems, token = refs[k], refs[k + 1], refs[-1]
        for i, (src, dst, to) in enumerate(plan(refs[:k])):
            pltpu.make_async_remote_copy(src_ref=src, dst_ref=dst, send_sem=send_sems.at[i],
                                         recv_sem=recv_sems.at[i], device_id=to, device_id_type=MESH).start()
        token[...] = jnp.zeros_like(token)

    out = pl.pallas_call(
        body, name=name,
        out_shape=(pltpu.SemaphoreType.DMA((n_copies,)), pltpu.SemaphoreType.DMA((n_copies,)),
                   *[pltpu.HBM(a.shape, a.dtype) for a in arrays], jax.ShapeDtypeStruct((8, 128), jnp.float32)),
        in_specs=[HBM] * k, out_specs=(SEM, SEM, *[HBM] * k, pl.BlockSpec(memory_space=pltpu.VMEM)),
        input_output_aliases={i: 2 + i for i in range(k)},
        compiler_params=pltpu.CompilerParams(has_side_effects=EFFECT),
    )(*[pltpu.with_memory_space_constraint(a, pltpu.HBM) for a in arrays])
    return out[0], out[1], out[2:2 + k], out[-1]


def _split_wait(name, arrays, send_sems, recv_sems, plan, after):
    k = len(arrays)

    def body(*refs):
        sends, recvs = refs[k], refs[k + 1]
        for i, (src, dst, to) in enumerate(plan(refs[:k])):
            cp = pltpu.make_async_remote_copy(src_ref=src, dst_ref=dst, send_sem=sends.at[i], recv_sem=recvs.at[i],
                                              device_id=to, device_id_type=MESH)
            cp.wait_send()
            cp.wait_recv()

    return pl.pallas_call(
        body, name=name,
        out_shape=tuple(pltpu.HBM(a.shape, a.dtype) for a in arrays),
        in_specs=[HBM] * k + [SEM, SEM, ANY], out_specs=(HBM,) * k,
        input_output_aliases={i: i for i in range(k)},
        compiler_params=pltpu.CompilerParams(has_side_effects=EFFECT),
    )(*arrays, send_sems, recv_sems, after)


def _plan_wout(refs):
    src, land = refs
    x, y, c, me = _place()
    return [(src, land.at[me], _peer(x, y, c, r)) for r in range(1, NDEV)]


def _plan_rest(refs):
    gob, gpf, r_out, r_pool = refs
    x, y, c, me = _place()
    plan = []
    for r in range(1, NDEV):
        plan.append((gob.at[me ^ r], r_out.at[r - 1], _peer(x, y, c, r)))
        plan.append((gpf.at[me ^ r], r_pool.at[r - 1], _peer(x, y, c, r)))
    return plan


def _plan_in(refs):
    sum_b, r_in = refs
    x, y, c, _ = _place()
    plan = []
    for j, (dx, dy) in enumerate(((1, 0), (0, 1), (1, 1))):
        px, py = x ^ dx, y ^ dy
        plan.append((sum_b.at[2 * px + py], r_in.at[j], (px, py, c)))
    return plan


def _gather_small(small, *after):
    def body(sm, *refs):
        r_small, send_sems, recv_sems, loc_sem = refs[len(after):]
        x, y, c, me = _place()
        loc = pltpu.make_async_copy(sm, r_small.at[me], loc_sem)
        loc.start()

        def copy(r, src_idx):
            return pltpu.make_async_remote_copy(
                src_ref=sm, dst_ref=r_small.at[src_idx], send_sem=send_sems.at[r - 1], recv_sem=recv_sems.at[r - 1],
                device_id=_peer(x, y, c, r), device_id_type=MESH)

        sends = [copy(r, me) for r in range(1, NDEV)]
        for cp in sends:
            cp.start()
        for r in range(1, NDEV):
            copy(r, me ^ r).wait_recv()
        for cp in sends:
            cp.wait_send()
        loc.wait()

    return pl.pallas_call(
        body, name="gather_small",
        out_shape=jax.ShapeDtypeStruct((NDEV, 8, D), jnp.float32),
        in_specs=[ANY] * (1 + len(after)), out_specs=ANY,
        scratch_shapes=[pltpu.SemaphoreType.DMA((NDEV - 1,)), pltpu.SemaphoreType.DMA((NDEV - 1,)),
                        pltpu.SemaphoreType.DMA],
    )(small, *after)


def _seg_tiles(s):
    return (s + 2) % NSEG


_POOL_SPECS = [pl.BlockSpec((None, T, GROUP), lambda g, base=base: (base + g, 0, 0)) for base in (0, 4)]
_HEAD_SPECS = [pl.BlockSpec((None, T, HEAD), lambda h, base=base: (base + h // 2, 0, h % 2))
               for base in (8, 12, 16, 20)]


def _row_ids(shape):
    return lax.broadcasted_iota(jnp.int32, shape, 0)


BAND_ROWS = 128
HALO = 16


def _window_sum(a, gidx, lead):
    width = lax.shift_left(jnp.int32(2), gidx)
    shape = (BAND_ROWS, BAND_ROWS + HALO)
    t, j = lax.broadcasted_iota(jnp.int32, shape, 0), lax.broadcasted_iota(jnp.int32, shape, 1)
    first = t if lead else t + HALO - width + 1
    band = _bf(jnp.where(j >= first, jnp.where(j < first + width, 1.0, 0.0), 0.0))
    zeros = jnp.zeros((HALO, a.shape[1]), jnp.bfloat16)
    padded = [jnp.concatenate([p, zeros] if lead else [zeros, p], axis=0) for p in _split2(a)]
    out = []
    for r0 in range(0, T, BAND_ROWS):
        slab = jnp.concatenate([p[r0:r0 + BAND_ROWS + HALO] for p in padded], axis=1)
        r = _dot(band, slab, 1, 0)
        out.append(r[:, :a.shape[1]] + r[:, a.shape[1]:])
    return jnp.concatenate(out, axis=0)


def _window_mean(s, gidx):
    inv = jnp.where(gidx == 0, 0.5, jnp.where(gidx == 1, 0.25, jnp.where(gidx == 2, 0.125, 0.0625)))
    width = lax.shift_left(jnp.int32(2), gidx)
    head = s[:16] / jnp.minimum(_row_ids((16, s.shape[1])) + 1, width).astype(jnp.float32)
    return jnp.concatenate([head, s[16:] * inv], axis=0)


def _pool_fwd(proj, pool_w, pool_scale, token):
    def body(u_ref, pg_ref, w_ref, sc_ref, token_any, y_ref):
        del token_any
        gidx = pl.program_id(0)
        u, pg = u_ref[...], pg_ref[...]
        d = _window_mean(_window_sum(u, gidx, False), gidx) - u
        mixed = _dot(_bf(d), w_ref[...], 1, 0)
        y_ref[...] = _bf(mixed * sc_ref[...] * (pg * _sigmoid(pg)))

    return pl.pallas_call(
        body, name="pool_fwd", grid=(NGROUP,),
        in_specs=[*_POOL_SPECS,
                  pl.BlockSpec((None, GROUP, GROUP), lambda g: (g, 0, 0)),
                  pl.BlockSpec((1, GROUP), lambda g: (0, g)), ANY],
        out_specs=pl.BlockSpec((T, GROUP), lambda g: (0, g)),
        out_shape=pltpu.HBM((T, DMIX), jnp.bfloat16),
        compiler_params=_params(("parallel",)),
    )(proj, proj, pool_w, pool_scale, token)


def _tri(lower):
    r = lax.broadcasted_iota(jnp.int32, (CHUNK, CHUNK), 0)
    c = lax.broadcasted_iota(jnp.int32, (CHUNK, CHUNK), 1)
    return (r >= c) if lower else (r <= c)


def _sum_rows_matrix():
    shape = (CHUNK + 16, CHUNK)
    r, c = lax.broadcasted_iota(jnp.int32, shape, 0), lax.broadcasted_iota(jnp.int32, shape, 1)
    run = jnp.where(c <= r, 1.0, 0.0)
    half = jnp.where(c < CHUNK // 2, 1.0, 0.0)
    return _bf(jnp.where(r < CHUNK, run, jnp.where(r < CHUNK + 8, 1.0, half)))


def _rev_sum_matrix():
    shape = (CHUNK, 2 * CHUNK)
    r, c = lax.broadcasted_iota(jnp.int32, shape, 0), lax.broadcasted_iota(jnp.int32, shape, 1)
    return _bf(jnp.where(c < CHUNK, jnp.where(c >= r, 1.0, 0.0), jnp.where(c - CHUNK < r, 1.0, 0.0)))


def _split2(a):
    hi = _bf(a)
    return [hi, _bf(a - hi.astype(jnp.float32))]


def _exact_sums(mat, pieces):
    x = jnp.concatenate([s for p in pieces for s in _split2(p)], axis=1)
    r = _dot(mat, x, 1, 0)
    return [r[:, 2 * j * HEAD:(2 * j + 1) * HEAD] + r[:, (2 * j + 1) * HEAD:(2 * j + 2) * HEAD]
            for j in range(len(pieces))]


def _gates(qv, fl, lb):
    sq = _sigmoid(qv)
    sg = _sigmoid(fl)
    f = lb + (1.0 - lb) * sg
    return dict(sq=sq, qs=qv * sq, sg=sg, f=f, kk=1.0 - f, g=jnp.log(f))


def _decays(sums):
    big_g = sums[:CHUNK]
    total = sums[CHUNK:CHUNK + 8]
    g_last = jnp.tile(total, (CHUNK // 8, 1))
    g_mid = jnp.tile(sums[CHUNK + 8:], (CHUNK // 8, 1))
    return dict(
        e_q=jnp.exp(big_g),
        e_k=jnp.exp(g_last - big_g),
        e_qm=jnp.exp(jnp.minimum(big_g - g_mid, EXP_CAP)),
        e_km=jnp.exp(jnp.minimum(g_mid - big_g, EXP_CAP)),
        total8=jnp.exp(total),
        state=jnp.exp(jnp.tile(total, (HEAD // 8, 1))))


def _group_rows(gi):
    return [pl.ds(pl.multiple_of((gi * NB + j) * CHUNK, CHUNK), CHUNK) for j in range(NB)]


def _lower_bound(lb_ref):
    return _sigmoid(lb_ref[0:1, :] - lb_ref[1:2, :])


def _hgrn_fwd(proj, lb_logits, rec_g, y_in):
    def body(q_ref, f_ref, i_ref, gate_ref, lb_ref, rg_ref, y_any, y_ref, o_ref, st_ref):
        del y_any
        lb = _lower_bound(lb_ref)
        causal = _tri(True)
        smat = _sum_rows_matrix()

        def group(gi, st):
            rows = _group_rows(gi)
            ts = [_gates(q_ref[r, :], f_ref[r, :], lb) for r in rows]
            ds = [_decays(s) for s in _exact_sums(smat, [t["g"] for t in ts])]
            vs = [_bf(i_ref[r, :]) for r in rows]
            q_m = [_bf(t["qs"] * d["e_qm"]) for t, d in zip(ts, ds)]
            k_m = [_bf(t["kk"] * d["e_km"]) for t, d in zip(ts, ds)]
            q_e = [_bf(t["qs"] * d["e_q"]) for t, d in zip(ts, ds)]
            k_e = [_bf(t["kk"] * d["e_k"]) for t, d in zip(ts, ds)]
            a = [_bf(jnp.where(causal, _dot(q_m[j], k_m[j], 1, 1), 0.0)) for j in range(NB)]
            intra = [_dot(a[j], vs[j], 1, 0) for j in range(NB)]
            upd = [_dot(vs[j], k_e[j], 0, 0) for j in range(NB)]
            for j in range(NB):
                st_ref[gi * NB + j] = st
                o_ref[rows[j], :] = intra[j] + _dot(q_e[j], _bf(st), 1, 1)
                st = st * ds[j]["state"] + upd[j]
            return st

        lax.fori_loop(0, NGRP, group, jnp.zeros((HEAD, HEAD), jnp.float32))
        o = o_ref[...]
        rn = o * lax.rsqrt(jnp.mean(o * o, axis=-1, keepdims=True) + EPS)
        gate = gate_ref[...]
        y_ref[...] = _bf(rn * rg_ref[...] * (gate * _sigmoid(gate)))

    return pl.pallas_call(
        body, name="hgrn_fwd", grid=(NHEAD,),
        in_specs=[*_HEAD_SPECS,
                  pl.BlockSpec((2, HEAD), lambda h: (0, h)),
                  pl.BlockSpec((1, HEAD), lambda h: (0, h)),
                  pl.BlockSpec(memory_space=pl.ANY)],
        out_specs=(pl.BlockSpec((T, HEAD), lambda h: (0, NHEAD + h)),
                   pl.BlockSpec((T, HEAD), lambda h: (0, h)),
                   pl.BlockSpec((None, NCHUNK, HEAD, HEAD), lambda h: (h, 0, 0, 0))),
        out_shape=(pltpu.HBM((T, DMIX), jnp.bfloat16), pltpu.HBM((T, D), jnp.float32),
                   pltpu.HBM((NHEAD, NCHUNK, HEAD, HEAD), jnp.float32)),
        input_output_aliases={6: 0},
        compiler_params=_params(("parallel",)),
    )(proj, proj, proj, proj, lb_logits, rec_g, y_in)


def _out_proj_loss(x, y, w_out, target, gf):
    rows = 512
    parts = [slice(k * rows // 2, (k + 1) * rows // 2) for k in range(2)]

    def body(x_ref, y_ref, w_ref, t_ref, g_ref, dz_ref, dzb_ref, sq_ref, dg_ref):
        zs = [x_ref[p, :] + _dot(y_ref[p, :], w_ref[...], 1, 0) for p in parts]
        sq = dg = 0.0
        for p, z in zip(parts, zs):
            r = lax.rsqrt(jnp.mean(z * z, axis=-1, keepdims=True) + EPS)
            zhat = z * r
            err = zhat * g_ref[...] - t_ref[p, :]
            dy = err * (1.0 / D)
            gdy = dy * g_ref[...]
            dz = r * (gdy - zhat * jnp.mean(zhat * gdy, axis=-1, keepdims=True))
            dz_ref[p, :] = dz
            dzb_ref[p, :] = _bf(dz)
            sq = sq + jnp.sum(err * err, axis=0, keepdims=True)
            dg = dg + jnp.sum(zhat * dy, axis=0, keepdims=True)

        @pl.when(pl.program_id(0) == 0)
        def _():
            sq_ref[...] = sq
            dg_ref[...] = dg

        @pl.when(pl.program_id(0) != 0)
        def _():
            sq_ref[...] += sq
            dg_ref[...] += dg

    tile = pl.BlockSpec((rows, D), lambda i: (i, 0))
    vec = pl.BlockSpec((1, D), lambda i: (0, 0))
    return pl.pallas_call(
        body, name="out_proj_loss", grid=(T // rows,),
        in_specs=[tile, pl.BlockSpec((rows, DMIX), lambda i: (i, 0)), pl.BlockSpec((DMIX, D), lambda i: (0, 0)),
                  tile, vec],
        out_specs=(tile, tile, vec, vec),
        out_shape=(pltpu.HBM((T, D), jnp.float32), pltpu.HBM((T, D), jnp.bfloat16),
                   jax.ShapeDtypeStruct((1, D), jnp.float32), jax.ShapeDtypeStruct((1, D), jnp.float32)),
        compiler_params=_params(("arbitrary",)),
    )(x, y, w_out, target, gf)


def _out_proj_bwd(dzb, w_out, y):
    tn = 512

    def body(dz_ref, w_ref, y_ref, dy_ref, gw_ref, gwb_ref):
        dz = dz_ref[...]
        dy_ref[...] = _dot(dz, w_ref[...], 1, 1)
        gw = _dot(y_ref[...], dz, 0, 0)
        gw_ref[...] = gw
        gwb_ref[...] = _bf(gw)

    return pl.pallas_call(
        body, name="out_proj_bwd", grid=(DMIX // tn,),
        in_specs=[pl.BlockSpec((T, D), lambda n: (0, 0)), pl.BlockSpec((tn, D), lambda n: (n, 0)),
                  pl.BlockSpec((T, tn), lambda n: (0, n))],
        out_specs=(pl.BlockSpec((T, tn), lambda n: (0, n)), pl.BlockSpec((tn, D), lambda n: (n, 0)),
                   pl.BlockSpec((tn, D), lambda n: (n, 0))),
        out_shape=(pltpu.HBM((T, DMIX), jnp.float32), pltpu.HBM((DMIX, D), jnp.float32),
                   pltpu.HBM((DMIX, D), jnp.bfloat16)),
        compiler_params=_params(("parallel",)),
    )(dzb, w_out, y)


def _hgrn_bwd(proj, lb_logits, rec_g, o, states, dymix, dproj_in, token):
    def body(q_ref, f_ref, i_ref, gate_ref, lb_ref, rg_ref, o_ref, st_ref, dy_ref, dp_any, token_any,
             dp_ref, drg_ref, dlb_ref, do_ref):
        del dp_any, token_any
        lb = _lower_bound(lb_ref)
        causal = _tri(True)
        smat, rmat = _sum_rows_matrix(), _rev_sum_matrix()

        o = o_ref[...]
        rs = lax.rsqrt(jnp.mean(o * o, axis=-1, keepdims=True) + EPS)
        rn = o * rs
        gate = gate_ref[...]
        sgate = _sigmoid(gate)
        dyv = dy_ref[...]
        d_r = dyv * (gate * sgate)
        dp_ref[3] = _bf(dyv * (rn * rg_ref[...]) * (sgate * (1.0 + gate * (1.0 - sgate))))
        drg_ref[...] = jnp.sum(d_r * rn, axis=0, keepdims=True)
        drn = d_r * rg_ref[...]
        do_ref[...] = rs * (drn - rn * jnp.mean(rn * drn, axis=-1, keepdims=True))

        def group(i, carry):
            dst, dlb = carry
            gi = NGRP - 1 - i
            rows = _group_rows(gi)
            span = range(NB)
            qvs = [q_ref[r, :] for r in rows]
            ts = [_gates(qv, f_ref[r, :], lb) for qv, r in zip(qvs, rows)]
            ds = [_decays(s) for s in _exact_sums(smat, [t["g"] for t in ts])]
            vs = [_bf(i_ref[r, :]) for r in rows]
            dos = [_bf(do_ref[r, :]) for r in rows]
            sts = [st_ref[gi * NB + j] for j in span]
            qe_f = [t["qs"] * d["e_q"] for t, d in zip(ts, ds)]
            ke_f = [t["kk"] * d["e_k"] for t, d in zip(ts, ds)]
            q_e, k_e = [_bf(a) for a in qe_f], [_bf(a) for a in ke_f]
            q_m = [_bf(t["qs"] * d["e_qm"]) for t, d in zip(ts, ds)]
            k_m = [_bf(t["kk"] * d["e_km"]) for t, d in zip(ts, ds)]
            a = [_bf(jnp.where(causal, _dot(q_m[j], k_m[j], 1, 1), 0.0)) for j in span]
            da = [_bf(jnp.where(causal, _dot(dos[j], vs[j], 1, 1), 0.0)) for j in span]
            dqm = [_dot(da[j], k_m[j], 1, 0) for j in span]
            dkm = [_dot(da[j], q_m[j], 0, 0) for j in span]
            dv_in = [_dot(a[j], dos[j], 0, 0) for j in span]
            dqe = [_dot(dos[j], _bf(sts[j]), 1, 0) for j in span]
            grow = [_dot(dos[j], q_e[j], 0, 0) for j in span]
            dke, carried = [None] * NB, [None] * NB
            for j in reversed(span):
                dst_b = _bf(dst)
                dke[j] = _dot(vs[j], dst_b, 1, 0)
                dp_ref[2, rows[j], :] = _bf(dv_in[j] + _dot(k_e[j], dst_b, 1, 1))
                carried[j] = ds[j]["total8"] * jnp.sum(dst * sts[j], axis=0, keepdims=True)
                dst = dst * ds[j]["state"] + grow[j]
            kdk = [ke_f[j] * dke[j] for j in span]
            pos = [(q_m[j].astype(jnp.float32) * dqm[j] - k_m[j].astype(jnp.float32) * dkm[j]) + qe_f[j] * dqe[j]
                   for j in span]
            dgs = _exact_sums(rmat, [jnp.concatenate([pos[j], kdk[j]], axis=0) for j in span])
            for j in span:
                t, d = ts[j], ds[j]
                dg = dgs[j] + jnp.tile(carried[j], (CHUNK // 8, 1))
                dqs = dqm[j] * d["e_qm"] + dqe[j] * d["e_q"]
                dkk = dkm[j] * d["e_km"] + dke[j] * d["e_k"]
                df = dg / t["f"] - dkk
                dp_ref[1, rows[j], :] = _bf(df * (1.0 - lb) * (t["sg"] * (1.0 - t["sg"])))
                dp_ref[0, rows[j], :] = _bf(dqs * (t["sq"] * (1.0 + qvs[j] * (1.0 - t["sq"]))))
                dlb = dlb + df * (1.0 - t["sg"])
            return dst, dlb

        _, dlb = lax.fori_loop(0, NGRP, group, (jnp.zeros((HEAD, HEAD), jnp.float32),
                                                jnp.zeros((CHUNK, HEAD), jnp.float32)))
        dlb_ref[...] = jnp.sum(dlb, axis=0, keepdims=True)

    vec = pl.BlockSpec((1, HEAD), lambda h: (0, h))
    return pl.pallas_call(
        body, name="hgrn_bwd", grid=(NHEAD,),
        in_specs=[*_HEAD_SPECS,
                  pl.BlockSpec((2, HEAD), lambda h: (0, h)), vec,
                  pl.BlockSpec((T, HEAD), lambda h: (0, h)),
                  pl.BlockSpec((None, NCHUNK, HEAD, HEAD), lambda h: (h, 0, 0, 0)),
                  pl.BlockSpec((T, HEAD), lambda h: (0, NHEAD + h)), ANY, ANY],
        out_specs=(pl.BlockSpec((4, T, HEAD), lambda h: (0, 0, h)), vec, vec),
        out_shape=(pltpu.HBM((NSEG, T, D), jnp.bfloat16),
                   jax.ShapeDtypeStruct((1, D), jnp.float32), jax.ShapeDtypeStruct((1, D), jnp.float32)),
        scratch_shapes=[pltpu.VMEM((T, HEAD), jnp.float32)],
        input_output_aliases={9: 0},
        compiler_params=_params(("parallel",)),
    )(proj, proj, proj, proj, lb_logits, rec_g, o, states, dymix, dproj_in, token)


def _pool_bwd(proj, pool_w, pool_scale, dymix):
    def body(u_ref, pg_ref, w_ref, sc_ref, dy_ref, dp_ref, gw_ref, gs_ref):
        gidx = pl.program_id(0)
        u, pg = u_ref[...], pg_ref[...]
        d = _bf(_window_mean(_window_sum(u, gidx, False), gidx) - u)
        mixed = _dot(d, w_ref[...], 1, 0)
        spg = _sigmoid(pg)
        dyv = dy_ref[...]
        d_p = dyv * (pg * spg)
        dp_ref[1] = _bf(dyv * (mixed * sc_ref[...]) * (spg * (1.0 + pg * (1.0 - spg))))
        gs_ref[...] = jnp.sum(d_p * mixed, axis=0, keepdims=True)
        dmixed = _bf(d_p * sc_ref[...])
        gw_ref[...] = _dot(d, dmixed, 0, 0)
        dd = _dot(dmixed, w_ref[...], 1, 1)
        dp_ref[0] = _bf(_window_sum(_window_mean(dd, gidx), gidx, True) - dd)

    return pl.pallas_call(
        body, name="pool_bwd", grid=(NGROUP,),
        in_specs=[*_POOL_SPECS,
                  pl.BlockSpec((None, GROUP, GROUP), lambda g: (g, 0, 0)),
                  pl.BlockSpec((1, GROUP), lambda g: (0, g)),
                  pl.BlockSpec((T, GROUP), lambda g: (0, g))],
        out_specs=(pl.BlockSpec((2, T, GROUP), lambda g: (2, 0, g)),
                   pl.BlockSpec((None, GROUP, GROUP), lambda g: (g, 0, 0)),
                   pl.BlockSpec((1, GROUP), lambda g: (0, g))),
        out_shape=(pltpu.HBM((NSEG, T, D), jnp.bfloat16),
                   jax.ShapeDtypeStruct((NGROUP, GROUP, GROUP), jnp.float32),
                   jax.ShapeDtypeStruct((1, D), jnp.float32)),
        compiler_params=_params(("parallel",)),
    )(proj, proj, pool_w, pool_scale, dymix)


def _proj_bwd_w(place, ht, dproj):
    half = NTILE // 2

    def owner_chip(i, pr):
        return jnp.where(i < half, i // 3, (pr[1] + 1 + (i - half) // 3) % 4)

    def tile_of(i, pr):
        side = jnp.where(i < half, 1 - pr[2], pr[2])
        return 6 * owner_chip(i, pr) + 3 * side + i % 3

    def dproj_block(i, pr):
        j = tile_of(i, pr)
        return ((j // 4 + 4) % NSEG, 0, j % 4)

    def mine(i):
        return jnp.maximum(i, half)

    def body(place_ref, h_ref, dp_ref, sum_ref, own_ref, sendbuf, recvbuf, send_sems, recv_sems):
        i = pl.program_id(0)
        px, py, c, _ = _place()
        gw = _dot(h_ref[...], dp_ref[...], 1, 0)

        def to_sibling(slot):
            return pltpu.make_async_remote_copy(
                src_ref=sendbuf.at[slot], dst_ref=recvbuf.at[slot], send_sem=send_sems.at[slot],
                recv_sem=recv_sems.at[slot], device_id=(px, py, 1 - c), device_id_type=MESH)

        @pl.when(i < half)
        def _():
            sendbuf[i] = _bf(gw)
            to_sibling(i).start()

        @pl.when(i >= half)
        def _():
            slot = 3 * owner_chip(i, place_ref) + i % 3
            to_sibling(slot).wait_recv()
            total = gw + recvbuf[slot].astype(jnp.float32)
            sum_ref[...] = _bf(total)
            own_ref[...] = total

        @pl.when(i == NTILE - 1)
        def _():
            for slot in range(half):
                to_sibling(slot).wait_send()

    return pl.pallas_call(
        body, name="proj_bwd_w",
        grid_spec=pltpu.PrefetchScalarGridSpec(
            num_scalar_prefetch=1, grid=(NTILE,),
            in_specs=[pl.BlockSpec((D, T), lambda i, pr: (0, 0)),
                      pl.BlockSpec((None, T, TILE), lambda i, pr: dproj_block(i, pr))],
            out_specs=(pl.BlockSpec((None, None, D, TILE), lambda i, pr: (owner_chip(mine(i), pr), mine(i) % 3, 0, 0)),
                       pl.BlockSpec((None, D, TILE), lambda i, pr: (jnp.where(i < NTILE - 3, 0, i % 3), 0, 0))),
            scratch_shapes=[pltpu.VMEM((half, D, TILE), jnp.bfloat16), pltpu.VMEM((half, D, TILE), jnp.bfloat16),
                            pltpu.SemaphoreType.DMA((half,)), pltpu.SemaphoreType.DMA((half,))]),
        out_shape=(pltpu.HBM((4, 3, D, TILE), jnp.bfloat16), pltpu.HBM((3, D, TILE), jnp.float32)),
        compiler_params=_params(("arbitrary",)),
    )(place, ht, dproj)


def _proj_bwd_x(dproj, w_t, x, g1, dz, win):
    tm = 512
    pairs = NSEG // 2

    def body(dp_ref, w_ref, x_ref, g_ref, dz_ref, sums_ref, land_ref, dx_ref, dg_ref, send_sems, recv_sems,
             sums_thru, land_thru, wcat, acc):
        del sums_thru, land_thru
        m, s = pl.program_id(0), pl.program_id(1)

        @pl.when(jnp.logical_and(m == 0, s == 0))
        def _():
            for i, (src, dst, to) in enumerate(_plan_in((sums_ref, land_ref))):
                pltpu.make_async_remote_copy(src_ref=src, dst_ref=dst, send_sem=send_sems.at[i],
                                             recv_sem=recv_sems.at[i], device_id=to, device_id_type=MESH).start()

        r = None
        for k in range(2):
            for i in range(4):
                wcat[k, :, i * TILE:(i + 1) * TILE] = w_ref[4 * k + i]
            part = _dot(dp_ref[k], wcat[k], 1, 1)
            r = part if r is None else r + part

        @pl.when(s == 0)
        def _():
            acc[...] = r

        @pl.when(s != 0)
        def _():
            acc[...] += r

        @pl.when(s == pairs - 1)
        def _():
            xv = x_ref[...]
            rs = lax.rsqrt(jnp.mean(xv * xv, axis=-1, keepdims=True) + EPS)
            xhat = xv * rs
            dhv = acc[...]
            gdh = dhv * g_ref[...]
            dx_ref[...] = dz_ref[...] + rs * (gdh - xhat * jnp.mean(xhat * gdh, axis=-1, keepdims=True))
            dg = jnp.sum(xhat * dhv, axis=0, keepdims=True)

            @pl.when(m == 0)
            def _():
                dg_ref[...] = dg

            @pl.when(m != 0)
            def _():
                dg_ref[...] += dg

    rows = pl.BlockSpec((tm, D), lambda m, s: (m, 0))
    vec = pl.BlockSpec((1, D), lambda m, s: (0, 0))
    return pl.pallas_call(
        body, name="proj_bwd_x", grid=(T // tm, pairs),
        in_specs=[pl.BlockSpec((2, tm, D), lambda m, s: (s, m, 0)),
                  pl.BlockSpec((8, D, TILE), lambda m, s: ((s + 1) % pairs, 0, 0)), rows, vec, rows, HBM, HBM],
        out_specs=(rows, vec, SEM, SEM, HBM, HBM),
        out_shape=(jax.ShapeDtypeStruct((T, D), jnp.float32), jax.ShapeDtypeStruct((1, D), jnp.float32),
                   pltpu.SemaphoreType.DMA((3,)), pltpu.SemaphoreType.DMA((3,)),
                   *[pltpu.HBM(a.shape, a.dtype) for a in win]),
        scratch_shapes=[pltpu.VMEM((2, D, D), jnp.bfloat16), pltpu.VMEM((tm, D), jnp.float32)],
        input_output_aliases={5: 4, 6: 5},
        compiler_params=pltpu.CompilerParams(dimension_semantics=("arbitrary", "arbitrary"),
                                             vmem_limit_bytes=56 * MIB, has_side_effects=EFFECT),
    )(dproj, w_t, x, g1, dz, *[_in_hbm(a) for a in win])


def _adamw(w, g, m, v):
    m_new = ADAM_B1 * m + (1.0 - ADAM_B1) * g
    v_new = ADAM_B2 * v + (1.0 - ADAM_B2) * (g * g)
    delta = -ADAM_LR * ((m_new / BC1) / (jnp.sqrt(v_new / BC2) + ADAM_EPS) + ADAM_WD * w)
    return delta, m_new, v_new


def _reduce_adam(name, place, parts, w, m, v, grid, w_spec):
    n = len(parts)

    def body(place_ref, *refs):
        del place_ref
        w_ref, m_ref, v_ref, g_ref, d_ref, mo_ref, vo_ref = refs[n:]
        g = None
        for ref, (_, _, stacked) in zip(refs[:n], parts):
            terms = [ref[r] for r in range(ref.shape[0])] if stacked else [ref[...]]
            for t in terms:
                if t.shape[-1] != w_ref.shape[-1]:
                    t = jnp.concatenate([t[p] for p in range(t.shape[0])], axis=1)
                g = t.astype(jnp.float32) if g is None else g + t.astype(jnp.float32)
        delta, m_new, v_new = _adamw(w_ref[...], g, m_ref[...], v_ref[...])
        g_ref[...] = g
        d_ref[...] = delta
        mo_ref[...] = m_new
        vo_ref[...] = v_new

    shape = jax.ShapeDtypeStruct(w.shape, jnp.float32)
    return pl.pallas_call(
        body, name=name,
        grid_spec=pltpu.PrefetchScalarGridSpec(
            num_scalar_prefetch=1, grid=grid,
            in_specs=[spec for _, spec, _ in parts] + [w_spec] * 3, out_specs=(w_spec,) * 4),
        out_shape=(shape,) * 4,
        compiler_params=_params(("parallel",)),
    )(place, *[_in_hbm(a) for a in [a for a, _, _ in parts] + [w, m, v]])


def _small_adam(parts, w, m, v):
    def body(p_ref, w_ref, m_ref, v_ref, g_ref, d_ref, mo_ref, vo_ref):
        g = p_ref[0]
        for s in range(1, NDEV):
            g = g + p_ref[s]
        wv = w_ref[...]
        rows = _row_ids(wv.shape)
        other = jnp.where(rows == 2, pltpu.roll(wv, 7, 0), jnp.where(rows == 3, pltpu.roll(wv, 1, 0), 0.0))
        lbv = _sigmoid(wv - other)
        sign = jnp.where(rows == 2, 1.0, -1.0)
        g = jnp.where((rows == 2) | (rows == 3), sign * g * lbv * (1.0 - lbv), g)
        delta, m_new, v_new = _adamw(wv, g, m_ref[...], v_ref[...])
        g_ref[...] = g
        d_ref[...] = delta
        mo_ref[...] = m_new
        vo_ref[...] = v_new

    shape = jax.ShapeDtypeStruct((8, D), jnp.float32)
    return pl.pallas_call(body, name="small_adam", out_shape=(shape,) * 4)(parts, w, m, v)


def _rows8(*vecs):
    rows = [a.reshape(-1, D) for a in vecs]
    n = sum(r.shape[0] for r in rows)
    return jnp.concatenate(rows + [jnp.zeros((8 - n, D), jnp.float32)], axis=0)


def kernel(x, norm1_g, w_in, pool_w, pool_scale, lb_logits, rec_norm_g, w_out, final_norm_g, loss_target, m_norm1_g, m_w_in, m_pool_w, m_pool_scale, m_lb_logits, m_rec_norm_g, m_w_out, m_final_norm_g, v_norm1_g, v_w_in, v_pool_w, v_pool_scale, v_lb_logits, v_rec_norm_g, v_w_out, v_final_norm_g):
    xs = x[0]
    target = loss_target[0]
    ix, iy, ic = lax.axis_index("x"), lax.axis_index("y"), lax.axis_index("c")
    place = jnp.stack([4 * ix + 2 * iy + ic, 2 * ix + iy, ic]).astype(jnp.int32)
    gf = final_norm_g.reshape(1, D)

    ht, w_t, w_out_b, w_out_g, pool_g, proj = _gather_proj(xs, norm1_g, w_in, w_out, pool_w)
    pool_full = pool_g.transpose(1, 0, 2, 3).reshape(NGROUP, GROUP, GROUP)
    wout = [w_out_b, w_out_g]
    wout_send, wout_recv, wout, wout_token = _split_start("gather_wout_start", wout, NDEV - 1, _plan_wout)

    y = _pool_fwd(proj, pool_full, pool_scale, wout_token)
    y, o, states = _hgrn_fwd(proj, lb_logits, rec_norm_g, y)
    _, w_out_g = _split_wait("gather_wout_wait", wout, wout_send, wout_recv, _plan_wout, o)
    w_out_full = _in_hbm(w_out_g.reshape(DMIX, D))
    dz, dzb, sq, dgf = _out_proj_loss(xs, y, w_out_full, target, gf)

    dymix, gwout_f, gwout_b = _out_proj_bwd(dzb, w_out_full, y)
    dproj, gpool, dscale = _pool_bwd(proj, pool_full, pool_scale, dymix)

    blk_out = (NDEV, DMIX // NDEV, D)
    blk_pool = (NDEV, NGROUP, GROUP // NDEV, GROUP)
    gpool_s = gpool.reshape(NGROUP, NDEV, GROUP // NDEV, GROUP).transpose(1, 0, 2, 3)
    rest = [gwout_b.reshape(blk_out), gpool_s,
            lax.empty((NDEV - 1,) + blk_out[1:], jnp.bfloat16), lax.empty((NDEV - 1,) + blk_pool[1:], jnp.float32)]
    rest_send, rest_recv, rest, rest_token = _split_start("scatter_rest_start", rest, 2 * (NDEV - 1), _plan_rest)

    dproj, drecg, dlb = _hgrn_bwd(proj, lb_logits, rec_norm_g, o, states, dymix, dproj, rest_token)
    chip_sums, own_sum = _proj_bwd_w(place, ht, dproj)
    grad_x, dg1, win_send, win_recv, *win = _proj_bwd_x(
        dproj, w_t, xs, norm1_g, dz, [chip_sums, lax.empty((3, 3, D, TILE), jnp.bfloat16)])

    _, gpool_own, r_out, r_pool = _split_wait("scatter_rest_wait", rest, rest_send, rest_recv, _plan_rest, grad_x)
    g_wout, d_wout, m_wout, v_wout = _reduce_adam(
        "adam_w_out", place,
        [(gwout_f.reshape(blk_out), pl.BlockSpec((None,) + blk_out[1:], lambda i, pr: (pr[0], 0, 0)), False),
         (r_out, pl.BlockSpec((NDEV - 1,) + blk_out[1:], lambda i, pr: (0, 0, 0)), True)],
        w_out, m_w_out, v_w_out, (1,), pl.BlockSpec((None,) + blk_out[1:], lambda i, pr: (0, 0, 0)))
    g_pool, d_pool, m_pool, v_pool = _reduce_adam(
        "adam_pool_w", place,
        [(gpool_own, pl.BlockSpec((None,) + blk_pool[1:], lambda i, pr: (pr[0], 0, 0, 0)), False),
         (r_pool, pl.BlockSpec((NDEV - 1,) + blk_pool[1:], lambda i, pr: (0, 0, 0, 0)), True)],
        pool_w, m_pool_w, v_pool_w, (1,), pl.BlockSpec((None,) + blk_pool[1:], lambda i, pr: (0, 0, 0, 0)))

    r_small = _gather_small(_rows8(dg1, dscale, dlb, dlb, drecg, dgf, sq), d_wout, d_pool)
    g_s, d_s, m_s, v_s = _small_adam(
        r_small,
        _rows8(norm1_g, pool_scale, lb_logits, rec_norm_g, final_norm_g),
        _rows8(m_norm1_g, m_pool_scale, m_lb_logits, m_rec_norm_g, m_final_norm_g),
        _rows8(v_norm1_g, v_pool_scale, v_lb_logits, v_rec_norm_g, v_final_norm_g))
    loss = jnp.sum(g_s[6]) * (0.5 / D)

    _, r_in = _split_wait("scatter_win_wait", win, win_send, win_recv, _plan_in, v_s)
    g_win, d_win, m_win, v_win = _reduce_adam(
        "adam_w_in", place,
        [(own_sum, pl.BlockSpec((3, D // 8, TILE), lambda i, pr: (0, i, 0)), False),
         (r_in, pl.BlockSpec((3, 3, D // 8, TILE), lambda i, pr: (0, 0, i, 0)), True)],
        w_in, m_w_in, v_w_in, (8,), pl.BlockSpec((None, D // 8, 3 * TILE), lambda i, pr: (0, i, 0)))

    def small_outs(a):
        return a[0:1], a[1:2], a[2:4], a[4:5], a[5]

    def outs(small_a, win, pool, wout):
        n1, ps, lbl, rg, fg = small_outs(small_a)
        return n1, win, pool, ps, lbl, rg, wout, fg

    return (loss, grad_x[None],
            *outs(g_s, g_win, g_pool, g_wout), *outs(d_s, d_win, d_pool, d_wout),
            *outs(m_s, m_win, m_pool, m_wout), *outs(v_s, v_win, v_pool, v_wout))
```

```python
import functools

import jax
import jax.numpy as jnp
from jax import lax
from jax.experimental import pallas as pl
from jax.experimental.pallas import tpu as pltpu

T = 2048
D = 1024
NSEG = 6
NTILE = 24
TILE = 256
DMIX = 2048
NDEV = 8
HEAD = 128
NHEAD = 8
CHUNK = 64
NCHUNK = T // CHUNK
NB = 32
NGRP = NCHUNK // NB
NGROUP = 4
GROUP = 256
EPS = 1e-6
EXP_CAP = 115.0
MESH = pl.DeviceIdType.MESH
AXES = ("x", "y", "c")
ANY = pl.BlockSpec(memory_space=pl.ANY)
HBM = pl.BlockSpec(memory_space=pltpu.HBM)
SEM = pl.BlockSpec(memory_space=pltpu.SEMAPHORE)
EFFECT = pltpu.SideEffectType.DATAFLOW_SIDE_EFFECTING

ADAM_LR = 0.001
ADAM_B1 = 0.9
ADAM_B2 = 0.999
ADAM_EPS = 1e-08
ADAM_WD = 0.01
ADAM_STEP = 10
BC1 = 1.0 - ADAM_B1 ** ADAM_STEP
BC2 = 1.0 - ADAM_B2 ** ADAM_STEP

MIB = 1 << 20


def _params(sem=None, vmem_mib=48):
    return pltpu.CompilerParams(dimension_semantics=sem, vmem_limit_bytes=vmem_mib * MIB)


def _sigmoid(v):
    return 1.0 / (1.0 + jnp.exp(-v))


def _dot(a, b, ca, cb, precision=None):
    return lax.dot_general(a, b, (((ca,), (cb,)), ((), ())), precision=precision,
                           preferred_element_type=jnp.float32)


def _bf(v):
    return v.astype(jnp.bfloat16)


def _in_hbm(a):
    return pltpu.with_memory_space_constraint(a, pltpu.HBM)


def _place():
    x, y, c = lax.axis_index("x"), lax.axis_index("y"), lax.axis_index("c")
    return x, y, c, 4 * x + 2 * y + c


def _peer(x, y, c, r):
    return (x ^ ((r >> 2) & 1), y ^ ((r >> 1) & 1), c ^ (r & 1))


def _gather_proj(x, g1, w_in, w_out, pool_w):
    def body(x_ref, g_ref, win_ref, wout_ref, pool_ref, ht_o, wt_o, woutb_o, wout_o, pool_o, proj_o,
             xbuf, hv, htv, wv, wob, pb, stage, send_sems, recv_sems, loc_sems, out_sems):
        px, py, c, my_idx = _place()
        fetch_x = pltpu.make_async_copy(x_ref, xbuf, loc_sems.at[5])
        fetch_x.start()
        me, sibling = (px, py, c), (px, py, 1 - c)
        chips = [(1 - px, py), (px, 1 - py), (1 - px, 1 - py)]
        for p in range(3):
            wv[3 * my_idx + p] = _bf(win_ref[0, :, p * TILE:(p + 1) * TILE])

        def index(bx, by, bc):
            return 4 * bx + 2 * by + bc

        def slot(w, block):
            return wv.at[pl.ds(3 * index(*block), 3)] if w == 0 else pool_o.at[index(*block)]

        def copy(k, w, block, to, src=None):
            return pltpu.make_async_remote_copy(
                src_ref=slot(w, block) if src is None else src, dst_ref=slot(w, block),
                send_sem=send_sems.at[2 * k + w], recv_sem=recv_sems.at[2 * k + w],
                device_id=to, device_id_type=MESH)

        def save(block):
            at = pl.ds(3 * index(*block), 3)
            pltpu.make_async_copy(wv.at[at], wt_o.at[at], loc_sems.at[4]).start()

        srcs = (slot(0, me), pb)
        first = []
        for w in (0, 1):
            if w == 1:
                pb[...] = _bf(pool_ref[0])
                wob[...] = _bf(wout_ref[0])
            group = [copy(1 + j, w, me, (*chip, c), src=srcs[w]) for j, chip in enumerate(chips[:2])]
            group.append(copy(0, w, me, sibling, src=srcs[w]))
            for cp in group:
                cp.start()
            first += group
        save(me)
        locs = [pltpu.make_async_copy(pb, slot(1, me), loc_sems.at[0]),
                pltpu.make_async_copy(wob, wout_o.at[my_idx], loc_sems.at[1]),
                pltpu.make_async_copy(wob, woutb_o, loc_sems.at[2])]
        for cp in locs:
            cp.start()

        fetch_x.wait()
        xv = xbuf[...]
        hv[...] = _bf(xv * lax.rsqrt(jnp.mean(xv * xv, axis=-1, keepdims=True) + EPS) * g_ref[...])
        rows = 256
        for r0 in range(0, T, rows):
            htv[:, r0:r0 + rows] = hv[r0:r0 + rows, :].T
        locs.append(pltpu.make_async_copy(htv, ht_o, loc_sems.at[3]))
        locs[-1].start()

        def out_copy(p, j):
            return pltpu.make_async_copy(stage.at[p], proj_o.at[j], out_sems.at[p])

        def project(nth, block):
            base = 3 * index(*block)

            def tile(p, carry):
                if nth > 0:
                    out_copy(p, base + p).wait()
                stage[p] = _dot(hv[...], wv[base + p], 1, 0)
                out_copy(p, base + p).start()
                return carry

            lax.fori_loop(0, 3, tile, 0)

        project(0, me)
        copy(0, 0, sibling, me).wait_recv()
        save(sibling)
        project(1, sibling)
        passed = []
        relay_from = (px ^ (1 - c), py ^ c, c)
        relay_to = (px ^ c, py ^ (1 - c), c)

        def arrived(w, j):
            copy(1 + j, w, (*chips[j], c), me).wait_recv()
            passed.append(copy(4 + j, w, (*chips[j], c), sibling))
            passed[-1].start()

        def relay(w):
            passed.append(copy(3, w, relay_from, relay_to))
            passed[-1].start()

        def handed(nth, j):
            copy(4 + j, 0, (*chips[j], 1 - c), me).wait_recv()
            save((*chips[j], 1 - c))
            project(nth, (*chips[j], 1 - c))

        arrived(0, 0)
        arrived(0, 1)
        relay(0)
        for j in range(2):
            save((*chips[j], c))
            project(2 + j, (*chips[j], c))
        handed(4, 0)
        handed(5, 1)
        arrived(1, 0)
        arrived(1, 1)
        relay(1)
        arrived(0, 2)
        save((*chips[2], c))
        project(6, (*chips[2], c))
        handed(7, 2)
        arrived(1, 2)
        copy(0, 1, sibling, me).wait_recv()
        for j, chip in enumerate(chips):
            copy(4 + j, 1, (*chip, 1 - c), me).wait_recv()
        keep = pltpu.make_async_copy(wv, wt_o, loc_sems.at[4])
        for p in range(3):
            out_copy(p, p).wait()
        for cp in first + passed:
            cp.wait_send()
        keep.wait()
        for cp in locs:
            cp.wait()

    vmem = pl.BlockSpec(memory_space=pltpu.VMEM)
    bf16 = jnp.bfloat16
    return pl.pallas_call(
        body, name="gather_proj",
        out_shape=(pltpu.HBM((D, T), bf16), pltpu.HBM((NTILE, D, TILE), bf16),
                   pltpu.HBM((DMIX // NDEV, D), bf16), pltpu.HBM((NDEV, DMIX // NDEV, D), bf16),
                   pltpu.HBM((NDEV, NGROUP, GROUP // NDEV, GROUP), bf16), pltpu.HBM((NTILE, T, TILE), jnp.float32)),
        in_specs=[ANY] + [vmem] * 4, out_specs=(ANY,) * 6,
        scratch_shapes=[pltpu.VMEM((T, D), jnp.float32),
                        pltpu.VMEM((T, D), bf16), pltpu.VMEM((D, T), bf16), pltpu.VMEM((NTILE, D, TILE), bf16),
                        pltpu.VMEM((DMIX // NDEV, D), bf16), pltpu.VMEM((NGROUP, GROUP // NDEV, GROUP), bf16),
                        pltpu.VMEM((3, T, TILE), jnp.float32),
                        pltpu.SemaphoreType.DMA((14,)), pltpu.SemaphoreType.DMA((14,)),
                        pltpu.SemaphoreType.DMA((6,)), pltpu.SemaphoreType.DMA((3,))],
        compiler_params=_params(vmem_mib=56),
    )(x, g1, w_in, w_out, pool_w)


def _split_start(name, arrays, n_copies, plan):
    k = len(arrays)

    def body(*refs):
        send_sems, recv_sems, token = refs[k], refs[k + 1], refs[-1]
        for i, (src, dst, to) in enumerate(plan(refs[:k])):
            pltpu.make_async_remote_copy(src_ref=src, dst_ref=dst, send_sem=send_sems.at[i],
                                         recv_sem=recv_sems.at[i], device_id=to, device_id_type=MESH).start()
        token[...] = jnp.zeros_like(token)

    out = pl.pallas_call(
        body, name=name,
        out_shape=(pltpu.SemaphoreType.DMA((n_copies,)), pltpu.SemaphoreType.DMA((n_copies,)),
                   *[pltpu.HBM(a.shape, a.dtype) for a in arrays], jax.ShapeDtypeStruct((8, 128), jnp.float32)),
        in_specs=[HBM] * k, out_specs=(SEM, SEM, *[HBM] * k, pl.BlockSpec(memory_space=pltpu.VMEM)),
        input_output_aliases={i: 2 + i for i in range(k)},
        compiler_params=pltpu.CompilerParams(has_side_effects=EFFECT),
    )(*[pltpu.with_memory_space_constraint(a, pltpu.HBM) for a in arrays])
    return out[0], out[1], out[2:2 + k], out[-1]


def _split_wait(name, arrays, send_sems, recv_sems, plan, after):
    k = len(arrays)

    def body(*refs):
        sends, recvs = refs[k], refs[k + 1]
        for i, (src, dst, to) in enumerate(plan(refs[:k])):
            cp = pltpu.make_async_remote_copy(src_ref=src, dst_ref=dst, send_sem=sends.at[i], recv_sem=recvs.at[i],
                                              device_id=to, device_id_type=MESH)
            cp.wait_send()
            cp.wait_recv()

    return pl.pallas_call(
        body, name=name,
        out_shape=tuple(pltpu.HBM(a.shape, a.dtype) for a in arrays),
        in_specs=[HBM] * k + [SEM, SEM, ANY], out_specs=(HBM,) * k,
        input_output_aliases={i: i for i in range(k)},
        compiler_params=pltpu.CompilerParams(has_side_effects=EFFECT),
    )(*arrays, send_sems, recv_sems, after)


def _plan_wout(refs):
    src, land = refs
    x, y, c, me = _place()
    return [(src, land.at[me], _peer(x, y, c, r)) for r in range(1, NDEV)]


def _plan_rest(refs):
    gob, gpf, r_out, r_pool = refs
    x, y, c, me = _place()
    plan = []
    for r in range(1, NDEV):
        plan.append((gob.at[me ^ r], r_out.at[r - 1], _peer(x, y, c, r)))
        plan.append((gpf.at[me ^ r], r_pool.at[r - 1], _peer(x, y, c, r)))
    return plan


def _plan_in(refs):
    sum_b, r_in = refs
    x, y, c, _ = _place()
    plan = []
    for j, (dx, dy) in enumerate(((1, 0), (0, 1), (1, 1))):
        px, py = x ^ dx, y ^ dy
        plan.append((sum_b.at[2 * px + py], r_in.at[j], (px, py, c)))
    return plan


def _gather_small(small, *after):
    def body(sm, *refs):
        r_small, send_sems, recv_sems, loc_sem = refs[len(after):]
        x, y, c, me = _place()
        loc = pltpu.make_async_copy(sm, r_small.at[me], loc_sem)
        loc.start()

        def copy(r, src_idx):
            return pltpu.make_async_remote_copy(
                src_ref=sm, dst_ref=r_small.at[src_idx], send_sem=send_sems.at[r - 1], recv_sem=recv_sems.at[r - 1],
                device_id=_peer(x, y, c, r), device_id_type=MESH)

        sends = [copy(r, me) for r in range(1, NDEV)]
        for cp in sends:
            cp.start()
        for r in range(1, NDEV):
            copy(r, me ^ r).wait_recv()
        for cp in sends:
            cp.wait_send()
        loc.wait()

    return pl.pallas_call(
        body, name="gather_small",
        out_shape=jax.ShapeDtypeStruct((NDEV, 8, D), jnp.float32),
        in_specs=[ANY] * (1 + len(after)), out_specs=ANY,
        scratch_shapes=[pltpu.SemaphoreType.DMA((NDEV - 1,)), pltpu.SemaphoreType.DMA((NDEV - 1,)),
                        pltpu.SemaphoreType.DMA],
    )(small, *after)


def _seg_tiles(s):
    return (s + 2) % NSEG


_POOL_SPECS = [pl.BlockSpec((None, T, GROUP), lambda g, base=base: (base + g, 0, 0)) for base in (0, 4)]
_HEAD_SPECS = [pl.BlockSpec((None, T, HEAD), lambda h, base=base: (base + h // 2, 0, h % 2))
               for base in (8, 12, 16, 20)]


def _row_ids(shape):
    return lax.broadcasted_iota(jnp.int32, shape, 0)


BAND_ROWS = 128
HALO = 16


def _window_sum(a, gidx, lead):
    width = lax.shift_left(jnp.int32(2), gidx)
    shape = (BAND_ROWS, BAND_ROWS + HALO)
    t, j = lax.broadcasted_iota(jnp.int32, shape, 0), lax.broadcasted_iota(jnp.int32, shape, 1)
    first = t if lead else t + HALO - width + 1
    band = _bf(jnp.where(j >= first, jnp.where(j < first + width, 1.0, 0.0), 0.0))
    zeros = jnp.zeros((HALO, a.shape[1]), jnp.bfloat16)
    padded = [jnp.concatenate([p, zeros] if lead else [zeros, p], axis=0) for p in _split2(a)]
    out = []
    for r0 in range(0, T, BAND_ROWS):
        slab = jnp.concatenate([p[r0:r0 + BAND_ROWS + HALO] for p in padded], axis=1)
        r = _dot(band, slab, 1, 0)
        out.append(r[:, :a.shape[1]] + r[:, a.shape[1]:])
    return jnp.concatenate(out, axis=0)


def _window_mean(s, gidx):
    inv = jnp.where(gidx == 0, 0.5, jnp.where(gidx == 1, 0.25, jnp.where(gidx == 2, 0.125, 0.0625)))
    width = lax.shift_left(jnp.int32(2), gidx)
    head = s[:16] / jnp.minimum(_row_ids((16, s.shape[1])) + 1, width).astype(jnp.float32)
    return jnp.concatenate([head, s[16:] * inv], axis=0)


def _pool_fwd(proj, pool_w, pool_scale, token):
    def body(u_ref, pg_ref, w_ref, sc_ref, token_any, y_ref):
        del token_any
        gidx = pl.program_id(0)
        u, pg = u_ref[...], pg_ref[...]
        d = _window_mean(_window_sum(u, gidx, False), gidx) - u
        mixed = _dot(_bf(d), w_ref[...], 1, 0)
        y_ref[...] = _bf(mixed * sc_ref[...] * (pg * _sigmoid(pg)))

    return pl.pallas_call(
        body, name="pool_fwd", grid=(NGROUP,),
        in_specs=[*_POOL_SPECS,
                  pl.BlockSpec((None, GROUP, GROUP), lambda g: (g, 0, 0)),
                  pl.BlockSpec((1, GROUP), lambda g: (0, g)), ANY],
        out_specs=pl.BlockSpec((T, GROUP), lambda g: (0, g)),
        out_shape=pltpu.HBM((T, DMIX), jnp.bfloat16),
        compiler_params=_params(("parallel",)),
    )(proj, proj, pool_w, pool_scale, token)


def _tri(lower):
    r = lax.broadcasted_iota(jnp.int32, (CHUNK, CHUNK), 0)
    c = lax.broadcasted_iota(jnp.int32, (CHUNK, CHUNK), 1)
    return (r >= c) if lower else (r <= c)


def _sum_rows_matrix():
    shape = (CHUNK + 16, CHUNK)
    r, c = lax.broadcasted_iota(jnp.int32, shape, 0), lax.broadcasted_iota(jnp.int32, shape, 1)
    run = jnp.where(c <= r, 1.0, 0.0)
    half = jnp.where(c < CHUNK // 2, 1.0, 0.0)
    return _bf(jnp.where(r < CHUNK, run, jnp.where(r < CHUNK + 8, 1.0, half)))


def _rev_sum_matrix():
    shape = (CHUNK, 2 * CHUNK)
    r, c = lax.broadcasted_iota(jnp.int32, shape, 0), lax.broadcasted_iota(jnp.int32, shape, 1)
    return _bf(jnp.where(c < CHUNK, jnp.where(c >= r, 1.0, 0.0), jnp.where(c - CHUNK < r, 1.0, 0.0)))


def _split2(a):
    hi = _bf(a)
    return [hi, _bf(a - hi.astype(jnp.float32))]


def _exact_sums(mat, pieces):
    x = jnp.concatenate([s for p in pieces for s in _split2(p)], axis=1)
    r = _dot(mat, x, 1, 0)
    return [r[:, 2 * j * HEAD:(2 * j + 1) * HEAD] + r[:, (2 * j + 1) * HEAD:(2 * j + 2) * HEAD]
            for j in range(len(pieces))]


def _gates(qv, fl, lb):
    sq = _sigmoid(qv)
    sg = _sigmoid(fl)
    f = lb + (1.0 - lb) * sg
    return dict(sq=sq, qs=qv * sq, sg=sg, f=f, kk=1.0 - f, g=jnp.log2(f))


def _decays(sums):
    big_g = sums[:CHUNK]
    total = sums[CHUNK:CHUNK + 8]
    g_last = jnp.tile(total, (CHUNK // 8, 1))
    g_mid = jnp.tile(sums[CHUNK + 8:], (CHUNK // 8, 1))
    return dict(
        e_q=jnp.exp2(big_g),
        e_k=jnp.exp2(g_last - big_g),
        e_qm=jnp.exp2(jnp.minimum(big_g - g_mid, EXP_CAP)),
        e_km=jnp.exp2(jnp.minimum(g_mid - big_g, EXP_CAP)),
        total8=jnp.exp2(total))


def _group_rows(gi):
    return [pl.ds(pl.multiple_of((gi * NB + j) * CHUNK, CHUNK), CHUNK) for j in range(NB)]


def _lower_bound(lb_ref):
    return _sigmoid(lb_ref[0:1, :] - lb_ref[1:2, :])


def _hgrn_fwd(proj, lb_logits, rec_g, y_in):
    def body(q_ref, f_ref, i_ref, gate_ref, lb_ref, rg_ref, y_any, y_ref, o_ref, st_ref):
        del y_any
        lb = _lower_bound(lb_ref)
        causal = _tri(True)
        smat = _sum_rows_matrix()

        def group(gi, st):
            rows = _group_rows(gi)
            ts = [_gates(q_ref[r, :], f_ref[r, :], lb) for r in rows]
            ds = [_decays(s) for s in _exact_sums(smat, [t["g"] for t in ts])]
            vs = [_bf(i_ref[r, :]) for r in rows]
            q_m = [_bf(t["qs"] * d["e_qm"]) for t, d in zip(ts, ds)]
            k_m = [_bf(t["kk"] * d["e_km"]) for t, d in zip(ts, ds)]
            q_e = [_bf(t["qs"] * d["e_q"]) for t, d in zip(ts, ds)]
            k_e = [_bf(t["kk"] * d["e_k"]) for t, d in zip(ts, ds)]
            a = [_bf(jnp.where(causal, _dot(q_m[j], k_m[j], 1, 1), 0.0)) for j in range(NB)]
            intra = [_dot(a[j], vs[j], 1, 0) for j in range(NB)]
            upd = [_dot(vs[j], k_e[j], 0, 0) for j in range(NB)]
            for j in range(NB):
                st_ref[gi * NB + j] = st
                o_ref[rows[j], :] = intra[j] + _dot(q_e[j], _bf(st), 1, 1)
                st = st * jnp.tile(ds[j]["total8"], (HEAD // 8, 1)) + upd[j]
            return st

        lax.fori_loop(0, NGRP, group, jnp.zeros((HEAD, HEAD), jnp.float32))
        o = o_ref[...]
        rn = o * lax.rsqrt(jnp.mean(o * o, axis=-1, keepdims=True) + EPS)
        gate = gate_ref[...]
        y_ref[...] = _bf(rn * rg_ref[...] * (gate * _sigmoid(gate)))

    return pl.pallas_call(
        body, name="hgrn_fwd", grid=(NHEAD,),
        in_specs=[*_HEAD_SPECS,
                  pl.BlockSpec((2, HEAD), lambda h: (0, h)),
                  pl.BlockSpec((1, HEAD), lambda h: (0, h)),
                  pl.BlockSpec(memory_space=pl.ANY)],
        out_specs=(pl.BlockSpec((T, HEAD), lambda h: (0, NHEAD + h)),
                   pl.BlockSpec((T, HEAD), lambda h: (0, h)),
                   pl.BlockSpec((None, NCHUNK, HEAD, HEAD), lambda h: (h, 0, 0, 0))),
        out_shape=(pltpu.HBM((T, DMIX), jnp.bfloat16), pltpu.HBM((T, D), jnp.float32),
                   pltpu.HBM((NHEAD, NCHUNK, HEAD, HEAD), jnp.float32)),
        input_output_aliases={6: 0},
        compiler_params=_params(("parallel",)),
    )(proj, proj, proj, proj, lb_logits, rec_g, y_in)


def _out_proj_loss(x, y, w_out, target, gf):
    rows = 512
    parts = [slice(k * rows // 2, (k + 1) * rows // 2) for k in range(2)]

    def body(x_ref, y_ref, w_ref, t_ref, g_ref, dz_ref, dzb_ref, sq_ref, dg_ref):
        zs = [x_ref[p, :] + _dot(y_ref[p, :], w_ref[...], 1, 0) for p in parts]
        sq = dg = 0.0
        for p, z in zip(parts, zs):
            r = lax.rsqrt(jnp.mean(z * z, axis=-1, keepdims=True) + EPS)
            zhat = z * r
            err = zhat * g_ref[...] - t_ref[p, :]
            dy = err * (1.0 / D)
            gdy = dy * g_ref[...]
            dz = r * (gdy - zhat * jnp.mean(zhat * gdy, axis=-1, keepdims=True))
            dz_ref[p, :] = dz
            dzb_ref[p, :] = _bf(dz)
            sq = sq + jnp.sum(err * err, axis=0, keepdims=True)
            dg = dg + jnp.sum(zhat * dy, axis=0, keepdims=True)

        @pl.when(pl.program_id(0) == 0)
        def _():
            sq_ref[...] = sq
            dg_ref[...] = dg

        @pl.when(pl.program_id(0) != 0)
        def _():
            sq_ref[...] += sq
            dg_ref[...] += dg

    tile = pl.BlockSpec((rows, D), lambda i: (i, 0))
    vec = pl.BlockSpec((1, D), lambda i: (0, 0))
    return pl.pallas_call(
        body, name="out_proj_loss", grid=(T // rows,),
        in_specs=[tile, pl.BlockSpec((rows, DMIX), lambda i: (i, 0)), pl.BlockSpec((DMIX, D), lambda i: (0, 0)),
                  tile, vec],
        out_specs=(tile, tile, vec, vec),
        out_shape=(pltpu.HBM((T, D), jnp.float32), pltpu.HBM((T, D), jnp.bfloat16),
                   jax.ShapeDtypeStruct((1, D), jnp.float32), jax.ShapeDtypeStruct((1, D), jnp.float32)),
        compiler_params=_params(("arbitrary",)),
    )(x, y, w_out, target, gf)


def _out_proj_bwd(dzb, w_out, y):
    tn = 512

    def body(dz_ref, w_ref, y_ref, dy_ref, gw_ref, gwb_ref):
        dz = dz_ref[...]
        dy_ref[...] = _dot(dz, w_ref[...], 1, 1)
        gw = _dot(y_ref[...], dz, 0, 0)
        gw_ref[...] = gw
        gwb_ref[...] = _bf(gw)

    return pl.pallas_call(
        body, name="out_proj_bwd", grid=(DMIX // tn,),
        in_specs=[pl.BlockSpec((T, D), lambda n: (0, 0)), pl.BlockSpec((tn, D), lambda n: (n, 0)),
                  pl.BlockSpec((T, tn), lambda n: (0, n))],
        out_specs=(pl.BlockSpec((T, tn), lambda n: (0, n)), pl.BlockSpec((tn, D), lambda n: (n, 0)),
                   pl.BlockSpec((tn, D), lambda n: (n, 0))),
        out_shape=(pltpu.HBM((T, DMIX), jnp.float32), pltpu.HBM((DMIX, D), jnp.float32),
                   pltpu.HBM((DMIX, D), jnp.bfloat16)),
        compiler_params=_params(("parallel",)),
    )(dzb, w_out, y)


def _hgrn_bwd(proj, lb_logits, rec_g, o, states, dymix, dproj_in, token):
    def body(q_ref, f_ref, i_ref, gate_ref, lb_ref, rg_ref, o_ref, st_ref, dy_ref, dp_any, token_any,
             dp_ref, drg_ref, dlb_ref, do_ref):
        del dp_any, token_any
        lb = _lower_bound(lb_ref)
        causal = _tri(True)
        smat, rmat = _sum_rows_matrix(), _rev_sum_matrix()

        o = o_ref[...]
        rs = lax.rsqrt(jnp.mean(o * o, axis=-1, keepdims=True) + EPS)
        rn = o * rs
        gate = gate_ref[...]
        sgate = _sigmoid(gate)
        dyv = dy_ref[...]
        d_r = dyv * (gate * sgate)
        dp_ref[3] = _bf(dyv * (rn * rg_ref[...]) * (sgate * (1.0 + gate * (1.0 - sgate))))
        drg_ref[...] = jnp.sum(d_r * rn, axis=0, keepdims=True)
        drn = d_r * rg_ref[...]
        do_ref[...] = rs * (drn - rn * jnp.mean(rn * drn, axis=-1, keepdims=True))

        def group(i, carry):
            dst, dlb = carry
            gi = NGRP - 1 - i
            rows = _group_rows(gi)
            span = range(NB)
            qvs = [q_ref[r, :] for r in rows]
            ts = [_gates(qv, f_ref[r, :], lb) for qv, r in zip(qvs, rows)]
            ds = [_decays(s) for s in _exact_sums(smat, [t["g"] for t in ts])]
            vs = [_bf(i_ref[r, :]) for r in rows]
            dos = [_bf(do_ref[r, :]) for r in rows]
            sts = [st_ref[gi * NB + j] for j in span]
            qe_f = [t["qs"] * d["e_q"] for t, d in zip(ts, ds)]
            ke_f = [t["kk"] * d["e_k"] for t, d in zip(ts, ds)]
            q_e, k_e = [_bf(a) for a in qe_f], [_bf(a) for a in ke_f]
            q_m = [_bf(t["qs"] * d["e_qm"]) for t, d in zip(ts, ds)]
            k_m = [_bf(t["kk"] * d["e_km"]) for t, d in zip(ts, ds)]
            a = [_bf(jnp.where(causal, _dot(q_m[j], k_m[j], 1, 1), 0.0)) for j in span]
            da = [_bf(jnp.where(causal, _dot(dos[j], vs[j], 1, 1), 0.0)) for j in span]
            dqm = [_dot(da[j], k_m[j], 1, 0) for j in span]
            dkm = [_dot(da[j], q_m[j], 0, 0) for j in span]
            dv_in = [_dot(a[j], dos[j], 0, 0) for j in span]
            dqe = [_dot(dos[j], _bf(sts[j]), 1, 0) for j in span]
            grow = [_dot(dos[j], q_e[j], 0, 0) for j in span]
            dke, carried = [None] * NB, [None] * NB
            for j in reversed(span):
                dst_b = _bf(dst)
                dke[j] = _dot(vs[j], dst_b, 1, 0)
                dp_ref[2, rows[j], :] = _bf(dv_in[j] + _dot(k_e[j], dst_b, 1, 1))
                carried[j] = ds[j]["total8"] * jnp.sum(dst * sts[j], axis=0, keepdims=True)
                dst = dst * jnp.tile(ds[j]["total8"], (HEAD // 8, 1)) + grow[j]
            kdk = [ke_f[j] * dke[j] for j in span]
            pos = [(q_m[j].astype(jnp.float32) * dqm[j] - k_m[j].astype(jnp.float32) * dkm[j]) + qe_f[j] * dqe[j]
                   for j in span]
            dgs = _exact_sums(rmat, [jnp.concatenate([pos[j], kdk[j]], axis=0) for j in span])
            for j in span:
                t, d = ts[j], ds[j]
                dg = dgs[j] + jnp.tile(carried[j], (CHUNK // 8, 1))
                dqs = dqm[j] * d["e_qm"] + dqe[j] * d["e_q"]
                dkk = dkm[j] * d["e_km"] + dke[j] * d["e_k"]
                df = dg / t["f"] - dkk
                dp_ref[1, rows[j], :] = _bf(df * (1.0 - lb) * (t["sg"] * (1.0 - t["sg"])))
                dp_ref[0, rows[j], :] = _bf(dqs * (t["sq"] * (1.0 + qvs[j] * (1.0 - t["sq"]))))
                dlb = dlb + df * (1.0 - t["sg"])
            return dst, dlb

        _, dlb = lax.fori_loop(0, NGRP, group, (jnp.zeros((HEAD, HEAD), jnp.float32),
                                                jnp.zeros((CHUNK, HEAD), jnp.float32)))
        dlb_ref[...] = jnp.sum(dlb, axis=0, keepdims=True)

    vec = pl.BlockSpec((1, HEAD), lambda h: (0, h))
    return pl.pallas_call(
        body, name="hgrn_bwd", grid=(NHEAD,),
        in_specs=[*_HEAD_SPECS,
                  pl.BlockSpec((2, HEAD), lambda h: (0, h)), vec,
                  pl.BlockSpec((T, HEAD), lambda h: (0, h)),
                  pl.BlockSpec((None, NCHUNK, HEAD, HEAD), lambda h: (h, 0, 0, 0)),
                  pl.BlockSpec((T, HEAD), lambda h: (0, NHEAD + h)), ANY, ANY],
        out_specs=(pl.BlockSpec((4, T, HEAD), lambda h: (0, 0, h)), vec, vec),
        out_shape=(pltpu.HBM((NSEG, T, D), jnp.bfloat16),
                   jax.ShapeDtypeStruct((1, D), jnp.float32), jax.ShapeDtypeStruct((1, D), jnp.float32)),
        scratch_shapes=[pltpu.VMEM((T, HEAD), jnp.float32)],
        input_output_aliases={9: 0},
        compiler_params=_params(("parallel",)),
    )(proj, proj, proj, proj, lb_logits, rec_g, o, states, dymix, dproj_in, token)


def _pool_bwd(proj, pool_w, pool_scale, dymix):
    def body(u_ref, pg_ref, w_ref, sc_ref, dy_ref, dp_ref, gw_ref, gs_ref):
        gidx = pl.program_id(0)
        u, pg = u_ref[...], pg_ref[...]
        d = _bf(_window_mean(_window_sum(u, gidx, False), gidx) - u)
        mixed = _dot(d, w_ref[...], 1, 0)
        spg = _sigmoid(pg)
        dyv = dy_ref[...]
        d_p = dyv * (pg * spg)
        dp_ref[1] = _bf(dyv * (mixed * sc_ref[...]) * (spg * (1.0 + pg * (1.0 - spg))))
        gs_ref[...] = jnp.sum(d_p * mixed, axis=0, keepdims=True)
        dmixed = _bf(d_p * sc_ref[...])
        gw_ref[...] = _dot(d, dmixed, 0, 0)
        dd = _dot(dmixed, w_ref[...], 1, 1)
        dp_ref[0] = _bf(_window_sum(_window_mean(dd, gidx), gidx, True) - dd)

    return pl.pallas_call(
        body, name="pool_bwd", grid=(NGROUP,),
        in_specs=[*_POOL_SPECS,
                  pl.BlockSpec((None, GROUP, GROUP), lambda g: (g, 0, 0)),
                  pl.BlockSpec((1, GROUP), lambda g: (0, g)),
                  pl.BlockSpec((T, GROUP), lambda g: (0, g))],
        out_specs=(pl.BlockSpec((2, T, GROUP), lambda g: (2, 0, g)),
                   pl.BlockSpec((None, GROUP, GROUP), lambda g: (g, 0, 0)),
                   pl.BlockSpec((1, GROUP), lambda g: (0, g))),
        out_shape=(pltpu.HBM((NSEG, T, D), jnp.bfloat16),
                   jax.ShapeDtypeStruct((NGROUP, GROUP, GROUP), jnp.float32),
                   jax.ShapeDtypeStruct((1, D), jnp.float32)),
        compiler_params=_params(("parallel",)),
    )(proj, proj, pool_w, pool_scale, dymix)


def _proj_bwd_w(place, ht, dproj):
    half = NTILE // 2

    def owner_chip(i, pr):
        return jnp.where(i < half, i // 3, (pr[1] + 1 + (i - half) // 3) % 4)

    def tile_of(i, pr):
        side = jnp.where(i < half, 1 - pr[2], pr[2])
        return 6 * owner_chip(i, pr) + 3 * side + i % 3

    def dproj_block(i, pr):
        j = tile_of(i, pr)
        return ((j // 4 + 4) % NSEG, 0, j % 4)

    def mine(i):
        return jnp.maximum(i, half)

    def body(place_ref, h_ref, dp_ref, sum_ref, own_ref, sendbuf, recvbuf, send_sems, recv_sems):
        i = pl.program_id(0)
        px, py, c, _ = _place()
        gw = _dot(h_ref[...], dp_ref[...], 1, 0)

        def to_sibling(slot):
            return pltpu.make_async_remote_copy(
                src_ref=sendbuf.at[slot], dst_ref=recvbuf.at[slot], send_sem=send_sems.at[slot],
                recv_sem=recv_sems.at[slot], device_id=(px, py, 1 - c), device_id_type=MESH)

        @pl.when(i < half)
        def _():
            sendbuf[i] = _bf(gw)
            to_sibling(i).start()

        @pl.when(i >= half)
        def _():
            slot = 3 * owner_chip(i, place_ref) + i % 3
            to_sibling(slot).wait_recv()
            total = gw + recvbuf[slot].astype(jnp.float32)
            sum_ref[...] = _bf(total)
            own_ref[...] = total

        @pl.when(i == NTILE - 1)
        def _():
            for slot in range(half):
                to_sibling(slot).wait_send()

    return pl.pallas_call(
        body, name="proj_bwd_w",
        grid_spec=pltpu.PrefetchScalarGridSpec(
            num_scalar_prefetch=1, grid=(NTILE,),
            in_specs=[pl.BlockSpec((D, T), lambda i, pr: (0, 0)),
                      pl.BlockSpec((None, T, TILE), lambda i, pr: dproj_block(i, pr))],
            out_specs=(pl.BlockSpec((None, None, D, TILE), lambda i, pr: (owner_chip(mine(i), pr), mine(i) % 3, 0, 0)),
                       pl.BlockSpec((None, D, TILE), lambda i, pr: (jnp.where(i < NTILE - 3, 0, i % 3), 0, 0))),
            scratch_shapes=[pltpu.VMEM((half, D, TILE), jnp.bfloat16), pltpu.VMEM((half, D, TILE), jnp.bfloat16),
                            pltpu.SemaphoreType.DMA((half,)), pltpu.SemaphoreType.DMA((half,))]),
        out_shape=(pltpu.HBM((4, 3, D, TILE), jnp.bfloat16), pltpu.HBM((3, D, TILE), jnp.float32)),
        compiler_params=_params(("arbitrary",)),
    )(place, ht, dproj)


def _proj_bwd_x(dproj, w_t, x, g1, dz, token):
    tm = 512
    pairs = NSEG // 2

    def body(dp_ref, w_ref, x_ref, g_ref, dz_ref, token_any, dx_ref, dg_ref, wcat, acc):
        del token_any
        m, s = pl.program_id(0), pl.program_id(1)
        r = None
        for k in range(2):
            for i in range(4):
                wcat[k, :, i * TILE:(i + 1) * TILE] = w_ref[4 * k + i]
            part = _dot(dp_ref[k], wcat[k], 1, 1)
            r = part if r is None else r + part

        @pl.when(s == 0)
        def _():
            acc[...] = r

        @pl.when(s != 0)
        def _():
            acc[...] += r

        @pl.when(s == pairs - 1)
        def _():
            xv = x_ref[...]
            rs = lax.rsqrt(jnp.mean(xv * xv, axis=-1, keepdims=True) + EPS)
            xhat = xv * rs
            dhv = acc[...]
            gdh = dhv * g_ref[...]
            dx_ref[...] = dz_ref[...] + rs * (gdh - xhat * jnp.mean(xhat * gdh, axis=-1, keepdims=True))
            dg = jnp.sum(xhat * dhv, axis=0, keepdims=True)

            @pl.when(m == 0)
            def _():
                dg_ref[...] = dg

            @pl.when(m != 0)
            def _():
                dg_ref[...] += dg

    rows = pl.BlockSpec((tm, D), lambda m, s: (m, 0))
    vec = pl.BlockSpec((1, D), lambda m, s: (0, 0))
    return pl.pallas_call(
        body, name="proj_bwd_x", grid=(T // tm, pairs),
        in_specs=[pl.BlockSpec((2, tm, D), lambda m, s: (s, m, 0)),
                  pl.BlockSpec((8, D, TILE), lambda m, s: ((s + 1) % pairs, 0, 0)), rows, vec, rows, ANY],
        out_specs=(rows, vec),
        out_shape=(jax.ShapeDtypeStruct((T, D), jnp.float32), jax.ShapeDtypeStruct((1, D), jnp.float32)),
        scratch_shapes=[pltpu.VMEM((2, D, D), jnp.bfloat16), pltpu.VMEM((tm, D), jnp.float32)],
        compiler_params=_params(("arbitrary", "arbitrary"), vmem_mib=56),
    )(dproj, w_t, x, g1, dz, token)


def _adamw(w, g, m, v):
    m_new = ADAM_B1 * m + (1.0 - ADAM_B1) * g
    v_new = ADAM_B2 * v + (1.0 - ADAM_B2) * (g * g)
    delta = -ADAM_LR * ((m_new / BC1) / (jnp.sqrt(v_new / BC2) + ADAM_EPS) + ADAM_WD * w)
    return delta, m_new, v_new


def _reduce_adam(name, place, parts, w, m, v, grid, w_spec):
    n = len(parts)

    def body(place_ref, *refs):
        del place_ref
        w_ref, m_ref, v_ref, g_ref, d_ref, mo_ref, vo_ref = refs[n:]
        g = None
        for ref, (_, _, stacked) in zip(refs[:n], parts):
            terms = [ref[r] for r in range(ref.shape[0])] if stacked else [ref[...]]
            for t in terms:
                if t.shape[-1] != w_ref.shape[-1]:
                    t = jnp.concatenate([t[p] for p in range(t.shape[0])], axis=1)
                g = t.astype(jnp.float32) if g is None else g + t.astype(jnp.float32)
        delta, m_new, v_new = _adamw(w_ref[...], g, m_ref[...], v_ref[...])
        g_ref[...] = g
        d_ref[...] = delta
        mo_ref[...] = m_new
        vo_ref[...] = v_new

    shape = jax.ShapeDtypeStruct(w.shape, jnp.float32)
    return pl.pallas_call(
        body, name=name,
        grid_spec=pltpu.PrefetchScalarGridSpec(
            num_scalar_prefetch=1, grid=grid,
            in_specs=[spec for _, spec, _ in parts] + [w_spec] * 3, out_specs=(w_spec,) * 4),
        out_shape=(shape,) * 4,
        compiler_params=_params(("parallel",)),
    )(place, *[_in_hbm(a) for a in [a for a, _, _ in parts] + [w, m, v]])


def _small_adam(parts, w, m, v):
    def body(p_ref, w_ref, m_ref, v_ref, g_ref, d_ref, mo_ref, vo_ref):
        g = p_ref[0]
        for s in range(1, NDEV):
            g = g + p_ref[s]
        wv = w_ref[...]
        rows = _row_ids(wv.shape)
        other = jnp.where(rows == 2, pltpu.roll(wv, 7, 0), jnp.where(rows == 3, pltpu.roll(wv, 1, 0), 0.0))
        lbv = _sigmoid(wv - other)
        sign = jnp.where(rows == 2, 1.0, -1.0)
        g = jnp.where((rows == 2) | (rows == 3), sign * g * lbv * (1.0 - lbv), g)
        delta, m_new, v_new = _adamw(wv, g, m_ref[...], v_ref[...])
        g_ref[...] = g
        d_ref[...] = delta
        mo_ref[...] = m_new
        vo_ref[...] = v_new

    shape = jax.ShapeDtypeStruct((8, D), jnp.float32)
    return pl.pallas_call(body, name="small_adam", out_shape=(shape,) * 4)(parts, w, m, v)


def _rows8(*vecs):
    rows = [a.reshape(-1, D) for a in vecs]
    n = sum(r.shape[0] for r in rows)
    return jnp.concatenate(rows + [jnp.zeros((8 - n, D), jnp.float32)], axis=0)


def kernel(x, norm1_g, w_in, pool_w, pool_scale, lb_logits, rec_norm_g, w_out, final_norm_g, loss_target, m_norm1_g, m_w_in, m_pool_w, m_pool_scale, m_lb_logits, m_rec_norm_g, m_w_out, m_final_norm_g, v_norm1_g, v_w_in, v_pool_w, v_pool_scale, v_lb_logits, v_rec_norm_g, v_w_out, v_final_norm_g):
    xs = x[0]
    target = loss_target[0]
    ix, iy, ic = lax.axis_index("x"), lax.axis_index("y"), lax.axis_index("c")
    place = jnp.stack([4 * ix + 2 * iy + ic, 2 * ix + iy, ic]).astype(jnp.int32)
    gf = final_norm_g.reshape(1, D)

    ht, w_t, w_out_b, w_out_g, pool_g, proj = _gather_proj(xs, norm1_g, w_in, w_out, pool_w)
    pool_full = pool_g.transpose(1, 0, 2, 3).reshape(NGROUP, GROUP, GROUP)
    wout = [w_out_b, w_out_g]
    wout_send, wout_recv, wout, wout_token = _split_start("gather_wout_start", wout, NDEV - 1, _plan_wout)

    y = _pool_fwd(proj, pool_full, pool_scale, wout_token)
    y, o, states = _hgrn_fwd(proj, lb_logits, rec_norm_g, y)
    _, w_out_g = _split_wait("gather_wout_wait", wout, wout_send, wout_recv, _plan_wout, o)
    w_out_full = _in_hbm(w_out_g.reshape(DMIX, D))
    dz, dzb, sq, dgf = _out_proj_loss(xs, y, w_out_full, target, gf)

    dymix, gwout_f, gwout_b = _out_proj_bwd(dzb, w_out_full, y)
    dproj, gpool, dscale = _pool_bwd(proj, pool_full, pool_scale, dymix)

    blk_out = (NDEV, DMIX // NDEV, D)
    blk_pool = (NDEV, NGROUP, GROUP // NDEV, GROUP)
    gpool_s = gpool.reshape(NGROUP, NDEV, GROUP // NDEV, GROUP).transpose(1, 0, 2, 3)
    rest = [gwout_b.reshape(blk_out), gpool_s,
            lax.empty((NDEV - 1,) + blk_out[1:], jnp.bfloat16), lax.empty((NDEV - 1,) + blk_pool[1:], jnp.float32)]
    rest_send, rest_recv, rest, rest_token = _split_start("scatter_rest_start", rest, 2 * (NDEV - 1), _plan_rest)

    dproj, drecg, dlb = _hgrn_bwd(proj, lb_logits, rec_norm_g, o, states, dymix, dproj, rest_token)
    chip_sums, own_sum = _proj_bwd_w(place, ht, dproj)
    win = [chip_sums, lax.empty((3, 3, D, TILE), jnp.bfloat16)]
    win_send, win_recv, win, win_token = _split_start("scatter_win_start", win, 3, _plan_in)

    grad_x, dg1 = _proj_bwd_x(dproj, w_t, xs, norm1_g, dz, win_token)

    _, gpool_own, r_out, r_pool = _split_wait("scatter_rest_wait", rest, rest_send, rest_recv, _plan_rest, grad_x)
    g_wout, d_wout, m_wout, v_wout = _reduce_adam(
        "adam_w_out", place,
        [(gwout_f.reshape(blk_out), pl.BlockSpec((None,) + blk_out[1:], lambda i, pr: (pr[0], 0, 0)), False),
         (r_out, pl.BlockSpec((NDEV - 1,) + blk_out[1:], lambda i, pr: (0, 0, 0)), True)],
        w_out, m_w_out, v_w_out, (1,), pl.BlockSpec((None,) + blk_out[1:], lambda i, pr: (0, 0, 0)))
    g_pool, d_pool, m_pool, v_pool = _reduce_adam(
        "adam_pool_w", place,
        [(gpool_own, pl.BlockSpec((None,) + blk_pool[1:], lambda i, pr: (pr[0], 0, 0, 0)), False),
         (r_pool, pl.BlockSpec((NDEV - 1,) + blk_pool[1:], lambda i, pr: (0, 0, 0, 0)), True)],
        pool_w, m_pool_w, v_pool_w, (1,), pl.BlockSpec((None,) + blk_pool[1:], lambda i, pr: (0, 0, 0, 0)))

    r_small = _gather_small(_rows8(dg1, dscale, dlb, dlb, drecg, dgf, sq), d_wout, d_pool)
    g_s, d_s, m_s, v_s = _small_adam(
        r_small,
        _rows8(norm1_g, pool_scale, lb_logits, rec_norm_g, final_norm_g),
        _rows8(m_norm1_g, m_pool_scale, m_lb_logits, m_rec_norm_g, m_final_norm_g),
        _rows8(v_norm1_g, v_pool_scale, v_lb_logits, v_rec_norm_g, v_final_norm_g))
    loss = jnp.sum(g_s[6]) * (0.5 / D)

    _, r_in = _split_wait("scatter_win_wait", win, win_send, win_recv, _plan_in, v_s)
    g_win, d_win, m_win, v_win = _reduce_adam(
        "adam_w_in", place,
        [(own_sum, pl.BlockSpec((3, D // 8, TILE), lambda i, pr: (0, i, 0)), False),
         (r_in, pl.BlockSpec((3, 3, D // 8, TILE), lambda i, pr: (0, 0, i, 0)), True)],
        w_in, m_w_in, v_w_in, (8,), pl.BlockSpec((None, D // 8, 3 * TILE), lambda i, pr: (0, i, 0)))

    def small_outs(a):
        return a[0:1], a[1:2], a[2:4], a[4:5], a[5]

    def outs(small_a, win, pool, wout):
        n1, ps, lbl, rg, fg = small_outs(small_a)
        return n1, win, pool, ps, lbl, rg, wout, fg

    return (loss, grad_x[None],
            *outs(g_s, g_win, g_pool, g_wout), *outs(d_s, d_win, d_pool, d_wout),
            *outs(m_s, m_win, m_pool, m_wout), *outs(v_s, v_win, v_pool, v_wout))
```

```python
import functools

import jax
import jax.numpy as jnp
from jax import lax
from jax.experimental import pallas as pl
from jax.experimental.pallas import tpu as pltpu

T = 2048
D = 1024
NSEG = 6
NTILE = 24
TILE = 256
DMIX = 2048
NDEV = 8
HEAD = 128
NHEAD = 8
CHUNK = 64
NCHUNK = T // CHUNK
NB = 32
NGRP = NCHUNK // NB
NGROUP = 4
GROUP = 256
EPS = 1e-6
EXP_CAP = 115.0
MESH = pl.DeviceIdType.MESH
AXES = ("x", "y", "c")
ANY = pl.BlockSpec(memory_space=pl.ANY)
HBM = pl.BlockSpec(memory_space=pltpu.HBM)
SEM = pl.BlockSpec(memory_space=pltpu.SEMAPHORE)
EFFECT = pltpu.SideEffectType.DATAFLOW_SIDE_EFFECTING

ADAM_LR = 0.001
ADAM_B1 = 0.9
ADAM_B2 = 0.999
ADAM_EPS = 1e-08
ADAM_WD = 0.01
ADAM_STEP = 10
BC1 = 1.0 - ADAM_B1 ** ADAM_STEP
BC2 = 1.0 - ADAM_B2 ** ADAM_STEP

MIB = 1 << 20


def _params(sem=None, vmem_mib=48):
    return pltpu.CompilerParams(dimension_semantics=sem, vmem_limit_bytes=vmem_mib * MIB)


def _sigmoid(v):
    return 1.0 / (1.0 + jnp.exp(-v))


def _dot(a, b, ca, cb, precision=None):
    return lax.dot_general(a, b, (((ca,), (cb,)), ((), ())), precision=precision,
                           preferred_element_type=jnp.float32)


def _bf(v):
    return v.astype(jnp.bfloat16)


def _in_hbm(a):
    return pltpu.with_memory_space_constraint(a, pltpu.HBM)


def _place():
    x, y, c = lax.axis_index("x"), lax.axis_index("y"), lax.axis_index("c")
    return x, y, c, 4 * x + 2 * y + c


def _peer(x, y, c, r):
    return (x ^ ((r >> 2) & 1), y ^ ((r >> 1) & 1), c ^ (r & 1))


def _gather_proj(x, g1, w_in, w_out, pool_w):
    def body(x_ref, g_ref, win_ref, wout_ref, pool_ref, ht_o, wt_o, woutb_o, wout_o, pool_o, proj_o,
             xbuf, hv, htv, wv, wob, pb, stage, send_sems, recv_sems, loc_sems, out_sems):
        px, py, c, my_idx = _place()
        fetch_x = pltpu.make_async_copy(x_ref, xbuf, loc_sems.at[5])
        fetch_x.start()
        me, sibling = (px, py, c), (px, py, 1 - c)
        chips = [(1 - px, py), (px, 1 - py), (1 - px, 1 - py)]
        for p in range(3):
            wv[3 * my_idx + p] = _bf(win_ref[0, :, p * TILE:(p + 1) * TILE])

        def index(bx, by, bc):
            return 4 * bx + 2 * by + bc

        def slot(w, block):
            return wv.at[pl.ds(3 * index(*block), 3)] if w == 0 else pool_o.at[index(*block)]

        def copy(k, w, block, to, src=None):
            return pltpu.make_async_remote_copy(
                src_ref=slot(w, block) if src is None else src, dst_ref=slot(w, block),
                send_sem=send_sems.at[2 * k + w], recv_sem=recv_sems.at[2 * k + w],
                device_id=to, device_id_type=MESH)

        def save(block):
            at = pl.ds(3 * index(*block), 3)
            pltpu.make_async_copy(wv.at[at], wt_o.at[at], loc_sems.at[4]).start()

        srcs = (slot(0, me), pb)
        first = []
        for w in (0, 1):
            if w == 1:
                pb[...] = _bf(pool_ref[0])
                wob[...] = _bf(wout_ref[0])
            group = [copy(1 + j, w, me, (*chip, c), src=srcs[w]) for j, chip in enumerate(chips[:2])]
            group.append(copy(0, w, me, sibling, src=srcs[w]))
            for cp in group:
                cp.start()
            first += group
        save(me)
        locs = [pltpu.make_async_copy(pb, slot(1, me), loc_sems.at[0]),
                pltpu.make_async_copy(wob, wout_o.at[my_idx], loc_sems.at[1]),
                pltpu.make_async_copy(wob, woutb_o, loc_sems.at[2])]
        for cp in locs:
            cp.start()

        fetch_x.wait()
        xv = xbuf[...]
        hv[...] = _bf(xv * lax.rsqrt(jnp.mean(xv * xv, axis=-1, keepdims=True) + EPS) * g_ref[...])
        rows = 256
        for r0 in range(0, T, rows):
            htv[:, r0:r0 + rows] = hv[r0:r0 + rows, :].T
        locs.append(pltpu.make_async_copy(htv, ht_o, loc_sems.at[3]))
        locs[-1].start()

        def out_copy(p, j):
            return pltpu.make_async_copy(stage.at[p], proj_o.at[j], out_sems.at[p])

        def project(nth, block):
            base = 3 * index(*block)

            def tile(p, carry):
                if nth > 0:
                    out_copy(p, base + p).wait()
                stage[p] = _dot(hv[...], wv[base + p], 1, 0)
                out_copy(p, base + p).start()
                return carry

            lax.fori_loop(0, 3, tile, 0)

        project(0, me)
        copy(0, 0, sibling, me).wait_recv()
        save(sibling)
        project(1, sibling)
        passed = []
        relay_from = (px ^ (1 - c), py ^ c, c)
        relay_to = (px ^ c, py ^ (1 - c), c)

        def arrived(w, j):
            copy(1 + j, w, (*chips[j], c), me).wait_recv()
            passed.append(copy(4 + j, w, (*chips[j], c), sibling))
            passed[-1].start()

        def relay(w):
            passed.append(copy(3, w, relay_from, relay_to))
            passed[-1].start()

        def handed(nth, j):
            copy(4 + j, 0, (*chips[j], 1 - c), me).wait_recv()
            save((*chips[j], 1 - c))
            project(nth, (*chips[j], 1 - c))

        arrived(0, 0)
        arrived(0, 1)
        relay(0)
        for j in range(2):
            save((*chips[j], c))
            project(2 + j, (*chips[j], c))
        handed(4, 0)
        handed(5, 1)
        arrived(1, 0)
        arrived(1, 1)
        relay(1)
        arrived(0, 2)
        save((*chips[2], c))
        project(6, (*chips[2], c))
        handed(7, 2)
        arrived(1, 2)
        copy(0, 1, sibling, me).wait_recv()
        for j, chip in enumerate(chips):
            copy(4 + j, 1, (*chip, 1 - c), me).wait_recv()
        keep = pltpu.make_async_copy(wv, wt_o, loc_sems.at[4])
        for p in range(3):
            out_copy(p, p).wait()
        for cp in first + passed:
            cp.wait_send()
        keep.wait()
        for cp in locs:
            cp.wait()

    vmem = pl.BlockSpec(memory_space=pltpu.VMEM)
    bf16 = jnp.bfloat16
    return pl.pallas_call(
        body, name="gather_proj",
        out_shape=(pltpu.HBM((D, T), bf16), pltpu.HBM((NTILE, D, TILE), bf16),
                   pltpu.HBM((DMIX // NDEV, D), bf16), pltpu.HBM((NDEV, DMIX // NDEV, D), bf16),
                   pltpu.HBM((NDEV, NGROUP, GROUP // NDEV, GROUP), bf16), pltpu.HBM((NTILE, T, TILE), jnp.float32)),
        in_specs=[ANY] + [vmem] * 4, out_specs=(ANY,) * 6,
        scratch_shapes=[pltpu.VMEM((T, D), jnp.float32),
                        pltpu.VMEM((T, D), bf16), pltpu.VMEM((D, T), bf16), pltpu.VMEM((NTILE, D, TILE), bf16),
                        pltpu.VMEM((DMIX // NDEV, D), bf16), pltpu.VMEM((NGROUP, GROUP // NDEV, GROUP), bf16),
                        pltpu.VMEM((3, T, TILE), jnp.float32),
                        pltpu.SemaphoreType.DMA((14,)), pltpu.SemaphoreType.DMA((14,)),
                        pltpu.SemaphoreType.DMA((6,)), pltpu.SemaphoreType.DMA((3,))],
        compiler_params=_params(vmem_mib=56),
    )(x, g1, w_in, w_out, pool_w)


def _split_start(name, arrays, n_copies, plan):
    k = len(arrays)

    def body(*refs):
        send_sems, recv_sems, token = refs[k], refs[k + 1], refs[-1]
        for i, (src, dst, to) in enumerate(plan(refs[:k])):
            pltpu.make_async_remote_copy(src_ref=src, dst_ref=dst, send_sem=send_sems.at[i],
                                         recv_sem=recv_sems.at[i], device_id=to, device_id_type=MESH).start()
        token[...] = jnp.zeros_like(token)

    out = pl.pallas_call(
        body, name=name,
        out_shape=(pltpu.SemaphoreType.DMA((n_copies,)), pltpu.SemaphoreType.DMA((n_copies,)),
                   *[pltpu.HBM(a.shape, a.dtype) for a in arrays], jax.ShapeDtypeStruct((8, 128), jnp.float32)),
        in_specs=[HBM] * k, out_specs=(SEM, SEM, *[HBM] * k, pl.BlockSpec(memory_space=pltpu.VMEM)),
        input_output_aliases={i: 2 + i for i in range(k)},
        compiler_params=pltpu.CompilerParams(has_side_effects=EFFECT),
    )(*[pltpu.with_memory_space_constraint(a, pltpu.HBM) for a in arrays])
    return out[0], out[1], out[2:2 + k], out[-1]


def _split_wait(name, arrays, send_sems, recv_sems, plan, after):
    k = len(arrays)

    def body(*refs):
        sends, recvs = refs[k], refs[k + 1]
        for i, (src, dst, to) in enumerate(plan(refs[:k])):
            cp = pltpu.make_async_remote_copy(src_ref=src, dst_ref=dst, send_sem=sends.at[i], recv_sem=recvs.at[i],
                                              device_id=to, device_id_type=MESH)
            cp.wait_send()
            cp.wait_recv()

    return pl.pallas_call(
        body, name=name,
        out_shape=tuple(pltpu.HBM(a.shape, a.dtype) for a in arrays),
        in_specs=[HBM] * k + [SEM, SEM, ANY], out_specs=(HBM,) * k,
        input_output_aliases={i: i for i in range(k)},
        compiler_params=pltpu.CompilerParams(has_side_effects=EFFECT),
    )(*arrays, send_sems, recv_sems, after)


def _plan_wout(refs):
    src, land = refs
    x, y, c, me = _place()
    return [(src, land.at[me], _peer(x, y, c, r)) for r in range(1, NDEV)]


def _plan_rest(refs):
    gob, gpf, r_out, r_pool = refs
    x, y, c, me = _place()
    plan = []
    for r in range(1, NDEV):
        plan.append((gob.at[me ^ r], r_out.at[r - 1], _peer(x, y, c, r)))
        plan.append((gpf.at[me ^ r], r_pool.at[r - 1], _peer(x, y, c, r)))
    return plan


def _plan_in(refs):
    sum_b, r_in = refs
    x, y, c, _ = _place()
    plan = []
    for j, (dx, dy) in enumerate(((1, 0), (0, 1), (1, 1))):
        px, py = x ^ dx, y ^ dy
        plan.append((sum_b.at[2 * px + py], r_in.at[j], (px, py, c)))
    return plan


def _gather_small(small, *after):
    def body(sm, *refs):
        r_small, send_sems, recv_sems, loc_sem = refs[len(after):]
        x, y, c, me = _place()
        loc = pltpu.make_async_copy(sm, r_small.at[me], loc_sem)
        loc.start()

        def copy(r, src_idx):
            return pltpu.make_async_remote_copy(
                src_ref=sm, dst_ref=r_small.at[src_idx], send_sem=send_sems.at[r - 1], recv_sem=recv_sems.at[r - 1],
                device_id=_peer(x, y, c, r), device_id_type=MESH)

        sends = [copy(r, me) for r in range(1, NDEV)]
        for cp in sends:
            cp.start()
        for r in range(1, NDEV):
            copy(r, me ^ r).wait_recv()
        for cp in sends:
            cp.wait_send()
        loc.wait()

    return pl.pallas_call(
        body, name="gather_small",
        out_shape=jax.ShapeDtypeStruct((NDEV, 8, D), jnp.float32),
        in_specs=[ANY] * (1 + len(after)), out_specs=ANY,
        scratch_shapes=[pltpu.SemaphoreType.DMA((NDEV - 1,)), pltpu.SemaphoreType.DMA((NDEV - 1,)),
                        pltpu.SemaphoreType.DMA],
    )(small, *after)


def _seg_tiles(s):
    return (s + 2) % NSEG


_POOL_SPECS = [pl.BlockSpec((None, T, GROUP), lambda g, base=base: (base + g, 0, 0)) for base in (0, 4)]
_HEAD_SPECS = [pl.BlockSpec((None, T, HEAD), lambda h, base=base: (base + h // 2, 0, h % 2))
               for base in (8, 12, 16, 20)]


def _row_ids(shape):
    return lax.broadcasted_iota(jnp.int32, shape, 0)


BAND_ROWS = 128
HALO = 16


def _window_sum(a, gidx, lead):
    width = lax.shift_left(jnp.int32(2), gidx)
    shape = (BAND_ROWS, BAND_ROWS + HALO)
    t, j = lax.broadcasted_iota(jnp.int32, shape, 0), lax.broadcasted_iota(jnp.int32, shape, 1)
    first = t if lead else t + HALO - width + 1
    band = _bf(jnp.where(j >= first, jnp.where(j < first + width, 1.0, 0.0), 0.0))
    zeros = jnp.zeros((HALO, a.shape[1]), jnp.bfloat16)
    padded = [jnp.concatenate([p, zeros] if lead else [zeros, p], axis=0) for p in _split2(a)]
    out = []
    for r0 in range(0, T, BAND_ROWS):
        slab = jnp.concatenate([p[r0:r0 + BAND_ROWS + HALO] for p in padded], axis=1)
        r = _dot(band, slab, 1, 0)
        out.append(r[:, :a.shape[1]] + r[:, a.shape[1]:])
    return jnp.concatenate(out, axis=0)


def _window_mean(s, gidx):
    inv = jnp.where(gidx == 0, 0.5, jnp.where(gidx == 1, 0.25, jnp.where(gidx == 2, 0.125, 0.0625)))
    width = lax.shift_left(jnp.int32(2), gidx)
    head = s[:16] / jnp.minimum(_row_ids((16, s.shape[1])) + 1, width).astype(jnp.float32)
    return jnp.concatenate([head, s[16:] * inv], axis=0)


def _pool_fwd(proj, pool_w, pool_scale, token):
    def body(u_ref, pg_ref, w_ref, sc_ref, token_any, y_ref):
        del token_any
        gidx = pl.program_id(0)
        u, pg = u_ref[...], pg_ref[...]
        d = _window_mean(_window_sum(u, gidx, False), gidx) - u
        mixed = _dot(_bf(d), w_ref[...], 1, 0)
        y_ref[...] = _bf(mixed * sc_ref[...] * (pg * _sigmoid(pg)))

    return pl.pallas_call(
        body, name="pool_fwd", grid=(NGROUP,),
        in_specs=[*_POOL_SPECS,
                  pl.BlockSpec((None, GROUP, GROUP), lambda g: (g, 0, 0)),
                  pl.BlockSpec((1, GROUP), lambda g: (0, g)), ANY],
        out_specs=pl.BlockSpec((T, GROUP), lambda g: (0, g)),
        out_shape=pltpu.HBM((T, DMIX), jnp.bfloat16),
        compiler_params=_params(("parallel",)),
    )(proj, proj, pool_w, pool_scale, token)


def _tri(lower):
    r = lax.broadcasted_iota(jnp.int32, (CHUNK, CHUNK), 0)
    c = lax.broadcasted_iota(jnp.int32, (CHUNK, CHUNK), 1)
    return (r >= c) if lower else (r <= c)


def _sum_rows_matrix():
    shape = (CHUNK + 16, CHUNK)
    r, c = lax.broadcasted_iota(jnp.int32, shape, 0), lax.broadcasted_iota(jnp.int32, shape, 1)
    run = jnp.where(c <= r, 1.0, 0.0)
    half = jnp.where(c < CHUNK // 2, 1.0, 0.0)
    return _bf(jnp.where(r < CHUNK, run, jnp.where(r < CHUNK + 8, 1.0, half)))


def _rev_sum_matrix():
    shape = (CHUNK, 2 * CHUNK)
    r, c = lax.broadcasted_iota(jnp.int32, shape, 0), lax.broadcasted_iota(jnp.int32, shape, 1)
    return _bf(jnp.where(c < CHUNK, jnp.where(c >= r, 1.0, 0.0), jnp.where(c - CHUNK < r, 1.0, 0.0)))


def _split2(a):
    hi = _bf(a)
    return [hi, _bf(a - hi.astype(jnp.float32))]


def _exact_sums(mat, pieces):
    x = jnp.concatenate([s for p in pieces for s in _split2(p)], axis=1)
    r = _dot(mat, x, 1, 0)
    return [r[:, 2 * j * HEAD:(2 * j + 1) * HEAD] + r[:, (2 * j + 1) * HEAD:(2 * j + 2) * HEAD]
            for j in range(len(pieces))]


def _gates(qv, fl, lb):
    sq = _sigmoid(qv)
    sg = _sigmoid(fl)
    f = lb + (1.0 - lb) * sg
    return dict(sq=sq, qs=qv * sq, sg=sg, f=f, kk=1.0 - f, g=jnp.log2(f))


def _decays(sums):
    big_g = sums[:CHUNK]
    total = sums[CHUNK:CHUNK + 8]
    g_last = jnp.tile(total, (CHUNK // 8, 1))
    g_mid = jnp.tile(sums[CHUNK + 8:], (CHUNK // 8, 1))
    return dict(
        e_q=jnp.exp2(big_g),
        e_k=jnp.exp2(g_last - big_g),
        e_qm=jnp.exp2(jnp.minimum(big_g - g_mid, EXP_CAP)),
        e_km=jnp.exp2(jnp.minimum(g_mid - big_g, EXP_CAP)),
        total8=jnp.exp2(total))


def _group_rows(gi):
    return [pl.ds(pl.multiple_of((gi * NB + j) * CHUNK, CHUNK), CHUNK) for j in range(NB)]


def _lower_bound(lb_ref):
    return _sigmoid(lb_ref[0:1, :] - lb_ref[1:2, :])


def _hgrn_fwd(proj, lb_logits, rec_g, y_in):
    def body(q_ref, f_ref, i_ref, gate_ref, lb_ref, rg_ref, y_any, y_ref, o_ref, st_ref):
        del y_any
        lb = _lower_bound(lb_ref)
        causal = _tri(True)
        smat = _sum_rows_matrix()

        def group(gi, st):
            rows = _group_rows(gi)
            ts = [_gates(q_ref[r, :], f_ref[r, :], lb) for r in rows]
            ds = [_decays(s) for s in _exact_sums(smat, [t["g"] for t in ts])]
            vs = [_bf(i_ref[r, :]) for r in rows]
            q_m = [_bf(t["qs"] * d["e_qm"]) for t, d in zip(ts, ds)]
            k_m = [_bf(t["kk"] * d["e_km"]) for t, d in zip(ts, ds)]
            q_e = [_bf(t["qs"] * d["e_q"]) for t, d in zip(ts, ds)]
            k_e = [_bf(t["kk"] * d["e_k"]) for t, d in zip(ts, ds)]
            a = [_bf(jnp.where(causal, _dot(q_m[j], k_m[j], 1, 1), 0.0)) for j in range(NB)]
            intra = [_dot(a[j], vs[j], 1, 0) for j in range(NB)]
            upd = [_dot(vs[j], k_e[j], 0, 0) for j in range(NB)]
            for j in range(NB):
                st_ref[gi * NB + j] = st
                o_ref[rows[j], :] = intra[j] + _dot(q_e[j], _bf(st), 1, 1)
                st = st * jnp.tile(ds[j]["total8"], (HEAD // 8, 1)) + upd[j]
            return st

        lax.fori_loop(0, NGRP, group, jnp.zeros((HEAD, HEAD), jnp.float32))
        o = o_ref[...]
        rn = o * lax.rsqrt(jnp.mean(o * o, axis=-1, keepdims=True) + EPS)
        gate = gate_ref[...]
        y_ref[...] = _bf(rn * rg_ref[...] * (gate * _sigmoid(gate)))

    return pl.pallas_call(
        body, name="hgrn_fwd", grid=(NHEAD,),
        in_specs=[*_HEAD_SPECS,
                  pl.BlockSpec((2, HEAD), lambda h: (0, h)),
                  pl.BlockSpec((1, HEAD), lambda h: (0, h)),
                  pl.BlockSpec(memory_space=pl.ANY)],
        out_specs=(pl.BlockSpec((T, HEAD), lambda h: (0, NHEAD + h)),
                   pl.BlockSpec((T, HEAD), lambda h: (0, h)),
                   pl.BlockSpec((None, NCHUNK, HEAD, HEAD), lambda h: (h, 0, 0, 0))),
        out_shape=(pltpu.HBM((T, DMIX), jnp.bfloat16), pltpu.HBM((T, D), jnp.float32),
                   pltpu.HBM((NHEAD, NCHUNK, HEAD, HEAD), jnp.float32)),
        input_output_aliases={6: 0},
        compiler_params=_params(("parallel",)),
    )(proj, proj, proj, proj, lb_logits, rec_g, y_in)


def _out_proj_loss(x, y, w_out, target, gf):
    rows = 512
    parts = [slice(k * rows // 2, (k + 1) * rows // 2) for k in range(2)]

    def body(x_ref, y_ref, w_ref, t_ref, g_ref, dz_ref, dzb_ref, sq_ref, dg_ref):
        zs = [x_ref[p, :] + _dot(y_ref[p, :], w_ref[...], 1, 0) for p in parts]
        sq = dg = 0.0
        for p, z in zip(parts, zs):
            r = lax.rsqrt(jnp.mean(z * z, axis=-1, keepdims=True) + EPS)
            zhat = z * r
            err = zhat * g_ref[...] - t_ref[p, :]
            dy = err * (1.0 / D)
            gdy = dy * g_ref[...]
            dz = r * (gdy - zhat * jnp.mean(zhat * gdy, axis=-1, keepdims=True))
            dz_ref[p, :] = dz
            dzb_ref[p, :] = _bf(dz)
            sq = sq + jnp.sum(err * err, axis=0, keepdims=True)
            dg = dg + jnp.sum(zhat * dy, axis=0, keepdims=True)

        @pl.when(pl.program_id(0) == 0)
        def _():
            sq_ref[...] = sq
            dg_ref[...] = dg

        @pl.when(pl.program_id(0) != 0)
        def _():
            sq_ref[...] += sq
            dg_ref[...] += dg

    tile = pl.BlockSpec((rows, D), lambda i: (i, 0))
    vec = pl.BlockSpec((1, D), lambda i: (0, 0))
    return pl.pallas_call(
        body, name="out_proj_loss", grid=(T // rows,),
        in_specs=[tile, pl.BlockSpec((rows, DMIX), lambda i: (i, 0)), pl.BlockSpec((DMIX, D), lambda i: (0, 0)),
                  tile, vec],
        out_specs=(tile, tile, vec, vec),
        out_shape=(pltpu.HBM((T, D), jnp.float32), pltpu.HBM((T, D), jnp.bfloat16),
                   jax.ShapeDtypeStruct((1, D), jnp.float32), jax.ShapeDtypeStruct((1, D), jnp.float32)),
        compiler_params=_params(("arbitrary",)),
    )(x, y, w_out, target, gf)


def _out_proj_bwd(dzb, w_out, y):
    tn = 512

    def body(dz_ref, w_ref, y_ref, dy_ref, gw_ref, gwb_ref):
        dz = dz_ref[...]
        dy_ref[...] = _dot(dz, w_ref[...], 1, 1)
        gw = _dot(y_ref[...], dz, 0, 0)
        gw_ref[...] = gw
        gwb_ref[...] = _bf(gw)

    return pl.pallas_call(
        body, name="out_proj_bwd", grid=(DMIX // tn,),
        in_specs=[pl.BlockSpec((T, D), lambda n: (0, 0)), pl.BlockSpec((tn, D), lambda n: (n, 0)),
                  pl.BlockSpec((T, tn), lambda n: (0, n))],
        out_specs=(pl.BlockSpec((T, tn), lambda n: (0, n)), pl.BlockSpec((tn, D), lambda n: (n, 0)),
                   pl.BlockSpec((tn, D), lambda n: (n, 0))),
        out_shape=(pltpu.HBM((T, DMIX), jnp.float32), pltpu.HBM((DMIX, D), jnp.float32),
                   pltpu.HBM((DMIX, D), jnp.bfloat16)),
        compiler_params=_params(("parallel",)),
    )(dzb, w_out, y)


def _hgrn_bwd(proj, lb_logits, rec_g, o, states, dymix, dproj_in, token):
    def body(q_ref, f_ref, i_ref, gate_ref, lb_ref, rg_ref, o_ref, st_ref, dy_ref, dp_any, token_any,
             dp_ref, drg_ref, dlb_ref, do_ref):
        del dp_any, token_any
        lb = _lower_bound(lb_ref)
        causal = _tri(True)
        smat, rmat = _sum_rows_matrix(), _rev_sum_matrix()

        o = o_ref[...]
        rs = lax.rsqrt(jnp.mean(o * o, axis=-1, keepdims=True) + EPS)
        rn = o * rs
        gate = gate_ref[...]
        sgate = _sigmoid(gate)
        dyv = dy_ref[...]
        d_r = dyv * (gate * sgate)
        dp_ref[3] = _bf(dyv * (rn * rg_ref[...]) * (sgate * (1.0 + gate * (1.0 - sgate))))
        drg_ref[...] = jnp.sum(d_r * rn, axis=0, keepdims=True)
        drn = d_r * rg_ref[...]
        do_ref[...] = rs * (drn - rn * jnp.mean(rn * drn, axis=-1, keepdims=True))

        def group(i, carry):
            dst, dlb = carry
            gi = NGRP - 1 - i
            rows = _group_rows(gi)
            span = range(NB)
            qvs = [q_ref[r, :] for r in rows]
            ts = [_gates(qv, f_ref[r, :], lb) for qv, r in zip(qvs, rows)]
            ds = [_decays(s) for s in _exact_sums(smat, [t["g"] for t in ts])]
            vs = [_bf(i_ref[r, :]) for r in rows]
            dos = [_bf(do_ref[r, :]) for r in rows]
            sts = [st_ref[gi * NB + j] for j in span]
            qe_f = [t["qs"] * d["e_q"] for t, d in zip(ts, ds)]
            ke_f = [t["kk"] * d["e_k"] for t, d in zip(ts, ds)]
            q_e, k_e = [_bf(a) for a in qe_f], [_bf(a) for a in ke_f]
            q_m = [_bf(t["qs"] * d["e_qm"]) for t, d in zip(ts, ds)]
            k_m = [_bf(t["kk"] * d["e_km"]) for t, d in zip(ts, ds)]
            a = [_bf(jnp.where(causal, _dot(q_m[j], k_m[j], 1, 1), 0.0)) for j in span]
            da = [_bf(jnp.where(causal, _dot(dos[j], vs[j], 1, 1), 0.0)) for j in span]
            dqm = [_dot(da[j], k_m[j], 1, 0) for j in span]
            dkm = [_dot(da[j], q_m[j], 0, 0) for j in span]
            dv_in = [_dot(a[j], dos[j], 0, 0) for j in span]
            dqe = [_dot(dos[j], _bf(sts[j]), 1, 0) for j in span]
            grow = [_dot(dos[j], q_e[j], 0, 0) for j in span]
            dke, carried = [None] * NB, [None] * NB
            for j in reversed(span):
                dst_b = _bf(dst)
                dke[j] = _dot(vs[j], dst_b, 1, 0)
                dp_ref[2, rows[j], :] = _bf(dv_in[j] + _dot(k_e[j], dst_b, 1, 1))
                carried[j] = ds[j]["total8"] * jnp.sum(dst * sts[j], axis=0, keepdims=True)
                dst = dst * jnp.tile(ds[j]["total8"], (HEAD // 8, 1)) + grow[j]
            kdk = [ke_f[j] * dke[j] for j in span]
            pos = [(q_m[j].astype(jnp.float32) * dqm[j] - k_m[j].astype(jnp.float32) * dkm[j]) + qe_f[j] * dqe[j]
                   for j in span]
            dgs = _exact_sums(rmat, [jnp.concatenate([pos[j], kdk[j]], axis=0) for j in span])
            for j in span:
                t, d = ts[j], ds[j]
                dg = dgs[j] + jnp.tile(carried[j], (CHUNK // 8, 1))
                dqs = dqm[j] * d["e_qm"] + dqe[j] * d["e_q"]
                dkk = dkm[j] * d["e_km"] + dke[j] * d["e_k"]
                df = dg / t["f"] - dkk
                dp_ref[1, rows[j], :] = _bf(df * (1.0 - lb) * (t["sg"] * (1.0 - t["sg"])))
                dp_ref[0, rows[j], :] = _bf(dqs * (t["sq"] * (1.0 + qvs[j] * (1.0 - t["sq"]))))
                dlb = dlb + df * (1.0 - t["sg"])
            return dst, dlb

        _, dlb = lax.fori_loop(0, NGRP, group, (jnp.zeros((HEAD, HEAD), jnp.float32),
                                                jnp.zeros((CHUNK, HEAD), jnp.float32)))
        dlb_ref[...] = jnp.sum(dlb, axis=0, keepdims=True)

    vec = pl.BlockSpec((1, HEAD), lambda h: (0, h))
    return pl.pallas_call(
        body, name="hgrn_bwd", grid=(NHEAD,),
        in_specs=[*_HEAD_SPECS,
                  pl.BlockSpec((2, HEAD), lambda h: (0, h)), vec,
                  pl.BlockSpec((T, HEAD), lambda h: (0, h)),
                  pl.BlockSpec((None, NCHUNK, HEAD, HEAD), lambda h: (h, 0, 0, 0)),
                  pl.BlockSpec((T, HEAD), lambda h: (0, NHEAD + h)), ANY, ANY],
        out_specs=(pl.BlockSpec((4, T, HEAD), lambda h: (0, 0, h)), vec, vec),
        out_shape=(pltpu.HBM((NSEG, T, D), jnp.bfloat16),
                   jax.ShapeDtypeStruct((1, D), jnp.float32), jax.ShapeDtypeStruct((1, D), jnp.float32)),
        scratch_shapes=[pltpu.VMEM((T, HEAD), jnp.float32)],
        input_output_aliases={9: 0},
        compiler_params=_params(("parallel",)),
    )(proj, proj, proj, proj, lb_logits, rec_g, o, states, dymix, dproj_in, token)


def _pool_bwd(proj, pool_w, pool_scale, dymix):
    def body(u_ref, pg_ref, w_ref, sc_ref, dy_ref, dp_ref, gw_ref, gs_ref):
        gidx = pl.program_id(0)
        u, pg = u_ref[...], pg_ref[...]
        d = _bf(_window_mean(_window_sum(u, gidx, False), gidx) - u)
        mixed = _dot(d, w_ref[...], 1, 0)
        spg = _sigmoid(pg)
        dyv = dy_ref[...]
        d_p = dyv * (pg * spg)
        dp_ref[1] = _bf(dyv * (mixed * sc_ref[...]) * (spg * (1.0 + pg * (1.0 - spg))))
        gs_ref[...] = jnp.sum(d_p * mixed, axis=0, keepdims=True)
        dmixed = _bf(d_p * sc_ref[...])
        gw_ref[...] = _dot(d, dmixed, 0, 0)
        dd = _dot(dmixed, w_ref[...], 1, 1)
        dp_ref[0] = _bf(_window_sum(_window_mean(dd, gidx), gidx, True) - dd)

    return pl.pallas_call(
        body, name="pool_bwd", grid=(NGROUP,),
        in_specs=[*_POOL_SPECS,
                  pl.BlockSpec((None, GROUP, GROUP), lambda g: (g, 0, 0)),
                  pl.BlockSpec((1, GROUP), lambda g: (0, g)),
                  pl.BlockSpec((T, GROUP), lambda g: (0, g))],
        out_specs=(pl.BlockSpec((2, T, GROUP), lambda g: (2, 0, g)),
                   pl.BlockSpec((None, GROUP, GROUP), lambda g: (g, 0, 0)),
                   pl.BlockSpec((1, GROUP), lambda g: (0, g))),
        out_shape=(pltpu.HBM((NSEG, T, D), jnp.bfloat16),
                   jax.ShapeDtypeStruct((NGROUP, GROUP, GROUP), jnp.float32),
                   jax.ShapeDtypeStruct((1, D), jnp.float32)),
        compiler_params=_params(("parallel",)),
    )(proj, proj, pool_w, pool_scale, dymix)


def _proj_bwd_w(place, ht, dproj):
    half = NTILE // 2

    def owner_chip(i, pr):
        return jnp.where(i < half, i // 3, (pr[1] + 1 + (i - half) // 3) % 4)

    def tile_of(i, pr):
        side = jnp.where(i < half, 1 - pr[2], pr[2])
        return 6 * owner_chip(i, pr) + 3 * side + i % 3

    def dproj_block(i, pr):
        j = tile_of(i, pr)
        return ((j // 4 + 4) % NSEG, 0, j % 4)

    def mine(i):
        return jnp.maximum(i, half)

    def body(place_ref, h_ref, dp_ref, sum_ref, own_ref, sendbuf, recvbuf, send_sems, recv_sems):
        i = pl.program_id(0)
        px, py, c, _ = _place()
        gw = _dot(h_ref[...], dp_ref[...], 1, 0)

        def to_sibling(slot):
            return pltpu.make_async_remote_copy(
                src_ref=sendbuf.at[slot], dst_ref=recvbuf.at[slot], send_sem=send_sems.at[slot],
                recv_sem=recv_sems.at[slot], device_id=(px, py, 1 - c), device_id_type=MESH)

        @pl.when(i < half)
        def _():
            sendbuf[i] = _bf(gw)
            to_sibling(i).start()

        @pl.when(i >= half)
        def _():
            slot = 3 * owner_chip(i, place_ref) + i % 3
            to_sibling(slot).wait_recv()
            total = gw + recvbuf[slot].astype(jnp.float32)
            sum_ref[...] = _bf(total)
            own_ref[...] = total

        @pl.when(i == NTILE - 1)
        def _():
            for slot in range(half):
                to_sibling(slot).wait_send()

    return pl.pallas_call(
        body, name="proj_bwd_w",
        grid_spec=pltpu.PrefetchScalarGridSpec(
            num_scalar_prefetch=1, grid=(NTILE,),
            in_specs=[pl.BlockSpec((D, T), lambda i, pr: (0, 0)),
                      pl.BlockSpec((None, T, TILE), lambda i, pr: dproj_block(i, pr))],
            out_specs=(pl.BlockSpec((None, None, D, TILE), lambda i, pr: (owner_chip(mine(i), pr), mine(i) % 3, 0, 0)),
                       pl.BlockSpec((None, D, TILE), lambda i, pr: (jnp.where(i < NTILE - 3, 0, i % 3), 0, 0))),
            scratch_shapes=[pltpu.VMEM((half, D, TILE), jnp.bfloat16), pltpu.VMEM((half, D, TILE), jnp.bfloat16),
                            pltpu.SemaphoreType.DMA((half,)), pltpu.SemaphoreType.DMA((half,))]),
        out_shape=(pltpu.HBM((4, 3, D, TILE), jnp.bfloat16), pltpu.HBM((3, D, TILE), jnp.float32)),
        compiler_params=_params(("arbitrary",)),
    )(place, ht, dproj)


def _proj_bwd_x(dproj, w_t, x, g1, dz, token):
    tm = 512
    pairs = NSEG // 2

    def body(dp_ref, w_ref, x_ref, g_ref, dz_ref, token_any, dx_ref, dg_ref, wcat, acc):
        del token_any
        s, m = pl.program_id(0), pl.program_id(1)

        @pl.when(m == 0)
        def _():
            for i in range(8):
                wcat[:, i * TILE:(i + 1) * TILE] = w_ref[i]

        mine = pl.ds(pl.multiple_of(m * tm, tm), tm)

        @pl.when(s == 0)
        def _():
            acc[mine, :] = jnp.zeros((tm, D), jnp.float32)

        acc[mine, :] += _dot(jnp.concatenate([dp_ref[0], dp_ref[1]], axis=1), wcat[...], 1, 1)

        @pl.when(s == pairs - 1)
        def _():
            xv = x_ref[...]
            rs = lax.rsqrt(jnp.mean(xv * xv, axis=-1, keepdims=True) + EPS)
            xhat = xv * rs
            dhv = acc[mine, :]
            gdh = dhv * g_ref[...]
            dx_ref[...] = dz_ref[...] + rs * (gdh - xhat * jnp.mean(xhat * gdh, axis=-1, keepdims=True))
            dg = jnp.sum(xhat * dhv, axis=0, keepdims=True)

            @pl.when(m == 0)
            def _():
                dg_ref[...] = dg

            @pl.when(m != 0)
            def _():
                dg_ref[...] += dg

    rows = pl.BlockSpec((tm, D), lambda s, m: (jnp.where(s == pairs - 1, m, 0), 0))
    vec = pl.BlockSpec((1, D), lambda s, m: (0, 0))
    return pl.pallas_call(
        body, name="proj_bwd_x", grid=(pairs, T // tm),
        in_specs=[pl.BlockSpec((2, tm, D), lambda s, m: (s, m, 0)),
                  pl.BlockSpec((8, D, TILE), lambda s, m: ((s + 1) % pairs, 0, 0)), rows, vec, rows, ANY],
        out_specs=(rows, vec),
        out_shape=(jax.ShapeDtypeStruct((T, D), jnp.float32), jax.ShapeDtypeStruct((1, D), jnp.float32)),
        scratch_shapes=[pltpu.VMEM((D, 2 * D), jnp.bfloat16), pltpu.VMEM((T, D), jnp.float32)],
        compiler_params=_params(("arbitrary", "arbitrary"), vmem_mib=56),
    )(dproj, w_t, x, g1, dz, token)


def _adamw(w, g, m, v):
    m_new = ADAM_B1 * m + (1.0 - ADAM_B1) * g
    v_new = ADAM_B2 * v + (1.0 - ADAM_B2) * (g * g)
    delta = -ADAM_LR * ((m_new / BC1) / (jnp.sqrt(v_new / BC2) + ADAM_EPS) + ADAM_WD * w)
    return delta, m_new, v_new


def _reduce_adam(name, place, parts, w, m, v, grid, w_spec):
    n = len(parts)

    def body(place_ref, *refs):
        del place_ref
        w_ref, m_ref, v_ref, g_ref, d_ref, mo_ref, vo_ref = refs[n:]
        g = None
        for ref, (_, _, stacked) in zip(refs[:n], parts):
            terms = [ref[r] for r in range(ref.shape[0])] if stacked else [ref[...]]
            for t in terms:
                if t.shape[-1] != w_ref.shape[-1]:
                    t = jnp.concatenate([t[p] for p in range(t.shape[0])], axis=1)
                g = t.astype(jnp.float32) if g is None else g + t.astype(jnp.float32)
        delta, m_new, v_new = _adamw(w_ref[...], g, m_ref[...], v_ref[...])
        g_ref[...] = g
        d_ref[...] = delta
        mo_ref[...] = m_new
        vo_ref[...] = v_new

    shape = jax.ShapeDtypeStruct(w.shape, jnp.float32)
    return pl.pallas_call(
        body, name=name,
        grid_spec=pltpu.PrefetchScalarGridSpec(
            num_scalar_prefetch=1, grid=grid,
            in_specs=[spec for _, spec, _ in parts] + [w_spec] * 3, out_specs=(w_spec,) * 4),
        out_shape=(shape,) * 4,
        compiler_params=_params(("parallel",)),
    )(place, *[_in_hbm(a) for a in [a for a, _, _ in parts] + [w, m, v]])


def _small_adam(parts, w, m, v):
    def body(p_ref, w_ref, m_ref, v_ref, g_ref, d_ref, mo_ref, vo_ref):
        g = p_ref[0]
        for s in range(1, NDEV):
            g = g + p_ref[s]
        wv = w_ref[...]
        rows = _row_ids(wv.shape)
        other = jnp.where(rows == 2, pltpu.roll(wv, 7, 0), jnp.where(rows == 3, pltpu.roll(wv, 1, 0), 0.0))
        lbv = _sigmoid(wv - other)
        sign = jnp.where(rows == 2, 1.0, -1.0)
        g = jnp.where((rows == 2) | (rows == 3), sign * g * lbv * (1.0 - lbv), g)
        delta, m_new, v_new = _adamw(wv, g, m_ref[...], v_ref[...])
        g_ref[...] = g
        d_ref[...] = delta
        mo_ref[...] = m_new
        vo_ref[...] = v_new

    shape = jax.ShapeDtypeStruct((8, D), jnp.float32)
    return pl.pallas_call(body, name="small_adam", out_shape=(shape,) * 4)(parts, w, m, v)


def _rows8(*vecs):
    rows = [a.reshape(-1, D) for a in vecs]
    n = sum(r.shape[0] for r in rows)
    return jnp.concatenate(rows + [jnp.zeros((8 - n, D), jnp.float32)], axis=0)


def kernel(x, norm1_g, w_in, pool_w, pool_scale, lb_logits, rec_norm_g, w_out, final_norm_g, loss_target, m_norm1_g, m_w_in, m_pool_w, m_pool_scale, m_lb_logits, m_rec_norm_g, m_w_out, m_final_norm_g, v_norm1_g, v_w_in, v_pool_w, v_pool_scale, v_lb_logits, v_rec_norm_g, v_w_out, v_final_norm_g):
    xs = x[0]
    target = loss_target[0]
    ix, iy, ic = lax.axis_index("x"), lax.axis_index("y"), lax.axis_index("c")
    place = jnp.stack([4 * ix + 2 * iy + ic, 2 * ix + iy, ic]).astype(jnp.int32)
    gf = final_norm_g.reshape(1, D)

    ht, w_t, w_out_b, w_out_g, pool_g, proj = _gather_proj(xs, norm1_g, w_in, w_out, pool_w)
    pool_full = pool_g.transpose(1, 0, 2, 3).reshape(NGROUP, GROUP, GROUP)
    wout = [w_out_b, w_out_g]
    wout_send, wout_recv, wout, wout_token = _split_start("gather_wout_start", wout, NDEV - 1, _plan_wout)

    y = _pool_fwd(proj, pool_full, pool_scale, wout_token)
    y, o, states = _hgrn_fwd(proj, lb_logits, rec_norm_g, y)
    _, w_out_g = _split_wait("gather_wout_wait", wout, wout_send, wout_recv, _plan_wout, o)
    w_out_full = _in_hbm(w_out_g.reshape(DMIX, D))
    dz, dzb, sq, dgf = _out_proj_loss(xs, y, w_out_full, target, gf)

    dymix, gwout_f, gwout_b = _out_proj_bwd(dzb, w_out_full, y)
    dproj, gpool, dscale = _pool_bwd(proj, pool_full, pool_scale, dymix)

    blk_out = (NDEV, DMIX // NDEV, D)
    blk_pool = (NDEV, NGROUP, GROUP // NDEV, GROUP)
    gpool_s = gpool.reshape(NGROUP, NDEV, GROUP // NDEV, GROUP).transpose(1, 0, 2, 3)
    rest = [gwout_b.reshape(blk_out), gpool_s,
            lax.empty((NDEV - 1,) + blk_out[1:], jnp.bfloat16), lax.empty((NDEV - 1,) + blk_pool[1:], jnp.float32)]
    rest_send, rest_recv, rest, rest_token = _split_start("scatter_rest_start", rest, 2 * (NDEV - 1), _plan_rest)

    dproj, drecg, dlb = _hgrn_bwd(proj, lb_logits, rec_norm_g, o, states, dymix, dproj, rest_token)
    chip_sums, own_sum = _proj_bwd_w(place, ht, dproj)
    win = [chip_sums, lax.empty((3, 3, D, TILE), jnp.bfloat16)]
    win_send, win_recv, win, win_token = _split_start("scatter_win_start", win, 3, _plan_in)

    grad_x, dg1 = _proj_bwd_x(dproj, w_t, xs, norm1_g, dz, win_token)

    _, gpool_own, r_out, r_pool = _split_wait("scatter_rest_wait", rest, rest_send, rest_recv, _plan_rest, grad_x)
    g_wout, d_wout, m_wout, v_wout = _reduce_adam(
        "adam_w_out", place,
        [(gwout_f.reshape(blk_out), pl.BlockSpec((None,) + blk_out[1:], lambda i, pr: (pr[0], 0, 0)), False),
         (r_out, pl.BlockSpec((NDEV - 1,) + blk_out[1:], lambda i, pr: (0, 0, 0)), True)],
        w_out, m_w_out, v_w_out, (1,), pl.BlockSpec((None,) + blk_out[1:], lambda i, pr: (0, 0, 0)))
    g_pool, d_pool, m_pool, v_pool = _reduce_adam(
        "adam_pool_w", place,
        [(gpool_own, pl.BlockSpec((None,) + blk_pool[1:], lambda i, pr: (pr[0], 0, 0, 0)), False),
         (r_pool, pl.BlockSpec((NDEV - 1,) + blk_pool[1:], lambda i, pr: (0, 0, 0, 0)), True)],
        pool_w, m_pool_w, v_pool_w, (1,), pl.BlockSpec((None,) + blk_pool[1:], lambda i, pr: (0, 0, 0, 0)))

    r_small = _gather_small(_rows8(dg1, dscale, dlb, dlb, drecg, dgf, sq), d_wout, d_pool)
    g_s, d_s, m_s, v_s = _small_adam(
        r_small,
        _rows8(norm1_g, pool_scale, lb_logits, rec_norm_g, final_norm_g),
        _rows8(m_norm1_g, m_pool_scale, m_lb_logits, m_rec_norm_g, m_final_norm_g),
        _rows8(v_norm1_g, v_pool_scale, v_lb_logits, v_rec_norm_g, v_final_norm_g))
    loss = jnp.sum(g_s[6]) * (0.5 / D)

    _, r_in = _split_wait("scatter_win_wait", win, win_send, win_recv, _plan_in, v_s)
    g_win, d_win, m_win, v_win = _reduce_adam(
        "adam_w_in", place,
        [(own_sum, pl.BlockSpec((3, D // 8, TILE), lambda i, pr: (0, i, 0)), False),
         (r_in, pl.BlockSpec((3, 3, D // 8, TILE), lambda i, pr: (0, 0, i, 0)), True)],
        w_in, m_w_in, v_w_in, (8,), pl.BlockSpec((None, D // 8, 3 * TILE), lambda i, pr: (0, i, 0)))

    def small_outs(a):
        return a[0:1], a[1:2], a[2:4], a[4:5], a[5]

    def outs(small_a, win, pool, wout):
        n1, ps, lbl, rg, fg = small_outs(small_a)
        return n1, win, pool, ps, lbl, rg, wout, fg

    return (loss, grad_x[None],
            *outs(g_s, g_win, g_pool, g_wout), *outs(d_s, d_win, d_pool, d_wout),
            *outs(m_s, m_win, m_pool, m_wout), *outs(v_s, v_win, v_pool, v_wout))
```

```python
import functools

import jax
import jax.numpy as jnp
from jax import lax
from jax.experimental import pallas as pl
from jax.experimental.pallas import tpu as pltpu

T = 2048
D = 1024
NSEG = 6
NTILE = 24
TILE = 256
DMIX = 2048
NDEV = 8
HEAD = 128
NHEAD = 8
CHUNK = 64
NCHUNK = T // CHUNK
NB = 32
NGRP = NCHUNK // NB
NGROUP = 4
GROUP = 256
EPS = 1e-6
EXP_CAP = 115.0
MESH = pl.DeviceIdType.MESH
AXES = ("x", "y", "c")
ANY = pl.BlockSpec(memory_space=pl.ANY)
HBM = pl.BlockSpec(memory_space=pltpu.HBM)
SEM = pl.BlockSpec(memory_space=pltpu.SEMAPHORE)
EFFECT = pltpu.SideEffectType.DATAFLOW_SIDE_EFFECTING

ADAM_LR = 0.001
ADAM_B1 = 0.9
ADAM_B2 = 0.999
ADAM_EPS = 1e-08
ADAM_WD = 0.01
ADAM_STEP = 10
BC1 = 1.0 - ADAM_B1 ** ADAM_STEP
BC2 = 1.0 - ADAM_B2 ** ADAM_STEP

MIB = 1 << 20


def _params(sem=None, vmem_mib=48):
    return pltpu.CompilerParams(dimension_semantics=sem, vmem_limit_bytes=vmem_mib * MIB)


def _sigmoid(v):
    return 1.0 / (1.0 + jnp.exp(-v))


def _dot(a, b, ca, cb, precision=None):
    return lax.dot_general(a, b, (((ca,), (cb,)), ((), ())), precision=precision,
                           preferred_element_type=jnp.float32)


def _bf(v):
    return v.astype(jnp.bfloat16)


def _in_hbm(a):
    return pltpu.with_memory_space_constraint(a, pltpu.HBM)


def _place():
    x, y, c = lax.axis_index("x"), lax.axis_index("y"), lax.axis_index("c")
    return x, y, c, 4 * x + 2 * y + c


def _peer(x, y, c, r):
    return (x ^ ((r >> 2) & 1), y ^ ((r >> 1) & 1), c ^ (r & 1))


def _gather_proj(x, g1, w_in, w_out, pool_w):
    def body(x_ref, g_ref, win_ref, wout_ref, pool_ref, ht_o, wt_o, woutb_o, wout_o, pool_o, proj_o,
             xbuf, hv, htv, wv, wob, pb, stage, send_sems, recv_sems, loc_sems, out_sems):
        px, py, c, my_idx = _place()
        fetch_x = pltpu.make_async_copy(x_ref, xbuf, loc_sems.at[5])
        fetch_x.start()
        me, sibling = (px, py, c), (px, py, 1 - c)
        chips = [(1 - px, py), (px, 1 - py), (1 - px, 1 - py)]
        for p in range(3):
            wv[3 * my_idx + p] = _bf(win_ref[0, :, p * TILE:(p + 1) * TILE])

        def index(bx, by, bc):
            return 4 * bx + 2 * by + bc

        def slot(w, block):
            return wv.at[pl.ds(3 * index(*block), 3)] if w == 0 else pool_o.at[index(*block)]

        def copy(k, w, block, to, src=None):
            return pltpu.make_async_remote_copy(
                src_ref=slot(w, block) if src is None else src, dst_ref=slot(w, block),
                send_sem=send_sems.at[2 * k + w], recv_sem=recv_sems.at[2 * k + w],
                device_id=to, device_id_type=MESH)

        def save(block):
            at = pl.ds(3 * index(*block), 3)
            pltpu.make_async_copy(wv.at[at], wt_o.at[at], loc_sems.at[4]).start()

        srcs = (slot(0, me), pb)
        first = []
        for w in (0, 1):
            if w == 1:
                pb[...] = _bf(pool_ref[0])
                wob[...] = _bf(wout_ref[0])
            group = [copy(1 + j, w, me, (*chip, c), src=srcs[w]) for j, chip in enumerate(chips[:2])]
            group.append(copy(0, w, me, sibling, src=srcs[w]))
            for cp in group:
                cp.start()
            first += group
        save(me)
        locs = [pltpu.make_async_copy(pb, slot(1, me), loc_sems.at[0]),
                pltpu.make_async_copy(wob, wout_o.at[my_idx], loc_sems.at[1]),
                pltpu.make_async_copy(wob, woutb_o, loc_sems.at[2])]
        for cp in locs:
            cp.start()

        fetch_x.wait()
        xv = xbuf[...]
        hv[...] = _bf(xv * lax.rsqrt(jnp.mean(xv * xv, axis=-1, keepdims=True) + EPS) * g_ref[...])
        rows = 256
        for r0 in range(0, T, rows):
            htv[:, r0:r0 + rows] = hv[r0:r0 + rows, :].T
        locs.append(pltpu.make_async_copy(htv, ht_o, loc_sems.at[3]))
        locs[-1].start()

        def out_copy(p, j):
            return pltpu.make_async_copy(stage.at[p], proj_o.at[j], out_sems.at[p])

        def project(nth, block):
            base = 3 * index(*block)

            def tile(p, carry):
                if nth > 0:
                    out_copy(p, base + p).wait()
                stage[p] = _dot(hv[...], wv[base + p], 1, 0)
                out_copy(p, base + p).start()
                return carry

            lax.fori_loop(0, 3, tile, 0)

        project(0, me)
        copy(0, 0, sibling, me).wait_recv()
        save(sibling)
        project(1, sibling)
        passed = []
        relay_from = (px ^ (1 - c), py ^ c, c)
        relay_to = (px ^ c, py ^ (1 - c), c)

        def arrived(w, j):
            copy(1 + j, w, (*chips[j], c), me).wait_recv()
            passed.append(copy(4 + j, w, (*chips[j], c), sibling))
            passed[-1].start()

        def relay(w):
            passed.append(copy(3, w, relay_from, relay_to))
            passed[-1].start()

        def handed(nth, j):
            copy(4 + j, 0, (*chips[j], 1 - c), me).wait_recv()
            save((*chips[j], 1 - c))
            project(nth, (*chips[j], 1 - c))

        arrived(0, 0)
        arrived(0, 1)
        relay(0)
        for j in range(2):
            save((*chips[j], c))
            project(2 + j, (*chips[j], c))
        handed(4, 0)
        handed(5, 1)
        arrived(1, 0)
        arrived(1, 1)
        relay(1)
        arrived(0, 2)
        save((*chips[2], c))
        project(6, (*chips[2], c))
        handed(7, 2)
        arrived(1, 2)
        copy(0, 1, sibling, me).wait_recv()
        for j, chip in enumerate(chips):
            copy(4 + j, 1, (*chip, 1 - c), me).wait_recv()
        keep = pltpu.make_async_copy(wv, wt_o, loc_sems.at[4])
        for p in range(3):
            out_copy(p, p).wait()
        for cp in first + passed:
            cp.wait_send()
        keep.wait()
        for cp in locs:
            cp.wait()

    vmem = pl.BlockSpec(memory_space=pltpu.VMEM)
    bf16 = jnp.bfloat16
    return pl.pallas_call(
        body, name="gather_proj",
        out_shape=(pltpu.HBM((D, T), bf16), pltpu.HBM((NTILE, D, TILE), bf16),
                   pltpu.HBM((DMIX // NDEV, D), bf16), pltpu.HBM((NDEV, DMIX // NDEV, D), bf16),
                   pltpu.HBM((NDEV, NGROUP, GROUP // NDEV, GROUP), bf16), pltpu.HBM((NTILE, T, TILE), jnp.float32)),
        in_specs=[ANY] + [vmem] * 4, out_specs=(ANY,) * 6,
        scratch_shapes=[pltpu.VMEM((T, D), jnp.float32),
                        pltpu.VMEM((T, D), bf16), pltpu.VMEM((D, T), bf16), pltpu.VMEM((NTILE, D, TILE), bf16),
                        pltpu.VMEM((DMIX // NDEV, D), bf16), pltpu.VMEM((NGROUP, GROUP // NDEV, GROUP), bf16),
                        pltpu.VMEM((3, T, TILE), jnp.float32),
                        pltpu.SemaphoreType.DMA((14,)), pltpu.SemaphoreType.DMA((14,)),
                        pltpu.SemaphoreType.DMA((6,)), pltpu.SemaphoreType.DMA((3,))],
        compiler_params=_params(vmem_mib=56),
    )(x, g1, w_in, w_out, pool_w)


def _split_start(name, arrays, n_copies, plan):
    k = len(arrays)

    def body(*refs):
        send_sems, recv_sems, token = refs[k], refs[k + 1], refs[-1]
        for i, (src, dst, to) in enumerate(plan(refs[:k])):
            pltpu.make_async_remote_copy(src_ref=src, dst_ref=dst, send_sem=send_sems.at[i],
                                         recv_sem=recv_sems.at[i], device_id=to, device_id_type=MESH).start()
        token[...] = jnp.zeros_like(token)

    out = pl.pallas_call(
        body, name=name,
        out_shape=(pltpu.SemaphoreType.DMA((n_copies,)), pltpu.SemaphoreType.DMA((n_copies,)),
                   *[pltpu.HBM(a.shape, a.dtype) for a in arrays], jax.ShapeDtypeStruct((8, 128), jnp.float32)),
        in_specs=[HBM] * k, out_specs=(SEM, SEM, *[HBM] * k, pl.BlockSpec(memory_space=pltpu.VMEM)),
        input_output_aliases={i: 2 + i for i in range(k)},
        compiler_params=pltpu.CompilerParams(has_side_effects=EFFECT),
    )(*[pltpu.with_memory_space_constraint(a, pltpu.HBM) for a in arrays])
    return out[0], out[1], out[2:2 + k], out[-1]


def _split_wait(name, arrays, send_sems, recv_sems, plan, after):
    k = len(arrays)

    def body(*refs):
        sends, recvs = refs[k], refs[k + 1]
        for i, (src, dst, to) in enumerate(plan(refs[:k])):
            cp = pltpu.make_async_remote_copy(src_ref=src, dst_ref=dst, send_sem=sends.at[i], recv_sem=recvs.at[i],
                                              device_id=to, device_id_type=MESH)
            cp.wait_send()
            cp.wait_recv()

    return pl.pallas_call(
        body, name=name,
        out_shape=tuple(pltpu.HBM(a.shape, a.dtype) for a in arrays),
        in_specs=[HBM] * k + [SEM, SEM, ANY], out_specs=(HBM,) * k,
        input_output_aliases={i: i for i in range(k)},
        compiler_params=pltpu.CompilerParams(has_side_effects=EFFECT),
    )(*arrays, send_sems, recv_sems, after)


def _plan_wout(refs):
    src, land = refs
    x, y, c, me = _place()
    return [(src, land.at[me], _peer(x, y, c, r)) for r in range(1, NDEV)]


def _plan_rest(refs):
    gob, gpf, r_out, r_pool = refs
    x, y, c, me = _place()
    plan = []
    for r in range(1, NDEV):
        plan.append((gob.at[me ^ r], r_out.at[r - 1], _peer(x, y, c, r)))
        plan.append((gpf.at[me ^ r], r_pool.at[r - 1], _peer(x, y, c, r)))
    return plan


def _plan_in(refs):
    sums, landing = refs
    x, y, c, _ = _place()
    plan = []
    for rel, (dx, dy) in enumerate(((1, 0), (0, 1), (1, 1))):
        for p in range(3):
            plan.append((sums.at[rel, p], landing.at[rel, p], (x ^ dx, y ^ dy, c)))
    return plan


def _gather_small(small, *after):
    def body(sm, *refs):
        r_small, send_sems, recv_sems, loc_sem = refs[len(after):]
        x, y, c, me = _place()
        loc = pltpu.make_async_copy(sm, r_small.at[me], loc_sem)
        loc.start()

        def copy(r, src_idx):
            return pltpu.make_async_remote_copy(
                src_ref=sm, dst_ref=r_small.at[src_idx], send_sem=send_sems.at[r - 1], recv_sem=recv_sems.at[r - 1],
                device_id=_peer(x, y, c, r), device_id_type=MESH)

        sends = [copy(r, me) for r in range(1, NDEV)]
        for cp in sends:
            cp.start()
        for r in range(1, NDEV):
            copy(r, me ^ r).wait_recv()
        for cp in sends:
            cp.wait_send()
        loc.wait()

    return pl.pallas_call(
        body, name="gather_small",
        out_shape=jax.ShapeDtypeStruct((NDEV, 8, D), jnp.float32),
        in_specs=[ANY] * (1 + len(after)), out_specs=ANY,
        scratch_shapes=[pltpu.SemaphoreType.DMA((NDEV - 1,)), pltpu.SemaphoreType.DMA((NDEV - 1,)),
                        pltpu.SemaphoreType.DMA],
    )(small, *after)


def _seg_tiles(s):
    return (s + 2) % NSEG


_POOL_SPECS = [pl.BlockSpec((None, T, GROUP), lambda g, base=base: (base + g, 0, 0)) for base in (0, 4)]
_HEAD_SPECS = [pl.BlockSpec((None, T, HEAD), lambda h, base=base: (base + h // 2, 0, h % 2))
               for base in (8, 12, 16, 20)]


def _row_ids(shape):
    return lax.broadcasted_iota(jnp.int32, shape, 0)


BAND_ROWS = 128
HALO = 16


def _window_sum(a, gidx, lead):
    width = lax.shift_left(jnp.int32(2), gidx)
    shape = (BAND_ROWS, BAND_ROWS + HALO)
    t, j = lax.broadcasted_iota(jnp.int32, shape, 0), lax.broadcasted_iota(jnp.int32, shape, 1)
    first = t if lead else t + HALO - width + 1
    band = _bf(jnp.where(j >= first, jnp.where(j < first + width, 1.0, 0.0), 0.0))
    zeros = jnp.zeros((HALO, a.shape[1]), jnp.bfloat16)
    padded = [jnp.concatenate([p, zeros] if lead else [zeros, p], axis=0) for p in _split2(a)]
    out = []
    for r0 in range(0, T, BAND_ROWS):
        slab = jnp.concatenate([p[r0:r0 + BAND_ROWS + HALO] for p in padded], axis=1)
        r = _dot(band, slab, 1, 0)
        out.append(r[:, :a.shape[1]] + r[:, a.shape[1]:])
    return jnp.concatenate(out, axis=0)


def _window_mean(s, gidx):
    inv = jnp.where(gidx == 0, 0.5, jnp.where(gidx == 1, 0.25, jnp.where(gidx == 2, 0.125, 0.0625)))
    width = lax.shift_left(jnp.int32(2), gidx)
    head = s[:16] / jnp.minimum(_row_ids((16, s.shape[1])) + 1, width).astype(jnp.float32)
    return jnp.concatenate([head, s[16:] * inv], axis=0)


def _pool_fwd(proj, pool_w, pool_scale, token):
    def body(u_ref, pg_ref, w_ref, sc_ref, token_any, y_ref):
        del token_any
        gidx = pl.program_id(0)
        u, pg = u_ref[...], pg_ref[...]
        d = _window_mean(_window_sum(u, gidx, False), gidx) - u
        mixed = _dot(_bf(d), w_ref[...], 1, 0)
        y_ref[...] = _bf(mixed * sc_ref[...] * (pg * _sigmoid(pg)))

    return pl.pallas_call(
        body, name="pool_fwd", grid=(NGROUP,),
        in_specs=[*_POOL_SPECS,
                  pl.BlockSpec((None, GROUP, GROUP), lambda g: (g, 0, 0)),
                  pl.BlockSpec((1, GROUP), lambda g: (0, g)), ANY],
        out_specs=pl.BlockSpec((T, GROUP), lambda g: (0, g)),
        out_shape=pltpu.HBM((T, DMIX), jnp.bfloat16),
        compiler_params=_params(("parallel",)),
    )(proj, proj, pool_w, pool_scale, token)


def _tri(lower):
    r = lax.broadcasted_iota(jnp.int32, (CHUNK, CHUNK), 0)
    c = lax.broadcasted_iota(jnp.int32, (CHUNK, CHUNK), 1)
    return (r >= c) if lower else (r <= c)


def _sum_rows_matrix():
    shape = (CHUNK + 16, CHUNK)
    r, c = lax.broadcasted_iota(jnp.int32, shape, 0), lax.broadcasted_iota(jnp.int32, shape, 1)
    run = jnp.where(c <= r, 1.0, 0.0)
    half = jnp.where(c < CHUNK // 2, 1.0, 0.0)
    return _bf(jnp.where(r < CHUNK, run, jnp.where(r < CHUNK + 8, 1.0, half)))


def _rev_sum_matrix():
    shape = (CHUNK, 2 * CHUNK)
    r, c = lax.broadcasted_iota(jnp.int32, shape, 0), lax.broadcasted_iota(jnp.int32, shape, 1)
    return _bf(jnp.where(c < CHUNK, jnp.where(c >= r, 1.0, 0.0), jnp.where(c - CHUNK < r, 1.0, 0.0)))


def _split2(a):
    hi = _bf(a)
    return [hi, _bf(a - hi.astype(jnp.float32))]


def _exact_sums(mat, pieces):
    x = jnp.concatenate([s for p in pieces for s in _split2(p)], axis=1)
    r = _dot(mat, x, 1, 0)
    return [r[:, 2 * j * HEAD:(2 * j + 1) * HEAD] + r[:, (2 * j + 1) * HEAD:(2 * j + 2) * HEAD]
            for j in range(len(pieces))]


def _gates(qv, fl, lb):
    sq = _sigmoid(qv)
    sg = _sigmoid(fl)
    f = lb + (1.0 - lb) * sg
    return dict(sq=sq, qs=qv * sq, sg=sg, f=f, kk=1.0 - f, g=jnp.log2(f))


def _decays(sums):
    big_g = sums[:CHUNK]
    total = sums[CHUNK:CHUNK + 8]
    g_last = jnp.tile(total, (CHUNK // 8, 1))
    g_mid = jnp.tile(sums[CHUNK + 8:], (CHUNK // 8, 1))
    return dict(
        e_q=jnp.exp2(big_g),
        e_k=jnp.exp2(g_last - big_g),
        e_qm=jnp.exp2(jnp.minimum(big_g - g_mid, EXP_CAP)),
        e_km=jnp.exp2(jnp.minimum(g_mid - big_g, EXP_CAP)),
        total8=jnp.exp2(total))


def _group_rows(gi):
    return [pl.ds(pl.multiple_of((gi * NB + j) * CHUNK, CHUNK), CHUNK) for j in range(NB)]


def _lower_bound(lb_ref):
    return _sigmoid(lb_ref[0:1, :] - lb_ref[1:2, :])


def _hgrn_fwd(proj, lb_logits, rec_g, y_in):
    def body(q_ref, f_ref, i_ref, gate_ref, lb_ref, rg_ref, y_any, y_ref, o_ref, st_ref):
        del y_any
        lb = _lower_bound(lb_ref)
        causal = _tri(True)
        smat = _sum_rows_matrix()

        def group(gi, st):
            rows = _group_rows(gi)
            ts = [_gates(q_ref[r, :], f_ref[r, :], lb) for r in rows]
            ds = [_decays(s) for s in _exact_sums(smat, [t["g"] for t in ts])]
            vs = [_bf(i_ref[r, :]) for r in rows]
            q_m = [_bf(t["qs"] * d["e_qm"]) for t, d in zip(ts, ds)]
            k_m = [_bf(t["kk"] * d["e_km"]) for t, d in zip(ts, ds)]
            q_e = [_bf(t["qs"] * d["e_q"]) for t, d in zip(ts, ds)]
            k_e = [_bf(t["kk"] * d["e_k"]) for t, d in zip(ts, ds)]
            a = [_bf(jnp.where(causal, _dot(q_m[j], k_m[j], 1, 1), 0.0)) for j in range(NB)]
            intra = [_dot(a[j], vs[j], 1, 0) for j in range(NB)]
            upd = [_dot(vs[j], k_e[j], 0, 0) for j in range(NB)]
            for j in range(NB):
                st_ref[gi * NB + j] = st
                o_ref[rows[j], :] = intra[j] + _dot(q_e[j], _bf(st), 1, 1)
                st = st * jnp.tile(ds[j]["total8"], (HEAD // 8, 1)) + upd[j]
            return st

        lax.fori_loop(0, NGRP, group, jnp.zeros((HEAD, HEAD), jnp.float32))
        o = o_ref[...]
        rn = o * lax.rsqrt(jnp.mean(o * o, axis=-1, keepdims=True) + EPS)
        gate = gate_ref[...]
        y_ref[...] = _bf(rn * rg_ref[...] * (gate * _sigmoid(gate)))

    return pl.pallas_call(
        body, name="hgrn_fwd", grid=(NHEAD,),
        in_specs=[*_HEAD_SPECS,
                  pl.BlockSpec((2, HEAD), lambda h: (0, h)),
                  pl.BlockSpec((1, HEAD), lambda h: (0, h)),
                  pl.BlockSpec(memory_space=pl.ANY)],
        out_specs=(pl.BlockSpec((T, HEAD), lambda h: (0, NHEAD + h)),
                   pl.BlockSpec((T, HEAD), lambda h: (0, h)),
                   pl.BlockSpec((None, NCHUNK, HEAD, HEAD), lambda h: (h, 0, 0, 0))),
        out_shape=(pltpu.HBM((T, DMIX), jnp.bfloat16), pltpu.HBM((T, D), jnp.float32),
                   pltpu.HBM((NHEAD, NCHUNK, HEAD, HEAD), jnp.float32)),
        input_output_aliases={6: 0},
        compiler_params=_params(("parallel",)),
    )(proj, proj, proj, proj, lb_logits, rec_g, y_in)


def _out_proj_loss(x, y, w_out, target, gf):
    rows = 512
    parts = [slice(k * rows // 2, (k + 1) * rows // 2) for k in range(2)]

    def body(x_ref, y_ref, w_ref, t_ref, g_ref, dz_ref, dzb_ref, sq_ref, dg_ref):
        zs = [x_ref[p, :] + _dot(y_ref[p, :], w_ref[...], 1, 0) for p in parts]
        sq = dg = 0.0
        for p, z in zip(parts, zs):
            r = lax.rsqrt(jnp.mean(z * z, axis=-1, keepdims=True) + EPS)
            zhat = z * r
            err = zhat * g_ref[...] - t_ref[p, :]
            dy = err * (1.0 / D)
            gdy = dy * g_ref[...]
            dz = r * (gdy - zhat * jnp.mean(zhat * gdy, axis=-1, keepdims=True))
            dz_ref[p, :] = dz
            dzb_ref[p, :] = _bf(dz)
            sq = sq + jnp.sum(err * err, axis=0, keepdims=True)
            dg = dg + jnp.sum(zhat * dy, axis=0, keepdims=True)

        @pl.when(pl.program_id(0) == 0)
        def _():
            sq_ref[...] = sq
            dg_ref[...] = dg

        @pl.when(pl.program_id(0) != 0)
        def _():
            sq_ref[...] += sq
            dg_ref[...] += dg

    tile = pl.BlockSpec((rows, D), lambda i: (i, 0))
    vec = pl.BlockSpec((1, D), lambda i: (0, 0))
    return pl.pallas_call(
        body, name="out_proj_loss", grid=(T // rows,),
        in_specs=[tile, pl.BlockSpec((rows, DMIX), lambda i: (i, 0)), pl.BlockSpec((DMIX, D), lambda i: (0, 0)),
                  tile, vec],
        out_specs=(tile, tile, vec, vec),
        out_shape=(pltpu.HBM((T, D), jnp.float32), pltpu.HBM((T, D), jnp.bfloat16),
                   jax.ShapeDtypeStruct((1, D), jnp.float32), jax.ShapeDtypeStruct((1, D), jnp.float32)),
        compiler_params=_params(("arbitrary",)),
    )(x, y, w_out, target, gf)


def _out_proj_bwd(dzb, w_out, y):
    tn = 512

    def body(dz_ref, w_ref, y_ref, dy_ref, gw_ref, gwb_ref):
        dz = dz_ref[...]
        dy_ref[...] = _dot(dz, w_ref[...], 1, 1)
        gw = _dot(y_ref[...], dz, 0, 0)
        gw_ref[...] = gw
        gwb_ref[...] = _bf(gw)

    return pl.pallas_call(
        body, name="out_proj_bwd", grid=(DMIX // tn,),
        in_specs=[pl.BlockSpec((T, D), lambda n: (0, 0)), pl.BlockSpec((tn, D), lambda n: (n, 0)),
                  pl.BlockSpec((T, tn), lambda n: (0, n))],
        out_specs=(pl.BlockSpec((T, tn), lambda n: (0, n)), pl.BlockSpec((tn, D), lambda n: (n, 0)),
                   pl.BlockSpec((tn, D), lambda n: (n, 0))),
        out_shape=(pltpu.HBM((T, DMIX), jnp.float32), pltpu.HBM((DMIX, D), jnp.float32),
                   pltpu.HBM((DMIX, D), jnp.bfloat16)),
        compiler_params=_params(("parallel",)),
    )(dzb, w_out, y)


def _hgrn_bwd(proj, lb_logits, rec_g, o, states, dymix, dproj_in, token):
    def body(q_ref, f_ref, i_ref, gate_ref, lb_ref, rg_ref, o_ref, st_ref, dy_ref, dp_any, token_any,
             dp_ref, drg_ref, dlb_ref, do_ref):
        del dp_any, token_any
        lb = _lower_bound(lb_ref)
        causal = _tri(True)
        smat, rmat = _sum_rows_matrix(), _rev_sum_matrix()

        o = o_ref[...]
        rs = lax.rsqrt(jnp.mean(o * o, axis=-1, keepdims=True) + EPS)
        rn = o * rs
        gate = gate_ref[...]
        sgate = _sigmoid(gate)
        dyv = dy_ref[...]
        d_r = dyv * (gate * sgate)
        dp_ref[3] = _bf(dyv * (rn * rg_ref[...]) * (sgate * (1.0 + gate * (1.0 - sgate))))
        drg_ref[...] = jnp.sum(d_r * rn, axis=0, keepdims=True)
        drn = d_r * rg_ref[...]
        do_ref[...] = rs * (drn - rn * jnp.mean(rn * drn, axis=-1, keepdims=True))

        def group(i, carry):
            dst, dlb = carry
            gi = NGRP - 1 - i
            rows = _group_rows(gi)
            span = range(NB)
            qvs = [q_ref[r, :] for r in rows]
            ts = [_gates(qv, f_ref[r, :], lb) for qv, r in zip(qvs, rows)]
            ds = [_decays(s) for s in _exact_sums(smat, [t["g"] for t in ts])]
            vs = [_bf(i_ref[r, :]) for r in rows]
            dos = [_bf(do_ref[r, :]) for r in rows]
            sts = [st_ref[gi * NB + j] for j in span]
            qe_f = [t["qs"] * d["e_q"] for t, d in zip(ts, ds)]
            ke_f = [t["kk"] * d["e_k"] for t, d in zip(ts, ds)]
            q_e, k_e = [_bf(a) for a in qe_f], [_bf(a) for a in ke_f]
            q_m = [_bf(t["qs"] * d["e_qm"]) for t, d in zip(ts, ds)]
            k_m = [_bf(t["kk"] * d["e_km"]) for t, d in zip(ts, ds)]
            a = [_bf(jnp.where(causal, _dot(q_m[j], k_m[j], 1, 1), 0.0)) for j in span]
            da = [_bf(jnp.where(causal, _dot(dos[j], vs[j], 1, 1), 0.0)) for j in span]
            dqm = [_dot(da[j], k_m[j], 1, 0) for j in span]
            dkm = [_dot(da[j], q_m[j], 0, 0) for j in span]
            dv_in = [_dot(a[j], dos[j], 0, 0) for j in span]
            dqe = [_dot(dos[j], _bf(sts[j]), 1, 0) for j in span]
            grow = [_dot(dos[j], q_e[j], 0, 0) for j in span]
            dke, carried = [None] * NB, [None] * NB
            for j in reversed(span):
                dst_b = _bf(dst)
                dke[j] = _dot(vs[j], dst_b, 1, 0)
                dp_ref[2, rows[j], :] = _bf(dv_in[j] + _dot(k_e[j], dst_b, 1, 1))
                carried[j] = ds[j]["total8"] * jnp.sum(dst * sts[j], axis=0, keepdims=True)
                dst = dst * jnp.tile(ds[j]["total8"], (HEAD // 8, 1)) + grow[j]
            kdk = [ke_f[j] * dke[j] for j in span]
            pos = [(q_m[j].astype(jnp.float32) * dqm[j] - k_m[j].astype(jnp.float32) * dkm[j]) + qe_f[j] * dqe[j]
                   for j in span]
            dgs = _exact_sums(rmat, [jnp.concatenate([pos[j], kdk[j]], axis=0) for j in span])
            for j in span:
                t, d = ts[j], ds[j]
                dg = dgs[j] + jnp.tile(carried[j], (CHUNK // 8, 1))
                dqs = dqm[j] * d["e_qm"] + dqe[j] * d["e_q"]
                dkk = dkm[j] * d["e_km"] + dke[j] * d["e_k"]
                df = dg / t["f"] - dkk
                dp_ref[1, rows[j], :] = _bf(df * (1.0 - lb) * (t["sg"] * (1.0 - t["sg"])))
                dp_ref[0, rows[j], :] = _bf(dqs * (t["sq"] * (1.0 + qvs[j] * (1.0 - t["sq"]))))
                dlb = dlb + df * (1.0 - t["sg"])
            return dst, dlb

        _, dlb = lax.fori_loop(0, NGRP, group, (jnp.zeros((HEAD, HEAD), jnp.float32),
                                                jnp.zeros((CHUNK, HEAD), jnp.float32)))
        dlb_ref[...] = jnp.sum(dlb, axis=0, keepdims=True)

    vec = pl.BlockSpec((1, HEAD), lambda h: (0, h))
    return pl.pallas_call(
        body, name="hgrn_bwd", grid=(NHEAD,),
        in_specs=[*_HEAD_SPECS,
                  pl.BlockSpec((2, HEAD), lambda h: (0, h)), vec,
                  pl.BlockSpec((T, HEAD), lambda h: (0, h)),
                  pl.BlockSpec((None, NCHUNK, HEAD, HEAD), lambda h: (h, 0, 0, 0)),
                  pl.BlockSpec((T, HEAD), lambda h: (0, NHEAD + h)), ANY, ANY],
        out_specs=(pl.BlockSpec((4, T, HEAD), lambda h: (0, 0, h)), vec, vec),
        out_shape=(pltpu.HBM((NSEG, T, D), jnp.bfloat16),
                   jax.ShapeDtypeStruct((1, D), jnp.float32), jax.ShapeDtypeStruct((1, D), jnp.float32)),
        scratch_shapes=[pltpu.VMEM((T, HEAD), jnp.float32)],
        input_output_aliases={9: 0},
        compiler_params=_params(("parallel",)),
    )(proj, proj, proj, proj, lb_logits, rec_g, o, states, dymix, dproj_in, token)


def _pool_bwd(proj, pool_w, pool_scale, dymix):
    def body(u_ref, pg_ref, w_ref, sc_ref, dy_ref, dp_ref, gw_ref, gs_ref):
        gidx = pl.program_id(0)
        u, pg = u_ref[...], pg_ref[...]
        d = _bf(_window_mean(_window_sum(u, gidx, False), gidx) - u)
        mixed = _dot(d, w_ref[...], 1, 0)
        spg = _sigmoid(pg)
        dyv = dy_ref[...]
        d_p = dyv * (pg * spg)
        dp_ref[1] = _bf(dyv * (mixed * sc_ref[...]) * (spg * (1.0 + pg * (1.0 - spg))))
        gs_ref[...] = jnp.sum(d_p * mixed, axis=0, keepdims=True)
        dmixed = _bf(d_p * sc_ref[...])
        gw_ref[...] = _dot(d, dmixed, 0, 0)
        dd = _dot(dmixed, w_ref[...], 1, 1)
        dp_ref[0] = _bf(_window_sum(_window_mean(dd, gidx), gidx, True) - dd)

    return pl.pallas_call(
        body, name="pool_bwd", grid=(NGROUP,),
        in_specs=[*_POOL_SPECS,
                  pl.BlockSpec((None, GROUP, GROUP), lambda g: (g, 0, 0)),
                  pl.BlockSpec((1, GROUP), lambda g: (0, g)),
                  pl.BlockSpec((T, GROUP), lambda g: (0, g))],
        out_specs=(pl.BlockSpec((2, T, GROUP), lambda g: (2, 0, g)),
                   pl.BlockSpec((None, GROUP, GROUP), lambda g: (g, 0, 0)),
                   pl.BlockSpec((1, GROUP), lambda g: (0, g))),
        out_shape=(pltpu.HBM((NSEG, T, D), jnp.bfloat16),
                   jax.ShapeDtypeStruct((NGROUP, GROUP, GROUP), jnp.float32),
                   jax.ShapeDtypeStruct((1, D), jnp.float32)),
        compiler_params=_params(("parallel",)),
    )(proj, proj, pool_w, pool_scale, dymix)


HALF = NTILE // 2
AWAY = HALF - 3


def _dproj_tile(chip, side, p):
    j = 6 * chip + 3 * side + p
    return ((j // 4 + 4) % NSEG, 0, j % 4)


def _sibling_copy(sib_out, sib_in, send_sems, recv_sems, slot):
    x, y, c, _ = _place()
    return pltpu.make_async_remote_copy(
        src_ref=sib_out.at[slot], dst_ref=sib_in.at[slot], send_sem=send_sems.at[slot],
        recv_sem=recv_sems.at[slot], device_id=(x, y, 1 - c), device_id_type=MESH)


def _proj_bwd_w_far(place, ht, dproj):
    def body(place_ref, h_ref, dp_ref, sib_out, sib_in, send_sems, recv_sems, stage, loc_sems):
        del place_ref
        i = pl.program_id(0)

        def to_hbm(k):
            return pltpu.make_async_copy(stage.at[k], sib_out.at[k], loc_sems.at[k])

        def send(k):
            to_hbm(k).wait()
            _sibling_copy(sib_out, sib_in, send_sems, recv_sems, k).start()

        @pl.when(i > 0)
        def _():
            send(i - 1)

        stage[i] = _bf(_dot(h_ref[...], dp_ref[...], 1, 0))
        to_hbm(i).start()

        @pl.when(i == HALF - 1)
        def _():
            send(i)

    buf = pltpu.HBM((HALF, D, TILE), jnp.bfloat16)
    sems = pltpu.SemaphoreType.DMA((HALF,))
    return pl.pallas_call(
        body, name="proj_bwd_w_far",
        grid_spec=pltpu.PrefetchScalarGridSpec(
            num_scalar_prefetch=1, grid=(HALF,),
            in_specs=[pl.BlockSpec((D, T), lambda i, pr: (0, 0)),
                      pl.BlockSpec((None, T, TILE), lambda i, pr: _dproj_tile(i // 3, 1 - pr[2], i % 3))],
            out_specs=(HBM, HBM, SEM, SEM),
            scratch_shapes=[pltpu.VMEM((HALF, D, TILE), jnp.bfloat16), pltpu.SemaphoreType.DMA((HALF,))]),
        out_shape=(buf, buf, sems, sems),
        compiler_params=pltpu.CompilerParams(dimension_semantics=("arbitrary",), vmem_limit_bytes=48 * MIB,
                                             has_side_effects=EFFECT),
    )(place, ht, dproj)


def _proj_bwd_w_near(place, ht, dproj, sib_out, sib_in, sib_send, sib_recv):
    def owner_chip(k, pr):
        return (pr[1] + 1 + k // 3) % 4

    def body(place_ref, h_ref, dp_ref, sib_out, sib_in, sib_send, sib_recv, sums, own_ref, landing, out_send,
             out_recv, recvbuf, outbuf, in_sems, loc_sems):
        i = pl.program_id(0)
        px, py, c, _ = _place()

        def slot_of(k):
            return 3 * owner_chip(k, place_ref) + k % 3

        def load(k):
            return pltpu.make_async_copy(sib_in.at[slot_of(k)], recvbuf.at[k % 2], in_sems.at[k % 2])

        def fetch(k):
            _sibling_copy(sib_out, sib_in, sib_send, sib_recv, slot_of(k)).wait_recv()
            load(k).start()

        def route(k):
            chip = owner_chip(k, place_ref)
            cx, cy = chip // 2, chip % 2
            return cx, cy, (cx ^ px) + 2 * (cy ^ py) - 1, k % 3

        def to_hbm(k):
            _, _, rel, p = route(k)
            return pltpu.make_async_copy(outbuf.at[k], sums.at[rel, p], loc_sems.at[k])

        def to_owner(k):
            cx, cy, rel, p = route(k)
            return pltpu.make_async_remote_copy(
                src_ref=sums.at[rel, p], dst_ref=landing.at[rel, p], send_sem=out_send.at[3 * rel + p],
                recv_sem=out_recv.at[3 * rel + p], device_id=(cx, cy, c), device_id_type=MESH)

        @pl.when(i == 0)
        def _():
            fetch(i)

        @pl.when(i < HALF - 1)
        def _():
            fetch(i + 1)

        @pl.when(jnp.logical_and(i > 0, i <= AWAY))
        def _():
            to_hbm(i - 1).wait()
            to_owner(i - 1).start()

        gw = _dot(h_ref[...], dp_ref[...], 1, 0)
        load(i).wait()
        total = gw + recvbuf[i % 2].astype(jnp.float32)
        own_ref[...] = total

        @pl.when(i < AWAY)
        def _():
            outbuf[i] = _bf(total)
            to_hbm(i).start()

        @pl.when(i == HALF - 1)
        def _():
            for slot in range(HALF):
                _sibling_copy(sib_out, sib_in, sib_send, sib_recv, slot).wait_send()

    travelling = pltpu.HBM((3, 3, D, TILE), jnp.bfloat16)
    sems = pltpu.SemaphoreType.DMA((AWAY,))
    return pl.pallas_call(
        body, name="proj_bwd_w_near",
        grid_spec=pltpu.PrefetchScalarGridSpec(
            num_scalar_prefetch=1, grid=(HALF,),
            in_specs=[pl.BlockSpec((D, T), lambda i, pr: (0, 0)),
                      pl.BlockSpec((None, T, TILE), lambda i, pr: _dproj_tile(owner_chip(i, pr), pr[2], i % 3)),
                      HBM, HBM, SEM, SEM],
            out_specs=(HBM, pl.BlockSpec((None, D, TILE), lambda i, pr: (jnp.where(i < AWAY, 0, i % 3), 0, 0)),
                       HBM, SEM, SEM),
            scratch_shapes=[pltpu.VMEM((2, D, TILE), jnp.bfloat16), pltpu.VMEM((AWAY, D, TILE), jnp.bfloat16),
                            pltpu.SemaphoreType.DMA((2,)), pltpu.SemaphoreType.DMA((AWAY,))]),
        out_shape=(travelling, pltpu.HBM((3, D, TILE), jnp.float32), travelling, sems, sems),
        compiler_params=pltpu.CompilerParams(dimension_semantics=("arbitrary",), vmem_limit_bytes=48 * MIB,
                                             has_side_effects=EFFECT),
    )(place, ht, dproj, sib_out, sib_in, sib_send, sib_recv)


def _proj_bwd_x(dproj, w_t, x, g1, dz, token):
    tm = 512
    pairs = NSEG // 2

    def body(dp_ref, w_ref, x_ref, g_ref, dz_ref, token_any, dx_ref, dg_ref, wcat, acc):
        del token_any
        s, m = pl.program_id(0), pl.program_id(1)

        @pl.when(m == 0)
        def _():
            for i in range(8):
                wcat[:, i * TILE:(i + 1) * TILE] = w_ref[i]

        mine = pl.ds(pl.multiple_of(m * tm, tm), tm)

        @pl.when(s == 0)
        def _():
            acc[mine, :] = jnp.zeros((tm, D), jnp.float32)

        acc[mine, :] += _dot(jnp.concatenate([dp_ref[0], dp_ref[1]], axis=1), wcat[...], 1, 1)

        @pl.when(s == pairs - 1)
        def _():
            xv = x_ref[...]
            rs = lax.rsqrt(jnp.mean(xv * xv, axis=-1, keepdims=True) + EPS)
            xhat = xv * rs
            dhv = acc[mine, :]
            gdh = dhv * g_ref[...]
            dx_ref[...] = dz_ref[...] + rs * (gdh - xhat * jnp.mean(xhat * gdh, axis=-1, keepdims=True))
            dg = jnp.sum(xhat * dhv, axis=0, keepdims=True)

            @pl.when(m == 0)
            def _():
                dg_ref[...] = dg

            @pl.when(m != 0)
            def _():
                dg_ref[...] += dg

    rows = pl.BlockSpec((tm, D), lambda s, m: (jnp.where(s == pairs - 1, m, 0), 0))
    vec = pl.BlockSpec((1, D), lambda s, m: (0, 0))
    return pl.pallas_call(
        body, name="proj_bwd_x", grid=(pairs, T // tm),
        in_specs=[pl.BlockSpec((2, tm, D), lambda s, m: (s, m, 0)),
                  pl.BlockSpec((8, D, TILE), lambda s, m: ((s + 1) % pairs, 0, 0)), rows, vec, rows, ANY],
        out_specs=(rows, vec),
        out_shape=(jax.ShapeDtypeStruct((T, D), jnp.float32), jax.ShapeDtypeStruct((1, D), jnp.float32)),
        scratch_shapes=[pltpu.VMEM((D, 2 * D), jnp.bfloat16), pltpu.VMEM((T, D), jnp.float32)],
        compiler_params=_params(("arbitrary", "arbitrary"), vmem_mib=56),
    )(dproj, w_t, x, g1, dz, token)


def _adamw(w, g, m, v):
    m_new = ADAM_B1 * m + (1.0 - ADAM_B1) * g
    v_new = ADAM_B2 * v + (1.0 - ADAM_B2) * (g * g)
    delta = -ADAM_LR * ((m_new / BC1) / (jnp.sqrt(v_new / BC2) + ADAM_EPS) + ADAM_WD * w)
    return delta, m_new, v_new


def _reduce_adam(name, place, parts, w, m, v, grid, w_spec):
    n = len(parts)

    def body(place_ref, *refs):
        del place_ref
        w_ref, m_ref, v_ref, g_ref, d_ref, mo_ref, vo_ref = refs[n:]
        g = None
        for ref, (_, _, stacked) in zip(refs[:n], parts):
            terms = [ref[r] for r in range(ref.shape[0])] if stacked else [ref[...]]
            for t in terms:
                if t.shape[-1] != w_ref.shape[-1]:
                    t = jnp.concatenate([t[p] for p in range(t.shape[0])], axis=1)
                g = t.astype(jnp.float32) if g is None else g + t.astype(jnp.float32)
        delta, m_new, v_new = _adamw(w_ref[...], g, m_ref[...], v_ref[...])
        g_ref[...] = g
        d_ref[...] = delta
        mo_ref[...] = m_new
        vo_ref[...] = v_new

    shape = jax.ShapeDtypeStruct(w.shape, jnp.float32)
    return pl.pallas_call(
        body, name=name,
        grid_spec=pltpu.PrefetchScalarGridSpec(
            num_scalar_prefetch=1, grid=grid,
            in_specs=[spec for _, spec, _ in parts] + [w_spec] * 3, out_specs=(w_spec,) * 4),
        out_shape=(shape,) * 4,
        compiler_params=_params(("parallel",)),
    )(place, *[_in_hbm(a) for a in [a for a, _, _ in parts] + [w, m, v]])


def _small_adam(parts, w, m, v):
    def body(p_ref, w_ref, m_ref, v_ref, g_ref, d_ref, mo_ref, vo_ref):
        g = p_ref[0]
        for s in range(1, NDEV):
            g = g + p_ref[s]
        wv = w_ref[...]
        rows = _row_ids(wv.shape)
        other = jnp.where(rows == 2, pltpu.roll(wv, 7, 0), jnp.where(rows == 3, pltpu.roll(wv, 1, 0), 0.0))
        lbv = _sigmoid(wv - other)
        sign = jnp.where(rows == 2, 1.0, -1.0)
        g = jnp.where((rows == 2) | (rows == 3), sign * g * lbv * (1.0 - lbv), g)
        delta, m_new, v_new = _adamw(wv, g, m_ref[...], v_ref[...])
        g_ref[...] = g
        d_ref[...] = delta
        mo_ref[...] = m_new
        vo_ref[...] = v_new

    shape = jax.ShapeDtypeStruct((8, D), jnp.float32)
    return pl.pallas_call(body, name="small_adam", out_shape=(shape,) * 4)(parts, w, m, v)


def _rows8(*vecs):
    rows = [a.reshape(-1, D) for a in vecs]
    n = sum(r.shape[0] for r in rows)
    return jnp.concatenate(rows + [jnp.zeros((8 - n, D), jnp.float32)], axis=0)


def kernel(x, norm1_g, w_in, pool_w, pool_scale, lb_logits, rec_norm_g, w_out, final_norm_g, loss_target, m_norm1_g, m_w_in, m_pool_w, m_pool_scale, m_lb_logits, m_rec_norm_g, m_w_out, m_final_norm_g, v_norm1_g, v_w_in, v_pool_w, v_pool_scale, v_lb_logits, v_rec_norm_g, v_w_out, v_final_norm_g):
    xs = x[0]
    target = loss_target[0]
    ix, iy, ic = lax.axis_index("x"), lax.axis_index("y"), lax.axis_index("c")
    place = jnp.stack([4 * ix + 2 * iy + ic, 2 * ix + iy, ic]).astype(jnp.int32)
    gf = final_norm_g.reshape(1, D)

    ht, w_t, w_out_b, w_out_g, pool_g, proj = _gather_proj(xs, norm1_g, w_in, w_out, pool_w)
    pool_full = pool_g.transpose(1, 0, 2, 3).reshape(NGROUP, GROUP, GROUP)
    wout = [w_out_b, w_out_g]
    wout_send, wout_recv, wout, wout_token = _split_start("gather_wout_start", wout, NDEV - 1, _plan_wout)

    y = _pool_fwd(proj, pool_full, pool_scale, wout_token)
    y, o, states = _hgrn_fwd(proj, lb_logits, rec_norm_g, y)
    _, w_out_g = _split_wait("gather_wout_wait", wout, wout_send, wout_recv, _plan_wout, o)
    w_out_full = _in_hbm(w_out_g.reshape(DMIX, D))
    dz, dzb, sq, dgf = _out_proj_loss(xs, y, w_out_full, target, gf)

    dymix, gwout_f, gwout_b = _out_proj_bwd(dzb, w_out_full, y)
    dproj, gpool, dscale = _pool_bwd(proj, pool_full, pool_scale, dymix)

    blk_out = (NDEV, DMIX // NDEV, D)
    blk_pool = (NDEV, NGROUP, GROUP // NDEV, GROUP)
    gpool_s = gpool.reshape(NGROUP, NDEV, GROUP // NDEV, GROUP).transpose(1, 0, 2, 3)
    rest = [gwout_b.reshape(blk_out), gpool_s,
            lax.empty((NDEV - 1,) + blk_out[1:], jnp.bfloat16), lax.empty((NDEV - 1,) + blk_pool[1:], jnp.float32)]
    rest_send, rest_recv, rest, rest_token = _split_start("scatter_rest_start", rest, 2 * (NDEV - 1), _plan_rest)

    dproj, drecg, dlb = _hgrn_bwd(proj, lb_logits, rec_norm_g, o, states, dymix, dproj, rest_token)
    chip_sums, own_sum, landing, win_send, win_recv = _proj_bwd_w_near(
        place, ht, dproj, *_proj_bwd_w_far(place, ht, dproj))
    win = [chip_sums, landing]

    grad_x, dg1 = _proj_bwd_x(dproj, w_t, xs, norm1_g, dz, chip_sums)

    _, gpool_own, r_out, r_pool = _split_wait("scatter_rest_wait", rest, rest_send, rest_recv, _plan_rest, grad_x)
    g_wout, d_wout, m_wout, v_wout = _reduce_adam(
        "adam_w_out", place,
        [(gwout_f.reshape(blk_out), pl.BlockSpec((None,) + blk_out[1:], lambda i, pr: (pr[0], 0, 0)), False),
         (r_out, pl.BlockSpec((NDEV - 1,) + blk_out[1:], lambda i, pr: (0, 0, 0)), True)],
        w_out, m_w_out, v_w_out, (1,), pl.BlockSpec((None,) + blk_out[1:], lambda i, pr: (0, 0, 0)))
    g_pool, d_pool, m_pool, v_pool = _reduce_adam(
        "adam_pool_w", place,
        [(gpool_own, pl.BlockSpec((None,) + blk_pool[1:], lambda i, pr: (pr[0], 0, 0, 0)), False),
         (r_pool, pl.BlockSpec((NDEV - 1,) + blk_pool[1:], lambda i, pr: (0, 0, 0, 0)), True)],
        pool_w, m_pool_w, v_pool_w, (1,), pl.BlockSpec((None,) + blk_pool[1:], lambda i, pr: (0, 0, 0, 0)))

    r_small = _gather_small(_rows8(dg1, dscale, dlb, dlb, drecg, dgf, sq), d_wout, d_pool)
    g_s, d_s, m_s, v_s = _small_adam(
        r_small,
        _rows8(norm1_g, pool_scale, lb_logits, rec_norm_g, final_norm_g),
        _rows8(m_norm1_g, m_pool_scale, m_lb_logits, m_rec_norm_g, m_final_norm_g),
        _rows8(v_norm1_g, v_pool_scale, v_lb_logits, v_rec_norm_g, v_final_norm_g))
    loss = jnp.sum(g_s[6]) * (0.5 / D)

    _, r_in = _split_wait("scatter_win_wait", win, win_send, win_recv, _plan_in, v_s)
    g_win, d_win, m_win, v_win = _reduce_adam(
        "adam_w_in", place,
        [(own_sum, pl.BlockSpec((3, D // 8, TILE), lambda i, pr: (0, i, 0)), False),
         (r_in, pl.BlockSpec((3, 3, D // 8, TILE), lambda i, pr: (0, 0, i, 0)), True)],
        w_in, m_w_in, v_w_in, (8,), pl.BlockSpec((None, D // 8, 3 * TILE), lambda i, pr: (0, i, 0)))

    def small_outs(a):
        return a[0:1], a[1:2], a[2:4], a[4:5], a[5]

    def outs(small_a, win, pool, wout):
        n1, ps, lbl, rg, fg = small_outs(small_a)
        return n1, win, pool, ps, lbl, rg, wout, fg

    return (loss, grad_x[None],
            *outs(g_s, g_win, g_pool, g_wout), *outs(d_s, d_win, d_pool, d_wout),
            *outs(m_s, m_win, m_pool, m_wout), *outs(v_s, v_win, v_pool, v_wout))
```

```python
import functools

import jax
import jax.numpy as jnp
from jax import lax
from jax.experimental import pallas as pl
from jax.experimental.pallas import tpu as pltpu

T = 2048
D = 1024
NSEG = 6
NTILE = 24
TILE = 256
DMIX = 2048
NDEV = 8
HEAD = 128
NHEAD = 8
CHUNK = 64
NCHUNK = T // CHUNK
NB = 32
NGRP = NCHUNK // NB
NGROUP = 4
GROUP = 256
EPS = 1e-6
EXP_CAP = 115.0
MESH = pl.DeviceIdType.MESH
AXES = ("x", "y", "c")
ANY = pl.BlockSpec(memory_space=pl.ANY)
HBM = pl.BlockSpec(memory_space=pltpu.HBM)
SEM = pl.BlockSpec(memory_space=pltpu.SEMAPHORE)
EFFECT = pltpu.SideEffectType.DATAFLOW_SIDE_EFFECTING

ADAM_LR = 0.001
ADAM_B1 = 0.9
ADAM_B2 = 0.999
ADAM_EPS = 1e-08
ADAM_WD = 0.01
ADAM_STEP = 10
BC1 = 1.0 - ADAM_B1 ** ADAM_STEP
BC2 = 1.0 - ADAM_B2 ** ADAM_STEP

MIB = 1 << 20


def _params(sem=None, vmem_mib=48):
    return pltpu.CompilerParams(dimension_semantics=sem, vmem_limit_bytes=vmem_mib * MIB)


def _sigmoid(v):
    return 1.0 / (1.0 + jnp.exp(-v))


def _dot(a, b, ca, cb, precision=None):
    return lax.dot_general(a, b, (((ca,), (cb,)), ((), ())), precision=precision,
                           preferred_element_type=jnp.float32)


def _bf(v):
    return v.astype(jnp.bfloat16)


def _in_hbm(a):
    return pltpu.with_memory_space_constraint(a, pltpu.HBM)


def _place():
    x, y, c = lax.axis_index("x"), lax.axis_index("y"), lax.axis_index("c")
    return x, y, c, 4 * x + 2 * y + c


def _peer(x, y, c, r):
    return (x ^ ((r >> 2) & 1), y ^ ((r >> 1) & 1), c ^ (r & 1))


def _gather_proj(x, g1, w_in, w_out, pool_w):
    def body(x_ref, g_ref, win_ref, wout_ref, pool_ref, ht_o, wt_o, woutb_o, wout_o, pool_o, proj_o,
             xbuf, hv, htv, wv, wob, pb, stage, send_sems, recv_sems, loc_sems, out_sems):
        px, py, c, my_idx = _place()
        fetch_x = pltpu.make_async_copy(x_ref, xbuf, loc_sems.at[5])
        fetch_x.start()
        me, sibling = (px, py, c), (px, py, 1 - c)
        chips = [(1 - px, py), (px, 1 - py), (1 - px, 1 - py)]
        for p in range(3):
            wv[3 * my_idx + p] = _bf(win_ref[0, :, p * TILE:(p + 1) * TILE])

        def index(bx, by, bc):
            return 4 * bx + 2 * by + bc

        def slot(w, block):
            return wv.at[pl.ds(3 * index(*block), 3)] if w == 0 else pool_o.at[index(*block)]

        def copy(k, w, block, to, src=None):
            return pltpu.make_async_remote_copy(
                src_ref=slot(w, block) if src is None else src, dst_ref=slot(w, block),
                send_sem=send_sems.at[2 * k + w], recv_sem=recv_sems.at[2 * k + w],
                device_id=to, device_id_type=MESH)

        def save(block):
            at = pl.ds(3 * index(*block), 3)
            pltpu.make_async_copy(wv.at[at], wt_o.at[at], loc_sems.at[4]).start()

        srcs = (slot(0, me), pb)
        first = []
        for w in (0, 1):
            if w == 1:
                pb[...] = _bf(pool_ref[0])
                wob[...] = _bf(wout_ref[0])
            group = [copy(1 + j, w, me, (*chip, c), src=srcs[w]) for j, chip in enumerate(chips[:2])]
            group.append(copy(0, w, me, sibling, src=srcs[w]))
            for cp in group:
                cp.start()
            first += group
        save(me)
        locs = [pltpu.make_async_copy(pb, slot(1, me), loc_sems.at[0]),
                pltpu.make_async_copy(wob, wout_o.at[my_idx], loc_sems.at[1]),
                pltpu.make_async_copy(wob, woutb_o, loc_sems.at[2])]
        for cp in locs:
            cp.start()

        fetch_x.wait()
        xv = xbuf[...]
        hv[...] = _bf(xv * lax.rsqrt(jnp.mean(xv * xv, axis=-1, keepdims=True) + EPS) * g_ref[...])
        rows = 256
        for r0 in range(0, T, rows):
            htv[:, r0:r0 + rows] = hv[r0:r0 + rows, :].T
        locs.append(pltpu.make_async_copy(htv, ht_o, loc_sems.at[3]))
        locs[-1].start()

        def out_copy(p, j):
            return pltpu.make_async_copy(stage.at[p], proj_o.at[j], out_sems.at[p])

        def project(nth, block):
            base = 3 * index(*block)

            def tile(p, carry):
                if nth > 0:
                    out_copy(p, base + p).wait()
                stage[p] = _dot(hv[...], wv[base + p], 1, 0)
                out_copy(p, base + p).start()
                return carry

            lax.fori_loop(0, 3, tile, 0)

        project(0, me)
        copy(0, 0, sibling, me).wait_recv()
        save(sibling)
        project(1, sibling)
        passed = []
        relay_from = (px ^ (1 - c), py ^ c, c)
        relay_to = (px ^ c, py ^ (1 - c), c)

        def arrived(w, j):
            copy(1 + j, w, (*chips[j], c), me).wait_recv()
            passed.append(copy(4 + j, w, (*chips[j], c), sibling))
            passed[-1].start()

        def relay(w):
            passed.append(copy(3, w, relay_from, relay_to))
            passed[-1].start()

        def handed(nth, j):
            copy(4 + j, 0, (*chips[j], 1 - c), me).wait_recv()
            save((*chips[j], 1 - c))
            project(nth, (*chips[j], 1 - c))

        arrived(0, 0)
        arrived(0, 1)
        relay(0)
        for j in range(2):
            save((*chips[j], c))
            project(2 + j, (*chips[j], c))
        handed(4, 0)
        handed(5, 1)
        arrived(1, 0)
        arrived(1, 1)
        relay(1)
        arrived(0, 2)
        save((*chips[2], c))
        project(6, (*chips[2], c))
        handed(7, 2)
        arrived(1, 2)
        copy(0, 1, sibling, me).wait_recv()
        for j, chip in enumerate(chips):
            copy(4 + j, 1, (*chip, 1 - c), me).wait_recv()
        keep = pltpu.make_async_copy(wv, wt_o, loc_sems.at[4])
        for p in range(3):
            out_copy(p, p).wait()
        for cp in first + passed:
            cp.wait_send()
        keep.wait()
        for cp in locs:
            cp.wait()

    vmem = pl.BlockSpec(memory_space=pltpu.VMEM)
    bf16 = jnp.bfloat16
    return pl.pallas_call(
        body, name="gather_proj",
        out_shape=(pltpu.HBM((D, T), bf16), pltpu.HBM((NTILE, D, TILE), bf16),
                   pltpu.HBM((DMIX // NDEV, D), bf16), pltpu.HBM((NDEV, DMIX // NDEV, D), bf16),
                   pltpu.HBM((NDEV, NGROUP, GROUP // NDEV, GROUP), bf16), pltpu.HBM((NTILE, T, TILE), jnp.float32)),
        in_specs=[ANY] + [vmem] * 4, out_specs=(ANY,) * 6,
        scratch_shapes=[pltpu.VMEM((T, D), jnp.float32),
                        pltpu.VMEM((T, D), bf16), pltpu.VMEM((D, T), bf16), pltpu.VMEM((NTILE, D, TILE), bf16),
                        pltpu.VMEM((DMIX // NDEV, D), bf16), pltpu.VMEM((NGROUP, GROUP // NDEV, GROUP), bf16),
                        pltpu.VMEM((3, T, TILE), jnp.float32),
                        pltpu.SemaphoreType.DMA((14,)), pltpu.SemaphoreType.DMA((14,)),
                        pltpu.SemaphoreType.DMA((6,)), pltpu.SemaphoreType.DMA((3,))],
        compiler_params=_params(vmem_mib=56),
    )(x, g1, w_in, w_out, pool_w)


def _split_start(name, arrays, n_copies, plan):
    k = len(arrays)

    def body(*refs):
        send_sems, recv_sems, token = refs[k], refs[k + 1], refs[-1]
        for i, (src, dst, to) in enumerate(plan(refs[:k])):
            pltpu.make_async_remote_copy(src_ref=src, dst_ref=dst, send_sem=send_sems.at[i],
                                         recv_sem=recv_sems.at[i], device_id=to, device_id_type=MESH).start()
        token[...] = jnp.zeros_like(token)

    out = pl.pallas_call(
        body, name=name,
        out_shape=(pltpu.SemaphoreType.DMA((n_copies,)), pltpu.SemaphoreType.DMA((n_copies,)),
                   *[pltpu.HBM(a.shape, a.dtype) for a in arrays], jax.ShapeDtypeStruct((8, 128), jnp.float32)),
        in_specs=[HBM] * k, out_specs=(SEM, SEM, *[HBM] * k, pl.BlockSpec(memory_space=pltpu.VMEM)),
        input_output_aliases={i: 2 + i for i in range(k)},
        compiler_params=pltpu.CompilerParams(has_side_effects=EFFECT),
    )(*[pltpu.with_memory_space_constraint(a, pltpu.HBM) for a in arrays])
    return out[0], out[1], out[2:2 + k], out[-1]


def _split_wait(name, arrays, send_sems, recv_sems, plan, after):
    k = len(arrays)

    def body(*refs):
        sends, recvs = refs[k], refs[k + 1]
        for i, (src, dst, to) in enumerate(plan(refs[:k])):
            cp = pltpu.make_async_remote_copy(src_ref=src, dst_ref=dst, send_sem=sends.at[i], recv_sem=recvs.at[i],
                                              device_id=to, device_id_type=MESH)
            cp.wait_send()
            cp.wait_recv()

    return pl.pallas_call(
        body, name=name,
        out_shape=tuple(pltpu.HBM(a.shape, a.dtype) for a in arrays),
        in_specs=[HBM] * k + [SEM, SEM, ANY], out_specs=(HBM,) * k,
        input_output_aliases={i: i for i in range(k)},
        compiler_params=pltpu.CompilerParams(has_side_effects=EFFECT),
    )(*arrays, send_sems, recv_sems, after)


def _plan_wout(refs):
    src, land = refs
    x, y, c, me = _place()
    return [(src, land.at[me], _peer(x, y, c, r)) for r in range(1, NDEV)]


def _plan_rest(refs):
    gob, gpf, r_out, r_pool = refs
    x, y, c, me = _place()
    plan = []
    for r in range(1, NDEV):
        plan.append((gob.at[me ^ r], r_out.at[r - 1], _peer(x, y, c, r)))
        plan.append((gpf.at[me ^ r], r_pool.at[r - 1], _peer(x, y, c, r)))
    return plan


def _plan_in(refs):
    sums, landing = refs
    x, y, c, _ = _place()
    plan = []
    for rel, (dx, dy) in enumerate(((1, 0), (0, 1), (1, 1))):
        for p in range(3):
            plan.append((sums.at[rel, p], landing.at[rel, p], (x ^ dx, y ^ dy, c)))
    return plan


def _gather_small(small, *after):
    def body(sm, *refs):
        r_small, send_sems, recv_sems, loc_sem = refs[len(after):]
        x, y, c, me = _place()
        loc = pltpu.make_async_copy(sm, r_small.at[me], loc_sem)
        loc.start()

        def copy(r, src_idx):
            return pltpu.make_async_remote_copy(
                src_ref=sm, dst_ref=r_small.at[src_idx], send_sem=send_sems.at[r - 1], recv_sem=recv_sems.at[r - 1],
                device_id=_peer(x, y, c, r), device_id_type=MESH)

        sends = [copy(r, me) for r in range(1, NDEV)]
        for cp in sends:
            cp.start()
        for r in range(1, NDEV):
            copy(r, me ^ r).wait_recv()
        for cp in sends:
            cp.wait_send()
        loc.wait()

    return pl.pallas_call(
        body, name="gather_small",
        out_shape=jax.ShapeDtypeStruct((NDEV, 8, D), jnp.float32),
        in_specs=[ANY] * (1 + len(after)), out_specs=ANY,
        scratch_shapes=[pltpu.SemaphoreType.DMA((NDEV - 1,)), pltpu.SemaphoreType.DMA((NDEV - 1,)),
                        pltpu.SemaphoreType.DMA],
    )(small, *after)


def _seg_tiles(s):
    return (s + 2) % NSEG


_POOL_SPECS = [pl.BlockSpec((None, T, GROUP), lambda g, base=base: (base + g, 0, 0)) for base in (0, 4)]
_HEAD_SPECS = [pl.BlockSpec((None, T, HEAD), lambda h, base=base: (base + h // 2, 0, h % 2))
               for base in (8, 12, 16, 20)]


def _row_ids(shape):
    return lax.broadcasted_iota(jnp.int32, shape, 0)


BAND_ROWS = 128
HALO = 16


def _window_sum(a, gidx, lead):
    width = lax.shift_left(jnp.int32(2), gidx)
    shape = (BAND_ROWS, BAND_ROWS + HALO)
    t, j = lax.broadcasted_iota(jnp.int32, shape, 0), lax.broadcasted_iota(jnp.int32, shape, 1)
    first = t if lead else t + HALO - width + 1
    band = _bf(jnp.where(j >= first, jnp.where(j < first + width, 1.0, 0.0), 0.0))
    zeros = jnp.zeros((HALO, a.shape[1]), jnp.bfloat16)
    padded = [jnp.concatenate([p, zeros] if lead else [zeros, p], axis=0) for p in _split2(a)]
    out = []
    for r0 in range(0, T, BAND_ROWS):
        slab = jnp.concatenate([p[r0:r0 + BAND_ROWS + HALO] for p in padded], axis=1)
        r = _dot(band, slab, 1, 0)
        out.append(r[:, :a.shape[1]] + r[:, a.shape[1]:])
    return jnp.concatenate(out, axis=0)


def _window_mean(s, gidx):
    inv = jnp.where(gidx == 0, 0.5, jnp.where(gidx == 1, 0.25, jnp.where(gidx == 2, 0.125, 0.0625)))
    width = lax.shift_left(jnp.int32(2), gidx)
    head = s[:16] / jnp.minimum(_row_ids((16, s.shape[1])) + 1, width).astype(jnp.float32)
    return jnp.concatenate([head, s[16:] * inv], axis=0)


def _pool_fwd(proj, pool_w, pool_scale, token):
    def body(u_ref, pg_ref, w_ref, sc_ref, token_any, y_ref):
        del token_any
        gidx = pl.program_id(0)
        u, pg = u_ref[...], pg_ref[...]
        d = _window_mean(_window_sum(u, gidx, False), gidx) - u
        mixed = _dot(_bf(d), w_ref[...], 1, 0)
        y_ref[...] = _bf(mixed * sc_ref[...] * (pg * _sigmoid(pg)))

    return pl.pallas_call(
        body, name="pool_fwd", grid=(NGROUP,),
        in_specs=[*_POOL_SPECS,
                  pl.BlockSpec((None, GROUP, GROUP), lambda g: (g, 0, 0)),
                  pl.BlockSpec((1, GROUP), lambda g: (0, g)), ANY],
        out_specs=pl.BlockSpec((T, GROUP), lambda g: (0, g)),
        out_shape=pltpu.HBM((T, DMIX), jnp.bfloat16),
        compiler_params=_params(("parallel",)),
    )(proj, proj, pool_w, pool_scale, token)


def _tri(lower):
    r = lax.broadcasted_iota(jnp.int32, (CHUNK, CHUNK), 0)
    c = lax.broadcasted_iota(jnp.int32, (CHUNK, CHUNK), 1)
    return (r >= c) if lower else (r <= c)


def _sum_rows_matrix():
    shape = (CHUNK + 16, CHUNK)
    r, c = lax.broadcasted_iota(jnp.int32, shape, 0), lax.broadcasted_iota(jnp.int32, shape, 1)
    run = jnp.where(c <= r, 1.0, 0.0)
    half = jnp.where(c < CHUNK // 2, 1.0, 0.0)
    return _bf(jnp.where(r < CHUNK, run, jnp.where(r < CHUNK + 8, 1.0, half)))


def _rev_sum_matrix():
    shape = (CHUNK, 2 * CHUNK)
    r, c = lax.broadcasted_iota(jnp.int32, shape, 0), lax.broadcasted_iota(jnp.int32, shape, 1)
    return _bf(jnp.where(c < CHUNK, jnp.where(c >= r, 1.0, 0.0), jnp.where(c - CHUNK < r, 1.0, 0.0)))


def _split2(a):
    hi = _bf(a)
    return [hi, _bf(a - hi.astype(jnp.float32))]


def _exact_sums(mat, pieces):
    x = jnp.concatenate([s for p in pieces for s in _split2(p)], axis=1)
    r = _dot(mat, x, 1, 0)
    return [r[:, 2 * j * HEAD:(2 * j + 1) * HEAD] + r[:, (2 * j + 1) * HEAD:(2 * j + 2) * HEAD]
            for j in range(len(pieces))]


def _gates(qv, fl, lb):
    sq = _sigmoid(qv)
    sg = _sigmoid(fl)
    f = lb + (1.0 - lb) * sg
    return dict(sq=sq, qs=qv * sq, sg=sg, f=f, kk=1.0 - f, g=jnp.log2(f))


def _decays(sums):
    big_g = sums[:CHUNK]
    total = sums[CHUNK:CHUNK + 8]
    g_last = jnp.tile(total, (CHUNK // 8, 1))
    g_mid = jnp.tile(sums[CHUNK + 8:], (CHUNK // 8, 1))
    return dict(
        e_q=jnp.exp2(big_g),
        e_k=jnp.exp2(g_last - big_g),
        e_qm=jnp.exp2(jnp.minimum(big_g - g_mid, EXP_CAP)),
        e_km=jnp.exp2(jnp.minimum(g_mid - big_g, EXP_CAP)),
        total8=jnp.exp2(total))


def _group_rows(gi):
    return [pl.ds(pl.multiple_of((gi * NB + j) * CHUNK, CHUNK), CHUNK) for j in range(NB)]


def _lower_bound(lb_ref):
    return _sigmoid(lb_ref[0:1, :] - lb_ref[1:2, :])


def _hgrn_fwd(proj, lb_logits, rec_g, y_in):
    def body(q_ref, f_ref, i_ref, gate_ref, lb_ref, rg_ref, y_any, y_ref, o_ref, st_ref):
        del y_any
        lb = _lower_bound(lb_ref)
        causal = _tri(True)
        smat = _sum_rows_matrix()

        def group(gi, st):
            rows = _group_rows(gi)
            ts = [_gates(q_ref[r, :], f_ref[r, :], lb) for r in rows]
            ds = [_decays(s) for s in _exact_sums(smat, [t["g"] for t in ts])]
            vs = [_bf(i_ref[r, :]) for r in rows]
            q_m = [_bf(t["qs"] * d["e_qm"]) for t, d in zip(ts, ds)]
            k_m = [_bf(t["kk"] * d["e_km"]) for t, d in zip(ts, ds)]
            q_e = [_bf(t["qs"] * d["e_q"]) for t, d in zip(ts, ds)]
            k_e = [_bf(t["kk"] * d["e_k"]) for t, d in zip(ts, ds)]
            a = [_bf(jnp.where(causal, _dot(q_m[j], k_m[j], 1, 1), 0.0)) for j in range(NB)]
            intra = [_dot(a[j], vs[j], 1, 0) for j in range(NB)]
            upd = [_dot(vs[j], k_e[j], 0, 0) for j in range(NB)]
            for j in range(NB):
                st_ref[gi * NB + j] = st
                o_ref[rows[j], :] = intra[j] + _dot(q_e[j], _bf(st), 1, 1)
                st = st * jnp.tile(ds[j]["total8"], (HEAD // 8, 1)) + upd[j]
            return st

        lax.fori_loop(0, NGRP, group, jnp.zeros((HEAD, HEAD), jnp.float32))
        o = o_ref[...]
        rn = o * lax.rsqrt(jnp.mean(o * o, axis=-1, keepdims=True) + EPS)
        gate = gate_ref[...]
        y_ref[...] = _bf(rn * rg_ref[...] * (gate * _sigmoid(gate)))

    return pl.pallas_call(
        body, name="hgrn_fwd", grid=(NHEAD,),
        in_specs=[*_HEAD_SPECS,
                  pl.BlockSpec((2, HEAD), lambda h: (0, h)),
                  pl.BlockSpec((1, HEAD), lambda h: (0, h)),
                  pl.BlockSpec(memory_space=pl.ANY)],
        out_specs=(pl.BlockSpec((T, HEAD), lambda h: (0, NHEAD + h)),
                   pl.BlockSpec((T, HEAD), lambda h: (0, h)),
                   pl.BlockSpec((None, NCHUNK, HEAD, HEAD), lambda h: (h, 0, 0, 0))),
        out_shape=(pltpu.HBM((T, DMIX), jnp.bfloat16), pltpu.HBM((T, D), jnp.float32),
                   pltpu.HBM((NHEAD, NCHUNK, HEAD, HEAD), jnp.float32)),
        input_output_aliases={6: 0},
        compiler_params=_params(("parallel",)),
    )(proj, proj, proj, proj, lb_logits, rec_g, y_in)


def _out_proj_loss(x, y, w_out, target, gf):
    rows = 512
    parts = [slice(k * rows // 2, (k + 1) * rows // 2) for k in range(2)]

    def body(x_ref, y_ref, w_ref, t_ref, g_ref, dz_ref, dzb_ref, sq_ref, dg_ref):
        zs = [x_ref[p, :] + _dot(y_ref[p, :], w_ref[...], 1, 0) for p in parts]
        sq = dg = 0.0
        for p, z in zip(parts, zs):
            r = lax.rsqrt(jnp.mean(z * z, axis=-1, keepdims=True) + EPS)
            zhat = z * r
            err = zhat * g_ref[...] - t_ref[p, :]
            dy = err * (1.0 / D)
            gdy = dy * g_ref[...]
            dz = r * (gdy - zhat * jnp.mean(zhat * gdy, axis=-1, keepdims=True))
            dz_ref[p, :] = dz
            dzb_ref[p, :] = _bf(dz)
            sq = sq + jnp.sum(err * err, axis=0, keepdims=True)
            dg = dg + jnp.sum(zhat * dy, axis=0, keepdims=True)

        @pl.when(pl.program_id(0) == 0)
        def _():
            sq_ref[...] = sq
            dg_ref[...] = dg

        @pl.when(pl.program_id(0) != 0)
        def _():
            sq_ref[...] += sq
            dg_ref[...] += dg

    tile = pl.BlockSpec((rows, D), lambda i: (i, 0))
    vec = pl.BlockSpec((1, D), lambda i: (0, 0))
    return pl.pallas_call(
        body, name="out_proj_loss", grid=(T // rows,),
        in_specs=[tile, pl.BlockSpec((rows, DMIX), lambda i: (i, 0)), pl.BlockSpec((DMIX, D), lambda i: (0, 0)),
                  tile, vec],
        out_specs=(tile, tile, vec, vec),
        out_shape=(pltpu.HBM((T, D), jnp.float32), pltpu.HBM((T, D), jnp.bfloat16),
                   jax.ShapeDtypeStruct((1, D), jnp.float32), jax.ShapeDtypeStruct((1, D), jnp.float32)),
        compiler_params=_params(("arbitrary",)),
    )(x, y, w_out, target, gf)


def _out_proj_bwd(dzb, w_out, y):
    tn = 512

    def body(dz_ref, w_ref, y_ref, dy_ref, gw_ref, gwb_ref):
        dz = dz_ref[...]
        dy_ref[...] = _dot(dz, w_ref[...], 1, 1)
        gw = _dot(y_ref[...], dz, 0, 0)
        gw_ref[...] = gw
        gwb_ref[...] = _bf(gw)

    return pl.pallas_call(
        body, name="out_proj_bwd", grid=(DMIX // tn,),
        in_specs=[pl.BlockSpec((T, D), lambda n: (0, 0)), pl.BlockSpec((tn, D), lambda n: (n, 0)),
                  pl.BlockSpec((T, tn), lambda n: (0, n))],
        out_specs=(pl.BlockSpec((T, tn), lambda n: (0, n)), pl.BlockSpec((tn, D), lambda n: (n, 0)),
                   pl.BlockSpec((tn, D), lambda n: (n, 0))),
        out_shape=(pltpu.HBM((T, DMIX), jnp.float32), pltpu.HBM((DMIX, D), jnp.float32),
                   pltpu.HBM((DMIX, D), jnp.bfloat16)),
        compiler_params=_params(("parallel",)),
    )(dzb, w_out, y)


def _hgrn_bwd(proj, lb_logits, rec_g, o, states, dymix, dproj_in, token):
    def body(q_ref, f_ref, i_ref, gate_ref, lb_ref, rg_ref, o_ref, st_ref, dy_ref, dp_any, token_any,
             dp_ref, drg_ref, dlb_ref, do_ref):
        del dp_any, token_any
        lb = _lower_bound(lb_ref)
        causal = _tri(True)
        smat, rmat = _sum_rows_matrix(), _rev_sum_matrix()

        o = o_ref[...]
        rs = lax.rsqrt(jnp.mean(o * o, axis=-1, keepdims=True) + EPS)
        rn = o * rs
        gate = gate_ref[...]
        sgate = _sigmoid(gate)
        dyv = dy_ref[...]
        d_r = dyv * (gate * sgate)
        dp_ref[3] = _bf(dyv * (rn * rg_ref[...]) * (sgate * (1.0 + gate * (1.0 - sgate))))
        drg_ref[...] = jnp.sum(d_r * rn, axis=0, keepdims=True)
        drn = d_r * rg_ref[...]
        do_ref[...] = rs * (drn - rn * jnp.mean(rn * drn, axis=-1, keepdims=True))

        def group(i, carry):
            dst, dlb = carry
            gi = NGRP - 1 - i
            rows = _group_rows(gi)
            span = range(NB)
            qvs = [q_ref[r, :] for r in rows]
            ts = [_gates(qv, f_ref[r, :], lb) for qv, r in zip(qvs, rows)]
            ds = [_decays(s) for s in _exact_sums(smat, [t["g"] for t in ts])]
            vs = [_bf(i_ref[r, :]) for r in rows]
            dos = [_bf(do_ref[r, :]) for r in rows]
            sts = [st_ref[gi * NB + j] for j in span]
            qe_f = [t["qs"] * d["e_q"] for t, d in zip(ts, ds)]
            ke_f = [t["kk"] * d["e_k"] for t, d in zip(ts, ds)]
            q_e, k_e = [_bf(a) for a in qe_f], [_bf(a) for a in ke_f]
            q_m = [_bf(t["qs"] * d["e_qm"]) for t, d in zip(ts, ds)]
            k_m = [_bf(t["kk"] * d["e_km"]) for t, d in zip(ts, ds)]
            a = [_bf(jnp.where(causal, _dot(q_m[j], k_m[j], 1, 1), 0.0)) for j in span]
            da = [_bf(jnp.where(causal, _dot(dos[j], vs[j], 1, 1), 0.0)) for j in span]
            dqm = [_dot(da[j], k_m[j], 1, 0) for j in span]
            dkm = [_dot(da[j], q_m[j], 0, 0) for j in span]
            dv_in = [_dot(a[j], dos[j], 0, 0) for j in span]
            dqe = [_dot(dos[j], _bf(sts[j]), 1, 0) for j in span]
            grow = [_dot(dos[j], q_e[j], 0, 0) for j in span]
            dke, carried = [None] * NB, [None] * NB
            for j in reversed(span):
                dst_b = _bf(dst)
                dke[j] = _dot(vs[j], dst_b, 1, 0)
                dp_ref[2, rows[j], :] = _bf(dv_in[j] + _dot(k_e[j], dst_b, 1, 1))
                carried[j] = ds[j]["total8"] * jnp.sum(dst * sts[j], axis=0, keepdims=True)
                dst = dst * jnp.tile(ds[j]["total8"], (HEAD // 8, 1)) + grow[j]
            kdk = [ke_f[j] * dke[j] for j in span]
            pos = [(q_m[j].astype(jnp.float32) * dqm[j] - k_m[j].astype(jnp.float32) * dkm[j]) + qe_f[j] * dqe[j]
                   for j in span]
            dgs = _exact_sums(rmat, [jnp.concatenate([pos[j], kdk[j]], axis=0) for j in span])
            for j in span:
                t, d = ts[j], ds[j]
                dg = dgs[j] + jnp.tile(carried[j], (CHUNK // 8, 1))
                dqs = dqm[j] * d["e_qm"] + dqe[j] * d["e_q"]
                dkk = dkm[j] * d["e_km"] + dke[j] * d["e_k"]
                df = dg / t["f"] - dkk
                dp_ref[1, rows[j], :] = _bf(df * (1.0 - lb) * (t["sg"] * (1.0 - t["sg"])))
                dp_ref[0, rows[j], :] = _bf(dqs * (t["sq"] * (1.0 + qvs[j] * (1.0 - t["sq"]))))
                dlb = dlb + df * (1.0 - t["sg"])
            return dst, dlb

        _, dlb = lax.fori_loop(0, NGRP, group, (jnp.zeros((HEAD, HEAD), jnp.float32),
                                                jnp.zeros((CHUNK, HEAD), jnp.float32)))
        dlb_ref[...] = jnp.sum(dlb, axis=0, keepdims=True)

    vec = pl.BlockSpec((1, HEAD), lambda h: (0, h))
    return pl.pallas_call(
        body, name="hgrn_bwd", grid=(NHEAD,),
        in_specs=[*_HEAD_SPECS,
                  pl.BlockSpec((2, HEAD), lambda h: (0, h)), vec,
                  pl.BlockSpec((T, HEAD), lambda h: (0, h)),
                  pl.BlockSpec((None, NCHUNK, HEAD, HEAD), lambda h: (h, 0, 0, 0)),
                  pl.BlockSpec((T, HEAD), lambda h: (0, NHEAD + h)), ANY, ANY],
        out_specs=(pl.BlockSpec((4, T, HEAD), lambda h: (0, 0, h)), vec, vec),
        out_shape=(pltpu.HBM((NSEG, T, D), jnp.bfloat16),
                   jax.ShapeDtypeStruct((1, D), jnp.float32), jax.ShapeDtypeStruct((1, D), jnp.float32)),
        scratch_shapes=[pltpu.VMEM((T, HEAD), jnp.float32)],
        input_output_aliases={9: 0},
        compiler_params=_params(("parallel",)),
    )(proj, proj, proj, proj, lb_logits, rec_g, o, states, dymix, dproj_in, token)


def _pool_bwd(proj, pool_w, pool_scale, dymix):
    def body(u_ref, pg_ref, w_ref, sc_ref, dy_ref, dp_ref, gw_ref, gs_ref):
        gidx = pl.program_id(0)
        u, pg = u_ref[...], pg_ref[...]
        d = _bf(_window_mean(_window_sum(u, gidx, False), gidx) - u)
        mixed = _dot(d, w_ref[...], 1, 0)
        spg = _sigmoid(pg)
        dyv = dy_ref[...]
        d_p = dyv * (pg * spg)
        dp_ref[1] = _bf(dyv * (mixed * sc_ref[...]) * (spg * (1.0 + pg * (1.0 - spg))))
        gs_ref[...] = jnp.sum(d_p * mixed, axis=0, keepdims=True)
        dmixed = _bf(d_p * sc_ref[...])
        gw_ref[...] = _dot(d, dmixed, 0, 0)
        dd = _dot(dmixed, w_ref[...], 1, 1)
        dp_ref[0] = _bf(_window_sum(_window_mean(dd, gidx), gidx, True) - dd)

    return pl.pallas_call(
        body, name="pool_bwd", grid=(NGROUP,),
        in_specs=[*_POOL_SPECS,
                  pl.BlockSpec((None, GROUP, GROUP), lambda g: (g, 0, 0)),
                  pl.BlockSpec((1, GROUP), lambda g: (0, g)),
                  pl.BlockSpec((T, GROUP), lambda g: (0, g))],
        out_specs=(pl.BlockSpec((2, T, GROUP), lambda g: (2, 0, g)),
                   pl.BlockSpec((None, GROUP, GROUP), lambda g: (g, 0, 0)),
                   pl.BlockSpec((1, GROUP), lambda g: (0, g))),
        out_shape=(pltpu.HBM((NSEG, T, D), jnp.bfloat16),
                   jax.ShapeDtypeStruct((NGROUP, GROUP, GROUP), jnp.float32),
                   jax.ShapeDtypeStruct((1, D), jnp.float32)),
        compiler_params=_params(("parallel",)),
    )(proj, proj, pool_w, pool_scale, dymix)


HALF = NTILE // 2
AWAY = HALF - 3


def _dproj_tile(chip, side, p):
    j = 6 * chip + 3 * side + p
    return ((j // 4 + 4) % NSEG, 0, j % 4)


def _sibling_copy(sib_out, sib_in, send_sems, recv_sems, slot):
    x, y, c, _ = _place()
    return pltpu.make_async_remote_copy(
        src_ref=sib_out.at[slot], dst_ref=sib_in.at[slot], send_sem=send_sems.at[slot],
        recv_sem=recv_sems.at[slot], device_id=(x, y, 1 - c), device_id_type=MESH)


def _proj_bwd_w_far(place, ht, dproj):
    def body(place_ref, h_ref, dp_ref, sib_out, sib_in, send_sems, recv_sems, stage, loc_sems):
        del place_ref
        i = pl.program_id(0)

        def to_hbm(k):
            return pltpu.make_async_copy(stage.at[k], sib_out.at[k], loc_sems.at[k])

        def send(k):
            to_hbm(k).wait()
            _sibling_copy(sib_out, sib_in, send_sems, recv_sems, k).start()

        stage[i] = _bf(_dot(h_ref[...], dp_ref[...], 1, 0))

        @pl.when(i > 0)
        def _():
            send(i - 1)

        to_hbm(i).start()

        @pl.when(i == HALF - 1)
        def _():
            send(i)

    buf = pltpu.HBM((HALF, D, TILE), jnp.bfloat16)
    sems = pltpu.SemaphoreType.DMA((HALF,))
    return pl.pallas_call(
        body, name="proj_bwd_w_far",
        grid_spec=pltpu.PrefetchScalarGridSpec(
            num_scalar_prefetch=1, grid=(HALF,),
            in_specs=[pl.BlockSpec((D, T), lambda i, pr: (0, 0)),
                      pl.BlockSpec((None, T, TILE), lambda i, pr: _dproj_tile(i // 3, 1 - pr[2], i % 3))],
            out_specs=(HBM, HBM, SEM, SEM),
            scratch_shapes=[pltpu.VMEM((HALF, D, TILE), jnp.bfloat16), pltpu.SemaphoreType.DMA((HALF,))]),
        out_shape=(buf, buf, sems, sems),
        compiler_params=pltpu.CompilerParams(dimension_semantics=("arbitrary",), vmem_limit_bytes=48 * MIB,
                                             has_side_effects=EFFECT),
    )(place, ht, dproj)


def _proj_bwd_w_near(place, ht, dproj, sib_out, sib_in, sib_send, sib_recv):
    def owner_chip(k, pr):
        return (pr[1] + 1 + k // 3) % 4

    def body(place_ref, h_ref, dp_ref, sib_out, sib_in, sib_send, sib_recv, sums, own_ref, landing, out_send,
             out_recv, recvbuf, outbuf, in_sems, loc_sems):
        i = pl.program_id(0)
        px, py, c, _ = _place()

        def slot_of(k):
            return 3 * owner_chip(k, place_ref) + k % 3

        def load(k):
            return pltpu.make_async_copy(sib_in.at[slot_of(k)], recvbuf.at[k % 2], in_sems.at[k % 2])

        def fetch(k):
            _sibling_copy(sib_out, sib_in, sib_send, sib_recv, slot_of(k)).wait_recv()
            load(k).start()

        def route(k):
            chip = owner_chip(k, place_ref)
            cx, cy = chip // 2, chip % 2
            return cx, cy, (cx ^ px) + 2 * (cy ^ py) - 1, k % 3

        def to_hbm(k):
            _, _, rel, p = route(k)
            return pltpu.make_async_copy(outbuf.at[k], sums.at[rel, p], loc_sems.at[k])

        def to_owner(k):
            cx, cy, rel, p = route(k)
            return pltpu.make_async_remote_copy(
                src_ref=sums.at[rel, p], dst_ref=landing.at[rel, p], send_sem=out_send.at[3 * rel + p],
                recv_sem=out_recv.at[3 * rel + p], device_id=(cx, cy, c), device_id_type=MESH)

        @pl.when(i == 0)
        def _():
            fetch(i)

        @pl.when(i < HALF - 1)
        def _():
            fetch(i + 1)

        gw = _dot(h_ref[...], dp_ref[...], 1, 0)

        @pl.when(jnp.logical_and(i > 0, i <= AWAY))
        def _():
            to_hbm(i - 1).wait()
            to_owner(i - 1).start()

        load(i).wait()
        total = gw + recvbuf[i % 2].astype(jnp.float32)
        own_ref[...] = total

        @pl.when(i < AWAY)
        def _():
            outbuf[i] = _bf(total)
            to_hbm(i).start()

        @pl.when(i == HALF - 1)
        def _():
            for slot in range(HALF):
                _sibling_copy(sib_out, sib_in, sib_send, sib_recv, slot).wait_send()

    travelling = pltpu.HBM((3, 3, D, TILE), jnp.bfloat16)
    sems = pltpu.SemaphoreType.DMA((AWAY,))
    return pl.pallas_call(
        body, name="proj_bwd_w_near",
        grid_spec=pltpu.PrefetchScalarGridSpec(
            num_scalar_prefetch=1, grid=(HALF,),
            in_specs=[pl.BlockSpec((D, T), lambda i, pr: (0, 0)),
                      pl.BlockSpec((None, T, TILE), lambda i, pr: _dproj_tile(owner_chip(i, pr), pr[2], i % 3)),
                      HBM, HBM, SEM, SEM],
            out_specs=(HBM, pl.BlockSpec((None, D, TILE), lambda i, pr: (jnp.where(i < AWAY, 0, i % 3), 0, 0)),
                       HBM, SEM, SEM),
            scratch_shapes=[pltpu.VMEM((2, D, TILE), jnp.bfloat16), pltpu.VMEM((AWAY, D, TILE), jnp.bfloat16),
                            pltpu.SemaphoreType.DMA((2,)), pltpu.SemaphoreType.DMA((AWAY,))]),
        out_shape=(travelling, pltpu.HBM((3, D, TILE), jnp.float32), travelling, sems, sems),
        compiler_params=pltpu.CompilerParams(dimension_semantics=("arbitrary",), vmem_limit_bytes=48 * MIB,
                                             has_side_effects=EFFECT),
    )(place, ht, dproj, sib_out, sib_in, sib_send, sib_recv)


def _proj_bwd_x(dproj, w_t, x, g1, dz, token):
    tm = 512
    pairs = NSEG // 2

    def body(dp_ref, w_ref, x_ref, g_ref, dz_ref, token_any, dx_ref, dg_ref, wcat, acc):
        del token_any
        s, m = pl.program_id(0), pl.program_id(1)

        @pl.when(m == 0)
        def _():
            for i in range(8):
                wcat[:, i * TILE:(i + 1) * TILE] = w_ref[i]

        mine = pl.ds(pl.multiple_of(m * tm, tm), tm)

        @pl.when(s == 0)
        def _():
            acc[mine, :] = jnp.zeros((tm, D), jnp.float32)

        acc[mine, :] += _dot(jnp.concatenate([dp_ref[0], dp_ref[1]], axis=1), wcat[...], 1, 1)

        @pl.when(s == pairs - 1)
        def _():
            xv = x_ref[...]
            rs = lax.rsqrt(jnp.mean(xv * xv, axis=-1, keepdims=True) + EPS)
            xhat = xv * rs
            dhv = acc[mine, :]
            gdh = dhv * g_ref[...]
            dx_ref[...] = dz_ref[...] + rs * (gdh - xhat * jnp.mean(xhat * gdh, axis=-1, keepdims=True))
            dg = jnp.sum(xhat * dhv, axis=0, keepdims=True)

            @pl.when(m == 0)
            def _():
                dg_ref[...] = dg

            @pl.when(m != 0)
            def _():
                dg_ref[...] += dg

    rows = pl.BlockSpec((tm, D), lambda s, m: (jnp.where(s == pairs - 1, m, 0), 0))
    vec = pl.BlockSpec((1, D), lambda s, m: (0, 0))
    return pl.pallas_call(
        body, name="proj_bwd_x", grid=(pairs, T // tm),
        in_specs=[pl.BlockSpec((2, tm, D), lambda s, m: (s, m, 0)),
                  pl.BlockSpec((8, D, TILE), lambda s, m: ((s + 1) % pairs, 0, 0)), rows, vec, rows, ANY],
        out_specs=(rows, vec),
        out_shape=(jax.ShapeDtypeStruct((T, D), jnp.float32), jax.ShapeDtypeStruct((1, D), jnp.float32)),
        scratch_shapes=[pltpu.VMEM((D, 2 * D), jnp.bfloat16), pltpu.VMEM((T, D), jnp.float32)],
        compiler_params=_params(("arbitrary", "arbitrary"), vmem_mib=56),
    )(dproj, w_t, x, g1, dz, token)


def _adamw(w, g, m, v):
    m_new = ADAM_B1 * m + (1.0 - ADAM_B1) * g
    v_new = ADAM_B2 * v + (1.0 - ADAM_B2) * (g * g)
    delta = -ADAM_LR * ((m_new / BC1) / (jnp.sqrt(v_new / BC2) + ADAM_EPS) + ADAM_WD * w)
    return delta, m_new, v_new


def _reduce_adam(name, place, parts, w, m, v, grid, w_spec):
    n = len(parts)

    def body(place_ref, *refs):
        del place_ref
        w_ref, m_ref, v_ref, g_ref, d_ref, mo_ref, vo_ref = refs[n:]
        g = None
        for ref, (_, _, stacked) in zip(refs[:n], parts):
            terms = [ref[r] for r in range(ref.shape[0])] if stacked else [ref[...]]
            for t in terms:
                if t.shape[-1] != w_ref.shape[-1]:
                    t = jnp.concatenate([t[p] for p in range(t.shape[0])], axis=1)
                g = t.astype(jnp.float32) if g is None else g + t.astype(jnp.float32)
        delta, m_new, v_new = _adamw(w_ref[...], g, m_ref[...], v_ref[...])
        g_ref[...] = g
        d_ref[...] = delta
        mo_ref[...] = m_new
        vo_ref[...] = v_new

    shape = jax.ShapeDtypeStruct(w.shape, jnp.float32)
    return pl.pallas_call(
        body, name=name,
        grid_spec=pltpu.PrefetchScalarGridSpec(
            num_scalar_prefetch=1, grid=grid,
            in_specs=[spec for _, spec, _ in parts] + [w_spec] * 3, out_specs=(w_spec,) * 4),
        out_shape=(shape,) * 4,
        compiler_params=_params(("parallel",)),
    )(place, *[_in_hbm(a) for a in [a for a, _, _ in parts] + [w, m, v]])


def _small_adam(parts, w, m, v):
    def body(p_ref, w_ref, m_ref, v_ref, g_ref, d_ref, mo_ref, vo_ref):
        g = p_ref[0]
        for s in range(1, NDEV):
            g = g + p_ref[s]
        wv = w_ref[...]
        rows = _row_ids(wv.shape)
        other = jnp.where(rows == 2, pltpu.roll(wv, 7, 0), jnp.where(rows == 3, pltpu.roll(wv, 1, 0), 0.0))
        lbv = _sigmoid(wv - other)
        sign = jnp.where(rows == 2, 1.0, -1.0)
        g = jnp.where((rows == 2) | (rows == 3), sign * g * lbv * (1.0 - lbv), g)
        delta, m_new, v_new = _adamw(wv, g, m_ref[...], v_ref[...])
        g_ref[...] = g
        d_ref[...] = delta
        mo_ref[...] = m_new
        vo_ref[...] = v_new

    shape = jax.ShapeDtypeStruct((8, D), jnp.float32)
    return pl.pallas_call(body, name="small_adam", out_shape=(shape,) * 4)(parts, w, m, v)


def _rows8(*vecs):
    rows = [a.reshape(-1, D) for a in vecs]
    n = sum(r.shape[0] for r in rows)
    return jnp.concatenate(rows + [jnp.zeros((8 - n, D), jnp.float32)], axis=0)


def kernel(x, norm1_g, w_in, pool_w, pool_scale, lb_logits, rec_norm_g, w_out, final_norm_g, loss_target, m_norm1_g, m_w_in, m_pool_w, m_pool_scale, m_lb_logits, m_rec_norm_g, m_w_out, m_final_norm_g, v_norm1_g, v_w_in, v_pool_w, v_pool_scale, v_lb_logits, v_rec_norm_g, v_w_out, v_final_norm_g):
    xs = x[0]
    target = loss_target[0]
    ix, iy, ic = lax.axis_index("x"), lax.axis_index("y"), lax.axis_index("c")
    place = jnp.stack([4 * ix + 2 * iy + ic, 2 * ix + iy, ic]).astype(jnp.int32)
    gf = final_norm_g.reshape(1, D)

    ht, w_t, w_out_b, w_out_g, pool_g, proj = _gather_proj(xs, norm1_g, w_in, w_out, pool_w)
    pool_full = pool_g.transpose(1, 0, 2, 3).reshape(NGROUP, GROUP, GROUP)
    wout = [w_out_b, w_out_g]
    wout_send, wout_recv, wout, wout_token = _split_start("gather_wout_start", wout, NDEV - 1, _plan_wout)

    y = _pool_fwd(proj, pool_full, pool_scale, wout_token)
    y, o, states = _hgrn_fwd(proj, lb_logits, rec_norm_g, y)
    _, w_out_g = _split_wait("gather_wout_wait", wout, wout_send, wout_recv, _plan_wout, o)
    w_out_full = _in_hbm(w_out_g.reshape(DMIX, D))
    dz, dzb, sq, dgf = _out_proj_loss(xs, y, w_out_full, target, gf)

    dymix, gwout_f, gwout_b = _out_proj_bwd(dzb, w_out_full, y)
    dproj, gpool, dscale = _pool_bwd(proj, pool_full, pool_scale, dymix)

    blk_out = (NDEV, DMIX // NDEV, D)
    blk_pool = (NDEV, NGROUP, GROUP // NDEV, GROUP)
    gpool_s = gpool.reshape(NGROUP, NDEV, GROUP // NDEV, GROUP).transpose(1, 0, 2, 3)
    rest = [gwout_b.reshape(blk_out), gpool_s,
            lax.empty((NDEV - 1,) + blk_out[1:], jnp.bfloat16), lax.empty((NDEV - 1,) + blk_pool[1:], jnp.float32)]
    rest_send, rest_recv, rest, rest_token = _split_start("scatter_rest_start", rest, 2 * (NDEV - 1), _plan_rest)

    dproj, drecg, dlb = _hgrn_bwd(proj, lb_logits, rec_norm_g, o, states, dymix, dproj, rest_token)
    chip_sums, own_sum, landing, win_send, win_recv = _proj_bwd_w_near(
        place, ht, dproj, *_proj_bwd_w_far(place, ht, dproj))
    win = [chip_sums, landing]

    grad_x, dg1 = _proj_bwd_x(dproj, w_t, xs, norm1_g, dz, chip_sums)

    _, gpool_own, r_out, r_pool = _split_wait("scatter_rest_wait", rest, rest_send, rest_recv, _plan_rest, grad_x)
    g_wout, d_wout, m_wout, v_wout = _reduce_adam(
        "adam_w_out", place,
        [(gwout_f.reshape(blk_out), pl.BlockSpec((None,) + blk_out[1:], lambda i, pr: (pr[0], 0, 0)), False),
         (r_out, pl.BlockSpec((NDEV - 1,) + blk_out[1:], lambda i, pr: (0, 0, 0)), True)],
        w_out, m_w_out, v_w_out, (1,), pl.BlockSpec((None,) + blk_out[1:], lambda i, pr: (0, 0, 0)))
    g_pool, d_pool, m_pool, v_pool = _reduce_adam(
        "adam_pool_w", place,
        [(gpool_own, pl.BlockSpec((None,) + blk_pool[1:], lambda i, pr: (pr[0], 0, 0, 0)), False),
         (r_pool, pl.BlockSpec((NDEV - 1,) + blk_pool[1:], lambda i, pr: (0, 0, 0, 0)), True)],
        pool_w, m_pool_w, v_pool_w, (1,), pl.BlockSpec((None,) + blk_pool[1:], lambda i, pr: (0, 0, 0, 0)))

    r_small = _gather_small(_rows8(dg1, dscale, dlb, dlb, drecg, dgf, sq), d_wout, d_pool)
    g_s, d_s, m_s, v_s = _small_adam(
        r_small,
        _rows8(norm1_g, pool_scale, lb_logits, rec_norm_g, final_norm_g),
        _rows8(m_norm1_g, m_pool_scale, m_lb_logits, m_rec_norm_g, m_final_norm_g),
        _rows8(v_norm1_g, v_pool_scale, v_lb_logits, v_rec_norm_g, v_final_norm_g))
    loss = jnp.sum(g_s[6]) * (0.5 / D)

    _, r_in = _split_wait("scatter_win_wait", win, win_send, win_recv, _plan_in, v_s)
    g_win, d_win, m_win, v_win = _reduce_adam(
        "adam_w_in", place,
        [(own_sum, pl.BlockSpec((3, D // 8, TILE), lambda i, pr: (0, i, 0)), False),
         (r_in, pl.BlockSpec((3, 3, D // 8, TILE), lambda i, pr: (0, 0, i, 0)), True)],
        w_in, m_w_in, v_w_in, (8,), pl.BlockSpec((None, D // 8, 3 * TILE), lambda i, pr: (0, i, 0)))

    def small_outs(a):
        return a[0:1], a[1:2], a[2:4], a[4:5], a[5]

    def outs(small_a, win, pool, wout):
        n1, ps, lbl, rg, fg = small_outs(small_a)
        return n1, win, pool, ps, lbl, rg, wout, fg

    return (loss, grad_x[None],
            *outs(g_s, g_win, g_pool, g_wout), *outs(d_s, d_win, d_pool, d_wout),
            *outs(m_s, m_win, m_pool, m_wout), *outs(v_s, v_win, v_pool, v_wout))
```

```python
import functools

import jax
import jax.numpy as jnp
from jax import lax
from jax.experimental import pallas as pl
from jax.experimental.pallas import tpu as pltpu

T = 2048
D = 1024
NSEG = 6
NTILE = 24
TILE = 256
DMIX = 2048
NDEV = 8
HEAD = 128
NHEAD = 8
CHUNK = 64
NCHUNK = T // CHUNK
NB = 32
NGRP = NCHUNK // NB
NGROUP = 4
GROUP = 256
EPS = 1e-6
EXP_CAP = 115.0
MESH = pl.DeviceIdType.MESH
AXES = ("x", "y", "c")
ANY = pl.BlockSpec(memory_space=pl.ANY)
HBM = pl.BlockSpec(memory_space=pltpu.HBM)
SEM = pl.BlockSpec(memory_space=pltpu.SEMAPHORE)
EFFECT = pltpu.SideEffectType.DATAFLOW_SIDE_EFFECTING

ADAM_LR = 0.001
ADAM_B1 = 0.9
ADAM_B2 = 0.999
ADAM_EPS = 1e-08
ADAM_WD = 0.01
ADAM_STEP = 10
BC1 = 1.0 - ADAM_B1 ** ADAM_STEP
BC2 = 1.0 - ADAM_B2 ** ADAM_STEP

MIB = 1 << 20


def _params(sem=None, vmem_mib=48):
    return pltpu.CompilerParams(dimension_semantics=sem, vmem_limit_bytes=vmem_mib * MIB)


def _sigmoid(v):
    return 1.0 / (1.0 + jnp.exp(-v))


def _dot(a, b, ca, cb, precision=None):
    return lax.dot_general(a, b, (((ca,), (cb,)), ((), ())), precision=precision,
                           preferred_element_type=jnp.float32)


def _bf(v):
    return v.astype(jnp.bfloat16)


def _in_hbm(a):
    return pltpu.with_memory_space_constraint(a, pltpu.HBM)


def _place():
    x, y, c = lax.axis_index("x"), lax.axis_index("y"), lax.axis_index("c")
    return x, y, c, 4 * x + 2 * y + c


def _peer(x, y, c, r):
    return (x ^ ((r >> 2) & 1), y ^ ((r >> 1) & 1), c ^ (r & 1))


def _gather_proj(x, g1, w_in, w_out, pool_w):
    def body(x_ref, g_ref, win_ref, wout_ref, pool_ref, ht_o, wt_o, woutb_o, wout_o, pool_o, proj_o,
             xbuf, hv, htv, wv, wob, pb, stage, send_sems, recv_sems, loc_sems, out_sems):
        px, py, c, my_idx = _place()
        fetch_x = pltpu.make_async_copy(x_ref, xbuf, loc_sems.at[5])
        fetch_x.start()
        me, sibling = (px, py, c), (px, py, 1 - c)
        chips = [(1 - px, py), (px, 1 - py), (1 - px, 1 - py)]
        for p in range(3):
            wv[3 * my_idx + p] = _bf(win_ref[0, :, p * TILE:(p + 1) * TILE])

        def index(bx, by, bc):
            return 4 * bx + 2 * by + bc

        def slot(w, block):
            return wv.at[pl.ds(3 * index(*block), 3)] if w == 0 else pool_o.at[index(*block)]

        def copy(k, w, block, to, src=None):
            return pltpu.make_async_remote_copy(
                src_ref=slot(w, block) if src is None else src, dst_ref=slot(w, block),
                send_sem=send_sems.at[2 * k + w], recv_sem=recv_sems.at[2 * k + w],
                device_id=to, device_id_type=MESH)

        def save(block):
            at = pl.ds(3 * index(*block), 3)
            pltpu.make_async_copy(wv.at[at], wt_o.at[at], loc_sems.at[4]).start()

        srcs = (slot(0, me), pb)
        first = []
        for w in (0, 1):
            if w == 1:
                pb[...] = _bf(pool_ref[0])
                wob[...] = _bf(wout_ref[0])
            group = [copy(1 + j, w, me, (*chip, c), src=srcs[w]) for j, chip in enumerate(chips[:2])]
            group.append(copy(0, w, me, sibling, src=srcs[w]))
            for cp in group:
                cp.start()
            first += group
        save(me)
        locs = [pltpu.make_async_copy(pb, slot(1, me), loc_sems.at[0]),
                pltpu.make_async_copy(wob, wout_o.at[my_idx], loc_sems.at[1]),
                pltpu.make_async_copy(wob, woutb_o, loc_sems.at[2])]
        for cp in locs:
            cp.start()

        fetch_x.wait()
        xv = xbuf[...]
        hv[...] = _bf(xv * lax.rsqrt(jnp.mean(xv * xv, axis=-1, keepdims=True) + EPS) * g_ref[...])
        rows = 256
        for r0 in range(0, T, rows):
            htv[:, r0:r0 + rows] = hv[r0:r0 + rows, :].T
        locs.append(pltpu.make_async_copy(htv, ht_o, loc_sems.at[3]))
        locs[-1].start()

        def out_copy(p, j):
            return pltpu.make_async_copy(stage.at[p], proj_o.at[j], out_sems.at[p])

        def project(nth, block):
            base = 3 * index(*block)

            def tile(p, carry):
                if nth > 0:
                    out_copy(p, base + p).wait()
                stage[p] = _dot(hv[...], wv[base + p], 1, 0)
                out_copy(p, base + p).start()
                return carry

            lax.fori_loop(0, 3, tile, 0)

        project(0, me)
        copy(0, 0, sibling, me).wait_recv()
        save(sibling)
        project(1, sibling)
        passed = []
        relay_from = (px ^ (1 - c), py ^ c, c)
        relay_to = (px ^ c, py ^ (1 - c), c)

        def arrived(w, j):
            copy(1 + j, w, (*chips[j], c), me).wait_recv()
            passed.append(copy(4 + j, w, (*chips[j], c), sibling))
            passed[-1].start()

        def relay(w):
            passed.append(copy(3, w, relay_from, relay_to))
            passed[-1].start()

        def handed(nth, j):
            copy(4 + j, 0, (*chips[j], 1 - c), me).wait_recv()
            save((*chips[j], 1 - c))
            project(nth, (*chips[j], 1 - c))

        arrived(0, 0)
        arrived(0, 1)
        relay(0)
        for j in range(2):
            save((*chips[j], c))
            project(2 + j, (*chips[j], c))
        handed(4, 0)
        handed(5, 1)
        arrived(1, 0)
        arrived(1, 1)
        relay(1)
        arrived(0, 2)
        save((*chips[2], c))
        project(6, (*chips[2], c))
        handed(7, 2)
        arrived(1, 2)
        copy(0, 1, sibling, me).wait_recv()
        for j, chip in enumerate(chips):
            copy(4 + j, 1, (*chip, 1 - c), me).wait_recv()
        keep = pltpu.make_async_copy(wv, wt_o, loc_sems.at[4])
        for p in range(3):
            out_copy(p, p).wait()
        for cp in first + passed:
            cp.wait_send()
        keep.wait()
        for cp in locs:
            cp.wait()

    vmem = pl.BlockSpec(memory_space=pltpu.VMEM)
    bf16 = jnp.bfloat16
    return pl.pallas_call(
        body, name="gather_proj",
        out_shape=(pltpu.HBM((D, T), bf16), pltpu.HBM((NTILE, D, TILE), bf16),
                   pltpu.HBM((DMIX // NDEV, D), bf16), pltpu.HBM((NDEV, DMIX // NDEV, D), bf16),
                   pltpu.HBM((NDEV, NGROUP, GROUP // NDEV, GROUP), bf16), pltpu.HBM((NTILE, T, TILE), jnp.float32)),
        in_specs=[ANY] + [vmem] * 4, out_specs=(ANY,) * 6,
        scratch_shapes=[pltpu.VMEM((T, D), jnp.float32),
                        pltpu.VMEM((T, D), bf16), pltpu.VMEM((D, T), bf16), pltpu.VMEM((NTILE, D, TILE), bf16),
                        pltpu.VMEM((DMIX // NDEV, D), bf16), pltpu.VMEM((NGROUP, GROUP // NDEV, GROUP), bf16),
                        pltpu.VMEM((3, T, TILE), jnp.float32),
                        pltpu.SemaphoreType.DMA((14,)), pltpu.SemaphoreType.DMA((14,)),
                        pltpu.SemaphoreType.DMA((6,)), pltpu.SemaphoreType.DMA((3,))],
        compiler_params=_params(vmem_mib=56),
    )(x, g1, w_in, w_out, pool_w)


def _split_start(name, arrays, n_copies, plan):
    k = len(arrays)

    def body(*refs):
        send_sems, recv_sems, token = refs[k], refs[k + 1], refs[-1]
        for i, (src, dst, to) in enumerate(plan(refs[:k])):
            pltpu.make_async_remote_copy(src_ref=src, dst_ref=dst, send_sem=send_sems.at[i],
                                         recv_sem=recv_sems.at[i], device_id=to, device_id_type=MESH).start()
        token[...] = jnp.zeros_like(token)

    out = pl.pallas_call(
        body, name=name,
        out_shape=(pltpu.SemaphoreType.DMA((n_copies,)), pltpu.SemaphoreType.DMA((n_copies,)),
                   *[pltpu.HBM(a.shape, a.dtype) for a in arrays], jax.ShapeDtypeStruct((8, 128), jnp.float32)),
        in_specs=[HBM] * k, out_specs=(SEM, SEM, *[HBM] * k, pl.BlockSpec(memory_space=pltpu.VMEM)),
        input_output_aliases={i: 2 + i for i in range(k)},
        compiler_params=pltpu.CompilerParams(has_side_effects=EFFECT),
    )(*[pltpu.with_memory_space_constraint(a, pltpu.HBM) for a in arrays])
    return out[0], out[1], out[2:2 + k], out[-1]


def _split_wait(name, arrays, send_sems, recv_sems, plan, after, which=None):
    k = len(arrays)

    def body(*refs):
        sends, recvs = refs[k], refs[k + 1]
        for i, (src, dst, to) in enumerate(plan(refs[:k])):
            if which is not None and i not in which:
                continue
            cp = pltpu.make_async_remote_copy(src_ref=src, dst_ref=dst, send_sem=sends.at[i], recv_sem=recvs.at[i],
                                              device_id=to, device_id_type=MESH)
            cp.wait_send()
            cp.wait_recv()

    return pl.pallas_call(
        body, name=name,
        out_shape=tuple(pltpu.HBM(a.shape, a.dtype) for a in arrays),
        in_specs=[HBM] * k + [SEM, SEM, ANY], out_specs=(HBM,) * k,
        input_output_aliases={i: i for i in range(k)},
        compiler_params=pltpu.CompilerParams(has_side_effects=EFFECT),
    )(*arrays, send_sems, recv_sems, after)


def _plan_wout(refs):
    src, land = refs
    x, y, c, me = _place()
    return [(src, land.at[me], _peer(x, y, c, r)) for r in range(1, NDEV)]


def _plan_rest(refs):
    gob, gpf, r_out, r_pool = refs
    x, y, c, me = _place()
    plan = []
    for r in range(1, NDEV):
        plan.append((gob.at[me ^ r], r_out.at[r - 1], _peer(x, y, c, r)))
        plan.append((gpf.at[me ^ r], r_pool.at[r - 1], _peer(x, y, c, r)))
    return plan


def _plan_in(refs):
    sum_b, r_in = refs
    x, y, c, _ = _place()
    plan = []
    for q in range(4):
        rows = pl.ds(q * (D // 4), D // 4)
        for j, (dx, dy) in enumerate(((1, 0), (0, 1), (1, 1))):
            px, py = x ^ dx, y ^ dy
            plan.append((sum_b.at[2 * px + py, :, rows], r_in.at[j, :, rows], (px, py, c)))
    return plan


def _gather_small(small, *after):
    def body(sm, *refs):
        r_small, send_sems, recv_sems, loc_sem = refs[len(after):]
        x, y, c, me = _place()
        loc = pltpu.make_async_copy(sm, r_small.at[me], loc_sem)
        loc.start()

        def copy(r, src_idx):
            return pltpu.make_async_remote_copy(
                src_ref=sm, dst_ref=r_small.at[src_idx], send_sem=send_sems.at[r - 1], recv_sem=recv_sems.at[r - 1],
                device_id=_peer(x, y, c, r), device_id_type=MESH)

        sends = [copy(r, me) for r in range(1, NDEV)]
        for cp in sends:
            cp.start()
        for r in range(1, NDEV):
            copy(r, me ^ r).wait_recv()
        for cp in sends:
            cp.wait_send()
        loc.wait()

    return pl.pallas_call(
        body, name="gather_small",
        out_shape=jax.ShapeDtypeStruct((NDEV, 8, D), jnp.float32),
        in_specs=[ANY] * (1 + len(after)), out_specs=ANY,
        scratch_shapes=[pltpu.SemaphoreType.DMA((NDEV - 1,)), pltpu.SemaphoreType.DMA((NDEV - 1,)),
                        pltpu.SemaphoreType.DMA],
    )(small, *after)


def _seg_tiles(s):
    return (s + 2) % NSEG


_POOL_SPECS = [pl.BlockSpec((None, T, GROUP), lambda g, base=base: (base + g, 0, 0)) for base in (0, 4)]
_HEAD_SPECS = [pl.BlockSpec((None, T, HEAD), lambda h, base=base: (base + h // 2, 0, h % 2))
               for base in (8, 12, 16, 20)]


def _row_ids(shape):
    return lax.broadcasted_iota(jnp.int32, shape, 0)


BAND_ROWS = 128
HALO = 16


def _window_sum(a, gidx, lead):
    width = lax.shift_left(jnp.int32(2), gidx)
    shape = (BAND_ROWS, BAND_ROWS + HALO)
    t, j = lax.broadcasted_iota(jnp.int32, shape, 0), lax.broadcasted_iota(jnp.int32, shape, 1)
    first = t if lead else t + HALO - width + 1
    band = _bf(jnp.where(j >= first, jnp.where(j < first + width, 1.0, 0.0), 0.0))
    zeros = jnp.zeros((HALO, a.shape[1]), jnp.bfloat16)
    padded = [jnp.concatenate([p, zeros] if lead else [zeros, p], axis=0) for p in _split2(a)]
    out = []
    for r0 in range(0, T, BAND_ROWS):
        slab = jnp.concatenate([p[r0:r0 + BAND_ROWS + HALO] for p in padded], axis=1)
        r = _dot(band, slab, 1, 0)
        out.append(r[:, :a.shape[1]] + r[:, a.shape[1]:])
    return jnp.concatenate(out, axis=0)


def _window_mean(s, gidx):
    inv = jnp.where(gidx == 0, 0.5, jnp.where(gidx == 1, 0.25, jnp.where(gidx == 2, 0.125, 0.0625)))
    width = lax.shift_left(jnp.int32(2), gidx)
    head = s[:16] / jnp.minimum(_row_ids((16, s.shape[1])) + 1, width).astype(jnp.float32)
    return jnp.concatenate([head, s[16:] * inv], axis=0)


def _pool_fwd(proj, pool_w, pool_scale, token):
    def body(u_ref, pg_ref, w_ref, sc_ref, token_any, y_ref):
        del token_any
        gidx = pl.program_id(0)
        u, pg = u_ref[...], pg_ref[...]
        d = _window_mean(_window_sum(u, gidx, False), gidx) - u
        mixed = _dot(_bf(d), w_ref[...], 1, 0)
        y_ref[...] = _bf(mixed * sc_ref[...] * (pg * _sigmoid(pg)))

    return pl.pallas_call(
        body, name="pool_fwd", grid=(NGROUP,),
        in_specs=[*_POOL_SPECS,
                  pl.BlockSpec((None, GROUP, GROUP), lambda g: (g, 0, 0)),
                  pl.BlockSpec((1, GROUP), lambda g: (0, g)), ANY],
        out_specs=pl.BlockSpec((T, GROUP), lambda g: (0, g)),
        out_shape=pltpu.HBM((T, DMIX), jnp.bfloat16),
        compiler_params=_params(("parallel",)),
    )(proj, proj, pool_w, pool_scale, token)


def _tri(lower):
    r = lax.broadcasted_iota(jnp.int32, (CHUNK, CHUNK), 0)
    c = lax.broadcasted_iota(jnp.int32, (CHUNK, CHUNK), 1)
    return (r >= c) if lower else (r <= c)


def _sum_rows_matrix():
    shape = (CHUNK + 16, CHUNK)
    r, c = lax.broadcasted_iota(jnp.int32, shape, 0), lax.broadcasted_iota(jnp.int32, shape, 1)
    run = jnp.where(c <= r, 1.0, 0.0)
    half = jnp.where(c < CHUNK // 2, 1.0, 0.0)
    return _bf(jnp.where(r < CHUNK, run, jnp.where(r < CHUNK + 8, 1.0, half)))


def _rev_sum_matrix():
    shape = (CHUNK, 2 * CHUNK)
    r, c = lax.broadcasted_iota(jnp.int32, shape, 0), lax.broadcasted_iota(jnp.int32, shape, 1)
    return _bf(jnp.where(c < CHUNK, jnp.where(c >= r, 1.0, 0.0), jnp.where(c - CHUNK < r, 1.0, 0.0)))


def _split2(a):
    hi = _bf(a)
    return [hi, _bf(a - hi.astype(jnp.float32))]


def _exact_sums(mat, pieces):
    x = jnp.concatenate([s for p in pieces for s in _split2(p)], axis=1)
    r = _dot(mat, x, 1, 0)
    return [r[:, 2 * j * HEAD:(2 * j + 1) * HEAD] + r[:, (2 * j + 1) * HEAD:(2 * j + 2) * HEAD]
            for j in range(len(pieces))]


def _gates(qv, fl, lb):
    sq = _sigmoid(qv)
    sg = _sigmoid(fl)
    f = lb + (1.0 - lb) * sg
    return dict(sq=sq, qs=qv * sq, sg=sg, f=f, kk=1.0 - f, g=jnp.log2(f))


def _decays(sums):
    big_g = sums[:CHUNK]
    total = sums[CHUNK:CHUNK + 8]
    g_last = jnp.tile(total, (CHUNK // 8, 1))
    g_mid = jnp.tile(sums[CHUNK + 8:], (CHUNK // 8, 1))
    return dict(
        e_q=jnp.exp2(big_g),
        e_k=jnp.exp2(g_last - big_g),
        e_qm=jnp.exp2(jnp.minimum(big_g - g_mid, EXP_CAP)),
        e_km=jnp.exp2(jnp.minimum(g_mid - big_g, EXP_CAP)),
        total8=jnp.exp2(total))


def _group_rows(gi):
    return [pl.ds(pl.multiple_of((gi * NB + j) * CHUNK, CHUNK), CHUNK) for j in range(NB)]


def _lower_bound(lb_ref):
    return _sigmoid(lb_ref[0:1, :] - lb_ref[1:2, :])


def _hgrn_fwd(proj, lb_logits, rec_g, y_in):
    def body(q_ref, f_ref, i_ref, gate_ref, lb_ref, rg_ref, y_any, y_ref, o_ref, st_ref):
        del y_any
        lb = _lower_bound(lb_ref)
        causal = _tri(True)
        smat = _sum_rows_matrix()

        def group(gi, st):
            rows = _group_rows(gi)
            ts = [_gates(q_ref[r, :], f_ref[r, :], lb) for r in rows]
            ds = [_decays(s) for s in _exact_sums(smat, [t["g"] for t in ts])]
            vs = [_bf(i_ref[r, :]) for r in rows]
            q_m = [_bf(t["qs"] * d["e_qm"]) for t, d in zip(ts, ds)]
            k_m = [_bf(t["kk"] * d["e_km"]) for t, d in zip(ts, ds)]
            q_e = [_bf(t["qs"] * d["e_q"]) for t, d in zip(ts, ds)]
            k_e = [_bf(t["kk"] * d["e_k"]) for t, d in zip(ts, ds)]
            a = [_bf(jnp.where(causal, _dot(q_m[j], k_m[j], 1, 1), 0.0)) for j in range(NB)]
            intra = [_dot(a[j], vs[j], 1, 0) for j in range(NB)]
            upd = [_dot(vs[j], k_e[j], 0, 0) for j in range(NB)]
            for j in range(NB):
                st_ref[gi * NB + j] = st
                o_ref[rows[j], :] = intra[j] + _dot(q_e[j], _bf(st), 1, 1)
                st = st * jnp.tile(ds[j]["total8"], (HEAD // 8, 1)) + upd[j]
            return st

        lax.fori_loop(0, NGRP, group, jnp.zeros((HEAD, HEAD), jnp.float32))
        o = o_ref[...]
        rn = o * lax.rsqrt(jnp.mean(o * o, axis=-1, keepdims=True) + EPS)
        gate = gate_ref[...]
        y_ref[...] = _bf(rn * rg_ref[...] * (gate * _sigmoid(gate)))

    return pl.pallas_call(
        body, name="hgrn_fwd", grid=(NHEAD,),
        in_specs=[*_HEAD_SPECS,
                  pl.BlockSpec((2, HEAD), lambda h: (0, h)),
                  pl.BlockSpec((1, HEAD), lambda h: (0, h)),
                  pl.BlockSpec(memory_space=pl.ANY)],
        out_specs=(pl.BlockSpec((T, HEAD), lambda h: (0, NHEAD + h)),
                   pl.BlockSpec((T, HEAD), lambda h: (0, h)),
                   pl.BlockSpec((None, NCHUNK, HEAD, HEAD), lambda h: (h, 0, 0, 0))),
        out_shape=(pltpu.HBM((T, DMIX), jnp.bfloat16), pltpu.HBM((T, D), jnp.float32),
                   pltpu.HBM((NHEAD, NCHUNK, HEAD, HEAD), jnp.float32)),
        input_output_aliases={6: 0},
        compiler_params=_params(("parallel",)),
    )(proj, proj, proj, proj, lb_logits, rec_g, y_in)


def _out_proj_loss(x, y, w_out, target, gf):
    rows = 512
    parts = [slice(k * rows // 2, (k + 1) * rows // 2) for k in range(2)]

    def body(x_ref, y_ref, w_ref, t_ref, g_ref, dz_ref, dzb_ref, sq_ref, dg_ref):
        zs = [x_ref[p, :] + _dot(y_ref[p, :], w_ref[...], 1, 0) for p in parts]
        sq = dg = 0.0
        for p, z in zip(parts, zs):
            r = lax.rsqrt(jnp.mean(z * z, axis=-1, keepdims=True) + EPS)
            zhat = z * r
            err = zhat * g_ref[...] - t_ref[p, :]
            dy = err * (1.0 / D)
            gdy = dy * g_ref[...]
            dz = r * (gdy - zhat * jnp.mean(zhat * gdy, axis=-1, keepdims=True))
            dz_ref[p, :] = dz
            dzb_ref[p, :] = _bf(dz)
            sq = sq + jnp.sum(err * err, axis=0, keepdims=True)
            dg = dg + jnp.sum(zhat * dy, axis=0, keepdims=True)

        @pl.when(pl.program_id(0) == 0)
        def _():
            sq_ref[...] = sq
            dg_ref[...] = dg

        @pl.when(pl.program_id(0) != 0)
        def _():
            sq_ref[...] += sq
            dg_ref[...] += dg

    tile = pl.BlockSpec((rows, D), lambda i: (i, 0))
    vec = pl.BlockSpec((1, D), lambda i: (0, 0))
    return pl.pallas_call(
        body, name="out_proj_loss", grid=(T // rows,),
        in_specs=[tile, pl.BlockSpec((rows, DMIX), lambda i: (i, 0)), pl.BlockSpec((DMIX, D), lambda i: (0, 0)),
                  tile, vec],
        out_specs=(tile, tile, vec, vec),
        out_shape=(pltpu.HBM((T, D), jnp.float32), pltpu.HBM((T, D), jnp.bfloat16),
                   jax.ShapeDtypeStruct((1, D), jnp.float32), jax.ShapeDtypeStruct((1, D), jnp.float32)),
        compiler_params=_params(("arbitrary",)),
    )(x, y, w_out, target, gf)


def _out_proj_bwd(dzb, w_out, y):
    tn = 512

    def body(dz_ref, w_ref, y_ref, dy_ref, gw_ref, gwb_ref):
        dz = dz_ref[...]
        dy_ref[...] = _dot(dz, w_ref[...], 1, 1)
        gw = _dot(y_ref[...], dz, 0, 0)
        gw_ref[...] = gw
        gwb_ref[...] = _bf(gw)

    return pl.pallas_call(
        body, name="out_proj_bwd", grid=(DMIX // tn,),
        in_specs=[pl.BlockSpec((T, D), lambda n: (0, 0)), pl.BlockSpec((tn, D), lambda n: (n, 0)),
                  pl.BlockSpec((T, tn), lambda n: (0, n))],
        out_specs=(pl.BlockSpec((T, tn), lambda n: (0, n)), pl.BlockSpec((tn, D), lambda n: (n, 0)),
                   pl.BlockSpec((tn, D), lambda n: (n, 0))),
        out_shape=(pltpu.HBM((T, DMIX), jnp.float32), pltpu.HBM((DMIX, D), jnp.float32),
                   pltpu.HBM((DMIX, D), jnp.bfloat16)),
        compiler_params=_params(("parallel",)),
    )(dzb, w_out, y)


def _hgrn_bwd(proj, lb_logits, rec_g, o, states, dymix, dproj_in, token):
    def body(q_ref, f_ref, i_ref, gate_ref, lb_ref, rg_ref, o_ref, st_ref, dy_ref, dp_any, token_any,
             dp_ref, drg_ref, dlb_ref, do_ref):
        del dp_any, token_any
        lb = _lower_bound(lb_ref)
        causal = _tri(True)
        smat, rmat = _sum_rows_matrix(), _rev_sum_matrix()

        o = o_ref[...]
        rs = lax.rsqrt(jnp.mean(o * o, axis=-1, keepdims=True) + EPS)
        rn = o * rs
        gate = gate_ref[...]
        sgate = _sigmoid(gate)
        dyv = dy_ref[...]
        d_r = dyv * (gate * sgate)
        dp_ref[3] = _bf(dyv * (rn * rg_ref[...]) * (sgate * (1.0 + gate * (1.0 - sgate))))
        drg_ref[...] = jnp.sum(d_r * rn, axis=0, keepdims=True)
        drn = d_r * rg_ref[...]
        do_ref[...] = rs * (drn - rn * jnp.mean(rn * drn, axis=-1, keepdims=True))

        def group(i, carry):
            dst, dlb = carry
            gi = NGRP - 1 - i
            rows = _group_rows(gi)
            span = range(NB)
            qvs = [q_ref[r, :] for r in rows]
            ts = [_gates(qv, f_ref[r, :], lb) for qv, r in zip(qvs, rows)]
            ds = [_decays(s) for s in _exact_sums(smat, [t["g"] for t in ts])]
            vs = [_bf(i_ref[r, :]) for r in rows]
            dos = [_bf(do_ref[r, :]) for r in rows]
            sts = [st_ref[gi * NB + j] for j in span]
            qe_f = [t["qs"] * d["e_q"] for t, d in zip(ts, ds)]
            ke_f = [t["kk"] * d["e_k"] for t, d in zip(ts, ds)]
            q_e, k_e = [_bf(a) for a in qe_f], [_bf(a) for a in ke_f]
            q_m = [_bf(t["qs"] * d["e_qm"]) for t, d in zip(ts, ds)]
            k_m = [_bf(t["kk"] * d["e_km"]) for t, d in zip(ts, ds)]
            a = [_bf(jnp.where(causal, _dot(q_m[j], k_m[j], 1, 1), 0.0)) for j in span]
            da = [_bf(jnp.where(causal, _dot(dos[j], vs[j], 1, 1), 0.0)) for j in span]
            dqm = [_dot(da[j], k_m[j], 1, 0) for j in span]
            dkm = [_dot(da[j], q_m[j], 0, 0) for j in span]
            dv_in = [_dot(a[j], dos[j], 0, 0) for j in span]
            dqe = [_dot(dos[j], _bf(sts[j]), 1, 0) for j in span]
            grow = [_dot(dos[j], q_e[j], 0, 0) for j in span]
            dke, carried = [None] * NB, [None] * NB
            for j in reversed(span):
                dst_b = _bf(dst)
                dke[j] = _dot(vs[j], dst_b, 1, 0)
                dp_ref[2, rows[j], :] = _bf(dv_in[j] + _dot(k_e[j], dst_b, 1, 1))
                carried[j] = ds[j]["total8"] * jnp.sum(dst * sts[j], axis=0, keepdims=True)
                dst = dst * jnp.tile(ds[j]["total8"], (HEAD // 8, 1)) + grow[j]
            kdk = [ke_f[j] * dke[j] for j in span]
            pos = [(q_m[j].astype(jnp.float32) * dqm[j] - k_m[j].astype(jnp.float32) * dkm[j]) + qe_f[j] * dqe[j]
                   for j in span]
            dgs = _exact_sums(rmat, [jnp.concatenate([pos[j], kdk[j]], axis=0) for j in span])
            for j in span:
                t, d = ts[j], ds[j]
                dg = dgs[j] + jnp.tile(carried[j], (CHUNK // 8, 1))
                dqs = dqm[j] * d["e_qm"] + dqe[j] * d["e_q"]
                dkk = dkm[j] * d["e_km"] + dke[j] * d["e_k"]
                df = dg / t["f"] - dkk
                dp_ref[1, rows[j], :] = _bf(df * (1.0 - lb) * (t["sg"] * (1.0 - t["sg"])))
                dp_ref[0, rows[j], :] = _bf(dqs * (t["sq"] * (1.0 + qvs[j] * (1.0 - t["sq"]))))
                dlb = dlb + df * (1.0 - t["sg"])
            return dst, dlb

        _, dlb = lax.fori_loop(0, NGRP, group, (jnp.zeros((HEAD, HEAD), jnp.float32),
                                                jnp.zeros((CHUNK, HEAD), jnp.float32)))
        dlb_ref[...] = jnp.sum(dlb, axis=0, keepdims=True)

    vec = pl.BlockSpec((1, HEAD), lambda h: (0, h))
    return pl.pallas_call(
        body, name="hgrn_bwd", grid=(NHEAD,),
        in_specs=[*_HEAD_SPECS,
                  pl.BlockSpec((2, HEAD), lambda h: (0, h)), vec,
                  pl.BlockSpec((T, HEAD), lambda h: (0, h)),
                  pl.BlockSpec((None, NCHUNK, HEAD, HEAD), lambda h: (h, 0, 0, 0)),
                  pl.BlockSpec((T, HEAD), lambda h: (0, NHEAD + h)), ANY, ANY],
        out_specs=(pl.BlockSpec((4, T, HEAD), lambda h: (0, 0, h)), vec, vec),
        out_shape=(pltpu.HBM((NSEG, T, D), jnp.bfloat16),
                   jax.ShapeDtypeStruct((1, D), jnp.float32), jax.ShapeDtypeStruct((1, D), jnp.float32)),
        scratch_shapes=[pltpu.VMEM((T, HEAD), jnp.float32)],
        input_output_aliases={9: 0},
        compiler_params=_params(("parallel",)),
    )(proj, proj, proj, proj, lb_logits, rec_g, o, states, dymix, dproj_in, token)


def _pool_bwd(proj, pool_w, pool_scale, dymix):
    def body(u_ref, pg_ref, w_ref, sc_ref, dy_ref, dp_ref, gw_ref, gs_ref):
        gidx = pl.program_id(0)
        u, pg = u_ref[...], pg_ref[...]
        d = _bf(_window_mean(_window_sum(u, gidx, False), gidx) - u)
        mixed = _dot(d, w_ref[...], 1, 0)
        spg = _sigmoid(pg)
        dyv = dy_ref[...]
        d_p = dyv * (pg * spg)
        dp_ref[1] = _bf(dyv * (mixed * sc_ref[...]) * (spg * (1.0 + pg * (1.0 - spg))))
        gs_ref[...] = jnp.sum(d_p * mixed, axis=0, keepdims=True)
        dmixed = _bf(d_p * sc_ref[...])
        gw_ref[...] = _dot(d, dmixed, 0, 0)
        dd = _dot(dmixed, w_ref[...], 1, 1)
        dp_ref[0] = _bf(_window_sum(_window_mean(dd, gidx), gidx, True) - dd)

    return pl.pallas_call(
        body, name="pool_bwd", grid=(NGROUP,),
        in_specs=[*_POOL_SPECS,
                  pl.BlockSpec((None, GROUP, GROUP), lambda g: (g, 0, 0)),
                  pl.BlockSpec((1, GROUP), lambda g: (0, g)),
                  pl.BlockSpec((T, GROUP), lambda g: (0, g))],
        out_specs=(pl.BlockSpec((2, T, GROUP), lambda g: (2, 0, g)),
                   pl.BlockSpec((None, GROUP, GROUP), lambda g: (g, 0, 0)),
                   pl.BlockSpec((1, GROUP), lambda g: (0, g))),
        out_shape=(pltpu.HBM((NSEG, T, D), jnp.bfloat16),
                   jax.ShapeDtypeStruct((NGROUP, GROUP, GROUP), jnp.float32),
                   jax.ShapeDtypeStruct((1, D), jnp.float32)),
        compiler_params=_params(("parallel",)),
    )(proj, proj, pool_w, pool_scale, dymix)


HALF = NTILE // 2
AWAY = HALF - 3


def _dproj_tile(chip, side, p):
    j = 6 * chip + 3 * side + p
    return ((j // 4 + 4) % NSEG, 0, j % 4)


def _sibling_copy(sib_out, sib_in, send_sems, recv_sems, slot):
    x, y, c, _ = _place()
    return pltpu.make_async_remote_copy(
        src_ref=sib_out.at[slot], dst_ref=sib_in.at[slot], send_sem=send_sems.at[slot],
        recv_sem=recv_sems.at[slot], device_id=(x, y, 1 - c), device_id_type=MESH)


def _proj_bwd_w_far(place, ht, dproj):
    def body(place_ref, h_ref, dp_ref, sib_out, sib_in, send_sems, recv_sems, stage, loc_sems):
        del place_ref
        i = pl.program_id(0)

        def to_hbm(k):
            return pltpu.make_async_copy(stage.at[k], sib_out.at[k], loc_sems.at[k])

        def send(k):
            to_hbm(k).wait()
            _sibling_copy(sib_out, sib_in, send_sems, recv_sems, k).start()

        stage[i] = _bf(_dot(h_ref[...], dp_ref[...], 1, 0))

        @pl.when(i > 0)
        def _():
            send(i - 1)

        to_hbm(i).start()

        @pl.when(i == HALF - 1)
        def _():
            send(i)

    buf = pltpu.HBM((HALF, D, TILE), jnp.bfloat16)
    sems = pltpu.SemaphoreType.DMA((HALF,))
    return pl.pallas_call(
        body, name="proj_bwd_w_far",
        grid_spec=pltpu.PrefetchScalarGridSpec(
            num_scalar_prefetch=1, grid=(HALF,),
            in_specs=[pl.BlockSpec((D, T), lambda i, pr: (0, 0)),
                      pl.BlockSpec((None, T, TILE), lambda i, pr: _dproj_tile(i // 3, 1 - pr[2], i % 3))],
            out_specs=(HBM, HBM, SEM, SEM),
            scratch_shapes=[pltpu.VMEM((HALF, D, TILE), jnp.bfloat16), pltpu.SemaphoreType.DMA((HALF,))]),
        out_shape=(buf, buf, sems, sems),
        compiler_params=pltpu.CompilerParams(dimension_semantics=("arbitrary",), vmem_limit_bytes=48 * MIB,
                                             has_side_effects=EFFECT),
    )(place, ht, dproj)


def _proj_bwd_w_near(place, ht, dproj, sib_out, sib_in, sib_send, sib_recv):
    def owner_chip(k, pr):
        return (pr[1] + 1 + k // 3) % 4

    def body(place_ref, h_ref, dp_ref, sib_out, sib_in, sib_send, sib_recv, sums, own_ref, landing, out_send,
             out_recv, recvbuf, outbuf, in_sems, loc_sems):
        i = pl.program_id(0)
        px, py, c, _ = _place()

        def slot_of(k):
            return 3 * owner_chip(k, place_ref) + k % 3

        def load(k):
            return pltpu.make_async_copy(sib_in.at[slot_of(k)], recvbuf.at[k % 2], in_sems.at[k % 2])

        def fetch(k):
            _sibling_copy(sib_out, sib_in, sib_send, sib_recv, slot_of(k)).wait_recv()
            load(k).start()

        def route(k):
            chip = owner_chip(k, place_ref)
            cx, cy = chip // 2, chip % 2
            return cx, cy, (cx ^ px) + 2 * (cy ^ py) - 1, k % 3

        def to_hbm(k):
            _, _, rel, p = route(k)
            return pltpu.make_async_copy(outbuf.at[k], sums.at[rel, p], loc_sems.at[k])

        def to_owner(k):
            cx, cy, rel, p = route(k)
            return pltpu.make_async_remote_copy(
                src_ref=sums.at[rel, p], dst_ref=landing.at[rel, p], send_sem=out_send.at[3 * rel + p],
                recv_sem=out_recv.at[3 * rel + p], device_id=(cx, cy, c), device_id_type=MESH)

        @pl.when(i == 0)
        def _():
            fetch(i)

        @pl.when(i < HALF - 1)
        def _():
            fetch(i + 1)

        gw = _dot(h_ref[...], dp_ref[...], 1, 0)

        @pl.when(jnp.logical_and(i > 0, i <= AWAY))
        def _():
            to_hbm(i - 1).wait()
            to_owner(i - 1).start()

        load(i).wait()
        total = gw + recvbuf[i % 2].astype(jnp.float32)
        own_ref[...] = total

        @pl.when(i < AWAY)
        def _():
            outbuf[i] = _bf(total)
            to_hbm(i).start()

        @pl.when(i == HALF - 1)
        def _():
            for slot in range(HALF):
                _sibling_copy(sib_out, sib_in, sib_send, sib_recv, slot).wait_send()

    travelling = pltpu.HBM((3, 3, D, TILE), jnp.bfloat16)
    sems = pltpu.SemaphoreType.DMA((AWAY,))
    return pl.pallas_call(
        body, name="proj_bwd_w_near",
        grid_spec=pltpu.PrefetchScalarGridSpec(
            num_scalar_prefetch=1, grid=(HALF,),
            in_specs=[pl.BlockSpec((D, T), lambda i, pr: (0, 0)),
                      pl.BlockSpec((None, T, TILE), lambda i, pr: _dproj_tile(owner_chip(i, pr), pr[2], i % 3)),
                      HBM, HBM, SEM, SEM],
            out_specs=(HBM, pl.BlockSpec((None, D, TILE), lambda i, pr: (jnp.where(i < AWAY, 0, i % 3), 0, 0)),
                       HBM, SEM, SEM),
            scratch_shapes=[pltpu.VMEM((2, D, TILE), jnp.bfloat16), pltpu.VMEM((AWAY, D, TILE), jnp.bfloat16),
                            pltpu.SemaphoreType.DMA((2,)), pltpu.SemaphoreType.DMA((AWAY,))]),
        out_shape=(travelling, pltpu.HBM((3, D, TILE), jnp.float32), travelling, sems, sems),
        compiler_params=pltpu.CompilerParams(dimension_semantics=("arbitrary",), vmem_limit_bytes=48 * MIB,
                                             has_side_effects=EFFECT),
    )(place, ht, dproj, sib_out, sib_in, sib_send, sib_recv)


def _proj_bwd_w(place, ht, dproj):
    half = NTILE // 2

    def owner_chip(i, pr):
        return jnp.where(i < half, i // 3, (pr[1] + 1 + (i - half) // 3) % 4)

    def tile_of(i, pr):
        side = jnp.where(i < half, 1 - pr[2], pr[2])
        return 6 * owner_chip(i, pr) + 3 * side + i % 3

    def dproj_block(i, pr):
        j = tile_of(i, pr)
        return ((j // 4 + 4) % NSEG, 0, j % 4)

    def mine(i):
        return jnp.maximum(i, half)

    def body(place_ref, h_ref, dp_ref, sum_ref, own_ref, sendbuf, recvbuf, send_sems, recv_sems):
        i = pl.program_id(0)
        px, py, c, _ = _place()
        gw = _dot(h_ref[...], dp_ref[...], 1, 0)

        def to_sibling(slot):
            return pltpu.make_async_remote_copy(
                src_ref=sendbuf.at[slot], dst_ref=recvbuf.at[slot], send_sem=send_sems.at[slot],
                recv_sem=recv_sems.at[slot], device_id=(px, py, 1 - c), device_id_type=MESH)

        @pl.when(i < half)
        def _():
            sendbuf[i] = _bf(gw)
            to_sibling(i).start()

        @pl.when(i >= half)
        def _():
            slot = 3 * owner_chip(i, place_ref) + i % 3
            to_sibling(slot).wait_recv()
            total = gw + recvbuf[slot].astype(jnp.float32)
            sum_ref[...] = _bf(total)
            own_ref[...] = total

        @pl.when(i == NTILE - 1)
        def _():
            for slot in range(half):
                to_sibling(slot).wait_send()

    return pl.pallas_call(
        body, name="proj_bwd_w",
        grid_spec=pltpu.PrefetchScalarGridSpec(
            num_scalar_prefetch=1, grid=(NTILE,),
            in_specs=[pl.BlockSpec((D, T), lambda i, pr: (0, 0)),
                      pl.BlockSpec((None, T, TILE), lambda i, pr: dproj_block(i, pr))],
            out_specs=(pl.BlockSpec((None, None, D, TILE), lambda i, pr: (owner_chip(mine(i), pr), mine(i) % 3, 0, 0)),
                       pl.BlockSpec((None, D, TILE), lambda i, pr: (jnp.where(i < NTILE - 3, 0, i % 3), 0, 0))),
            scratch_shapes=[pltpu.VMEM((half, D, TILE), jnp.bfloat16), pltpu.VMEM((half, D, TILE), jnp.bfloat16),
                            pltpu.SemaphoreType.DMA((half,)), pltpu.SemaphoreType.DMA((half,))]),
        out_shape=(pltpu.HBM((4, 3, D, TILE), jnp.bfloat16), pltpu.HBM((3, D, TILE), jnp.float32)),
        compiler_params=_params(("arbitrary",)),
    )(place, ht, dproj)


def _proj_bwd_x(dproj, w_t, x, g1, dz, token):
    tm = 512
    pairs = NSEG // 2

    def body(dp_ref, w_ref, x_ref, g_ref, dz_ref, token_any, dx_ref, dg_ref, wcat, acc):
        del token_any
        s, m = pl.program_id(0), pl.program_id(1)

        @pl.when(m == 0)
        def _():
            for i in range(8):
                wcat[:, i * TILE:(i + 1) * TILE] = w_ref[i]

        mine = pl.ds(pl.multiple_of(m * tm, tm), tm)

        @pl.when(s == 0)
        def _():
            acc[mine, :] = jnp.zeros((tm, D), jnp.float32)

        acc[mine, :] += _dot(jnp.concatenate([dp_ref[0], dp_ref[1]], axis=1), wcat[...], 1, 1)

        @pl.when(s == pairs - 1)
        def _():
            xv = x_ref[...]
            rs = lax.rsqrt(jnp.mean(xv * xv, axis=-1, keepdims=True) + EPS)
            xhat = xv * rs
            dhv = acc[mine, :]
            gdh = dhv * g_ref[...]
            dx_ref[...] = dz_ref[...] + rs * (gdh - xhat * jnp.mean(xhat * gdh, axis=-1, keepdims=True))
            dg = jnp.sum(xhat * dhv, axis=0, keepdims=True)

            @pl.when(m == 0)
            def _():
                dg_ref[...] = dg

            @pl.when(m != 0)
            def _():
                dg_ref[...] += dg

    rows = pl.BlockSpec((tm, D), lambda s, m: (jnp.where(s == pairs - 1, m, 0), 0))
    vec = pl.BlockSpec((1, D), lambda s, m: (0, 0))
    return pl.pallas_call(
        body, name="proj_bwd_x", grid=(pairs, T // tm),
        in_specs=[pl.BlockSpec((2, tm, D), lambda s, m: (s, m, 0)),
                  pl.BlockSpec((8, D, TILE), lambda s, m: ((s + 1) % pairs, 0, 0)), rows, vec, rows, ANY],
        out_specs=(rows, vec),
        out_shape=(jax.ShapeDtypeStruct((T, D), jnp.float32), jax.ShapeDtypeStruct((1, D), jnp.float32)),
        scratch_shapes=[pltpu.VMEM((D, 2 * D), jnp.bfloat16), pltpu.VMEM((T, D), jnp.float32)],
        compiler_params=_params(("arbitrary", "arbitrary"), vmem_mib=56),
    )(dproj, w_t, x, g1, dz, token)


def _adamw(w, g, m, v):
    m_new = ADAM_B1 * m + (1.0 - ADAM_B1) * g
    v_new = ADAM_B2 * v + (1.0 - ADAM_B2) * (g * g)
    delta = -ADAM_LR * ((m_new / BC1) / (jnp.sqrt(v_new / BC2) + ADAM_EPS) + ADAM_WD * w)
    return delta, m_new, v_new


def _reduce_adam(name, place, parts, w, m, v, grid, w_spec, into=()):
    n = len(parts)

    def body(place_ref, *refs):
        del place_ref
        w_ref, m_ref, v_ref = refs[n:n + 3]
        g_ref, d_ref, mo_ref, vo_ref = refs[-4:]
        g = None
        for ref, (_, _, stacked) in zip(refs[:n], parts):
            terms = [ref[r] for r in range(ref.shape[0])] if stacked else [ref[...]]
            for t in terms:
                if t.shape[-1] != w_ref.shape[-1]:
                    t = jnp.concatenate([t[p] for p in range(t.shape[0])], axis=1)
                g = t.astype(jnp.float32) if g is None else g + t.astype(jnp.float32)
        delta, m_new, v_new = _adamw(w_ref[...], g, m_ref[...], v_ref[...])
        g_ref[...] = g
        d_ref[...] = delta
        mo_ref[...] = m_new
        vo_ref[...] = v_new

    shape = jax.ShapeDtypeStruct(w.shape, jnp.float32)
    return pl.pallas_call(
        body, name=name,
        grid_spec=pltpu.PrefetchScalarGridSpec(
            num_scalar_prefetch=1, grid=grid,
            in_specs=[spec for _, spec, _ in parts] + [w_spec] * 3 + [ANY] * len(into), out_specs=(w_spec,) * 4),
        out_shape=(shape,) * 4,
        input_output_aliases={1 + n + 3 + j: j for j in range(len(into))},
        compiler_params=_params(("parallel",)),
    )(place, *[_in_hbm(a) for a in [a for a, _, _ in parts] + [w, m, v, *into]])


def _small_adam(parts, w, m, v):
    def body(p_ref, w_ref, m_ref, v_ref, g_ref, d_ref, mo_ref, vo_ref):
        g = p_ref[0]
        for s in range(1, NDEV):
            g = g + p_ref[s]
        wv = w_ref[...]
        rows = _row_ids(wv.shape)
        other = jnp.where(rows == 2, pltpu.roll(wv, 7, 0), jnp.where(rows == 3, pltpu.roll(wv, 1, 0), 0.0))
        lbv = _sigmoid(wv - other)
        sign = jnp.where(rows == 2, 1.0, -1.0)
        g = jnp.where((rows == 2) | (rows == 3), sign * g * lbv * (1.0 - lbv), g)
        delta, m_new, v_new = _adamw(wv, g, m_ref[...], v_ref[...])
        g_ref[...] = g
        d_ref[...] = delta
        mo_ref[...] = m_new
        vo_ref[...] = v_new

    shape = jax.ShapeDtypeStruct((8, D), jnp.float32)
    return pl.pallas_call(body, name="small_adam", out_shape=(shape,) * 4)(parts, w, m, v)


def _rows8(*vecs):
    rows = [a.reshape(-1, D) for a in vecs]
    n = sum(r.shape[0] for r in rows)
    return jnp.concatenate(rows + [jnp.zeros((8 - n, D), jnp.float32)], axis=0)


def kernel(x, norm1_g, w_in, pool_w, pool_scale, lb_logits, rec_norm_g, w_out, final_norm_g, loss_target, m_norm1_g, m_w_in, m_pool_w, m_pool_scale, m_lb_logits, m_rec_norm_g, m_w_out, m_final_norm_g, v_norm1_g, v_w_in, v_pool_w, v_pool_scale, v_lb_logits, v_rec_norm_g, v_w_out, v_final_norm_g):
    xs = x[0]
    target = loss_target[0]
    ix, iy, ic = lax.axis_index("x"), lax.axis_index("y"), lax.axis_index("c")
    place = jnp.stack([4 * ix + 2 * iy + ic, 2 * ix + iy, ic]).astype(jnp.int32)
    gf = final_norm_g.reshape(1, D)

    ht, w_t, w_out_b, w_out_g, pool_g, proj = _gather_proj(xs, norm1_g, w_in, w_out, pool_w)
    pool_full = pool_g.transpose(1, 0, 2, 3).reshape(NGROUP, GROUP, GROUP)
    wout = [w_out_b, w_out_g]
    wout_send, wout_recv, wout, wout_token = _split_start("gather_wout_start", wout, NDEV - 1, _plan_wout)

    y = _pool_fwd(proj, pool_full, pool_scale, wout_token)
    y, o, states = _hgrn_fwd(proj, lb_logits, rec_norm_g, y)
    _, w_out_g = _split_wait("gather_wout_wait", wout, wout_send, wout_recv, _plan_wout, o)
    w_out_full = _in_hbm(w_out_g.reshape(DMIX, D))
    dz, dzb, sq, dgf = _out_proj_loss(xs, y, w_out_full, target, gf)

    dymix, gwout_f, gwout_b = _out_proj_bwd(dzb, w_out_full, y)
    dproj, gpool, dscale = _pool_bwd(proj, pool_full, pool_scale, dymix)

    blk_out = (NDEV, DMIX // NDEV, D)
    blk_pool = (NDEV, NGROUP, GROUP // NDEV, GROUP)
    gpool_s = gpool.reshape(NGROUP, NDEV, GROUP // NDEV, GROUP).transpose(1, 0, 2, 3)
    rest = [gwout_b.reshape(blk_out), gpool_s,
            lax.empty((NDEV - 1,) + blk_out[1:], jnp.bfloat16), lax.empty((NDEV - 1,) + blk_pool[1:], jnp.float32)]
    rest_send, rest_recv, rest, rest_token = _split_start("scatter_rest_start", rest, 2 * (NDEV - 1), _plan_rest)

    dproj, drecg, dlb = _hgrn_bwd(proj, lb_logits, rec_norm_g, o, states, dymix, dproj, rest_token)
    chip_sums, own_sum = _proj_bwd_w(place, ht, dproj)
    win = [chip_sums, lax.empty((3, 3, D, TILE), jnp.bfloat16)]
    win_send, win_recv, win, win_token = _split_start("scatter_win_start", win, 12, _plan_in)

    grad_x, dg1 = _proj_bwd_x(dproj, w_t, xs, norm1_g, dz, win_token)

    _, gpool_own, r_out, r_pool = _split_wait("scatter_rest_wait", rest, rest_send, rest_recv, _plan_rest, grad_x)
    g_wout, d_wout, m_wout, v_wout = _reduce_adam(
        "adam_w_out", place,
        [(gwout_f.reshape(blk_out), pl.BlockSpec((None,) + blk_out[1:], lambda i, pr: (pr[0], 0, 0)), False),
         (r_out, pl.BlockSpec((NDEV - 1,) + blk_out[1:], lambda i, pr: (0, 0, 0)), True)],
        w_out, m_w_out, v_w_out, (1,), pl.BlockSpec((None,) + blk_out[1:], lambda i, pr: (0, 0, 0)))
    g_pool, d_pool, m_pool, v_pool = _reduce_adam(
        "adam_pool_w", place,
        [(gpool_own, pl.BlockSpec((None,) + blk_pool[1:], lambda i, pr: (pr[0], 0, 0, 0)), False),
         (r_pool, pl.BlockSpec((NDEV - 1,) + blk_pool[1:], lambda i, pr: (0, 0, 0, 0)), True)],
        pool_w, m_pool_w, v_pool_w, (1,), pl.BlockSpec((None,) + blk_pool[1:], lambda i, pr: (0, 0, 0, 0)))

    def adam_w_in(name, r_in, first, count, into=()):
        return _reduce_adam(
            name, place,
            [(own_sum, pl.BlockSpec((3, D // 8, TILE), lambda i, pr: (0, first + i, 0)), False),
             (r_in, pl.BlockSpec((3, 3, D // 8, TILE), lambda i, pr: (0, 0, first + i, 0)), True)],
            w_in, m_w_in, v_w_in, (count,), pl.BlockSpec((None, D // 8, 3 * TILE), lambda i, pr: (0, first + i, 0)),
            into)

    win = _split_wait("scatter_win_wait_early", win, win_send, win_recv, _plan_in, d_pool, range(9))
    early = adam_w_in("adam_w_in_early", win[1], 0, 6)

    r_small = _gather_small(_rows8(dg1, dscale, dlb, dlb, drecg, dgf, sq), d_wout, early[1])
    g_s, d_s, m_s, v_s = _small_adam(
        r_small,
        _rows8(norm1_g, pool_scale, lb_logits, rec_norm_g, final_norm_g),
        _rows8(m_norm1_g, m_pool_scale, m_lb_logits, m_rec_norm_g, m_final_norm_g),
        _rows8(v_norm1_g, v_pool_scale, v_lb_logits, v_rec_norm_g, v_final_norm_g))
    loss = jnp.sum(g_s[6]) * (0.5 / D)

    _, r_in = _split_wait("scatter_win_wait", win, win_send, win_recv, _plan_in, v_s, range(9, 12))
    g_win, d_win, m_win, v_win = adam_w_in("adam_w_in", r_in, 6, 2, early)

    def small_outs(a):
        return a[0:1], a[1:2], a[2:4], a[4:5], a[5]

    def outs(small_a, win, pool, wout):
        n1, ps, lbl, rg, fg = small_outs(small_a)
        return n1, win, pool, ps, lbl, rg, wout, fg

    return (loss, grad_x[None],
            *outs(g_s, g_win, g_pool, g_wout), *outs(d_s, d_win, d_pool, d_wout),
            *outs(m_s, m_win, m_pool, m_wout), *outs(v_s, v_win, v_pool, v_wout))
```

```python
import functools

import jax
import jax.numpy as jnp
from jax import lax
from jax.experimental import pallas as pl
from jax.experimental.pallas import tpu as pltpu

T = 2048
D = 1024
NSEG = 6
NTILE = 24
TILE = 256
DMIX = 2048
NDEV = 8
HEAD = 128
NHEAD = 8
CHUNK = 64
NCHUNK = T // CHUNK
NB = 32
NGRP = NCHUNK // NB
NGROUP = 4
GROUP = 256
EPS = 1e-6
EXP_CAP = 115.0
MESH = pl.DeviceIdType.MESH
AXES = ("x", "y", "c")
ANY = pl.BlockSpec(memory_space=pl.ANY)
HBM = pl.BlockSpec(memory_space=pltpu.HBM)
SEM = pl.BlockSpec(memory_space=pltpu.SEMAPHORE)
EFFECT = pltpu.SideEffectType.DATAFLOW_SIDE_EFFECTING

ADAM_LR = 0.001
ADAM_B1 = 0.9
ADAM_B2 = 0.999
ADAM_EPS = 1e-08
ADAM_WD = 0.01
ADAM_STEP = 10
BC1 = 1.0 - ADAM_B1 ** ADAM_STEP
BC2 = 1.0 - ADAM_B2 ** ADAM_STEP

MIB = 1 << 20


def _params(sem=None, vmem_mib=48):
    return pltpu.CompilerParams(dimension_semantics=sem, vmem_limit_bytes=vmem_mib * MIB)


def _sigmoid(v):
    return 1.0 / (1.0 + jnp.exp(-v))


def _dot(a, b, ca, cb, precision=None):
    return lax.dot_general(a, b, (((ca,), (cb,)), ((), ())), precision=precision,
                           preferred_element_type=jnp.float32)


def _bf(v):
    return v.astype(jnp.bfloat16)


def _in_hbm(a):
    return pltpu.with_memory_space_constraint(a, pltpu.HBM)


def _place():
    x, y, c = lax.axis_index("x"), lax.axis_index("y"), lax.axis_index("c")
    return x, y, c, 4 * x + 2 * y + c


def _peer(x, y, c, r):
    return (x ^ ((r >> 2) & 1), y ^ ((r >> 1) & 1), c ^ (r & 1))


def _gather_proj(x, g1, w_in, w_out, pool_w):
    def body(x_ref, g_ref, win_ref, wout_ref, pool_ref, ht_o, wt_o, woutb_o, wout_o, pool_o, proj_o,
             xbuf, hv, htv, wv, wob, pb, stage, send_sems, recv_sems, loc_sems, out_sems):
        px, py, c, my_idx = _place()
        fetch_x = pltpu.make_async_copy(x_ref, xbuf, loc_sems.at[5])
        fetch_x.start()
        me, sibling = (px, py, c), (px, py, 1 - c)
        chips = [(1 - px, py), (px, 1 - py), (1 - px, 1 - py)]
        for p in range(3):
            wv[3 * my_idx + p] = _bf(win_ref[0, :, p * TILE:(p + 1) * TILE])

        def index(bx, by, bc):
            return 4 * bx + 2 * by + bc

        def slot(w, block):
            return wv.at[pl.ds(3 * index(*block), 3)] if w == 0 else pool_o.at[index(*block)]

        def copy(k, w, block, to, src=None):
            return pltpu.make_async_remote_copy(
                src_ref=slot(w, block) if src is None else src, dst_ref=slot(w, block),
                send_sem=send_sems.at[2 * k + w], recv_sem=recv_sems.at[2 * k + w],
                device_id=to, device_id_type=MESH)

        def save(block):
            at = pl.ds(3 * index(*block), 3)
            pltpu.make_async_copy(wv.at[at], wt_o.at[at], loc_sems.at[4]).start()

        srcs = (slot(0, me), pb)
        first = []
        for w in (0, 1):
            if w == 1:
                pb[...] = _bf(pool_ref[0])
                wob[...] = _bf(wout_ref[0])
            group = [copy(1 + j, w, me, (*chip, c), src=srcs[w]) for j, chip in enumerate(chips[:2])]
            group.append(copy(0, w, me, sibling, src=srcs[w]))
            for cp in group:
                cp.start()
            first += group
        save(me)
        locs = [pltpu.make_async_copy(pb, slot(1, me), loc_sems.at[0]),
                pltpu.make_async_copy(wob, wout_o.at[my_idx], loc_sems.at[1]),
                pltpu.make_async_copy(wob, woutb_o, loc_sems.at[2])]
        for cp in locs:
            cp.start()

        fetch_x.wait()
        xv = xbuf[...]
        hv[...] = _bf(xv * lax.rsqrt(jnp.mean(xv * xv, axis=-1, keepdims=True) + EPS) * g_ref[...])
        rows = 256
        for r0 in range(0, T, rows):
            htv[:, r0:r0 + rows] = hv[r0:r0 + rows, :].T
        locs.append(pltpu.make_async_copy(htv, ht_o, loc_sems.at[3]))
        locs[-1].start()

        def out_copy(p, j):
            return pltpu.make_async_copy(stage.at[p], proj_o.at[j], out_sems.at[p])

        def project(nth, block):
            base = 3 * index(*block)

            def tile(p, carry):
                if nth > 0:
                    out_copy(p, base + p).wait()
                stage[p] = _dot(hv[...], wv[base + p], 1, 0)
                out_copy(p, base + p).start()
                return carry

            lax.fori_loop(0, 3, tile, 0)

        project(0, me)
        copy(0, 0, sibling, me).wait_recv()
        save(sibling)
        project(1, sibling)
        passed = []
        relay_from = (px ^ (1 - c), py ^ c, c)
        relay_to = (px ^ c, py ^ (1 - c), c)

        def arrived(w, j):
            copy(1 + j, w, (*chips[j], c), me).wait_recv()
            passed.append(copy(4 + j, w, (*chips[j], c), sibling))
            passed[-1].start()

        def relay(w):
            passed.append(copy(3, w, relay_from, relay_to))
            passed[-1].start()

        def handed(nth, j):
            copy(4 + j, 0, (*chips[j], 1 - c), me).wait_recv()
            save((*chips[j], 1 - c))
            project(nth, (*chips[j], 1 - c))

        arrived(0, 0)
        arrived(0, 1)
        relay(0)
        for j in range(2):
            save((*chips[j], c))
            project(2 + j, (*chips[j], c))
        handed(4, 0)
        handed(5, 1)
        arrived(1, 0)
        arrived(1, 1)
        relay(1)
        arrived(0, 2)
        save((*chips[2], c))
        project(6, (*chips[2], c))
        handed(7, 2)
        arrived(1, 2)
        copy(0, 1, sibling, me).wait_recv()
        for j, chip in enumerate(chips):
            copy(4 + j, 1, (*chip, 1 - c), me).wait_recv()
        keep = pltpu.make_async_copy(wv, wt_o, loc_sems.at[4])
        for p in range(3):
            out_copy(p, p).wait()
        for cp in first + passed:
            cp.wait_send()
        keep.wait()
        for cp in locs:
            cp.wait()

    vmem = pl.BlockSpec(memory_space=pltpu.VMEM)
    bf16 = jnp.bfloat16
    return pl.pallas_call(
        body, name="gather_proj",
        out_shape=(pltpu.HBM((D, T), bf16), pltpu.HBM((NTILE, D, TILE), bf16),
                   pltpu.HBM((DMIX // NDEV, D), bf16), pltpu.HBM((NDEV, DMIX // NDEV, D), bf16),
                   pltpu.HBM((NDEV, NGROUP, GROUP // NDEV, GROUP), bf16), pltpu.HBM((NTILE, T, TILE), jnp.float32)),
        in_specs=[ANY] + [vmem] * 4, out_specs=(ANY,) * 6,
        scratch_shapes=[pltpu.VMEM((T, D), jnp.float32),
                        pltpu.VMEM((T, D), bf16), pltpu.VMEM((D, T), bf16), pltpu.VMEM((NTILE, D, TILE), bf16),
                        pltpu.VMEM((DMIX // NDEV, D), bf16), pltpu.VMEM((NGROUP, GROUP // NDEV, GROUP), bf16),
                        pltpu.VMEM((3, T, TILE), jnp.float32),
                        pltpu.SemaphoreType.DMA((14,)), pltpu.SemaphoreType.DMA((14,)),
                        pltpu.SemaphoreType.DMA((6,)), pltpu.SemaphoreType.DMA((3,))],
        compiler_params=_params(vmem_mib=56),
    )(x, g1, w_in, w_out, pool_w)


def _split_start(name, arrays, n_copies, plan):
    k = len(arrays)

    def body(*refs):
        send_sems, recv_sems, token = refs[k], refs[k + 1], refs[-1]
        for i, (src, dst, to) in enumerate(plan(refs[:k])):
            pltpu.make_async_remote_copy(src_ref=src, dst_ref=dst, send_sem=send_sems.at[i],
                                         recv_sem=recv_sems.at[i], device_id=to, device_id_type=MESH).start()
        token[...] = jnp.zeros_like(token)

    out = pl.pallas_call(
        body, name=name,
        out_shape=(pltpu.SemaphoreType.DMA((n_copies,)), pltpu.SemaphoreType.DMA((n_copies,)),
                   *[pltpu.HBM(a.shape, a.dtype) for a in arrays], jax.ShapeDtypeStruct((8, 128), jnp.float32)),
        in_specs=[HBM] * k, out_specs=(SEM, SEM, *[HBM] * k, pl.BlockSpec(memory_space=pltpu.VMEM)),
        input_output_aliases={i: 2 + i for i in range(k)},
        compiler_params=pltpu.CompilerParams(has_side_effects=EFFECT),
    )(*[pltpu.with_memory_space_constraint(a, pltpu.HBM) for a in arrays])
    return out[0], out[1], out[2:2 + k], out[-1]


def _split_wait(name, arrays, send_sems, recv_sems, plan, after, which=None):
    k = len(arrays)

    def body(*refs):
        sends, recvs = refs[k], refs[k + 1]
        for i, (src, dst, to) in enumerate(plan(refs[:k])):
            if which is not None and i not in which:
                continue
            cp = pltpu.make_async_remote_copy(src_ref=src, dst_ref=dst, send_sem=sends.at[i], recv_sem=recvs.at[i],
                                              device_id=to, device_id_type=MESH)
            cp.wait_send()
            cp.wait_recv()

    return pl.pallas_call(
        body, name=name,
        out_shape=tuple(pltpu.HBM(a.shape, a.dtype) for a in arrays),
        in_specs=[HBM] * k + [SEM, SEM, ANY], out_specs=(HBM,) * k,
        input_output_aliases={i: i for i in range(k)},
        compiler_params=pltpu.CompilerParams(has_side_effects=EFFECT),
    )(*arrays, send_sems, recv_sems, after)


def _plan_wout(refs):
    src, land = refs
    x, y, c, me = _place()
    return [(src, land.at[me], _peer(x, y, c, r)) for r in range(1, NDEV)]


def _plan_rest(refs):
    gob, gpf, r_out, r_pool = refs
    x, y, c, me = _place()
    plan = []
    for r in range(1, NDEV):
        plan.append((gob.at[me ^ r], r_out.at[r - 1], _peer(x, y, c, r)))
        plan.append((gpf.at[me ^ r], r_pool.at[r - 1], _peer(x, y, c, r)))
    return plan


def _plan_in(refs):
    sum_b, r_in = refs
    x, y, c, _ = _place()
    plan = []
    for q in range(4):
        rows = pl.ds(q * (D // 4), D // 4)
        for j, (dx, dy) in enumerate(((1, 0), (0, 1), (1, 1))):
            px, py = x ^ dx, y ^ dy
            plan.append((sum_b.at[2 * px + py, :, rows], r_in.at[j, :, rows], (px, py, c)))
    return plan


def _plan_small(refs):
    small, land = refs
    x, y, c, me = _place()
    return [(small, land.at[me], _peer(x, y, c, r)) for r in range(1, NDEV)]


def _seg_tiles(s):
    return (s + 2) % NSEG


_POOL_SPECS = [pl.BlockSpec((None, T, GROUP), lambda g, base=base: (base + g, 0, 0)) for base in (0, 4)]
_HEAD_SPECS = [pl.BlockSpec((None, T, HEAD), lambda h, base=base: (base + h // 2, 0, h % 2))
               for base in (8, 12, 16, 20)]


def _row_ids(shape):
    return lax.broadcasted_iota(jnp.int32, shape, 0)


BAND_ROWS = 128
HALO = 16


def _window_sum(a, gidx, lead):
    width = lax.shift_left(jnp.int32(2), gidx)
    shape = (BAND_ROWS, BAND_ROWS + HALO)
    t, j = lax.broadcasted_iota(jnp.int32, shape, 0), lax.broadcasted_iota(jnp.int32, shape, 1)
    first = t if lead else t + HALO - width + 1
    band = _bf(jnp.where(j >= first, jnp.where(j < first + width, 1.0, 0.0), 0.0))
    zeros = jnp.zeros((HALO, a.shape[1]), jnp.bfloat16)
    padded = [jnp.concatenate([p, zeros] if lead else [zeros, p], axis=0) for p in _split2(a)]
    out = []
    for r0 in range(0, T, BAND_ROWS):
        slab = jnp.concatenate([p[r0:r0 + BAND_ROWS + HALO] for p in padded], axis=1)
        r = _dot(band, slab, 1, 0)
        out.append(r[:, :a.shape[1]] + r[:, a.shape[1]:])
    return jnp.concatenate(out, axis=0)


def _window_mean(s, gidx):
    inv = jnp.where(gidx == 0, 0.5, jnp.where(gidx == 1, 0.25, jnp.where(gidx == 2, 0.125, 0.0625)))
    width = lax.shift_left(jnp.int32(2), gidx)
    head = s[:16] / jnp.minimum(_row_ids((16, s.shape[1])) + 1, width).astype(jnp.float32)
    return jnp.concatenate([head, s[16:] * inv], axis=0)


def _pool_fwd(proj, pool_w, pool_scale, token):
    def body(u_ref, pg_ref, w_ref, sc_ref, token_any, y_ref):
        del token_any
        gidx = pl.program_id(0)
        u, pg = u_ref[...], pg_ref[...]
        d = _window_mean(_window_sum(u, gidx, False), gidx) - u
        mixed = _dot(_bf(d), w_ref[...], 1, 0)
        y_ref[...] = _bf(mixed * sc_ref[...] * (pg * _sigmoid(pg)))

    return pl.pallas_call(
        body, name="pool_fwd", grid=(NGROUP,),
        in_specs=[*_POOL_SPECS,
                  pl.BlockSpec((None, GROUP, GROUP), lambda g: (g, 0, 0)),
                  pl.BlockSpec((1, GROUP), lambda g: (0, g)), ANY],
        out_specs=pl.BlockSpec((T, GROUP), lambda g: (0, g)),
        out_shape=pltpu.HBM((T, DMIX), jnp.bfloat16),
        compiler_params=_params(("parallel",)),
    )(proj, proj, pool_w, pool_scale, token)


def _tri(lower):
    r = lax.broadcasted_iota(jnp.int32, (CHUNK, CHUNK), 0)
    c = lax.broadcasted_iota(jnp.int32, (CHUNK, CHUNK), 1)
    return (r >= c) if lower else (r <= c)


def _sum_rows_matrix():
    shape = (CHUNK + 16, CHUNK)
    r, c = lax.broadcasted_iota(jnp.int32, shape, 0), lax.broadcasted_iota(jnp.int32, shape, 1)
    run = jnp.where(c <= r, 1.0, 0.0)
    half = jnp.where(c < CHUNK // 2, 1.0, 0.0)
    return _bf(jnp.where(r < CHUNK, run, jnp.where(r < CHUNK + 8, 1.0, half)))


def _rev_sum_matrix():
    shape = (CHUNK, 2 * CHUNK)
    r, c = lax.broadcasted_iota(jnp.int32, shape, 0), lax.broadcasted_iota(jnp.int32, shape, 1)
    return _bf(jnp.where(c < CHUNK, jnp.where(c >= r, 1.0, 0.0), jnp.where(c - CHUNK < r, 1.0, 0.0)))


def _split2(a):
    hi = _bf(a)
    return [hi, _bf(a - hi.astype(jnp.float32))]


def _exact_sums(mat, pieces):
    x = jnp.concatenate([s for p in pieces for s in _split2(p)], axis=1)
    r = _dot(mat, x, 1, 0)
    return [r[:, 2 * j * HEAD:(2 * j + 1) * HEAD] + r[:, (2 * j + 1) * HEAD:(2 * j + 2) * HEAD]
            for j in range(len(pieces))]


def _gates(qv, fl, lb):
    sq = _sigmoid(qv)
    sg = _sigmoid(fl)
    f = lb + (1.0 - lb) * sg
    return dict(sq=sq, qs=qv * sq, sg=sg, f=f, kk=1.0 - f, g=jnp.log2(f))


def _decays(sums):
    big_g = sums[:CHUNK]
    total = sums[CHUNK:CHUNK + 8]
    g_last = jnp.tile(total, (CHUNK // 8, 1))
    g_mid = jnp.tile(sums[CHUNK + 8:], (CHUNK // 8, 1))
    return dict(
        e_q=jnp.exp2(big_g),
        e_k=jnp.exp2(g_last - big_g),
        e_qm=jnp.exp2(jnp.minimum(big_g - g_mid, EXP_CAP)),
        e_km=jnp.exp2(jnp.minimum(g_mid - big_g, EXP_CAP)),
        total8=jnp.exp2(total))


def _group_rows(gi):
    return [pl.ds(pl.multiple_of((gi * NB + j) * CHUNK, CHUNK), CHUNK) for j in range(NB)]


def _lower_bound(lb_ref):
    return _sigmoid(lb_ref[0:1, :] - lb_ref[1:2, :])


def _hgrn_fwd(proj, lb_logits, rec_g, y_in):
    def body(q_ref, f_ref, i_ref, gate_ref, lb_ref, rg_ref, y_any, y_ref, o_ref, st_ref):
        del y_any
        lb = _lower_bound(lb_ref)
        causal = _tri(True)
        smat = _sum_rows_matrix()

        def group(gi, st):
            rows = _group_rows(gi)
            ts = [_gates(q_ref[r, :], f_ref[r, :], lb) for r in rows]
            ds = [_decays(s) for s in _exact_sums(smat, [t["g"] for t in ts])]
            vs = [_bf(i_ref[r, :]) for r in rows]
            q_m = [_bf(t["qs"] * d["e_qm"]) for t, d in zip(ts, ds)]
            k_m = [_bf(t["kk"] * d["e_km"]) for t, d in zip(ts, ds)]
            q_e = [_bf(t["qs"] * d["e_q"]) for t, d in zip(ts, ds)]
            k_e = [_bf(t["kk"] * d["e_k"]) for t, d in zip(ts, ds)]
            a = [_bf(jnp.where(causal, _dot(q_m[j], k_m[j], 1, 1), 0.0)) for j in range(NB)]
            intra = [_dot(a[j], vs[j], 1, 0) for j in range(NB)]
            upd = [_dot(vs[j], k_e[j], 0, 0) for j in range(NB)]
            for j in range(NB):
                st_ref[gi * NB + j] = st
                o_ref[rows[j], :] = intra[j] + _dot(q_e[j], _bf(st), 1, 1)
                st = st * jnp.tile(ds[j]["total8"], (HEAD // 8, 1)) + upd[j]
            return st

        lax.fori_loop(0, NGRP, group, jnp.zeros((HEAD, HEAD), jnp.float32))
        o = o_ref[...]
        rn = o * lax.rsqrt(jnp.mean(o * o, axis=-1, keepdims=True) + EPS)
        gate = gate_ref[...]
        y_ref[...] = _bf(rn * rg_ref[...] * (gate * _sigmoid(gate)))

    return pl.pallas_call(
        body, name="hgrn_fwd", grid=(NHEAD,),
        in_specs=[*_HEAD_SPECS,
                  pl.BlockSpec((2, HEAD), lambda h: (0, h)),
                  pl.BlockSpec((1, HEAD), lambda h: (0, h)),
                  pl.BlockSpec(memory_space=pl.ANY)],
        out_specs=(pl.BlockSpec((T, HEAD), lambda h: (0, NHEAD + h)),
                   pl.BlockSpec((T, HEAD), lambda h: (0, h)),
                   pl.BlockSpec((None, NCHUNK, HEAD, HEAD), lambda h: (h, 0, 0, 0))),
        out_shape=(pltpu.HBM((T, DMIX), jnp.bfloat16), pltpu.HBM((T, D), jnp.float32),
                   pltpu.HBM((NHEAD, NCHUNK, HEAD, HEAD), jnp.float32)),
        input_output_aliases={6: 0},
        compiler_params=_params(("parallel",)),
    )(proj, proj, proj, proj, lb_logits, rec_g, y_in)


def _out_proj_loss(x, y, w_out, target, gf):
    rows = 512
    parts = [slice(k * rows // 2, (k + 1) * rows // 2) for k in range(2)]

    def body(x_ref, y_ref, w_ref, t_ref, g_ref, dz_ref, dzb_ref, sq_ref, dg_ref):
        zs = [x_ref[p, :] + _dot(y_ref[p, :], w_ref[...], 1, 0) for p in parts]
        sq = dg = 0.0
        for p, z in zip(parts, zs):
            r = lax.rsqrt(jnp.mean(z * z, axis=-1, keepdims=True) + EPS)
            zhat = z * r
            err = zhat * g_ref[...] - t_ref[p, :]
            dy = err * (1.0 / D)
            gdy = dy * g_ref[...]
            dz = r * (gdy - zhat * jnp.mean(zhat * gdy, axis=-1, keepdims=True))
            dz_ref[p, :] = dz
            dzb_ref[p, :] = _bf(dz)
            sq = sq + jnp.sum(err * err, axis=0, keepdims=True)
            dg = dg + jnp.sum(zhat * dy, axis=0, keepdims=True)

        @pl.when(pl.program_id(0) == 0)
        def _():
            sq_ref[...] = sq
            dg_ref[...] = dg

        @pl.when(pl.program_id(0) != 0)
        def _():
            sq_ref[...] += sq
            dg_ref[...] += dg

    tile = pl.BlockSpec((rows, D), lambda i: (i, 0))
    vec = pl.BlockSpec((1, D), lambda i: (0, 0))
    return pl.pallas_call(
        body, name="out_proj_loss", grid=(T // rows,),
        in_specs=[tile, pl.BlockSpec((rows, DMIX), lambda i: (i, 0)), pl.BlockSpec((DMIX, D), lambda i: (0, 0)),
                  tile, vec],
        out_specs=(tile, tile, vec, vec),
        out_shape=(pltpu.HBM((T, D), jnp.float32), pltpu.HBM((T, D), jnp.bfloat16),
                   jax.ShapeDtypeStruct((1, D), jnp.float32), jax.ShapeDtypeStruct((1, D), jnp.float32)),
        compiler_params=_params(("arbitrary",)),
    )(x, y, w_out, target, gf)


def _out_proj_bwd(dzb, w_out, y):
    tn = 512

    def body(dz_ref, w_ref, y_ref, dy_ref, gw_ref, gwb_ref):
        dz = dz_ref[...]
        dy_ref[...] = _dot(dz, w_ref[...], 1, 1)
        gw = _dot(y_ref[...], dz, 0, 0)
        gw_ref[...] = gw
        gwb_ref[...] = _bf(gw)

    return pl.pallas_call(
        body, name="out_proj_bwd", grid=(DMIX // tn,),
        in_specs=[pl.BlockSpec((T, D), lambda n: (0, 0)), pl.BlockSpec((tn, D), lambda n: (n, 0)),
                  pl.BlockSpec((T, tn), lambda n: (0, n))],
        out_specs=(pl.BlockSpec((T, tn), lambda n: (0, n)), pl.BlockSpec((tn, D), lambda n: (n, 0)),
                   pl.BlockSpec((tn, D), lambda n: (n, 0))),
        out_shape=(pltpu.HBM((T, DMIX), jnp.float32), pltpu.HBM((DMIX, D), jnp.float32),
                   pltpu.HBM((DMIX, D), jnp.bfloat16)),
        compiler_params=_params(("parallel",)),
    )(dzb, w_out, y)


def _hgrn_bwd(proj, lb_logits, rec_g, o, states, dymix, dproj_in, token):
    def body(q_ref, f_ref, i_ref, gate_ref, lb_ref, rg_ref, o_ref, st_ref, dy_ref, dp_any, token_any,
             dp_ref, drg_ref, dlb_ref, do_ref):
        del dp_any, token_any
        lb = _lower_bound(lb_ref)
        causal = _tri(True)
        smat, rmat = _sum_rows_matrix(), _rev_sum_matrix()

        o = o_ref[...]
        rs = lax.rsqrt(jnp.mean(o * o, axis=-1, keepdims=True) + EPS)
        rn = o * rs
        gate = gate_ref[...]
        sgate = _sigmoid(gate)
        dyv = dy_ref[...]
        d_r = dyv * (gate * sgate)
        dp_ref[3] = _bf(dyv * (rn * rg_ref[...]) * (sgate * (1.0 + gate * (1.0 - sgate))))
        drg_ref[...] = jnp.sum(d_r * rn, axis=0, keepdims=True)
        drn = d_r * rg_ref[...]
        do_ref[...] = rs * (drn - rn * jnp.mean(rn * drn, axis=-1, keepdims=True))

        def group(i, carry):
            dst, dlb = carry
            gi = NGRP - 1 - i
            rows = _group_rows(gi)
            span = range(NB)
            qvs = [q_ref[r, :] for r in rows]
            ts = [_gates(qv, f_ref[r, :], lb) for qv, r in zip(qvs, rows)]
            ds = [_decays(s) for s in _exact_sums(smat, [t["g"] for t in ts])]
            vs = [_bf(i_ref[r, :]) for r in rows]
            dos = [_bf(do_ref[r, :]) for r in rows]
            sts = [st_ref[gi * NB + j] for j in span]
            qe_f = [t["qs"] * d["e_q"] for t, d in zip(ts, ds)]
            ke_f = [t["kk"] * d["e_k"] for t, d in zip(ts, ds)]
            q_e, k_e = [_bf(a) for a in qe_f], [_bf(a) for a in ke_f]
            q_m = [_bf(t["qs"] * d["e_qm"]) for t, d in zip(ts, ds)]
            k_m = [_bf(t["kk"] * d["e_km"]) for t, d in zip(ts, ds)]
            a = [_bf(jnp.where(causal, _dot(q_m[j], k_m[j], 1, 1), 0.0)) for j in span]
            da = [_bf(jnp.where(causal, _dot(dos[j], vs[j], 1, 1), 0.0)) for j in span]
            dqm = [_dot(da[j], k_m[j], 1, 0) for j in span]
            dkm = [_dot(da[j], q_m[j], 0, 0) for j in span]
            dv_in = [_dot(a[j], dos[j], 0, 0) for j in span]
            dqe = [_dot(dos[j], _bf(sts[j]), 1, 0) for j in span]
            grow = [_dot(dos[j], q_e[j], 0, 0) for j in span]
            dke, carried = [None] * NB, [None] * NB
            for j in reversed(span):
                dst_b = _bf(dst)
                dke[j] = _dot(vs[j], dst_b, 1, 0)
                dp_ref[2, rows[j], :] = _bf(dv_in[j] + _dot(k_e[j], dst_b, 1, 1))
                carried[j] = ds[j]["total8"] * jnp.sum(dst * sts[j], axis=0, keepdims=True)
                dst = dst * jnp.tile(ds[j]["total8"], (HEAD // 8, 1)) + grow[j]
            kdk = [ke_f[j] * dke[j] for j in span]
            pos = [(q_m[j].astype(jnp.float32) * dqm[j] - k_m[j].astype(jnp.float32) * dkm[j]) + qe_f[j] * dqe[j]
                   for j in span]
            dgs = _exact_sums(rmat, [jnp.concatenate([pos[j], kdk[j]], axis=0) for j in span])
            for j in span:
                t, d = ts[j], ds[j]
                dg = dgs[j] + jnp.tile(carried[j], (CHUNK // 8, 1))
                dqs = dqm[j] * d["e_qm"] + dqe[j] * d["e_q"]
                dkk = dkm[j] * d["e_km"] + dke[j] * d["e_k"]
                df = dg / t["f"] - dkk
                dp_ref[1, rows[j], :] = _bf(df * (1.0 - lb) * (t["sg"] * (1.0 - t["sg"])))
                dp_ref[0, rows[j], :] = _bf(dqs * (t["sq"] * (1.0 + qvs[j] * (1.0 - t["sq"]))))
                dlb = dlb + df * (1.0 - t["sg"])
            return dst, dlb

        _, dlb = lax.fori_loop(0, NGRP, group, (jnp.zeros((HEAD, HEAD), jnp.float32),
                                                jnp.zeros((CHUNK, HEAD), jnp.float32)))
        dlb_ref[...] = jnp.sum(dlb, axis=0, keepdims=True)

    vec = pl.BlockSpec((1, HEAD), lambda h: (0, h))
    return pl.pallas_call(
        body, name="hgrn_bwd", grid=(NHEAD,),
        in_specs=[*_HEAD_SPECS,
                  pl.BlockSpec((2, HEAD), lambda h: (0, h)), vec,
                  pl.BlockSpec((T, HEAD), lambda h: (0, h)),
                  pl.BlockSpec((None, NCHUNK, HEAD, HEAD), lambda h: (h, 0, 0, 0)),
                  pl.BlockSpec((T, HEAD), lambda h: (0, NHEAD + h)), ANY, ANY],
        out_specs=(pl.BlockSpec((4, T, HEAD), lambda h: (0, 0, h)), vec, vec),
        out_shape=(pltpu.HBM((NSEG, T, D), jnp.bfloat16),
                   jax.ShapeDtypeStruct((1, D), jnp.float32), jax.ShapeDtypeStruct((1, D), jnp.float32)),
        scratch_shapes=[pltpu.VMEM((T, HEAD), jnp.float32)],
        input_output_aliases={9: 0},
        compiler_params=_params(("parallel",)),
    )(proj, proj, proj, proj, lb_logits, rec_g, o, states, dymix, dproj_in, token)


def _pool_bwd(proj, pool_w, pool_scale, dymix):
    def body(u_ref, pg_ref, w_ref, sc_ref, dy_ref, dp_ref, gw_ref, gs_ref):
        gidx = pl.program_id(0)
        u, pg = u_ref[...], pg_ref[...]
        d = _bf(_window_mean(_window_sum(u, gidx, False), gidx) - u)
        mixed = _dot(d, w_ref[...], 1, 0)
        spg = _sigmoid(pg)
        dyv = dy_ref[...]
        d_p = dyv * (pg * spg)
        dp_ref[1] = _bf(dyv * (mixed * sc_ref[...]) * (spg * (1.0 + pg * (1.0 - spg))))
        gs_ref[...] = jnp.sum(d_p * mixed, axis=0, keepdims=True)
        dmixed = _bf(d_p * sc_ref[...])
        gw_ref[...] = _dot(d, dmixed, 0, 0)
        dd = _dot(dmixed, w_ref[...], 1, 1)
        dp_ref[0] = _bf(_window_sum(_window_mean(dd, gidx), gidx, True) - dd)

    return pl.pallas_call(
        body, name="pool_bwd", grid=(NGROUP,),
        in_specs=[*_POOL_SPECS,
                  pl.BlockSpec((None, GROUP, GROUP), lambda g: (g, 0, 0)),
                  pl.BlockSpec((1, GROUP), lambda g: (0, g)),
                  pl.BlockSpec((T, GROUP), lambda g: (0, g))],
        out_specs=(pl.BlockSpec((2, T, GROUP), lambda g: (2, 0, g)),
                   pl.BlockSpec((None, GROUP, GROUP), lambda g: (g, 0, 0)),
                   pl.BlockSpec((1, GROUP), lambda g: (0, g))),
        out_shape=(pltpu.HBM((NSEG, T, D), jnp.bfloat16),
                   jax.ShapeDtypeStruct((NGROUP, GROUP, GROUP), jnp.float32),
                   jax.ShapeDtypeStruct((1, D), jnp.float32)),
        compiler_params=_params(("parallel",)),
    )(proj, proj, pool_w, pool_scale, dymix)


HALF = NTILE // 2
AWAY = HALF - 3


def _dproj_tile(chip, side, p):
    j = 6 * chip + 3 * side + p
    return ((j // 4 + 4) % NSEG, 0, j % 4)


def _sibling_copy(sib_out, sib_in, send_sems, recv_sems, slot):
    x, y, c, _ = _place()
    return pltpu.make_async_remote_copy(
        src_ref=sib_out.at[slot], dst_ref=sib_in.at[slot], send_sem=send_sems.at[slot],
        recv_sem=recv_sems.at[slot], device_id=(x, y, 1 - c), device_id_type=MESH)


def _proj_bwd_w_far(place, ht, dproj):
    def body(place_ref, h_ref, dp_ref, sib_out, sib_in, send_sems, recv_sems, stage, loc_sems):
        del place_ref
        i = pl.program_id(0)

        def to_hbm(k):
            return pltpu.make_async_copy(stage.at[k], sib_out.at[k], loc_sems.at[k])

        def send(k):
            to_hbm(k).wait()
            _sibling_copy(sib_out, sib_in, send_sems, recv_sems, k).start()

        stage[i] = _bf(_dot(h_ref[...], dp_ref[...], 1, 0))

        @pl.when(i > 0)
        def _():
            send(i - 1)

        to_hbm(i).start()

        @pl.when(i == HALF - 1)
        def _():
            send(i)

    buf = pltpu.HBM((HALF, D, TILE), jnp.bfloat16)
    sems = pltpu.SemaphoreType.DMA((HALF,))
    return pl.pallas_call(
        body, name="proj_bwd_w_far",
        grid_spec=pltpu.PrefetchScalarGridSpec(
            num_scalar_prefetch=1, grid=(HALF,),
            in_specs=[pl.BlockSpec((D, T), lambda i, pr: (0, 0)),
                      pl.BlockSpec((None, T, TILE), lambda i, pr: _dproj_tile(i // 3, 1 - pr[2], i % 3))],
            out_specs=(HBM, HBM, SEM, SEM),
            scratch_shapes=[pltpu.VMEM((HALF, D, TILE), jnp.bfloat16), pltpu.SemaphoreType.DMA((HALF,))]),
        out_shape=(buf, buf, sems, sems),
        compiler_params=pltpu.CompilerParams(dimension_semantics=("arbitrary",), vmem_limit_bytes=48 * MIB,
                                             has_side_effects=EFFECT),
    )(place, ht, dproj)


def _proj_bwd_w_near(place, ht, dproj, sib_out, sib_in, sib_send, sib_recv):
    def owner_chip(k, pr):
        return (pr[1] + 1 + k // 3) % 4

    def body(place_ref, h_ref, dp_ref, sib_out, sib_in, sib_send, sib_recv, sums, own_ref, landing, out_send,
             out_recv, recvbuf, outbuf, in_sems, loc_sems):
        i = pl.program_id(0)
        px, py, c, _ = _place()

        def slot_of(k):
            return 3 * owner_chip(k, place_ref) + k % 3

        def load(k):
            return pltpu.make_async_copy(sib_in.at[slot_of(k)], recvbuf.at[k % 2], in_sems.at[k % 2])

        def fetch(k):
            _sibling_copy(sib_out, sib_in, sib_send, sib_recv, slot_of(k)).wait_recv()
            load(k).start()

        def route(k):
            chip = owner_chip(k, place_ref)
            cx, cy = chip // 2, chip % 2
            return cx, cy, (cx ^ px) + 2 * (cy ^ py) - 1, k % 3

        def to_hbm(k):
            _, _, rel, p = route(k)
            return pltpu.make_async_copy(outbuf.at[k], sums.at[rel, p], loc_sems.at[k])

        def to_owner(k):
            cx, cy, rel, p = route(k)
            return pltpu.make_async_remote_copy(
                src_ref=sums.at[rel, p], dst_ref=landing.at[rel, p], send_sem=out_send.at[3 * rel + p],
                recv_sem=out_recv.at[3 * rel + p], device_id=(cx, cy, c), device_id_type=MESH)

        @pl.when(i == 0)
        def _():
            fetch(i)

        @pl.when(i < HALF - 1)
        def _():
            fetch(i + 1)

        gw = _dot(h_ref[...], dp_ref[...], 1, 0)

        @pl.when(jnp.logical_and(i > 0, i <= AWAY))
        def _():
            to_hbm(i - 1).wait()
            to_owner(i - 1).start()

        load(i).wait()
        total = gw + recvbuf[i % 2].astype(jnp.float32)
        own_ref[...] = total

        @pl.when(i < AWAY)
        def _():
            outbuf[i] = _bf(total)
            to_hbm(i).start()

        @pl.when(i == HALF - 1)
        def _():
            for slot in range(HALF):
                _sibling_copy(sib_out, sib_in, sib_send, sib_recv, slot).wait_send()

    travelling = pltpu.HBM((3, 3, D, TILE), jnp.bfloat16)
    sems = pltpu.SemaphoreType.DMA((AWAY,))
    return pl.pallas_call(
        body, name="proj_bwd_w_near",
        grid_spec=pltpu.PrefetchScalarGridSpec(
            num_scalar_prefetch=1, grid=(HALF,),
            in_specs=[pl.BlockSpec((D, T), lambda i, pr: (0, 0)),
                      pl.BlockSpec((None, T, TILE), lambda i, pr: _dproj_tile(owner_chip(i, pr), pr[2], i % 3)),
                      HBM, HBM, SEM, SEM],
            out_specs=(HBM, pl.BlockSpec((None, D, TILE), lambda i, pr: (jnp.where(i < AWAY, 0, i % 3), 0, 0)),
                       HBM, SEM, SEM),
            scratch_shapes=[pltpu.VMEM((2, D, TILE), jnp.bfloat16), pltpu.VMEM((AWAY, D, TILE), jnp.bfloat16),
                            pltpu.SemaphoreType.DMA((2,)), pltpu.SemaphoreType.DMA((AWAY,))]),
        out_shape=(travelling, pltpu.HBM((3, D, TILE), jnp.float32), travelling, sems, sems),
        compiler_params=pltpu.CompilerParams(dimension_semantics=("arbitrary",), vmem_limit_bytes=48 * MIB,
                                             has_side_effects=EFFECT),
    )(place, ht, dproj, sib_out, sib_in, sib_send, sib_recv)


def _proj_bwd_w(place, ht, dproj):
    half = NTILE // 2

    def owner_chip(i, pr):
        return jnp.where(i < half, i // 3, (pr[1] + 1 + (i - half) // 3) % 4)

    def tile_of(i, pr):
        side = jnp.where(i < half, 1 - pr[2], pr[2])
        return 6 * owner_chip(i, pr) + 3 * side + i % 3

    def dproj_block(i, pr):
        j = tile_of(i, pr)
        return ((j // 4 + 4) % NSEG, 0, j % 4)

    def mine(i):
        return jnp.maximum(i, half)

    def body(place_ref, h_ref, dp_ref, sum_ref, own_ref, sendbuf, recvbuf, send_sems, recv_sems):
        i = pl.program_id(0)
        px, py, c, _ = _place()
        gw = _dot(h_ref[...], dp_ref[...], 1, 0)

        def to_sibling(slot):
            return pltpu.make_async_remote_copy(
                src_ref=sendbuf.at[slot], dst_ref=recvbuf.at[slot], send_sem=send_sems.at[slot],
                recv_sem=recv_sems.at[slot], device_id=(px, py, 1 - c), device_id_type=MESH)

        @pl.when(i < half)
        def _():
            sendbuf[i] = _bf(gw)
            to_sibling(i).start()

        @pl.when(i >= half)
        def _():
            slot = 3 * owner_chip(i, place_ref) + i % 3
            to_sibling(slot).wait_recv()
            total = gw + recvbuf[slot].astype(jnp.float32)
            sum_ref[...] = _bf(total)
            own_ref[...] = total

        @pl.when(i == NTILE - 1)
        def _():
            for slot in range(half):
                to_sibling(slot).wait_send()

    return pl.pallas_call(
        body, name="proj_bwd_w",
        grid_spec=pltpu.PrefetchScalarGridSpec(
            num_scalar_prefetch=1, grid=(NTILE,),
            in_specs=[pl.BlockSpec((D, T), lambda i, pr: (0, 0)),
                      pl.BlockSpec((None, T, TILE), lambda i, pr: dproj_block(i, pr))],
            out_specs=(pl.BlockSpec((None, None, D, TILE), lambda i, pr: (owner_chip(mine(i), pr), mine(i) % 3, 0, 0)),
                       pl.BlockSpec((None, D, TILE), lambda i, pr: (jnp.where(i < NTILE - 3, 0, i % 3), 0, 0))),
            scratch_shapes=[pltpu.VMEM((half, D, TILE), jnp.bfloat16), pltpu.VMEM((half, D, TILE), jnp.bfloat16),
                            pltpu.SemaphoreType.DMA((half,)), pltpu.SemaphoreType.DMA((half,))]),
        out_shape=(pltpu.HBM((4, 3, D, TILE), jnp.bfloat16), pltpu.HBM((3, D, TILE), jnp.float32)),
        compiler_params=_params(("arbitrary",)),
    )(place, ht, dproj)


def _proj_bwd_x(dproj, w_t, x, g1, dz, token):
    tm = 512
    pairs = NSEG // 2

    def body(dp_ref, w_ref, x_ref, g_ref, dz_ref, token_any, dx_ref, dg_ref, wcat, acc):
        del token_any
        s, m = pl.program_id(0), pl.program_id(1)

        @pl.when(m == 0)
        def _():
            for i in range(8):
                wcat[:, i * TILE:(i + 1) * TILE] = w_ref[i]

        mine = pl.ds(pl.multiple_of(m * tm, tm), tm)

        @pl.when(s == 0)
        def _():
            acc[mine, :] = jnp.zeros((tm, D), jnp.float32)

        acc[mine, :] += _dot(jnp.concatenate([dp_ref[0], dp_ref[1]], axis=1), wcat[...], 1, 1)

        @pl.when(s == pairs - 1)
        def _():
            xv = x_ref[...]
            rs = lax.rsqrt(jnp.mean(xv * xv, axis=-1, keepdims=True) + EPS)
            xhat = xv * rs
            dhv = acc[mine, :]
            gdh = dhv * g_ref[...]
            dx_ref[...] = dz_ref[...] + rs * (gdh - xhat * jnp.mean(xhat * gdh, axis=-1, keepdims=True))
            dg = jnp.sum(xhat * dhv, axis=0, keepdims=True)

            @pl.when(m == 0)
            def _():
                dg_ref[...] = dg

            @pl.when(m != 0)
            def _():
                dg_ref[...] += dg

    rows = pl.BlockSpec((tm, D), lambda s, m: (jnp.where(s == pairs - 1, m, 0), 0))
    vec = pl.BlockSpec((1, D), lambda s, m: (0, 0))
    return pl.pallas_call(
        body, name="proj_bwd_x", grid=(pairs, T // tm),
        in_specs=[pl.BlockSpec((2, tm, D), lambda s, m: (s, m, 0)),
                  pl.BlockSpec((8, D, TILE), lambda s, m: ((s + 1) % pairs, 0, 0)), rows, vec, rows, ANY],
        out_specs=(rows, vec),
        out_shape=(jax.ShapeDtypeStruct((T, D), jnp.float32), jax.ShapeDtypeStruct((1, D), jnp.float32)),
        scratch_shapes=[pltpu.VMEM((D, 2 * D), jnp.bfloat16), pltpu.VMEM((T, D), jnp.float32)],
        compiler_params=_params(("arbitrary", "arbitrary"), vmem_mib=56),
    )(dproj, w_t, x, g1, dz, token)


def _adamw(w, g, m, v):
    m_new = ADAM_B1 * m + (1.0 - ADAM_B1) * g
    v_new = ADAM_B2 * v + (1.0 - ADAM_B2) * (g * g)
    delta = -ADAM_LR * ((m_new / BC1) / (jnp.sqrt(v_new / BC2) + ADAM_EPS) + ADAM_WD * w)
    return delta, m_new, v_new


def _reduce_adam(name, place, parts, w, m, v, grid, w_spec, into=()):
    n = len(parts)

    def body(place_ref, *refs):
        del place_ref
        w_ref, m_ref, v_ref = refs[n:n + 3]
        g_ref, d_ref, mo_ref, vo_ref = refs[-4:]
        g = None
        for ref, (_, _, stacked) in zip(refs[:n], parts):
            terms = [ref[r] for r in range(ref.shape[0])] if stacked else [ref[...]]
            for t in terms:
                if t.shape[-1] != w_ref.shape[-1]:
                    t = jnp.concatenate([t[p] for p in range(t.shape[0])], axis=1)
                g = t.astype(jnp.float32) if g is None else g + t.astype(jnp.float32)
        delta, m_new, v_new = _adamw(w_ref[...], g, m_ref[...], v_ref[...])
        g_ref[...] = g
        d_ref[...] = delta
        mo_ref[...] = m_new
        vo_ref[...] = v_new

    shape = jax.ShapeDtypeStruct(w.shape, jnp.float32)
    return pl.pallas_call(
        body, name=name,
        grid_spec=pltpu.PrefetchScalarGridSpec(
            num_scalar_prefetch=1, grid=grid,
            in_specs=[spec for _, spec, _ in parts] + [w_spec] * 3 + [ANY] * len(into), out_specs=(w_spec,) * 4),
        out_shape=(shape,) * 4,
        input_output_aliases={1 + n + 3 + j: j for j in range(len(into))},
        compiler_params=_params(("parallel",)),
    )(place, *[_in_hbm(a) for a in [a for a, _, _ in parts] + [w, m, v, *into]])


def _small_adam(place, own, parts, w, m, v):
    def body(place_ref, own_ref, p_ref, w_ref, m_ref, v_ref, g_ref, d_ref, mo_ref, vo_ref):
        me = place_ref[0]
        g = None
        for s in range(NDEV):
            term = jnp.where(me == s, own_ref[...], p_ref[s])
            g = term if g is None else g + term
        wv = w_ref[...]
        rows = _row_ids(wv.shape)
        other = jnp.where(rows == 2, pltpu.roll(wv, 7, 0), jnp.where(rows == 3, pltpu.roll(wv, 1, 0), 0.0))
        lbv = _sigmoid(wv - other)
        sign = jnp.where(rows == 2, 1.0, -1.0)
        g = jnp.where((rows == 2) | (rows == 3), sign * g * lbv * (1.0 - lbv), g)
        delta, m_new, v_new = _adamw(wv, g, m_ref[...], v_ref[...])
        g_ref[...] = g
        d_ref[...] = delta
        mo_ref[...] = m_new
        vo_ref[...] = v_new

    shape = jax.ShapeDtypeStruct((8, D), jnp.float32)
    vmem = pl.BlockSpec(memory_space=pltpu.VMEM)
    return pl.pallas_call(
        body, name="small_adam", out_shape=(shape,) * 4,
        in_specs=[pl.BlockSpec(memory_space=pltpu.SMEM)] + [vmem] * 5, out_specs=(vmem,) * 4,
    )(place, own, parts, w, m, v)


def _rows8(*vecs):
    rows = [a.reshape(-1, D) for a in vecs]
    n = sum(r.shape[0] for r in rows)
    return jnp.concatenate(rows + [jnp.zeros((8 - n, D), jnp.float32)], axis=0)


def kernel(x, norm1_g, w_in, pool_w, pool_scale, lb_logits, rec_norm_g, w_out, final_norm_g, loss_target, m_norm1_g, m_w_in, m_pool_w, m_pool_scale, m_lb_logits, m_rec_norm_g, m_w_out, m_final_norm_g, v_norm1_g, v_w_in, v_pool_w, v_pool_scale, v_lb_logits, v_rec_norm_g, v_w_out, v_final_norm_g):
    xs = x[0]
    target = loss_target[0]
    ix, iy, ic = lax.axis_index("x"), lax.axis_index("y"), lax.axis_index("c")
    place = jnp.stack([4 * ix + 2 * iy + ic, 2 * ix + iy, ic]).astype(jnp.int32)
    gf = final_norm_g.reshape(1, D)

    ht, w_t, w_out_b, w_out_g, pool_g, proj = _gather_proj(xs, norm1_g, w_in, w_out, pool_w)
    pool_full = pool_g.transpose(1, 0, 2, 3).reshape(NGROUP, GROUP, GROUP)
    wout = [w_out_b, w_out_g]
    wout_send, wout_recv, wout, wout_token = _split_start("gather_wout_start", wout, NDEV - 1, _plan_wout)

    y = _pool_fwd(proj, pool_full, pool_scale, wout_token)
    y, o, states = _hgrn_fwd(proj, lb_logits, rec_norm_g, y)
    _, w_out_g = _split_wait("gather_wout_wait", wout, wout_send, wout_recv, _plan_wout, o)
    w_out_full = _in_hbm(w_out_g.reshape(DMIX, D))
    dz, dzb, sq, dgf = _out_proj_loss(xs, y, w_out_full, target, gf)

    dymix, gwout_f, gwout_b = _out_proj_bwd(dzb, w_out_full, y)
    dproj, gpool, dscale = _pool_bwd(proj, pool_full, pool_scale, dymix)

    blk_out = (NDEV, DMIX // NDEV, D)
    blk_pool = (NDEV, NGROUP, GROUP // NDEV, GROUP)
    gpool_s = gpool.reshape(NGROUP, NDEV, GROUP // NDEV, GROUP).transpose(1, 0, 2, 3)
    rest = [gwout_b.reshape(blk_out), gpool_s,
            lax.empty((NDEV - 1,) + blk_out[1:], jnp.bfloat16), lax.empty((NDEV - 1,) + blk_pool[1:], jnp.float32)]
    rest_send, rest_recv, rest, rest_token = _split_start("scatter_rest_start", rest, 2 * (NDEV - 1), _plan_rest)

    dproj, drecg, dlb = _hgrn_bwd(proj, lb_logits, rec_norm_g, o, states, dymix, dproj, rest_token)
    chip_sums, own_sum = _proj_bwd_w(place, ht, dproj)
    win = [chip_sums, lax.empty((3, 3, D, TILE), jnp.bfloat16)]
    win_send, win_recv, win, win_token = _split_start("scatter_win_start", win, 12, _plan_in)

    grad_x, dg1 = _proj_bwd_x(dproj, w_t, xs, norm1_g, dz, win_token)

    small = [_rows8(dg1, dscale, dlb, dlb, drecg, dgf, sq), lax.empty((NDEV, 8, D), jnp.float32)]
    small_send, small_recv, small, small_token = _split_start("gather_small_start", small, NDEV - 1, _plan_small)

    _, gpool_own, r_out, r_pool = _split_wait("scatter_rest_wait", rest, rest_send, rest_recv, _plan_rest,
                                              small_token)
    g_wout, d_wout, m_wout, v_wout = _reduce_adam(
        "adam_w_out", place,
        [(gwout_f.reshape(blk_out), pl.BlockSpec((None,) + blk_out[1:], lambda i, pr: (pr[0], 0, 0)), False),
         (r_out, pl.BlockSpec((NDEV - 1,) + blk_out[1:], lambda i, pr: (0, 0, 0)), True)],
        w_out, m_w_out, v_w_out, (1,), pl.BlockSpec((None,) + blk_out[1:], lambda i, pr: (0, 0, 0)))
    g_pool, d_pool, m_pool, v_pool = _reduce_adam(
        "adam_pool_w", place,
        [(gpool_own, pl.BlockSpec((None,) + blk_pool[1:], lambda i, pr: (pr[0], 0, 0, 0)), False),
         (r_pool, pl.BlockSpec((NDEV - 1,) + blk_pool[1:], lambda i, pr: (0, 0, 0, 0)), True)],
        pool_w, m_pool_w, v_pool_w, (1,), pl.BlockSpec((None,) + blk_pool[1:], lambda i, pr: (0, 0, 0, 0)))

    def adam_w_in(name, r_in, first, count, into=()):
        return _reduce_adam(
            name, place,
            [(own_sum, pl.BlockSpec((3, D // 8, TILE), lambda i, pr: (0, first + i, 0)), False),
             (r_in, pl.BlockSpec((3, 3, D // 8, TILE), lambda i, pr: (0, 0, first + i, 0)), True)],
            w_in, m_w_in, v_w_in, (count,), pl.BlockSpec((None, D // 8, 3 * TILE), lambda i, pr: (0, first + i, 0)),
            into)

    win = _split_wait("scatter_win_wait_early", win, win_send, win_recv, _plan_in, d_pool, range(9))
    early = adam_w_in("adam_w_in_early", win[1], 0, 6)

    own_small, r_small = _split_wait("gather_small_wait", small, small_send, small_recv, _plan_small, early[1])
    g_s, d_s, m_s, v_s = _small_adam(
        place, own_small, r_small,
        _rows8(norm1_g, pool_scale, lb_logits, rec_norm_g, final_norm_g),
        _rows8(m_norm1_g, m_pool_scale, m_lb_logits, m_rec_norm_g, m_final_norm_g),
        _rows8(v_norm1_g, v_pool_scale, v_lb_logits, v_rec_norm_g, v_final_norm_g))
    loss = jnp.sum(g_s[6]) * (0.5 / D)

    _, r_in = _split_wait("scatter_win_wait", win, win_send, win_recv, _plan_in, v_s, range(9, 12))
    g_win, d_win, m_win, v_win = adam_w_in("adam_w_in", r_in, 6, 2, early)

    def small_outs(a):
        return a[0:1], a[1:2], a[2:4], a[4:5], a[5]

    def outs(small_a, win, pool, wout):
        n1, ps, lbl, rg, fg = small_outs(small_a)
        return n1, win, pool, ps, lbl, rg, wout, fg

    return (loss, grad_x[None],
            *outs(g_s, g_win, g_pool, g_wout), *outs(d_s, d_win, d_pool, d_wout),
            *outs(m_s, m_win, m_pool, m_wout), *outs(v_s, v_win, v_pool, v_wout))
```

```python
import functools

import jax
import jax.numpy as jnp
from jax import lax
from jax.experimental import pallas as pl
from jax.experimental.pallas import tpu as pltpu

T = 2048
D = 1024
NSEG = 6
NTILE = 24
TILE = 256
DMIX = 2048
NDEV = 8
HEAD = 128
NHEAD = 8
CHUNK = 64
NCHUNK = T // CHUNK
NB = 32
NGRP = NCHUNK // NB
NGROUP = 4
GROUP = 256
EPS = 1e-6
EXP_CAP = 115.0
MESH = pl.DeviceIdType.MESH
AXES = ("x", "y", "c")
ANY = pl.BlockSpec(memory_space=pl.ANY)
HBM = pl.BlockSpec(memory_space=pltpu.HBM)
SEM = pl.BlockSpec(memory_space=pltpu.SEMAPHORE)
EFFECT = pltpu.SideEffectType.DATAFLOW_SIDE_EFFECTING

ADAM_LR = 0.001
ADAM_B1 = 0.9
ADAM_B2 = 0.999
ADAM_EPS = 1e-08
ADAM_WD = 0.01
ADAM_STEP = 10
BC1 = 1.0 - ADAM_B1 ** ADAM_STEP
BC2 = 1.0 - ADAM_B2 ** ADAM_STEP

MIB = 1 << 20


def _params(sem=None, vmem_mib=48):
    return pltpu.CompilerParams(dimension_semantics=sem, vmem_limit_bytes=vmem_mib * MIB)


def _sigmoid(v):
    return 1.0 / (1.0 + jnp.exp(-v))


def _dot(a, b, ca, cb, precision=None):
    return lax.dot_general(a, b, (((ca,), (cb,)), ((), ())), precision=precision,
                           preferred_element_type=jnp.float32)


def _bf(v):
    return v.astype(jnp.bfloat16)


def _in_hbm(a):
    return pltpu.with_memory_space_constraint(a, pltpu.HBM)


def _place():
    x, y, c = lax.axis_index("x"), lax.axis_index("y"), lax.axis_index("c")
    return x, y, c, 4 * x + 2 * y + c


def _peer(x, y, c, r):
    return (x ^ ((r >> 2) & 1), y ^ ((r >> 1) & 1), c ^ (r & 1))


def _gather_proj(x, g1, w_in, w_out, pool_w):
    def body(x_ref, g_ref, win_ref, wout_ref, pool_ref, ht_o, wt_o, woutb_o, wout_o, pool_o, proj_o,
             xbuf, hv, htv, wv, wob, pb, stage, send_sems, recv_sems, loc_sems, out_sems):
        px, py, c, my_idx = _place()
        fetch_x = pltpu.make_async_copy(x_ref, xbuf, loc_sems.at[5])
        fetch_x.start()
        me, sibling = (px, py, c), (px, py, 1 - c)
        chips = [(1 - px, py), (px, 1 - py), (1 - px, 1 - py)]
        for p in range(3):
            wv[3 * my_idx + p] = _bf(win_ref[0, :, p * TILE:(p + 1) * TILE])

        def index(bx, by, bc):
            return 4 * bx + 2 * by + bc

        def slot(w, block):
            return wv.at[pl.ds(3 * index(*block), 3)] if w == 0 else pool_o.at[index(*block)]

        def copy(k, w, block, to, src=None):
            return pltpu.make_async_remote_copy(
                src_ref=slot(w, block) if src is None else src, dst_ref=slot(w, block),
                send_sem=send_sems.at[2 * k + w], recv_sem=recv_sems.at[2 * k + w],
                device_id=to, device_id_type=MESH)

        def save(block):
            at = pl.ds(3 * index(*block), 3)
            pltpu.make_async_copy(wv.at[at], wt_o.at[at], loc_sems.at[4]).start()

        srcs = (slot(0, me), pb)
        first = []
        for w in (0, 1):
            if w == 1:
                pb[...] = _bf(pool_ref[0])
                wob[...] = _bf(wout_ref[0])
            group = [copy(1 + j, w, me, (*chip, c), src=srcs[w]) for j, chip in enumerate(chips[:2])]
            group.append(copy(0, w, me, sibling, src=srcs[w]))
            for cp in group:
                cp.start()
            first += group
        save(me)
        locs = [pltpu.make_async_copy(pb, slot(1, me), loc_sems.at[0]),
                pltpu.make_async_copy(wob, wout_o.at[my_idx], loc_sems.at[1]),
                pltpu.make_async_copy(wob, woutb_o, loc_sems.at[2])]
        for cp in locs:
            cp.start()

        fetch_x.wait()
        xv = xbuf[...]
        hv[...] = _bf(xv * lax.rsqrt(jnp.mean(xv * xv, axis=-1, keepdims=True) + EPS) * g_ref[...])
        rows = 256
        for r0 in range(0, T, rows):
            htv[:, r0:r0 + rows] = hv[r0:r0 + rows, :].T
        locs.append(pltpu.make_async_copy(htv, ht_o, loc_sems.at[3]))
        locs[-1].start()

        def out_copy(p, j):
            return pltpu.make_async_copy(stage.at[p], proj_o.at[j], out_sems.at[p])

        def project(nth, block):
            base = 3 * index(*block)

            def tile(p, carry):
                if nth > 0:
                    out_copy(p, base + p).wait()
                stage[p] = _dot(hv[...], wv[base + p], 1, 0)
                out_copy(p, base + p).start()
                return carry

            lax.fori_loop(0, 3, tile, 0)

        project(0, me)
        copy(0, 0, sibling, me).wait_recv()
        save(sibling)
        project(1, sibling)
        passed = []
        relay_from = (px ^ (1 - c), py ^ c, c)
        relay_to = (px ^ c, py ^ (1 - c), c)

        def arrived(w, j):
            copy(1 + j, w, (*chips[j], c), me).wait_recv()
            passed.append(copy(4 + j, w, (*chips[j], c), sibling))
            passed[-1].start()

        def relay(w):
            passed.append(copy(3, w, relay_from, relay_to))
            passed[-1].start()

        def handed(nth, j):
            copy(4 + j, 0, (*chips[j], 1 - c), me).wait_recv()
            save((*chips[j], 1 - c))
            project(nth, (*chips[j], 1 - c))

        arrived(0, 0)
        arrived(0, 1)
        relay(0)
        for j in range(2):
            save((*chips[j], c))
            project(2 + j, (*chips[j], c))
        handed(4, 0)
        handed(5, 1)
        arrived(1, 0)
        arrived(1, 1)
        relay(1)
        arrived(0, 2)
        save((*chips[2], c))
        project(6, (*chips[2], c))
        handed(7, 2)
        arrived(1, 2)
        copy(0, 1, sibling, me).wait_recv()
        for j, chip in enumerate(chips):
            copy(4 + j, 1, (*chip, 1 - c), me).wait_recv()
        keep = pltpu.make_async_copy(wv, wt_o, loc_sems.at[4])
        for p in range(3):
            out_copy(p, p).wait()
        for cp in first + passed:
            cp.wait_send()
        keep.wait()
        for cp in locs:
            cp.wait()

    vmem = pl.BlockSpec(memory_space=pltpu.VMEM)
    bf16 = jnp.bfloat16
    return pl.pallas_call(
        body, name="gather_proj",
        out_shape=(pltpu.HBM((D, T), bf16), pltpu.HBM((NTILE, D, TILE), bf16),
                   pltpu.HBM((DMIX // NDEV, D), bf16), pltpu.HBM((NDEV, DMIX // NDEV, D), bf16),
                   pltpu.HBM((NDEV, NGROUP, GROUP // NDEV, GROUP), bf16), pltpu.HBM((NTILE, T, TILE), jnp.float32)),
        in_specs=[ANY] + [vmem] * 4, out_specs=(ANY,) * 6,
        scratch_shapes=[pltpu.VMEM((T, D), jnp.float32),
                        pltpu.VMEM((T, D), bf16), pltpu.VMEM((D, T), bf16), pltpu.VMEM((NTILE, D, TILE), bf16),
                        pltpu.VMEM((DMIX // NDEV, D), bf16), pltpu.VMEM((NGROUP, GROUP // NDEV, GROUP), bf16),
                        pltpu.VMEM((3, T, TILE), jnp.float32),
                        pltpu.SemaphoreType.DMA((14,)), pltpu.SemaphoreType.DMA((14,)),
                        pltpu.SemaphoreType.DMA((6,)), pltpu.SemaphoreType.DMA((3,))],
        compiler_params=_params(vmem_mib=56),
    )(x, g1, w_in, w_out, pool_w)


def _split_start(name, arrays, n_copies, plan):
    k = len(arrays)

    def body(*refs):
        send_sems, recv_sems, token = refs[k], refs[k + 1], refs[-1]
        for i, (src, dst, to) in enumerate(plan(refs[:k])):
            pltpu.make_async_remote_copy(src_ref=src, dst_ref=dst, send_sem=send_sems.at[i],
                                         recv_sem=recv_sems.at[i], device_id=to, device_id_type=MESH).start()
        token[...] = jnp.zeros_like(token)

    out = pl.pallas_call(
        body, name=name,
        out_shape=(pltpu.SemaphoreType.DMA((n_copies,)), pltpu.SemaphoreType.DMA((n_copies,)),
                   *[pltpu.HBM(a.shape, a.dtype) for a in arrays], jax.ShapeDtypeStruct((8, 128), jnp.float32)),
        in_specs=[HBM] * k, out_specs=(SEM, SEM, *[HBM] * k, pl.BlockSpec(memory_space=pltpu.VMEM)),
        input_output_aliases={i: 2 + i for i in range(k)},
        compiler_params=pltpu.CompilerParams(has_side_effects=EFFECT),
    )(*[pltpu.with_memory_space_constraint(a, pltpu.HBM) for a in arrays])
    return out[0], out[1], out[2:2 + k], out[-1]


def _split_wait(name, arrays, send_sems, recv_sems, plan, after, which=None):
    k = len(arrays)

    def body(*refs):
        sends, recvs = refs[k], refs[k + 1]
        for i, (src, dst, to) in enumerate(plan(refs[:k])):
            if which is not None and i not in which:
                continue
            cp = pltpu.make_async_remote_copy(src_ref=src, dst_ref=dst, send_sem=sends.at[i], recv_sem=recvs.at[i],
                                              device_id=to, device_id_type=MESH)
            cp.wait_send()
            cp.wait_recv()

    return pl.pallas_call(
        body, name=name,
        out_shape=tuple(pltpu.HBM(a.shape, a.dtype) for a in arrays),
        in_specs=[HBM] * k + [SEM, SEM, ANY], out_specs=(HBM,) * k,
        input_output_aliases={i: i for i in range(k)},
        compiler_params=pltpu.CompilerParams(has_side_effects=EFFECT),
    )(*arrays, send_sems, recv_sems, after)


def _plan_wout(refs):
    src, land = refs
    x, y, c, me = _place()
    return [(src, land.at[me], _peer(x, y, c, r)) for r in range(1, NDEV)]


def _plan_rest(refs):
    gob, gpf, r_out, r_pool = refs
    x, y, c, me = _place()
    plan = []
    for r in range(1, NDEV):
        plan.append((gob.at[me ^ r], r_out.at[r - 1], _peer(x, y, c, r)))
        plan.append((gpf.at[me ^ r], r_pool.at[r - 1], _peer(x, y, c, r)))
    return plan


def _plan_in(refs):
    sum_b, r_in = refs
    x, y, c, _ = _place()
    plan = []
    for q in range(4):
        rows = pl.ds(q * (D // 4), D // 4)
        for j, (dx, dy) in enumerate(((1, 0), (0, 1), (1, 1))):
            px, py = x ^ dx, y ^ dy
            plan.append((sum_b.at[2 * px + py, :, rows], r_in.at[j, :, rows], (px, py, c)))
    return plan


def _plan_in_tiles(refs):
    sums, landing = refs
    x, y, c, _ = _place()
    plan = []
    for rel, (dx, dy) in enumerate(((1, 0), (0, 1), (1, 1))):
        for p in range(3):
            plan.append((sums.at[rel, p], landing.at[rel, p], (x ^ dx, y ^ dy, c)))
    return plan


def _plan_small(refs):
    small, land = refs
    x, y, c, me = _place()
    return [(small, land.at[me], _peer(x, y, c, r)) for r in range(1, NDEV)]


def _seg_tiles(s):
    return (s + 2) % NSEG


_POOL_SPECS = [pl.BlockSpec((None, T, GROUP), lambda g, base=base: (base + g, 0, 0)) for base in (0, 4)]
_HEAD_SPECS = [pl.BlockSpec((None, T, HEAD), lambda h, base=base: (base + h // 2, 0, h % 2))
               for base in (8, 12, 16, 20)]


def _row_ids(shape):
    return lax.broadcasted_iota(jnp.int32, shape, 0)


BAND_ROWS = 128
HALO = 16


def _window_sum(a, gidx, lead):
    width = lax.shift_left(jnp.int32(2), gidx)
    shape = (BAND_ROWS, BAND_ROWS + HALO)
    t, j = lax.broadcasted_iota(jnp.int32, shape, 0), lax.broadcasted_iota(jnp.int32, shape, 1)
    first = t if lead else t + HALO - width + 1
    band = _bf(jnp.where(j >= first, jnp.where(j < first + width, 1.0, 0.0), 0.0))
    zeros = jnp.zeros((HALO, a.shape[1]), jnp.bfloat16)
    padded = [jnp.concatenate([p, zeros] if lead else [zeros, p], axis=0) for p in _split2(a)]
    out = []
    for r0 in range(0, T, BAND_ROWS):
        slab = jnp.concatenate([p[r0:r0 + BAND_ROWS + HALO] for p in padded], axis=1)
        r = _dot(band, slab, 1, 0)
        out.append(r[:, :a.shape[1]] + r[:, a.shape[1]:])
    return jnp.concatenate(out, axis=0)


def _window_mean(s, gidx):
    inv = jnp.where(gidx == 0, 0.5, jnp.where(gidx == 1, 0.25, jnp.where(gidx == 2, 0.125, 0.0625)))
    width = lax.shift_left(jnp.int32(2), gidx)
    head = s[:16] / jnp.minimum(_row_ids((16, s.shape[1])) + 1, width).astype(jnp.float32)
    return jnp.concatenate([head, s[16:] * inv], axis=0)


def _pool_fwd(proj, pool_w, pool_scale, token):
    def body(u_ref, pg_ref, w_ref, sc_ref, token_any, y_ref):
        del token_any
        gidx = pl.program_id(0)
        u, pg = u_ref[...], pg_ref[...]
        d = _window_mean(_window_sum(u, gidx, False), gidx) - u
        mixed = _dot(_bf(d), w_ref[...], 1, 0)
        y_ref[...] = _bf(mixed * sc_ref[...] * (pg * _sigmoid(pg)))

    return pl.pallas_call(
        body, name="pool_fwd", grid=(NGROUP,),
        in_specs=[*_POOL_SPECS,
                  pl.BlockSpec((None, GROUP, GROUP), lambda g: (g, 0, 0)),
                  pl.BlockSpec((1, GROUP), lambda g: (0, g)), ANY],
        out_specs=pl.BlockSpec((T, GROUP), lambda g: (0, g)),
        out_shape=pltpu.HBM((T, DMIX), jnp.bfloat16),
        compiler_params=_params(("parallel",)),
    )(proj, proj, pool_w, pool_scale, token)


def _tri(lower):
    r = lax.broadcasted_iota(jnp.int32, (CHUNK, CHUNK), 0)
    c = lax.broadcasted_iota(jnp.int32, (CHUNK, CHUNK), 1)
    return (r >= c) if lower else (r <= c)


def _sum_rows_matrix():
    shape = (CHUNK + 16, CHUNK)
    r, c = lax.broadcasted_iota(jnp.int32, shape, 0), lax.broadcasted_iota(jnp.int32, shape, 1)
    run = jnp.where(c <= r, 1.0, 0.0)
    half = jnp.where(c < CHUNK // 2, 1.0, 0.0)
    return _bf(jnp.where(r < CHUNK, run, jnp.where(r < CHUNK + 8, 1.0, half)))


def _rev_sum_matrix():
    shape = (CHUNK, 2 * CHUNK)
    r, c = lax.broadcasted_iota(jnp.int32, shape, 0), lax.broadcasted_iota(jnp.int32, shape, 1)
    return _bf(jnp.where(c < CHUNK, jnp.where(c >= r, 1.0, 0.0), jnp.where(c - CHUNK < r, 1.0, 0.0)))


def _split2(a):
    hi = _bf(a)
    return [hi, _bf(a - hi.astype(jnp.float32))]


def _exact_sums(mat, pieces):
    x = jnp.concatenate([s for p in pieces for s in _split2(p)], axis=1)
    r = _dot(mat, x, 1, 0)
    return [r[:, 2 * j * HEAD:(2 * j + 1) * HEAD] + r[:, (2 * j + 1) * HEAD:(2 * j + 2) * HEAD]
            for j in range(len(pieces))]


def _gates(qv, fl, lb):
    sq = _sigmoid(qv)
    sg = _sigmoid(fl)
    f = lb + (1.0 - lb) * sg
    return dict(sq=sq, qs=qv * sq, sg=sg, f=f, kk=1.0 - f, g=jnp.log2(f))


def _decays(sums):
    big_g = sums[:CHUNK]
    total = sums[CHUNK:CHUNK + 8]
    g_last = jnp.tile(total, (CHUNK // 8, 1))
    g_mid = jnp.tile(sums[CHUNK + 8:], (CHUNK // 8, 1))
    return dict(
        e_q=jnp.exp2(big_g),
        e_k=jnp.exp2(g_last - big_g),
        e_qm=jnp.exp2(jnp.minimum(big_g - g_mid, EXP_CAP)),
        e_km=jnp.exp2(jnp.minimum(g_mid - big_g, EXP_CAP)),
        total8=jnp.exp2(total))


def _group_rows(gi):
    return [pl.ds(pl.multiple_of((gi * NB + j) * CHUNK, CHUNK), CHUNK) for j in range(NB)]


def _lower_bound(lb_ref):
    return _sigmoid(lb_ref[0:1, :] - lb_ref[1:2, :])


def _hgrn_fwd(proj, lb_logits, rec_g, y_in):
    def body(q_ref, f_ref, i_ref, gate_ref, lb_ref, rg_ref, y_any, y_ref, o_ref, st_ref):
        del y_any
        lb = _lower_bound(lb_ref)
        causal = _tri(True)
        smat = _sum_rows_matrix()

        def group(gi, st):
            rows = _group_rows(gi)
            ts = [_gates(q_ref[r, :], f_ref[r, :], lb) for r in rows]
            ds = [_decays(s) for s in _exact_sums(smat, [t["g"] for t in ts])]
            vs = [_bf(i_ref[r, :]) for r in rows]
            q_m = [_bf(t["qs"] * d["e_qm"]) for t, d in zip(ts, ds)]
            k_m = [_bf(t["kk"] * d["e_km"]) for t, d in zip(ts, ds)]
            q_e = [_bf(t["qs"] * d["e_q"]) for t, d in zip(ts, ds)]
            k_e = [_bf(t["kk"] * d["e_k"]) for t, d in zip(ts, ds)]
            a = [_bf(jnp.where(causal, _dot(q_m[j], k_m[j], 1, 1), 0.0)) for j in range(NB)]
            intra = [_dot(a[j], vs[j], 1, 0) for j in range(NB)]
            upd = [_dot(vs[j], k_e[j], 0, 0) for j in range(NB)]
            for j in range(NB):
                st_ref[gi * NB + j] = st
                o_ref[rows[j], :] = intra[j] + _dot(q_e[j], _bf(st), 1, 1)
                st = st * jnp.tile(ds[j]["total8"], (HEAD // 8, 1)) + upd[j]
            return st

        lax.fori_loop(0, NGRP, group, jnp.zeros((HEAD, HEAD), jnp.float32))
        o = o_ref[...]
        rn = o * lax.rsqrt(jnp.mean(o * o, axis=-1, keepdims=True) + EPS)
        gate = gate_ref[...]
        y_ref[...] = _bf(rn * rg_ref[...] * (gate * _sigmoid(gate)))

    return pl.pallas_call(
        body, name="hgrn_fwd", grid=(NHEAD,),
        in_specs=[*_HEAD_SPECS,
                  pl.BlockSpec((2, HEAD), lambda h: (0, h)),
                  pl.BlockSpec((1, HEAD), lambda h: (0, h)),
                  pl.BlockSpec(memory_space=pl.ANY)],
        out_specs=(pl.BlockSpec((T, HEAD), lambda h: (0, NHEAD + h)),
                   pl.BlockSpec((T, HEAD), lambda h: (0, h)),
                   pl.BlockSpec((None, NCHUNK, HEAD, HEAD), lambda h: (h, 0, 0, 0))),
        out_shape=(pltpu.HBM((T, DMIX), jnp.bfloat16), pltpu.HBM((T, D), jnp.float32),
                   pltpu.HBM((NHEAD, NCHUNK, HEAD, HEAD), jnp.float32)),
        input_output_aliases={6: 0},
        compiler_params=_params(("parallel",)),
    )(proj, proj, proj, proj, lb_logits, rec_g, y_in)


def _out_proj_loss(x, y, w_out, target, gf):
    rows = 512
    parts = [slice(k * rows // 2, (k + 1) * rows // 2) for k in range(2)]

    def body(x_ref, y_ref, w_ref, t_ref, g_ref, dz_ref, dzb_ref, sq_ref, dg_ref):
        zs = [x_ref[p, :] + _dot(y_ref[p, :], w_ref[...], 1, 0) for p in parts]
        sq = dg = 0.0
        for p, z in zip(parts, zs):
            r = lax.rsqrt(jnp.mean(z * z, axis=-1, keepdims=True) + EPS)
            zhat = z * r
            err = zhat * g_ref[...] - t_ref[p, :]
            dy = err * (1.0 / D)
            gdy = dy * g_ref[...]
            dz = r * (gdy - zhat * jnp.mean(zhat * gdy, axis=-1, keepdims=True))
            dz_ref[p, :] = dz
            dzb_ref[p, :] = _bf(dz)
            sq = sq + jnp.sum(err * err, axis=0, keepdims=True)
            dg = dg + jnp.sum(zhat * dy, axis=0, keepdims=True)

        @pl.when(pl.program_id(0) == 0)
        def _():
            sq_ref[...] = sq
            dg_ref[...] = dg

        @pl.when(pl.program_id(0) != 0)
        def _():
            sq_ref[...] += sq
            dg_ref[...] += dg

    tile = pl.BlockSpec((rows, D), lambda i: (i, 0))
    vec = pl.BlockSpec((1, D), lambda i: (0, 0))
    return pl.pallas_call(
        body, name="out_proj_loss", grid=(T // rows,),
        in_specs=[tile, pl.BlockSpec((rows, DMIX), lambda i: (i, 0)), pl.BlockSpec((DMIX, D), lambda i: (0, 0)),
                  tile, vec],
        out_specs=(tile, tile, vec, vec),
        out_shape=(pltpu.HBM((T, D), jnp.float32), pltpu.HBM((T, D), jnp.bfloat16),
                   jax.ShapeDtypeStruct((1, D), jnp.float32), jax.ShapeDtypeStruct((1, D), jnp.float32)),
        compiler_params=_params(("arbitrary",)),
    )(x, y, w_out, target, gf)


def _out_proj_bwd(dzb, w_out, y):
    tn = 512

    def body(dz_ref, w_ref, y_ref, dy_ref, gw_ref, gwb_ref):
        dz = dz_ref[...]
        dy_ref[...] = _dot(dz, w_ref[...], 1, 1)
        gw = _dot(y_ref[...], dz, 0, 0)
        gw_ref[...] = gw
        gwb_ref[...] = _bf(gw)

    return pl.pallas_call(
        body, name="out_proj_bwd", grid=(DMIX // tn,),
        in_specs=[pl.BlockSpec((T, D), lambda n: (0, 0)), pl.BlockSpec((tn, D), lambda n: (n, 0)),
                  pl.BlockSpec((T, tn), lambda n: (0, n))],
        out_specs=(pl.BlockSpec((T, tn), lambda n: (0, n)), pl.BlockSpec((tn, D), lambda n: (n, 0)),
                   pl.BlockSpec((tn, D), lambda n: (n, 0))),
        out_shape=(pltpu.HBM((T, DMIX), jnp.float32), pltpu.HBM((DMIX, D), jnp.float32),
                   pltpu.HBM((DMIX, D), jnp.bfloat16)),
        compiler_params=_params(("parallel",)),
    )(dzb, w_out, y)


def _hgrn_bwd(proj, lb_logits, rec_g, o, states, dymix, dproj_in, token):
    def body(q_ref, f_ref, i_ref, gate_ref, lb_ref, rg_ref, o_ref, st_ref, dy_ref, dp_any, token_any,
             dp_ref, drg_ref, dlb_ref, do_ref):
        del dp_any, token_any
        lb = _lower_bound(lb_ref)
        causal = _tri(True)
        smat, rmat = _sum_rows_matrix(), _rev_sum_matrix()

        o = o_ref[...]
        rs = lax.rsqrt(jnp.mean(o * o, axis=-1, keepdims=True) + EPS)
        rn = o * rs
        gate = gate_ref[...]
        sgate = _sigmoid(gate)
        dyv = dy_ref[...]
        d_r = dyv * (gate * sgate)
        dp_ref[3] = _bf(dyv * (rn * rg_ref[...]) * (sgate * (1.0 + gate * (1.0 - sgate))))
        drg_ref[...] = jnp.sum(d_r * rn, axis=0, keepdims=True)
        drn = d_r * rg_ref[...]
        do_ref[...] = rs * (drn - rn * jnp.mean(rn * drn, axis=-1, keepdims=True))

        def group(i, carry):
            dst, dlb = carry
            gi = NGRP - 1 - i
            rows = _group_rows(gi)
            span = range(NB)
            qvs = [q_ref[r, :] for r in rows]
            ts = [_gates(qv, f_ref[r, :], lb) for qv, r in zip(qvs, rows)]
            ds = [_decays(s) for s in _exact_sums(smat, [t["g"] for t in ts])]
            vs = [_bf(i_ref[r, :]) for r in rows]
            dos = [_bf(do_ref[r, :]) for r in rows]
            sts = [st_ref[gi * NB + j] for j in span]
            qe_f = [t["qs"] * d["e_q"] for t, d in zip(ts, ds)]
            ke_f = [t["kk"] * d["e_k"] for t, d in zip(ts, ds)]
            q_e, k_e = [_bf(a) for a in qe_f], [_bf(a) for a in ke_f]
            q_m = [_bf(t["qs"] * d["e_qm"]) for t, d in zip(ts, ds)]
            k_m = [_bf(t["kk"] * d["e_km"]) for t, d in zip(ts, ds)]
            a = [_bf(jnp.where(causal, _dot(q_m[j], k_m[j], 1, 1), 0.0)) for j in span]
            da = [_bf(jnp.where(causal, _dot(dos[j], vs[j], 1, 1), 0.0)) for j in span]
            dqm = [_dot(da[j], k_m[j], 1, 0) for j in span]
            dkm = [_dot(da[j], q_m[j], 0, 0) for j in span]
            dv_in = [_dot(a[j], dos[j], 0, 0) for j in span]
            dqe = [_dot(dos[j], _bf(sts[j]), 1, 0) for j in span]
            grow = [_dot(dos[j], q_e[j], 0, 0) for j in span]
            dke, carried = [None] * NB, [None] * NB
            for j in reversed(span):
                dst_b = _bf(dst)
                dke[j] = _dot(vs[j], dst_b, 1, 0)
                dp_ref[2, rows[j], :] = _bf(dv_in[j] + _dot(k_e[j], dst_b, 1, 1))
                carried[j] = ds[j]["total8"] * jnp.sum(dst * sts[j], axis=0, keepdims=True)
                dst = dst * jnp.tile(ds[j]["total8"], (HEAD // 8, 1)) + grow[j]
            kdk = [ke_f[j] * dke[j] for j in span]
            pos = [(q_m[j].astype(jnp.float32) * dqm[j] - k_m[j].astype(jnp.float32) * dkm[j]) + qe_f[j] * dqe[j]
                   for j in span]
            dgs = _exact_sums(rmat, [jnp.concatenate([pos[j], kdk[j]], axis=0) for j in span])
            for j in span:
                t, d = ts[j], ds[j]
                dg = dgs[j] + jnp.tile(carried[j], (CHUNK // 8, 1))
                dqs = dqm[j] * d["e_qm"] + dqe[j] * d["e_q"]
                dkk = dkm[j] * d["e_km"] + dke[j] * d["e_k"]
                df = dg / t["f"] - dkk
                dp_ref[1, rows[j], :] = _bf(df * (1.0 - lb) * (t["sg"] * (1.0 - t["sg"])))
                dp_ref[0, rows[j], :] = _bf(dqs * (t["sq"] * (1.0 + qvs[j] * (1.0 - t["sq"]))))
                dlb = dlb + df * (1.0 - t["sg"])
            return dst, dlb

        _, dlb = lax.fori_loop(0, NGRP, group, (jnp.zeros((HEAD, HEAD), jnp.float32),
                                                jnp.zeros((CHUNK, HEAD), jnp.float32)))
        dlb_ref[...] = jnp.sum(dlb, axis=0, keepdims=True)

    vec = pl.BlockSpec((1, HEAD), lambda h: (0, h))
    return pl.pallas_call(
        body, name="hgrn_bwd", grid=(NHEAD,),
        in_specs=[*_HEAD_SPECS,
                  pl.BlockSpec((2, HEAD), lambda h: (0, h)), vec,
                  pl.BlockSpec((T, HEAD), lambda h: (0, h)),
                  pl.BlockSpec((None, NCHUNK, HEAD, HEAD), lambda h: (h, 0, 0, 0)),
                  pl.BlockSpec((T, HEAD), lambda h: (0, NHEAD + h)), ANY, ANY],
        out_specs=(pl.BlockSpec((4, T, HEAD), lambda h: (0, 0, h)), vec, vec),
        out_shape=(pltpu.HBM((NSEG, T, D), jnp.bfloat16),
                   jax.ShapeDtypeStruct((1, D), jnp.float32), jax.ShapeDtypeStruct((1, D), jnp.float32)),
        scratch_shapes=[pltpu.VMEM((T, HEAD), jnp.float32)],
        input_output_aliases={9: 0},
        compiler_params=_params(("parallel",)),
    )(proj, proj, proj, proj, lb_logits, rec_g, o, states, dymix, dproj_in, token)


def _pool_bwd(proj, pool_w, pool_scale, dymix):
    def body(u_ref, pg_ref, w_ref, sc_ref, dy_ref, dp_ref, gw_ref, gs_ref):
        gidx = pl.program_id(0)
        u, pg = u_ref[...], pg_ref[...]
        d = _bf(_window_mean(_window_sum(u, gidx, False), gidx) - u)
        mixed = _dot(d, w_ref[...], 1, 0)
        spg = _sigmoid(pg)
        dyv = dy_ref[...]
        d_p = dyv * (pg * spg)
        dp_ref[1] = _bf(dyv * (mixed * sc_ref[...]) * (spg * (1.0 + pg * (1.0 - spg))))
        gs_ref[...] = jnp.sum(d_p * mixed, axis=0, keepdims=True)
        dmixed = _bf(d_p * sc_ref[...])
        gw_ref[...] = _dot(d, dmixed, 0, 0)
        dd = _dot(dmixed, w_ref[...], 1, 1)
        dp_ref[0] = _bf(_window_sum(_window_mean(dd, gidx), gidx, True) - dd)

    return pl.pallas_call(
        body, name="pool_bwd", grid=(NGROUP,),
        in_specs=[*_POOL_SPECS,
                  pl.BlockSpec((None, GROUP, GROUP), lambda g: (g, 0, 0)),
                  pl.BlockSpec((1, GROUP), lambda g: (0, g)),
                  pl.BlockSpec((T, GROUP), lambda g: (0, g))],
        out_specs=(pl.BlockSpec((2, T, GROUP), lambda g: (2, 0, g)),
                   pl.BlockSpec((None, GROUP, GROUP), lambda g: (g, 0, 0)),
                   pl.BlockSpec((1, GROUP), lambda g: (0, g))),
        out_shape=(pltpu.HBM((NSEG, T, D), jnp.bfloat16),
                   jax.ShapeDtypeStruct((NGROUP, GROUP, GROUP), jnp.float32),
                   jax.ShapeDtypeStruct((1, D), jnp.float32)),
        compiler_params=_params(("parallel",)),
    )(proj, proj, pool_w, pool_scale, dymix)


HALF = NTILE // 2
AWAY = HALF - 3


def _dproj_tile(chip, side, p):
    j = 6 * chip + 3 * side + p
    return ((j // 4 + 4) % NSEG, 0, j % 4)


def _sibling_copy(sib_out, sib_in, send_sems, recv_sems, slot):
    x, y, c, _ = _place()
    return pltpu.make_async_remote_copy(
        src_ref=sib_out.at[slot], dst_ref=sib_in.at[slot], send_sem=send_sems.at[slot],
        recv_sem=recv_sems.at[slot], device_id=(x, y, 1 - c), device_id_type=MESH)


def _proj_bwd_w_far(place, ht, dproj):
    def body(place_ref, h_ref, dp_ref, sib_out, sib_in, send_sems, recv_sems, stage, loc_sems):
        del place_ref
        i = pl.program_id(0)

        def to_hbm(k):
            return pltpu.make_async_copy(stage.at[k], sib_out.at[k], loc_sems.at[k])

        def send(k):
            to_hbm(k).wait()
            _sibling_copy(sib_out, sib_in, send_sems, recv_sems, k).start()

        stage[i] = _bf(_dot(h_ref[...], dp_ref[...], 1, 0))

        @pl.when(i > 0)
        def _():
            send(i - 1)

        to_hbm(i).start()

        @pl.when(i == HALF - 1)
        def _():
            send(i)

    buf = pltpu.HBM((HALF, D, TILE), jnp.bfloat16)
    sems = pltpu.SemaphoreType.DMA((HALF,))
    return pl.pallas_call(
        body, name="proj_bwd_w_far",
        grid_spec=pltpu.PrefetchScalarGridSpec(
            num_scalar_prefetch=1, grid=(HALF,),
            in_specs=[pl.BlockSpec((D, T), lambda i, pr: (0, 0)),
                      pl.BlockSpec((None, T, TILE), lambda i, pr: _dproj_tile(i // 3, 1 - pr[2], i % 3))],
            out_specs=(HBM, HBM, SEM, SEM),
            scratch_shapes=[pltpu.VMEM((HALF, D, TILE), jnp.bfloat16), pltpu.SemaphoreType.DMA((HALF,))]),
        out_shape=(buf, buf, sems, sems),
        compiler_params=pltpu.CompilerParams(dimension_semantics=("arbitrary",), vmem_limit_bytes=48 * MIB,
                                             has_side_effects=EFFECT),
    )(place, ht, dproj)


def _proj_bwd_w_near(place, ht, dproj, sib_out, sib_in, sib_send, sib_recv):
    def owner_chip(k, pr):
        return jnp.where(k < AWAY, (pr[1] + 1 + k % 3) % 4, pr[1])

    def tile_p(k):
        return jnp.where(k < AWAY, k // 3, k - AWAY)

    def body(place_ref, h_ref, dp_ref, sib_out, sib_in, sib_send, sib_recv, sums, own_ref, landing, out_send,
             out_recv, recvbuf, outbuf, in_sems, loc_sems):
        i = pl.program_id(0)
        px, py, c, _ = _place()

        def slot_of(k):
            return 3 * owner_chip(k, place_ref) + tile_p(k)

        def load(k):
            return pltpu.make_async_copy(sib_in.at[slot_of(k)], recvbuf.at[k % 2], in_sems.at[k % 2])

        def fetch(k):
            _sibling_copy(sib_out, sib_in, sib_send, sib_recv, slot_of(k)).wait_recv()
            load(k).start()

        def route(k):
            chip = owner_chip(k, place_ref)
            cx, cy = chip // 2, chip % 2
            return cx, cy, (cx ^ px) + 2 * (cy ^ py) - 1, tile_p(k)

        def to_hbm(k):
            _, _, rel, p = route(k)
            return pltpu.make_async_copy(outbuf.at[k], sums.at[rel, p], loc_sems.at[k])

        def to_owner(k):
            cx, cy, rel, p = route(k)
            return pltpu.make_async_remote_copy(
                src_ref=sums.at[rel, p], dst_ref=landing.at[rel, p], send_sem=out_send.at[3 * rel + p],
                recv_sem=out_recv.at[3 * rel + p], device_id=(cx, cy, c), device_id_type=MESH)

        @pl.when(i == 0)
        def _():
            fetch(i)

        @pl.when(i < HALF - 1)
        def _():
            fetch(i + 1)

        gw = _dot(h_ref[...], dp_ref[...], 1, 0)

        @pl.when(jnp.logical_and(i > 0, i <= AWAY))
        def _():
            to_hbm(i - 1).wait()
            to_owner(i - 1).start()

        load(i).wait()
        total = gw + recvbuf[i % 2].astype(jnp.float32)
        own_ref[...] = total

        @pl.when(i < AWAY)
        def _():
            outbuf[i] = _bf(total)
            to_hbm(i).start()

        @pl.when(i == HALF - 1)
        def _():
            for slot in range(HALF):
                _sibling_copy(sib_out, sib_in, sib_send, sib_recv, slot).wait_send()

    travelling = pltpu.HBM((3, 3, D, TILE), jnp.bfloat16)
    sems = pltpu.SemaphoreType.DMA((AWAY,))
    return pl.pallas_call(
        body, name="proj_bwd_w_near",
        grid_spec=pltpu.PrefetchScalarGridSpec(
            num_scalar_prefetch=1, grid=(HALF,),
            in_specs=[pl.BlockSpec((D, T), lambda i, pr: (0, 0)),
                      pl.BlockSpec((None, T, TILE), lambda i, pr: _dproj_tile(owner_chip(i, pr), pr[2], tile_p(i))),
                      HBM, HBM, SEM, SEM],
            out_specs=(HBM, pl.BlockSpec((None, D, TILE), lambda i, pr: (jnp.where(i < AWAY, 0, i % 3), 0, 0)),
                       HBM, SEM, SEM),
            scratch_shapes=[pltpu.VMEM((2, D, TILE), jnp.bfloat16), pltpu.VMEM((AWAY, D, TILE), jnp.bfloat16),
                            pltpu.SemaphoreType.DMA((2,)), pltpu.SemaphoreType.DMA((AWAY,))]),
        out_shape=(travelling, pltpu.HBM((3, D, TILE), jnp.float32), travelling, sems, sems),
        compiler_params=pltpu.CompilerParams(dimension_semantics=("arbitrary",), vmem_limit_bytes=48 * MIB,
                                             has_side_effects=EFFECT),
    )(place, ht, dproj, sib_out, sib_in, sib_send, sib_recv)


def _proj_bwd_w(place, ht, dproj):
    half = NTILE // 2

    def owner_chip(i, pr):
        return jnp.where(i < half, i // 3, (pr[1] + 1 + (i - half) // 3) % 4)

    def tile_of(i, pr):
        side = jnp.where(i < half, 1 - pr[2], pr[2])
        return 6 * owner_chip(i, pr) + 3 * side + i % 3

    def dproj_block(i, pr):
        j = tile_of(i, pr)
        return ((j // 4 + 4) % NSEG, 0, j % 4)

    def mine(i):
        return jnp.maximum(i, half)

    def body(place_ref, h_ref, dp_ref, sum_ref, own_ref, sendbuf, recvbuf, send_sems, recv_sems):
        i = pl.program_id(0)
        px, py, c, _ = _place()
        gw = _dot(h_ref[...], dp_ref[...], 1, 0)

        def to_sibling(slot):
            return pltpu.make_async_remote_copy(
                src_ref=sendbuf.at[slot], dst_ref=recvbuf.at[slot], send_sem=send_sems.at[slot],
                recv_sem=recv_sems.at[slot], device_id=(px, py, 1 - c), device_id_type=MESH)

        @pl.when(i < half)
        def _():
            sendbuf[i] = _bf(gw)
            to_sibling(i).start()

        @pl.when(i >= half)
        def _():
            slot = 3 * owner_chip(i, place_ref) + i % 3
            to_sibling(slot).wait_recv()
            total = gw + recvbuf[slot].astype(jnp.float32)
            sum_ref[...] = _bf(total)
            own_ref[...] = total

        @pl.when(i == NTILE - 1)
        def _():
            for slot in range(half):
                to_sibling(slot).wait_send()

    return pl.pallas_call(
        body, name="proj_bwd_w",
        grid_spec=pltpu.PrefetchScalarGridSpec(
            num_scalar_prefetch=1, grid=(NTILE,),
            in_specs=[pl.BlockSpec((D, T), lambda i, pr: (0, 0)),
                      pl.BlockSpec((None, T, TILE), lambda i, pr: dproj_block(i, pr))],
            out_specs=(pl.BlockSpec((None, None, D, TILE), lambda i, pr: (owner_chip(mine(i), pr), mine(i) % 3, 0, 0)),
                       pl.BlockSpec((None, D, TILE), lambda i, pr: (jnp.where(i < NTILE - 3, 0, i % 3), 0, 0))),
            scratch_shapes=[pltpu.VMEM((half, D, TILE), jnp.bfloat16), pltpu.VMEM((half, D, TILE), jnp.bfloat16),
                            pltpu.SemaphoreType.DMA((half,)), pltpu.SemaphoreType.DMA((half,))]),
        out_shape=(pltpu.HBM((4, 3, D, TILE), jnp.bfloat16), pltpu.HBM((3, D, TILE), jnp.float32)),
        compiler_params=_params(("arbitrary",)),
    )(place, ht, dproj)


def _proj_bwd_x(dproj, w_t, x, g1, dz, token):
    tm = 512
    pairs = NSEG // 2

    def body(dp_ref, w_ref, x_ref, g_ref, dz_ref, token_any, dx_ref, dg_ref, wcat, acc):
        del token_any
        s, m = pl.program_id(0), pl.program_id(1)

        @pl.when(m == 0)
        def _():
            for i in range(8):
                wcat[:, i * TILE:(i + 1) * TILE] = w_ref[i]

        mine = pl.ds(pl.multiple_of(m * tm, tm), tm)

        @pl.when(s == 0)
        def _():
            acc[mine, :] = jnp.zeros((tm, D), jnp.float32)

        acc[mine, :] += _dot(jnp.concatenate([dp_ref[0], dp_ref[1]], axis=1), wcat[...], 1, 1)

        @pl.when(s == pairs - 1)
        def _():
            xv = x_ref[...]
            rs = lax.rsqrt(jnp.mean(xv * xv, axis=-1, keepdims=True) + EPS)
            xhat = xv * rs
            dhv = acc[mine, :]
            gdh = dhv * g_ref[...]
            dx_ref[...] = dz_ref[...] + rs * (gdh - xhat * jnp.mean(xhat * gdh, axis=-1, keepdims=True))
            dg = jnp.sum(xhat * dhv, axis=0, keepdims=True)

            @pl.when(m == 0)
            def _():
                dg_ref[...] = dg

            @pl.when(m != 0)
            def _():
                dg_ref[...] += dg

    rows = pl.BlockSpec((tm, D), lambda s, m: (jnp.where(s == pairs - 1, m, 0), 0))
    vec = pl.BlockSpec((1, D), lambda s, m: (0, 0))
    return pl.pallas_call(
        body, name="proj_bwd_x", grid=(pairs, T // tm),
        in_specs=[pl.BlockSpec((2, tm, D), lambda s, m: (s, m, 0)),
                  pl.BlockSpec((8, D, TILE), lambda s, m: ((s + 1) % pairs, 0, 0)), rows, vec, rows, ANY],
        out_specs=(rows, vec),
        out_shape=(jax.ShapeDtypeStruct((T, D), jnp.float32), jax.ShapeDtypeStruct((1, D), jnp.float32)),
        scratch_shapes=[pltpu.VMEM((D, 2 * D), jnp.bfloat16), pltpu.VMEM((T, D), jnp.float32)],
        compiler_params=_params(("arbitrary", "arbitrary"), vmem_mib=56),
    )(dproj, w_t, x, g1, dz, token)


def _adamw(w, g, m, v):
    m_new = ADAM_B1 * m + (1.0 - ADAM_B1) * g
    v_new = ADAM_B2 * v + (1.0 - ADAM_B2) * (g * g)
    delta = -ADAM_LR * ((m_new / BC1) / (jnp.sqrt(v_new / BC2) + ADAM_EPS) + ADAM_WD * w)
    return delta, m_new, v_new


def _reduce_adam(name, place, parts, w, m, v, grid, w_spec, into=()):
    n = len(parts)

    def body(place_ref, *refs):
        del place_ref
        w_ref, m_ref, v_ref = refs[n:n + 3]
        g_ref, d_ref, mo_ref, vo_ref = refs[-4:]
        g = None
        for ref, (_, _, stacked) in zip(refs[:n], parts):
            terms = [ref[r] for r in range(ref.shape[0])] if stacked else [ref[...]]
            for t in terms:
                if t.shape[-1] != w_ref.shape[-1]:
                    t = jnp.concatenate([t[p] for p in range(t.shape[0])], axis=1)
                g = t.astype(jnp.float32) if g is None else g + t.astype(jnp.float32)
        delta, m_new, v_new = _adamw(w_ref[...], g, m_ref[...], v_ref[...])
        g_ref[...] = g
        d_ref[...] = delta
        mo_ref[...] = m_new
        vo_ref[...] = v_new

    shape = jax.ShapeDtypeStruct(w.shape, jnp.float32)
    return pl.pallas_call(
        body, name=name,
        grid_spec=pltpu.PrefetchScalarGridSpec(
            num_scalar_prefetch=1, grid=grid,
            in_specs=[spec for _, spec, _ in parts] + [w_spec] * 3 + [ANY] * len(into), out_specs=(w_spec,) * 4),
        out_shape=(shape,) * 4,
        input_output_aliases={1 + n + 3 + j: j for j in range(len(into))},
        compiler_params=_params(("parallel",)),
    )(place, *[_in_hbm(a) for a in [a for a, _, _ in parts] + [w, m, v, *into]])


def _small_adam(place, own, parts, w, m, v):
    def body(place_ref, own_ref, p_ref, w_ref, m_ref, v_ref, g_ref, d_ref, mo_ref, vo_ref):
        me = place_ref[0]
        g = None
        for s in range(NDEV):
            term = jnp.where(me == s, own_ref[...], p_ref[s])
            g = term if g is None else g + term
        wv = w_ref[...]
        rows = _row_ids(wv.shape)
        other = jnp.where(rows == 2, pltpu.roll(wv, 7, 0), jnp.where(rows == 3, pltpu.roll(wv, 1, 0), 0.0))
        lbv = _sigmoid(wv - other)
        sign = jnp.where(rows == 2, 1.0, -1.0)
        g = jnp.where((rows == 2) | (rows == 3), sign * g * lbv * (1.0 - lbv), g)
        delta, m_new, v_new = _adamw(wv, g, m_ref[...], v_ref[...])
        g_ref[...] = g
        d_ref[...] = delta
        mo_ref[...] = m_new
        vo_ref[...] = v_new

    shape = jax.ShapeDtypeStruct((8, D), jnp.float32)
    vmem = pl.BlockSpec(memory_space=pltpu.VMEM)
    return pl.pallas_call(
        body, name="small_adam", out_shape=(shape,) * 4,
        in_specs=[pl.BlockSpec(memory_space=pltpu.SMEM)] + [vmem] * 5, out_specs=(vmem,) * 4,
    )(place, own, parts, w, m, v)


def _rows8(*vecs):
    rows = [a.reshape(-1, D) for a in vecs]
    n = sum(r.shape[0] for r in rows)
    return jnp.concatenate(rows + [jnp.zeros((8 - n, D), jnp.float32)], axis=0)


def kernel(x, norm1_g, w_in, pool_w, pool_scale, lb_logits, rec_norm_g, w_out, final_norm_g, loss_target, m_norm1_g, m_w_in, m_pool_w, m_pool_scale, m_lb_logits, m_rec_norm_g, m_w_out, m_final_norm_g, v_norm1_g, v_w_in, v_pool_w, v_pool_scale, v_lb_logits, v_rec_norm_g, v_w_out, v_final_norm_g):
    xs = x[0]
    target = loss_target[0]
    ix, iy, ic = lax.axis_index("x"), lax.axis_index("y"), lax.axis_index("c")
    place = jnp.stack([4 * ix + 2 * iy + ic, 2 * ix + iy, ic]).astype(jnp.int32)
    gf = final_norm_g.reshape(1, D)

    ht, w_t, w_out_b, w_out_g, pool_g, proj = _gather_proj(xs, norm1_g, w_in, w_out, pool_w)
    pool_full = pool_g.transpose(1, 0, 2, 3).reshape(NGROUP, GROUP, GROUP)
    wout = [w_out_b, w_out_g]
    wout_send, wout_recv, wout, wout_token = _split_start("gather_wout_start", wout, NDEV - 1, _plan_wout)

    y = _pool_fwd(proj, pool_full, pool_scale, wout_token)
    y, o, states = _hgrn_fwd(proj, lb_logits, rec_norm_g, y)
    _, w_out_g = _split_wait("gather_wout_wait", wout, wout_send, wout_recv, _plan_wout, o)
    w_out_full = _in_hbm(w_out_g.reshape(DMIX, D))
    dz, dzb, sq, dgf = _out_proj_loss(xs, y, w_out_full, target, gf)

    dymix, gwout_f, gwout_b = _out_proj_bwd(dzb, w_out_full, y)
    dproj, gpool, dscale = _pool_bwd(proj, pool_full, pool_scale, dymix)

    blk_out = (NDEV, DMIX // NDEV, D)
    blk_pool = (NDEV, NGROUP, GROUP // NDEV, GROUP)
    gpool_s = gpool.reshape(NGROUP, NDEV, GROUP // NDEV, GROUP).transpose(1, 0, 2, 3)
    rest = [gwout_b.reshape(blk_out), gpool_s,
            lax.empty((NDEV - 1,) + blk_out[1:], jnp.bfloat16), lax.empty((NDEV - 1,) + blk_pool[1:], jnp.float32)]
    rest_send, rest_recv, rest, rest_token = _split_start("scatter_rest_start", rest, 2 * (NDEV - 1), _plan_rest)

    dproj, drecg, dlb = _hgrn_bwd(proj, lb_logits, rec_norm_g, o, states, dymix, dproj, rest_token)
    chip_sums, own_sum, landing, win_send, win_recv = _proj_bwd_w_near(
        place, ht, dproj, *_proj_bwd_w_far(place, ht, dproj))
    win = [chip_sums, landing]

    grad_x, dg1 = _proj_bwd_x(dproj, w_t, xs, norm1_g, dz, chip_sums)

    small = [_rows8(dg1, dscale, dlb, dlb, drecg, dgf, sq), lax.empty((NDEV, 8, D), jnp.float32)]
    small_send, small_recv, small, small_token = _split_start("gather_small_start", small, NDEV - 1, _plan_small)

    _, gpool_own, r_out, r_pool = _split_wait("scatter_rest_wait", rest, rest_send, rest_recv, _plan_rest,
                                              small_token)
    g_wout, d_wout, m_wout, v_wout = _reduce_adam(
        "adam_w_out", place,
        [(gwout_f.reshape(blk_out), pl.BlockSpec((None,) + blk_out[1:], lambda i, pr: (pr[0], 0, 0)), False),
         (r_out, pl.BlockSpec((NDEV - 1,) + blk_out[1:], lambda i, pr: (0, 0, 0)), True)],
        w_out, m_w_out, v_w_out, (1,), pl.BlockSpec((None,) + blk_out[1:], lambda i, pr: (0, 0, 0)))
    g_pool, d_pool, m_pool, v_pool = _reduce_adam(
        "adam_pool_w", place,
        [(gpool_own, pl.BlockSpec((None,) + blk_pool[1:], lambda i, pr: (pr[0], 0, 0, 0)), False),
         (r_pool, pl.BlockSpec((NDEV - 1,) + blk_pool[1:], lambda i, pr: (0, 0, 0, 0)), True)],
        pool_w, m_pool_w, v_pool_w, (1,), pl.BlockSpec((None,) + blk_pool[1:], lambda i, pr: (0, 0, 0, 0)))

    def adam_w_in(name, r_in, first, count, into=()):
        return _reduce_adam(
            name, place,
            [(own_sum, pl.BlockSpec((3, D // 8, TILE), lambda i, pr: (0, first + i, 0)), False),
             (r_in, pl.BlockSpec((3, 3, D // 8, TILE), lambda i, pr: (0, 0, first + i, 0)), True)],
            w_in, m_w_in, v_w_in, (count,), pl.BlockSpec((None, D // 8, 3 * TILE), lambda i, pr: (0, first + i, 0)),
            into)

    _, r_in = _split_wait("scatter_win_wait", win, win_send, win_recv, _plan_in_tiles, d_pool)
    g_win, d_win, m_win, v_win = adam_w_in("adam_w_in", r_in, 0, 8)

    own_small, r_small = _split_wait("gather_small_wait", small, small_send, small_recv, _plan_small, d_win)
    g_s, d_s, m_s, v_s = _small_adam(
        place, own_small, r_small,
        _rows8(norm1_g, pool_scale, lb_logits, rec_norm_g, final_norm_g),
        _rows8(m_norm1_g, m_pool_scale, m_lb_logits, m_rec_norm_g, m_final_norm_g),
        _rows8(v_norm1_g, v_pool_scale, v_lb_logits, v_rec_norm_g, v_final_norm_g))
    loss = jnp.sum(g_s[6]) * (0.5 / D)

    def small_outs(a):
        return a[0:1], a[1:2], a[2:4], a[4:5], a[5]

    def outs(small_a, win, pool, wout):
        n1, ps, lbl, rg, fg = small_outs(small_a)
        return n1, win, pool, ps, lbl, rg, wout, fg

    return (loss, grad_x[None],
            *outs(g_s, g_win, g_pool, g_wout), *outs(d_s, d_win, d_pool, d_wout),
            *outs(m_s, m_win, m_pool, m_wout), *outs(v_s, v_win, v_pool, v_wout))
```

```python
import functools

import jax
import jax.numpy as jnp
from jax import lax
from jax.experimental import pallas as pl
from jax.experimental.pallas import tpu as pltpu

T = 2048
D = 1024
NSEG = 6
NTILE = 24
TILE = 256
DMIX = 2048
NDEV = 8
HEAD = 128
NHEAD = 8
CHUNK = 64
NCHUNK = T // CHUNK
NB = 32
NGRP = NCHUNK // NB
NGROUP = 4
GROUP = 256
EPS = 1e-6
EXP_CAP = 115.0
MESH = pl.DeviceIdType.MESH
AXES = ("x", "y", "c")
ANY = pl.BlockSpec(memory_space=pl.ANY)
HBM = pl.BlockSpec(memory_space=pltpu.HBM)
SEM = pl.BlockSpec(memory_space=pltpu.SEMAPHORE)
EFFECT = pltpu.SideEffectType.DATAFLOW_SIDE_EFFECTING

ADAM_LR = 0.001
ADAM_B1 = 0.9
ADAM_B2 = 0.999
ADAM_EPS = 1e-08
ADAM_WD = 0.01
ADAM_STEP = 10
BC1 = 1.0 - ADAM_B1 ** ADAM_STEP
BC2 = 1.0 - ADAM_B2 ** ADAM_STEP

MIB = 1 << 20


def _params(sem=None, vmem_mib=48):
    return pltpu.CompilerParams(dimension_semantics=sem, vmem_limit_bytes=vmem_mib * MIB)


def _sigmoid(v):
    return 1.0 / (1.0 + jnp.exp(-v))


def _dot(a, b, ca, cb, precision=None):
    return lax.dot_general(a, b, (((ca,), (cb,)), ((), ())), precision=precision,
                           preferred_element_type=jnp.float32)


def _bf(v):
    return v.astype(jnp.bfloat16)


def _in_hbm(a):
    return pltpu.with_memory_space_constraint(a, pltpu.HBM)


def _place():
    x, y, c = lax.axis_index("x"), lax.axis_index("y"), lax.axis_index("c")
    return x, y, c, 4 * x + 2 * y + c


def _peer(x, y, c, r):
    return (x ^ ((r >> 2) & 1), y ^ ((r >> 1) & 1), c ^ (r & 1))


def _gather_proj(x, g1, w_in, w_out, pool_w):
    def body(x_ref, g_ref, win_ref, wout_ref, pool_ref, ht_o, wt_o, woutb_o, wout_o, pool_o, proj_o,
             xbuf, hv, htv, wv, wob, pb, stage, send_sems, recv_sems, loc_sems, out_sems):
        px, py, c, my_idx = _place()
        fetch_x = pltpu.make_async_copy(x_ref, xbuf, loc_sems.at[5])
        fetch_x.start()
        me, sibling = (px, py, c), (px, py, 1 - c)
        chips = [(1 - px, py), (px, 1 - py), (1 - px, 1 - py)]
        for p in range(3):
            wv[3 * my_idx + p] = _bf(win_ref[0, :, p * TILE:(p + 1) * TILE])

        def index(bx, by, bc):
            return 4 * bx + 2 * by + bc

        def slot(w, block):
            return wv.at[pl.ds(3 * index(*block), 3)] if w == 0 else pool_o.at[index(*block)]

        def copy(k, w, block, to, src=None):
            return pltpu.make_async_remote_copy(
                src_ref=slot(w, block) if src is None else src, dst_ref=slot(w, block),
                send_sem=send_sems.at[2 * k + w], recv_sem=recv_sems.at[2 * k + w],
                device_id=to, device_id_type=MESH)

        def save(block):
            at = pl.ds(3 * index(*block), 3)
            pltpu.make_async_copy(wv.at[at], wt_o.at[at], loc_sems.at[4]).start()

        srcs = (slot(0, me), pb)
        first = []
        for w in (0, 1):
            if w == 1:
                pb[...] = _bf(pool_ref[0])
                wob[...] = _bf(wout_ref[0])
            group = [copy(1 + j, w, me, (*chip, c), src=srcs[w]) for j, chip in enumerate(chips[:2])]
            group.append(copy(0, w, me, sibling, src=srcs[w]))
            for cp in group:
                cp.start()
            first += group
        save(me)
        locs = [pltpu.make_async_copy(pb, slot(1, me), loc_sems.at[0]),
                pltpu.make_async_copy(wob, wout_o.at[my_idx], loc_sems.at[1]),
                pltpu.make_async_copy(wob, woutb_o, loc_sems.at[2])]
        for cp in locs:
            cp.start()

        fetch_x.wait()
        xv = xbuf[...]
        hv[...] = _bf(xv * lax.rsqrt(jnp.mean(xv * xv, axis=-1, keepdims=True) + EPS) * g_ref[...])
        rows = 256
        for r0 in range(0, T, rows):
            htv[:, r0:r0 + rows] = hv[r0:r0 + rows, :].T
        locs.append(pltpu.make_async_copy(htv, ht_o, loc_sems.at[3]))
        locs[-1].start()

        def out_copy(p, j):
            return pltpu.make_async_copy(stage.at[p], proj_o.at[j], out_sems.at[p])

        def project(nth, block):
            base = 3 * index(*block)

            def tile(p, carry):
                if nth > 0:
                    out_copy(p, base + p).wait()
                stage[p] = _dot(hv[...], wv[base + p], 1, 0)
                out_copy(p, base + p).start()
                return carry

            lax.fori_loop(0, 3, tile, 0)

        project(0, me)
        copy(0, 0, sibling, me).wait_recv()
        save(sibling)
        project(1, sibling)
        passed = []
        relay_from = (px ^ (1 - c), py ^ c, c)
        relay_to = (px ^ c, py ^ (1 - c), c)

        def arrived(w, j):
            copy(1 + j, w, (*chips[j], c), me).wait_recv()
            passed.append(copy(4 + j, w, (*chips[j], c), sibling))
            passed[-1].start()

        def relay(w):
            passed.append(copy(3, w, relay_from, relay_to))
            passed[-1].start()

        def handed(nth, j):
            copy(4 + j, 0, (*chips[j], 1 - c), me).wait_recv()
            save((*chips[j], 1 - c))
            project(nth, (*chips[j], 1 - c))

        arrived(0, 0)
        arrived(0, 1)
        relay(0)
        for j in range(2):
            save((*chips[j], c))
            project(2 + j, (*chips[j], c))
        handed(4, 0)
        handed(5, 1)
        arrived(1, 0)
        arrived(1, 1)
        relay(1)
        arrived(0, 2)
        save((*chips[2], c))
        project(6, (*chips[2], c))
        handed(7, 2)
        arrived(1, 2)
        copy(0, 1, sibling, me).wait_recv()
        for j, chip in enumerate(chips):
            copy(4 + j, 1, (*chip, 1 - c), me).wait_recv()
        keep = pltpu.make_async_copy(wv, wt_o, loc_sems.at[4])
        for p in range(3):
            out_copy(p, p).wait()
        for cp in first + passed:
            cp.wait_send()
        keep.wait()
        for cp in locs:
            cp.wait()

    vmem = pl.BlockSpec(memory_space=pltpu.VMEM)
    bf16 = jnp.bfloat16
    return pl.pallas_call(
        body, name="gather_proj",
        out_shape=(pltpu.HBM((D, T), bf16), pltpu.HBM((NTILE, D, TILE), bf16),
                   pltpu.HBM((DMIX // NDEV, D), bf16), pltpu.HBM((NDEV, DMIX // NDEV, D), bf16),
                   pltpu.HBM((NDEV, NGROUP, GROUP // NDEV, GROUP), bf16), pltpu.HBM((NTILE, T, TILE), jnp.float32)),
        in_specs=[ANY] + [vmem] * 4, out_specs=(ANY,) * 6,
        scratch_shapes=[pltpu.VMEM((T, D), jnp.float32),
                        pltpu.VMEM((T, D), bf16), pltpu.VMEM((D, T), bf16), pltpu.VMEM((NTILE, D, TILE), bf16),
                        pltpu.VMEM((DMIX // NDEV, D), bf16), pltpu.VMEM((NGROUP, GROUP // NDEV, GROUP), bf16),
                        pltpu.VMEM((3, T, TILE), jnp.float32),
                        pltpu.SemaphoreType.DMA((14,)), pltpu.SemaphoreType.DMA((14,)),
                        pltpu.SemaphoreType.DMA((6,)), pltpu.SemaphoreType.DMA((3,))],
        compiler_params=_params(vmem_mib=56),
    )(x, g1, w_in, w_out, pool_w)


def _split_start(name, arrays, n_copies, plan):
    k = len(arrays)

    def body(*refs):
        send_sems, recv_sems, token = refs[k], refs[k + 1], refs[-1]
        for i, (src, dst, to) in enumerate(plan(refs[:k])):
            pltpu.make_async_remote_copy(src_ref=src, dst_ref=dst, send_sem=send_sems.at[i],
                                         recv_sem=recv_sems.at[i], device_id=to, device_id_type=MESH).start()
        token[...] = jnp.zeros_like(token)

    out = pl.pallas_call(
        body, name=name,
        out_shape=(pltpu.SemaphoreType.DMA((n_copies,)), pltpu.SemaphoreType.DMA((n_copies,)),
                   *[pltpu.HBM(a.shape, a.dtype) for a in arrays], jax.ShapeDtypeStruct((8, 128), jnp.float32)),
        in_specs=[HBM] * k, out_specs=(SEM, SEM, *[HBM] * k, pl.BlockSpec(memory_space=pltpu.VMEM)),
        input_output_aliases={i: 2 + i for i in range(k)},
        compiler_params=pltpu.CompilerParams(has_side_effects=EFFECT),
    )(*[pltpu.with_memory_space_constraint(a, pltpu.HBM) for a in arrays])
    return out[0], out[1], out[2:2 + k], out[-1]


def _split_wait(name, arrays, send_sems, recv_sems, plan, after, which=None):
    k = len(arrays)

    def body(*refs):
        sends, recvs = refs[k], refs[k + 1]
        for i, (src, dst, to) in enumerate(plan(refs[:k])):
            if which is not None and i not in which:
                continue
            cp = pltpu.make_async_remote_copy(src_ref=src, dst_ref=dst, send_sem=sends.at[i], recv_sem=recvs.at[i],
                                              device_id=to, device_id_type=MESH)
            cp.wait_send()
            cp.wait_recv()

    return pl.pallas_call(
        body, name=name,
        out_shape=tuple(pltpu.HBM(a.shape, a.dtype) for a in arrays),
        in_specs=[HBM] * k + [SEM, SEM, ANY], out_specs=(HBM,) * k,
        input_output_aliases={i: i for i in range(k)},
        compiler_params=pltpu.CompilerParams(has_side_effects=EFFECT),
    )(*arrays, send_sems, recv_sems, after)


def _plan_wout(refs):
    src, land = refs
    x, y, c, me = _place()
    return [(src, land.at[me], _peer(x, y, c, r)) for r in range(1, NDEV)]


def _plan_rest(refs):
    gob, gpf, r_out, r_pool = refs
    x, y, c, me = _place()
    plan = []
    for r in range(1, NDEV):
        plan.append((gob.at[me ^ r], r_out.at[r - 1], _peer(x, y, c, r)))
        plan.append((gpf.at[me ^ r], r_pool.at[r - 1], _peer(x, y, c, r)))
    return plan


def _plan_in(refs):
    sum_b, r_in = refs
    x, y, c, _ = _place()
    plan = []
    for q in range(4):
        rows = pl.ds(q * (D // 4), D // 4)
        for j, (dx, dy) in enumerate(((1, 0), (0, 1), (1, 1))):
            px, py = x ^ dx, y ^ dy
            plan.append((sum_b.at[2 * px + py, :, rows], r_in.at[j, :, rows], (px, py, c)))
    return plan


def _plan_in_tiles(refs):
    sums, landing = refs
    x, y, c, _ = _place()
    plan = []
    for rel, (dx, dy) in enumerate(((1, 0), (0, 1), (1, 1))):
        for p in range(3):
            plan.append((sums.at[rel, p], landing.at[rel, p], (x ^ dx, y ^ dy, c)))
    return plan


def _plan_small(refs):
    small, land = refs
    x, y, c, me = _place()
    return [(small, land.at[me], _peer(x, y, c, r)) for r in range(1, NDEV)]


def _seg_tiles(s):
    return (s + 2) % NSEG


_POOL_SPECS = [pl.BlockSpec((None, T, GROUP), lambda g, base=base: (base + g, 0, 0)) for base in (0, 4)]
_HEAD_SPECS = [pl.BlockSpec((None, T, HEAD), lambda h, base=base: (base + h // 2, 0, h % 2))
               for base in (8, 12, 16, 20)]


def _row_ids(shape):
    return lax.broadcasted_iota(jnp.int32, shape, 0)


BAND_ROWS = 128
HALO = 16


def _window_sum(a, gidx, lead):
    width = lax.shift_left(jnp.int32(2), gidx)
    shape = (BAND_ROWS, BAND_ROWS + HALO)
    t, j = lax.broadcasted_iota(jnp.int32, shape, 0), lax.broadcasted_iota(jnp.int32, shape, 1)
    first = t if lead else t + HALO - width + 1
    band = _bf(jnp.where(j >= first, jnp.where(j < first + width, 1.0, 0.0), 0.0))
    zeros = jnp.zeros((HALO, a.shape[1]), jnp.bfloat16)
    padded = [jnp.concatenate([p, zeros] if lead else [zeros, p], axis=0) for p in _split2(a)]
    out = []
    for r0 in range(0, T, BAND_ROWS):
        slab = jnp.concatenate([p[r0:r0 + BAND_ROWS + HALO] for p in padded], axis=1)
        r = _dot(band, slab, 1, 0)
        out.append(r[:, :a.shape[1]] + r[:, a.shape[1]:])
    return jnp.concatenate(out, axis=0)


def _window_mean(s, gidx):
    inv = jnp.where(gidx == 0, 0.5, jnp.where(gidx == 1, 0.25, jnp.where(gidx == 2, 0.125, 0.0625)))
    width = lax.shift_left(jnp.int32(2), gidx)
    head = s[:16] / jnp.minimum(_row_ids((16, s.shape[1])) + 1, width).astype(jnp.float32)
    return jnp.concatenate([head, s[16:] * inv], axis=0)


def _pool_fwd(proj, pool_w, pool_scale, token):
    def body(u_ref, pg_ref, w_ref, sc_ref, token_any, y_ref):
        del token_any
        gidx = pl.program_id(0)
        u, pg = u_ref[...], pg_ref[...]
        d = _window_mean(_window_sum(u, gidx, False), gidx) - u
        mixed = _dot(_bf(d), w_ref[...], 1, 0)
        y_ref[...] = _bf(mixed * sc_ref[...] * (pg * _sigmoid(pg)))

    return pl.pallas_call(
        body, name="pool_fwd", grid=(NGROUP,),
        in_specs=[*_POOL_SPECS,
                  pl.BlockSpec((None, GROUP, GROUP), lambda g: (g, 0, 0)),
                  pl.BlockSpec((1, GROUP), lambda g: (0, g)), ANY],
        out_specs=pl.BlockSpec((T, GROUP), lambda g: (0, g)),
        out_shape=pltpu.HBM((T, DMIX), jnp.bfloat16),
        compiler_params=_params(("parallel",)),
    )(proj, proj, pool_w, pool_scale, token)


def _tri(lower):
    r = lax.broadcasted_iota(jnp.int32, (CHUNK, CHUNK), 0)
    c = lax.broadcasted_iota(jnp.int32, (CHUNK, CHUNK), 1)
    return (r >= c) if lower else (r <= c)


def _sum_rows_matrix():
    shape = (CHUNK + 16, CHUNK)
    r, c = lax.broadcasted_iota(jnp.int32, shape, 0), lax.broadcasted_iota(jnp.int32, shape, 1)
    run = jnp.where(c <= r, 1.0, 0.0)
    half = jnp.where(c < CHUNK // 2, 1.0, 0.0)
    return _bf(jnp.where(r < CHUNK, run, jnp.where(r < CHUNK + 8, 1.0, half)))


def _rev_sum_matrix():
    shape = (CHUNK, 2 * CHUNK)
    r, c = lax.broadcasted_iota(jnp.int32, shape, 0), lax.broadcasted_iota(jnp.int32, shape, 1)
    return _bf(jnp.where(c < CHUNK, jnp.where(c >= r, 1.0, 0.0), jnp.where(c - CHUNK < r, 1.0, 0.0)))


def _split2(a):
    hi = _bf(a)
    return [hi, _bf(a - hi.astype(jnp.float32))]


def _exact_sums(mat, pieces):
    x = jnp.concatenate([s for p in pieces for s in _split2(p)], axis=1)
    r = _dot(mat, x, 1, 0)
    return [r[:, 2 * j * HEAD:(2 * j + 1) * HEAD] + r[:, (2 * j + 1) * HEAD:(2 * j + 2) * HEAD]
            for j in range(len(pieces))]


def _gates(qv, fl, lb):
    sq = _sigmoid(qv)
    sg = _sigmoid(fl)
    f = lb + (1.0 - lb) * sg
    return dict(sq=sq, qs=qv * sq, sg=sg, f=f, kk=1.0 - f, g=jnp.log2(f))


def _decays(sums):
    big_g = sums[:CHUNK]
    total = sums[CHUNK:CHUNK + 8]
    g_last = jnp.tile(total, (CHUNK // 8, 1))
    g_mid = jnp.tile(sums[CHUNK + 8:], (CHUNK // 8, 1))
    return dict(
        e_q=jnp.exp2(big_g),
        e_k=jnp.exp2(g_last - big_g),
        e_qm=jnp.exp2(jnp.minimum(big_g - g_mid, EXP_CAP)),
        e_km=jnp.exp2(jnp.minimum(g_mid - big_g, EXP_CAP)),
        total8=jnp.exp2(total))


def _group_rows(gi):
    return [pl.ds(pl.multiple_of((gi * NB + j) * CHUNK, CHUNK), CHUNK) for j in range(NB)]


def _lower_bound(lb_ref):
    return _sigmoid(lb_ref[0:1, :] - lb_ref[1:2, :])


def _hgrn_fwd(proj, lb_logits, rec_g, y_in):
    def body(q_ref, f_ref, i_ref, gate_ref, lb_ref, rg_ref, y_any, y_ref, o_ref, st_ref):
        del y_any
        lb = _lower_bound(lb_ref)
        causal = _tri(True)
        smat = _sum_rows_matrix()

        def group(gi, st):
            rows = _group_rows(gi)
            ts = [_gates(q_ref[r, :], f_ref[r, :], lb) for r in rows]
            ds = [_decays(s) for s in _exact_sums(smat, [t["g"] for t in ts])]
            vs = [_bf(i_ref[r, :]) for r in rows]
            q_m = [_bf(t["qs"] * d["e_qm"]) for t, d in zip(ts, ds)]
            k_m = [_bf(t["kk"] * d["e_km"]) for t, d in zip(ts, ds)]
            q_e = [_bf(t["qs"] * d["e_q"]) for t, d in zip(ts, ds)]
            k_e = [_bf(t["kk"] * d["e_k"]) for t, d in zip(ts, ds)]
            a = [_bf(jnp.where(causal, _dot(q_m[j], k_m[j], 1, 1), 0.0)) for j in range(NB)]
            intra = [_dot(a[j], vs[j], 1, 0) for j in range(NB)]
            upd = [_dot(vs[j], k_e[j], 0, 0) for j in range(NB)]
            for j in range(NB):
                st_ref[gi * NB + j] = st
                o_ref[rows[j], :] = intra[j] + _dot(q_e[j], _bf(st), 1, 1)
                st = st * jnp.tile(ds[j]["total8"], (HEAD // 8, 1)) + upd[j]
            return st

        lax.fori_loop(0, NGRP, group, jnp.zeros((HEAD, HEAD), jnp.float32))
        o = o_ref[...]
        rn = o * lax.rsqrt(jnp.mean(o * o, axis=-1, keepdims=True) + EPS)
        gate = gate_ref[...]
        y_ref[...] = _bf(rn * rg_ref[...] * (gate * _sigmoid(gate)))

    return pl.pallas_call(
        body, name="hgrn_fwd", grid=(NHEAD,),
        in_specs=[*_HEAD_SPECS,
                  pl.BlockSpec((2, HEAD), lambda h: (0, h)),
                  pl.BlockSpec((1, HEAD), lambda h: (0, h)),
                  pl.BlockSpec(memory_space=pl.ANY)],
        out_specs=(pl.BlockSpec((T, HEAD), lambda h: (0, NHEAD + h)),
                   pl.BlockSpec((T, HEAD), lambda h: (0, h)),
                   pl.BlockSpec((None, NCHUNK, HEAD, HEAD), lambda h: (h, 0, 0, 0))),
        out_shape=(pltpu.HBM((T, DMIX), jnp.bfloat16), pltpu.HBM((T, D), jnp.float32),
                   pltpu.HBM((NHEAD, NCHUNK, HEAD, HEAD), jnp.float32)),
        input_output_aliases={6: 0},
        compiler_params=_params(("parallel",)),
    )(proj, proj, proj, proj, lb_logits, rec_g, y_in)


def _out_proj_loss(x, y, w_out, target, gf):
    rows = 512
    parts = [slice(k * rows // 2, (k + 1) * rows // 2) for k in range(2)]

    def body(x_ref, y_ref, w_ref, t_ref, g_ref, dz_ref, dzb_ref, sq_ref, dg_ref):
        zs = [x_ref[p, :] + _dot(y_ref[p, :], w_ref[...], 1, 0) for p in parts]
        sq = dg = 0.0
        for p, z in zip(parts, zs):
            r = lax.rsqrt(jnp.mean(z * z, axis=-1, keepdims=True) + EPS)
            zhat = z * r
            err = zhat * g_ref[...] - t_ref[p, :]
            dy = err * (1.0 / D)
            gdy = dy * g_ref[...]
            dz = r * (gdy - zhat * jnp.mean(zhat * gdy, axis=-1, keepdims=True))
            dz_ref[p, :] = dz
            dzb_ref[p, :] = _bf(dz)
            sq = sq + jnp.sum(err * err, axis=0, keepdims=True)
            dg = dg + jnp.sum(zhat * dy, axis=0, keepdims=True)

        @pl.when(pl.program_id(0) == 0)
        def _():
            sq_ref[...] = sq
            dg_ref[...] = dg

        @pl.when(pl.program_id(0) != 0)
        def _():
            sq_ref[...] += sq
            dg_ref[...] += dg

    tile = pl.BlockSpec((rows, D), lambda i: (i, 0))
    vec = pl.BlockSpec((1, D), lambda i: (0, 0))
    return pl.pallas_call(
        body, name="out_proj_loss", grid=(T // rows,),
        in_specs=[tile, pl.BlockSpec((rows, DMIX), lambda i: (i, 0)), pl.BlockSpec((DMIX, D), lambda i: (0, 0)),
                  tile, vec],
        out_specs=(tile, tile, vec, vec),
        out_shape=(pltpu.HBM((T, D), jnp.float32), pltpu.HBM((T, D), jnp.bfloat16),
                   jax.ShapeDtypeStruct((1, D), jnp.float32), jax.ShapeDtypeStruct((1, D), jnp.float32)),
        compiler_params=_params(("arbitrary",)),
    )(x, y, w_out, target, gf)


def _out_proj_bwd(dzb, w_out, y):
    tn = 512

    def body(dz_ref, w_ref, y_ref, dy_ref, gw_ref, gwb_ref):
        dz = dz_ref[...]
        dy_ref[...] = _dot(dz, w_ref[...], 1, 1)
        gw = _dot(y_ref[...], dz, 0, 0)
        gw_ref[...] = gw
        gwb_ref[...] = _bf(gw)

    return pl.pallas_call(
        body, name="out_proj_bwd", grid=(DMIX // tn,),
        in_specs=[pl.BlockSpec((T, D), lambda n: (0, 0)), pl.BlockSpec((tn, D), lambda n: (n, 0)),
                  pl.BlockSpec((T, tn), lambda n: (0, n))],
        out_specs=(pl.BlockSpec((T, tn), lambda n: (0, n)), pl.BlockSpec((tn, D), lambda n: (n, 0)),
                   pl.BlockSpec((tn, D), lambda n: (n, 0))),
        out_shape=(pltpu.HBM((T, DMIX), jnp.float32), pltpu.HBM((DMIX, D), jnp.float32),
                   pltpu.HBM((DMIX, D), jnp.bfloat16)),
        compiler_params=_params(("parallel",)),
    )(dzb, w_out, y)


def _hgrn_bwd(proj, lb_logits, rec_g, o, states, dymix, dproj_in, token):
    def body(q_ref, f_ref, i_ref, gate_ref, lb_ref, rg_ref, o_ref, st_ref, dy_ref, dp_any, token_any,
             dp_ref, drg_ref, dlb_ref, do_ref):
        del dp_any, token_any
        lb = _lower_bound(lb_ref)
        causal = _tri(True)
        smat, rmat = _sum_rows_matrix(), _rev_sum_matrix()

        o = o_ref[...]
        rs = lax.rsqrt(jnp.mean(o * o, axis=-1, keepdims=True) + EPS)
        rn = o * rs
        gate = gate_ref[...]
        sgate = _sigmoid(gate)
        dyv = dy_ref[...]
        d_r = dyv * (gate * sgate)
        dp_ref[3] = _bf(dyv * (rn * rg_ref[...]) * (sgate * (1.0 + gate * (1.0 - sgate))))
        drg_ref[...] = jnp.sum(d_r * rn, axis=0, keepdims=True)
        drn = d_r * rg_ref[...]
        do_ref[...] = rs * (drn - rn * jnp.mean(rn * drn, axis=-1, keepdims=True))

        def group(i, carry):
            dst, dlb = carry
            gi = NGRP - 1 - i
            rows = _group_rows(gi)
            span = range(NB)
            qvs = [q_ref[r, :] for r in rows]
            ts = [_gates(qv, f_ref[r, :], lb) for qv, r in zip(qvs, rows)]
            ds = [_decays(s) for s in _exact_sums(smat, [t["g"] for t in ts])]
            vs = [_bf(i_ref[r, :]) for r in rows]
            dos = [_bf(do_ref[r, :]) for r in rows]
            sts = [st_ref[gi * NB + j] for j in span]
            qe_f = [t["qs"] * d["e_q"] for t, d in zip(ts, ds)]
            ke_f = [t["kk"] * d["e_k"] for t, d in zip(ts, ds)]
            q_e, k_e = [_bf(a) for a in qe_f], [_bf(a) for a in ke_f]
            q_m = [_bf(t["qs"] * d["e_qm"]) for t, d in zip(ts, ds)]
            k_m = [_bf(t["kk"] * d["e_km"]) for t, d in zip(ts, ds)]
            a = [_bf(jnp.where(causal, _dot(q_m[j], k_m[j], 1, 1), 0.0)) for j in span]
            da = [_bf(jnp.where(causal, _dot(dos[j], vs[j], 1, 1), 0.0)) for j in span]
            dqm = [_dot(da[j], k_m[j], 1, 0) for j in span]
            dkm = [_dot(da[j], q_m[j], 0, 0) for j in span]
            dv_in = [_dot(a[j], dos[j], 0, 0) for j in span]
            dqe = [_dot(dos[j], _bf(sts[j]), 1, 0) for j in span]
            grow = [_dot(dos[j], q_e[j], 0, 0) for j in span]
            dke, carried = [None] * NB, [None] * NB
            for j in reversed(span):
                dst_b = _bf(dst)
                dke[j] = _dot(vs[j], dst_b, 1, 0)
                dp_ref[2, rows[j], :] = _bf(dv_in[j] + _dot(k_e[j], dst_b, 1, 1))
                carried[j] = ds[j]["total8"] * jnp.sum(dst * sts[j], axis=0, keepdims=True)
                dst = dst * jnp.tile(ds[j]["total8"], (HEAD // 8, 1)) + grow[j]
            kdk = [ke_f[j] * dke[j] for j in span]
            pos = [(q_m[j].astype(jnp.float32) * dqm[j] - k_m[j].astype(jnp.float32) * dkm[j]) + qe_f[j] * dqe[j]
                   for j in span]
            dgs = _exact_sums(rmat, [jnp.concatenate([pos[j], kdk[j]], axis=0) for j in span])
            for j in span:
                t, d = ts[j], ds[j]
                dg = dgs[j] + jnp.tile(carried[j], (CHUNK // 8, 1))
                dqs = dqm[j] * d["e_qm"] + dqe[j] * d["e_q"]
                dkk = dkm[j] * d["e_km"] + dke[j] * d["e_k"]
                df = dg / t["f"] - dkk
                dp_ref[1, rows[j], :] = _bf(df * (1.0 - lb) * (t["sg"] * (1.0 - t["sg"])))
                dp_ref[0, rows[j], :] = _bf(dqs * (t["sq"] * (1.0 + qvs[j] * (1.0 - t["sq"]))))
                dlb = dlb + df * (1.0 - t["sg"])
            return dst, dlb

        _, dlb = lax.fori_loop(0, NGRP, group, (jnp.zeros((HEAD, HEAD), jnp.float32),
                                                jnp.zeros((CHUNK, HEAD), jnp.float32)))
        dlb_ref[...] = jnp.sum(dlb, axis=0, keepdims=True)

    vec = pl.BlockSpec((1, HEAD), lambda h: (0, h))
    return pl.pallas_call(
        body, name="hgrn_bwd", grid=(NHEAD,),
        in_specs=[*_HEAD_SPECS,
                  pl.BlockSpec((2, HEAD), lambda h: (0, h)), vec,
                  pl.BlockSpec((T, HEAD), lambda h: (0, h)),
                  pl.BlockSpec((None, NCHUNK, HEAD, HEAD), lambda h: (h, 0, 0, 0)),
                  pl.BlockSpec((T, HEAD), lambda h: (0, NHEAD + h)), ANY, ANY],
        out_specs=(pl.BlockSpec((4, T, HEAD), lambda h: (0, 0, h)), vec, vec),
        out_shape=(pltpu.HBM((NSEG, T, D), jnp.bfloat16),
                   jax.ShapeDtypeStruct((1, D), jnp.float32), jax.ShapeDtypeStruct((1, D), jnp.float32)),
        scratch_shapes=[pltpu.VMEM((T, HEAD), jnp.float32)],
        input_output_aliases={9: 0},
        compiler_params=_params(("parallel",)),
    )(proj, proj, proj, proj, lb_logits, rec_g, o, states, dymix, dproj_in, token)


def _pool_bwd(proj, pool_w, pool_scale, dymix):
    def body(u_ref, pg_ref, w_ref, sc_ref, dy_ref, dp_ref, gw_ref, gs_ref):
        gidx = pl.program_id(0)
        u, pg = u_ref[...], pg_ref[...]
        d = _bf(_window_mean(_window_sum(u, gidx, False), gidx) - u)
        mixed = _dot(d, w_ref[...], 1, 0)
        spg = _sigmoid(pg)
        dyv = dy_ref[...]
        d_p = dyv * (pg * spg)
        dp_ref[1] = _bf(dyv * (mixed * sc_ref[...]) * (spg * (1.0 + pg * (1.0 - spg))))
        gs_ref[...] = jnp.sum(d_p * mixed, axis=0, keepdims=True)
        dmixed = _bf(d_p * sc_ref[...])
        gw_ref[...] = _dot(d, dmixed, 0, 0)
        dd = _dot(dmixed, w_ref[...], 1, 1)
        dp_ref[0] = _bf(_window_sum(_window_mean(dd, gidx), gidx, True) - dd)

    return pl.pallas_call(
        body, name="pool_bwd", grid=(NGROUP,),
        in_specs=[*_POOL_SPECS,
                  pl.BlockSpec((None, GROUP, GROUP), lambda g: (g, 0, 0)),
                  pl.BlockSpec((1, GROUP), lambda g: (0, g)),
                  pl.BlockSpec((T, GROUP), lambda g: (0, g))],
        out_specs=(pl.BlockSpec((2, T, GROUP), lambda g: (2, 0, g)),
                   pl.BlockSpec((None, GROUP, GROUP), lambda g: (g, 0, 0)),
                   pl.BlockSpec((1, GROUP), lambda g: (0, g))),
        out_shape=(pltpu.HBM((NSEG, T, D), jnp.bfloat16),
                   jax.ShapeDtypeStruct((NGROUP, GROUP, GROUP), jnp.float32),
                   jax.ShapeDtypeStruct((1, D), jnp.float32)),
        compiler_params=_params(("parallel",)),
    )(proj, proj, pool_w, pool_scale, dymix)


HALF = NTILE // 2
AWAY = HALF - 3


def _dproj_tile(chip, side, p):
    j = 6 * chip + 3 * side + p
    return ((j // 4 + 4) % NSEG, 0, j % 4)


def _sibling_copy(sib_out, sib_in, send_sems, recv_sems, slot):
    x, y, c, _ = _place()
    return pltpu.make_async_remote_copy(
        src_ref=sib_out.at[slot], dst_ref=sib_in.at[slot], send_sem=send_sems.at[slot],
        recv_sem=recv_sems.at[slot], device_id=(x, y, 1 - c), device_id_type=MESH)


def _proj_bwd_w_far(place, ht, dproj):
    def body(place_ref, h_ref, dp_ref, sib_out, sib_in, send_sems, recv_sems, stage, loc_sems):
        del place_ref
        i = pl.program_id(0)

        def to_hbm(k):
            return pltpu.make_async_copy(stage.at[k], sib_out.at[k], loc_sems.at[k])

        def send(k):
            to_hbm(k).wait()
            _sibling_copy(sib_out, sib_in, send_sems, recv_sems, k).start()

        stage[i] = _bf(_dot(h_ref[...], dp_ref[...], 1, 0))

        @pl.when(i > 0)
        def _():
            send(i - 1)

        to_hbm(i).start()

        @pl.when(i == HALF - 1)
        def _():
            send(i)

    buf = pltpu.HBM((HALF, D, TILE), jnp.bfloat16)
    sems = pltpu.SemaphoreType.DMA((HALF,))
    return pl.pallas_call(
        body, name="proj_bwd_w_far",
        grid_spec=pltpu.PrefetchScalarGridSpec(
            num_scalar_prefetch=1, grid=(HALF,),
            in_specs=[pl.BlockSpec((D, T), lambda i, pr: (0, 0)),
                      pl.BlockSpec((None, T, TILE), lambda i, pr: _dproj_tile(i // 3, 1 - pr[2], i % 3))],
            out_specs=(HBM, HBM, SEM, SEM),
            scratch_shapes=[pltpu.VMEM((HALF, D, TILE), jnp.bfloat16), pltpu.SemaphoreType.DMA((HALF,))]),
        out_shape=(buf, buf, sems, sems),
        compiler_params=pltpu.CompilerParams(dimension_semantics=("arbitrary",), vmem_limit_bytes=48 * MIB,
                                             has_side_effects=EFFECT),
    )(place, ht, dproj)


def _proj_bwd_w_near(place, ht, dproj, sib_out, sib_in, sib_send, sib_recv):
    def owner_chip(k, pr):
        return jnp.where(k < AWAY, (pr[1] + 1 + k % 3) % 4, pr[1])

    def tile_p(k):
        return jnp.where(k < AWAY, k // 3, k - AWAY)

    def body(place_ref, h_ref, dp_ref, sib_out, sib_in, sib_send, sib_recv, sums, own_ref, landing, out_send,
             out_recv, recvbuf, outbuf, in_sems, loc_sems):
        i = pl.program_id(0)
        px, py, c, _ = _place()

        def slot_of(k):
            return 3 * owner_chip(k, place_ref) + tile_p(k)

        def load(k):
            return pltpu.make_async_copy(sib_in.at[slot_of(k)], recvbuf.at[k % 2], in_sems.at[k % 2])

        def fetch(k):
            _sibling_copy(sib_out, sib_in, sib_send, sib_recv, slot_of(k)).wait_recv()
            load(k).start()

        def route(k):
            chip = owner_chip(k, place_ref)
            cx, cy = chip // 2, chip % 2
            return cx, cy, (cx ^ px) + 2 * (cy ^ py) - 1, tile_p(k)

        def to_hbm(k):
            _, _, rel, p = route(k)
            return pltpu.make_async_copy(outbuf.at[k], sums.at[rel, p], loc_sems.at[k])

        def to_owner(k):
            cx, cy, rel, p = route(k)
            return pltpu.make_async_remote_copy(
                src_ref=sums.at[rel, p], dst_ref=landing.at[rel, p], send_sem=out_send.at[3 * rel + p],
                recv_sem=out_recv.at[3 * rel + p], device_id=(cx, cy, c), device_id_type=MESH)

        @pl.when(i == 0)
        def _():
            fetch(i)

        @pl.when(i < HALF - 1)
        def _():
            fetch(i + 1)

        gw = _dot(h_ref[...], dp_ref[...], 1, 0)

        @pl.when(jnp.logical_and(i > 0, i <= AWAY))
        def _():
            to_hbm(i - 1).wait()

        load(i).wait()
        total = gw + recvbuf[i % 2].astype(jnp.float32)
        own_ref[...] = total

        @pl.when(i < AWAY)
        def _():
            outbuf[i] = _bf(total)
            to_hbm(i).start()

        @pl.when(i == HALF - 1)
        def _():
            for k in range(AWAY):
                to_owner(k).start()
            for slot in range(HALF):
                _sibling_copy(sib_out, sib_in, sib_send, sib_recv, slot).wait_send()

    travelling = pltpu.HBM((3, 3, D, TILE), jnp.bfloat16)
    sems = pltpu.SemaphoreType.DMA((AWAY,))
    return pl.pallas_call(
        body, name="proj_bwd_w_near",
        grid_spec=pltpu.PrefetchScalarGridSpec(
            num_scalar_prefetch=1, grid=(HALF,),
            in_specs=[pl.BlockSpec((D, T), lambda i, pr: (0, 0)),
                      pl.BlockSpec((None, T, TILE), lambda i, pr: _dproj_tile(owner_chip(i, pr), pr[2], tile_p(i))),
                      HBM, HBM, SEM, SEM],
            out_specs=(HBM, pl.BlockSpec((None, D, TILE), lambda i, pr: (jnp.where(i < AWAY, 0, i % 3), 0, 0)),
                       HBM, SEM, SEM),
            scratch_shapes=[pltpu.VMEM((2, D, TILE), jnp.bfloat16), pltpu.VMEM((AWAY, D, TILE), jnp.bfloat16),
                            pltpu.SemaphoreType.DMA((2,)), pltpu.SemaphoreType.DMA((AWAY,))]),
        out_shape=(travelling, pltpu.HBM((3, D, TILE), jnp.float32), travelling, sems, sems),
        compiler_params=pltpu.CompilerParams(dimension_semantics=("arbitrary",), vmem_limit_bytes=48 * MIB,
                                             has_side_effects=EFFECT),
    )(place, ht, dproj, sib_out, sib_in, sib_send, sib_recv)


def _proj_bwd_w(place, ht, dproj):
    half = NTILE // 2

    def owner_chip(i, pr):
        return jnp.where(i < half, i // 3, (pr[1] + 1 + (i - half) // 3) % 4)

    def tile_of(i, pr):
        side = jnp.where(i < half, 1 - pr[2], pr[2])
        return 6 * owner_chip(i, pr) + 3 * side + i % 3

    def dproj_block(i, pr):
        j = tile_of(i, pr)
        return ((j // 4 + 4) % NSEG, 0, j % 4)

    def mine(i):
        return jnp.maximum(i, half)

    def body(place_ref, h_ref, dp_ref, sum_ref, own_ref, sendbuf, recvbuf, send_sems, recv_sems):
        i = pl.program_id(0)
        px, py, c, _ = _place()
        gw = _dot(h_ref[...], dp_ref[...], 1, 0)

        def to_sibling(slot):
            return pltpu.make_async_remote_copy(
                src_ref=sendbuf.at[slot], dst_ref=recvbuf.at[slot], send_sem=send_sems.at[slot],
                recv_sem=recv_sems.at[slot], device_id=(px, py, 1 - c), device_id_type=MESH)

        @pl.when(i < half)
        def _():
            sendbuf[i] = _bf(gw)
            to_sibling(i).start()

        @pl.when(i >= half)
        def _():
            slot = 3 * owner_chip(i, place_ref) + i % 3
            to_sibling(slot).wait_recv()
            total = gw + recvbuf[slot].astype(jnp.float32)
            sum_ref[...] = _bf(total)
            own_ref[...] = total

        @pl.when(i == NTILE - 1)
        def _():
            for slot in range(half):
                to_sibling(slot).wait_send()

    return pl.pallas_call(
        body, name="proj_bwd_w",
        grid_spec=pltpu.PrefetchScalarGridSpec(
            num_scalar_prefetch=1, grid=(NTILE,),
            in_specs=[pl.BlockSpec((D, T), lambda i, pr: (0, 0)),
                      pl.BlockSpec((None, T, TILE), lambda i, pr: dproj_block(i, pr))],
            out_specs=(pl.BlockSpec((None, None, D, TILE), lambda i, pr: (owner_chip(mine(i), pr), mine(i) % 3, 0, 0)),
                       pl.BlockSpec((None, D, TILE), lambda i, pr: (jnp.where(i < NTILE - 3, 0, i % 3), 0, 0))),
            scratch_shapes=[pltpu.VMEM((half, D, TILE), jnp.bfloat16), pltpu.VMEM((half, D, TILE), jnp.bfloat16),
                            pltpu.SemaphoreType.DMA((half,)), pltpu.SemaphoreType.DMA((half,))]),
        out_shape=(pltpu.HBM((4, 3, D, TILE), jnp.bfloat16), pltpu.HBM((3, D, TILE), jnp.float32)),
        compiler_params=_params(("arbitrary",)),
    )(place, ht, dproj)


def _proj_bwd_x(dproj, w_t, x, g1, dz, token):
    tm = 512
    pairs = NSEG // 2

    def body(dp_ref, w_ref, x_ref, g_ref, dz_ref, token_any, dx_ref, dg_ref, wcat, acc):
        del token_any
        s, m = pl.program_id(0), pl.program_id(1)

        @pl.when(m == 0)
        def _():
            for i in range(8):
                wcat[:, i * TILE:(i + 1) * TILE] = w_ref[i]

        mine = pl.ds(pl.multiple_of(m * tm, tm), tm)

        @pl.when(s == 0)
        def _():
            acc[mine, :] = jnp.zeros((tm, D), jnp.float32)

        acc[mine, :] += _dot(jnp.concatenate([dp_ref[0], dp_ref[1]], axis=1), wcat[...], 1, 1)

        @pl.when(s == pairs - 1)
        def _():
            xv = x_ref[...]
            rs = lax.rsqrt(jnp.mean(xv * xv, axis=-1, keepdims=True) + EPS)
            xhat = xv * rs
            dhv = acc[mine, :]
            gdh = dhv * g_ref[...]
            dx_ref[...] = dz_ref[...] + rs * (gdh - xhat * jnp.mean(xhat * gdh, axis=-1, keepdims=True))
            dg = jnp.sum(xhat * dhv, axis=0, keepdims=True)

            @pl.when(m == 0)
            def _():
                dg_ref[...] = dg

            @pl.when(m != 0)
            def _():
                dg_ref[...] += dg

    rows = pl.BlockSpec((tm, D), lambda s, m: (jnp.where(s == pairs - 1, m, 0), 0))
    vec = pl.BlockSpec((1, D), lambda s, m: (0, 0))
    return pl.pallas_call(
        body, name="proj_bwd_x", grid=(pairs, T // tm),
        in_specs=[pl.BlockSpec((2, tm, D), lambda s, m: (s, m, 0)),
                  pl.BlockSpec((8, D, TILE), lambda s, m: ((s + 1) % pairs, 0, 0)), rows, vec, rows, ANY],
        out_specs=(rows, vec),
        out_shape=(jax.ShapeDtypeStruct((T, D), jnp.float32), jax.ShapeDtypeStruct((1, D), jnp.float32)),
        scratch_shapes=[pltpu.VMEM((D, 2 * D), jnp.bfloat16), pltpu.VMEM((T, D), jnp.float32)],
        compiler_params=_params(("arbitrary", "arbitrary"), vmem_mib=56),
    )(dproj, w_t, x, g1, dz, token)


def _adamw(w, g, m, v):
    m_new = ADAM_B1 * m + (1.0 - ADAM_B1) * g
    v_new = ADAM_B2 * v + (1.0 - ADAM_B2) * (g * g)
    delta = -ADAM_LR * ((m_new / BC1) / (jnp.sqrt(v_new / BC2) + ADAM_EPS) + ADAM_WD * w)
    return delta, m_new, v_new


def _reduce_adam(name, place, parts, w, m, v, grid, w_spec, into=()):
    n = len(parts)

    def body(place_ref, *refs):
        del place_ref
        w_ref, m_ref, v_ref = refs[n:n + 3]
        g_ref, d_ref, mo_ref, vo_ref = refs[-4:]
        g = None
        for ref, (_, _, stacked) in zip(refs[:n], parts):
            terms = [ref[r] for r in range(ref.shape[0])] if stacked else [ref[...]]
            for t in terms:
                if t.shape[-1] != w_ref.shape[-1]:
                    t = jnp.concatenate([t[p] for p in range(t.shape[0])], axis=1)
                g = t.astype(jnp.float32) if g is None else g + t.astype(jnp.float32)
        delta, m_new, v_new = _adamw(w_ref[...], g, m_ref[...], v_ref[...])
        g_ref[...] = g
        d_ref[...] = delta
        mo_ref[...] = m_new
        vo_ref[...] = v_new

    shape = jax.ShapeDtypeStruct(w.shape, jnp.float32)
    return pl.pallas_call(
        body, name=name,
        grid_spec=pltpu.PrefetchScalarGridSpec(
            num_scalar_prefetch=1, grid=grid,
            in_specs=[spec for _, spec, _ in parts] + [w_spec] * 3 + [ANY] * len(into), out_specs=(w_spec,) * 4),
        out_shape=(shape,) * 4,
        input_output_aliases={1 + n + 3 + j: j for j in range(len(into))},
        compiler_params=_params(("parallel",)),
    )(place, *[_in_hbm(a) for a in [a for a, _, _ in parts] + [w, m, v, *into]])


def _small_adam(place, own, parts, w, m, v):
    def body(place_ref, own_ref, p_ref, w_ref, m_ref, v_ref, g_ref, d_ref, mo_ref, vo_ref):
        me = place_ref[0]
        g = None
        for s in range(NDEV):
            term = jnp.where(me == s, own_ref[...], p_ref[s])
            g = term if g is None else g + term
        wv = w_ref[...]
        rows = _row_ids(wv.shape)
        other = jnp.where(rows == 2, pltpu.roll(wv, 7, 0), jnp.where(rows == 3, pltpu.roll(wv, 1, 0), 0.0))
        lbv = _sigmoid(wv - other)
        sign = jnp.where(rows == 2, 1.0, -1.0)
        g = jnp.where((rows == 2) | (rows == 3), sign * g * lbv * (1.0 - lbv), g)
        delta, m_new, v_new = _adamw(wv, g, m_ref[...], v_ref[...])
        g_ref[...] = g
        d_ref[...] = delta
        mo_ref[...] = m_new
        vo_ref[...] = v_new

    shape = jax.ShapeDtypeStruct((8, D), jnp.float32)
    vmem = pl.BlockSpec(memory_space=pltpu.VMEM)
    return pl.pallas_call(
        body, name="small_adam", out_shape=(shape,) * 4,
        in_specs=[pl.BlockSpec(memory_space=pltpu.SMEM)] + [vmem] * 5, out_specs=(vmem,) * 4,
    )(place, own, parts, w, m, v)


def _rows8(*vecs):
    rows = [a.reshape(-1, D) for a in vecs]
    n = sum(r.shape[0] for r in rows)
    return jnp.concatenate(rows + [jnp.zeros((8 - n, D), jnp.float32)], axis=0)


def kernel(x, norm1_g, w_in, pool_w, pool_scale, lb_logits, rec_norm_g, w_out, final_norm_g, loss_target, m_norm1_g, m_w_in, m_pool_w, m_pool_scale, m_lb_logits, m_rec_norm_g, m_w_out, m_final_norm_g, v_norm1_g, v_w_in, v_pool_w, v_pool_scale, v_lb_logits, v_rec_norm_g, v_w_out, v_final_norm_g):
    xs = x[0]
    target = loss_target[0]
    ix, iy, ic = lax.axis_index("x"), lax.axis_index("y"), lax.axis_index("c")
    place = jnp.stack([4 * ix + 2 * iy + ic, 2 * ix + iy, ic]).astype(jnp.int32)
    gf = final_norm_g.reshape(1, D)

    ht, w_t, w_out_b, w_out_g, pool_g, proj = _gather_proj(xs, norm1_g, w_in, w_out, pool_w)
    pool_full = pool_g.transpose(1, 0, 2, 3).reshape(NGROUP, GROUP, GROUP)
    wout = [w_out_b, w_out_g]
    wout_send, wout_recv, wout, wout_token = _split_start("gather_wout_start", wout, NDEV - 1, _plan_wout)

    y = _pool_fwd(proj, pool_full, pool_scale, wout_token)
    y, o, states = _hgrn_fwd(proj, lb_logits, rec_norm_g, y)
    _, w_out_g = _split_wait("gather_wout_wait", wout, wout_send, wout_recv, _plan_wout, o)
    w_out_full = _in_hbm(w_out_g.reshape(DMIX, D))
    dz, dzb, sq, dgf = _out_proj_loss(xs, y, w_out_full, target, gf)

    dymix, gwout_f, gwout_b = _out_proj_bwd(dzb, w_out_full, y)
    dproj, gpool, dscale = _pool_bwd(proj, pool_full, pool_scale, dymix)

    blk_out = (NDEV, DMIX // NDEV, D)
    blk_pool = (NDEV, NGROUP, GROUP // NDEV, GROUP)
    gpool_s = gpool.reshape(NGROUP, NDEV, GROUP // NDEV, GROUP).transpose(1, 0, 2, 3)
    rest = [gwout_b.reshape(blk_out), gpool_s,
            lax.empty((NDEV - 1,) + blk_out[1:], jnp.bfloat16), lax.empty((NDEV - 1,) + blk_pool[1:], jnp.float32)]
    rest_send, rest_recv, rest, rest_token = _split_start("scatter_rest_start", rest, 2 * (NDEV - 1), _plan_rest)

    dproj, drecg, dlb = _hgrn_bwd(proj, lb_logits, rec_norm_g, o, states, dymix, dproj, rest_token)
    chip_sums, own_sum, landing, win_send, win_recv = _proj_bwd_w_near(
        place, ht, dproj, *_proj_bwd_w_far(place, ht, dproj))
    win = [chip_sums, landing]

    grad_x, dg1 = _proj_bwd_x(dproj, w_t, xs, norm1_g, dz, chip_sums)

    small = [_rows8(dg1, dscale, dlb, dlb, drecg, dgf, sq), lax.empty((NDEV, 8, D), jnp.float32)]
    small_send, small_recv, small, small_token = _split_start("gather_small_start", small, NDEV - 1, _plan_small)

    _, gpool_own, r_out, r_pool = _split_wait("scatter_rest_wait", rest, rest_send, rest_recv, _plan_rest,
                                              small_token)
    g_wout, d_wout, m_wout, v_wout = _reduce_adam(
        "adam_w_out", place,
        [(gwout_f.reshape(blk_out), pl.BlockSpec((None,) + blk_out[1:], lambda i, pr: (pr[0], 0, 0)), False),
         (r_out, pl.BlockSpec((NDEV - 1,) + blk_out[1:], lambda i, pr: (0, 0, 0)), True)],
        w_out, m_w_out, v_w_out, (1,), pl.BlockSpec((None,) + blk_out[1:], lambda i, pr: (0, 0, 0)))
    g_pool, d_pool, m_pool, v_pool = _reduce_adam(
        "adam_pool_w", place,
        [(gpool_own, pl.BlockSpec((None,) + blk_pool[1:], lambda i, pr: (pr[0], 0, 0, 0)), False),
         (r_pool, pl.BlockSpec((NDEV - 1,) + blk_pool[1:], lambda i, pr: (0, 0, 0, 0)), True)],
        pool_w, m_pool_w, v_pool_w, (1,), pl.BlockSpec((None,) + blk_pool[1:], lambda i, pr: (0, 0, 0, 0)))

    def adam_w_in(name, r_in, first, count, into=()):
        return _reduce_adam(
            name, place,
            [(own_sum, pl.BlockSpec((3, D // 8, TILE), lambda i, pr: (0, first + i, 0)), False),
             (r_in, pl.BlockSpec((3, 3, D // 8, TILE), lambda i, pr: (0, 0, first + i, 0)), True)],
            w_in, m_w_in, v_w_in, (count,), pl.BlockSpec((None, D // 8, 3 * TILE), lambda i, pr: (0, first + i, 0)),
            into)

    _, r_in = _split_wait("scatter_win_wait", win, win_send, win_recv, _plan_in_tiles, d_pool)
    g_win, d_win, m_win, v_win = adam_w_in("adam_w_in", r_in, 0, 8)

    own_small, r_small = _split_wait("gather_small_wait", small, small_send, small_recv, _plan_small, d_win)
    g_s, d_s, m_s, v_s = _small_adam(
        place, own_small, r_small,
        _rows8(norm1_g, pool_scale, lb_logits, rec_norm_g, final_norm_g),
        _rows8(m_norm1_g, m_pool_scale, m_lb_logits, m_rec_norm_g, m_final_norm_g),
        _rows8(v_norm1_g, v_pool_scale, v_lb_logits, v_rec_norm_g, v_final_norm_g))
    loss = jnp.sum(g_s[6]) * (0.5 / D)

    def small_outs(a):
        return a[0:1], a[1:2], a[2:4], a[4:5], a[5]

    def outs(small_a, win, pool, wout):
        n1, ps, lbl, rg, fg = small_outs(small_a)
        return n1, win, pool, ps, lbl, rg, wout, fg

    return (loss, grad_x[None],
            *outs(g_s, g_win, g_pool, g_wout), *outs(d_s, d_win, d_pool, d_wout),
            *outs(m_s, m_win, m_pool, m_wout), *outs(v_s, v_win, v_pool, v_wout))
```

```python
import functools

import jax
import jax.numpy as jnp
from jax import lax
from jax.experimental import pallas as pl
from jax.experimental.pallas import tpu as pltpu

T = 2048
D = 1024
NSEG = 6
NTILE = 24
TILE = 256
DMIX = 2048
NDEV = 8
HEAD = 128
NHEAD = 8
CHUNK = 64
NCHUNK = T // CHUNK
NB = 32
NGRP = NCHUNK // NB
NGROUP = 4
GROUP = 256
EPS = 1e-6
EXP_CAP = 115.0
MESH = pl.DeviceIdType.MESH
AXES = ("x", "y", "c")
ANY = pl.BlockSpec(memory_space=pl.ANY)
HBM = pl.BlockSpec(memory_space=pltpu.HBM)
SEM = pl.BlockSpec(memory_space=pltpu.SEMAPHORE)
EFFECT = pltpu.SideEffectType.DATAFLOW_SIDE_EFFECTING

ADAM_LR = 0.001
ADAM_B1 = 0.9
ADAM_B2 = 0.999
ADAM_EPS = 1e-08
ADAM_WD = 0.01
ADAM_STEP = 10
BC1 = 1.0 - ADAM_B1 ** ADAM_STEP
BC2 = 1.0 - ADAM_B2 ** ADAM_STEP

MIB = 1 << 20


def _params(sem=None, vmem_mib=48):
    return pltpu.CompilerParams(dimension_semantics=sem, vmem_limit_bytes=vmem_mib * MIB)


def _sigmoid(v):
    return 1.0 / (1.0 + jnp.exp(-v))


def _dot(a, b, ca, cb, precision=None):
    return lax.dot_general(a, b, (((ca,), (cb,)), ((), ())), precision=precision,
                           preferred_element_type=jnp.float32)


def _bf(v):
    return v.astype(jnp.bfloat16)


def _in_hbm(a):
    return pltpu.with_memory_space_constraint(a, pltpu.HBM)


def _place():
    x, y, c = lax.axis_index("x"), lax.axis_index("y"), lax.axis_index("c")
    return x, y, c, 4 * x + 2 * y + c


def _peer(x, y, c, r):
    return (x ^ ((r >> 2) & 1), y ^ ((r >> 1) & 1), c ^ (r & 1))


def _gather_proj(x, g1, w_in, w_out, pool_w):
    def body(x_ref, g_ref, win_ref, wout_ref, pool_ref, ht_o, wt_o, woutb_o, wout_o, pool_o, proj_o,
             xbuf, hv, htv, wv, wob, pb, stage, send_sems, recv_sems, loc_sems, out_sems):
        px, py, c, my_idx = _place()
        fetch_x = pltpu.make_async_copy(x_ref, xbuf, loc_sems.at[5])
        fetch_x.start()
        me, sibling = (px, py, c), (px, py, 1 - c)
        chips = [(1 - px, py), (px, 1 - py), (1 - px, 1 - py)]
        for p in range(3):
            wv[3 * my_idx + p] = _bf(win_ref[0, :, p * TILE:(p + 1) * TILE])

        def index(bx, by, bc):
            return 4 * bx + 2 * by + bc

        def slot(w, block):
            return wv.at[pl.ds(3 * index(*block), 3)] if w == 0 else pool_o.at[index(*block)]

        def copy(k, w, block, to, src=None):
            return pltpu.make_async_remote_copy(
                src_ref=slot(w, block) if src is None else src, dst_ref=slot(w, block),
                send_sem=send_sems.at[2 * k + w], recv_sem=recv_sems.at[2 * k + w],
                device_id=to, device_id_type=MESH)

        def save(block):
            at = pl.ds(3 * index(*block), 3)
            pltpu.make_async_copy(wv.at[at], wt_o.at[at], loc_sems.at[4]).start()

        srcs = (slot(0, me), pb)
        first = []
        for w in (0, 1):
            if w == 1:
                pb[...] = _bf(pool_ref[0])
                wob[...] = _bf(wout_ref[0])
            group = [copy(1 + j, w, me, (*chip, c), src=srcs[w]) for j, chip in enumerate(chips[:2])]
            group.append(copy(0, w, me, sibling, src=srcs[w]))
            for cp in group:
                cp.start()
            first += group
        save(me)
        locs = [pltpu.make_async_copy(pb, slot(1, me), loc_sems.at[0]),
                pltpu.make_async_copy(wob, wout_o.at[my_idx], loc_sems.at[1]),
                pltpu.make_async_copy(wob, woutb_o, loc_sems.at[2])]
        for cp in locs:
            cp.start()

        fetch_x.wait()
        xv = xbuf[...]
        hv[...] = _bf(xv * lax.rsqrt(jnp.mean(xv * xv, axis=-1, keepdims=True) + EPS) * g_ref[...])
        rows = 256
        for r0 in range(0, T, rows):
            htv[:, r0:r0 + rows] = hv[r0:r0 + rows, :].T
        locs.append(pltpu.make_async_copy(htv, ht_o, loc_sems.at[3]))
        locs[-1].start()

        def out_copy(p, j):
            return pltpu.make_async_copy(stage.at[p], proj_o.at[j], out_sems.at[p])

        def project(nth, block):
            base = 3 * index(*block)

            def tile(p, carry):
                if nth > 0:
                    out_copy(p, base + p).wait()
                stage[p] = _dot(hv[...], wv[base + p], 1, 0)
                out_copy(p, base + p).start()
                return carry

            lax.fori_loop(0, 3, tile, 0)

        project(0, me)
        copy(0, 0, sibling, me).wait_recv()
        save(sibling)
        project(1, sibling)
        passed = []
        relay_from = (px ^ (1 - c), py ^ c, c)
        relay_to = (px ^ c, py ^ (1 - c), c)

        def arrived(w, j):
            copy(1 + j, w, (*chips[j], c), me).wait_recv()
            passed.append(copy(4 + j, w, (*chips[j], c), sibling))
            passed[-1].start()

        def relay(w):
            passed.append(copy(3, w, relay_from, relay_to))
            passed[-1].start()

        def handed(nth, j):
            copy(4 + j, 0, (*chips[j], 1 - c), me).wait_recv()
            save((*chips[j], 1 - c))
            project(nth, (*chips[j], 1 - c))

        arrived(0, 0)
        arrived(0, 1)
        relay(0)
        for j in range(2):
            save((*chips[j], c))
            project(2 + j, (*chips[j], c))
        handed(4, 0)
        handed(5, 1)
        arrived(1, 0)
        arrived(1, 1)
        relay(1)
        arrived(0, 2)
        save((*chips[2], c))
        project(6, (*chips[2], c))
        handed(7, 2)
        arrived(1, 2)
        copy(0, 1, sibling, me).wait_recv()
        for j, chip in enumerate(chips):
            copy(4 + j, 1, (*chip, 1 - c), me).wait_recv()
        keep = pltpu.make_async_copy(wv, wt_o, loc_sems.at[4])
        for p in range(3):
            out_copy(p, p).wait()
        for cp in first + passed:
            cp.wait_send()
        keep.wait()
        for cp in locs:
            cp.wait()

    vmem = pl.BlockSpec(memory_space=pltpu.VMEM)
    bf16 = jnp.bfloat16
    return pl.pallas_call(
        body, name="gather_proj",
        out_shape=(pltpu.HBM((D, T), bf16), pltpu.HBM((NTILE, D, TILE), bf16),
                   pltpu.HBM((DMIX // NDEV, D), bf16), pltpu.HBM((NDEV, DMIX // NDEV, D), bf16),
                   pltpu.HBM((NDEV, NGROUP, GROUP // NDEV, GROUP), bf16), pltpu.HBM((NTILE, T, TILE), jnp.float32)),
        in_specs=[ANY] + [vmem] * 4, out_specs=(ANY,) * 6,
        scratch_shapes=[pltpu.VMEM((T, D), jnp.float32),
                        pltpu.VMEM((T, D), bf16), pltpu.VMEM((D, T), bf16), pltpu.VMEM((NTILE, D, TILE), bf16),
                        pltpu.VMEM((DMIX // NDEV, D), bf16), pltpu.VMEM((NGROUP, GROUP // NDEV, GROUP), bf16),
                        pltpu.VMEM((3, T, TILE), jnp.float32),
                        pltpu.SemaphoreType.DMA((14,)), pltpu.SemaphoreType.DMA((14,)),
                        pltpu.SemaphoreType.DMA((6,)), pltpu.SemaphoreType.DMA((3,))],
        compiler_params=_params(vmem_mib=56),
    )(x, g1, w_in, w_out, pool_w)


def _split_start(name, arrays, n_copies, plan, after=()):
    k = len(arrays)

    def body(*refs):
        send_sems, recv_sems, token = refs[k + len(after)], refs[k + len(after) + 1], refs[-1]
        for i, (src, dst, to) in enumerate(plan(refs[:k])):
            pltpu.make_async_remote_copy(src_ref=src, dst_ref=dst, send_sem=send_sems.at[i],
                                         recv_sem=recv_sems.at[i], device_id=to, device_id_type=MESH).start()
        token[...] = jnp.zeros_like(token)

    out = pl.pallas_call(
        body, name=name,
        out_shape=(pltpu.SemaphoreType.DMA((n_copies,)), pltpu.SemaphoreType.DMA((n_copies,)),
                   *[pltpu.HBM(a.shape, a.dtype) for a in arrays], jax.ShapeDtypeStruct((8, 128), jnp.float32)),
        in_specs=[HBM] * k + [ANY] * len(after),
        out_specs=(SEM, SEM, *[HBM] * k, pl.BlockSpec(memory_space=pltpu.VMEM)),
        input_output_aliases={i: 2 + i for i in range(k)},
        compiler_params=pltpu.CompilerParams(has_side_effects=EFFECT),
    )(*[pltpu.with_memory_space_constraint(a, pltpu.HBM) for a in arrays], *after)
    return out[0], out[1], out[2:2 + k], out[-1]


def _split_wait(name, arrays, send_sems, recv_sems, plan, after):
    k = len(arrays)

    def body(*refs):
        sends, recvs = refs[k], refs[k + 1]
        for i, (src, dst, to) in enumerate(plan(refs[:k])):
            cp = pltpu.make_async_remote_copy(src_ref=src, dst_ref=dst, send_sem=sends.at[i], recv_sem=recvs.at[i],
                                              device_id=to, device_id_type=MESH)
            cp.wait_send()
            cp.wait_recv()

    return pl.pallas_call(
        body, name=name,
        out_shape=tuple(pltpu.HBM(a.shape, a.dtype) for a in arrays),
        in_specs=[HBM] * k + [SEM, SEM, ANY], out_specs=(HBM,) * k,
        input_output_aliases={i: i for i in range(k)},
        compiler_params=pltpu.CompilerParams(has_side_effects=EFFECT),
    )(*arrays, send_sems, recv_sems, after)


def _plan_wout(refs):
    src, land = refs
    x, y, c, me = _place()
    return [(src, land.at[me], _peer(x, y, c, r)) for r in range(1, NDEV)]


def _plan_rest(refs):
    gob, gpf, r_out, r_pool = refs
    x, y, c, me = _place()
    plan = []
    for r in range(1, NDEV):
        plan.append((gob.at[me ^ r], r_out.at[r - 1], _peer(x, y, c, r)))
        plan.append((gpf.at[me ^ r], r_pool.at[r - 1], _peer(x, y, c, r)))
    return plan


def _plan_in(refs):
    sums, landing = refs
    x, y, c, _ = _place()
    plan = []
    for rel, (dx, dy) in enumerate(((1, 0), (0, 1), (1, 1))):
        for p in range(3):
            plan.append((sums.at[rel, p], landing.at[rel, p], (x ^ dx, y ^ dy, c)))
    return plan


def _plan_small(refs):
    small, land = refs
    x, y, c, me = _place()
    return [(small, land.at[me], _peer(x, y, c, r)) for r in range(1, NDEV)]


def _seg_tiles(s):
    return (s + 2) % NSEG


_POOL_SPECS = [pl.BlockSpec((None, T, GROUP), lambda g, base=base: (base + g, 0, 0)) for base in (0, 4)]
_HEAD_SPECS = [pl.BlockSpec((None, T, HEAD), lambda h, base=base: (base + h // 2, 0, h % 2))
               for base in (8, 12, 16, 20)]


def _row_ids(shape):
    return lax.broadcasted_iota(jnp.int32, shape, 0)


BAND_ROWS = 128
HALO = 16


def _window_sum(a, gidx, lead):
    width = lax.shift_left(jnp.int32(2), gidx)
    shape = (BAND_ROWS, BAND_ROWS + HALO)
    t, j = lax.broadcasted_iota(jnp.int32, shape, 0), lax.broadcasted_iota(jnp.int32, shape, 1)
    first = t if lead else t + HALO - width + 1
    band = _bf(jnp.where(j >= first, jnp.where(j < first + width, 1.0, 0.0), 0.0))
    zeros = jnp.zeros((HALO, a.shape[1]), jnp.bfloat16)
    padded = [jnp.concatenate([p, zeros] if lead else [zeros, p], axis=0) for p in _split2(a)]
    out = []
    for r0 in range(0, T, BAND_ROWS):
        slab = jnp.concatenate([p[r0:r0 + BAND_ROWS + HALO] for p in padded], axis=1)
        r = _dot(band, slab, 1, 0)
        out.append(r[:, :a.shape[1]] + r[:, a.shape[1]:])
    return jnp.concatenate(out, axis=0)


def _window_mean(s, gidx):
    inv = jnp.where(gidx == 0, 0.5, jnp.where(gidx == 1, 0.25, jnp.where(gidx == 2, 0.125, 0.0625)))
    width = lax.shift_left(jnp.int32(2), gidx)
    head = s[:16] / jnp.minimum(_row_ids((16, s.shape[1])) + 1, width).astype(jnp.float32)
    return jnp.concatenate([head, s[16:] * inv], axis=0)


def _pool_fwd(proj, pool_w, pool_scale, token):
    def body(u_ref, pg_ref, w_ref, sc_ref, token_any, y_ref):
        del token_any
        gidx = pl.program_id(0)
        u, pg = u_ref[...], pg_ref[...]
        d = _window_mean(_window_sum(u, gidx, False), gidx) - u
        mixed = _dot(_bf(d), w_ref[...], 1, 0)
        y_ref[...] = _bf(mixed * sc_ref[...] * (pg * _sigmoid(pg)))

    return pl.pallas_call(
        body, name="pool_fwd", grid=(NGROUP,),
        in_specs=[*_POOL_SPECS,
                  pl.BlockSpec((None, GROUP, GROUP), lambda g: (g, 0, 0)),
                  pl.BlockSpec((1, GROUP), lambda g: (0, g)), ANY],
        out_specs=pl.BlockSpec((T, GROUP), lambda g: (0, g)),
        out_shape=pltpu.HBM((T, DMIX), jnp.bfloat16),
        compiler_params=_params(("parallel",)),
    )(proj, proj, pool_w, pool_scale, token)


def _tri(lower):
    r = lax.broadcasted_iota(jnp.int32, (CHUNK, CHUNK), 0)
    c = lax.broadcasted_iota(jnp.int32, (CHUNK, CHUNK), 1)
    return (r >= c) if lower else (r <= c)


def _sum_rows_matrix():
    shape = (CHUNK + 16, CHUNK)
    r, c = lax.broadcasted_iota(jnp.int32, shape, 0), lax.broadcasted_iota(jnp.int32, shape, 1)
    run = jnp.where(c <= r, 1.0, 0.0)
    half = jnp.where(c < CHUNK // 2, 1.0, 0.0)
    return _bf(jnp.where(r < CHUNK, run, jnp.where(r < CHUNK + 8, 1.0, half)))


def _rev_sum_matrix():
    shape = (CHUNK, 2 * CHUNK)
    r, c = lax.broadcasted_iota(jnp.int32, shape, 0), lax.broadcasted_iota(jnp.int32, shape, 1)
    return _bf(jnp.where(c < CHUNK, jnp.where(c >= r, 1.0, 0.0), jnp.where(c - CHUNK < r, 1.0, 0.0)))


def _split2(a):
    hi = _bf(a)
    return [hi, _bf(a - hi.astype(jnp.float32))]


def _exact_sums(mat, pieces):
    x = jnp.concatenate([s for p in pieces for s in _split2(p)], axis=1)
    r = _dot(mat, x, 1, 0)
    return [r[:, 2 * j * HEAD:(2 * j + 1) * HEAD] + r[:, (2 * j + 1) * HEAD:(2 * j + 2) * HEAD]
            for j in range(len(pieces))]


def _gates(qv, fl, lb):
    sq = _sigmoid(qv)
    sg = _sigmoid(fl)
    f = lb + (1.0 - lb) * sg
    return dict(sq=sq, qs=qv * sq, sg=sg, f=f, kk=1.0 - f, g=jnp.log2(f))


def _decays(sums):
    big_g = sums[:CHUNK]
    total = sums[CHUNK:CHUNK + 8]
    g_last = jnp.tile(total, (CHUNK // 8, 1))
    g_mid = jnp.tile(sums[CHUNK + 8:], (CHUNK // 8, 1))
    return dict(
        e_q=jnp.exp2(big_g),
        e_k=jnp.exp2(g_last - big_g),
        e_qm=jnp.exp2(jnp.minimum(big_g - g_mid, EXP_CAP)),
        e_km=jnp.exp2(jnp.minimum(g_mid - big_g, EXP_CAP)),
        total8=jnp.exp2(total))


def _group_rows(gi):
    return [pl.ds(pl.multiple_of((gi * NB + j) * CHUNK, CHUNK), CHUNK) for j in range(NB)]


def _lower_bound(lb_ref):
    return _sigmoid(lb_ref[0:1, :] - lb_ref[1:2, :])


def _hgrn_fwd(proj, lb_logits, rec_g, y_in):
    def body(q_ref, f_ref, i_ref, gate_ref, lb_ref, rg_ref, y_any, y_ref, o_ref, st_ref):
        del y_any
        lb = _lower_bound(lb_ref)
        causal = _tri(True)
        smat = _sum_rows_matrix()

        def group(gi, st):
            rows = _group_rows(gi)
            ts = [_gates(q_ref[r, :], f_ref[r, :], lb) for r in rows]
            ds = [_decays(s) for s in _exact_sums(smat, [t["g"] for t in ts])]
            vs = [_bf(i_ref[r, :]) for r in rows]
            q_m = [_bf(t["qs"] * d["e_qm"]) for t, d in zip(ts, ds)]
            k_m = [_bf(t["kk"] * d["e_km"]) for t, d in zip(ts, ds)]
            q_e = [_bf(t["qs"] * d["e_q"]) for t, d in zip(ts, ds)]
            k_e = [_bf(t["kk"] * d["e_k"]) for t, d in zip(ts, ds)]
            a = [_bf(jnp.where(causal, _dot(q_m[j], k_m[j], 1, 1), 0.0)) for j in range(NB)]
            intra = [_dot(a[j], vs[j], 1, 0) for j in range(NB)]
            upd = [_dot(vs[j], k_e[j], 0, 0) for j in range(NB)]
            for j in range(NB):
                st_ref[gi * NB + j] = st
                o_ref[rows[j], :] = intra[j] + _dot(q_e[j], _bf(st), 1, 1)
                st = st * jnp.tile(ds[j]["total8"], (HEAD // 8, 1)) + upd[j]
            return st

        lax.fori_loop(0, NGRP, group, jnp.zeros((HEAD, HEAD), jnp.float32))
        o = o_ref[...]
        rn = o * lax.rsqrt(jnp.mean(o * o, axis=-1, keepdims=True) + EPS)
        gate = gate_ref[...]
        y_ref[...] = _bf(rn * rg_ref[...] * (gate * _sigmoid(gate)))

    return pl.pallas_call(
        body, name="hgrn_fwd", grid=(NHEAD,),
        in_specs=[*_HEAD_SPECS,
                  pl.BlockSpec((2, HEAD), lambda h: (0, h)),
                  pl.BlockSpec((1, HEAD), lambda h: (0, h)),
                  pl.BlockSpec(memory_space=pl.ANY)],
        out_specs=(pl.BlockSpec((T, HEAD), lambda h: (0, NHEAD + h)),
                   pl.BlockSpec((T, HEAD), lambda h: (0, h)),
                   pl.BlockSpec((None, NCHUNK, HEAD, HEAD), lambda h: (h, 0, 0, 0))),
        out_shape=(pltpu.HBM((T, DMIX), jnp.bfloat16), pltpu.HBM((T, D), jnp.float32),
                   pltpu.HBM((NHEAD, NCHUNK, HEAD, HEAD), jnp.float32)),
        input_output_aliases={6: 0},
        compiler_params=_params(("parallel",)),
    )(proj, proj, proj, proj, lb_logits, rec_g, y_in)


def _out_proj_loss(x, y, w_out, target, gf):
    rows = 512
    parts = [slice(k * rows // 2, (k + 1) * rows // 2) for k in range(2)]

    def body(x_ref, y_ref, w_ref, t_ref, g_ref, dz_ref, dzb_ref, sq_ref, dg_ref):
        zs = [x_ref[p, :] + _dot(y_ref[p, :], w_ref[...], 1, 0) for p in parts]
        sq = dg = 0.0
        for p, z in zip(parts, zs):
            r = lax.rsqrt(jnp.mean(z * z, axis=-1, keepdims=True) + EPS)
            zhat = z * r
            err = zhat * g_ref[...] - t_ref[p, :]
            dy = err * (1.0 / D)
            gdy = dy * g_ref[...]
            dz = r * (gdy - zhat * jnp.mean(zhat * gdy, axis=-1, keepdims=True))
            dz_ref[p, :] = dz
            dzb_ref[p, :] = _bf(dz)
            sq = sq + jnp.sum(err * err, axis=0, keepdims=True)
            dg = dg + jnp.sum(zhat * dy, axis=0, keepdims=True)

        @pl.when(pl.program_id(0) == 0)
        def _():
            sq_ref[...] = sq
            dg_ref[...] = dg

        @pl.when(pl.program_id(0) != 0)
        def _():
            sq_ref[...] += sq
            dg_ref[...] += dg

    tile = pl.BlockSpec((rows, D), lambda i: (i, 0))
    vec = pl.BlockSpec((1, D), lambda i: (0, 0))
    return pl.pallas_call(
        body, name="out_proj_loss", grid=(T // rows,),
        in_specs=[tile, pl.BlockSpec((rows, DMIX), lambda i: (i, 0)), pl.BlockSpec((DMIX, D), lambda i: (0, 0)),
                  tile, vec],
        out_specs=(tile, tile, vec, vec),
        out_shape=(pltpu.HBM((T, D), jnp.float32), pltpu.HBM((T, D), jnp.bfloat16),
                   jax.ShapeDtypeStruct((1, D), jnp.float32), jax.ShapeDtypeStruct((1, D), jnp.float32)),
        compiler_params=_params(("arbitrary",)),
    )(x, y, w_out, target, gf)


def _out_proj_bwd(dzb, w_out, y):
    tn = 512

    def body(dz_ref, w_ref, y_ref, dy_ref, gw_ref, gwb_ref):
        dz = dz_ref[...]
        dy_ref[...] = _dot(dz, w_ref[...], 1, 1)
        gw = _dot(y_ref[...], dz, 0, 0)
        gw_ref[...] = gw
        gwb_ref[...] = _bf(gw)

    return pl.pallas_call(
        body, name="out_proj_bwd", grid=(DMIX // tn,),
        in_specs=[pl.BlockSpec((T, D), lambda n: (0, 0)), pl.BlockSpec((tn, D), lambda n: (n, 0)),
                  pl.BlockSpec((T, tn), lambda n: (0, n))],
        out_specs=(pl.BlockSpec((T, tn), lambda n: (0, n)), pl.BlockSpec((tn, D), lambda n: (n, 0)),
                   pl.BlockSpec((tn, D), lambda n: (n, 0))),
        out_shape=(pltpu.HBM((T, DMIX), jnp.float32), pltpu.HBM((DMIX, D), jnp.float32),
                   pltpu.HBM((DMIX, D), jnp.bfloat16)),
        compiler_params=_params(("parallel",)),
    )(dzb, w_out, y)


def _hgrn_bwd(proj, lb_logits, rec_g, o, states, dymix, dproj_in, token):
    def body(q_ref, f_ref, i_ref, gate_ref, lb_ref, rg_ref, o_ref, st_ref, dy_ref, dp_any, token_any,
             dp_ref, drg_ref, dlb_ref, do_ref):
        del dp_any, token_any
        lb = _lower_bound(lb_ref)
        causal = _tri(True)
        smat, rmat = _sum_rows_matrix(), _rev_sum_matrix()

        o = o_ref[...]
        rs = lax.rsqrt(jnp.mean(o * o, axis=-1, keepdims=True) + EPS)
        rn = o * rs
        gate = gate_ref[...]
        sgate = _sigmoid(gate)
        dyv = dy_ref[...]
        d_r = dyv * (gate * sgate)
        dp_ref[3] = _bf(dyv * (rn * rg_ref[...]) * (sgate * (1.0 + gate * (1.0 - sgate))))
        drg_ref[...] = jnp.sum(d_r * rn, axis=0, keepdims=True)
        drn = d_r * rg_ref[...]
        do_ref[...] = rs * (drn - rn * jnp.mean(rn * drn, axis=-1, keepdims=True))

        def group(i, carry):
            dst, dlb = carry
            gi = NGRP - 1 - i
            rows = _group_rows(gi)
            span = range(NB)
            qvs = [q_ref[r, :] for r in rows]
            ts = [_gates(qv, f_ref[r, :], lb) for qv, r in zip(qvs, rows)]
            ds = [_decays(s) for s in _exact_sums(smat, [t["g"] for t in ts])]
            vs = [_bf(i_ref[r, :]) for r in rows]
            dos = [_bf(do_ref[r, :]) for r in rows]
            sts = [st_ref[gi * NB + j] for j in span]
            qe_f = [t["qs"] * d["e_q"] for t, d in zip(ts, ds)]
            ke_f = [t["kk"] * d["e_k"] for t, d in zip(ts, ds)]
            q_e, k_e = [_bf(a) for a in qe_f], [_bf(a) for a in ke_f]
            q_m = [_bf(t["qs"] * d["e_qm"]) for t, d in zip(ts, ds)]
            k_m = [_bf(t["kk"] * d["e_km"]) for t, d in zip(ts, ds)]
            a = [_bf(jnp.where(causal, _dot(q_m[j], k_m[j], 1, 1), 0.0)) for j in span]
            da = [_bf(jnp.where(causal, _dot(dos[j], vs[j], 1, 1), 0.0)) for j in span]
            dqm = [_dot(da[j], k_m[j], 1, 0) for j in span]
            dkm = [_dot(da[j], q_m[j], 0, 0) for j in span]
            dv_in = [_dot(a[j], dos[j], 0, 0) for j in span]
            dqe = [_dot(dos[j], _bf(sts[j]), 1, 0) for j in span]
            grow = [_dot(dos[j], q_e[j], 0, 0) for j in span]
            dke, carried = [None] * NB, [None] * NB
            for j in reversed(span):
                dst_b = _bf(dst)
                dke[j] = _dot(vs[j], dst_b, 1, 0)
                dp_ref[2, rows[j], :] = _bf(dv_in[j] + _dot(k_e[j], dst_b, 1, 1))
                carried[j] = ds[j]["total8"] * jnp.sum(dst * sts[j], axis=0, keepdims=True)
                dst = dst * jnp.tile(ds[j]["total8"], (HEAD // 8, 1)) + grow[j]
            kdk = [ke_f[j] * dke[j] for j in span]
            pos = [(q_m[j].astype(jnp.float32) * dqm[j] - k_m[j].astype(jnp.float32) * dkm[j]) + qe_f[j] * dqe[j]
                   for j in span]
            dgs = _exact_sums(rmat, [jnp.concatenate([pos[j], kdk[j]], axis=0) for j in span])
            for j in span:
                t, d = ts[j], ds[j]
                dg = dgs[j] + jnp.tile(carried[j], (CHUNK // 8, 1))
                dqs = dqm[j] * d["e_qm"] + dqe[j] * d["e_q"]
                dkk = dkm[j] * d["e_km"] + dke[j] * d["e_k"]
                df = dg / t["f"] - dkk
                dp_ref[1, rows[j], :] = _bf(df * (1.0 - lb) * (t["sg"] * (1.0 - t["sg"])))
                dp_ref[0, rows[j], :] = _bf(dqs * (t["sq"] * (1.0 + qvs[j] * (1.0 - t["sq"]))))
                dlb = dlb + df * (1.0 - t["sg"])
            return dst, dlb

        _, dlb = lax.fori_loop(0, NGRP, group, (jnp.zeros((HEAD, HEAD), jnp.float32),
                                                jnp.zeros((CHUNK, HEAD), jnp.float32)))
        dlb_ref[...] = jnp.sum(dlb, axis=0, keepdims=True)

    vec = pl.BlockSpec((1, HEAD), lambda h: (0, h))
    return pl.pallas_call(
        body, name="hgrn_bwd", grid=(NHEAD,),
        in_specs=[*_HEAD_SPECS,
                  pl.BlockSpec((2, HEAD), lambda h: (0, h)), vec,
                  pl.BlockSpec((T, HEAD), lambda h: (0, h)),
                  pl.BlockSpec((None, NCHUNK, HEAD, HEAD), lambda h: (h, 0, 0, 0)),
                  pl.BlockSpec((T, HEAD), lambda h: (0, NHEAD + h)), ANY, ANY],
        out_specs=(pl.BlockSpec((4, T, HEAD), lambda h: (0, 0, h)), vec, vec),
        out_shape=(pltpu.HBM((NSEG, T, D), jnp.bfloat16),
                   jax.ShapeDtypeStruct((1, D), jnp.float32), jax.ShapeDtypeStruct((1, D), jnp.float32)),
        scratch_shapes=[pltpu.VMEM((T, HEAD), jnp.float32)],
        input_output_aliases={9: 0},
        compiler_params=_params(("parallel",)),
    )(proj, proj, proj, proj, lb_logits, rec_g, o, states, dymix, dproj_in, token)


def _pool_bwd(proj, pool_w, pool_scale, dymix):
    def body(u_ref, pg_ref, w_ref, sc_ref, dy_ref, dp_ref, gw_ref, gs_ref):
        gidx = pl.program_id(0)
        u, pg = u_ref[...], pg_ref[...]
        d = _bf(_window_mean(_window_sum(u, gidx, False), gidx) - u)
        mixed = _dot(d, w_ref[...], 1, 0)
        spg = _sigmoid(pg)
        dyv = dy_ref[...]
        d_p = dyv * (pg * spg)
        dp_ref[1] = _bf(dyv * (mixed * sc_ref[...]) * (spg * (1.0 + pg * (1.0 - spg))))
        gs_ref[...] = jnp.sum(d_p * mixed, axis=0, keepdims=True)
        dmixed = _bf(d_p * sc_ref[...])
        gw_ref[...] = _dot(d, dmixed, 0, 0)
        dd = _dot(dmixed, w_ref[...], 1, 1)
        dp_ref[0] = _bf(_window_sum(_window_mean(dd, gidx), gidx, True) - dd)

    return pl.pallas_call(
        body, name="pool_bwd", grid=(NGROUP,),
        in_specs=[*_POOL_SPECS,
                  pl.BlockSpec((None, GROUP, GROUP), lambda g: (g, 0, 0)),
                  pl.BlockSpec((1, GROUP), lambda g: (0, g)),
                  pl.BlockSpec((T, GROUP), lambda g: (0, g))],
        out_specs=(pl.BlockSpec((2, T, GROUP), lambda g: (2, 0, g)),
                   pl.BlockSpec((None, GROUP, GROUP), lambda g: (g, 0, 0)),
                   pl.BlockSpec((1, GROUP), lambda g: (0, g))),
        out_shape=(pltpu.HBM((NSEG, T, D), jnp.bfloat16),
                   jax.ShapeDtypeStruct((NGROUP, GROUP, GROUP), jnp.float32),
                   jax.ShapeDtypeStruct((1, D), jnp.float32)),
        compiler_params=_params(("parallel",)),
    )(proj, proj, pool_w, pool_scale, dymix)


HALF = NTILE // 2
AWAY = HALF - 3


def _dproj_tile(chip, side, p):
    j = 6 * chip + 3 * side + p
    return ((j // 4 + 4) % NSEG, 0, j % 4)


def _sibling_copy(sib_out, sib_in, send_sems, recv_sems, slot):
    x, y, c, _ = _place()
    return pltpu.make_async_remote_copy(
        src_ref=sib_out.at[slot], dst_ref=sib_in.at[slot], send_sem=send_sems.at[slot],
        recv_sem=recv_sems.at[slot], device_id=(x, y, 1 - c), device_id_type=MESH)


def _proj_bwd_w_far(place, ht, dproj):
    def body(place_ref, h_ref, dp_ref, sib_out, sib_in, send_sems, recv_sems, stage, loc_sems):
        del place_ref
        i = pl.program_id(0)

        def to_hbm(k):
            return pltpu.make_async_copy(stage.at[k], sib_out.at[k], loc_sems.at[k])

        def send(k):
            to_hbm(k).wait()
            _sibling_copy(sib_out, sib_in, send_sems, recv_sems, k).start()

        stage[i] = _bf(_dot(h_ref[...], dp_ref[...], 1, 0))

        @pl.when(i > 0)
        def _():
            send(i - 1)

        to_hbm(i).start()

        @pl.when(i == HALF - 1)
        def _():
            send(i)

    buf = pltpu.HBM((HALF, D, TILE), jnp.bfloat16)
    sems = pltpu.SemaphoreType.DMA((HALF,))
    return pl.pallas_call(
        body, name="proj_bwd_w_far",
        grid_spec=pltpu.PrefetchScalarGridSpec(
            num_scalar_prefetch=1, grid=(HALF,),
            in_specs=[pl.BlockSpec((D, T), lambda i, pr: (0, 0)),
                      pl.BlockSpec((None, T, TILE), lambda i, pr: _dproj_tile(i // 3, 1 - pr[2], i % 3))],
            out_specs=(HBM, HBM, SEM, SEM),
            scratch_shapes=[pltpu.VMEM((HALF, D, TILE), jnp.bfloat16), pltpu.SemaphoreType.DMA((HALF,))]),
        out_shape=(buf, buf, sems, sems),
        compiler_params=pltpu.CompilerParams(dimension_semantics=("arbitrary",), vmem_limit_bytes=48 * MIB,
                                             has_side_effects=EFFECT),
    )(place, ht, dproj)


def _proj_bwd_w_near(place, ht, dproj, sib_out, sib_in, sib_send, sib_recv):
    def owner_chip(k, pr):
        return jnp.where(k < AWAY, (pr[1] + 1 + k % 3) % 4, pr[1])

    def tile_p(k):
        return jnp.where(k < AWAY, k // 3, k - AWAY)

    def body(place_ref, h_ref, dp_ref, sib_out, sib_in, sib_send, sib_recv, sums, own_ref, landing, out_send,
             out_recv, recvbuf, outbuf, in_sems, loc_sems):
        i = pl.program_id(0)
        px, py, c, _ = _place()

        def slot_of(k):
            return 3 * owner_chip(k, place_ref) + tile_p(k)

        def load(k):
            return pltpu.make_async_copy(sib_in.at[slot_of(k)], recvbuf.at[k % 2], in_sems.at[k % 2])

        def fetch(k):
            _sibling_copy(sib_out, sib_in, sib_send, sib_recv, slot_of(k)).wait_recv()
            load(k).start()

        def route(k):
            chip = owner_chip(k, place_ref)
            cx, cy = chip // 2, chip % 2
            return cx, cy, (cx ^ px) + 2 * (cy ^ py) - 1, tile_p(k)

        def to_hbm(k):
            _, _, rel, p = route(k)
            return pltpu.make_async_copy(outbuf.at[k], sums.at[rel, p], loc_sems.at[k])

        def to_owner(k):
            cx, cy, rel, p = route(k)
            return pltpu.make_async_remote_copy(
                src_ref=sums.at[rel, p], dst_ref=landing.at[rel, p], send_sem=out_send.at[3 * rel + p],
                recv_sem=out_recv.at[3 * rel + p], device_id=(cx, cy, c), device_id_type=MESH)

        @pl.when(i == 0)
        def _():
            fetch(i)

        @pl.when(i < HALF - 1)
        def _():
            fetch(i + 1)

        gw = _dot(h_ref[...], dp_ref[...], 1, 0)

        @pl.when(jnp.logical_and(i > 0, i <= AWAY))
        def _():
            to_hbm(i - 1).wait()
            to_owner(i - 1).start()

        load(i).wait()
        total = gw + recvbuf[i % 2].astype(jnp.float32)

        @pl.when(i < AWAY)
        def _():
            outbuf[i] = _bf(total)
            to_hbm(i).start()

        @pl.when(i >= AWAY)
        def _():
            own_ref[...] = total

        @pl.when(i == HALF - 1)
        def _():
            for slot in range(HALF):
                _sibling_copy(sib_out, sib_in, sib_send, sib_recv, slot).wait_send()

    travelling = pltpu.HBM((3, 3, D, TILE), jnp.bfloat16)
    sems = pltpu.SemaphoreType.DMA((AWAY,))
    return pl.pallas_call(
        body, name="proj_bwd_w_near",
        grid_spec=pltpu.PrefetchScalarGridSpec(
            num_scalar_prefetch=1, grid=(HALF,),
            in_specs=[pl.BlockSpec((D, T), lambda i, pr: (0, 0)),
                      pl.BlockSpec((None, T, TILE), lambda i, pr: _dproj_tile(owner_chip(i, pr), pr[2], tile_p(i))),
                      HBM, HBM, SEM, SEM],
            out_specs=(HBM, pl.BlockSpec((None, D, TILE), lambda i, pr: (jnp.where(i < AWAY, 0, i % 3), 0, 0)),
                       HBM, SEM, SEM),
            scratch_shapes=[pltpu.VMEM((2, D, TILE), jnp.bfloat16), pltpu.VMEM((AWAY, D, TILE), jnp.bfloat16),
                            pltpu.SemaphoreType.DMA((2,)), pltpu.SemaphoreType.DMA((AWAY,))]),
        out_shape=(travelling, pltpu.HBM((3, D, TILE), jnp.float32), travelling, sems, sems),
        compiler_params=pltpu.CompilerParams(dimension_semantics=("arbitrary",), vmem_limit_bytes=48 * MIB,
                                             has_side_effects=EFFECT),
    )(place, ht, dproj, sib_out, sib_in, sib_send, sib_recv)


def _proj_bwd_x(dproj, w_t, x, g1, dz, token):
    tm = 512
    pairs = NSEG // 2

    def body(dp_ref, w_ref, x_ref, g_ref, dz_ref, token_any, dx_ref, dg_ref, wcat, acc):
        del token_any
        s, m = pl.program_id(0), pl.program_id(1)

        @pl.when(m == 0)
        def _():
            for i in range(8):
                wcat[:, i * TILE:(i + 1) * TILE] = w_ref[i]

        mine = pl.ds(pl.multiple_of(m * tm, tm), tm)

        @pl.when(s == 0)
        def _():
            acc[mine, :] = jnp.zeros((tm, D), jnp.float32)

        acc[mine, :] += _dot(jnp.concatenate([dp_ref[0], dp_ref[1]], axis=1), wcat[...], 1, 1)

        @pl.when(s == pairs - 1)
        def _():
            xv = x_ref[...]
            rs = lax.rsqrt(jnp.mean(xv * xv, axis=-1, keepdims=True) + EPS)
            xhat = xv * rs
            dhv = acc[mine, :]
            gdh = dhv * g_ref[...]
            dx_ref[...] = dz_ref[...] + rs * (gdh - xhat * jnp.mean(xhat * gdh, axis=-1, keepdims=True))
            dg = jnp.sum(xhat * dhv, axis=0, keepdims=True)

            @pl.when(m == 0)
            def _():
                dg_ref[...] = dg

            @pl.when(m != 0)
            def _():
                dg_ref[...] += dg

    rows = pl.BlockSpec((tm, D), lambda s, m: (jnp.where(s == pairs - 1, m, 0), 0))
    vec = pl.BlockSpec((1, D), lambda s, m: (0, 0))
    return pl.pallas_call(
        body, name="proj_bwd_x", grid=(pairs, T // tm),
        in_specs=[pl.BlockSpec((2, tm, D), lambda s, m: (s, m, 0)),
                  pl.BlockSpec((8, D, TILE), lambda s, m: ((s + 1) % pairs, 0, 0)), rows, vec, rows, ANY],
        out_specs=(rows, vec),
        out_shape=(jax.ShapeDtypeStruct((T, D), jnp.float32), jax.ShapeDtypeStruct((1, D), jnp.float32)),
        scratch_shapes=[pltpu.VMEM((D, 2 * D), jnp.bfloat16), pltpu.VMEM((T, D), jnp.float32)],
        compiler_params=_params(("arbitrary", "arbitrary"), vmem_mib=56),
    )(dproj, w_t, x, g1, dz, token)


def _adamw(w, g, m, v):
    m_new = ADAM_B1 * m + (1.0 - ADAM_B1) * g
    v_new = ADAM_B2 * v + (1.0 - ADAM_B2) * (g * g)
    delta = -ADAM_LR * ((m_new / BC1) / (jnp.sqrt(v_new / BC2) + ADAM_EPS) + ADAM_WD * w)
    return delta, m_new, v_new


def _reduce_adam(name, place, parts, w, m, v, grid, w_spec):
    n = len(parts)

    def body(place_ref, *refs):
        del place_ref
        w_ref, m_ref, v_ref, g_ref, d_ref, mo_ref, vo_ref = refs[n:]
        g = None
        for ref, (_, _, stacked) in zip(refs[:n], parts):
            terms = [ref[r] for r in range(ref.shape[0])] if stacked else [ref[...]]
            for t in terms:
                if t.shape[-1] != w_ref.shape[-1]:
                    t = jnp.concatenate([t[p] for p in range(t.shape[0])], axis=1)
                g = t.astype(jnp.float32) if g is None else g + t.astype(jnp.float32)
        delta, m_new, v_new = _adamw(w_ref[...], g, m_ref[...], v_ref[...])
        g_ref[...] = g
        d_ref[...] = delta
        mo_ref[...] = m_new
        vo_ref[...] = v_new

    shape = jax.ShapeDtypeStruct(w.shape, jnp.float32)
    return pl.pallas_call(
        body, name=name,
        grid_spec=pltpu.PrefetchScalarGridSpec(
            num_scalar_prefetch=1, grid=grid,
            in_specs=[spec for _, spec, _ in parts] + [w_spec] * 3, out_specs=(w_spec,) * 4),
        out_shape=(shape,) * 4,
        compiler_params=_params(("parallel",)),
    )(place, *[_in_hbm(a) for a in [a for a, _, _ in parts] + [w, m, v]])


def _small_adam(place, own, parts, w, m, v):
    def body(place_ref, own_ref, p_ref, w_ref, m_ref, v_ref, g_ref, d_ref, mo_ref, vo_ref):
        me = place_ref[0]
        g = None
        for s in range(NDEV):
            term = jnp.where(me == s, own_ref[...], p_ref[s])
            g = term if g is None else g + term
        wv = w_ref[...]
        rows = _row_ids(wv.shape)
        other = jnp.where(rows == 2, pltpu.roll(wv, 7, 0), jnp.where(rows == 3, pltpu.roll(wv, 1, 0), 0.0))
        lbv = _sigmoid(wv - other)
        sign = jnp.where(rows == 2, 1.0, -1.0)
        g = jnp.where((rows == 2) | (rows == 3), sign * g * lbv * (1.0 - lbv), g)
        delta, m_new, v_new = _adamw(wv, g, m_ref[...], v_ref[...])
        g_ref[...] = g
        d_ref[...] = delta
        mo_ref[...] = m_new
        vo_ref[...] = v_new

    shape = jax.ShapeDtypeStruct((8, D), jnp.float32)
    vmem = pl.BlockSpec(memory_space=pltpu.VMEM)
    return pl.pallas_call(
        body, name="small_adam", out_shape=(shape,) * 4,
        in_specs=[pl.BlockSpec(memory_space=pltpu.SMEM)] + [vmem] * 5, out_specs=(vmem,) * 4,
    )(place, own, parts, w, m, v)


def _rows8(*vecs):
    rows = [a.reshape(-1, D) for a in vecs]
    n = sum(r.shape[0] for r in rows)
    return jnp.concatenate(rows + [jnp.zeros((8 - n, D), jnp.float32)], axis=0)


def kernel(x, norm1_g, w_in, pool_w, pool_scale, lb_logits, rec_norm_g, w_out, final_norm_g, loss_target, m_norm1_g, m_w_in, m_pool_w, m_pool_scale, m_lb_logits, m_rec_norm_g, m_w_out, m_final_norm_g, v_norm1_g, v_w_in, v_pool_w, v_pool_scale, v_lb_logits, v_rec_norm_g, v_w_out, v_final_norm_g):
    xs = x[0]
    target = loss_target[0]
    ix, iy, ic = lax.axis_index("x"), lax.axis_index("y"), lax.axis_index("c")
    place = jnp.stack([4 * ix + 2 * iy + ic, 2 * ix + iy, ic]).astype(jnp.int32)
    gf = final_norm_g.reshape(1, D)

    ht, w_t, w_out_b, w_out_g, pool_g, proj = _gather_proj(xs, norm1_g, w_in, w_out, pool_w)
    pool_full = pool_g.transpose(1, 0, 2, 3).reshape(NGROUP, GROUP, GROUP)
    wout = [w_out_b, w_out_g]
    wout_send, wout_recv, wout, wout_token = _split_start("gather_wout_start", wout, NDEV - 1, _plan_wout)

    y = _pool_fwd(proj, pool_full, pool_scale, wout_token)
    y, o, states = _hgrn_fwd(proj, lb_logits, rec_norm_g, y)
    _, w_out_g = _split_wait("gather_wout_wait", wout, wout_send, wout_recv, _plan_wout, o)
    w_out_full = _in_hbm(w_out_g.reshape(DMIX, D))
    dz, dzb, sq, dgf = _out_proj_loss(xs, y, w_out_full, target, gf)

    dymix, gwout_f, gwout_b = _out_proj_bwd(dzb, w_out_full, y)
    dproj, gpool, dscale = _pool_bwd(proj, pool_full, pool_scale, dymix)

    blk_out = (NDEV, DMIX // NDEV, D)
    blk_pool = (NDEV, NGROUP, GROUP // NDEV, GROUP)
    gpool_s = gpool.reshape(NGROUP, NDEV, GROUP // NDEV, GROUP).transpose(1, 0, 2, 3)
    rest = [gwout_b.reshape(blk_out), gpool_s,
            lax.empty((NDEV - 1,) + blk_out[1:], jnp.bfloat16), lax.empty((NDEV - 1,) + blk_pool[1:], jnp.float32)]
    rest_send, rest_recv, rest, rest_token = _split_start("scatter_rest_start", rest, 2 * (NDEV - 1), _plan_rest)

    dproj, drecg, dlb = _hgrn_bwd(proj, lb_logits, rec_norm_g, o, states, dymix, dproj, rest_token)
    chip_sums, own_sum, landing, win_send, win_recv = _proj_bwd_w_near(
        place, ht, dproj, *_proj_bwd_w_far(place, ht, dproj))
    win = [chip_sums, landing]

    grad_x, dg1 = _proj_bwd_x(dproj, w_t, xs, norm1_g, dz, chip_sums)

    _, gpool_own, r_out, r_pool = _split_wait("scatter_rest_wait", rest, rest_send, rest_recv, _plan_rest, grad_x)
    g_wout, d_wout, m_wout, v_wout = _reduce_adam(
        "adam_w_out", place,
        [(gwout_f.reshape(blk_out), pl.BlockSpec((None,) + blk_out[1:], lambda i, pr: (pr[0], 0, 0)), False),
         (r_out, pl.BlockSpec((NDEV - 1,) + blk_out[1:], lambda i, pr: (0, 0, 0)), True)],
        w_out, m_w_out, v_w_out, (1,), pl.BlockSpec((None,) + blk_out[1:], lambda i, pr: (0, 0, 0)))
    g_pool, d_pool, m_pool, v_pool = _reduce_adam(
        "adam_pool_w", place,
        [(gpool_own, pl.BlockSpec((None,) + blk_pool[1:], lambda i, pr: (pr[0], 0, 0, 0)), False),
         (r_pool, pl.BlockSpec((NDEV - 1,) + blk_pool[1:], lambda i, pr: (0, 0, 0, 0)), True)],
        pool_w, m_pool_w, v_pool_w, (1,), pl.BlockSpec((None,) + blk_pool[1:], lambda i, pr: (0, 0, 0, 0)))

    small = [_rows8(dg1, dscale, dlb, dlb, drecg, dgf, sq), lax.empty((NDEV, 8, D), jnp.float32)]
    small_send, small_recv, small, small_token = _split_start(
        "gather_small_start", small, NDEV - 1, _plan_small, (d_wout, d_pool))

    _, r_in = _split_wait("scatter_win_wait", win, win_send, win_recv, _plan_in, small_token)
    g_win, d_win, m_win, v_win = _reduce_adam(
        "adam_w_in", place,
        [(own_sum, pl.BlockSpec((3, D // 8, TILE), lambda i, pr: (0, i, 0)), False),
         (r_in, pl.BlockSpec((3, 3, D // 8, TILE), lambda i, pr: (0, 0, i, 0)), True)],
        w_in, m_w_in, v_w_in, (8,), pl.BlockSpec((None, D // 8, 3 * TILE), lambda i, pr: (0, i, 0)))

    own_small, r_small = _split_wait("gather_small_wait", small, small_send, small_recv, _plan_small, d_win)
    g_s, d_s, m_s, v_s = _small_adam(
        place, own_small, r_small,
        _rows8(norm1_g, pool_scale, lb_logits, rec_norm_g, final_norm_g),
        _rows8(m_norm1_g, m_pool_scale, m_lb_logits, m_rec_norm_g, m_final_norm_g),
        _rows8(v_norm1_g, v_pool_scale, v_lb_logits, v_rec_norm_g, v_final_norm_g))
    loss = jnp.sum(g_s[6]) * (0.5 / D)

    def small_outs(a):
        return a[0:1], a[1:2], a[2:4], a[4:5], a[5]

    def outs(small_a, win, pool, wout):
        n1, ps, lbl, rg, fg = small_outs(small_a)
        return n1, win, pool, ps, lbl, rg, wout, fg

    return (loss, grad_x[None],
            *outs(g_s, g_win, g_pool, g_wout), *outs(d_s, d_win, d_pool, d_wout),
            *outs(m_s, m_win, m_pool, m_wout), *outs(v_s, v_win, v_pool, v_wout))
```

```python
import functools

import jax
import jax.numpy as jnp
from jax import lax
from jax.experimental import pallas as pl
from jax.experimental.pallas import tpu as pltpu

T = 2048
D = 1024
NSEG = 6
NTILE = 24
TILE = 256
DMIX = 2048
NDEV = 8
HEAD = 128
NHEAD = 8
CHUNK = 64
NCHUNK = T // CHUNK
NB = 32
NGRP = NCHUNK // NB
NGROUP = 4
GROUP = 256
EPS = 1e-6
EXP_CAP = 115.0
MESH = pl.DeviceIdType.MESH
AXES = ("x", "y", "c")
ANY = pl.BlockSpec(memory_space=pl.ANY)
HBM = pl.BlockSpec(memory_space=pltpu.HBM)
SEM = pl.BlockSpec(memory_space=pltpu.SEMAPHORE)
EFFECT = pltpu.SideEffectType.DATAFLOW_SIDE_EFFECTING

ADAM_LR = 0.001
ADAM_B1 = 0.9
ADAM_B2 = 0.999
ADAM_EPS = 1e-08
ADAM_WD = 0.01
ADAM_STEP = 10
BC1 = 1.0 - ADAM_B1 ** ADAM_STEP
BC2 = 1.0 - ADAM_B2 ** ADAM_STEP

MIB = 1 << 20


def _params(sem=None, vmem_mib=48):
    return pltpu.CompilerParams(dimension_semantics=sem, vmem_limit_bytes=vmem_mib * MIB)


def _sigmoid(v):
    return 1.0 / (1.0 + jnp.exp(-v))


def _dot(a, b, ca, cb, precision=None):
    return lax.dot_general(a, b, (((ca,), (cb,)), ((), ())), precision=precision,
                           preferred_element_type=jnp.float32)


def _bf(v):
    return v.astype(jnp.bfloat16)


def _in_hbm(a):
    return pltpu.with_memory_space_constraint(a, pltpu.HBM)


def _place():
    x, y, c = lax.axis_index("x"), lax.axis_index("y"), lax.axis_index("c")
    return x, y, c, 4 * x + 2 * y + c


def _peer(x, y, c, r):
    return (x ^ ((r >> 2) & 1), y ^ ((r >> 1) & 1), c ^ (r & 1))


def _gather_proj(x, g1, w_in, w_out, pool_w):
    def body(x_ref, g_ref, win_ref, wout_ref, pool_ref, ht_o, wt_o, woutb_o, wout_o, pool_o, proj_o,
             xbuf, hv, htv, wv, wob, pb, stage, send_sems, recv_sems, loc_sems, out_sems):
        px, py, c, my_idx = _place()
        fetch_x = pltpu.make_async_copy(x_ref, xbuf, loc_sems.at[5])
        fetch_x.start()
        me, sibling = (px, py, c), (px, py, 1 - c)
        chips = [(1 - px, py), (px, 1 - py), (1 - px, 1 - py)]
        for p in range(3):
            wv[3 * my_idx + p] = _bf(win_ref[0, :, p * TILE:(p + 1) * TILE])

        def index(bx, by, bc):
            return 4 * bx + 2 * by + bc

        def slot(w, block):
            return wv.at[pl.ds(3 * index(*block), 3)] if w == 0 else pool_o.at[index(*block)]

        def copy(k, w, block, to, src=None):
            return pltpu.make_async_remote_copy(
                src_ref=slot(w, block) if src is None else src, dst_ref=slot(w, block),
                send_sem=send_sems.at[2 * k + w], recv_sem=recv_sems.at[2 * k + w],
                device_id=to, device_id_type=MESH)

        def save(block):
            at = pl.ds(3 * index(*block), 3)
            pltpu.make_async_copy(wv.at[at], wt_o.at[at], loc_sems.at[4]).start()

        srcs = (slot(0, me), pb)
        first = []
        for w in (0, 1):
            if w == 1:
                pb[...] = _bf(pool_ref[0])
                wob[...] = _bf(wout_ref[0])
            group = [copy(1 + j, w, me, (*chip, c), src=srcs[w]) for j, chip in enumerate(chips[:2])]
            group.append(copy(0, w, me, sibling, src=srcs[w]))
            for cp in group:
                cp.start()
            first += group
        save(me)
        locs = [pltpu.make_async_copy(pb, slot(1, me), loc_sems.at[0]),
                pltpu.make_async_copy(wob, wout_o.at[my_idx], loc_sems.at[1]),
                pltpu.make_async_copy(wob, woutb_o, loc_sems.at[2])]
        for cp in locs:
            cp.start()

        fetch_x.wait()
        xv = xbuf[...]
        hv[...] = _bf(xv * lax.rsqrt(jnp.mean(xv * xv, axis=-1, keepdims=True) + EPS) * g_ref[...])
        rows = 256
        for r0 in range(0, T, rows):
            htv[:, r0:r0 + rows] = hv[r0:r0 + rows, :].T
        locs.append(pltpu.make_async_copy(htv, ht_o, loc_sems.at[3]))
        locs[-1].start()

        def out_copy(p, j):
            return pltpu.make_async_copy(stage.at[p], proj_o.at[j], out_sems.at[p])

        def project(nth, block):
            base = 3 * index(*block)

            def tile(p, carry):
                if nth > 0:
                    out_copy(p, base + p).wait()
                stage[p] = _dot(hv[...], wv[base + p], 1, 0)
                out_copy(p, base + p).start()
                return carry

            lax.fori_loop(0, 3, tile, 0)

        project(0, me)
        copy(0, 0, sibling, me).wait_recv()
        save(sibling)
        project(1, sibling)
        passed = []
        relay_from = (px ^ (1 - c), py ^ c, c)
        relay_to = (px ^ c, py ^ (1 - c), c)

        def arrived(w, j):
            copy(1 + j, w, (*chips[j], c), me).wait_recv()
            passed.append(copy(4 + j, w, (*chips[j], c), sibling))
            passed[-1].start()

        def relay(w):
            passed.append(copy(3, w, relay_from, relay_to))
            passed[-1].start()

        def handed(nth, j):
            copy(4 + j, 0, (*chips[j], 1 - c), me).wait_recv()
            save((*chips[j], 1 - c))
            project(nth, (*chips[j], 1 - c))

        arrived(0, 0)
        arrived(0, 1)
        relay(0)
        for j in range(2):
            save((*chips[j], c))
            project(2 + j, (*chips[j], c))
        handed(4, 0)
        handed(5, 1)
        arrived(1, 0)
        arrived(1, 1)
        relay(1)
        arrived(0, 2)
        save((*chips[2], c))
        project(6, (*chips[2], c))
        handed(7, 2)
        arrived(1, 2)
        copy(0, 1, sibling, me).wait_recv()
        for j, chip in enumerate(chips):
            copy(4 + j, 1, (*chip, 1 - c), me).wait_recv()
        keep = pltpu.make_async_copy(wv, wt_o, loc_sems.at[4])
        for p in range(3):
            out_copy(p, p).wait()
        for cp in first + passed:
            cp.wait_send()
        keep.wait()
        for cp in locs:
            cp.wait()

    vmem = pl.BlockSpec(memory_space=pltpu.VMEM)
    bf16 = jnp.bfloat16
    return pl.pallas_call(
        body, name="gather_proj",
        out_shape=(pltpu.HBM((D, T), bf16), pltpu.HBM((NTILE, D, TILE), bf16),
                   pltpu.HBM((DMIX // NDEV, D), bf16), pltpu.HBM((NDEV, DMIX // NDEV, D), bf16),
                   pltpu.HBM((NDEV, NGROUP, GROUP // NDEV, GROUP), bf16), pltpu.HBM((NTILE, T, TILE), jnp.float32)),
        in_specs=[ANY] + [vmem] * 4, out_specs=(ANY,) * 6,
        scratch_shapes=[pltpu.VMEM((T, D), jnp.float32),
                        pltpu.VMEM((T, D), bf16), pltpu.VMEM((D, T), bf16), pltpu.VMEM((NTILE, D, TILE), bf16),
                        pltpu.VMEM((DMIX // NDEV, D), bf16), pltpu.VMEM((NGROUP, GROUP // NDEV, GROUP), bf16),
                        pltpu.VMEM((3, T, TILE), jnp.float32),
                        pltpu.SemaphoreType.DMA((14,)), pltpu.SemaphoreType.DMA((14,)),
                        pltpu.SemaphoreType.DMA((6,)), pltpu.SemaphoreType.DMA((3,))],
        compiler_params=_params(vmem_mib=56),
    )(x, g1, w_in, w_out, pool_w)


def _split_start(name, arrays, n_copies, plan):
    k = len(arrays)

    def body(*refs):
        send_sems, recv_sems, token = refs[k], refs[k + 1], refs[-1]
        for i, (src, dst, to) in enumerate(plan(refs[:k])):
            pltpu.make_async_remote_copy(src_ref=src, dst_ref=dst, send_sem=send_sems.at[i],
                                         recv_sem=recv_sems.at[i], device_id=to, device_id_type=MESH).start()
        token[...] = jnp.zeros_like(token)

    out = pl.pallas_call(
        body, name=name,
        out_shape=(pltpu.SemaphoreType.DMA((n_copies,)), pltpu.SemaphoreType.DMA((n_copies,)),
                   *[pltpu.HBM(a.shape, a.dtype) for a in arrays], jax.ShapeDtypeStruct((8, 128), jnp.float32)),
        in_specs=[HBM] * k, out_specs=(SEM, SEM, *[HBM] * k, pl.BlockSpec(memory_space=pltpu.VMEM)),
        input_output_aliases={i: 2 + i for i in range(k)},
        compiler_params=pltpu.CompilerParams(has_side_effects=EFFECT),
    )(*[pltpu.with_memory_space_constraint(a, pltpu.HBM) for a in arrays])
    return out[0], out[1], out[2:2 + k], out[-1]


def _split_wait(name, arrays, send_sems, recv_sems, plan, after):
    k = len(arrays)

    def body(*refs):
        sends, recvs = refs[k], refs[k + 1]
        for i, (src, dst, to) in enumerate(plan(refs[:k])):
            cp = pltpu.make_async_remote_copy(src_ref=src, dst_ref=dst, send_sem=sends.at[i], recv_sem=recvs.at[i],
                                              device_id=to, device_id_type=MESH)
            cp.wait_send()
            cp.wait_recv()

    return pl.pallas_call(
        body, name=name,
        out_shape=tuple(pltpu.HBM(a.shape, a.dtype) for a in arrays),
        in_specs=[HBM] * k + [SEM, SEM, ANY], out_specs=(HBM,) * k,
        input_output_aliases={i: i for i in range(k)},
        compiler_params=pltpu.CompilerParams(has_side_effects=EFFECT),
    )(*arrays, send_sems, recv_sems, after)


def _plan_wout(refs):
    src, land = refs
    x, y, c, me = _place()
    return [(src, land.at[me], _peer(x, y, c, r)) for r in range(1, NDEV)]


def _plan_rest(refs):
    gob, gpf, r_out, r_pool = refs
    x, y, c, me = _place()
    plan = []
    for r in range(1, NDEV):
        plan.append((gob.at[me ^ r], r_out.at[r - 1], _peer(x, y, c, r)))
        plan.append((gpf.at[me ^ r], r_pool.at[r - 1], _peer(x, y, c, r)))
    return plan


def _plan_in(refs):
    sums, landing = refs
    x, y, c, _ = _place()
    plan = []
    for rel, (dx, dy) in enumerate(((1, 0), (0, 1), (1, 1))):
        for p in range(3):
            plan.append((sums.at[rel, p], landing.at[rel, p], (x ^ dx, y ^ dy, c)))
    return plan


def _plan_small(refs):
    small, land = refs
    x, y, c, me = _place()
    return [(small, land.at[me], _peer(x, y, c, r)) for r in range(1, NDEV)]


def _seg_tiles(s):
    return (s + 2) % NSEG


_POOL_SPECS = [pl.BlockSpec((None, T, GROUP), lambda g, base=base: (base + g, 0, 0)) for base in (0, 4)]
_HEAD_SPECS = [pl.BlockSpec((None, T, HEAD), lambda h, base=base: (base + h // 2, 0, h % 2))
               for base in (8, 12, 16, 20)]


def _row_ids(shape):
    return lax.broadcasted_iota(jnp.int32, shape, 0)


BAND_ROWS = 128
HALO = 16


def _window_sum(a, gidx, lead):
    width = lax.shift_left(jnp.int32(2), gidx)
    shape = (BAND_ROWS, BAND_ROWS + HALO)
    t, j = lax.broadcasted_iota(jnp.int32, shape, 0), lax.broadcasted_iota(jnp.int32, shape, 1)
    first = t if lead else t + HALO - width + 1
    band = _bf(jnp.where(j >= first, jnp.where(j < first + width, 1.0, 0.0), 0.0))
    zeros = jnp.zeros((HALO, a.shape[1]), jnp.bfloat16)
    padded = [jnp.concatenate([p, zeros] if lead else [zeros, p], axis=0) for p in _split2(a)]
    out = []
    for r0 in range(0, T, BAND_ROWS):
        slab = jnp.concatenate([p[r0:r0 + BAND_ROWS + HALO] for p in padded], axis=1)
        r = _dot(band, slab, 1, 0)
        out.append(r[:, :a.shape[1]] + r[:, a.shape[1]:])
    return jnp.concatenate(out, axis=0)


def _window_mean(s, gidx):
    inv = jnp.where(gidx == 0, 0.5, jnp.where(gidx == 1, 0.25, jnp.where(gidx == 2, 0.125, 0.0625)))
    width = lax.shift_left(jnp.int32(2), gidx)
    head = s[:16] / jnp.minimum(_row_ids((16, s.shape[1])) + 1, width).astype(jnp.float32)
    return jnp.concatenate([head, s[16:] * inv], axis=0)


def _pool_fwd(proj, pool_w, pool_scale, token):
    def body(u_ref, pg_ref, w_ref, sc_ref, token_any, y_ref):
        del token_any
        gidx = pl.program_id(0)
        u, pg = u_ref[...], pg_ref[...]
        d = _window_mean(_window_sum(u, gidx, False), gidx) - u
        mixed = _dot(_bf(d), w_ref[...], 1, 0)
        y_ref[...] = _bf(mixed * sc_ref[...] * (pg * _sigmoid(pg)))

    return pl.pallas_call(
        body, name="pool_fwd", grid=(NGROUP,),
        in_specs=[*_POOL_SPECS,
                  pl.BlockSpec((None, GROUP, GROUP), lambda g: (g, 0, 0)),
                  pl.BlockSpec((1, GROUP), lambda g: (0, g)), ANY],
        out_specs=pl.BlockSpec((T, GROUP), lambda g: (0, g)),
        out_shape=pltpu.HBM((T, DMIX), jnp.bfloat16),
        compiler_params=_params(("parallel",)),
    )(proj, proj, pool_w, pool_scale, token)


def _tri(lower):
    r = lax.broadcasted_iota(jnp.int32, (CHUNK, CHUNK), 0)
    c = lax.broadcasted_iota(jnp.int32, (CHUNK, CHUNK), 1)
    return (r >= c) if lower else (r <= c)


def _sum_rows_matrix():
    shape = (CHUNK + 16, CHUNK)
    r, c = lax.broadcasted_iota(jnp.int32, shape, 0), lax.broadcasted_iota(jnp.int32, shape, 1)
    run = jnp.where(c <= r, 1.0, 0.0)
    half = jnp.where(c < CHUNK // 2, 1.0, 0.0)
    return _bf(jnp.where(r < CHUNK, run, jnp.where(r < CHUNK + 8, 1.0, half)))


def _rev_sum_matrix():
    shape = (CHUNK, 2 * CHUNK)
    r, c = lax.broadcasted_iota(jnp.int32, shape, 0), lax.broadcasted_iota(jnp.int32, shape, 1)
    return _bf(jnp.where(c < CHUNK, jnp.where(c >= r, 1.0, 0.0), jnp.where(c - CHUNK < r, 1.0, 0.0)))


def _split2(a):
    hi = _bf(a)
    return [hi, _bf(a - hi.astype(jnp.float32))]


def _exact_sums(mat, pieces):
    x = jnp.concatenate([s for p in pieces for s in _split2(p)], axis=1)
    r = _dot(mat, x, 1, 0)
    return [r[:, 2 * j * HEAD:(2 * j + 1) * HEAD] + r[:, (2 * j + 1) * HEAD:(2 * j + 2) * HEAD]
            for j in range(len(pieces))]


def _gates(qv, fl, lb):
    sq = _sigmoid(qv)
    sg = _sigmoid(fl)
    f = lb + (1.0 - lb) * sg
    return dict(sq=sq, qs=qv * sq, sg=sg, f=f, kk=1.0 - f, g=jnp.log2(f))


def _decays(sums):
    big_g = sums[:CHUNK]
    total = sums[CHUNK:CHUNK + 8]
    g_last = jnp.tile(total, (CHUNK // 8, 1))
    g_mid = jnp.tile(sums[CHUNK + 8:], (CHUNK // 8, 1))
    return dict(
        e_q=jnp.exp2(big_g),
        e_k=jnp.exp2(g_last - big_g),
        e_qm=jnp.exp2(jnp.minimum(big_g - g_mid, EXP_CAP)),
        e_km=jnp.exp2(jnp.minimum(g_mid - big_g, EXP_CAP)),
        total8=jnp.exp2(total))


def _group_rows(gi):
    return [pl.ds(pl.multiple_of((gi * NB + j) * CHUNK, CHUNK), CHUNK) for j in range(NB)]


def _lower_bound(lb_ref):
    return _sigmoid(lb_ref[0:1, :] - lb_ref[1:2, :])


def _hgrn_fwd(proj, lb_logits, rec_g, y_in):
    def body(q_ref, f_ref, i_ref, gate_ref, lb_ref, rg_ref, y_any, y_ref, o_ref, st_ref):
        del y_any
        lb = _lower_bound(lb_ref)
        causal = _tri(True)
        smat = _sum_rows_matrix()

        def group(gi, st):
            rows = _group_rows(gi)
            ts = [_gates(q_ref[r, :], f_ref[r, :], lb) for r in rows]
            ds = [_decays(s) for s in _exact_sums(smat, [t["g"] for t in ts])]
            vs = [_bf(i_ref[r, :]) for r in rows]
            q_m = [_bf(t["qs"] * d["e_qm"]) for t, d in zip(ts, ds)]
            k_m = [_bf(t["kk"] * d["e_km"]) for t, d in zip(ts, ds)]
            q_e = [_bf(t["qs"] * d["e_q"]) for t, d in zip(ts, ds)]
            k_e = [_bf(t["kk"] * d["e_k"]) for t, d in zip(ts, ds)]
            a = [_bf(jnp.where(causal, _dot(q_m[j], k_m[j], 1, 1), 0.0)) for j in range(NB)]
            intra = [_dot(a[j], vs[j], 1, 0) for j in range(NB)]
            upd = [_dot(vs[j], k_e[j], 0, 0) for j in range(NB)]
            for j in range(NB):
                st_ref[gi * NB + j] = st
                o_ref[rows[j], :] = intra[j] + _dot(q_e[j], _bf(st), 1, 1)
                st = st * jnp.tile(ds[j]["total8"], (HEAD // 8, 1)) + upd[j]
            return st

        lax.fori_loop(0, NGRP, group, jnp.zeros((HEAD, HEAD), jnp.float32))
        o = o_ref[...]
        rn = o * lax.rsqrt(jnp.mean(o * o, axis=-1, keepdims=True) + EPS)
        gate = gate_ref[...]
        y_ref[...] = _bf(rn * rg_ref[...] * (gate * _sigmoid(gate)))

    return pl.pallas_call(
        body, name="hgrn_fwd", grid=(NHEAD,),
        in_specs=[*_HEAD_SPECS,
                  pl.BlockSpec((2, HEAD), lambda h: (0, h)),
                  pl.BlockSpec((1, HEAD), lambda h: (0, h)),
                  pl.BlockSpec(memory_space=pl.ANY)],
        out_specs=(pl.BlockSpec((T, HEAD), lambda h: (0, NHEAD + h)),
                   pl.BlockSpec((T, HEAD), lambda h: (0, h)),
                   pl.BlockSpec((None, NCHUNK, HEAD, HEAD), lambda h: (h, 0, 0, 0))),
        out_shape=(pltpu.HBM((T, DMIX), jnp.bfloat16), pltpu.HBM((T, D), jnp.float32),
                   pltpu.HBM((NHEAD, NCHUNK, HEAD, HEAD), jnp.float32)),
        input_output_aliases={6: 0},
        compiler_params=_params(("parallel",)),
    )(proj, proj, proj, proj, lb_logits, rec_g, y_in)


def _out_proj_loss(x, y, w_out, target, gf):
    rows = 512
    parts = [slice(k * rows // 2, (k + 1) * rows // 2) for k in range(2)]

    def body(x_ref, y_ref, w_ref, t_ref, g_ref, dz_ref, dzb_ref, sq_ref, dg_ref):
        zs = [x_ref[p, :] + _dot(y_ref[p, :], w_ref[...], 1, 0) for p in parts]
        sq = dg = 0.0
        for p, z in zip(parts, zs):
            r = lax.rsqrt(jnp.mean(z * z, axis=-1, keepdims=True) + EPS)
            zhat = z * r
            err = zhat * g_ref[...] - t_ref[p, :]
            dy = err * (1.0 / D)
            gdy = dy * g_ref[...]
            dz = r * (gdy - zhat * jnp.mean(zhat * gdy, axis=-1, keepdims=True))
            dz_ref[p, :] = dz
            dzb_ref[p, :] = _bf(dz)
            sq = sq + jnp.sum(err * err, axis=0, keepdims=True)
            dg = dg + jnp.sum(zhat * dy, axis=0, keepdims=True)

        @pl.when(pl.program_id(0) == 0)
        def _():
            sq_ref[...] = sq
            dg_ref[...] = dg

        @pl.when(pl.program_id(0) != 0)
        def _():
            sq_ref[...] += sq
            dg_ref[...] += dg

    tile = pl.BlockSpec((rows, D), lambda i: (i, 0))
    vec = pl.BlockSpec((1, D), lambda i: (0, 0))
    return pl.pallas_call(
        body, name="out_proj_loss", grid=(T // rows,),
        in_specs=[tile, pl.BlockSpec((rows, DMIX), lambda i: (i, 0)), pl.BlockSpec((DMIX, D), lambda i: (0, 0)),
                  tile, vec],
        out_specs=(tile, tile, vec, vec),
        out_shape=(pltpu.HBM((T, D), jnp.float32), pltpu.HBM((T, D), jnp.bfloat16),
                   jax.ShapeDtypeStruct((1, D), jnp.float32), jax.ShapeDtypeStruct((1, D), jnp.float32)),
        compiler_params=_params(("arbitrary",)),
    )(x, y, w_out, target, gf)


def _out_proj_bwd(dzb, w_out, y):
    tn = 512

    def body(dz_ref, w_ref, y_ref, dy_ref, gw_ref, gwb_ref):
        dz = dz_ref[...]
        dy_ref[...] = _dot(dz, w_ref[...], 1, 1)
        gw = _dot(y_ref[...], dz, 0, 0)
        gw_ref[...] = gw
        gwb_ref[...] = _bf(gw)

    return pl.pallas_call(
        body, name="out_proj_bwd", grid=(DMIX // tn,),
        in_specs=[pl.BlockSpec((T, D), lambda n: (0, 0)), pl.BlockSpec((tn, D), lambda n: (n, 0)),
                  pl.BlockSpec((T, tn), lambda n: (0, n))],
        out_specs=(pl.BlockSpec((T, tn), lambda n: (0, n)), pl.BlockSpec((tn, D), lambda n: (n, 0)),
                   pl.BlockSpec((tn, D), lambda n: (n, 0))),
        out_shape=(pltpu.HBM((T, DMIX), jnp.float32), pltpu.HBM((DMIX, D), jnp.float32),
                   pltpu.HBM((DMIX, D), jnp.bfloat16)),
        compiler_params=_params(("parallel",)),
    )(dzb, w_out, y)


def _hgrn_bwd(proj, lb_logits, rec_g, o, states, dymix, dproj_in, token):
    def body(q_ref, f_ref, i_ref, gate_ref, lb_ref, rg_ref, o_ref, st_ref, dy_ref, dp_any, token_any,
             dp_ref, drg_ref, dlb_ref, do_ref):
        del dp_any, token_any
        lb = _lower_bound(lb_ref)
        causal = _tri(True)
        smat, rmat = _sum_rows_matrix(), _rev_sum_matrix()

        o = o_ref[...]
        rs = lax.rsqrt(jnp.mean(o * o, axis=-1, keepdims=True) + EPS)
        rn = o * rs
        gate = gate_ref[...]
        sgate = _sigmoid(gate)
        dyv = dy_ref[...]
        d_r = dyv * (gate * sgate)
        dp_ref[3] = _bf(dyv * (rn * rg_ref[...]) * (sgate * (1.0 + gate * (1.0 - sgate))))
        drg_ref[...] = jnp.sum(d_r * rn, axis=0, keepdims=True)
        drn = d_r * rg_ref[...]
        do_ref[...] = rs * (drn - rn * jnp.mean(rn * drn, axis=-1, keepdims=True))

        def group(i, carry):
            dst, dlb = carry
            gi = NGRP - 1 - i
            rows = _group_rows(gi)
            span = range(NB)
            qvs = [q_ref[r, :] for r in rows]
            ts = [_gates(qv, f_ref[r, :], lb) for qv, r in zip(qvs, rows)]
            ds = [_decays(s) for s in _exact_sums(smat, [t["g"] for t in ts])]
            vs = [_bf(i_ref[r, :]) for r in rows]
            dos = [_bf(do_ref[r, :]) for r in rows]
            sts = [st_ref[gi * NB + j] for j in span]
            qe_f = [t["qs"] * d["e_q"] for t, d in zip(ts, ds)]
            ke_f = [t["kk"] * d["e_k"] for t, d in zip(ts, ds)]
            q_e, k_e = [_bf(a) for a in qe_f], [_bf(a) for a in ke_f]
            q_m = [_bf(t["qs"] * d["e_qm"]) for t, d in zip(ts, ds)]
            k_m = [_bf(t["kk"] * d["e_km"]) for t, d in zip(ts, ds)]
            a = [_bf(jnp.where(causal, _dot(q_m[j], k_m[j], 1, 1), 0.0)) for j in span]
            da = [_bf(jnp.where(causal, _dot(dos[j], vs[j], 1, 1), 0.0)) for j in span]
            dqm = [_dot(da[j], k_m[j], 1, 0) for j in span]
            dkm = [_dot(da[j], q_m[j], 0, 0) for j in span]
            dv_in = [_dot(a[j], dos[j], 0, 0) for j in span]
            dqe = [_dot(dos[j], _bf(sts[j]), 1, 0) for j in span]
            grow = [_dot(dos[j], q_e[j], 0, 0) for j in span]
            dke, carried = [None] * NB, [None] * NB
            for j in reversed(span):
                dst_b = _bf(dst)
                dke[j] = _dot(vs[j], dst_b, 1, 0)
                dp_ref[2, rows[j], :] = _bf(dv_in[j] + _dot(k_e[j], dst_b, 1, 1))
                carried[j] = ds[j]["total8"] * jnp.sum(dst * sts[j], axis=0, keepdims=True)
                dst = dst * jnp.tile(ds[j]["total8"], (HEAD // 8, 1)) + grow[j]
            kdk = [ke_f[j] * dke[j] for j in span]
            pos = [(q_m[j].astype(jnp.float32) * dqm[j] - k_m[j].astype(jnp.float32) * dkm[j]) + qe_f[j] * dqe[j]
                   for j in span]
            dgs = _exact_sums(rmat, [jnp.concatenate([pos[j], kdk[j]], axis=0) for j in span])
            for j in span:
                t, d = ts[j], ds[j]
                dg = dgs[j] + jnp.tile(carried[j], (CHUNK // 8, 1))
                dqs = dqm[j] * d["e_qm"] + dqe[j] * d["e_q"]
                dkk = dkm[j] * d["e_km"] + dke[j] * d["e_k"]
                df = dg / t["f"] - dkk
                dp_ref[1, rows[j], :] = _bf(df * (1.0 - lb) * (t["sg"] * (1.0 - t["sg"])))
                dp_ref[0, rows[j], :] = _bf(dqs * (t["sq"] * (1.0 + qvs[j] * (1.0 - t["sq"]))))
                dlb = dlb + df * (1.0 - t["sg"])
            return dst, dlb

        _, dlb = lax.fori_loop(0, NGRP, group, (jnp.zeros((HEAD, HEAD), jnp.float32),
                                                jnp.zeros((CHUNK, HEAD), jnp.float32)))
        dlb_ref[...] = jnp.sum(dlb, axis=0, keepdims=True)

    vec = pl.BlockSpec((1, HEAD), lambda h: (0, h))
    return pl.pallas_call(
        body, name="hgrn_bwd", grid=(NHEAD,),
        in_specs=[*_HEAD_SPECS,
                  pl.BlockSpec((2, HEAD), lambda h: (0, h)), vec,
                  pl.BlockSpec((T, HEAD), lambda h: (0, h)),
                  pl.BlockSpec((None, NCHUNK, HEAD, HEAD), lambda h: (h, 0, 0, 0)),
                  pl.BlockSpec((T, HEAD), lambda h: (0, NHEAD + h)), ANY, ANY],
        out_specs=(pl.BlockSpec((4, T, HEAD), lambda h: (0, 0, h)), vec, vec),
        out_shape=(pltpu.HBM((NSEG, T, D), jnp.bfloat16),
                   jax.ShapeDtypeStruct((1, D), jnp.float32), jax.ShapeDtypeStruct((1, D), jnp.float32)),
        scratch_shapes=[pltpu.VMEM((T, HEAD), jnp.float32)],
        input_output_aliases={9: 0},
        compiler_params=_params(("parallel",)),
    )(proj, proj, proj, proj, lb_logits, rec_g, o, states, dymix, dproj_in, token)


def _pool_bwd(proj, pool_w, pool_scale, dymix):
    def body(u_ref, pg_ref, w_ref, sc_ref, dy_ref, dp_ref, gw_ref, gs_ref):
        gidx = pl.program_id(0)
        u, pg = u_ref[...], pg_ref[...]
        d = _bf(_window_mean(_window_sum(u, gidx, False), gidx) - u)
        mixed = _dot(d, w_ref[...], 1, 0)
        spg = _sigmoid(pg)
        dyv = dy_ref[...]
        d_p = dyv * (pg * spg)
        dp_ref[1] = _bf(dyv * (mixed * sc_ref[...]) * (spg * (1.0 + pg * (1.0 - spg))))
        gs_ref[...] = jnp.sum(d_p * mixed, axis=0, keepdims=True)
        dmixed = _bf(d_p * sc_ref[...])
        gw_ref[...] = _dot(d, dmixed, 0, 0)
        dd = _dot(dmixed, w_ref[...], 1, 1)
        dp_ref[0] = _bf(_window_sum(_window_mean(dd, gidx), gidx, True) - dd)

    return pl.pallas_call(
        body, name="pool_bwd", grid=(NGROUP,),
        in_specs=[*_POOL_SPECS,
                  pl.BlockSpec((None, GROUP, GROUP), lambda g: (g, 0, 0)),
                  pl.BlockSpec((1, GROUP), lambda g: (0, g)),
                  pl.BlockSpec((T, GROUP), lambda g: (0, g))],
        out_specs=(pl.BlockSpec((2, T, GROUP), lambda g: (2, 0, g)),
                   pl.BlockSpec((None, GROUP, GROUP), lambda g: (g, 0, 0)),
                   pl.BlockSpec((1, GROUP), lambda g: (0, g))),
        out_shape=(pltpu.HBM((NSEG, T, D), jnp.bfloat16),
                   jax.ShapeDtypeStruct((NGROUP, GROUP, GROUP), jnp.float32),
                   jax.ShapeDtypeStruct((1, D), jnp.float32)),
        compiler_params=_params(("parallel",)),
    )(proj, proj, pool_w, pool_scale, dymix)


HALF = NTILE // 2
AWAY = HALF - 3


def _dproj_tile(chip, side, p):
    j = 6 * chip + 3 * side + p
    return ((j // 4 + 4) % NSEG, 0, j % 4)


def _sibling_copy(sib_out, sib_in, send_sems, recv_sems, slot):
    x, y, c, _ = _place()
    return pltpu.make_async_remote_copy(
        src_ref=sib_out.at[slot], dst_ref=sib_in.at[slot], send_sem=send_sems.at[slot],
        recv_sem=recv_sems.at[slot], device_id=(x, y, 1 - c), device_id_type=MESH)


def _proj_bwd_w_far(place, ht, dproj):
    def body(place_ref, h_ref, dp_ref, sib_out, sib_in, send_sems, recv_sems, stage, loc_sems):
        del place_ref
        i = pl.program_id(0)

        def to_hbm(k):
            return pltpu.make_async_copy(stage.at[k], sib_out.at[k], loc_sems.at[k])

        def send(k):
            to_hbm(k).wait()
            _sibling_copy(sib_out, sib_in, send_sems, recv_sems, k).start()

        stage[i] = _bf(_dot(h_ref[...], dp_ref[...], 1, 0))

        @pl.when(i > 0)
        def _():
            send(i - 1)

        to_hbm(i).start()

        @pl.when(i == HALF - 1)
        def _():
            send(i)

    buf = pltpu.HBM((HALF, D, TILE), jnp.bfloat16)
    sems = pltpu.SemaphoreType.DMA((HALF,))
    return pl.pallas_call(
        body, name="proj_bwd_w_far",
        grid_spec=pltpu.PrefetchScalarGridSpec(
            num_scalar_prefetch=1, grid=(HALF,),
            in_specs=[pl.BlockSpec((D, T), lambda i, pr: (0, 0)),
                      pl.BlockSpec((None, T, TILE), lambda i, pr: _dproj_tile(i // 3, 1 - pr[2], i % 3))],
            out_specs=(HBM, HBM, SEM, SEM),
            scratch_shapes=[pltpu.VMEM((HALF, D, TILE), jnp.bfloat16), pltpu.SemaphoreType.DMA((HALF,))]),
        out_shape=(buf, buf, sems, sems),
        compiler_params=pltpu.CompilerParams(dimension_semantics=("arbitrary",), vmem_limit_bytes=48 * MIB,
                                             has_side_effects=EFFECT),
    )(place, ht, dproj)


def _proj_bwd_w_near(place, ht, dproj, sib_out, sib_in, sib_send, sib_recv):
    def owner_chip(k, pr):
        return jnp.where(k < AWAY, (pr[1] + 1 + k % 3) % 4, pr[1])

    def tile_p(k):
        return jnp.where(k < AWAY, k // 3, k - AWAY)

    def body(place_ref, h_ref, dp_ref, sib_out, sib_in, sib_send, sib_recv, sums, own_ref, landing, out_send,
             out_recv, recvbuf, outbuf, in_sems, loc_sems):
        i = pl.program_id(0)
        px, py, c, _ = _place()

        def slot_of(k):
            return 3 * owner_chip(k, place_ref) + tile_p(k)

        def load(k):
            return pltpu.make_async_copy(sib_in.at[slot_of(k)], recvbuf.at[k % 2], in_sems.at[k % 2])

        def fetch(k):
            _sibling_copy(sib_out, sib_in, sib_send, sib_recv, slot_of(k)).wait_recv()
            load(k).start()

        def route(k):
            chip = owner_chip(k, place_ref)
            cx, cy = chip // 2, chip % 2
            return cx, cy, (cx ^ px) + 2 * (cy ^ py) - 1, tile_p(k)

        def to_hbm(k):
            _, _, rel, p = route(k)
            return pltpu.make_async_copy(outbuf.at[k], sums.at[rel, p], loc_sems.at[k])

        def to_owner(k):
            cx, cy, rel, p = route(k)
            return pltpu.make_async_remote_copy(
                src_ref=sums.at[rel, p], dst_ref=landing.at[rel, p], send_sem=out_send.at[3 * rel + p],
                recv_sem=out_recv.at[3 * rel + p], device_id=(cx, cy, c), device_id_type=MESH)

        @pl.when(i == 0)
        def _():
            fetch(i)

        @pl.when(i < HALF - 1)
        def _():
            fetch(i + 1)

        gw = _dot(h_ref[...], dp_ref[...], 1, 0)

        @pl.when(jnp.logical_and(i > 0, i <= AWAY))
        def _():
            to_hbm(i - 1).wait()
            to_owner(i - 1).start()

        load(i).wait()
        total = gw + recvbuf[i % 2].astype(jnp.float32)

        @pl.when(i < AWAY)
        def _():
            outbuf[i] = _bf(total)
            to_hbm(i).start()

        @pl.when(i >= AWAY)
        def _():
            own_ref[...] = total

        @pl.when(i == HALF - 1)
        def _():
            for slot in range(HALF):
                _sibling_copy(sib_out, sib_in, sib_send, sib_recv, slot).wait_send()

    travelling = pltpu.HBM((3, 3, D, TILE), jnp.bfloat16)
    sems = pltpu.SemaphoreType.DMA((AWAY,))
    return pl.pallas_call(
        body, name="proj_bwd_w_near",
        grid_spec=pltpu.PrefetchScalarGridSpec(
            num_scalar_prefetch=1, grid=(HALF,),
            in_specs=[pl.BlockSpec((D, T), lambda i, pr: (0, 0)),
                      pl.BlockSpec((None, T, TILE), lambda i, pr: _dproj_tile(owner_chip(i, pr), pr[2], tile_p(i))),
                      HBM, HBM, SEM, SEM],
            out_specs=(HBM, pl.BlockSpec((None, D, TILE), lambda i, pr: (jnp.where(i < AWAY, 0, i % 3), 0, 0)),
                       HBM, SEM, SEM),
            scratch_shapes=[pltpu.VMEM((2, D, TILE), jnp.bfloat16), pltpu.VMEM((AWAY, D, TILE), jnp.bfloat16),
                            pltpu.SemaphoreType.DMA((2,)), pltpu.SemaphoreType.DMA((AWAY,))]),
        out_shape=(travelling, pltpu.HBM((3, D, TILE), jnp.float32), travelling, sems, sems),
        compiler_params=pltpu.CompilerParams(dimension_semantics=("arbitrary",), vmem_limit_bytes=48 * MIB,
                                             has_side_effects=EFFECT),
    )(place, ht, dproj, sib_out, sib_in, sib_send, sib_recv)


def _proj_bwd_x(dproj, w_t, x, g1, dz, token):
    tm = 512
    pairs = NSEG // 2

    def body(dp_ref, w_ref, x_ref, g_ref, dz_ref, token_any, dx_ref, dg_ref, wcat, acc):
        del token_any
        s, m = pl.program_id(0), pl.program_id(1)

        @pl.when(m == 0)
        def _():
            for i in range(8):
                wcat[:, i * TILE:(i + 1) * TILE] = w_ref[i]

        mine = pl.ds(pl.multiple_of(m * tm, tm), tm)

        @pl.when(s == 0)
        def _():
            acc[mine, :] = jnp.zeros((tm, D), jnp.float32)

        acc[mine, :] += _dot(jnp.concatenate([dp_ref[0], dp_ref[1]], axis=1), wcat[...], 1, 1)

        @pl.when(s == pairs - 1)
        def _():
            xv = x_ref[...]
            rs = lax.rsqrt(jnp.mean(xv * xv, axis=-1, keepdims=True) + EPS)
            xhat = xv * rs
            dhv = acc[mine, :]
            gdh = dhv * g_ref[...]
            dx_ref[...] = dz_ref[...] + rs * (gdh - xhat * jnp.mean(xhat * gdh, axis=-1, keepdims=True))
            dg = jnp.sum(xhat * dhv, axis=0, keepdims=True)

            @pl.when(m == 0)
            def _():
                dg_ref[...] = dg

            @pl.when(m != 0)
            def _():
                dg_ref[...] += dg

    rows = pl.BlockSpec((tm, D), lambda s, m: (jnp.where(s == pairs - 1, m, 0), 0))
    vec = pl.BlockSpec((1, D), lambda s, m: (0, 0))
    return pl.pallas_call(
        body, name="proj_bwd_x", grid=(pairs, T // tm),
        in_specs=[pl.BlockSpec((2, tm, D), lambda s, m: (s, m, 0)),
                  pl.BlockSpec((8, D, TILE), lambda s, m: ((s + 1) % pairs, 0, 0)), rows, vec, rows, ANY],
        out_specs=(rows, vec),
        out_shape=(jax.ShapeDtypeStruct((T, D), jnp.float32), jax.ShapeDtypeStruct((1, D), jnp.float32)),
        scratch_shapes=[pltpu.VMEM((D, 2 * D), jnp.bfloat16), pltpu.VMEM((T, D), jnp.float32)],
        compiler_params=_params(("arbitrary", "arbitrary"), vmem_mib=56),
    )(dproj, w_t, x, g1, dz, token)


def _adamw(w, g, m, v):
    m_new = ADAM_B1 * m + (1.0 - ADAM_B1) * g
    v_new = ADAM_B2 * v + (1.0 - ADAM_B2) * (g * g)
    delta = -ADAM_LR * ((m_new / BC1) / (jnp.sqrt(v_new / BC2) + ADAM_EPS) + ADAM_WD * w)
    return delta, m_new, v_new


def _reduce_adam(name, place, parts, w, m, v, grid, w_spec):
    n = len(parts)

    def body(place_ref, *refs):
        del place_ref
        w_ref, m_ref, v_ref, g_ref, d_ref, mo_ref, vo_ref = refs[n:]
        g = None
        for ref, (_, _, stacked) in zip(refs[:n], parts):
            terms = [ref[r] for r in range(ref.shape[0])] if stacked else [ref[...]]
            for t in terms:
                if t.shape[-1] != w_ref.shape[-1]:
                    t = jnp.concatenate([t[p] for p in range(t.shape[0])], axis=1)
                g = t.astype(jnp.float32) if g is None else g + t.astype(jnp.float32)
        delta, m_new, v_new = _adamw(w_ref[...], g, m_ref[...], v_ref[...])
        g_ref[...] = g
        d_ref[...] = delta
        mo_ref[...] = m_new
        vo_ref[...] = v_new

    shape = jax.ShapeDtypeStruct(w.shape, jnp.float32)
    return pl.pallas_call(
        body, name=name,
        grid_spec=pltpu.PrefetchScalarGridSpec(
            num_scalar_prefetch=1, grid=grid,
            in_specs=[spec for _, spec, _ in parts] + [w_spec] * 3, out_specs=(w_spec,) * 4),
        out_shape=(shape,) * 4,
        compiler_params=_params(("parallel",)),
    )(place, *[_in_hbm(a) for a in [a for a, _, _ in parts] + [w, m, v]])


def _small_adam(place, own, parts, w, m, v):
    def body(place_ref, own_ref, p_ref, w_ref, m_ref, v_ref, g_ref, d_ref, mo_ref, vo_ref):
        me = place_ref[0]
        g = None
        for s in range(NDEV):
            term = jnp.where(me == s, own_ref[...], p_ref[s])
            g = term if g is None else g + term
        wv = w_ref[...]
        rows = _row_ids(wv.shape)
        other = jnp.where(rows == 2, pltpu.roll(wv, 7, 0), jnp.where(rows == 3, pltpu.roll(wv, 1, 0), 0.0))
        lbv = _sigmoid(wv - other)
        sign = jnp.where(rows == 2, 1.0, -1.0)
        g = jnp.where((rows == 2) | (rows == 3), sign * g * lbv * (1.0 - lbv), g)
        delta, m_new, v_new = _adamw(wv, g, m_ref[...], v_ref[...])
        g_ref[...] = g
        d_ref[...] = delta
        mo_ref[...] = m_new
        vo_ref[...] = v_new

    shape = jax.ShapeDtypeStruct((8, D), jnp.float32)
    vmem = pl.BlockSpec(memory_space=pltpu.VMEM)
    return pl.pallas_call(
        body, name="small_adam", out_shape=(shape,) * 4,
        in_specs=[pl.BlockSpec(memory_space=pltpu.SMEM)] + [vmem] * 5, out_specs=(vmem,) * 4,
    )(place, own, parts, w, m, v)


def _rows8(*vecs):
    rows = [a.reshape(-1, D) for a in vecs]
    n = sum(r.shape[0] for r in rows)
    return jnp.concatenate(rows + [jnp.zeros((8 - n, D), jnp.float32)], axis=0)


def kernel(x, norm1_g, w_in, pool_w, pool_scale, lb_logits, rec_norm_g, w_out, final_norm_g, loss_target, m_norm1_g, m_w_in, m_pool_w, m_pool_scale, m_lb_logits, m_rec_norm_g, m_w_out, m_final_norm_g, v_norm1_g, v_w_in, v_pool_w, v_pool_scale, v_lb_logits, v_rec_norm_g, v_w_out, v_final_norm_g):
    xs = x[0]
    target = loss_target[0]
    ix, iy, ic = lax.axis_index("x"), lax.axis_index("y"), lax.axis_index("c")
    place = jnp.stack([4 * ix + 2 * iy + ic, 2 * ix + iy, ic]).astype(jnp.int32)
    gf = final_norm_g.reshape(1, D)

    ht, w_t, w_out_b, w_out_g, pool_g, proj = _gather_proj(xs, norm1_g, w_in, w_out, pool_w)
    pool_full = pool_g.transpose(1, 0, 2, 3).reshape(NGROUP, GROUP, GROUP)
    wout = [w_out_b, w_out_g]
    wout_send, wout_recv, wout, wout_token = _split_start("gather_wout_start", wout, NDEV - 1, _plan_wout)

    y = _pool_fwd(proj, pool_full, pool_scale, wout_token)
    y, o, states = _hgrn_fwd(proj, lb_logits, rec_norm_g, y)
    _, w_out_g = _split_wait("gather_wout_wait", wout, wout_send, wout_recv, _plan_wout, o)
    w_out_full = _in_hbm(w_out_g.reshape(DMIX, D))
    dz, dzb, sq, dgf = _out_proj_loss(xs, y, w_out_full, target, gf)

    dymix, gwout_f, gwout_b = _out_proj_bwd(dzb, w_out_full, y)
    dproj, gpool, dscale = _pool_bwd(proj, pool_full, pool_scale, dymix)

    blk_out = (NDEV, DMIX // NDEV, D)
    blk_pool = (NDEV, NGROUP, GROUP // NDEV, GROUP)
    gpool_s = gpool.reshape(NGROUP, NDEV, GROUP // NDEV, GROUP).transpose(1, 0, 2, 3)
    rest = [gwout_b.reshape(blk_out), gpool_s,
            lax.empty((NDEV - 1,) + blk_out[1:], jnp.bfloat16), lax.empty((NDEV - 1,) + blk_pool[1:], jnp.float32)]
    rest_send, rest_recv, rest, rest_token = _split_start("scatter_rest_start", rest, 2 * (NDEV - 1), _plan_rest)

    dproj, drecg, dlb = _hgrn_bwd(proj, lb_logits, rec_norm_g, o, states, dymix, dproj, rest_token)
    chip_sums, own_sum, landing, win_send, win_recv = _proj_bwd_w_near(
        place, ht, dproj, *_proj_bwd_w_far(place, ht, dproj))
    win = [chip_sums, landing]

    grad_x, dg1 = _proj_bwd_x(dproj, w_t, xs, norm1_g, dz, chip_sums)

    small = [_rows8(dg1, dscale, dlb, dlb, drecg, dgf, sq), lax.empty((NDEV, 8, D), jnp.float32)]
    small_send, small_recv, small, small_token = _split_start("gather_small_start", small, NDEV - 1, _plan_small)

    _, gpool_own, r_out, r_pool = _split_wait("scatter_rest_wait", rest, rest_send, rest_recv, _plan_rest,
                                              small_token)
    g_wout, d_wout, m_wout, v_wout = _reduce_adam(
        "adam_w_out", place,
        [(gwout_f.reshape(blk_out), pl.BlockSpec((None,) + blk_out[1:], lambda i, pr: (pr[0], 0, 0)), False),
         (r_out, pl.BlockSpec((NDEV - 1,) + blk_out[1:], lambda i, pr: (0, 0, 0)), True)],
        w_out, m_w_out, v_w_out, (1,), pl.BlockSpec((None,) + blk_out[1:], lambda i, pr: (0, 0, 0)))
    g_pool, d_pool, m_pool, v_pool = _reduce_adam(
        "adam_pool_w", place,
        [(gpool_own, pl.BlockSpec((None,) + blk_pool[1:], lambda i, pr: (pr[0], 0, 0, 0)), False),
         (r_pool, pl.BlockSpec((NDEV - 1,) + blk_pool[1:], lambda i, pr: (0, 0, 0, 0)), True)],
        pool_w, m_pool_w, v_pool_w, (1,), pl.BlockSpec((None,) + blk_pool[1:], lambda i, pr: (0, 0, 0, 0)))

    _, r_in = _split_wait("scatter_win_wait", win, win_send, win_recv, _plan_in, d_pool)
    g_win, d_win, m_win, v_win = _reduce_adam(
        "adam_w_in", place,
        [(own_sum, pl.BlockSpec((3, D // 8, TILE), lambda i, pr: (0, i, 0)), False),
         (r_in, pl.BlockSpec((3, 3, D // 8, TILE), lambda i, pr: (0, 0, i, 0)), True)],
        w_in, m_w_in, v_w_in, (8,), pl.BlockSpec((None, D // 8, 3 * TILE), lambda i, pr: (0, i, 0)))

    own_small, r_small = _split_wait("gather_small_wait", small, small_send, small_recv, _plan_small, d_win)
    g_s, d_s, m_s, v_s = _small_adam(
        place, own_small, r_small,
        _rows8(norm1_g, pool_scale, lb_logits, rec_norm_g, final_norm_g),
        _rows8(m_norm1_g, m_pool_scale, m_lb_logits, m_rec_norm_g, m_final_norm_g),
        _rows8(v_norm1_g, v_pool_scale, v_lb_logits, v_rec_norm_g, v_final_norm_g))
    loss = jnp.sum(g_s[6]) * (0.5 / D)

    def small_outs(a):
        return a[0:1], a[1:2], a[2:4], a[4:5], a[5]

    def outs(small_a, win, pool, wout):
        n1, ps, lbl, rg, fg = small_outs(small_a)
        return n1, win, pool, ps, lbl, rg, wout, fg

    return (loss, grad_x[None],
            *outs(g_s, g_win, g_pool, g_wout), *outs(d_s, d_win, d_pool, d_wout),
            *outs(m_s, m_win, m_pool, m_wout), *outs(v_s, v_win, v_pool, v_wout))
```

```python
import functools

import jax
import jax.numpy as jnp
from jax import lax
from jax.experimental import pallas as pl
from jax.experimental.pallas import tpu as pltpu

T = 2048
D = 1024
NSEG = 6
NTILE = 24
TILE = 256
DMIX = 2048
NDEV = 8
HEAD = 128
NHEAD = 8
CHUNK = 64
NCHUNK = T // CHUNK
NB = 32
NGRP = NCHUNK // NB
NGROUP = 4
GROUP = 256
EPS = 1e-6
EXP_CAP = 115.0
MESH = pl.DeviceIdType.MESH
AXES = ("x", "y", "c")
ANY = pl.BlockSpec(memory_space=pl.ANY)
HBM = pl.BlockSpec(memory_space=pltpu.HBM)
SEM = pl.BlockSpec(memory_space=pltpu.SEMAPHORE)
EFFECT = pltpu.SideEffectType.DATAFLOW_SIDE_EFFECTING

ADAM_LR = 0.001
ADAM_B1 = 0.9
ADAM_B2 = 0.999
ADAM_EPS = 1e-08
ADAM_WD = 0.01
ADAM_STEP = 10
BC1 = 1.0 - ADAM_B1 ** ADAM_STEP
BC2 = 1.0 - ADAM_B2 ** ADAM_STEP

MIB = 1 << 20


def _params(sem=None, vmem_mib=48):
    return pltpu.CompilerParams(dimension_semantics=sem, vmem_limit_bytes=vmem_mib * MIB)


def _sigmoid(v):
    return 1.0 / (1.0 + jnp.exp(-v))


def _dot(a, b, ca, cb, precision=None):
    return lax.dot_general(a, b, (((ca,), (cb,)), ((), ())), precision=precision,
                           preferred_element_type=jnp.float32)


def _bf(v):
    return v.astype(jnp.bfloat16)


def _in_hbm(a):
    return pltpu.with_memory_space_constraint(a, pltpu.HBM)


def _place():
    x, y, c = lax.axis_index("x"), lax.axis_index("y"), lax.axis_index("c")
    return x, y, c, 4 * x + 2 * y + c


def _peer(x, y, c, r):
    return (x ^ ((r >> 2) & 1), y ^ ((r >> 1) & 1), c ^ (r & 1))


def _gather_proj(x, g1, w_in, w_out, pool_w):
    def body(x_ref, g_ref, win_ref, wout_ref, pool_ref, ht_o, wt_o, woutb_o, wout_o, pool_o, proj_o,
             xbuf, hv, htv, wv, wob, pb, stage, send_sems, recv_sems, loc_sems, out_sems):
        px, py, c, my_idx = _place()
        fetch_x = pltpu.make_async_copy(x_ref, xbuf, loc_sems.at[5])
        fetch_x.start()
        me, sibling = (px, py, c), (px, py, 1 - c)
        chips = [(1 - px, py), (px, 1 - py), (1 - px, 1 - py)]
        for p in range(3):
            wv[3 * my_idx + p] = _bf(win_ref[0, :, p * TILE:(p + 1) * TILE])

        def index(bx, by, bc):
            return 4 * bx + 2 * by + bc

        def slot(w, block):
            return wv.at[pl.ds(3 * index(*block), 3)] if w == 0 else pool_o.at[index(*block)]

        def copy(k, w, block, to, src=None):
            return pltpu.make_async_remote_copy(
                src_ref=slot(w, block) if src is None else src, dst_ref=slot(w, block),
                send_sem=send_sems.at[2 * k + w], recv_sem=recv_sems.at[2 * k + w],
                device_id=to, device_id_type=MESH)

        def save(block):
            at = pl.ds(3 * index(*block), 3)
            pltpu.make_async_copy(wv.at[at], wt_o.at[at], loc_sems.at[4]).start()

        srcs = (slot(0, me), pb)
        first = []
        for w in (0, 1):
            if w == 1:
                pb[...] = _bf(pool_ref[0])
                wob[...] = _bf(wout_ref[0])
            group = [copy(1 + j, w, me, (*chip, c), src=srcs[w]) for j, chip in enumerate(chips[:2])]
            group.append(copy(0, w, me, sibling, src=srcs[w]))
            for cp in group:
                cp.start()
            first += group
        save(me)
        locs = [pltpu.make_async_copy(pb, slot(1, me), loc_sems.at[0]),
                pltpu.make_async_copy(wob, wout_o.at[my_idx], loc_sems.at[1]),
                pltpu.make_async_copy(wob, woutb_o, loc_sems.at[2])]
        for cp in locs:
            cp.start()

        fetch_x.wait()
        xv = xbuf[...]
        hv[...] = _bf(xv * lax.rsqrt(jnp.mean(xv * xv, axis=-1, keepdims=True) + EPS) * g_ref[...])
        rows = 256
        for r0 in range(0, T, rows):
            htv[:, r0:r0 + rows] = hv[r0:r0 + rows, :].T
        locs.append(pltpu.make_async_copy(htv, ht_o, loc_sems.at[3]))
        locs[-1].start()

        def out_copy(p, j):
            return pltpu.make_async_copy(stage.at[p], proj_o.at[j], out_sems.at[p])

        def project(nth, block):
            base = 3 * index(*block)

            def tile(p, carry):
                if nth > 0:
                    out_copy(p, base + p).wait()
                stage[p] = _dot(hv[...], wv[base + p], 1, 0)
                out_copy(p, base + p).start()
                return carry

            lax.fori_loop(0, 3, tile, 0)

        project(0, me)
        copy(0, 0, sibling, me).wait_recv()
        save(sibling)
        project(1, sibling)
        passed = []
        relay_from = (px ^ (1 - c), py ^ c, c)
        relay_to = (px ^ c, py ^ (1 - c), c)

        def arrived(w, j):
            copy(1 + j, w, (*chips[j], c), me).wait_recv()
            passed.append(copy(4 + j, w, (*chips[j], c), sibling))
            passed[-1].start()

        def relay(w):
            passed.append(copy(3, w, relay_from, relay_to))
            passed[-1].start()

        def handed(nth, j):
            copy(4 + j, 0, (*chips[j], 1 - c), me).wait_recv()
            save((*chips[j], 1 - c))
            project(nth, (*chips[j], 1 - c))

        arrived(0, 0)
        arrived(0, 1)
        relay(0)
        for j in range(2):
            save((*chips[j], c))
            project(2 + j, (*chips[j], c))
        handed(4, 0)
        handed(5, 1)
        arrived(1, 0)
        arrived(1, 1)
        relay(1)
        arrived(0, 2)
        save((*chips[2], c))
        project(6, (*chips[2], c))
        handed(7, 2)
        arrived(1, 2)
        copy(0, 1, sibling, me).wait_recv()
        for j, chip in enumerate(chips):
            copy(4 + j, 1, (*chip, 1 - c), me).wait_recv()
        keep = pltpu.make_async_copy(wv, wt_o, loc_sems.at[4])
        for p in range(3):
            out_copy(p, p).wait()
        for cp in first + passed:
            cp.wait_send()
        keep.wait()
        for cp in locs:
            cp.wait()

    vmem = pl.BlockSpec(memory_space=pltpu.VMEM)
    bf16 = jnp.bfloat16
    return pl.pallas_call(
        body, name="gather_proj",
        out_shape=(pltpu.HBM((D, T), bf16), pltpu.HBM((NTILE, D, TILE), bf16),
                   pltpu.HBM((DMIX // NDEV, D), bf16), pltpu.HBM((NDEV, DMIX // NDEV, D), bf16),
                   pltpu.HBM((NDEV, NGROUP, GROUP // NDEV, GROUP), bf16), pltpu.HBM((NTILE, T, TILE), jnp.float32)),
        in_specs=[ANY] + [vmem] * 4, out_specs=(ANY,) * 6,
        scratch_shapes=[pltpu.VMEM((T, D), jnp.float32),
                        pltpu.VMEM((T, D), bf16), pltpu.VMEM((D, T), bf16), pltpu.VMEM((NTILE, D, TILE), bf16),
                        pltpu.VMEM((DMIX // NDEV, D), bf16), pltpu.VMEM((NGROUP, GROUP // NDEV, GROUP), bf16),
                        pltpu.VMEM((3, T, TILE), jnp.float32),
                        pltpu.SemaphoreType.DMA((14,)), pltpu.SemaphoreType.DMA((14,)),
                        pltpu.SemaphoreType.DMA((6,)), pltpu.SemaphoreType.DMA((3,))],
        compiler_params=_params(vmem_mib=56),
    )(x, g1, w_in, w_out, pool_w)


def _split_start(name, arrays, n_copies, plan):
    k = len(arrays)

    def body(*refs):
        send_sems, recv_sems, token = refs[k], refs[k + 1], refs[-1]
        for i, (src, dst, to) in enumerate(plan(refs[:k])):
            pltpu.make_async_remote_copy(src_ref=src, dst_ref=dst, send_sem=send_sems.at[i],
                                         recv_sem=recv_sems.at[i], device_id=to, device_id_type=MESH).start()
        token[...] = jnp.zeros_like(token)

    out = pl.pallas_call(
        body, name=name,
        out_shape=(pltpu.SemaphoreType.DMA((n_copies,)), pltpu.SemaphoreType.DMA((n_copies,)),
                   *[pltpu.HBM(a.shape, a.dtype) for a in arrays], jax.ShapeDtypeStruct((8, 128), jnp.float32)),
        in_specs=[HBM] * k, out_specs=(SEM, SEM, *[HBM] * k, pl.BlockSpec(memory_space=pltpu.VMEM)),
        input_output_aliases={i: 2 + i for i in range(k)},
        compiler_params=pltpu.CompilerParams(has_side_effects=EFFECT),
    )(*[pltpu.with_memory_space_constraint(a, pltpu.HBM) for a in arrays])
    return out[0], out[1], out[2:2 + k], out[-1]


def _split_wait(name, arrays, send_sems, recv_sems, plan, after):
    k = len(arrays)

    def body(*refs):
        sends, recvs = refs[k], refs[k + 1]
        for i, (src, dst, to) in enumerate(plan(refs[:k])):
            cp = pltpu.make_async_remote_copy(src_ref=src, dst_ref=dst, send_sem=sends.at[i], recv_sem=recvs.at[i],
                                              device_id=to, device_id_type=MESH)
            cp.wait_send()
            cp.wait_recv()

    return pl.pallas_call(
        body, name=name,
        out_shape=tuple(pltpu.HBM(a.shape, a.dtype) for a in arrays),
        in_specs=[HBM] * k + [SEM, SEM, ANY], out_specs=(HBM,) * k,
        input_output_aliases={i: i for i in range(k)},
        compiler_params=pltpu.CompilerParams(has_side_effects=EFFECT),
    )(*arrays, send_sems, recv_sems, after)


def _plan_wout(refs):
    src, land = refs
    x, y, c, me = _place()
    return [(src, land.at[me], _peer(x, y, c, r)) for r in range(1, NDEV)]


def _plan_rest(refs):
    gob, gpf, r_out, r_pool = refs
    x, y, c, me = _place()
    plan = []
    for r in range(1, NDEV):
        plan.append((gob.at[me ^ r], r_out.at[r - 1], _peer(x, y, c, r)))
        plan.append((gpf.at[me ^ r], r_pool.at[r - 1], _peer(x, y, c, r)))
    return plan


def _plan_in(refs):
    sums, landing = refs
    x, y, c, _ = _place()
    plan = []
    for rel, (dx, dy) in enumerate(((1, 0), (0, 1), (1, 1))):
        for p in range(3):
            plan.append((sums.at[rel, p], landing.at[rel, p], (x ^ dx, y ^ dy, c)))
    return plan


def _plan_small(refs):
    small, land = refs
    x, y, c, me = _place()
    return [(small, land.at[me], _peer(x, y, c, r)) for r in range(1, NDEV)]


def _seg_tiles(s):
    return (s + 2) % NSEG


_POOL_SPECS = [pl.BlockSpec((None, T, GROUP), lambda g, base=base: (base + g, 0, 0)) for base in (0, 4)]
_HEAD_SPECS = [pl.BlockSpec((None, T, HEAD), lambda h, base=base: (base + h // 2, 0, h % 2))
               for base in (8, 12, 16, 20)]


def _row_ids(shape):
    return lax.broadcasted_iota(jnp.int32, shape, 0)


BAND_ROWS = 128
HALO = 16


def _window_sum(a, gidx, lead):
    width = lax.shift_left(jnp.int32(2), gidx)
    shape = (BAND_ROWS, BAND_ROWS + HALO)
    t, j = lax.broadcasted_iota(jnp.int32, shape, 0), lax.broadcasted_iota(jnp.int32, shape, 1)
    first = t if lead else t + HALO - width + 1
    band = _bf(jnp.where(j >= first, jnp.where(j < first + width, 1.0, 0.0), 0.0))
    zeros = jnp.zeros((HALO, a.shape[1]), jnp.bfloat16)
    padded = [jnp.concatenate([p, zeros] if lead else [zeros, p], axis=0) for p in _split2(a)]
    out = []
    for r0 in range(0, T, BAND_ROWS):
        slab = jnp.concatenate([p[r0:r0 + BAND_ROWS + HALO] for p in padded], axis=1)
        r = _dot(band, slab, 1, 0)
        out.append(r[:, :a.shape[1]] + r[:, a.shape[1]:])
    return jnp.concatenate(out, axis=0)


def _window_mean(s, gidx):
    inv = jnp.where(gidx == 0, 0.5, jnp.where(gidx == 1, 0.25, jnp.where(gidx == 2, 0.125, 0.0625)))
    width = lax.shift_left(jnp.int32(2), gidx)
    head = s[:16] / jnp.minimum(_row_ids((16, s.shape[1])) + 1, width).astype(jnp.float32)
    return jnp.concatenate([head, s[16:] * inv], axis=0)


def _pool_fwd(proj, pool_w, pool_scale, token):
    def body(u_ref, pg_ref, w_ref, sc_ref, token_any, y_ref):
        del token_any
        gidx = pl.program_id(0)
        u, pg = u_ref[...], pg_ref[...]
        d = _window_mean(_window_sum(u, gidx, False), gidx) - u
        mixed = _dot(_bf(d), w_ref[...], 1, 0)
        y_ref[...] = _bf(mixed * sc_ref[...] * (pg * _sigmoid(pg)))

    return pl.pallas_call(
        body, name="pool_fwd", grid=(NGROUP,),
        in_specs=[*_POOL_SPECS,
                  pl.BlockSpec((None, GROUP, GROUP), lambda g: (g, 0, 0)),
                  pl.BlockSpec((1, GROUP), lambda g: (0, g)), ANY],
        out_specs=pl.BlockSpec((T, GROUP), lambda g: (0, g)),
        out_shape=pltpu.HBM((T, DMIX), jnp.bfloat16),
        compiler_params=_params(("parallel",)),
    )(proj, proj, pool_w, pool_scale, token)


def _tri(lower):
    r = lax.broadcasted_iota(jnp.int32, (CHUNK, CHUNK), 0)
    c = lax.broadcasted_iota(jnp.int32, (CHUNK, CHUNK), 1)
    return (r >= c) if lower else (r <= c)


def _sum_rows_matrix():
    shape = (CHUNK + 16, CHUNK)
    r, c = lax.broadcasted_iota(jnp.int32, shape, 0), lax.broadcasted_iota(jnp.int32, shape, 1)
    run = jnp.where(c <= r, 1.0, 0.0)
    half = jnp.where(c < CHUNK // 2, 1.0, 0.0)
    return _bf(jnp.where(r < CHUNK, run, jnp.where(r < CHUNK + 8, 1.0, half)))


def _rev_sum_matrix():
    shape = (CHUNK, 2 * CHUNK)
    r, c = lax.broadcasted_iota(jnp.int32, shape, 0), lax.broadcasted_iota(jnp.int32, shape, 1)
    return _bf(jnp.where(c < CHUNK, jnp.where(c >= r, 1.0, 0.0), jnp.where(c - CHUNK < r, 1.0, 0.0)))


def _split2(a):
    hi = _bf(a)
    return [hi, _bf(a - hi.astype(jnp.float32))]


def _exact_sums(mat, pieces):
    x = jnp.concatenate([s for p in pieces for s in _split2(p)], axis=1)
    r = _dot(mat, x, 1, 0)
    return [r[:, 2 * j * HEAD:(2 * j + 1) * HEAD] + r[:, (2 * j + 1) * HEAD:(2 * j + 2) * HEAD]
            for j in range(len(pieces))]


def _gates(qv, fl, lb):
    sq = _sigmoid(qv)
    sg = _sigmoid(fl)
    f = lb + (1.0 - lb) * sg
    return dict(sq=sq, qs=qv * sq, sg=sg, f=f, kk=1.0 - f, g=jnp.log2(f))


def _decays(sums):
    big_g = sums[:CHUNK]
    total = sums[CHUNK:CHUNK + 8]
    g_last = jnp.tile(total, (CHUNK // 8, 1))
    g_mid = jnp.tile(sums[CHUNK + 8:], (CHUNK // 8, 1))
    return dict(
        e_q=jnp.exp2(big_g),
        e_k=jnp.exp2(g_last - big_g),
        e_qm=jnp.exp2(jnp.minimum(big_g - g_mid, EXP_CAP)),
        e_km=jnp.exp2(jnp.minimum(g_mid - big_g, EXP_CAP)),
        total8=jnp.exp2(total))


def _group_rows(gi):
    return [pl.ds(pl.multiple_of((gi * NB + j) * CHUNK, CHUNK), CHUNK) for j in range(NB)]


def _lower_bound(lb_ref):
    return _sigmoid(lb_ref[0:1, :] - lb_ref[1:2, :])


def _hgrn_fwd(proj, lb_logits, rec_g, y_in):
    def body(q_ref, f_ref, i_ref, gate_ref, lb_ref, rg_ref, y_any, y_ref, o_ref, st_ref):
        del y_any
        lb = _lower_bound(lb_ref)
        causal = _tri(True)
        smat = _sum_rows_matrix()

        def group(gi, st):
            rows = _group_rows(gi)
            ts = [_gates(q_ref[r, :], f_ref[r, :], lb) for r in rows]
            ds = [_decays(s) for s in _exact_sums(smat, [t["g"] for t in ts])]
            vs = [_bf(i_ref[r, :]) for r in rows]
            q_m = [_bf(t["qs"] * d["e_qm"]) for t, d in zip(ts, ds)]
            k_m = [_bf(t["kk"] * d["e_km"]) for t, d in zip(ts, ds)]
            q_e = [_bf(t["qs"] * d["e_q"]) for t, d in zip(ts, ds)]
            k_e = [_bf(t["kk"] * d["e_k"]) for t, d in zip(ts, ds)]
            a = [_bf(jnp.where(causal, _dot(q_m[j], k_m[j], 1, 1), 0.0)) for j in range(NB)]
            intra = [_dot(a[j], vs[j], 1, 0) for j in range(NB)]
            upd = [_dot(vs[j], k_e[j], 0, 0) for j in range(NB)]
            for j in range(NB):
                st_ref[gi * NB + j] = st
                o_ref[rows[j], :] = intra[j] + _dot(q_e[j], _bf(st), 1, 1)
                st = st * jnp.tile(ds[j]["total8"], (HEAD // 8, 1)) + upd[j]
            return st

        lax.fori_loop(0, NGRP, group, jnp.zeros((HEAD, HEAD), jnp.float32))
        o = o_ref[...]
        rn = o * lax.rsqrt(jnp.mean(o * o, axis=-1, keepdims=True) + EPS)
        gate = gate_ref[...]
        y_ref[...] = _bf(rn * rg_ref[...] * (gate * _sigmoid(gate)))

    return pl.pallas_call(
        body, name="hgrn_fwd", grid=(NHEAD,),
        in_specs=[*_HEAD_SPECS,
                  pl.BlockSpec((2, HEAD), lambda h: (0, h)),
                  pl.BlockSpec((1, HEAD), lambda h: (0, h)),
                  pl.BlockSpec(memory_space=pl.ANY)],
        out_specs=(pl.BlockSpec((T, HEAD), lambda h: (0, NHEAD + h)),
                   pl.BlockSpec((T, HEAD), lambda h: (0, h)),
                   pl.BlockSpec((None, NCHUNK, HEAD, HEAD), lambda h: (h, 0, 0, 0))),
        out_shape=(pltpu.HBM((T, DMIX), jnp.bfloat16), pltpu.HBM((T, D), jnp.float32),
                   pltpu.HBM((NHEAD, NCHUNK, HEAD, HEAD), jnp.float32)),
        input_output_aliases={6: 0},
        compiler_params=_params(("parallel",)),
    )(proj, proj, proj, proj, lb_logits, rec_g, y_in)


def _out_proj_loss(x, y, w_out, target, gf):
    rows = 512
    parts = [slice(k * rows // 2, (k + 1) * rows // 2) for k in range(2)]

    def body(x_ref, y_ref, w_ref, t_ref, g_ref, dz_ref, dzb_ref, sq_ref, dg_ref):
        zs = [x_ref[p, :] + _dot(y_ref[p, :], w_ref[...], 1, 0) for p in parts]
        sq = dg = 0.0
        for p, z in zip(parts, zs):
            r = lax.rsqrt(jnp.mean(z * z, axis=-1, keepdims=True) + EPS)
            zhat = z * r
            err = zhat * g_ref[...] - t_ref[p, :]
            dy = err * (1.0 / D)
            gdy = dy * g_ref[...]
            dz = r * (gdy - zhat * jnp.mean(zhat * gdy, axis=-1, keepdims=True))
            dz_ref[p, :] = dz
            dzb_ref[p, :] = _bf(dz)
            sq = sq + jnp.sum(err * err, axis=0, keepdims=True)
            dg = dg + jnp.sum(zhat * dy, axis=0, keepdims=True)

        @pl.when(pl.program_id(0) == 0)
        def _():
            sq_ref[...] = sq
            dg_ref[...] = dg

        @pl.when(pl.program_id(0) != 0)
        def _():
            sq_ref[...] += sq
            dg_ref[...] += dg

    tile = pl.BlockSpec((rows, D), lambda i: (i, 0))
    vec = pl.BlockSpec((1, D), lambda i: (0, 0))
    return pl.pallas_call(
        body, name="out_proj_loss", grid=(T // rows,),
        in_specs=[tile, pl.BlockSpec((rows, DMIX), lambda i: (i, 0)), pl.BlockSpec((DMIX, D), lambda i: (0, 0)),
                  tile, vec],
        out_specs=(tile, tile, vec, vec),
        out_shape=(pltpu.HBM((T, D), jnp.float32), pltpu.HBM((T, D), jnp.bfloat16),
                   jax.ShapeDtypeStruct((1, D), jnp.float32), jax.ShapeDtypeStruct((1, D), jnp.float32)),
        compiler_params=_params(("arbitrary",)),
    )(x, y, w_out, target, gf)


def _out_proj_bwd(dzb, w_out, y):
    tn = 512

    def body(dz_ref, w_ref, y_ref, dy_ref, gw_ref, gwb_ref):
        dz = dz_ref[...]
        dy_ref[...] = _dot(dz, w_ref[...], 1, 1)
        gw = _dot(y_ref[...], dz, 0, 0)
        gw_ref[...] = gw
        gwb_ref[...] = _bf(gw)

    return pl.pallas_call(
        body, name="out_proj_bwd", grid=(DMIX // tn,),
        in_specs=[pl.BlockSpec((T, D), lambda n: (0, 0)), pl.BlockSpec((tn, D), lambda n: (n, 0)),
                  pl.BlockSpec((T, tn), lambda n: (0, n))],
        out_specs=(pl.BlockSpec((T, tn), lambda n: (0, n)), pl.BlockSpec((tn, D), lambda n: (n, 0)),
                   pl.BlockSpec((tn, D), lambda n: (n, 0))),
        out_shape=(pltpu.HBM((T, DMIX), jnp.float32), pltpu.HBM((DMIX, D), jnp.float32),
                   pltpu.HBM((DMIX, D), jnp.bfloat16)),
        compiler_params=_params(("parallel",)),
    )(dzb, w_out, y)


def _hgrn_bwd(proj, lb_logits, rec_g, o, states, dymix, dproj_in, token):
    def body(q_ref, f_ref, i_ref, gate_ref, lb_ref, rg_ref, o_ref, st_ref, dy_ref, dp_any, token_any,
             dp_ref, drg_ref, dlb_ref, do_ref):
        del dp_any, token_any
        lb = _lower_bound(lb_ref)
        causal = _tri(True)
        smat, rmat = _sum_rows_matrix(), _rev_sum_matrix()

        o = o_ref[...]
        rs = lax.rsqrt(jnp.mean(o * o, axis=-1, keepdims=True) + EPS)
        rn = o * rs
        gate = gate_ref[...]
        sgate = _sigmoid(gate)
        dyv = dy_ref[...]
        d_r = dyv * (gate * sgate)
        dp_ref[3] = _bf(dyv * (rn * rg_ref[...]) * (sgate * (1.0 + gate * (1.0 - sgate))))
        drg_ref[...] = jnp.sum(d_r * rn, axis=0, keepdims=True)
        drn = d_r * rg_ref[...]
        do_ref[...] = rs * (drn - rn * jnp.mean(rn * drn, axis=-1, keepdims=True))

        def group(i, carry):
            dst, dlb = carry
            gi = NGRP - 1 - i
            rows = _group_rows(gi)
            span = range(NB)
            qvs = [q_ref[r, :] for r in rows]
            ts = [_gates(qv, f_ref[r, :], lb) for qv, r in zip(qvs, rows)]
            ds = [_decays(s) for s in _exact_sums(smat, [t["g"] for t in ts])]
            vs = [_bf(i_ref[r, :]) for r in rows]
            dos = [_bf(do_ref[r, :]) for r in rows]
            sts = [st_ref[gi * NB + j] for j in span]
            qe_f = [t["qs"] * d["e_q"] for t, d in zip(ts, ds)]
            ke_f = [t["kk"] * d["e_k"] for t, d in zip(ts, ds)]
            q_e, k_e = [_bf(a) for a in qe_f], [_bf(a) for a in ke_f]
            q_m = [_bf(t["qs"] * d["e_qm"]) for t, d in zip(ts, ds)]
            k_m = [_bf(t["kk"] * d["e_km"]) for t, d in zip(ts, ds)]
            a = [_bf(jnp.where(causal, _dot(q_m[j], k_m[j], 1, 1), 0.0)) for j in span]
            da = [_bf(jnp.where(causal, _dot(dos[j], vs[j], 1, 1), 0.0)) for j in span]
            dqm = [_dot(da[j], k_m[j], 1, 0) for j in span]
            dkm = [_dot(da[j], q_m[j], 0, 0) for j in span]
            dv_in = [_dot(a[j], dos[j], 0, 0) for j in span]
            dqe = [_dot(dos[j], _bf(sts[j]), 1, 0) for j in span]
            grow = [_dot(dos[j], q_e[j], 0, 0) for j in span]
            dke, carried = [None] * NB, [None] * NB
            for j in reversed(span):
                dst_b = _bf(dst)
                dke[j] = _dot(vs[j], dst_b, 1, 0)
                dp_ref[2, rows[j], :] = _bf(dv_in[j] + _dot(k_e[j], dst_b, 1, 1))
                carried[j] = ds[j]["total8"] * jnp.sum(dst * sts[j], axis=0, keepdims=True)
                dst = dst * jnp.tile(ds[j]["total8"], (HEAD // 8, 1)) + grow[j]
            kdk = [ke_f[j] * dke[j] for j in span]
            pos = [(q_m[j].astype(jnp.float32) * dqm[j] - k_m[j].astype(jnp.float32) * dkm[j]) + qe_f[j] * dqe[j]
                   for j in span]
            dgs = _exact_sums(rmat, [jnp.concatenate([pos[j], kdk[j]], axis=0) for j in span])
            for j in span:
                t, d = ts[j], ds[j]
                dg = dgs[j] + jnp.tile(carried[j], (CHUNK // 8, 1))
                dqs = dqm[j] * d["e_qm"] + dqe[j] * d["e_q"]
                dkk = dkm[j] * d["e_km"] + dke[j] * d["e_k"]
                df = dg / t["f"] - dkk
                dp_ref[1, rows[j], :] = _bf(df * (1.0 - lb) * (t["sg"] * (1.0 - t["sg"])))
                dp_ref[0, rows[j], :] = _bf(dqs * (t["sq"] * (1.0 + qvs[j] * (1.0 - t["sq"]))))
                dlb = dlb + df * (1.0 - t["sg"])
            return dst, dlb

        _, dlb = lax.fori_loop(0, NGRP, group, (jnp.zeros((HEAD, HEAD), jnp.float32),
                                                jnp.zeros((CHUNK, HEAD), jnp.float32)))
        dlb_ref[...] = jnp.sum(dlb, axis=0, keepdims=True)

    vec = pl.BlockSpec((1, HEAD), lambda h: (0, h))
    return pl.pallas_call(
        body, name="hgrn_bwd", grid=(NHEAD,),
        in_specs=[*_HEAD_SPECS,
                  pl.BlockSpec((2, HEAD), lambda h: (0, h)), vec,
                  pl.BlockSpec((T, HEAD), lambda h: (0, h)),
                  pl.BlockSpec((None, NCHUNK, HEAD, HEAD), lambda h: (h, 0, 0, 0)),
                  pl.BlockSpec((T, HEAD), lambda h: (0, NHEAD + h)), ANY, ANY],
        out_specs=(pl.BlockSpec((4, T, HEAD), lambda h: (0, 0, h)), vec, vec),
        out_shape=(pltpu.HBM((NSEG, T, D), jnp.bfloat16),
                   jax.ShapeDtypeStruct((1, D), jnp.float32), jax.ShapeDtypeStruct((1, D), jnp.float32)),
        scratch_shapes=[pltpu.VMEM((T, HEAD), jnp.float32)],
        input_output_aliases={9: 0},
        compiler_params=_params(("parallel",)),
    )(proj, proj, proj, proj, lb_logits, rec_g, o, states, dymix, dproj_in, token)


def _pool_bwd(proj, pool_w, pool_scale, dymix):
    def body(u_ref, pg_ref, w_ref, sc_ref, dy_ref, dp_ref, gw_ref, gs_ref):
        gidx = pl.program_id(0)
        u, pg = u_ref[...], pg_ref[...]
        d = _bf(_window_mean(_window_sum(u, gidx, False), gidx) - u)
        mixed = _dot(d, w_ref[...], 1, 0)
        spg = _sigmoid(pg)
        dyv = dy_ref[...]
        d_p = dyv * (pg * spg)
        dp_ref[1] = _bf(dyv * (mixed * sc_ref[...]) * (spg * (1.0 + pg * (1.0 - spg))))
        gs_ref[...] = jnp.sum(d_p * mixed, axis=0, keepdims=True)
        dmixed = _bf(d_p * sc_ref[...])
        gw_ref[...] = _dot(d, dmixed, 0, 0)
        dd = _dot(dmixed, w_ref[...], 1, 1)
        dp_ref[0] = _bf(_window_sum(_window_mean(dd, gidx), gidx, True) - dd)

    return pl.pallas_call(
        body, name="pool_bwd", grid=(NGROUP,),
        in_specs=[*_POOL_SPECS,
                  pl.BlockSpec((None, GROUP, GROUP), lambda g: (g, 0, 0)),
                  pl.BlockSpec((1, GROUP), lambda g: (0, g)),
                  pl.BlockSpec((T, GROUP), lambda g: (0, g))],
        out_specs=(pl.BlockSpec((2, T, GROUP), lambda g: (2, 0, g)),
                   pl.BlockSpec((None, GROUP, GROUP), lambda g: (g, 0, 0)),
                   pl.BlockSpec((1, GROUP), lambda g: (0, g))),
        out_shape=(pltpu.HBM((NSEG, T, D), jnp.bfloat16),
                   jax.ShapeDtypeStruct((NGROUP, GROUP, GROUP), jnp.float32),
                   jax.ShapeDtypeStruct((1, D), jnp.float32)),
        compiler_params=_params(("parallel",)),
    )(proj, proj, pool_w, pool_scale, dymix)


HALF = NTILE // 2
AWAY = HALF - 3


def _dproj_tile(chip, side, p):
    j = 6 * chip + 3 * side + p
    return ((j // 4 + 4) % NSEG, 0, j % 4)


def _sibling_copy(sib_out, sib_in, send_sems, recv_sems, slot):
    x, y, c, _ = _place()
    return pltpu.make_async_remote_copy(
        src_ref=sib_out.at[slot], dst_ref=sib_in.at[slot], send_sem=send_sems.at[slot],
        recv_sem=recv_sems.at[slot], device_id=(x, y, 1 - c), device_id_type=MESH)


def _proj_bwd_w_far(place, ht, dproj):
    def body(place_ref, h_ref, dp_ref, sib_out, sib_in, send_sems, recv_sems, stage, loc_sems):
        del place_ref
        i = pl.program_id(0)

        def to_hbm(k):
            return pltpu.make_async_copy(stage.at[k], sib_out.at[k], loc_sems.at[k])

        def send(k):
            to_hbm(k).wait()
            _sibling_copy(sib_out, sib_in, send_sems, recv_sems, k).start()

        stage[i] = _bf(_dot(h_ref[...], dp_ref[...], 1, 0))

        @pl.when(i > 0)
        def _():
            send(i - 1)

        to_hbm(i).start()

        @pl.when(i == HALF - 1)
        def _():
            send(i)

    buf = pltpu.HBM((HALF, D, TILE), jnp.bfloat16)
    sems = pltpu.SemaphoreType.DMA((HALF,))
    return pl.pallas_call(
        body, name="proj_bwd_w_far",
        grid_spec=pltpu.PrefetchScalarGridSpec(
            num_scalar_prefetch=1, grid=(HALF,),
            in_specs=[pl.BlockSpec((D, T), lambda i, pr: (0, 0)),
                      pl.BlockSpec((None, T, TILE), lambda i, pr: _dproj_tile(i // 3, 1 - pr[2], i % 3))],
            out_specs=(HBM, HBM, SEM, SEM),
            scratch_shapes=[pltpu.VMEM((HALF, D, TILE), jnp.bfloat16), pltpu.SemaphoreType.DMA((HALF,))]),
        out_shape=(buf, buf, sems, sems),
        compiler_params=pltpu.CompilerParams(dimension_semantics=("arbitrary",), vmem_limit_bytes=48 * MIB,
                                             has_side_effects=EFFECT),
    )(place, ht, dproj)


def _proj_bwd_w_near(place, ht, dproj, sib_out, sib_in, sib_send, sib_recv):
    def owner_chip(k, pr):
        return jnp.where(k < AWAY, (pr[1] + 1 + k % 3) % 4, pr[1])

    def tile_p(k):
        return jnp.where(k < AWAY, k // 3, k - AWAY)

    def body(place_ref, h_ref, dp_ref, sib_out, sib_in, sib_send, sib_recv, sums, own_ref, landing, out_send,
             out_recv, recvbuf, outbuf, in_sems, loc_sems):
        i = pl.program_id(0)
        px, py, c, _ = _place()

        def slot_of(k):
            return 3 * owner_chip(k, place_ref) + tile_p(k)

        def load(k):
            return pltpu.make_async_copy(sib_in.at[slot_of(k)], recvbuf.at[k % 2], in_sems.at[k % 2])

        def fetch(k):
            _sibling_copy(sib_out, sib_in, sib_send, sib_recv, slot_of(k)).wait_recv()
            load(k).start()

        def route(k):
            chip = owner_chip(k, place_ref)
            cx, cy = chip // 2, chip % 2
            return cx, cy, (cx ^ px) + 2 * (cy ^ py) - 1, tile_p(k)

        def to_hbm(k):
            _, _, rel, p = route(k)
            return pltpu.make_async_copy(outbuf.at[k], sums.at[rel, p], loc_sems.at[k])

        def to_owner(k):
            cx, cy, rel, p = route(k)
            return pltpu.make_async_remote_copy(
                src_ref=sums.at[rel, p], dst_ref=landing.at[rel, p], send_sem=out_send.at[3 * rel + p],
                recv_sem=out_recv.at[3 * rel + p], device_id=(cx, cy, c), device_id_type=MESH)

        @pl.when(i == 0)
        def _():
            fetch(i)

        @pl.when(i < HALF - 1)
        def _():
            fetch(i + 1)

        gw = _dot(h_ref[...], dp_ref[...], 1, 0)

        @pl.when(jnp.logical_and(i > 0, i <= AWAY))
        def _():
            to_hbm(i - 1).wait()
            to_owner(i - 1).start()

        load(i).wait()
        total = gw + recvbuf[i % 2].astype(jnp.float32)

        @pl.when(i < AWAY)
        def _():
            outbuf[i] = _bf(total)
            to_hbm(i).start()

        @pl.when(i >= AWAY)
        def _():
            own_ref[...] = total

        @pl.when(i == HALF - 1)
        def _():
            for slot in range(HALF):
                _sibling_copy(sib_out, sib_in, sib_send, sib_recv, slot).wait_send()

    travelling = pltpu.HBM((3, 3, D, TILE), jnp.bfloat16)
    sems = pltpu.SemaphoreType.DMA((AWAY,))
    return pl.pallas_call(
        body, name="proj_bwd_w_near",
        grid_spec=pltpu.PrefetchScalarGridSpec(
            num_scalar_prefetch=1, grid=(HALF,),
            in_specs=[pl.BlockSpec((D, T), lambda i, pr: (0, 0)),
                      pl.BlockSpec((None, T, TILE), lambda i, pr: _dproj_tile(owner_chip(i, pr), pr[2], tile_p(i))),
                      HBM, HBM, SEM, SEM],
            out_specs=(HBM, pl.BlockSpec((None, D, TILE), lambda i, pr: (jnp.where(i < AWAY, 0, i % 3), 0, 0)),
                       HBM, SEM, SEM),
            scratch_shapes=[pltpu.VMEM((2, D, TILE), jnp.bfloat16), pltpu.VMEM((AWAY, D, TILE), jnp.bfloat16),
                            pltpu.SemaphoreType.DMA((2,)), pltpu.SemaphoreType.DMA((AWAY,))]),
        out_shape=(travelling, pltpu.HBM((3, D, TILE), jnp.float32), travelling, sems, sems),
        compiler_params=pltpu.CompilerParams(dimension_semantics=("arbitrary",), vmem_limit_bytes=48 * MIB,
                                             has_side_effects=EFFECT),
    )(place, ht, dproj, sib_out, sib_in, sib_send, sib_recv)


def _proj_bwd_x(dproj, w_t, x, g1, dz, token):
    tm = 512
    pairs = NSEG // 2

    def body(dp_ref, w_ref, x_ref, g_ref, dz_ref, token_any, dx_ref, dg_ref, wcat, acc):
        del token_any
        m, s = pl.program_id(0), pl.program_id(1)

        @pl.when(m == 0)
        def _():
            for i in range(8):
                wcat[s, :, i * TILE:(i + 1) * TILE] = w_ref[i]

        @pl.when(s == 0)
        def _():
            acc[...] = jnp.zeros((tm, D), jnp.float32)

        acc[...] += _dot(jnp.concatenate([dp_ref[0], dp_ref[1]], axis=1), wcat[s], 1, 1)

        @pl.when(s == pairs - 1)
        def _():
            xv = x_ref[...]
            rs = lax.rsqrt(jnp.mean(xv * xv, axis=-1, keepdims=True) + EPS)
            xhat = xv * rs
            dhv = acc[...]
            gdh = dhv * g_ref[...]
            dx_ref[...] = dz_ref[...] + rs * (gdh - xhat * jnp.mean(xhat * gdh, axis=-1, keepdims=True))
            dg = jnp.sum(xhat * dhv, axis=0, keepdims=True)

            @pl.when(m == 0)
            def _():
                dg_ref[...] = dg

            @pl.when(m != 0)
            def _():
                dg_ref[...] += dg

    rows = pl.BlockSpec((tm, D), lambda m, s: (m, 0))
    vec = pl.BlockSpec((1, D), lambda m, s: (0, 0))
    return pl.pallas_call(
        body, name="proj_bwd_x", grid=(T // tm, pairs),
        in_specs=[pl.BlockSpec((2, tm, D), lambda m, s: (s, m, 0)),
                  pl.BlockSpec((8, D, TILE), lambda m, s: ((jnp.where(m == 0, s, pairs - 1) + 1) % pairs, 0, 0)),
                  rows, vec, rows, ANY],
        out_specs=(rows, vec),
        out_shape=(jax.ShapeDtypeStruct((T, D), jnp.float32), jax.ShapeDtypeStruct((1, D), jnp.float32)),
        scratch_shapes=[pltpu.VMEM((pairs, D, 2 * D), jnp.bfloat16), pltpu.VMEM((tm, D), jnp.float32)],
        compiler_params=_params(("arbitrary", "arbitrary"), vmem_mib=56),
    )(dproj, w_t, x, g1, dz, token)


def _adamw(w, g, m, v):
    m_new = ADAM_B1 * m + (1.0 - ADAM_B1) * g
    v_new = ADAM_B2 * v + (1.0 - ADAM_B2) * (g * g)
    delta = -ADAM_LR * ((m_new / BC1) / (jnp.sqrt(v_new / BC2) + ADAM_EPS) + ADAM_WD * w)
    return delta, m_new, v_new


def _reduce_adam(name, place, parts, w, m, v, grid, w_spec):
    n = len(parts)

    def body(place_ref, *refs):
        del place_ref
        w_ref, m_ref, v_ref, g_ref, d_ref, mo_ref, vo_ref = refs[n:]
        g = None
        for ref, (_, _, stacked) in zip(refs[:n], parts):
            terms = [ref[r] for r in range(ref.shape[0])] if stacked else [ref[...]]
            for t in terms:
                if t.shape[-1] != w_ref.shape[-1]:
                    t = jnp.concatenate([t[p] for p in range(t.shape[0])], axis=1)
                g = t.astype(jnp.float32) if g is None else g + t.astype(jnp.float32)
        delta, m_new, v_new = _adamw(w_ref[...], g, m_ref[...], v_ref[...])
        g_ref[...] = g
        d_ref[...] = delta
        mo_ref[...] = m_new
        vo_ref[...] = v_new

    shape = jax.ShapeDtypeStruct(w.shape, jnp.float32)
    return pl.pallas_call(
        body, name=name,
        grid_spec=pltpu.PrefetchScalarGridSpec(
            num_scalar_prefetch=1, grid=grid,
            in_specs=[spec for _, spec, _ in parts] + [w_spec] * 3, out_specs=(w_spec,) * 4),
        out_shape=(shape,) * 4,
        compiler_params=_params(("parallel",)),
    )(place, *[_in_hbm(a) for a in [a for a, _, _ in parts] + [w, m, v]])


def _small_adam(place, own, parts, w, m, v):
    def body(place_ref, own_ref, p_ref, w_ref, m_ref, v_ref, g_ref, d_ref, mo_ref, vo_ref):
        me = place_ref[0]
        g = None
        for s in range(NDEV):
            term = jnp.where(me == s, own_ref[...], p_ref[s])
            g = term if g is None else g + term
        wv = w_ref[...]
        rows = _row_ids(wv.shape)
        other = jnp.where(rows == 2, pltpu.roll(wv, 7, 0), jnp.where(rows == 3, pltpu.roll(wv, 1, 0), 0.0))
        lbv = _sigmoid(wv - other)
        sign = jnp.where(rows == 2, 1.0, -1.0)
        g = jnp.where((rows == 2) | (rows == 3), sign * g * lbv * (1.0 - lbv), g)
        delta, m_new, v_new = _adamw(wv, g, m_ref[...], v_ref[...])
        g_ref[...] = g
        d_ref[...] = delta
        mo_ref[...] = m_new
        vo_ref[...] = v_new

    shape = jax.ShapeDtypeStruct((8, D), jnp.float32)
    vmem = pl.BlockSpec(memory_space=pltpu.VMEM)
    return pl.pallas_call(
        body, name="small_adam", out_shape=(shape,) * 4,
        in_specs=[pl.BlockSpec(memory_space=pltpu.SMEM)] + [vmem] * 5, out_specs=(vmem,) * 4,
    )(place, own, parts, w, m, v)


def _rows8(*vecs):
    rows = [a.reshape(-1, D) for a in vecs]
    n = sum(r.shape[0] for r in rows)
    return jnp.concatenate(rows + [jnp.zeros((8 - n, D), jnp.float32)], axis=0)


def kernel(x, norm1_g, w_in, pool_w, pool_scale, lb_logits, rec_norm_g, w_out, final_norm_g, loss_target, m_norm1_g, m_w_in, m_pool_w, m_pool_scale, m_lb_logits, m_rec_norm_g, m_w_out, m_final_norm_g, v_norm1_g, v_w_in, v_pool_w, v_pool_scale, v_lb_logits, v_rec_norm_g, v_w_out, v_final_norm_g):
    xs = x[0]
    target = loss_target[0]
    ix, iy, ic = lax.axis_index("x"), lax.axis_index("y"), lax.axis_index("c")
    place = jnp.stack([4 * ix + 2 * iy + ic, 2 * ix + iy, ic]).astype(jnp.int32)
    gf = final_norm_g.reshape(1, D)

    ht, w_t, w_out_b, w_out_g, pool_g, proj = _gather_proj(xs, norm1_g, w_in, w_out, pool_w)
    pool_full = pool_g.transpose(1, 0, 2, 3).reshape(NGROUP, GROUP, GROUP)
    wout = [w_out_b, w_out_g]
    wout_send, wout_recv, wout, wout_token = _split_start("gather_wout_start", wout, NDEV - 1, _plan_wout)

    y = _pool_fwd(proj, pool_full, pool_scale, wout_token)
    y, o, states = _hgrn_fwd(proj, lb_logits, rec_norm_g, y)
    _, w_out_g = _split_wait("gather_wout_wait", wout, wout_send, wout_recv, _plan_wout, o)
    w_out_full = _in_hbm(w_out_g.reshape(DMIX, D))
    dz, dzb, sq, dgf = _out_proj_loss(xs, y, w_out_full, target, gf)

    dymix, gwout_f, gwout_b = _out_proj_bwd(dzb, w_out_full, y)
    dproj, gpool, dscale = _pool_bwd(proj, pool_full, pool_scale, dymix)

    blk_out = (NDEV, DMIX // NDEV, D)
    blk_pool = (NDEV, NGROUP, GROUP // NDEV, GROUP)
    gpool_s = gpool.reshape(NGROUP, NDEV, GROUP // NDEV, GROUP).transpose(1, 0, 2, 3)
    rest = [gwout_b.reshape(blk_out), gpool_s,
            lax.empty((NDEV - 1,) + blk_out[1:], jnp.bfloat16), lax.empty((NDEV - 1,) + blk_pool[1:], jnp.float32)]
    rest_send, rest_recv, rest, rest_token = _split_start("scatter_rest_start", rest, 2 * (NDEV - 1), _plan_rest)

    dproj, drecg, dlb = _hgrn_bwd(proj, lb_logits, rec_norm_g, o, states, dymix, dproj, rest_token)
    chip_sums, own_sum, landing, win_send, win_recv = _proj_bwd_w_near(
        place, ht, dproj, *_proj_bwd_w_far(place, ht, dproj))
    win = [chip_sums, landing]

    grad_x, dg1 = _proj_bwd_x(dproj, w_t, xs, norm1_g, dz, chip_sums)

    small = [_rows8(dg1, dscale, dlb, dlb, drecg, dgf, sq), lax.empty((NDEV, 8, D), jnp.float32)]
    small_send, small_recv, small, small_token = _split_start("gather_small_start", small, NDEV - 1, _plan_small)

    _, gpool_own, r_out, r_pool = _split_wait("scatter_rest_wait", rest, rest_send, rest_recv, _plan_rest,
                                              small_token)
    g_wout, d_wout, m_wout, v_wout = _reduce_adam(
        "adam_w_out", place,
        [(gwout_f.reshape(blk_out), pl.BlockSpec((None,) + blk_out[1:], lambda i, pr: (pr[0], 0, 0)), False),
         (r_out, pl.BlockSpec((NDEV - 1,) + blk_out[1:], lambda i, pr: (0, 0, 0)), True)],
        w_out, m_w_out, v_w_out, (1,), pl.BlockSpec((None,) + blk_out[1:], lambda i, pr: (0, 0, 0)))
    g_pool, d_pool, m_pool, v_pool = _reduce_adam(
        "adam_pool_w", place,
        [(gpool_own, pl.BlockSpec((None,) + blk_pool[1:], lambda i, pr: (pr[0], 0, 0, 0)), False),
         (r_pool, pl.BlockSpec((NDEV - 1,) + blk_pool[1:], lambda i, pr: (0, 0, 0, 0)), True)],
        pool_w, m_pool_w, v_pool_w, (1,), pl.BlockSpec((None,) + blk_pool[1:], lambda i, pr: (0, 0, 0, 0)))

    _, r_in = _split_wait("scatter_win_wait", win, win_send, win_recv, _plan_in, d_pool)
    g_win, d_win, m_win, v_win = _reduce_adam(
        "adam_w_in", place,
        [(own_sum, pl.BlockSpec((3, D // 8, TILE), lambda i, pr: (0, i, 0)), False),
         (r_in, pl.BlockSpec((3, 3, D // 8, TILE), lambda i, pr: (0, 0, i, 0)), True)],
        w_in, m_w_in, v_w_in, (8,), pl.BlockSpec((None, D // 8, 3 * TILE), lambda i, pr: (0, i, 0)))

    own_small, r_small = _split_wait("gather_small_wait", small, small_send, small_recv, _plan_small, d_win)
    g_s, d_s, m_s, v_s = _small_adam(
        place, own_small, r_small,
        _rows8(norm1_g, pool_scale, lb_logits, rec_norm_g, final_norm_g),
        _rows8(m_norm1_g, m_pool_scale, m_lb_logits, m_rec_norm_g, m_final_norm_g),
        _rows8(v_norm1_g, v_pool_scale, v_lb_logits, v_rec_norm_g, v_final_norm_g))
    loss = jnp.sum(g_s[6]) * (0.5 / D)

    def small_outs(a):
        return a[0:1], a[1:2], a[2:4], a[4:5], a[5]

    def outs(small_a, win, pool, wout):
        n1, ps, lbl, rg, fg = small_outs(small_a)
        return n1, win, pool, ps, lbl, rg, wout, fg

    return (loss, grad_x[None],
            *outs(g_s, g_win, g_pool, g_wout), *outs(d_s, d_win, d_pool, d_wout),
            *outs(m_s, m_win, m_pool, m_wout), *outs(v_s, v_win, v_pool, v_wout))
```

```python
import functools

import jax
import jax.numpy as jnp
from jax import lax
from jax.experimental import pallas as pl
from jax.experimental.pallas import tpu as pltpu

T = 2048
D = 1024
NSEG = 6
NTILE = 24
TILE = 256
DMIX = 2048
NDEV = 8
HEAD = 128
NHEAD = 8
CHUNK = 64
NCHUNK = T // CHUNK
NB = 32
NGRP = NCHUNK // NB
NGROUP = 4
GROUP = 256
EPS = 1e-6
EXP_CAP = 115.0
MESH = pl.DeviceIdType.MESH
AXES = ("x", "y", "c")
ANY = pl.BlockSpec(memory_space=pl.ANY)
HBM = pl.BlockSpec(memory_space=pltpu.HBM)
SEM = pl.BlockSpec(memory_space=pltpu.SEMAPHORE)
EFFECT = pltpu.SideEffectType.DATAFLOW_SIDE_EFFECTING

ADAM_LR = 0.001
ADAM_B1 = 0.9
ADAM_B2 = 0.999
ADAM_EPS = 1e-08
ADAM_WD = 0.01
ADAM_STEP = 10
BC1 = 1.0 - ADAM_B1 ** ADAM_STEP
BC2 = 1.0 - ADAM_B2 ** ADAM_STEP

MIB = 1 << 20


def _params(sem=None, vmem_mib=48):
    return pltpu.CompilerParams(dimension_semantics=sem, vmem_limit_bytes=vmem_mib * MIB)


def _sigmoid(v):
    return 1.0 / (1.0 + jnp.exp(-v))


def _dot(a, b, ca, cb, precision=None):
    return lax.dot_general(a, b, (((ca,), (cb,)), ((), ())), precision=precision,
                           preferred_element_type=jnp.float32)


def _bf(v):
    return v.astype(jnp.bfloat16)


def _in_hbm(a):
    return pltpu.with_memory_space_constraint(a, pltpu.HBM)


def _place():
    x, y, c = lax.axis_index("x"), lax.axis_index("y"), lax.axis_index("c")
    return x, y, c, 4 * x + 2 * y + c


def _peer(x, y, c, r):
    return (x ^ ((r >> 2) & 1), y ^ ((r >> 1) & 1), c ^ (r & 1))


def _gather_proj(x, g1, w_in, w_out, pool_w):
    def body(x_ref, g_ref, win_ref, wout_ref, pool_ref, ht_o, wt_o, woutb_o, wout_o, pool_o, proj_o,
             xbuf, hv, htv, wv, wob, pb, stage, send_sems, recv_sems, loc_sems, out_sems):
        px, py, c, my_idx = _place()
        fetch_x = pltpu.make_async_copy(x_ref, xbuf, loc_sems.at[5])
        fetch_x.start()
        me, sibling = (px, py, c), (px, py, 1 - c)
        chips = [(1 - px, py), (px, 1 - py), (1 - px, 1 - py)]
        for p in range(3):
            wv[3 * my_idx + p] = _bf(win_ref[0, :, p * TILE:(p + 1) * TILE])

        def index(bx, by, bc):
            return 4 * bx + 2 * by + bc

        def slot(w, block):
            return wv.at[pl.ds(3 * index(*block), 3)] if w == 0 else pool_o.at[index(*block)]

        def copy(k, w, block, to, src=None):
            return pltpu.make_async_remote_copy(
                src_ref=slot(w, block) if src is None else src, dst_ref=slot(w, block),
                send_sem=send_sems.at[2 * k + w], recv_sem=recv_sems.at[2 * k + w],
                device_id=to, device_id_type=MESH)

        def save(block):
            at = pl.ds(3 * index(*block), 3)
            pltpu.make_async_copy(wv.at[at], wt_o.at[at], loc_sems.at[4]).start()

        srcs = (slot(0, me), pb)
        first = []
        for w in (0, 1):
            if w == 1:
                pb[...] = _bf(pool_ref[0])
                wob[...] = _bf(wout_ref[0])
            group = [copy(1 + j, w, me, (*chip, c), src=srcs[w]) for j, chip in enumerate(chips[:2])]
            group.append(copy(0, w, me, sibling, src=srcs[w]))
            for cp in group:
                cp.start()
            first += group
        save(me)
        locs = [pltpu.make_async_copy(pb, slot(1, me), loc_sems.at[0]),
                pltpu.make_async_copy(wob, wout_o.at[my_idx], loc_sems.at[1]),
                pltpu.make_async_copy(wob, woutb_o, loc_sems.at[2])]
        for cp in locs:
            cp.start()

        fetch_x.wait()
        xv = xbuf[...]
        hv[...] = _bf(xv * lax.rsqrt(jnp.mean(xv * xv, axis=-1, keepdims=True) + EPS) * g_ref[...])
        rows = 256
        for r0 in range(0, T, rows):
            htv[:, r0:r0 + rows] = hv[r0:r0 + rows, :].T
        locs.append(pltpu.make_async_copy(htv, ht_o, loc_sems.at[3]))
        locs[-1].start()

        def out_copy(p, j):
            return pltpu.make_async_copy(stage.at[p], proj_o.at[j], out_sems.at[p])

        def project(nth, block):
            base = 3 * index(*block)

            def tile(p, carry):
                if nth > 0:
                    out_copy(p, base + p).wait()
                stage[p] = _dot(hv[...], wv[base + p], 1, 0)
                out_copy(p, base + p).start()
                return carry

            lax.fori_loop(0, 3, tile, 0)

        project(0, me)
        copy(0, 0, sibling, me).wait_recv()
        save(sibling)
        project(1, sibling)
        passed = []
        relay_from = (px ^ (1 - c), py ^ c, c)
        relay_to = (px ^ c, py ^ (1 - c), c)

        def arrived(w, j):
            copy(1 + j, w, (*chips[j], c), me).wait_recv()
            passed.append(copy(4 + j, w, (*chips[j], c), sibling))
            passed[-1].start()

        def relay(w):
            passed.append(copy(3, w, relay_from, relay_to))
            passed[-1].start()

        def handed(nth, j):
            copy(4 + j, 0, (*chips[j], 1 - c), me).wait_recv()
            save((*chips[j], 1 - c))
            project(nth, (*chips[j], 1 - c))

        arrived(0, 0)
        arrived(0, 1)
        relay(0)
        for j in range(2):
            save((*chips[j], c))
            project(2 + j, (*chips[j], c))
        handed(4, 0)
        handed(5, 1)
        arrived(1, 0)
        arrived(1, 1)
        relay(1)
        arrived(0, 2)
        save((*chips[2], c))
        project(6, (*chips[2], c))
        handed(7, 2)
        arrived(1, 2)
        copy(0, 1, sibling, me).wait_recv()
        for j, chip in enumerate(chips):
            copy(4 + j, 1, (*chip, 1 - c), me).wait_recv()
        keep = pltpu.make_async_copy(wv, wt_o, loc_sems.at[4])
        for p in range(3):
            out_copy(p, p).wait()
        for cp in first + passed:
            cp.wait_send()
        keep.wait()
        for cp in locs:
            cp.wait()

    vmem = pl.BlockSpec(memory_space=pltpu.VMEM)
    bf16 = jnp.bfloat16
    return pl.pallas_call(
        body, name="gather_proj",
        out_shape=(pltpu.HBM((D, T), bf16), pltpu.HBM((NTILE, D, TILE), bf16),
                   pltpu.HBM((DMIX // NDEV, D), bf16), pltpu.HBM((NDEV, DMIX // NDEV, D), bf16),
                   pltpu.HBM((NDEV, NGROUP, GROUP // NDEV, GROUP), bf16), pltpu.HBM((NTILE, T, TILE), jnp.float32)),
        in_specs=[ANY] + [vmem] * 4, out_specs=(ANY,) * 6,
        scratch_shapes=[pltpu.VMEM((T, D), jnp.float32),
                        pltpu.VMEM((T, D), bf16), pltpu.VMEM((D, T), bf16), pltpu.VMEM((NTILE, D, TILE), bf16),
                        pltpu.VMEM((DMIX // NDEV, D), bf16), pltpu.VMEM((NGROUP, GROUP // NDEV, GROUP), bf16),
                        pltpu.VMEM((3, T, TILE), jnp.float32),
                        pltpu.SemaphoreType.DMA((14,)), pltpu.SemaphoreType.DMA((14,)),
                        pltpu.SemaphoreType.DMA((6,)), pltpu.SemaphoreType.DMA((3,))],
        compiler_params=_params(vmem_mib=56),
    )(x, g1, w_in, w_out, pool_w)


def _split_start(name, arrays, n_copies, plan):
    k = len(arrays)

    def body(*refs):
        send_sems, recv_sems, token = refs[k], refs[k + 1], refs[-1]
        for i, (src, dst, to) in enumerate(plan(refs[:k])):
            pltpu.make_async_remote_copy(src_ref=src, dst_ref=dst, send_sem=send_sems.at[i],
                                         recv_sem=recv_sems.at[i], device_id=to, device_id_type=MESH).start()
        token[...] = jnp.zeros_like(token)

    out = pl.pallas_call(
        body, name=name,
        out_shape=(pltpu.SemaphoreType.DMA((n_copies,)), pltpu.SemaphoreType.DMA((n_copies,)),
                   *[pltpu.HBM(a.shape, a.dtype) for a in arrays], jax.ShapeDtypeStruct((8, 128), jnp.float32)),
        in_specs=[HBM] * k, out_specs=(SEM, SEM, *[HBM] * k, pl.BlockSpec(memory_space=pltpu.VMEM)),
        input_output_aliases={i: 2 + i for i in range(k)},
        compiler_params=pltpu.CompilerParams(has_side_effects=EFFECT),
    )(*[pltpu.with_memory_space_constraint(a, pltpu.HBM) for a in arrays])
    return out[0], out[1], out[2:2 + k], out[-1]


def _split_wait(name, arrays, send_sems, recv_sems, plan, after):
    k = len(arrays)

    def body(*refs):
        sends, recvs = refs[k], refs[k + 1]
        for i, (src, dst, to) in enumerate(plan(refs[:k])):
            cp = pltpu.make_async_remote_copy(src_ref=src, dst_ref=dst, send_sem=sends.at[i], recv_sem=recvs.at[i],
                                              device_id=to, device_id_type=MESH)
            cp.wait_send()
            cp.wait_recv()

    return pl.pallas_call(
        body, name=name,
        out_shape=tuple(pltpu.HBM(a.shape, a.dtype) for a in arrays),
        in_specs=[HBM] * k + [SEM, SEM, ANY], out_specs=(HBM,) * k,
        input_output_aliases={i: i for i in range(k)},
        compiler_params=pltpu.CompilerParams(has_side_effects=EFFECT),
    )(*arrays, send_sems, recv_sems, after)


def _plan_wout(refs):
    src, land = refs
    x, y, c, me = _place()
    return [(src, land.at[me], _peer(x, y, c, r)) for r in range(1, NDEV)]


def _plan_rest(refs):
    gob, gpf, r_out, r_pool = refs
    x, y, c, me = _place()
    plan = []
    for r in range(1, NDEV):
        plan.append((gob.at[me ^ r], r_out.at[r - 1], _peer(x, y, c, r)))
        plan.append((gpf.at[me ^ r], r_pool.at[r - 1], _peer(x, y, c, r)))
    return plan


def _plan_in(refs):
    sums, landing = refs
    x, y, c, _ = _place()
    plan = []
    for rel, (dx, dy) in enumerate(((1, 0), (0, 1), (1, 1))):
        for p in range(3):
            plan.append((sums.at[rel, p], landing.at[rel, p], (x ^ dx, y ^ dy, c)))
    return plan


def _plan_small(refs):
    small, land = refs
    x, y, c, me = _place()
    return [(small, land.at[me], _peer(x, y, c, r)) for r in range(1, NDEV)]


def _seg_tiles(s):
    return (s + 2) % NSEG


_POOL_SPECS = [pl.BlockSpec((None, T, GROUP), lambda g, base=base: (base + g, 0, 0)) for base in (0, 4)]
_HEAD_SPECS = [pl.BlockSpec((None, T, HEAD), lambda h, base=base: (base + h // 2, 0, h % 2))
               for base in (8, 12, 16, 20)]


def _row_ids(shape):
    return lax.broadcasted_iota(jnp.int32, shape, 0)


BAND_ROWS = 128
HALO = 16


def _window_sum(a, gidx, lead):
    width = lax.shift_left(jnp.int32(2), gidx)
    shape = (BAND_ROWS, BAND_ROWS + HALO)
    t, j = lax.broadcasted_iota(jnp.int32, shape, 0), lax.broadcasted_iota(jnp.int32, shape, 1)
    first = t if lead else t + HALO - width + 1
    band = _bf(jnp.where(j >= first, jnp.where(j < first + width, 1.0, 0.0), 0.0))
    zeros = jnp.zeros((HALO, a.shape[1]), jnp.bfloat16)
    padded = [jnp.concatenate([p, zeros] if lead else [zeros, p], axis=0) for p in _split2(a)]
    out = []
    for r0 in range(0, T, BAND_ROWS):
        slab = jnp.concatenate([p[r0:r0 + BAND_ROWS + HALO] for p in padded], axis=1)
        r = _dot(band, slab, 1, 0)
        out.append(r[:, :a.shape[1]] + r[:, a.shape[1]:])
    return jnp.concatenate(out, axis=0)


def _window_mean(s, gidx):
    inv = jnp.where(gidx == 0, 0.5, jnp.where(gidx == 1, 0.25, jnp.where(gidx == 2, 0.125, 0.0625)))
    width = lax.shift_left(jnp.int32(2), gidx)
    head = s[:16] / jnp.minimum(_row_ids((16, s.shape[1])) + 1, width).astype(jnp.float32)
    return jnp.concatenate([head, s[16:] * inv], axis=0)


def _pool_fwd(proj, pool_w, pool_scale, token):
    def body(u_ref, pg_ref, w_ref, sc_ref, token_any, y_ref):
        del token_any
        gidx = pl.program_id(0)
        u, pg = u_ref[...], pg_ref[...]
        d = _window_mean(_window_sum(u, gidx, False), gidx) - u
        mixed = _dot(_bf(d), w_ref[...], 1, 0)
        y_ref[...] = _bf(mixed * sc_ref[...] * (pg * _sigmoid(pg)))

    return pl.pallas_call(
        body, name="pool_fwd", grid=(NGROUP,),
        in_specs=[*_POOL_SPECS,
                  pl.BlockSpec((None, GROUP, GROUP), lambda g: (g, 0, 0)),
                  pl.BlockSpec((1, GROUP), lambda g: (0, g)), ANY],
        out_specs=pl.BlockSpec((T, GROUP), lambda g: (0, g)),
        out_shape=pltpu.HBM((T, DMIX), jnp.bfloat16),
        compiler_params=_params(("parallel",)),
    )(proj, proj, pool_w, pool_scale, token)


def _tri(lower):
    r = lax.broadcasted_iota(jnp.int32, (CHUNK, CHUNK), 0)
    c = lax.broadcasted_iota(jnp.int32, (CHUNK, CHUNK), 1)
    return (r >= c) if lower else (r <= c)


def _sum_rows_matrix():
    shape = (CHUNK + 16, CHUNK)
    r, c = lax.broadcasted_iota(jnp.int32, shape, 0), lax.broadcasted_iota(jnp.int32, shape, 1)
    run = jnp.where(c <= r, 1.0, 0.0)
    half = jnp.where(c < CHUNK // 2, 1.0, 0.0)
    return _bf(jnp.where(r < CHUNK, run, jnp.where(r < CHUNK + 8, 1.0, half)))


def _rev_sum_matrix():
    shape = (CHUNK, 2 * CHUNK)
    r, c = lax.broadcasted_iota(jnp.int32, shape, 0), lax.broadcasted_iota(jnp.int32, shape, 1)
    return _bf(jnp.where(c < CHUNK, jnp.where(c >= r, 1.0, 0.0), jnp.where(c - CHUNK < r, 1.0, 0.0)))


def _split2(a):
    hi = _bf(a)
    return [hi, _bf(a - hi.astype(jnp.float32))]


def _exact_sums(mat, pieces):
    x = jnp.concatenate([s for p in pieces for s in _split2(p)], axis=1)
    r = _dot(mat, x, 1, 0)
    return [r[:, 2 * j * HEAD:(2 * j + 1) * HEAD] + r[:, (2 * j + 1) * HEAD:(2 * j + 2) * HEAD]
            for j in range(len(pieces))]


def _gates(qv, fl, lb):
    sq = _sigmoid(qv)
    sg = _sigmoid(fl)
    f = lb + (1.0 - lb) * sg
    return dict(sq=sq, qs=qv * sq, sg=sg, f=f, kk=1.0 - f, g=jnp.log2(f))


def _decays(sums):
    big_g = sums[:CHUNK]
    total = sums[CHUNK:CHUNK + 8]
    g_last = jnp.tile(total, (CHUNK // 8, 1))
    g_mid = jnp.tile(sums[CHUNK + 8:], (CHUNK // 8, 1))
    return dict(
        e_q=jnp.exp2(big_g),
        e_k=jnp.exp2(g_last - big_g),
        e_qm=jnp.exp2(jnp.minimum(big_g - g_mid, EXP_CAP)),
        e_km=jnp.exp2(jnp.minimum(g_mid - big_g, EXP_CAP)),
        total8=jnp.exp2(total))


def _group_rows(gi):
    return [pl.ds(pl.multiple_of((gi * NB + j) * CHUNK, CHUNK), CHUNK) for j in range(NB)]


def _lower_bound(lb_ref):
    return _sigmoid(lb_ref[0:1, :] - lb_ref[1:2, :])


def _hgrn_fwd(proj, lb_logits, rec_g, y_in):
    def body(q_ref, f_ref, i_ref, gate_ref, lb_ref, rg_ref, y_any, y_ref, o_ref, st_ref):
        del y_any
        lb = _lower_bound(lb_ref)
        causal = _tri(True)
        smat = _sum_rows_matrix()

        def group(gi, st):
            rows = _group_rows(gi)
            ts = [_gates(q_ref[r, :], f_ref[r, :], lb) for r in rows]
            ds = [_decays(s) for s in _exact_sums(smat, [t["g"] for t in ts])]
            vs = [_bf(i_ref[r, :]) for r in rows]
            q_m = [_bf(t["qs"] * d["e_qm"]) for t, d in zip(ts, ds)]
            k_m = [_bf(t["kk"] * d["e_km"]) for t, d in zip(ts, ds)]
            q_e = [_bf(t["qs"] * d["e_q"]) for t, d in zip(ts, ds)]
            k_e = [_bf(t["kk"] * d["e_k"]) for t, d in zip(ts, ds)]
            a = [_bf(jnp.where(causal, _dot(q_m[j], k_m[j], 1, 1), 0.0)) for j in range(NB)]
            intra = [_dot(a[j], vs[j], 1, 0) for j in range(NB)]
            upd = [_dot(vs[j], k_e[j], 0, 0) for j in range(NB)]
            for j in range(NB):
                st_ref[gi * NB + j] = st
                o_ref[rows[j], :] = intra[j] + _dot(q_e[j], _bf(st), 1, 1)
                st = st * jnp.tile(ds[j]["total8"], (HEAD // 8, 1)) + upd[j]
            return st

        lax.fori_loop(0, NGRP, group, jnp.zeros((HEAD, HEAD), jnp.float32))
        o = o_ref[...]
        rn = o * lax.rsqrt(jnp.mean(o * o, axis=-1, keepdims=True) + EPS)
        gate = gate_ref[...]
        y_ref[...] = _bf(rn * rg_ref[...] * (gate * _sigmoid(gate)))

    return pl.pallas_call(
        body, name="hgrn_fwd", grid=(NHEAD,),
        in_specs=[*_HEAD_SPECS,
                  pl.BlockSpec((2, HEAD), lambda h: (0, h)),
                  pl.BlockSpec((1, HEAD), lambda h: (0, h)),
                  pl.BlockSpec(memory_space=pl.ANY)],
        out_specs=(pl.BlockSpec((T, HEAD), lambda h: (0, NHEAD + h)),
                   pl.BlockSpec((T, HEAD), lambda h: (0, h)),
                   pl.BlockSpec((None, NCHUNK, HEAD, HEAD), lambda h: (h, 0, 0, 0))),
        out_shape=(pltpu.HBM((T, DMIX), jnp.bfloat16), pltpu.HBM((T, D), jnp.float32),
                   pltpu.HBM((NHEAD, NCHUNK, HEAD, HEAD), jnp.float32)),
        input_output_aliases={6: 0},
        compiler_params=_params(("parallel",)),
    )(proj, proj, proj, proj, lb_logits, rec_g, y_in)


def _out_proj_loss(x, y, w_out, target, gf):
    rows = 512
    parts = [slice(k * rows // 2, (k + 1) * rows // 2) for k in range(2)]

    def body(x_ref, y_ref, w_ref, t_ref, g_ref, dz_ref, dzb_ref, sq_ref, dg_ref):
        zs = [x_ref[p, :] + _dot(y_ref[p, :], w_ref[...], 1, 0) for p in parts]
        sq = dg = 0.0
        for p, z in zip(parts, zs):
            r = lax.rsqrt(jnp.mean(z * z, axis=-1, keepdims=True) + EPS)
            zhat = z * r
            err = zhat * g_ref[...] - t_ref[p, :]
            dy = err * (1.0 / D)
            gdy = dy * g_ref[...]
            dz = r * (gdy - zhat * jnp.mean(zhat * gdy, axis=-1, keepdims=True))
            dz_ref[p, :] = dz
            dzb_ref[p, :] = _bf(dz)
            sq = sq + jnp.sum(err * err, axis=0, keepdims=True)
            dg = dg + jnp.sum(zhat * dy, axis=0, keepdims=True)

        @pl.when(pl.program_id(0) == 0)
        def _():
            sq_ref[...] = sq
            dg_ref[...] = dg

        @pl.when(pl.program_id(0) != 0)
        def _():
            sq_ref[...] += sq
            dg_ref[...] += dg

    tile = pl.BlockSpec((rows, D), lambda i: (i, 0))
    vec = pl.BlockSpec((1, D), lambda i: (0, 0))
    return pl.pallas_call(
        body, name="out_proj_loss", grid=(T // rows,),
        in_specs=[tile, pl.BlockSpec((rows, DMIX), lambda i: (i, 0)), pl.BlockSpec((DMIX, D), lambda i: (0, 0)),
                  tile, vec],
        out_specs=(tile, tile, vec, vec),
        out_shape=(pltpu.HBM((T, D), jnp.float32), pltpu.HBM((T, D), jnp.bfloat16),
                   jax.ShapeDtypeStruct((1, D), jnp.float32), jax.ShapeDtypeStruct((1, D), jnp.float32)),
        compiler_params=_params(("arbitrary",)),
    )(x, y, w_out, target, gf)


def _out_proj_bwd(dzb, w_out, y):
    tn = 512

    def body(dz_ref, w_ref, y_ref, dy_ref, gw_ref, gwb_ref):
        dz = dz_ref[...]
        dy_ref[...] = _dot(dz, w_ref[...], 1, 1)
        gw = _dot(y_ref[...], dz, 0, 0)
        gw_ref[...] = gw
        gwb_ref[...] = _bf(gw)

    return pl.pallas_call(
        body, name="out_proj_bwd", grid=(DMIX // tn,),
        in_specs=[pl.BlockSpec((T, D), lambda n: (0, 0)), pl.BlockSpec((tn, D), lambda n: (n, 0)),
                  pl.BlockSpec((T, tn), lambda n: (0, n))],
        out_specs=(pl.BlockSpec((T, tn), lambda n: (0, n)), pl.BlockSpec((tn, D), lambda n: (n, 0)),
                   pl.BlockSpec((tn, D), lambda n: (n, 0))),
        out_shape=(pltpu.HBM((T, DMIX), jnp.float32), pltpu.HBM((DMIX, D), jnp.float32),
                   pltpu.HBM((DMIX, D), jnp.bfloat16)),
        compiler_params=_params(("parallel",)),
    )(dzb, w_out, y)


def _hgrn_bwd(proj, lb_logits, rec_g, o, states, dymix, dproj_in, token):
    def body(q_ref, f_ref, i_ref, gate_ref, lb_ref, rg_ref, o_ref, st_ref, dy_ref, dp_any, token_any,
             dp_ref, drg_ref, dlb_ref, do_ref):
        del dp_any, token_any
        lb = _lower_bound(lb_ref)
        causal = _tri(True)
        smat, rmat = _sum_rows_matrix(), _rev_sum_matrix()

        o = o_ref[...]
        rs = lax.rsqrt(jnp.mean(o * o, axis=-1, keepdims=True) + EPS)
        rn = o * rs
        gate = gate_ref[...]
        sgate = _sigmoid(gate)
        dyv = dy_ref[...]
        d_r = dyv * (gate * sgate)
        dp_ref[3] = _bf(dyv * (rn * rg_ref[...]) * (sgate * (1.0 + gate * (1.0 - sgate))))
        drg_ref[...] = jnp.sum(d_r * rn, axis=0, keepdims=True)
        drn = d_r * rg_ref[...]
        do_ref[...] = rs * (drn - rn * jnp.mean(rn * drn, axis=-1, keepdims=True))

        def group(i, carry):
            dst, dlb = carry
            gi = NGRP - 1 - i
            rows = _group_rows(gi)
            span = range(NB)
            qvs = [q_ref[r, :] for r in rows]
            ts = [_gates(qv, f_ref[r, :], lb) for qv, r in zip(qvs, rows)]
            ds = [_decays(s) for s in _exact_sums(smat, [t["g"] for t in ts])]
            vs = [_bf(i_ref[r, :]) for r in rows]
            dos = [_bf(do_ref[r, :]) for r in rows]
            sts = [st_ref[gi * NB + j] for j in span]
            qe_f = [t["qs"] * d["e_q"] for t, d in zip(ts, ds)]
            ke_f = [t["kk"] * d["e_k"] for t, d in zip(ts, ds)]
            q_e, k_e = [_bf(a) for a in qe_f], [_bf(a) for a in ke_f]
            q_m = [_bf(t["qs"] * d["e_qm"]) for t, d in zip(ts, ds)]
            k_m = [_bf(t["kk"] * d["e_km"]) for t, d in zip(ts, ds)]
            a = [_bf(jnp.where(causal, _dot(q_m[j], k_m[j], 1, 1), 0.0)) for j in span]
            da = [_bf(jnp.where(causal, _dot(dos[j], vs[j], 1, 1), 0.0)) for j in span]
            dqm = [_dot(da[j], k_m[j], 1, 0) for j in span]
            dkm = [_dot(da[j], q_m[j], 0, 0) for j in span]
            dv_in = [_dot(a[j], dos[j], 0, 0) for j in span]
            dqe = [_dot(dos[j], _bf(sts[j]), 1, 0) for j in span]
            grow = [_dot(dos[j], q_e[j], 0, 0) for j in span]
            dke, carried = [None] * NB, [None] * NB
            for j in reversed(span):
                dst_b = _bf(dst)
                dke[j] = _dot(vs[j], dst_b, 1, 0)
                dp_ref[2, rows[j], :] = _bf(dv_in[j] + _dot(k_e[j], dst_b, 1, 1))
                carried[j] = ds[j]["total8"] * jnp.sum(dst * sts[j], axis=0, keepdims=True)
                dst = dst * jnp.tile(ds[j]["total8"], (HEAD // 8, 1)) + grow[j]
            kdk = [ke_f[j] * dke[j] for j in span]
            pos = [(q_m[j].astype(jnp.float32) * dqm[j] - k_m[j].astype(jnp.float32) * dkm[j]) + qe_f[j] * dqe[j]
                   for j in span]
            dgs = _exact_sums(rmat, [jnp.concatenate([pos[j], kdk[j]], axis=0) for j in span])
            for j in span:
                t, d = ts[j], ds[j]
                dg = dgs[j] + jnp.tile(carried[j], (CHUNK // 8, 1))
                dqs = dqm[j] * d["e_qm"] + dqe[j] * d["e_q"]
                dkk = dkm[j] * d["e_km"] + dke[j] * d["e_k"]
                df = dg / t["f"] - dkk
                dp_ref[1, rows[j], :] = _bf(df * (1.0 - lb) * (t["sg"] * (1.0 - t["sg"])))
                dp_ref[0, rows[j], :] = _bf(dqs * (t["sq"] * (1.0 + qvs[j] * (1.0 - t["sq"]))))
                dlb = dlb + df * (1.0 - t["sg"])
            return dst, dlb

        _, dlb = lax.fori_loop(0, NGRP, group, (jnp.zeros((HEAD, HEAD), jnp.float32),
                                                jnp.zeros((CHUNK, HEAD), jnp.float32)))
        dlb_ref[...] = jnp.sum(dlb, axis=0, keepdims=True)

    vec = pl.BlockSpec((1, HEAD), lambda h: (0, h))
    return pl.pallas_call(
        body, name="hgrn_bwd", grid=(NHEAD,),
        in_specs=[*_HEAD_SPECS,
                  pl.BlockSpec((2, HEAD), lambda h: (0, h)), vec,
                  pl.BlockSpec((T, HEAD), lambda h: (0, h)),
                  pl.BlockSpec((None, NCHUNK, HEAD, HEAD), lambda h: (h, 0, 0, 0)),
                  pl.BlockSpec((T, HEAD), lambda h: (0, NHEAD + h)), ANY, ANY],
        out_specs=(pl.BlockSpec((4, T, HEAD), lambda h: (0, 0, h)), vec, vec),
        out_shape=(pltpu.HBM((NSEG, T, D), jnp.bfloat16),
                   jax.ShapeDtypeStruct((1, D), jnp.float32), jax.ShapeDtypeStruct((1, D), jnp.float32)),
        scratch_shapes=[pltpu.VMEM((T, HEAD), jnp.float32)],
        input_output_aliases={9: 0},
        compiler_params=_params(("parallel",)),
    )(proj, proj, proj, proj, lb_logits, rec_g, o, states, dymix, dproj_in, token)


def _pool_bwd(proj, pool_w, pool_scale, dymix):
    def body(u_ref, pg_ref, w_ref, sc_ref, dy_ref, dp_ref, gw_ref, gs_ref):
        gidx = pl.program_id(0)
        u, pg = u_ref[...], pg_ref[...]
        d = _bf(_window_mean(_window_sum(u, gidx, False), gidx) - u)
        mixed = _dot(d, w_ref[...], 1, 0)
        spg = _sigmoid(pg)
        dyv = dy_ref[...]
        d_p = dyv * (pg * spg)
        dp_ref[1] = _bf(dyv * (mixed * sc_ref[...]) * (spg * (1.0 + pg * (1.0 - spg))))
        gs_ref[...] = jnp.sum(d_p * mixed, axis=0, keepdims=True)
        dmixed = _bf(d_p * sc_ref[...])
        gw_ref[...] = _dot(d, dmixed, 0, 0)
        dd = _dot(dmixed, w_ref[...], 1, 1)
        dp_ref[0] = _bf(_window_sum(_window_mean(dd, gidx), gidx, True) - dd)

    return pl.pallas_call(
        body, name="pool_bwd", grid=(NGROUP,),
        in_specs=[*_POOL_SPECS,
                  pl.BlockSpec((None, GROUP, GROUP), lambda g: (g, 0, 0)),
                  pl.BlockSpec((1, GROUP), lambda g: (0, g)),
                  pl.BlockSpec((T, GROUP), lambda g: (0, g))],
        out_specs=(pl.BlockSpec((2, T, GROUP), lambda g: (2, 0, g)),
                   pl.BlockSpec((None, GROUP, GROUP), lambda g: (g, 0, 0)),
                   pl.BlockSpec((1, GROUP), lambda g: (0, g))),
        out_shape=(pltpu.HBM((NSEG, T, D), jnp.bfloat16),
                   jax.ShapeDtypeStruct((NGROUP, GROUP, GROUP), jnp.float32),
                   jax.ShapeDtypeStruct((1, D), jnp.float32)),
        compiler_params=_params(("parallel",)),
    )(proj, proj, pool_w, pool_scale, dymix)


HALF = NTILE // 2
AWAY = HALF - 3


def _dproj_tile(chip, side, p):
    j = 6 * chip + 3 * side + p
    return ((j // 4 + 4) % NSEG, 0, j % 4)


def _sibling_copy(sib_out, sib_in, send_sems, recv_sems, slot):
    x, y, c, _ = _place()
    return pltpu.make_async_remote_copy(
        src_ref=sib_out.at[slot], dst_ref=sib_in.at[slot], send_sem=send_sems.at[slot],
        recv_sem=recv_sems.at[slot], device_id=(x, y, 1 - c), device_id_type=MESH)


def _proj_bwd_w_far(place, ht, dproj):
    def body(place_ref, h_ref, dp_ref, sib_out, sib_in, send_sems, recv_sems, stage, loc_sems):
        del place_ref
        i = pl.program_id(0)

        def to_hbm(k):
            return pltpu.make_async_copy(stage.at[k], sib_out.at[k], loc_sems.at[k])

        def send(k):
            to_hbm(k).wait()
            _sibling_copy(sib_out, sib_in, send_sems, recv_sems, k).start()

        stage[i] = _bf(_dot(h_ref[...], dp_ref[...], 1, 0))

        @pl.when(i > 0)
        def _():
            send(i - 1)

        to_hbm(i).start()

        @pl.when(i == HALF - 1)
        def _():
            send(i)

    buf = pltpu.HBM((HALF, D, TILE), jnp.bfloat16)
    sems = pltpu.SemaphoreType.DMA((HALF,))
    return pl.pallas_call(
        body, name="proj_bwd_w_far",
        grid_spec=pltpu.PrefetchScalarGridSpec(
            num_scalar_prefetch=1, grid=(HALF,),
            in_specs=[pl.BlockSpec((D, T), lambda i, pr: (0, 0)),
                      pl.BlockSpec((None, T, TILE), lambda i, pr: _dproj_tile(i // 3, 1 - pr[2], i % 3))],
            out_specs=(HBM, HBM, SEM, SEM),
            scratch_shapes=[pltpu.VMEM((HALF, D, TILE), jnp.bfloat16), pltpu.SemaphoreType.DMA((HALF,))]),
        out_shape=(buf, buf, sems, sems),
        compiler_params=pltpu.CompilerParams(dimension_semantics=("arbitrary",), vmem_limit_bytes=48 * MIB,
                                             has_side_effects=EFFECT),
    )(place, ht, dproj)


def _proj_bwd_w_near(place, ht, dproj, sib_out, sib_in, sib_send, sib_recv):
    def owner_chip(k, pr):
        return jnp.where(k < AWAY, (pr[1] + 1 + k % 3) % 4, pr[1])

    def tile_p(k):
        return jnp.where(k < AWAY, k // 3, k - AWAY)

    def body(place_ref, h_ref, dp_ref, sib_out, sib_in, sib_send, sib_recv, sums, own_ref, landing, out_send,
             out_recv, recvbuf, outbuf, in_sems, loc_sems):
        i = pl.program_id(0)
        px, py, c, _ = _place()

        def slot_of(k):
            return 3 * owner_chip(k, place_ref) + tile_p(k)

        def load(k):
            return pltpu.make_async_copy(sib_in.at[slot_of(k)], recvbuf.at[k % 2], in_sems.at[k % 2])

        def fetch(k):
            _sibling_copy(sib_out, sib_in, sib_send, sib_recv, slot_of(k)).wait_recv()
            load(k).start()

        def route(k):
            chip = owner_chip(k, place_ref)
            cx, cy = chip // 2, chip % 2
            return cx, cy, (cx ^ px) + 2 * (cy ^ py) - 1, tile_p(k)

        def to_hbm(k):
            _, _, rel, p = route(k)
            return pltpu.make_async_copy(outbuf.at[k], sums.at[rel, p], loc_sems.at[k])

        def to_owner(k):
            cx, cy, rel, p = route(k)
            return pltpu.make_async_remote_copy(
                src_ref=sums.at[rel, p], dst_ref=landing.at[rel, p], send_sem=out_send.at[3 * rel + p],
                recv_sem=out_recv.at[3 * rel + p], device_id=(cx, cy, c), device_id_type=MESH)

        @pl.when(i == 0)
        def _():
            fetch(i)

        @pl.when(i < HALF - 1)
        def _():
            fetch(i + 1)

        load(i).wait()
        total = _dot(h_ref[...], dp_ref[...], 1, 0) + recvbuf[i % 2].astype(jnp.float32)
        own_ref[...] = total
        outbuf[jnp.minimum(i, AWAY)] = _bf(total)

        @pl.when(i < AWAY)
        def _():
            to_hbm(i).start()

        @pl.when(jnp.logical_and(i > 0, i <= AWAY))
        def _():
            to_hbm(i - 1).wait()
            to_owner(i - 1).start()

        @pl.when(i == HALF - 1)
        def _():
            for slot in range(HALF):
                _sibling_copy(sib_out, sib_in, sib_send, sib_recv, slot).wait_send()

    travelling = pltpu.HBM((3, 3, D, TILE), jnp.bfloat16)
    sems = pltpu.SemaphoreType.DMA((AWAY,))
    return pl.pallas_call(
        body, name="proj_bwd_w_near",
        grid_spec=pltpu.PrefetchScalarGridSpec(
            num_scalar_prefetch=1, grid=(HALF,),
            in_specs=[pl.BlockSpec((D, T), lambda i, pr: (0, 0)),
                      pl.BlockSpec((None, T, TILE), lambda i, pr: _dproj_tile(owner_chip(i, pr), pr[2], tile_p(i))),
                      HBM, HBM, SEM, SEM],
            out_specs=(HBM, pl.BlockSpec((None, D, TILE), lambda i, pr: (jnp.where(i < AWAY, 0, i % 3), 0, 0)),
                       HBM, SEM, SEM),
            scratch_shapes=[pltpu.VMEM((2, D, TILE), jnp.bfloat16), pltpu.VMEM((AWAY + 1, D, TILE), jnp.bfloat16),
                            pltpu.SemaphoreType.DMA((2,)), pltpu.SemaphoreType.DMA((AWAY,))]),
        out_shape=(travelling, pltpu.HBM((3, D, TILE), jnp.float32), travelling, sems, sems),
        compiler_params=pltpu.CompilerParams(dimension_semantics=("arbitrary",), vmem_limit_bytes=48 * MIB,
                                             has_side_effects=EFFECT),
    )(place, ht, dproj, sib_out, sib_in, sib_send, sib_recv)


def _proj_bwd_x(dproj, w_t, x, g1, dz, token):
    tm = 512
    pairs = NSEG // 2

    def body(dp_ref, w_ref, x_ref, g_ref, dz_ref, token_any, dx_ref, dg_ref, wcat, acc):
        del token_any
        m, s = pl.program_id(0), pl.program_id(1)

        @pl.when(m == 0)
        def _():
            for i in range(8):
                wcat[s, :, i * TILE:(i + 1) * TILE] = w_ref[i]

        @pl.when(s == 0)
        def _():
            acc[...] = jnp.zeros((tm, D), jnp.float32)

        acc[...] += _dot(jnp.concatenate([dp_ref[0], dp_ref[1]], axis=1), wcat[s], 1, 1)

        @pl.when(s == pairs - 1)
        def _():
            xv = x_ref[...]
            rs = lax.rsqrt(jnp.mean(xv * xv, axis=-1, keepdims=True) + EPS)
            xhat = xv * rs
            dhv = acc[...]
            gdh = dhv * g_ref[...]
            dx_ref[...] = dz_ref[...] + rs * (gdh - xhat * jnp.mean(xhat * gdh, axis=-1, keepdims=True))
            dg = jnp.sum(xhat * dhv, axis=0, keepdims=True)

            @pl.when(m == 0)
            def _():
                dg_ref[...] = dg

            @pl.when(m != 0)
            def _():
                dg_ref[...] += dg

    rows = pl.BlockSpec((tm, D), lambda m, s: (m, 0))
    vec = pl.BlockSpec((1, D), lambda m, s: (0, 0))
    return pl.pallas_call(
        body, name="proj_bwd_x", grid=(T // tm, pairs),
        in_specs=[pl.BlockSpec((2, tm, D), lambda m, s: (s, m, 0)),
                  pl.BlockSpec((8, D, TILE), lambda m, s: ((jnp.where(m == 0, s, pairs - 1) + 1) % pairs, 0, 0)),
                  rows, vec, rows, ANY],
        out_specs=(rows, vec),
        out_shape=(jax.ShapeDtypeStruct((T, D), jnp.float32), jax.ShapeDtypeStruct((1, D), jnp.float32)),
        scratch_shapes=[pltpu.VMEM((pairs, D, 2 * D), jnp.bfloat16), pltpu.VMEM((tm, D), jnp.float32)],
        compiler_params=_params(("arbitrary", "arbitrary"), vmem_mib=56),
    )(dproj, w_t, x, g1, dz, token)


def _adamw(w, g, m, v):
    m_new = ADAM_B1 * m + (1.0 - ADAM_B1) * g
    v_new = ADAM_B2 * v + (1.0 - ADAM_B2) * (g * g)
    delta = -ADAM_LR * ((m_new / BC1) / (jnp.sqrt(v_new / BC2) + ADAM_EPS) + ADAM_WD * w)
    return delta, m_new, v_new


def _reduce_adam(name, place, parts, w, m, v, grid, w_spec):
    n = len(parts)

    def body(place_ref, *refs):
        del place_ref
        w_ref, m_ref, v_ref, g_ref, d_ref, mo_ref, vo_ref = refs[n:]
        g = None
        for ref, (_, _, stacked) in zip(refs[:n], parts):
            terms = [ref[r] for r in range(ref.shape[0])] if stacked else [ref[...]]
            for t in terms:
                if t.shape[-1] != w_ref.shape[-1]:
                    t = jnp.concatenate([t[p] for p in range(t.shape[0])], axis=1)
                g = t.astype(jnp.float32) if g is None else g + t.astype(jnp.float32)
        delta, m_new, v_new = _adamw(w_ref[...], g, m_ref[...], v_ref[...])
        g_ref[...] = g
        d_ref[...] = delta
        mo_ref[...] = m_new
        vo_ref[...] = v_new

    shape = jax.ShapeDtypeStruct(w.shape, jnp.float32)
    return pl.pallas_call(
        body, name=name,
        grid_spec=pltpu.PrefetchScalarGridSpec(
            num_scalar_prefetch=1, grid=grid,
            in_specs=[spec for _, spec, _ in parts] + [w_spec] * 3, out_specs=(w_spec,) * 4),
        out_shape=(shape,) * 4,
        compiler_params=_params(("parallel",)),
    )(place, *[_in_hbm(a) for a in [a for a, _, _ in parts] + [w, m, v]])


def _small_adam(place, own, parts, w, m, v):
    def body(place_ref, own_ref, p_ref, w_ref, m_ref, v_ref, g_ref, d_ref, mo_ref, vo_ref):
        me = place_ref[0]
        g = None
        for s in range(NDEV):
            term = jnp.where(me == s, own_ref[...], p_ref[s])
            g = term if g is None else g + term
        wv = w_ref[...]
        rows = _row_ids(wv.shape)
        other = jnp.where(rows == 2, pltpu.roll(wv, 7, 0), jnp.where(rows == 3, pltpu.roll(wv, 1, 0), 0.0))
        lbv = _sigmoid(wv - other)
        sign = jnp.where(rows == 2, 1.0, -1.0)
        g = jnp.where((rows == 2) | (rows == 3), sign * g * lbv * (1.0 - lbv), g)
        delta, m_new, v_new = _adamw(wv, g, m_ref[...], v_ref[...])
        g_ref[...] = g
        d_ref[...] = delta
        mo_ref[...] = m_new
        vo_ref[...] = v_new

    shape = jax.ShapeDtypeStruct((8, D), jnp.float32)
    vmem = pl.BlockSpec(memory_space=pltpu.VMEM)
    return pl.pallas_call(
        body, name="small_adam", out_shape=(shape,) * 4,
        in_specs=[pl.BlockSpec(memory_space=pltpu.SMEM)] + [vmem] * 5, out_specs=(vmem,) * 4,
    )(place, own, parts, w, m, v)


def _rows8(*vecs):
    rows = [a.reshape(-1, D) for a in vecs]
    n = sum(r.shape[0] for r in rows)
    return jnp.concatenate(rows + [jnp.zeros((8 - n, D), jnp.float32)], axis=0)


def kernel(x, norm1_g, w_in, pool_w, pool_scale, lb_logits, rec_norm_g, w_out, final_norm_g, loss_target, m_norm1_g, m_w_in, m_pool_w, m_pool_scale, m_lb_logits, m_rec_norm_g, m_w_out, m_final_norm_g, v_norm1_g, v_w_in, v_pool_w, v_pool_scale, v_lb_logits, v_rec_norm_g, v_w_out, v_final_norm_g):
    xs = x[0]
    target = loss_target[0]
    ix, iy, ic = lax.axis_index("x"), lax.axis_index("y"), lax.axis_index("c")
    place = jnp.stack([4 * ix + 2 * iy + ic, 2 * ix + iy, ic]).astype(jnp.int32)
    gf = final_norm_g.reshape(1, D)

    ht, w_t, w_out_b, w_out_g, pool_g, proj = _gather_proj(xs, norm1_g, w_in, w_out, pool_w)
    pool_full = pool_g.transpose(1, 0, 2, 3).reshape(NGROUP, GROUP, GROUP)
    wout = [w_out_b, w_out_g]
    wout_send, wout_recv, wout, wout_token = _split_start("gather_wout_start", wout, NDEV - 1, _plan_wout)

    y = _pool_fwd(proj, pool_full, pool_scale, wout_token)
    y, o, states = _hgrn_fwd(proj, lb_logits, rec_norm_g, y)
    _, w_out_g = _split_wait("gather_wout_wait", wout, wout_send, wout_recv, _plan_wout, o)
    w_out_full = _in_hbm(w_out_g.reshape(DMIX, D))
    dz, dzb, sq, dgf = _out_proj_loss(xs, y, w_out_full, target, gf)

    dymix, gwout_f, gwout_b = _out_proj_bwd(dzb, w_out_full, y)
    dproj, gpool, dscale = _pool_bwd(proj, pool_full, pool_scale, dymix)

    blk_out = (NDEV, DMIX // NDEV, D)
    blk_pool = (NDEV, NGROUP, GROUP // NDEV, GROUP)
    gpool_s = gpool.reshape(NGROUP, NDEV, GROUP // NDEV, GROUP).transpose(1, 0, 2, 3)
    rest = [gwout_b.reshape(blk_out), gpool_s,
            lax.empty((NDEV - 1,) + blk_out[1:], jnp.bfloat16), lax.empty((NDEV - 1,) + blk_pool[1:], jnp.float32)]
    rest_send, rest_recv, rest, rest_token = _split_start("scatter_rest_start", rest, 2 * (NDEV - 1), _plan_rest)

    dproj, drecg, dlb = _hgrn_bwd(proj, lb_logits, rec_norm_g, o, states, dymix, dproj, rest_token)
    chip_sums, own_sum, landing, win_send, win_recv = _proj_bwd_w_near(
        place, ht, dproj, *_proj_bwd_w_far(place, ht, dproj))
    win = [chip_sums, landing]

    grad_x, dg1 = _proj_bwd_x(dproj, w_t, xs, norm1_g, dz, chip_sums)

    small = [_rows8(dg1, dscale, dlb, dlb, drecg, dgf, sq), lax.empty((NDEV, 8, D), jnp.float32)]
    small_send, small_recv, small, small_token = _split_start("gather_small_start", small, NDEV - 1, _plan_small)

    _, gpool_own, r_out, r_pool = _split_wait("scatter_rest_wait", rest, rest_send, rest_recv, _plan_rest,
                                              small_token)
    g_wout, d_wout, m_wout, v_wout = _reduce_adam(
        "adam_w_out", place,
        [(gwout_f.reshape(blk_out), pl.BlockSpec((None,) + blk_out[1:], lambda i, pr: (pr[0], 0, 0)), False),
         (r_out, pl.BlockSpec((NDEV - 1,) + blk_out[1:], lambda i, pr: (0, 0, 0)), True)],
        w_out, m_w_out, v_w_out, (1,), pl.BlockSpec((None,) + blk_out[1:], lambda i, pr: (0, 0, 0)))
    g_pool, d_pool, m_pool, v_pool = _reduce_adam(
        "adam_pool_w", place,
        [(gpool_own, pl.BlockSpec((None,) + blk_pool[1:], lambda i, pr: (pr[0], 0, 0, 0)), False),
         (r_pool, pl.BlockSpec((NDEV - 1,) + blk_pool[1:], lambda i, pr: (0, 0, 0, 0)), True)],
        pool_w, m_pool_w, v_pool_w, (1,), pl.BlockSpec((None,) + blk_pool[1:], lambda i, pr: (0, 0, 0, 0)))

    _, r_in = _split_wait("scatter_win_wait", win, win_send, win_recv, _plan_in, d_pool)
    g_win, d_win, m_win, v_win = _reduce_adam(
        "adam_w_in", place,
        [(own_sum, pl.BlockSpec((3, D // 8, TILE), lambda i, pr: (0, i, 0)), False),
         (r_in, pl.BlockSpec((3, 3, D // 8, TILE), lambda i, pr: (0, 0, i, 0)), True)],
        w_in, m_w_in, v_w_in, (8,), pl.BlockSpec((None, D // 8, 3 * TILE), lambda i, pr: (0, i, 0)))

    own_small, r_small = _split_wait("gather_small_wait", small, small_send, small_recv, _plan_small, d_win)
    g_s, d_s, m_s, v_s = _small_adam(
        place, own_small, r_small,
        _rows8(norm1_g, pool_scale, lb_logits, rec_norm_g, final_norm_g),
        _rows8(m_norm1_g, m_pool_scale, m_lb_logits, m_rec_norm_g, m_final_norm_g),
        _rows8(v_norm1_g, v_pool_scale, v_lb_logits, v_rec_norm_g, v_final_norm_g))
    loss = jnp.sum(g_s[6]) * (0.5 / D)

    def small_outs(a):
        return a[0:1], a[1:2], a[2:4], a[4:5], a[5]

    def outs(small_a, win, pool, wout):
        n1, ps, lbl, rg, fg = small_outs(small_a)
        return n1, win, pool, ps, lbl, rg, wout, fg

    return (loss, grad_x[None],
            *outs(g_s, g_win, g_pool, g_wout), *outs(d_s, d_win, d_pool, d_wout),
            *outs(m_s, m_win, m_pool, m_wout), *outs(v_s, v_win, v_pool, v_wout))
```

```python
import functools

import jax
import jax.numpy as jnp
from jax import lax
from jax.experimental import pallas as pl
from jax.experimental.pallas import tpu as pltpu

T = 2048
D = 1024
NSEG = 6
NTILE = 24
TILE = 256
DMIX = 2048
NDEV = 8
HEAD = 128
NHEAD = 8
CHUNK = 64
NCHUNK = T // CHUNK
NB = 32
NGRP = NCHUNK // NB
NGROUP = 4
GROUP = 256
EPS = 1e-6
EXP_CAP = 115.0
MESH = pl.DeviceIdType.MESH
AXES = ("x", "y", "c")
ANY = pl.BlockSpec(memory_space=pl.ANY)
HBM = pl.BlockSpec(memory_space=pltpu.HBM)
SEM = pl.BlockSpec(memory_space=pltpu.SEMAPHORE)
EFFECT = pltpu.SideEffectType.DATAFLOW_SIDE_EFFECTING

ADAM_LR = 0.001
ADAM_B1 = 0.9
ADAM_B2 = 0.999
ADAM_EPS = 1e-08
ADAM_WD = 0.01
ADAM_STEP = 10
BC1 = 1.0 - ADAM_B1 ** ADAM_STEP
BC2 = 1.0 - ADAM_B2 ** ADAM_STEP

MIB = 1 << 20


def _params(sem=None, vmem_mib=48):
    return pltpu.CompilerParams(dimension_semantics=sem, vmem_limit_bytes=vmem_mib * MIB)


def _sigmoid(v):
    return 1.0 / (1.0 + jnp.exp(-v))


def _dot(a, b, ca, cb, precision=None):
    return lax.dot_general(a, b, (((ca,), (cb,)), ((), ())), precision=precision,
                           preferred_element_type=jnp.float32)


def _bf(v):
    return v.astype(jnp.bfloat16)


def _in_hbm(a):
    return pltpu.with_memory_space_constraint(a, pltpu.HBM)


def _place():
    x, y, c = lax.axis_index("x"), lax.axis_index("y"), lax.axis_index("c")
    return x, y, c, 4 * x + 2 * y + c


def _peer(x, y, c, r):
    return (x ^ ((r >> 2) & 1), y ^ ((r >> 1) & 1), c ^ (r & 1))


def _gather_proj(x, g1, w_in, w_out, pool_w):
    def body(x_ref, g_ref, win_ref, wout_ref, pool_ref, ht_o, wt_o, woutb_o, wout_o, pool_o, proj_o,
             xbuf, hv, htv, wv, wob, pb, stage, send_sems, recv_sems, loc_sems, out_sems):
        px, py, c, my_idx = _place()
        fetch_x = pltpu.make_async_copy(x_ref, xbuf, loc_sems.at[5])
        fetch_x.start()
        me, sibling = (px, py, c), (px, py, 1 - c)
        chips = [(1 - px, py), (px, 1 - py), (1 - px, 1 - py)]
        for p in range(3):
            wv[3 * my_idx + p] = _bf(win_ref[0, :, p * TILE:(p + 1) * TILE])

        def index(bx, by, bc):
            return 4 * bx + 2 * by + bc

        def slot(w, block):
            return wv.at[pl.ds(3 * index(*block), 3)] if w == 0 else pool_o.at[index(*block)]

        def copy(k, w, block, to, src=None):
            return pltpu.make_async_remote_copy(
                src_ref=slot(w, block) if src is None else src, dst_ref=slot(w, block),
                send_sem=send_sems.at[2 * k + w], recv_sem=recv_sems.at[2 * k + w],
                device_id=to, device_id_type=MESH)

        def save(block):
            at = pl.ds(3 * index(*block), 3)
            pltpu.make_async_copy(wv.at[at], wt_o.at[at], loc_sems.at[4]).start()

        srcs = (slot(0, me), pb)
        first = []
        for w in (0, 1):
            if w == 1:
                pb[...] = _bf(pool_ref[0])
                wob[...] = _bf(wout_ref[0])
            group = [copy(1 + j, w, me, (*chip, c), src=srcs[w]) for j, chip in enumerate(chips[:2])]
            group.append(copy(0, w, me, sibling, src=srcs[w]))
            for cp in group:
                cp.start()
            first += group
        save(me)
        locs = [pltpu.make_async_copy(pb, slot(1, me), loc_sems.at[0]),
                pltpu.make_async_copy(wob, wout_o.at[my_idx], loc_sems.at[1]),
                pltpu.make_async_copy(wob, woutb_o, loc_sems.at[2])]
        for cp in locs:
            cp.start()

        fetch_x.wait()
        xv = xbuf[...]
        hv[...] = _bf(xv * lax.rsqrt(jnp.mean(xv * xv, axis=-1, keepdims=True) + EPS) * g_ref[...])
        rows = 256
        for r0 in range(0, T, rows):
            htv[:, r0:r0 + rows] = hv[r0:r0 + rows, :].T
        locs.append(pltpu.make_async_copy(htv, ht_o, loc_sems.at[3]))
        locs[-1].start()

        def out_copy(p, j):
            return pltpu.make_async_copy(stage.at[p], proj_o.at[j], out_sems.at[p])

        def project(nth, block):
            base = 3 * index(*block)

            def tile(p, carry):
                if nth > 0:
                    out_copy(p, base + p).wait()
                stage[p] = _dot(hv[...], wv[base + p], 1, 0)
                out_copy(p, base + p).start()
                return carry

            lax.fori_loop(0, 3, tile, 0)

        project(0, me)
        copy(0, 0, sibling, me).wait_recv()
        save(sibling)
        project(1, sibling)
        passed = []
        relay_from = (px ^ (1 - c), py ^ c, c)
        relay_to = (px ^ c, py ^ (1 - c), c)

        def arrived(w, j):
            copy(1 + j, w, (*chips[j], c), me).wait_recv()
            passed.append(copy(4 + j, w, (*chips[j], c), sibling))
            passed[-1].start()

        def relay(w):
            passed.append(copy(3, w, relay_from, relay_to))
            passed[-1].start()

        def handed(nth, j):
            copy(4 + j, 0, (*chips[j], 1 - c), me).wait_recv()
            save((*chips[j], 1 - c))
            project(nth, (*chips[j], 1 - c))

        arrived(0, 0)
        arrived(0, 1)
        relay(0)
        for j in range(2):
            save((*chips[j], c))
            project(2 + j, (*chips[j], c))
        handed(4, 0)
        handed(5, 1)
        arrived(1, 0)
        arrived(1, 1)
        relay(1)
        arrived(0, 2)
        save((*chips[2], c))
        project(6, (*chips[2], c))
        handed(7, 2)
        arrived(1, 2)
        copy(0, 1, sibling, me).wait_recv()
        for j, chip in enumerate(chips):
            copy(4 + j, 1, (*chip, 1 - c), me).wait_recv()
        keep = pltpu.make_async_copy(wv, wt_o, loc_sems.at[4])
        for p in range(3):
            out_copy(p, p).wait()
        for cp in first + passed:
            cp.wait_send()
        keep.wait()
        for cp in locs:
            cp.wait()

    vmem = pl.BlockSpec(memory_space=pltpu.VMEM)
    bf16 = jnp.bfloat16
    return pl.pallas_call(
        body, name="gather_proj",
        out_shape=(pltpu.HBM((D, T), bf16), pltpu.HBM((NTILE, D, TILE), bf16),
                   pltpu.HBM((DMIX // NDEV, D), bf16), pltpu.HBM((NDEV, DMIX // NDEV, D), bf16),
                   pltpu.HBM((NDEV, NGROUP, GROUP // NDEV, GROUP), bf16), pltpu.HBM((NTILE, T, TILE), jnp.float32)),
        in_specs=[ANY] + [vmem] * 4, out_specs=(ANY,) * 6,
        scratch_shapes=[pltpu.VMEM((T, D), jnp.float32),
                        pltpu.VMEM((T, D), bf16), pltpu.VMEM((D, T), bf16), pltpu.VMEM((NTILE, D, TILE), bf16),
                        pltpu.VMEM((DMIX // NDEV, D), bf16), pltpu.VMEM((NGROUP, GROUP // NDEV, GROUP), bf16),
                        pltpu.VMEM((3, T, TILE), jnp.float32),
                        pltpu.SemaphoreType.DMA((14,)), pltpu.SemaphoreType.DMA((14,)),
                        pltpu.SemaphoreType.DMA((6,)), pltpu.SemaphoreType.DMA((3,))],
        compiler_params=_params(vmem_mib=56),
    )(x, g1, w_in, w_out, pool_w)


def _split_start(name, arrays, n_copies, plan):
    k = len(arrays)

    def body(*refs):
        send_sems, recv_sems, token = refs[k], refs[k + 1], refs[-1]
        for i, (src, dst, to) in enumerate(plan(refs[:k])):
            pltpu.make_async_remote_copy(src_ref=src, dst_ref=dst, send_sem=send_sems.at[i],
                                         recv_sem=recv_sems.at[i], device_id=to, device_id_type=MESH).start()
        token[...] = jnp.zeros_like(token)

    out = pl.pallas_call(
        body, name=name,
        out_shape=(pltpu.SemaphoreType.DMA((n_copies,)), pltpu.SemaphoreType.DMA((n_copies,)),
                   *[pltpu.HBM(a.shape, a.dtype) for a in arrays], jax.ShapeDtypeStruct((8, 128), jnp.float32)),
        in_specs=[HBM] * k, out_specs=(SEM, SEM, *[HBM] * k, pl.BlockSpec(memory_space=pltpu.VMEM)),
        input_output_aliases={i: 2 + i for i in range(k)},
        compiler_params=pltpu.CompilerParams(has_side_effects=EFFECT),
    )(*[pltpu.with_memory_space_constraint(a, pltpu.HBM) for a in arrays])
    return out[0], out[1], out[2:2 + k], out[-1]


def _split_wait(name, arrays, send_sems, recv_sems, plan, after):
    k = len(arrays)

    def body(*refs):
        sends, recvs = refs[k], refs[k + 1]
        for i, (src, dst, to) in enumerate(plan(refs[:k])):
            cp = pltpu.make_async_remote_copy(src_ref=src, dst_ref=dst, send_sem=sends.at[i], recv_sem=recvs.at[i],
                                              device_id=to, device_id_type=MESH)
            cp.wait_send()
            cp.wait_recv()

    return pl.pallas_call(
        body, name=name,
        out_shape=tuple(pltpu.HBM(a.shape, a.dtype) for a in arrays),
        in_specs=[HBM] * k + [SEM, SEM, ANY], out_specs=(HBM,) * k,
        input_output_aliases={i: i for i in range(k)},
        compiler_params=pltpu.CompilerParams(has_side_effects=EFFECT),
    )(*arrays, send_sems, recv_sems, after)


def _plan_wout(refs):
    src, land = refs
    x, y, c, me = _place()
    return [(src, land.at[me], _peer(x, y, c, r)) for r in range(1, NDEV)]


def _plan_rest(refs):
    gob, gpf, r_out, r_pool = refs
    x, y, c, me = _place()
    plan = []
    for r in range(1, NDEV):
        plan.append((gob.at[me ^ r], r_out.at[r - 1], _peer(x, y, c, r)))
        plan.append((gpf.at[me ^ r], r_pool.at[r - 1], _peer(x, y, c, r)))
    return plan


def _plan_in(refs):
    sums, landing = refs
    x, y, c, _ = _place()
    plan = []
    for rel, (dx, dy) in enumerate(((1, 0), (0, 1), (1, 1))):
        for p in range(3):
            plan.append((sums.at[rel, p], landing.at[rel, p], (x ^ dx, y ^ dy, c)))
    return plan


def _plan_small(refs):
    small, land = refs
    x, y, c, me = _place()
    return [(small, land.at[me], _peer(x, y, c, r)) for r in range(1, NDEV)]


def _seg_tiles(s):
    return (s + 2) % NSEG


_POOL_SPECS = [pl.BlockSpec((None, T, GROUP), lambda g, base=base: (base + g, 0, 0)) for base in (0, 4)]
_HEAD_SPECS = [pl.BlockSpec((None, T, HEAD), lambda h, base=base: (base + h // 2, 0, h % 2))
               for base in (8, 12, 16, 20)]


def _row_ids(shape):
    return lax.broadcasted_iota(jnp.int32, shape, 0)


BAND_ROWS = 128
HALO = 16


def _window_sum(a, gidx, lead):
    width = lax.shift_left(jnp.int32(2), gidx)
    shape = (BAND_ROWS, BAND_ROWS + HALO)
    t, j = lax.broadcasted_iota(jnp.int32, shape, 0), lax.broadcasted_iota(jnp.int32, shape, 1)
    first = t if lead else t + HALO - width + 1
    band = _bf(jnp.where(j >= first, jnp.where(j < first + width, 1.0, 0.0), 0.0))
    zeros = jnp.zeros((HALO, a.shape[1]), jnp.bfloat16)
    padded = [jnp.concatenate([p, zeros] if lead else [zeros, p], axis=0) for p in _split2(a)]
    out = []
    for r0 in range(0, T, BAND_ROWS):
        slab = jnp.concatenate([p[r0:r0 + BAND_ROWS + HALO] for p in padded], axis=1)
        r = _dot(band, slab, 1, 0)
        out.append(r[:, :a.shape[1]] + r[:, a.shape[1]:])
    return jnp.concatenate(out, axis=0)


def _window_mean(s, gidx):
    inv = jnp.where(gidx == 0, 0.5, jnp.where(gidx == 1, 0.25, jnp.where(gidx == 2, 0.125, 0.0625)))
    width = lax.shift_left(jnp.int32(2), gidx)
    head = s[:16] / jnp.minimum(_row_ids((16, s.shape[1])) + 1, width).astype(jnp.float32)
    return jnp.concatenate([head, s[16:] * inv], axis=0)


def _pool_fwd(proj, pool_w, pool_scale, token):
    def body(u_ref, pg_ref, w_ref, sc_ref, token_any, y_ref):
        del token_any
        gidx = pl.program_id(0)
        u, pg = u_ref[...], pg_ref[...]
        d = _window_mean(_window_sum(u, gidx, False), gidx) - u
        mixed = _dot(_bf(d), w_ref[...], 1, 0)
        y_ref[...] = _bf(mixed * sc_ref[...] * (pg * _sigmoid(pg)))

    return pl.pallas_call(
        body, name="pool_fwd", grid=(NGROUP,),
        in_specs=[*_POOL_SPECS,
                  pl.BlockSpec((None, GROUP, GROUP), lambda g: (g, 0, 0)),
                  pl.BlockSpec((1, GROUP), lambda g: (0, g)), ANY],
        out_specs=pl.BlockSpec((T, GROUP), lambda g: (0, g)),
        out_shape=pltpu.HBM((T, DMIX), jnp.bfloat16),
        compiler_params=_params(("parallel",)),
    )(proj, proj, pool_w, pool_scale, token)


def _tri(lower):
    r = lax.broadcasted_iota(jnp.int32, (CHUNK, CHUNK), 0)
    c = lax.broadcasted_iota(jnp.int32, (CHUNK, CHUNK), 1)
    return (r >= c) if lower else (r <= c)


def _sum_rows_matrix():
    shape = (CHUNK + 16, CHUNK)
    r, c = lax.broadcasted_iota(jnp.int32, shape, 0), lax.broadcasted_iota(jnp.int32, shape, 1)
    run = jnp.where(c <= r, 1.0, 0.0)
    half = jnp.where(c < CHUNK // 2, 1.0, 0.0)
    return _bf(jnp.where(r < CHUNK, run, jnp.where(r < CHUNK + 8, 1.0, half)))


def _rev_sum_matrix():
    shape = (CHUNK, 2 * CHUNK)
    r, c = lax.broadcasted_iota(jnp.int32, shape, 0), lax.broadcasted_iota(jnp.int32, shape, 1)
    return _bf(jnp.where(c < CHUNK, jnp.where(c >= r, 1.0, 0.0), jnp.where(c - CHUNK < r, 1.0, 0.0)))


def _split2(a):
    hi = _bf(a)
    return [hi, _bf(a - hi.astype(jnp.float32))]


def _exact_sums(mat, pieces):
    x = jnp.concatenate([s for p in pieces for s in _split2(p)], axis=1)
    r = _dot(mat, x, 1, 0)
    return [r[:, 2 * j * HEAD:(2 * j + 1) * HEAD] + r[:, (2 * j + 1) * HEAD:(2 * j + 2) * HEAD]
            for j in range(len(pieces))]


def _gates(qv, fl, lb):
    sq = _sigmoid(qv)
    sg = _sigmoid(fl)
    f = lb + (1.0 - lb) * sg
    return dict(sq=sq, qs=qv * sq, sg=sg, f=f, kk=1.0 - f, g=jnp.log2(f))


def _decays(sums):
    big_g = sums[:CHUNK]
    total = sums[CHUNK:CHUNK + 8]
    g_last = jnp.tile(total, (CHUNK // 8, 1))
    g_mid = jnp.tile(sums[CHUNK + 8:], (CHUNK // 8, 1))
    return dict(
        e_q=jnp.exp2(big_g),
        e_k=jnp.exp2(g_last - big_g),
        e_qm=jnp.exp2(jnp.minimum(big_g - g_mid, EXP_CAP)),
        e_km=jnp.exp2(jnp.minimum(g_mid - big_g, EXP_CAP)),
        total8=jnp.exp2(total))


def _group_rows(gi):
    return [pl.ds(pl.multiple_of((gi * NB + j) * CHUNK, CHUNK), CHUNK) for j in range(NB)]


def _lower_bound(lb_ref):
    return _sigmoid(lb_ref[0:1, :] - lb_ref[1:2, :])


def _hgrn_fwd(proj, lb_logits, rec_g, y_in):
    def body(q_ref, f_ref, i_ref, gate_ref, lb_ref, rg_ref, y_any, y_ref, o_ref, st_ref):
        del y_any
        lb = _lower_bound(lb_ref)
        causal = _tri(True)
        smat = _sum_rows_matrix()

        def group(gi, st):
            rows = _group_rows(gi)
            ts = [_gates(q_ref[r, :], f_ref[r, :], lb) for r in rows]
            ds = [_decays(s) for s in _exact_sums(smat, [t["g"] for t in ts])]
            vs = [_bf(i_ref[r, :]) for r in rows]
            q_m = [_bf(t["qs"] * d["e_qm"]) for t, d in zip(ts, ds)]
            k_m = [_bf(t["kk"] * d["e_km"]) for t, d in zip(ts, ds)]
            q_e = [_bf(t["qs"] * d["e_q"]) for t, d in zip(ts, ds)]
            k_e = [_bf(t["kk"] * d["e_k"]) for t, d in zip(ts, ds)]
            a = [_bf(jnp.where(causal, _dot(q_m[j], k_m[j], 1, 1), 0.0)) for j in range(NB)]
            intra = [_dot(a[j], vs[j], 1, 0) for j in range(NB)]
            upd = [_dot(vs[j], k_e[j], 0, 0) for j in range(NB)]
            for j in range(NB):
                st_ref[gi * NB + j] = st
                o_ref[rows[j], :] = intra[j] + _dot(q_e[j], _bf(st), 1, 1)
                st = st * jnp.tile(ds[j]["total8"], (HEAD // 8, 1)) + upd[j]
            return st

        lax.fori_loop(0, NGRP, group, jnp.zeros((HEAD, HEAD), jnp.float32))
        o = o_ref[...]
        rn = o * lax.rsqrt(jnp.mean(o * o, axis=-1, keepdims=True) + EPS)
        gate = gate_ref[...]
        y_ref[...] = _bf(rn * rg_ref[...] * (gate * _sigmoid(gate)))

    return pl.pallas_call(
        body, name="hgrn_fwd", grid=(NHEAD,),
        in_specs=[*_HEAD_SPECS,
                  pl.BlockSpec((2, HEAD), lambda h: (0, h)),
                  pl.BlockSpec((1, HEAD), lambda h: (0, h)),
                  pl.BlockSpec(memory_space=pl.ANY)],
        out_specs=(pl.BlockSpec((T, HEAD), lambda h: (0, NHEAD + h)),
                   pl.BlockSpec((T, HEAD), lambda h: (0, h)),
                   pl.BlockSpec((None, NCHUNK, HEAD, HEAD), lambda h: (h, 0, 0, 0))),
        out_shape=(pltpu.HBM((T, DMIX), jnp.bfloat16), pltpu.HBM((T, D), jnp.float32),
                   pltpu.HBM((NHEAD, NCHUNK, HEAD, HEAD), jnp.float32)),
        input_output_aliases={6: 0},
        compiler_params=_params(("parallel",)),
    )(proj, proj, proj, proj, lb_logits, rec_g, y_in)


def _out_proj_loss(x, y, w_out, target, gf):
    rows = 512
    parts = [slice(k * rows // 2, (k + 1) * rows // 2) for k in range(2)]

    def body(x_ref, y_ref, w_ref, t_ref, g_ref, dz_ref, dzb_ref, sq_ref, dg_ref):
        zs = [x_ref[p, :] + _dot(y_ref[p, :], w_ref[...], 1, 0) for p in parts]
        sq = dg = 0.0
        for p, z in zip(parts, zs):
            r = lax.rsqrt(jnp.mean(z * z, axis=-1, keepdims=True) + EPS)
            zhat = z * r
            err = zhat * g_ref[...] - t_ref[p, :]
            dy = err * (1.0 / D)
            gdy = dy * g_ref[...]
            dz = r * (gdy - zhat * jnp.mean(zhat * gdy, axis=-1, keepdims=True))
            dz_ref[p, :] = dz
            dzb_ref[p, :] = _bf(dz)
            sq = sq + jnp.sum(err * err, axis=0, keepdims=True)
            dg = dg + jnp.sum(zhat * dy, axis=0, keepdims=True)

        @pl.when(pl.program_id(0) == 0)
        def _():
            sq_ref[...] = sq
            dg_ref[...] = dg

        @pl.when(pl.program_id(0) != 0)
        def _():
            sq_ref[...] += sq
            dg_ref[...] += dg

    tile = pl.BlockSpec((rows, D), lambda i: (i, 0))
    vec = pl.BlockSpec((1, D), lambda i: (0, 0))
    return pl.pallas_call(
        body, name="out_proj_loss", grid=(T // rows,),
        in_specs=[tile, pl.BlockSpec((rows, DMIX), lambda i: (i, 0)), pl.BlockSpec((DMIX, D), lambda i: (0, 0)),
                  tile, vec],
        out_specs=(tile, tile, vec, vec),
        out_shape=(pltpu.HBM((T, D), jnp.float32), pltpu.HBM((T, D), jnp.bfloat16),
                   jax.ShapeDtypeStruct((1, D), jnp.float32), jax.ShapeDtypeStruct((1, D), jnp.float32)),
        compiler_params=_params(("arbitrary",)),
    )(x, y, w_out, target, gf)


def _out_proj_bwd(dzb, w_out, y):
    tn = 512

    def body(dz_ref, w_ref, y_ref, dy_ref, gw_ref, gwb_ref):
        dz = dz_ref[...]
        dy_ref[...] = _dot(dz, w_ref[...], 1, 1)
        gw = _dot(y_ref[...], dz, 0, 0)
        gw_ref[...] = gw
        gwb_ref[...] = _bf(gw)

    return pl.pallas_call(
        body, name="out_proj_bwd", grid=(DMIX // tn,),
        in_specs=[pl.BlockSpec((T, D), lambda n: (0, 0)), pl.BlockSpec((tn, D), lambda n: (n, 0)),
                  pl.BlockSpec((T, tn), lambda n: (0, n))],
        out_specs=(pl.BlockSpec((T, tn), lambda n: (0, n)), pl.BlockSpec((tn, D), lambda n: (n, 0)),
                   pl.BlockSpec((tn, D), lambda n: (n, 0))),
        out_shape=(pltpu.HBM((T, DMIX), jnp.float32), pltpu.HBM((DMIX, D), jnp.float32),
                   pltpu.HBM((DMIX, D), jnp.bfloat16)),
        compiler_params=_params(("parallel",)),
    )(dzb, w_out, y)


def _hgrn_bwd(proj, lb_logits, rec_g, o, states, dymix, dproj_in, token):
    def body(q_ref, f_ref, i_ref, gate_ref, lb_ref, rg_ref, o_ref, st_ref, dy_ref, dp_any, token_any,
             dp_ref, drg_ref, dlb_ref, do_ref):
        del dp_any, token_any
        lb = _lower_bound(lb_ref)
        causal = _tri(True)
        smat, rmat = _sum_rows_matrix(), _rev_sum_matrix()

        o = o_ref[...]
        rs = lax.rsqrt(jnp.mean(o * o, axis=-1, keepdims=True) + EPS)
        rn = o * rs
        gate = gate_ref[...]
        sgate = _sigmoid(gate)
        dyv = dy_ref[...]
        d_r = dyv * (gate * sgate)
        dp_ref[3] = _bf(dyv * (rn * rg_ref[...]) * (sgate * (1.0 + gate * (1.0 - sgate))))
        drg_ref[...] = jnp.sum(d_r * rn, axis=0, keepdims=True)
        drn = d_r * rg_ref[...]
        do_ref[...] = rs * (drn - rn * jnp.mean(rn * drn, axis=-1, keepdims=True))

        def group(i, carry):
            dst, dlb = carry
            gi = NGRP - 1 - i
            rows = _group_rows(gi)
            span = range(NB)
            qvs = [q_ref[r, :] for r in rows]
            ts = [_gates(qv, f_ref[r, :], lb) for qv, r in zip(qvs, rows)]
            ds = [_decays(s) for s in _exact_sums(smat, [t["g"] for t in ts])]
            vs = [_bf(i_ref[r, :]) for r in rows]
            dos = [_bf(do_ref[r, :]) for r in rows]
            sts = [st_ref[gi * NB + j] for j in span]
            qe_f = [t["qs"] * d["e_q"] for t, d in zip(ts, ds)]
            ke_f = [t["kk"] * d["e_k"] for t, d in zip(ts, ds)]
            q_e, k_e = [_bf(a) for a in qe_f], [_bf(a) for a in ke_f]
            q_m = [_bf(t["qs"] * d["e_qm"]) for t, d in zip(ts, ds)]
            k_m = [_bf(t["kk"] * d["e_km"]) for t, d in zip(ts, ds)]
            a = [_bf(jnp.where(causal, _dot(q_m[j], k_m[j], 1, 1), 0.0)) for j in span]
            da = [_bf(jnp.where(causal, _dot(dos[j], vs[j], 1, 1), 0.0)) for j in span]
            dqm = [_dot(da[j], k_m[j], 1, 0) for j in span]
            dkm = [_dot(da[j], q_m[j], 0, 0) for j in span]
            dv_in = [_dot(a[j], dos[j], 0, 0) for j in span]
            dqe = [_dot(dos[j], _bf(sts[j]), 1, 0) for j in span]
            grow = [_dot(dos[j], q_e[j], 0, 0) for j in span]
            dke, carried = [None] * NB, [None] * NB
            for j in reversed(span):
                dst_b = _bf(dst)
                dke[j] = _dot(vs[j], dst_b, 1, 0)
                dp_ref[2, rows[j], :] = _bf(dv_in[j] + _dot(k_e[j], dst_b, 1, 1))
                carried[j] = ds[j]["total8"] * jnp.sum(dst * sts[j], axis=0, keepdims=True)
                dst = dst * jnp.tile(ds[j]["total8"], (HEAD // 8, 1)) + grow[j]
            kdk = [ke_f[j] * dke[j] for j in span]
            pos = [(q_m[j].astype(jnp.float32) * dqm[j] - k_m[j].astype(jnp.float32) * dkm[j]) + qe_f[j] * dqe[j]
                   for j in span]
            dgs = _exact_sums(rmat, [jnp.concatenate([pos[j], kdk[j]], axis=0) for j in span])
            for j in span:
                t, d = ts[j], ds[j]
                dg = dgs[j] + jnp.tile(carried[j], (CHUNK // 8, 1))
                dqs = dqm[j] * d["e_qm"] + dqe[j] * d["e_q"]
                dkk = dkm[j] * d["e_km"] + dke[j] * d["e_k"]
                df = dg / t["f"] - dkk
                dp_ref[1, rows[j], :] = _bf(df * (1.0 - lb) * (t["sg"] * (1.0 - t["sg"])))
                dp_ref[0, rows[j], :] = _bf(dqs * (t["sq"] * (1.0 + qvs[j] * (1.0 - t["sq"]))))
                dlb = dlb + df * (1.0 - t["sg"])
            return dst, dlb

        _, dlb = lax.fori_loop(0, NGRP, group, (jnp.zeros((HEAD, HEAD), jnp.float32),
                                                jnp.zeros((CHUNK, HEAD), jnp.float32)))
        dlb_ref[...] = jnp.sum(dlb, axis=0, keepdims=True)

    vec = pl.BlockSpec((1, HEAD), lambda h: (0, h))
    return pl.pallas_call(
        body, name="hgrn_bwd", grid=(NHEAD,),
        in_specs=[*_HEAD_SPECS,
                  pl.BlockSpec((2, HEAD), lambda h: (0, h)), vec,
                  pl.BlockSpec((T, HEAD), lambda h: (0, h)),
                  pl.BlockSpec((None, NCHUNK, HEAD, HEAD), lambda h: (h, 0, 0, 0)),
                  pl.BlockSpec((T, HEAD), lambda h: (0, NHEAD + h)), ANY, ANY],
        out_specs=(pl.BlockSpec((4, T, HEAD), lambda h: (0, 0, h)), vec, vec),
        out_shape=(pltpu.HBM((NSEG, T, D), jnp.bfloat16),
                   jax.ShapeDtypeStruct((1, D), jnp.float32), jax.ShapeDtypeStruct((1, D), jnp.float32)),
        scratch_shapes=[pltpu.VMEM((T, HEAD), jnp.float32)],
        input_output_aliases={9: 0},
        compiler_params=_params(("parallel",)),
    )(proj, proj, proj, proj, lb_logits, rec_g, o, states, dymix, dproj_in, token)


def _pool_bwd(proj, pool_w, pool_scale, dymix):
    def body(u_ref, pg_ref, w_ref, sc_ref, dy_ref, dp_ref, gw_ref, gs_ref):
        gidx = pl.program_id(0)
        u, pg = u_ref[...], pg_ref[...]
        d = _bf(_window_mean(_window_sum(u, gidx, False), gidx) - u)
        mixed = _dot(d, w_ref[...], 1, 0)
        spg = _sigmoid(pg)
        dyv = dy_ref[...]
        d_p = dyv * (pg * spg)
        dp_ref[1] = _bf(dyv * (mixed * sc_ref[...]) * (spg * (1.0 + pg * (1.0 - spg))))
        gs_ref[...] = jnp.sum(d_p * mixed, axis=0, keepdims=True)
        dmixed = _bf(d_p * sc_ref[...])
        gw_ref[...] = _dot(d, dmixed, 0, 0)
        dd = _dot(dmixed, w_ref[...], 1, 1)
        dp_ref[0] = _bf(_window_sum(_window_mean(dd, gidx), gidx, True) - dd)

    return pl.pallas_call(
        body, name="pool_bwd", grid=(NGROUP,),
        in_specs=[*_POOL_SPECS,
                  pl.BlockSpec((None, GROUP, GROUP), lambda g: (g, 0, 0)),
                  pl.BlockSpec((1, GROUP), lambda g: (0, g)),
                  pl.BlockSpec((T, GROUP), lambda g: (0, g))],
        out_specs=(pl.BlockSpec((2, T, GROUP), lambda g: (2, 0, g)),
                   pl.BlockSpec((None, GROUP, GROUP), lambda g: (g, 0, 0)),
                   pl.BlockSpec((1, GROUP), lambda g: (0, g))),
        out_shape=(pltpu.HBM((NSEG, T, D), jnp.bfloat16),
                   jax.ShapeDtypeStruct((NGROUP, GROUP, GROUP), jnp.float32),
                   jax.ShapeDtypeStruct((1, D), jnp.float32)),
        compiler_params=_params(("parallel",)),
    )(proj, proj, pool_w, pool_scale, dymix)


HALF = NTILE // 2
AWAY = HALF - 3


def _dproj_tile(chip, side, p):
    j = 6 * chip + 3 * side + p
    return ((j // 4 + 4) % NSEG, 0, j % 4)


def _sibling_copy(sib_out, sib_in, send_sems, recv_sems, slot):
    x, y, c, _ = _place()
    return pltpu.make_async_remote_copy(
        src_ref=sib_out.at[slot], dst_ref=sib_in.at[slot], send_sem=send_sems.at[slot],
        recv_sem=recv_sems.at[slot], device_id=(x, y, 1 - c), device_id_type=MESH)


def _proj_bwd_w_far(place, ht, dproj):
    def body(place_ref, h_ref, dp_ref, sib_out, sib_in, send_sems, recv_sems, stage, loc_sems):
        del place_ref
        i = pl.program_id(0)

        def to_hbm(k):
            return pltpu.make_async_copy(stage.at[k], sib_out.at[k], loc_sems.at[k])

        def send(k):
            to_hbm(k).wait()
            _sibling_copy(sib_out, sib_in, send_sems, recv_sems, k).start()

        stage[i] = _bf(_dot(h_ref[...], dp_ref[...], 1, 0))

        @pl.when(i > 0)
        def _():
            send(i - 1)

        to_hbm(i).start()

        @pl.when(i == HALF - 1)
        def _():
            send(i)

    buf = pltpu.HBM((HALF, D, TILE), jnp.bfloat16)
    sems = pltpu.SemaphoreType.DMA((HALF,))
    return pl.pallas_call(
        body, name="proj_bwd_w_far",
        grid_spec=pltpu.PrefetchScalarGridSpec(
            num_scalar_prefetch=1, grid=(HALF,),
            in_specs=[pl.BlockSpec((D, T), lambda i, pr: (0, 0)),
                      pl.BlockSpec((None, T, TILE), lambda i, pr: _dproj_tile(i // 3, 1 - pr[2], i % 3))],
            out_specs=(HBM, HBM, SEM, SEM),
            scratch_shapes=[pltpu.VMEM((HALF, D, TILE), jnp.bfloat16), pltpu.SemaphoreType.DMA((HALF,))]),
        out_shape=(buf, buf, sems, sems),
        compiler_params=pltpu.CompilerParams(dimension_semantics=("arbitrary",), vmem_limit_bytes=48 * MIB,
                                             has_side_effects=EFFECT),
    )(place, ht, dproj)


def _proj_bwd_w_near(place, ht, dproj, sib_out, sib_in, sib_send, sib_recv):
    def owner_chip(k, pr):
        return jnp.where(k < AWAY, (pr[1] + 1 + k % 3) % 4, pr[1])

    def tile_p(k):
        return jnp.where(k < AWAY, k // 3, k - AWAY)

    def body(place_ref, h_ref, dp_ref, sib_out, sib_in, sib_send, sib_recv, sums, own_ref, landing, out_send,
             out_recv, recvbuf, outbuf, in_sems, loc_sems):
        i = pl.program_id(0)
        px, py, c, _ = _place()

        def slot_of(k):
            return 3 * owner_chip(k, place_ref) + tile_p(k)

        def load(k):
            return pltpu.make_async_copy(sib_in.at[slot_of(k)], recvbuf.at[k % 2], in_sems.at[k % 2])

        def fetch(k):
            _sibling_copy(sib_out, sib_in, sib_send, sib_recv, slot_of(k)).wait_recv()
            load(k).start()

        def route(k):
            chip = owner_chip(k, place_ref)
            cx, cy = chip // 2, chip % 2
            return cx, cy, (cx ^ px) + 2 * (cy ^ py) - 1, tile_p(k)

        def to_hbm(k):
            _, _, rel, p = route(k)
            return pltpu.make_async_copy(outbuf.at[k], sums.at[rel, p], loc_sems.at[k])

        def to_owner(k):
            cx, cy, rel, p = route(k)
            return pltpu.make_async_remote_copy(
                src_ref=sums.at[rel, p], dst_ref=landing.at[rel, p], send_sem=out_send.at[3 * rel + p],
                recv_sem=out_recv.at[3 * rel + p], device_id=(cx, cy, c), device_id_type=MESH)

        @pl.when(i == 0)
        def _():
            fetch(i)

        @pl.when(i < HALF - 1)
        def _():
            fetch(i + 1)

        load(i).wait()
        total = _dot(h_ref[...], dp_ref[...], 1, 0) + recvbuf[i % 2].astype(jnp.float32)
        own_ref[...] = total
        outbuf[jnp.minimum(i, AWAY)] = _bf(total)

        @pl.when(i < AWAY)
        def _():
            to_hbm(i).start()

        @pl.when(jnp.logical_and(i > 0, i <= AWAY))
        def _():
            to_hbm(i - 1).wait()
            to_owner(i - 1).start()

        @pl.when(i == HALF - 1)
        def _():
            for slot in range(HALF):
                _sibling_copy(sib_out, sib_in, sib_send, sib_recv, slot).wait_send()

    travelling = pltpu.HBM((3, 3, D, TILE), jnp.bfloat16)
    sems = pltpu.SemaphoreType.DMA((AWAY,))
    return pl.pallas_call(
        body, name="proj_bwd_w_near",
        grid_spec=pltpu.PrefetchScalarGridSpec(
            num_scalar_prefetch=1, grid=(HALF,),
            in_specs=[pl.BlockSpec((D, T), lambda i, pr: (0, 0)),
                      pl.BlockSpec((None, T, TILE), lambda i, pr: _dproj_tile(owner_chip(i, pr), pr[2], tile_p(i))),
                      HBM, HBM, SEM, SEM],
            out_specs=(HBM, pl.BlockSpec((None, D, TILE), lambda i, pr: (jnp.where(i < AWAY, 0, i % 3), 0, 0)),
                       HBM, SEM, SEM),
            scratch_shapes=[pltpu.VMEM((2, D, TILE), jnp.bfloat16), pltpu.VMEM((AWAY + 1, D, TILE), jnp.bfloat16),
                            pltpu.SemaphoreType.DMA((2,)), pltpu.SemaphoreType.DMA((AWAY,))]),
        out_shape=(travelling, pltpu.HBM((3, D, TILE), jnp.float32), travelling, sems, sems),
        compiler_params=pltpu.CompilerParams(dimension_semantics=("arbitrary",), vmem_limit_bytes=48 * MIB,
                                             has_side_effects=EFFECT),
    )(place, ht, dproj, sib_out, sib_in, sib_send, sib_recv)


def _proj_bwd_x(dproj, w_t, x, g1, dz, token):
    tm = 512
    pairs = NSEG // 2

    def body(dp_ref, w_ref, x_ref, g_ref, dz_ref, token_any, dx_ref, dg_ref, wcat, acc):
        del token_any
        m, s = pl.program_id(0), pl.program_id(1)

        @pl.when(m == 0)
        def _():
            for i in range(8):
                wcat[s, :, i * TILE:(i + 1) * TILE] = w_ref[i]

        @pl.when(s == 0)
        def _():
            acc[...] = jnp.zeros((tm, D), jnp.float32)

        acc[...] += _dot(jnp.concatenate([dp_ref[0], dp_ref[1]], axis=1), wcat[s], 1, 1)

        @pl.when(s == pairs - 1)
        def _():
            xv = x_ref[...]
            rs = lax.rsqrt(jnp.mean(xv * xv, axis=-1, keepdims=True) + EPS)
            xhat = xv * rs
            dhv = acc[...]
            gdh = dhv * g_ref[...]
            dx_ref[...] = dz_ref[...] + rs * (gdh - xhat * jnp.mean(xhat * gdh, axis=-1, keepdims=True))
            dg = jnp.sum(xhat * dhv, axis=0, keepdims=True)

            @pl.when(m == 0)
            def _():
                dg_ref[...] = dg

            @pl.when(m != 0)
            def _():
                dg_ref[...] += dg

    rows = pl.BlockSpec((tm, D), lambda m, s: (m, 0))
    vec = pl.BlockSpec((1, D), lambda m, s: (0, 0))
    return pl.pallas_call(
        body, name="proj_bwd_x", grid=(T // tm, pairs),
        in_specs=[pl.BlockSpec((2, tm, D), lambda m, s: (s, m, 0)),
                  pl.BlockSpec((8, D, TILE), lambda m, s: ((jnp.where(m == 0, s, pairs - 1) + 1) % pairs, 0, 0)),
                  rows, vec, rows, ANY],
        out_specs=(rows, vec),
        out_shape=(jax.ShapeDtypeStruct((T, D), jnp.float32), jax.ShapeDtypeStruct((1, D), jnp.float32)),
        scratch_shapes=[pltpu.VMEM((pairs, D, 2 * D), jnp.bfloat16), pltpu.VMEM((tm, D), jnp.float32)],
        compiler_params=_params(("arbitrary", "arbitrary"), vmem_mib=56),
    )(dproj, w_t, x, g1, dz, token)


def _adamw(w, g, m, v):
    m_new = ADAM_B1 * m + (1.0 - ADAM_B1) * g
    v_new = ADAM_B2 * v + (1.0 - ADAM_B2) * (g * g)
    delta = -ADAM_LR * ((m_new / BC1) / (jnp.sqrt(v_new / BC2) + ADAM_EPS) + ADAM_WD * w)
    return delta, m_new, v_new


def _reduce_adam(name, place, parts, w, m, v, grid, w_spec):
    n = len(parts)

    def body(place_ref, *refs):
        del place_ref
        w_ref, m_ref, v_ref, g_ref, d_ref, mo_ref, vo_ref = refs[n:]
        g = None
        for ref, (_, _, stacked) in zip(refs[:n], parts):
            terms = [ref[r] for r in range(ref.shape[0])] if stacked else [ref[...]]
            for t in terms:
                if t.shape[-1] != w_ref.shape[-1]:
                    t = jnp.concatenate([t[p] for p in range(t.shape[0])], axis=1)
                g = t.astype(jnp.float32) if g is None else g + t.astype(jnp.float32)
        delta, m_new, v_new = _adamw(w_ref[...], g, m_ref[...], v_ref[...])
        g_ref[...] = g
        d_ref[...] = delta
        mo_ref[...] = m_new
        vo_ref[...] = v_new

    shape = jax.ShapeDtypeStruct(w.shape, jnp.float32)
    return pl.pallas_call(
        body, name=name,
        grid_spec=pltpu.PrefetchScalarGridSpec(
            num_scalar_prefetch=1, grid=grid,
            in_specs=[spec for _, spec, _ in parts] + [w_spec] * 3, out_specs=(w_spec,) * 4),
        out_shape=(shape,) * 4,
        compiler_params=_params(("parallel",)),
    )(place, *[_in_hbm(a) for a in [a for a, _, _ in parts] + [w, m, v]])


SMALL_ROWS = (1, 1, 2, 1, 1)


def _small_adam(place, own, parts, ws, ms, vs):
    n = len(SMALL_ROWS)

    def body(place_ref, own_ref, p_ref, *refs):
        ins, outs, bufs = refs[:3 * n], refs[3 * n:3 * n + 1 + 4 * n], refs[3 * n + 1 + 4 * n:]

        def stacked(group, buf):
            r0 = 0
            for ref, k in zip(group, SMALL_ROWS):
                buf[r0:r0 + k, :] = ref[...]
                r0 += k
            buf[r0:, :] = jnp.zeros((8 - r0, D), jnp.float32)
            return buf[...]

        wv, mv, vv = (stacked(ins[n * j:n * j + n], bufs[j]) for j in range(3))
        me = place_ref[0]
        g = None
        for s in range(NDEV):
            term = jnp.where(me == s, own_ref[...], p_ref[s])
            g = term if g is None else g + term
        rows = _row_ids(wv.shape)
        other = jnp.where(rows == 2, pltpu.roll(wv, 7, 0), jnp.where(rows == 3, pltpu.roll(wv, 1, 0), 0.0))
        lbv = _sigmoid(wv - other)
        sign = jnp.where(rows == 2, 1.0, -1.0)
        g = jnp.where((rows == 2) | (rows == 3), sign * g * lbv * (1.0 - lbv), g)
        delta, m_new, v_new = _adamw(wv, g, mv, vv)
        outs[0][...] = jnp.sum(g[6:7], axis=1, keepdims=True) * (0.5 / D)
        for j, val in enumerate((g, delta, m_new, v_new)):
            r0 = 0
            for ref, k in zip(outs[1 + n * j:1 + n * j + n], SMALL_ROWS):
                ref[...] = val[r0:r0 + k]
                r0 += k

    vmem = pl.BlockSpec(memory_space=pltpu.VMEM)
    shapes = [jax.ShapeDtypeStruct((k, D), jnp.float32) for k in SMALL_ROWS]
    out = pl.pallas_call(
        body, name="small_adam", out_shape=(jax.ShapeDtypeStruct((1, 1), jnp.float32), *shapes * 4),
        in_specs=[pl.BlockSpec(memory_space=pltpu.SMEM)] + [vmem] * (2 + 3 * n), out_specs=(vmem,) * (1 + 4 * n),
        scratch_shapes=[pltpu.VMEM((8, D), jnp.float32)] * 3,
    )(place, own, parts, *ws, *ms, *vs)
    return out[0], [out[1 + n * j:1 + n * j + n] for j in range(4)]


def _rows8(*vecs):
    rows = [a.reshape(-1, D) for a in vecs]
    n = sum(r.shape[0] for r in rows)
    return jnp.concatenate(rows + [jnp.zeros((8 - n, D), jnp.float32)], axis=0)


def kernel(x, norm1_g, w_in, pool_w, pool_scale, lb_logits, rec_norm_g, w_out, final_norm_g, loss_target, m_norm1_g, m_w_in, m_pool_w, m_pool_scale, m_lb_logits, m_rec_norm_g, m_w_out, m_final_norm_g, v_norm1_g, v_w_in, v_pool_w, v_pool_scale, v_lb_logits, v_rec_norm_g, v_w_out, v_final_norm_g):
    xs = x[0]
    target = loss_target[0]
    ix, iy, ic = lax.axis_index("x"), lax.axis_index("y"), lax.axis_index("c")
    place = jnp.stack([4 * ix + 2 * iy + ic, 2 * ix + iy, ic]).astype(jnp.int32)
    gf = final_norm_g.reshape(1, D)

    ht, w_t, w_out_b, w_out_g, pool_g, proj = _gather_proj(xs, norm1_g, w_in, w_out, pool_w)
    pool_full = pool_g.transpose(1, 0, 2, 3).reshape(NGROUP, GROUP, GROUP)
    wout = [w_out_b, w_out_g]
    wout_send, wout_recv, wout, wout_token = _split_start("gather_wout_start", wout, NDEV - 1, _plan_wout)

    y = _pool_fwd(proj, pool_full, pool_scale, wout_token)
    y, o, states = _hgrn_fwd(proj, lb_logits, rec_norm_g, y)
    _, w_out_g = _split_wait("gather_wout_wait", wout, wout_send, wout_recv, _plan_wout, o)
    w_out_full = _in_hbm(w_out_g.reshape(DMIX, D))
    dz, dzb, sq, dgf = _out_proj_loss(xs, y, w_out_full, target, gf)

    dymix, gwout_f, gwout_b = _out_proj_bwd(dzb, w_out_full, y)
    dproj, gpool, dscale = _pool_bwd(proj, pool_full, pool_scale, dymix)

    blk_out = (NDEV, DMIX // NDEV, D)
    blk_pool = (NDEV, NGROUP, GROUP // NDEV, GROUP)
    gpool_s = gpool.reshape(NGROUP, NDEV, GROUP // NDEV, GROUP).transpose(1, 0, 2, 3)
    rest = [gwout_b.reshape(blk_out), gpool_s,
            lax.empty((NDEV - 1,) + blk_out[1:], jnp.bfloat16), lax.empty((NDEV - 1,) + blk_pool[1:], jnp.float32)]
    rest_send, rest_recv, rest, rest_token = _split_start("scatter_rest_start", rest, 2 * (NDEV - 1), _plan_rest)

    dproj, drecg, dlb = _hgrn_bwd(proj, lb_logits, rec_norm_g, o, states, dymix, dproj, rest_token)
    chip_sums, own_sum, landing, win_send, win_recv = _proj_bwd_w_near(
        place, ht, dproj, *_proj_bwd_w_far(place, ht, dproj))
    win = [chip_sums, landing]

    grad_x, dg1 = _proj_bwd_x(dproj, w_t, xs, norm1_g, dz, chip_sums)

    small = [_rows8(dg1, dscale, dlb, dlb, drecg, dgf, sq), lax.empty((NDEV, 8, D), jnp.float32)]
    small_send, small_recv, small, small_token = _split_start("gather_small_start", small, NDEV - 1, _plan_small)

    _, gpool_own, r_out, r_pool = _split_wait("scatter_rest_wait", rest, rest_send, rest_recv, _plan_rest,
                                              small_token)
    g_wout, d_wout, m_wout, v_wout = _reduce_adam(
        "adam_w_out", place,
        [(gwout_f.reshape(blk_out), pl.BlockSpec((None,) + blk_out[1:], lambda i, pr: (pr[0], 0, 0)), False),
         (r_out, pl.BlockSpec((NDEV - 1,) + blk_out[1:], lambda i, pr: (0, 0, 0)), True)],
        w_out, m_w_out, v_w_out, (1,), pl.BlockSpec((None,) + blk_out[1:], lambda i, pr: (0, 0, 0)))
    g_pool, d_pool, m_pool, v_pool = _reduce_adam(
        "adam_pool_w", place,
        [(gpool_own, pl.BlockSpec((None,) + blk_pool[1:], lambda i, pr: (pr[0], 0, 0, 0)), False),
         (r_pool, pl.BlockSpec((NDEV - 1,) + blk_pool[1:], lambda i, pr: (0, 0, 0, 0)), True)],
        pool_w, m_pool_w, v_pool_w, (1,), pl.BlockSpec((None,) + blk_pool[1:], lambda i, pr: (0, 0, 0, 0)))

    _, r_in = _split_wait("scatter_win_wait", win, win_send, win_recv, _plan_in, d_pool)
    g_win, d_win, m_win, v_win = _reduce_adam(
        "adam_w_in", place,
        [(own_sum, pl.BlockSpec((3, D // 8, TILE), lambda i, pr: (0, i, 0)), False),
         (r_in, pl.BlockSpec((3, 3, D // 8, TILE), lambda i, pr: (0, 0, i, 0)), True)],
        w_in, m_w_in, v_w_in, (8,), pl.BlockSpec((None, D // 8, 3 * TILE), lambda i, pr: (0, i, 0)))

    own_small, r_small = _split_wait("gather_small_wait", small, small_send, small_recv, _plan_small, d_win)
    loss, (g_s, d_s, m_s, v_s) = _small_adam(
        place, own_small, r_small,
        (norm1_g, pool_scale, lb_logits, rec_norm_g, gf),
        (m_norm1_g, m_pool_scale, m_lb_logits, m_rec_norm_g, m_final_norm_g.reshape(1, D)),
        (v_norm1_g, v_pool_scale, v_lb_logits, v_rec_norm_g, v_final_norm_g.reshape(1, D)))

    def outs(small, win, pool, wout):
        n1, ps, lbl, rg, fg = small
        return n1, win, pool, ps, lbl, rg, wout, fg.reshape(D)

    return (loss.reshape(()), grad_x[None],
            *outs(g_s, g_win, g_pool, g_wout), *outs(d_s, d_win, d_pool, d_wout),
            *outs(m_s, m_win, m_pool, m_wout), *outs(v_s, v_win, v_pool, v_wout))
```

```python
import functools

import jax
import jax.numpy as jnp
from jax import lax
from jax.experimental import pallas as pl
from jax.experimental.pallas import tpu as pltpu

T = 2048
D = 1024
NSEG = 6
NTILE = 24
TILE = 256
DMIX = 2048
NDEV = 8
HEAD = 128
NHEAD = 8
CHUNK = 64
NCHUNK = T // CHUNK
NB = 32
NGRP = NCHUNK // NB
NGROUP = 4
GROUP = 256
EPS = 1e-6
EXP_CAP = 115.0
MESH = pl.DeviceIdType.MESH
AXES = ("x", "y", "c")
ANY = pl.BlockSpec(memory_space=pl.ANY)
HBM = pl.BlockSpec(memory_space=pltpu.HBM)
SEM = pl.BlockSpec(memory_space=pltpu.SEMAPHORE)
EFFECT = pltpu.SideEffectType.DATAFLOW_SIDE_EFFECTING

ADAM_LR = 0.001
ADAM_B1 = 0.9
ADAM_B2 = 0.999
ADAM_EPS = 1e-08
ADAM_WD = 0.01
ADAM_STEP = 10
BC1 = 1.0 - ADAM_B1 ** ADAM_STEP
BC2 = 1.0 - ADAM_B2 ** ADAM_STEP

MIB = 1 << 20


def _params(sem=None, vmem_mib=48):
    return pltpu.CompilerParams(dimension_semantics=sem, vmem_limit_bytes=vmem_mib * MIB)


def _sigmoid(v):
    return 1.0 / (1.0 + jnp.exp(-v))


def _dot(a, b, ca, cb, precision=None):
    return lax.dot_general(a, b, (((ca,), (cb,)), ((), ())), precision=precision,
                           preferred_element_type=jnp.float32)


def _bf(v):
    return v.astype(jnp.bfloat16)


def _in_hbm(a):
    return pltpu.with_memory_space_constraint(a, pltpu.HBM)


def _place():
    x, y, c = lax.axis_index("x"), lax.axis_index("y"), lax.axis_index("c")
    return x, y, c, 4 * x + 2 * y + c


def _peer(x, y, c, r):
    return (x ^ ((r >> 2) & 1), y ^ ((r >> 1) & 1), c ^ (r & 1))


def _gather_proj(x, g1, w_in, w_out, pool_w):
    def body(x_ref, g_ref, win_ref, wout_ref, pool_ref, ht_o, wt_o, woutb_o, wout_o, pool_o, proj_o,
             xbuf, hv, htv, wv, wob, pb, stage, send_sems, recv_sems, loc_sems, out_sems):
        px, py, c, my_idx = _place()
        fetch_x = pltpu.make_async_copy(x_ref, xbuf, loc_sems.at[5])
        fetch_x.start()
        me, sibling = (px, py, c), (px, py, 1 - c)
        chips = [(1 - px, py), (px, 1 - py), (1 - px, 1 - py)]
        for p in range(3):
            wv[3 * my_idx + p] = _bf(win_ref[0, :, p * TILE:(p + 1) * TILE])

        def index(bx, by, bc):
            return 4 * bx + 2 * by + bc

        def slot(w, block):
            return wv.at[pl.ds(3 * index(*block), 3)] if w == 0 else pool_o.at[index(*block)]

        def copy(k, w, block, to, src=None):
            return pltpu.make_async_remote_copy(
                src_ref=slot(w, block) if src is None else src, dst_ref=slot(w, block),
                send_sem=send_sems.at[2 * k + w], recv_sem=recv_sems.at[2 * k + w],
                device_id=to, device_id_type=MESH)

        def save(block):
            at = pl.ds(3 * index(*block), 3)
            pltpu.make_async_copy(wv.at[at], wt_o.at[at], loc_sems.at[4]).start()

        srcs = (slot(0, me), pb)
        first = []
        for w in (0, 1):
            if w == 1:
                pb[...] = _bf(pool_ref[0])
                wob[...] = _bf(wout_ref[0])
            group = [copy(1 + j, w, me, (*chip, c), src=srcs[w]) for j, chip in enumerate(chips[:2])]
            group.append(copy(0, w, me, sibling, src=srcs[w]))
            for cp in group:
                cp.start()
            first += group
        save(me)
        locs = [pltpu.make_async_copy(pb, slot(1, me), loc_sems.at[0]),
                pltpu.make_async_copy(wob, wout_o.at[my_idx], loc_sems.at[1]),
                pltpu.make_async_copy(wob, woutb_o, loc_sems.at[2])]
        for cp in locs:
            cp.start()

        fetch_x.wait()
        xv = xbuf[...]
        hv[...] = _bf(xv * lax.rsqrt(jnp.mean(xv * xv, axis=-1, keepdims=True) + EPS) * g_ref[...])
        rows = 256
        for r0 in range(0, T, rows):
            htv[:, r0:r0 + rows] = hv[r0:r0 + rows, :].T
        locs.append(pltpu.make_async_copy(htv, ht_o, loc_sems.at[3]))
        locs[-1].start()

        def out_copy(p, j):
            return pltpu.make_async_copy(stage.at[p], proj_o.at[j], out_sems.at[p])

        def project(nth, block):
            base = 3 * index(*block)

            def tile(p, carry):
                if nth > 0:
                    out_copy(p, base + p).wait()
                stage[p] = _dot(hv[...], wv[base + p], 1, 0)
                out_copy(p, base + p).start()
                return carry

            lax.fori_loop(0, 3, tile, 0)

        project(0, me)
        copy(0, 0, sibling, me).wait_recv()
        save(sibling)
        project(1, sibling)
        passed = []
        relay_from = (px ^ (1 - c), py ^ c, c)
        relay_to = (px ^ c, py ^ (1 - c), c)

        def arrived(w, j):
            copy(1 + j, w, (*chips[j], c), me).wait_recv()
            passed.append(copy(4 + j, w, (*chips[j], c), sibling))
            passed[-1].start()

        def relay(w):
            passed.append(copy(3, w, relay_from, relay_to))
            passed[-1].start()

        def handed(nth, j):
            copy(4 + j, 0, (*chips[j], 1 - c), me).wait_recv()
            save((*chips[j], 1 - c))
            project(nth, (*chips[j], 1 - c))

        arrived(0, 0)
        arrived(0, 1)
        relay(0)
        for j in range(2):
            save((*chips[j], c))
            project(2 + j, (*chips[j], c))
        handed(4, 0)
        handed(5, 1)
        arrived(1, 0)
        arrived(1, 1)
        relay(1)
        arrived(0, 2)
        save((*chips[2], c))
        project(6, (*chips[2], c))
        handed(7, 2)
        arrived(1, 2)
        copy(0, 1, sibling, me).wait_recv()
        for j, chip in enumerate(chips):
            copy(4 + j, 1, (*chip, 1 - c), me).wait_recv()
        keep = pltpu.make_async_copy(wv, wt_o, loc_sems.at[4])
        for p in range(3):
            out_copy(p, p).wait()
        for cp in first + passed:
            cp.wait_send()
        keep.wait()
        for cp in locs:
            cp.wait()

    vmem = pl.BlockSpec(memory_space=pltpu.VMEM)
    bf16 = jnp.bfloat16
    return pl.pallas_call(
        body, name="gather_proj",
        out_shape=(pltpu.HBM((D, T), bf16), pltpu.HBM((NTILE, D, TILE), bf16),
                   pltpu.HBM((DMIX // NDEV, D), bf16), pltpu.HBM((NDEV, DMIX // NDEV, D), bf16),
                   pltpu.HBM((NDEV, NGROUP, GROUP // NDEV, GROUP), bf16), pltpu.HBM((NTILE, T, TILE), jnp.float32)),
        in_specs=[ANY] + [vmem] * 4, out_specs=(ANY,) * 6,
        scratch_shapes=[pltpu.VMEM((T, D), jnp.float32),
                        pltpu.VMEM((T, D), bf16), pltpu.VMEM((D, T), bf16), pltpu.VMEM((NTILE, D, TILE), bf16),
                        pltpu.VMEM((DMIX // NDEV, D), bf16), pltpu.VMEM((NGROUP, GROUP // NDEV, GROUP), bf16),
                        pltpu.VMEM((3, T, TILE), jnp.float32),
                        pltpu.SemaphoreType.DMA((14,)), pltpu.SemaphoreType.DMA((14,)),
                        pltpu.SemaphoreType.DMA((6,)), pltpu.SemaphoreType.DMA((3,))],
        compiler_params=_params(vmem_mib=56),
    )(x, g1, w_in, w_out, pool_w)


def _split_start(name, arrays, n_copies, plan):
    k = len(arrays)

    def body(*refs):
        send_sems, recv_sems, token = refs[k], refs[k + 1], refs[-1]
        for i, (src, dst, to) in enumerate(plan(refs[:k])):
            pltpu.make_async_remote_copy(src_ref=src, dst_ref=dst, send_sem=send_sems.at[i],
                                         recv_sem=recv_sems.at[i], device_id=to, device_id_type=MESH).start()
        token[...] = jnp.zeros_like(token)

    out = pl.pallas_call(
        body, name=name,
        out_shape=(pltpu.SemaphoreType.DMA((n_copies,)), pltpu.SemaphoreType.DMA((n_copies,)),
                   *[pltpu.HBM(a.shape, a.dtype) for a in arrays], jax.ShapeDtypeStruct((8, 128), jnp.float32)),
        in_specs=[HBM] * k, out_specs=(SEM, SEM, *[HBM] * k, pl.BlockSpec(memory_space=pltpu.VMEM)),
        input_output_aliases={i: 2 + i for i in range(k)},
        compiler_params=pltpu.CompilerParams(has_side_effects=EFFECT),
    )(*[pltpu.with_memory_space_constraint(a, pltpu.HBM) for a in arrays])
    return out[0], out[1], out[2:2 + k], out[-1]


def _split_wait(name, arrays, send_sems, recv_sems, plan, after):
    k = len(arrays)

    def body(*refs):
        sends, recvs = refs[k], refs[k + 1]
        for i, (src, dst, to) in enumerate(plan(refs[:k])):
            cp = pltpu.make_async_remote_copy(src_ref=src, dst_ref=dst, send_sem=sends.at[i], recv_sem=recvs.at[i],
                                              device_id=to, device_id_type=MESH)
            cp.wait_send()
            cp.wait_recv()

    return pl.pallas_call(
        body, name=name,
        out_shape=tuple(pltpu.HBM(a.shape, a.dtype) for a in arrays),
        in_specs=[HBM] * k + [SEM, SEM, ANY], out_specs=(HBM,) * k,
        input_output_aliases={i: i for i in range(k)},
        compiler_params=pltpu.CompilerParams(has_side_effects=EFFECT),
    )(*arrays, send_sems, recv_sems, after)


def _plan_wout(refs):
    src, land = refs
    x, y, c, me = _place()
    return [(src, land.at[me], _peer(x, y, c, r)) for r in range(1, NDEV)]


def _plan_rest(refs):
    gob, gpf, r_out, r_pool = refs
    x, y, c, me = _place()
    plan = []
    for r in range(1, NDEV):
        plan.append((gob.at[me ^ r], r_out.at[r - 1], _peer(x, y, c, r)))
        plan.append((gpf.at[me ^ r], r_pool.at[r - 1], _peer(x, y, c, r)))
    return plan


def _plan_in(refs):
    sums, landing = refs
    x, y, c, _ = _place()
    plan = []
    for rel, (dx, dy) in enumerate(((1, 0), (0, 1), (1, 1))):
        for p in range(3):
            plan.append((sums.at[rel, p], landing.at[rel, p], (x ^ dx, y ^ dy, c)))
    return plan


def _plan_small(refs):
    small, land = refs
    x, y, c, me = _place()
    return [(small, land.at[me], _peer(x, y, c, r)) for r in range(1, NDEV)]


def _small_start(pieces):
    n = len(pieces)

    def body(*refs):
        small, land, send_sems, recv_sems, token, buf, loc_sem = refs[n:]
        r0 = 0
        for ref in refs[:n]:
            buf[r0:r0 + ref.shape[0], :] = ref[...]
            r0 += ref.shape[0]
        buf[r0:, :] = jnp.zeros((8 - r0, D), jnp.float32)
        to_hbm = pltpu.make_async_copy(buf, small, loc_sem)
        to_hbm.start()
        to_hbm.wait()
        for i, (src, dst, to) in enumerate(_plan_small((small, land))):
            pltpu.make_async_remote_copy(src_ref=src, dst_ref=dst, send_sem=send_sems.at[i],
                                         recv_sem=recv_sems.at[i], device_id=to, device_id_type=MESH).start()
        token[...] = jnp.zeros_like(token)

    sems = pltpu.SemaphoreType.DMA((NDEV - 1,))
    out = pl.pallas_call(
        body, name="gather_small_start",
        out_shape=(pltpu.HBM((8, D), jnp.float32), pltpu.HBM((NDEV, 8, D), jnp.float32), sems, sems,
                   jax.ShapeDtypeStruct((8, 128), jnp.float32)),
        in_specs=[pl.BlockSpec(memory_space=pltpu.VMEM)] * n,
        out_specs=(HBM, HBM, SEM, SEM, pl.BlockSpec(memory_space=pltpu.VMEM)),
        scratch_shapes=[pltpu.VMEM((8, D), jnp.float32), pltpu.SemaphoreType.DMA],
        compiler_params=pltpu.CompilerParams(has_side_effects=EFFECT),
    )(*pieces)
    return out[2], out[3], [out[0], out[1]], out[4]


def _seg_tiles(s):
    return (s + 2) % NSEG


_POOL_SPECS = [pl.BlockSpec((None, T, GROUP), lambda g, base=base: (base + g, 0, 0)) for base in (0, 4)]
_HEAD_SPECS = [pl.BlockSpec((None, T, HEAD), lambda h, base=base: (base + h // 2, 0, h % 2))
               for base in (8, 12, 16, 20)]


_POOL_W_SPEC = pl.BlockSpec((NDEV, None, GROUP // NDEV, GROUP), lambda g: (0, g, 0, 0))


def _row_ids(shape):
    return lax.broadcasted_iota(jnp.int32, shape, 0)


BAND_ROWS = 128
HALO = 16


def _window_sum(a, gidx, lead):
    width = lax.shift_left(jnp.int32(2), gidx)
    shape = (BAND_ROWS, BAND_ROWS + HALO)
    t, j = lax.broadcasted_iota(jnp.int32, shape, 0), lax.broadcasted_iota(jnp.int32, shape, 1)
    first = t if lead else t + HALO - width + 1
    band = _bf(jnp.where(j >= first, jnp.where(j < first + width, 1.0, 0.0), 0.0))
    zeros = jnp.zeros((HALO, a.shape[1]), jnp.bfloat16)
    padded = [jnp.concatenate([p, zeros] if lead else [zeros, p], axis=0) for p in _split2(a)]
    out = []
    for r0 in range(0, T, BAND_ROWS):
        slab = jnp.concatenate([p[r0:r0 + BAND_ROWS + HALO] for p in padded], axis=1)
        r = _dot(band, slab, 1, 0)
        out.append(r[:, :a.shape[1]] + r[:, a.shape[1]:])
    return jnp.concatenate(out, axis=0)


def _window_mean(s, gidx):
    inv = jnp.where(gidx == 0, 0.5, jnp.where(gidx == 1, 0.25, jnp.where(gidx == 2, 0.125, 0.0625)))
    width = lax.shift_left(jnp.int32(2), gidx)
    head = s[:16] / jnp.minimum(_row_ids((16, s.shape[1])) + 1, width).astype(jnp.float32)
    return jnp.concatenate([head, s[16:] * inv], axis=0)


def _pool_fwd(proj, pool_w, pool_scale, token):
    def body(u_ref, pg_ref, w_ref, sc_ref, token_any, y_ref):
        del token_any
        gidx = pl.program_id(0)
        u, pg = u_ref[...], pg_ref[...]
        d = _window_mean(_window_sum(u, gidx, False), gidx) - u
        mixed = _dot(_bf(d), w_ref[...].reshape(GROUP, GROUP), 1, 0)
        y_ref[...] = _bf(mixed * sc_ref[...] * (pg * _sigmoid(pg)))

    return pl.pallas_call(
        body, name="pool_fwd", grid=(NGROUP,),
        in_specs=[*_POOL_SPECS, _POOL_W_SPEC, pl.BlockSpec((1, GROUP), lambda g: (0, g)), ANY],
        out_specs=pl.BlockSpec((T, GROUP), lambda g: (0, g)),
        out_shape=pltpu.HBM((T, DMIX), jnp.bfloat16),
        compiler_params=_params(("parallel",)),
    )(proj, proj, pool_w, pool_scale, token)


def _tri(lower):
    r = lax.broadcasted_iota(jnp.int32, (CHUNK, CHUNK), 0)
    c = lax.broadcasted_iota(jnp.int32, (CHUNK, CHUNK), 1)
    return (r >= c) if lower else (r <= c)


def _sum_rows_matrix():
    shape = (CHUNK + 16, CHUNK)
    r, c = lax.broadcasted_iota(jnp.int32, shape, 0), lax.broadcasted_iota(jnp.int32, shape, 1)
    run = jnp.where(c <= r, 1.0, 0.0)
    half = jnp.where(c < CHUNK // 2, 1.0, 0.0)
    return _bf(jnp.where(r < CHUNK, run, jnp.where(r < CHUNK + 8, 1.0, half)))


def _rev_sum_matrix():
    shape = (CHUNK, 2 * CHUNK)
    r, c = lax.broadcasted_iota(jnp.int32, shape, 0), lax.broadcasted_iota(jnp.int32, shape, 1)
    return _bf(jnp.where(c < CHUNK, jnp.where(c >= r, 1.0, 0.0), jnp.where(c - CHUNK < r, 1.0, 0.0)))


def _split2(a):
    hi = _bf(a)
    return [hi, _bf(a - hi.astype(jnp.float32))]


def _exact_sums(mat, pieces):
    x = jnp.concatenate([s for p in pieces for s in _split2(p)], axis=1)
    r = _dot(mat, x, 1, 0)
    return [r[:, 2 * j * HEAD:(2 * j + 1) * HEAD] + r[:, (2 * j + 1) * HEAD:(2 * j + 2) * HEAD]
            for j in range(len(pieces))]


def _gates(qv, fl, lb):
    sq = _sigmoid(qv)
    sg = _sigmoid(fl)
    f = lb + (1.0 - lb) * sg
    return dict(sq=sq, qs=qv * sq, sg=sg, f=f, kk=1.0 - f, g=jnp.log2(f))


def _decays(sums):
    big_g = sums[:CHUNK]
    total = sums[CHUNK:CHUNK + 8]
    g_last = jnp.tile(total, (CHUNK // 8, 1))
    g_mid = jnp.tile(sums[CHUNK + 8:], (CHUNK // 8, 1))
    return dict(
        e_q=jnp.exp2(big_g),
        e_k=jnp.exp2(g_last - big_g),
        e_qm=jnp.exp2(jnp.minimum(big_g - g_mid, EXP_CAP)),
        e_km=jnp.exp2(jnp.minimum(g_mid - big_g, EXP_CAP)),
        total8=jnp.exp2(total))


def _group_rows(gi):
    return [pl.ds(pl.multiple_of((gi * NB + j) * CHUNK, CHUNK), CHUNK) for j in range(NB)]


def _lower_bound(lb_ref):
    return _sigmoid(lb_ref[0:1, :] - lb_ref[1:2, :])


def _hgrn_fwd(proj, lb_logits, rec_g, y_in):
    def body(q_ref, f_ref, i_ref, gate_ref, lb_ref, rg_ref, y_any, y_ref, o_ref, st_ref):
        del y_any
        lb = _lower_bound(lb_ref)
        causal = _tri(True)
        smat = _sum_rows_matrix()

        def group(gi, st):
            rows = _group_rows(gi)
            ts = [_gates(q_ref[r, :], f_ref[r, :], lb) for r in rows]
            ds = [_decays(s) for s in _exact_sums(smat, [t["g"] for t in ts])]
            vs = [_bf(i_ref[r, :]) for r in rows]
            q_m = [_bf(t["qs"] * d["e_qm"]) for t, d in zip(ts, ds)]
            k_m = [_bf(t["kk"] * d["e_km"]) for t, d in zip(ts, ds)]
            q_e = [_bf(t["qs"] * d["e_q"]) for t, d in zip(ts, ds)]
            k_e = [_bf(t["kk"] * d["e_k"]) for t, d in zip(ts, ds)]
            a = [_bf(jnp.where(causal, _dot(q_m[j], k_m[j], 1, 1), 0.0)) for j in range(NB)]
            intra = [_dot(a[j], vs[j], 1, 0) for j in range(NB)]
            upd = [_dot(vs[j], k_e[j], 0, 0) for j in range(NB)]
            for j in range(NB):
                st_ref[gi * NB + j] = st
                o_ref[rows[j], :] = intra[j] + _dot(q_e[j], _bf(st), 1, 1)
                st = st * jnp.tile(ds[j]["total8"], (HEAD // 8, 1)) + upd[j]
            return st

        lax.fori_loop(0, NGRP, group, jnp.zeros((HEAD, HEAD), jnp.float32))
        o = o_ref[...]
        rn = o * lax.rsqrt(jnp.mean(o * o, axis=-1, keepdims=True) + EPS)
        gate = gate_ref[...]
        y_ref[...] = _bf(rn * rg_ref[...] * (gate * _sigmoid(gate)))

    return pl.pallas_call(
        body, name="hgrn_fwd", grid=(NHEAD,),
        in_specs=[*_HEAD_SPECS,
                  pl.BlockSpec((2, HEAD), lambda h: (0, h)),
                  pl.BlockSpec((1, HEAD), lambda h: (0, h)),
                  pl.BlockSpec(memory_space=pl.ANY)],
        out_specs=(pl.BlockSpec((T, HEAD), lambda h: (0, NHEAD + h)),
                   pl.BlockSpec((T, HEAD), lambda h: (0, h)),
                   pl.BlockSpec((None, NCHUNK, HEAD, HEAD), lambda h: (h, 0, 0, 0))),
        out_shape=(pltpu.HBM((T, DMIX), jnp.bfloat16), pltpu.HBM((T, D), jnp.float32),
                   pltpu.HBM((NHEAD, NCHUNK, HEAD, HEAD), jnp.float32)),
        input_output_aliases={6: 0},
        compiler_params=_params(("parallel",)),
    )(proj, proj, proj, proj, lb_logits, rec_g, y_in)


def _out_proj_loss(x, y, w_out, target, gf):
    rows = 512
    parts = [slice(k * rows // 2, (k + 1) * rows // 2) for k in range(2)]

    def body(x_ref, y_ref, w_ref, t_ref, g_ref, dz_ref, dzb_ref, sq_ref, dg_ref):
        zs = [x_ref[p, :] + _dot(y_ref[p, :], w_ref[...], 1, 0) for p in parts]
        sq = dg = 0.0
        for p, z in zip(parts, zs):
            r = lax.rsqrt(jnp.mean(z * z, axis=-1, keepdims=True) + EPS)
            zhat = z * r
            err = zhat * g_ref[...] - t_ref[p, :]
            dy = err * (1.0 / D)
            gdy = dy * g_ref[...]
            dz = r * (gdy - zhat * jnp.mean(zhat * gdy, axis=-1, keepdims=True))
            dz_ref[p, :] = dz
            dzb_ref[p, :] = _bf(dz)
            sq = sq + jnp.sum(err * err, axis=0, keepdims=True)
            dg = dg + jnp.sum(zhat * dy, axis=0, keepdims=True)

        @pl.when(pl.program_id(0) == 0)
        def _():
            sq_ref[...] = sq
            dg_ref[...] = dg

        @pl.when(pl.program_id(0) != 0)
        def _():
            sq_ref[...] += sq
            dg_ref[...] += dg

    tile = pl.BlockSpec((rows, D), lambda i: (i, 0))
    vec = pl.BlockSpec((1, D), lambda i: (0, 0))
    return pl.pallas_call(
        body, name="out_proj_loss", grid=(T // rows,),
        in_specs=[tile, pl.BlockSpec((rows, DMIX), lambda i: (i, 0)), pl.BlockSpec((DMIX, D), lambda i: (0, 0)),
                  tile, vec],
        out_specs=(tile, tile, vec, vec),
        out_shape=(pltpu.HBM((T, D), jnp.float32), pltpu.HBM((T, D), jnp.bfloat16),
                   jax.ShapeDtypeStruct((1, D), jnp.float32), jax.ShapeDtypeStruct((1, D), jnp.float32)),
        compiler_params=_params(("arbitrary",)),
    )(x, y, w_out, target, gf)


def _out_proj_bwd(dzb, w_out, y):
    tn = 512

    def body(dz_ref, w_ref, y_ref, dy_ref, gw_ref, gwb_ref):
        dz = dz_ref[...]
        dy_ref[...] = _dot(dz, w_ref[...], 1, 1)
        gw = _dot(y_ref[...], dz, 0, 0)
        gw_ref[...] = gw
        gwb_ref[...] = _bf(gw)

    return pl.pallas_call(
        body, name="out_proj_bwd", grid=(DMIX // tn,),
        in_specs=[pl.BlockSpec((T, D), lambda n: (0, 0)), pl.BlockSpec((tn, D), lambda n: (n, 0)),
                  pl.BlockSpec((T, tn), lambda n: (0, n))],
        out_specs=(pl.BlockSpec((T, tn), lambda n: (0, n)), pl.BlockSpec((tn, D), lambda n: (n, 0)),
                   pl.BlockSpec((tn, D), lambda n: (n, 0))),
        out_shape=(pltpu.HBM((T, DMIX), jnp.float32), pltpu.HBM((DMIX, D), jnp.float32),
                   pltpu.HBM((DMIX, D), jnp.bfloat16)),
        compiler_params=_params(("parallel",)),
    )(dzb, w_out, y)


def _hgrn_bwd(proj, lb_logits, rec_g, o, states, dymix, dproj_in, token):
    def body(q_ref, f_ref, i_ref, gate_ref, lb_ref, rg_ref, o_ref, st_ref, dy_ref, dp_any, token_any,
             dp_ref, drg_ref, dlb_ref, do_ref):
        del dp_any, token_any
        lb = _lower_bound(lb_ref)
        causal = _tri(True)
        smat, rmat = _sum_rows_matrix(), _rev_sum_matrix()

        o = o_ref[...]
        rs = lax.rsqrt(jnp.mean(o * o, axis=-1, keepdims=True) + EPS)
        rn = o * rs
        gate = gate_ref[...]
        sgate = _sigmoid(gate)
        dyv = dy_ref[...]
        d_r = dyv * (gate * sgate)
        dp_ref[3] = _bf(dyv * (rn * rg_ref[...]) * (sgate * (1.0 + gate * (1.0 - sgate))))
        drg_ref[...] = jnp.sum(d_r * rn, axis=0, keepdims=True)
        drn = d_r * rg_ref[...]
        do_ref[...] = rs * (drn - rn * jnp.mean(rn * drn, axis=-1, keepdims=True))

        def group(i, carry):
            dst, dlb = carry
            gi = NGRP - 1 - i
            rows = _group_rows(gi)
            span = range(NB)
            qvs = [q_ref[r, :] for r in rows]
            ts = [_gates(qv, f_ref[r, :], lb) for qv, r in zip(qvs, rows)]
            ds = [_decays(s) for s in _exact_sums(smat, [t["g"] for t in ts])]
            vs = [_bf(i_ref[r, :]) for r in rows]
            dos = [_bf(do_ref[r, :]) for r in rows]
            sts = [st_ref[gi * NB + j] for j in span]
            qe_f = [t["qs"] * d["e_q"] for t, d in zip(ts, ds)]
            ke_f = [t["kk"] * d["e_k"] for t, d in zip(ts, ds)]
            q_e, k_e = [_bf(a) for a in qe_f], [_bf(a) for a in ke_f]
            q_m = [_bf(t["qs"] * d["e_qm"]) for t, d in zip(ts, ds)]
            k_m = [_bf(t["kk"] * d["e_km"]) for t, d in zip(ts, ds)]
            a = [_bf(jnp.where(causal, _dot(q_m[j], k_m[j], 1, 1), 0.0)) for j in span]
            da = [_bf(jnp.where(causal, _dot(dos[j], vs[j], 1, 1), 0.0)) for j in span]
            dqm = [_dot(da[j], k_m[j], 1, 0) for j in span]
            dkm = [_dot(da[j], q_m[j], 0, 0) for j in span]
            dv_in = [_dot(a[j], dos[j], 0, 0) for j in span]
            dqe = [_dot(dos[j], _bf(sts[j]), 1, 0) for j in span]
            grow = [_dot(dos[j], q_e[j], 0, 0) for j in span]
            dke, carried = [None] * NB, [None] * NB
            for j in reversed(span):
                dst_b = _bf(dst)
                dke[j] = _dot(vs[j], dst_b, 1, 0)
                dp_ref[2, rows[j], :] = _bf(dv_in[j] + _dot(k_e[j], dst_b, 1, 1))
                carried[j] = ds[j]["total8"] * jnp.sum(dst * sts[j], axis=0, keepdims=True)
                dst = dst * jnp.tile(ds[j]["total8"], (HEAD // 8, 1)) + grow[j]
            kdk = [ke_f[j] * dke[j] for j in span]
            pos = [(q_m[j].astype(jnp.float32) * dqm[j] - k_m[j].astype(jnp.float32) * dkm[j]) + qe_f[j] * dqe[j]
                   for j in span]
            dgs = _exact_sums(rmat, [jnp.concatenate([pos[j], kdk[j]], axis=0) for j in span])
            for j in span:
                t, d = ts[j], ds[j]
                dg = dgs[j] + jnp.tile(carried[j], (CHUNK // 8, 1))
                dqs = dqm[j] * d["e_qm"] + dqe[j] * d["e_q"]
                dkk = dkm[j] * d["e_km"] + dke[j] * d["e_k"]
                df = dg / t["f"] - dkk
                dp_ref[1, rows[j], :] = _bf(df * (1.0 - lb) * (t["sg"] * (1.0 - t["sg"])))
                dp_ref[0, rows[j], :] = _bf(dqs * (t["sq"] * (1.0 + qvs[j] * (1.0 - t["sq"]))))
                dlb = dlb + df * (1.0 - t["sg"])
            return dst, dlb

        _, dlb = lax.fori_loop(0, NGRP, group, (jnp.zeros((HEAD, HEAD), jnp.float32),
                                                jnp.zeros((CHUNK, HEAD), jnp.float32)))
        dlb_ref[...] = jnp.sum(dlb, axis=0, keepdims=True)

    vec = pl.BlockSpec((1, HEAD), lambda h: (0, h))
    return pl.pallas_call(
        body, name="hgrn_bwd", grid=(NHEAD,),
        in_specs=[*_HEAD_SPECS,
                  pl.BlockSpec((2, HEAD), lambda h: (0, h)), vec,
                  pl.BlockSpec((T, HEAD), lambda h: (0, h)),
                  pl.BlockSpec((None, NCHUNK, HEAD, HEAD), lambda h: (h, 0, 0, 0)),
                  pl.BlockSpec((T, HEAD), lambda h: (0, NHEAD + h)), ANY, ANY],
        out_specs=(pl.BlockSpec((4, T, HEAD), lambda h: (0, 0, h)), vec, vec),
        out_shape=(pltpu.HBM((NSEG, T, D), jnp.bfloat16),
                   jax.ShapeDtypeStruct((1, D), jnp.float32), jax.ShapeDtypeStruct((1, D), jnp.float32)),
        scratch_shapes=[pltpu.VMEM((T, HEAD), jnp.float32)],
        input_output_aliases={9: 0},
        compiler_params=_params(("parallel",)),
    )(proj, proj, proj, proj, lb_logits, rec_g, o, states, dymix, dproj_in, token)


def _pool_bwd(proj, pool_w, pool_scale, dymix):
    def body(u_ref, pg_ref, w_ref, sc_ref, dy_ref, dp_ref, gw_ref, gs_ref):
        gidx = pl.program_id(0)
        u, pg = u_ref[...], pg_ref[...]
        w = w_ref[...].reshape(GROUP, GROUP)
        d = _bf(_window_mean(_window_sum(u, gidx, False), gidx) - u)
        mixed = _dot(d, w, 1, 0)
        spg = _sigmoid(pg)
        dyv = dy_ref[...]
        d_p = dyv * (pg * spg)
        dp_ref[1] = _bf(dyv * (mixed * sc_ref[...]) * (spg * (1.0 + pg * (1.0 - spg))))
        gs_ref[...] = jnp.sum(d_p * mixed, axis=0, keepdims=True)
        dmixed = _bf(d_p * sc_ref[...])
        gw_ref[...] = _dot(d, dmixed, 0, 0).reshape(gw_ref.shape)
        dd = _dot(dmixed, w, 1, 1)
        dp_ref[0] = _bf(_window_sum(_window_mean(dd, gidx), gidx, True) - dd)

    return pl.pallas_call(
        body, name="pool_bwd", grid=(NGROUP,),
        in_specs=[*_POOL_SPECS, _POOL_W_SPEC,
                  pl.BlockSpec((1, GROUP), lambda g: (0, g)),
                  pl.BlockSpec((T, GROUP), lambda g: (0, g))],
        out_specs=(pl.BlockSpec((2, T, GROUP), lambda g: (2, 0, g)), _POOL_W_SPEC,
                   pl.BlockSpec((1, GROUP), lambda g: (0, g))),
        out_shape=(pltpu.HBM((NSEG, T, D), jnp.bfloat16),
                   jax.ShapeDtypeStruct((NDEV, NGROUP, GROUP // NDEV, GROUP), jnp.float32),
                   jax.ShapeDtypeStruct((1, D), jnp.float32)),
        compiler_params=_params(("parallel",)),
    )(proj, proj, pool_w, pool_scale, dymix)


HALF = NTILE // 2
AWAY = HALF - 3


def _dproj_tile(chip, side, p):
    j = 6 * chip + 3 * side + p
    return ((j // 4 + 4) % NSEG, 0, j % 4)


def _sibling_copy(sib_out, sib_in, send_sems, recv_sems, slot):
    x, y, c, _ = _place()
    return pltpu.make_async_remote_copy(
        src_ref=sib_out.at[slot], dst_ref=sib_in.at[slot], send_sem=send_sems.at[slot],
        recv_sem=recv_sems.at[slot], device_id=(x, y, 1 - c), device_id_type=MESH)


def _proj_bwd_w_far(place, ht, dproj):
    def body(place_ref, h_ref, dp_ref, sib_out, sib_in, send_sems, recv_sems, stage, loc_sems):
        del place_ref
        i = pl.program_id(0)

        def to_hbm(k):
            return pltpu.make_async_copy(stage.at[k], sib_out.at[k], loc_sems.at[k])

        def send(k):
            to_hbm(k).wait()
            _sibling_copy(sib_out, sib_in, send_sems, recv_sems, k).start()

        stage[i] = _bf(_dot(h_ref[...], dp_ref[...], 1, 0))

        @pl.when(i > 0)
        def _():
            send(i - 1)

        to_hbm(i).start()

        @pl.when(i == HALF - 1)
        def _():
            send(i)

    buf = pltpu.HBM((HALF, D, TILE), jnp.bfloat16)
    sems = pltpu.SemaphoreType.DMA((HALF,))
    return pl.pallas_call(
        body, name="proj_bwd_w_far",
        grid_spec=pltpu.PrefetchScalarGridSpec(
            num_scalar_prefetch=1, grid=(HALF,),
            in_specs=[pl.BlockSpec((D, T), lambda i, pr: (0, 0)),
                      pl.BlockSpec((None, T, TILE), lambda i, pr: _dproj_tile(i // 3, 1 - pr[2], i % 3))],
            out_specs=(HBM, HBM, SEM, SEM),
            scratch_shapes=[pltpu.VMEM((HALF, D, TILE), jnp.bfloat16), pltpu.SemaphoreType.DMA((HALF,))]),
        out_shape=(buf, buf, sems, sems),
        compiler_params=pltpu.CompilerParams(dimension_semantics=("arbitrary",), vmem_limit_bytes=48 * MIB,
                                             has_side_effects=EFFECT),
    )(place, ht, dproj)


def _proj_bwd_w_near(place, ht, dproj, sib_out, sib_in, sib_send, sib_recv):
    def owner_chip(k, pr):
        return jnp.where(k < AWAY, (pr[1] + 1 + k % 3) % 4, pr[1])

    def tile_p(k):
        return jnp.where(k < AWAY, k // 3, k - AWAY)

    def body(place_ref, h_ref, dp_ref, sib_out, sib_in, sib_send, sib_recv, sums, own_ref, landing, out_send,
             out_recv, recvbuf, outbuf, in_sems, loc_sems):
        i = pl.program_id(0)
        px, py, c, _ = _place()

        def slot_of(k):
            return 3 * owner_chip(k, place_ref) + tile_p(k)

        def load(k):
            return pltpu.make_async_copy(sib_in.at[slot_of(k)], recvbuf.at[k % 2], in_sems.at[k % 2])

        def fetch(k):
            _sibling_copy(sib_out, sib_in, sib_send, sib_recv, slot_of(k)).wait_recv()
            load(k).start()

        def route(k):
            chip = owner_chip(k, place_ref)
            cx, cy = chip // 2, chip % 2
            return cx, cy, (cx ^ px) + 2 * (cy ^ py) - 1, tile_p(k)

        def to_hbm(k):
            _, _, rel, p = route(k)
            return pltpu.make_async_copy(outbuf.at[k], sums.at[rel, p], loc_sems.at[k])

        def to_owner(k):
            cx, cy, rel, p = route(k)
            return pltpu.make_async_remote_copy(
                src_ref=sums.at[rel, p], dst_ref=landing.at[rel, p], send_sem=out_send.at[3 * rel + p],
                recv_sem=out_recv.at[3 * rel + p], device_id=(cx, cy, c), device_id_type=MESH)

        @pl.when(i == 0)
        def _():
            fetch(i)

        @pl.when(i < HALF - 1)
        def _():
            fetch(i + 1)

        load(i).wait()
        total = _dot(h_ref[...], dp_ref[...], 1, 0) + recvbuf[i % 2].astype(jnp.float32)
        own_ref[...] = total
        outbuf[jnp.minimum(i, AWAY)] = _bf(total)

        @pl.when(i < AWAY)
        def _():
            to_hbm(i).start()

        @pl.when(jnp.logical_and(i > 0, i <= AWAY))
        def _():
            to_hbm(i - 1).wait()
            to_owner(i - 1).start()

        @pl.when(i == HALF - 1)
        def _():
            for slot in range(HALF):
                _sibling_copy(sib_out, sib_in, sib_send, sib_recv, slot).wait_send()

    travelling = pltpu.HBM((3, 3, D, TILE), jnp.bfloat16)
    sems = pltpu.SemaphoreType.DMA((AWAY,))
    return pl.pallas_call(
        body, name="proj_bwd_w_near",
        grid_spec=pltpu.PrefetchScalarGridSpec(
            num_scalar_prefetch=1, grid=(HALF,),
            in_specs=[pl.BlockSpec((D, T), lambda i, pr: (0, 0)),
                      pl.BlockSpec((None, T, TILE), lambda i, pr: _dproj_tile(owner_chip(i, pr), pr[2], tile_p(i))),
                      HBM, HBM, SEM, SEM],
            out_specs=(HBM, pl.BlockSpec((None, D, TILE), lambda i, pr: (jnp.where(i < AWAY, 0, i % 3), 0, 0)),
                       HBM, SEM, SEM),
            scratch_shapes=[pltpu.VMEM((2, D, TILE), jnp.bfloat16), pltpu.VMEM((AWAY + 1, D, TILE), jnp.bfloat16),
                            pltpu.SemaphoreType.DMA((2,)), pltpu.SemaphoreType.DMA((AWAY,))]),
        out_shape=(travelling, pltpu.HBM((3, D, TILE), jnp.float32), travelling, sems, sems),
        compiler_params=pltpu.CompilerParams(dimension_semantics=("arbitrary",), vmem_limit_bytes=48 * MIB,
                                             has_side_effects=EFFECT),
    )(place, ht, dproj, sib_out, sib_in, sib_send, sib_recv)


def _proj_bwd_x(dproj, w_t, x, g1, dz, token):
    tm = 512
    pairs = NSEG // 2

    def body(dp_ref, w_ref, x_ref, g_ref, dz_ref, token_any, dx_ref, dg_ref, wcat, acc):
        del token_any
        m, s = pl.program_id(0), pl.program_id(1)

        @pl.when(m == 0)
        def _():
            for i in range(8):
                wcat[s, :, i * TILE:(i + 1) * TILE] = w_ref[i]

        @pl.when(s == 0)
        def _():
            acc[...] = jnp.zeros((tm, D), jnp.float32)

        acc[...] += _dot(jnp.concatenate([dp_ref[0], dp_ref[1]], axis=1), wcat[s], 1, 1)

        @pl.when(s == pairs - 1)
        def _():
            xv = x_ref[...]
            rs = lax.rsqrt(jnp.mean(xv * xv, axis=-1, keepdims=True) + EPS)
            xhat = xv * rs
            dhv = acc[...]
            gdh = dhv * g_ref[...]
            dx_ref[...] = dz_ref[...] + rs * (gdh - xhat * jnp.mean(xhat * gdh, axis=-1, keepdims=True))
            dg = jnp.sum(xhat * dhv, axis=0, keepdims=True)

            @pl.when(m == 0)
            def _():
                dg_ref[...] = dg

            @pl.when(m != 0)
            def _():
                dg_ref[...] += dg

    rows = pl.BlockSpec((tm, D), lambda m, s: (m, 0))
    vec = pl.BlockSpec((1, D), lambda m, s: (0, 0))
    return pl.pallas_call(
        body, name="proj_bwd_x", grid=(T // tm, pairs),
        in_specs=[pl.BlockSpec((2, tm, D), lambda m, s: (s, m, 0)),
                  pl.BlockSpec((8, D, TILE), lambda m, s: ((jnp.where(m == 0, s, pairs - 1) + 1) % pairs, 0, 0)),
                  rows, vec, rows, ANY],
        out_specs=(rows, vec),
        out_shape=(jax.ShapeDtypeStruct((T, D), jnp.float32), jax.ShapeDtypeStruct((1, D), jnp.float32)),
        scratch_shapes=[pltpu.VMEM((pairs, D, 2 * D), jnp.bfloat16), pltpu.VMEM((tm, D), jnp.float32)],
        compiler_params=_params(("arbitrary", "arbitrary"), vmem_mib=56),
    )(dproj, w_t, x, g1, dz, token)


def _adamw(w, g, m, v):
    m_new = ADAM_B1 * m + (1.0 - ADAM_B1) * g
    v_new = ADAM_B2 * v + (1.0 - ADAM_B2) * (g * g)
    delta = -ADAM_LR * ((m_new / BC1) / (jnp.sqrt(v_new / BC2) + ADAM_EPS) + ADAM_WD * w)
    return delta, m_new, v_new


def _reduce_adam(name, place, parts, w, m, v, grid, w_spec):
    n = len(parts)

    def body(place_ref, *refs):
        del place_ref
        w_ref, m_ref, v_ref, g_ref, d_ref, mo_ref, vo_ref = refs[n:]
        g = None
        for ref, (_, _, stacked) in zip(refs[:n], parts):
            terms = [ref[r] for r in range(ref.shape[0])] if stacked else [ref[...]]
            for t in terms:
                if t.shape[-1] != w_ref.shape[-1]:
                    t = jnp.concatenate([t[p] for p in range(t.shape[0])], axis=1)
                g = t.astype(jnp.float32) if g is None else g + t.astype(jnp.float32)
        delta, m_new, v_new = _adamw(w_ref[...], g, m_ref[...], v_ref[...])
        g_ref[...] = g
        d_ref[...] = delta
        mo_ref[...] = m_new
        vo_ref[...] = v_new

    shape = jax.ShapeDtypeStruct(w.shape, jnp.float32)
    return pl.pallas_call(
        body, name=name,
        grid_spec=pltpu.PrefetchScalarGridSpec(
            num_scalar_prefetch=1, grid=grid,
            in_specs=[spec for _, spec, _ in parts] + [w_spec] * 3, out_specs=(w_spec,) * 4),
        out_shape=(shape,) * 4,
        compiler_params=_params(("parallel",)),
    )(place, *[_in_hbm(a) for a in [a for a, _, _ in parts] + [w, m, v]])


SMALL_ROWS = (1, 1, 2, 1, 1)


def _small_adam(place, own, parts, ws, ms, vs):
    n = len(SMALL_ROWS)

    def body(place_ref, own_ref, p_ref, *refs):
        ins, outs, bufs = refs[:3 * n], refs[3 * n:3 * n + 1 + 4 * n], refs[3 * n + 1 + 4 * n:]

        def stacked(group, buf):
            r0 = 0
            for ref, k in zip(group, SMALL_ROWS):
                buf[r0:r0 + k, :] = ref[...]
                r0 += k
            buf[r0:, :] = jnp.zeros((8 - r0, D), jnp.float32)
            return buf[...]

        wv, mv, vv = (stacked(ins[n * j:n * j + n], bufs[j]) for j in range(3))
        me = place_ref[0]
        g = None
        for s in range(NDEV):
            term = jnp.where(me == s, own_ref[...], p_ref[s])
            g = term if g is None else g + term
        rows = _row_ids(wv.shape)
        other = jnp.where(rows == 2, pltpu.roll(wv, 7, 0), jnp.where(rows == 3, pltpu.roll(wv, 1, 0), 0.0))
        lbv = _sigmoid(wv - other)
        sign = jnp.where(rows == 2, 1.0, -1.0)
        g = jnp.where((rows == 2) | (rows == 3), sign * g * lbv * (1.0 - lbv), g)
        delta, m_new, v_new = _adamw(wv, g, mv, vv)
        outs[0][...] = jnp.sum(g[6:7], axis=1, keepdims=True) * (0.5 / D)
        for j, val in enumerate((g, delta, m_new, v_new)):
            r0 = 0
            for ref, k in zip(outs[1 + n * j:1 + n * j + n], SMALL_ROWS):
                ref[...] = val[r0:r0 + k]
                r0 += k

    vmem = pl.BlockSpec(memory_space=pltpu.VMEM)
    shapes = [jax.ShapeDtypeStruct((k, D), jnp.float32) for k in SMALL_ROWS]
    out = pl.pallas_call(
        body, name="small_adam", out_shape=(jax.ShapeDtypeStruct((1, 1), jnp.float32), *shapes * 4),
        in_specs=[pl.BlockSpec(memory_space=pltpu.SMEM)] + [vmem] * (2 + 3 * n), out_specs=(vmem,) * (1 + 4 * n),
        scratch_shapes=[pltpu.VMEM((8, D), jnp.float32)] * 3,
    )(place, own, parts, *ws, *ms, *vs)
    return out[0], [out[1 + n * j:1 + n * j + n] for j in range(4)]


def kernel(x, norm1_g, w_in, pool_w, pool_scale, lb_logits, rec_norm_g, w_out, final_norm_g, loss_target, m_norm1_g, m_w_in, m_pool_w, m_pool_scale, m_lb_logits, m_rec_norm_g, m_w_out, m_final_norm_g, v_norm1_g, v_w_in, v_pool_w, v_pool_scale, v_lb_logits, v_rec_norm_g, v_w_out, v_final_norm_g):
    xs = x[0]
    target = loss_target[0]
    ix, iy, ic = lax.axis_index("x"), lax.axis_index("y"), lax.axis_index("c")
    place = jnp.stack([4 * ix + 2 * iy + ic, 2 * ix + iy, ic]).astype(jnp.int32)
    gf = final_norm_g.reshape(1, D)

    ht, w_t, w_out_b, w_out_g, pool_g, proj = _gather_proj(xs, norm1_g, w_in, w_out, pool_w)
    wout = [w_out_b, w_out_g]
    wout_send, wout_recv, wout, wout_token = _split_start("gather_wout_start", wout, NDEV - 1, _plan_wout)

    y = _pool_fwd(proj, pool_g, pool_scale, wout_token)
    y, o, states = _hgrn_fwd(proj, lb_logits, rec_norm_g, y)
    _, w_out_g = _split_wait("gather_wout_wait", wout, wout_send, wout_recv, _plan_wout, o)
    w_out_full = _in_hbm(w_out_g.reshape(DMIX, D))
    dz, dzb, sq, dgf = _out_proj_loss(xs, y, w_out_full, target, gf)

    dymix, gwout_f, gwout_b = _out_proj_bwd(dzb, w_out_full, y)
    dproj, gpool, dscale = _pool_bwd(proj, pool_g, pool_scale, dymix)

    blk_out = (NDEV, DMIX // NDEV, D)
    blk_pool = (NDEV, NGROUP, GROUP // NDEV, GROUP)
    rest = [gwout_b.reshape(blk_out), gpool,
            lax.empty((NDEV - 1,) + blk_out[1:], jnp.bfloat16), lax.empty((NDEV - 1,) + blk_pool[1:], jnp.float32)]
    rest_send, rest_recv, rest, rest_token = _split_start("scatter_rest_start", rest, 2 * (NDEV - 1), _plan_rest)

    dproj, drecg, dlb = _hgrn_bwd(proj, lb_logits, rec_norm_g, o, states, dymix, dproj, rest_token)
    chip_sums, own_sum, landing, win_send, win_recv = _proj_bwd_w_near(
        place, ht, dproj, *_proj_bwd_w_far(place, ht, dproj))
    win = [chip_sums, landing]

    grad_x, dg1 = _proj_bwd_x(dproj, w_t, xs, norm1_g, dz, chip_sums)

    small_send, small_recv, small, small_token = _small_start(
        [a.reshape(-1, D) for a in (dg1, dscale, dlb, dlb, drecg, dgf, sq)])

    _, gpool_own, r_out, r_pool = _split_wait("scatter_rest_wait", rest, rest_send, rest_recv, _plan_rest,
                                              small_token)
    g_wout, d_wout, m_wout, v_wout = _reduce_adam(
        "adam_w_out", place,
        [(gwout_f.reshape(blk_out), pl.BlockSpec((None,) + blk_out[1:], lambda i, pr: (pr[0], 0, 0)), False),
         (r_out, pl.BlockSpec((NDEV - 1,) + blk_out[1:], lambda i, pr: (0, 0, 0)), True)],
        w_out, m_w_out, v_w_out, (1,), pl.BlockSpec((None,) + blk_out[1:], lambda i, pr: (0, 0, 0)))
    g_pool, d_pool, m_pool, v_pool = _reduce_adam(
        "adam_pool_w", place,
        [(gpool_own, pl.BlockSpec((None,) + blk_pool[1:], lambda i, pr: (pr[0], 0, 0, 0)), False),
         (r_pool, pl.BlockSpec((NDEV - 1,) + blk_pool[1:], lambda i, pr: (0, 0, 0, 0)), True)],
        pool_w, m_pool_w, v_pool_w, (1,), pl.BlockSpec((None,) + blk_pool[1:], lambda i, pr: (0, 0, 0, 0)))

    _, r_in = _split_wait("scatter_win_wait", win, win_send, win_recv, _plan_in, d_pool)
    g_win, d_win, m_win, v_win = _reduce_adam(
        "adam_w_in", place,
        [(own_sum, pl.BlockSpec((3, D // 8, TILE), lambda i, pr: (0, i, 0)), False),
         (r_in, pl.BlockSpec((3, 3, D // 8, TILE), lambda i, pr: (0, 0, i, 0)), True)],
        w_in, m_w_in, v_w_in, (8,), pl.BlockSpec((None, D // 8, 3 * TILE), lambda i, pr: (0, i, 0)))

    own_small, r_small = _split_wait("gather_small_wait", small, small_send, small_recv, _plan_small, d_win)
    loss, (g_s, d_s, m_s, v_s) = _small_adam(
        place, own_small, r_small,
        (norm1_g, pool_scale, lb_logits, rec_norm_g, gf),
        (m_norm1_g, m_pool_scale, m_lb_logits, m_rec_norm_g, m_final_norm_g.reshape(1, D)),
        (v_norm1_g, v_pool_scale, v_lb_logits, v_rec_norm_g, v_final_norm_g.reshape(1, D)))

    def outs(small, win, pool, wout):
        n1, ps, lbl, rg, fg = small
        return n1, win, pool, ps, lbl, rg, wout, fg.reshape(D)

    return (loss.reshape(()), grad_x[None],
            *outs(g_s, g_win, g_pool, g_wout), *outs(d_s, d_win, d_pool, d_wout),
            *outs(m_s, m_win, m_pool, m_wout), *outs(v_s, v_win, v_pool, v_wout))
```

```python
import functools

import jax
import jax.numpy as jnp
from jax import lax
from jax.experimental import pallas as pl
from jax.experimental.pallas import tpu as pltpu

T = 2048
D = 1024
NSEG = 6
NTILE = 24
TILE = 256
DMIX = 2048
NDEV = 8
HEAD = 128
NHEAD = 8
CHUNK = 64
NCHUNK = T // CHUNK
NB = 32
NGRP = NCHUNK // NB
NGROUP = 4
GROUP = 256
EPS = 1e-6
EXP_CAP = 115.0
MESH = pl.DeviceIdType.MESH
AXES = ("x", "y", "c")
ANY = pl.BlockSpec(memory_space=pl.ANY)
HBM = pl.BlockSpec(memory_space=pltpu.HBM)
SEM = pl.BlockSpec(memory_space=pltpu.SEMAPHORE)
EFFECT = pltpu.SideEffectType.DATAFLOW_SIDE_EFFECTING

ADAM_LR = 0.001
ADAM_B1 = 0.9
ADAM_B2 = 0.999
ADAM_EPS = 1e-08
ADAM_WD = 0.01
ADAM_STEP = 10
BC1 = 1.0 - ADAM_B1 ** ADAM_STEP
BC2 = 1.0 - ADAM_B2 ** ADAM_STEP

MIB = 1 << 20


def _params(sem=None, vmem_mib=48):
    return pltpu.CompilerParams(dimension_semantics=sem, vmem_limit_bytes=vmem_mib * MIB)


def _sigmoid(v):
    return 1.0 / (1.0 + jnp.exp(-v))


def _dot(a, b, ca, cb, precision=None):
    return lax.dot_general(a, b, (((ca,), (cb,)), ((), ())), precision=precision,
                           preferred_element_type=jnp.float32)


def _bf(v):
    return v.astype(jnp.bfloat16)


def _in_hbm(a):
    return pltpu.with_memory_space_constraint(a, pltpu.HBM)


def _place():
    x, y, c = lax.axis_index("x"), lax.axis_index("y"), lax.axis_index("c")
    return x, y, c, 4 * x + 2 * y + c


def _peer(x, y, c, r):
    return (x ^ ((r >> 2) & 1), y ^ ((r >> 1) & 1), c ^ (r & 1))


def _gather_proj(x, g1, w_in, w_out, pool_w):
    def body(x_ref, g_ref, win_ref, wout_ref, pool_ref, ht_o, wt_o, woutb_o, wout_o, pool_o, proj_o,
             xbuf, hv, htv, wv, wob, pb, stage, send_sems, recv_sems, loc_sems, out_sems):
        px, py, c, my_idx = _place()
        fetch_x = pltpu.make_async_copy(x_ref, xbuf, loc_sems.at[5])
        fetch_x.start()
        me, sibling = (px, py, c), (px, py, 1 - c)
        chips = [(1 - px, py), (px, 1 - py), (1 - px, 1 - py)]
        for p in range(3):
            wv[3 * my_idx + p] = _bf(win_ref[0, :, p * TILE:(p + 1) * TILE])

        def index(bx, by, bc):
            return 4 * bx + 2 * by + bc

        def slot(w, block):
            return wv.at[pl.ds(3 * index(*block), 3)] if w == 0 else pool_o.at[index(*block)]

        def copy(k, w, block, to, src=None):
            return pltpu.make_async_remote_copy(
                src_ref=slot(w, block) if src is None else src, dst_ref=slot(w, block),
                send_sem=send_sems.at[2 * k + w], recv_sem=recv_sems.at[2 * k + w],
                device_id=to, device_id_type=MESH)

        def save(block):
            at = pl.ds(3 * index(*block), 3)
            pltpu.make_async_copy(wv.at[at], wt_o.at[at], loc_sems.at[4]).start()

        srcs = (slot(0, me), pb)
        first = []
        for w in (0, 1):
            if w == 1:
                pb[...] = _bf(pool_ref[0])
                wob[...] = _bf(wout_ref[0])
            group = [copy(1 + j, w, me, (*chip, c), src=srcs[w]) for j, chip in enumerate(chips[:2])]
            group.append(copy(0, w, me, sibling, src=srcs[w]))
            for cp in group:
                cp.start()
            first += group
        save(me)
        locs = [pltpu.make_async_copy(pb, slot(1, me), loc_sems.at[0]),
                pltpu.make_async_copy(wob, wout_o.at[my_idx], loc_sems.at[1]),
                pltpu.make_async_copy(wob, woutb_o, loc_sems.at[2])]
        for cp in locs:
            cp.start()

        fetch_x.wait()
        xv = xbuf[...]
        hv[...] = _bf(xv * lax.rsqrt(jnp.mean(xv * xv, axis=-1, keepdims=True) + EPS) * g_ref[...])
        rows = 256
        for r0 in range(0, T, rows):
            htv[:, r0:r0 + rows] = hv[r0:r0 + rows, :].T
        locs.append(pltpu.make_async_copy(htv, ht_o, loc_sems.at[3]))
        locs[-1].start()

        def out_copy(p, j):
            return pltpu.make_async_copy(stage.at[p], proj_o.at[j], out_sems.at[p])

        def project(nth, block):
            base = 3 * index(*block)

            def tile(p, carry):
                if nth > 0:
                    out_copy(p, base + p).wait()
                stage[p] = _dot(hv[...], wv[base + p], 1, 0)
                out_copy(p, base + p).start()
                return carry

            lax.fori_loop(0, 3, tile, 0)

        project(0, me)
        copy(0, 0, sibling, me).wait_recv()
        save(sibling)
        project(1, sibling)
        passed = []
        relay_from = (px ^ (1 - c), py ^ c, c)
        relay_to = (px ^ c, py ^ (1 - c), c)

        def arrived(w, j):
            copy(1 + j, w, (*chips[j], c), me).wait_recv()
            passed.append(copy(4 + j, w, (*chips[j], c), sibling))
            passed[-1].start()

        def relay(w):
            passed.append(copy(3, w, relay_from, relay_to))
            passed[-1].start()

        def handed(nth, j):
            copy(4 + j, 0, (*chips[j], 1 - c), me).wait_recv()
            save((*chips[j], 1 - c))
            project(nth, (*chips[j], 1 - c))

        arrived(0, 0)
        arrived(0, 1)
        relay(0)
        for j in range(2):
            save((*chips[j], c))
            project(2 + j, (*chips[j], c))
        handed(4, 0)
        handed(5, 1)
        arrived(1, 0)
        arrived(1, 1)
        relay(1)
        arrived(0, 2)
        save((*chips[2], c))
        project(6, (*chips[2], c))
        handed(7, 2)
        arrived(1, 2)
        copy(0, 1, sibling, me).wait_recv()
        for j, chip in enumerate(chips):
            copy(4 + j, 1, (*chip, 1 - c), me).wait_recv()
        keep = pltpu.make_async_copy(wv, wt_o, loc_sems.at[4])
        for p in range(3):
            out_copy(p, p).wait()
        for cp in first + passed:
            cp.wait_send()
        keep.wait()
        for cp in locs:
            cp.wait()

    vmem = pl.BlockSpec(memory_space=pltpu.VMEM)
    bf16 = jnp.bfloat16
    return pl.pallas_call(
        body, name="gather_proj",
        out_shape=(pltpu.HBM((D, T), bf16), pltpu.HBM((NTILE, D, TILE), bf16),
                   pltpu.HBM((DMIX // NDEV, D), bf16), pltpu.HBM((NDEV, DMIX // NDEV, D), bf16),
                   pltpu.HBM((NDEV, NGROUP, GROUP // NDEV, GROUP), bf16), pltpu.HBM((NTILE, T, TILE), jnp.float32)),
        in_specs=[ANY] + [vmem] * 4, out_specs=(ANY,) * 6,
        scratch_shapes=[pltpu.VMEM((T, D), jnp.float32),
                        pltpu.VMEM((T, D), bf16), pltpu.VMEM((D, T), bf16), pltpu.VMEM((NTILE, D, TILE), bf16),
                        pltpu.VMEM((DMIX // NDEV, D), bf16), pltpu.VMEM((NGROUP, GROUP // NDEV, GROUP), bf16),
                        pltpu.VMEM((3, T, TILE), jnp.float32),
                        pltpu.SemaphoreType.DMA((14,)), pltpu.SemaphoreType.DMA((14,)),
                        pltpu.SemaphoreType.DMA((6,)), pltpu.SemaphoreType.DMA((3,))],
        compiler_params=_params(vmem_mib=56),
    )(x, g1, w_in, w_out, pool_w)


def _split_start(name, arrays, n_copies, plan):
    k = len(arrays)

    def body(*refs):
        send_sems, recv_sems, token = refs[k], refs[k + 1], refs[-1]
        for i, (src, dst, to) in enumerate(plan(refs[:k])):
            pltpu.make_async_remote_copy(src_ref=src, dst_ref=dst, send_sem=send_sems.at[i],
                                         recv_sem=recv_sems.at[i], device_id=to, device_id_type=MESH).start()
        token[...] = jnp.zeros_like(token)

    out = pl.pallas_call(
        body, name=name,
        out_shape=(pltpu.SemaphoreType.DMA((n_copies,)), pltpu.SemaphoreType.DMA((n_copies,)),
                   *[pltpu.HBM(a.shape, a.dtype) for a in arrays], jax.ShapeDtypeStruct((8, 128), jnp.float32)),
        in_specs=[HBM] * k, out_specs=(SEM, SEM, *[HBM] * k, pl.BlockSpec(memory_space=pltpu.VMEM)),
        input_output_aliases={i: 2 + i for i in range(k)},
        compiler_params=pltpu.CompilerParams(has_side_effects=EFFECT),
    )(*[pltpu.with_memory_space_constraint(a, pltpu.HBM) for a in arrays])
    return out[0], out[1], out[2:2 + k], out[-1]


def _split_wait(name, arrays, send_sems, recv_sems, plan, after):
    k = len(arrays)

    def body(*refs):
        sends, recvs = refs[k], refs[k + 1]
        for i, (src, dst, to) in enumerate(plan(refs[:k])):
            cp = pltpu.make_async_remote_copy(src_ref=src, dst_ref=dst, send_sem=sends.at[i], recv_sem=recvs.at[i],
                                              device_id=to, device_id_type=MESH)
            cp.wait_send()
            cp.wait_recv()

    return pl.pallas_call(
        body, name=name,
        out_shape=tuple(pltpu.HBM(a.shape, a.dtype) for a in arrays),
        in_specs=[HBM] * k + [SEM, SEM, ANY], out_specs=(HBM,) * k,
        input_output_aliases={i: i for i in range(k)},
        compiler_params=pltpu.CompilerParams(has_side_effects=EFFECT),
    )(*arrays, send_sems, recv_sems, after)


def _plan_wout(refs):
    src, land = refs
    x, y, c, me = _place()
    return [(src, land.at[me], _peer(x, y, c, r)) for r in range(1, NDEV)]


def _plan_rest(refs):
    gob, gpf, r_out, r_pool = refs
    x, y, c, me = _place()
    plan = []
    for r in range(1, NDEV):
        plan.append((gob.at[me ^ r], r_out.at[r - 1], _peer(x, y, c, r)))
        plan.append((gpf.at[me ^ r], r_pool.at[r - 1], _peer(x, y, c, r)))
    return plan


def _plan_in(refs):
    sums, landing = refs
    x, y, c, _ = _place()
    plan = []
    for rel, (dx, dy) in enumerate(((1, 0), (0, 1), (1, 1))):
        for p in range(3):
            plan.append((sums.at[rel, p], landing.at[rel, p], (x ^ dx, y ^ dy, c)))
    return plan


def _plan_small(refs):
    small, land = refs
    x, y, c, me = _place()
    return [(small, land.at[me], _peer(x, y, c, r)) for r in range(1, NDEV)]


def _seg_tiles(s):
    return (s + 2) % NSEG


_POOL_SPECS = [pl.BlockSpec((None, T, GROUP), lambda g, base=base: (base + g, 0, 0)) for base in (0, 4)]
_HEAD_SPECS = [pl.BlockSpec((None, T, HEAD), lambda h, base=base: (base + h // 2, 0, h % 2))
               for base in (8, 12, 16, 20)]


_POOL_W_SPEC = pl.BlockSpec((NDEV, None, GROUP // NDEV, GROUP), lambda g: (0, g, 0, 0))


def _row_ids(shape):
    return lax.broadcasted_iota(jnp.int32, shape, 0)


BAND_ROWS = 128
HALO = 16


def _window_sum(a, gidx, lead):
    width = lax.shift_left(jnp.int32(2), gidx)
    shape = (BAND_ROWS, BAND_ROWS + HALO)
    t, j = lax.broadcasted_iota(jnp.int32, shape, 0), lax.broadcasted_iota(jnp.int32, shape, 1)
    first = t if lead else t + HALO - width + 1
    band = _bf(jnp.where(j >= first, jnp.where(j < first + width, 1.0, 0.0), 0.0))
    zeros = jnp.zeros((HALO, a.shape[1]), jnp.bfloat16)
    padded = [jnp.concatenate([p, zeros] if lead else [zeros, p], axis=0) for p in _split2(a)]
    out = []
    for r0 in range(0, T, BAND_ROWS):
        slab = jnp.concatenate([p[r0:r0 + BAND_ROWS + HALO] for p in padded], axis=1)
        r = _dot(band, slab, 1, 0)
        out.append(r[:, :a.shape[1]] + r[:, a.shape[1]:])
    return jnp.concatenate(out, axis=0)


def _window_mean(s, gidx):
    inv = jnp.where(gidx == 0, 0.5, jnp.where(gidx == 1, 0.25, jnp.where(gidx == 2, 0.125, 0.0625)))
    width = lax.shift_left(jnp.int32(2), gidx)
    head = s[:16] / jnp.minimum(_row_ids((16, s.shape[1])) + 1, width).astype(jnp.float32)
    return jnp.concatenate([head, s[16:] * inv], axis=0)


def _pool_fwd(proj, pool_w, pool_scale, token):
    def body(u_ref, pg_ref, w_ref, sc_ref, token_any, y_ref):
        del token_any
        gidx = pl.program_id(0)
        u, pg = u_ref[...], pg_ref[...]
        d = _window_mean(_window_sum(u, gidx, False), gidx) - u
        mixed = _dot(_bf(d), w_ref[...].reshape(GROUP, GROUP), 1, 0)
        y_ref[...] = _bf(mixed * sc_ref[...] * (pg * _sigmoid(pg)))

    return pl.pallas_call(
        body, name="pool_fwd", grid=(NGROUP,),
        in_specs=[*_POOL_SPECS, _POOL_W_SPEC, pl.BlockSpec((1, GROUP), lambda g: (0, g)), ANY],
        out_specs=pl.BlockSpec((T, GROUP), lambda g: (0, g)),
        out_shape=pltpu.HBM((T, DMIX), jnp.bfloat16),
        compiler_params=_params(("parallel",)),
    )(proj, proj, pool_w, pool_scale, token)


def _tri(lower):
    r = lax.broadcasted_iota(jnp.int32, (CHUNK, CHUNK), 0)
    c = lax.broadcasted_iota(jnp.int32, (CHUNK, CHUNK), 1)
    return (r >= c) if lower else (r <= c)


def _sum_rows_matrix():
    shape = (CHUNK + 16, CHUNK)
    r, c = lax.broadcasted_iota(jnp.int32, shape, 0), lax.broadcasted_iota(jnp.int32, shape, 1)
    run = jnp.where(c <= r, 1.0, 0.0)
    half = jnp.where(c < CHUNK // 2, 1.0, 0.0)
    return _bf(jnp.where(r < CHUNK, run, jnp.where(r < CHUNK + 8, 1.0, half)))


def _rev_sum_matrix():
    shape = (CHUNK, 2 * CHUNK)
    r, c = lax.broadcasted_iota(jnp.int32, shape, 0), lax.broadcasted_iota(jnp.int32, shape, 1)
    return _bf(jnp.where(c < CHUNK, jnp.where(c >= r, 1.0, 0.0), jnp.where(c - CHUNK < r, 1.0, 0.0)))


def _split2(a):
    hi = _bf(a)
    return [hi, _bf(a - hi.astype(jnp.float32))]


def _exact_sums(mat, pieces):
    x = jnp.concatenate([s for p in pieces for s in _split2(p)], axis=1)
    r = _dot(mat, x, 1, 0)
    return [r[:, 2 * j * HEAD:(2 * j + 1) * HEAD] + r[:, (2 * j + 1) * HEAD:(2 * j + 2) * HEAD]
            for j in range(len(pieces))]


def _gates(qv, fl, lb):
    sq = _sigmoid(qv)
    sg = _sigmoid(fl)
    f = lb + (1.0 - lb) * sg
    return dict(sq=sq, qs=qv * sq, sg=sg, f=f, kk=1.0 - f, g=jnp.log2(f))


def _decays(sums):
    big_g = sums[:CHUNK]
    total = sums[CHUNK:CHUNK + 8]
    g_last = jnp.tile(total, (CHUNK // 8, 1))
    g_mid = jnp.tile(sums[CHUNK + 8:], (CHUNK // 8, 1))
    return dict(
        e_q=jnp.exp2(big_g),
        e_k=jnp.exp2(g_last - big_g),
        e_qm=jnp.exp2(jnp.minimum(big_g - g_mid, EXP_CAP)),
        e_km=jnp.exp2(jnp.minimum(g_mid - big_g, EXP_CAP)),
        total8=jnp.exp2(total))


def _group_rows(gi):
    return [pl.ds(pl.multiple_of((gi * NB + j) * CHUNK, CHUNK), CHUNK) for j in range(NB)]


def _lower_bound(lb_ref):
    return _sigmoid(lb_ref[0:1, :] - lb_ref[1:2, :])


def _hgrn_fwd(proj, lb_logits, rec_g, y_in):
    def body(q_ref, f_ref, i_ref, gate_ref, lb_ref, rg_ref, y_any, y_ref, o_ref, st_ref):
        del y_any
        lb = _lower_bound(lb_ref)
        causal = _tri(True)
        smat = _sum_rows_matrix()

        def group(gi, st):
            rows = _group_rows(gi)
            ts = [_gates(q_ref[r, :], f_ref[r, :], lb) for r in rows]
            ds = [_decays(s) for s in _exact_sums(smat, [t["g"] for t in ts])]
            vs = [_bf(i_ref[r, :]) for r in rows]
            q_m = [_bf(t["qs"] * d["e_qm"]) for t, d in zip(ts, ds)]
            k_m = [_bf(t["kk"] * d["e_km"]) for t, d in zip(ts, ds)]
            q_e = [_bf(t["qs"] * d["e_q"]) for t, d in zip(ts, ds)]
            k_e = [_bf(t["kk"] * d["e_k"]) for t, d in zip(ts, ds)]
            a = [_bf(jnp.where(causal, _dot(q_m[j], k_m[j], 1, 1), 0.0)) for j in range(NB)]
            intra = [_dot(a[j], vs[j], 1, 0) for j in range(NB)]
            upd = [_dot(vs[j], k_e[j], 0, 0) for j in range(NB)]
            for j in range(NB):
                st_ref[gi * NB + j] = st
                o_ref[rows[j], :] = intra[j] + _dot(q_e[j], _bf(st), 1, 1)
                st = st * jnp.tile(ds[j]["total8"], (HEAD // 8, 1)) + upd[j]
            return st

        lax.fori_loop(0, NGRP, group, jnp.zeros((HEAD, HEAD), jnp.float32))
        o = o_ref[...]
        rn = o * lax.rsqrt(jnp.mean(o * o, axis=-1, keepdims=True) + EPS)
        gate = gate_ref[...]
        y_ref[...] = _bf(rn * rg_ref[...] * (gate * _sigmoid(gate)))

    return pl.pallas_call(
        body, name="hgrn_fwd", grid=(NHEAD,),
        in_specs=[*_HEAD_SPECS,
                  pl.BlockSpec((2, HEAD), lambda h: (0, h)),
                  pl.BlockSpec((1, HEAD), lambda h: (0, h)),
                  pl.BlockSpec(memory_space=pl.ANY)],
        out_specs=(pl.BlockSpec((T, HEAD), lambda h: (0, NHEAD + h)),
                   pl.BlockSpec((T, HEAD), lambda h: (0, h)),
                   pl.BlockSpec((None, NCHUNK, HEAD, HEAD), lambda h: (h, 0, 0, 0))),
        out_shape=(pltpu.HBM((T, DMIX), jnp.bfloat16), pltpu.HBM((T, D), jnp.float32),
                   pltpu.HBM((NHEAD, NCHUNK, HEAD, HEAD), jnp.float32)),
        input_output_aliases={6: 0},
        compiler_params=_params(("parallel",)),
    )(proj, proj, proj, proj, lb_logits, rec_g, y_in)


def _out_proj_loss(x, y, w_out, target, gf):
    rows = 512
    parts = [slice(k * rows // 2, (k + 1) * rows // 2) for k in range(2)]

    def body(x_ref, y_ref, w_ref, t_ref, g_ref, dz_ref, dzb_ref, sq_ref, dg_ref):
        zs = [x_ref[p, :] + _dot(y_ref[p, :], w_ref[...], 1, 0) for p in parts]
        sq = dg = 0.0
        for p, z in zip(parts, zs):
            r = lax.rsqrt(jnp.mean(z * z, axis=-1, keepdims=True) + EPS)
            zhat = z * r
            err = zhat * g_ref[...] - t_ref[p, :]
            dy = err * (1.0 / D)
            gdy = dy * g_ref[...]
            dz = r * (gdy - zhat * jnp.mean(zhat * gdy, axis=-1, keepdims=True))
            dz_ref[p, :] = dz
            dzb_ref[p, :] = _bf(dz)
            sq = sq + jnp.sum(err * err, axis=0, keepdims=True)
            dg = dg + jnp.sum(zhat * dy, axis=0, keepdims=True)

        @pl.when(pl.program_id(0) == 0)
        def _():
            sq_ref[...] = sq
            dg_ref[...] = dg

        @pl.when(pl.program_id(0) != 0)
        def _():
            sq_ref[...] += sq
            dg_ref[...] += dg

    tile = pl.BlockSpec((rows, D), lambda i: (i, 0))
    vec = pl.BlockSpec((1, D), lambda i: (0, 0))
    return pl.pallas_call(
        body, name="out_proj_loss", grid=(T // rows,),
        in_specs=[tile, pl.BlockSpec((rows, DMIX), lambda i: (i, 0)), pl.BlockSpec((DMIX, D), lambda i: (0, 0)),
                  tile, vec],
        out_specs=(tile, tile, vec, vec),
        out_shape=(pltpu.HBM((T, D), jnp.float32), pltpu.HBM((T, D), jnp.bfloat16),
                   jax.ShapeDtypeStruct((1, D), jnp.float32), jax.ShapeDtypeStruct((1, D), jnp.float32)),
        compiler_params=_params(("arbitrary",)),
    )(x, y, w_out, target, gf)


def _out_proj_bwd(dzb, w_out, y):
    tn = 512

    def body(dz_ref, w_ref, y_ref, dy_ref, gw_ref, gwb_ref):
        dz = dz_ref[...]
        dy_ref[...] = _dot(dz, w_ref[...], 1, 1)
        gw = _dot(y_ref[...], dz, 0, 0)
        gw_ref[...] = gw
        gwb_ref[...] = _bf(gw)

    return pl.pallas_call(
        body, name="out_proj_bwd", grid=(DMIX // tn,),
        in_specs=[pl.BlockSpec((T, D), lambda n: (0, 0)), pl.BlockSpec((tn, D), lambda n: (n, 0)),
                  pl.BlockSpec((T, tn), lambda n: (0, n))],
        out_specs=(pl.BlockSpec((T, tn), lambda n: (0, n)), pl.BlockSpec((tn, D), lambda n: (n, 0)),
                   pl.BlockSpec((tn, D), lambda n: (n, 0))),
        out_shape=(pltpu.HBM((T, DMIX), jnp.float32), pltpu.HBM((DMIX, D), jnp.float32),
                   pltpu.HBM((DMIX, D), jnp.bfloat16)),
        compiler_params=_params(("parallel",)),
    )(dzb, w_out, y)


def _hgrn_bwd(proj, lb_logits, rec_g, o, states, dymix, dproj_in, token):
    def body(q_ref, f_ref, i_ref, gate_ref, lb_ref, rg_ref, o_ref, st_ref, dy_ref, dp_any, token_any,
             dp_ref, drg_ref, dlb_ref, do_ref):
        del dp_any, token_any
        lb = _lower_bound(lb_ref)
        causal = _tri(True)
        smat, rmat = _sum_rows_matrix(), _rev_sum_matrix()

        o = o_ref[...]
        rs = lax.rsqrt(jnp.mean(o * o, axis=-1, keepdims=True) + EPS)
        rn = o * rs
        gate = gate_ref[...]
        sgate = _sigmoid(gate)
        dyv = dy_ref[...]
        d_r = dyv * (gate * sgate)
        dp_ref[3] = _bf(dyv * (rn * rg_ref[...]) * (sgate * (1.0 + gate * (1.0 - sgate))))
        drg_ref[...] = jnp.sum(d_r * rn, axis=0, keepdims=True)
        drn = d_r * rg_ref[...]
        do_ref[...] = rs * (drn - rn * jnp.mean(rn * drn, axis=-1, keepdims=True))

        def group(i, carry):
            dst, dlb = carry
            gi = NGRP - 1 - i
            rows = _group_rows(gi)
            span = range(NB)
            qvs = [q_ref[r, :] for r in rows]
            ts = [_gates(qv, f_ref[r, :], lb) for qv, r in zip(qvs, rows)]
            ds = [_decays(s) for s in _exact_sums(smat, [t["g"] for t in ts])]
            vs = [_bf(i_ref[r, :]) for r in rows]
            dos = [_bf(do_ref[r, :]) for r in rows]
            sts = [st_ref[gi * NB + j] for j in span]
            qe_f = [t["qs"] * d["e_q"] for t, d in zip(ts, ds)]
            ke_f = [t["kk"] * d["e_k"] for t, d in zip(ts, ds)]
            q_e, k_e = [_bf(a) for a in qe_f], [_bf(a) for a in ke_f]
            q_m = [_bf(t["qs"] * d["e_qm"]) for t, d in zip(ts, ds)]
            k_m = [_bf(t["kk"] * d["e_km"]) for t, d in zip(ts, ds)]
            a = [_bf(jnp.where(causal, _dot(q_m[j], k_m[j], 1, 1), 0.0)) for j in span]
            da = [_bf(jnp.where(causal, _dot(dos[j], vs[j], 1, 1), 0.0)) for j in span]
            dqm = [_dot(da[j], k_m[j], 1, 0) for j in span]
            dkm = [_dot(da[j], q_m[j], 0, 0) for j in span]
            dv_in = [_dot(a[j], dos[j], 0, 0) for j in span]
            dqe = [_dot(dos[j], _bf(sts[j]), 1, 0) for j in span]
            grow = [_dot(dos[j], q_e[j], 0, 0) for j in span]
            dke, carried = [None] * NB, [None] * NB
            for j in reversed(span):
                dst_b = _bf(dst)
                dke[j] = _dot(vs[j], dst_b, 1, 0)
                dp_ref[2, rows[j], :] = _bf(dv_in[j] + _dot(k_e[j], dst_b, 1, 1))
                carried[j] = ds[j]["total8"] * jnp.sum(dst * sts[j], axis=0, keepdims=True)
                dst = dst * jnp.tile(ds[j]["total8"], (HEAD // 8, 1)) + grow[j]
            kdk = [ke_f[j] * dke[j] for j in span]
            pos = [(q_m[j].astype(jnp.float32) * dqm[j] - k_m[j].astype(jnp.float32) * dkm[j]) + qe_f[j] * dqe[j]
                   for j in span]
            dgs = _exact_sums(rmat, [jnp.concatenate([pos[j], kdk[j]], axis=0) for j in span])
            for j in span:
                t, d = ts[j], ds[j]
                dg = dgs[j] + jnp.tile(carried[j], (CHUNK // 8, 1))
                dqs = dqm[j] * d["e_qm"] + dqe[j] * d["e_q"]
                dkk = dkm[j] * d["e_km"] + dke[j] * d["e_k"]
                df = dg / t["f"] - dkk
                dp_ref[1, rows[j], :] = _bf(df * (1.0 - lb) * (t["sg"] * (1.0 - t["sg"])))
                dp_ref[0, rows[j], :] = _bf(dqs * (t["sq"] * (1.0 + qvs[j] * (1.0 - t["sq"]))))
                dlb = dlb + df * (1.0 - t["sg"])
            return dst, dlb

        _, dlb = lax.fori_loop(0, NGRP, group, (jnp.zeros((HEAD, HEAD), jnp.float32),
                                                jnp.zeros((CHUNK, HEAD), jnp.float32)))
        dlb_ref[...] = jnp.sum(dlb, axis=0, keepdims=True)

    vec = pl.BlockSpec((1, HEAD), lambda h: (0, h))
    return pl.pallas_call(
        body, name="hgrn_bwd", grid=(NHEAD,),
        in_specs=[*_HEAD_SPECS,
                  pl.BlockSpec((2, HEAD), lambda h: (0, h)), vec,
                  pl.BlockSpec((T, HEAD), lambda h: (0, h)),
                  pl.BlockSpec((None, NCHUNK, HEAD, HEAD), lambda h: (h, 0, 0, 0)),
                  pl.BlockSpec((T, HEAD), lambda h: (0, NHEAD + h)), ANY, ANY],
        out_specs=(pl.BlockSpec((4, T, HEAD), lambda h: (0, 0, h)), vec, vec),
        out_shape=(pltpu.HBM((NSEG, T, D), jnp.bfloat16),
                   jax.ShapeDtypeStruct((1, D), jnp.float32), jax.ShapeDtypeStruct((1, D), jnp.float32)),
        scratch_shapes=[pltpu.VMEM((T, HEAD), jnp.float32)],
        input_output_aliases={9: 0},
        compiler_params=_params(("parallel",)),
    )(proj, proj, proj, proj, lb_logits, rec_g, o, states, dymix, dproj_in, token)


def _pool_bwd(proj, pool_w, pool_scale, dymix):
    def body(u_ref, pg_ref, w_ref, sc_ref, dy_ref, dp_ref, gw_ref, gs_ref):
        gidx = pl.program_id(0)
        u, pg = u_ref[...], pg_ref[...]
        w = w_ref[...].reshape(GROUP, GROUP)
        d = _bf(_window_mean(_window_sum(u, gidx, False), gidx) - u)
        mixed = _dot(d, w, 1, 0)
        spg = _sigmoid(pg)
        dyv = dy_ref[...]
        d_p = dyv * (pg * spg)
        dp_ref[1] = _bf(dyv * (mixed * sc_ref[...]) * (spg * (1.0 + pg * (1.0 - spg))))
        gs_ref[...] = jnp.sum(d_p * mixed, axis=0, keepdims=True)
        dmixed = _bf(d_p * sc_ref[...])
        gw_ref[...] = _dot(d, dmixed, 0, 0).reshape(gw_ref.shape)
        dd = _dot(dmixed, w, 1, 1)
        dp_ref[0] = _bf(_window_sum(_window_mean(dd, gidx), gidx, True) - dd)

    return pl.pallas_call(
        body, name="pool_bwd", grid=(NGROUP,),
        in_specs=[*_POOL_SPECS, _POOL_W_SPEC,
                  pl.BlockSpec((1, GROUP), lambda g: (0, g)),
                  pl.BlockSpec((T, GROUP), lambda g: (0, g))],
        out_specs=(pl.BlockSpec((2, T, GROUP), lambda g: (2, 0, g)), _POOL_W_SPEC,
                   pl.BlockSpec((1, GROUP), lambda g: (0, g))),
        out_shape=(pltpu.HBM((NSEG, T, D), jnp.bfloat16),
                   jax.ShapeDtypeStruct((NDEV, NGROUP, GROUP // NDEV, GROUP), jnp.float32),
                   jax.ShapeDtypeStruct((1, D), jnp.float32)),
        compiler_params=_params(("parallel",)),
    )(proj, proj, pool_w, pool_scale, dymix)


HALF = NTILE // 2
AWAY = HALF - 3


def _dproj_tile(chip, side, p):
    j = 6 * chip + 3 * side + p
    return ((j // 4 + 4) % NSEG, 0, j % 4)


def _sibling_copy(sib_out, sib_in, send_sems, recv_sems, slot):
    x, y, c, _ = _place()
    return pltpu.make_async_remote_copy(
        src_ref=sib_out.at[slot], dst_ref=sib_in.at[slot], send_sem=send_sems.at[slot],
        recv_sem=recv_sems.at[slot], device_id=(x, y, 1 - c), device_id_type=MESH)


def _proj_bwd_w_far(place, ht, dproj):
    def body(place_ref, h_ref, dp_ref, sib_out, sib_in, send_sems, recv_sems, stage, loc_sems):
        del place_ref
        i = pl.program_id(0)

        def to_hbm(k):
            return pltpu.make_async_copy(stage.at[k], sib_out.at[k], loc_sems.at[k])

        def send(k):
            to_hbm(k).wait()
            _sibling_copy(sib_out, sib_in, send_sems, recv_sems, k).start()

        stage[i] = _bf(_dot(h_ref[...], dp_ref[...], 1, 0))

        @pl.when(i > 0)
        def _():
            send(i - 1)

        to_hbm(i).start()

        @pl.when(i == HALF - 1)
        def _():
            send(i)

    buf = pltpu.HBM((HALF, D, TILE), jnp.bfloat16)
    sems = pltpu.SemaphoreType.DMA((HALF,))
    return pl.pallas_call(
        body, name="proj_bwd_w_far",
        grid_spec=pltpu.PrefetchScalarGridSpec(
            num_scalar_prefetch=1, grid=(HALF,),
            in_specs=[pl.BlockSpec((D, T), lambda i, pr: (0, 0)),
                      pl.BlockSpec((None, T, TILE), lambda i, pr: _dproj_tile(i // 3, 1 - pr[2], i % 3))],
            out_specs=(HBM, HBM, SEM, SEM),
            scratch_shapes=[pltpu.VMEM((HALF, D, TILE), jnp.bfloat16), pltpu.SemaphoreType.DMA((HALF,))]),
        out_shape=(buf, buf, sems, sems),
        compiler_params=pltpu.CompilerParams(dimension_semantics=("arbitrary",), vmem_limit_bytes=48 * MIB,
                                             has_side_effects=EFFECT),
    )(place, ht, dproj)


def _proj_bwd_w_near(place, ht, dproj, sib_out, sib_in, sib_send, sib_recv):
    def owner_chip(k, pr):
        return jnp.where(k < AWAY, (pr[1] + 1 + k % 3) % 4, pr[1])

    def tile_p(k):
        return jnp.where(k < AWAY, k // 3, k - AWAY)

    def body(place_ref, h_ref, dp_ref, sib_out, sib_in, sib_send, sib_recv, sums, own_ref, landing, out_send,
             out_recv, recvbuf, outbuf, in_sems, loc_sems):
        i = pl.program_id(0)
        px, py, c, _ = _place()

        def slot_of(k):
            return 3 * owner_chip(k, place_ref) + tile_p(k)

        def load(k):
            return pltpu.make_async_copy(sib_in.at[slot_of(k)], recvbuf.at[k % 2], in_sems.at[k % 2])

        def fetch(k):
            _sibling_copy(sib_out, sib_in, sib_send, sib_recv, slot_of(k)).wait_recv()
            load(k).start()

        def route(k):
            chip = owner_chip(k, place_ref)
            cx, cy = chip // 2, chip % 2
            return cx, cy, (cx ^ px) + 2 * (cy ^ py) - 1, tile_p(k)

        def to_hbm(k):
            _, _, rel, p = route(k)
            return pltpu.make_async_copy(outbuf.at[k], sums.at[rel, p], loc_sems.at[k])

        def to_owner(k):
            cx, cy, rel, p = route(k)
            return pltpu.make_async_remote_copy(
                src_ref=sums.at[rel, p], dst_ref=landing.at[rel, p], send_sem=out_send.at[3 * rel + p],
                recv_sem=out_recv.at[3 * rel + p], device_id=(cx, cy, c), device_id_type=MESH)

        @pl.when(i == 0)
        def _():
            fetch(i)

        @pl.when(i < HALF - 1)
        def _():
            fetch(i + 1)

        load(i).wait()
        total = _dot(h_ref[...], dp_ref[...], 1, 0) + recvbuf[i % 2].astype(jnp.float32)
        own_ref[...] = total
        outbuf[jnp.minimum(i, AWAY)] = _bf(total)

        @pl.when(i < AWAY)
        def _():
            to_hbm(i).start()

        @pl.when(jnp.logical_and(i > 0, i <= AWAY))
        def _():
            to_hbm(i - 1).wait()
            to_owner(i - 1).start()

        @pl.when(i == HALF - 1)
        def _():
            for slot in range(HALF):
                _sibling_copy(sib_out, sib_in, sib_send, sib_recv, slot).wait_send()

    travelling = pltpu.HBM((3, 3, D, TILE), jnp.bfloat16)
    sems = pltpu.SemaphoreType.DMA((AWAY,))
    return pl.pallas_call(
        body, name="proj_bwd_w_near",
        grid_spec=pltpu.PrefetchScalarGridSpec(
            num_scalar_prefetch=1, grid=(HALF,),
            in_specs=[pl.BlockSpec((D, T), lambda i, pr: (0, 0)),
                      pl.BlockSpec((None, T, TILE), lambda i, pr: _dproj_tile(owner_chip(i, pr), pr[2], tile_p(i))),
                      HBM, HBM, SEM, SEM],
            out_specs=(HBM, pl.BlockSpec((None, D, TILE), lambda i, pr: (jnp.where(i < AWAY, 0, i % 3), 0, 0)),
                       HBM, SEM, SEM),
            scratch_shapes=[pltpu.VMEM((2, D, TILE), jnp.bfloat16), pltpu.VMEM((AWAY + 1, D, TILE), jnp.bfloat16),
                            pltpu.SemaphoreType.DMA((2,)), pltpu.SemaphoreType.DMA((AWAY,))]),
        out_shape=(travelling, pltpu.HBM((3, D, TILE), jnp.float32), travelling, sems, sems),
        compiler_params=pltpu.CompilerParams(dimension_semantics=("arbitrary",), vmem_limit_bytes=48 * MIB,
                                             has_side_effects=EFFECT),
    )(place, ht, dproj, sib_out, sib_in, sib_send, sib_recv)


def _proj_bwd_x(dproj, w_t, x, g1, dz, others, token):
    tm = 512
    pairs = NSEG // 2
    k = len(others)

    def body(dp_ref, w_ref, x_ref, g_ref, dz_ref, *refs):
        dx_ref, dg_ref, wcat, acc = refs[k + 1:]
        m, s = pl.program_id(0), pl.program_id(1)

        @pl.when(m == 0)
        def _():
            for i in range(8):
                wcat[s, :, i * TILE:(i + 1) * TILE] = w_ref[i]

        @pl.when(s == 0)
        def _():
            acc[...] = jnp.zeros((tm, D), jnp.float32)

        acc[...] += _dot(jnp.concatenate([dp_ref[0], dp_ref[1]], axis=1), wcat[s], 1, 1)

        @pl.when(s == pairs - 1)
        def _():
            xv = x_ref[...]
            rs = lax.rsqrt(jnp.mean(xv * xv, axis=-1, keepdims=True) + EPS)
            xhat = xv * rs
            dhv = acc[...]
            gdh = dhv * g_ref[...]
            dx_ref[...] = dz_ref[...] + rs * (gdh - xhat * jnp.mean(xhat * gdh, axis=-1, keepdims=True))
            dg = jnp.sum(xhat * dhv, axis=0, keepdims=True)

            @pl.when(m == 0)
            def _():
                dg_ref[0:1, :] = dg
                for r, ref in enumerate(refs[:k]):
                    dg_ref[1 + r:2 + r, :] = ref[...]
                dg_ref[1 + k:, :] = jnp.zeros((7 - k, D), jnp.float32)

            @pl.when(m != 0)
            def _():
                dg_ref[0:1, :] += dg

    rows = pl.BlockSpec((tm, D), lambda m, s: (m, 0))
    vec = pl.BlockSpec((1, D), lambda m, s: (0, 0))
    return pl.pallas_call(
        body, name="proj_bwd_x", grid=(T // tm, pairs),
        in_specs=[pl.BlockSpec((2, tm, D), lambda m, s: (s, m, 0)),
                  pl.BlockSpec((8, D, TILE), lambda m, s: ((jnp.where(m == 0, s, pairs - 1) + 1) % pairs, 0, 0)),
                  rows, vec, rows, *[vec] * k, ANY],
        out_specs=(rows, pl.BlockSpec((8, D), lambda m, s: (0, 0))),
        out_shape=(jax.ShapeDtypeStruct((T, D), jnp.float32), pltpu.HBM((8, D), jnp.float32)),
        scratch_shapes=[pltpu.VMEM((pairs, D, 2 * D), jnp.bfloat16), pltpu.VMEM((tm, D), jnp.float32)],
        compiler_params=_params(("arbitrary", "arbitrary"), vmem_mib=56),
    )(dproj, w_t, x, g1, dz, *others, token)


def _adamw(w, g, m, v):
    m_new = ADAM_B1 * m + (1.0 - ADAM_B1) * g
    v_new = ADAM_B2 * v + (1.0 - ADAM_B2) * (g * g)
    delta = -ADAM_LR * ((m_new / BC1) / (jnp.sqrt(v_new / BC2) + ADAM_EPS) + ADAM_WD * w)
    return delta, m_new, v_new


def _reduce_adam(name, place, parts, w, m, v, grid, w_spec):
    n = len(parts)

    def body(place_ref, *refs):
        del place_ref
        w_ref, m_ref, v_ref, g_ref, d_ref, mo_ref, vo_ref = refs[n:]
        g = None
        for ref, (_, _, stacked) in zip(refs[:n], parts):
            terms = [ref[r] for r in range(ref.shape[0])] if stacked else [ref[...]]
            for t in terms:
                if t.shape[-1] != w_ref.shape[-1]:
                    t = jnp.concatenate([t[p] for p in range(t.shape[0])], axis=1)
                g = t.astype(jnp.float32) if g is None else g + t.astype(jnp.float32)
        delta, m_new, v_new = _adamw(w_ref[...], g, m_ref[...], v_ref[...])
        g_ref[...] = g
        d_ref[...] = delta
        mo_ref[...] = m_new
        vo_ref[...] = v_new

    shape = jax.ShapeDtypeStruct(w.shape, jnp.float32)
    return pl.pallas_call(
        body, name=name,
        grid_spec=pltpu.PrefetchScalarGridSpec(
            num_scalar_prefetch=1, grid=grid,
            in_specs=[spec for _, spec, _ in parts] + [w_spec] * 3, out_specs=(w_spec,) * 4),
        out_shape=(shape,) * 4,
        compiler_params=_params(("parallel",)),
    )(place, *[_in_hbm(a) for a in [a for a, _, _ in parts] + [w, m, v]])


SMALL_ROWS = (1, 1, 2, 1, 1)


def _small_adam(place, own, parts, ws, ms, vs):
    n = len(SMALL_ROWS)

    def body(place_ref, own_ref, p_ref, *refs):
        ins, outs, bufs = refs[:3 * n], refs[3 * n:3 * n + 1 + 4 * n], refs[3 * n + 1 + 4 * n:]

        def stacked(group, buf):
            r0 = 0
            for ref, k in zip(group, SMALL_ROWS):
                buf[r0:r0 + k, :] = ref[...]
                r0 += k
            buf[r0:, :] = jnp.zeros((8 - r0, D), jnp.float32)
            return buf[...]

        wv, mv, vv = (stacked(ins[n * j:n * j + n], bufs[j]) for j in range(3))
        me = place_ref[0]
        g = None
        for s in range(NDEV):
            term = jnp.where(me == s, own_ref[...], p_ref[s])
            g = term if g is None else g + term
        rows = _row_ids(wv.shape)
        other = jnp.where(rows == 2, pltpu.roll(wv, 7, 0), jnp.where(rows == 3, pltpu.roll(wv, 1, 0), 0.0))
        lbv = _sigmoid(wv - other)
        sign = jnp.where(rows == 2, 1.0, -1.0)
        g = jnp.where((rows == 2) | (rows == 3), sign * g * lbv * (1.0 - lbv), g)
        delta, m_new, v_new = _adamw(wv, g, mv, vv)
        outs[0][...] = jnp.sum(g[6:7], axis=1, keepdims=True) * (0.5 / D)
        for j, val in enumerate((g, delta, m_new, v_new)):
            r0 = 0
            for ref, k in zip(outs[1 + n * j:1 + n * j + n], SMALL_ROWS):
                ref[...] = val[r0:r0 + k]
                r0 += k

    vmem = pl.BlockSpec(memory_space=pltpu.VMEM)
    shapes = [jax.ShapeDtypeStruct((k, D), jnp.float32) for k in SMALL_ROWS]
    out = pl.pallas_call(
        body, name="small_adam", out_shape=(jax.ShapeDtypeStruct((1, 1), jnp.float32), *shapes * 4),
        in_specs=[pl.BlockSpec(memory_space=pltpu.SMEM)] + [vmem] * (2 + 3 * n), out_specs=(vmem,) * (1 + 4 * n),
        scratch_shapes=[pltpu.VMEM((8, D), jnp.float32)] * 3,
    )(place, own, parts, *ws, *ms, *vs)
    return out[0], [out[1 + n * j:1 + n * j + n] for j in range(4)]


def kernel(x, norm1_g, w_in, pool_w, pool_scale, lb_logits, rec_norm_g, w_out, final_norm_g, loss_target, m_norm1_g, m_w_in, m_pool_w, m_pool_scale, m_lb_logits, m_rec_norm_g, m_w_out, m_final_norm_g, v_norm1_g, v_w_in, v_pool_w, v_pool_scale, v_lb_logits, v_rec_norm_g, v_w_out, v_final_norm_g):
    xs = x[0]
    target = loss_target[0]
    ix, iy, ic = lax.axis_index("x"), lax.axis_index("y"), lax.axis_index("c")
    place = jnp.stack([4 * ix + 2 * iy + ic, 2 * ix + iy, ic]).astype(jnp.int32)
    gf = final_norm_g.reshape(1, D)

    ht, w_t, w_out_b, w_out_g, pool_g, proj = _gather_proj(xs, norm1_g, w_in, w_out, pool_w)
    wout = [w_out_b, w_out_g]
    wout_send, wout_recv, wout, wout_token = _split_start("gather_wout_start", wout, NDEV - 1, _plan_wout)

    y = _pool_fwd(proj, pool_g, pool_scale, wout_token)
    y, o, states = _hgrn_fwd(proj, lb_logits, rec_norm_g, y)
    _, w_out_g = _split_wait("gather_wout_wait", wout, wout_send, wout_recv, _plan_wout, o)
    w_out_full = _in_hbm(w_out_g.reshape(DMIX, D))
    dz, dzb, sq, dgf = _out_proj_loss(xs, y, w_out_full, target, gf)

    dymix, gwout_f, gwout_b = _out_proj_bwd(dzb, w_out_full, y)
    dproj, gpool, dscale = _pool_bwd(proj, pool_g, pool_scale, dymix)

    blk_out = (NDEV, DMIX // NDEV, D)
    blk_pool = (NDEV, NGROUP, GROUP // NDEV, GROUP)
    rest = [gwout_b.reshape(blk_out), gpool,
            lax.empty((NDEV - 1,) + blk_out[1:], jnp.bfloat16), lax.empty((NDEV - 1,) + blk_pool[1:], jnp.float32)]
    rest_send, rest_recv, rest, rest_token = _split_start("scatter_rest_start", rest, 2 * (NDEV - 1), _plan_rest)

    dproj, drecg, dlb = _hgrn_bwd(proj, lb_logits, rec_norm_g, o, states, dymix, dproj, rest_token)
    chip_sums, own_sum, landing, win_send, win_recv = _proj_bwd_w_near(
        place, ht, dproj, *_proj_bwd_w_far(place, ht, dproj))
    win = [chip_sums, landing]

    others = [a.reshape(1, D) for a in (dscale, dlb, dlb, drecg, dgf, sq)]
    grad_x, small_block = _proj_bwd_x(dproj, w_t, xs, norm1_g, dz, others, chip_sums)

    small = [small_block, lax.empty((NDEV, 8, D), jnp.float32)]
    small_send, small_recv, small, small_token = _split_start("gather_small_start", small, NDEV - 1, _plan_small)

    _, gpool_own, r_out, r_pool = _split_wait("scatter_rest_wait", rest, rest_send, rest_recv, _plan_rest,
                                              small_token)
    g_wout, d_wout, m_wout, v_wout = _reduce_adam(
        "adam_w_out", place,
        [(gwout_f.reshape(blk_out), pl.BlockSpec((None,) + blk_out[1:], lambda i, pr: (pr[0], 0, 0)), False),
         (r_out, pl.BlockSpec((NDEV - 1,) + blk_out[1:], lambda i, pr: (0, 0, 0)), True)],
        w_out, m_w_out, v_w_out, (1,), pl.BlockSpec((None,) + blk_out[1:], lambda i, pr: (0, 0, 0)))
    g_pool, d_pool, m_pool, v_pool = _reduce_adam(
        "adam_pool_w", place,
        [(gpool_own, pl.BlockSpec((None,) + blk_pool[1:], lambda i, pr: (pr[0], 0, 0, 0)), False),
         (r_pool, pl.BlockSpec((NDEV - 1,) + blk_pool[1:], lambda i, pr: (0, 0, 0, 0)), True)],
        pool_w, m_pool_w, v_pool_w, (1,), pl.BlockSpec((None,) + blk_pool[1:], lambda i, pr: (0, 0, 0, 0)))

    _, r_in = _split_wait("scatter_win_wait", win, win_send, win_recv, _plan_in, d_pool)
    g_win, d_win, m_win, v_win = _reduce_adam(
        "adam_w_in", place,
        [(own_sum, pl.BlockSpec((3, D // 8, TILE), lambda i, pr: (0, i, 0)), False),
         (r_in, pl.BlockSpec((3, 3, D // 8, TILE), lambda i, pr: (0, 0, i, 0)), True)],
        w_in, m_w_in, v_w_in, (8,), pl.BlockSpec((None, D // 8, 3 * TILE), lambda i, pr: (0, i, 0)))

    own_small, r_small = _split_wait("gather_small_wait", small, small_send, small_recv, _plan_small, d_win)
    loss, (g_s, d_s, m_s, v_s) = _small_adam(
        place, own_small, r_small,
        (norm1_g, pool_scale, lb_logits, rec_norm_g, gf),
        (m_norm1_g, m_pool_scale, m_lb_logits, m_rec_norm_g, m_final_norm_g.reshape(1, D)),
        (v_norm1_g, v_pool_scale, v_lb_logits, v_rec_norm_g, v_final_norm_g.reshape(1, D)))

    def outs(small, win, pool, wout):
        n1, ps, lbl, rg, fg = small
        return n1, win, pool, ps, lbl, rg, wout, fg.reshape(D)

    return (loss.reshape(()), grad_x[None],
            *outs(g_s, g_win, g_pool, g_wout), *outs(d_s, d_win, d_pool, d_wout),
            *outs(m_s, m_win, m_pool, m_wout), *outs(v_s, v_win, v_pool, v_wout))
```

```python
import functools

import jax
import jax.numpy as jnp
from jax import lax
from jax.experimental import pallas as pl
from jax.experimental.pallas import tpu as pltpu

T = 2048
D = 1024
NSEG = 6
NTILE = 24
TILE = 256
DMIX = 2048
NDEV = 8
HEAD = 128
NHEAD = 8
CHUNK = 64
NCHUNK = T // CHUNK
NB = 32
NGRP = NCHUNK // NB
NGROUP = 4
GROUP = 256
EPS = 1e-6
EXP_CAP = 115.0
MESH = pl.DeviceIdType.MESH
AXES = ("x", "y", "c")
ANY = pl.BlockSpec(memory_space=pl.ANY)
HBM = pl.BlockSpec(memory_space=pltpu.HBM)
SEM = pl.BlockSpec(memory_space=pltpu.SEMAPHORE)
EFFECT = pltpu.SideEffectType.DATAFLOW_SIDE_EFFECTING

ADAM_LR = 0.001
ADAM_B1 = 0.9
ADAM_B2 = 0.999
ADAM_EPS = 1e-08
ADAM_WD = 0.01
ADAM_STEP = 10
BC1 = 1.0 - ADAM_B1 ** ADAM_STEP
BC2 = 1.0 - ADAM_B2 ** ADAM_STEP

MIB = 1 << 20


def _params(sem=None, vmem_mib=48):
    return pltpu.CompilerParams(dimension_semantics=sem, vmem_limit_bytes=vmem_mib * MIB)


def _sigmoid(v):
    return 1.0 / (1.0 + jnp.exp(-v))


def _dot(a, b, ca, cb, precision=None):
    return lax.dot_general(a, b, (((ca,), (cb,)), ((), ())), precision=precision,
                           preferred_element_type=jnp.float32)


def _bf(v):
    return v.astype(jnp.bfloat16)


def _in_hbm(a):
    return pltpu.with_memory_space_constraint(a, pltpu.HBM)


def _place():
    x, y, c = lax.axis_index("x"), lax.axis_index("y"), lax.axis_index("c")
    return x, y, c, 4 * x + 2 * y + c


def _peer(x, y, c, r):
    return (x ^ ((r >> 2) & 1), y ^ ((r >> 1) & 1), c ^ (r & 1))


def _gather_proj(x, g1, w_in, w_out, pool_w):
    def body(x_ref, g_ref, win_ref, wout_ref, pool_ref, ht_o, wt_o, woutb_o, wout_o, pool_o, proj_o,
             xbuf, hv, htv, wv, wob, pb, stage, send_sems, recv_sems, loc_sems, out_sems):
        px, py, c, my_idx = _place()
        fetch_x = pltpu.make_async_copy(x_ref, xbuf, loc_sems.at[5])
        fetch_x.start()
        me, sibling = (px, py, c), (px, py, 1 - c)
        chips = [(1 - px, py), (px, 1 - py), (1 - px, 1 - py)]
        for p in range(3):
            wv[3 * my_idx + p] = _bf(win_ref[0, :, p * TILE:(p + 1) * TILE])

        def index(bx, by, bc):
            return 4 * bx + 2 * by + bc

        def slot(w, block):
            return wv.at[pl.ds(3 * index(*block), 3)] if w == 0 else pool_o.at[index(*block)]

        def copy(k, w, block, to, src=None):
            return pltpu.make_async_remote_copy(
                src_ref=slot(w, block) if src is None else src, dst_ref=slot(w, block),
                send_sem=send_sems.at[2 * k + w], recv_sem=recv_sems.at[2 * k + w],
                device_id=to, device_id_type=MESH)

        def save(block):
            at = pl.ds(3 * index(*block), 3)
            pltpu.make_async_copy(wv.at[at], wt_o.at[at], loc_sems.at[4]).start()

        srcs = (slot(0, me), pb)
        first = []
        for w in (0, 1):
            if w == 1:
                pb[...] = _bf(pool_ref[0])
                wob[...] = _bf(wout_ref[0])
            group = [copy(1 + j, w, me, (*chip, c), src=srcs[w]) for j, chip in enumerate(chips[:2])]
            group.append(copy(0, w, me, sibling, src=srcs[w]))
            for cp in group:
                cp.start()
            first += group
        save(me)
        locs = [pltpu.make_async_copy(pb, slot(1, me), loc_sems.at[0]),
                pltpu.make_async_copy(wob, wout_o.at[my_idx], loc_sems.at[1]),
                pltpu.make_async_copy(wob, woutb_o, loc_sems.at[2])]
        for cp in locs:
            cp.start()

        fetch_x.wait()
        xv = xbuf[...]
        hv[...] = _bf(xv * lax.rsqrt(jnp.mean(xv * xv, axis=-1, keepdims=True) + EPS) * g_ref[...])
        rows = 256
        for r0 in range(0, T, rows):
            htv[:, r0:r0 + rows] = hv[r0:r0 + rows, :].T
        locs.append(pltpu.make_async_copy(htv, ht_o, loc_sems.at[3]))
        locs[-1].start()

        def out_copy(p, j):
            return pltpu.make_async_copy(stage.at[p], proj_o.at[j], out_sems.at[p])

        def project(nth, block):
            base = 3 * index(*block)

            def tile(p, carry):
                if nth > 0:
                    out_copy(p, base + p).wait()
                stage[p] = _dot(hv[...], wv[base + p], 1, 0)
                out_copy(p, base + p).start()
                return carry

            lax.fori_loop(0, 3, tile, 0)

        project(0, me)
        copy(0, 0, sibling, me).wait_recv()
        save(sibling)
        project(1, sibling)
        passed = []
        relay_from = (px ^ (1 - c), py ^ c, c)
        relay_to = (px ^ c, py ^ (1 - c), c)

        def arrived(w, j):
            copy(1 + j, w, (*chips[j], c), me).wait_recv()
            passed.append(copy(4 + j, w, (*chips[j], c), sibling))
            passed[-1].start()

        def relay(w):
            passed.append(copy(3, w, relay_from, relay_to))
            passed[-1].start()

        def handed(nth, j):
            copy(4 + j, 0, (*chips[j], 1 - c), me).wait_recv()
            save((*chips[j], 1 - c))
            project(nth, (*chips[j], 1 - c))

        arrived(0, 0)
        arrived(0, 1)
        relay(0)
        for j in range(2):
            save((*chips[j], c))
            project(2 + j, (*chips[j], c))
        handed(4, 0)
        handed(5, 1)
        arrived(1, 0)
        arrived(1, 1)
        relay(1)
        arrived(0, 2)
        save((*chips[2], c))
        project(6, (*chips[2], c))
        handed(7, 2)
        arrived(1, 2)
        copy(0, 1, sibling, me).wait_recv()
        for j, chip in enumerate(chips):
            copy(4 + j, 1, (*chip, 1 - c), me).wait_recv()
        keep = pltpu.make_async_copy(wv, wt_o, loc_sems.at[4])
        for p in range(3):
            out_copy(p, p).wait()
        for cp in first + passed:
            cp.wait_send()
        keep.wait()
        for cp in locs:
            cp.wait()

    vmem = pl.BlockSpec(memory_space=pltpu.VMEM)
    bf16 = jnp.bfloat16
    return pl.pallas_call(
        body, name="gather_proj",
        out_shape=(pltpu.HBM((D, T), bf16), pltpu.HBM((NTILE, D, TILE), bf16),
                   pltpu.HBM((DMIX // NDEV, D), bf16), pltpu.HBM((NDEV, DMIX // NDEV, D), bf16),
                   pltpu.HBM((NDEV, NGROUP, GROUP // NDEV, GROUP), bf16), pltpu.HBM((NTILE, T, TILE), jnp.float32)),
        in_specs=[ANY] + [vmem] * 4, out_specs=(ANY,) * 6,
        scratch_shapes=[pltpu.VMEM((T, D), jnp.float32),
                        pltpu.VMEM((T, D), bf16), pltpu.VMEM((D, T), bf16), pltpu.VMEM((NTILE, D, TILE), bf16),
                        pltpu.VMEM((DMIX // NDEV, D), bf16), pltpu.VMEM((NGROUP, GROUP // NDEV, GROUP), bf16),
                        pltpu.VMEM((3, T, TILE), jnp.float32),
                        pltpu.SemaphoreType.DMA((14,)), pltpu.SemaphoreType.DMA((14,)),
                        pltpu.SemaphoreType.DMA((6,)), pltpu.SemaphoreType.DMA((3,))],
        compiler_params=_params(vmem_mib=56),
    )(x, g1, w_in, w_out, pool_w)


def _split_start(name, arrays, n_copies, plan):
    k = len(arrays)

    def body(*refs):
        send_sems, recv_sems, token = refs[k], refs[k + 1], refs[-1]
        for i, (src, dst, to) in enumerate(plan(refs[:k])):
            pltpu.make_async_remote_copy(src_ref=src, dst_ref=dst, send_sem=send_sems.at[i],
                                         recv_sem=recv_sems.at[i], device_id=to, device_id_type=MESH).start()
        token[...] = jnp.zeros_like(token)

    out = pl.pallas_call(
        body, name=name,
        out_shape=(pltpu.SemaphoreType.DMA((n_copies,)), pltpu.SemaphoreType.DMA((n_copies,)),
                   *[pltpu.HBM(a.shape, a.dtype) for a in arrays], jax.ShapeDtypeStruct((8, 128), jnp.float32)),
        in_specs=[HBM] * k, out_specs=(SEM, SEM, *[HBM] * k, pl.BlockSpec(memory_space=pltpu.VMEM)),
        input_output_aliases={i: 2 + i for i in range(k)},
        compiler_params=pltpu.CompilerParams(has_side_effects=EFFECT),
    )(*[pltpu.with_memory_space_constraint(a, pltpu.HBM) for a in arrays])
    return out[0], out[1], out[2:2 + k], out[-1]


def _split_wait(name, arrays, send_sems, recv_sems, plan, after):
    k = len(arrays)

    def body(*refs):
        sends, recvs = refs[k], refs[k + 1]
        for i, (src, dst, to) in enumerate(plan(refs[:k])):
            cp = pltpu.make_async_remote_copy(src_ref=src, dst_ref=dst, send_sem=sends.at[i], recv_sem=recvs.at[i],
                                              device_id=to, device_id_type=MESH)
            cp.wait_send()
            cp.wait_recv()

    return pl.pallas_call(
        body, name=name,
        out_shape=tuple(pltpu.HBM(a.shape, a.dtype) for a in arrays),
        in_specs=[HBM] * k + [SEM, SEM, ANY], out_specs=(HBM,) * k,
        input_output_aliases={i: i for i in range(k)},
        compiler_params=pltpu.CompilerParams(has_side_effects=EFFECT),
    )(*arrays, send_sems, recv_sems, after)


def _plan_wout(refs):
    src, land = refs
    x, y, c, me = _place()
    return [(src, land.at[me], _peer(x, y, c, r)) for r in range(1, NDEV)]


def _plan_rest(refs):
    gob, gpf, r_out, r_pool = refs
    x, y, c, me = _place()
    plan = []
    for r in range(1, NDEV):
        plan.append((gob.at[me ^ r], r_out.at[r - 1], _peer(x, y, c, r)))
        plan.append((gpf.at[me ^ r], r_pool.at[r - 1], _peer(x, y, c, r)))
    return plan


def _plan_in(refs):
    sums, landing = refs
    x, y, c, _ = _place()
    plan = []
    for rel, (dx, dy) in enumerate(((1, 0), (0, 1), (1, 1))):
        for p in range(3):
            plan.append((sums.at[rel, p], landing.at[rel, p], (x ^ dx, y ^ dy, c)))
    return plan


def _plan_small(refs):
    small, land = refs
    x, y, c, me = _place()
    return [(small, land.at[me], _peer(x, y, c, r)) for r in range(1, NDEV)]


def _seg_tiles(s):
    return (s + 2) % NSEG


_POOL_SPECS = [pl.BlockSpec((None, T, GROUP), lambda g, base=base: (base + g, 0, 0)) for base in (0, 4)]
_HEAD_SPECS = [pl.BlockSpec((None, T, HEAD), lambda h, base=base: (base + h // 2, 0, h % 2))
               for base in (8, 12, 16, 20)]


_POOL_W_SPEC = pl.BlockSpec((NDEV, None, GROUP // NDEV, GROUP), lambda g: (0, g, 0, 0))


def _row_ids(shape):
    return lax.broadcasted_iota(jnp.int32, shape, 0)


BAND_ROWS = 128
HALO = 16


def _window_sum(a, gidx, lead):
    width = lax.shift_left(jnp.int32(2), gidx)
    shape = (BAND_ROWS, BAND_ROWS + HALO)
    t, j = lax.broadcasted_iota(jnp.int32, shape, 0), lax.broadcasted_iota(jnp.int32, shape, 1)
    first = t if lead else t + HALO - width + 1
    band = _bf(jnp.where(j >= first, jnp.where(j < first + width, 1.0, 0.0), 0.0))
    zeros = jnp.zeros((HALO, a.shape[1]), jnp.bfloat16)
    padded = [jnp.concatenate([p, zeros] if lead else [zeros, p], axis=0) for p in _split2(a)]
    out = []
    for r0 in range(0, T, BAND_ROWS):
        slab = jnp.concatenate([p[r0:r0 + BAND_ROWS + HALO] for p in padded], axis=1)
        r = _dot(band, slab, 1, 0)
        out.append(r[:, :a.shape[1]] + r[:, a.shape[1]:])
    return jnp.concatenate(out, axis=0)


def _window_mean(s, gidx):
    inv = jnp.where(gidx == 0, 0.5, jnp.where(gidx == 1, 0.25, jnp.where(gidx == 2, 0.125, 0.0625)))
    width = lax.shift_left(jnp.int32(2), gidx)
    head = s[:16] / jnp.minimum(_row_ids((16, s.shape[1])) + 1, width).astype(jnp.float32)
    return jnp.concatenate([head, s[16:] * inv], axis=0)


def _pool_fwd(proj, pool_w, pool_scale, token):
    def body(u_ref, pg_ref, w_ref, sc_ref, token_any, y_ref):
        del token_any
        gidx = pl.program_id(0)
        u, pg = u_ref[...], pg_ref[...]
        d = _window_mean(_window_sum(u, gidx, False), gidx) - u
        mixed = _dot(_bf(d), w_ref[...].reshape(GROUP, GROUP), 1, 0)
        y_ref[...] = _bf(mixed * sc_ref[...] * (pg * _sigmoid(pg)))

    return pl.pallas_call(
        body, name="pool_fwd", grid=(NGROUP,),
        in_specs=[*_POOL_SPECS, _POOL_W_SPEC, pl.BlockSpec((1, GROUP), lambda g: (0, g)), ANY],
        out_specs=pl.BlockSpec((T, GROUP), lambda g: (0, g)),
        out_shape=pltpu.HBM((T, DMIX), jnp.bfloat16),
        compiler_params=_params(("parallel",)),
    )(proj, proj, pool_w, pool_scale, token)


def _tri(lower):
    r = lax.broadcasted_iota(jnp.int32, (CHUNK, CHUNK), 0)
    c = lax.broadcasted_iota(jnp.int32, (CHUNK, CHUNK), 1)
    return (r >= c) if lower else (r <= c)


def _sum_rows_matrix():
    shape = (CHUNK + 16, CHUNK)
    r, c = lax.broadcasted_iota(jnp.int32, shape, 0), lax.broadcasted_iota(jnp.int32, shape, 1)
    run = jnp.where(c <= r, 1.0, 0.0)
    half = jnp.where(c < CHUNK // 2, 1.0, 0.0)
    return _bf(jnp.where(r < CHUNK, run, jnp.where(r < CHUNK + 8, 1.0, half)))


def _rev_sum_matrix():
    shape = (CHUNK, 2 * CHUNK)
    r, c = lax.broadcasted_iota(jnp.int32, shape, 0), lax.broadcasted_iota(jnp.int32, shape, 1)
    return _bf(jnp.where(c < CHUNK, jnp.where(c >= r, 1.0, 0.0), jnp.where(c - CHUNK < r, 1.0, 0.0)))


def _split2(a):
    hi = _bf(a)
    return [hi, _bf(a - hi.astype(jnp.float32))]


def _exact_sums(mat, pieces):
    x = jnp.concatenate([s for p in pieces for s in _split2(p)], axis=1)
    r = _dot(mat, x, 1, 0)
    return [r[:, 2 * j * HEAD:(2 * j + 1) * HEAD] + r[:, (2 * j + 1) * HEAD:(2 * j + 2) * HEAD]
            for j in range(len(pieces))]


def _gates(qv, fl, lb):
    sq = _sigmoid(qv)
    sg = _sigmoid(fl)
    f = lb + (1.0 - lb) * sg
    return dict(sq=sq, qs=qv * sq, sg=sg, f=f, kk=1.0 - f, g=jnp.log2(f))


def _decays(sums):
    big_g = sums[:CHUNK]
    total = sums[CHUNK:CHUNK + 8]
    g_last = jnp.tile(total, (CHUNK // 8, 1))
    g_mid = jnp.tile(sums[CHUNK + 8:], (CHUNK // 8, 1))
    return dict(
        e_q=jnp.exp2(big_g),
        e_k=jnp.exp2(g_last - big_g),
        e_qm=jnp.exp2(jnp.minimum(big_g - g_mid, EXP_CAP)),
        e_km=jnp.exp2(jnp.minimum(g_mid - big_g, EXP_CAP)),
        total8=jnp.exp2(total))


def _group_rows(gi):
    return [pl.ds(pl.multiple_of((gi * NB + j) * CHUNK, CHUNK), CHUNK) for j in range(NB)]


def _lower_bound(lb_ref):
    return _sigmoid(lb_ref[0:1, :] - lb_ref[1:2, :])


def _hgrn_fwd(proj, lb_logits, rec_g, y_in):
    def body(q_ref, f_ref, i_ref, gate_ref, lb_ref, rg_ref, y_any, y_ref, o_ref, st_ref):
        del y_any
        lb = _lower_bound(lb_ref)
        causal = _tri(True)
        smat = _sum_rows_matrix()

        def group(gi, st):
            rows = _group_rows(gi)
            ts = [_gates(q_ref[r, :], f_ref[r, :], lb) for r in rows]
            ds = [_decays(s) for s in _exact_sums(smat, [t["g"] for t in ts])]
            vs = [_bf(i_ref[r, :]) for r in rows]
            q_m = [_bf(t["qs"] * d["e_qm"]) for t, d in zip(ts, ds)]
            k_m = [_bf(t["kk"] * d["e_km"]) for t, d in zip(ts, ds)]
            q_e = [_bf(t["qs"] * d["e_q"]) for t, d in zip(ts, ds)]
            k_e = [_bf(t["kk"] * d["e_k"]) for t, d in zip(ts, ds)]
            a = [_bf(jnp.where(causal, _dot(q_m[j], k_m[j], 1, 1), 0.0)) for j in range(NB)]
            intra = [_dot(a[j], vs[j], 1, 0) for j in range(NB)]
            upd = [_dot(vs[j], k_e[j], 0, 0) for j in range(NB)]
            for j in range(NB):
                st_ref[gi * NB + j] = st
                o_ref[rows[j], :] = intra[j] + _dot(q_e[j], _bf(st), 1, 1)
                st = st * jnp.tile(ds[j]["total8"], (HEAD // 8, 1)) + upd[j]
            return st

        lax.fori_loop(0, NGRP, group, jnp.zeros((HEAD, HEAD), jnp.float32))
        o = o_ref[...]
        rn = o * lax.rsqrt(jnp.mean(o * o, axis=-1, keepdims=True) + EPS)
        gate = gate_ref[...]
        y_ref[...] = _bf(rn * rg_ref[...] * (gate * _sigmoid(gate)))

    return pl.pallas_call(
        body, name="hgrn_fwd", grid=(NHEAD,),
        in_specs=[*_HEAD_SPECS,
                  pl.BlockSpec((2, HEAD), lambda h: (0, h)),
                  pl.BlockSpec((1, HEAD), lambda h: (0, h)),
                  pl.BlockSpec(memory_space=pl.ANY)],
        out_specs=(pl.BlockSpec((T, HEAD), lambda h: (0, NHEAD + h)),
                   pl.BlockSpec((T, HEAD), lambda h: (0, h)),
                   pl.BlockSpec((None, NCHUNK, HEAD, HEAD), lambda h: (h, 0, 0, 0))),
        out_shape=(pltpu.HBM((T, DMIX), jnp.bfloat16), pltpu.HBM((T, D), jnp.float32),
                   pltpu.HBM((NHEAD, NCHUNK, HEAD, HEAD), jnp.float32)),
        input_output_aliases={6: 0},
        compiler_params=_params(("parallel",)),
    )(proj, proj, proj, proj, lb_logits, rec_g, y_in)


def _out_proj_loss(x, y, w_out, target, gf):
    rows = 512
    parts = [slice(k * rows // 2, (k + 1) * rows // 2) for k in range(2)]

    def body(x_ref, y_ref, w_ref, t_ref, g_ref, dz_ref, dzb_ref, sq_ref, dg_ref):
        zs = [x_ref[p, :] + _dot(y_ref[p, :], w_ref[...], 1, 0) for p in parts]
        sq = dg = 0.0
        for p, z in zip(parts, zs):
            r = lax.rsqrt(jnp.mean(z * z, axis=-1, keepdims=True) + EPS)
            zhat = z * r
            err = zhat * g_ref[...] - t_ref[p, :]
            dy = err * (1.0 / D)
            gdy = dy * g_ref[...]
            dz = r * (gdy - zhat * jnp.mean(zhat * gdy, axis=-1, keepdims=True))
            dz_ref[p, :] = dz
            dzb_ref[p, :] = _bf(dz)
            sq = sq + jnp.sum(err * err, axis=0, keepdims=True)
            dg = dg + jnp.sum(zhat * dy, axis=0, keepdims=True)

        @pl.when(pl.program_id(0) == 0)
        def _():
            sq_ref[...] = sq
            dg_ref[...] = dg

        @pl.when(pl.program_id(0) != 0)
        def _():
            sq_ref[...] += sq
            dg_ref[...] += dg

    tile = pl.BlockSpec((rows, D), lambda i: (i, 0))
    vec = pl.BlockSpec((1, D), lambda i: (0, 0))
    return pl.pallas_call(
        body, name="out_proj_loss", grid=(T // rows,),
        in_specs=[tile, pl.BlockSpec((rows, DMIX), lambda i: (i, 0)), pl.BlockSpec((DMIX, D), lambda i: (0, 0)),
                  tile, vec],
        out_specs=(tile, tile, vec, vec),
        out_shape=(pltpu.HBM((T, D), jnp.float32), pltpu.HBM((T, D), jnp.bfloat16),
                   jax.ShapeDtypeStruct((1, D), jnp.float32), jax.ShapeDtypeStruct((1, D), jnp.float32)),
        compiler_params=_params(("arbitrary",)),
    )(x, y, w_out, target, gf)


def _out_proj_bwd(dzb, w_out, y):
    tn = 512

    def body(dz_ref, w_ref, y_ref, dy_ref, gw_ref, gwb_ref):
        dz = dz_ref[...]
        dy_ref[...] = _dot(dz, w_ref[...], 1, 1)
        gw = _dot(y_ref[...], dz, 0, 0)
        gw_ref[...] = gw
        gwb_ref[...] = _bf(gw)

    return pl.pallas_call(
        body, name="out_proj_bwd", grid=(DMIX // tn,),
        in_specs=[pl.BlockSpec((T, D), lambda n: (0, 0)), pl.BlockSpec((tn, D), lambda n: (n, 0)),
                  pl.BlockSpec((T, tn), lambda n: (0, n))],
        out_specs=(pl.BlockSpec((T, tn), lambda n: (0, n)), pl.BlockSpec((tn, D), lambda n: (n, 0)),
                   pl.BlockSpec((tn, D), lambda n: (n, 0))),
        out_shape=(pltpu.HBM((T, DMIX), jnp.float32), pltpu.HBM((DMIX, D), jnp.float32),
                   pltpu.HBM((DMIX, D), jnp.bfloat16)),
        compiler_params=_params(("parallel",)),
    )(dzb, w_out, y)


def _hgrn_bwd(proj, lb_logits, rec_g, o, states, dymix, dproj_in, token):
    def body(q_ref, f_ref, i_ref, gate_ref, lb_ref, rg_ref, o_ref, st_ref, dy_ref, dp_any, token_any,
             dp_ref, drg_ref, dlb_ref, do_ref):
        del dp_any, token_any
        lb = _lower_bound(lb_ref)
        causal = _tri(True)
        smat, rmat = _sum_rows_matrix(), _rev_sum_matrix()

        o = o_ref[...]
        rs = lax.rsqrt(jnp.mean(o * o, axis=-1, keepdims=True) + EPS)
        rn = o * rs
        gate = gate_ref[...]
        sgate = _sigmoid(gate)
        dyv = dy_ref[...]
        d_r = dyv * (gate * sgate)
        dp_ref[3] = _bf(dyv * (rn * rg_ref[...]) * (sgate * (1.0 + gate * (1.0 - sgate))))
        drg_ref[...] = jnp.sum(d_r * rn, axis=0, keepdims=True)
        drn = d_r * rg_ref[...]
        do_ref[...] = rs * (drn - rn * jnp.mean(rn * drn, axis=-1, keepdims=True))

        def group(i, carry):
            dst, dlb = carry
            gi = NGRP - 1 - i
            rows = _group_rows(gi)
            span = range(NB)
            qvs = [q_ref[r, :] for r in rows]
            ts = [_gates(qv, f_ref[r, :], lb) for qv, r in zip(qvs, rows)]
            ds = [_decays(s) for s in _exact_sums(smat, [t["g"] for t in ts])]
            vs = [_bf(i_ref[r, :]) for r in rows]
            dos = [_bf(do_ref[r, :]) for r in rows]
            sts = [st_ref[gi * NB + j] for j in span]
            qe_f = [t["qs"] * d["e_q"] for t, d in zip(ts, ds)]
            ke_f = [t["kk"] * d["e_k"] for t, d in zip(ts, ds)]
            q_e, k_e = [_bf(a) for a in qe_f], [_bf(a) for a in ke_f]
            q_m = [_bf(t["qs"] * d["e_qm"]) for t, d in zip(ts, ds)]
            k_m = [_bf(t["kk"] * d["e_km"]) for t, d in zip(ts, ds)]
            a = [_bf(jnp.where(causal, _dot(q_m[j], k_m[j], 1, 1), 0.0)) for j in span]
            da = [_bf(jnp.where(causal, _dot(dos[j], vs[j], 1, 1), 0.0)) for j in span]
            dqm = [_dot(da[j], k_m[j], 1, 0) for j in span]
            dkm = [_dot(da[j], q_m[j], 0, 0) for j in span]
            dv_in = [_dot(a[j], dos[j], 0, 0) for j in span]
            dqe = [_dot(dos[j], _bf(sts[j]), 1, 0) for j in span]
            grow = [_dot(dos[j], q_e[j], 0, 0) for j in span]
            dke, carried = [None] * NB, [None] * NB
            for j in reversed(span):
                dst_b = _bf(dst)
                dke[j] = _dot(vs[j], dst_b, 1, 0)
                dp_ref[2, rows[j], :] = _bf(dv_in[j] + _dot(k_e[j], dst_b, 1, 1))
                carried[j] = ds[j]["total8"] * jnp.sum(dst * sts[j], axis=0, keepdims=True)
                dst = dst * jnp.tile(ds[j]["total8"], (HEAD // 8, 1)) + grow[j]
            kdk = [ke_f[j] * dke[j] for j in span]
            pos = [(q_m[j].astype(jnp.float32) * dqm[j] - k_m[j].astype(jnp.float32) * dkm[j]) + qe_f[j] * dqe[j]
                   for j in span]
            dgs = _exact_sums(rmat, [jnp.concatenate([pos[j], kdk[j]], axis=0) for j in span])
            for j in span:
                t, d = ts[j], ds[j]
                dg = dgs[j] + jnp.tile(carried[j], (CHUNK // 8, 1))
                dqs = dqm[j] * d["e_qm"] + dqe[j] * d["e_q"]
                dkk = dkm[j] * d["e_km"] + dke[j] * d["e_k"]
                df = dg / t["f"] - dkk
                dp_ref[1, rows[j], :] = _bf(df * (1.0 - lb) * (t["sg"] * (1.0 - t["sg"])))
                dp_ref[0, rows[j], :] = _bf(dqs * (t["sq"] * (1.0 + qvs[j] * (1.0 - t["sq"]))))
                dlb = dlb + df * (1.0 - t["sg"])
            return dst, dlb

        _, dlb = lax.fori_loop(0, NGRP, group, (jnp.zeros((HEAD, HEAD), jnp.float32),
                                                jnp.zeros((CHUNK, HEAD), jnp.float32)))
        dlb_ref[...] = jnp.sum(dlb, axis=0, keepdims=True)

    vec = pl.BlockSpec((1, HEAD), lambda h: (0, h))
    return pl.pallas_call(
        body, name="hgrn_bwd", grid=(NHEAD,),
        in_specs=[*_HEAD_SPECS,
                  pl.BlockSpec((2, HEAD), lambda h: (0, h)), vec,
                  pl.BlockSpec((T, HEAD), lambda h: (0, h)),
                  pl.BlockSpec((None, NCHUNK, HEAD, HEAD), lambda h: (h, 0, 0, 0)),
                  pl.BlockSpec((T, HEAD), lambda h: (0, NHEAD + h)), ANY, ANY],
        out_specs=(pl.BlockSpec((4, T, HEAD), lambda h: (0, 0, h)), vec, vec),
        out_shape=(pltpu.HBM((NSEG, T, D), jnp.bfloat16),
                   jax.ShapeDtypeStruct((1, D), jnp.float32), jax.ShapeDtypeStruct((1, D), jnp.float32)),
        scratch_shapes=[pltpu.VMEM((T, HEAD), jnp.float32)],
        input_output_aliases={9: 0},
        compiler_params=_params(("parallel",)),
    )(proj, proj, proj, proj, lb_logits, rec_g, o, states, dymix, dproj_in, token)


def _pool_bwd(proj, pool_w, pool_scale, dymix):
    def body(u_ref, pg_ref, w_ref, sc_ref, dy_ref, dp_ref, gw_ref, gs_ref):
        gidx = pl.program_id(0)
        u, pg = u_ref[...], pg_ref[...]
        w = w_ref[...].reshape(GROUP, GROUP)
        d = _bf(_window_mean(_window_sum(u, gidx, False), gidx) - u)
        mixed = _dot(d, w, 1, 0)
        spg = _sigmoid(pg)
        dyv = dy_ref[...]
        d_p = dyv * (pg * spg)
        dp_ref[1] = _bf(dyv * (mixed * sc_ref[...]) * (spg * (1.0 + pg * (1.0 - spg))))
        gs_ref[...] = jnp.sum(d_p * mixed, axis=0, keepdims=True)
        dmixed = _bf(d_p * sc_ref[...])
        gw_ref[...] = _dot(d, dmixed, 0, 0).reshape(gw_ref.shape)
        dd = _dot(dmixed, w, 1, 1)
        dp_ref[0] = _bf(_window_sum(_window_mean(dd, gidx), gidx, True) - dd)

    return pl.pallas_call(
        body, name="pool_bwd", grid=(NGROUP,),
        in_specs=[*_POOL_SPECS, _POOL_W_SPEC,
                  pl.BlockSpec((1, GROUP), lambda g: (0, g)),
                  pl.BlockSpec((T, GROUP), lambda g: (0, g))],
        out_specs=(pl.BlockSpec((2, T, GROUP), lambda g: (2, 0, g)), _POOL_W_SPEC,
                   pl.BlockSpec((1, GROUP), lambda g: (0, g))),
        out_shape=(pltpu.HBM((NSEG, T, D), jnp.bfloat16),
                   pltpu.HBM((NDEV, NGROUP, GROUP // NDEV, GROUP), jnp.float32),
                   jax.ShapeDtypeStruct((1, D), jnp.float32)),
        compiler_params=_params(("parallel",)),
    )(proj, proj, pool_w, pool_scale, dymix)


HALF = NTILE // 2
AWAY = HALF - 3


def _dproj_tile(chip, side, p):
    j = 6 * chip + 3 * side + p
    return ((j // 4 + 4) % NSEG, 0, j % 4)


def _sibling_copy(sib_out, sib_in, send_sems, recv_sems, slot):
    x, y, c, _ = _place()
    return pltpu.make_async_remote_copy(
        src_ref=sib_out.at[slot], dst_ref=sib_in.at[slot], send_sem=send_sems.at[slot],
        recv_sem=recv_sems.at[slot], device_id=(x, y, 1 - c), device_id_type=MESH)


def _proj_bwd_w_far(place, ht, dproj):
    def body(place_ref, h_ref, dp_ref, sib_out, sib_in, send_sems, recv_sems, stage, loc_sems):
        del place_ref
        i = pl.program_id(0)

        def to_hbm(k):
            return pltpu.make_async_copy(stage.at[k], sib_out.at[k], loc_sems.at[k])

        def send(k):
            to_hbm(k).wait()
            _sibling_copy(sib_out, sib_in, send_sems, recv_sems, k).start()

        stage[i] = _bf(_dot(h_ref[...], dp_ref[...], 1, 0))

        @pl.when(i > 0)
        def _():
            send(i - 1)

        to_hbm(i).start()

        @pl.when(i == HALF - 1)
        def _():
            send(i)

    buf = pltpu.HBM((HALF, D, TILE), jnp.bfloat16)
    sems = pltpu.SemaphoreType.DMA((HALF,))
    return pl.pallas_call(
        body, name="proj_bwd_w_far",
        grid_spec=pltpu.PrefetchScalarGridSpec(
            num_scalar_prefetch=1, grid=(HALF,),
            in_specs=[pl.BlockSpec((D, T), lambda i, pr: (0, 0)),
                      pl.BlockSpec((None, T, TILE), lambda i, pr: _dproj_tile(i // 3, 1 - pr[2], i % 3))],
            out_specs=(HBM, HBM, SEM, SEM),
            scratch_shapes=[pltpu.VMEM((HALF, D, TILE), jnp.bfloat16), pltpu.SemaphoreType.DMA((HALF,))]),
        out_shape=(buf, buf, sems, sems),
        compiler_params=pltpu.CompilerParams(dimension_semantics=("arbitrary",), vmem_limit_bytes=48 * MIB,
                                             has_side_effects=EFFECT),
    )(place, ht, dproj)


def _proj_bwd_w_near(place, ht, dproj, sib_out, sib_in, sib_send, sib_recv):
    def owner_chip(k, pr):
        return jnp.where(k < AWAY, (pr[1] + 1 + k % 3) % 4, pr[1])

    def tile_p(k):
        return jnp.where(k < AWAY, k // 3, k - AWAY)

    def body(place_ref, h_ref, dp_ref, sib_out, sib_in, sib_send, sib_recv, sums, own_ref, landing, out_send,
             out_recv, recvbuf, outbuf, in_sems, loc_sems):
        i = pl.program_id(0)
        px, py, c, _ = _place()

        def slot_of(k):
            return 3 * owner_chip(k, place_ref) + tile_p(k)

        def load(k):
            return pltpu.make_async_copy(sib_in.at[slot_of(k)], recvbuf.at[k % 2], in_sems.at[k % 2])

        def fetch(k):
            _sibling_copy(sib_out, sib_in, sib_send, sib_recv, slot_of(k)).wait_recv()
            load(k).start()

        def route(k):
            chip = owner_chip(k, place_ref)
            cx, cy = chip // 2, chip % 2
            return cx, cy, (cx ^ px) + 2 * (cy ^ py) - 1, tile_p(k)

        def to_hbm(k):
            _, _, rel, p = route(k)
            return pltpu.make_async_copy(outbuf.at[k], sums.at[rel, p], loc_sems.at[k])

        def to_owner(k):
            cx, cy, rel, p = route(k)
            return pltpu.make_async_remote_copy(
                src_ref=sums.at[rel, p], dst_ref=landing.at[rel, p], send_sem=out_send.at[3 * rel + p],
                recv_sem=out_recv.at[3 * rel + p], device_id=(cx, cy, c), device_id_type=MESH)

        @pl.when(i == 0)
        def _():
            fetch(i)

        @pl.when(i < HALF - 1)
        def _():
            fetch(i + 1)

        load(i).wait()
        total = _dot(h_ref[...], dp_ref[...], 1, 0) + recvbuf[i % 2].astype(jnp.float32)
        own_ref[...] = total
        outbuf[jnp.minimum(i, AWAY)] = _bf(total)

        @pl.when(i < AWAY)
        def _():
            to_hbm(i).start()

        @pl.when(jnp.logical_and(i > 0, i <= AWAY))
        def _():
            to_hbm(i - 1).wait()
            to_owner(i - 1).start()

        @pl.when(i == HALF - 1)
        def _():
            for slot in range(HALF):
                _sibling_copy(sib_out, sib_in, sib_send, sib_recv, slot).wait_send()

    travelling = pltpu.HBM((3, 3, D, TILE), jnp.bfloat16)
    sems = pltpu.SemaphoreType.DMA((AWAY,))
    return pl.pallas_call(
        body, name="proj_bwd_w_near",
        grid_spec=pltpu.PrefetchScalarGridSpec(
            num_scalar_prefetch=1, grid=(HALF,),
            in_specs=[pl.BlockSpec((D, T), lambda i, pr: (0, 0)),
                      pl.BlockSpec((None, T, TILE), lambda i, pr: _dproj_tile(owner_chip(i, pr), pr[2], tile_p(i))),
                      HBM, HBM, SEM, SEM],
            out_specs=(HBM, pl.BlockSpec((None, D, TILE), lambda i, pr: (jnp.where(i < AWAY, 0, i % 3), 0, 0)),
                       HBM, SEM, SEM),
            scratch_shapes=[pltpu.VMEM((2, D, TILE), jnp.bfloat16), pltpu.VMEM((AWAY + 1, D, TILE), jnp.bfloat16),
                            pltpu.SemaphoreType.DMA((2,)), pltpu.SemaphoreType.DMA((AWAY,))]),
        out_shape=(travelling, pltpu.HBM((3, D, TILE), jnp.float32), travelling, sems, sems),
        compiler_params=pltpu.CompilerParams(dimension_semantics=("arbitrary",), vmem_limit_bytes=48 * MIB,
                                             has_side_effects=EFFECT),
    )(place, ht, dproj, sib_out, sib_in, sib_send, sib_recv)


def _proj_bwd_x(dproj, w_t, x, g1, dz, others, token):
    tm = 512
    pairs = NSEG // 2
    k = len(others)

    def body(dp_ref, w_ref, x_ref, g_ref, dz_ref, *refs):
        dx_ref, dg_ref, wcat, acc = refs[k + 1:]
        m, s = pl.program_id(0), pl.program_id(1)

        @pl.when(m == 0)
        def _():
            for i in range(8):
                wcat[s, :, i * TILE:(i + 1) * TILE] = w_ref[i]

        @pl.when(s == 0)
        def _():
            acc[...] = jnp.zeros((tm, D), jnp.float32)

        acc[...] += _dot(jnp.concatenate([dp_ref[0], dp_ref[1]], axis=1), wcat[s], 1, 1)

        @pl.when(s == pairs - 1)
        def _():
            xv = x_ref[...]
            rs = lax.rsqrt(jnp.mean(xv * xv, axis=-1, keepdims=True) + EPS)
            xhat = xv * rs
            dhv = acc[...]
            gdh = dhv * g_ref[...]
            dx_ref[...] = dz_ref[...] + rs * (gdh - xhat * jnp.mean(xhat * gdh, axis=-1, keepdims=True))
            dg = jnp.sum(xhat * dhv, axis=0, keepdims=True)

            @pl.when(m == 0)
            def _():
                dg_ref[0:1, :] = dg
                for r, ref in enumerate(refs[:k]):
                    dg_ref[1 + r:2 + r, :] = ref[...]
                dg_ref[1 + k:, :] = jnp.zeros((7 - k, D), jnp.float32)

            @pl.when(m != 0)
            def _():
                dg_ref[0:1, :] += dg

    rows = pl.BlockSpec((tm, D), lambda m, s: (m, 0))
    vec = pl.BlockSpec((1, D), lambda m, s: (0, 0))
    return pl.pallas_call(
        body, name="proj_bwd_x", grid=(T // tm, pairs),
        in_specs=[pl.BlockSpec((2, tm, D), lambda m, s: (s, m, 0)),
                  pl.BlockSpec((8, D, TILE), lambda m, s: ((jnp.where(m == 0, s, pairs - 1) + 1) % pairs, 0, 0)),
                  rows, vec, rows, *[pl.BlockSpec(memory_space=pltpu.VMEM)] * k, ANY],
        out_specs=(rows, pl.BlockSpec((8, D), lambda m, s: (0, 0))),
        out_shape=(jax.ShapeDtypeStruct((T, D), jnp.float32), pltpu.HBM((8, D), jnp.float32)),
        scratch_shapes=[pltpu.VMEM((pairs, D, 2 * D), jnp.bfloat16), pltpu.VMEM((tm, D), jnp.float32)],
        compiler_params=_params(("arbitrary", "arbitrary"), vmem_mib=56),
    )(dproj, w_t, x, g1, dz, *others, token)


def _adamw(w, g, m, v):
    m_new = ADAM_B1 * m + (1.0 - ADAM_B1) * g
    v_new = ADAM_B2 * v + (1.0 - ADAM_B2) * (g * g)
    delta = -ADAM_LR * ((m_new / BC1) / (jnp.sqrt(v_new / BC2) + ADAM_EPS) + ADAM_WD * w)
    return delta, m_new, v_new


def _reduce_adam(name, place, parts, w, m, v, grid, w_spec):
    n = len(parts)

    def body(place_ref, *refs):
        del place_ref
        w_ref, m_ref, v_ref, g_ref, d_ref, mo_ref, vo_ref = refs[n:]
        g = None
        for ref, (_, _, stacked) in zip(refs[:n], parts):
            terms = [ref[r] for r in range(ref.shape[0])] if stacked else [ref[...]]
            for t in terms:
                if t.shape[-1] != w_ref.shape[-1]:
                    t = jnp.concatenate([t[p] for p in range(t.shape[0])], axis=1)
                g = t.astype(jnp.float32) if g is None else g + t.astype(jnp.float32)
        delta, m_new, v_new = _adamw(w_ref[...], g, m_ref[...], v_ref[...])
        g_ref[...] = g
        d_ref[...] = delta
        mo_ref[...] = m_new
        vo_ref[...] = v_new

    shape = jax.ShapeDtypeStruct(w.shape, jnp.float32)
    return pl.pallas_call(
        body, name=name,
        grid_spec=pltpu.PrefetchScalarGridSpec(
            num_scalar_prefetch=1, grid=grid,
            in_specs=[spec for _, spec, _ in parts] + [w_spec] * 3, out_specs=(w_spec,) * 4),
        out_shape=(shape,) * 4,
        compiler_params=_params(("parallel",)),
    )(place, *[_in_hbm(a) for a in [a for a, _, _ in parts] + [w, m, v]])


SMALL_ROWS = (1, 1, 2, 1, 1)


def _small_adam(place, own, parts, ws, ms, vs):
    n = len(SMALL_ROWS)

    def body(place_ref, own_ref, p_ref, *refs):
        ins, outs, bufs = refs[:3 * n], refs[3 * n:3 * n + 1 + 4 * n], refs[3 * n + 1 + 4 * n:]

        def stacked(group, buf):
            r0 = 0
            for ref, k in zip(group, SMALL_ROWS):
                buf[r0:r0 + k, :] = ref[...]
                r0 += k
            buf[r0:, :] = jnp.zeros((8 - r0, D), jnp.float32)
            return buf[...]

        wv, mv, vv = (stacked(ins[n * j:n * j + n], bufs[j]) for j in range(3))
        me = place_ref[0]
        g = None
        for s in range(NDEV):
            term = jnp.where(me == s, own_ref[...], p_ref[s])
            g = term if g is None else g + term
        rows = _row_ids(wv.shape)
        other = jnp.where(rows == 2, pltpu.roll(wv, 7, 0), jnp.where(rows == 3, pltpu.roll(wv, 1, 0), 0.0))
        lbv = _sigmoid(wv - other)
        sign = jnp.where(rows == 2, 1.0, -1.0)
        g = jnp.where((rows == 2) | (rows == 3), sign * g * lbv * (1.0 - lbv), g)
        delta, m_new, v_new = _adamw(wv, g, mv, vv)
        outs[0][...] = jnp.sum(g[6:7], axis=1, keepdims=True) * (0.5 / D)
        for j, val in enumerate((g, delta, m_new, v_new)):
            r0 = 0
            for ref, k in zip(outs[1 + n * j:1 + n * j + n], SMALL_ROWS):
                ref[...] = val[r0:r0 + k]
                r0 += k

    vmem = pl.BlockSpec(memory_space=pltpu.VMEM)
    shapes = [jax.ShapeDtypeStruct((k, D), jnp.float32) for k in SMALL_ROWS]
    out = pl.pallas_call(
        body, name="small_adam", out_shape=(jax.ShapeDtypeStruct((1, 1), jnp.float32), *shapes * 4),
        in_specs=[pl.BlockSpec(memory_space=pltpu.SMEM)] + [vmem] * (2 + 3 * n), out_specs=(vmem,) * (1 + 4 * n),
        scratch_shapes=[pltpu.VMEM((8, D), jnp.float32)] * 3,
    )(place, own, parts, *ws, *ms, *vs)
    return out[0], [out[1 + n * j:1 + n * j + n] for j in range(4)]


def kernel(x, norm1_g, w_in, pool_w, pool_scale, lb_logits, rec_norm_g, w_out, final_norm_g, loss_target, m_norm1_g, m_w_in, m_pool_w, m_pool_scale, m_lb_logits, m_rec_norm_g, m_w_out, m_final_norm_g, v_norm1_g, v_w_in, v_pool_w, v_pool_scale, v_lb_logits, v_rec_norm_g, v_w_out, v_final_norm_g):
    xs = x[0]
    target = loss_target[0]
    ix, iy, ic = lax.axis_index("x"), lax.axis_index("y"), lax.axis_index("c")
    place = jnp.stack([4 * ix + 2 * iy + ic, 2 * ix + iy, ic]).astype(jnp.int32)
    gf = final_norm_g.reshape(1, D)

    ht, w_t, w_out_b, w_out_g, pool_g, proj = _gather_proj(xs, norm1_g, w_in, w_out, pool_w)
    wout = [w_out_b, w_out_g]
    wout_send, wout_recv, wout, wout_token = _split_start("gather_wout_start", wout, NDEV - 1, _plan_wout)

    y = _pool_fwd(proj, pool_g, pool_scale, wout_token)
    y, o, states = _hgrn_fwd(proj, lb_logits, rec_norm_g, y)
    _, w_out_g = _split_wait("gather_wout_wait", wout, wout_send, wout_recv, _plan_wout, o)
    w_out_full = _in_hbm(w_out_g.reshape(DMIX, D))
    dz, dzb, sq, dgf = _out_proj_loss(xs, y, w_out_full, target, gf)

    dymix, gwout_f, gwout_b = _out_proj_bwd(dzb, w_out_full, y)
    dproj, gpool, dscale = _pool_bwd(proj, pool_g, pool_scale, dymix)

    blk_out = (NDEV, DMIX // NDEV, D)
    blk_pool = (NDEV, NGROUP, GROUP // NDEV, GROUP)
    rest = [gwout_b.reshape(blk_out), gpool,
            lax.empty((NDEV - 1,) + blk_out[1:], jnp.bfloat16), lax.empty((NDEV - 1,) + blk_pool[1:], jnp.float32)]
    rest_send, rest_recv, rest, rest_token = _split_start("scatter_rest_start", rest, 2 * (NDEV - 1), _plan_rest)

    dproj, drecg, dlb = _hgrn_bwd(proj, lb_logits, rec_norm_g, o, states, dymix, dproj, rest_token)
    chip_sums, own_sum, landing, win_send, win_recv = _proj_bwd_w_near(
        place, ht, dproj, *_proj_bwd_w_far(place, ht, dproj))
    win = [chip_sums, landing]

    others = [a.reshape(1, D) for a in (dscale, dlb, dlb, drecg, dgf, sq)]
    grad_x, small_block = _proj_bwd_x(dproj, w_t, xs, norm1_g, dz, others, chip_sums)

    small = [small_block, lax.empty((NDEV, 8, D), jnp.float32)]
    small_send, small_recv, small, small_token = _split_start("gather_small_start", small, NDEV - 1, _plan_small)

    _, gpool_own, r_out, r_pool = _split_wait("scatter_rest_wait", rest, rest_send, rest_recv, _plan_rest,
                                              small_token)
    g_wout, d_wout, m_wout, v_wout = _reduce_adam(
        "adam_w_out", place,
        [(gwout_f.reshape(blk_out), pl.BlockSpec((None,) + blk_out[1:], lambda i, pr: (pr[0], 0, 0)), False),
         (r_out, pl.BlockSpec((NDEV - 1,) + blk_out[1:], lambda i, pr: (0, 0, 0)), True)],
        w_out, m_w_out, v_w_out, (1,), pl.BlockSpec((None,) + blk_out[1:], lambda i, pr: (0, 0, 0)))
    g_pool, d_pool, m_pool, v_pool = _reduce_adam(
        "adam_pool_w", place,
        [(gpool_own, pl.BlockSpec((None,) + blk_pool[1:], lambda i, pr: (pr[0], 0, 0, 0)), False),
         (r_pool, pl.BlockSpec((NDEV - 1,) + blk_pool[1:], lambda i, pr: (0, 0, 0, 0)), True)],
        pool_w, m_pool_w, v_pool_w, (1,), pl.BlockSpec((None,) + blk_pool[1:], lambda i, pr: (0, 0, 0, 0)))

    _, r_in = _split_wait("scatter_win_wait", win, win_send, win_recv, _plan_in, d_pool)
    g_win, d_win, m_win, v_win = _reduce_adam(
        "adam_w_in", place,
        [(own_sum, pl.BlockSpec((3, D // 8, TILE), lambda i, pr: (0, i, 0)), False),
         (r_in, pl.BlockSpec((3, 3, D // 8, TILE), lambda i, pr: (0, 0, i, 0)), True)],
        w_in, m_w_in, v_w_in, (8,), pl.BlockSpec((None, D // 8, 3 * TILE), lambda i, pr: (0, i, 0)))

    own_small, r_small = _split_wait("gather_small_wait", small, small_send, small_recv, _plan_small, d_win)
    loss, (g_s, d_s, m_s, v_s) = _small_adam(
        place, own_small, r_small,
        (norm1_g, pool_scale, lb_logits, rec_norm_g, gf),
        (m_norm1_g, m_pool_scale, m_lb_logits, m_rec_norm_g, m_final_norm_g.reshape(1, D)),
        (v_norm1_g, v_pool_scale, v_lb_logits, v_rec_norm_g, v_final_norm_g.reshape(1, D)))

    def outs(small, win, pool, wout):
        n1, ps, lbl, rg, fg = small
        return n1, win, pool, ps, lbl, rg, wout, fg.reshape(D)

    return (loss.reshape(()), grad_x[None],
            *outs(g_s, g_win, g_pool, g_wout), *outs(d_s, d_win, d_pool, d_wout),
            *outs(m_s, m_win, m_pool, m_wout), *outs(v_s, v_win, v_pool, v_wout))
```

```python
import functools

import jax
import jax.numpy as jnp
from jax import lax
from jax.experimental import pallas as pl
from jax.experimental.pallas import tpu as pltpu

T = 2048
D = 1024
NSEG = 6
NTILE = 24
TILE = 256
DMIX = 2048
NDEV = 8
HEAD = 128
NHEAD = 8
CHUNK = 64
NCHUNK = T // CHUNK
NB = 32
NGRP = NCHUNK // NB
NGROUP = 4
GROUP = 256
EPS = 1e-6
EXP_CAP = 115.0
MESH = pl.DeviceIdType.MESH
AXES = ("x", "y", "c")
ANY = pl.BlockSpec(memory_space=pl.ANY)
HBM = pl.BlockSpec(memory_space=pltpu.HBM)
SEM = pl.BlockSpec(memory_space=pltpu.SEMAPHORE)
EFFECT = pltpu.SideEffectType.DATAFLOW_SIDE_EFFECTING

ADAM_LR = 0.001
ADAM_B1 = 0.9
ADAM_B2 = 0.999
ADAM_EPS = 1e-08
ADAM_WD = 0.01
ADAM_STEP = 10
BC1 = 1.0 - ADAM_B1 ** ADAM_STEP
BC2 = 1.0 - ADAM_B2 ** ADAM_STEP

MIB = 1 << 20


def _params(sem=None, vmem_mib=48):
    return pltpu.CompilerParams(dimension_semantics=sem, vmem_limit_bytes=vmem_mib * MIB)


def _sigmoid(v):
    return 1.0 / (1.0 + jnp.exp(-v))


def _dot(a, b, ca, cb, precision=None):
    return lax.dot_general(a, b, (((ca,), (cb,)), ((), ())), precision=precision,
                           preferred_element_type=jnp.float32)


def _bf(v):
    return v.astype(jnp.bfloat16)


def _in_hbm(a):
    return pltpu.with_memory_space_constraint(a, pltpu.HBM)


def _place():
    x, y, c = lax.axis_index("x"), lax.axis_index("y"), lax.axis_index("c")
    return x, y, c, 4 * x + 2 * y + c


def _peer(x, y, c, r):
    return (x ^ ((r >> 2) & 1), y ^ ((r >> 1) & 1), c ^ (r & 1))


def _gather_proj(x, g1, w_in, w_out, pool_w):
    def body(x_ref, g_ref, win_ref, wout_ref, pool_ref, ht_o, wt_o, woutb_o, wout_o, pool_o, proj_o,
             xbuf, hv, htv, wv, wob, pb, stage, send_sems, recv_sems, loc_sems, out_sems):
        px, py, c, my_idx = _place()
        fetch_x = pltpu.make_async_copy(x_ref, xbuf, loc_sems.at[5])
        fetch_x.start()
        me, sibling = (px, py, c), (px, py, 1 - c)
        chips = [(1 - px, py), (px, 1 - py), (1 - px, 1 - py)]
        for p in range(3):
            wv[3 * my_idx + p] = _bf(win_ref[0, :, p * TILE:(p + 1) * TILE])

        def index(bx, by, bc):
            return 4 * bx + 2 * by + bc

        def slot(w, block):
            return wv.at[pl.ds(3 * index(*block), 3)] if w == 0 else pool_o.at[index(*block)]

        def copy(k, w, block, to, src=None):
            return pltpu.make_async_remote_copy(
                src_ref=slot(w, block) if src is None else src, dst_ref=slot(w, block),
                send_sem=send_sems.at[2 * k + w], recv_sem=recv_sems.at[2 * k + w],
                device_id=to, device_id_type=MESH)

        def save(block):
            at = pl.ds(3 * index(*block), 3)
            pltpu.make_async_copy(wv.at[at], wt_o.at[at], loc_sems.at[4]).start()

        srcs = (slot(0, me), pb)
        first = []
        for w in (0, 1):
            if w == 1:
                pb[...] = _bf(pool_ref[0])
                wob[...] = _bf(wout_ref[0])
            group = [copy(1 + j, w, me, (*chip, c), src=srcs[w]) for j, chip in enumerate(chips[:2])]
            group.append(copy(0, w, me, sibling, src=srcs[w]))
            for cp in group:
                cp.start()
            first += group
        save(me)
        locs = [pltpu.make_async_copy(pb, slot(1, me), loc_sems.at[0]),
                pltpu.make_async_copy(wob, wout_o.at[my_idx], loc_sems.at[1]),
                pltpu.make_async_copy(wob, woutb_o, loc_sems.at[2])]
        for cp in locs:
            cp.start()

        fetch_x.wait()
        xv = xbuf[...]
        hv[...] = _bf(xv * lax.rsqrt(jnp.mean(xv * xv, axis=-1, keepdims=True) + EPS) * g_ref[...])
        rows = 256
        for r0 in range(0, T, rows):
            htv[:, r0:r0 + rows] = hv[r0:r0 + rows, :].T
        locs.append(pltpu.make_async_copy(htv, ht_o, loc_sems.at[3]))
        locs[-1].start()

        def out_copy(p, j):
            return pltpu.make_async_copy(stage.at[p], proj_o.at[j], out_sems.at[p])

        def project(nth, block):
            base = 3 * index(*block)

            def tile(p, carry):
                if nth > 0:
                    out_copy(p, base + p).wait()
                stage[p] = _dot(hv[...], wv[base + p], 1, 0)
                out_copy(p, base + p).start()
                return carry

            lax.fori_loop(0, 3, tile, 0)

        project(0, me)
        copy(0, 0, sibling, me).wait_recv()
        save(sibling)
        project(1, sibling)
        passed = []
        relay_from = (px ^ (1 - c), py ^ c, c)
        relay_to = (px ^ c, py ^ (1 - c), c)

        def arrived(w, j):
            copy(1 + j, w, (*chips[j], c), me).wait_recv()
            passed.append(copy(4 + j, w, (*chips[j], c), sibling))
            passed[-1].start()

        def relay(w):
            passed.append(copy(3, w, relay_from, relay_to))
            passed[-1].start()

        def handed(nth, j):
            copy(4 + j, 0, (*chips[j], 1 - c), me).wait_recv()
            save((*chips[j], 1 - c))
            project(nth, (*chips[j], 1 - c))

        arrived(0, 0)
        arrived(0, 1)
        relay(0)
        for j in range(2):
            save((*chips[j], c))
            project(2 + j, (*chips[j], c))
        handed(4, 0)
        handed(5, 1)
        arrived(1, 0)
        arrived(1, 1)
        relay(1)
        arrived(0, 2)
        save((*chips[2], c))
        project(6, (*chips[2], c))
        handed(7, 2)
        arrived(1, 2)
        copy(0, 1, sibling, me).wait_recv()
        for j, chip in enumerate(chips):
            copy(4 + j, 1, (*chip, 1 - c), me).wait_recv()
        keep = pltpu.make_async_copy(wv, wt_o, loc_sems.at[4])
        for p in range(3):
            out_copy(p, p).wait()
        for cp in first + passed:
            cp.wait_send()
        keep.wait()
        for cp in locs:
            cp.wait()

    vmem = pl.BlockSpec(memory_space=pltpu.VMEM)
    bf16 = jnp.bfloat16
    return pl.pallas_call(
        body, name="gather_proj",
        out_shape=(pltpu.HBM((D, T), bf16), pltpu.HBM((NTILE, D, TILE), bf16),
                   pltpu.HBM((DMIX // NDEV, D), bf16), pltpu.HBM((NDEV, DMIX // NDEV, D), bf16),
                   pltpu.HBM((NDEV, NGROUP, GROUP // NDEV, GROUP), bf16), pltpu.HBM((NTILE, T, TILE), jnp.float32)),
        in_specs=[ANY] + [vmem] * 4, out_specs=(ANY,) * 6,
        scratch_shapes=[pltpu.VMEM((T, D), jnp.float32),
                        pltpu.VMEM((T, D), bf16), pltpu.VMEM((D, T), bf16), pltpu.VMEM((NTILE, D, TILE), bf16),
                        pltpu.VMEM((DMIX // NDEV, D), bf16), pltpu.VMEM((NGROUP, GROUP // NDEV, GROUP), bf16),
                        pltpu.VMEM((3, T, TILE), jnp.float32),
                        pltpu.SemaphoreType.DMA((14,)), pltpu.SemaphoreType.DMA((14,)),
                        pltpu.SemaphoreType.DMA((6,)), pltpu.SemaphoreType.DMA((3,))],
        compiler_params=_params(vmem_mib=56),
    )(x, g1, w_in, w_out, pool_w)


def _split_start(name, arrays, n_copies, plan):
    k = len(arrays)

    def body(*refs):
        send_sems, recv_sems, token = refs[k], refs[k + 1], refs[-1]
        for i, (src, dst, to) in enumerate(plan(refs[:k])):
            pltpu.make_async_remote_copy(src_ref=src, dst_ref=dst, send_sem=send_sems.at[i],
                                         recv_sem=recv_sems.at[i], device_id=to, device_id_type=MESH).start()
        token[...] = jnp.zeros_like(token)

    out = pl.pallas_call(
        body, name=name,
        out_shape=(pltpu.SemaphoreType.DMA((n_copies,)), pltpu.SemaphoreType.DMA((n_copies,)),
                   *[pltpu.HBM(a.shape, a.dtype) for a in arrays], jax.ShapeDtypeStruct((8, 128), jnp.float32)),
        in_specs=[HBM] * k, out_specs=(SEM, SEM, *[HBM] * k, pl.BlockSpec(memory_space=pltpu.VMEM)),
        input_output_aliases={i: 2 + i for i in range(k)},
        compiler_params=pltpu.CompilerParams(has_side_effects=EFFECT),
    )(*[pltpu.with_memory_space_constraint(a, pltpu.HBM) for a in arrays])
    return out[0], out[1], out[2:2 + k], out[-1]


def _split_wait(name, arrays, send_sems, recv_sems, plan, after):
    k = len(arrays)

    def body(*refs):
        sends, recvs = refs[k], refs[k + 1]
        for i, (src, dst, to) in enumerate(plan(refs[:k])):
            cp = pltpu.make_async_remote_copy(src_ref=src, dst_ref=dst, send_sem=sends.at[i], recv_sem=recvs.at[i],
                                              device_id=to, device_id_type=MESH)
            cp.wait_send()
            cp.wait_recv()

    return pl.pallas_call(
        body, name=name,
        out_shape=tuple(pltpu.HBM(a.shape, a.dtype) for a in arrays),
        in_specs=[HBM] * k + [SEM, SEM, ANY], out_specs=(HBM,) * k,
        input_output_aliases={i: i for i in range(k)},
        compiler_params=pltpu.CompilerParams(has_side_effects=EFFECT),
    )(*arrays, send_sems, recv_sems, after)


def _plan_wout(refs):
    src, land = refs
    x, y, c, me = _place()
    return [(src, land.at[me], _peer(x, y, c, r)) for r in range(1, NDEV)]


def _plan_rest(refs):
    gob, gpf, r_out, r_pool = refs
    x, y, c, me = _place()
    plan = []
    for r in range(1, NDEV):
        plan.append((gob.at[me ^ r], r_out.at[r - 1], _peer(x, y, c, r)))
        plan.append((gpf.at[me ^ r], r_pool.at[r - 1], _peer(x, y, c, r)))
    return plan


def _plan_in(refs):
    sums, landing = refs
    x, y, c, _ = _place()
    plan = []
    for rel, (dx, dy) in enumerate(((1, 0), (0, 1), (1, 1))):
        for p in range(3):
            plan.append((sums.at[rel, p], landing.at[rel, p], (x ^ dx, y ^ dy, c)))
    return plan


def _plan_small(refs):
    small, land = refs
    x, y, c, me = _place()
    return [(small, land.at[me], _peer(x, y, c, r)) for r in range(1, NDEV)]


def _seg_tiles(s):
    return (s + 2) % NSEG


_POOL_SPECS = [pl.BlockSpec((None, T, GROUP), lambda g, base=base: (base + g, 0, 0)) for base in (0, 4)]
_HEAD_SPECS = [pl.BlockSpec((None, T, HEAD), lambda h, base=base: (base + h // 2, 0, h % 2))
               for base in (8, 12, 16, 20)]


_POOL_W_SPEC = pl.BlockSpec((NDEV, None, GROUP // NDEV, GROUP), lambda g: (0, g, 0, 0))


def _row_ids(shape):
    return lax.broadcasted_iota(jnp.int32, shape, 0)


BAND_ROWS = 128
HALO = 16


def _window_sum(a, gidx, lead):
    width = lax.shift_left(jnp.int32(2), gidx)
    shape = (BAND_ROWS, BAND_ROWS + HALO)
    t, j = lax.broadcasted_iota(jnp.int32, shape, 0), lax.broadcasted_iota(jnp.int32, shape, 1)
    first = t if lead else t + HALO - width + 1
    band = _bf(jnp.where(j >= first, jnp.where(j < first + width, 1.0, 0.0), 0.0))
    zeros = jnp.zeros((HALO, a.shape[1]), jnp.bfloat16)
    padded = [jnp.concatenate([p, zeros] if lead else [zeros, p], axis=0) for p in _split2(a)]
    out = []
    for r0 in range(0, T, BAND_ROWS):
        slab = jnp.concatenate([p[r0:r0 + BAND_ROWS + HALO] for p in padded], axis=1)
        r = _dot(band, slab, 1, 0)
        out.append(r[:, :a.shape[1]] + r[:, a.shape[1]:])
    return jnp.concatenate(out, axis=0)


def _window_mean(s, gidx):
    inv = jnp.where(gidx == 0, 0.5, jnp.where(gidx == 1, 0.25, jnp.where(gidx == 2, 0.125, 0.0625)))
    width = lax.shift_left(jnp.int32(2), gidx)
    head = s[:16] / jnp.minimum(_row_ids((16, s.shape[1])) + 1, width).astype(jnp.float32)
    return jnp.concatenate([head, s[16:] * inv], axis=0)


def _pool_fwd(proj, pool_w, pool_scale, token):
    def body(u_ref, pg_ref, w_ref, sc_ref, token_any, y_ref):
        del token_any
        gidx = pl.program_id(0)
        u, pg = u_ref[...], pg_ref[...]
        d = _window_mean(_window_sum(u, gidx, False), gidx) - u
        mixed = _dot(_bf(d), w_ref[...].reshape(GROUP, GROUP), 1, 0)
        y_ref[...] = _bf(mixed * sc_ref[...] * (pg * _sigmoid(pg)))

    return pl.pallas_call(
        body, name="pool_fwd", grid=(NGROUP,),
        in_specs=[*_POOL_SPECS, _POOL_W_SPEC, pl.BlockSpec((1, GROUP), lambda g: (0, g)), ANY],
        out_specs=pl.BlockSpec((T, GROUP), lambda g: (0, g)),
        out_shape=pltpu.HBM((T, DMIX), jnp.bfloat16),
        compiler_params=_params(("parallel",)),
    )(proj, proj, pool_w, pool_scale, token)


def _tri(lower):
    r = lax.broadcasted_iota(jnp.int32, (CHUNK, CHUNK), 0)
    c = lax.broadcasted_iota(jnp.int32, (CHUNK, CHUNK), 1)
    return (r >= c) if lower else (r <= c)


def _sum_rows_matrix():
    shape = (CHUNK + 16, CHUNK)
    r, c = lax.broadcasted_iota(jnp.int32, shape, 0), lax.broadcasted_iota(jnp.int32, shape, 1)
    run = jnp.where(c <= r, 1.0, 0.0)
    half = jnp.where(c < CHUNK // 2, 1.0, 0.0)
    return _bf(jnp.where(r < CHUNK, run, jnp.where(r < CHUNK + 8, 1.0, half)))


def _rev_sum_matrix():
    shape = (CHUNK, 2 * CHUNK)
    r, c = lax.broadcasted_iota(jnp.int32, shape, 0), lax.broadcasted_iota(jnp.int32, shape, 1)
    return _bf(jnp.where(c < CHUNK, jnp.where(c >= r, 1.0, 0.0), jnp.where(c - CHUNK < r, 1.0, 0.0)))


def _split2(a):
    hi = _bf(a)
    return [hi, _bf(a - hi.astype(jnp.float32))]


def _exact_sums(mat, pieces):
    x = jnp.concatenate([s for p in pieces for s in _split2(p)], axis=1)
    r = _dot(mat, x, 1, 0)
    return [r[:, 2 * j * HEAD:(2 * j + 1) * HEAD] + r[:, (2 * j + 1) * HEAD:(2 * j + 2) * HEAD]
            for j in range(len(pieces))]


def _gates(qv, fl, lb):
    sq = _sigmoid(qv)
    sg = _sigmoid(fl)
    f = lb + (1.0 - lb) * sg
    return dict(sq=sq, qs=qv * sq, sg=sg, f=f, kk=1.0 - f, g=jnp.log2(f))


def _decays(sums):
    big_g = sums[:CHUNK]
    total = sums[CHUNK:CHUNK + 8]
    g_last = jnp.tile(total, (CHUNK // 8, 1))
    g_mid = jnp.tile(sums[CHUNK + 8:], (CHUNK // 8, 1))
    return dict(
        e_q=jnp.exp2(big_g),
        e_k=jnp.exp2(g_last - big_g),
        e_qm=jnp.exp2(jnp.minimum(big_g - g_mid, EXP_CAP)),
        e_km=jnp.exp2(jnp.minimum(g_mid - big_g, EXP_CAP)),
        total8=jnp.exp2(total))


def _group_rows(gi):
    return [pl.ds(pl.multiple_of((gi * NB + j) * CHUNK, CHUNK), CHUNK) for j in range(NB)]


def _lower_bound(lb_ref):
    return _sigmoid(lb_ref[0:1, :] - lb_ref[1:2, :])


def _hgrn_fwd(proj, lb_logits, rec_g, y_in):
    def body(q_ref, f_ref, i_ref, gate_ref, lb_ref, rg_ref, y_any, y_ref, o_ref, st_ref):
        del y_any
        lb = _lower_bound(lb_ref)
        causal = _tri(True)
        smat = _sum_rows_matrix()

        def group(gi, st):
            rows = _group_rows(gi)
            ts = [_gates(q_ref[r, :], f_ref[r, :], lb) for r in rows]
            ds = [_decays(s) for s in _exact_sums(smat, [t["g"] for t in ts])]
            vs = [_bf(i_ref[r, :]) for r in rows]
            q_m = [_bf(t["qs"] * d["e_qm"]) for t, d in zip(ts, ds)]
            k_m = [_bf(t["kk"] * d["e_km"]) for t, d in zip(ts, ds)]
            q_e = [_bf(t["qs"] * d["e_q"]) for t, d in zip(ts, ds)]
            k_e = [_bf(t["kk"] * d["e_k"]) for t, d in zip(ts, ds)]
            a = [_bf(jnp.where(causal, _dot(q_m[j], k_m[j], 1, 1), 0.0)) for j in range(NB)]
            intra = [_dot(a[j], vs[j], 1, 0) for j in range(NB)]
            upd = [_dot(vs[j], k_e[j], 0, 0) for j in range(NB)]
            for j in range(NB):
                st_ref[gi * NB + j] = st
                o_ref[rows[j], :] = intra[j] + _dot(q_e[j], _bf(st), 1, 1)
                st = st * jnp.tile(ds[j]["total8"], (HEAD // 8, 1)) + upd[j]
            return st

        lax.fori_loop(0, NGRP, group, jnp.zeros((HEAD, HEAD), jnp.float32))
        o = o_ref[...]
        rn = o * lax.rsqrt(jnp.mean(o * o, axis=-1, keepdims=True) + EPS)
        gate = gate_ref[...]
        y_ref[...] = _bf(rn * rg_ref[...] * (gate * _sigmoid(gate)))

    return pl.pallas_call(
        body, name="hgrn_fwd", grid=(NHEAD,),
        in_specs=[*_HEAD_SPECS,
                  pl.BlockSpec((2, HEAD), lambda h: (0, h)),
                  pl.BlockSpec((1, HEAD), lambda h: (0, h)),
                  pl.BlockSpec(memory_space=pl.ANY)],
        out_specs=(pl.BlockSpec((T, HEAD), lambda h: (0, NHEAD + h)),
                   pl.BlockSpec((T, HEAD), lambda h: (0, h)),
                   pl.BlockSpec((None, NCHUNK, HEAD, HEAD), lambda h: (h, 0, 0, 0))),
        out_shape=(pltpu.HBM((T, DMIX), jnp.bfloat16), pltpu.HBM((T, D), jnp.float32),
                   pltpu.HBM((NHEAD, NCHUNK, HEAD, HEAD), jnp.float32)),
        input_output_aliases={6: 0},
        compiler_params=_params(("parallel",)),
    )(proj, proj, proj, proj, lb_logits, rec_g, y_in)


def _out_proj_loss(x, y, w_out, target, gf):
    rows = 512
    parts = [slice(k * rows // 2, (k + 1) * rows // 2) for k in range(2)]

    def body(x_ref, y_ref, w_ref, t_ref, g_ref, dz_ref, dzb_ref, sq_ref, dg_ref):
        zs = [x_ref[p, :] + _dot(y_ref[p, :], w_ref[...], 1, 0) for p in parts]
        sq = dg = 0.0
        for p, z in zip(parts, zs):
            r = lax.rsqrt(jnp.mean(z * z, axis=-1, keepdims=True) + EPS)
            zhat = z * r
            err = zhat * g_ref[...] - t_ref[p, :]
            dy = err * (1.0 / D)
            gdy = dy * g_ref[...]
            dz = r * (gdy - zhat * jnp.mean(zhat * gdy, axis=-1, keepdims=True))
            dz_ref[p, :] = dz
            dzb_ref[p, :] = _bf(dz)
            sq = sq + jnp.sum(err * err, axis=0, keepdims=True)
            dg = dg + jnp.sum(zhat * dy, axis=0, keepdims=True)

        @pl.when(pl.program_id(0) == 0)
        def _():
            sq_ref[...] = sq
            dg_ref[...] = dg

        @pl.when(pl.program_id(0) != 0)
        def _():
            sq_ref[...] += sq
            dg_ref[...] += dg

    tile = pl.BlockSpec((rows, D), lambda i: (i, 0))
    vec = pl.BlockSpec((1, D), lambda i: (0, 0))
    return pl.pallas_call(
        body, name="out_proj_loss", grid=(T // rows,),
        in_specs=[tile, pl.BlockSpec((rows, DMIX), lambda i: (i, 0)), pl.BlockSpec((DMIX, D), lambda i: (0, 0)),
                  tile, vec],
        out_specs=(tile, tile, vec, vec),
        out_shape=(pltpu.HBM((T, D), jnp.float32), pltpu.HBM((T, D), jnp.bfloat16),
                   pltpu.HBM((1, D), jnp.float32), pltpu.HBM((1, D), jnp.float32)),
        compiler_params=_params(("arbitrary",)),
    )(x, y, w_out, target, gf)


def _out_proj_bwd(dzb, w_out, y):
    tn = 512

    def body(dz_ref, w_ref, y_ref, dy_ref, gw_ref, gwb_ref):
        dz = dz_ref[...]
        dy_ref[...] = _dot(dz, w_ref[...], 1, 1)
        gw = _dot(y_ref[...], dz, 0, 0)
        gw_ref[...] = gw
        gwb_ref[...] = _bf(gw)

    return pl.pallas_call(
        body, name="out_proj_bwd", grid=(DMIX // tn,),
        in_specs=[pl.BlockSpec((T, D), lambda n: (0, 0)), pl.BlockSpec((tn, D), lambda n: (n, 0)),
                  pl.BlockSpec((T, tn), lambda n: (0, n))],
        out_specs=(pl.BlockSpec((T, tn), lambda n: (0, n)), pl.BlockSpec((tn, D), lambda n: (n, 0)),
                   pl.BlockSpec((tn, D), lambda n: (n, 0))),
        out_shape=(pltpu.HBM((T, DMIX), jnp.float32), pltpu.HBM((DMIX, D), jnp.float32),
                   pltpu.HBM((DMIX, D), jnp.bfloat16)),
        compiler_params=_params(("parallel",)),
    )(dzb, w_out, y)


def _hgrn_bwd(proj, lb_logits, rec_g, o, states, dymix, dproj_in, token):
    def body(q_ref, f_ref, i_ref, gate_ref, lb_ref, rg_ref, o_ref, st_ref, dy_ref, dp_any, token_any,
             dp_ref, drg_ref, dlb_ref, do_ref):
        del dp_any, token_any
        lb = _lower_bound(lb_ref)
        causal = _tri(True)
        smat, rmat = _sum_rows_matrix(), _rev_sum_matrix()

        o = o_ref[...]
        rs = lax.rsqrt(jnp.mean(o * o, axis=-1, keepdims=True) + EPS)
        rn = o * rs
        gate = gate_ref[...]
        sgate = _sigmoid(gate)
        dyv = dy_ref[...]
        d_r = dyv * (gate * sgate)
        dp_ref[3] = _bf(dyv * (rn * rg_ref[...]) * (sgate * (1.0 + gate * (1.0 - sgate))))
        drg_ref[...] = jnp.sum(d_r * rn, axis=0, keepdims=True)
        drn = d_r * rg_ref[...]
        do_ref[...] = rs * (drn - rn * jnp.mean(rn * drn, axis=-1, keepdims=True))

        def group(i, carry):
            dst, dlb = carry
            gi = NGRP - 1 - i
            rows = _group_rows(gi)
            span = range(NB)
            qvs = [q_ref[r, :] for r in rows]
            ts = [_gates(qv, f_ref[r, :], lb) for qv, r in zip(qvs, rows)]
            ds = [_decays(s) for s in _exact_sums(smat, [t["g"] for t in ts])]
            vs = [_bf(i_ref[r, :]) for r in rows]
            dos = [_bf(do_ref[r, :]) for r in rows]
            sts = [st_ref[gi * NB + j] for j in span]
            qe_f = [t["qs"] * d["e_q"] for t, d in zip(ts, ds)]
            ke_f = [t["kk"] * d["e_k"] for t, d in zip(ts, ds)]
            q_e, k_e = [_bf(a) for a in qe_f], [_bf(a) for a in ke_f]
            q_m = [_bf(t["qs"] * d["e_qm"]) for t, d in zip(ts, ds)]
            k_m = [_bf(t["kk"] * d["e_km"]) for t, d in zip(ts, ds)]
            a = [_bf(jnp.where(causal, _dot(q_m[j], k_m[j], 1, 1), 0.0)) for j in span]
            da = [_bf(jnp.where(causal, _dot(dos[j], vs[j], 1, 1), 0.0)) for j in span]
            dqm = [_dot(da[j], k_m[j], 1, 0) for j in span]
            dkm = [_dot(da[j], q_m[j], 0, 0) for j in span]
            dv_in = [_dot(a[j], dos[j], 0, 0) for j in span]
            dqe = [_dot(dos[j], _bf(sts[j]), 1, 0) for j in span]
            grow = [_dot(dos[j], q_e[j], 0, 0) for j in span]
            dke, carried = [None] * NB, [None] * NB
            for j in reversed(span):
                dst_b = _bf(dst)
                dke[j] = _dot(vs[j], dst_b, 1, 0)
                dp_ref[2, rows[j], :] = _bf(dv_in[j] + _dot(k_e[j], dst_b, 1, 1))
                carried[j] = ds[j]["total8"] * jnp.sum(dst * sts[j], axis=0, keepdims=True)
                dst = dst * jnp.tile(ds[j]["total8"], (HEAD // 8, 1)) + grow[j]
            kdk = [ke_f[j] * dke[j] for j in span]
            pos = [(q_m[j].astype(jnp.float32) * dqm[j] - k_m[j].astype(jnp.float32) * dkm[j]) + qe_f[j] * dqe[j]
                   for j in span]
            dgs = _exact_sums(rmat, [jnp.concatenate([pos[j], kdk[j]], axis=0) for j in span])
            for j in span:
                t, d = ts[j], ds[j]
                dg = dgs[j] + jnp.tile(carried[j], (CHUNK // 8, 1))
                dqs = dqm[j] * d["e_qm"] + dqe[j] * d["e_q"]
                dkk = dkm[j] * d["e_km"] + dke[j] * d["e_k"]
                df = dg / t["f"] - dkk
                dp_ref[1, rows[j], :] = _bf(df * (1.0 - lb) * (t["sg"] * (1.0 - t["sg"])))
                dp_ref[0, rows[j], :] = _bf(dqs * (t["sq"] * (1.0 + qvs[j] * (1.0 - t["sq"]))))
                dlb = dlb + df * (1.0 - t["sg"])
            return dst, dlb

        _, dlb = lax.fori_loop(0, NGRP, group, (jnp.zeros((HEAD, HEAD), jnp.float32),
                                                jnp.zeros((CHUNK, HEAD), jnp.float32)))
        dlb_ref[...] = jnp.sum(dlb, axis=0, keepdims=True)

    vec = pl.BlockSpec((1, HEAD), lambda h: (0, h))
    return pl.pallas_call(
        body, name="hgrn_bwd", grid=(NHEAD,),
        in_specs=[*_HEAD_SPECS,
                  pl.BlockSpec((2, HEAD), lambda h: (0, h)), vec,
                  pl.BlockSpec((T, HEAD), lambda h: (0, h)),
                  pl.BlockSpec((None, NCHUNK, HEAD, HEAD), lambda h: (h, 0, 0, 0)),
                  pl.BlockSpec((T, HEAD), lambda h: (0, NHEAD + h)), ANY, ANY],
        out_specs=(pl.BlockSpec((4, T, HEAD), lambda h: (0, 0, h)), vec, vec),
        out_shape=(pltpu.HBM((NSEG, T, D), jnp.bfloat16),
                   pltpu.HBM((1, D), jnp.float32), pltpu.HBM((1, D), jnp.float32)),
        scratch_shapes=[pltpu.VMEM((T, HEAD), jnp.float32)],
        input_output_aliases={9: 0},
        compiler_params=_params(("parallel",)),
    )(proj, proj, proj, proj, lb_logits, rec_g, o, states, dymix, dproj_in, token)


def _pool_bwd(proj, pool_w, pool_scale, dymix):
    def body(u_ref, pg_ref, w_ref, sc_ref, dy_ref, dp_ref, gw_ref, gs_ref):
        gidx = pl.program_id(0)
        u, pg = u_ref[...], pg_ref[...]
        w = w_ref[...].reshape(GROUP, GROUP)
        d = _bf(_window_mean(_window_sum(u, gidx, False), gidx) - u)
        mixed = _dot(d, w, 1, 0)
        spg = _sigmoid(pg)
        dyv = dy_ref[...]
        d_p = dyv * (pg * spg)
        dp_ref[1] = _bf(dyv * (mixed * sc_ref[...]) * (spg * (1.0 + pg * (1.0 - spg))))
        gs_ref[...] = jnp.sum(d_p * mixed, axis=0, keepdims=True)
        dmixed = _bf(d_p * sc_ref[...])
        gw_ref[...] = _dot(d, dmixed, 0, 0).reshape(gw_ref.shape)
        dd = _dot(dmixed, w, 1, 1)
        dp_ref[0] = _bf(_window_sum(_window_mean(dd, gidx), gidx, True) - dd)

    return pl.pallas_call(
        body, name="pool_bwd", grid=(NGROUP,),
        in_specs=[*_POOL_SPECS, _POOL_W_SPEC,
                  pl.BlockSpec((1, GROUP), lambda g: (0, g)),
                  pl.BlockSpec((T, GROUP), lambda g: (0, g))],
        out_specs=(pl.BlockSpec((2, T, GROUP), lambda g: (2, 0, g)), _POOL_W_SPEC,
                   pl.BlockSpec((1, GROUP), lambda g: (0, g))),
        out_shape=(pltpu.HBM((NSEG, T, D), jnp.bfloat16),
                   pltpu.HBM((NDEV, NGROUP, GROUP // NDEV, GROUP), jnp.float32),
                   pltpu.HBM((1, D), jnp.float32)),
        compiler_params=_params(("parallel",)),
    )(proj, proj, pool_w, pool_scale, dymix)


HALF = NTILE // 2
AWAY = HALF - 3


def _dproj_tile(chip, side, p):
    j = 6 * chip + 3 * side + p
    return ((j // 4 + 4) % NSEG, 0, j % 4)


def _sibling_copy(sib_out, sib_in, send_sems, recv_sems, slot):
    x, y, c, _ = _place()
    return pltpu.make_async_remote_copy(
        src_ref=sib_out.at[slot], dst_ref=sib_in.at[slot], send_sem=send_sems.at[slot],
        recv_sem=recv_sems.at[slot], device_id=(x, y, 1 - c), device_id_type=MESH)


def _proj_bwd_w_far(place, ht, dproj):
    def body(place_ref, h_ref, dp_ref, sib_out, sib_in, send_sems, recv_sems, stage, loc_sems):
        del place_ref
        i = pl.program_id(0)

        def to_hbm(k):
            return pltpu.make_async_copy(stage.at[k], sib_out.at[k], loc_sems.at[k])

        def send(k):
            to_hbm(k).wait()
            _sibling_copy(sib_out, sib_in, send_sems, recv_sems, k).start()

        stage[i] = _bf(_dot(h_ref[...], dp_ref[...], 1, 0))

        @pl.when(i > 0)
        def _():
            send(i - 1)

        to_hbm(i).start()

        @pl.when(i == HALF - 1)
        def _():
            send(i)

    buf = pltpu.HBM((HALF, D, TILE), jnp.bfloat16)
    sems = pltpu.SemaphoreType.DMA((HALF,))
    return pl.pallas_call(
        body, name="proj_bwd_w_far",
        grid_spec=pltpu.PrefetchScalarGridSpec(
            num_scalar_prefetch=1, grid=(HALF,),
            in_specs=[pl.BlockSpec((D, T), lambda i, pr: (0, 0)),
                      pl.BlockSpec((None, T, TILE), lambda i, pr: _dproj_tile(i // 3, 1 - pr[2], i % 3))],
            out_specs=(HBM, HBM, SEM, SEM),
            scratch_shapes=[pltpu.VMEM((HALF, D, TILE), jnp.bfloat16), pltpu.SemaphoreType.DMA((HALF,))]),
        out_shape=(buf, buf, sems, sems),
        compiler_params=pltpu.CompilerParams(dimension_semantics=("arbitrary",), vmem_limit_bytes=48 * MIB,
                                             has_side_effects=EFFECT),
    )(place, ht, dproj)


def _proj_bwd_w_near(place, ht, dproj, sib_out, sib_in, sib_send, sib_recv):
    def owner_chip(k, pr):
        return jnp.where(k < AWAY, (pr[1] + 1 + k % 3) % 4, pr[1])

    def tile_p(k):
        return jnp.where(k < AWAY, k // 3, k - AWAY)

    def body(place_ref, h_ref, dp_ref, sib_out, sib_in, sib_send, sib_recv, sums, own_ref, landing, out_send,
             out_recv, recvbuf, outbuf, in_sems, loc_sems):
        i = pl.program_id(0)
        px, py, c, _ = _place()

        def slot_of(k):
            return 3 * owner_chip(k, place_ref) + tile_p(k)

        def load(k):
            return pltpu.make_async_copy(sib_in.at[slot_of(k)], recvbuf.at[k % 2], in_sems.at[k % 2])

        def fetch(k):
            _sibling_copy(sib_out, sib_in, sib_send, sib_recv, slot_of(k)).wait_recv()
            load(k).start()

        def route(k):
            chip = owner_chip(k, place_ref)
            cx, cy = chip // 2, chip % 2
            return cx, cy, (cx ^ px) + 2 * (cy ^ py) - 1, tile_p(k)

        def to_hbm(k):
            _, _, rel, p = route(k)
            return pltpu.make_async_copy(outbuf.at[k], sums.at[rel, p], loc_sems.at[k])

        def to_owner(k):
            cx, cy, rel, p = route(k)
            return pltpu.make_async_remote_copy(
                src_ref=sums.at[rel, p], dst_ref=landing.at[rel, p], send_sem=out_send.at[3 * rel + p],
                recv_sem=out_recv.at[3 * rel + p], device_id=(cx, cy, c), device_id_type=MESH)

        @pl.when(i == 0)
        def _():
            fetch(i)

        @pl.when(i < HALF - 1)
        def _():
            fetch(i + 1)

        load(i).wait()
        total = _dot(h_ref[...], dp_ref[...], 1, 0) + recvbuf[i % 2].astype(jnp.float32)
        own_ref[...] = total
        outbuf[jnp.minimum(i, AWAY)] = _bf(total)

        @pl.when(i < AWAY)
        def _():
            to_hbm(i).start()

        @pl.when(jnp.logical_and(i > 0, i <= AWAY))
        def _():
            to_hbm(i - 1).wait()
            to_owner(i - 1).start()

        @pl.when(i == HALF - 1)
        def _():
            for slot in range(HALF):
                _sibling_copy(sib_out, sib_in, sib_send, sib_recv, slot).wait_send()

    travelling = pltpu.HBM((3, 3, D, TILE), jnp.bfloat16)
    sems = pltpu.SemaphoreType.DMA((AWAY,))
    return pl.pallas_call(
        body, name="proj_bwd_w_near",
        grid_spec=pltpu.PrefetchScalarGridSpec(
            num_scalar_prefetch=1, grid=(HALF,),
            in_specs=[pl.BlockSpec((D, T), lambda i, pr: (0, 0)),
                      pl.BlockSpec((None, T, TILE), lambda i, pr: _dproj_tile(owner_chip(i, pr), pr[2], tile_p(i))),
                      HBM, HBM, SEM, SEM],
            out_specs=(HBM, pl.BlockSpec((None, D, TILE), lambda i, pr: (jnp.where(i < AWAY, 0, i % 3), 0, 0)),
                       HBM, SEM, SEM),
            scratch_shapes=[pltpu.VMEM((2, D, TILE), jnp.bfloat16), pltpu.VMEM((AWAY + 1, D, TILE), jnp.bfloat16),
                            pltpu.SemaphoreType.DMA((2,)), pltpu.SemaphoreType.DMA((AWAY,))]),
        out_shape=(travelling, pltpu.HBM((3, D, TILE), jnp.float32), travelling, sems, sems),
        compiler_params=pltpu.CompilerParams(dimension_semantics=("arbitrary",), vmem_limit_bytes=48 * MIB,
                                             has_side_effects=EFFECT),
    )(place, ht, dproj, sib_out, sib_in, sib_send, sib_recv)


def _proj_bwd_x(dproj, w_t, x, g1, dz, others, token):
    tm = 512
    pairs = NSEG // 2
    k = len(others)

    def body(dp_ref, w_ref, x_ref, g_ref, dz_ref, *refs):
        dx_ref, dg_ref, wcat, acc = refs[k + 1:]
        m, s = pl.program_id(0), pl.program_id(1)

        @pl.when(m == 0)
        def _():
            for i in range(8):
                wcat[s, :, i * TILE:(i + 1) * TILE] = w_ref[i]

        @pl.when(s == 0)
        def _():
            acc[...] = jnp.zeros((tm, D), jnp.float32)

        acc[...] += _dot(jnp.concatenate([dp_ref[0], dp_ref[1]], axis=1), wcat[s], 1, 1)

        @pl.when(s == pairs - 1)
        def _():
            xv = x_ref[...]
            rs = lax.rsqrt(jnp.mean(xv * xv, axis=-1, keepdims=True) + EPS)
            xhat = xv * rs
            dhv = acc[...]
            gdh = dhv * g_ref[...]
            dx_ref[...] = dz_ref[...] + rs * (gdh - xhat * jnp.mean(xhat * gdh, axis=-1, keepdims=True))
            dg = jnp.sum(xhat * dhv, axis=0, keepdims=True)

            @pl.when(m == 0)
            def _():
                dg_ref[0:1, :] = dg
                for r, ref in enumerate(refs[:k]):
                    dg_ref[1 + r:2 + r, :] = ref[...]
                dg_ref[1 + k:, :] = jnp.zeros((7 - k, D), jnp.float32)

            @pl.when(m != 0)
            def _():
                dg_ref[0:1, :] += dg

    rows = pl.BlockSpec((tm, D), lambda m, s: (m, 0))
    vec = pl.BlockSpec((1, D), lambda m, s: (0, 0))
    return pl.pallas_call(
        body, name="proj_bwd_x", grid=(T // tm, pairs),
        in_specs=[pl.BlockSpec((2, tm, D), lambda m, s: (s, m, 0)),
                  pl.BlockSpec((8, D, TILE), lambda m, s: ((jnp.where(m == 0, s, pairs - 1) + 1) % pairs, 0, 0)),
                  rows, vec, rows, *[vec] * k, ANY],
        out_specs=(rows, pl.BlockSpec((8, D), lambda m, s: (0, 0))),
        out_shape=(jax.ShapeDtypeStruct((T, D), jnp.float32), pltpu.HBM((8, D), jnp.float32)),
        scratch_shapes=[pltpu.VMEM((pairs, D, 2 * D), jnp.bfloat16), pltpu.VMEM((tm, D), jnp.float32)],
        compiler_params=_params(("arbitrary", "arbitrary"), vmem_mib=56),
    )(dproj, w_t, x, g1, dz, *[_in_hbm(a) for a in others], token)


def _adamw(w, g, m, v):
    m_new = ADAM_B1 * m + (1.0 - ADAM_B1) * g
    v_new = ADAM_B2 * v + (1.0 - ADAM_B2) * (g * g)
    delta = -ADAM_LR * ((m_new / BC1) / (jnp.sqrt(v_new / BC2) + ADAM_EPS) + ADAM_WD * w)
    return delta, m_new, v_new


def _reduce_adam(name, place, parts, w, m, v, grid, w_spec):
    n = len(parts)

    def body(place_ref, *refs):
        del place_ref
        w_ref, m_ref, v_ref, g_ref, d_ref, mo_ref, vo_ref = refs[n:]
        g = None
        for ref, (_, _, stacked) in zip(refs[:n], parts):
            terms = [ref[r] for r in range(ref.shape[0])] if stacked else [ref[...]]
            for t in terms:
                if t.shape[-1] != w_ref.shape[-1]:
                    t = jnp.concatenate([t[p] for p in range(t.shape[0])], axis=1)
                g = t.astype(jnp.float32) if g is None else g + t.astype(jnp.float32)
        delta, m_new, v_new = _adamw(w_ref[...], g, m_ref[...], v_ref[...])
        g_ref[...] = g
        d_ref[...] = delta
        mo_ref[...] = m_new
        vo_ref[...] = v_new

    shape = jax.ShapeDtypeStruct(w.shape, jnp.float32)
    return pl.pallas_call(
        body, name=name,
        grid_spec=pltpu.PrefetchScalarGridSpec(
            num_scalar_prefetch=1, grid=grid,
            in_specs=[spec for _, spec, _ in parts] + [w_spec] * 3, out_specs=(w_spec,) * 4),
        out_shape=(shape,) * 4,
        compiler_params=_params(("parallel",)),
    )(place, *[_in_hbm(a) for a in [a for a, _, _ in parts] + [w, m, v]])


SMALL_ROWS = (1, 1, 2, 1, 1)


def _small_adam(place, own, parts, ws, ms, vs):
    n = len(SMALL_ROWS)

    def body(place_ref, own_ref, p_ref, *refs):
        ins, outs, bufs = refs[:3 * n], refs[3 * n:3 * n + 1 + 4 * n], refs[3 * n + 1 + 4 * n:]

        def stacked(group, buf):
            r0 = 0
            for ref, k in zip(group, SMALL_ROWS):
                buf[r0:r0 + k, :] = ref[...]
                r0 += k
            buf[r0:, :] = jnp.zeros((8 - r0, D), jnp.float32)
            return buf[...]

        wv, mv, vv = (stacked(ins[n * j:n * j + n], bufs[j]) for j in range(3))
        me = place_ref[0]
        g = None
        for s in range(NDEV):
            term = jnp.where(me == s, own_ref[...], p_ref[s])
            g = term if g is None else g + term
        rows = _row_ids(wv.shape)
        other = jnp.where(rows == 2, pltpu.roll(wv, 7, 0), jnp.where(rows == 3, pltpu.roll(wv, 1, 0), 0.0))
        lbv = _sigmoid(wv - other)
        sign = jnp.where(rows == 2, 1.0, -1.0)
        g = jnp.where((rows == 2) | (rows == 3), sign * g * lbv * (1.0 - lbv), g)
        delta, m_new, v_new = _adamw(wv, g, mv, vv)
        outs[0][...] = jnp.sum(g[6:7], axis=1, keepdims=True) * (0.5 / D)
        for j, val in enumerate((g, delta, m_new, v_new)):
            r0 = 0
            for ref, k in zip(outs[1 + n * j:1 + n * j + n], SMALL_ROWS):
                ref[...] = val[r0:r0 + k]
                r0 += k

    vmem = pl.BlockSpec(memory_space=pltpu.VMEM)
    shapes = [jax.ShapeDtypeStruct((k, D), jnp.float32) for k in SMALL_ROWS]
    out = pl.pallas_call(
        body, name="small_adam", out_shape=(jax.ShapeDtypeStruct((1, 1), jnp.float32), *shapes * 4),
        in_specs=[pl.BlockSpec(memory_space=pltpu.SMEM)] + [vmem] * (2 + 3 * n), out_specs=(vmem,) * (1 + 4 * n),
        scratch_shapes=[pltpu.VMEM((8, D), jnp.float32)] * 3,
    )(place, own, parts, *ws, *ms, *vs)
    return out[0], [out[1 + n * j:1 + n * j + n] for j in range(4)]


def kernel(x, norm1_g, w_in, pool_w, pool_scale, lb_logits, rec_norm_g, w_out, final_norm_g, loss_target, m_norm1_g, m_w_in, m_pool_w, m_pool_scale, m_lb_logits, m_rec_norm_g, m_w_out, m_final_norm_g, v_norm1_g, v_w_in, v_pool_w, v_pool_scale, v_lb_logits, v_rec_norm_g, v_w_out, v_final_norm_g):
    xs = x[0]
    target = loss_target[0]
    ix, iy, ic = lax.axis_index("x"), lax.axis_index("y"), lax.axis_index("c")
    place = jnp.stack([4 * ix + 2 * iy + ic, 2 * ix + iy, ic]).astype(jnp.int32)
    gf = final_norm_g.reshape(1, D)

    ht, w_t, w_out_b, w_out_g, pool_g, proj = _gather_proj(xs, norm1_g, w_in, w_out, pool_w)
    wout = [w_out_b, w_out_g]
    wout_send, wout_recv, wout, wout_token = _split_start("gather_wout_start", wout, NDEV - 1, _plan_wout)

    y = _pool_fwd(proj, pool_g, pool_scale, wout_token)
    y, o, states = _hgrn_fwd(proj, lb_logits, rec_norm_g, y)
    _, w_out_g = _split_wait("gather_wout_wait", wout, wout_send, wout_recv, _plan_wout, o)
    w_out_full = _in_hbm(w_out_g.reshape(DMIX, D))
    dz, dzb, sq, dgf = _out_proj_loss(xs, y, w_out_full, target, gf)

    dymix, gwout_f, gwout_b = _out_proj_bwd(dzb, w_out_full, y)
    dproj, gpool, dscale = _pool_bwd(proj, pool_g, pool_scale, dymix)

    blk_out = (NDEV, DMIX // NDEV, D)
    blk_pool = (NDEV, NGROUP, GROUP // NDEV, GROUP)
    rest = [gwout_b.reshape(blk_out), gpool,
            lax.empty((NDEV - 1,) + blk_out[1:], jnp.bfloat16), lax.empty((NDEV - 1,) + blk_pool[1:], jnp.float32)]
    rest_send, rest_recv, rest, rest_token = _split_start("scatter_rest_start", rest, 2 * (NDEV - 1), _plan_rest)

    dproj, drecg, dlb = _hgrn_bwd(proj, lb_logits, rec_norm_g, o, states, dymix, dproj, rest_token)
    chip_sums, own_sum, landing, win_send, win_recv = _proj_bwd_w_near(
        place, ht, dproj, *_proj_bwd_w_far(place, ht, dproj))
    win = [chip_sums, landing]

    others = [a.reshape(1, D) for a in (dscale, dlb, dlb, drecg, dgf, sq)]
    grad_x, small_block = _proj_bwd_x(dproj, w_t, xs, norm1_g, dz, others, chip_sums)

    small = [small_block, lax.empty((NDEV, 8, D), jnp.float32)]
    small_send, small_recv, small, small_token = _split_start("gather_small_start", small, NDEV - 1, _plan_small)

    _, gpool_own, r_out, r_pool = _split_wait("scatter_rest_wait", rest, rest_send, rest_recv, _plan_rest,
                                              small_token)
    g_wout, d_wout, m_wout, v_wout = _reduce_adam(
        "adam_w_out", place,
        [(gwout_f.reshape(blk_out), pl.BlockSpec((None,) + blk_out[1:], lambda i, pr: (pr[0], 0, 0)), False),
         (r_out, pl.BlockSpec((NDEV - 1,) + blk_out[1:], lambda i, pr: (0, 0, 0)), True)],
        w_out, m_w_out, v_w_out, (1,), pl.BlockSpec((None,) + blk_out[1:], lambda i, pr: (0, 0, 0)))
    g_pool, d_pool, m_pool, v_pool = _reduce_adam(
        "adam_pool_w", place,
        [(gpool_own, pl.BlockSpec((None,) + blk_pool[1:], lambda i, pr: (pr[0], 0, 0, 0)), False),
         (r_pool, pl.BlockSpec((NDEV - 1,) + blk_pool[1:], lambda i, pr: (0, 0, 0, 0)), True)],
        pool_w, m_pool_w, v_pool_w, (1,), pl.BlockSpec((None,) + blk_pool[1:], lambda i, pr: (0, 0, 0, 0)))

    _, r_in = _split_wait("scatter_win_wait", win, win_send, win_recv, _plan_in, d_pool)
    g_win, d_win, m_win, v_win = _reduce_adam(
        "adam_w_in", place,
        [(own_sum, pl.BlockSpec((3, D // 8, TILE), lambda i, pr: (0, i, 0)), False),
         (r_in, pl.BlockSpec((3, 3, D // 8, TILE), lambda i, pr: (0, 0, i, 0)), True)],
        w_in, m_w_in, v_w_in, (8,), pl.BlockSpec((None, D // 8, 3 * TILE), lambda i, pr: (0, i, 0)))

    own_small, r_small = _split_wait("gather_small_wait", small, small_send, small_recv, _plan_small, d_win)
    loss, (g_s, d_s, m_s, v_s) = _small_adam(
        place, own_small, r_small,
        (norm1_g, pool_scale, lb_logits, rec_norm_g, gf),
        (m_norm1_g, m_pool_scale, m_lb_logits, m_rec_norm_g, m_final_norm_g.reshape(1, D)),
        (v_norm1_g, v_pool_scale, v_lb_logits, v_rec_norm_g, v_final_norm_g.reshape(1, D)))

    def outs(small, win, pool, wout):
        n1, ps, lbl, rg, fg = small
        return n1, win, pool, ps, lbl, rg, wout, fg.reshape(D)

    return (loss.reshape(()), grad_x[None],
            *outs(g_s, g_win, g_pool, g_wout), *outs(d_s, d_win, d_pool, d_wout),
            *outs(m_s, m_win, m_pool, m_wout), *outs(v_s, v_win, v_pool, v_wout))
```

```python
import functools

import jax
import jax.numpy as jnp
from jax import lax
from jax.experimental import pallas as pl
from jax.experimental.pallas import tpu as pltpu

T = 2048
D = 1024
NSEG = 6
NTILE = 24
TILE = 256
DMIX = 2048
NDEV = 8
HEAD = 128
NHEAD = 8
CHUNK = 64
NCHUNK = T // CHUNK
NB = 32
NGRP = NCHUNK // NB
NGROUP = 4
GROUP = 256
EPS = 1e-6
EXP_CAP = 115.0
MESH = pl.DeviceIdType.MESH
AXES = ("x", "y", "c")
ANY = pl.BlockSpec(memory_space=pl.ANY)
HBM = pl.BlockSpec(memory_space=pltpu.HBM)
SEM = pl.BlockSpec(memory_space=pltpu.SEMAPHORE)
EFFECT = pltpu.SideEffectType.DATAFLOW_SIDE_EFFECTING

ADAM_LR = 0.001
ADAM_B1 = 0.9
ADAM_B2 = 0.999
ADAM_EPS = 1e-08
ADAM_WD = 0.01
ADAM_STEP = 10
BC1 = 1.0 - ADAM_B1 ** ADAM_STEP
BC2 = 1.0 - ADAM_B2 ** ADAM_STEP

MIB = 1 << 20


def _params(sem=None, vmem_mib=48):
    return pltpu.CompilerParams(dimension_semantics=sem, vmem_limit_bytes=vmem_mib * MIB)


def _sigmoid(v):
    return 1.0 / (1.0 + jnp.exp(-v))


def _dot(a, b, ca, cb, precision=None):
    return lax.dot_general(a, b, (((ca,), (cb,)), ((), ())), precision=precision,
                           preferred_element_type=jnp.float32)


def _bf(v):
    return v.astype(jnp.bfloat16)


def _in_hbm(a):
    return pltpu.with_memory_space_constraint(a, pltpu.HBM)


def _place():
    x, y, c = lax.axis_index("x"), lax.axis_index("y"), lax.axis_index("c")
    return x, y, c, 4 * x + 2 * y + c


def _peer(x, y, c, r):
    return (x ^ ((r >> 2) & 1), y ^ ((r >> 1) & 1), c ^ (r & 1))


def _gather_proj(x, g1, w_in, w_out, pool_w):
    def body(x_ref, g_ref, win_ref, wout_ref, pool_ref, ht_o, wt_o, woutb_o, wout_o, pool_o, proj_o,
             xbuf, hv, htv, wv, wob, pb, stage, send_sems, recv_sems, loc_sems, out_sems):
        px, py, c, my_idx = _place()
        fetch_x = pltpu.make_async_copy(x_ref, xbuf, loc_sems.at[5])
        fetch_x.start()
        me, sibling = (px, py, c), (px, py, 1 - c)
        chips = [(1 - px, py), (px, 1 - py), (1 - px, 1 - py)]
        for p in range(3):
            wv[3 * my_idx + p] = _bf(win_ref[0, :, p * TILE:(p + 1) * TILE])

        def index(bx, by, bc):
            return 4 * bx + 2 * by + bc

        def slot(w, block):
            return wv.at[pl.ds(3 * index(*block), 3)] if w == 0 else pool_o.at[index(*block)]

        def copy(k, w, block, to, src=None):
            return pltpu.make_async_remote_copy(
                src_ref=slot(w, block) if src is None else src, dst_ref=slot(w, block),
                send_sem=send_sems.at[2 * k + w], recv_sem=recv_sems.at[2 * k + w],
                device_id=to, device_id_type=MESH)

        def save(block):
            at = pl.ds(3 * index(*block), 3)
            pltpu.make_async_copy(wv.at[at], wt_o.at[at], loc_sems.at[4]).start()

        srcs = (slot(0, me), pb)
        first = []
        for w in (0, 1):
            if w == 1:
                pb[...] = _bf(pool_ref[0])
                wob[...] = _bf(wout_ref[0])
            group = [copy(1 + j, w, me, (*chip, c), src=srcs[w]) for j, chip in enumerate(chips[:2])]
            group.append(copy(0, w, me, sibling, src=srcs[w]))
            for cp in group:
                cp.start()
            first += group
        save(me)
        locs = [pltpu.make_async_copy(pb, slot(1, me), loc_sems.at[0]),
                pltpu.make_async_copy(wob, wout_o.at[my_idx], loc_sems.at[1]),
                pltpu.make_async_copy(wob, woutb_o, loc_sems.at[2])]
        for cp in locs:
            cp.start()

        fetch_x.wait()
        xv = xbuf[...]
        hv[...] = _bf(xv * lax.rsqrt(jnp.mean(xv * xv, axis=-1, keepdims=True) + EPS) * g_ref[...])
        rows = 256
        for r0 in range(0, T, rows):
            htv[:, r0:r0 + rows] = hv[r0:r0 + rows, :].T
        locs.append(pltpu.make_async_copy(htv, ht_o, loc_sems.at[3]))
        locs[-1].start()

        def out_copy(p, j):
            return pltpu.make_async_copy(stage.at[p], proj_o.at[j], out_sems.at[p])

        def project(nth, block):
            base = 3 * index(*block)

            def tile(p, carry):
                if nth > 0:
                    out_copy(p, base + p).wait()
                stage[p] = _dot(hv[...], wv[base + p], 1, 0)
                out_copy(p, base + p).start()
                return carry

            lax.fori_loop(0, 3, tile, 0)

        project(0, me)
        copy(0, 0, sibling, me).wait_recv()
        save(sibling)
        project(1, sibling)
        passed = []
        relay_from = (px ^ (1 - c), py ^ c, c)
        relay_to = (px ^ c, py ^ (1 - c), c)

        def arrived(w, j):
            copy(1 + j, w, (*chips[j], c), me).wait_recv()
            passed.append(copy(4 + j, w, (*chips[j], c), sibling))
            passed[-1].start()

        def relay(w):
            passed.append(copy(3, w, relay_from, relay_to))
            passed[-1].start()

        def handed(nth, j):
            copy(4 + j, 0, (*chips[j], 1 - c), me).wait_recv()
            save((*chips[j], 1 - c))
            project(nth, (*chips[j], 1 - c))

        arrived(0, 0)
        arrived(0, 1)
        relay(0)
        for j in range(2):
            save((*chips[j], c))
            project(2 + j, (*chips[j], c))
        handed(4, 0)
        handed(5, 1)
        arrived(1, 0)
        arrived(1, 1)
        relay(1)
        arrived(0, 2)
        save((*chips[2], c))
        project(6, (*chips[2], c))
        handed(7, 2)
        arrived(1, 2)
        copy(0, 1, sibling, me).wait_recv()
        for j, chip in enumerate(chips):
            copy(4 + j, 1, (*chip, 1 - c), me).wait_recv()
        keep = pltpu.make_async_copy(wv, wt_o, loc_sems.at[4])
        for p in range(3):
            out_copy(p, p).wait()
        for cp in first + passed:
            cp.wait_send()
        keep.wait()
        for cp in locs:
            cp.wait()

    vmem = pl.BlockSpec(memory_space=pltpu.VMEM)
    bf16 = jnp.bfloat16
    return pl.pallas_call(
        body, name="gather_proj",
        out_shape=(pltpu.HBM((D, T), bf16), pltpu.HBM((NTILE, D, TILE), bf16),
                   pltpu.HBM((DMIX // NDEV, D), bf16), pltpu.HBM((NDEV, DMIX // NDEV, D), bf16),
                   pltpu.HBM((NDEV, NGROUP, GROUP // NDEV, GROUP), bf16), pltpu.HBM((NTILE, T, TILE), jnp.float32)),
        in_specs=[ANY] + [vmem] * 4, out_specs=(ANY,) * 6,
        scratch_shapes=[pltpu.VMEM((T, D), jnp.float32),
                        pltpu.VMEM((T, D), bf16), pltpu.VMEM((D, T), bf16), pltpu.VMEM((NTILE, D, TILE), bf16),
                        pltpu.VMEM((DMIX // NDEV, D), bf16), pltpu.VMEM((NGROUP, GROUP // NDEV, GROUP), bf16),
                        pltpu.VMEM((3, T, TILE), jnp.float32),
                        pltpu.SemaphoreType.DMA((14,)), pltpu.SemaphoreType.DMA((14,)),
                        pltpu.SemaphoreType.DMA((6,)), pltpu.SemaphoreType.DMA((3,))],
        compiler_params=_params(vmem_mib=56),
    )(x, g1, w_in, w_out, pool_w)


def _split_start(name, arrays, n_copies, plan):
    k = len(arrays)

    def body(*refs):
        send_sems, recv_sems, token = refs[k], refs[k + 1], refs[-1]
        for i, (src, dst, to) in enumerate(plan(refs[:k])):
            pltpu.make_async_remote_copy(src_ref=src, dst_ref=dst, send_sem=send_sems.at[i],
                                         recv_sem=recv_sems.at[i], device_id=to, device_id_type=MESH).start()
        token[...] = jnp.zeros_like(token)

    out = pl.pallas_call(
        body, name=name,
        out_shape=(pltpu.SemaphoreType.DMA((n_copies,)), pltpu.SemaphoreType.DMA((n_copies,)),
                   *[pltpu.HBM(a.shape, a.dtype) for a in arrays], jax.ShapeDtypeStruct((8, 128), jnp.float32)),
        in_specs=[HBM] * k, out_specs=(SEM, SEM, *[HBM] * k, pl.BlockSpec(memory_space=pltpu.VMEM)),
        input_output_aliases={i: 2 + i for i in range(k)},
        compiler_params=pltpu.CompilerParams(has_side_effects=EFFECT),
    )(*[pltpu.with_memory_space_constraint(a, pltpu.HBM) for a in arrays])
    return out[0], out[1], out[2:2 + k], out[-1]


def _split_wait(name, arrays, send_sems, recv_sems, plan, after):
    k = len(arrays)

    def body(*refs):
        sends, recvs = refs[k], refs[k + 1]
        for i, (src, dst, to) in enumerate(plan(refs[:k])):
            cp = pltpu.make_async_remote_copy(src_ref=src, dst_ref=dst, send_sem=sends.at[i], recv_sem=recvs.at[i],
                                              device_id=to, device_id_type=MESH)
            cp.wait_send()
            cp.wait_recv()

    return pl.pallas_call(
        body, name=name,
        out_shape=tuple(pltpu.HBM(a.shape, a.dtype) for a in arrays),
        in_specs=[HBM] * k + [SEM, SEM, ANY], out_specs=(HBM,) * k,
        input_output_aliases={i: i for i in range(k)},
        compiler_params=pltpu.CompilerParams(has_side_effects=EFFECT),
    )(*arrays, send_sems, recv_sems, after)


def _plan_wout(refs):
    src, land = refs
    x, y, c, me = _place()
    return [(src, land.at[me], _peer(x, y, c, r)) for r in range(1, NDEV)]


def _plan_rest(refs):
    gob, gpf, r_out, r_pool = refs
    x, y, c, me = _place()
    plan = []
    for r in range(1, NDEV):
        plan.append((gob.at[me ^ r], r_out.at[r - 1], _peer(x, y, c, r)))
        plan.append((gpf.at[me ^ r], r_pool.at[r - 1], _peer(x, y, c, r)))
    return plan


def _plan_in(refs):
    sums, landing = refs
    x, y, c, _ = _place()
    plan = []
    for rel, (dx, dy) in enumerate(((1, 0), (0, 1), (1, 1))):
        for p in range(3):
            plan.append((sums.at[rel, p], landing.at[rel, p], (x ^ dx, y ^ dy, c)))
    return plan


def _plan_small(refs):
    small, land = refs
    x, y, c, me = _place()
    return [(small, land.at[me], _peer(x, y, c, r)) for r in range(1, NDEV)]


def _seg_tiles(s):
    return (s + 2) % NSEG


_POOL_SPECS = [pl.BlockSpec((None, T, GROUP), lambda g, base=base: (base + g, 0, 0)) for base in (0, 4)]
_HEAD_SPECS = [pl.BlockSpec((None, T, HEAD), lambda h, base=base: (base + h // 2, 0, h % 2))
               for base in (8, 12, 16, 20)]


_POOL_W_SPEC = pl.BlockSpec((NDEV, None, GROUP // NDEV, GROUP), lambda g: (0, g, 0, 0))


def _row_ids(shape):
    return lax.broadcasted_iota(jnp.int32, shape, 0)


BAND_ROWS = 128
HALO = 16


def _window_sum(a, gidx, lead):
    width = lax.shift_left(jnp.int32(2), gidx)
    shape = (BAND_ROWS, BAND_ROWS + HALO)
    t, j = lax.broadcasted_iota(jnp.int32, shape, 0), lax.broadcasted_iota(jnp.int32, shape, 1)
    first = t if lead else t + HALO - width + 1
    band = _bf(jnp.where(j >= first, jnp.where(j < first + width, 1.0, 0.0), 0.0))
    zeros = jnp.zeros((HALO, a.shape[1]), jnp.bfloat16)
    padded = [jnp.concatenate([p, zeros] if lead else [zeros, p], axis=0) for p in _split2(a)]
    out = []
    for r0 in range(0, T, BAND_ROWS):
        slab = jnp.concatenate([p[r0:r0 + BAND_ROWS + HALO] for p in padded], axis=1)
        r = _dot(band, slab, 1, 0)
        out.append(r[:, :a.shape[1]] + r[:, a.shape[1]:])
    return jnp.concatenate(out, axis=0)


def _window_mean(s, gidx):
    inv = jnp.where(gidx == 0, 0.5, jnp.where(gidx == 1, 0.25, jnp.where(gidx == 2, 0.125, 0.0625)))
    width = lax.shift_left(jnp.int32(2), gidx)
    head = s[:16] / jnp.minimum(_row_ids((16, s.shape[1])) + 1, width).astype(jnp.float32)
    return jnp.concatenate([head, s[16:] * inv], axis=0)


def _pool_fwd(proj, pool_w, pool_scale, token):
    def body(u_ref, pg_ref, w_ref, sc_ref, token_any, y_ref):
        del token_any
        gidx = pl.program_id(0)
        u, pg = u_ref[...], pg_ref[...]
        d = _window_mean(_window_sum(u, gidx, False), gidx) - u
        mixed = _dot(_bf(d), w_ref[...].reshape(GROUP, GROUP), 1, 0)
        y_ref[...] = _bf(mixed * sc_ref[...] * (pg * _sigmoid(pg)))

    return pl.pallas_call(
        body, name="pool_fwd", grid=(NGROUP,),
        in_specs=[*_POOL_SPECS, _POOL_W_SPEC, pl.BlockSpec((1, GROUP), lambda g: (0, g)), ANY],
        out_specs=pl.BlockSpec((T, GROUP), lambda g: (0, g)),
        out_shape=pltpu.HBM((T, DMIX), jnp.bfloat16),
        compiler_params=_params(("parallel",)),
    )(proj, proj, pool_w, pool_scale, token)


def _tri(lower):
    r = lax.broadcasted_iota(jnp.int32, (CHUNK, CHUNK), 0)
    c = lax.broadcasted_iota(jnp.int32, (CHUNK, CHUNK), 1)
    return (r >= c) if lower else (r <= c)


def _sum_rows_matrix():
    shape = (CHUNK + 16, CHUNK)
    r, c = lax.broadcasted_iota(jnp.int32, shape, 0), lax.broadcasted_iota(jnp.int32, shape, 1)
    run = jnp.where(c <= r, 1.0, 0.0)
    half = jnp.where(c < CHUNK // 2, 1.0, 0.0)
    return _bf(jnp.where(r < CHUNK, run, jnp.where(r < CHUNK + 8, 1.0, half)))


def _rev_sum_matrix():
    shape = (CHUNK, 2 * CHUNK)
    r, c = lax.broadcasted_iota(jnp.int32, shape, 0), lax.broadcasted_iota(jnp.int32, shape, 1)
    return _bf(jnp.where(c < CHUNK, jnp.where(c >= r, 1.0, 0.0), jnp.where(c - CHUNK < r, 1.0, 0.0)))


def _split2(a):
    hi = _bf(a)
    return [hi, _bf(a - hi.astype(jnp.float32))]


def _exact_sums(mat, pieces):
    x = jnp.concatenate([s for p in pieces for s in _split2(p)], axis=1)
    r = _dot(mat, x, 1, 0)
    return [r[:, 2 * j * HEAD:(2 * j + 1) * HEAD] + r[:, (2 * j + 1) * HEAD:(2 * j + 2) * HEAD]
            for j in range(len(pieces))]


def _gates(qv, fl, lb):
    sq = _sigmoid(qv)
    sg = _sigmoid(fl)
    f = lb + (1.0 - lb) * sg
    return dict(sq=sq, qs=qv * sq, sg=sg, f=f, kk=1.0 - f, g=jnp.log2(f))


def _decays(sums):
    big_g = sums[:CHUNK]
    total = sums[CHUNK:CHUNK + 8]
    g_last = jnp.tile(total, (CHUNK // 8, 1))
    g_mid = jnp.tile(sums[CHUNK + 8:], (CHUNK // 8, 1))
    return dict(
        e_q=jnp.exp2(big_g),
        e_k=jnp.exp2(g_last - big_g),
        e_qm=jnp.exp2(jnp.minimum(big_g - g_mid, EXP_CAP)),
        e_km=jnp.exp2(jnp.minimum(g_mid - big_g, EXP_CAP)),
        total8=jnp.exp2(total))


def _group_rows(gi):
    return [pl.ds(pl.multiple_of((gi * NB + j) * CHUNK, CHUNK), CHUNK) for j in range(NB)]


def _lower_bound(lb_ref):
    return _sigmoid(lb_ref[0:1, :] - lb_ref[1:2, :])


def _hgrn_fwd(proj, lb_logits, rec_g, y_in):
    def body(q_ref, f_ref, i_ref, gate_ref, lb_ref, rg_ref, y_any, y_ref, o_ref, st_ref):
        del y_any
        lb = _lower_bound(lb_ref)
        causal = _tri(True)
        smat = _sum_rows_matrix()

        def group(gi, st):
            rows = _group_rows(gi)
            ts = [_gates(q_ref[r, :], f_ref[r, :], lb) for r in rows]
            ds = [_decays(s) for s in _exact_sums(smat, [t["g"] for t in ts])]
            vs = [_bf(i_ref[r, :]) for r in rows]
            q_m = [_bf(t["qs"] * d["e_qm"]) for t, d in zip(ts, ds)]
            k_m = [_bf(t["kk"] * d["e_km"]) for t, d in zip(ts, ds)]
            q_e = [_bf(t["qs"] * d["e_q"]) for t, d in zip(ts, ds)]
            k_e = [_bf(t["kk"] * d["e_k"]) for t, d in zip(ts, ds)]
            a = [_bf(jnp.where(causal, _dot(q_m[j], k_m[j], 1, 1), 0.0)) for j in range(NB)]
            intra = [_dot(a[j], vs[j], 1, 0) for j in range(NB)]
            upd = [_dot(vs[j], k_e[j], 0, 0) for j in range(NB)]
            for j in range(NB):
                st_ref[gi * NB + j] = st
                o_ref[rows[j], :] = intra[j] + _dot(q_e[j], _bf(st), 1, 1)
                st = st * jnp.tile(ds[j]["total8"], (HEAD // 8, 1)) + upd[j]
            return st

        lax.fori_loop(0, NGRP, group, jnp.zeros((HEAD, HEAD), jnp.float32))
        o = o_ref[...]
        rn = o * lax.rsqrt(jnp.mean(o * o, axis=-1, keepdims=True) + EPS)
        gate = gate_ref[...]
        y_ref[...] = _bf(rn * rg_ref[...] * (gate * _sigmoid(gate)))

    return pl.pallas_call(
        body, name="hgrn_fwd", grid=(NHEAD,),
        in_specs=[*_HEAD_SPECS,
                  pl.BlockSpec((2, HEAD), lambda h: (0, h)),
                  pl.BlockSpec((1, HEAD), lambda h: (0, h)),
                  pl.BlockSpec(memory_space=pl.ANY)],
        out_specs=(pl.BlockSpec((T, HEAD), lambda h: (0, NHEAD + h)),
                   pl.BlockSpec((T, HEAD), lambda h: (0, h)),
                   pl.BlockSpec((None, NCHUNK, HEAD, HEAD), lambda h: (h, 0, 0, 0))),
        out_shape=(pltpu.HBM((T, DMIX), jnp.bfloat16), pltpu.HBM((T, D), jnp.float32),
                   pltpu.HBM((NHEAD, NCHUNK, HEAD, HEAD), jnp.float32)),
        input_output_aliases={6: 0},
        compiler_params=_params(("parallel",)),
    )(proj, proj, proj, proj, lb_logits, rec_g, y_in)


def _out_proj_loss(x, y, w_out, target, gf):
    rows = 512
    parts = [slice(k * rows // 2, (k + 1) * rows // 2) for k in range(2)]

    def body(x_ref, y_ref, w_ref, t_ref, g_ref, dz_ref, dzb_ref, sq_ref, dg_ref):
        zs = [x_ref[p, :] + _dot(y_ref[p, :], w_ref[...], 1, 0) for p in parts]
        sq = dg = 0.0
        for p, z in zip(parts, zs):
            r = lax.rsqrt(jnp.mean(z * z, axis=-1, keepdims=True) + EPS)
            zhat = z * r
            err = zhat * g_ref[...] - t_ref[p, :]
            dy = err * (1.0 / D)
            gdy = dy * g_ref[...]
            dz = r * (gdy - zhat * jnp.mean(zhat * gdy, axis=-1, keepdims=True))
            dz_ref[p, :] = dz
            dzb_ref[p, :] = _bf(dz)
            sq = sq + jnp.sum(err * err, axis=0, keepdims=True)
            dg = dg + jnp.sum(zhat * dy, axis=0, keepdims=True)

        @pl.when(pl.program_id(0) == 0)
        def _():
            sq_ref[...] = sq
            dg_ref[...] = dg

        @pl.when(pl.program_id(0) != 0)
        def _():
            sq_ref[...] += sq
            dg_ref[...] += dg

    tile = pl.BlockSpec((rows, D), lambda i: (i, 0))
    vec = pl.BlockSpec((1, D), lambda i: (0, 0))
    return pl.pallas_call(
        body, name="out_proj_loss", grid=(T // rows,),
        in_specs=[tile, pl.BlockSpec((rows, DMIX), lambda i: (i, 0)), pl.BlockSpec((DMIX, D), lambda i: (0, 0)),
                  tile, vec],
        out_specs=(tile, tile, vec, vec),
        out_shape=(pltpu.HBM((T, D), jnp.float32), pltpu.HBM((T, D), jnp.bfloat16),
                   pltpu.HBM((1, D), jnp.float32), pltpu.HBM((1, D), jnp.float32)),
        compiler_params=_params(("arbitrary",)),
    )(x, y, w_out, target, gf)


def _out_proj_bwd(dzb, w_out, y):
    tn = 512

    def body(dz_ref, w_ref, y_ref, dy_ref, gw_ref, gwb_ref):
        dz = dz_ref[...]
        dy_ref[...] = _dot(dz, w_ref[...], 1, 1)
        gw = _dot(y_ref[...], dz, 0, 0)
        gw_ref[...] = gw
        gwb_ref[...] = _bf(gw)

    return pl.pallas_call(
        body, name="out_proj_bwd", grid=(DMIX // tn,),
        in_specs=[pl.BlockSpec((T, D), lambda n: (0, 0)), pl.BlockSpec((tn, D), lambda n: (n, 0)),
                  pl.BlockSpec((T, tn), lambda n: (0, n))],
        out_specs=(pl.BlockSpec((T, tn), lambda n: (0, n)), pl.BlockSpec((tn, D), lambda n: (n, 0)),
                   pl.BlockSpec((tn, D), lambda n: (n, 0))),
        out_shape=(pltpu.HBM((T, DMIX), jnp.float32), pltpu.HBM((DMIX, D), jnp.float32),
                   pltpu.HBM((DMIX, D), jnp.bfloat16)),
        compiler_params=_params(("parallel",)),
    )(_in_hbm(dzb), w_out, y)


def _hgrn_bwd(proj, lb_logits, rec_g, o, states, dymix, dproj_in, token):
    def body(q_ref, f_ref, i_ref, gate_ref, lb_ref, rg_ref, o_ref, st_ref, dy_ref, dp_any, token_any,
             dp_ref, drg_ref, dlb_ref, do_ref):
        del dp_any, token_any
        lb = _lower_bound(lb_ref)
        causal = _tri(True)
        smat, rmat = _sum_rows_matrix(), _rev_sum_matrix()

        o = o_ref[...]
        rs = lax.rsqrt(jnp.mean(o * o, axis=-1, keepdims=True) + EPS)
        rn = o * rs
        gate = gate_ref[...]
        sgate = _sigmoid(gate)
        dyv = dy_ref[...]
        d_r = dyv * (gate * sgate)
        dp_ref[3] = _bf(dyv * (rn * rg_ref[...]) * (sgate * (1.0 + gate * (1.0 - sgate))))
        drg_ref[...] = jnp.sum(d_r * rn, axis=0, keepdims=True)
        drn = d_r * rg_ref[...]
        do_ref[...] = rs * (drn - rn * jnp.mean(rn * drn, axis=-1, keepdims=True))

        def group(i, carry):
            dst, dlb = carry
            gi = NGRP - 1 - i
            rows = _group_rows(gi)
            span = range(NB)
            qvs = [q_ref[r, :] for r in rows]
            ts = [_gates(qv, f_ref[r, :], lb) for qv, r in zip(qvs, rows)]
            ds = [_decays(s) for s in _exact_sums(smat, [t["g"] for t in ts])]
            vs = [_bf(i_ref[r, :]) for r in rows]
            dos = [_bf(do_ref[r, :]) for r in rows]
            sts = [st_ref[gi * NB + j] for j in span]
            qe_f = [t["qs"] * d["e_q"] for t, d in zip(ts, ds)]
            ke_f = [t["kk"] * d["e_k"] for t, d in zip(ts, ds)]
            q_e, k_e = [_bf(a) for a in qe_f], [_bf(a) for a in ke_f]
            q_m = [_bf(t["qs"] * d["e_qm"]) for t, d in zip(ts, ds)]
            k_m = [_bf(t["kk"] * d["e_km"]) for t, d in zip(ts, ds)]
            a = [_bf(jnp.where(causal, _dot(q_m[j], k_m[j], 1, 1), 0.0)) for j in span]
            da = [_bf(jnp.where(causal, _dot(dos[j], vs[j], 1, 1), 0.0)) for j in span]
            dqm = [_dot(da[j], k_m[j], 1, 0) for j in span]
            dkm = [_dot(da[j], q_m[j], 0, 0) for j in span]
            dv_in = [_dot(a[j], dos[j], 0, 0) for j in span]
            dqe = [_dot(dos[j], _bf(sts[j]), 1, 0) for j in span]
            grow = [_dot(dos[j], q_e[j], 0, 0) for j in span]
            dke, carried = [None] * NB, [None] * NB
            for j in reversed(span):
                dst_b = _bf(dst)
                dke[j] = _dot(vs[j], dst_b, 1, 0)
                dp_ref[2, rows[j], :] = _bf(dv_in[j] + _dot(k_e[j], dst_b, 1, 1))
                carried[j] = ds[j]["total8"] * jnp.sum(dst * sts[j], axis=0, keepdims=True)
                dst = dst * jnp.tile(ds[j]["total8"], (HEAD // 8, 1)) + grow[j]
            kdk = [ke_f[j] * dke[j] for j in span]
            pos = [(q_m[j].astype(jnp.float32) * dqm[j] - k_m[j].astype(jnp.float32) * dkm[j]) + qe_f[j] * dqe[j]
                   for j in span]
            dgs = _exact_sums(rmat, [jnp.concatenate([pos[j], kdk[j]], axis=0) for j in span])
            for j in span:
                t, d = ts[j], ds[j]
                dg = dgs[j] + jnp.tile(carried[j], (CHUNK // 8, 1))
                dqs = dqm[j] * d["e_qm"] + dqe[j] * d["e_q"]
                dkk = dkm[j] * d["e_km"] + dke[j] * d["e_k"]
                df = dg / t["f"] - dkk
                dp_ref[1, rows[j], :] = _bf(df * (1.0 - lb) * (t["sg"] * (1.0 - t["sg"])))
                dp_ref[0, rows[j], :] = _bf(dqs * (t["sq"] * (1.0 + qvs[j] * (1.0 - t["sq"]))))
                dlb = dlb + df * (1.0 - t["sg"])
            return dst, dlb

        _, dlb = lax.fori_loop(0, NGRP, group, (jnp.zeros((HEAD, HEAD), jnp.float32),
                                                jnp.zeros((CHUNK, HEAD), jnp.float32)))
        dlb_ref[...] = jnp.sum(dlb, axis=0, keepdims=True)

    vec = pl.BlockSpec((1, HEAD), lambda h: (0, h))
    return pl.pallas_call(
        body, name="hgrn_bwd", grid=(NHEAD,),
        in_specs=[*_HEAD_SPECS,
                  pl.BlockSpec((2, HEAD), lambda h: (0, h)), vec,
                  pl.BlockSpec((T, HEAD), lambda h: (0, h)),
                  pl.BlockSpec((None, NCHUNK, HEAD, HEAD), lambda h: (h, 0, 0, 0)),
                  pl.BlockSpec((T, HEAD), lambda h: (0, NHEAD + h)), ANY, ANY],
        out_specs=(pl.BlockSpec((4, T, HEAD), lambda h: (0, 0, h)), vec, vec),
        out_shape=(pltpu.HBM((NSEG, T, D), jnp.bfloat16),
                   pltpu.HBM((1, D), jnp.float32), pltpu.HBM((1, D), jnp.float32)),
        scratch_shapes=[pltpu.VMEM((T, HEAD), jnp.float32)],
        input_output_aliases={9: 0},
        compiler_params=_params(("parallel",)),
    )(proj, proj, proj, proj, lb_logits, rec_g, o, states, dymix, dproj_in, token)


def _pool_bwd(proj, pool_w, pool_scale, dymix):
    def body(u_ref, pg_ref, w_ref, sc_ref, dy_ref, dp_ref, gw_ref, gs_ref):
        gidx = pl.program_id(0)
        u, pg = u_ref[...], pg_ref[...]
        w = w_ref[...].reshape(GROUP, GROUP)
        d = _bf(_window_mean(_window_sum(u, gidx, False), gidx) - u)
        mixed = _dot(d, w, 1, 0)
        spg = _sigmoid(pg)
        dyv = dy_ref[...]
        d_p = dyv * (pg * spg)
        dp_ref[1] = _bf(dyv * (mixed * sc_ref[...]) * (spg * (1.0 + pg * (1.0 - spg))))
        gs_ref[...] = jnp.sum(d_p * mixed, axis=0, keepdims=True)
        dmixed = _bf(d_p * sc_ref[...])
        gw_ref[...] = _dot(d, dmixed, 0, 0).reshape(gw_ref.shape)
        dd = _dot(dmixed, w, 1, 1)
        dp_ref[0] = _bf(_window_sum(_window_mean(dd, gidx), gidx, True) - dd)

    return pl.pallas_call(
        body, name="pool_bwd", grid=(NGROUP,),
        in_specs=[*_POOL_SPECS, _POOL_W_SPEC,
                  pl.BlockSpec((1, GROUP), lambda g: (0, g)),
                  pl.BlockSpec((T, GROUP), lambda g: (0, g))],
        out_specs=(pl.BlockSpec((2, T, GROUP), lambda g: (2, 0, g)), _POOL_W_SPEC,
                   pl.BlockSpec((1, GROUP), lambda g: (0, g))),
        out_shape=(pltpu.HBM((NSEG, T, D), jnp.bfloat16),
                   pltpu.HBM((NDEV, NGROUP, GROUP // NDEV, GROUP), jnp.float32),
                   pltpu.HBM((1, D), jnp.float32)),
        compiler_params=_params(("parallel",)),
    )(proj, proj, pool_w, pool_scale, dymix)


HALF = NTILE // 2
AWAY = HALF - 3


def _dproj_tile(chip, side, p):
    j = 6 * chip + 3 * side + p
    return ((j // 4 + 4) % NSEG, 0, j % 4)


def _sibling_copy(sib_out, sib_in, send_sems, recv_sems, slot):
    x, y, c, _ = _place()
    return pltpu.make_async_remote_copy(
        src_ref=sib_out.at[slot], dst_ref=sib_in.at[slot], send_sem=send_sems.at[slot],
        recv_sem=recv_sems.at[slot], device_id=(x, y, 1 - c), device_id_type=MESH)


def _proj_bwd_w_far(place, ht, dproj):
    def body(place_ref, h_ref, dp_ref, sib_out, sib_in, send_sems, recv_sems, stage, loc_sems):
        del place_ref
        i = pl.program_id(0)

        def to_hbm(k):
            return pltpu.make_async_copy(stage.at[k], sib_out.at[k], loc_sems.at[k])

        def send(k):
            to_hbm(k).wait()
            _sibling_copy(sib_out, sib_in, send_sems, recv_sems, k).start()

        stage[i] = _bf(_dot(h_ref[...], dp_ref[...], 1, 0))

        @pl.when(i > 0)
        def _():
            send(i - 1)

        to_hbm(i).start()

        @pl.when(i == HALF - 1)
        def _():
            send(i)

    buf = pltpu.HBM((HALF, D, TILE), jnp.bfloat16)
    sems = pltpu.SemaphoreType.DMA((HALF,))
    return pl.pallas_call(
        body, name="proj_bwd_w_far",
        grid_spec=pltpu.PrefetchScalarGridSpec(
            num_scalar_prefetch=1, grid=(HALF,),
            in_specs=[pl.BlockSpec((D, T), lambda i, pr: (0, 0)),
                      pl.BlockSpec((None, T, TILE), lambda i, pr: _dproj_tile(i // 3, 1 - pr[2], i % 3))],
            out_specs=(HBM, HBM, SEM, SEM),
            scratch_shapes=[pltpu.VMEM((HALF, D, TILE), jnp.bfloat16), pltpu.SemaphoreType.DMA((HALF,))]),
        out_shape=(buf, buf, sems, sems),
        compiler_params=pltpu.CompilerParams(dimension_semantics=("arbitrary",), vmem_limit_bytes=48 * MIB,
                                             has_side_effects=EFFECT),
    )(place, ht, dproj)


def _proj_bwd_w_near(place, ht, dproj, sib_out, sib_in, sib_send, sib_recv):
    def owner_chip(k, pr):
        return jnp.where(k < AWAY, (pr[1] + 1 + k % 3) % 4, pr[1])

    def tile_p(k):
        return jnp.where(k < AWAY, k // 3, k - AWAY)

    def body(place_ref, h_ref, dp_ref, sib_out, sib_in, sib_send, sib_recv, sums, own_ref, landing, out_send,
             out_recv, recvbuf, outbuf, in_sems, loc_sems):
        i = pl.program_id(0)
        px, py, c, _ = _place()

        def slot_of(k):
            return 3 * owner_chip(k, place_ref) + tile_p(k)

        def load(k):
            return pltpu.make_async_copy(sib_in.at[slot_of(k)], recvbuf.at[k % 2], in_sems.at[k % 2])

        def fetch(k):
            _sibling_copy(sib_out, sib_in, sib_send, sib_recv, slot_of(k)).wait_recv()
            load(k).start()

        def route(k):
            chip = owner_chip(k, place_ref)
            cx, cy = chip // 2, chip % 2
            return cx, cy, (cx ^ px) + 2 * (cy ^ py) - 1, tile_p(k)

        def to_hbm(k):
            _, _, rel, p = route(k)
            return pltpu.make_async_copy(outbuf.at[k], sums.at[rel, p], loc_sems.at[k])

        def to_owner(k):
            cx, cy, rel, p = route(k)
            return pltpu.make_async_remote_copy(
                src_ref=sums.at[rel, p], dst_ref=landing.at[rel, p], send_sem=out_send.at[3 * rel + p],
                recv_sem=out_recv.at[3 * rel + p], device_id=(cx, cy, c), device_id_type=MESH)

        @pl.when(i == 0)
        def _():
            fetch(i)

        @pl.when(i < HALF - 1)
        def _():
            fetch(i + 1)

        load(i).wait()
        total = _dot(h_ref[...], dp_ref[...], 1, 0) + recvbuf[i % 2].astype(jnp.float32)
        own_ref[...] = total
        outbuf[jnp.minimum(i, AWAY)] = _bf(total)

        @pl.when(i < AWAY)
        def _():
            to_hbm(i).start()

        @pl.when(jnp.logical_and(i > 0, i <= AWAY))
        def _():
            to_hbm(i - 1).wait()
            to_owner(i - 1).start()

        @pl.when(i == HALF - 1)
        def _():
            for slot in range(HALF):
                _sibling_copy(sib_out, sib_in, sib_send, sib_recv, slot).wait_send()

    travelling = pltpu.HBM((3, 3, D, TILE), jnp.bfloat16)
    sems = pltpu.SemaphoreType.DMA((AWAY,))
    return pl.pallas_call(
        body, name="proj_bwd_w_near",
        grid_spec=pltpu.PrefetchScalarGridSpec(
            num_scalar_prefetch=1, grid=(HALF,),
            in_specs=[pl.BlockSpec((D, T), lambda i, pr: (0, 0)),
                      pl.BlockSpec((None, T, TILE), lambda i, pr: _dproj_tile(owner_chip(i, pr), pr[2], tile_p(i))),
                      HBM, HBM, SEM, SEM],
            out_specs=(HBM, pl.BlockSpec((None, D, TILE), lambda i, pr: (jnp.where(i < AWAY, 0, i % 3), 0, 0)),
                       HBM, SEM, SEM),
            scratch_shapes=[pltpu.VMEM((2, D, TILE), jnp.bfloat16), pltpu.VMEM((AWAY + 1, D, TILE), jnp.bfloat16),
                            pltpu.SemaphoreType.DMA((2,)), pltpu.SemaphoreType.DMA((AWAY,))]),
        out_shape=(travelling, pltpu.HBM((3, D, TILE), jnp.float32), travelling, sems, sems),
        compiler_params=pltpu.CompilerParams(dimension_semantics=("arbitrary",), vmem_limit_bytes=48 * MIB,
                                             has_side_effects=EFFECT),
    )(place, ht, dproj, sib_out, sib_in, sib_send, sib_recv)


def _proj_bwd_x(dproj, w_t, x, g1, dz, others, token):
    tm = 512
    pairs = NSEG // 2
    k = len(others)

    def body(dp_ref, w_ref, x_ref, g_ref, dz_ref, *refs):
        dx_ref, dg_ref, wcat, acc = refs[k + 1:]
        m, s = pl.program_id(0), pl.program_id(1)

        @pl.when(m == 0)
        def _():
            for i in range(8):
                wcat[s, :, i * TILE:(i + 1) * TILE] = w_ref[i]

        @pl.when(s == 0)
        def _():
            acc[...] = jnp.zeros((tm, D), jnp.float32)

        acc[...] += _dot(jnp.concatenate([dp_ref[0], dp_ref[1]], axis=1), wcat[s], 1, 1)

        @pl.when(s == pairs - 1)
        def _():
            xv = x_ref[...]
            rs = lax.rsqrt(jnp.mean(xv * xv, axis=-1, keepdims=True) + EPS)
            xhat = xv * rs
            dhv = acc[...]
            gdh = dhv * g_ref[...]
            dx_ref[...] = dz_ref[...] + rs * (gdh - xhat * jnp.mean(xhat * gdh, axis=-1, keepdims=True))
            dg = jnp.sum(xhat * dhv, axis=0, keepdims=True)

            @pl.when(m == 0)
            def _():
                dg_ref[0:1, :] = dg
                for r, ref in enumerate(refs[:k]):
                    dg_ref[1 + r:2 + r, :] = ref[...]
                dg_ref[1 + k:, :] = jnp.zeros((7 - k, D), jnp.float32)

            @pl.when(m != 0)
            def _():
                dg_ref[0:1, :] += dg

    rows = pl.BlockSpec((tm, D), lambda m, s: (m, 0))
    vec = pl.BlockSpec((1, D), lambda m, s: (0, 0))
    return pl.pallas_call(
        body, name="proj_bwd_x", grid=(T // tm, pairs),
        in_specs=[pl.BlockSpec((2, tm, D), lambda m, s: (s, m, 0)),
                  pl.BlockSpec((8, D, TILE), lambda m, s: ((jnp.where(m == 0, s, pairs - 1) + 1) % pairs, 0, 0)),
                  rows, vec, rows, *[vec] * k, ANY],
        out_specs=(rows, pl.BlockSpec((8, D), lambda m, s: (0, 0))),
        out_shape=(jax.ShapeDtypeStruct((T, D), jnp.float32), pltpu.HBM((8, D), jnp.float32)),
        scratch_shapes=[pltpu.VMEM((pairs, D, 2 * D), jnp.bfloat16), pltpu.VMEM((tm, D), jnp.float32)],
        compiler_params=_params(("arbitrary", "arbitrary"), vmem_mib=56),
    )(dproj, w_t, x, g1, dz, *[_in_hbm(a) for a in others], token)


def _adamw(w, g, m, v):
    m_new = ADAM_B1 * m + (1.0 - ADAM_B1) * g
    v_new = ADAM_B2 * v + (1.0 - ADAM_B2) * (g * g)
    delta = -ADAM_LR * ((m_new / BC1) / (jnp.sqrt(v_new / BC2) + ADAM_EPS) + ADAM_WD * w)
    return delta, m_new, v_new


def _reduce_adam(name, place, parts, w, m, v, grid, w_spec):
    n = len(parts)

    def body(place_ref, *refs):
        del place_ref
        w_ref, m_ref, v_ref, g_ref, d_ref, mo_ref, vo_ref = refs[n:]
        g = None
        for ref, (_, _, stacked) in zip(refs[:n], parts):
            terms = [ref[r] for r in range(ref.shape[0])] if stacked else [ref[...]]
            for t in terms:
                if t.shape[-1] != w_ref.shape[-1]:
                    t = jnp.concatenate([t[p] for p in range(t.shape[0])], axis=1)
                g = t.astype(jnp.float32) if g is None else g + t.astype(jnp.float32)
        delta, m_new, v_new = _adamw(w_ref[...], g, m_ref[...], v_ref[...])
        g_ref[...] = g
        d_ref[...] = delta
        mo_ref[...] = m_new
        vo_ref[...] = v_new

    shape = jax.ShapeDtypeStruct(w.shape, jnp.float32)
    return pl.pallas_call(
        body, name=name,
        grid_spec=pltpu.PrefetchScalarGridSpec(
            num_scalar_prefetch=1, grid=grid,
            in_specs=[spec for _, spec, _ in parts] + [w_spec] * 3, out_specs=(w_spec,) * 4),
        out_shape=(shape,) * 4,
        compiler_params=_params(("parallel",)),
    )(place, *[_in_hbm(a) for a in [a for a, _, _ in parts] + [w, m, v]])


SMALL_ROWS = (1, 1, 2, 1, 1)


def _small_adam(place, own, parts, ws, ms, vs):
    n = len(SMALL_ROWS)

    def body(place_ref, own_ref, p_ref, *refs):
        ins, outs, bufs = refs[:3 * n], refs[3 * n:3 * n + 1 + 4 * n], refs[3 * n + 1 + 4 * n:]

        def stacked(group, buf):
            r0 = 0
            for ref, k in zip(group, SMALL_ROWS):
                buf[r0:r0 + k, :] = ref[...]
                r0 += k
            buf[r0:, :] = jnp.zeros((8 - r0, D), jnp.float32)
            return buf[...]

        wv, mv, vv = (stacked(ins[n * j:n * j + n], bufs[j]) for j in range(3))
        me = place_ref[0]
        g = None
        for s in range(NDEV):
            term = jnp.where(me == s, own_ref[...], p_ref[s])
            g = term if g is None else g + term
        rows = _row_ids(wv.shape)
        other = jnp.where(rows == 2, pltpu.roll(wv, 7, 0), jnp.where(rows == 3, pltpu.roll(wv, 1, 0), 0.0))
        lbv = _sigmoid(wv - other)
        sign = jnp.where(rows == 2, 1.0, -1.0)
        g = jnp.where((rows == 2) | (rows == 3), sign * g * lbv * (1.0 - lbv), g)
        delta, m_new, v_new = _adamw(wv, g, mv, vv)
        outs[0][...] = jnp.sum(g[6:7], axis=1, keepdims=True) * (0.5 / D)
        for j, val in enumerate((g, delta, m_new, v_new)):
            r0 = 0
            for ref, k in zip(outs[1 + n * j:1 + n * j + n], SMALL_ROWS):
                ref[...] = val[r0:r0 + k]
                r0 += k

    vmem = pl.BlockSpec(memory_space=pltpu.VMEM)
    shapes = [jax.ShapeDtypeStruct((k, D), jnp.float32) for k in SMALL_ROWS]
    out = pl.pallas_call(
        body, name="small_adam", out_shape=(jax.ShapeDtypeStruct((1, 1), jnp.float32), *shapes * 4),
        in_specs=[pl.BlockSpec(memory_space=pltpu.SMEM)] + [vmem] * (2 + 3 * n), out_specs=(vmem,) * (1 + 4 * n),
        scratch_shapes=[pltpu.VMEM((8, D), jnp.float32)] * 3,
    )(place, own, parts, *ws, *ms, *vs)
    return out[0], [out[1 + n * j:1 + n * j + n] for j in range(4)]


def kernel(x, norm1_g, w_in, pool_w, pool_scale, lb_logits, rec_norm_g, w_out, final_norm_g, loss_target, m_norm1_g, m_w_in, m_pool_w, m_pool_scale, m_lb_logits, m_rec_norm_g, m_w_out, m_final_norm_g, v_norm1_g, v_w_in, v_pool_w, v_pool_scale, v_lb_logits, v_rec_norm_g, v_w_out, v_final_norm_g):
    xs = x[0]
    target = loss_target[0]
    ix, iy, ic = lax.axis_index("x"), lax.axis_index("y"), lax.axis_index("c")
    place = jnp.stack([4 * ix + 2 * iy + ic, 2 * ix + iy, ic]).astype(jnp.int32)
    gf = final_norm_g.reshape(1, D)

    ht, w_t, w_out_b, w_out_g, pool_g, proj = _gather_proj(xs, norm1_g, w_in, w_out, pool_w)
    wout = [w_out_b, w_out_g]
    wout_send, wout_recv, wout, wout_token = _split_start("gather_wout_start", wout, NDEV - 1, _plan_wout)

    y = _pool_fwd(proj, pool_g, pool_scale, wout_token)
    y, o, states = _hgrn_fwd(proj, lb_logits, rec_norm_g, y)
    _, w_out_g = _split_wait("gather_wout_wait", wout, wout_send, wout_recv, _plan_wout, o)
    w_out_full = _in_hbm(w_out_g.reshape(DMIX, D))
    dz, dzb, sq, dgf = _out_proj_loss(xs, y, w_out_full, target, gf)

    dymix, gwout_f, gwout_b = _out_proj_bwd(dzb, w_out_full, y)
    dproj, gpool, dscale = _pool_bwd(proj, pool_g, pool_scale, dymix)

    blk_out = (NDEV, DMIX // NDEV, D)
    blk_pool = (NDEV, NGROUP, GROUP // NDEV, GROUP)
    rest = [gwout_b.reshape(blk_out), gpool,
            lax.empty((NDEV - 1,) + blk_out[1:], jnp.bfloat16), lax.empty((NDEV - 1,) + blk_pool[1:], jnp.float32)]
    rest_send, rest_recv, rest, rest_token = _split_start("scatter_rest_start", rest, 2 * (NDEV - 1), _plan_rest)

    dproj, drecg, dlb = _hgrn_bwd(proj, lb_logits, rec_norm_g, o, states, dymix, dproj, rest_token)
    chip_sums, own_sum, landing, win_send, win_recv = _proj_bwd_w_near(
        place, ht, dproj, *_proj_bwd_w_far(place, ht, dproj))
    win = [chip_sums, landing]

    others = [a.reshape(1, D) for a in (dscale, dlb, dlb, drecg, dgf, sq)]
    grad_x, small_block = _proj_bwd_x(dproj, w_t, xs, norm1_g, dz, others, chip_sums)

    small = [small_block, lax.empty((NDEV, 8, D), jnp.float32)]
    small_send, small_recv, small, small_token = _split_start("gather_small_start", small, NDEV - 1, _plan_small)

    _, gpool_own, r_out, r_pool = _split_wait("scatter_rest_wait", rest, rest_send, rest_recv, _plan_rest,
                                              small_token)
    g_wout, d_wout, m_wout, v_wout = _reduce_adam(
        "adam_w_out", place,
        [(gwout_f.reshape(blk_out), pl.BlockSpec((None,) + blk_out[1:], lambda i, pr: (pr[0], 0, 0)), False),
         (r_out, pl.BlockSpec((NDEV - 1,) + blk_out[1:], lambda i, pr: (0, 0, 0)), True)],
        w_out, m_w_out, v_w_out, (1,), pl.BlockSpec((None,) + blk_out[1:], lambda i, pr: (0, 0, 0)))
    g_pool, d_pool, m_pool, v_pool = _reduce_adam(
        "adam_pool_w", place,
        [(gpool_own, pl.BlockSpec((None,) + blk_pool[1:], lambda i, pr: (pr[0], 0, 0, 0)), False),
         (r_pool, pl.BlockSpec((NDEV - 1,) + blk_pool[1:], lambda i, pr: (0, 0, 0, 0)), True)],
        pool_w, m_pool_w, v_pool_w, (1,), pl.BlockSpec((None,) + blk_pool[1:], lambda i, pr: (0, 0, 0, 0)))

    _, r_in = _split_wait("scatter_win_wait", win, win_send, win_recv, _plan_in, d_pool)
    g_win, d_win, m_win, v_win = _reduce_adam(
        "adam_w_in", place,
        [(own_sum, pl.BlockSpec((3, D // 8, TILE), lambda i, pr: (0, i, 0)), False),
         (r_in, pl.BlockSpec((3, 3, D // 8, TILE), lambda i, pr: (0, 0, i, 0)), True)],
        w_in, m_w_in, v_w_in, (8,), pl.BlockSpec((None, D // 8, 3 * TILE), lambda i, pr: (0, i, 0)))

    own_small, r_small = _split_wait("gather_small_wait", small, small_send, small_recv, _plan_small, d_win)
    loss, (g_s, d_s, m_s, v_s) = _small_adam(
        place, own_small, r_small,
        (norm1_g, pool_scale, lb_logits, rec_norm_g, gf),
        (m_norm1_g, m_pool_scale, m_lb_logits, m_rec_norm_g, m_final_norm_g.reshape(1, D)),
        (v_norm1_g, v_pool_scale, v_lb_logits, v_rec_norm_g, v_final_norm_g.reshape(1, D)))

    def outs(small, win, pool, wout):
        n1, ps, lbl, rg, fg = small
        return n1, win, pool, ps, lbl, rg, wout, fg.reshape(D)

    return (loss.reshape(()), grad_x[None],
            *outs(g_s, g_win, g_pool, g_wout), *outs(d_s, d_win, d_pool, d_wout),
            *outs(m_s, m_win, m_pool, m_wout), *outs(v_s, v_win, v_pool, v_wout))
```

```python
import functools

import jax
import jax.numpy as jnp
from jax import lax
from jax.experimental import pallas as pl
from jax.experimental.pallas import tpu as pltpu

T = 2048
D = 1024
NSEG = 6
NTILE = 24
TILE = 256
DMIX = 2048
NDEV = 8
HEAD = 128
NHEAD = 8
CHUNK = 64
NCHUNK = T // CHUNK
NB = 32
NGRP = NCHUNK // NB
NGROUP = 4
GROUP = 256
EPS = 1e-6
EXP_CAP = 115.0
MESH = pl.DeviceIdType.MESH
AXES = ("x", "y", "c")
ANY = pl.BlockSpec(memory_space=pl.ANY)
HBM = pl.BlockSpec(memory_space=pltpu.HBM)
SEM = pl.BlockSpec(memory_space=pltpu.SEMAPHORE)
EFFECT = pltpu.SideEffectType.DATAFLOW_SIDE_EFFECTING

ADAM_LR = 0.001
ADAM_B1 = 0.9
ADAM_B2 = 0.999
ADAM_EPS = 1e-08
ADAM_WD = 0.01
ADAM_STEP = 10
BC1 = 1.0 - ADAM_B1 ** ADAM_STEP
BC2 = 1.0 - ADAM_B2 ** ADAM_STEP

MIB = 1 << 20


def _params(sem=None, vmem_mib=48):
    return pltpu.CompilerParams(dimension_semantics=sem, vmem_limit_bytes=vmem_mib * MIB)


def _sigmoid(v):
    return 1.0 / (1.0 + jnp.exp(-v))


def _dot(a, b, ca, cb, precision=None):
    return lax.dot_general(a, b, (((ca,), (cb,)), ((), ())), precision=precision,
                           preferred_element_type=jnp.float32)


def _bf(v):
    return v.astype(jnp.bfloat16)


def _in_hbm(a):
    return pltpu.with_memory_space_constraint(a, pltpu.HBM)


def _place():
    x, y, c = lax.axis_index("x"), lax.axis_index("y"), lax.axis_index("c")
    return x, y, c, 4 * x + 2 * y + c


def _peer(x, y, c, r):
    return (x ^ ((r >> 2) & 1), y ^ ((r >> 1) & 1), c ^ (r & 1))


def _gather_proj(x, g1, w_in, w_out, pool_w):
    def body(x_ref, g_ref, win_ref, wout_ref, pool_ref, ht_o, wt_o, woutb_o, wout_o, pool_o, proj_o,
             xbuf, hv, htv, wv, wob, pb, stage, send_sems, recv_sems, loc_sems, out_sems):
        px, py, c, my_idx = _place()
        fetch_x = pltpu.make_async_copy(x_ref, xbuf, loc_sems.at[5])
        fetch_x.start()
        me, sibling = (px, py, c), (px, py, 1 - c)
        chips = [(1 - px, py), (px, 1 - py), (1 - px, 1 - py)]
        for p in range(3):
            wv[3 * my_idx + p] = _bf(win_ref[0, :, p * TILE:(p + 1) * TILE])

        def index(bx, by, bc):
            return 4 * bx + 2 * by + bc

        def slot(w, block):
            return wv.at[pl.ds(3 * index(*block), 3)] if w == 0 else pool_o.at[index(*block)]

        def copy(k, w, block, to, src=None):
            return pltpu.make_async_remote_copy(
                src_ref=slot(w, block) if src is None else src, dst_ref=slot(w, block),
                send_sem=send_sems.at[2 * k + w], recv_sem=recv_sems.at[2 * k + w],
                device_id=to, device_id_type=MESH)

        def save(block):
            at = pl.ds(3 * index(*block), 3)
            pltpu.make_async_copy(wv.at[at], wt_o.at[at], loc_sems.at[4]).start()

        srcs = (slot(0, me), pb)
        first = []
        for w in (0, 1):
            if w == 1:
                pb[...] = _bf(pool_ref[0])
                wob[...] = _bf(wout_ref[0])
            group = [copy(1 + j, w, me, (*chip, c), src=srcs[w]) for j, chip in enumerate(chips[:2])]
            group.append(copy(0, w, me, sibling, src=srcs[w]))
            for cp in group:
                cp.start()
            first += group
        save(me)
        locs = [pltpu.make_async_copy(pb, slot(1, me), loc_sems.at[0]),
                pltpu.make_async_copy(wob, wout_o.at[my_idx], loc_sems.at[1]),
                pltpu.make_async_copy(wob, woutb_o, loc_sems.at[2])]
        for cp in locs:
            cp.start()

        fetch_x.wait()
        xv = xbuf[...]
        hv[...] = _bf(xv * lax.rsqrt(jnp.mean(xv * xv, axis=-1, keepdims=True) + EPS) * g_ref[...])
        rows = 256
        for r0 in range(0, T, rows):
            htv[:, r0:r0 + rows] = hv[r0:r0 + rows, :].T
        locs.append(pltpu.make_async_copy(htv, ht_o, loc_sems.at[3]))
        locs[-1].start()

        def out_copy(p, j):
            return pltpu.make_async_copy(stage.at[p], proj_o.at[j], out_sems.at[p])

        def project(nth, block):
            base = 3 * index(*block)

            def tile(p, carry):
                if nth > 0:
                    out_copy(p, base + p).wait()
                stage[p] = _dot(hv[...], wv[base + p], 1, 0)
                out_copy(p, base + p).start()
                return carry

            lax.fori_loop(0, 3, tile, 0)

        project(0, me)
        copy(0, 0, sibling, me).wait_recv()
        save(sibling)
        project(1, sibling)
        passed = []
        relay_from = (px ^ (1 - c), py ^ c, c)
        relay_to = (px ^ c, py ^ (1 - c), c)

        def arrived(w, j):
            copy(1 + j, w, (*chips[j], c), me).wait_recv()
            passed.append(copy(4 + j, w, (*chips[j], c), sibling))
            passed[-1].start()

        def relay(w):
            passed.append(copy(3, w, relay_from, relay_to))
            passed[-1].start()

        def handed(nth, j):
            copy(4 + j, 0, (*chips[j], 1 - c), me).wait_recv()
            save((*chips[j], 1 - c))
            project(nth, (*chips[j], 1 - c))

        arrived(0, 0)
        arrived(0, 1)
        relay(0)
        for j in range(2):
            save((*chips[j], c))
            project(2 + j, (*chips[j], c))
        handed(4, 0)
        handed(5, 1)
        arrived(1, 0)
        arrived(1, 1)
        relay(1)
        arrived(0, 2)
        save((*chips[2], c))
        project(6, (*chips[2], c))
        handed(7, 2)
        arrived(1, 2)
        copy(0, 1, sibling, me).wait_recv()
        for j, chip in enumerate(chips):
            copy(4 + j, 1, (*chip, 1 - c), me).wait_recv()
        keep = pltpu.make_async_copy(wv, wt_o, loc_sems.at[4])
        for p in range(3):
            out_copy(p, p).wait()
        for cp in first + passed:
            cp.wait_send()
        keep.wait()
        for cp in locs:
            cp.wait()

    vmem = pl.BlockSpec(memory_space=pltpu.VMEM)
    bf16 = jnp.bfloat16
    return pl.pallas_call(
        body, name="gather_proj",
        out_shape=(pltpu.HBM((D, T), bf16), pltpu.HBM((NTILE, D, TILE), bf16),
                   pltpu.HBM((DMIX // NDEV, D), bf16), pltpu.HBM((NDEV, DMIX // NDEV, D), bf16),
                   pltpu.HBM((NDEV, NGROUP, GROUP // NDEV, GROUP), bf16), pltpu.HBM((NTILE, T, TILE), jnp.float32)),
        in_specs=[ANY] + [vmem] * 4, out_specs=(ANY,) * 6,
        scratch_shapes=[pltpu.VMEM((T, D), jnp.float32),
                        pltpu.VMEM((T, D), bf16), pltpu.VMEM((D, T), bf16), pltpu.VMEM((NTILE, D, TILE), bf16),
                        pltpu.VMEM((DMIX // NDEV, D), bf16), pltpu.VMEM((NGROUP, GROUP // NDEV, GROUP), bf16),
                        pltpu.VMEM((3, T, TILE), jnp.float32),
                        pltpu.SemaphoreType.DMA((14,)), pltpu.SemaphoreType.DMA((14,)),
                        pltpu.SemaphoreType.DMA((6,)), pltpu.SemaphoreType.DMA((3,))],
        compiler_params=_params(vmem_mib=56),
    )(x, g1, w_in, w_out, pool_w)


def _split_start(name, arrays, n_copies, plan):
    k = len(arrays)

    def body(*refs):
        send_sems, recv_sems, token = refs[k], refs[k + 1], refs[-1]
        for i, (src, dst, to) in enumerate(plan(refs[:k])):
            pltpu.make_async_remote_copy(src_ref=src, dst_ref=dst, send_sem=send_sems.at[i],
                                         recv_sem=recv_sems.at[i], device_id=to, device_id_type=MESH).start()
        token[...] = jnp.zeros_like(token)

    out = pl.pallas_call(
        body, name=name,
        out_shape=(pltpu.SemaphoreType.DMA((n_copies,)), pltpu.SemaphoreType.DMA((n_copies,)),
                   *[pltpu.HBM(a.shape, a.dtype) for a in arrays], jax.ShapeDtypeStruct((8, 128), jnp.float32)),
        in_specs=[HBM] * k, out_specs=(SEM, SEM, *[HBM] * k, pl.BlockSpec(memory_space=pltpu.VMEM)),
        input_output_aliases={i: 2 + i for i in range(k)},
        compiler_params=pltpu.CompilerParams(has_side_effects=EFFECT),
    )(*[pltpu.with_memory_space_constraint(a, pltpu.HBM) for a in arrays])
    return out[0], out[1], out[2:2 + k], out[-1]


def _split_wait(name, arrays, send_sems, recv_sems, plan, after):
    k = len(arrays)

    def body(*refs):
        sends, recvs = refs[k], refs[k + 1]
        for i, (src, dst, to) in enumerate(plan(refs[:k])):
            cp = pltpu.make_async_remote_copy(src_ref=src, dst_ref=dst, send_sem=sends.at[i], recv_sem=recvs.at[i],
                                              device_id=to, device_id_type=MESH)
            cp.wait_send()
            cp.wait_recv()

    return pl.pallas_call(
        body, name=name,
        out_shape=tuple(pltpu.HBM(a.shape, a.dtype) for a in arrays),
        in_specs=[HBM] * k + [SEM, SEM, ANY], out_specs=(HBM,) * k,
        input_output_aliases={i: i for i in range(k)},
        compiler_params=pltpu.CompilerParams(has_side_effects=EFFECT),
    )(*arrays, send_sems, recv_sems, after)


def _plan_wout(refs):
    src, land = refs
    x, y, c, me = _place()
    return [(src, land.at[me], _peer(x, y, c, r)) for r in range(1, NDEV)]


def _plan_rest(refs):
    gob, gpf, r_out, r_pool = refs
    x, y, c, me = _place()
    plan = []
    for r in range(1, NDEV):
        plan.append((gob.at[me ^ r], r_out.at[r - 1], _peer(x, y, c, r)))
        plan.append((gpf.at[me ^ r], r_pool.at[r - 1], _peer(x, y, c, r)))
    return plan


def _plan_in(refs):
    sums, landing = refs
    x, y, c, _ = _place()
    plan = []
    for rel, (dx, dy) in enumerate(((1, 0), (0, 1), (1, 1))):
        for p in range(3):
            plan.append((sums.at[rel, p], landing.at[rel, p], (x ^ dx, y ^ dy, c)))
    return plan


def _plan_small(refs):
    small, land = refs
    x, y, c, me = _place()
    return [(small, land.at[me], _peer(x, y, c, r)) for r in range(1, NDEV)]


def _seg_tiles(s):
    return (s + 2) % NSEG


_POOL_SPECS = [pl.BlockSpec((None, T, GROUP), lambda g, base=base: (base + g, 0, 0)) for base in (0, 4)]
_HEAD_SPECS = [pl.BlockSpec((None, T, HEAD), lambda h, base=base: (base + h // 2, 0, h % 2))
               for base in (8, 12, 16, 20)]


_POOL_W_SPEC = pl.BlockSpec((NDEV, None, GROUP // NDEV, GROUP), lambda g: (0, g, 0, 0))


def _row_ids(shape):
    return lax.broadcasted_iota(jnp.int32, shape, 0)


BAND_ROWS = 128
HALO = 16


def _window_sum(a, gidx, lead):
    width = lax.shift_left(jnp.int32(2), gidx)
    shape = (BAND_ROWS, BAND_ROWS + HALO)
    t, j = lax.broadcasted_iota(jnp.int32, shape, 0), lax.broadcasted_iota(jnp.int32, shape, 1)
    first = t if lead else t + HALO - width + 1
    band = _bf(jnp.where(j >= first, jnp.where(j < first + width, 1.0, 0.0), 0.0))
    zeros = jnp.zeros((HALO, a.shape[1]), jnp.bfloat16)
    padded = [jnp.concatenate([p, zeros] if lead else [zeros, p], axis=0) for p in _split2(a)]
    out = []
    for r0 in range(0, T, BAND_ROWS):
        slab = jnp.concatenate([p[r0:r0 + BAND_ROWS + HALO] for p in padded], axis=1)
        r = _dot(band, slab, 1, 0)
        out.append(r[:, :a.shape[1]] + r[:, a.shape[1]:])
    return jnp.concatenate(out, axis=0)


def _window_mean(s, gidx):
    inv = jnp.where(gidx == 0, 0.5, jnp.where(gidx == 1, 0.25, jnp.where(gidx == 2, 0.125, 0.0625)))
    width = lax.shift_left(jnp.int32(2), gidx)
    head = s[:16] / jnp.minimum(_row_ids((16, s.shape[1])) + 1, width).astype(jnp.float32)
    return jnp.concatenate([head, s[16:] * inv], axis=0)


def _pool_fwd(proj, pool_w, pool_scale, token):
    def body(u_ref, pg_ref, w_ref, sc_ref, token_any, y_ref):
        del token_any
        gidx = pl.program_id(0)
        u, pg = u_ref[...], pg_ref[...]
        d = _window_mean(_window_sum(u, gidx, False), gidx) - u
        mixed = _dot(_bf(d), w_ref[...].reshape(GROUP, GROUP), 1, 0)
        y_ref[...] = _bf(mixed * sc_ref[...] * (pg * _sigmoid(pg)))

    return pl.pallas_call(
        body, name="pool_fwd", grid=(NGROUP,),
        in_specs=[*_POOL_SPECS, _POOL_W_SPEC, pl.BlockSpec((1, GROUP), lambda g: (0, g)), ANY],
        out_specs=pl.BlockSpec((T, GROUP), lambda g: (0, g)),
        out_shape=pltpu.HBM((T, DMIX), jnp.bfloat16),
        compiler_params=_params(("parallel",)),
    )(proj, proj, pool_w, pool_scale, token)


def _tri(lower):
    r = lax.broadcasted_iota(jnp.int32, (CHUNK, CHUNK), 0)
    c = lax.broadcasted_iota(jnp.int32, (CHUNK, CHUNK), 1)
    return (r >= c) if lower else (r <= c)


def _sum_rows_matrix():
    shape = (CHUNK + 16, CHUNK)
    r, c = lax.broadcasted_iota(jnp.int32, shape, 0), lax.broadcasted_iota(jnp.int32, shape, 1)
    run = jnp.where(c <= r, 1.0, 0.0)
    half = jnp.where(c < CHUNK // 2, 1.0, 0.0)
    return _bf(jnp.where(r < CHUNK, run, jnp.where(r < CHUNK + 8, 1.0, half)))


def _rev_sum_matrix():
    shape = (CHUNK, 2 * CHUNK)
    r, c = lax.broadcasted_iota(jnp.int32, shape, 0), lax.broadcasted_iota(jnp.int32, shape, 1)
    return _bf(jnp.where(c < CHUNK, jnp.where(c >= r, 1.0, 0.0), jnp.where(c - CHUNK < r, 1.0, 0.0)))


def _split2(a):
    hi = _bf(a)
    return [hi, _bf(a - hi.astype(jnp.float32))]


def _exact_sums(mat, pieces):
    x = jnp.concatenate([s for p in pieces for s in _split2(p)], axis=1)
    r = _dot(mat, x, 1, 0)
    return [r[:, 2 * j * HEAD:(2 * j + 1) * HEAD] + r[:, (2 * j + 1) * HEAD:(2 * j + 2) * HEAD]
            for j in range(len(pieces))]


def _gates(qv, fl, lb):
    sq = _sigmoid(qv)
    sg = _sigmoid(fl)
    f = lb + (1.0 - lb) * sg
    return dict(sq=sq, qs=qv * sq, sg=sg, f=f, kk=1.0 - f, g=jnp.log2(f))


def _decays(sums):
    big_g = sums[:CHUNK]
    total = sums[CHUNK:CHUNK + 8]
    g_last = jnp.tile(total, (CHUNK // 8, 1))
    g_mid = jnp.tile(sums[CHUNK + 8:], (CHUNK // 8, 1))
    return dict(
        e_q=jnp.exp2(big_g),
        e_k=jnp.exp2(g_last - big_g),
        e_qm=jnp.exp2(jnp.minimum(big_g - g_mid, EXP_CAP)),
        e_km=jnp.exp2(jnp.minimum(g_mid - big_g, EXP_CAP)),
        total8=jnp.exp2(total))


def _group_rows(gi):
    return [pl.ds(pl.multiple_of((gi * NB + j) * CHUNK, CHUNK), CHUNK) for j in range(NB)]


def _lower_bound(lb_ref):
    return _sigmoid(lb_ref[0:1, :] - lb_ref[1:2, :])


def _hgrn_fwd(proj, lb_logits, rec_g, y_in):
    def body(q_ref, f_ref, i_ref, gate_ref, lb_ref, rg_ref, y_any, y_ref, o_ref, st_ref):
        del y_any
        lb = _lower_bound(lb_ref)
        causal = _tri(True)
        smat = _sum_rows_matrix()

        def group(gi, st):
            rows = _group_rows(gi)
            ts = [_gates(q_ref[r, :], f_ref[r, :], lb) for r in rows]
            ds = [_decays(s) for s in _exact_sums(smat, [t["g"] for t in ts])]
            vs = [_bf(i_ref[r, :]) for r in rows]
            q_m = [_bf(t["qs"] * d["e_qm"]) for t, d in zip(ts, ds)]
            k_m = [_bf(t["kk"] * d["e_km"]) for t, d in zip(ts, ds)]
            q_e = [_bf(t["qs"] * d["e_q"]) for t, d in zip(ts, ds)]
            k_e = [_bf(t["kk"] * d["e_k"]) for t, d in zip(ts, ds)]
            a = [_bf(jnp.where(causal, _dot(q_m[j], k_m[j], 1, 1), 0.0)) for j in range(NB)]
            intra = [_dot(a[j], vs[j], 1, 0) for j in range(NB)]
            upd = [_dot(vs[j], k_e[j], 0, 0) for j in range(NB)]
            for j in range(NB):
                st_ref[gi * NB + j] = st
                o_ref[rows[j], :] = intra[j] + _dot(q_e[j], _bf(st), 1, 1)
                st = st * jnp.tile(ds[j]["total8"], (HEAD // 8, 1)) + upd[j]
            return st

        lax.fori_loop(0, NGRP, group, jnp.zeros((HEAD, HEAD), jnp.float32))
        o = o_ref[...]
        rn = o * lax.rsqrt(jnp.mean(o * o, axis=-1, keepdims=True) + EPS)
        gate = gate_ref[...]
        y_ref[...] = _bf(rn * rg_ref[...] * (gate * _sigmoid(gate)))

    return pl.pallas_call(
        body, name="hgrn_fwd", grid=(NHEAD,),
        in_specs=[*_HEAD_SPECS,
                  pl.BlockSpec((2, HEAD), lambda h: (0, h)),
                  pl.BlockSpec((1, HEAD), lambda h: (0, h)),
                  pl.BlockSpec(memory_space=pl.ANY)],
        out_specs=(pl.BlockSpec((T, HEAD), lambda h: (0, NHEAD + h)),
                   pl.BlockSpec((T, HEAD), lambda h: (0, h)),
                   pl.BlockSpec((None, NCHUNK, HEAD, HEAD), lambda h: (h, 0, 0, 0))),
        out_shape=(pltpu.HBM((T, DMIX), jnp.bfloat16), pltpu.HBM((T, D), jnp.float32),
                   pltpu.HBM((NHEAD, NCHUNK, HEAD, HEAD), jnp.float32)),
        input_output_aliases={6: 0},
        compiler_params=_params(("parallel",)),
    )(proj, proj, proj, proj, lb_logits, rec_g, y_in)


def _out_proj_loss(x, y, w_out, target, gf):
    rows = 512
    parts = [slice(k * rows // 2, (k + 1) * rows // 2) for k in range(2)]

    def body(x_ref, y_ref, w_ref, t_ref, g_ref, dz_ref, dzb_ref, sq_ref, dg_ref):
        zs = [x_ref[p, :] + _dot(y_ref[p, :], w_ref[...], 1, 0) for p in parts]
        sq = dg = 0.0
        for p, z in zip(parts, zs):
            r = lax.rsqrt(jnp.mean(z * z, axis=-1, keepdims=True) + EPS)
            zhat = z * r
            err = zhat * g_ref[...] - t_ref[p, :]
            dy = err * (1.0 / D)
            gdy = dy * g_ref[...]
            dz = r * (gdy - zhat * jnp.mean(zhat * gdy, axis=-1, keepdims=True))
            dz_ref[p, :] = dz
            dzb_ref[p, :] = _bf(dz)
            sq = sq + jnp.sum(err * err, axis=0, keepdims=True)
            dg = dg + jnp.sum(zhat * dy, axis=0, keepdims=True)

        @pl.when(pl.program_id(0) == 0)
        def _():
            sq_ref[...] = sq
            dg_ref[...] = dg

        @pl.when(pl.program_id(0) != 0)
        def _():
            sq_ref[...] += sq
            dg_ref[...] += dg

    tile = pl.BlockSpec((rows, D), lambda i: (i, 0))
    vec = pl.BlockSpec((1, D), lambda i: (0, 0))
    return pl.pallas_call(
        body, name="out_proj_loss", grid=(T // rows,),
        in_specs=[tile, pl.BlockSpec((rows, DMIX), lambda i: (i, 0)), pl.BlockSpec((DMIX, D), lambda i: (0, 0)),
                  tile, vec],
        out_specs=(tile, tile, vec, vec),
        out_shape=(pltpu.HBM((T, D), jnp.float32), pltpu.HBM((T, D), jnp.bfloat16),
                   pltpu.HBM((1, D), jnp.float32), pltpu.HBM((1, D), jnp.float32)),
        compiler_params=_params(("arbitrary",)),
    )(x, y, w_out, target, gf)


def _out_proj_bwd(dzb, w_out, y):
    tn = 512

    def body(dz_ref, w_ref, y_ref, dy_ref, gw_ref, gwb_ref):
        dz = dz_ref[...]
        dy_ref[...] = _dot(dz, w_ref[...], 1, 1)
        gw = _dot(y_ref[...], dz, 0, 0)
        gw_ref[...] = gw
        gwb_ref[...] = _bf(gw)

    return pl.pallas_call(
        body, name="out_proj_bwd", grid=(DMIX // tn,),
        in_specs=[pl.BlockSpec((T, D), lambda n: (0, 0)), pl.BlockSpec((tn, D), lambda n: (n, 0)),
                  pl.BlockSpec((T, tn), lambda n: (0, n))],
        out_specs=(pl.BlockSpec((T, tn), lambda n: (0, n)), pl.BlockSpec((tn, D), lambda n: (n, 0)),
                   pl.BlockSpec((tn, D), lambda n: (n, 0))),
        out_shape=(pltpu.HBM((T, DMIX), jnp.float32), pltpu.HBM((DMIX, D), jnp.float32),
                   pltpu.HBM((DMIX, D), jnp.bfloat16)),
        compiler_params=_params(("parallel",)),
    )(dzb, w_out, y)


def _hgrn_bwd(proj, lb_logits, rec_g, o, states, dymix, dproj_in, token):
    def body(q_ref, f_ref, i_ref, gate_ref, lb_ref, rg_ref, o_ref, st_ref, dy_ref, dp_any, token_any,
             dp_ref, drg_ref, dlb_ref, do_ref):
        del dp_any, token_any
        lb = _lower_bound(lb_ref)
        causal = _tri(True)
        smat, rmat = _sum_rows_matrix(), _rev_sum_matrix()

        o = o_ref[...]
        rs = lax.rsqrt(jnp.mean(o * o, axis=-1, keepdims=True) + EPS)
        rn = o * rs
        gate = gate_ref[...]
        sgate = _sigmoid(gate)
        dyv = dy_ref[...]
        d_r = dyv * (gate * sgate)
        dp_ref[3] = _bf(dyv * (rn * rg_ref[...]) * (sgate * (1.0 + gate * (1.0 - sgate))))
        drg_ref[...] = jnp.sum(d_r * rn, axis=0, keepdims=True)
        drn = d_r * rg_ref[...]
        do_ref[...] = rs * (drn - rn * jnp.mean(rn * drn, axis=-1, keepdims=True))

        def group(i, carry):
            dst, dlb = carry
            gi = NGRP - 1 - i
            rows = _group_rows(gi)
            span = range(NB)
            qvs = [q_ref[r, :] for r in rows]
            ts = [_gates(qv, f_ref[r, :], lb) for qv, r in zip(qvs, rows)]
            ds = [_decays(s) for s in _exact_sums(smat, [t["g"] for t in ts])]
            vs = [_bf(i_ref[r, :]) for r in rows]
            dos = [_bf(do_ref[r, :]) for r in rows]
            sts = [st_ref[gi * NB + j] for j in span]
            qe_f = [t["qs"] * d["e_q"] for t, d in zip(ts, ds)]
            ke_f = [t["kk"] * d["e_k"] for t, d in zip(ts, ds)]
            q_e, k_e = [_bf(a) for a in qe_f], [_bf(a) for a in ke_f]
            q_m = [_bf(t["qs"] * d["e_qm"]) for t, d in zip(ts, ds)]
            k_m = [_bf(t["kk"] * d["e_km"]) for t, d in zip(ts, ds)]
            a = [_bf(jnp.where(causal, _dot(q_m[j], k_m[j], 1, 1), 0.0)) for j in span]
            da = [_bf(jnp.where(causal, _dot(dos[j], vs[j], 1, 1), 0.0)) for j in span]
            dqm = [_dot(da[j], k_m[j], 1, 0) for j in span]
            dkm = [_dot(da[j], q_m[j], 0, 0) for j in span]
            dv_in = [_dot(a[j], dos[j], 0, 0) for j in span]
            dqe = [_dot(dos[j], _bf(sts[j]), 1, 0) for j in span]
            grow = [_dot(dos[j], q_e[j], 0, 0) for j in span]
            dke, carried = [None] * NB, [None] * NB
            for j in reversed(span):
                dst_b = _bf(dst)
                dke[j] = _dot(vs[j], dst_b, 1, 0)
                dp_ref[2, rows[j], :] = _bf(dv_in[j] + _dot(k_e[j], dst_b, 1, 1))
                carried[j] = ds[j]["total8"] * jnp.sum(dst * sts[j], axis=0, keepdims=True)
                dst = dst * jnp.tile(ds[j]["total8"], (HEAD // 8, 1)) + grow[j]
            kdk = [ke_f[j] * dke[j] for j in span]
            pos = [(q_m[j].astype(jnp.float32) * dqm[j] - k_m[j].astype(jnp.float32) * dkm[j]) + qe_f[j] * dqe[j]
                   for j in span]
            dgs = _exact_sums(rmat, [jnp.concatenate([pos[j], kdk[j]], axis=0) for j in span])
            for j in span:
                t, d = ts[j], ds[j]
                dg = dgs[j] + jnp.tile(carried[j], (CHUNK // 8, 1))
                dqs = dqm[j] * d["e_qm"] + dqe[j] * d["e_q"]
                dkk = dkm[j] * d["e_km"] + dke[j] * d["e_k"]
                df = dg / t["f"] - dkk
                dp_ref[1, rows[j], :] = _bf(df * (1.0 - lb) * (t["sg"] * (1.0 - t["sg"])))
                dp_ref[0, rows[j], :] = _bf(dqs * (t["sq"] * (1.0 + qvs[j] * (1.0 - t["sq"]))))
                dlb = dlb + df * (1.0 - t["sg"])
            return dst, dlb

        _, dlb = lax.fori_loop(0, NGRP, group, (jnp.zeros((HEAD, HEAD), jnp.float32),
                                                jnp.zeros((CHUNK, HEAD), jnp.float32)))
        dlb_ref[...] = jnp.sum(dlb, axis=0, keepdims=True)

    vec = pl.BlockSpec((1, HEAD), lambda h: (0, h))
    return pl.pallas_call(
        body, name="hgrn_bwd", grid=(NHEAD,),
        in_specs=[*_HEAD_SPECS,
                  pl.BlockSpec((2, HEAD), lambda h: (0, h)), vec,
                  pl.BlockSpec((T, HEAD), lambda h: (0, h)),
                  pl.BlockSpec((None, NCHUNK, HEAD, HEAD), lambda h: (h, 0, 0, 0)),
                  pl.BlockSpec((T, HEAD), lambda h: (0, NHEAD + h)), ANY, ANY],
        out_specs=(pl.BlockSpec((4, T, HEAD), lambda h: (0, 0, h)), vec, vec),
        out_shape=(pltpu.HBM((NSEG, T, D), jnp.bfloat16),
                   pltpu.HBM((1, D), jnp.float32), pltpu.HBM((1, D), jnp.float32)),
        scratch_shapes=[pltpu.VMEM((T, HEAD), jnp.float32)],
        input_output_aliases={9: 0},
        compiler_params=_params(("parallel",)),
    )(proj, proj, proj, proj, lb_logits, rec_g, o, states, dymix, dproj_in, token)


def _pool_bwd(proj, pool_w, pool_scale, dymix):
    def body(u_ref, pg_ref, w_ref, sc_ref, dy_ref, dp_ref, gw_ref, gs_ref):
        gidx = pl.program_id(0)
        u, pg = u_ref[...], pg_ref[...]
        w = w_ref[...].reshape(GROUP, GROUP)
        d = _bf(_window_mean(_window_sum(u, gidx, False), gidx) - u)
        mixed = _dot(d, w, 1, 0)
        spg = _sigmoid(pg)
        dyv = dy_ref[...]
        d_p = dyv * (pg * spg)
        dp_ref[1] = _bf(dyv * (mixed * sc_ref[...]) * (spg * (1.0 + pg * (1.0 - spg))))
        gs_ref[...] = jnp.sum(d_p * mixed, axis=0, keepdims=True)
        dmixed = _bf(d_p * sc_ref[...])
        gw_ref[...] = _dot(d, dmixed, 0, 0).reshape(gw_ref.shape)
        dd = _dot(dmixed, w, 1, 1)
        dp_ref[0] = _bf(_window_sum(_window_mean(dd, gidx), gidx, True) - dd)

    return pl.pallas_call(
        body, name="pool_bwd", grid=(NGROUP,),
        in_specs=[*_POOL_SPECS, _POOL_W_SPEC,
                  pl.BlockSpec((1, GROUP), lambda g: (0, g)),
                  pl.BlockSpec((T, GROUP), lambda g: (0, g))],
        out_specs=(pl.BlockSpec((2, T, GROUP), lambda g: (2, 0, g)), _POOL_W_SPEC,
                   pl.BlockSpec((1, GROUP), lambda g: (0, g))),
        out_shape=(pltpu.HBM((NSEG, T, D), jnp.bfloat16),
                   pltpu.HBM((NDEV, NGROUP, GROUP // NDEV, GROUP), jnp.float32),
                   pltpu.HBM((1, D), jnp.float32)),
        compiler_params=_params(("parallel",)),
    )(proj, proj, pool_w, pool_scale, dymix)


HALF = NTILE // 2
AWAY = HALF - 3


def _dproj_tile(chip, side, p):
    j = 6 * chip + 3 * side + p
    return ((j // 4 + 4) % NSEG, 0, j % 4)


def _sibling_copy(sib_out, sib_in, send_sems, recv_sems, slot):
    x, y, c, _ = _place()
    return pltpu.make_async_remote_copy(
        src_ref=sib_out.at[slot], dst_ref=sib_in.at[slot], send_sem=send_sems.at[slot],
        recv_sem=recv_sems.at[slot], device_id=(x, y, 1 - c), device_id_type=MESH)


def _proj_bwd_w_far(place, ht, dproj):
    def body(place_ref, h_ref, dpa_ref, dpb_ref, sib_out, sib_in, send_sems, recv_sems, stage, loc_sems):
        del place_ref
        i = pl.program_id(0)

        def to_hbm(k):
            return pltpu.make_async_copy(stage.at[k], sib_out.at[k], loc_sems.at[k])

        def send(k):
            to_hbm(k).wait()
            _sibling_copy(sib_out, sib_in, send_sems, recv_sems, k).start()

        stage[2 * i] = _bf(_dot(h_ref[...], dpa_ref[...], 1, 0))
        stage[2 * i + 1] = _bf(_dot(h_ref[...], dpb_ref[...], 1, 0))

        @pl.when(i > 0)
        def _():
            send(2 * i - 2)
            send(2 * i - 1)

        to_hbm(2 * i).start()
        to_hbm(2 * i + 1).start()

        @pl.when(i == HALF // 2 - 1)
        def _():
            send(2 * i)
            send(2 * i + 1)

    def tile(k, pr):
        return _dproj_tile(k // 3, 1 - pr[2], k % 3)

    buf = pltpu.HBM((HALF, D, TILE), jnp.bfloat16)
    sems = pltpu.SemaphoreType.DMA((HALF,))
    return pl.pallas_call(
        body, name="proj_bwd_w_far",
        grid_spec=pltpu.PrefetchScalarGridSpec(
            num_scalar_prefetch=1, grid=(HALF // 2,),
            in_specs=[pl.BlockSpec((D, T), lambda i, pr: (0, 0)),
                      pl.BlockSpec((None, T, TILE), lambda i, pr: tile(2 * i, pr)),
                      pl.BlockSpec((None, T, TILE), lambda i, pr: tile(2 * i + 1, pr))],
            out_specs=(HBM, HBM, SEM, SEM),
            scratch_shapes=[pltpu.VMEM((HALF, D, TILE), jnp.bfloat16), pltpu.SemaphoreType.DMA((HALF,))]),
        out_shape=(buf, buf, sems, sems),
        compiler_params=pltpu.CompilerParams(dimension_semantics=("arbitrary",), vmem_limit_bytes=48 * MIB,
                                             has_side_effects=EFFECT),
    )(place, ht, dproj, dproj)


def _proj_bwd_w_near(place, ht, dproj, sib_out, sib_in, sib_send, sib_recv):
    def owner_chip(k, pr):
        return jnp.where(k < AWAY, (pr[1] + 1 + k % 3) % 4, pr[1])

    def tile_p(k):
        return jnp.where(k < AWAY, k // 3, k - AWAY)

    def body(place_ref, h_ref, dp_ref, sib_out, sib_in, sib_send, sib_recv, sums, own_ref, landing, out_send,
             out_recv, recvbuf, outbuf, in_sems, loc_sems):
        i = pl.program_id(0)
        px, py, c, _ = _place()

        def slot_of(k):
            return 3 * owner_chip(k, place_ref) + tile_p(k)

        def load(k):
            return pltpu.make_async_copy(sib_in.at[slot_of(k)], recvbuf.at[k % 2], in_sems.at[k % 2])

        def fetch(k):
            _sibling_copy(sib_out, sib_in, sib_send, sib_recv, slot_of(k)).wait_recv()
            load(k).start()

        def route(k):
            chip = owner_chip(k, place_ref)
            cx, cy = chip // 2, chip % 2
            return cx, cy, (cx ^ px) + 2 * (cy ^ py) - 1, tile_p(k)

        def to_hbm(k):
            _, _, rel, p = route(k)
            return pltpu.make_async_copy(outbuf.at[k], sums.at[rel, p], loc_sems.at[k])

        def to_owner(k):
            cx, cy, rel, p = route(k)
            return pltpu.make_async_remote_copy(
                src_ref=sums.at[rel, p], dst_ref=landing.at[rel, p], send_sem=out_send.at[3 * rel + p],
                recv_sem=out_recv.at[3 * rel + p], device_id=(cx, cy, c), device_id_type=MESH)

        @pl.when(i == 0)
        def _():
            fetch(i)

        @pl.when(i < HALF - 1)
        def _():
            fetch(i + 1)

        load(i).wait()
        total = _dot(h_ref[...], dp_ref[...], 1, 0) + recvbuf[i % 2].astype(jnp.float32)
        own_ref[...] = total
        outbuf[jnp.minimum(i, AWAY)] = _bf(total)

        @pl.when(i < AWAY)
        def _():
            to_hbm(i).start()

        @pl.when(jnp.logical_and(i > 0, i <= AWAY))
        def _():
            to_hbm(i - 1).wait()
            to_owner(i - 1).start()

        @pl.when(i == HALF - 1)
        def _():
            for slot in range(HALF):
                _sibling_copy(sib_out, sib_in, sib_send, sib_recv, slot).wait_send()

    travelling = pltpu.HBM((3, 3, D, TILE), jnp.bfloat16)
    sems = pltpu.SemaphoreType.DMA((AWAY,))
    return pl.pallas_call(
        body, name="proj_bwd_w_near",
        grid_spec=pltpu.PrefetchScalarGridSpec(
            num_scalar_prefetch=1, grid=(HALF,),
            in_specs=[pl.BlockSpec((D, T), lambda i, pr: (0, 0)),
                      pl.BlockSpec((None, T, TILE), lambda i, pr: _dproj_tile(owner_chip(i, pr), pr[2], tile_p(i))),
                      HBM, HBM, SEM, SEM],
            out_specs=(HBM, pl.BlockSpec((None, D, TILE), lambda i, pr: (jnp.where(i < AWAY, 0, i % 3), 0, 0)),
                       HBM, SEM, SEM),
            scratch_shapes=[pltpu.VMEM((2, D, TILE), jnp.bfloat16), pltpu.VMEM((AWAY + 1, D, TILE), jnp.bfloat16),
                            pltpu.SemaphoreType.DMA((2,)), pltpu.SemaphoreType.DMA((AWAY,))]),
        out_shape=(travelling, pltpu.HBM((3, D, TILE), jnp.float32), travelling, sems, sems),
        compiler_params=pltpu.CompilerParams(dimension_semantics=("arbitrary",), vmem_limit_bytes=48 * MIB,
                                             has_side_effects=EFFECT),
    )(place, ht, dproj, sib_out, sib_in, sib_send, sib_recv)


def _proj_bwd_x(dproj, w_t, x, g1, dz, others, token):
    tm = 512
    pairs = NSEG // 2
    k = len(others)

    def body(dp_ref, w_ref, x_ref, g_ref, dz_ref, *refs):
        dx_ref, dg_ref, wcat, acc = refs[k + 1:]
        m, s = pl.program_id(0), pl.program_id(1)

        @pl.when(m == 0)
        def _():
            for i in range(8):
                wcat[s, :, i * TILE:(i + 1) * TILE] = w_ref[i]

        @pl.when(s == 0)
        def _():
            acc[...] = jnp.zeros((tm, D), jnp.float32)

        acc[...] += _dot(jnp.concatenate([dp_ref[0], dp_ref[1]], axis=1), wcat[s], 1, 1)

        @pl.when(s == pairs - 1)
        def _():
            xv = x_ref[...]
            rs = lax.rsqrt(jnp.mean(xv * xv, axis=-1, keepdims=True) + EPS)
            xhat = xv * rs
            dhv = acc[...]
            gdh = dhv * g_ref[...]
            dx_ref[...] = dz_ref[...] + rs * (gdh - xhat * jnp.mean(xhat * gdh, axis=-1, keepdims=True))
            dg = jnp.sum(xhat * dhv, axis=0, keepdims=True)

            @pl.when(m == 0)
            def _():
                dg_ref[0:1, :] = dg
                for r, ref in enumerate(refs[:k]):
                    dg_ref[1 + r:2 + r, :] = ref[...]
                dg_ref[1 + k:, :] = jnp.zeros((7 - k, D), jnp.float32)

            @pl.when(m != 0)
            def _():
                dg_ref[0:1, :] += dg

    rows = pl.BlockSpec((tm, D), lambda m, s: (m, 0))
    vec = pl.BlockSpec((1, D), lambda m, s: (0, 0))
    return pl.pallas_call(
        body, name="proj_bwd_x", grid=(T // tm, pairs),
        in_specs=[pl.BlockSpec((2, tm, D), lambda m, s: (s, m, 0)),
                  pl.BlockSpec((8, D, TILE), lambda m, s: ((jnp.where(m == 0, s, pairs - 1) + 1) % pairs, 0, 0)),
                  rows, vec, rows, *[vec] * k, ANY],
        out_specs=(rows, pl.BlockSpec((8, D), lambda m, s: (0, 0))),
        out_shape=(jax.ShapeDtypeStruct((T, D), jnp.float32), pltpu.HBM((8, D), jnp.float32)),
        scratch_shapes=[pltpu.VMEM((pairs, D, 2 * D), jnp.bfloat16), pltpu.VMEM((tm, D), jnp.float32)],
        compiler_params=_params(("arbitrary", "arbitrary"), vmem_mib=56),
    )(dproj, w_t, x, g1, dz, *[_in_hbm(a) for a in others], token)


def _adamw(w, g, m, v):
    m_new = ADAM_B1 * m + (1.0 - ADAM_B1) * g
    v_new = ADAM_B2 * v + (1.0 - ADAM_B2) * (g * g)
    delta = -ADAM_LR * ((m_new / BC1) / (jnp.sqrt(v_new / BC2) + ADAM_EPS) + ADAM_WD * w)
    return delta, m_new, v_new


def _reduce_adam(name, place, parts, w, m, v, grid, w_spec):
    n = len(parts)

    def body(place_ref, *refs):
        del place_ref
        w_ref, m_ref, v_ref, g_ref, d_ref, mo_ref, vo_ref = refs[n:]
        g = None
        for ref, (_, _, stacked) in zip(refs[:n], parts):
            terms = [ref[r] for r in range(ref.shape[0])] if stacked else [ref[...]]
            for t in terms:
                if t.shape[-1] != w_ref.shape[-1]:
                    t = jnp.concatenate([t[p] for p in range(t.shape[0])], axis=1)
                g = t.astype(jnp.float32) if g is None else g + t.astype(jnp.float32)
        delta, m_new, v_new = _adamw(w_ref[...], g, m_ref[...], v_ref[...])
        g_ref[...] = g
        d_ref[...] = delta
        mo_ref[...] = m_new
        vo_ref[...] = v_new

    shape = jax.ShapeDtypeStruct(w.shape, jnp.float32)
    return pl.pallas_call(
        body, name=name,
        grid_spec=pltpu.PrefetchScalarGridSpec(
            num_scalar_prefetch=1, grid=grid,
            in_specs=[spec for _, spec, _ in parts] + [w_spec] * 3, out_specs=(w_spec,) * 4),
        out_shape=(shape,) * 4,
        compiler_params=_params(("parallel",)),
    )(place, *[_in_hbm(a) for a in [a for a, _, _ in parts] + [w, m, v]])


SMALL_ROWS = (1, 1, 2, 1, 1)


def _small_adam(place, own, parts, ws, ms, vs):
    n = len(SMALL_ROWS)

    def body(place_ref, own_ref, p_ref, *refs):
        ins, outs, bufs = refs[:3 * n], refs[3 * n:3 * n + 1 + 4 * n], refs[3 * n + 1 + 4 * n:]

        def stacked(group, buf):
            r0 = 0
            for ref, k in zip(group, SMALL_ROWS):
                buf[r0:r0 + k, :] = ref[...]
                r0 += k
            buf[r0:, :] = jnp.zeros((8 - r0, D), jnp.float32)
            return buf[...]

        wv, mv, vv = (stacked(ins[n * j:n * j + n], bufs[j]) for j in range(3))
        me = place_ref[0]
        g = None
        for s in range(NDEV):
            term = jnp.where(me == s, own_ref[...], p_ref[s])
            g = term if g is None else g + term
        rows = _row_ids(wv.shape)
        other = jnp.where(rows == 2, pltpu.roll(wv, 7, 0), jnp.where(rows == 3, pltpu.roll(wv, 1, 0), 0.0))
        lbv = _sigmoid(wv - other)
        sign = jnp.where(rows == 2, 1.0, -1.0)
        g = jnp.where((rows == 2) | (rows == 3), sign * g * lbv * (1.0 - lbv), g)
        delta, m_new, v_new = _adamw(wv, g, mv, vv)
        outs[0][...] = jnp.sum(g[6:7], axis=1, keepdims=True) * (0.5 / D)
        for j, val in enumerate((g, delta, m_new, v_new)):
            r0 = 0
            for ref, k in zip(outs[1 + n * j:1 + n * j + n], SMALL_ROWS):
                ref[...] = val[r0:r0 + k]
                r0 += k

    vmem = pl.BlockSpec(memory_space=pltpu.VMEM)
    shapes = [jax.ShapeDtypeStruct((k, D), jnp.float32) for k in SMALL_ROWS]
    out = pl.pallas_call(
        body, name="small_adam", out_shape=(jax.ShapeDtypeStruct((1, 1), jnp.float32), *shapes * 4),
        in_specs=[pl.BlockSpec(memory_space=pltpu.SMEM)] + [vmem] * (2 + 3 * n), out_specs=(vmem,) * (1 + 4 * n),
        scratch_shapes=[pltpu.VMEM((8, D), jnp.float32)] * 3,
    )(place, own, parts, *ws, *ms, *vs)
    return out[0], [out[1 + n * j:1 + n * j + n] for j in range(4)]


def kernel(x, norm1_g, w_in, pool_w, pool_scale, lb_logits, rec_norm_g, w_out, final_norm_g, loss_target, m_norm1_g, m_w_in, m_pool_w, m_pool_scale, m_lb_logits, m_rec_norm_g, m_w_out, m_final_norm_g, v_norm1_g, v_w_in, v_pool_w, v_pool_scale, v_lb_logits, v_rec_norm_g, v_w_out, v_final_norm_g):
    xs = x[0]
    target = loss_target[0]
    ix, iy, ic = lax.axis_index("x"), lax.axis_index("y"), lax.axis_index("c")
    place = jnp.stack([4 * ix + 2 * iy + ic, 2 * ix + iy, ic]).astype(jnp.int32)
    gf = final_norm_g.reshape(1, D)

    ht, w_t, w_out_b, w_out_g, pool_g, proj = _gather_proj(xs, norm1_g, w_in, w_out, pool_w)
    wout = [w_out_b, w_out_g]
    wout_send, wout_recv, wout, wout_token = _split_start("gather_wout_start", wout, NDEV - 1, _plan_wout)

    y = _pool_fwd(proj, pool_g, pool_scale, wout_token)
    y, o, states = _hgrn_fwd(proj, lb_logits, rec_norm_g, y)
    _, w_out_g = _split_wait("gather_wout_wait", wout, wout_send, wout_recv, _plan_wout, o)
    w_out_full = _in_hbm(w_out_g.reshape(DMIX, D))
    dz, dzb, sq, dgf = _out_proj_loss(xs, y, w_out_full, target, gf)

    dymix, gwout_f, gwout_b = _out_proj_bwd(dzb, w_out_full, y)
    dproj, gpool, dscale = _pool_bwd(proj, pool_g, pool_scale, dymix)

    blk_out = (NDEV, DMIX // NDEV, D)
    blk_pool = (NDEV, NGROUP, GROUP // NDEV, GROUP)
    rest = [gwout_b.reshape(blk_out), gpool,
            lax.empty((NDEV - 1,) + blk_out[1:], jnp.bfloat16), lax.empty((NDEV - 1,) + blk_pool[1:], jnp.float32)]
    rest_send, rest_recv, rest, rest_token = _split_start("scatter_rest_start", rest, 2 * (NDEV - 1), _plan_rest)

    dproj, drecg, dlb = _hgrn_bwd(proj, lb_logits, rec_norm_g, o, states, dymix, dproj, rest_token)
    chip_sums, own_sum, landing, win_send, win_recv = _proj_bwd_w_near(
        place, ht, dproj, *_proj_bwd_w_far(place, ht, dproj))
    win = [chip_sums, landing]

    others = [a.reshape(1, D) for a in (dscale, dlb, dlb, drecg, dgf, sq)]
    grad_x, small_block = _proj_bwd_x(dproj, w_t, xs, norm1_g, dz, others, chip_sums)

    small = [small_block, lax.empty((NDEV, 8, D), jnp.float32)]
    small_send, small_recv, small, small_token = _split_start("gather_small_start", small, NDEV - 1, _plan_small)

    _, gpool_own, r_out, r_pool = _split_wait("scatter_rest_wait", rest, rest_send, rest_recv, _plan_rest,
                                              small_token)
    g_wout, d_wout, m_wout, v_wout = _reduce_adam(
        "adam_w_out", place,
        [(gwout_f.reshape(blk_out), pl.BlockSpec((None,) + blk_out[1:], lambda i, pr: (pr[0], 0, 0)), False),
         (r_out, pl.BlockSpec((NDEV - 1,) + blk_out[1:], lambda i, pr: (0, 0, 0)), True)],
        w_out, m_w_out, v_w_out, (1,), pl.BlockSpec((None,) + blk_out[1:], lambda i, pr: (0, 0, 0)))
    g_pool, d_pool, m_pool, v_pool = _reduce_adam(
        "adam_pool_w", place,
        [(gpool_own, pl.BlockSpec((None,) + blk_pool[1:], lambda i, pr: (pr[0], 0, 0, 0)), False),
         (r_pool, pl.BlockSpec((NDEV - 1,) + blk_pool[1:], lambda i, pr: (0, 0, 0, 0)), True)],
        pool_w, m_pool_w, v_pool_w, (1,), pl.BlockSpec((None,) + blk_pool[1:], lambda i, pr: (0, 0, 0, 0)))

    _, r_in = _split_wait("scatter_win_wait", win, win_send, win_recv, _plan_in, d_pool)
    g_win, d_win, m_win, v_win = _reduce_adam(
        "adam_w_in", place,
        [(own_sum, pl.BlockSpec((3, D // 8, TILE), lambda i, pr: (0, i, 0)), False),
         (r_in, pl.BlockSpec((3, 3, D // 8, TILE), lambda i, pr: (0, 0, i, 0)), True)],
        w_in, m_w_in, v_w_in, (8,), pl.BlockSpec((None, D // 8, 3 * TILE), lambda i, pr: (0, i, 0)))

    own_small, r_small = _split_wait("gather_small_wait", small, small_send, small_recv, _plan_small, d_win)
    loss, (g_s, d_s, m_s, v_s) = _small_adam(
        place, own_small, r_small,
        (norm1_g, pool_scale, lb_logits, rec_norm_g, gf),
        (m_norm1_g, m_pool_scale, m_lb_logits, m_rec_norm_g, m_final_norm_g.reshape(1, D)),
        (v_norm1_g, v_pool_scale, v_lb_logits, v_rec_norm_g, v_final_norm_g.reshape(1, D)))

    def outs(small, win, pool, wout):
        n1, ps, lbl, rg, fg = small
        return n1, win, pool, ps, lbl, rg, wout, fg.reshape(D)

    return (loss.reshape(()), grad_x[None],
            *outs(g_s, g_win, g_pool, g_wout), *outs(d_s, d_win, d_pool, d_wout),
            *outs(m_s, m_win, m_pool, m_wout), *outs(v_s, v_win, v_pool, v_wout))
```

```python
import functools

import jax
import jax.numpy as jnp
from jax import lax
from jax.experimental import pallas as pl
from jax.experimental.pallas import tpu as pltpu

T = 2048
D = 1024
NSEG = 6
NTILE = 24
TILE = 256
DMIX = 2048
NDEV = 8
HEAD = 128
NHEAD = 8
CHUNK = 64
NCHUNK = T // CHUNK
NB = 32
NGRP = NCHUNK // NB
NGROUP = 4
GROUP = 256
EPS = 1e-6
EXP_CAP = 115.0
MESH = pl.DeviceIdType.MESH
AXES = ("x", "y", "c")
ANY = pl.BlockSpec(memory_space=pl.ANY)
HBM = pl.BlockSpec(memory_space=pltpu.HBM)
SEM = pl.BlockSpec(memory_space=pltpu.SEMAPHORE)
EFFECT = pltpu.SideEffectType.DATAFLOW_SIDE_EFFECTING

ADAM_LR = 0.001
ADAM_B1 = 0.9
ADAM_B2 = 0.999
ADAM_EPS = 1e-08
ADAM_WD = 0.01
ADAM_STEP = 10
BC1 = 1.0 - ADAM_B1 ** ADAM_STEP
BC2 = 1.0 - ADAM_B2 ** ADAM_STEP

MIB = 1 << 20


def _params(sem=None, vmem_mib=48):
    return pltpu.CompilerParams(dimension_semantics=sem, vmem_limit_bytes=vmem_mib * MIB)


def _sigmoid(v):
    return 1.0 / (1.0 + jnp.exp(-v))


def _dot(a, b, ca, cb, precision=None):
    return lax.dot_general(a, b, (((ca,), (cb,)), ((), ())), precision=precision,
                           preferred_element_type=jnp.float32)


def _bf(v):
    return v.astype(jnp.bfloat16)


def _in_hbm(a):
    return pltpu.with_memory_space_constraint(a, pltpu.HBM)


def _place():
    x, y, c = lax.axis_index("x"), lax.axis_index("y"), lax.axis_index("c")
    return x, y, c, 4 * x + 2 * y + c


def _peer(x, y, c, r):
    return (x ^ ((r >> 2) & 1), y ^ ((r >> 1) & 1), c ^ (r & 1))


def _gather_proj(x, g1, w_in, w_out, pool_w):
    def body(x_ref, g_ref, win_ref, wout_ref, pool_ref, ht_o, wt_o, woutb_o, wout_o, pool_o, proj_o,
             xbuf, hv, htv, wv, wob, pb, stage, send_sems, recv_sems, loc_sems, out_sems):
        px, py, c, my_idx = _place()
        fetch_x = pltpu.make_async_copy(x_ref, xbuf, loc_sems.at[5])
        fetch_x.start()
        me, sibling = (px, py, c), (px, py, 1 - c)
        chips = [(1 - px, py), (px, 1 - py), (1 - px, 1 - py)]
        for p in range(3):
            wv[3 * my_idx + p] = _bf(win_ref[0, :, p * TILE:(p + 1) * TILE])

        def index(bx, by, bc):
            return 4 * bx + 2 * by + bc

        def slot(w, block):
            return wv.at[pl.ds(3 * index(*block), 3)] if w == 0 else pool_o.at[index(*block)]

        def copy(k, w, block, to, src=None):
            return pltpu.make_async_remote_copy(
                src_ref=slot(w, block) if src is None else src, dst_ref=slot(w, block),
                send_sem=send_sems.at[2 * k + w], recv_sem=recv_sems.at[2 * k + w],
                device_id=to, device_id_type=MESH)

        def save(block):
            at = pl.ds(3 * index(*block), 3)
            pltpu.make_async_copy(wv.at[at], wt_o.at[at], loc_sems.at[4]).start()

        srcs = (slot(0, me), pb)
        first = []
        for w in (0, 1):
            if w == 1:
                pb[...] = _bf(pool_ref[0])
                wob[...] = _bf(wout_ref[0])
            group = [copy(1 + j, w, me, (*chip, c), src=srcs[w]) for j, chip in enumerate(chips[:2])]
            group.append(copy(0, w, me, sibling, src=srcs[w]))
            for cp in group:
                cp.start()
            first += group
        save(me)
        locs = [pltpu.make_async_copy(pb, slot(1, me), loc_sems.at[0]),
                pltpu.make_async_copy(wob, wout_o.at[my_idx], loc_sems.at[1]),
                pltpu.make_async_copy(wob, woutb_o, loc_sems.at[2])]
        for cp in locs:
            cp.start()

        fetch_x.wait()
        xv = xbuf[...]
        hv[...] = _bf(xv * lax.rsqrt(jnp.mean(xv * xv, axis=-1, keepdims=True) + EPS) * g_ref[...])
        rows = 256
        for r0 in range(0, T, rows):
            htv[:, r0:r0 + rows] = hv[r0:r0 + rows, :].T
        locs.append(pltpu.make_async_copy(htv, ht_o, loc_sems.at[3]))
        locs[-1].start()

        def out_copy(p, j):
            return pltpu.make_async_copy(stage.at[p], proj_o.at[j], out_sems.at[p])

        def project(nth, block):
            base = 3 * index(*block)

            def tile(p, carry):
                if nth > 0:
                    out_copy(p, base + p).wait()
                stage[p] = _dot(hv[...], wv[base + p], 1, 0)
                out_copy(p, base + p).start()
                return carry

            lax.fori_loop(0, 3, tile, 0)

        project(0, me)
        copy(0, 0, sibling, me).wait_recv()
        save(sibling)
        project(1, sibling)
        passed = []
        relay_from = (px ^ (1 - c), py ^ c, c)
        relay_to = (px ^ c, py ^ (1 - c), c)

        def arrived(w, j):
            copy(1 + j, w, (*chips[j], c), me).wait_recv()
            passed.append(copy(4 + j, w, (*chips[j], c), sibling))
            passed[-1].start()

        def relay(w):
            passed.append(copy(3, w, relay_from, relay_to))
            passed[-1].start()

        def handed(nth, j):
            copy(4 + j, 0, (*chips[j], 1 - c), me).wait_recv()
            save((*chips[j], 1 - c))
            project(nth, (*chips[j], 1 - c))

        arrived(0, 0)
        arrived(0, 1)
        relay(0)
        for j in range(2):
            save((*chips[j], c))
            project(2 + j, (*chips[j], c))
        handed(4, 0)
        handed(5, 1)
        arrived(1, 0)
        arrived(1, 1)
        relay(1)
        arrived(0, 2)
        save((*chips[2], c))
        project(6, (*chips[2], c))
        handed(7, 2)
        arrived(1, 2)
        copy(0, 1, sibling, me).wait_recv()
        for j, chip in enumerate(chips):
            copy(4 + j, 1, (*chip, 1 - c), me).wait_recv()
        keep = pltpu.make_async_copy(wv, wt_o, loc_sems.at[4])
        for p in range(3):
            out_copy(p, p).wait()
        for cp in first + passed:
            cp.wait_send()
        keep.wait()
        for cp in locs:
            cp.wait()

    vmem = pl.BlockSpec(memory_space=pltpu.VMEM)
    bf16 = jnp.bfloat16
    return pl.pallas_call(
        body, name="gather_proj",
        out_shape=(pltpu.HBM((D, T), bf16), pltpu.HBM((NTILE, D, TILE), bf16),
                   pltpu.HBM((DMIX // NDEV, D), bf16), pltpu.HBM((NDEV, DMIX // NDEV, D), bf16),
                   pltpu.HBM((NDEV, NGROUP, GROUP // NDEV, GROUP), bf16), pltpu.HBM((NTILE, T, TILE), jnp.float32)),
        in_specs=[ANY] + [vmem] * 4, out_specs=(ANY,) * 6,
        scratch_shapes=[pltpu.VMEM((T, D), jnp.float32),
                        pltpu.VMEM((T, D), bf16), pltpu.VMEM((D, T), bf16), pltpu.VMEM((NTILE, D, TILE), bf16),
                        pltpu.VMEM((DMIX // NDEV, D), bf16), pltpu.VMEM((NGROUP, GROUP // NDEV, GROUP), bf16),
                        pltpu.VMEM((3, T, TILE), jnp.float32),
                        pltpu.SemaphoreType.DMA((14,)), pltpu.SemaphoreType.DMA((14,)),
                        pltpu.SemaphoreType.DMA((6,)), pltpu.SemaphoreType.DMA((3,))],
        compiler_params=_params(vmem_mib=56),
    )(x, g1, w_in, w_out, pool_w)


def _split_start(name, arrays, n_copies, plan):
    k = len(arrays)

    def body(*refs):
        send_sems, recv_sems, token = refs[k], refs[k + 1], refs[-1]
        for i, (src, dst, to) in enumerate(plan(refs[:k])):
            pltpu.make_async_remote_copy(src_ref=src, dst_ref=dst, send_sem=send_sems.at[i],
                                         recv_sem=recv_sems.at[i], device_id=to, device_id_type=MESH).start()
        token[...] = jnp.zeros_like(token)

    out = pl.pallas_call(
        body, name=name,
        out_shape=(pltpu.SemaphoreType.DMA((n_copies,)), pltpu.SemaphoreType.DMA((n_copies,)),
                   *[pltpu.HBM(a.shape, a.dtype) for a in arrays], jax.ShapeDtypeStruct((8, 128), jnp.float32)),
        in_specs=[HBM] * k, out_specs=(SEM, SEM, *[HBM] * k, pl.BlockSpec(memory_space=pltpu.VMEM)),
        input_output_aliases={i: 2 + i for i in range(k)},
        compiler_params=pltpu.CompilerParams(has_side_effects=EFFECT),
    )(*[pltpu.with_memory_space_constraint(a, pltpu.HBM) for a in arrays])
    return out[0], out[1], out[2:2 + k], out[-1]


def _split_wait(name, arrays, send_sems, recv_sems, plan, after):
    k = len(arrays)

    def body(*refs):
        sends, recvs = refs[k], refs[k + 1]
        for i, (src, dst, to) in enumerate(plan(refs[:k])):
            cp = pltpu.make_async_remote_copy(src_ref=src, dst_ref=dst, send_sem=sends.at[i], recv_sem=recvs.at[i],
                                              device_id=to, device_id_type=MESH)
            cp.wait_send()
            cp.wait_recv()

    return pl.pallas_call(
        body, name=name,
        out_shape=tuple(pltpu.HBM(a.shape, a.dtype) for a in arrays),
        in_specs=[HBM] * k + [SEM, SEM, ANY], out_specs=(HBM,) * k,
        input_output_aliases={i: i for i in range(k)},
        compiler_params=pltpu.CompilerParams(has_side_effects=EFFECT),
    )(*arrays, send_sems, recv_sems, after)


def _plan_wout(refs):
    src, land = refs
    x, y, c, me = _place()
    return [(src, land.at[me], _peer(x, y, c, r)) for r in range(1, NDEV)]


def _plan_rest(refs):
    gob, gpf, r_out, r_pool = refs
    x, y, c, me = _place()
    plan = []
    for r in range(1, NDEV):
        plan.append((gob.at[me ^ r], r_out.at[r - 1], _peer(x, y, c, r)))
        plan.append((gpf.at[me ^ r], r_pool.at[r - 1], _peer(x, y, c, r)))
    return plan


def _plan_in(refs):
    sums, landing = refs
    x, y, c, _ = _place()
    plan = []
    for rel, (dx, dy) in enumerate(((1, 0), (0, 1), (1, 1))):
        for p in range(3):
            plan.append((sums.at[rel, p], landing.at[rel, p], (x ^ dx, y ^ dy, c)))
    return plan


def _plan_small(refs):
    small, land = refs
    x, y, c, me = _place()
    return [(small, land.at[me], _peer(x, y, c, r)) for r in range(1, NDEV)]


def _seg_tiles(s):
    return (s + 2) % NSEG


_POOL_SPECS = [pl.BlockSpec((None, T, GROUP), lambda g, base=base: (base + g, 0, 0)) for base in (0, 4)]
_HEAD_SPECS = [pl.BlockSpec((None, T, HEAD), lambda h, base=base: (base + h // 2, 0, h % 2))
               for base in (8, 12, 16, 20)]


_POOL_W_SPEC = pl.BlockSpec((NDEV, None, GROUP // NDEV, GROUP), lambda g: (0, g, 0, 0))


def _row_ids(shape):
    return lax.broadcasted_iota(jnp.int32, shape, 0)


BAND_ROWS = 128
HALO = 16


def _window_sum(a, gidx, lead):
    width = lax.shift_left(jnp.int32(2), gidx)
    shape = (BAND_ROWS, BAND_ROWS + HALO)
    t, j = lax.broadcasted_iota(jnp.int32, shape, 0), lax.broadcasted_iota(jnp.int32, shape, 1)
    first = t if lead else t + HALO - width + 1
    band = _bf(jnp.where(j >= first, jnp.where(j < first + width, 1.0, 0.0), 0.0))
    zeros = jnp.zeros((HALO, a.shape[1]), jnp.bfloat16)
    padded = [jnp.concatenate([p, zeros] if lead else [zeros, p], axis=0) for p in _split2(a)]
    out = []
    for r0 in range(0, T, BAND_ROWS):
        slab = jnp.concatenate([p[r0:r0 + BAND_ROWS + HALO] for p in padded], axis=1)
        r = _dot(band, slab, 1, 0)
        out.append(r[:, :a.shape[1]] + r[:, a.shape[1]:])
    return jnp.concatenate(out, axis=0)


def _window_mean(s, gidx):
    inv = jnp.where(gidx == 0, 0.5, jnp.where(gidx == 1, 0.25, jnp.where(gidx == 2, 0.125, 0.0625)))
    width = lax.shift_left(jnp.int32(2), gidx)
    head = s[:16] / jnp.minimum(_row_ids((16, s.shape[1])) + 1, width).astype(jnp.float32)
    return jnp.concatenate([head, s[16:] * inv], axis=0)


def _pool_fwd(proj, pool_w, pool_scale, token):
    def body(u_ref, pg_ref, w_ref, sc_ref, token_any, y_ref):
        del token_any
        gidx = pl.program_id(0)
        u, pg = u_ref[...], pg_ref[...]
        d = _window_mean(_window_sum(u, gidx, False), gidx) - u
        mixed = _dot(_bf(d), w_ref[...].reshape(GROUP, GROUP), 1, 0)
        y_ref[...] = _bf(mixed * sc_ref[...] * (pg * _sigmoid(pg)))

    return pl.pallas_call(
        body, name="pool_fwd", grid=(NGROUP,),
        in_specs=[*_POOL_SPECS, _POOL_W_SPEC, pl.BlockSpec((1, GROUP), lambda g: (0, g)), ANY],
        out_specs=pl.BlockSpec((T, GROUP), lambda g: (0, g)),
        out_shape=pltpu.HBM((T, DMIX), jnp.bfloat16),
        compiler_params=_params(("parallel",)),
    )(proj, proj, pool_w, pool_scale, token)


def _tri(lower):
    r = lax.broadcasted_iota(jnp.int32, (CHUNK, CHUNK), 0)
    c = lax.broadcasted_iota(jnp.int32, (CHUNK, CHUNK), 1)
    return (r >= c) if lower else (r <= c)


def _sum_rows_matrix():
    shape = (CHUNK + 16, CHUNK)
    r, c = lax.broadcasted_iota(jnp.int32, shape, 0), lax.broadcasted_iota(jnp.int32, shape, 1)
    run = jnp.where(c <= r, 1.0, 0.0)
    half = jnp.where(c < CHUNK // 2, 1.0, 0.0)
    return _bf(jnp.where(r < CHUNK, run, jnp.where(r < CHUNK + 8, 1.0, half)))


def _rev_sum_matrix():
    shape = (CHUNK, 2 * CHUNK)
    r, c = lax.broadcasted_iota(jnp.int32, shape, 0), lax.broadcasted_iota(jnp.int32, shape, 1)
    return _bf(jnp.where(c < CHUNK, jnp.where(c >= r, 1.0, 0.0), jnp.where(c - CHUNK < r, 1.0, 0.0)))


def _split2(a):
    hi = _bf(a)
    return [hi, _bf(a - hi.astype(jnp.float32))]


def _exact_sums(mat, pieces):
    x = jnp.concatenate([s for p in pieces for s in _split2(p)], axis=1)
    r = _dot(mat, x, 1, 0)
    return [r[:, 2 * j * HEAD:(2 * j + 1) * HEAD] + r[:, (2 * j + 1) * HEAD:(2 * j + 2) * HEAD]
            for j in range(len(pieces))]


def _gates(qv, fl, lb):
    sq = _sigmoid(qv)
    sg = _sigmoid(fl)
    f = lb + (1.0 - lb) * sg
    return dict(sq=sq, qs=qv * sq, sg=sg, f=f, kk=1.0 - f, g=jnp.log2(f))


def _decays(sums):
    big_g = sums[:CHUNK]
    total = sums[CHUNK:CHUNK + 8]
    g_last = jnp.tile(total, (CHUNK // 8, 1))
    g_mid = jnp.tile(sums[CHUNK + 8:], (CHUNK // 8, 1))
    return dict(
        e_q=jnp.exp2(big_g),
        e_k=jnp.exp2(g_last - big_g),
        e_qm=jnp.exp2(jnp.minimum(big_g - g_mid, EXP_CAP)),
        e_km=jnp.exp2(jnp.minimum(g_mid - big_g, EXP_CAP)),
        total8=jnp.exp2(total))


def _group_rows(gi):
    return [pl.ds(pl.multiple_of((gi * NB + j) * CHUNK, CHUNK), CHUNK) for j in range(NB)]


def _lower_bound(lb_ref):
    return _sigmoid(lb_ref[0:1, :] - lb_ref[1:2, :])


def _hgrn_fwd(proj, lb_logits, rec_g, y_in):
    def body(q_ref, f_ref, i_ref, gate_ref, lb_ref, rg_ref, y_any, y_ref, o_ref, st_ref):
        del y_any
        lb = _lower_bound(lb_ref)
        causal = _tri(True)
        smat = _sum_rows_matrix()

        def group(gi, st):
            rows = _group_rows(gi)
            ts = [_gates(q_ref[r, :], f_ref[r, :], lb) for r in rows]
            ds = [_decays(s) for s in _exact_sums(smat, [t["g"] for t in ts])]
            vs = [_bf(i_ref[r, :]) for r in rows]
            q_m = [_bf(t["qs"] * d["e_qm"]) for t, d in zip(ts, ds)]
            k_m = [_bf(t["kk"] * d["e_km"]) for t, d in zip(ts, ds)]
            q_e = [_bf(t["qs"] * d["e_q"]) for t, d in zip(ts, ds)]
            k_e = [_bf(t["kk"] * d["e_k"]) for t, d in zip(ts, ds)]
            a = [_bf(jnp.where(causal, _dot(q_m[j], k_m[j], 1, 1), 0.0)) for j in range(NB)]
            intra = [_dot(a[j], vs[j], 1, 0) for j in range(NB)]
            upd = [_dot(vs[j], k_e[j], 0, 0) for j in range(NB)]
            for j in range(NB):
                st_ref[gi * NB + j] = st
                o_ref[rows[j], :] = intra[j] + _dot(q_e[j], _bf(st), 1, 1)
                st = st * jnp.tile(ds[j]["total8"], (HEAD // 8, 1)) + upd[j]
            return st

        lax.fori_loop(0, NGRP, group, jnp.zeros((HEAD, HEAD), jnp.float32))
        o = o_ref[...]
        rn = o * lax.rsqrt(jnp.mean(o * o, axis=-1, keepdims=True) + EPS)
        gate = gate_ref[...]
        y_ref[...] = _bf(rn * rg_ref[...] * (gate * _sigmoid(gate)))

    return pl.pallas_call(
        body, name="hgrn_fwd", grid=(NHEAD,),
        in_specs=[*_HEAD_SPECS,
                  pl.BlockSpec((2, HEAD), lambda h: (0, h)),
                  pl.BlockSpec((1, HEAD), lambda h: (0, h)),
                  pl.BlockSpec(memory_space=pl.ANY)],
        out_specs=(pl.BlockSpec((T, HEAD), lambda h: (0, NHEAD + h)),
                   pl.BlockSpec((T, HEAD), lambda h: (0, h)),
                   pl.BlockSpec((None, NCHUNK, HEAD, HEAD), lambda h: (h, 0, 0, 0))),
        out_shape=(pltpu.HBM((T, DMIX), jnp.bfloat16), pltpu.HBM((T, D), jnp.float32),
                   pltpu.HBM((NHEAD, NCHUNK, HEAD, HEAD), jnp.float32)),
        input_output_aliases={6: 0},
        compiler_params=_params(("parallel",)),
    )(proj, proj, proj, proj, lb_logits, rec_g, y_in)


def _out_proj_loss(x, y, w_out, target, gf):
    rows = 512
    parts = [slice(k * rows // 2, (k + 1) * rows // 2) for k in range(2)]

    def body(x_ref, y_ref, w_ref, t_ref, g_ref, dz_ref, dzb_ref, sq_ref, dg_ref):
        zs = [x_ref[p, :] + _dot(y_ref[p, :], w_ref[...], 1, 0) for p in parts]
        sq = dg = 0.0
        for p, z in zip(parts, zs):
            r = lax.rsqrt(jnp.mean(z * z, axis=-1, keepdims=True) + EPS)
            zhat = z * r
            err = zhat * g_ref[...] - t_ref[p, :]
            dy = err * (1.0 / D)
            gdy = dy * g_ref[...]
            dz = r * (gdy - zhat * jnp.mean(zhat * gdy, axis=-1, keepdims=True))
            dz_ref[p, :] = dz
            dzb_ref[p, :] = _bf(dz)
            sq = sq + jnp.sum(err * err, axis=0, keepdims=True)
            dg = dg + jnp.sum(zhat * dy, axis=0, keepdims=True)

        @pl.when(pl.program_id(0) == 0)
        def _():
            sq_ref[...] = sq
            dg_ref[...] = dg

        @pl.when(pl.program_id(0) != 0)
        def _():
            sq_ref[...] += sq
            dg_ref[...] += dg

    tile = pl.BlockSpec((rows, D), lambda i: (i, 0))
    vec = pl.BlockSpec((1, D), lambda i: (0, 0))
    return pl.pallas_call(
        body, name="out_proj_loss", grid=(T // rows,),
        in_specs=[tile, pl.BlockSpec((rows, DMIX), lambda i: (i, 0)), pl.BlockSpec((DMIX, D), lambda i: (0, 0)),
                  tile, vec],
        out_specs=(tile, tile, vec, vec),
        out_shape=(pltpu.HBM((T, D), jnp.float32), pltpu.HBM((T, D), jnp.bfloat16),
                   pltpu.HBM((1, D), jnp.float32), pltpu.HBM((1, D), jnp.float32)),
        compiler_params=_params(("arbitrary",)),
    )(x, y, w_out, target, gf)


def _out_proj_bwd(dzb, w_out, y):
    tn = 512

    def body(dz_ref, w_ref, y_ref, dy_ref, gw_ref, gwb_ref):
        dz = dz_ref[...]
        dy_ref[...] = _dot(dz, w_ref[...], 1, 1)
        gw = _dot(y_ref[...], dz, 0, 0)
        gw_ref[...] = gw
        gwb_ref[...] = _bf(gw)

    return pl.pallas_call(
        body, name="out_proj_bwd", grid=(DMIX // tn,),
        in_specs=[pl.BlockSpec((T, D), lambda n: (0, 0)), pl.BlockSpec((tn, D), lambda n: (n, 0)),
                  pl.BlockSpec((T, tn), lambda n: (0, n))],
        out_specs=(pl.BlockSpec((T, tn), lambda n: (0, n)), pl.BlockSpec((tn, D), lambda n: (n, 0)),
                   pl.BlockSpec((tn, D), lambda n: (n, 0))),
        out_shape=(pltpu.HBM((T, DMIX), jnp.float32), pltpu.HBM((DMIX, D), jnp.float32),
                   pltpu.HBM((DMIX, D), jnp.bfloat16)),
        compiler_params=_params(("parallel",)),
    )(dzb, w_out, y)


def _hgrn_bwd(proj, lb_logits, rec_g, o, states, dymix, dproj_in, token):
    def body(q_ref, f_ref, i_ref, gate_ref, lb_ref, rg_ref, o_ref, st_ref, dy_ref, dp_any, token_any,
             dp_ref, drg_ref, dlb_ref, do_ref):
        del dp_any, token_any
        lb = _lower_bound(lb_ref)
        causal = _tri(True)
        smat, rmat = _sum_rows_matrix(), _rev_sum_matrix()

        o = o_ref[...]
        rs = lax.rsqrt(jnp.mean(o * o, axis=-1, keepdims=True) + EPS)
        rn = o * rs
        gate = gate_ref[...]
        sgate = _sigmoid(gate)
        dyv = dy_ref[...]
        d_r = dyv * (gate * sgate)
        dp_ref[3] = _bf(dyv * (rn * rg_ref[...]) * (sgate * (1.0 + gate * (1.0 - sgate))))
        drg_ref[...] = jnp.sum(d_r * rn, axis=0, keepdims=True)
        drn = d_r * rg_ref[...]
        do_ref[...] = rs * (drn - rn * jnp.mean(rn * drn, axis=-1, keepdims=True))

        def group(i, carry):
            dst, dlb = carry
            gi = NGRP - 1 - i
            rows = _group_rows(gi)
            span = range(NB)
            qvs = [q_ref[r, :] for r in rows]
            ts = [_gates(qv, f_ref[r, :], lb) for qv, r in zip(qvs, rows)]
            ds = [_decays(s) for s in _exact_sums(smat, [t["g"] for t in ts])]
            vs = [_bf(i_ref[r, :]) for r in rows]
            dos = [_bf(do_ref[r, :]) for r in rows]
            sts = [st_ref[gi * NB + j] for j in span]
            qe_f = [t["qs"] * d["e_q"] for t, d in zip(ts, ds)]
            ke_f = [t["kk"] * d["e_k"] for t, d in zip(ts, ds)]
            q_e, k_e = [_bf(a) for a in qe_f], [_bf(a) for a in ke_f]
            q_m = [_bf(t["qs"] * d["e_qm"]) for t, d in zip(ts, ds)]
            k_m = [_bf(t["kk"] * d["e_km"]) for t, d in zip(ts, ds)]
            a = [_bf(jnp.where(causal, _dot(q_m[j], k_m[j], 1, 1), 0.0)) for j in span]
            da = [_bf(jnp.where(causal, _dot(dos[j], vs[j], 1, 1), 0.0)) for j in span]
            dqm = [_dot(da[j], k_m[j], 1, 0) for j in span]
            dkm = [_dot(da[j], q_m[j], 0, 0) for j in span]
            dv_in = [_dot(a[j], dos[j], 0, 0) for j in span]
            dqe = [_dot(dos[j], _bf(sts[j]), 1, 0) for j in span]
            grow = [_dot(dos[j], q_e[j], 0, 0) for j in span]
            dke, carried = [None] * NB, [None] * NB
            for j in reversed(span):
                dst_b = _bf(dst)
                dke[j] = _dot(vs[j], dst_b, 1, 0)
                dp_ref[2, rows[j], :] = _bf(dv_in[j] + _dot(k_e[j], dst_b, 1, 1))
                carried[j] = ds[j]["total8"] * jnp.sum(dst * sts[j], axis=0, keepdims=True)
                dst = dst * jnp.tile(ds[j]["total8"], (HEAD // 8, 1)) + grow[j]
            kdk = [ke_f[j] * dke[j] for j in span]
            pos = [(q_m[j].astype(jnp.float32) * dqm[j] - k_m[j].astype(jnp.float32) * dkm[j]) + qe_f[j] * dqe[j]
                   for j in span]
            dgs = _exact_sums(rmat, [jnp.concatenate([pos[j], kdk[j]], axis=0) for j in span])
            for j in span:
                t, d = ts[j], ds[j]
                dg = dgs[j] + jnp.tile(carried[j], (CHUNK // 8, 1))
                dqs = dqm[j] * d["e_qm"] + dqe[j] * d["e_q"]
                dkk = dkm[j] * d["e_km"] + dke[j] * d["e_k"]
                df = dg / t["f"] - dkk
                dp_ref[1, rows[j], :] = _bf(df * (1.0 - lb) * (t["sg"] * (1.0 - t["sg"])))
                dp_ref[0, rows[j], :] = _bf(dqs * (t["sq"] * (1.0 + qvs[j] * (1.0 - t["sq"]))))
                dlb = dlb + df * (1.0 - t["sg"])
            return dst, dlb

        _, dlb = lax.fori_loop(0, NGRP, group, (jnp.zeros((HEAD, HEAD), jnp.float32),
                                                jnp.zeros((CHUNK, HEAD), jnp.float32)))
        dlb_ref[...] = jnp.sum(dlb, axis=0, keepdims=True)

    vec = pl.BlockSpec((1, HEAD), lambda h: (0, h))
    return pl.pallas_call(
        body, name="hgrn_bwd", grid=(NHEAD,),
        in_specs=[*_HEAD_SPECS,
                  pl.BlockSpec((2, HEAD), lambda h: (0, h)), vec,
                  pl.BlockSpec((T, HEAD), lambda h: (0, h)),
                  pl.BlockSpec((None, NCHUNK, HEAD, HEAD), lambda h: (h, 0, 0, 0)),
                  pl.BlockSpec((T, HEAD), lambda h: (0, NHEAD + h)), ANY, ANY],
        out_specs=(pl.BlockSpec((4, T, HEAD), lambda h: (0, 0, h)), vec, vec),
        out_shape=(pltpu.HBM((NSEG, T, D), jnp.bfloat16),
                   pltpu.HBM((1, D), jnp.float32), pltpu.HBM((1, D), jnp.float32)),
        scratch_shapes=[pltpu.VMEM((T, HEAD), jnp.float32)],
        input_output_aliases={9: 0},
        compiler_params=_params(("parallel",)),
    )(proj, proj, proj, proj, lb_logits, rec_g, o, states, dymix, dproj_in, token)


def _pool_bwd(proj, pool_w, pool_scale, dymix):
    def body(u_ref, pg_ref, w_ref, sc_ref, dy_ref, dp_ref, gw_ref, gs_ref):
        gidx = pl.program_id(0)
        u, pg = u_ref[...], pg_ref[...]
        w = w_ref[...].reshape(GROUP, GROUP)
        d = _bf(_window_mean(_window_sum(u, gidx, False), gidx) - u)
        mixed = _dot(d, w, 1, 0)
        spg = _sigmoid(pg)
        dyv = dy_ref[...]
        d_p = dyv * (pg * spg)
        dp_ref[1] = _bf(dyv * (mixed * sc_ref[...]) * (spg * (1.0 + pg * (1.0 - spg))))
        gs_ref[...] = jnp.sum(d_p * mixed, axis=0, keepdims=True)
        dmixed = _bf(d_p * sc_ref[...])
        gw_ref[...] = _dot(d, dmixed, 0, 0).reshape(gw_ref.shape)
        dd = _dot(dmixed, w, 1, 1)
        dp_ref[0] = _bf(_window_sum(_window_mean(dd, gidx), gidx, True) - dd)

    return pl.pallas_call(
        body, name="pool_bwd", grid=(NGROUP,),
        in_specs=[*_POOL_SPECS, _POOL_W_SPEC,
                  pl.BlockSpec((1, GROUP), lambda g: (0, g)),
                  pl.BlockSpec((T, GROUP), lambda g: (0, g))],
        out_specs=(pl.BlockSpec((2, T, GROUP), lambda g: (2, 0, g)), _POOL_W_SPEC,
                   pl.BlockSpec((1, GROUP), lambda g: (0, g))),
        out_shape=(pltpu.HBM((NSEG, T, D), jnp.bfloat16),
                   pltpu.HBM((NDEV, NGROUP, GROUP // NDEV, GROUP), jnp.float32),
                   pltpu.HBM((1, D), jnp.float32)),
        compiler_params=_params(("parallel",)),
    )(proj, proj, pool_w, pool_scale, dymix)


HALF = NTILE // 2
AWAY = HALF - 3


def _dproj_tile(chip, side, p):
    j = 6 * chip + 3 * side + p
    return ((j // 4 + 4) % NSEG, 0, j % 4)


def _sibling_copy(sib_out, sib_in, send_sems, recv_sems, slot):
    x, y, c, _ = _place()
    return pltpu.make_async_remote_copy(
        src_ref=sib_out.at[slot], dst_ref=sib_in.at[slot], send_sem=send_sems.at[slot],
        recv_sem=recv_sems.at[slot], device_id=(x, y, 1 - c), device_id_type=MESH)


def _proj_bwd_w_far(place, ht, dproj):
    def body(place_ref, h_ref, dpa_ref, dpb_ref, sib_out, sib_in, send_sems, recv_sems, stage, loc_sems):
        del place_ref
        i = pl.program_id(0)

        def to_hbm(k):
            return pltpu.make_async_copy(stage.at[k], sib_out.at[k], loc_sems.at[k])

        def send(k):
            to_hbm(k).wait()
            _sibling_copy(sib_out, sib_in, send_sems, recv_sems, k).start()

        stage[2 * i] = _bf(_dot(h_ref[...], dpa_ref[...], 1, 0))
        stage[2 * i + 1] = _bf(_dot(h_ref[...], dpb_ref[...], 1, 0))

        @pl.when(i > 0)
        def _():
            send(2 * i - 2)
            send(2 * i - 1)

        to_hbm(2 * i).start()
        to_hbm(2 * i + 1).start()

        @pl.when(i == HALF // 2 - 1)
        def _():
            send(2 * i)
            send(2 * i + 1)

    def tile(k, pr):
        return _dproj_tile(k // 3, 1 - pr[2], k % 3)

    buf = pltpu.HBM((HALF, D, TILE), jnp.bfloat16)
    sems = pltpu.SemaphoreType.DMA((HALF,))
    return pl.pallas_call(
        body, name="proj_bwd_w_far",
        grid_spec=pltpu.PrefetchScalarGridSpec(
            num_scalar_prefetch=1, grid=(HALF // 2,),
            in_specs=[pl.BlockSpec((D, T), lambda i, pr: (0, 0)),
                      pl.BlockSpec((None, T, TILE), lambda i, pr: tile(2 * i, pr)),
                      pl.BlockSpec((None, T, TILE), lambda i, pr: tile(2 * i + 1, pr))],
            out_specs=(HBM, HBM, SEM, SEM),
            scratch_shapes=[pltpu.VMEM((HALF, D, TILE), jnp.bfloat16), pltpu.SemaphoreType.DMA((HALF,))]),
        out_shape=(buf, buf, sems, sems),
        compiler_params=pltpu.CompilerParams(dimension_semantics=("arbitrary",), vmem_limit_bytes=48 * MIB,
                                             has_side_effects=EFFECT),
    )(place, ht, dproj, dproj)


def _proj_bwd_w_near(place, ht, dproj, sib_out, sib_in, sib_send, sib_recv):
    def owner_chip(k, pr):
        return jnp.where(k < AWAY, (pr[1] + 1 + k % 3) % 4, pr[1])

    def tile_p(k):
        return jnp.where(k < AWAY, k // 3, k - AWAY)

    def body(place_ref, h_ref, dpa_ref, dpb_ref, sib_out, sib_in, sib_send, sib_recv, sums, own_ref, landing,
             out_send, out_recv, recvbuf, outbuf, in_sems, loc_sems):
        i = pl.program_id(0)
        px, py, c, _ = _place()

        def slot_of(k):
            return 3 * owner_chip(k, place_ref) + tile_p(k)

        def load(k):
            return pltpu.make_async_copy(sib_in.at[slot_of(k)], recvbuf.at[k % 4], in_sems.at[k % 4])

        def fetch(k):
            _sibling_copy(sib_out, sib_in, sib_send, sib_recv, slot_of(k)).wait_recv()
            load(k).start()

        def route(k):
            chip = owner_chip(k, place_ref)
            cx, cy = chip // 2, chip % 2
            return cx, cy, (cx ^ px) + 2 * (cy ^ py) - 1, tile_p(k)

        def to_hbm(k):
            _, _, rel, p = route(k)
            return pltpu.make_async_copy(outbuf.at[k], sums.at[rel, p], loc_sems.at[k])

        def to_owner(k):
            cx, cy, rel, p = route(k)
            return pltpu.make_async_remote_copy(
                src_ref=sums.at[rel, p], dst_ref=landing.at[rel, p], send_sem=out_send.at[3 * rel + p],
                recv_sem=out_recv.at[3 * rel + p], device_id=(cx, cy, c), device_id_type=MESH)

        first, second = 2 * i, 2 * i + 1

        @pl.when(i == 0)
        def _():
            fetch(first)
            fetch(second)

        @pl.when(i < HALF // 2 - 1)
        def _():
            fetch(first + 2)
            fetch(second + 2)

        load(first).wait()
        load(second).wait()
        for k, dp_ref in ((first, dpa_ref), (second, dpb_ref)):
            total =_dot(h_ref[...], dp_ref[...], 1, 0) + recvbuf[k % 4].astype(jnp.float32)
            own_ref[jnp.maximum(k - AWAY, 0)] = total
            outbuf[jnp.minimum(k, AWAY)] = _bf(total)

        for k in (first, second):
            @pl.when(k < AWAY)
            def _():
                to_hbm(k).start()

        for k in (first - 2, second - 2):
            @pl.when(jnp.logical_and(k >= 0, k < AWAY))
            def _():
                to_hbm(k).wait()
                to_owner(k).start()

        @pl.when(i == HALF // 2 - 1)
        def _():
            for slot in range(HALF):
                _sibling_copy(sib_out, sib_in, sib_send, sib_recv, slot).wait_send()

    def tile(k, pr):
        return _dproj_tile(owner_chip(k, pr), pr[2], tile_p(k))

    travelling = pltpu.HBM((3, 3, D, TILE), jnp.bfloat16)
    sems = pltpu.SemaphoreType.DMA((AWAY,))
    return pl.pallas_call(
        body, name="proj_bwd_w_near",
        grid_spec=pltpu.PrefetchScalarGridSpec(
            num_scalar_prefetch=1, grid=(HALF // 2,),
            in_specs=[pl.BlockSpec((D, T), lambda i, pr: (0, 0)),
                      pl.BlockSpec((None, T, TILE), lambda i, pr: tile(2 * i, pr)),
                      pl.BlockSpec((None, T, TILE), lambda i, pr: tile(2 * i + 1, pr)),
                      HBM, HBM, SEM, SEM],
            out_specs=(HBM, pl.BlockSpec((3, D, TILE), lambda i, pr: (0, 0, 0)), HBM, SEM, SEM),
            scratch_shapes=[pltpu.VMEM((4, D, TILE), jnp.bfloat16), pltpu.VMEM((AWAY + 1, D, TILE), jnp.bfloat16),
                            pltpu.SemaphoreType.DMA((4,)), pltpu.SemaphoreType.DMA((AWAY,))]),
        out_shape=(travelling, pltpu.HBM((3, D, TILE), jnp.float32), travelling, sems, sems),
        compiler_params=pltpu.CompilerParams(dimension_semantics=("arbitrary",), vmem_limit_bytes=48 * MIB,
                                             has_side_effects=EFFECT),
    )(place, ht, dproj, dproj, sib_out, sib_in, sib_send, sib_recv)


def _proj_bwd_x(dproj, w_t, x, g1, dz, others, token):
    tm = 512
    pairs = NSEG // 2
    k = len(others)

    def body(dp_ref, w_ref, x_ref, g_ref, dz_ref, *refs):
        dx_ref, dg_ref, wcat, acc = refs[k + 1:]
        m, s = pl.program_id(0), pl.program_id(1)

        @pl.when(m == 0)
        def _():
            for i in range(8):
                wcat[s, :, i * TILE:(i + 1) * TILE] = w_ref[i]

        @pl.when(s == 0)
        def _():
            acc[...] = jnp.zeros((tm, D), jnp.float32)

        acc[...] += _dot(jnp.concatenate([dp_ref[0], dp_ref[1]], axis=1), wcat[s], 1, 1)

        @pl.when(s == pairs - 1)
        def _():
            xv = x_ref[...]
            rs = lax.rsqrt(jnp.mean(xv * xv, axis=-1, keepdims=True) + EPS)
            xhat = xv * rs
            dhv = acc[...]
            gdh = dhv * g_ref[...]
            dx_ref[...] = dz_ref[...] + rs * (gdh - xhat * jnp.mean(xhat * gdh, axis=-1, keepdims=True))
            dg = jnp.sum(xhat * dhv, axis=0, keepdims=True)

            @pl.when(m == 0)
            def _():
                dg_ref[0:1, :] = dg
                for r, ref in enumerate(refs[:k]):
                    dg_ref[1 + r:2 + r, :] = ref[...]
                dg_ref[1 + k:, :] = jnp.zeros((7 - k, D), jnp.float32)

            @pl.when(m != 0)
            def _():
                dg_ref[0:1, :] += dg

    rows = pl.BlockSpec((tm, D), lambda m, s: (m, 0))
    vec = pl.BlockSpec((1, D), lambda m, s: (0, 0))
    return pl.pallas_call(
        body, name="proj_bwd_x", grid=(T // tm, pairs),
        in_specs=[pl.BlockSpec((2, tm, D), lambda m, s: (s, m, 0)),
                  pl.BlockSpec((8, D, TILE), lambda m, s: ((jnp.where(m == 0, s, pairs - 1) + 1) % pairs, 0, 0)),
                  rows, vec, rows, *[vec] * k, ANY],
        out_specs=(rows, pl.BlockSpec((8, D), lambda m, s: (0, 0))),
        out_shape=(jax.ShapeDtypeStruct((T, D), jnp.float32), pltpu.HBM((8, D), jnp.float32)),
        scratch_shapes=[pltpu.VMEM((pairs, D, 2 * D), jnp.bfloat16), pltpu.VMEM((tm, D), jnp.float32)],
        compiler_params=_params(("arbitrary", "arbitrary"), vmem_mib=56),
    )(dproj, w_t, x, g1, dz, *[_in_hbm(a) for a in others], token)


def _adamw(w, g, m, v):
    m_new = ADAM_B1 * m + (1.0 - ADAM_B1) * g
    v_new = ADAM_B2 * v + (1.0 - ADAM_B2) * (g * g)
    delta = -ADAM_LR * ((m_new / BC1) / (jnp.sqrt(v_new / BC2) + ADAM_EPS) + ADAM_WD * w)
    return delta, m_new, v_new


def _reduce_adam(name, place, parts, w, m, v, grid, w_spec):
    n = len(parts)

    def body(place_ref, *refs):
        del place_ref
        w_ref, m_ref, v_ref, g_ref, d_ref, mo_ref, vo_ref = refs[n:]
        g = None
        for ref, (_, _, stacked) in zip(refs[:n], parts):
            terms = [ref[r] for r in range(ref.shape[0])] if stacked else [ref[...]]
            for t in terms:
                if t.shape[-1] != w_ref.shape[-1]:
                    t = jnp.concatenate([t[p] for p in range(t.shape[0])], axis=1)
                g = t.astype(jnp.float32) if g is None else g + t.astype(jnp.float32)
        delta, m_new, v_new = _adamw(w_ref[...], g, m_ref[...], v_ref[...])
        g_ref[...] = g
        d_ref[...] = delta
        mo_ref[...] = m_new
        vo_ref[...] = v_new

    shape = jax.ShapeDtypeStruct(w.shape, jnp.float32)
    return pl.pallas_call(
        body, name=name,
        grid_spec=pltpu.PrefetchScalarGridSpec(
            num_scalar_prefetch=1, grid=grid,
            in_specs=[spec for _, spec, _ in parts] + [w_spec] * 3, out_specs=(w_spec,) * 4),
        out_shape=(shape,) * 4,
        compiler_params=_params(("parallel",)),
    )(place, *[_in_hbm(a) for a in [a for a, _, _ in parts] + [w, m, v]])


SMALL_ROWS = (1, 1, 2, 1, 1)


def _small_adam(place, own, parts, ws, ms, vs):
    n = len(SMALL_ROWS)

    def body(place_ref, own_ref, p_ref, *refs):
        ins, outs, bufs = refs[:3 * n], refs[3 * n:3 * n + 1 + 4 * n], refs[3 * n + 1 + 4 * n:]

        def stacked(group, buf):
            r0 = 0
            for ref, k in zip(group, SMALL_ROWS):
                buf[r0:r0 + k, :] = ref[...]
                r0 += k
            buf[r0:, :] = jnp.zeros((8 - r0, D), jnp.float32)
            return buf[...]

        wv, mv, vv = (stacked(ins[n * j:n * j + n], bufs[j]) for j in range(3))
        me = place_ref[0]
        g = None
        for s in range(NDEV):
            term = jnp.where(me == s, own_ref[...], p_ref[s])
            g = term if g is None else g + term
        rows = _row_ids(wv.shape)
        other = jnp.where(rows == 2, pltpu.roll(wv, 7, 0), jnp.where(rows == 3, pltpu.roll(wv, 1, 0), 0.0))
        lbv = _sigmoid(wv - other)
        sign = jnp.where(rows == 2, 1.0, -1.0)
        g = jnp.where((rows == 2) | (rows == 3), sign * g * lbv * (1.0 - lbv), g)
        delta, m_new, v_new = _adamw(wv, g, mv, vv)
        outs[0][...] = jnp.sum(g[6:7], axis=1, keepdims=True) * (0.5 / D)
        for j, val in enumerate((g, delta, m_new, v_new)):
            r0 = 0
            for ref, k in zip(outs[1 + n * j:1 + n * j + n], SMALL_ROWS):
                ref[...] = val[r0:r0 + k]
                r0 += k

    vmem = pl.BlockSpec(memory_space=pltpu.VMEM)
    shapes = [jax.ShapeDtypeStruct((k, D), jnp.float32) for k in SMALL_ROWS]
    out = pl.pallas_call(
        body, name="small_adam", out_shape=(jax.ShapeDtypeStruct((1, 1), jnp.float32), *shapes * 4),
        in_specs=[pl.BlockSpec(memory_space=pltpu.SMEM)] + [vmem] * (2 + 3 * n), out_specs=(vmem,) * (1 + 4 * n),
        scratch_shapes=[pltpu.VMEM((8, D), jnp.float32)] * 3,
    )(place, own, parts, *ws, *ms, *vs)
    return out[0], [out[1 + n * j:1 + n * j + n] for j in range(4)]


def kernel(x, norm1_g, w_in, pool_w, pool_scale, lb_logits, rec_norm_g, w_out, final_norm_g, loss_target, m_norm1_g, m_w_in, m_pool_w, m_pool_scale, m_lb_logits, m_rec_norm_g, m_w_out, m_final_norm_g, v_norm1_g, v_w_in, v_pool_w, v_pool_scale, v_lb_logits, v_rec_norm_g, v_w_out, v_final_norm_g):
    xs = x[0]
    target = loss_target[0]
    ix, iy, ic = lax.axis_index("x"), lax.axis_index("y"), lax.axis_index("c")
    place = jnp.stack([4 * ix + 2 * iy + ic, 2 * ix + iy, ic]).astype(jnp.int32)
    gf = final_norm_g.reshape(1, D)

    ht, w_t, w_out_b, w_out_g, pool_g, proj = _gather_proj(xs, norm1_g, w_in, w_out, pool_w)
    wout = [w_out_b, w_out_g]
    wout_send, wout_recv, wout, wout_token = _split_start("gather_wout_start", wout, NDEV - 1, _plan_wout)

    y = _pool_fwd(proj, pool_g, pool_scale, wout_token)
    y, o, states = _hgrn_fwd(proj, lb_logits, rec_norm_g, y)
    _, w_out_g = _split_wait("gather_wout_wait", wout, wout_send, wout_recv, _plan_wout, o)
    w_out_full = _in_hbm(w_out_g.reshape(DMIX, D))
    dz, dzb, sq, dgf = _out_proj_loss(xs, y, w_out_full, target, gf)

    dymix, gwout_f, gwout_b = _out_proj_bwd(dzb, w_out_full, y)
    dproj, gpool, dscale = _pool_bwd(proj, pool_g, pool_scale, dymix)

    blk_out = (NDEV, DMIX // NDEV, D)
    blk_pool = (NDEV, NGROUP, GROUP // NDEV, GROUP)
    rest = [gwout_b.reshape(blk_out), gpool,
            lax.empty((NDEV - 1,) + blk_out[1:], jnp.bfloat16), lax.empty((NDEV - 1,) + blk_pool[1:], jnp.float32)]
    rest_send, rest_recv, rest, rest_token = _split_start("scatter_rest_start", rest, 2 * (NDEV - 1), _plan_rest)

    dproj, drecg, dlb = _hgrn_bwd(proj, lb_logits, rec_norm_g, o, states, dymix, dproj, rest_token)
    chip_sums, own_sum, landing, win_send, win_recv = _proj_bwd_w_near(
        place, ht, dproj, *_proj_bwd_w_far(place, ht, dproj))
    win = [chip_sums, landing]

    others = [a.reshape(1, D) for a in (dscale, dlb, dlb, drecg, dgf, sq)]
    grad_x, small_block = _proj_bwd_x(dproj, w_t, xs, norm1_g, dz, others, chip_sums)

    small = [small_block, lax.empty((NDEV, 8, D), jnp.float32)]
    small_send, small_recv, small, small_token = _split_start("gather_small_start", small, NDEV - 1, _plan_small)

    _, gpool_own, r_out, r_pool = _split_wait("scatter_rest_wait", rest, rest_send, rest_recv, _plan_rest,
                                              small_token)
    g_wout, d_wout, m_wout, v_wout = _reduce_adam(
        "adam_w_out", place,
        [(gwout_f.reshape(blk_out), pl.BlockSpec((None,) + blk_out[1:], lambda i, pr: (pr[0], 0, 0)), False),
         (r_out, pl.BlockSpec((NDEV - 1,) + blk_out[1:], lambda i, pr: (0, 0, 0)), True)],
        w_out, m_w_out, v_w_out, (1,), pl.BlockSpec((None,) + blk_out[1:], lambda i, pr: (0, 0, 0)))
    g_pool, d_pool, m_pool, v_pool = _reduce_adam(
        "adam_pool_w", place,
        [(gpool_own, pl.BlockSpec((None,) + blk_pool[1:], lambda i, pr: (pr[0], 0, 0, 0)), False),
         (r_pool, pl.BlockSpec((NDEV - 1,) + blk_pool[1:], lambda i, pr: (0, 0, 0, 0)), True)],
        pool_w, m_pool_w, v_pool_w, (1,), pl.BlockSpec((None,) + blk_pool[1:], lambda i, pr: (0, 0, 0, 0)))

    _, r_in = _split_wait("scatter_win_wait", win, win_send, win_recv, _plan_in, d_pool)
    g_win, d_win, m_win, v_win = _reduce_adam(
        "adam_w_in", place,
        [(own_sum, pl.BlockSpec((3, D // 8, TILE), lambda i, pr: (0, i, 0)), False),
         (r_in, pl.BlockSpec((3, 3, D // 8, TILE), lambda i, pr: (0, 0, i, 0)), True)],
        w_in, m_w_in, v_w_in, (8,), pl.BlockSpec((None, D // 8, 3 * TILE), lambda i, pr: (0, i, 0)))

    own_small, r_small = _split_wait("gather_small_wait", small, small_send, small_recv, _plan_small, d_win)
    loss, (g_s, d_s, m_s, v_s) = _small_adam(
        place, own_small, r_small,
        (norm1_g, pool_scale, lb_logits, rec_norm_g, gf),
        (m_norm1_g, m_pool_scale, m_lb_logits, m_rec_norm_g, m_final_norm_g.reshape(1, D)),
        (v_norm1_g, v_pool_scale, v_lb_logits, v_rec_norm_g, v_final_norm_g.reshape(1, D)))

    def outs(small, win, pool, wout):
        n1, ps, lbl, rg, fg = small
        return n1, win, pool, ps, lbl, rg, wout, fg.reshape(D)

    return (loss.reshape(()), grad_x[None],
            *outs(g_s, g_win, g_pool, g_wout), *outs(d_s, d_win, d_pool, d_wout),
            *outs(m_s, m_win, m_pool, m_wout), *outs(v_s, v_win, v_pool, v_wout))
```

```python
import jax
import jax.numpy as jnp
from jax import lax
from jax.experimental import pallas as pl
from jax.experimental.pallas import tpu as pltpu

T = 2048
D = 1024
NSEG = 6
NTILE = 24
TILE = 256
DMIX = 2048
NDEV = 8
HEAD = 128
NHEAD = 8
CHUNK = 64
NCHUNK = T // CHUNK
NB = 32
NGRP = NCHUNK // NB
NGROUP = 4
GROUP = 256
EPS = 1e-6
EXP_CAP = 115.0
MESH = pl.DeviceIdType.MESH
ANY = pl.BlockSpec(memory_space=pl.ANY)
HBM = pl.BlockSpec(memory_space=pltpu.HBM)
SEM = pl.BlockSpec(memory_space=pltpu.SEMAPHORE)
EFFECT = pltpu.SideEffectType.DATAFLOW_SIDE_EFFECTING

ADAM_LR = 0.001
ADAM_B1 = 0.9
ADAM_B2 = 0.999
ADAM_EPS = 1e-08
ADAM_WD = 0.01
ADAM_STEP = 10
BC1 = 1.0 - ADAM_B1 ** ADAM_STEP
BC2 = 1.0 - ADAM_B2 ** ADAM_STEP

MIB = 1 << 20


def _params(sem=None, vmem_mib=48):
    return pltpu.CompilerParams(dimension_semantics=sem, vmem_limit_bytes=vmem_mib * MIB)


def _sigmoid(v):
    return 1.0 / (1.0 + jnp.exp(-v))


def _dot(a, b, ca, cb, precision=None):
    return lax.dot_general(a, b, (((ca,), (cb,)), ((), ())), precision=precision,
                           preferred_element_type=jnp.float32)


def _bf(v):
    return v.astype(jnp.bfloat16)


def _in_hbm(a):
    return pltpu.with_memory_space_constraint(a, pltpu.HBM)


def _place():
    x, y, c = lax.axis_index("x"), lax.axis_index("y"), lax.axis_index("c")
    return x, y, c, 4 * x + 2 * y + c


def _peer(x, y, c, r):
    return (x ^ ((r >> 2) & 1), y ^ ((r >> 1) & 1), c ^ (r & 1))


def _gather_proj(x, g1, w_in, w_out, pool_w):
    def body(x_ref, g_ref, win_ref, wout_ref, pool_ref, ht_o, wt_o, woutb_o, wout_o, pool_o, proj_o,
             xbuf, hv, htv, wv, wob, pb, stage, send_sems, recv_sems, loc_sems, out_sems):
        px, py, c, my_idx = _place()
        fetch_x = pltpu.make_async_copy(x_ref, xbuf, loc_sems.at[5])
        fetch_x.start()
        me, sibling = (px, py, c), (px, py, 1 - c)
        chips = [(1 - px, py), (px, 1 - py), (1 - px, 1 - py)]
        for p in range(3):
            wv[3 * my_idx + p] = _bf(win_ref[0, :, p * TILE:(p + 1) * TILE])

        def index(bx, by, bc):
            return 4 * bx + 2 * by + bc

        def slot(w, block):
            return wv.at[pl.ds(3 * index(*block), 3)] if w == 0 else pool_o.at[index(*block)]

        def copy(k, w, block, to, src=None):
            return pltpu.make_async_remote_copy(
                src_ref=slot(w, block) if src is None else src, dst_ref=slot(w, block),
                send_sem=send_sems.at[2 * k + w], recv_sem=recv_sems.at[2 * k + w],
                device_id=to, device_id_type=MESH)

        def save(block):
            at = pl.ds(3 * index(*block), 3)
            pltpu.make_async_copy(wv.at[at], wt_o.at[at], loc_sems.at[4]).start()

        srcs = (slot(0, me), pb)
        first = []
        for w in (0, 1):
            if w == 1:
                pb[...] = _bf(pool_ref[0])
                wob[...] = _bf(wout_ref[0])
            group = [copy(1 + j, w, me, (*chip, c), src=srcs[w]) for j, chip in enumerate(chips[:2])]
            group.append(copy(0, w, me, sibling, src=srcs[w]))
            for cp in group:
                cp.start()
            first += group
        save(me)
        locs = [pltpu.make_async_copy(pb, slot(1, me), loc_sems.at[0]),
                pltpu.make_async_copy(wob, wout_o.at[my_idx], loc_sems.at[1]),
                pltpu.make_async_copy(wob, woutb_o, loc_sems.at[2])]
        for cp in locs:
            cp.start()

        fetch_x.wait()
        xv = xbuf[...]
        hv[...] = _bf(xv * lax.rsqrt(jnp.mean(xv * xv, axis=-1, keepdims=True) + EPS) * g_ref[...])
        rows = 256
        for r0 in range(0, T, rows):
            htv[:, r0:r0 + rows] = hv[r0:r0 + rows, :].T
        locs.append(pltpu.make_async_copy(htv, ht_o, loc_sems.at[3]))
        locs[-1].start()

        def out_copy(p, j):
            return pltpu.make_async_copy(stage.at[p], proj_o.at[j], out_sems.at[p])

        def project(nth, block):
            base = 3 * index(*block)

            def tile(p, carry):
                if nth > 0:
                    out_copy(p, base + p).wait()
                stage[p] = _dot(hv[...], wv[base + p], 1, 0)
                out_copy(p, base + p).start()
                return carry

            lax.fori_loop(0, 3, tile, 0)

        project(0, me)
        copy(0, 0, sibling, me).wait_recv()
        save(sibling)
        project(1, sibling)
        passed = []
        relay_from = (px ^ (1 - c), py ^ c, c)
        relay_to = (px ^ c, py ^ (1 - c), c)

        def arrived(w, j):
            copy(1 + j, w, (*chips[j], c), me).wait_recv()
            passed.append(copy(4 + j, w, (*chips[j], c), sibling))
            passed[-1].start()

        def relay(w):
            passed.append(copy(3, w, relay_from, relay_to))
            passed[-1].start()

        def handed(nth, j):
            copy(4 + j, 0, (*chips[j], 1 - c), me).wait_recv()
            save((*chips[j], 1 - c))
            project(nth, (*chips[j], 1 - c))

        arrived(0, 0)
        arrived(0, 1)
        relay(0)
        for j in range(2):
            save((*chips[j], c))
            project(2 + j, (*chips[j], c))
        handed(4, 0)
        handed(5, 1)
        arrived(1, 0)
        arrived(1, 1)
        relay(1)
        arrived(0, 2)
        save((*chips[2], c))
        project(6, (*chips[2], c))
        handed(7, 2)
        arrived(1, 2)
        copy(0, 1, sibling, me).wait_recv()
        for j, chip in enumerate(chips):
            copy(4 + j, 1, (*chip, 1 - c), me).wait_recv()
        keep = pltpu.make_async_copy(wv, wt_o, loc_sems.at[4])
        for p in range(3):
            out_copy(p, p).wait()
        for cp in first + passed:
            cp.wait_send()
        keep.wait()
        for cp in locs:
            cp.wait()

    vmem = pl.BlockSpec(memory_space=pltpu.VMEM)
    bf16 = jnp.bfloat16
    return pl.pallas_call(
        body, name="gather_proj",
        out_shape=(pltpu.HBM((D, T), bf16), pltpu.HBM((NTILE, D, TILE), bf16),
                   pltpu.HBM((DMIX // NDEV, D), bf16), pltpu.HBM((NDEV, DMIX // NDEV, D), bf16),
                   pltpu.HBM((NDEV, NGROUP, GROUP // NDEV, GROUP), bf16), pltpu.HBM((NTILE, T, TILE), jnp.float32)),
        in_specs=[ANY] + [vmem] * 4, out_specs=(ANY,) * 6,
        scratch_shapes=[pltpu.VMEM((T, D), jnp.float32),
                        pltpu.VMEM((T, D), bf16), pltpu.VMEM((D, T), bf16), pltpu.VMEM((NTILE, D, TILE), bf16),
                        pltpu.VMEM((DMIX // NDEV, D), bf16), pltpu.VMEM((NGROUP, GROUP // NDEV, GROUP), bf16),
                        pltpu.VMEM((3, T, TILE), jnp.float32),
                        pltpu.SemaphoreType.DMA((14,)), pltpu.SemaphoreType.DMA((14,)),
                        pltpu.SemaphoreType.DMA((6,)), pltpu.SemaphoreType.DMA((3,))],
        compiler_params=_params(vmem_mib=56),
    )(x, g1, w_in, w_out, pool_w)


def _split_start(name, arrays, n_copies, plan):
    k = len(arrays)

    def body(*refs):
        send_sems, recv_sems, token = refs[k], refs[k + 1], refs[-1]
        for i, (src, dst, to) in enumerate(plan(refs[:k])):
            pltpu.make_async_remote_copy(src_ref=src, dst_ref=dst, send_sem=send_sems.at[i],
                                         recv_sem=recv_sems.at[i], device_id=to, device_id_type=MESH).start()
        token[...] = jnp.zeros_like(token)

    out = pl.pallas_call(
        body, name=name,
        out_shape=(pltpu.SemaphoreType.DMA((n_copies,)), pltpu.SemaphoreType.DMA((n_copies,)),
                   *[pltpu.HBM(a.shape, a.dtype) for a in arrays], jax.ShapeDtypeStruct((8, 128), jnp.float32)),
        in_specs=[HBM] * k, out_specs=(SEM, SEM, *[HBM] * k, pl.BlockSpec(memory_space=pltpu.VMEM)),
        input_output_aliases={i: 2 + i for i in range(k)},
        compiler_params=pltpu.CompilerParams(has_side_effects=EFFECT),
    )(*[pltpu.with_memory_space_constraint(a, pltpu.HBM) for a in arrays])
    return out[0], out[1], out[2:2 + k], out[-1]


def _split_wait(name, arrays, send_sems, recv_sems, plan, after):
    k = len(arrays)

    def body(*refs):
        sends, recvs = refs[k], refs[k + 1]
        for i, (src, dst, to) in enumerate(plan(refs[:k])):
            cp = pltpu.make_async_remote_copy(src_ref=src, dst_ref=dst, send_sem=sends.at[i], recv_sem=recvs.at[i],
                                              device_id=to, device_id_type=MESH)
            cp.wait_send()
            cp.wait_recv()

    return pl.pallas_call(
        body, name=name,
        out_shape=tuple(pltpu.HBM(a.shape, a.dtype) for a in arrays),
        in_specs=[HBM] * k + [SEM, SEM, ANY], out_specs=(HBM,) * k,
        input_output_aliases={i: i for i in range(k)},
        compiler_params=pltpu.CompilerParams(has_side_effects=EFFECT),
    )(*arrays, send_sems, recv_sems, after)


def _plan_wout(refs):
    src, land = refs
    x, y, c, me = _place()
    return [(src, land.at[me], _peer(x, y, c, r)) for r in range(1, NDEV)]


def _plan_rest(refs):
    gob, gpf, r_out, r_pool = refs
    x, y, c, me = _place()
    plan = []
    for r in range(1, NDEV):
        plan.append((gob.at[me ^ r], r_out.at[r - 1], _peer(x, y, c, r)))
        plan.append((gpf.at[me ^ r], r_pool.at[r - 1], _peer(x, y, c, r)))
    return plan


def _plan_in(refs):
    sums, landing = refs
    x, y, c, _ = _place()
    plan = []
    for rel, (dx, dy) in enumerate(((1, 0), (0, 1), (1, 1))):
        for p in range(3):
            plan.append((sums.at[rel, p], landing.at[rel, p], (x ^ dx, y ^ dy, c)))
    return plan


def _plan_small(refs):
    small, land = refs
    x, y, c, me = _place()
    return [(small, land.at[me], _peer(x, y, c, r)) for r in range(1, NDEV)]


_POOL_SPECS = [pl.BlockSpec((None, T, GROUP), lambda g, base=base: (base + g, 0, 0)) for base in (0, 4)]
_HEAD_SPECS = [pl.BlockSpec((None, T, HEAD), lambda h, base=base: (base + h // 2, 0, h % 2))
               for base in (8, 12, 16, 20)]


_POOL_W_SPEC = pl.BlockSpec((NDEV, None, GROUP // NDEV, GROUP), lambda g: (0, g, 0, 0))


def _row_ids(shape):
    return lax.broadcasted_iota(jnp.int32, shape, 0)


BAND_ROWS = 128
HALO = 16


def _window_sum(a, gidx, lead):
    width = lax.shift_left(jnp.int32(2), gidx)
    shape = (BAND_ROWS, BAND_ROWS + HALO)
    t, j = lax.broadcasted_iota(jnp.int32, shape, 0), lax.broadcasted_iota(jnp.int32, shape, 1)
    first = t if lead else t + HALO - width + 1
    band = _bf(jnp.where(j >= first, jnp.where(j < first + width, 1.0, 0.0), 0.0))
    zeros = jnp.zeros((HALO, a.shape[1]), jnp.bfloat16)
    padded = [jnp.concatenate([p, zeros] if lead else [zeros, p], axis=0) for p in _split2(a)]
    out = []
    for r0 in range(0, T, BAND_ROWS):
        slab = jnp.concatenate([p[r0:r0 + BAND_ROWS + HALO] for p in padded], axis=1)
        r = _dot(band, slab, 1, 0)
        out.append(r[:, :a.shape[1]] + r[:, a.shape[1]:])
    return jnp.concatenate(out, axis=0)


def _window_mean(s, gidx):
    inv = jnp.where(gidx == 0, 0.5, jnp.where(gidx == 1, 0.25, jnp.where(gidx == 2, 0.125, 0.0625)))
    width = lax.shift_left(jnp.int32(2), gidx)
    head = s[:16] / jnp.minimum(_row_ids((16, s.shape[1])) + 1, width).astype(jnp.float32)
    return jnp.concatenate([head, s[16:] * inv], axis=0)


def _pool_fwd(proj, pool_w, pool_scale, token):
    def body(u_ref, pg_ref, w_ref, sc_ref, token_any, y_ref):
        del token_any
        gidx = pl.program_id(0)
        u, pg = u_ref[...], pg_ref[...]
        d = _window_mean(_window_sum(u, gidx, False), gidx) - u
        mixed = _dot(_bf(d), w_ref[...].reshape(GROUP, GROUP), 1, 0)
        y_ref[...] = _bf(mixed * sc_ref[...] * (pg * _sigmoid(pg)))

    return pl.pallas_call(
        body, name="pool_fwd", grid=(NGROUP,),
        in_specs=[*_POOL_SPECS, _POOL_W_SPEC, pl.BlockSpec((1, GROUP), lambda g: (0, g)), ANY],
        out_specs=pl.BlockSpec((T, GROUP), lambda g: (0, g)),
        out_shape=pltpu.HBM((T, DMIX), jnp.bfloat16),
        compiler_params=_params(("parallel",)),
    )(proj, proj, pool_w, pool_scale, token)


def _tri(lower):
    r = lax.broadcasted_iota(jnp.int32, (CHUNK, CHUNK), 0)
    c = lax.broadcasted_iota(jnp.int32, (CHUNK, CHUNK), 1)
    return (r >= c) if lower else (r <= c)


def _sum_rows_matrix():
    shape = (CHUNK + 16, CHUNK)
    r, c = lax.broadcasted_iota(jnp.int32, shape, 0), lax.broadcasted_iota(jnp.int32, shape, 1)
    run = jnp.where(c <= r, 1.0, 0.0)
    half = jnp.where(c < CHUNK // 2, 1.0, 0.0)
    return _bf(jnp.where(r < CHUNK, run, jnp.where(r < CHUNK + 8, 1.0, half)))


def _rev_sum_matrix():
    shape = (CHUNK, 2 * CHUNK)
    r, c = lax.broadcasted_iota(jnp.int32, shape, 0), lax.broadcasted_iota(jnp.int32, shape, 1)
    return _bf(jnp.where(c < CHUNK, jnp.where(c >= r, 1.0, 0.0), jnp.where(c - CHUNK < r, 1.0, 0.0)))


def _split2(a):
    hi = _bf(a)
    return [hi, _bf(a - hi.astype(jnp.float32))]


def _exact_sums(mat, pieces):
    x = jnp.concatenate([s for p in pieces for s in _split2(p)], axis=1)
    r = _dot(mat, x, 1, 0)
    return [r[:, 2 * j * HEAD:(2 * j + 1) * HEAD] + r[:, (2 * j + 1) * HEAD:(2 * j + 2) * HEAD]
            for j in range(len(pieces))]


def _gates(qv, fl, lb):
    sq = _sigmoid(qv)
    sg = _sigmoid(fl)
    f = lb + (1.0 - lb) * sg
    return dict(sq=sq, qs=qv * sq, sg=sg, f=f, kk=1.0 - f, g=jnp.log2(f))


def _decays(sums):
    big_g = sums[:CHUNK]
    total = sums[CHUNK:CHUNK + 8]
    g_last = jnp.tile(total, (CHUNK // 8, 1))
    g_mid = jnp.tile(sums[CHUNK + 8:], (CHUNK // 8, 1))
    return dict(
        e_q=jnp.exp2(big_g),
        e_k=jnp.exp2(g_last - big_g),
        e_qm=jnp.exp2(jnp.minimum(big_g - g_mid, EXP_CAP)),
        e_km=jnp.exp2(jnp.minimum(g_mid - big_g, EXP_CAP)),
        total8=jnp.exp2(total))


def _group_rows(gi):
    return [pl.ds(pl.multiple_of((gi * NB + j) * CHUNK, CHUNK), CHUNK) for j in range(NB)]


def _lower_bound(lb_ref):
    return _sigmoid(lb_ref[0:1, :] - lb_ref[1:2, :])


def _hgrn_fwd(proj, lb_logits, rec_g, y_in):
    def body(q_ref, f_ref, i_ref, gate_ref, lb_ref, rg_ref, y_any, y_ref, o_ref, st_ref):
        del y_any
        lb = _lower_bound(lb_ref)
        causal = _tri(True)
        smat = _sum_rows_matrix()

        def group(gi, st):
            rows = _group_rows(gi)
            ts = [_gates(q_ref[r, :], f_ref[r, :], lb) for r in rows]
            ds = [_decays(s) for s in _exact_sums(smat, [t["g"] for t in ts])]
            vs = [_bf(i_ref[r, :]) for r in rows]
            q_m = [_bf(t["qs"] * d["e_qm"]) for t, d in zip(ts, ds)]
            k_m = [_bf(t["kk"] * d["e_km"]) for t, d in zip(ts, ds)]
            q_e = [_bf(t["qs"] * d["e_q"]) for t, d in zip(ts, ds)]
            k_e = [_bf(t["kk"] * d["e_k"]) for t, d in zip(ts, ds)]
            a = [_bf(jnp.where(causal, _dot(q_m[j], k_m[j], 1, 1), 0.0)) for j in range(NB)]
            intra = [_dot(a[j], vs[j], 1, 0) for j in range(NB)]
            upd = [_dot(vs[j], k_e[j], 0, 0) for j in range(NB)]
            for j in range(NB):
                st_ref[gi * NB + j] = st
                o_ref[rows[j], :] = intra[j] + _dot(q_e[j], _bf(st), 1, 1)
                st = st * jnp.tile(ds[j]["total8"], (HEAD // 8, 1)) + upd[j]
            return st

        lax.fori_loop(0, NGRP, group, jnp.zeros((HEAD, HEAD), jnp.float32))
        o = o_ref[...]
        rn = o * lax.rsqrt(jnp.mean(o * o, axis=-1, keepdims=True) + EPS)
        gate = gate_ref[...]
        y_ref[...] = _bf(rn * rg_ref[...] * (gate * _sigmoid(gate)))

    return pl.pallas_call(
        body, name="hgrn_fwd", grid=(NHEAD,),
        in_specs=[*_HEAD_SPECS,
                  pl.BlockSpec((2, HEAD), lambda h: (0, h)),
                  pl.BlockSpec((1, HEAD), lambda h: (0, h)),
                  pl.BlockSpec(memory_space=pl.ANY)],
        out_specs=(pl.BlockSpec((T, HEAD), lambda h: (0, NHEAD + h)),
                   pl.BlockSpec((T, HEAD), lambda h: (0, h)),
                   pl.BlockSpec((None, NCHUNK, HEAD, HEAD), lambda h: (h, 0, 0, 0))),
        out_shape=(pltpu.HBM((T, DMIX), jnp.bfloat16), pltpu.HBM((T, D), jnp.float32),
                   pltpu.HBM((NHEAD, NCHUNK, HEAD, HEAD), jnp.float32)),
        input_output_aliases={6: 0},
        compiler_params=_params(("parallel",)),
    )(proj, proj, proj, proj, lb_logits, rec_g, y_in)


def _out_proj_loss(x, y, w_out, target, gf):
    rows = 512
    parts = [slice(k * rows // 2, (k + 1) * rows // 2) for k in range(2)]

    def body(x_ref, y_ref, w_ref, t_ref, g_ref, dz_ref, dzb_ref, sq_ref, dg_ref):
        zs = [x_ref[p, :] + _dot(y_ref[p, :], w_ref[...], 1, 0) for p in parts]
        sq = dg = 0.0
        for p, z in zip(parts, zs):
            r = lax.rsqrt(jnp.mean(z * z, axis=-1, keepdims=True) + EPS)
            zhat = z * r
            err = zhat * g_ref[...] - t_ref[p, :]
            dy = err * (1.0 / D)
            gdy = dy * g_ref[...]
            dz = r * (gdy - zhat * jnp.mean(zhat * gdy, axis=-1, keepdims=True))
            dz_ref[p, :] = dz
            dzb_ref[p, :] = _bf(dz)
            sq = sq + jnp.sum(err * err, axis=0, keepdims=True)
            dg = dg + jnp.sum(zhat * dy, axis=0, keepdims=True)

        @pl.when(pl.program_id(0) == 0)
        def _():
            sq_ref[...] = sq
            dg_ref[...] = dg

        @pl.when(pl.program_id(0) != 0)
        def _():
            sq_ref[...] += sq
            dg_ref[...] += dg

    tile = pl.BlockSpec((rows, D), lambda i: (i, 0))
    vec = pl.BlockSpec((1, D), lambda i: (0, 0))
    return pl.pallas_call(
        body, name="out_proj_loss", grid=(T // rows,),
        in_specs=[tile, pl.BlockSpec((rows, DMIX), lambda i: (i, 0)), pl.BlockSpec((DMIX, D), lambda i: (0, 0)),
                  tile, vec],
        out_specs=(tile, tile, vec, vec),
        out_shape=(pltpu.HBM((T, D), jnp.float32), pltpu.HBM((T, D), jnp.bfloat16),
                   pltpu.HBM((1, D), jnp.float32), pltpu.HBM((1, D), jnp.float32)),
        compiler_params=_params(("arbitrary",)),
    )(x, y, w_out, target, gf)


def _out_proj_bwd(dzb, w_out, y):
    tn = 512

    def body(dz_ref, w_ref, y_ref, dy_ref, gw_ref, gwb_ref):
        dz = dz_ref[...]
        dy_ref[...] = _dot(dz, w_ref[...], 1, 1)
        gw = _dot(y_ref[...], dz, 0, 0)
        gw_ref[...] = gw
        gwb_ref[...] = _bf(gw)

    return pl.pallas_call(
        body, name="out_proj_bwd", grid=(DMIX // tn,),
        in_specs=[pl.BlockSpec((T, D), lambda n: (0, 0)), pl.BlockSpec((tn, D), lambda n: (n, 0)),
                  pl.BlockSpec((T, tn), lambda n: (0, n))],
        out_specs=(pl.BlockSpec((T, tn), lambda n: (0, n)), pl.BlockSpec((tn, D), lambda n: (n, 0)),
                   pl.BlockSpec((tn, D), lambda n: (n, 0))),
        out_shape=(pltpu.HBM((T, DMIX), jnp.float32), pltpu.HBM((DMIX, D), jnp.float32),
                   pltpu.HBM((DMIX, D), jnp.bfloat16)),
        compiler_params=_params(("parallel",)),
    )(dzb, w_out, y)


def _hgrn_bwd(proj, lb_logits, rec_g, o, states, dymix, dproj_in, token):
    def body(q_ref, f_ref, i_ref, gate_ref, lb_ref, rg_ref, o_ref, st_ref, dy_ref, dp_any, token_any,
             dp_ref, drg_ref, dlb_ref, do_ref):
        del dp_any, token_any
        lb = _lower_bound(lb_ref)
        causal = _tri(True)
        smat, rmat = _sum_rows_matrix(), _rev_sum_matrix()

        o = o_ref[...]
        rs = lax.rsqrt(jnp.mean(o * o, axis=-1, keepdims=True) + EPS)
        rn = o * rs
        gate = gate_ref[...]
        sgate = _sigmoid(gate)
        dyv = dy_ref[...]
        d_r = dyv * (gate * sgate)
        dp_ref[3] = _bf(dyv * (rn * rg_ref[...]) * (sgate * (1.0 + gate * (1.0 - sgate))))
        drg_ref[...] = jnp.sum(d_r * rn, axis=0, keepdims=True)
        drn = d_r * rg_ref[...]
        do_ref[...] = rs * (drn - rn * jnp.mean(rn * drn, axis=-1, keepdims=True))

        def group(i, carry):
            dst, dlb = carry
            gi = NGRP - 1 - i
            rows = _group_rows(gi)
            span = range(NB)
            qvs = [q_ref[r, :] for r in rows]
            ts = [_gates(qv, f_ref[r, :], lb) for qv, r in zip(qvs, rows)]
            ds = [_decays(s) for s in _exact_sums(smat, [t["g"] for t in ts])]
            vs = [_bf(i_ref[r, :]) for r in rows]
            dos = [_bf(do_ref[r, :]) for r in rows]
            sts = [st_ref[gi * NB + j] for j in span]
            qe_f = [t["qs"] * d["e_q"] for t, d in zip(ts, ds)]
            ke_f = [t["kk"] * d["e_k"] for t, d in zip(ts, ds)]
            q_e, k_e = [_bf(a) for a in qe_f], [_bf(a) for a in ke_f]
            q_m = [_bf(t["qs"] * d["e_qm"]) for t, d in zip(ts, ds)]
            k_m = [_bf(t["kk"] * d["e_km"]) for t, d in zip(ts, ds)]
            a = [_bf(jnp.where(causal, _dot(q_m[j], k_m[j], 1, 1), 0.0)) for j in span]
            da = [_bf(jnp.where(causal, _dot(dos[j], vs[j], 1, 1), 0.0)) for j in span]
            dqm = [_dot(da[j], k_m[j], 1, 0) for j in span]
            dkm = [_dot(da[j], q_m[j], 0, 0) for j in span]
            dv_in = [_dot(a[j], dos[j], 0, 0) for j in span]
            dqe = [_dot(dos[j], _bf(sts[j]), 1, 0) for j in span]
            grow = [_dot(dos[j], q_e[j], 0, 0) for j in span]
            dke, carried = [None] * NB, [None] * NB
            for j in reversed(span):
                dst_b = _bf(dst)
                dke[j] = _dot(vs[j], dst_b, 1, 0)
                dp_ref[2, rows[j], :] = _bf(dv_in[j] + _dot(k_e[j], dst_b, 1, 1))
                carried[j] = ds[j]["total8"] * jnp.sum(dst * sts[j], axis=0, keepdims=True)
                dst = dst * jnp.tile(ds[j]["total8"], (HEAD // 8, 1)) + grow[j]
            kdk = [ke_f[j] * dke[j] for j in span]
            pos = [(q_m[j].astype(jnp.float32) * dqm[j] - k_m[j].astype(jnp.float32) * dkm[j]) + qe_f[j] * dqe[j]
                   for j in span]
            dgs = _exact_sums(rmat, [jnp.concatenate([pos[j], kdk[j]], axis=0) for j in span])
            for j in span:
                t, d = ts[j], ds[j]
                dg = dgs[j] + jnp.tile(carried[j], (CHUNK // 8, 1))
                dqs = dqm[j] * d["e_qm"] + dqe[j] * d["e_q"]
                dkk = dkm[j] * d["e_km"] + dke[j] * d["e_k"]
                df = dg / t["f"] - dkk
                dp_ref[1, rows[j], :] = _bf(df * (1.0 - lb) * (t["sg"] * (1.0 - t["sg"])))
                dp_ref[0, rows[j], :] = _bf(dqs * (t["sq"] * (1.0 + qvs[j] * (1.0 - t["sq"]))))
                dlb = dlb + df * (1.0 - t["sg"])
            return dst, dlb

        _, dlb = lax.fori_loop(0, NGRP, group, (jnp.zeros((HEAD, HEAD), jnp.float32),
                                                jnp.zeros((CHUNK, HEAD), jnp.float32)))
        dlb_ref[...] = jnp.sum(dlb, axis=0, keepdims=True)

    vec = pl.BlockSpec((1, HEAD), lambda h: (0, h))
    return pl.pallas_call(
        body, name="hgrn_bwd", grid=(NHEAD,),
        in_specs=[*_HEAD_SPECS,
                  pl.BlockSpec((2, HEAD), lambda h: (0, h)), vec,
                  pl.BlockSpec((T, HEAD), lambda h: (0, h)),
                  pl.BlockSpec((None, NCHUNK, HEAD, HEAD), lambda h: (h, 0, 0, 0)),
                  pl.BlockSpec((T, HEAD), lambda h: (0, NHEAD + h)), ANY, ANY],
        out_specs=(pl.BlockSpec((4, T, HEAD), lambda h: (0, 0, h)), vec, vec),
        out_shape=(pltpu.HBM((NSEG, T, D), jnp.bfloat16),
                   pltpu.HBM((1, D), jnp.float32), pltpu.HBM((1, D), jnp.float32)),
        scratch_shapes=[pltpu.VMEM((T, HEAD), jnp.float32)],
        input_output_aliases={9: 0},
        compiler_params=_params(("parallel",)),
    )(proj, proj, proj, proj, lb_logits, rec_g, o, states, dymix, dproj_in, token)


def _pool_bwd(proj, pool_w, pool_scale, dymix):
    def body(u_ref, pg_ref, w_ref, sc_ref, dy_ref, dp_ref, gw_ref, gs_ref):
        gidx = pl.program_id(0)
        u, pg = u_ref[...], pg_ref[...]
        w = w_ref[...].reshape(GROUP, GROUP)
        d = _bf(_window_mean(_window_sum(u, gidx, False), gidx) - u)
        mixed = _dot(d, w, 1, 0)
        spg = _sigmoid(pg)
        dyv = dy_ref[...]
        d_p = dyv * (pg * spg)
        dp_ref[1] = _bf(dyv * (mixed * sc_ref[...]) * (spg * (1.0 + pg * (1.0 - spg))))
        gs_ref[...] = jnp.sum(d_p * mixed, axis=0, keepdims=True)
        dmixed = _bf(d_p * sc_ref[...])
        gw_ref[...] = _dot(d, dmixed, 0, 0).reshape(gw_ref.shape)
        dd = _dot(dmixed, w, 1, 1)
        dp_ref[0] = _bf(_window_sum(_window_mean(dd, gidx), gidx, True) - dd)

    return pl.pallas_call(
        body, name="pool_bwd", grid=(NGROUP,),
        in_specs=[*_POOL_SPECS, _POOL_W_SPEC,
                  pl.BlockSpec((1, GROUP), lambda g: (0, g)),
                  pl.BlockSpec((T, GROUP), lambda g: (0, g))],
        out_specs=(pl.BlockSpec((2, T, GROUP), lambda g: (2, 0, g)), _POOL_W_SPEC,
                   pl.BlockSpec((1, GROUP), lambda g: (0, g))),
        out_shape=(pltpu.HBM((NSEG, T, D), jnp.bfloat16),
                   pltpu.HBM((NDEV, NGROUP, GROUP // NDEV, GROUP), jnp.float32),
                   pltpu.HBM((1, D), jnp.float32)),
        compiler_params=_params(("parallel",)),
    )(proj, proj, pool_w, pool_scale, dymix)


HALF = NTILE // 2
AWAY = HALF - 3


def _dproj_tile(chip, side, p):
    j = 6 * chip + 3 * side + p
    return ((j // 4 + 4) % NSEG, 0, j % 4)


def _sibling_copy(sib_out, sib_in, send_sems, recv_sems, slot):
    x, y, c, _ = _place()
    return pltpu.make_async_remote_copy(
        src_ref=sib_out.at[slot], dst_ref=sib_in.at[slot], send_sem=send_sems.at[slot],
        recv_sem=recv_sems.at[slot], device_id=(x, y, 1 - c), device_id_type=MESH)


def _proj_bwd_w_far(place, ht, dproj):
    def body(place_ref, h_ref, dpa_ref, dpb_ref, sib_out, sib_in, send_sems, recv_sems, stage, loc_sems):
        del place_ref
        i = pl.program_id(0)

        def to_hbm(k):
            return pltpu.make_async_copy(stage.at[k], sib_out.at[k], loc_sems.at[k])

        def send(k):
            to_hbm(k).wait()
            _sibling_copy(sib_out, sib_in, send_sems, recv_sems, k).start()

        stage[2 * i] = _bf(_dot(h_ref[...], dpa_ref[...], 1, 0))
        stage[2 * i + 1] = _bf(_dot(h_ref[...], dpb_ref[...], 1, 0))

        @pl.when(i > 0)
        def _():
            send(2 * i - 2)
            send(2 * i - 1)

        to_hbm(2 * i).start()
        to_hbm(2 * i + 1).start()

        @pl.when(i == HALF // 2 - 1)
        def _():
            send(2 * i)
            send(2 * i + 1)

    def tile(k, pr):
        return _dproj_tile(k // 3, 1 - pr[2], k % 3)

    buf = pltpu.HBM((HALF, D, TILE), jnp.bfloat16)
    sems = pltpu.SemaphoreType.DMA((HALF,))
    return pl.pallas_call(
        body, name="proj_bwd_w_far",
        grid_spec=pltpu.PrefetchScalarGridSpec(
            num_scalar_prefetch=1, grid=(HALF // 2,),
            in_specs=[pl.BlockSpec((D, T), lambda i, pr: (0, 0)),
                      pl.BlockSpec((None, T, TILE), lambda i, pr: tile(2 * i, pr)),
                      pl.BlockSpec((None, T, TILE), lambda i, pr: tile(2 * i + 1, pr))],
            out_specs=(HBM, HBM, SEM, SEM),
            scratch_shapes=[pltpu.VMEM((HALF, D, TILE), jnp.bfloat16), pltpu.SemaphoreType.DMA((HALF,))]),
        out_shape=(buf, buf, sems, sems),
        compiler_params=pltpu.CompilerParams(dimension_semantics=("arbitrary",), vmem_limit_bytes=48 * MIB,
                                             has_side_effects=EFFECT),
    )(place, ht, dproj, dproj)


def _proj_bwd_w_near(place, ht, dproj, sib_out, sib_in, sib_send, sib_recv):
    def owner_chip(k, pr):
        return jnp.where(k < AWAY, (pr[1] + 1 + k % 3) % 4, pr[1])

    def tile_p(k):
        return jnp.where(k < AWAY, k // 3, k - AWAY)

    def body(place_ref, h_ref, dp_ref, sib_out, sib_in, sib_send, sib_recv, sums, own_ref, landing, out_send,
             out_recv, recvbuf, outbuf, in_sems, loc_sems):
        i = pl.program_id(0)
        px, py, c, _ = _place()

        def slot_of(k):
            return 3 * owner_chip(k, place_ref) + tile_p(k)

        def load(k):
            return pltpu.make_async_copy(sib_in.at[slot_of(k)], recvbuf.at[k % 2], in_sems.at[k % 2])

        def fetch(k):
            _sibling_copy(sib_out, sib_in, sib_send, sib_recv, slot_of(k)).wait_recv()
            load(k).start()

        def route(k):
            chip = owner_chip(k, place_ref)
            cx, cy = chip // 2, chip % 2
            return cx, cy, (cx ^ px) + 2 * (cy ^ py) - 1, tile_p(k)

        def to_hbm(k):
            _, _, rel, p = route(k)
            return pltpu.make_async_copy(outbuf.at[k], sums.at[rel, p], loc_sems.at[k])

        def to_owner(k):
            cx, cy, rel, p = route(k)
            return pltpu.make_async_remote_copy(
                src_ref=sums.at[rel, p], dst_ref=landing.at[rel, p], send_sem=out_send.at[3 * rel + p],
                recv_sem=out_recv.at[3 * rel + p], device_id=(cx, cy, c), device_id_type=MESH)

        @pl.when(i == 0)
        def _():
            fetch(i)

        @pl.when(i < HALF - 1)
        def _():
            fetch(i + 1)

        load(i).wait()
        total = _dot(h_ref[...], dp_ref[...], 1, 0) + recvbuf[i % 2].astype(jnp.float32)
        own_ref[...] = total
        outbuf[jnp.minimum(i, AWAY)] = _bf(total)

        @pl.when(i < AWAY)
        def _():
            to_hbm(i).start()

        @pl.when(jnp.logical_and(i > 0, i <= AWAY))
        def _():
            to_hbm(i - 1).wait()
            to_owner(i - 1).start()

        @pl.when(i == HALF - 1)
        def _():
            for slot in range(HALF):
                _sibling_copy(sib_out, sib_in, sib_send, sib_recv, slot).wait_send()

    travelling = pltpu.HBM((3, 3, D, TILE), jnp.bfloat16)
    sems = pltpu.SemaphoreType.DMA((AWAY,))
    return pl.pallas_call(
        body, name="proj_bwd_w_near",
        grid_spec=pltpu.PrefetchScalarGridSpec(
            num_scalar_prefetch=1, grid=(HALF,),
            in_specs=[pl.BlockSpec((D, T), lambda i, pr: (0, 0)),
                      pl.BlockSpec((None, T, TILE), lambda i, pr: _dproj_tile(owner_chip(i, pr), pr[2], tile_p(i))),
                      HBM, HBM, SEM, SEM],
            out_specs=(HBM, pl.BlockSpec((None, D, TILE), lambda i, pr: (jnp.where(i < AWAY, 0, i % 3), 0, 0)),
                       HBM, SEM, SEM),
            scratch_shapes=[pltpu.VMEM((2, D, TILE), jnp.bfloat16), pltpu.VMEM((AWAY + 1, D, TILE), jnp.bfloat16),
                            pltpu.SemaphoreType.DMA((2,)), pltpu.SemaphoreType.DMA((AWAY,))]),
        out_shape=(travelling, pltpu.HBM((3, D, TILE), jnp.float32), travelling, sems, sems),
        compiler_params=pltpu.CompilerParams(dimension_semantics=("arbitrary",), vmem_limit_bytes=48 * MIB,
                                             has_side_effects=EFFECT),
    )(place, ht, dproj, sib_out, sib_in, sib_send, sib_recv)


def _proj_bwd_x(dproj, w_t, x, g1, dz, others, token):
    tm = 512
    pairs = NSEG // 2
    k = len(others)

    def body(dp_ref, w_ref, x_any, g_ref, dz_any, *refs):
        dx_ref, dg_ref, wcat, acc, xbuf, dzbuf, row_sems = refs[k + 1:]
        m, s = pl.program_id(0), pl.program_id(1)
        mine = pl.ds(pl.multiple_of(m * tm, tm), tm)
        fetches = [pltpu.make_async_copy(x_any.at[mine], xbuf, row_sems.at[0]),
                   pltpu.make_async_copy(dz_any.at[mine], dzbuf, row_sems.at[1])]

        @pl.when(m == 0)
        def _():
            for i in range(8):
                wcat[s, :, i * TILE:(i + 1) * TILE] = w_ref[i]

        @pl.when(s == 0)
        def _():
            for cp in fetches:
                cp.start()
            acc[...] = jnp.zeros((tm, D), jnp.float32)

        acc[...] += _dot(jnp.concatenate([dp_ref[0], dp_ref[1]], axis=1), wcat[s], 1, 1)

        @pl.when(s == pairs - 1)
        def _():
            for cp in fetches:
                cp.wait()
            xv = xbuf[...]
            rs = lax.rsqrt(jnp.mean(xv * xv, axis=-1, keepdims=True) + EPS)
            xhat = xv * rs
            dhv = acc[...]
            gdh = dhv * g_ref[...]
            dx_ref[...] = dzbuf[...] + rs * (gdh - xhat * jnp.mean(xhat * gdh, axis=-1, keepdims=True))
            dg = jnp.sum(xhat * dhv, axis=0, keepdims=True)

            @pl.when(m == 0)
            def _():
                dg_ref[0:1, :] = dg
                for r, ref in enumerate(refs[:k]):
                    dg_ref[1 + r:2 + r, :] = ref[...]
                dg_ref[1 + k:, :] = jnp.zeros((7 - k, D), jnp.float32)

            @pl.when(m != 0)
            def _():
                dg_ref[0:1, :] += dg

    rows = pl.BlockSpec((tm, D), lambda m, s: (m, 0))
    vec = pl.BlockSpec((1, D), lambda m, s: (0, 0))
    return pl.pallas_call(
        body, name="proj_bwd_x", grid=(T // tm, pairs),
        in_specs=[pl.BlockSpec((2, tm, D), lambda m, s: (s, m, 0)),
                  pl.BlockSpec((8, D, TILE), lambda m, s: ((jnp.where(m == 0, s, pairs - 1) + 1) % pairs, 0, 0)),
                  HBM, vec, HBM, *[vec] * k, ANY],
        out_specs=(rows, pl.BlockSpec((8, D), lambda m, s: (0, 0))),
        out_shape=(jax.ShapeDtypeStruct((T, D), jnp.float32), pltpu.HBM((8, D), jnp.float32)),
        scratch_shapes=[pltpu.VMEM((pairs, D, 2 * D), jnp.bfloat16), pltpu.VMEM((tm, D), jnp.float32),
                        pltpu.VMEM((tm, D), jnp.float32), pltpu.VMEM((tm, D), jnp.float32),
                        pltpu.SemaphoreType.DMA((2,))],
        compiler_params=_params(("arbitrary", "arbitrary"), vmem_mib=56),
    )(dproj, w_t, _in_hbm(x), g1, _in_hbm(dz), *[_in_hbm(a) for a in others], token)


def _adamw(w, g, m, v):
    m_new = ADAM_B1 * m + (1.0 - ADAM_B1) * g
    v_new = ADAM_B2 * v + (1.0 - ADAM_B2) * (g * g)
    delta = -ADAM_LR * ((m_new / BC1) / (jnp.sqrt(v_new / BC2) + ADAM_EPS) + ADAM_WD * w)
    return delta, m_new, v_new


def _reduce_adam(name, place, parts, w, m, v, grid, w_spec):
    n = len(parts)

    def body(place_ref, *refs):
        del place_ref
        w_ref, m_ref, v_ref, g_ref, d_ref, mo_ref, vo_ref = refs[n:]
        g = None
        for ref, (_, _, stacked) in zip(refs[:n], parts):
            terms = [ref[r] for r in range(ref.shape[0])] if stacked else [ref[...]]
            for t in terms:
                if t.shape[-1] != w_ref.shape[-1]:
                    t = jnp.concatenate([t[p] for p in range(t.shape[0])], axis=1)
                g = t.astype(jnp.float32) if g is None else g + t.astype(jnp.float32)
        delta, m_new, v_new = _adamw(w_ref[...], g, m_ref[...], v_ref[...])
        g_ref[...] = g
        d_ref[...] = delta
        mo_ref[...] = m_new
        vo_ref[...] = v_new

    shape = jax.ShapeDtypeStruct(w.shape, jnp.float32)
    return pl.pallas_call(
        body, name=name,
        grid_spec=pltpu.PrefetchScalarGridSpec(
            num_scalar_prefetch=1, grid=grid,
            in_specs=[spec for _, spec, _ in parts] + [w_spec] * 3, out_specs=(w_spec,) * 4),
        out_shape=(shape,) * 4,
        compiler_params=_params(("parallel",)),
    )(place, *[_in_hbm(a) for a in [a for a, _, _ in parts] + [w, m, v]])


SMALL_ROWS = (1, 1, 2, 1, 1)


def _small_adam(place, own, parts, ws, ms, vs):
    n = len(SMALL_ROWS)

    def body(place_ref, own_ref, p_ref, *refs):
        ins, outs, bufs = refs[:3 * n], refs[3 * n:3 * n + 1 + 4 * n], refs[3 * n + 1 + 4 * n:]

        def stacked(group, buf):
            r0 = 0
            for ref, k in zip(group, SMALL_ROWS):
                buf[r0:r0 + k, :] = ref[...]
                r0 += k
            buf[r0:, :] = jnp.zeros((8 - r0, D), jnp.float32)
            return buf[...]

        wv, mv, vv = (stacked(ins[n * j:n * j + n], bufs[j]) for j in range(3))
        me = place_ref[0]
        g = None
        for s in range(NDEV):
            term = jnp.where(me == s, own_ref[...], p_ref[s])
            g = term if g is None else g + term
        rows = _row_ids(wv.shape)
        other = jnp.where(rows == 2, pltpu.roll(wv, 7, 0), jnp.where(rows == 3, pltpu.roll(wv, 1, 0), 0.0))
        lbv = _sigmoid(wv - other)
        sign = jnp.where(rows == 2, 1.0, -1.0)
        g = jnp.where((rows == 2) | (rows == 3), sign * g * lbv * (1.0 - lbv), g)
        delta, m_new, v_new = _adamw(wv, g, mv, vv)
        outs[0][...] = jnp.sum(g[6:7], axis=1, keepdims=True) * (0.5 / D)
        for j, val in enumerate((g, delta, m_new, v_new)):
            r0 = 0
            for ref, k in zip(outs[1 + n * j:1 + n * j + n], SMALL_ROWS):
                ref[...] = val[r0:r0 + k]
                r0 += k

    vmem = pl.BlockSpec(memory_space=pltpu.VMEM)
    shapes = [jax.ShapeDtypeStruct((k, D), jnp.float32) for k in SMALL_ROWS]
    out = pl.pallas_call(
        body, name="small_adam", out_shape=(jax.ShapeDtypeStruct((1, 1), jnp.float32), *shapes * 4),
        in_specs=[pl.BlockSpec(memory_space=pltpu.SMEM)] + [vmem] * (2 + 3 * n), out_specs=(vmem,) * (1 + 4 * n),
        scratch_shapes=[pltpu.VMEM((8, D), jnp.float32)] * 3,
    )(place, own, parts, *ws, *ms, *vs)
    return out[0], [out[1 + n * j:1 + n * j + n] for j in range(4)]


def kernel(x, norm1_g, w_in, pool_w, pool_scale, lb_logits, rec_norm_g, w_out, final_norm_g, loss_target, m_norm1_g, m_w_in, m_pool_w, m_pool_scale, m_lb_logits, m_rec_norm_g, m_w_out, m_final_norm_g, v_norm1_g, v_w_in, v_pool_w, v_pool_scale, v_lb_logits, v_rec_norm_g, v_w_out, v_final_norm_g):
    xs = x[0]
    target = loss_target[0]
    ix, iy, ic = lax.axis_index("x"), lax.axis_index("y"), lax.axis_index("c")
    place = jnp.stack([4 * ix + 2 * iy + ic, 2 * ix + iy, ic]).astype(jnp.int32)
    gf = final_norm_g.reshape(1, D)

    ht, w_t, w_out_b, w_out_g, pool_g, proj = _gather_proj(xs, norm1_g, w_in, w_out, pool_w)
    wout = [w_out_b, w_out_g]
    wout_send, wout_recv, wout, wout_token = _split_start("gather_wout_start", wout, NDEV - 1, _plan_wout)

    y = _pool_fwd(proj, pool_g, pool_scale, wout_token)
    y, o, states = _hgrn_fwd(proj, lb_logits, rec_norm_g, y)
    _, w_out_g = _split_wait("gather_wout_wait", wout, wout_send, wout_recv, _plan_wout, o)
    w_out_full = _in_hbm(w_out_g.reshape(DMIX, D))
    dz, dzb, sq, dgf = _out_proj_loss(xs, y, w_out_full, target, gf)

    dymix, gwout_f, gwout_b = _out_proj_bwd(dzb, w_out_full, y)
    dproj, gpool, dscale = _pool_bwd(proj, pool_g, pool_scale, dymix)

    blk_out = (NDEV, DMIX // NDEV, D)
    blk_pool = (NDEV, NGROUP, GROUP // NDEV, GROUP)
    rest = [gwout_b.reshape(blk_out), gpool,
            lax.empty((NDEV - 1,) + blk_out[1:], jnp.bfloat16), lax.empty((NDEV - 1,) + blk_pool[1:], jnp.float32)]
    rest_send, rest_recv, rest, rest_token = _split_start("scatter_rest_start", rest, 2 * (NDEV - 1), _plan_rest)

    dproj, drecg, dlb = _hgrn_bwd(proj, lb_logits, rec_norm_g, o, states, dymix, dproj, rest_token)
    chip_sums, own_sum, landing, win_send, win_recv = _proj_bwd_w_near(
        place, ht, dproj, *_proj_bwd_w_far(place, ht, dproj))
    win = [chip_sums, landing]

    others = [a.reshape(1, D) for a in (dscale, dlb, dlb, drecg, dgf, sq)]
    grad_x, small_block = _proj_bwd_x(dproj, w_t, xs, norm1_g, dz, others, chip_sums)

    small = [small_block, lax.empty((NDEV, 8, D), jnp.float32)]
    small_send, small_recv, small, small_token = _split_start("gather_small_start", small, NDEV - 1, _plan_small)

    _, gpool_own, r_out, r_pool = _split_wait("scatter_rest_wait", rest, rest_send, rest_recv, _plan_rest,
                                              small_token)
    g_wout, d_wout, m_wout, v_wout = _reduce_adam(
        "adam_w_out", place,
        [(gwout_f.reshape(blk_out), pl.BlockSpec((None,) + blk_out[1:], lambda i, pr: (pr[0], 0, 0)), False),
         (r_out, pl.BlockSpec((NDEV - 1,) + blk_out[1:], lambda i, pr: (0, 0, 0)), True)],
        w_out, m_w_out, v_w_out, (1,), pl.BlockSpec((None,) + blk_out[1:], lambda i, pr: (0, 0, 0)))
    g_pool, d_pool, m_pool, v_pool = _reduce_adam(
        "adam_pool_w", place,
        [(gpool_own, pl.BlockSpec((None,) + blk_pool[1:], lambda i, pr: (pr[0], 0, 0, 0)), False),
         (r_pool, pl.BlockSpec((NDEV - 1,) + blk_pool[1:], lambda i, pr: (0, 0, 0, 0)), True)],
        pool_w, m_pool_w, v_pool_w, (1,), pl.BlockSpec((None,) + blk_pool[1:], lambda i, pr: (0, 0, 0, 0)))

    _, r_in = _split_wait("scatter_win_wait", win, win_send, win_recv, _plan_in, d_pool)
    g_win, d_win, m_win, v_win = _reduce_adam(
        "adam_w_in", place,
        [(own_sum, pl.BlockSpec((3, D // 8, TILE), lambda i, pr: (0, i, 0)), False),
         (r_in, pl.BlockSpec((3, 3, D // 8, TILE), lambda i, pr: (0, 0, i, 0)), True)],
        w_in, m_w_in, v_w_in, (8,), pl.BlockSpec((None, D // 8, 3 * TILE), lambda i, pr: (0, i, 0)))

    own_small, r_small = _split_wait("gather_small_wait", small, small_send, small_recv, _plan_small, d_win)
    loss, (g_s, d_s, m_s, v_s) = _small_adam(
        place, own_small, r_small,
        (norm1_g, pool_scale, lb_logits, rec_norm_g, gf),
        (m_norm1_g, m_pool_scale, m_lb_logits, m_rec_norm_g, m_final_norm_g.reshape(1, D)),
        (v_norm1_g, v_pool_scale, v_lb_logits, v_rec_norm_g, v_final_norm_g.reshape(1, D)))

    def outs(small, win, pool, wout):
        n1, ps, lbl, rg, fg = small
        return n1, win, pool, ps, lbl, rg, wout, fg.reshape(D)

    return (loss.reshape(()), grad_x[None],
            *outs(g_s, g_win, g_pool, g_wout), *outs(d_s, d_win, d_pool, d_wout),
            *outs(m_s, m_win, m_pool, m_wout), *outs(v_s, v_win, v_pool, v_wout))
```

```python
import jax
import jax.numpy as jnp
from jax import lax
from jax.experimental import pallas as pl
from jax.experimental.pallas import tpu as pltpu

T = 2048
D = 1024
NSEG = 6
NTILE = 24
TILE = 256
DMIX = 2048
NDEV = 8
HEAD = 128
NHEAD = 8
CHUNK = 64
NCHUNK = T // CHUNK
NB = 32
NGRP = NCHUNK // NB
NGROUP = 4
GROUP = 256
EPS = 1e-6
EXP_CAP = 115.0
MESH = pl.DeviceIdType.MESH
ANY = pl.BlockSpec(memory_space=pl.ANY)
HBM = pl.BlockSpec(memory_space=pltpu.HBM)
SEM = pl.BlockSpec(memory_space=pltpu.SEMAPHORE)
EFFECT = pltpu.SideEffectType.DATAFLOW_SIDE_EFFECTING

ADAM_LR = 0.001
ADAM_B1 = 0.9
ADAM_B2 = 0.999
ADAM_EPS = 1e-08
ADAM_WD = 0.01
ADAM_STEP = 10
BC1 = 1.0 - ADAM_B1 ** ADAM_STEP
BC2 = 1.0 - ADAM_B2 ** ADAM_STEP

MIB = 1 << 20


def _params(sem=None, vmem_mib=48):
    return pltpu.CompilerParams(dimension_semantics=sem, vmem_limit_bytes=vmem_mib * MIB)


def _sigmoid(v):
    return 1.0 / (1.0 + jnp.exp(-v))


def _dot(a, b, ca, cb, precision=None):
    return lax.dot_general(a, b, (((ca,), (cb,)), ((), ())), precision=precision,
                           preferred_element_type=jnp.float32)


def _bf(v):
    return v.astype(jnp.bfloat16)


def _in_hbm(a):
    return pltpu.with_memory_space_constraint(a, pltpu.HBM)


def _place():
    x, y, c = lax.axis_index("x"), lax.axis_index("y"), lax.axis_index("c")
    return x, y, c, 4 * x + 2 * y + c


def _peer(x, y, c, r):
    return (x ^ ((r >> 2) & 1), y ^ ((r >> 1) & 1), c ^ (r & 1))


def _gather_proj(x, g1, w_in, w_out, pool_w):
    def body(x_ref, g_ref, win_ref, wout_ref, pool_ref, ht_o, wt_o, woutb_o, wout_o, pool_o, proj_o,
             xbuf, hv, htv, wv, wob, pb, stage, send_sems, recv_sems, loc_sems, out_sems):
        px, py, c, my_idx = _place()
        fetch_x = pltpu.make_async_copy(x_ref, xbuf, loc_sems.at[5])
        fetch_x.start()
        me, sibling = (px, py, c), (px, py, 1 - c)
        chips = [(1 - px, py), (px, 1 - py), (1 - px, 1 - py)]
        for p in range(3):
            wv[3 * my_idx + p] = _bf(win_ref[0, :, p * TILE:(p + 1) * TILE])

        def index(bx, by, bc):
            return 4 * bx + 2 * by + bc

        def slot(w, block):
            return wv.at[pl.ds(3 * index(*block), 3)] if w == 0 else pool_o.at[index(*block)]

        def copy(k, w, block, to, src=None):
            return pltpu.make_async_remote_copy(
                src_ref=slot(w, block) if src is None else src, dst_ref=slot(w, block),
                send_sem=send_sems.at[2 * k + w], recv_sem=recv_sems.at[2 * k + w],
                device_id=to, device_id_type=MESH)

        def save(block):
            at = pl.ds(3 * index(*block), 3)
            pltpu.make_async_copy(wv.at[at], wt_o.at[at], loc_sems.at[4]).start()

        srcs = (slot(0, me), pb)
        first = []
        for w in (0, 1):
            if w == 1:
                pb[...] = _bf(pool_ref[0])
                wob[...] = _bf(wout_ref[0])
            group = [copy(1 + j, w, me, (*chip, c), src=srcs[w]) for j, chip in enumerate(chips[:2])]
            group.append(copy(0, w, me, sibling, src=srcs[w]))
            for cp in group:
                cp.start()
            first += group
        save(me)
        locs = [pltpu.make_async_copy(pb, slot(1, me), loc_sems.at[0]),
                pltpu.make_async_copy(wob, wout_o.at[my_idx], loc_sems.at[1]),
                pltpu.make_async_copy(wob, woutb_o, loc_sems.at[2])]
        for cp in locs:
            cp.start()

        fetch_x.wait()
        xv = xbuf[...]
        hv[...] = _bf(xv * lax.rsqrt(jnp.mean(xv * xv, axis=-1, keepdims=True) + EPS) * g_ref[...])
        rows = 256
        for r0 in range(0, T, rows):
            htv[:, r0:r0 + rows] = hv[r0:r0 + rows, :].T
        locs.append(pltpu.make_async_copy(htv, ht_o, loc_sems.at[3]))
        locs[-1].start()

        def out_copy(p, j):
            return pltpu.make_async_copy(stage.at[p], proj_o.at[j], out_sems.at[p])

        def project(nth, block):
            base = 3 * index(*block)

            def tile(p, carry):
                if nth > 0:
                    out_copy(p, base + p).wait()
                stage[p] = _dot(hv[...], wv[base + p], 1, 0)
                out_copy(p, base + p).start()
                return carry

            lax.fori_loop(0, 3, tile, 0)

        project(0, me)
        copy(0, 0, sibling, me).wait_recv()
        save(sibling)
        project(1, sibling)
        passed = []
        relay_from = (px ^ (1 - c), py ^ c, c)
        relay_to = (px ^ c, py ^ (1 - c), c)

        def arrived(w, j):
            copy(1 + j, w, (*chips[j], c), me).wait_recv()
            passed.append(copy(4 + j, w, (*chips[j], c), sibling))
            passed[-1].start()

        def relay(w):
            passed.append(copy(3, w, relay_from, relay_to))
            passed[-1].start()

        def handed(nth, j):
            copy(4 + j, 0, (*chips[j], 1 - c), me).wait_recv()
            save((*chips[j], 1 - c))
            project(nth, (*chips[j], 1 - c))

        arrived(0, 0)
        arrived(0, 1)
        relay(0)
        for j in range(2):
            save((*chips[j], c))
            project(2 + j, (*chips[j], c))
        handed(4, 0)
        handed(5, 1)
        arrived(1, 0)
        arrived(1, 1)
        relay(1)
        arrived(0, 2)
        save((*chips[2], c))
        project(6, (*chips[2], c))
        handed(7, 2)
        arrived(1, 2)
        copy(0, 1, sibling, me).wait_recv()
        for j, chip in enumerate(chips):
            copy(4 + j, 1, (*chip, 1 - c), me).wait_recv()
        keep = pltpu.make_async_copy(wv, wt_o, loc_sems.at[4])
        for p in range(3):
            out_copy(p, p).wait()
        for cp in first + passed:
            cp.wait_send()
        keep.wait()
        for cp in locs:
            cp.wait()

    vmem = pl.BlockSpec(memory_space=pltpu.VMEM)
    bf16 = jnp.bfloat16
    return pl.pallas_call(
        body, name="gather_proj",
        out_shape=(pltpu.HBM((D, T), bf16), pltpu.HBM((NTILE, D, TILE), bf16),
                   pltpu.HBM((DMIX // NDEV, D), bf16), pltpu.HBM((NDEV, DMIX // NDEV, D), bf16),
                   pltpu.HBM((NDEV, NGROUP, GROUP // NDEV, GROUP), bf16), pltpu.HBM((NTILE, T, TILE), jnp.float32)),
        in_specs=[ANY] + [vmem] * 4, out_specs=(ANY,) * 6,
        scratch_shapes=[pltpu.VMEM((T, D), jnp.float32),
                        pltpu.VMEM((T, D), bf16), pltpu.VMEM((D, T), bf16), pltpu.VMEM((NTILE, D, TILE), bf16),
                        pltpu.VMEM((DMIX // NDEV, D), bf16), pltpu.VMEM((NGROUP, GROUP // NDEV, GROUP), bf16),
                        pltpu.VMEM((3, T, TILE), jnp.float32),
                        pltpu.SemaphoreType.DMA((14,)), pltpu.SemaphoreType.DMA((14,)),
                        pltpu.SemaphoreType.DMA((6,)), pltpu.SemaphoreType.DMA((3,))],
        compiler_params=_params(vmem_mib=56),
    )(x, g1, w_in, w_out, pool_w)


def _split_start(name, arrays, n_copies, plan):
    k = len(arrays)

    def body(*refs):
        send_sems, recv_sems, token = refs[k], refs[k + 1], refs[-1]
        for i, (src, dst, to) in enumerate(plan(refs[:k])):
            pltpu.make_async_remote_copy(src_ref=src, dst_ref=dst, send_sem=send_sems.at[i],
                                         recv_sem=recv_sems.at[i], device_id=to, device_id_type=MESH).start()
        token[...] = jnp.zeros_like(token)

    out = pl.pallas_call(
        body, name=name,
        out_shape=(pltpu.SemaphoreType.DMA((n_copies,)), pltpu.SemaphoreType.DMA((n_copies,)),
                   *[pltpu.HBM(a.shape, a.dtype) for a in arrays], jax.ShapeDtypeStruct((8, 128), jnp.float32)),
        in_specs=[HBM] * k, out_specs=(SEM, SEM, *[HBM] * k, pl.BlockSpec(memory_space=pltpu.VMEM)),
        input_output_aliases={i: 2 + i for i in range(k)},
        compiler_params=pltpu.CompilerParams(has_side_effects=EFFECT),
    )(*[pltpu.with_memory_space_constraint(a, pltpu.HBM) for a in arrays])
    return out[0], out[1], out[2:2 + k], out[-1]


def _split_wait(name, arrays, send_sems, recv_sems, plan, after):
    k = len(arrays)

    def body(*refs):
        sends, recvs = refs[k], refs[k + 1]
        for i, (src, dst, to) in enumerate(plan(refs[:k])):
            cp = pltpu.make_async_remote_copy(src_ref=src, dst_ref=dst, send_sem=sends.at[i], recv_sem=recvs.at[i],
                                              device_id=to, device_id_type=MESH)
            cp.wait_send()
            cp.wait_recv()

    return pl.pallas_call(
        body, name=name,
        out_shape=tuple(pltpu.HBM(a.shape, a.dtype) for a in arrays),
        in_specs=[HBM] * k + [SEM, SEM, ANY], out_specs=(HBM,) * k,
        input_output_aliases={i: i for i in range(k)},
        compiler_params=pltpu.CompilerParams(has_side_effects=EFFECT),
    )(*arrays, send_sems, recv_sems, after)


def _plan_wout(refs):
    src, land = refs
    x, y, c, me = _place()
    return [(src, land.at[me], _peer(x, y, c, r)) for r in range(1, NDEV)]


def _plan_rest(refs):
    gob, gpf, r_out, r_pool = refs
    x, y, c, me = _place()
    plan = []
    for r in range(1, NDEV):
        plan.append((gob.at[me ^ r], r_out.at[r - 1], _peer(x, y, c, r)))
        plan.append((gpf.at[me ^ r], r_pool.at[r - 1], _peer(x, y, c, r)))
    return plan


def _plan_in(refs):
    sums, landing = refs
    x, y, c, _ = _place()
    plan = []
    for rel, (dx, dy) in enumerate(((1, 0), (0, 1), (1, 1))):
        for p in range(3):
            plan.append((sums.at[rel, p], landing.at[rel, p], (x ^ dx, y ^ dy, c)))
    return plan


def _plan_small(refs):
    small, land = refs
    x, y, c, me = _place()
    return [(small, land.at[me], _peer(x, y, c, r)) for r in range(1, NDEV)]


_POOL_SPECS = [pl.BlockSpec((None, T, GROUP), lambda g, base=base: (base + g, 0, 0)) for base in (0, 4)]
_HEAD_SPECS = [pl.BlockSpec((None, T, HEAD), lambda h, base=base: (base + h // 2, 0, h % 2))
               for base in (8, 12, 16, 20)]


_POOL_W_SPEC = pl.BlockSpec((NDEV, None, GROUP // NDEV, GROUP), lambda g: (0, g, 0, 0))


def _row_ids(shape):
    return lax.broadcasted_iota(jnp.int32, shape, 0)


BAND_ROWS = 128
HALO = 16


def _window_sum(a, gidx, lead):
    width = lax.shift_left(jnp.int32(2), gidx)
    shape = (BAND_ROWS, BAND_ROWS + HALO)
    t, j = lax.broadcasted_iota(jnp.int32, shape, 0), lax.broadcasted_iota(jnp.int32, shape, 1)
    first = t if lead else t + HALO - width + 1
    band = _bf(jnp.where(j >= first, jnp.where(j < first + width, 1.0, 0.0), 0.0))
    zeros = jnp.zeros((HALO, a.shape[1]), jnp.bfloat16)
    padded = [jnp.concatenate([p, zeros] if lead else [zeros, p], axis=0) for p in _split2(a)]
    out = []
    for r0 in range(0, T, BAND_ROWS):
        slab = jnp.concatenate([p[r0:r0 + BAND_ROWS + HALO] for p in padded], axis=1)
        r = _dot(band, slab, 1, 0)
        out.append(r[:, :a.shape[1]] + r[:, a.shape[1]:])
    return jnp.concatenate(out, axis=0)


def _window_mean(s, gidx):
    inv = jnp.where(gidx == 0, 0.5, jnp.where(gidx == 1, 0.25, jnp.where(gidx == 2, 0.125, 0.0625)))
    width = lax.shift_left(jnp.int32(2), gidx)
    head = s[:16] / jnp.minimum(_row_ids((16, s.shape[1])) + 1, width).astype(jnp.float32)
    return jnp.concatenate([head, s[16:] * inv], axis=0)


def _pool_fwd(proj, pool_w, pool_scale, token):
    def body(u_ref, pg_ref, w_ref, sc_ref, token_any, y_ref):
        del token_any
        gidx = pl.program_id(0)
        u, pg = u_ref[...], pg_ref[...]
        d = _window_mean(_window_sum(u, gidx, False), gidx) - u
        mixed = _dot(_bf(d), w_ref[...].reshape(GROUP, GROUP), 1, 0)
        y_ref[...] = _bf(mixed * sc_ref[...] * (pg * _sigmoid(pg)))

    return pl.pallas_call(
        body, name="pool_fwd", grid=(NGROUP,),
        in_specs=[*_POOL_SPECS, _POOL_W_SPEC, pl.BlockSpec((1, GROUP), lambda g: (0, g)), ANY],
        out_specs=pl.BlockSpec((T, GROUP), lambda g: (0, g)),
        out_shape=pltpu.HBM((T, DMIX), jnp.bfloat16),
        compiler_params=_params(("parallel",)),
    )(proj, proj, pool_w, pool_scale, token)


def _tri(lower):
    r = lax.broadcasted_iota(jnp.int32, (CHUNK, CHUNK), 0)
    c = lax.broadcasted_iota(jnp.int32, (CHUNK, CHUNK), 1)
    return (r >= c) if lower else (r <= c)


def _sum_rows_matrix():
    shape = (CHUNK + 16, CHUNK)
    r, c = lax.broadcasted_iota(jnp.int32, shape, 0), lax.broadcasted_iota(jnp.int32, shape, 1)
    run = jnp.where(c <= r, 1.0, 0.0)
    half = jnp.where(c < CHUNK // 2, 1.0, 0.0)
    return _bf(jnp.where(r < CHUNK, run, jnp.where(r < CHUNK + 8, 1.0, half)))


def _rev_sum_matrix():
    shape = (CHUNK, 2 * CHUNK)
    r, c = lax.broadcasted_iota(jnp.int32, shape, 0), lax.broadcasted_iota(jnp.int32, shape, 1)
    return _bf(jnp.where(c < CHUNK, jnp.where(c >= r, 1.0, 0.0), jnp.where(c - CHUNK < r, 1.0, 0.0)))


def _split2(a):
    hi = _bf(a)
    return [hi, _bf(a - hi.astype(jnp.float32))]


def _exact_sums(mat, pieces):
    x = jnp.concatenate([s for p in pieces for s in _split2(p)], axis=1)
    r = _dot(mat, x, 1, 0)
    return [r[:, 2 * j * HEAD:(2 * j + 1) * HEAD] + r[:, (2 * j + 1) * HEAD:(2 * j + 2) * HEAD]
            for j in range(len(pieces))]


def _gates(qv, fl, lb):
    sq = _sigmoid(qv)
    sg = _sigmoid(fl)
    f = lb + (1.0 - lb) * sg
    return dict(sq=sq, qs=qv * sq, sg=sg, f=f, kk=1.0 - f, g=jnp.log2(f))


def _decays(sums):
    big_g = sums[:CHUNK]
    total = sums[CHUNK:CHUNK + 8]
    g_last = jnp.tile(total, (CHUNK // 8, 1))
    g_mid = jnp.tile(sums[CHUNK + 8:], (CHUNK // 8, 1))
    return dict(
        e_q=jnp.exp2(big_g),
        e_k=jnp.exp2(g_last - big_g),
        e_qm=jnp.exp2(jnp.minimum(big_g - g_mid, EXP_CAP)),
        e_km=jnp.exp2(jnp.minimum(g_mid - big_g, EXP_CAP)),
        total8=jnp.exp2(total))


def _group_rows(gi):
    return [pl.ds(pl.multiple_of((gi * NB + j) * CHUNK, CHUNK), CHUNK) for j in range(NB)]


def _lower_bound(lb_ref):
    return _sigmoid(lb_ref[0:1, :] - lb_ref[1:2, :])


def _hgrn_fwd(proj, lb_logits, rec_g, y_in):
    def body(q_ref, f_ref, i_ref, gate_ref, lb_ref, rg_ref, y_any, y_ref, o_ref, st_ref):
        del y_any
        lb = _lower_bound(lb_ref)
        causal = _tri(True)
        smat = _sum_rows_matrix()

        def group(gi, st):
            rows = _group_rows(gi)
            ts = [_gates(q_ref[r, :], f_ref[r, :], lb) for r in rows]
            ds = [_decays(s) for s in _exact_sums(smat, [t["g"] for t in ts])]
            vs = [_bf(i_ref[r, :]) for r in rows]
            q_m = [_bf(t["qs"] * d["e_qm"]) for t, d in zip(ts, ds)]
            k_m = [_bf(t["kk"] * d["e_km"]) for t, d in zip(ts, ds)]
            q_e = [_bf(t["qs"] * d["e_q"]) for t, d in zip(ts, ds)]
            k_e = [_bf(t["kk"] * d["e_k"]) for t, d in zip(ts, ds)]
            a = [_bf(jnp.where(causal, _dot(q_m[j], k_m[j], 1, 1), 0.0)) for j in range(NB)]
            intra = [_dot(a[j], vs[j], 1, 0) for j in range(NB)]
            upd = [_dot(vs[j], k_e[j], 0, 0) for j in range(NB)]
            for j in range(NB):
                st_ref[gi * NB + j] = st
                o_ref[rows[j], :] = intra[j] + _dot(q_e[j], _bf(st), 1, 1)
                st = st * jnp.tile(ds[j]["total8"], (HEAD // 8, 1)) + upd[j]
            return st

        lax.fori_loop(0, NGRP, group, jnp.zeros((HEAD, HEAD), jnp.float32))
        o = o_ref[...]
        rn = o * lax.rsqrt(jnp.mean(o * o, axis=-1, keepdims=True) + EPS)
        gate = gate_ref[...]
        y_ref[...] = _bf(rn * rg_ref[...] * (gate * _sigmoid(gate)))

    return pl.pallas_call(
        body, name="hgrn_fwd", grid=(NHEAD,),
        in_specs=[*_HEAD_SPECS,
                  pl.BlockSpec((2, HEAD), lambda h: (0, h)),
                  pl.BlockSpec((1, HEAD), lambda h: (0, h)),
                  pl.BlockSpec(memory_space=pl.ANY)],
        out_specs=(pl.BlockSpec((T, HEAD), lambda h: (0, NHEAD + h)),
                   pl.BlockSpec((T, HEAD), lambda h: (0, h)),
                   pl.BlockSpec((None, NCHUNK, HEAD, HEAD), lambda h: (h, 0, 0, 0))),
        out_shape=(pltpu.HBM((T, DMIX), jnp.bfloat16), pltpu.HBM((T, D), jnp.float32),
                   pltpu.HBM((NHEAD, NCHUNK, HEAD, HEAD), jnp.float32)),
        input_output_aliases={6: 0},
        compiler_params=_params(("parallel",)),
    )(proj, proj, proj, proj, lb_logits, rec_g, y_in)


def _out_proj_loss(x, y, w_out, target, gf):
    rows = 512
    parts = [slice(k * rows // 2, (k + 1) * rows // 2) for k in range(2)]

    def body(x_ref, y_ref, w_ref, t_ref, g_ref, dz_ref, dzb_ref, sq_ref, dg_ref):
        zs = [x_ref[p, :] + _dot(y_ref[p, :], w_ref[...], 1, 0) for p in parts]
        sq = dg = 0.0
        for p, z in zip(parts, zs):
            r = lax.rsqrt(jnp.mean(z * z, axis=-1, keepdims=True) + EPS)
            zhat = z * r
            err = zhat * g_ref[...] - t_ref[p, :]
            dy = err * (1.0 / D)
            gdy = dy * g_ref[...]
            dz = r * (gdy - zhat * jnp.mean(zhat * gdy, axis=-1, keepdims=True))
            dz_ref[p, :] = dz
            dzb_ref[p, :] = _bf(dz)
            sq = sq + jnp.sum(err * err, axis=0, keepdims=True)
            dg = dg + jnp.sum(zhat * dy, axis=0, keepdims=True)

        @pl.when(pl.program_id(0) == 0)
        def _():
            sq_ref[...] = sq
            dg_ref[...] = dg

        @pl.when(pl.program_id(0) != 0)
        def _():
            sq_ref[...] += sq
            dg_ref[...] += dg

    tile = pl.BlockSpec((rows, D), lambda i: (i, 0))
    vec = pl.BlockSpec((1, D), lambda i: (0, 0))
    return pl.pallas_call(
        body, name="out_proj_loss", grid=(T // rows,),
        in_specs=[tile, pl.BlockSpec((rows, DMIX), lambda i: (i, 0)), pl.BlockSpec((DMIX, D), lambda i: (0, 0)),
                  tile, vec],
        out_specs=(tile, tile, vec, vec),
        out_shape=(pltpu.HBM((T, D), jnp.float32), pltpu.HBM((T, D), jnp.bfloat16),
                   pltpu.HBM((1, D), jnp.float32), pltpu.HBM((1, D), jnp.float32)),
        compiler_params=_params(("arbitrary",)),
    )(x, y, w_out, target, gf)


def _out_proj_bwd(dzb, w_out, y):
    tn = 512

    def body(dz_ref, w_ref, y_ref, dy_ref, gw_ref, gwb_ref):
        dz = dz_ref[...]
        dy_ref[...] = _dot(dz, w_ref[...], 1, 1)
        gw = _dot(y_ref[...], dz, 0, 0)
        gw_ref[...] = gw
        gwb_ref[...] = _bf(gw)

    return pl.pallas_call(
        body, name="out_proj_bwd", grid=(DMIX // tn,),
        in_specs=[pl.BlockSpec((T, D), lambda n: (0, 0)), pl.BlockSpec((tn, D), lambda n: (n, 0)),
                  pl.BlockSpec((T, tn), lambda n: (0, n))],
        out_specs=(pl.BlockSpec((T, tn), lambda n: (0, n)), pl.BlockSpec((tn, D), lambda n: (n, 0)),
                   pl.BlockSpec((tn, D), lambda n: (n, 0))),
        out_shape=(pltpu.HBM((T, DMIX), jnp.float32), pltpu.HBM((DMIX, D), jnp.float32),
                   pltpu.HBM((DMIX, D), jnp.bfloat16)),
        compiler_params=_params(("parallel",)),
    )(dzb, w_out, y)


def _hgrn_bwd(proj, lb_logits, rec_g, o, states, dymix, dproj_in, token):
    def body(q_ref, f_ref, i_ref, gate_ref, lb_ref, rg_ref, o_ref, st_ref, dy_ref, dp_any, token_any,
             dp_ref, drg_ref, dlb_ref, do_ref):
        del dp_any, token_any
        lb = _lower_bound(lb_ref)
        causal = _tri(True)
        smat, rmat = _sum_rows_matrix(), _rev_sum_matrix()

        o = o_ref[...]
        rs = lax.rsqrt(jnp.mean(o * o, axis=-1, keepdims=True) + EPS)
        rn = o * rs
        gate = gate_ref[...]
        sgate = _sigmoid(gate)
        dyv = dy_ref[...]
        d_r = dyv * (gate * sgate)
        dp_ref[3] = _bf(dyv * (rn * rg_ref[...]) * (sgate * (1.0 + gate * (1.0 - sgate))))
        drg_ref[...] = jnp.sum(d_r * rn, axis=0, keepdims=True)
        drn = d_r * rg_ref[...]
        do_ref[...] = rs * (drn - rn * jnp.mean(rn * drn, axis=-1, keepdims=True))

        def group(i, carry):
            dst, dlb = carry
            gi = NGRP - 1 - i
            rows = _group_rows(gi)
            span = range(NB)
            qvs = [q_ref[r, :] for r in rows]
            ts = [_gates(qv, f_ref[r, :], lb) for qv, r in zip(qvs, rows)]
            ds = [_decays(s) for s in _exact_sums(smat, [t["g"] for t in ts])]
            vs = [_bf(i_ref[r, :]) for r in rows]
            dos = [_bf(do_ref[r, :]) for r in rows]
            sts = [st_ref[gi * NB + j] for j in span]
            qe_f = [t["qs"] * d["e_q"] for t, d in zip(ts, ds)]
            ke_f = [t["kk"] * d["e_k"] for t, d in zip(ts, ds)]
            q_e, k_e = [_bf(a) for a in qe_f], [_bf(a) for a in ke_f]
            q_m = [_bf(t["qs"] * d["e_qm"]) for t, d in zip(ts, ds)]
            k_m = [_bf(t["kk"] * d["e_km"]) for t, d in zip(ts, ds)]
            a = [_bf(jnp.where(causal, _dot(q_m[j], k_m[j], 1, 1), 0.0)) for j in span]
            da = [_bf(jnp.where(causal, _dot(dos[j], vs[j], 1, 1), 0.0)) for j in span]
            dqm = [_dot(da[j], k_m[j], 1, 0) for j in span]
            dkm = [_dot(da[j], q_m[j], 0, 0) for j in span]
            dv_in = [_dot(a[j], dos[j], 0, 0) for j in span]
            dqe = [_dot(dos[j], _bf(sts[j]), 1, 0) for j in span]
            grow = [_dot(dos[j], q_e[j], 0, 0) for j in span]
            dke, carried = [None] * NB, [None] * NB
            for j in reversed(span):
                dst_b = _bf(dst)
                dke[j] = _dot(vs[j], dst_b, 1, 0)
                dp_ref[2, rows[j], :] = _bf(dv_in[j] + _dot(k_e[j], dst_b, 1, 1))
                carried[j] = ds[j]["total8"] * jnp.sum(dst * sts[j], axis=0, keepdims=True)
                dst = dst * jnp.tile(ds[j]["total8"], (HEAD // 8, 1)) + grow[j]
            kdk = [ke_f[j] * dke[j] for j in span]
            pos = [(q_m[j].astype(jnp.float32) * dqm[j] - k_m[j].astype(jnp.float32) * dkm[j]) + qe_f[j] * dqe[j]
                   for j in span]
            dgs = _exact_sums(rmat, [jnp.concatenate([pos[j], kdk[j]], axis=0) for j in span])
            for j in span:
                t, d = ts[j], ds[j]
                dg = dgs[j] + jnp.tile(carried[j], (CHUNK // 8, 1))
                dqs = dqm[j] * d["e_qm"] + dqe[j] * d["e_q"]
                dkk = dkm[j] * d["e_km"] + dke[j] * d["e_k"]
                df = dg / t["f"] - dkk
                dp_ref[1, rows[j], :] = _bf(df * (1.0 - lb) * (t["sg"] * (1.0 - t["sg"])))
                dp_ref[0, rows[j], :] = _bf(dqs * (t["sq"] * (1.0 + qvs[j] * (1.0 - t["sq"]))))
                dlb = dlb + df * (1.0 - t["sg"])
            return dst, dlb

        _, dlb = lax.fori_loop(0, NGRP, group, (jnp.zeros((HEAD, HEAD), jnp.float32),
                                                jnp.zeros((CHUNK, HEAD), jnp.float32)))
        dlb_ref[...] = jnp.sum(dlb, axis=0, keepdims=True)

    vec = pl.BlockSpec((1, HEAD), lambda h: (0, h))
    return pl.pallas_call(
        body, name="hgrn_bwd", grid=(NHEAD,),
        in_specs=[*_HEAD_SPECS,
                  pl.BlockSpec((2, HEAD), lambda h: (0, h)), vec,
                  pl.BlockSpec((T, HEAD), lambda h: (0, h)),
                  pl.BlockSpec((None, NCHUNK, HEAD, HEAD), lambda h: (h, 0, 0, 0)),
                  pl.BlockSpec((T, HEAD), lambda h: (0, NHEAD + h)), ANY, ANY],
        out_specs=(pl.BlockSpec((4, T, HEAD), lambda h: (0, 0, h)), vec, vec),
        out_shape=(pltpu.HBM((NSEG, T, D), jnp.bfloat16),
                   pltpu.HBM((1, D), jnp.float32), pltpu.HBM((1, D), jnp.float32)),
        scratch_shapes=[pltpu.VMEM((T, HEAD), jnp.float32)],
        input_output_aliases={9: 0},
        compiler_params=_params(("parallel",)),
    )(proj, proj, proj, proj, lb_logits, rec_g, o, states, dymix, dproj_in, token)


def _pool_bwd(proj, pool_w, pool_scale, dymix):
    def body(u_ref, pg_ref, w_ref, sc_ref, dy_ref, dp_ref, gw_ref, gs_ref):
        gidx = pl.program_id(0)
        u, pg = u_ref[...], pg_ref[...]
        w = w_ref[...].reshape(GROUP, GROUP)
        d = _bf(_window_mean(_window_sum(u, gidx, False), gidx) - u)
        mixed = _dot(d, w, 1, 0)
        spg = _sigmoid(pg)
        dyv = dy_ref[...]
        d_p = dyv * (pg * spg)
        dp_ref[1] = _bf(dyv * (mixed * sc_ref[...]) * (spg * (1.0 + pg * (1.0 - spg))))
        gs_ref[...] = jnp.sum(d_p * mixed, axis=0, keepdims=True)
        dmixed = _bf(d_p * sc_ref[...])
        gw_ref[...] = _dot(d, dmixed, 0, 0).reshape(gw_ref.shape)
        dd = _dot(dmixed, w, 1, 1)
        dp_ref[0] = _bf(_window_sum(_window_mean(dd, gidx), gidx, True) - dd)

    return pl.pallas_call(
        body, name="pool_bwd", grid=(NGROUP,),
        in_specs=[*_POOL_SPECS, _POOL_W_SPEC,
                  pl.BlockSpec((1, GROUP), lambda g: (0, g)),
                  pl.BlockSpec((T, GROUP), lambda g: (0, g))],
        out_specs=(pl.BlockSpec((2, T, GROUP), lambda g: (2, 0, g)), _POOL_W_SPEC,
                   pl.BlockSpec((1, GROUP), lambda g: (0, g))),
        out_shape=(pltpu.HBM((NSEG, T, D), jnp.bfloat16),
                   pltpu.HBM((NDEV, NGROUP, GROUP // NDEV, GROUP), jnp.float32),
                   pltpu.HBM((1, D), jnp.float32)),
        compiler_params=_params(("parallel",)),
    )(proj, proj, pool_w, pool_scale, dymix)


HALF = NTILE // 2
AWAY = HALF - 3


def _dproj_tile(chip, side, p):
    j = 6 * chip + 3 * side + p
    return ((j // 4 + 4) % NSEG, 0, j % 4)


def _sibling_copy(sib_out, sib_in, send_sems, recv_sems, slot):
    x, y, c, _ = _place()
    return pltpu.make_async_remote_copy(
        src_ref=sib_out.at[slot], dst_ref=sib_in.at[slot], send_sem=send_sems.at[slot],
        recv_sem=recv_sems.at[slot], device_id=(x, y, 1 - c), device_id_type=MESH)


def _proj_bwd_w_far(place, ht, dproj):
    def body(place_ref, h_ref, dpa_ref, dpb_ref, sib_out, sib_in, send_sems, recv_sems, stage, loc_sems):
        del place_ref
        i = pl.program_id(0)

        def to_hbm(k):
            return pltpu.make_async_copy(stage.at[k], sib_out.at[k], loc_sems.at[k])

        def send(k):
            to_hbm(k).wait()
            _sibling_copy(sib_out, sib_in, send_sems, recv_sems, k).start()

        stage[2 * i] = _bf(_dot(h_ref[...], dpa_ref[...], 1, 0))
        stage[2 * i + 1] = _bf(_dot(h_ref[...], dpb_ref[...], 1, 0))

        @pl.when(i > 0)
        def _():
            send(2 * i - 2)
            send(2 * i - 1)

        to_hbm(2 * i).start()
        to_hbm(2 * i + 1).start()

        @pl.when(i == HALF // 2 - 1)
        def _():
            send(2 * i)
            send(2 * i + 1)

    def tile(k, pr):
        return _dproj_tile(k // 3, 1 - pr[2], k % 3)

    buf = pltpu.HBM((HALF, D, TILE), jnp.bfloat16)
    sems = pltpu.SemaphoreType.DMA((HALF,))
    return pl.pallas_call(
        body, name="proj_bwd_w_far",
        grid_spec=pltpu.PrefetchScalarGridSpec(
            num_scalar_prefetch=1, grid=(HALF // 2,),
            in_specs=[pl.BlockSpec((D, T), lambda i, pr: (0, 0)),
                      pl.BlockSpec((None, T, TILE), lambda i, pr: tile(2 * i, pr)),
                      pl.BlockSpec((None, T, TILE), lambda i, pr: tile(2 * i + 1, pr))],
            out_specs=(HBM, HBM, SEM, SEM),
            scratch_shapes=[pltpu.VMEM((HALF, D, TILE), jnp.bfloat16), pltpu.SemaphoreType.DMA((HALF,))]),
        out_shape=(buf, buf, sems, sems),
        compiler_params=pltpu.CompilerParams(dimension_semantics=("arbitrary",), vmem_limit_bytes=48 * MIB,
                                             has_side_effects=EFFECT),
    )(place, ht, dproj, dproj)


def _proj_bwd_w_near(place, ht, dproj, sib_out, sib_in, sib_send, sib_recv):
    def owner_chip(k, pr):
        return jnp.where(k < AWAY, (pr[1] + 1 + k % 3) % 4, pr[1])

    def tile_p(k):
        return jnp.where(k < AWAY, k // 3, k - AWAY)

    def body(place_ref, h_ref, dp_ref, sib_out, sib_in, sib_send, sib_recv, sums, own_ref, landing, out_send,
             out_recv, recvbuf, outbuf, in_sems, loc_sems):
        i = pl.program_id(0)
        px, py, c, _ = _place()

        def slot_of(k):
            return 3 * owner_chip(k, place_ref) + tile_p(k)

        def load(k):
            return pltpu.make_async_copy(sib_in.at[slot_of(k)], recvbuf.at[k % 2], in_sems.at[k % 2])

        def fetch(k):
            _sibling_copy(sib_out, sib_in, sib_send, sib_recv, slot_of(k)).wait_recv()
            load(k).start()

        def route(k):
            chip = owner_chip(k, place_ref)
            cx, cy = chip // 2, chip % 2
            return cx, cy, (cx ^ px) + 2 * (cy ^ py) - 1, tile_p(k)

        def to_hbm(k):
            _, _, rel, p = route(k)
            return pltpu.make_async_copy(outbuf.at[k], sums.at[rel, p], loc_sems.at[k])

        def to_owner(k):
            cx, cy, rel, p = route(k)
            return pltpu.make_async_remote_copy(
                src_ref=sums.at[rel, p], dst_ref=landing.at[rel, p], send_sem=out_send.at[3 * rel + p],
                recv_sem=out_recv.at[3 * rel + p], device_id=(cx, cy, c), device_id_type=MESH)

        @pl.when(i == 0)
        def _():
            fetch(i)

        @pl.when(i < HALF - 1)
        def _():
            fetch(i + 1)

        load(i).wait()
        total = _dot(h_ref[...], dp_ref[...], 1, 0) + recvbuf[i % 2].astype(jnp.float32)
        own_ref[...] = total
        outbuf[jnp.minimum(i, AWAY)] = _bf(total)

        @pl.when(i < AWAY)
        def _():
            to_hbm(i).start()

        @pl.when(jnp.logical_and(i > 0, i <= AWAY))
        def _():
            to_hbm(i - 1).wait()
            to_owner(i - 1).start()

        @pl.when(i == HALF - 1)
        def _():
            for slot in range(HALF):
                _sibling_copy(sib_out, sib_in, sib_send, sib_recv, slot).wait_send()

    travelling = pltpu.HBM((3, 3, D, TILE), jnp.bfloat16)
    sems = pltpu.SemaphoreType.DMA((AWAY,))
    return pl.pallas_call(
        body, name="proj_bwd_w_near",
        grid_spec=pltpu.PrefetchScalarGridSpec(
            num_scalar_prefetch=1, grid=(HALF,),
            in_specs=[pl.BlockSpec((D, T), lambda i, pr: (0, 0)),
                      pl.BlockSpec((None, T, TILE), lambda i, pr: _dproj_tile(owner_chip(i, pr), pr[2], tile_p(i))),
                      HBM, HBM, SEM, SEM],
            out_specs=(HBM, pl.BlockSpec((None, D, TILE), lambda i, pr: (jnp.where(i < AWAY, 0, i % 3), 0, 0)),
                       HBM, SEM, SEM),
            scratch_shapes=[pltpu.VMEM((2, D, TILE), jnp.bfloat16), pltpu.VMEM((AWAY + 1, D, TILE), jnp.bfloat16),
                            pltpu.SemaphoreType.DMA((2,)), pltpu.SemaphoreType.DMA((AWAY,))]),
        out_shape=(travelling, pltpu.HBM((3, D, TILE), jnp.float32), travelling, sems, sems),
        compiler_params=pltpu.CompilerParams(dimension_semantics=("arbitrary",), vmem_limit_bytes=48 * MIB,
                                             has_side_effects=EFFECT),
    )(place, ht, dproj, sib_out, sib_in, sib_send, sib_recv)


def _proj_bwd_x(dproj, w_t, x, g1, dz, others, token):
    tm = 1024
    pairs = NSEG // 2
    k = len(others)

    def body(dp_ref, w_ref, x_any, g_ref, dz_any, *refs):
        dx_ref, dg_ref, wcat, acc, xbuf, dzbuf, row_sems = refs[k + 1:]
        m, s = pl.program_id(0), pl.program_id(1)
        mine = pl.ds(pl.multiple_of(m * tm, tm), tm)
        fetches = [pltpu.make_async_copy(x_any.at[mine], xbuf, row_sems.at[0]),
                   pltpu.make_async_copy(dz_any.at[mine], dzbuf, row_sems.at[1])]

        @pl.when(m == 0)
        def _():
            for i in range(8):
                wcat[s, :, i * TILE:(i + 1) * TILE] = w_ref[i]

        @pl.when(s == 0)
        def _():
            for cp in fetches:
                cp.start()
            acc[...] = jnp.zeros((tm, D), jnp.float32)

        acc[...] += _dot(jnp.concatenate([dp_ref[0], dp_ref[1]], axis=1), wcat[s], 1, 1)

        @pl.when(s == pairs - 1)
        def _():
            for cp in fetches:
                cp.wait()
            xv = xbuf[...]
            rs = lax.rsqrt(jnp.mean(xv * xv, axis=-1, keepdims=True) + EPS)
            xhat = xv * rs
            dhv = acc[...]
            gdh = dhv * g_ref[...]
            dx_ref[...] = dzbuf[...] + rs * (gdh - xhat * jnp.mean(xhat * gdh, axis=-1, keepdims=True))
            dg = jnp.sum(xhat * dhv, axis=0, keepdims=True)

            @pl.when(m == 0)
            def _():
                dg_ref[0:1, :] = dg
                for r, ref in enumerate(refs[:k]):
                    dg_ref[1 + r:2 + r, :] = ref[...]
                dg_ref[1 + k:, :] = jnp.zeros((7 - k, D), jnp.float32)

            @pl.when(m != 0)
            def _():
                dg_ref[0:1, :] += dg

    rows = pl.BlockSpec((tm, D), lambda m, s: (m, 0))
    vec = pl.BlockSpec((1, D), lambda m, s: (0, 0))
    return pl.pallas_call(
        body, name="proj_bwd_x", grid=(T // tm, pairs),
        in_specs=[pl.BlockSpec((2, tm, D), lambda m, s: (s, m, 0)),
                  pl.BlockSpec((8, D, TILE), lambda m, s: ((jnp.where(m == 0, s, pairs - 1) + 1) % pairs, 0, 0)),
                  HBM, vec, HBM, *[vec] * k, ANY],
        out_specs=(rows, pl.BlockSpec((8, D), lambda m, s: (0, 0))),
        out_shape=(jax.ShapeDtypeStruct((T, D), jnp.float32), pltpu.HBM((8, D), jnp.float32)),
        scratch_shapes=[pltpu.VMEM((pairs, D, 2 * D), jnp.bfloat16), pltpu.VMEM((tm, D), jnp.float32),
                        pltpu.VMEM((tm, D), jnp.float32), pltpu.VMEM((tm, D), jnp.float32),
                        pltpu.SemaphoreType.DMA((2,))],
        compiler_params=_params(("arbitrary", "arbitrary"), vmem_mib=58),
    )(dproj, w_t, _in_hbm(x), g1, _in_hbm(dz), *[_in_hbm(a) for a in others], token)


def _adamw(w, g, m, v):
    m_new = ADAM_B1 * m + (1.0 - ADAM_B1) * g
    v_new = ADAM_B2 * v + (1.0 - ADAM_B2) * (g * g)
    delta = -ADAM_LR * ((m_new / BC1) / (jnp.sqrt(v_new / BC2) + ADAM_EPS) + ADAM_WD * w)
    return delta, m_new, v_new


def _reduce_adam(name, place, parts, w, m, v, grid, w_spec):
    n = len(parts)

    def body(place_ref, *refs):
        del place_ref
        w_ref, m_ref, v_ref, g_ref, d_ref, mo_ref, vo_ref = refs[n:]
        g = None
        for ref, (_, _, stacked) in zip(refs[:n], parts):
            terms = [ref[r] for r in range(ref.shape[0])] if stacked else [ref[...]]
            for t in terms:
                if t.shape[-1] != w_ref.shape[-1]:
                    t = jnp.concatenate([t[p] for p in range(t.shape[0])], axis=1)
                g = t.astype(jnp.float32) if g is None else g + t.astype(jnp.float32)
        delta, m_new, v_new = _adamw(w_ref[...], g, m_ref[...], v_ref[...])
        g_ref[...] = g
        d_ref[...] = delta
        mo_ref[...] = m_new
        vo_ref[...] = v_new

    shape = jax.ShapeDtypeStruct(w.shape, jnp.float32)
    return pl.pallas_call(
        body, name=name,
        grid_spec=pltpu.PrefetchScalarGridSpec(
            num_scalar_prefetch=1, grid=grid,
            in_specs=[spec for _, spec, _ in parts] + [w_spec] * 3, out_specs=(w_spec,) * 4),
        out_shape=(shape,) * 4,
        compiler_params=_params(("parallel",)),
    )(place, *[_in_hbm(a) for a in [a for a, _, _ in parts] + [w, m, v]])


SMALL_ROWS = (1, 1, 2, 1, 1)


def _small_adam(place, own, parts, ws, ms, vs):
    n = len(SMALL_ROWS)

    def body(place_ref, own_ref, p_ref, *refs):
        ins, outs, bufs = refs[:3 * n], refs[3 * n:3 * n + 1 + 4 * n], refs[3 * n + 1 + 4 * n:]

        def stacked(group, buf):
            r0 = 0
            for ref, k in zip(group, SMALL_ROWS):
                buf[r0:r0 + k, :] = ref[...]
                r0 += k
            buf[r0:, :] = jnp.zeros((8 - r0, D), jnp.float32)
            return buf[...]

        wv, mv, vv = (stacked(ins[n * j:n * j + n], bufs[j]) for j in range(3))
        me = place_ref[0]
        g = None
        for s in range(NDEV):
            term = jnp.where(me == s, own_ref[...], p_ref[s])
            g = term if g is None else g + term
        rows = _row_ids(wv.shape)
        other = jnp.where(rows == 2, pltpu.roll(wv, 7, 0), jnp.where(rows == 3, pltpu.roll(wv, 1, 0), 0.0))
        lbv = _sigmoid(wv - other)
        sign = jnp.where(rows == 2, 1.0, -1.0)
        g = jnp.where((rows == 2) | (rows == 3), sign * g * lbv * (1.0 - lbv), g)
        delta, m_new, v_new = _adamw(wv, g, mv, vv)
        outs[0][...] = jnp.sum(g[6:7], axis=1, keepdims=True) * (0.5 / D)
        for j, val in enumerate((g, delta, m_new, v_new)):
            r0 = 0
            for ref, k in zip(outs[1 + n * j:1 + n * j + n], SMALL_ROWS):
                ref[...] = val[r0:r0 + k]
                r0 += k

    vmem = pl.BlockSpec(memory_space=pltpu.VMEM)
    shapes = [jax.ShapeDtypeStruct((k, D), jnp.float32) for k in SMALL_ROWS]
    out = pl.pallas_call(
        body, name="small_adam", out_shape=(jax.ShapeDtypeStruct((1, 1), jnp.float32), *shapes * 4),
        in_specs=[pl.BlockSpec(memory_space=pltpu.SMEM)] + [vmem] * (2 + 3 * n), out_specs=(vmem,) * (1 + 4 * n),
        scratch_shapes=[pltpu.VMEM((8, D), jnp.float32)] * 3,
    )(place, own, parts, *ws, *ms, *vs)
    return out[0], [out[1 + n * j:1 + n * j + n] for j in range(4)]


def kernel(x, norm1_g, w_in, pool_w, pool_scale, lb_logits, rec_norm_g, w_out, final_norm_g, loss_target, m_norm1_g, m_w_in, m_pool_w, m_pool_scale, m_lb_logits, m_rec_norm_g, m_w_out, m_final_norm_g, v_norm1_g, v_w_in, v_pool_w, v_pool_scale, v_lb_logits, v_rec_norm_g, v_w_out, v_final_norm_g):
    xs = x[0]
    target = loss_target[0]
    ix, iy, ic = lax.axis_index("x"), lax.axis_index("y"), lax.axis_index("c")
    place = jnp.stack([4 * ix + 2 * iy + ic, 2 * ix + iy, ic]).astype(jnp.int32)
    gf = final_norm_g.reshape(1, D)

    ht, w_t, w_out_b, w_out_g, pool_g, proj = _gather_proj(xs, norm1_g, w_in, w_out, pool_w)
    wout = [w_out_b, w_out_g]
    wout_send, wout_recv, wout, wout_token = _split_start("gather_wout_start", wout, NDEV - 1, _plan_wout)

    y = _pool_fwd(proj, pool_g, pool_scale, wout_token)
    y, o, states = _hgrn_fwd(proj, lb_logits, rec_norm_g, y)
    _, w_out_g = _split_wait("gather_wout_wait", wout, wout_send, wout_recv, _plan_wout, o)
    w_out_full = _in_hbm(w_out_g.reshape(DMIX, D))
    dz, dzb, sq, dgf = _out_proj_loss(xs, y, w_out_full, target, gf)

    dymix, gwout_f, gwout_b = _out_proj_bwd(dzb, w_out_full, y)
    dproj, gpool, dscale = _pool_bwd(proj, pool_g, pool_scale, dymix)

    blk_out = (NDEV, DMIX // NDEV, D)
    blk_pool = (NDEV, NGROUP, GROUP // NDEV, GROUP)
    rest = [gwout_b.reshape(blk_out), gpool,
            lax.empty((NDEV - 1,) + blk_out[1:], jnp.bfloat16), lax.empty((NDEV - 1,) + blk_pool[1:], jnp.float32)]
    rest_send, rest_recv, rest, rest_token = _split_start("scatter_rest_start", rest, 2 * (NDEV - 1), _plan_rest)

    dproj, drecg, dlb = _hgrn_bwd(proj, lb_logits, rec_norm_g, o, states, dymix, dproj, rest_token)
    chip_sums, own_sum, landing, win_send, win_recv = _proj_bwd_w_near(
        place, ht, dproj, *_proj_bwd_w_far(place, ht, dproj))
    win = [chip_sums, landing]

    others = [a.reshape(1, D) for a in (dscale, dlb, dlb, drecg, dgf, sq)]
    grad_x, small_block = _proj_bwd_x(dproj, w_t, xs, norm1_g, dz, others, chip_sums)

    small = [small_block, lax.empty((NDEV, 8, D), jnp.float32)]
    small_send, small_recv, small, small_token = _split_start("gather_small_start", small, NDEV - 1, _plan_small)

    _, gpool_own, r_out, r_pool = _split_wait("scatter_rest_wait", rest, rest_send, rest_recv, _plan_rest,
                                              small_token)
    g_wout, d_wout, m_wout, v_wout = _reduce_adam(
        "adam_w_out", place,
        [(gwout_f.reshape(blk_out), pl.BlockSpec((None,) + blk_out[1:], lambda i, pr: (pr[0], 0, 0)), False),
         (r_out, pl.BlockSpec((NDEV - 1,) + blk_out[1:], lambda i, pr: (0, 0, 0)), True)],
        w_out, m_w_out, v_w_out, (1,), pl.BlockSpec((None,) + blk_out[1:], lambda i, pr: (0, 0, 0)))
    g_pool, d_pool, m_pool, v_pool = _reduce_adam(
        "adam_pool_w", place,
        [(gpool_own, pl.BlockSpec((None,) + blk_pool[1:], lambda i, pr: (pr[0], 0, 0, 0)), False),
         (r_pool, pl.BlockSpec((NDEV - 1,) + blk_pool[1:], lambda i, pr: (0, 0, 0, 0)), True)],
        pool_w, m_pool_w, v_pool_w, (1,), pl.BlockSpec((None,) + blk_pool[1:], lambda i, pr: (0, 0, 0, 0)))

    _, r_in = _split_wait("scatter_win_wait", win, win_send, win_recv, _plan_in, d_pool)
    g_win, d_win, m_win, v_win = _reduce_adam(
        "adam_w_in", place,
        [(own_sum, pl.BlockSpec((3, D // 8, TILE), lambda i, pr: (0, i, 0)), False),
         (r_in, pl.BlockSpec((3, 3, D // 8, TILE), lambda i, pr: (0, 0, i, 0)), True)],
        w_in, m_w_in, v_w_in, (8,), pl.BlockSpec((None, D // 8, 3 * TILE), lambda i, pr: (0, i, 0)))

    own_small, r_small = _split_wait("gather_small_wait", small, small_send, small_recv, _plan_small, d_win)
    loss, (g_s, d_s, m_s, v_s) = _small_adam(
        place, own_small, r_small,
        (norm1_g, pool_scale, lb_logits, rec_norm_g, gf),
        (m_norm1_g, m_pool_scale, m_lb_logits, m_rec_norm_g, m_final_norm_g.reshape(1, D)),
        (v_norm1_g, v_pool_scale, v_lb_logits, v_rec_norm_g, v_final_norm_g.reshape(1, D)))

    def outs(small, win, pool, wout):
        n1, ps, lbl, rg, fg = small
        return n1, win, pool, ps, lbl, rg, wout, fg.reshape(D)

    return (loss.reshape(()), grad_x[None],
            *outs(g_s, g_win, g_pool, g_wout), *outs(d_s, d_win, d_pool, d_wout),
            *outs(m_s, m_win, m_pool, m_wout), *outs(v_s, v_win, v_pool, v_wout))
```

```python
import jax
import jax.numpy as jnp
from jax import lax
from jax.experimental import pallas as pl
from jax.experimental.pallas import tpu as pltpu

T = 2048
D = 1024
NSEG = 6
NTILE = 24
TILE = 256
DMIX = 2048
NDEV = 8
HEAD = 128
NHEAD = 8
CHUNK = 64
NCHUNK = T // CHUNK
NB = 32
NGRP = NCHUNK // NB
NGROUP = 4
GROUP = 256
EPS = 1e-6
EXP_CAP = 115.0
MESH = pl.DeviceIdType.MESH
ANY = pl.BlockSpec(memory_space=pl.ANY)
HBM = pl.BlockSpec(memory_space=pltpu.HBM)
SEM = pl.BlockSpec(memory_space=pltpu.SEMAPHORE)
EFFECT = pltpu.SideEffectType.DATAFLOW_SIDE_EFFECTING

ADAM_LR = 0.001
ADAM_B1 = 0.9
ADAM_B2 = 0.999
ADAM_EPS = 1e-08
ADAM_WD = 0.01
ADAM_STEP = 10
BC1 = 1.0 - ADAM_B1 ** ADAM_STEP
BC2 = 1.0 - ADAM_B2 ** ADAM_STEP

MIB = 1 << 20


def _params(sem=None, vmem_mib=48):
    return pltpu.CompilerParams(dimension_semantics=sem, vmem_limit_bytes=vmem_mib * MIB)


def _sigmoid(v):
    return 1.0 / (1.0 + jnp.exp(-v))


def _dot(a, b, ca, cb, precision=None):
    return lax.dot_general(a, b, (((ca,), (cb,)), ((), ())), precision=precision,
                           preferred_element_type=jnp.float32)


def _bf(v):
    return v.astype(jnp.bfloat16)


def _in_hbm(a):
    return pltpu.with_memory_space_constraint(a, pltpu.HBM)


def _place():
    x, y, c = lax.axis_index("x"), lax.axis_index("y"), lax.axis_index("c")
    return x, y, c, 4 * x + 2 * y + c


def _peer(x, y, c, r):
    return (x ^ ((r >> 2) & 1), y ^ ((r >> 1) & 1), c ^ (r & 1))


def _gather_proj(x, g1, w_in, w_out, pool_w):
    def body(x_ref, g_ref, win_ref, wout_ref, pool_ref, ht_o, wt_o, woutb_o, wout_o, pool_o, proj_o,
             xbuf, hv, htv, wv, wob, pb, stage, send_sems, recv_sems, loc_sems, out_sems):
        px, py, c, my_idx = _place()
        fetch_x = pltpu.make_async_copy(x_ref, xbuf, loc_sems.at[5])
        fetch_x.start()
        me, sibling = (px, py, c), (px, py, 1 - c)
        chips = [(1 - px, py), (px, 1 - py), (1 - px, 1 - py)]
        for p in range(3):
            wv[3 * my_idx + p] = _bf(win_ref[0, :, p * TILE:(p + 1) * TILE])

        def index(bx, by, bc):
            return 4 * bx + 2 * by + bc

        def slot(w, block):
            return wv.at[pl.ds(3 * index(*block), 3)] if w == 0 else pool_o.at[index(*block)]

        def copy(k, w, block, to, src=None):
            return pltpu.make_async_remote_copy(
                src_ref=slot(w, block) if src is None else src, dst_ref=slot(w, block),
                send_sem=send_sems.at[2 * k + w], recv_sem=recv_sems.at[2 * k + w],
                device_id=to, device_id_type=MESH)

        def save(block):
            at = pl.ds(3 * index(*block), 3)
            pltpu.make_async_copy(wv.at[at], wt_o.at[at], loc_sems.at[4]).start()

        srcs = (slot(0, me), pb)
        first = []
        for w in (0, 1):
            if w == 1:
                pb[...] = _bf(pool_ref[0])
                wob[...] = _bf(wout_ref[0])
            group = [copy(1 + j, w, me, (*chip, c), src=srcs[w]) for j, chip in enumerate(chips[:2])]
            group.append(copy(0, w, me, sibling, src=srcs[w]))
            for cp in group:
                cp.start()
            first += group
        save(me)
        locs = [pltpu.make_async_copy(pb, slot(1, me), loc_sems.at[0]),
                pltpu.make_async_copy(wob, wout_o.at[my_idx], loc_sems.at[1]),
                pltpu.make_async_copy(wob, woutb_o, loc_sems.at[2])]
        for cp in locs:
            cp.start()

        fetch_x.wait()
        xv = xbuf[...]
        hv[...] = _bf(xv * lax.rsqrt(jnp.mean(xv * xv, axis=-1, keepdims=True) + EPS) * g_ref[...])
        rows = 256
        for r0 in range(0, T, rows):
            htv[:, r0:r0 + rows] = hv[r0:r0 + rows, :].T
        locs.append(pltpu.make_async_copy(htv, ht_o, loc_sems.at[3]))
        locs[-1].start()

        def out_copy(p, j):
            return pltpu.make_async_copy(stage.at[p], proj_o.at[j], out_sems.at[p])

        def project(nth, block):
            base = 3 * index(*block)

            def tile(p, carry):
                if nth > 0:
                    out_copy(p, base + p).wait()
                stage[p] = _dot(hv[...], wv[base + p], 1, 0)
                out_copy(p, base + p).start()
                return carry

            lax.fori_loop(0, 3, tile, 0)

        project(0, me)
        copy(0, 0, sibling, me).wait_recv()
        save(sibling)
        project(1, sibling)
        passed = []
        relay_from = (px ^ (1 - c), py ^ c, c)
        relay_to = (px ^ c, py ^ (1 - c), c)

        def arrived(w, j):
            copy(1 + j, w, (*chips[j], c), me).wait_recv()
            passed.append(copy(4 + j, w, (*chips[j], c), sibling))
            passed[-1].start()

        def relay(w):
            passed.append(copy(3, w, relay_from, relay_to))
            passed[-1].start()

        def handed(nth, j):
            copy(4 + j, 0, (*chips[j], 1 - c), me).wait_recv()
            save((*chips[j], 1 - c))
            project(nth, (*chips[j], 1 - c))

        arrived(0, 0)
        arrived(0, 1)
        relay(0)
        for j in range(2):
            save((*chips[j], c))
            project(2 + j, (*chips[j], c))
        handed(4, 0)
        handed(5, 1)
        arrived(1, 0)
        arrived(1, 1)
        relay(1)
        arrived(0, 2)
        save((*chips[2], c))
        project(6, (*chips[2], c))
        handed(7, 2)
        arrived(1, 2)
        copy(0, 1, sibling, me).wait_recv()
        for j, chip in enumerate(chips):
            copy(4 + j, 1, (*chip, 1 - c), me).wait_recv()
        keep = pltpu.make_async_copy(wv, wt_o, loc_sems.at[4])
        for p in range(3):
            out_copy(p, p).wait()
        for cp in first + passed:
            cp.wait_send()
        keep.wait()
        for cp in locs:
            cp.wait()

    vmem = pl.BlockSpec(memory_space=pltpu.VMEM)
    bf16 = jnp.bfloat16
    return pl.pallas_call(
        body, name="gather_proj",
        out_shape=(pltpu.HBM((D, T), bf16), pltpu.HBM((NTILE, D, TILE), bf16),
                   pltpu.HBM((DMIX // NDEV, D), bf16), pltpu.HBM((NDEV, DMIX // NDEV, D), bf16),
                   pltpu.HBM((NDEV, NGROUP, GROUP // NDEV, GROUP), bf16), pltpu.HBM((NTILE, T, TILE), jnp.float32)),
        in_specs=[ANY] + [vmem] * 4, out_specs=(ANY,) * 6,
        scratch_shapes=[pltpu.VMEM((T, D), jnp.float32),
                        pltpu.VMEM((T, D), bf16), pltpu.VMEM((D, T), bf16), pltpu.VMEM((NTILE, D, TILE), bf16),
                        pltpu.VMEM((DMIX // NDEV, D), bf16), pltpu.VMEM((NGROUP, GROUP // NDEV, GROUP), bf16),
                        pltpu.VMEM((3, T, TILE), jnp.float32),
                        pltpu.SemaphoreType.DMA((14,)), pltpu.SemaphoreType.DMA((14,)),
                        pltpu.SemaphoreType.DMA((6,)), pltpu.SemaphoreType.DMA((3,))],
        compiler_params=_params(vmem_mib=56),
    )(x, g1, w_in, w_out, pool_w)


def _split_start(name, arrays, n_copies, plan):
    k = len(arrays)

    def body(*refs):
        send_sems, recv_sems, token = refs[k], refs[k + 1], refs[-1]
        for i, (src, dst, to) in enumerate(plan(refs[:k])):
            pltpu.make_async_remote_copy(src_ref=src, dst_ref=dst, send_sem=send_sems.at[i],
                                         recv_sem=recv_sems.at[i], device_id=to, device_id_type=MESH).start()
        token[...] = jnp.zeros_like(token)

    out = pl.pallas_call(
        body, name=name,
        out_shape=(pltpu.SemaphoreType.DMA((n_copies,)), pltpu.SemaphoreType.DMA((n_copies,)),
                   *[pltpu.HBM(a.shape, a.dtype) for a in arrays], jax.ShapeDtypeStruct((8, 128), jnp.float32)),
        in_specs=[HBM] * k, out_specs=(SEM, SEM, *[HBM] * k, pl.BlockSpec(memory_space=pltpu.VMEM)),
        input_output_aliases={i: 2 + i for i in range(k)},
        compiler_params=pltpu.CompilerParams(has_side_effects=EFFECT),
    )(*[pltpu.with_memory_space_constraint(a, pltpu.HBM) for a in arrays])
    return out[0], out[1], out[2:2 + k], out[-1]


def _split_wait(name, arrays, send_sems, recv_sems, plan, after):
    k = len(arrays)

    def body(*refs):
        sends, recvs = refs[k], refs[k + 1]
        for i, (src, dst, to) in enumerate(plan(refs[:k])):
            cp = pltpu.make_async_remote_copy(src_ref=src, dst_ref=dst, send_sem=sends.at[i], recv_sem=recvs.at[i],
                                              device_id=to, device_id_type=MESH)
            cp.wait_send()
            cp.wait_recv()

    return pl.pallas_call(
        body, name=name,
        out_shape=tuple(pltpu.HBM(a.shape, a.dtype) for a in arrays),
        in_specs=[HBM] * k + [SEM, SEM, ANY], out_specs=(HBM,) * k,
        input_output_aliases={i: i for i in range(k)},
        compiler_params=pltpu.CompilerParams(has_side_effects=EFFECT),
    )(*arrays, send_sems, recv_sems, after)


def _plan_wout(refs):
    src, land = refs
    x, y, c, me = _place()
    return [(src, land.at[me], _peer(x, y, c, r)) for r in range(1, NDEV)]


def _plan_rest(refs):
    gob, gpf, r_out, r_pool = refs
    x, y, c, me = _place()
    plan = []
    for r in range(1, NDEV):
        plan.append((gob.at[me ^ r], r_out.at[r - 1], _peer(x, y, c, r)))
        plan.append((gpf.at[me ^ r], r_pool.at[r - 1], _peer(x, y, c, r)))
    return plan


def _plan_in(refs):
    sums, landing = refs
    x, y, c, _ = _place()
    plan = []
    for rel, (dx, dy) in enumerate(((1, 0), (0, 1), (1, 1))):
        for p in range(3):
            plan.append((sums.at[rel, p], landing.at[rel, p], (x ^ dx, y ^ dy, c)))
    return plan


def _plan_small(refs):
    small, land = refs
    x, y, c, me = _place()
    return [(small, land.at[me], _peer(x, y, c, r)) for r in range(1, NDEV)]


_POOL_SPECS = [pl.BlockSpec((None, T, GROUP), lambda g, base=base: (base + g, 0, 0)) for base in (0, 4)]
_HEAD_SPECS = [pl.BlockSpec((None, T, HEAD), lambda h, base=base: (base + h // 2, 0, h % 2))
               for base in (8, 12, 16, 20)]


_POOL_W_SPEC = pl.BlockSpec((NDEV, None, GROUP // NDEV, GROUP), lambda g: (0, g, 0, 0))


def _row_ids(shape):
    return lax.broadcasted_iota(jnp.int32, shape, 0)


BAND_ROWS = 128
HALO = 16


def _window_sum(a, gidx, lead):
    width = lax.shift_left(jnp.int32(2), gidx)
    shape = (BAND_ROWS, BAND_ROWS + HALO)
    t, j = lax.broadcasted_iota(jnp.int32, shape, 0), lax.broadcasted_iota(jnp.int32, shape, 1)
    first = t if lead else t + HALO - width + 1
    band = _bf(jnp.where(j >= first, jnp.where(j < first + width, 1.0, 0.0), 0.0))
    zeros = jnp.zeros((HALO, a.shape[1]), jnp.bfloat16)
    padded = [jnp.concatenate([p, zeros] if lead else [zeros, p], axis=0) for p in _split2(a)]
    out = []
    for r0 in range(0, T, BAND_ROWS):
        slab = jnp.concatenate([p[r0:r0 + BAND_ROWS + HALO] for p in padded], axis=1)
        r = _dot(band, slab, 1, 0)
        out.append(r[:, :a.shape[1]] + r[:, a.shape[1]:])
    return jnp.concatenate(out, axis=0)


def _window_mean(s, gidx):
    inv = jnp.where(gidx == 0, 0.5, jnp.where(gidx == 1, 0.25, jnp.where(gidx == 2, 0.125, 0.0625)))
    width = lax.shift_left(jnp.int32(2), gidx)
    head = s[:16] / jnp.minimum(_row_ids((16, s.shape[1])) + 1, width).astype(jnp.float32)
    return jnp.concatenate([head, s[16:] * inv], axis=0)


def _pool_fwd(proj, pool_w, pool_scale, token):
    def body(u_ref, pg_ref, w_ref, sc_ref, token_any, y_ref):
        del token_any
        gidx = pl.program_id(0)
        u, pg = u_ref[...], pg_ref[...]
        d = _window_mean(_window_sum(u, gidx, False), gidx) - u
        mixed = _dot(_bf(d), w_ref[...].reshape(GROUP, GROUP), 1, 0)
        y_ref[...] = _bf(mixed * sc_ref[...] * (pg * _sigmoid(pg)))

    return pl.pallas_call(
        body, name="pool_fwd", grid=(NGROUP,),
        in_specs=[*_POOL_SPECS, _POOL_W_SPEC, pl.BlockSpec((1, GROUP), lambda g: (0, g)), ANY],
        out_specs=pl.BlockSpec((T, GROUP), lambda g: (0, g)),
        out_shape=pltpu.HBM((T, DMIX), jnp.bfloat16),
        compiler_params=_params(("parallel",)),
    )(proj, proj, pool_w, pool_scale, token)


def _tri(lower):
    r = lax.broadcasted_iota(jnp.int32, (CHUNK, CHUNK), 0)
    c = lax.broadcasted_iota(jnp.int32, (CHUNK, CHUNK), 1)
    return (r >= c) if lower else (r <= c)


def _sum_rows_matrix():
    shape = (CHUNK + 16, CHUNK)
    r, c = lax.broadcasted_iota(jnp.int32, shape, 0), lax.broadcasted_iota(jnp.int32, shape, 1)
    run = jnp.where(c <= r, 1.0, 0.0)
    half = jnp.where(c < CHUNK // 2, 1.0, 0.0)
    return _bf(jnp.where(r < CHUNK, run, jnp.where(r < CHUNK + 8, 1.0, half)))


def _rev_sum_matrix():
    shape = (CHUNK, 2 * CHUNK)
    r, c = lax.broadcasted_iota(jnp.int32, shape, 0), lax.broadcasted_iota(jnp.int32, shape, 1)
    return _bf(jnp.where(c < CHUNK, jnp.where(c >= r, 1.0, 0.0), jnp.where(c - CHUNK < r, 1.0, 0.0)))


def _split2(a):
    hi = _bf(a)
    return [hi, _bf(a - hi.astype(jnp.float32))]


def _exact_sums(mat, pieces):
    x = jnp.concatenate([s for p in pieces for s in _split2(p)], axis=1)
    r = _dot(mat, x, 1, 0)
    return [r[:, 2 * j * HEAD:(2 * j + 1) * HEAD] + r[:, (2 * j + 1) * HEAD:(2 * j + 2) * HEAD]
            for j in range(len(pieces))]


def _gates(qv, fl, lb):
    sq = _sigmoid(qv)
    sg = _sigmoid(fl)
    f = lb + (1.0 - lb) * sg
    return dict(sq=sq, qs=qv * sq, sg=sg, f=f, kk=1.0 - f, g=jnp.log2(f))


def _decays(sums):
    big_g = sums[:CHUNK]
    total = sums[CHUNK:CHUNK + 8]
    g_last = jnp.tile(total, (CHUNK // 8, 1))
    g_mid = jnp.tile(sums[CHUNK + 8:], (CHUNK // 8, 1))
    return dict(
        e_q=jnp.exp2(big_g),
        e_k=jnp.exp2(g_last - big_g),
        e_qm=jnp.exp2(jnp.minimum(big_g - g_mid, EXP_CAP)),
        e_km=jnp.exp2(jnp.minimum(g_mid - big_g, EXP_CAP)),
        total8=jnp.exp2(total))


def _group_rows(gi):
    return [pl.ds(pl.multiple_of((gi * NB + j) * CHUNK, CHUNK), CHUNK) for j in range(NB)]


def _lower_bound(lb_ref):
    return _sigmoid(lb_ref[0:1, :] - lb_ref[1:2, :])


def _hgrn_fwd(proj, lb_logits, rec_g, y_in):
    def body(q_ref, f_ref, i_ref, gate_ref, lb_ref, rg_ref, y_any, y_ref, o_ref, st_ref):
        del y_any
        lb = _lower_bound(lb_ref)
        causal = _tri(True)
        smat = _sum_rows_matrix()

        def group(gi, st):
            rows = _group_rows(gi)
            ts = [_gates(q_ref[r, :], f_ref[r, :], lb) for r in rows]
            ds = [_decays(s) for s in _exact_sums(smat, [t["g"] for t in ts])]
            vs = [_bf(i_ref[r, :]) for r in rows]
            q_m = [_bf(t["qs"] * d["e_qm"]) for t, d in zip(ts, ds)]
            k_m = [_bf(t["kk"] * d["e_km"]) for t, d in zip(ts, ds)]
            q_e = [_bf(t["qs"] * d["e_q"]) for t, d in zip(ts, ds)]
            k_e = [_bf(t["kk"] * d["e_k"]) for t, d in zip(ts, ds)]
            a = [_bf(jnp.where(causal, _dot(q_m[j], k_m[j], 1, 1), 0.0)) for j in range(NB)]
            intra = [_dot(a[j], vs[j], 1, 0) for j in range(NB)]
            upd = [_dot(vs[j], k_e[j], 0, 0) for j in range(NB)]
            for j in range(NB):
                st_ref[gi * NB + j] = st
                o_ref[rows[j], :] = intra[j] + _dot(q_e[j], _bf(st), 1, 1)
                st = st * jnp.tile(ds[j]["total8"], (HEAD // 8, 1)) + upd[j]
            return st

        lax.fori_loop(0, NGRP, group, jnp.zeros((HEAD, HEAD), jnp.float32))
        o = o_ref[...]
        rn = o * lax.rsqrt(jnp.mean(o * o, axis=-1, keepdims=True) + EPS)
        gate = gate_ref[...]
        y_ref[...] = _bf(rn * rg_ref[...] * (gate * _sigmoid(gate)))

    return pl.pallas_call(
        body, name="hgrn_fwd", grid=(NHEAD,),
        in_specs=[*_HEAD_SPECS,
                  pl.BlockSpec((2, HEAD), lambda h: (0, h)),
                  pl.BlockSpec((1, HEAD), lambda h: (0, h)),
                  pl.BlockSpec(memory_space=pl.ANY)],
        out_specs=(pl.BlockSpec((T, HEAD), lambda h: (0, NHEAD + h)),
                   pl.BlockSpec((T, HEAD), lambda h: (0, h)),
                   pl.BlockSpec((None, NCHUNK, HEAD, HEAD), lambda h: (h, 0, 0, 0))),
        out_shape=(pltpu.HBM((T, DMIX), jnp.bfloat16), pltpu.HBM((T, D), jnp.float32),
                   pltpu.HBM((NHEAD, NCHUNK, HEAD, HEAD), jnp.float32)),
        input_output_aliases={6: 0},
        compiler_params=_params(("parallel",)),
    )(proj, proj, proj, proj, lb_logits, rec_g, y_in)


def _out_proj_loss(x, y, w_out, target, gf):
    rows = 512
    parts = [slice(k * rows // 2, (k + 1) * rows // 2) for k in range(2)]

    def body(x_ref, y_ref, w_ref, t_ref, g_ref, dz_ref, dzb_ref, sq_ref, dg_ref):
        zs = [x_ref[p, :] + _dot(y_ref[p, :], w_ref[...], 1, 0) for p in parts]
        sq = dg = 0.0
        for p, z in zip(parts, zs):
            r = lax.rsqrt(jnp.mean(z * z, axis=-1, keepdims=True) + EPS)
            zhat = z * r
            err = zhat * g_ref[...] - t_ref[p, :]
            dy = err * (1.0 / D)
            gdy = dy * g_ref[...]
            dz = r * (gdy - zhat * jnp.mean(zhat * gdy, axis=-1, keepdims=True))
            dz_ref[p, :] = dz
            dzb_ref[p, :] = _bf(dz)
            sq = sq + jnp.sum(err * err, axis=0, keepdims=True)
            dg = dg + jnp.sum(zhat * dy, axis=0, keepdims=True)

        @pl.when(pl.program_id(0) == 0)
        def _():
            sq_ref[...] = sq
            dg_ref[...] = dg

        @pl.when(pl.program_id(0) != 0)
        def _():
            sq_ref[...] += sq
            dg_ref[...] += dg

    tile = pl.BlockSpec((rows, D), lambda i: (i, 0))
    vec = pl.BlockSpec((1, D), lambda i: (0, 0))
    return pl.pallas_call(
        body, name="out_proj_loss", grid=(T // rows,),
        in_specs=[tile, pl.BlockSpec((rows, DMIX), lambda i: (i, 0)), pl.BlockSpec((DMIX, D), lambda i: (0, 0)),
                  tile, vec],
        out_specs=(tile, tile, vec, vec),
        out_shape=(pltpu.HBM((T, D), jnp.float32), pltpu.HBM((T, D), jnp.bfloat16),
                   pltpu.HBM((1, D), jnp.float32), pltpu.HBM((1, D), jnp.float32)),
        compiler_params=_params(("arbitrary",)),
    )(x, y, w_out, target, gf)


def _out_proj_bwd(dzb, w_out, y):
    tn = 512

    def body(dz_ref, w_ref, y_ref, dy_ref, gw_ref, gwb_ref):
        dz = dz_ref[...]
        dy_ref[...] = _dot(dz, w_ref[...], 1, 1)
        gw = _dot(y_ref[...], dz, 0, 0)
        gw_ref[...] = gw
        gwb_ref[...] = _bf(gw)

    return pl.pallas_call(
        body, name="out_proj_bwd", grid=(DMIX // tn,),
        in_specs=[pl.BlockSpec((T, D), lambda n: (0, 0)), pl.BlockSpec((tn, D), lambda n: (n, 0)),
                  pl.BlockSpec((T, tn), lambda n: (0, n))],
        out_specs=(pl.BlockSpec((T, tn), lambda n: (0, n)), pl.BlockSpec((tn, D), lambda n: (n, 0)),
                   pl.BlockSpec((tn, D), lambda n: (n, 0))),
        out_shape=(pltpu.HBM((T, DMIX), jnp.float32), pltpu.HBM((DMIX, D), jnp.float32),
                   pltpu.HBM((DMIX, D), jnp.bfloat16)),
        compiler_params=_params(("parallel",)),
    )(dzb, w_out, y)


def _hgrn_bwd(proj, lb_logits, rec_g, o, states, dymix, dproj_in, token):
    def body(q_ref, f_ref, i_ref, gate_ref, lb_ref, rg_ref, o_ref, st_ref, dy_ref, dp_any, token_any,
             dp_ref, drg_ref, dlb_ref, do_ref):
        del dp_any, token_any
        lb = _lower_bound(lb_ref)
        causal = _tri(True)
        smat, rmat = _sum_rows_matrix(), _rev_sum_matrix()

        o = o_ref[...]
        rs = lax.rsqrt(jnp.mean(o * o, axis=-1, keepdims=True) + EPS)
        rn = o * rs
        gate = gate_ref[...]
        sgate = _sigmoid(gate)
        dyv = dy_ref[...]
        d_r = dyv * (gate * sgate)
        dp_ref[3] = _bf(dyv * (rn * rg_ref[...]) * (sgate * (1.0 + gate * (1.0 - sgate))))
        drg_ref[...] = jnp.sum(d_r * rn, axis=0, keepdims=True)
        drn = d_r * rg_ref[...]
        do_ref[...] = rs * (drn - rn * jnp.mean(rn * drn, axis=-1, keepdims=True))

        def group(i, carry):
            dst, dlb = carry
            gi = NGRP - 1 - i
            rows = _group_rows(gi)
            span = range(NB)
            qvs = [q_ref[r, :] for r in rows]
            ts = [_gates(qv, f_ref[r, :], lb) for qv, r in zip(qvs, rows)]
            ds = [_decays(s) for s in _exact_sums(smat, [t["g"] for t in ts])]
            vs = [_bf(i_ref[r, :]) for r in rows]
            dos = [_bf(do_ref[r, :]) for r in rows]
            sts = [st_ref[gi * NB + j] for j in span]
            qe_f = [t["qs"] * d["e_q"] for t, d in zip(ts, ds)]
            ke_f = [t["kk"] * d["e_k"] for t, d in zip(ts, ds)]
            q_e, k_e = [_bf(a) for a in qe_f], [_bf(a) for a in ke_f]
            q_m = [_bf(t["qs"] * d["e_qm"]) for t, d in zip(ts, ds)]
            k_m = [_bf(t["kk"] * d["e_km"]) for t, d in zip(ts, ds)]
            a = [_bf(jnp.where(causal, _dot(q_m[j], k_m[j], 1, 1), 0.0)) for j in span]
            da = [_bf(jnp.where(causal, _dot(dos[j], vs[j], 1, 1), 0.0)) for j in span]
            dqm = [_dot(da[j], k_m[j], 1, 0) for j in span]
            dkm = [_dot(da[j], q_m[j], 0, 0) for j in span]
            dv_in = [_dot(a[j], dos[j], 0, 0) for j in span]
            dqe = [_dot(dos[j], _bf(sts[j]), 1, 0) for j in span]
            grow = [_dot(dos[j], q_e[j], 0, 0) for j in span]
            dke, carried = [None] * NB, [None] * NB
            for j in reversed(span):
                dst_b = _bf(dst)
                dke[j] = _dot(vs[j], dst_b, 1, 0)
                dp_ref[2, rows[j], :] = _bf(dv_in[j] + _dot(k_e[j], dst_b, 1, 1))
                carried[j] = ds[j]["total8"] * jnp.sum(dst * sts[j], axis=0, keepdims=True)
                dst = dst * jnp.tile(ds[j]["total8"], (HEAD // 8, 1)) + grow[j]
            kdk = [ke_f[j] * dke[j] for j in span]
            pos = [(q_m[j].astype(jnp.float32) * dqm[j] - k_m[j].astype(jnp.float32) * dkm[j]) + qe_f[j] * dqe[j]
                   for j in span]
            dgs = _exact_sums(rmat, [jnp.concatenate([pos[j], kdk[j]], axis=0) for j in span])
            for j in span:
                t, d = ts[j], ds[j]
                dg = dgs[j] + jnp.tile(carried[j], (CHUNK // 8, 1))
                dqs = dqm[j] * d["e_qm"] + dqe[j] * d["e_q"]
                dkk = dkm[j] * d["e_km"] + dke[j] * d["e_k"]
                df = dg / t["f"] - dkk
                dp_ref[1, rows[j], :] = _bf(df * (1.0 - lb) * (t["sg"] * (1.0 - t["sg"])))
                dp_ref[0, rows[j], :] = _bf(dqs * (t["sq"] * (1.0 + qvs[j] * (1.0 - t["sq"]))))
                dlb = dlb + df * (1.0 - t["sg"])
            return dst, dlb

        _, dlb = lax.fori_loop(0, NGRP, group, (jnp.zeros((HEAD, HEAD), jnp.float32),
                                                jnp.zeros((CHUNK, HEAD), jnp.float32)))
        dlb_ref[...] = jnp.sum(dlb, axis=0, keepdims=True)

    vec = pl.BlockSpec((1, HEAD), lambda h: (0, h))
    return pl.pallas_call(
        body, name="hgrn_bwd", grid=(NHEAD,),
        in_specs=[*_HEAD_SPECS,
                  pl.BlockSpec((2, HEAD), lambda h: (0, h)), vec,
                  pl.BlockSpec((T, HEAD), lambda h: (0, h)),
                  pl.BlockSpec((None, NCHUNK, HEAD, HEAD), lambda h: (h, 0, 0, 0)),
                  pl.BlockSpec((T, HEAD), lambda h: (0, NHEAD + h)), ANY, ANY],
        out_specs=(pl.BlockSpec((4, T, HEAD), lambda h: (0, 0, h)), vec, vec),
        out_shape=(pltpu.HBM((NSEG, T, D), jnp.bfloat16),
                   pltpu.HBM((1, D), jnp.float32), pltpu.HBM((1, D), jnp.float32)),
        scratch_shapes=[pltpu.VMEM((T, HEAD), jnp.float32)],
        input_output_aliases={9: 0},
        compiler_params=_params(("parallel",)),
    )(proj, proj, proj, proj, lb_logits, rec_g, o, states, dymix, dproj_in, token)


def _pool_bwd(proj, pool_w, pool_scale, dymix):
    def body(u_ref, pg_ref, w_ref, sc_ref, dy_ref, dp_ref, gw_ref, gs_ref):
        gidx = pl.program_id(0)
        u, pg = u_ref[...], pg_ref[...]
        w = w_ref[...].reshape(GROUP, GROUP)
        d = _bf(_window_mean(_window_sum(u, gidx, False), gidx) - u)
        mixed = _dot(d, w, 1, 0)
        spg = _sigmoid(pg)
        dyv = dy_ref[...]
        d_p = dyv * (pg * spg)
        dp_ref[1] = _bf(dyv * (mixed * sc_ref[...]) * (spg * (1.0 + pg * (1.0 - spg))))
        gs_ref[...] = jnp.sum(d_p * mixed, axis=0, keepdims=True)
        dmixed = _bf(d_p * sc_ref[...])
        gw_ref[...] = _dot(d, dmixed, 0, 0).reshape(gw_ref.shape)
        dd = _dot(dmixed, w, 1, 1)
        dp_ref[0] = _bf(_window_sum(_window_mean(dd, gidx), gidx, True) - dd)

    return pl.pallas_call(
        body, name="pool_bwd", grid=(NGROUP,),
        in_specs=[*_POOL_SPECS, _POOL_W_SPEC,
                  pl.BlockSpec((1, GROUP), lambda g: (0, g)),
                  pl.BlockSpec((T, GROUP), lambda g: (0, g))],
        out_specs=(pl.BlockSpec((2, T, GROUP), lambda g: (2, 0, g)), _POOL_W_SPEC,
                   pl.BlockSpec((1, GROUP), lambda g: (0, g))),
        out_shape=(pltpu.HBM((NSEG, T, D), jnp.bfloat16),
                   pltpu.HBM((NDEV, NGROUP, GROUP // NDEV, GROUP), jnp.float32),
                   pltpu.HBM((1, D), jnp.float32)),
        compiler_params=_params(("parallel",)),
    )(proj, proj, pool_w, pool_scale, dymix)


HALF = NTILE // 2
AWAY = HALF - 3


def _dproj_tile(chip, side, p):
    j = 6 * chip + 3 * side + p
    return ((j // 4 + 4) % NSEG, 0, j % 4)


def _sibling_copy(sib_out, sib_in, send_sems, recv_sems, slot):
    x, y, c, _ = _place()
    return pltpu.make_async_remote_copy(
        src_ref=sib_out.at[slot], dst_ref=sib_in.at[slot], send_sem=send_sems.at[slot],
        recv_sem=recv_sems.at[slot], device_id=(x, y, 1 - c), device_id_type=MESH)


def _proj_bwd_w_far(place, ht, dproj):
    per = 4

    def body(place_ref, h_ref, *refs):
        del place_ref
        sib_out, sib_in, send_sems, recv_sems, stage, loc_sems = refs[per:]
        i = pl.program_id(0)

        def to_hbm(k):
            return pltpu.make_async_copy(stage.at[k], sib_out.at[k], loc_sems.at[k])

        def send(k):
            to_hbm(k).wait()
            _sibling_copy(sib_out, sib_in, send_sems, recv_sems, k).start()

        for j, dp_ref in enumerate(refs[:per]):
            stage[per * i + j] = _bf(_dot(h_ref[...], dp_ref[...], 1, 0))

        @pl.when(i > 0)
        def _():
            for j in range(per):
                send(per * (i - 1) + j)

        for j in range(per):
            to_hbm(per * i + j).start()

        @pl.when(i == HALF // per - 1)
        def _():
            for j in range(per):
                send(per * i + j)

    def tile(k, pr):
        return _dproj_tile(k // 3, 1 - pr[2], k % 3)

    buf = pltpu.HBM((HALF, D, TILE), jnp.bfloat16)
    sems = pltpu.SemaphoreType.DMA((HALF,))
    return pl.pallas_call(
        body, name="proj_bwd_w_far",
        grid_spec=pltpu.PrefetchScalarGridSpec(
            num_scalar_prefetch=1, grid=(HALF // per,),
            in_specs=[pl.BlockSpec((D, T), lambda i, pr: (0, 0)),
                      *[pl.BlockSpec((None, T, TILE), lambda i, pr, j=j: tile(per * i + j, pr)) for j in range(per)]],
            out_specs=(HBM, HBM, SEM, SEM),
            scratch_shapes=[pltpu.VMEM((HALF, D, TILE), jnp.bfloat16), pltpu.SemaphoreType.DMA((HALF,))]),
        out_shape=(buf, buf, sems, sems),
        compiler_params=pltpu.CompilerParams(dimension_semantics=("arbitrary",), vmem_limit_bytes=48 * MIB,
                                             has_side_effects=EFFECT),
    )(place, ht, *[dproj] * per)


def _proj_bwd_w_near(place, ht, dproj, sib_out, sib_in, sib_send, sib_recv):
    def owner_chip(k, pr):
        return jnp.where(k < AWAY, (pr[1] + 1 + k % 3) % 4, pr[1])

    def tile_p(k):
        return jnp.where(k < AWAY, k // 3, k - AWAY)

    def body(place_ref, h_ref, dp_ref, sib_out, sib_in, sib_send, sib_recv, sums, own_ref, landing, out_send,
             out_recv, recvbuf, outbuf, in_sems, loc_sems):
        i = pl.program_id(0)
        px, py, c, _ = _place()

        def slot_of(k):
            return 3 * owner_chip(k, place_ref) + tile_p(k)

        def load(k):
            return pltpu.make_async_copy(sib_in.at[slot_of(k)], recvbuf.at[k % 2], in_sems.at[k % 2])

        def fetch(k):
            _sibling_copy(sib_out, sib_in, sib_send, sib_recv, slot_of(k)).wait_recv()
            load(k).start()

        def route(k):
            chip = owner_chip(k, place_ref)
            cx, cy = chip // 2, chip % 2
            return cx, cy, (cx ^ px) + 2 * (cy ^ py) - 1, tile_p(k)

        def to_hbm(k):
            _, _, rel, p = route(k)
            return pltpu.make_async_copy(outbuf.at[k], sums.at[rel, p], loc_sems.at[k])

        def to_owner(k):
            cx, cy, rel, p = route(k)
            return pltpu.make_async_remote_copy(
                src_ref=sums.at[rel, p], dst_ref=landing.at[rel, p], send_sem=out_send.at[3 * rel + p],
                recv_sem=out_recv.at[3 * rel + p], device_id=(cx, cy, c), device_id_type=MESH)

        @pl.when(i == 0)
        def _():
            fetch(i)

        @pl.when(i < HALF - 1)
        def _():
            fetch(i + 1)

        load(i).wait()
        total = _dot(h_ref[...], dp_ref[...], 1, 0) + recvbuf[i % 2].astype(jnp.float32)
        own_ref[...] = total
        outbuf[jnp.minimum(i, AWAY)] = _bf(total)

        @pl.when(i < AWAY)
        def _():
            to_hbm(i).start()

        @pl.when(jnp.logical_and(i > 0, i <= AWAY))
        def _():
            to_hbm(i - 1).wait()
            to_owner(i - 1).start()

        @pl.when(i == HALF - 1)
        def _():
            for slot in range(HALF):
                _sibling_copy(sib_out, sib_in, sib_send, sib_recv, slot).wait_send()

    travelling = pltpu.HBM((3, 3, D, TILE), jnp.bfloat16)
    sems = pltpu.SemaphoreType.DMA((AWAY,))
    return pl.pallas_call(
        body, name="proj_bwd_w_near",
        grid_spec=pltpu.PrefetchScalarGridSpec(
            num_scalar_prefetch=1, grid=(HALF,),
            in_specs=[pl.BlockSpec((D, T), lambda i, pr: (0, 0)),
                      pl.BlockSpec((None, T, TILE), lambda i, pr: _dproj_tile(owner_chip(i, pr), pr[2], tile_p(i))),
                      HBM, HBM, SEM, SEM],
            out_specs=(HBM, pl.BlockSpec((None, D, TILE), lambda i, pr: (jnp.where(i < AWAY, 0, i % 3), 0, 0)),
                       HBM, SEM, SEM),
            scratch_shapes=[pltpu.VMEM((2, D, TILE), jnp.bfloat16), pltpu.VMEM((AWAY + 1, D, TILE), jnp.bfloat16),
                            pltpu.SemaphoreType.DMA((2,)), pltpu.SemaphoreType.DMA((AWAY,))]),
        out_shape=(travelling, pltpu.HBM((3, D, TILE), jnp.float32), travelling, sems, sems),
        compiler_params=pltpu.CompilerParams(dimension_semantics=("arbitrary",), vmem_limit_bytes=48 * MIB,
                                             has_side_effects=EFFECT),
    )(place, ht, dproj, sib_out, sib_in, sib_send, sib_recv)


def _proj_bwd_x(dproj, w_t, x, g1, dz, others, token):
    tm = 1024
    pairs = NSEG // 2
    k = len(others)

    def body(dp_ref, w_ref, x_any, g_ref, dz_any, *refs):
        dx_ref, dg_ref, wcat, acc, xbuf, dzbuf, row_sems = refs[k + 1:]
        m, s = pl.program_id(0), pl.program_id(1)
        mine = pl.ds(pl.multiple_of(m * tm, tm), tm)
        fetches = [pltpu.make_async_copy(x_any.at[mine], xbuf, row_sems.at[0]),
                   pltpu.make_async_copy(dz_any.at[mine], dzbuf, row_sems.at[1])]

        @pl.when(m == 0)
        def _():
            for i in range(8):
                wcat[s, :, i * TILE:(i + 1) * TILE] = w_ref[i]

        @pl.when(s == 0)
        def _():
            for cp in fetches:
                cp.start()
            acc[...] = jnp.zeros((tm, D), jnp.float32)

        acc[...] += _dot(jnp.concatenate([dp_ref[0], dp_ref[1]], axis=1), wcat[s], 1, 1)

        @pl.when(s == pairs - 1)
        def _():
            for cp in fetches:
                cp.wait()
            xv = xbuf[...]
            rs = lax.rsqrt(jnp.mean(xv * xv, axis=-1, keepdims=True) + EPS)
            xhat = xv * rs
            dhv = acc[...]
            gdh = dhv * g_ref[...]
            dx_ref[...] = dzbuf[...] + rs * (gdh - xhat * jnp.mean(xhat * gdh, axis=-1, keepdims=True))
            dg = jnp.sum(xhat * dhv, axis=0, keepdims=True)

            @pl.when(m == 0)
            def _():
                dg_ref[0:1, :] = dg
                for r, ref in enumerate(refs[:k]):
                    dg_ref[1 + r:2 + r, :] = ref[...]
                dg_ref[1 + k:, :] = jnp.zeros((7 - k, D), jnp.float32)

            @pl.when(m != 0)
            def _():
                dg_ref[0:1, :] += dg

    rows = pl.BlockSpec((tm, D), lambda m, s: (m, 0))
    vec = pl.BlockSpec((1, D), lambda m, s: (0, 0))
    return pl.pallas_call(
        body, name="proj_bwd_x", grid=(T // tm, pairs),
        in_specs=[pl.BlockSpec((2, tm, D), lambda m, s: (s, m, 0)),
                  pl.BlockSpec((8, D, TILE), lambda m, s: ((jnp.where(m == 0, s, pairs - 1) + 1) % pairs, 0, 0)),
                  HBM, vec, HBM, *[vec] * k, ANY],
        out_specs=(rows, pl.BlockSpec((8, D), lambda m, s: (0, 0))),
        out_shape=(jax.ShapeDtypeStruct((T, D), jnp.float32), pltpu.HBM((8, D), jnp.float32)),
        scratch_shapes=[pltpu.VMEM((pairs, D, 2 * D), jnp.bfloat16), pltpu.VMEM((tm, D), jnp.float32),
                        pltpu.VMEM((tm, D), jnp.float32), pltpu.VMEM((tm, D), jnp.float32),
                        pltpu.SemaphoreType.DMA((2,))],
        compiler_params=_params(("arbitrary", "arbitrary"), vmem_mib=58),
    )(dproj, w_t, _in_hbm(x), g1, _in_hbm(dz), *[_in_hbm(a) for a in others], token)


def _adamw(w, g, m, v):
    m_new = ADAM_B1 * m + (1.0 - ADAM_B1) * g
    v_new = ADAM_B2 * v + (1.0 - ADAM_B2) * (g * g)
    delta = -ADAM_LR * ((m_new / BC1) / (jnp.sqrt(v_new / BC2) + ADAM_EPS) + ADAM_WD * w)
    return delta, m_new, v_new


def _reduce_adam(name, place, parts, w, m, v, grid, w_spec):
    n = len(parts)

    def body(place_ref, *refs):
        del place_ref
        w_ref, m_ref, v_ref, g_ref, d_ref, mo_ref, vo_ref = refs[n:]
        g = None
        for ref, (_, _, stacked) in zip(refs[:n], parts):
            terms = [ref[r] for r in range(ref.shape[0])] if stacked else [ref[...]]
            for t in terms:
                if t.shape[-1] != w_ref.shape[-1]:
                    t = jnp.concatenate([t[p] for p in range(t.shape[0])], axis=1)
                g = t.astype(jnp.float32) if g is None else g + t.astype(jnp.float32)
        delta, m_new, v_new = _adamw(w_ref[...], g, m_ref[...], v_ref[...])
        g_ref[...] = g
        d_ref[...] = delta
        mo_ref[...] = m_new
        vo_ref[...] = v_new

    shape = jax.ShapeDtypeStruct(w.shape, jnp.float32)
    return pl.pallas_call(
        body, name=name,
        grid_spec=pltpu.PrefetchScalarGridSpec(
            num_scalar_prefetch=1, grid=grid,
            in_specs=[spec for _, spec, _ in parts] + [w_spec] * 3, out_specs=(w_spec,) * 4),
        out_shape=(shape,) * 4,
        compiler_params=_params(("parallel",)),
    )(place, *[_in_hbm(a) for a in [a for a, _, _ in parts] + [w, m, v]])


SMALL_ROWS = (1, 1, 2, 1, 1)


def _small_adam(place, own, parts, ws, ms, vs):
    n = len(SMALL_ROWS)

    def body(place_ref, own_ref, p_ref, *refs):
        ins, outs, bufs = refs[:3 * n], refs[3 * n:3 * n + 1 + 4 * n], refs[3 * n + 1 + 4 * n:]

        def stacked(group, buf):
            r0 = 0
            for ref, k in zip(group, SMALL_ROWS):
                buf[r0:r0 + k, :] = ref[...]
                r0 += k
            buf[r0:, :] = jnp.zeros((8 - r0, D), jnp.float32)
            return buf[...]

        wv, mv, vv = (stacked(ins[n * j:n * j + n], bufs[j]) for j in range(3))
        me = place_ref[0]
        g = None
        for s in range(NDEV):
            term = jnp.where(me == s, own_ref[...], p_ref[s])
            g = term if g is None else g + term
        rows = _row_ids(wv.shape)
        other = jnp.where(rows == 2, pltpu.roll(wv, 7, 0), jnp.where(rows == 3, pltpu.roll(wv, 1, 0), 0.0))
        lbv = _sigmoid(wv - other)
        sign = jnp.where(rows == 2, 1.0, -1.0)
        g = jnp.where((rows == 2) | (rows == 3), sign * g * lbv * (1.0 - lbv), g)
        delta, m_new, v_new = _adamw(wv, g, mv, vv)
        outs[0][...] = jnp.sum(g[6:7], axis=1, keepdims=True) * (0.5 / D)
        for j, val in enumerate((g, delta, m_new, v_new)):
            r0 = 0
            for ref, k in zip(outs[1 + n * j:1 + n * j + n], SMALL_ROWS):
                ref[...] = val[r0:r0 + k]
                r0 += k

    vmem = pl.BlockSpec(memory_space=pltpu.VMEM)
    shapes = [jax.ShapeDtypeStruct((k, D), jnp.float32) for k in SMALL_ROWS]
    out = pl.pallas_call(
        body, name="small_adam", out_shape=(jax.ShapeDtypeStruct((1, 1), jnp.float32), *shapes * 4),
        in_specs=[pl.BlockSpec(memory_space=pltpu.SMEM)] + [vmem] * (2 + 3 * n), out_specs=(vmem,) * (1 + 4 * n),
        scratch_shapes=[pltpu.VMEM((8, D), jnp.float32)] * 3,
    )(place, own, parts, *ws, *ms, *vs)
    return out[0], [out[1 + n * j:1 + n * j + n] for j in range(4)]


def kernel(x, norm1_g, w_in, pool_w, pool_scale, lb_logits, rec_norm_g, w_out, final_norm_g, loss_target, m_norm1_g, m_w_in, m_pool_w, m_pool_scale, m_lb_logits, m_rec_norm_g, m_w_out, m_final_norm_g, v_norm1_g, v_w_in, v_pool_w, v_pool_scale, v_lb_logits, v_rec_norm_g, v_w_out, v_final_norm_g):
    xs = x[0]
    target = loss_target[0]
    ix, iy, ic = lax.axis_index("x"), lax.axis_index("y"), lax.axis_index("c")
    place = jnp.stack([4 * ix + 2 * iy + ic, 2 * ix + iy, ic]).astype(jnp.int32)
    gf = final_norm_g.reshape(1, D)

    ht, w_t, w_out_b, w_out_g, pool_g, proj = _gather_proj(xs, norm1_g, w_in, w_out, pool_w)
    wout = [w_out_b, w_out_g]
    wout_send, wout_recv, wout, wout_token = _split_start("gather_wout_start", wout, NDEV - 1, _plan_wout)

    y = _pool_fwd(proj, pool_g, pool_scale, wout_token)
    y, o, states = _hgrn_fwd(proj, lb_logits, rec_norm_g, y)
    _, w_out_g = _split_wait("gather_wout_wait", wout, wout_send, wout_recv, _plan_wout, o)
    w_out_full = _in_hbm(w_out_g.reshape(DMIX, D))
    dz, dzb, sq, dgf = _out_proj_loss(xs, y, w_out_full, target, gf)

    dymix, gwout_f, gwout_b = _out_proj_bwd(dzb, w_out_full, y)
    dproj, gpool, dscale = _pool_bwd(proj, pool_g, pool_scale, dymix)

    blk_out = (NDEV, DMIX // NDEV, D)
    blk_pool = (NDEV, NGROUP, GROUP // NDEV, GROUP)
    rest = [gwout_b.reshape(blk_out), gpool,
            lax.empty((NDEV - 1,) + blk_out[1:], jnp.bfloat16), lax.empty((NDEV - 1,) + blk_pool[1:], jnp.float32)]
    rest_send, rest_recv, rest, rest_token = _split_start("scatter_rest_start", rest, 2 * (NDEV - 1), _plan_rest)

    dproj, drecg, dlb = _hgrn_bwd(proj, lb_logits, rec_norm_g, o, states, dymix, dproj, rest_token)
    chip_sums, own_sum, landing, win_send, win_recv = _proj_bwd_w_near(
        place, ht, dproj, *_proj_bwd_w_far(place, ht, dproj))
    win = [chip_sums, landing]

    others = [a.reshape(1, D) for a in (dscale, dlb, dlb, drecg, dgf, sq)]
    grad_x, small_block = _proj_bwd_x(dproj, w_t, xs, norm1_g, dz, others, chip_sums)

    small = [small_block, lax.empty((NDEV, 8, D), jnp.float32)]
    small_send, small_recv, small, small_token = _split_start("gather_small_start", small, NDEV - 1, _plan_small)

    _, gpool_own, r_out, r_pool = _split_wait("scatter_rest_wait", rest, rest_send, rest_recv, _plan_rest,
                                              small_token)
    g_wout, d_wout, m_wout, v_wout = _reduce_adam(
        "adam_w_out", place,
        [(gwout_f.reshape(blk_out), pl.BlockSpec((None,) + blk_out[1:], lambda i, pr: (pr[0], 0, 0)), False),
         (r_out, pl.BlockSpec((NDEV - 1,) + blk_out[1:], lambda i, pr: (0, 0, 0)), True)],
        w_out, m_w_out, v_w_out, (1,), pl.BlockSpec((None,) + blk_out[1:], lambda i, pr: (0, 0, 0)))
    g_pool, d_pool, m_pool, v_pool = _reduce_adam(
        "adam_pool_w", place,
        [(gpool_own, pl.BlockSpec((None,) + blk_pool[1:], lambda i, pr: (pr[0], 0, 0, 0)), False),
         (r_pool, pl.BlockSpec((NDEV - 1,) + blk_pool[1:], lambda i, pr: (0, 0, 0, 0)), True)],
        pool_w, m_pool_w, v_pool_w, (1,), pl.BlockSpec((None,) + blk_pool[1:], lambda i, pr: (0, 0, 0, 0)))

    _, r_in = _split_wait("scatter_win_wait", win, win_send, win_recv, _plan_in, d_pool)
    g_win, d_win, m_win, v_win = _reduce_adam(
        "adam_w_in", place,
        [(own_sum, pl.BlockSpec((3, D // 8, TILE), lambda i, pr: (0, i, 0)), False),
         (r_in, pl.BlockSpec((3, 3, D // 8, TILE), lambda i, pr: (0, 0, i, 0)), True)],
        w_in, m_w_in, v_w_in, (8,), pl.BlockSpec((None, D // 8, 3 * TILE), lambda i, pr: (0, i, 0)))

    own_small, r_small = _split_wait("gather_small_wait", small, small_send, small_recv, _plan_small, d_win)
    loss, (g_s, d_s, m_s, v_s) = _small_adam(
        place, own_small, r_small,
        (norm1_g, pool_scale, lb_logits, rec_norm_g, gf),
        (m_norm1_g, m_pool_scale, m_lb_logits, m_rec_norm_g, m_final_norm_g.reshape(1, D)),
        (v_norm1_g, v_pool_scale, v_lb_logits, v_rec_norm_g, v_final_norm_g.reshape(1, D)))

    def outs(small, win, pool, wout):
        n1, ps, lbl, rg, fg = small
        return n1, win, pool, ps, lbl, rg, wout, fg.reshape(D)

    return (loss.reshape(()), grad_x[None],
            *outs(g_s, g_win, g_pool, g_wout), *outs(d_s, d_win, d_pool, d_wout),
            *outs(m_s, m_win, m_pool, m_wout), *outs(v_s, v_win, v_pool, v_wout))
```

```python
import jax
import jax.numpy as jnp
from jax import lax
from jax.experimental import pallas as pl
from jax.experimental.pallas import tpu as pltpu

T = 2048
D = 1024
NSEG = 6
NTILE = 24
TILE = 256
DMIX = 2048
NDEV = 8
HEAD = 128
NHEAD = 8
CHUNK = 64
NCHUNK = T // CHUNK
NB = 32
NGRP = NCHUNK // NB
NGROUP = 4
GROUP = 256
EPS = 1e-6
EXP_CAP = 115.0
MESH = pl.DeviceIdType.MESH
ANY = pl.BlockSpec(memory_space=pl.ANY)
HBM = pl.BlockSpec(memory_space=pltpu.HBM)
SEM = pl.BlockSpec(memory_space=pltpu.SEMAPHORE)
EFFECT = pltpu.SideEffectType.DATAFLOW_SIDE_EFFECTING

ADAM_LR = 0.001
ADAM_B1 = 0.9
ADAM_B2 = 0.999
ADAM_EPS = 1e-08
ADAM_WD = 0.01
ADAM_STEP = 10
BC1 = 1.0 - ADAM_B1 ** ADAM_STEP
BC2 = 1.0 - ADAM_B2 ** ADAM_STEP

MIB = 1 << 20


def _params(sem=None, vmem_mib=48):
    return pltpu.CompilerParams(dimension_semantics=sem, vmem_limit_bytes=vmem_mib * MIB)


def _sigmoid(v):
    return 1.0 / (1.0 + jnp.exp(-v))


def _dot(a, b, ca, cb, precision=None):
    return lax.dot_general(a, b, (((ca,), (cb,)), ((), ())), precision=precision,
                           preferred_element_type=jnp.float32)


def _bf(v):
    return v.astype(jnp.bfloat16)


def _in_hbm(a):
    return pltpu.with_memory_space_constraint(a, pltpu.HBM)


def _place():
    x, y, c = lax.axis_index("x"), lax.axis_index("y"), lax.axis_index("c")
    return x, y, c, 4 * x + 2 * y + c


def _peer(x, y, c, r):
    return (x ^ ((r >> 2) & 1), y ^ ((r >> 1) & 1), c ^ (r & 1))


def _gather_proj(x, g1, w_in, w_out, pool_w):
    def body(x_ref, g_ref, win_ref, wout_ref, pool_ref, ht_o, wt_o, woutb_o, wout_o, pool_o, proj_o,
             xbuf, hv, htv, wv, wob, pb, stage, send_sems, recv_sems, loc_sems, out_sems):
        px, py, c, my_idx = _place()
        fetch_x = pltpu.make_async_copy(x_ref, xbuf, loc_sems.at[5])
        fetch_x.start()
        me, sibling = (px, py, c), (px, py, 1 - c)
        chips = [(1 - px, py), (px, 1 - py), (1 - px, 1 - py)]
        for p in range(3):
            wv[3 * my_idx + p] = _bf(win_ref[0, :, p * TILE:(p + 1) * TILE])

        def index(bx, by, bc):
            return 4 * bx + 2 * by + bc

        def slot(w, block):
            return wv.at[pl.ds(3 * index(*block), 3)] if w == 0 else pool_o.at[index(*block)]

        def copy(k, w, block, to, src=None):
            return pltpu.make_async_remote_copy(
                src_ref=slot(w, block) if src is None else src, dst_ref=slot(w, block),
                send_sem=send_sems.at[2 * k + w], recv_sem=recv_sems.at[2 * k + w],
                device_id=to, device_id_type=MESH)

        def save(block):
            at = pl.ds(3 * index(*block), 3)
            pltpu.make_async_copy(wv.at[at], wt_o.at[at], loc_sems.at[4]).start()

        srcs = (slot(0, me), pb)
        first = []
        for w in (0, 1):
            if w == 1:
                pb[...] = _bf(pool_ref[0])
                wob[...] = _bf(wout_ref[0])
            group = [copy(1 + j, w, me, (*chip, c), src=srcs[w]) for j, chip in enumerate(chips[:2])]
            group.append(copy(0, w, me, sibling, src=srcs[w]))
            for cp in group:
                cp.start()
            first += group
        save(me)
        locs = [pltpu.make_async_copy(pb, slot(1, me), loc_sems.at[0]),
                pltpu.make_async_copy(wob, wout_o.at[my_idx], loc_sems.at[1]),
                pltpu.make_async_copy(wob, woutb_o, loc_sems.at[2])]
        for cp in locs:
            cp.start()

        fetch_x.wait()
        xv = xbuf[...]
        hv[...] = _bf(xv * lax.rsqrt(jnp.mean(xv * xv, axis=-1, keepdims=True) + EPS) * g_ref[...])
        rows = 256
        for r0 in range(0, T, rows):
            htv[:, r0:r0 + rows] = hv[r0:r0 + rows, :].T
        locs.append(pltpu.make_async_copy(htv, ht_o, loc_sems.at[3]))
        locs[-1].start()

        def out_copy(p, j):
            return pltpu.make_async_copy(stage.at[p], proj_o.at[j], out_sems.at[p])

        def project(nth, block):
            base = 3 * index(*block)

            def tile(p, carry):
                if nth > 0:
                    out_copy(p, base + p).wait()
                stage[p] = _dot(hv[...], wv[base + p], 1, 0)
                out_copy(p, base + p).start()
                return carry

            lax.fori_loop(0, 3, tile, 0)

        project(0, me)
        copy(0, 0, sibling, me).wait_recv()
        save(sibling)
        project(1, sibling)
        passed = []
        relay_from = (px ^ (1 - c), py ^ c, c)
        relay_to = (px ^ c, py ^ (1 - c), c)

        def arrived(w, j):
            copy(1 + j, w, (*chips[j], c), me).wait_recv()
            passed.append(copy(4 + j, w, (*chips[j], c), sibling))
            passed[-1].start()

        def relay(w):
            passed.append(copy(3, w, relay_from, relay_to))
            passed[-1].start()

        def handed(nth, j):
            copy(4 + j, 0, (*chips[j], 1 - c), me).wait_recv()
            save((*chips[j], 1 - c))
            project(nth, (*chips[j], 1 - c))

        arrived(0, 0)
        arrived(0, 1)
        relay(0)
        for j in range(2):
            save((*chips[j], c))
            project(2 + j, (*chips[j], c))
        handed(4, 0)
        handed(5, 1)
        arrived(1, 0)
        arrived(1, 1)
        relay(1)
        arrived(0, 2)
        save((*chips[2], c))
        project(6, (*chips[2], c))
        handed(7, 2)
        arrived(1, 2)
        copy(0, 1, sibling, me).wait_recv()
        for j, chip in enumerate(chips):
            copy(4 + j, 1, (*chip, 1 - c), me).wait_recv()
        keep = pltpu.make_async_copy(wv, wt_o, loc_sems.at[4])
        for p in range(3):
            out_copy(p, p).wait()
        for cp in first + passed:
            cp.wait_send()
        keep.wait()
        for cp in locs:
            cp.wait()

    vmem = pl.BlockSpec(memory_space=pltpu.VMEM)
    bf16 = jnp.bfloat16
    return pl.pallas_call(
        body, name="gather_proj",
        out_shape=(pltpu.HBM((D, T), bf16), pltpu.HBM((NTILE, D, TILE), bf16),
                   pltpu.HBM((DMIX // NDEV, D), bf16), pltpu.HBM((NDEV, DMIX // NDEV, D), bf16),
                   pltpu.HBM((NDEV, NGROUP, GROUP // NDEV, GROUP), bf16), pltpu.HBM((NTILE, T, TILE), jnp.float32)),
        in_specs=[ANY] + [vmem] * 4, out_specs=(ANY,) * 6,
        scratch_shapes=[pltpu.VMEM((T, D), jnp.float32),
                        pltpu.VMEM((T, D), bf16), pltpu.VMEM((D, T), bf16), pltpu.VMEM((NTILE, D, TILE), bf16),
                        pltpu.VMEM((DMIX // NDEV, D), bf16), pltpu.VMEM((NGROUP, GROUP // NDEV, GROUP), bf16),
                        pltpu.VMEM((3, T, TILE), jnp.float32),
                        pltpu.SemaphoreType.DMA((14,)), pltpu.SemaphoreType.DMA((14,)),
                        pltpu.SemaphoreType.DMA((6,)), pltpu.SemaphoreType.DMA((3,))],
        compiler_params=_params(vmem_mib=56),
    )(x, g1, w_in, w_out, pool_w)


def _split_start(name, arrays, n_copies, plan):
    k = len(arrays)

    def body(*refs):
        send_sems, recv_sems, token = refs[k], refs[k + 1], refs[-1]
        for i, (src, dst, to) in enumerate(plan(refs[:k])):
            pltpu.make_async_remote_copy(src_ref=src, dst_ref=dst, send_sem=send_sems.at[i],
                                         recv_sem=recv_sems.at[i], device_id=to, device_id_type=MESH).start()
        token[...] = jnp.zeros_like(token)

    out = pl.pallas_call(
        body, name=name,
        out_shape=(pltpu.SemaphoreType.DMA((n_copies,)), pltpu.SemaphoreType.DMA((n_copies,)),
                   *[pltpu.HBM(a.shape, a.dtype) for a in arrays], jax.ShapeDtypeStruct((8, 128), jnp.float32)),
        in_specs=[HBM] * k, out_specs=(SEM, SEM, *[HBM] * k, pl.BlockSpec(memory_space=pltpu.VMEM)),
        input_output_aliases={i: 2 + i for i in range(k)},
        compiler_params=pltpu.CompilerParams(has_side_effects=EFFECT),
    )(*[pltpu.with_memory_space_constraint(a, pltpu.HBM) for a in arrays])
    return out[0], out[1], out[2:2 + k], out[-1]


def _split_wait(name, arrays, send_sems, recv_sems, plan, after):
    k = len(arrays)

    def body(*refs):
        sends, recvs = refs[k], refs[k + 1]
        for i, (src, dst, to) in enumerate(plan(refs[:k])):
            cp = pltpu.make_async_remote_copy(src_ref=src, dst_ref=dst, send_sem=sends.at[i], recv_sem=recvs.at[i],
                                              device_id=to, device_id_type=MESH)
            cp.wait_send()
            cp.wait_recv()

    return pl.pallas_call(
        body, name=name,
        out_shape=tuple(pltpu.HBM(a.shape, a.dtype) for a in arrays),
        in_specs=[HBM] * k + [SEM, SEM, ANY], out_specs=(HBM,) * k,
        input_output_aliases={i: i for i in range(k)},
        compiler_params=pltpu.CompilerParams(has_side_effects=EFFECT),
    )(*arrays, send_sems, recv_sems, after)


def _plan_wout(refs):
    src, land = refs
    x, y, c, me = _place()
    return [(src, land.at[me], _peer(x, y, c, r)) for r in range(1, NDEV)]


def _plan_rest(refs):
    gob, gpf, r_out, r_pool = refs
    x, y, c, me = _place()
    plan = []
    for r in range(1, NDEV):
        plan.append((gob.at[me ^ r], r_out.at[r - 1], _peer(x, y, c, r)))
        plan.append((gpf.at[me ^ r], r_pool.at[r - 1], _peer(x, y, c, r)))
    return plan


def _plan_in(refs):
    sums, landing = refs
    x, y, c, _ = _place()
    plan = []
    for rel, (dx, dy) in enumerate(((1, 0), (0, 1), (1, 1))):
        for p in range(3):
            plan.append((sums.at[rel, p], landing.at[rel, p], (x ^ dx, y ^ dy, c)))
    return plan


def _plan_small(refs):
    small, land = refs
    x, y, c, me = _place()
    return [(small, land.at[me], _peer(x, y, c, r)) for r in range(1, NDEV)]


_POOL_SPECS = [pl.BlockSpec((None, T, GROUP), lambda g, base=base: (base + g, 0, 0)) for base in (0, 4)]
_HEAD_SPECS = [pl.BlockSpec((None, T, HEAD), lambda h, base=base: (base + h // 2, 0, h % 2))
               for base in (8, 12, 16, 20)]


_POOL_W_SPEC = pl.BlockSpec((NDEV, None, GROUP // NDEV, GROUP), lambda g: (0, g, 0, 0))


def _row_ids(shape):
    return lax.broadcasted_iota(jnp.int32, shape, 0)


BAND_ROWS = 128
HALO = 16


def _window_sum(a, gidx, lead):
    width = lax.shift_left(jnp.int32(2), gidx)
    shape = (BAND_ROWS, BAND_ROWS + HALO)
    t, j = lax.broadcasted_iota(jnp.int32, shape, 0), lax.broadcasted_iota(jnp.int32, shape, 1)
    first = t if lead else t + HALO - width + 1
    band = _bf(jnp.where(j >= first, jnp.where(j < first + width, 1.0, 0.0), 0.0))
    zeros = jnp.zeros((HALO, a.shape[1]), jnp.bfloat16)
    padded = [jnp.concatenate([p, zeros] if lead else [zeros, p], axis=0) for p in _split2(a)]
    out = []
    for r0 in range(0, T, BAND_ROWS):
        slab = jnp.concatenate([p[r0:r0 + BAND_ROWS + HALO] for p in padded], axis=1)
        r = _dot(band, slab, 1, 0)
        out.append(r[:, :a.shape[1]] + r[:, a.shape[1]:])
    return jnp.concatenate(out, axis=0)


def _window_mean(s, gidx):
    inv = jnp.where(gidx == 0, 0.5, jnp.where(gidx == 1, 0.25, jnp.where(gidx == 2, 0.125, 0.0625)))
    width = lax.shift_left(jnp.int32(2), gidx)
    head = s[:16] / jnp.minimum(_row_ids((16, s.shape[1])) + 1, width).astype(jnp.float32)
    return jnp.concatenate([head, s[16:] * inv], axis=0)


def _pool_fwd(proj, pool_w, pool_scale, token):
    def body(u_ref, pg_ref, w_ref, sc_ref, token_any, y_ref):
        del token_any
        gidx = pl.program_id(0)
        u, pg = u_ref[...], pg_ref[...]
        d = _window_mean(_window_sum(u, gidx, False), gidx) - u
        mixed = _dot(_bf(d), w_ref[...].reshape(GROUP, GROUP), 1, 0)
        y_ref[...] = _bf(mixed * sc_ref[...] * (pg * _sigmoid(pg)))

    return pl.pallas_call(
        body, name="pool_fwd", grid=(NGROUP,),
        in_specs=[*_POOL_SPECS, _POOL_W_SPEC, pl.BlockSpec((1, GROUP), lambda g: (0, g)), ANY],
        out_specs=pl.BlockSpec((T, GROUP), lambda g: (0, g)),
        out_shape=pltpu.HBM((T, DMIX), jnp.bfloat16),
        compiler_params=_params(("parallel",)),
    )(proj, proj, pool_w, pool_scale, token)


def _tri(lower):
    r = lax.broadcasted_iota(jnp.int32, (CHUNK, CHUNK), 0)
    c = lax.broadcasted_iota(jnp.int32, (CHUNK, CHUNK), 1)
    return (r >= c) if lower else (r <= c)


def _sum_rows_matrix():
    shape = (CHUNK + 16, CHUNK)
    r, c = lax.broadcasted_iota(jnp.int32, shape, 0), lax.broadcasted_iota(jnp.int32, shape, 1)
    run = jnp.where(c <= r, 1.0, 0.0)
    half = jnp.where(c < CHUNK // 2, 1.0, 0.0)
    return _bf(jnp.where(r < CHUNK, run, jnp.where(r < CHUNK + 8, 1.0, half)))


def _rev_sum_matrix():
    shape = (CHUNK, 2 * CHUNK)
    r, c = lax.broadcasted_iota(jnp.int32, shape, 0), lax.broadcasted_iota(jnp.int32, shape, 1)
    return _bf(jnp.where(c < CHUNK, jnp.where(c >= r, 1.0, 0.0), jnp.where(c - CHUNK < r, 1.0, 0.0)))


def _split2(a):
    hi = _bf(a)
    return [hi, _bf(a - hi.astype(jnp.float32))]


def _exact_sums(mat, pieces):
    x = jnp.concatenate([s for p in pieces for s in _split2(p)], axis=1)
    r = _dot(mat, x, 1, 0)
    return [r[:, 2 * j * HEAD:(2 * j + 1) * HEAD] + r[:, (2 * j + 1) * HEAD:(2 * j + 2) * HEAD]
            for j in range(len(pieces))]


def _gates(qv, fl, lb):
    sq = _sigmoid(qv)
    sg = _sigmoid(fl)
    f = lb + (1.0 - lb) * sg
    return dict(sq=sq, qs=qv * sq, sg=sg, f=f, kk=1.0 - f, g=jnp.log2(f))


def _decays(sums):
    big_g = sums[:CHUNK]
    total = sums[CHUNK:CHUNK + 8]
    g_last = jnp.tile(total, (CHUNK // 8, 1))
    g_mid = jnp.tile(sums[CHUNK + 8:], (CHUNK // 8, 1))
    return dict(
        e_q=jnp.exp2(big_g),
        e_k=jnp.exp2(g_last - big_g),
        e_qm=jnp.exp2(jnp.minimum(big_g - g_mid, EXP_CAP)),
        e_km=jnp.exp2(jnp.minimum(g_mid - big_g, EXP_CAP)),
        total8=jnp.exp2(total))


def _group_rows(gi):
    return [pl.ds(pl.multiple_of((gi * NB + j) * CHUNK, CHUNK), CHUNK) for j in range(NB)]


def _lower_bound(lb_ref):
    return _sigmoid(lb_ref[0:1, :] - lb_ref[1:2, :])


def _hgrn_fwd(proj, lb_logits, rec_g, y_in):
    def body(q_ref, f_ref, i_ref, gate_ref, lb_ref, rg_ref, y_any, y_ref, o_ref, st_ref):
        del y_any
        lb = _lower_bound(lb_ref)
        causal = _tri(True)
        smat = _sum_rows_matrix()

        def group(gi, st):
            rows = _group_rows(gi)
            ts = [_gates(q_ref[r, :], f_ref[r, :], lb) for r in rows]
            ds = [_decays(s) for s in _exact_sums(smat, [t["g"] for t in ts])]
            vs = [_bf(i_ref[r, :]) for r in rows]
            q_m = [_bf(t["qs"] * d["e_qm"]) for t, d in zip(ts, ds)]
            k_m = [_bf(t["kk"] * d["e_km"]) for t, d in zip(ts, ds)]
            q_e = [_bf(t["qs"] * d["e_q"]) for t, d in zip(ts, ds)]
            k_e = [_bf(t["kk"] * d["e_k"]) for t, d in zip(ts, ds)]
            a = [_bf(jnp.where(causal, _dot(q_m[j], k_m[j], 1, 1), 0.0)) for j in range(NB)]
            intra = [_dot(a[j], vs[j], 1, 0) for j in range(NB)]
            upd = [_dot(vs[j], k_e[j], 0, 0) for j in range(NB)]
            for j in range(NB):
                st_ref[gi * NB + j] = st
                o_ref[rows[j], :] = intra[j] + _dot(q_e[j], _bf(st), 1, 1)
                st = st * jnp.tile(ds[j]["total8"], (HEAD // 8, 1)) + upd[j]
            return st

        lax.fori_loop(0, NGRP, group, jnp.zeros((HEAD, HEAD), jnp.float32))
        o = o_ref[...]
        rn = o * lax.rsqrt(jnp.mean(o * o, axis=-1, keepdims=True) + EPS)
        gate = gate_ref[...]
        y_ref[...] = _bf(rn * rg_ref[...] * (gate * _sigmoid(gate)))

    return pl.pallas_call(
        body, name="hgrn_fwd", grid=(NHEAD,),
        in_specs=[*_HEAD_SPECS,
                  pl.BlockSpec((2, HEAD), lambda h: (0, h)),
                  pl.BlockSpec((1, HEAD), lambda h: (0, h)),
                  pl.BlockSpec(memory_space=pl.ANY)],
        out_specs=(pl.BlockSpec((T, HEAD), lambda h: (0, NHEAD + h)),
                   pl.BlockSpec((T, HEAD), lambda h: (0, h)),
                   pl.BlockSpec((None, NCHUNK, HEAD, HEAD), lambda h: (h, 0, 0, 0))),
        out_shape=(pltpu.HBM((T, DMIX), jnp.bfloat16), pltpu.HBM((T, D), jnp.float32),
                   pltpu.HBM((NHEAD, NCHUNK, HEAD, HEAD), jnp.float32)),
        input_output_aliases={6: 0},
        compiler_params=_params(("parallel",)),
    )(proj, proj, proj, proj, lb_logits, rec_g, y_in)


def _out_proj_loss(x, y, w_out, target, gf):
    rows = 512
    parts = [slice(k * rows // 2, (k + 1) * rows // 2) for k in range(2)]

    def body(x_ref, y_ref, w_ref, t_ref, g_ref, dz_ref, dzb_ref, sq_ref, dg_ref):
        zs = [x_ref[p, :] + _dot(y_ref[p, :], w_ref[...], 1, 0) for p in parts]
        sq = dg = 0.0
        for p, z in zip(parts, zs):
            r = lax.rsqrt(jnp.mean(z * z, axis=-1, keepdims=True) + EPS)
            zhat = z * r
            err = zhat * g_ref[...] - t_ref[p, :]
            dy = err * (1.0 / D)
            gdy = dy * g_ref[...]
            dz = r * (gdy - zhat * jnp.mean(zhat * gdy, axis=-1, keepdims=True))
            dz_ref[p, :] = dz
            dzb_ref[p, :] = _bf(dz)
            sq = sq + jnp.sum(err * err, axis=0, keepdims=True)
            dg = dg + jnp.sum(zhat * dy, axis=0, keepdims=True)

        @pl.when(pl.program_id(0) == 0)
        def _():
            sq_ref[...] = sq
            dg_ref[...] = dg

        @pl.when(pl.program_id(0) != 0)
        def _():
            sq_ref[...] += sq
            dg_ref[...] += dg

    tile = pl.BlockSpec((rows, D), lambda i: (i, 0))
    vec = pl.BlockSpec((1, D), lambda i: (0, 0))
    return pl.pallas_call(
        body, name="out_proj_loss", grid=(T // rows,),
        in_specs=[tile, pl.BlockSpec((rows, DMIX), lambda i: (i, 0)), pl.BlockSpec((DMIX, D), lambda i: (0, 0)),
                  tile, vec],
        out_specs=(tile, tile, vec, vec),
        out_shape=(pltpu.HBM((T, D), jnp.float32), pltpu.HBM((T, D), jnp.bfloat16),
                   pltpu.HBM((1, D), jnp.float32), pltpu.HBM((1, D), jnp.float32)),
        compiler_params=_params(("arbitrary",)),
    )(x, y, w_out, target, gf)


def _out_proj_bwd(dzb, w_out, y):
    tn = 512

    def body(dz_ref, w_ref, y_ref, dy_ref, gw_ref, gwb_ref):
        dz = dz_ref[...]
        dy_ref[...] = _dot(dz, w_ref[...], 1, 1)
        gw = _dot(y_ref[...], dz, 0, 0)
        gw_ref[...] = gw
        gwb_ref[...] = _bf(gw)

    return pl.pallas_call(
        body, name="out_proj_bwd", grid=(DMIX // tn,),
        in_specs=[pl.BlockSpec((T, D), lambda n: (0, 0)), pl.BlockSpec((tn, D), lambda n: (n, 0)),
                  pl.BlockSpec((T, tn), lambda n: (0, n))],
        out_specs=(pl.BlockSpec((T, tn), lambda n: (0, n)), pl.BlockSpec((tn, D), lambda n: (n, 0)),
                   pl.BlockSpec((tn, D), lambda n: (n, 0))),
        out_shape=(pltpu.HBM((T, DMIX), jnp.float32), pltpu.HBM((DMIX, D), jnp.float32),
                   pltpu.HBM((DMIX, D), jnp.bfloat16)),
        compiler_params=_params(("parallel",)),
    )(dzb, w_out, y)


def _hgrn_bwd(proj, lb_logits, rec_g, o, states, dymix, dproj_in, token):
    def body(q_ref, f_ref, i_ref, gate_ref, lb_ref, rg_ref, o_ref, st_ref, dy_ref, dp_any, token_any,
             dp_ref, drg_ref, dlb_ref, do_ref):
        del dp_any, token_any
        lb = _lower_bound(lb_ref)
        causal = _tri(True)
        smat, rmat = _sum_rows_matrix(), _rev_sum_matrix()

        o = o_ref[...]
        rs = lax.rsqrt(jnp.mean(o * o, axis=-1, keepdims=True) + EPS)
        rn = o * rs
        gate = gate_ref[...]
        sgate = _sigmoid(gate)
        dyv = dy_ref[...]
        d_r = dyv * (gate * sgate)
        dp_ref[3] = _bf(dyv * (rn * rg_ref[...]) * (sgate * (1.0 + gate * (1.0 - sgate))))
        drg_ref[...] = jnp.sum(d_r * rn, axis=0, keepdims=True)
        drn = d_r * rg_ref[...]
        do_ref[...] = rs * (drn - rn * jnp.mean(rn * drn, axis=-1, keepdims=True))

        def group(i, carry):
            dst, dlb = carry
            gi = NGRP - 1 - i
            rows = _group_rows(gi)
            span = range(NB)
            qvs = [q_ref[r, :] for r in rows]
            ts = [_gates(qv, f_ref[r, :], lb) for qv, r in zip(qvs, rows)]
            ds = [_decays(s) for s in _exact_sums(smat, [t["g"] for t in ts])]
            vs = [_bf(i_ref[r, :]) for r in rows]
            dos = [_bf(do_ref[r, :]) for r in rows]
            sts = [st_ref[gi * NB + j] for j in span]
            qe_f = [t["qs"] * d["e_q"] for t, d in zip(ts, ds)]
            ke_f = [t["kk"] * d["e_k"] for t, d in zip(ts, ds)]
            q_e, k_e = [_bf(a) for a in qe_f], [_bf(a) for a in ke_f]
            q_m = [_bf(t["qs"] * d["e_qm"]) for t, d in zip(ts, ds)]
            k_m = [_bf(t["kk"] * d["e_km"]) for t, d in zip(ts, ds)]
            a = [_bf(jnp.where(causal, _dot(q_m[j], k_m[j], 1, 1), 0.0)) for j in span]
            da = [_bf(jnp.where(causal, _dot(dos[j], vs[j], 1, 1), 0.0)) for j in span]
            dqm = [_dot(da[j], k_m[j], 1, 0) for j in span]
            dkm = [_dot(da[j], q_m[j], 0, 0) for j in span]
            dv_in = [_dot(a[j], dos[j], 0, 0) for j in span]
            dqe = [_dot(dos[j], _bf(sts[j]), 1, 0) for j in span]
            grow = [_dot(dos[j], q_e[j], 0, 0) for j in span]
            dke, carried = [None] * NB, [None] * NB
            for j in reversed(span):
                dst_b = _bf(dst)
                dke[j] = _dot(vs[j], dst_b, 1, 0)
                dp_ref[2, rows[j], :] = _bf(dv_in[j] + _dot(k_e[j], dst_b, 1, 1))
                carried[j] = ds[j]["total8"] * jnp.sum(dst * sts[j], axis=0, keepdims=True)
                dst = dst * jnp.tile(ds[j]["total8"], (HEAD // 8, 1)) + grow[j]
            kdk = [ke_f[j] * dke[j] for j in span]
            pos = [(q_m[j].astype(jnp.float32) * dqm[j] - k_m[j].astype(jnp.float32) * dkm[j]) + qe_f[j] * dqe[j]
                   for j in span]
            dgs = _exact_sums(rmat, [jnp.concatenate([pos[j], kdk[j]], axis=0) for j in span])
            for j in span:
                t, d = ts[j], ds[j]
                dg = dgs[j] + jnp.tile(carried[j], (CHUNK // 8, 1))
                dqs = dqm[j] * d["e_qm"] + dqe[j] * d["e_q"]
                dkk = dkm[j] * d["e_km"] + dke[j] * d["e_k"]
                df = dg / t["f"] - dkk
                dp_ref[1, rows[j], :] = _bf(df * (1.0 - lb) * (t["sg"] * (1.0 - t["sg"])))
                dp_ref[0, rows[j], :] = _bf(dqs * (t["sq"] * (1.0 + qvs[j] * (1.0 - t["sq"]))))
                dlb = dlb + df * (1.0 - t["sg"])
            return dst, dlb

        _, dlb = lax.fori_loop(0, NGRP, group, (jnp.zeros((HEAD, HEAD), jnp.float32),
                                                jnp.zeros((CHUNK, HEAD), jnp.float32)))
        dlb_ref[...] = jnp.sum(dlb, axis=0, keepdims=True)

    vec = pl.BlockSpec((1, HEAD), lambda h: (0, h))
    return pl.pallas_call(
        body, name="hgrn_bwd", grid=(NHEAD,),
        in_specs=[*_HEAD_SPECS,
                  pl.BlockSpec((2, HEAD), lambda h: (0, h)), vec,
                  pl.BlockSpec((T, HEAD), lambda h: (0, h)),
                  pl.BlockSpec((None, NCHUNK, HEAD, HEAD), lambda h: (h, 0, 0, 0)),
                  pl.BlockSpec((T, HEAD), lambda h: (0, NHEAD + h)), ANY, ANY],
        out_specs=(pl.BlockSpec((4, T, HEAD), lambda h: (0, 0, h)), vec, vec),
        out_shape=(pltpu.HBM((NSEG, T, D), jnp.bfloat16),
                   pltpu.HBM((1, D), jnp.float32), pltpu.HBM((1, D), jnp.float32)),
        scratch_shapes=[pltpu.VMEM((T, HEAD), jnp.float32)],
        input_output_aliases={9: 0},
        compiler_params=_params(("parallel",)),
    )(proj, proj, proj, proj, lb_logits, rec_g, o, states, dymix, dproj_in, token)


def _pool_bwd(proj, pool_w, pool_scale, dymix):
    def body(u_ref, pg_ref, w_ref, sc_ref, dy_ref, dp_ref, gw_ref, gs_ref):
        gidx = pl.program_id(0)
        u, pg = u_ref[...], pg_ref[...]
        w = w_ref[...].reshape(GROUP, GROUP)
        d = _bf(_window_mean(_window_sum(u, gidx, False), gidx) - u)
        mixed = _dot(d, w, 1, 0)
        spg = _sigmoid(pg)
        dyv = dy_ref[...]
        d_p = dyv * (pg * spg)
        dp_ref[1] = _bf(dyv * (mixed * sc_ref[...]) * (spg * (1.0 + pg * (1.0 - spg))))
        gs_ref[...] = jnp.sum(d_p * mixed, axis=0, keepdims=True)
        dmixed = _bf(d_p * sc_ref[...])
        gw_ref[...] = _dot(d, dmixed, 0, 0).reshape(gw_ref.shape)
        dd = _dot(dmixed, w, 1, 1)
        dp_ref[0] = _bf(_window_sum(_window_mean(dd, gidx), gidx, True) - dd)

    return pl.pallas_call(
        body, name="pool_bwd", grid=(NGROUP,),
        in_specs=[*_POOL_SPECS, _POOL_W_SPEC,
                  pl.BlockSpec((1, GROUP), lambda g: (0, g)),
                  pl.BlockSpec((T, GROUP), lambda g: (0, g))],
        out_specs=(pl.BlockSpec((2, T, GROUP), lambda g: (2, 0, g)), _POOL_W_SPEC,
                   pl.BlockSpec((1, GROUP), lambda g: (0, g))),
        out_shape=(pltpu.HBM((NSEG, T, D), jnp.bfloat16),
                   pltpu.HBM((NDEV, NGROUP, GROUP // NDEV, GROUP), jnp.float32),
                   pltpu.HBM((1, D), jnp.float32)),
        compiler_params=_params(("parallel",)),
    )(proj, proj, pool_w, pool_scale, dymix)


HALF = NTILE // 2
AWAY = HALF - 3


def _dproj_tile(chip, side, p):
    j = 6 * chip + 3 * side + p
    return ((j // 4 + 4) % NSEG, 0, j % 4)


def _sibling_copy(sib_out, sib_in, send_sems, recv_sems, slot):
    x, y, c, _ = _place()
    return pltpu.make_async_remote_copy(
        src_ref=sib_out.at[slot], dst_ref=sib_in.at[slot], send_sem=send_sems.at[slot],
        recv_sem=recv_sems.at[slot], device_id=(x, y, 1 - c), device_id_type=MESH)


def _proj_bwd_w_far(place, ht, dproj):
    def body(place_ref, h_ref, dpa_ref, dpb_ref, sib_out, sib_in, send_sems, recv_sems, stage, loc_sems):
        del place_ref
        i = pl.program_id(0)

        def to_hbm(k):
            return pltpu.make_async_copy(stage.at[k], sib_out.at[k], loc_sems.at[k])

        def send(k):
            to_hbm(k).wait()
            _sibling_copy(sib_out, sib_in, send_sems, recv_sems, k).start()

        stage[2 * i] = _bf(_dot(h_ref[...], dpa_ref[...], 1, 0))
        stage[2 * i + 1] = _bf(_dot(h_ref[...], dpb_ref[...], 1, 0))

        @pl.when(i > 0)
        def _():
            send(2 * i - 2)
            send(2 * i - 1)

        to_hbm(2 * i).start()
        to_hbm(2 * i + 1).start()

        @pl.when(i == HALF // 2 - 1)
        def _():
            send(2 * i)
            send(2 * i + 1)

    def tile(k, pr):
        return _dproj_tile(k // 3, 1 - pr[2], k % 3)

    buf = pltpu.HBM((HALF, D, TILE), jnp.bfloat16)
    sems = pltpu.SemaphoreType.DMA((HALF,))
    return pl.pallas_call(
        body, name="proj_bwd_w_far",
        grid_spec=pltpu.PrefetchScalarGridSpec(
            num_scalar_prefetch=1, grid=(HALF // 2,),
            in_specs=[pl.BlockSpec((D, T), lambda i, pr: (0, 0)),
                      pl.BlockSpec((None, T, TILE), lambda i, pr: tile(2 * i, pr)),
                      pl.BlockSpec((None, T, TILE), lambda i, pr: tile(2 * i + 1, pr))],
            out_specs=(HBM, HBM, SEM, SEM),
            scratch_shapes=[pltpu.VMEM((HALF, D, TILE), jnp.bfloat16), pltpu.SemaphoreType.DMA((HALF,))]),
        out_shape=(buf, buf, sems, sems),
        compiler_params=pltpu.CompilerParams(dimension_semantics=("arbitrary",), vmem_limit_bytes=48 * MIB,
                                             has_side_effects=EFFECT),
    )(place, ht, dproj, dproj)


def _proj_bwd_w_near(place, ht, dproj, sib_out, sib_in, sib_send, sib_recv):
    def owner_chip(k, pr):
        return jnp.where(k < AWAY, (pr[1] + 1 + k % 3) % 4, pr[1])

    def tile_p(k):
        return jnp.where(k < AWAY, k // 3, k - AWAY)

    def body(place_ref, h_ref, dp_ref, sib_out, sib_in, sib_send, sib_recv, sums, own_ref, landing, out_send,
             out_recv, recvbuf, outbuf, in_sems, loc_sems):
        i = pl.program_id(0)
        px, py, c, _ = _place()

        def slot_of(k):
            return 3 * owner_chip(k, place_ref) + tile_p(k)

        def load(k):
            return pltpu.make_async_copy(sib_in.at[slot_of(k)], recvbuf.at[k % 2], in_sems.at[k % 2])

        def fetch(k):
            _sibling_copy(sib_out, sib_in, sib_send, sib_recv, slot_of(k)).wait_recv()
            load(k).start()

        def route(k):
            chip = owner_chip(k, place_ref)
            cx, cy = chip // 2, chip % 2
            return cx, cy, (cx ^ px) + 2 * (cy ^ py) - 1, tile_p(k)

        def to_hbm(k):
            _, _, rel, p = route(k)
            return pltpu.make_async_copy(outbuf.at[k], sums.at[rel, p], loc_sems.at[k])

        def to_owner(k):
            cx, cy, rel, p = route(k)
            return pltpu.make_async_remote_copy(
                src_ref=sums.at[rel, p], dst_ref=landing.at[rel, p], send_sem=out_send.at[3 * rel + p],
                recv_sem=out_recv.at[3 * rel + p], device_id=(cx, cy, c), device_id_type=MESH)

        @pl.when(i == 0)
        def _():
            fetch(i)

        @pl.when(i < HALF - 1)
        def _():
            fetch(i + 1)

        load(i).wait()
        total = _dot(h_ref[...], dp_ref[...], 1, 0) + recvbuf[i % 2].astype(jnp.float32)
        own_ref[...] = total
        outbuf[jnp.minimum(i, AWAY)] = _bf(total)

        @pl.when(i < AWAY)
        def _():
            to_hbm(i).start()

        @pl.when(jnp.logical_and(i > 0, i <= AWAY))
        def _():
            to_hbm(i - 1).wait()
            to_owner(i - 1).start()

        @pl.when(i == HALF - 1)
        def _():
            for slot in range(HALF):
                _sibling_copy(sib_out, sib_in, sib_send, sib_recv, slot).wait_send()

    travelling = pltpu.HBM((3, 3, D, TILE), jnp.bfloat16)
    sems = pltpu.SemaphoreType.DMA((AWAY,))
    return pl.pallas_call(
        body, name="proj_bwd_w_near",
        grid_spec=pltpu.PrefetchScalarGridSpec(
            num_scalar_prefetch=1, grid=(HALF,),
            in_specs=[pl.BlockSpec((D, T), lambda i, pr: (0, 0)),
                      pl.BlockSpec((None, T, TILE), lambda i, pr: _dproj_tile(owner_chip(i, pr), pr[2], tile_p(i))),
                      HBM, HBM, SEM, SEM],
            out_specs=(HBM, pl.BlockSpec((None, D, TILE), lambda i, pr: (jnp.where(i < AWAY, 0, i % 3), 0, 0)),
                       HBM, SEM, SEM),
            scratch_shapes=[pltpu.VMEM((2, D, TILE), jnp.bfloat16), pltpu.VMEM((AWAY + 1, D, TILE), jnp.bfloat16),
                            pltpu.SemaphoreType.DMA((2,)), pltpu.SemaphoreType.DMA((AWAY,))]),
        out_shape=(travelling, pltpu.HBM((3, D, TILE), jnp.float32), travelling, sems, sems),
        compiler_params=pltpu.CompilerParams(dimension_semantics=("arbitrary",), vmem_limit_bytes=48 * MIB,
                                             has_side_effects=EFFECT),
    )(place, ht, dproj, sib_out, sib_in, sib_send, sib_recv)


def _proj_bwd_x(dproj, w_t, x, g1, dz, others, token):
    tm = 1024
    pairs = NSEG // 2
    k = len(others)

    def body(dp_ref, w_ref, x_any, g_ref, dz_any, *refs):
        dx_ref, dg_ref, wcat, acc, xbuf, dzbuf, row_sems = refs[k + 1:]
        m, s = pl.program_id(0), pl.program_id(1)
        mine = pl.ds(pl.multiple_of(m * tm, tm), tm)
        fetches = [pltpu.make_async_copy(x_any.at[mine], xbuf, row_sems.at[0]),
                   pltpu.make_async_copy(dz_any.at[mine], dzbuf, row_sems.at[1])]

        @pl.when(m == 0)
        def _():
            for i in range(8):
                wcat[s, :, i * TILE:(i + 1) * TILE] = w_ref[i]

        @pl.when(s == 0)
        def _():
            for cp in fetches:
                cp.start()
            acc[...] = jnp.zeros((tm, D), jnp.float32)

        acc[...] += _dot(jnp.concatenate([dp_ref[0], dp_ref[1]], axis=1), wcat[s], 1, 1)

        @pl.when(s == pairs - 1)
        def _():
            for cp in fetches:
                cp.wait()
            xv = xbuf[...]
            rs = lax.rsqrt(jnp.mean(xv * xv, axis=-1, keepdims=True) + EPS)
            xhat = xv * rs
            dhv = acc[...]
            gdh = dhv * g_ref[...]
            dx_ref[...] = dzbuf[...] + rs * (gdh - xhat * jnp.mean(xhat * gdh, axis=-1, keepdims=True))
            dg = jnp.sum(xhat * dhv, axis=0, keepdims=True)

            @pl.when(m == 0)
            def _():
                dg_ref[0:1, :] = dg
                for r, ref in enumerate(refs[:k]):
                    dg_ref[1 + r:2 + r, :] = ref[...]
                dg_ref[1 + k:, :] = jnp.zeros((7 - k, D), jnp.float32)

            @pl.when(m != 0)
            def _():
                dg_ref[0:1, :] += dg

    rows = pl.BlockSpec((tm, D), lambda m, s: (m, 0))
    vec = pl.BlockSpec((1, D), lambda m, s: (0, 0))
    return pl.pallas_call(
        body, name="proj_bwd_x", grid=(T // tm, pairs),
        in_specs=[pl.BlockSpec((2, tm, D), lambda m, s: (s, m, 0)),
                  pl.BlockSpec((8, D, TILE), lambda m, s: ((jnp.where(m == 0, s, pairs - 1) + 1) % pairs, 0, 0)),
                  HBM, vec, HBM, *[vec] * k, ANY],
        out_specs=(rows, pl.BlockSpec((8, D), lambda m, s: (0, 0))),
        out_shape=(jax.ShapeDtypeStruct((T, D), jnp.float32), pltpu.HBM((8, D), jnp.float32)),
        scratch_shapes=[pltpu.VMEM((pairs, D, 2 * D), jnp.bfloat16), pltpu.VMEM((tm, D), jnp.float32),
                        pltpu.VMEM((tm, D), jnp.float32), pltpu.VMEM((tm, D), jnp.float32),
                        pltpu.SemaphoreType.DMA((2,))],
        compiler_params=_params(("arbitrary", "arbitrary"), vmem_mib=58),
    )(dproj, w_t, _in_hbm(x), g1, _in_hbm(dz), *[_in_hbm(a) for a in others], token)


def _adamw(w, g, m, v):
    m_new = ADAM_B1 * m + (1.0 - ADAM_B1) * g
    v_new = ADAM_B2 * v + (1.0 - ADAM_B2) * (g * g)
    delta = -ADAM_LR * ((m_new / BC1) / (jnp.sqrt(v_new / BC2) + ADAM_EPS) + ADAM_WD * w)
    return delta, m_new, v_new


def _reduce_adam(name, place, parts, w, m, v, grid, w_spec):
    n = len(parts)

    def body(place_ref, *refs):
        del place_ref
        w_ref, m_ref, v_ref, g_ref, d_ref, mo_ref, vo_ref = refs[n:]
        g = None
        for ref, (_, _, stacked) in zip(refs[:n], parts):
            terms = [ref[r] for r in range(ref.shape[0])] if stacked else [ref[...]]
            for t in terms:
                if t.shape[-1] != w_ref.shape[-1]:
                    t = jnp.concatenate([t[p] for p in range(t.shape[0])], axis=1)
                g = t.astype(jnp.float32) if g is None else g + t.astype(jnp.float32)
        delta, m_new, v_new = _adamw(w_ref[...], g, m_ref[...], v_ref[...])
        g_ref[...] = g
        d_ref[...] = delta
        mo_ref[...] = m_new
        vo_ref[...] = v_new

    shape = jax.ShapeDtypeStruct(w.shape, jnp.float32)
    return pl.pallas_call(
        body, name=name,
        grid_spec=pltpu.PrefetchScalarGridSpec(
            num_scalar_prefetch=1, grid=grid,
            in_specs=[spec for _, spec, _ in parts] + [w_spec] * 3, out_specs=(w_spec,) * 4),
        out_shape=(shape,) * 4,
        compiler_params=_params(("parallel",)),
    )(place, *[_in_hbm(a) for a in [a for a, _, _ in parts] + [w, m, v]])


SMALL_ROWS = (1, 1, 2, 1, 1)


def _small_adam(place, own, parts, ws, ms, vs):
    n = len(SMALL_ROWS)

    def body(place_ref, own_ref, p_ref, *refs):
        ins, outs, bufs = refs[:3 * n], refs[3 * n:3 * n + 1 + 4 * n], refs[3 * n + 1 + 4 * n:]

        def stacked(group, buf):
            r0 = 0
            for ref, k in zip(group, SMALL_ROWS):
                buf[r0:r0 + k, :] = ref[...]
                r0 += k
            buf[r0:, :] = jnp.zeros((8 - r0, D), jnp.float32)
            return buf[...]

        wv, mv, vv = (stacked(ins[n * j:n * j + n], bufs[j]) for j in range(3))
        me = place_ref[0]
        g = None
        for s in range(NDEV):
            term = jnp.where(me == s, own_ref[...], p_ref[s])
            g = term if g is None else g + term
        rows = _row_ids(wv.shape)
        other = jnp.where(rows == 2, pltpu.roll(wv, 7, 0), jnp.where(rows == 3, pltpu.roll(wv, 1, 0), 0.0))
        lbv = _sigmoid(wv - other)
        sign = jnp.where(rows == 2, 1.0, -1.0)
        g = jnp.where((rows == 2) | (rows == 3), sign * g * lbv * (1.0 - lbv), g)
        delta, m_new, v_new = _adamw(wv, g, mv, vv)
        outs[0][...] = jnp.sum(g[6:7], axis=1, keepdims=True) * (0.5 / D)
        for j, val in enumerate((g, delta, m_new, v_new)):
            r0 = 0
            for ref, k in zip(outs[1 + n * j:1 + n * j + n], SMALL_ROWS):
                ref[...] = val[r0:r0 + k]
                r0 += k

    vmem = pl.BlockSpec(memory_space=pltpu.VMEM)
    shapes = [jax.ShapeDtypeStruct((k, D), jnp.float32) for k in SMALL_ROWS]
    out = pl.pallas_call(
        body, name="small_adam", out_shape=(jax.ShapeDtypeStruct((1, 1), jnp.float32), *shapes * 4),
        in_specs=[pl.BlockSpec(memory_space=pltpu.SMEM)] + [vmem] * (2 + 3 * n), out_specs=(vmem,) * (1 + 4 * n),
        scratch_shapes=[pltpu.VMEM((8, D), jnp.float32)] * 3,
    )(place, own, parts, *ws, *ms, *vs)
    return out[0], [out[1 + n * j:1 + n * j + n] for j in range(4)]


def kernel(x, norm1_g, w_in, pool_w, pool_scale, lb_logits, rec_norm_g, w_out, final_norm_g, loss_target, m_norm1_g, m_w_in, m_pool_w, m_pool_scale, m_lb_logits, m_rec_norm_g, m_w_out, m_final_norm_g, v_norm1_g, v_w_in, v_pool_w, v_pool_scale, v_lb_logits, v_rec_norm_g, v_w_out, v_final_norm_g):
    xs = x[0]
    target = loss_target[0]
    ix, iy, ic = lax.axis_index("x"), lax.axis_index("y"), lax.axis_index("c")
    place = jnp.stack([4 * ix + 2 * iy + ic, 2 * ix + iy, ic]).astype(jnp.int32)
    gf = final_norm_g.reshape(1, D)

    ht, w_t, w_out_b, w_out_g, pool_g, proj = _gather_proj(xs, norm1_g, w_in, w_out, pool_w)
    wout = [w_out_b, w_out_g]
    wout_send, wout_recv, wout, wout_token = _split_start("gather_wout_start", wout, NDEV - 1, _plan_wout)

    y = _pool_fwd(proj, pool_g, pool_scale, wout_token)
    y, o, states = _hgrn_fwd(proj, lb_logits, rec_norm_g, y)
    _, w_out_g = _split_wait("gather_wout_wait", wout, wout_send, wout_recv, _plan_wout, o)
    w_out_full = _in_hbm(w_out_g.reshape(DMIX, D))
    dz, dzb, sq, dgf = _out_proj_loss(xs, y, w_out_full, target, gf)

    dymix, gwout_f, gwout_b = _out_proj_bwd(dzb, w_out_full, y)
    dproj, gpool, dscale = _pool_bwd(proj, pool_g, pool_scale, dymix)

    blk_out = (NDEV, DMIX // NDEV, D)
    blk_pool = (NDEV, NGROUP, GROUP // NDEV, GROUP)
    rest = [gwout_b.reshape(blk_out), gpool,
            lax.empty((NDEV - 1,) + blk_out[1:], jnp.bfloat16), lax.empty((NDEV - 1,) + blk_pool[1:], jnp.float32)]
    rest_send, rest_recv, rest, rest_token = _split_start("scatter_rest_start", rest, 2 * (NDEV - 1), _plan_rest)

    dproj, drecg, dlb = _hgrn_bwd(proj, lb_logits, rec_norm_g, o, states, dymix, dproj, rest_token)
    chip_sums, own_sum, landing, win_send, win_recv = _proj_bwd_w_near(
        place, ht, dproj, *_proj_bwd_w_far(place, ht, dproj))
    win = [chip_sums, landing]

    others = [a.reshape(1, D) for a in (dscale, dlb, dlb, drecg, dgf, sq)]
    grad_x, small_block = _proj_bwd_x(dproj, w_t, xs, norm1_g, dz, others, chip_sums)

    small = [small_block, lax.empty((NDEV, 8, D), jnp.float32)]
    small_send, small_recv, small, small_token = _split_start("gather_small_start", small, NDEV - 1, _plan_small)

    _, gpool_own, r_out, r_pool = _split_wait("scatter_rest_wait", rest, rest_send, rest_recv, _plan_rest,
                                              small_token)
    g_wout, d_wout, m_wout, v_wout = _reduce_adam(
        "adam_w_out", place,
        [(gwout_f.reshape(blk_out), pl.BlockSpec((None,) + blk_out[1:], lambda i, pr: (pr[0], 0, 0)), False),
         (r_out, pl.BlockSpec((NDEV - 1,) + blk_out[1:], lambda i, pr: (0, 0, 0)), True)],
        w_out, m_w_out, v_w_out, (1,), pl.BlockSpec((None,) + blk_out[1:], lambda i, pr: (0, 0, 0)))
    g_pool, d_pool, m_pool, v_pool = _reduce_adam(
        "adam_pool_w", place,
        [(gpool_own, pl.BlockSpec((None,) + blk_pool[1:], lambda i, pr: (pr[0], 0, 0, 0)), False),
         (r_pool, pl.BlockSpec((NDEV - 1,) + blk_pool[1:], lambda i, pr: (0, 0, 0, 0)), True)],
        pool_w, m_pool_w, v_pool_w, (1,), pl.BlockSpec((None,) + blk_pool[1:], lambda i, pr: (0, 0, 0, 0)))

    _, r_in = _split_wait("scatter_win_wait", win, win_send, win_recv, _plan_in, d_pool)
    g_win, d_win, m_win, v_win = _reduce_adam(
        "adam_w_in", place,
        [(own_sum, pl.BlockSpec((3, D // 4, TILE), lambda i, pr: (0, i, 0)), False),
         (r_in, pl.BlockSpec((3, 3, D // 4, TILE), lambda i, pr: (0, 0, i, 0)), True)],
        w_in, m_w_in, v_w_in, (4,), pl.BlockSpec((None, D // 4, 3 * TILE), lambda i, pr: (0, i, 0)))

    own_small, r_small = _split_wait("gather_small_wait", small, small_send, small_recv, _plan_small, d_win)
    loss, (g_s, d_s, m_s, v_s) = _small_adam(
        place, own_small, r_small,
        (norm1_g, pool_scale, lb_logits, rec_norm_g, gf),
        (m_norm1_g, m_pool_scale, m_lb_logits, m_rec_norm_g, m_final_norm_g.reshape(1, D)),
        (v_norm1_g, v_pool_scale, v_lb_logits, v_rec_norm_g, v_final_norm_g.reshape(1, D)))

    def outs(small, win, pool, wout):
        n1, ps, lbl, rg, fg = small
        return n1, win, pool, ps, lbl, rg, wout, fg.reshape(D)

    return (loss.reshape(()), grad_x[None],
            *outs(g_s, g_win, g_pool, g_wout), *outs(d_s, d_win, d_pool, d_wout),
            *outs(m_s, m_win, m_pool, m_wout), *outs(v_s, v_win, v_pool, v_wout))
```

```python
import jax
import jax.numpy as jnp
from jax import lax
from jax.experimental import pallas as pl
from jax.experimental.pallas import tpu as pltpu

T = 2048
D = 1024
NSEG = 6
NTILE = 24
TILE = 256
DMIX = 2048
NDEV = 8
HEAD = 128
NHEAD = 8
CHUNK = 64
NCHUNK = T // CHUNK
NB = 32
NGRP = NCHUNK // NB
NGROUP = 4
GROUP = 256
EPS = 1e-6
EXP_CAP = 115.0
MESH = pl.DeviceIdType.MESH
ANY = pl.BlockSpec(memory_space=pl.ANY)
HBM = pl.BlockSpec(memory_space=pltpu.HBM)
SEM = pl.BlockSpec(memory_space=pltpu.SEMAPHORE)
EFFECT = pltpu.SideEffectType.DATAFLOW_SIDE_EFFECTING

ADAM_LR = 0.001
ADAM_B1 = 0.9
ADAM_B2 = 0.999
ADAM_EPS = 1e-08
ADAM_WD = 0.01
ADAM_STEP = 10
BC1 = 1.0 - ADAM_B1 ** ADAM_STEP
BC2 = 1.0 - ADAM_B2 ** ADAM_STEP

MIB = 1 << 20


def _params(sem=None, vmem_mib=48):
    return pltpu.CompilerParams(dimension_semantics=sem, vmem_limit_bytes=vmem_mib * MIB)


def _sigmoid(v):
    return 1.0 / (1.0 + jnp.exp(-v))


def _dot(a, b, ca, cb, precision=None):
    return lax.dot_general(a, b, (((ca,), (cb,)), ((), ())), precision=precision,
                           preferred_element_type=jnp.float32)


def _bf(v):
    return v.astype(jnp.bfloat16)


def _in_hbm(a):
    return pltpu.with_memory_space_constraint(a, pltpu.HBM)


def _place():
    x, y, c = lax.axis_index("x"), lax.axis_index("y"), lax.axis_index("c")
    return x, y, c, 4 * x + 2 * y + c


def _peer(x, y, c, r):
    return (x ^ ((r >> 2) & 1), y ^ ((r >> 1) & 1), c ^ (r & 1))


def _gather_proj(x, g1, w_in, w_out, pool_w):
    def body(x_ref, g_ref, win_ref, wout_ref, pool_ref, ht_o, wt_o, woutb_o, wout_o, pool_o, proj_o,
             xbuf, hv, htv, wv, wob, pb, stage, send_sems, recv_sems, loc_sems, out_sems):
        px, py, c, my_idx = _place()
        fetch_x = pltpu.make_async_copy(x_ref, xbuf, loc_sems.at[5])
        fetch_x.start()
        me, sibling = (px, py, c), (px, py, 1 - c)
        chips = [(1 - px, py), (px, 1 - py), (1 - px, 1 - py)]
        for p in range(3):
            wv[3 * my_idx + p] = _bf(win_ref[0, :, p * TILE:(p + 1) * TILE])

        def index(bx, by, bc):
            return 4 * bx + 2 * by + bc

        def slot(w, block):
            return wv.at[pl.ds(3 * index(*block), 3)] if w == 0 else pool_o.at[index(*block)]

        def copy(k, w, block, to, src=None):
            return pltpu.make_async_remote_copy(
                src_ref=slot(w, block) if src is None else src, dst_ref=slot(w, block),
                send_sem=send_sems.at[2 * k + w], recv_sem=recv_sems.at[2 * k + w],
                device_id=to, device_id_type=MESH)

        def save(block):
            at = pl.ds(3 * index(*block), 3)
            pltpu.make_async_copy(wv.at[at], wt_o.at[at], loc_sems.at[4]).start()

        srcs = (slot(0, me), pb)
        first = []
        for w in (0, 1):
            if w == 1:
                pb[...] = _bf(pool_ref[0])
                wob[...] = _bf(wout_ref[0])
            group = [copy(1 + j, w, me, (*chip, c), src=srcs[w]) for j, chip in enumerate(chips[:2])]
            group.append(copy(0, w, me, sibling, src=srcs[w]))
            for cp in group:
                cp.start()
            first += group
        save(me)
        locs = [pltpu.make_async_copy(pb, slot(1, me), loc_sems.at[0]),
                pltpu.make_async_copy(wob, wout_o.at[my_idx], loc_sems.at[1]),
                pltpu.make_async_copy(wob, woutb_o, loc_sems.at[2])]
        for cp in locs:
            cp.start()

        fetch_x.wait()
        xv = xbuf[...]
        hv[...] = _bf(xv * lax.rsqrt(jnp.mean(xv * xv, axis=-1, keepdims=True) + EPS) * g_ref[...])
        rows = 256
        for r0 in range(0, T, rows):
            htv[:, r0:r0 + rows] = hv[r0:r0 + rows, :].T
        locs.append(pltpu.make_async_copy(htv, ht_o, loc_sems.at[3]))
        locs[-1].start()

        def out_copy(p, j):
            return pltpu.make_async_copy(stage.at[p], proj_o.at[j], out_sems.at[p])

        def project(nth, block):
            base = 3 * index(*block)

            def tile(p, carry):
                if nth > 0:
                    out_copy(p, base + p).wait()
                stage[p] = _dot(hv[...], wv[base + p], 1, 0)
                out_copy(p, base + p).start()
                return carry

            lax.fori_loop(0, 3, tile, 0)

        project(0, me)
        copy(0, 0, sibling, me).wait_recv()
        save(sibling)
        project(1, sibling)
        passed = []
        relay_from = (px ^ (1 - c), py ^ c, c)
        relay_to = (px ^ c, py ^ (1 - c), c)

        def arrived(w, j):
            copy(1 + j, w, (*chips[j], c), me).wait_recv()
            passed.append(copy(4 + j, w, (*chips[j], c), sibling))
            passed[-1].start()

        def relay(w):
            passed.append(copy(3, w, relay_from, relay_to))
            passed[-1].start()

        def handed(nth, j):
            copy(4 + j, 0, (*chips[j], 1 - c), me).wait_recv()
            save((*chips[j], 1 - c))
            project(nth, (*chips[j], 1 - c))

        arrived(0, 0)
        arrived(0, 1)
        relay(0)
        for j in range(2):
            save((*chips[j], c))
            project(2 + j, (*chips[j], c))
        handed(4, 0)
        handed(5, 1)
        arrived(1, 0)
        arrived(1, 1)
        relay(1)
        arrived(0, 2)
        save((*chips[2], c))
        project(6, (*chips[2], c))
        handed(7, 2)
        arrived(1, 2)
        copy(0, 1, sibling, me).wait_recv()
        for j, chip in enumerate(chips):
            copy(4 + j, 1, (*chip, 1 - c), me).wait_recv()
        keep = pltpu.make_async_copy(wv, wt_o, loc_sems.at[4])
        for p in range(3):
            out_copy(p, p).wait()
        for cp in first + passed:
            cp.wait_send()
        keep.wait()
        for cp in locs:
            cp.wait()

    vmem = pl.BlockSpec(memory_space=pltpu.VMEM)
    bf16 = jnp.bfloat16
    return pl.pallas_call(
        body, name="gather_proj",
        out_shape=(pltpu.HBM((D, T), bf16), pltpu.HBM((NTILE, D, TILE), bf16),
                   pltpu.HBM((DMIX // NDEV, D), bf16), pltpu.HBM((NDEV, DMIX // NDEV, D), bf16),
                   pltpu.HBM((NDEV, NGROUP, GROUP // NDEV, GROUP), bf16), pltpu.HBM((NTILE, T, TILE), jnp.float32)),
        in_specs=[ANY] + [vmem] * 4, out_specs=(ANY,) * 6,
        scratch_shapes=[pltpu.VMEM((T, D), jnp.float32),
                        pltpu.VMEM((T, D), bf16), pltpu.VMEM((D, T), bf16), pltpu.VMEM((NTILE, D, TILE), bf16),
                        pltpu.VMEM((DMIX // NDEV, D), bf16), pltpu.VMEM((NGROUP, GROUP // NDEV, GROUP), bf16),
                        pltpu.VMEM((3, T, TILE), jnp.float32),
                        pltpu.SemaphoreType.DMA((14,)), pltpu.SemaphoreType.DMA((14,)),
                        pltpu.SemaphoreType.DMA((6,)), pltpu.SemaphoreType.DMA((3,))],
        compiler_params=_params(vmem_mib=56),
    )(x, g1, w_in, w_out, pool_w)


def _split_start(name, arrays, n_copies, plan):
    k = len(arrays)

    def body(*refs):
        send_sems, recv_sems, token = refs[k], refs[k + 1], refs[-1]
        for i, (src, dst, to) in enumerate(plan(refs[:k])):
            pltpu.make_async_remote_copy(src_ref=src, dst_ref=dst, send_sem=send_sems.at[i],
                                         recv_sem=recv_sems.at[i], device_id=to, device_id_type=MESH).start()
        token[...] = jnp.zeros_like(token)

    out = pl.pallas_call(
        body, name=name,
        out_shape=(pltpu.SemaphoreType.DMA((n_copies,)), pltpu.SemaphoreType.DMA((n_copies,)),
                   *[pltpu.HBM(a.shape, a.dtype) for a in arrays], jax.ShapeDtypeStruct((8, 128), jnp.float32)),
        in_specs=[HBM] * k, out_specs=(SEM, SEM, *[HBM] * k, pl.BlockSpec(memory_space=pltpu.VMEM)),
        input_output_aliases={i: 2 + i for i in range(k)},
        compiler_params=pltpu.CompilerParams(has_side_effects=EFFECT),
    )(*[pltpu.with_memory_space_constraint(a, pltpu.HBM) for a in arrays])
    return out[0], out[1], out[2:2 + k], out[-1]


def _split_wait(name, arrays, send_sems, recv_sems, plan, after):
    k = len(arrays)

    def body(*refs):
        sends, recvs = refs[k], refs[k + 1]
        for i, (src, dst, to) in enumerate(plan(refs[:k])):
            cp = pltpu.make_async_remote_copy(src_ref=src, dst_ref=dst, send_sem=sends.at[i], recv_sem=recvs.at[i],
                                              device_id=to, device_id_type=MESH)
            cp.wait_send()
            cp.wait_recv()

    return pl.pallas_call(
        body, name=name,
        out_shape=tuple(pltpu.HBM(a.shape, a.dtype) for a in arrays),
        in_specs=[HBM] * k + [SEM, SEM, ANY], out_specs=(HBM,) * k,
        input_output_aliases={i: i for i in range(k)},
        compiler_params=pltpu.CompilerParams(has_side_effects=EFFECT),
    )(*arrays, send_sems, recv_sems, after)


def _plan_wout(refs):
    src, land = refs
    x, y, c, me = _place()
    return [(src, land.at[me], _peer(x, y, c, r)) for r in range(1, NDEV)]


def _plan_rest(refs):
    gob, gpf, r_out, r_pool = refs
    x, y, c, me = _place()
    plan = []
    for r in range(1, NDEV):
        plan.append((gob.at[me ^ r], r_out.at[r - 1], _peer(x, y, c, r)))
        plan.append((gpf.at[me ^ r], r_pool.at[r - 1], _peer(x, y, c, r)))
    return plan


def _plan_in(refs):
    sums, landing = refs
    x, y, c, _ = _place()
    plan = []
    for rel, (dx, dy) in enumerate(((1, 0), (0, 1), (1, 1))):
        for p in range(3):
            plan.append((sums.at[rel, p], landing.at[rel, p], (x ^ dx, y ^ dy, c)))
    return plan


def _plan_small(refs):
    small, land = refs
    x, y, c, me = _place()
    return [(small, land.at[me], _peer(x, y, c, r)) for r in range(1, NDEV)]


_POOL_SPECS = [pl.BlockSpec((None, T, GROUP), lambda g, base=base: (base + g, 0, 0)) for base in (0, 4)]
_HEAD_SPECS = [pl.BlockSpec((None, T, HEAD), lambda h, base=base: (base + h // 2, 0, h % 2))
               for base in (8, 12, 16, 20)]


_POOL_W_SPEC = pl.BlockSpec((NDEV, None, GROUP // NDEV, GROUP), lambda g: (0, g, 0, 0))


def _row_ids(shape):
    return lax.broadcasted_iota(jnp.int32, shape, 0)


BAND_ROWS = 128
HALO = 16


def _window_sum(a, gidx, lead):
    width = lax.shift_left(jnp.int32(2), gidx)
    shape = (BAND_ROWS, BAND_ROWS + HALO)
    t, j = lax.broadcasted_iota(jnp.int32, shape, 0), lax.broadcasted_iota(jnp.int32, shape, 1)
    first = t if lead else t + HALO - width + 1
    band = _bf(jnp.where(j >= first, jnp.where(j < first + width, 1.0, 0.0), 0.0))
    zeros = jnp.zeros((HALO, a.shape[1]), jnp.bfloat16)
    padded = [jnp.concatenate([p, zeros] if lead else [zeros, p], axis=0) for p in _split2(a)]
    out = []
    for r0 in range(0, T, BAND_ROWS):
        slab = jnp.concatenate([p[r0:r0 + BAND_ROWS + HALO] for p in padded], axis=1)
        r = _dot(band, slab, 1, 0)
        out.append(r[:, :a.shape[1]] + r[:, a.shape[1]:])
    return jnp.concatenate(out, axis=0)


def _window_mean(s, gidx):
    inv = jnp.where(gidx == 0, 0.5, jnp.where(gidx == 1, 0.25, jnp.where(gidx == 2, 0.125, 0.0625)))
    width = lax.shift_left(jnp.int32(2), gidx)
    head = s[:16] / jnp.minimum(_row_ids((16, s.shape[1])) + 1, width).astype(jnp.float32)
    return jnp.concatenate([head, s[16:] * inv], axis=0)


def _pool_fwd(proj, pool_w, pool_scale, token):
    def body(u_ref, pg_ref, w_ref, sc_ref, token_any, y_ref):
        del token_any
        gidx = pl.program_id(0)
        u, pg = u_ref[...], pg_ref[...]
        d = _window_mean(_window_sum(u, gidx, False), gidx) - u
        mixed = _dot(_bf(d), w_ref[...].reshape(GROUP, GROUP), 1, 0)
        y_ref[...] = _bf(mixed * sc_ref[...] * (pg * _sigmoid(pg)))

    return pl.pallas_call(
        body, name="pool_fwd", grid=(NGROUP,),
        in_specs=[*_POOL_SPECS, _POOL_W_SPEC, pl.BlockSpec((1, GROUP), lambda g: (0, g)), ANY],
        out_specs=pl.BlockSpec((T, GROUP), lambda g: (0, g)),
        out_shape=pltpu.HBM((T, DMIX), jnp.bfloat16),
        compiler_params=_params(("parallel",)),
    )(proj, proj, pool_w, pool_scale, token)


def _tri(lower):
    r = lax.broadcasted_iota(jnp.int32, (CHUNK, CHUNK), 0)
    c = lax.broadcasted_iota(jnp.int32, (CHUNK, CHUNK), 1)
    return (r >= c) if lower else (r <= c)


def _sum_rows_matrix():
    shape = (CHUNK + 16, CHUNK)
    r, c = lax.broadcasted_iota(jnp.int32, shape, 0), lax.broadcasted_iota(jnp.int32, shape, 1)
    run = jnp.where(c <= r, 1.0, 0.0)
    half = jnp.where(c < CHUNK // 2, 1.0, 0.0)
    return _bf(jnp.where(r < CHUNK, run, jnp.where(r < CHUNK + 8, 1.0, half)))


def _rev_sum_matrix():
    shape = (CHUNK, 2 * CHUNK)
    r, c = lax.broadcasted_iota(jnp.int32, shape, 0), lax.broadcasted_iota(jnp.int32, shape, 1)
    return _bf(jnp.where(c < CHUNK, jnp.where(c >= r, 1.0, 0.0), jnp.where(c - CHUNK < r, 1.0, 0.0)))


def _split2(a):
    hi = _bf(a)
    return [hi, _bf(a - hi.astype(jnp.float32))]


def _exact_sums(mat, pieces):
    x = jnp.concatenate([s for p in pieces for s in _split2(p)], axis=1)
    r = _dot(mat, x, 1, 0)
    return [r[:, 2 * j * HEAD:(2 * j + 1) * HEAD] + r[:, (2 * j + 1) * HEAD:(2 * j + 2) * HEAD]
            for j in range(len(pieces))]


def _gates(qv, fl, lb):
    sq = _sigmoid(qv)
    sg = _sigmoid(fl)
    f = lb + (1.0 - lb) * sg
    return dict(sq=sq, qs=qv * sq, sg=sg, f=f, kk=1.0 - f, g=jnp.log2(f))


def _decays(sums):
    big_g = sums[:CHUNK]
    total = sums[CHUNK:CHUNK + 8]
    g_last = jnp.tile(total, (CHUNK // 8, 1))
    g_mid = jnp.tile(sums[CHUNK + 8:], (CHUNK // 8, 1))
    return dict(
        e_q=jnp.exp2(big_g),
        e_k=jnp.exp2(g_last - big_g),
        e_qm=jnp.exp2(jnp.minimum(big_g - g_mid, EXP_CAP)),
        e_km=jnp.exp2(jnp.minimum(g_mid - big_g, EXP_CAP)),
        total8=jnp.exp2(total))


def _group_rows(gi):
    return [pl.ds(pl.multiple_of((gi * NB + j) * CHUNK, CHUNK), CHUNK) for j in range(NB)]


def _lower_bound(lb_ref):
    return _sigmoid(lb_ref[0:1, :] - lb_ref[1:2, :])


def _hgrn_fwd(proj, lb_logits, rec_g, y_in):
    def body(q_ref, f_ref, i_ref, gate_ref, lb_ref, rg_ref, y_any, y_ref, o_ref, st_ref):
        del y_any
        lb = _lower_bound(lb_ref)
        causal = _tri(True)
        smat = _sum_rows_matrix()

        def group(gi, st):
            rows = _group_rows(gi)
            ts = [_gates(q_ref[r, :], f_ref[r, :], lb) for r in rows]
            ds = [_decays(s) for s in _exact_sums(smat, [t["g"] for t in ts])]
            vs = [_bf(i_ref[r, :]) for r in rows]
            q_m = [_bf(t["qs"] * d["e_qm"]) for t, d in zip(ts, ds)]
            k_m = [_bf(t["kk"] * d["e_km"]) for t, d in zip(ts, ds)]
            q_e = [_bf(t["qs"] * d["e_q"]) for t, d in zip(ts, ds)]
            k_e = [_bf(t["kk"] * d["e_k"]) for t, d in zip(ts, ds)]
            a = [_bf(jnp.where(causal, _dot(q_m[j], k_m[j], 1, 1), 0.0)) for j in range(NB)]
            intra = [_dot(a[j], vs[j], 1, 0) for j in range(NB)]
            upd = [_dot(vs[j], k_e[j], 0, 0) for j in range(NB)]
            for j in range(NB):
                st_ref[gi * NB + j] = st
                o_ref[rows[j], :] = intra[j] + _dot(q_e[j], _bf(st), 1, 1)
                st = st * jnp.tile(ds[j]["total8"], (HEAD // 8, 1)) + upd[j]
            return st

        lax.fori_loop(0, NGRP, group, jnp.zeros((HEAD, HEAD), jnp.float32))
        o = o_ref[...]
        rn = o * lax.rsqrt(jnp.mean(o * o, axis=-1, keepdims=True) + EPS)
        gate = gate_ref[...]
        y_ref[...] = _bf(rn * rg_ref[...] * (gate * _sigmoid(gate)))

    return pl.pallas_call(
        body, name="hgrn_fwd", grid=(NHEAD,),
        in_specs=[*_HEAD_SPECS,
                  pl.BlockSpec((2, HEAD), lambda h: (0, h)),
                  pl.BlockSpec((1, HEAD), lambda h: (0, h)),
                  pl.BlockSpec(memory_space=pl.ANY)],
        out_specs=(pl.BlockSpec((T, HEAD), lambda h: (0, NHEAD + h)),
                   pl.BlockSpec((T, HEAD), lambda h: (0, h)),
                   pl.BlockSpec((None, NCHUNK, HEAD, HEAD), lambda h: (h, 0, 0, 0))),
        out_shape=(pltpu.HBM((T, DMIX), jnp.bfloat16), pltpu.HBM((T, D), jnp.float32),
                   pltpu.HBM((NHEAD, NCHUNK, HEAD, HEAD), jnp.float32)),
        input_output_aliases={6: 0},
        compiler_params=_params(("parallel",)),
    )(proj, proj, proj, proj, lb_logits, rec_g, y_in)


def _out_proj_loss(x, y, w_out, target, gf):
    rows = 512
    parts = [slice(k * rows // 2, (k + 1) * rows // 2) for k in range(2)]

    def body(x_ref, y_ref, w_ref, t_ref, g_ref, dz_ref, dzb_ref, sq_ref, dg_ref):
        zs = [x_ref[p, :] + _dot(y_ref[p, :], w_ref[...], 1, 0) for p in parts]
        sq = dg = 0.0
        for p, z in zip(parts, zs):
            r = lax.rsqrt(jnp.mean(z * z, axis=-1, keepdims=True) + EPS)
            zhat = z * r
            err = zhat * g_ref[...] - t_ref[p, :]
            dy = err * (1.0 / D)
            gdy = dy * g_ref[...]
            dz = r * (gdy - zhat * jnp.mean(zhat * gdy, axis=-1, keepdims=True))
            dz_ref[p, :] = dz
            dzb_ref[p, :] = _bf(dz)
            sq = sq + jnp.sum(err * err, axis=0, keepdims=True)
            dg = dg + jnp.sum(zhat * dy, axis=0, keepdims=True)

        @pl.when(pl.program_id(0) == 0)
        def _():
            sq_ref[...] = sq
            dg_ref[...] = dg

        @pl.when(pl.program_id(0) != 0)
        def _():
            sq_ref[...] += sq
            dg_ref[...] += dg

    tile = pl.BlockSpec((rows, D), lambda i: (i, 0))
    vec = pl.BlockSpec((1, D), lambda i: (0, 0))
    return pl.pallas_call(
        body, name="out_proj_loss", grid=(T // rows,),
        in_specs=[tile, pl.BlockSpec((rows, DMIX), lambda i: (i, 0)), pl.BlockSpec((DMIX, D), lambda i: (0, 0)),
                  tile, vec],
        out_specs=(tile, tile, vec, vec),
        out_shape=(pltpu.HBM((T, D), jnp.float32), pltpu.HBM((T, D), jnp.bfloat16),
                   pltpu.HBM((1, D), jnp.float32), pltpu.HBM((1, D), jnp.float32)),
        compiler_params=_params(("arbitrary",)),
    )(x, y, w_out, target, gf)


def _out_proj_bwd(dzb, w_out, y):
    tn = 512

    def body(dz_ref, w_ref, y_ref, dy_ref, gw_ref, gwb_ref):
        dz = dz_ref[...]
        dy_ref[...] = _dot(dz, w_ref[...], 1, 1)
        gw = _dot(y_ref[...], dz, 0, 0)
        gw_ref[...] = gw
        gwb_ref[...] = _bf(gw)

    return pl.pallas_call(
        body, name="out_proj_bwd", grid=(DMIX // tn,),
        in_specs=[pl.BlockSpec((T, D), lambda n: (0, 0)), pl.BlockSpec((tn, D), lambda n: (n, 0)),
                  pl.BlockSpec((T, tn), lambda n: (0, n))],
        out_specs=(pl.BlockSpec((T, tn), lambda n: (0, n)), pl.BlockSpec((tn, D), lambda n: (n, 0)),
                   pl.BlockSpec((tn, D), lambda n: (n, 0))),
        out_shape=(pltpu.HBM((T, DMIX), jnp.float32), pltpu.HBM((DMIX, D), jnp.float32),
                   pltpu.HBM((DMIX, D), jnp.bfloat16)),
        compiler_params=_params(("parallel",)),
    )(dzb, w_out, y)


def _hgrn_bwd(proj, lb_logits, rec_g, o, states, dymix, dproj_in, token):
    def body(q_ref, f_ref, i_ref, gate_ref, lb_ref, rg_ref, o_ref, st_ref, dy_ref, dp_any, token_any,
             dp_ref, drg_ref, dlb_ref, do_ref):
        del dp_any, token_any
        lb = _lower_bound(lb_ref)
        causal = _tri(True)
        smat, rmat = _sum_rows_matrix(), _rev_sum_matrix()

        o = o_ref[...]
        rs = lax.rsqrt(jnp.mean(o * o, axis=-1, keepdims=True) + EPS)
        rn = o * rs
        gate = gate_ref[...]
        sgate = _sigmoid(gate)
        dyv = dy_ref[...]
        d_r = dyv * (gate * sgate)
        dp_ref[3] = _bf(dyv * (rn * rg_ref[...]) * (sgate * (1.0 + gate * (1.0 - sgate))))
        drg_ref[...] = jnp.sum(d_r * rn, axis=0, keepdims=True)
        drn = d_r * rg_ref[...]
        do_ref[...] = rs * (drn - rn * jnp.mean(rn * drn, axis=-1, keepdims=True))

        def group(i, carry):
            dst, dlb = carry
            gi = NGRP - 1 - i
            rows = _group_rows(gi)
            span = range(NB)
            qvs = [q_ref[r, :] for r in rows]
            ts = [_gates(qv, f_ref[r, :], lb) for qv, r in zip(qvs, rows)]
            ds = [_decays(s) for s in _exact_sums(smat, [t["g"] for t in ts])]
            vs = [_bf(i_ref[r, :]) for r in rows]
            dos = [_bf(do_ref[r, :]) for r in rows]
            sts = [st_ref[gi * NB + j] for j in span]
            qe_f = [t["qs"] * d["e_q"] for t, d in zip(ts, ds)]
            ke_f = [t["kk"] * d["e_k"] for t, d in zip(ts, ds)]
            q_e, k_e = [_bf(a) for a in qe_f], [_bf(a) for a in ke_f]
            q_m = [_bf(t["qs"] * d["e_qm"]) for t, d in zip(ts, ds)]
            k_m = [_bf(t["kk"] * d["e_km"]) for t, d in zip(ts, ds)]
            a = [_bf(jnp.where(causal, _dot(q_m[j], k_m[j], 1, 1), 0.0)) for j in span]
            da = [_bf(jnp.where(causal, _dot(dos[j], vs[j], 1, 1), 0.0)) for j in span]
            dqm = [_dot(da[j], k_m[j], 1, 0) for j in span]
            dkm = [_dot(da[j], q_m[j], 0, 0) for j in span]
            dv_in = [_dot(a[j], dos[j], 0, 0) for j in span]
            dqe = [_dot(dos[j], _bf(sts[j]), 1, 0) for j in span]
            grow = [_dot(dos[j], q_e[j], 0, 0) for j in span]
            dke, carried = [None] * NB, [None] * NB
            for j in reversed(span):
                dst_b = _bf(dst)
                dke[j] = _dot(vs[j], dst_b, 1, 0)
                dp_ref[2, rows[j], :] = _bf(dv_in[j] + _dot(k_e[j], dst_b, 1, 1))
                carried[j] = ds[j]["total8"] * jnp.sum(dst * sts[j], axis=0, keepdims=True)
                dst = dst * jnp.tile(ds[j]["total8"], (HEAD // 8, 1)) + grow[j]
            kdk = [ke_f[j] * dke[j] for j in span]
            pos = [(q_m[j].astype(jnp.float32) * dqm[j] - k_m[j].astype(jnp.float32) * dkm[j]) + qe_f[j] * dqe[j]
                   for j in span]
            dgs = _exact_sums(rmat, [jnp.concatenate([pos[j], kdk[j]], axis=0) for j in span])
            for j in span:
                t, d = ts[j], ds[j]
                dg = dgs[j] + jnp.tile(carried[j], (CHUNK // 8, 1))
                dqs = dqm[j] * d["e_qm"] + dqe[j] * d["e_q"]
                dkk = dkm[j] * d["e_km"] + dke[j] * d["e_k"]
                df = dg / t["f"] - dkk
                dp_ref[1, rows[j], :] = _bf(df * (1.0 - lb) * (t["sg"] * (1.0 - t["sg"])))
                dp_ref[0, rows[j], :] = _bf(dqs * (t["sq"] * (1.0 + qvs[j] * (1.0 - t["sq"]))))
                dlb = dlb + df * (1.0 - t["sg"])
            return dst, dlb

        _, dlb = lax.fori_loop(0, NGRP, group, (jnp.zeros((HEAD, HEAD), jnp.float32),
                                                jnp.zeros((CHUNK, HEAD), jnp.float32)))
        dlb_ref[...] = jnp.sum(dlb, axis=0, keepdims=True)

    vec = pl.BlockSpec((1, HEAD), lambda h: (0, h))
    return pl.pallas_call(
        body, name="hgrn_bwd", grid=(NHEAD,),
        in_specs=[*_HEAD_SPECS,
                  pl.BlockSpec((2, HEAD), lambda h: (0, h)), vec,
                  pl.BlockSpec((T, HEAD), lambda h: (0, h)),
                  pl.BlockSpec((None, NCHUNK, HEAD, HEAD), lambda h: (h, 0, 0, 0)),
                  pl.BlockSpec((T, HEAD), lambda h: (0, NHEAD + h)), ANY, ANY],
        out_specs=(pl.BlockSpec((4, T, HEAD), lambda h: (0, 0, h)), vec, vec),
        out_shape=(pltpu.HBM((NSEG, T, D), jnp.bfloat16),
                   pltpu.HBM((1, D), jnp.float32), pltpu.HBM((1, D), jnp.float32)),
        scratch_shapes=[pltpu.VMEM((T, HEAD), jnp.float32)],
        input_output_aliases={9: 0},
        compiler_params=_params(("parallel",)),
    )(proj, proj, proj, proj, lb_logits, rec_g, o, states, dymix, dproj_in, token)


def _pool_bwd(proj, pool_w, pool_scale, dymix):
    def body(u_ref, pg_ref, w_ref, sc_ref, dy_ref, dp_ref, gw_ref, gs_ref):
        gidx = pl.program_id(0)
        u, pg = u_ref[...], pg_ref[...]
        w = w_ref[...].reshape(GROUP, GROUP)
        d = _bf(_window_mean(_window_sum(u, gidx, False), gidx) - u)
        mixed = _dot(d, w, 1, 0)
        spg = _sigmoid(pg)
        dyv = dy_ref[...]
        d_p = dyv * (pg * spg)
        dp_ref[1] = _bf(dyv * (mixed * sc_ref[...]) * (spg * (1.0 + pg * (1.0 - spg))))
        gs_ref[...] = jnp.sum(d_p * mixed, axis=0, keepdims=True)
        dmixed = _bf(d_p * sc_ref[...])
        gw_ref[...] = _dot(d, dmixed, 0, 0).reshape(gw_ref.shape)
        dd = _dot(dmixed, w, 1, 1)
        dp_ref[0] = _bf(_window_sum(_window_mean(dd, gidx), gidx, True) - dd)

    return pl.pallas_call(
        body, name="pool_bwd", grid=(NGROUP,),
        in_specs=[*_POOL_SPECS, _POOL_W_SPEC,
                  pl.BlockSpec((1, GROUP), lambda g: (0, g)),
                  pl.BlockSpec((T, GROUP), lambda g: (0, g))],
        out_specs=(pl.BlockSpec((2, T, GROUP), lambda g: (2, 0, g)), _POOL_W_SPEC,
                   pl.BlockSpec((1, GROUP), lambda g: (0, g))),
        out_shape=(pltpu.HBM((NSEG, T, D), jnp.bfloat16),
                   pltpu.HBM((NDEV, NGROUP, GROUP // NDEV, GROUP), jnp.float32),
                   pltpu.HBM((1, D), jnp.float32)),
        compiler_params=_params(("parallel",)),
    )(proj, proj, pool_w, pool_scale, dymix)


HALF = NTILE // 2
AWAY = HALF - 3


def _dproj_tile(chip, side, p):
    j = 6 * chip + 3 * side + p
    return ((j // 4 + 4) % NSEG, 0, j % 4)


def _sibling_copy(sib_out, sib_in, send_sems, recv_sems, slot):
    x, y, c, _ = _place()
    return pltpu.make_async_remote_copy(
        src_ref=sib_out.at[slot], dst_ref=sib_in.at[slot], send_sem=send_sems.at[slot],
        recv_sem=recv_sems.at[slot], device_id=(x, y, 1 - c), device_id_type=MESH)


def _proj_bwd_w_far(place, ht, dproj):
    def body(place_ref, h_ref, dpa_ref, dpb_ref, sib_out, sib_in, send_sems, recv_sems, stage, loc_sems):
        del place_ref
        i = pl.program_id(0)

        def to_hbm(k):
            return pltpu.make_async_copy(stage.at[k], sib_out.at[k], loc_sems.at[k])

        def send(k):
            to_hbm(k).wait()
            _sibling_copy(sib_out, sib_in, send_sems, recv_sems, k).start()

        stage[2 * i] = _bf(_dot(h_ref[...], dpa_ref[...], 1, 0))
        stage[2 * i + 1] = _bf(_dot(h_ref[...], dpb_ref[...], 1, 0))

        @pl.when(i > 0)
        def _():
            send(2 * i - 2)
            send(2 * i - 1)

        to_hbm(2 * i).start()
        to_hbm(2 * i + 1).start()

        @pl.when(i == HALF // 2 - 1)
        def _():
            send(2 * i)
            send(2 * i + 1)

    def tile(k, pr):
        return _dproj_tile(k // 3, 1 - pr[2], k % 3)

    buf = pltpu.HBM((HALF, D, TILE), jnp.bfloat16)
    sems = pltpu.SemaphoreType.DMA((HALF,))
    return pl.pallas_call(
        body, name="proj_bwd_w_far",
        grid_spec=pltpu.PrefetchScalarGridSpec(
            num_scalar_prefetch=1, grid=(HALF // 2,),
            in_specs=[pl.BlockSpec((D, T), lambda i, pr: (0, 0)),
                      pl.BlockSpec((None, T, TILE), lambda i, pr: tile(2 * i, pr)),
                      pl.BlockSpec((None, T, TILE), lambda i, pr: tile(2 * i + 1, pr))],
            out_specs=(HBM, HBM, SEM, SEM),
            scratch_shapes=[pltpu.VMEM((HALF, D, TILE), jnp.bfloat16), pltpu.SemaphoreType.DMA((HALF,))]),
        out_shape=(buf, buf, sems, sems),
        compiler_params=pltpu.CompilerParams(dimension_semantics=("arbitrary",), vmem_limit_bytes=48 * MIB,
                                             has_side_effects=EFFECT),
    )(place, ht, dproj, dproj)


def _proj_bwd_w_near(place, ht, dproj, sib_out, sib_in, sib_send, sib_recv):
    def owner_chip(k, pr):
        return jnp.where(k < AWAY, (pr[1] + 1 + k % 3) % 4, pr[1])

    def tile_p(k):
        return jnp.where(k < AWAY, k // 3, k - AWAY)

    def body(place_ref, h_ref, dp_ref, sib_out, sib_in, sib_send, sib_recv, sums, own_ref, landing, out_send,
             out_recv, recvbuf, outbuf, in_sems, loc_sems):
        i = pl.program_id(0)
        px, py, c, _ = _place()

        def slot_of(k):
            return 3 * owner_chip(k, place_ref) + tile_p(k)

        def load(k):
            return pltpu.make_async_copy(sib_in.at[slot_of(k)], recvbuf.at[k % 2], in_sems.at[k % 2])

        def fetch(k):
            _sibling_copy(sib_out, sib_in, sib_send, sib_recv, slot_of(k)).wait_recv()
            load(k).start()

        def route(k):
            chip = owner_chip(k, place_ref)
            cx, cy = chip // 2, chip % 2
            return cx, cy, (cx ^ px) + 2 * (cy ^ py) - 1, tile_p(k)

        def to_hbm(k):
            _, _, rel, p = route(k)
            return pltpu.make_async_copy(outbuf.at[k], sums.at[rel, p], loc_sems.at[k])

        def to_owner(k):
            cx, cy, rel, p = route(k)
            return pltpu.make_async_remote_copy(
                src_ref=sums.at[rel, p], dst_ref=landing.at[rel, p], send_sem=out_send.at[3 * rel + p],
                recv_sem=out_recv.at[3 * rel + p], device_id=(cx, cy, c), device_id_type=MESH)

        @pl.when(i == 0)
        def _():
            fetch(i)

        @pl.when(i < HALF - 1)
        def _():
            fetch(i + 1)

        load(i).wait()
        total = _dot(h_ref[...], dp_ref[...], 1, 0) + recvbuf[i % 2].astype(jnp.float32)
        own_ref[...] = total
        outbuf[jnp.minimum(i, AWAY)] = _bf(total)

        @pl.when(i < AWAY)
        def _():
            to_hbm(i).start()

        @pl.when(jnp.logical_and(i > 0, i <= AWAY))
        def _():
            to_hbm(i - 1).wait()
            to_owner(i - 1).start()

        @pl.when(i == HALF - 1)
        def _():
            for slot in range(HALF):
                _sibling_copy(sib_out, sib_in, sib_send, sib_recv, slot).wait_send()

    travelling = pltpu.HBM((3, 3, D, TILE), jnp.bfloat16)
    sems = pltpu.SemaphoreType.DMA((AWAY,))
    return pl.pallas_call(
        body, name="proj_bwd_w_near",
        grid_spec=pltpu.PrefetchScalarGridSpec(
            num_scalar_prefetch=1, grid=(HALF,),
            in_specs=[pl.BlockSpec((D, T), lambda i, pr: (0, 0)),
                      pl.BlockSpec((None, T, TILE), lambda i, pr: _dproj_tile(owner_chip(i, pr), pr[2], tile_p(i))),
                      HBM, HBM, SEM, SEM],
            out_specs=(HBM, pl.BlockSpec((None, D, TILE), lambda i, pr: (jnp.where(i < AWAY, 0, i % 3), 0, 0)),
                       HBM, SEM, SEM),
            scratch_shapes=[pltpu.VMEM((2, D, TILE), jnp.bfloat16), pltpu.VMEM((AWAY + 1, D, TILE), jnp.bfloat16),
                            pltpu.SemaphoreType.DMA((2,)), pltpu.SemaphoreType.DMA((AWAY,))]),
        out_shape=(travelling, pltpu.HBM((3, D, TILE), jnp.float32), travelling, sems, sems),
        compiler_params=pltpu.CompilerParams(dimension_semantics=("arbitrary",), vmem_limit_bytes=48 * MIB,
                                             has_side_effects=EFFECT),
    )(place, ht, dproj, sib_out, sib_in, sib_send, sib_recv)


def _proj_bwd_x(dproj, w_t, x, g1, dz, others, token):
    tm = 1024
    pairs = NSEG // 2
    k = len(others)

    def body(dp_ref, w_ref, x_any, g_ref, dz_any, *refs):
        dx_ref, dg_ref, wcat, acc, xbuf, dzbuf, row_sems = refs[k + 1:]
        m, s = pl.program_id(0), pl.program_id(1)
        mine = pl.ds(pl.multiple_of(m * tm, tm), tm)
        fetches = [pltpu.make_async_copy(x_any.at[mine], xbuf, row_sems.at[0]),
                   pltpu.make_async_copy(dz_any.at[mine], dzbuf, row_sems.at[1])]

        @pl.when(m == 0)
        def _():
            for i in range(8):
                wcat[s, :, i * TILE:(i + 1) * TILE] = w_ref[i]

        @pl.when(s == 0)
        def _():
            for cp in fetches:
                cp.start()
            acc[...] = jnp.zeros((tm, D), jnp.float32)

        acc[...] += _dot(jnp.concatenate([dp_ref[0], dp_ref[1]], axis=1), wcat[s], 1, 1)

        @pl.when(s == pairs - 1)
        def _():
            for cp in fetches:
                cp.wait()
            xv = xbuf[...]
            rs = lax.rsqrt(jnp.mean(xv * xv, axis=-1, keepdims=True) + EPS)
            xhat = xv * rs
            dhv = acc[...]
            gdh = dhv * g_ref[...]
            dx_ref[...] = dzbuf[...] + rs * (gdh - xhat * jnp.mean(xhat * gdh, axis=-1, keepdims=True))
            dg = jnp.sum(xhat * dhv, axis=0, keepdims=True)

            @pl.when(m == 0)
            def _():
                dg_ref[0:1, :] = dg
                for r, ref in enumerate(refs[:k]):
                    dg_ref[1 + r:2 + r, :] = ref[...]
                dg_ref[1 + k:, :] = jnp.zeros((7 - k, D), jnp.float32)

            @pl.when(m != 0)
            def _():
                dg_ref[0:1, :] += dg

    rows = pl.BlockSpec((tm, D), lambda m, s: (m, 0))
    vec = pl.BlockSpec((1, D), lambda m, s: (0, 0))
    return pl.pallas_call(
        body, name="proj_bwd_x", grid=(T // tm, pairs),
        in_specs=[pl.BlockSpec((2, tm, D), lambda m, s: (s, m, 0)),
                  pl.BlockSpec((8, D, TILE), lambda m, s: ((jnp.where(m == 0, s, pairs - 1) + 1) % pairs, 0, 0)),
                  HBM, vec, HBM, *[vec] * k, ANY],
        out_specs=(rows, pl.BlockSpec((8, D), lambda m, s: (0, 0))),
        out_shape=(jax.ShapeDtypeStruct((T, D), jnp.float32), pltpu.HBM((8, D), jnp.float32)),
        scratch_shapes=[pltpu.VMEM((pairs, D, 2 * D), jnp.bfloat16), pltpu.VMEM((tm, D), jnp.float32),
                        pltpu.VMEM((tm, D), jnp.float32), pltpu.VMEM((tm, D), jnp.float32),
                        pltpu.SemaphoreType.DMA((2,))],
        compiler_params=_params(("arbitrary", "arbitrary"), vmem_mib=58),
    )(dproj, w_t, _in_hbm(x), g1, _in_hbm(dz), *[_in_hbm(a) for a in others], token)


def _adamw(w, g, m, v):
    m_new = ADAM_B1 * m + (1.0 - ADAM_B1) * g
    v_new = ADAM_B2 * v + (1.0 - ADAM_B2) * (g * g)
    delta = -ADAM_LR * ((m_new / BC1) / (jnp.sqrt(v_new / BC2) + ADAM_EPS) + ADAM_WD * w)
    return delta, m_new, v_new


def _reduce_adam(name, place, parts, w, m, v, grid, w_spec):
    n = len(parts)

    def body(place_ref, *refs):
        del place_ref
        w_ref, m_ref, v_ref, g_ref, d_ref, mo_ref, vo_ref = refs[n:]
        g = None
        for ref, (_, _, stacked) in zip(refs[:n], parts):
            terms = [ref[r] for r in range(ref.shape[0])] if stacked else [ref[...]]
            for t in terms:
                if t.shape[-1] != w_ref.shape[-1]:
                    t = jnp.concatenate([t[p] for p in range(t.shape[0])], axis=1)
                g = t.astype(jnp.float32) if g is None else g + t.astype(jnp.float32)
        delta, m_new, v_new = _adamw(w_ref[...], g, m_ref[...], v_ref[...])
        g_ref[...] = g
        d_ref[...] = delta
        mo_ref[...] = m_new
        vo_ref[...] = v_new

    shape = jax.ShapeDtypeStruct(w.shape, jnp.float32)
    return pl.pallas_call(
        body, name=name,
        grid_spec=pltpu.PrefetchScalarGridSpec(
            num_scalar_prefetch=1, grid=grid,
            in_specs=[spec for _, spec, _ in parts] + [w_spec] * 3, out_specs=(w_spec,) * 4),
        out_shape=(shape,) * 4,
        compiler_params=_params(("parallel",)),
    )(place, *[_in_hbm(a) for a in [a for a, _, _ in parts] + [w, m, v]])


SMALL_ROWS = (1, 1, 2, 1, 1)


def _small_adam(place, own, parts, ws, ms, vs):
    n = len(SMALL_ROWS)

    def body(place_ref, own_ref, p_ref, *refs):
        ins, outs, bufs = refs[:3 * n], refs[3 * n:3 * n + 1 + 4 * n], refs[3 * n + 1 + 4 * n:]

        def stacked(group, buf):
            r0 = 0
            for ref, k in zip(group, SMALL_ROWS):
                buf[r0:r0 + k, :] = ref[...]
                r0 += k
            buf[r0:, :] = jnp.zeros((8 - r0, D), jnp.float32)
            return buf[...]

        wv, mv, vv = (stacked(ins[n * j:n * j + n], bufs[j]) for j in range(3))
        me = place_ref[0]
        g = None
        for s in range(NDEV):
            term = jnp.where(me == s, own_ref[...], p_ref[s])
            g = term if g is None else g + term
        rows = _row_ids(wv.shape)
        other = jnp.where(rows == 2, pltpu.roll(wv, 7, 0), jnp.where(rows == 3, pltpu.roll(wv, 1, 0), 0.0))
        lbv = _sigmoid(wv - other)
        sign = jnp.where(rows == 2, 1.0, -1.0)
        g = jnp.where((rows == 2) | (rows == 3), sign * g * lbv * (1.0 - lbv), g)
        delta, m_new, v_new = _adamw(wv, g, mv, vv)
        outs[0][...] = jnp.sum(g[6:7], axis=1, keepdims=True) * (0.5 / D)
        for j, val in enumerate((g, delta, m_new, v_new)):
            r0 = 0
            for ref, k in zip(outs[1 + n * j:1 + n * j + n], SMALL_ROWS):
                ref[...] = val[r0:r0 + k]
                r0 += k

    vmem = pl.BlockSpec(memory_space=pltpu.VMEM)
    shapes = [jax.ShapeDtypeStruct((k, D), jnp.float32) for k in SMALL_ROWS]
    out = pl.pallas_call(
        body, name="small_adam", out_shape=(jax.ShapeDtypeStruct((1, 1), jnp.float32), *shapes * 4),
        in_specs=[pl.BlockSpec(memory_space=pltpu.SMEM)] + [vmem] * (2 + 3 * n), out_specs=(vmem,) * (1 + 4 * n),
        scratch_shapes=[pltpu.VMEM((8, D), jnp.float32)] * 3,
    )(place, own, parts, *ws, *ms, *vs)
    return out[0], [out[1 + n * j:1 + n * j + n] for j in range(4)]


def kernel(x, norm1_g, w_in, pool_w, pool_scale, lb_logits, rec_norm_g, w_out, final_norm_g, loss_target, m_norm1_g, m_w_in, m_pool_w, m_pool_scale, m_lb_logits, m_rec_norm_g, m_w_out, m_final_norm_g, v_norm1_g, v_w_in, v_pool_w, v_pool_scale, v_lb_logits, v_rec_norm_g, v_w_out, v_final_norm_g):
    xs = x[0]
    target = loss_target[0]
    ix, iy, ic = lax.axis_index("x"), lax.axis_index("y"), lax.axis_index("c")
    place = jnp.stack([4 * ix + 2 * iy + ic, 2 * ix + iy, ic]).astype(jnp.int32)
    gf = final_norm_g.reshape(1, D)

    ht, w_t, w_out_b, w_out_g, pool_g, proj = _gather_proj(xs, norm1_g, w_in, w_out, pool_w)
    wout = [w_out_b, w_out_g]
    wout_send, wout_recv, wout, wout_token = _split_start("gather_wout_start", wout, NDEV - 1, _plan_wout)

    y = _pool_fwd(proj, pool_g, pool_scale, wout_token)
    y, o, states = _hgrn_fwd(proj, lb_logits, rec_norm_g, y)
    _, w_out_g = _split_wait("gather_wout_wait", wout, wout_send, wout_recv, _plan_wout, o)
    w_out_full = _in_hbm(w_out_g.reshape(DMIX, D))
    dz, dzb, sq, dgf = _out_proj_loss(xs, y, w_out_full, target, gf)

    dymix, gwout_f, gwout_b = _out_proj_bwd(dzb, w_out_full, y)
    dproj, gpool, dscale = _pool_bwd(proj, pool_g, pool_scale, dymix)

    blk_out = (NDEV, DMIX // NDEV, D)
    blk_pool = (NDEV, NGROUP, GROUP // NDEV, GROUP)
    rest = [gwout_b.reshape(blk_out), gpool,
            lax.empty((NDEV - 1,) + blk_out[1:], jnp.bfloat16), lax.empty((NDEV - 1,) + blk_pool[1:], jnp.float32)]
    rest_send, rest_recv, rest, rest_token = _split_start("scatter_rest_start", rest, 2 * (NDEV - 1), _plan_rest)

    dproj, drecg, dlb = _hgrn_bwd(proj, lb_logits, rec_norm_g, o, states, dymix, dproj, rest_token)
    chip_sums, own_sum, landing, win_send, win_recv = _proj_bwd_w_near(
        place, ht, dproj, *_proj_bwd_w_far(place, ht, dproj))
    win = [chip_sums, landing]

    others = [a.reshape(1, D) for a in (dscale, dlb, dlb, drecg, dgf, sq)]
    grad_x, small_block = _proj_bwd_x(dproj, w_t, xs, norm1_g, dz, others, chip_sums)

    small = [small_block, lax.empty((NDEV, 8, D), jnp.float32)]
    small_send, small_recv, small, small_token = _split_start("gather_small_start", small, NDEV - 1, _plan_small)

    _, gpool_own, r_out, r_pool = _split_wait("scatter_rest_wait", rest, rest_send, rest_recv, _plan_rest,
                                              small_token)
    g_wout, d_wout, m_wout, v_wout = _reduce_adam(
        "adam_w_out", place,
        [(gwout_f.reshape(blk_out), pl.BlockSpec((None,) + blk_out[1:], lambda i, pr: (pr[0], 0, 0)), False),
         (r_out, pl.BlockSpec((NDEV - 1,) + blk_out[1:], lambda i, pr: (0, 0, 0)), True)],
        w_out, m_w_out, v_w_out, (1,), pl.BlockSpec((None,) + blk_out[1:], lambda i, pr: (0, 0, 0)))
    g_pool, d_pool, m_pool, v_pool = _reduce_adam(
        "adam_pool_w", place,
        [(gpool_own, pl.BlockSpec((None,) + blk_pool[1:], lambda i, pr: (pr[0], 0, 0, 0)), False),
         (r_pool, pl.BlockSpec((NDEV - 1,) + blk_pool[1:], lambda i, pr: (0, 0, 0, 0)), True)],
        pool_w, m_pool_w, v_pool_w, (1,), pl.BlockSpec((None,) + blk_pool[1:], lambda i, pr: (0, 0, 0, 0)))

    _, r_in = _split_wait("scatter_win_wait", win, win_send, win_recv, _plan_in, d_pool)
    g_win, d_win, m_win, v_win = _reduce_adam(
        "adam_w_in", place,
        [(own_sum, pl.BlockSpec((3, D // 2, TILE), lambda i, pr: (0, i, 0)), False),
         (r_in, pl.BlockSpec((3, 3, D // 2, TILE), lambda i, pr: (0, 0, i, 0)), True)],
        w_in, m_w_in, v_w_in, (2,), pl.BlockSpec((None, D // 2, 3 * TILE), lambda i, pr: (0, i, 0)))

    own_small, r_small = _split_wait("gather_small_wait", small, small_send, small_recv, _plan_small, d_win)
    loss, (g_s, d_s, m_s, v_s) = _small_adam(
        place, own_small, r_small,
        (norm1_g, pool_scale, lb_logits, rec_norm_g, gf),
        (m_norm1_g, m_pool_scale, m_lb_logits, m_rec_norm_g, m_final_norm_g.reshape(1, D)),
        (v_norm1_g, v_pool_scale, v_lb_logits, v_rec_norm_g, v_final_norm_g.reshape(1, D)))

    def outs(small, win, pool, wout):
        n1, ps, lbl, rg, fg = small
        return n1, win, pool, ps, lbl, rg, wout, fg.reshape(D)

    return (loss.reshape(()), grad_x[None],
            *outs(g_s, g_win, g_pool, g_wout), *outs(d_s, d_win, d_pool, d_wout),
            *outs(m_s, m_win, m_pool, m_wout), *outs(v_s, v_win, v_pool, v_wout))
```

```python
import jax
import jax.numpy as jnp
from jax import lax
from jax.experimental import pallas as pl
from jax.experimental.pallas import tpu as pltpu

T = 2048
D = 1024
NSEG = 6
NTILE = 24
TILE = 256
DMIX = 2048
NDEV = 8
HEAD = 128
NHEAD = 8
CHUNK = 64
NCHUNK = T // CHUNK
NB = 32
NGRP = NCHUNK // NB
NGROUP = 4
GROUP = 256
EPS = 1e-6
EXP_CAP = 115.0
MESH = pl.DeviceIdType.MESH
ANY = pl.BlockSpec(memory_space=pl.ANY)
HBM = pl.BlockSpec(memory_space=pltpu.HBM)
SEM = pl.BlockSpec(memory_space=pltpu.SEMAPHORE)
EFFECT = pltpu.SideEffectType.DATAFLOW_SIDE_EFFECTING

ADAM_LR = 0.001
ADAM_B1 = 0.9
ADAM_B2 = 0.999
ADAM_EPS = 1e-08
ADAM_WD = 0.01
ADAM_STEP = 10
BC1 = 1.0 - ADAM_B1 ** ADAM_STEP
BC2 = 1.0 - ADAM_B2 ** ADAM_STEP

MIB = 1 << 20


def _params(sem=None, vmem_mib=48):
    return pltpu.CompilerParams(dimension_semantics=sem, vmem_limit_bytes=vmem_mib * MIB)


def _sigmoid(v):
    return 1.0 / (1.0 + jnp.exp(-v))


def _dot(a, b, ca, cb, precision=None):
    return lax.dot_general(a, b, (((ca,), (cb,)), ((), ())), precision=precision,
                           preferred_element_type=jnp.float32)


def _bf(v):
    return v.astype(jnp.bfloat16)


def _in_hbm(a):
    return pltpu.with_memory_space_constraint(a, pltpu.HBM)


def _place():
    x, y, c = lax.axis_index("x"), lax.axis_index("y"), lax.axis_index("c")
    return x, y, c, 4 * x + 2 * y + c


def _peer(x, y, c, r):
    return (x ^ ((r >> 2) & 1), y ^ ((r >> 1) & 1), c ^ (r & 1))


def _gather_proj(x, g1, w_in, w_out, pool_w):
    def body(x_ref, g_ref, win_ref, wout_ref, pool_ref, ht_o, wt_o, woutb_o, wout_o, pool_o, proj_o,
             xbuf, hv, htv, wv, wob, pb, stage, send_sems, recv_sems, loc_sems, out_sems):
        px, py, c, my_idx = _place()
        fetch_x = pltpu.make_async_copy(x_ref, xbuf, loc_sems.at[5])
        fetch_x.start()
        me, sibling = (px, py, c), (px, py, 1 - c)
        chips = [(1 - px, py), (px, 1 - py), (1 - px, 1 - py)]
        for p in range(3):
            wv[3 * my_idx + p] = _bf(win_ref[0, :, p * TILE:(p + 1) * TILE])

        def index(bx, by, bc):
            return 4 * bx + 2 * by + bc

        def slot(w, block):
            return wv.at[pl.ds(3 * index(*block), 3)] if w == 0 else pool_o.at[index(*block)]

        def copy(k, w, block, to, src=None):
            return pltpu.make_async_remote_copy(
                src_ref=slot(w, block) if src is None else src, dst_ref=slot(w, block),
                send_sem=send_sems.at[2 * k + w], recv_sem=recv_sems.at[2 * k + w],
                device_id=to, device_id_type=MESH)

        def save(block):
            at = pl.ds(3 * index(*block), 3)
            pltpu.make_async_copy(wv.at[at], wt_o.at[at], loc_sems.at[4]).start()

        srcs = (slot(0, me), pb)
        first = []
        for w in (0, 1):
            if w == 1:
                pb[...] = _bf(pool_ref[0])
                wob[...] = _bf(wout_ref[0])
            group = [copy(1 + j, w, me, (*chip, c), src=srcs[w]) for j, chip in enumerate(chips[:2])]
            group.append(copy(0, w, me, sibling, src=srcs[w]))
            for cp in group:
                cp.start()
            first += group
        save(me)
        locs = [pltpu.make_async_copy(pb, slot(1, me), loc_sems.at[0]),
                pltpu.make_async_copy(wob, wout_o.at[my_idx], loc_sems.at[1]),
                pltpu.make_async_copy(wob, woutb_o, loc_sems.at[2])]
        for cp in locs:
            cp.start()

        fetch_x.wait()
        xv = xbuf[...]
        hv[...] = _bf(xv * lax.rsqrt(jnp.mean(xv * xv, axis=-1, keepdims=True) + EPS) * g_ref[...])
        rows = 256
        for r0 in range(0, T, rows):
            htv[:, r0:r0 + rows] = hv[r0:r0 + rows, :].T
        locs.append(pltpu.make_async_copy(htv, ht_o, loc_sems.at[3]))
        locs[-1].start()

        def out_copy(p, j):
            return pltpu.make_async_copy(stage.at[p], proj_o.at[j], out_sems.at[p])

        def project(nth, block):
            base = 3 * index(*block)

            def tile(p, carry):
                if nth > 0:
                    out_copy(p, base + p).wait()
                stage[p] = _dot(hv[...], wv[base + p], 1, 0)
                out_copy(p, base + p).start()
                return carry

            lax.fori_loop(0, 3, tile, 0)

        project(0, me)
        copy(0, 0, sibling, me).wait_recv()
        save(sibling)
        project(1, sibling)
        passed = []
        relay_from = (px ^ (1 - c), py ^ c, c)
        relay_to = (px ^ c, py ^ (1 - c), c)

        def arrived(w, j):
            copy(1 + j, w, (*chips[j], c), me).wait_recv()
            passed.append(copy(4 + j, w, (*chips[j], c), sibling))
            passed[-1].start()

        def relay(w):
            passed.append(copy(3, w, relay_from, relay_to))
            passed[-1].start()

        def handed(nth, j):
            copy(4 + j, 0, (*chips[j], 1 - c), me).wait_recv()
            save((*chips[j], 1 - c))
            project(nth, (*chips[j], 1 - c))

        arrived(0, 0)
        arrived(0, 1)
        relay(0)
        for j in range(2):
            save((*chips[j], c))
            project(2 + j, (*chips[j], c))
        handed(4, 0)
        handed(5, 1)
        arrived(1, 0)
        arrived(1, 1)
        relay(1)
        arrived(0, 2)
        save((*chips[2], c))
        project(6, (*chips[2], c))
        handed(7, 2)
        arrived(1, 2)
        copy(0, 1, sibling, me).wait_recv()
        for j, chip in enumerate(chips):
            copy(4 + j, 1, (*chip, 1 - c), me).wait_recv()
        keep = pltpu.make_async_copy(wv, wt_o, loc_sems.at[4])
        for p in range(3):
            out_copy(p, p).wait()
        for cp in first + passed:
            cp.wait_send()
        keep.wait()
        for cp in locs:
            cp.wait()

    vmem = pl.BlockSpec(memory_space=pltpu.VMEM)
    bf16 = jnp.bfloat16
    return pl.pallas_call(
        body, name="gather_proj",
        out_shape=(pltpu.HBM((D, T), bf16), pltpu.HBM((NTILE, D, TILE), bf16),
                   pltpu.HBM((DMIX // NDEV, D), bf16), pltpu.HBM((NDEV, DMIX // NDEV, D), bf16),
                   pltpu.HBM((NDEV, NGROUP, GROUP // NDEV, GROUP), bf16), pltpu.HBM((NTILE, T, TILE), jnp.float32)),
        in_specs=[ANY] + [vmem] * 4, out_specs=(ANY,) * 6,
        scratch_shapes=[pltpu.VMEM((T, D), jnp.float32),
                        pltpu.VMEM((T, D), bf16), pltpu.VMEM((D, T), bf16), pltpu.VMEM((NTILE, D, TILE), bf16),
                        pltpu.VMEM((DMIX // NDEV, D), bf16), pltpu.VMEM((NGROUP, GROUP // NDEV, GROUP), bf16),
                        pltpu.VMEM((3, T, TILE), jnp.float32),
                        pltpu.SemaphoreType.DMA((14,)), pltpu.SemaphoreType.DMA((14,)),
                        pltpu.SemaphoreType.DMA((6,)), pltpu.SemaphoreType.DMA((3,))],
        compiler_params=_params(vmem_mib=56),
    )(x, g1, w_in, w_out, pool_w)


def _split_start(name, arrays, n_copies, plan):
    k = len(arrays)

    def body(*refs):
        send_sems, recv_sems, token = refs[k], refs[k + 1], refs[-1]
        for i, (src, dst, to) in enumerate(plan(refs[:k])):
            pltpu.make_async_remote_copy(src_ref=src, dst_ref=dst, send_sem=send_sems.at[i],
                                         recv_sem=recv_sems.at[i], device_id=to, device_id_type=MESH).start()
        token[...] = jnp.zeros_like(token)

    out = pl.pallas_call(
        body, name=name,
        out_shape=(pltpu.SemaphoreType.DMA((n_copies,)), pltpu.SemaphoreType.DMA((n_copies,)),
                   *[pltpu.HBM(a.shape, a.dtype) for a in arrays], jax.ShapeDtypeStruct((8, 128), jnp.float32)),
        in_specs=[HBM] * k, out_specs=(SEM, SEM, *[HBM] * k, pl.BlockSpec(memory_space=pltpu.VMEM)),
        input_output_aliases={i: 2 + i for i in range(k)},
        compiler_params=pltpu.CompilerParams(has_side_effects=EFFECT),
    )(*[pltpu.with_memory_space_constraint(a, pltpu.HBM) for a in arrays])
    return out[0], out[1], out[2:2 + k], out[-1]


def _split_wait(name, arrays, send_sems, recv_sems, plan, after):
    k = len(arrays)

    def body(*refs):
        sends, recvs = refs[k], refs[k + 1]
        for i, (src, dst, to) in enumerate(plan(refs[:k])):
            cp = pltpu.make_async_remote_copy(src_ref=src, dst_ref=dst, send_sem=sends.at[i], recv_sem=recvs.at[i],
                                              device_id=to, device_id_type=MESH)
            cp.wait_send()
            cp.wait_recv()

    return pl.pallas_call(
        body, name=name,
        out_shape=tuple(pltpu.HBM(a.shape, a.dtype) for a in arrays),
        in_specs=[HBM] * k + [SEM, SEM, ANY], out_specs=(HBM,) * k,
        input_output_aliases={i: i for i in range(k)},
        compiler_params=pltpu.CompilerParams(has_side_effects=EFFECT),
    )(*arrays, send_sems, recv_sems, after)


def _plan_wout(refs):
    src, land = refs
    x, y, c, me = _place()
    return [(src, land.at[me], _peer(x, y, c, r)) for r in range(1, NDEV)]


def _plan_rest(refs):
    gob, gpf, r_out, r_pool = refs
    x, y, c, me = _place()
    plan = []
    for r in range(1, NDEV):
        plan.append((gob.at[me ^ r], r_out.at[r - 1], _peer(x, y, c, r)))
        plan.append((gpf.at[me ^ r], r_pool.at[r - 1], _peer(x, y, c, r)))
    return plan


def _plan_in(refs):
    sums, landing = refs
    x, y, c, _ = _place()
    plan = []
    for rel, (dx, dy) in enumerate(((1, 0), (0, 1), (1, 1))):
        for p in range(3):
            plan.append((sums.at[rel, p], landing.at[rel, p], (x ^ dx, y ^ dy, c)))
    return plan


def _plan_small(refs):
    small, land = refs
    x, y, c, me = _place()
    return [(small, land.at[me], _peer(x, y, c, r)) for r in range(1, NDEV)]


_POOL_SPECS = [pl.BlockSpec((None, T, GROUP), lambda g, base=base: (base + g, 0, 0)) for base in (0, 4)]
_HEAD_SPECS = [pl.BlockSpec((None, T, HEAD), lambda h, base=base: (base + h // 2, 0, h % 2))
               for base in (8, 12, 16, 20)]


_POOL_W_SPEC = pl.BlockSpec((NDEV, None, GROUP // NDEV, GROUP), lambda g: (0, g, 0, 0))


def _row_ids(shape):
    return lax.broadcasted_iota(jnp.int32, shape, 0)


BAND_ROWS = 128
HALO = 16


def _window_sum(a, gidx, lead):
    width = lax.shift_left(jnp.int32(2), gidx)
    shape = (BAND_ROWS, BAND_ROWS + HALO)
    t, j = lax.broadcasted_iota(jnp.int32, shape, 0), lax.broadcasted_iota(jnp.int32, shape, 1)
    first = t if lead else t + HALO - width + 1
    band = _bf(jnp.where(j >= first, jnp.where(j < first + width, 1.0, 0.0), 0.0))
    zeros = jnp.zeros((HALO, a.shape[1]), jnp.bfloat16)
    padded = [jnp.concatenate([p, zeros] if lead else [zeros, p], axis=0) for p in _split2(a)]
    out = []
    for r0 in range(0, T, BAND_ROWS):
        slab = jnp.concatenate([p[r0:r0 + BAND_ROWS + HALO] for p in padded], axis=1)
        r = _dot(band, slab, 1, 0)
        out.append(r[:, :a.shape[1]] + r[:, a.shape[1]:])
    return jnp.concatenate(out, axis=0)


def _window_mean(s, gidx):
    inv = jnp.where(gidx == 0, 0.5, jnp.where(gidx == 1, 0.25, jnp.where(gidx == 2, 0.125, 0.0625)))
    width = lax.shift_left(jnp.int32(2), gidx)
    head = s[:16] / jnp.minimum(_row_ids((16, s.shape[1])) + 1, width).astype(jnp.float32)
    return jnp.concatenate([head, s[16:] * inv], axis=0)


def _pool_fwd(proj, pool_w, pool_scale, token):
    def body(u_ref, pg_ref, w_ref, sc_ref, token_any, y_ref):
        del token_any
        gidx = pl.program_id(0)
        u, pg = u_ref[...], pg_ref[...]
        d = _window_mean(_window_sum(u, gidx, False), gidx) - u
        mixed = _dot(_bf(d), w_ref[...].reshape(GROUP, GROUP), 1, 0)
        y_ref[...] = _bf(mixed * sc_ref[...] * (pg * _sigmoid(pg)))

    return pl.pallas_call(
        body, name="pool_fwd", grid=(NGROUP,),
        in_specs=[*_POOL_SPECS, _POOL_W_SPEC, pl.BlockSpec((1, GROUP), lambda g: (0, g)), ANY],
        out_specs=pl.BlockSpec((T, GROUP), lambda g: (0, g)),
        out_shape=pltpu.HBM((T, DMIX), jnp.bfloat16),
        compiler_params=_params(("parallel",)),
    )(proj, proj, pool_w, pool_scale, token)


def _tri(lower):
    r = lax.broadcasted_iota(jnp.int32, (CHUNK, CHUNK), 0)
    c = lax.broadcasted_iota(jnp.int32, (CHUNK, CHUNK), 1)
    return (r >= c) if lower else (r <= c)


def _sum_rows_matrix():
    shape = (CHUNK + 16, CHUNK)
    r, c = lax.broadcasted_iota(jnp.int32, shape, 0), lax.broadcasted_iota(jnp.int32, shape, 1)
    run = jnp.where(c <= r, 1.0, 0.0)
    half = jnp.where(c < CHUNK // 2, 1.0, 0.0)
    return _bf(jnp.where(r < CHUNK, run, jnp.where(r < CHUNK + 8, 1.0, half)))


def _rev_sum_matrix():
    shape = (CHUNK, 2 * CHUNK)
    r, c = lax.broadcasted_iota(jnp.int32, shape, 0), lax.broadcasted_iota(jnp.int32, shape, 1)
    return _bf(jnp.where(c < CHUNK, jnp.where(c >= r, 1.0, 0.0), jnp.where(c - CHUNK < r, 1.0, 0.0)))


def _split2(a):
    hi = _bf(a)
    return [hi, _bf(a - hi.astype(jnp.float32))]


def _exact_sums(mat, pieces):
    x = jnp.concatenate([s for p in pieces for s in _split2(p)], axis=1)
    r = _dot(mat, x, 1, 0)
    return [r[:, 2 * j * HEAD:(2 * j + 1) * HEAD] + r[:, (2 * j + 1) * HEAD:(2 * j + 2) * HEAD]
            for j in range(len(pieces))]


def _gates(qv, fl, lb):
    sq = _sigmoid(qv)
    sg = _sigmoid(fl)
    f = lb + (1.0 - lb) * sg
    return dict(sq=sq, qs=qv * sq, sg=sg, f=f, kk=1.0 - f, g=jnp.log2(f))


def _decays(sums):
    big_g = sums[:CHUNK]
    total = sums[CHUNK:CHUNK + 8]
    g_last = jnp.tile(total, (CHUNK // 8, 1))
    g_mid = jnp.tile(sums[CHUNK + 8:], (CHUNK // 8, 1))
    return dict(
        e_q=jnp.exp2(big_g),
        e_k=jnp.exp2(g_last - big_g),
        e_qm=jnp.exp2(jnp.minimum(big_g - g_mid, EXP_CAP)),
        e_km=jnp.exp2(jnp.minimum(g_mid - big_g, EXP_CAP)),
        total8=jnp.exp2(total))


def _group_rows(gi):
    return [pl.ds(pl.multiple_of((gi * NB + j) * CHUNK, CHUNK), CHUNK) for j in range(NB)]


def _lower_bound(lb_ref):
    return _sigmoid(lb_ref[0:1, :] - lb_ref[1:2, :])


def _hgrn_fwd(proj, lb_logits, rec_g, y_in):
    def body(q_ref, f_ref, i_ref, gate_ref, lb_ref, rg_ref, y_any, y_ref, o_ref, st_ref):
        del y_any
        lb = _lower_bound(lb_ref)
        causal = _tri(True)
        smat = _sum_rows_matrix()

        def group(gi, st):
            rows = _group_rows(gi)
            ts = [_gates(q_ref[r, :], f_ref[r, :], lb) for r in rows]
            ds = [_decays(s) for s in _exact_sums(smat, [t["g"] for t in ts])]
            vs = [_bf(i_ref[r, :]) for r in rows]
            q_m = [_bf(t["qs"] * d["e_qm"]) for t, d in zip(ts, ds)]
            k_m = [_bf(t["kk"] * d["e_km"]) for t, d in zip(ts, ds)]
            q_e = [_bf(t["qs"] * d["e_q"]) for t, d in zip(ts, ds)]
            k_e = [_bf(t["kk"] * d["e_k"]) for t, d in zip(ts, ds)]
            a = [_bf(jnp.where(causal, _dot(q_m[j], k_m[j], 1, 1), 0.0)) for j in range(NB)]
            intra = [_dot(a[j], vs[j], 1, 0) for j in range(NB)]
            upd = [_dot(vs[j], k_e[j], 0, 0) for j in range(NB)]
            for j in range(NB):
                st_ref[gi * NB + j] = st
                o_ref[rows[j], :] = intra[j] + _dot(q_e[j], _bf(st), 1, 1)
                st = st * jnp.tile(ds[j]["total8"], (HEAD // 8, 1)) + upd[j]
            return st

        lax.fori_loop(0, NGRP, group, jnp.zeros((HEAD, HEAD), jnp.float32))
        o = o_ref[...]
        rn = o * lax.rsqrt(jnp.mean(o * o, axis=-1, keepdims=True) + EPS)
        gate = gate_ref[...]
        y_ref[...] = _bf(rn * rg_ref[...] * (gate * _sigmoid(gate)))

    return pl.pallas_call(
        body, name="hgrn_fwd", grid=(NHEAD,),
        in_specs=[*_HEAD_SPECS,
                  pl.BlockSpec((2, HEAD), lambda h: (0, h)),
                  pl.BlockSpec((1, HEAD), lambda h: (0, h)),
                  pl.BlockSpec(memory_space=pl.ANY)],
        out_specs=(pl.BlockSpec((T, HEAD), lambda h: (0, NHEAD + h)),
                   pl.BlockSpec((T, HEAD), lambda h: (0, h)),
                   pl.BlockSpec((None, NCHUNK, HEAD, HEAD), lambda h: (h, 0, 0, 0))),
        out_shape=(pltpu.HBM((T, DMIX), jnp.bfloat16), pltpu.HBM((T, D), jnp.float32),
                   pltpu.HBM((NHEAD, NCHUNK, HEAD, HEAD), jnp.float32)),
        input_output_aliases={6: 0},
        compiler_params=_params(("parallel",)),
    )(proj, proj, proj, proj, lb_logits, rec_g, y_in)


def _out_proj_loss(x, y, w_out, target, gf):
    rows = 512
    parts = [slice(k * rows // 2, (k + 1) * rows // 2) for k in range(2)]

    def body(x_ref, y_ref, w_ref, t_ref, g_ref, dz_ref, dzb_ref, sq_ref, dg_ref):
        zs = [x_ref[p, :] + _dot(y_ref[p, :], w_ref[...], 1, 0) for p in parts]
        sq = dg = 0.0
        for p, z in zip(parts, zs):
            r = lax.rsqrt(jnp.mean(z * z, axis=-1, keepdims=True) + EPS)
            zhat = z * r
            err = zhat * g_ref[...] - t_ref[p, :]
            dy = err * (1.0 / D)
            gdy = dy * g_ref[...]
            dz = r * (gdy - zhat * jnp.mean(zhat * gdy, axis=-1, keepdims=True))
            dz_ref[p, :] = dz
            dzb_ref[p, :] = _bf(dz)
            sq = sq + jnp.sum(err * err, axis=0, keepdims=True)
            dg = dg + jnp.sum(zhat * dy, axis=0, keepdims=True)

        @pl.when(pl.program_id(0) == 0)
        def _():
            sq_ref[...] = sq
            dg_ref[...] = dg

        @pl.when(pl.program_id(0) != 0)
        def _():
            sq_ref[...] += sq
            dg_ref[...] += dg

    tile = pl.BlockSpec((rows, D), lambda i: (i, 0))
    vec = pl.BlockSpec((1, D), lambda i: (0, 0))
    return pl.pallas_call(
        body, name="out_proj_loss", grid=(T // rows,),
        in_specs=[tile, pl.BlockSpec((rows, DMIX), lambda i: (i, 0)), pl.BlockSpec((DMIX, D), lambda i: (0, 0)),
                  tile, vec],
        out_specs=(tile, tile, vec, vec),
        out_shape=(pltpu.HBM((T, D), jnp.float32), pltpu.HBM((T, D), jnp.bfloat16),
                   pltpu.HBM((1, D), jnp.float32), pltpu.HBM((1, D), jnp.float32)),
        compiler_params=_params(("arbitrary",)),
    )(x, y, w_out, target, gf)


def _out_proj_bwd(dzb, w_out, y):
    tn = 512

    def body(dz_ref, w_ref, y_ref, dy_ref, gw_ref, gwb_ref):
        dz = dz_ref[...]
        dy_ref[...] = _dot(dz, w_ref[...], 1, 1)
        gw = _dot(y_ref[...], dz, 0, 0)
        gw_ref[...] = gw
        gwb_ref[...] = _bf(gw)

    return pl.pallas_call(
        body, name="out_proj_bwd", grid=(DMIX // tn,),
        in_specs=[pl.BlockSpec((T, D), lambda n: (0, 0)), pl.BlockSpec((tn, D), lambda n: (n, 0)),
                  pl.BlockSpec((T, tn), lambda n: (0, n))],
        out_specs=(pl.BlockSpec((T, tn), lambda n: (0, n)), pl.BlockSpec((tn, D), lambda n: (n, 0)),
                   pl.BlockSpec((tn, D), lambda n: (n, 0))),
        out_shape=(pltpu.HBM((T, DMIX), jnp.float32), pltpu.HBM((DMIX, D), jnp.float32),
                   pltpu.HBM((DMIX, D), jnp.bfloat16)),
        compiler_params=_params(("parallel",)),
    )(dzb, w_out, y)


def _hgrn_bwd(proj, lb_logits, rec_g, o, states, dymix, dproj_in, token):
    def body(q_ref, f_ref, i_ref, gate_ref, lb_ref, rg_ref, o_ref, st_ref, dy_ref, dp_any, token_any,
             dp_ref, drg_ref, dlb_ref, do_ref):
        del dp_any, token_any
        lb = _lower_bound(lb_ref)
        causal = _tri(True)
        smat, rmat = _sum_rows_matrix(), _rev_sum_matrix()

        o = o_ref[...]
        rs = lax.rsqrt(jnp.mean(o * o, axis=-1, keepdims=True) + EPS)
        rn = o * rs
        gate = gate_ref[...]
        sgate = _sigmoid(gate)
        dyv = dy_ref[...]
        d_r = dyv * (gate * sgate)
        dp_ref[3] = _bf(dyv * (rn * rg_ref[...]) * (sgate * (1.0 + gate * (1.0 - sgate))))
        drg_ref[...] = jnp.sum(d_r * rn, axis=0, keepdims=True)
        drn = d_r * rg_ref[...]
        do_ref[...] = rs * (drn - rn * jnp.mean(rn * drn, axis=-1, keepdims=True))

        def group(i, carry):
            dst, dlb = carry
            gi = NGRP - 1 - i
            rows = _group_rows(gi)
            span = range(NB)
            qvs = [q_ref[r, :] for r in rows]
            ts = [_gates(qv, f_ref[r, :], lb) for qv, r in zip(qvs, rows)]
            ds = [_decays(s) for s in _exact_sums(smat, [t["g"] for t in ts])]
            vs = [_bf(i_ref[r, :]) for r in rows]
            dos = [_bf(do_ref[r, :]) for r in rows]
            sts = [st_ref[gi * NB + j] for j in span]
            qe_f = [t["qs"] * d["e_q"] for t, d in zip(ts, ds)]
            ke_f = [t["kk"] * d["e_k"] for t, d in zip(ts, ds)]
            q_e, k_e = [_bf(a) for a in qe_f], [_bf(a) for a in ke_f]
            q_m = [_bf(t["qs"] * d["e_qm"]) for t, d in zip(ts, ds)]
            k_m = [_bf(t["kk"] * d["e_km"]) for t, d in zip(ts, ds)]
            a = [_bf(jnp.where(causal, _dot(q_m[j], k_m[j], 1, 1), 0.0)) for j in span]
            da = [_bf(jnp.where(causal, _dot(dos[j], vs[j], 1, 1), 0.0)) for j in span]
            dqm = [_dot(da[j], k_m[j], 1, 0) for j in span]
            dkm = [_dot(da[j], q_m[j], 0, 0) for j in span]
            dv_in = [_dot(a[j], dos[j], 0, 0) for j in span]
            dqe = [_dot(dos[j], _bf(sts[j]), 1, 0) for j in span]
            grow = [_dot(dos[j], q_e[j], 0, 0) for j in span]
            dke, carried = [None] * NB, [None] * NB
            for j in reversed(span):
                dst_b = _bf(dst)
                dke[j] = _dot(vs[j], dst_b, 1, 0)
                dp_ref[2, rows[j], :] = _bf(dv_in[j] + _dot(k_e[j], dst_b, 1, 1))
                carried[j] = ds[j]["total8"] * jnp.sum(dst * sts[j], axis=0, keepdims=True)
                dst = dst * jnp.tile(ds[j]["total8"], (HEAD // 8, 1)) + grow[j]
            kdk = [ke_f[j] * dke[j] for j in span]
            pos = [(q_m[j].astype(jnp.float32) * dqm[j] - k_m[j].astype(jnp.float32) * dkm[j]) + qe_f[j] * dqe[j]
                   for j in span]
            dgs = _exact_sums(rmat, [jnp.concatenate([pos[j], kdk[j]], axis=0) for j in span])
            for j in span:
                t, d = ts[j], ds[j]
                dg = dgs[j] + jnp.tile(carried[j], (CHUNK // 8, 1))
                dqs = dqm[j] * d["e_qm"] + dqe[j] * d["e_q"]
                dkk = dkm[j] * d["e_km"] + dke[j] * d["e_k"]
                df = dg / t["f"] - dkk
                dp_ref[1, rows[j], :] = _bf(df * (1.0 - lb) * (t["sg"] * (1.0 - t["sg"])))
                dp_ref[0, rows[j], :] = _bf(dqs * (t["sq"] * (1.0 + qvs[j] * (1.0 - t["sq"]))))
                dlb = dlb + df * (1.0 - t["sg"])
            return dst, dlb

        _, dlb = lax.fori_loop(0, NGRP, group, (jnp.zeros((HEAD, HEAD), jnp.float32),
                                                jnp.zeros((CHUNK, HEAD), jnp.float32)))
        dlb_ref[...] = jnp.sum(dlb, axis=0, keepdims=True)

    vec = pl.BlockSpec((1, HEAD), lambda h: (0, h))
    return pl.pallas_call(
        body, name="hgrn_bwd", grid=(NHEAD,),
        in_specs=[*_HEAD_SPECS,
                  pl.BlockSpec((2, HEAD), lambda h: (0, h)), vec,
                  pl.BlockSpec((T, HEAD), lambda h: (0, h)),
                  pl.BlockSpec((None, NCHUNK, HEAD, HEAD), lambda h: (h, 0, 0, 0)),
                  pl.BlockSpec((T, HEAD), lambda h: (0, NHEAD + h)), ANY, ANY],
        out_specs=(pl.BlockSpec((4, T, HEAD), lambda h: (0, 0, h)), vec, vec),
        out_shape=(pltpu.HBM((NSEG, T, D), jnp.bfloat16),
                   pltpu.HBM((1, D), jnp.float32), pltpu.HBM((1, D), jnp.float32)),
        scratch_shapes=[pltpu.VMEM((T, HEAD), jnp.float32)],
        input_output_aliases={9: 0},
        compiler_params=_params(("parallel",)),
    )(proj, proj, proj, proj, lb_logits, rec_g, o, states, dymix, dproj_in, token)


def _pool_bwd(proj, pool_w, pool_scale, dymix):
    def body(u_ref, pg_ref, w_ref, sc_ref, dy_ref, dp_ref, gw_ref, gs_ref):
        gidx = pl.program_id(0)
        u, pg = u_ref[...], pg_ref[...]
        w = w_ref[...].reshape(GROUP, GROUP)
        d = _bf(_window_mean(_window_sum(u, gidx, False), gidx) - u)
        mixed = _dot(d, w, 1, 0)
        spg = _sigmoid(pg)
        dyv = dy_ref[...]
        d_p = dyv * (pg * spg)
        dp_ref[1] = _bf(dyv * (mixed * sc_ref[...]) * (spg * (1.0 + pg * (1.0 - spg))))
        gs_ref[...] = jnp.sum(d_p * mixed, axis=0, keepdims=True)
        dmixed = _bf(d_p * sc_ref[...])
        gw_ref[...] = _dot(d, dmixed, 0, 0).reshape(gw_ref.shape)
        dd = _dot(dmixed, w, 1, 1)
        dp_ref[0] = _bf(_window_sum(_window_mean(dd, gidx), gidx, True) - dd)

    return pl.pallas_call(
        body, name="pool_bwd", grid=(NGROUP,),
        in_specs=[*_POOL_SPECS, _POOL_W_SPEC,
                  pl.BlockSpec((1, GROUP), lambda g: (0, g)),
                  pl.BlockSpec((T, GROUP), lambda g: (0, g))],
        out_specs=(pl.BlockSpec((2, T, GROUP), lambda g: (2, 0, g)), _POOL_W_SPEC,
                   pl.BlockSpec((1, GROUP), lambda g: (0, g))),
        out_shape=(pltpu.HBM((NSEG, T, D), jnp.bfloat16),
                   pltpu.HBM((NDEV, NGROUP, GROUP // NDEV, GROUP), jnp.float32),
                   pltpu.HBM((1, D), jnp.float32)),
        compiler_params=_params(("parallel",)),
    )(proj, proj, pool_w, pool_scale, dymix)


HALF = NTILE // 2
AWAY = HALF - 3


def _dproj_tile(chip, side, p):
    j = 6 * chip + 3 * side + p
    return ((j // 4 + 4) % NSEG, 0, j % 4)


def _sibling_copy(sib_out, sib_in, send_sems, recv_sems, slot):
    x, y, c, _ = _place()
    return pltpu.make_async_remote_copy(
        src_ref=sib_out.at[slot], dst_ref=sib_in.at[slot], send_sem=send_sems.at[slot],
        recv_sem=recv_sems.at[slot], device_id=(x, y, 1 - c), device_id_type=MESH)


def _proj_bwd_w_far(place, ht, dproj):
    def body(place_ref, h_ref, dpa_ref, dpb_ref, sib_out, sib_in, send_sems, recv_sems, stage, loc_sems):
        del place_ref
        i = pl.program_id(0)

        def to_hbm(k):
            return pltpu.make_async_copy(stage.at[k], sib_out.at[k], loc_sems.at[k])

        def send(k):
            to_hbm(k).wait()
            _sibling_copy(sib_out, sib_in, send_sems, recv_sems, k).start()

        stage[2 * i] = _bf(_dot(h_ref[...], dpa_ref[...], 1, 0))
        stage[2 * i + 1] = _bf(_dot(h_ref[...], dpb_ref[...], 1, 0))

        @pl.when(i > 0)
        def _():
            send(2 * i - 2)
            send(2 * i - 1)

        to_hbm(2 * i).start()
        to_hbm(2 * i + 1).start()

        @pl.when(i == HALF // 2 - 1)
        def _():
            send(2 * i)
            send(2 * i + 1)

    def tile(k, pr):
        return _dproj_tile(k // 3, 1 - pr[2], k % 3)

    buf = pltpu.HBM((HALF, D, TILE), jnp.bfloat16)
    sems = pltpu.SemaphoreType.DMA((HALF,))
    return pl.pallas_call(
        body, name="proj_bwd_w_far",
        grid_spec=pltpu.PrefetchScalarGridSpec(
            num_scalar_prefetch=1, grid=(HALF // 2,),
            in_specs=[pl.BlockSpec((D, T), lambda i, pr: (0, 0)),
                      pl.BlockSpec((None, T, TILE), lambda i, pr: tile(2 * i, pr)),
                      pl.BlockSpec((None, T, TILE), lambda i, pr: tile(2 * i + 1, pr))],
            out_specs=(HBM, HBM, SEM, SEM),
            scratch_shapes=[pltpu.VMEM((HALF, D, TILE), jnp.bfloat16), pltpu.SemaphoreType.DMA((HALF,))]),
        out_shape=(buf, buf, sems, sems),
        compiler_params=pltpu.CompilerParams(dimension_semantics=("arbitrary",), vmem_limit_bytes=48 * MIB,
                                             has_side_effects=EFFECT),
    )(place, ht, dproj, dproj)


def _proj_bwd_w_near(place, ht, dproj, sib_out, sib_in, sib_send, sib_recv):
    def owner_chip(k, pr):
        return jnp.where(k < AWAY, (pr[1] + 1 + k % 3) % 4, pr[1])

    def tile_p(k):
        return jnp.where(k < AWAY, k // 3, k - AWAY)

    def body(place_ref, h_ref, dp_ref, sib_out, sib_in, sib_send, sib_recv, sums, own_ref, landing, out_send,
             out_recv, recvbuf, outbuf, in_sems, loc_sems):
        i = pl.program_id(0)
        px, py, c, _ = _place()

        def slot_of(k):
            return 3 * owner_chip(k, place_ref) + tile_p(k)

        def load(k):
            return pltpu.make_async_copy(sib_in.at[slot_of(k)], recvbuf.at[k % 2], in_sems.at[k % 2])

        def fetch(k):
            _sibling_copy(sib_out, sib_in, sib_send, sib_recv, slot_of(k)).wait_recv()
            load(k).start()

        def route(k):
            chip = owner_chip(k, place_ref)
            cx, cy = chip // 2, chip % 2
            return cx, cy, (cx ^ px) + 2 * (cy ^ py) - 1, tile_p(k)

        def to_hbm(k):
            _, _, rel, p = route(k)
            return pltpu.make_async_copy(outbuf.at[k], sums.at[rel, p], loc_sems.at[k])

        def to_owner(k):
            cx, cy, rel, p = route(k)
            return pltpu.make_async_remote_copy(
                src_ref=sums.at[rel, p], dst_ref=landing.at[rel, p], send_sem=out_send.at[3 * rel + p],
                recv_sem=out_recv.at[3 * rel + p], device_id=(cx, cy, c), device_id_type=MESH)

        @pl.when(i == 0)
        def _():
            fetch(i)

        @pl.when(i < HALF - 1)
        def _():
            fetch(i + 1)

        load(i).wait()
        total = _dot(h_ref[...], dp_ref[...], 1, 0) + recvbuf[i % 2].astype(jnp.float32)
        own_ref[...] = total
        outbuf[jnp.minimum(i, AWAY)] = _bf(total)

        @pl.when(i < AWAY)
        def _():
            to_hbm(i).start()

        @pl.when(jnp.logical_and(i > 0, i <= AWAY))
        def _():
            to_hbm(i - 1).wait()
            to_owner(i - 1).start()

        @pl.when(i == HALF - 1)
        def _():
            for slot in range(HALF):
                _sibling_copy(sib_out, sib_in, sib_send, sib_recv, slot).wait_send()

    travelling = pltpu.HBM((3, 3, D, TILE), jnp.bfloat16)
    sems = pltpu.SemaphoreType.DMA((AWAY,))
    return pl.pallas_call(
        body, name="proj_bwd_w_near",
        grid_spec=pltpu.PrefetchScalarGridSpec(
            num_scalar_prefetch=1, grid=(HALF,),
            in_specs=[pl.BlockSpec((D, T), lambda i, pr: (0, 0)),
                      pl.BlockSpec((None, T, TILE), lambda i, pr: _dproj_tile(owner_chip(i, pr), pr[2], tile_p(i))),
                      HBM, HBM, SEM, SEM],
            out_specs=(HBM, pl.BlockSpec((None, D, TILE), lambda i, pr: (jnp.where(i < AWAY, 0, i % 3), 0, 0)),
                       HBM, SEM, SEM),
            scratch_shapes=[pltpu.VMEM((2, D, TILE), jnp.bfloat16), pltpu.VMEM((AWAY + 1, D, TILE), jnp.bfloat16),
                            pltpu.SemaphoreType.DMA((2,)), pltpu.SemaphoreType.DMA((AWAY,))]),
        out_shape=(travelling, pltpu.HBM((3, D, TILE), jnp.float32), travelling, sems, sems),
        compiler_params=pltpu.CompilerParams(dimension_semantics=("arbitrary",), vmem_limit_bytes=48 * MIB,
                                             has_side_effects=EFFECT),
    )(place, ht, dproj, sib_out, sib_in, sib_send, sib_recv)


def _proj_bwd_x(dproj, w_t, x, g1, dz, others, token):
    tm = 1024
    pairs = NSEG // 2
    k = len(others)

    def body(dp_ref, w_ref, x_any, g_ref, dz_any, *refs):
        dx_ref, dg_ref, wcat, acc, xbuf, dzbuf, row_sems = refs[k + 1:]
        m, s = pl.program_id(0), pl.program_id(1)
        mine = pl.ds(pl.multiple_of(m * tm, tm), tm)
        fetches = [pltpu.make_async_copy(x_any.at[mine], xbuf, row_sems.at[0]),
                   pltpu.make_async_copy(dz_any.at[mine], dzbuf, row_sems.at[1])]

        @pl.when(m == 0)
        def _():
            for i in range(8):
                wcat[s, :, i * TILE:(i + 1) * TILE] = w_ref[i]

        @pl.when(s == 0)
        def _():
            for cp in fetches:
                cp.start(priority=1)
            acc[...] = jnp.zeros((tm, D), jnp.float32)

        acc[...] += _dot(jnp.concatenate([dp_ref[0], dp_ref[1]], axis=1), wcat[s], 1, 1)

        @pl.when(s == pairs - 1)
        def _():
            for cp in fetches:
                cp.wait()
            xv = xbuf[...]
            rs = lax.rsqrt(jnp.mean(xv * xv, axis=-1, keepdims=True) + EPS)
            xhat = xv * rs
            dhv = acc[...]
            gdh = dhv * g_ref[...]
            dx_ref[...] = dzbuf[...] + rs * (gdh - xhat * jnp.mean(xhat * gdh, axis=-1, keepdims=True))
            dg = jnp.sum(xhat * dhv, axis=0, keepdims=True)

            @pl.when(m == 0)
            def _():
                dg_ref[0:1, :] = dg
                for r, ref in enumerate(refs[:k]):
                    dg_ref[1 + r:2 + r, :] = ref[...]
                dg_ref[1 + k:, :] = jnp.zeros((7 - k, D), jnp.float32)

            @pl.when(m != 0)
            def _():
                dg_ref[0:1, :] += dg

    rows = pl.BlockSpec((tm, D), lambda m, s: (m, 0))
    vec = pl.BlockSpec((1, D), lambda m, s: (0, 0))
    return pl.pallas_call(
        body, name="proj_bwd_x", grid=(T // tm, pairs),
        in_specs=[pl.BlockSpec((2, tm, D), lambda m, s: (s, m, 0)),
                  pl.BlockSpec((8, D, TILE), lambda m, s: ((jnp.where(m == 0, s, pairs - 1) + 1) % pairs, 0, 0)),
                  HBM, vec, HBM, *[vec] * k, ANY],
        out_specs=(rows, pl.BlockSpec((8, D), lambda m, s: (0, 0))),
        out_shape=(jax.ShapeDtypeStruct((T, D), jnp.float32), pltpu.HBM((8, D), jnp.float32)),
        scratch_shapes=[pltpu.VMEM((pairs, D, 2 * D), jnp.bfloat16), pltpu.VMEM((tm, D), jnp.float32),
                        pltpu.VMEM((tm, D), jnp.float32), pltpu.VMEM((tm, D), jnp.float32),
                        pltpu.SemaphoreType.DMA((2,))],
        compiler_params=_params(("arbitrary", "arbitrary"), vmem_mib=58),
    )(dproj, w_t, _in_hbm(x), g1, _in_hbm(dz), *[_in_hbm(a) for a in others], token)


def _adamw(w, g, m, v):
    m_new = ADAM_B1 * m + (1.0 - ADAM_B1) * g
    v_new = ADAM_B2 * v + (1.0 - ADAM_B2) * (g * g)
    delta = -ADAM_LR * ((m_new / BC1) / (jnp.sqrt(v_new / BC2) + ADAM_EPS) + ADAM_WD * w)
    return delta, m_new, v_new


def _reduce_adam(name, place, parts, w, m, v, grid, w_spec):
    n = len(parts)

    def body(place_ref, *refs):
        del place_ref
        w_ref, m_ref, v_ref, g_ref, d_ref, mo_ref, vo_ref = refs[n:]
        g = None
        for ref, (_, _, stacked) in zip(refs[:n], parts):
            terms = [ref[r] for r in range(ref.shape[0])] if stacked else [ref[...]]
            for t in terms:
                if t.shape[-1] != w_ref.shape[-1]:
                    t = jnp.concatenate([t[p] for p in range(t.shape[0])], axis=1)
                g = t.astype(jnp.float32) if g is None else g + t.astype(jnp.float32)
        delta, m_new, v_new = _adamw(w_ref[...], g, m_ref[...], v_ref[...])
        g_ref[...] = g
        d_ref[...] = delta
        mo_ref[...] = m_new
        vo_ref[...] = v_new

    shape = jax.ShapeDtypeStruct(w.shape, jnp.float32)
    return pl.pallas_call(
        body, name=name,
        grid_spec=pltpu.PrefetchScalarGridSpec(
            num_scalar_prefetch=1, grid=grid,
            in_specs=[spec for _, spec, _ in parts] + [w_spec] * 3, out_specs=(w_spec,) * 4),
        out_shape=(shape,) * 4,
        compiler_params=_params(("parallel",)),
    )(place, *[_in_hbm(a) for a in [a for a, _, _ in parts] + [w, m, v]])


SMALL_ROWS = (1, 1, 2, 1, 1)


def _small_adam(place, own, parts, ws, ms, vs):
    n = len(SMALL_ROWS)

    def body(place_ref, own_ref, p_ref, *refs):
        ins, outs, bufs = refs[:3 * n], refs[3 * n:3 * n + 1 + 4 * n], refs[3 * n + 1 + 4 * n:]

        def stacked(group, buf):
            r0 = 0
            for ref, k in zip(group, SMALL_ROWS):
                buf[r0:r0 + k, :] = ref[...]
                r0 += k
            buf[r0:, :] = jnp.zeros((8 - r0, D), jnp.float32)
            return buf[...]

        wv, mv, vv = (stacked(ins[n * j:n * j + n], bufs[j]) for j in range(3))
        me = place_ref[0]
        g = None
        for s in range(NDEV):
            term = jnp.where(me == s, own_ref[...], p_ref[s])
            g = term if g is None else g + term
        rows = _row_ids(wv.shape)
        other = jnp.where(rows == 2, pltpu.roll(wv, 7, 0), jnp.where(rows == 3, pltpu.roll(wv, 1, 0), 0.0))
        lbv = _sigmoid(wv - other)
        sign = jnp.where(rows == 2, 1.0, -1.0)
        g = jnp.where((rows == 2) | (rows == 3), sign * g * lbv * (1.0 - lbv), g)
        delta, m_new, v_new = _adamw(wv, g, mv, vv)
        outs[0][...] = jnp.sum(g[6:7], axis=1, keepdims=True) * (0.5 / D)
        for j, val in enumerate((g, delta, m_new, v_new)):
            r0 = 0
            for ref, k in zip(outs[1 + n * j:1 + n * j + n], SMALL_ROWS):
                ref[...] = val[r0:r0 + k]
                r0 += k

    vmem = pl.BlockSpec(memory_space=pltpu.VMEM)
    shapes = [jax.ShapeDtypeStruct((k, D), jnp.float32) for k in SMALL_ROWS]
    out = pl.pallas_call(
        body, name="small_adam", out_shape=(jax.ShapeDtypeStruct((1, 1), jnp.float32), *shapes * 4),
        in_specs=[pl.BlockSpec(memory_space=pltpu.SMEM)] + [vmem] * (2 + 3 * n), out_specs=(vmem,) * (1 + 4 * n),
        scratch_shapes=[pltpu.VMEM((8, D), jnp.float32)] * 3,
    )(place, own, parts, *ws, *ms, *vs)
    return out[0], [out[1 + n * j:1 + n * j + n] for j in range(4)]


def kernel(x, norm1_g, w_in, pool_w, pool_scale, lb_logits, rec_norm_g, w_out, final_norm_g, loss_target, m_norm1_g, m_w_in, m_pool_w, m_pool_scale, m_lb_logits, m_rec_norm_g, m_w_out, m_final_norm_g, v_norm1_g, v_w_in, v_pool_w, v_pool_scale, v_lb_logits, v_rec_norm_g, v_w_out, v_final_norm_g):
    xs = x[0]
    target = loss_target[0]
    ix, iy, ic = lax.axis_index("x"), lax.axis_index("y"), lax.axis_index("c")
    place = jnp.stack([4 * ix + 2 * iy + ic, 2 * ix + iy, ic]).astype(jnp.int32)
    gf = final_norm_g.reshape(1, D)

    ht, w_t, w_out_b, w_out_g, pool_g, proj = _gather_proj(xs, norm1_g, w_in, w_out, pool_w)
    wout = [w_out_b, w_out_g]
    wout_send, wout_recv, wout, wout_token = _split_start("gather_wout_start", wout, NDEV - 1, _plan_wout)

    y = _pool_fwd(proj, pool_g, pool_scale, wout_token)
    y, o, states = _hgrn_fwd(proj, lb_logits, rec_norm_g, y)
    _, w_out_g = _split_wait("gather_wout_wait", wout, wout_send, wout_recv, _plan_wout, o)
    w_out_full = _in_hbm(w_out_g.reshape(DMIX, D))
    dz, dzb, sq, dgf = _out_proj_loss(xs, y, w_out_full, target, gf)

    dymix, gwout_f, gwout_b = _out_proj_bwd(dzb, w_out_full, y)
    dproj, gpool, dscale = _pool_bwd(proj, pool_g, pool_scale, dymix)

    blk_out = (NDEV, DMIX // NDEV, D)
    blk_pool = (NDEV, NGROUP, GROUP // NDEV, GROUP)
    rest = [gwout_b.reshape(blk_out), gpool,
            lax.empty((NDEV - 1,) + blk_out[1:], jnp.bfloat16), lax.empty((NDEV - 1,) + blk_pool[1:], jnp.float32)]
    rest_send, rest_recv, rest, rest_token = _split_start("scatter_rest_start", rest, 2 * (NDEV - 1), _plan_rest)

    dproj, drecg, dlb = _hgrn_bwd(proj, lb_logits, rec_norm_g, o, states, dymix, dproj, rest_token)
    chip_sums, own_sum, landing, win_send, win_recv = _proj_bwd_w_near(
        place, ht, dproj, *_proj_bwd_w_far(place, ht, dproj))
    win = [chip_sums, landing]

    others = [a.reshape(1, D) for a in (dscale, dlb, dlb, drecg, dgf, sq)]
    grad_x, small_block = _proj_bwd_x(dproj, w_t, xs, norm1_g, dz, others, chip_sums)

    small = [small_block, lax.empty((NDEV, 8, D), jnp.float32)]
    small_send, small_recv, small, small_token = _split_start("gather_small_start", small, NDEV - 1, _plan_small)

    _, gpool_own, r_out, r_pool = _split_wait("scatter_rest_wait", rest, rest_send, rest_recv, _plan_rest,
                                              small_token)
    g_wout, d_wout, m_wout, v_wout = _reduce_adam(
        "adam_w_out", place,
        [(gwout_f.reshape(blk_out), pl.BlockSpec((None,) + blk_out[1:], lambda i, pr: (pr[0], 0, 0)), False),
         (r_out, pl.BlockSpec((NDEV - 1,) + blk_out[1:], lambda i, pr: (0, 0, 0)), True)],
        w_out, m_w_out, v_w_out, (1,), pl.BlockSpec((None,) + blk_out[1:], lambda i, pr: (0, 0, 0)))
    g_pool, d_pool, m_pool, v_pool = _reduce_adam(
        "adam_pool_w", place,
        [(gpool_own, pl.BlockSpec((None,) + blk_pool[1:], lambda i, pr: (pr[0], 0, 0, 0)), False),
         (r_pool, pl.BlockSpec((NDEV - 1,) + blk_pool[1:], lambda i, pr: (0, 0, 0, 0)), True)],
        pool_w, m_pool_w, v_pool_w, (1,), pl.BlockSpec((None,) + blk_pool[1:], lambda i, pr: (0, 0, 0, 0)))

    _, r_in = _split_wait("scatter_win_wait", win, win_send, win_recv, _plan_in, d_pool)
    g_win, d_win, m_win, v_win = _reduce_adam(
        "adam_w_in", place,
        [(own_sum, pl.BlockSpec((3, D // 2, TILE), lambda i, pr: (0, i, 0)), False),
         (r_in, pl.BlockSpec((3, 3, D // 2, TILE), lambda i, pr: (0, 0, i, 0)), True)],
        w_in, m_w_in, v_w_in, (2,), pl.BlockSpec((None, D // 2, 3 * TILE), lambda i, pr: (0, i, 0)))

    own_small, r_small = _split_wait("gather_small_wait", small, small_send, small_recv, _plan_small, d_win)
    loss, (g_s, d_s, m_s, v_s) = _small_adam(
        place, own_small, r_small,
        (norm1_g, pool_scale, lb_logits, rec_norm_g, gf),
        (m_norm1_g, m_pool_scale, m_lb_logits, m_rec_norm_g, m_final_norm_g.reshape(1, D)),
        (v_norm1_g, v_pool_scale, v_lb_logits, v_rec_norm_g, v_final_norm_g.reshape(1, D)))

    def outs(small, win, pool, wout):
        n1, ps, lbl, rg, fg = small
        return n1, win, pool, ps, lbl, rg, wout, fg.reshape(D)

    return (loss.reshape(()), grad_x[None],
            *outs(g_s, g_win, g_pool, g_wout), *outs(d_s, d_win, d_pool, d_wout),
            *outs(m_s, m_win, m_pool, m_wout), *outs(v_s, v_win, v_pool, v_wout))
```

```python
import jax
import jax.numpy as jnp
from jax import lax
from jax.experimental import pallas as pl
from jax.experimental.pallas import tpu as pltpu

T = 2048
D = 1024
NSEG = 6
NTILE = 24
TILE = 256
DMIX = 2048
NDEV = 8
HEAD = 128
NHEAD = 8
CHUNK = 64
NCHUNK = T // CHUNK
NB = 32
NGRP = NCHUNK // NB
NGROUP = 4
GROUP = 256
EPS = 1e-6
EXP_CAP = 115.0
MESH = pl.DeviceIdType.MESH
ANY = pl.BlockSpec(memory_space=pl.ANY)
HBM = pl.BlockSpec(memory_space=pltpu.HBM)
SEM = pl.BlockSpec(memory_space=pltpu.SEMAPHORE)
EFFECT = pltpu.SideEffectType.DATAFLOW_SIDE_EFFECTING

ADAM_LR = 0.001
ADAM_B1 = 0.9
ADAM_B2 = 0.999
ADAM_EPS = 1e-08
ADAM_WD = 0.01
ADAM_STEP = 10
BC1 = 1.0 - ADAM_B1 ** ADAM_STEP
BC2 = 1.0 - ADAM_B2 ** ADAM_STEP

MIB = 1 << 20


def _params(sem=None, vmem_mib=48):
    return pltpu.CompilerParams(dimension_semantics=sem, vmem_limit_bytes=vmem_mib * MIB)


def _sigmoid(v):
    return 1.0 / (1.0 + jnp.exp(-v))


def _dot(a, b, ca, cb, precision=None):
    return lax.dot_general(a, b, (((ca,), (cb,)), ((), ())), precision=precision,
                           preferred_element_type=jnp.float32)


def _bf(v):
    return v.astype(jnp.bfloat16)


def _in_hbm(a):
    return pltpu.with_memory_space_constraint(a, pltpu.HBM)


def _place():
    x, y, c = lax.axis_index("x"), lax.axis_index("y"), lax.axis_index("c")
    return x, y, c, 4 * x + 2 * y + c


def _peer(x, y, c, r):
    return (x ^ ((r >> 2) & 1), y ^ ((r >> 1) & 1), c ^ (r & 1))


def _gather_proj(x, g1, w_in, w_out, pool_w):
    def body(x_ref, g_ref, win_ref, wout_ref, pool_ref, ht_o, wt_o, woutb_o, wout_o, pool_o, proj_o,
             xbuf, hv, htv, wv, wob, pb, stage, send_sems, recv_sems, loc_sems, out_sems):
        px, py, c, my_idx = _place()
        fetch_x = pltpu.make_async_copy(x_ref, xbuf, loc_sems.at[5])
        fetch_x.start()
        me, sibling = (px, py, c), (px, py, 1 - c)
        chips = [(1 - px, py), (px, 1 - py), (1 - px, 1 - py)]
        for p in range(3):
            wv[3 * my_idx + p] = _bf(win_ref[0, :, p * TILE:(p + 1) * TILE])

        def index(bx, by, bc):
            return 4 * bx + 2 * by + bc

        def slot(w, block):
            return wv.at[pl.ds(3 * index(*block), 3)] if w == 0 else pool_o.at[index(*block)]

        def copy(k, w, block, to, src=None):
            return pltpu.make_async_remote_copy(
                src_ref=slot(w, block) if src is None else src, dst_ref=slot(w, block),
                send_sem=send_sems.at[2 * k + w], recv_sem=recv_sems.at[2 * k + w],
                device_id=to, device_id_type=MESH)

        def save(block):
            at = pl.ds(3 * index(*block), 3)
            pltpu.make_async_copy(wv.at[at], wt_o.at[at], loc_sems.at[4]).start()

        srcs = (slot(0, me), pb)
        first = []
        for w in (0, 1):
            if w == 1:
                pb[...] = _bf(pool_ref[0])
                wob[...] = _bf(wout_ref[0])
            group = [copy(1 + j, w, me, (*chip, c), src=srcs[w]) for j, chip in enumerate(chips[:2])]
            group.append(copy(0, w, me, sibling, src=srcs[w]))
            for cp in group:
                cp.start()
            first += group
        save(me)
        locs = [pltpu.make_async_copy(pb, slot(1, me), loc_sems.at[0]),
                pltpu.make_async_copy(wob, wout_o.at[my_idx], loc_sems.at[1]),
                pltpu.make_async_copy(wob, woutb_o, loc_sems.at[2])]
        for cp in locs:
            cp.start()

        fetch_x.wait()
        xv = xbuf[...]
        hv[...] = _bf(xv * lax.rsqrt(jnp.mean(xv * xv, axis=-1, keepdims=True) + EPS) * g_ref[...])
        rows = 256
        for r0 in range(0, T, rows):
            htv[:, r0:r0 + rows] = hv[r0:r0 + rows, :].T
        locs.append(pltpu.make_async_copy(htv, ht_o, loc_sems.at[3]))
        locs[-1].start()

        def out_copy(p, j):
            return pltpu.make_async_copy(stage.at[p], proj_o.at[j], out_sems.at[p])

        def project(nth, block):
            base = 3 * index(*block)

            def tile(p, carry):
                if nth > 0:
                    out_copy(p, base + p).wait()
                stage[p] = _dot(hv[...], wv[base + p], 1, 0)
                out_copy(p, base + p).start()
                return carry

            lax.fori_loop(0, 3, tile, 0)

        project(0, me)
        copy(0, 0, sibling, me).wait_recv()
        save(sibling)
        project(1, sibling)
        passed = []
        relay_from = (px ^ (1 - c), py ^ c, c)
        relay_to = (px ^ c, py ^ (1 - c), c)

        def arrived(w, j):
            copy(1 + j, w, (*chips[j], c), me).wait_recv()
            passed.append(copy(4 + j, w, (*chips[j], c), sibling))
            passed[-1].start()

        def relay(w):
            passed.append(copy(3, w, relay_from, relay_to))
            passed[-1].start()

        def handed(nth, j):
            copy(4 + j, 0, (*chips[j], 1 - c), me).wait_recv()
            save((*chips[j], 1 - c))
            project(nth, (*chips[j], 1 - c))

        arrived(0, 0)
        arrived(0, 1)
        relay(0)
        for j in range(2):
            save((*chips[j], c))
            project(2 + j, (*chips[j], c))
        handed(4, 0)
        handed(5, 1)
        arrived(1, 0)
        arrived(1, 1)
        relay(1)
        arrived(0, 2)
        save((*chips[2], c))
        project(6, (*chips[2], c))
        handed(7, 2)
        arrived(1, 2)
        copy(0, 1, sibling, me).wait_recv()
        for j, chip in enumerate(chips):
            copy(4 + j, 1, (*chip, 1 - c), me).wait_recv()
        keep = pltpu.make_async_copy(wv, wt_o, loc_sems.at[4])
        for p in range(3):
            out_copy(p, p).wait()
        for cp in first + passed:
            cp.wait_send()
        keep.wait()
        for cp in locs:
            cp.wait()

    vmem = pl.BlockSpec(memory_space=pltpu.VMEM)
    bf16 = jnp.bfloat16
    return pl.pallas_call(
        body, name="gather_proj",
        out_shape=(pltpu.HBM((D, T), bf16), pltpu.HBM((NTILE, D, TILE), bf16),
                   pltpu.HBM((DMIX // NDEV, D), bf16), pltpu.HBM((NDEV, DMIX // NDEV, D), bf16),
                   pltpu.HBM((NDEV, NGROUP, GROUP // NDEV, GROUP), bf16), pltpu.HBM((NTILE, T, TILE), jnp.float32)),
        in_specs=[ANY] + [vmem] * 4, out_specs=(ANY,) * 6,
        scratch_shapes=[pltpu.VMEM((T, D), jnp.float32),
                        pltpu.VMEM((T, D), bf16), pltpu.VMEM((D, T), bf16), pltpu.VMEM((NTILE, D, TILE), bf16),
                        pltpu.VMEM((DMIX // NDEV, D), bf16), pltpu.VMEM((NGROUP, GROUP // NDEV, GROUP), bf16),
                        pltpu.VMEM((3, T, TILE), jnp.float32),
                        pltpu.SemaphoreType.DMA((14,)), pltpu.SemaphoreType.DMA((14,)),
                        pltpu.SemaphoreType.DMA((6,)), pltpu.SemaphoreType.DMA((3,))],
        compiler_params=_params(vmem_mib=56),
    )(x, g1, w_in, w_out, pool_w)


def _split_start(name, arrays, n_copies, plan):
    k = len(arrays)

    def body(*refs):
        send_sems, recv_sems, token = refs[k], refs[k + 1], refs[-1]
        for i, (src, dst, to) in enumerate(plan(refs[:k])):
            pltpu.make_async_remote_copy(src_ref=src, dst_ref=dst, send_sem=send_sems.at[i],
                                         recv_sem=recv_sems.at[i], device_id=to, device_id_type=MESH).start()
        token[...] = jnp.zeros_like(token)

    out = pl.pallas_call(
        body, name=name,
        out_shape=(pltpu.SemaphoreType.DMA((n_copies,)), pltpu.SemaphoreType.DMA((n_copies,)),
                   *[pltpu.HBM(a.shape, a.dtype) for a in arrays], jax.ShapeDtypeStruct((8, 128), jnp.float32)),
        in_specs=[HBM] * k, out_specs=(SEM, SEM, *[HBM] * k, pl.BlockSpec(memory_space=pltpu.VMEM)),
        input_output_aliases={i: 2 + i for i in range(k)},
        compiler_params=pltpu.CompilerParams(has_side_effects=EFFECT),
    )(*[pltpu.with_memory_space_constraint(a, pltpu.HBM) for a in arrays])
    return out[0], out[1], out[2:2 + k], out[-1]


def _split_wait(name, arrays, send_sems, recv_sems, plan, after):
    k = len(arrays)

    def body(*refs):
        sends, recvs = refs[k], refs[k + 1]
        for i, (src, dst, to) in enumerate(plan(refs[:k])):
            cp = pltpu.make_async_remote_copy(src_ref=src, dst_ref=dst, send_sem=sends.at[i], recv_sem=recvs.at[i],
                                              device_id=to, device_id_type=MESH)
            cp.wait_send()
            cp.wait_recv()

    return pl.pallas_call(
        body, name=name,
        out_shape=tuple(pltpu.HBM(a.shape, a.dtype) for a in arrays),
        in_specs=[HBM] * k + [SEM, SEM, ANY], out_specs=(HBM,) * k,
        input_output_aliases={i: i for i in range(k)},
        compiler_params=pltpu.CompilerParams(has_side_effects=EFFECT),
    )(*arrays, send_sems, recv_sems, after)


def _plan_wout(refs):
    src, land = refs
    x, y, c, me = _place()
    return [(src, land.at[me], _peer(x, y, c, r)) for r in range(1, NDEV)]


def _plan_rest(refs):
    gob, gpf, r_out, r_pool = refs
    x, y, c, me = _place()
    plan = []
    for r in range(1, NDEV):
        plan.append((gob.at[me ^ r], r_out.at[r - 1], _peer(x, y, c, r)))
        plan.append((gpf.at[me ^ r], r_pool.at[r - 1], _peer(x, y, c, r)))
    return plan


def _plan_in(refs):
    sums, landing = refs
    x, y, c, _ = _place()
    plan = []
    for rel, (dx, dy) in enumerate(((1, 0), (0, 1), (1, 1))):
        for p in range(3):
            plan.append((sums.at[rel, p], landing.at[rel, p], (x ^ dx, y ^ dy, c)))
    return plan


def _plan_small(refs):
    small, land = refs
    x, y, c, me = _place()
    return [(small, land.at[me], _peer(x, y, c, r)) for r in range(1, NDEV)]


_POOL_SPECS = [pl.BlockSpec((None, T, GROUP), lambda g, base=base: (base + g, 0, 0)) for base in (0, 4)]
_HEAD_SPECS = [pl.BlockSpec((None, T, HEAD), lambda h, base=base: (base + h // 2, 0, h % 2))
               for base in (8, 12, 16, 20)]


_POOL_W_SPEC = pl.BlockSpec((NDEV, None, GROUP // NDEV, GROUP), lambda g: (0, g, 0, 0))


def _row_ids(shape):
    return lax.broadcasted_iota(jnp.int32, shape, 0)


BAND_ROWS = 128
HALO = 16


def _window_sum(a, gidx, lead):
    width = lax.shift_left(jnp.int32(2), gidx)
    shape = (BAND_ROWS, BAND_ROWS + HALO)
    t, j = lax.broadcasted_iota(jnp.int32, shape, 0), lax.broadcasted_iota(jnp.int32, shape, 1)
    first = t if lead else t + HALO - width + 1
    band = _bf(jnp.where(j >= first, jnp.where(j < first + width, 1.0, 0.0), 0.0))
    zeros = jnp.zeros((HALO, a.shape[1]), jnp.bfloat16)
    padded = [jnp.concatenate([p, zeros] if lead else [zeros, p], axis=0) for p in _split2(a)]
    out = []
    for r0 in range(0, T, BAND_ROWS):
        slab = jnp.concatenate([p[r0:r0 + BAND_ROWS + HALO] for p in padded], axis=1)
        r = _dot(band, slab, 1, 0)
        out.append(r[:, :a.shape[1]] + r[:, a.shape[1]:])
    return jnp.concatenate(out, axis=0)


def _window_mean(s, gidx):
    inv = jnp.where(gidx == 0, 0.5, jnp.where(gidx == 1, 0.25, jnp.where(gidx == 2, 0.125, 0.0625)))
    width = lax.shift_left(jnp.int32(2), gidx)
    head = s[:16] / jnp.minimum(_row_ids((16, s.shape[1])) + 1, width).astype(jnp.float32)
    return jnp.concatenate([head, s[16:] * inv], axis=0)


def _pool_fwd(proj, pool_w, pool_scale, token):
    def body(u_ref, pg_ref, w_ref, sc_ref, token_any, y_ref):
        del token_any
        gidx = pl.program_id(0)
        u, pg = u_ref[...], pg_ref[...]
        d = _window_mean(_window_sum(u, gidx, False), gidx) - u
        mixed = _dot(_bf(d), w_ref[...].reshape(GROUP, GROUP), 1, 0)
        y_ref[...] = _bf(mixed * sc_ref[...] * (pg * _sigmoid(pg)))

    return pl.pallas_call(
        body, name="pool_fwd", grid=(NGROUP,),
        in_specs=[*_POOL_SPECS, _POOL_W_SPEC, pl.BlockSpec((1, GROUP), lambda g: (0, g)), ANY],
        out_specs=pl.BlockSpec((T, GROUP), lambda g: (0, g)),
        out_shape=pltpu.HBM((T, DMIX), jnp.bfloat16),
        compiler_params=_params(("parallel",)),
    )(proj, proj, pool_w, pool_scale, token)


def _tri(lower):
    r = lax.broadcasted_iota(jnp.int32, (CHUNK, CHUNK), 0)
    c = lax.broadcasted_iota(jnp.int32, (CHUNK, CHUNK), 1)
    return (r >= c) if lower else (r <= c)


def _sum_rows_matrix():
    shape = (CHUNK + 16, CHUNK)
    r, c = lax.broadcasted_iota(jnp.int32, shape, 0), lax.broadcasted_iota(jnp.int32, shape, 1)
    run = jnp.where(c <= r, 1.0, 0.0)
    half = jnp.where(c < CHUNK // 2, 1.0, 0.0)
    return _bf(jnp.where(r < CHUNK, run, jnp.where(r < CHUNK + 8, 1.0, half)))


def _rev_sum_matrix():
    shape = (CHUNK, 2 * CHUNK)
    r, c = lax.broadcasted_iota(jnp.int32, shape, 0), lax.broadcasted_iota(jnp.int32, shape, 1)
    return _bf(jnp.where(c < CHUNK, jnp.where(c >= r, 1.0, 0.0), jnp.where(c - CHUNK < r, 1.0, 0.0)))


def _split2(a):
    hi = _bf(a)
    return [hi, _bf(a - hi.astype(jnp.float32))]


def _exact_sums(mat, pieces):
    x = jnp.concatenate([s for p in pieces for s in _split2(p)], axis=1)
    r = _dot(mat, x, 1, 0)
    return [r[:, 2 * j * HEAD:(2 * j + 1) * HEAD] + r[:, (2 * j + 1) * HEAD:(2 * j + 2) * HEAD]
            for j in range(len(pieces))]


def _gates(qv, fl, lb):
    sq = _sigmoid(qv)
    sg = _sigmoid(fl)
    f = lb + (1.0 - lb) * sg
    return dict(sq=sq, qs=qv * sq, sg=sg, f=f, kk=1.0 - f, g=jnp.log2(f))


def _decays(sums):
    big_g = sums[:CHUNK]
    total = sums[CHUNK:CHUNK + 8]
    g_last = jnp.tile(total, (CHUNK // 8, 1))
    g_mid = jnp.tile(sums[CHUNK + 8:], (CHUNK // 8, 1))
    return dict(
        e_q=jnp.exp2(big_g),
        e_k=jnp.exp2(g_last - big_g),
        e_qm=jnp.exp2(jnp.minimum(big_g - g_mid, EXP_CAP)),
        e_km=jnp.exp2(jnp.minimum(g_mid - big_g, EXP_CAP)),
        total8=jnp.exp2(total))


def _group_rows(gi):
    return [pl.ds(pl.multiple_of((gi * NB + j) * CHUNK, CHUNK), CHUNK) for j in range(NB)]


def _lower_bound(lb_ref):
    return _sigmoid(lb_ref[0:1, :] - lb_ref[1:2, :])


def _hgrn_fwd(proj, lb_logits, rec_g, y_in):
    def body(q_ref, f_ref, i_ref, gate_ref, lb_ref, rg_ref, y_any, y_ref, o_ref, st_ref):
        del y_any
        lb = _lower_bound(lb_ref)
        causal = _tri(True)
        smat = _sum_rows_matrix()

        def group(gi, st):
            rows = _group_rows(gi)
            ts = [_gates(q_ref[r, :], f_ref[r, :], lb) for r in rows]
            ds = [_decays(s) for s in _exact_sums(smat, [t["g"] for t in ts])]
            vs = [_bf(i_ref[r, :]) for r in rows]
            q_m = [_bf(t["qs"] * d["e_qm"]) for t, d in zip(ts, ds)]
            k_m = [_bf(t["kk"] * d["e_km"]) for t, d in zip(ts, ds)]
            q_e = [_bf(t["qs"] * d["e_q"]) for t, d in zip(ts, ds)]
            k_e = [_bf(t["kk"] * d["e_k"]) for t, d in zip(ts, ds)]
            a = [_bf(jnp.where(causal, _dot(q_m[j], k_m[j], 1, 1), 0.0)) for j in range(NB)]
            intra = [_dot(a[j], vs[j], 1, 0) for j in range(NB)]
            upd = [_dot(vs[j], k_e[j], 0, 0) for j in range(NB)]
            for j in range(NB):
                st_ref[gi * NB + j] = st
                o_ref[rows[j], :] = intra[j] + _dot(q_e[j], _bf(st), 1, 1)
                st = st * jnp.tile(ds[j]["total8"], (HEAD // 8, 1)) + upd[j]
            return st

        lax.fori_loop(0, NGRP, group, jnp.zeros((HEAD, HEAD), jnp.float32))
        o = o_ref[...]
        rn = o * lax.rsqrt(jnp.mean(o * o, axis=-1, keepdims=True) + EPS)
        gate = gate_ref[...]
        y_ref[...] = _bf(rn * rg_ref[...] * (gate * _sigmoid(gate)))

    return pl.pallas_call(
        body, name="hgrn_fwd", grid=(NHEAD,),
        in_specs=[*_HEAD_SPECS,
                  pl.BlockSpec((2, HEAD), lambda h: (0, h)),
                  pl.BlockSpec((1, HEAD), lambda h: (0, h)),
                  pl.BlockSpec(memory_space=pl.ANY)],
        out_specs=(pl.BlockSpec((T, HEAD), lambda h: (0, NHEAD + h)),
                   pl.BlockSpec((T, HEAD), lambda h: (0, h)),
                   pl.BlockSpec((None, NCHUNK, HEAD, HEAD), lambda h: (h, 0, 0, 0))),
        out_shape=(pltpu.HBM((T, DMIX), jnp.bfloat16), pltpu.HBM((T, D), jnp.float32),
                   pltpu.HBM((NHEAD, NCHUNK, HEAD, HEAD), jnp.float32)),
        input_output_aliases={6: 0},
        compiler_params=_params(("parallel",)),
    )(proj, proj, proj, proj, lb_logits, rec_g, y_in)


def _out_proj_loss(x, y, w_out, target, gf):
    rows = 512
    parts = [slice(k * rows // 2, (k + 1) * rows // 2) for k in range(2)]

    def body(x_ref, y_ref, w_ref, t_ref, g_ref, dz_ref, dzb_ref, sq_ref, dg_ref):
        zs = [x_ref[p, :] + _dot(y_ref[p, :], w_ref[...], 1, 0) for p in parts]
        sq = dg = 0.0
        for p, z in zip(parts, zs):
            r = lax.rsqrt(jnp.mean(z * z, axis=-1, keepdims=True) + EPS)
            zhat = z * r
            err = zhat * g_ref[...] - t_ref[p, :]
            dy = err * (1.0 / D)
            gdy = dy * g_ref[...]
            dz = r * (gdy - zhat * jnp.mean(zhat * gdy, axis=-1, keepdims=True))
            dz_ref[p, :] = dz
            dzb_ref[p, :] = _bf(dz)
            sq = sq + jnp.sum(err * err, axis=0, keepdims=True)
            dg = dg + jnp.sum(zhat * dy, axis=0, keepdims=True)

        @pl.when(pl.program_id(0) == 0)
        def _():
            sq_ref[...] = sq
            dg_ref[...] = dg

        @pl.when(pl.program_id(0) != 0)
        def _():
            sq_ref[...] += sq
            dg_ref[...] += dg

    tile = pl.BlockSpec((rows, D), lambda i: (i, 0))
    vec = pl.BlockSpec((1, D), lambda i: (0, 0))
    return pl.pallas_call(
        body, name="out_proj_loss", grid=(T // rows,),
        in_specs=[tile, pl.BlockSpec((rows, DMIX), lambda i: (i, 0)), pl.BlockSpec((DMIX, D), lambda i: (0, 0)),
                  tile, vec],
        out_specs=(tile, tile, vec, vec),
        out_shape=(pltpu.HBM((T, D), jnp.float32), pltpu.HBM((T, D), jnp.bfloat16),
                   pltpu.HBM((1, D), jnp.float32), pltpu.HBM((1, D), jnp.float32)),
        compiler_params=_params(("arbitrary",)),
    )(x, y, w_out, target, gf)


def _out_proj_bwd(dzb, w_out, y):
    tn = 512

    def body(dz_ref, w_ref, y_ref, dy_ref, gw_ref, gwb_ref):
        dz = dz_ref[...]
        dy_ref[...] = _dot(dz, w_ref[...], 1, 1)
        gw = _dot(y_ref[...], dz, 0, 0)
        gw_ref[...] = gw
        gwb_ref[...] = _bf(gw)

    return pl.pallas_call(
        body, name="out_proj_bwd", grid=(DMIX // tn,),
        in_specs=[pl.BlockSpec((T, D), lambda n: (0, 0)), pl.BlockSpec((tn, D), lambda n: (n, 0)),
                  pl.BlockSpec((T, tn), lambda n: (0, n))],
        out_specs=(pl.BlockSpec((T, tn), lambda n: (0, n)), pl.BlockSpec((tn, D), lambda n: (n, 0)),
                   pl.BlockSpec((tn, D), lambda n: (n, 0))),
        out_shape=(pltpu.HBM((T, DMIX), jnp.float32), pltpu.HBM((DMIX, D), jnp.float32),
                   pltpu.HBM((DMIX, D), jnp.bfloat16)),
        compiler_params=_params(("parallel",)),
    )(dzb, w_out, y)


def _hgrn_bwd(proj, lb_logits, rec_g, o, states, dymix, dproj_in, token):
    def body(q_ref, f_ref, i_ref, gate_ref, lb_ref, rg_ref, o_ref, st_ref, dy_ref, dp_any, token_any,
             dp_ref, drg_ref, dlb_ref, do_ref):
        del dp_any, token_any
        lb = _lower_bound(lb_ref)
        causal = _tri(True)
        smat, rmat = _sum_rows_matrix(), _rev_sum_matrix()

        o = o_ref[...]
        rs = lax.rsqrt(jnp.mean(o * o, axis=-1, keepdims=True) + EPS)
        rn = o * rs
        gate = gate_ref[...]
        sgate = _sigmoid(gate)
        dyv = dy_ref[...]
        d_r = dyv * (gate * sgate)
        dp_ref[3] = _bf(dyv * (rn * rg_ref[...]) * (sgate * (1.0 + gate * (1.0 - sgate))))
        drg_ref[...] = jnp.sum(d_r * rn, axis=0, keepdims=True)
        drn = d_r * rg_ref[...]
        do_ref[...] = rs * (drn - rn * jnp.mean(rn * drn, axis=-1, keepdims=True))

        def group(i, carry):
            dst, dlb = carry
            gi = NGRP - 1 - i
            rows = _group_rows(gi)
            span = range(NB)
            qvs = [q_ref[r, :] for r in rows]
            ts = [_gates(qv, f_ref[r, :], lb) for qv, r in zip(qvs, rows)]
            ds = [_decays(s) for s in _exact_sums(smat, [t["g"] for t in ts])]
            vs = [_bf(i_ref[r, :]) for r in rows]
            dos = [_bf(do_ref[r, :]) for r in rows]
            sts = [st_ref[gi * NB + j] for j in span]
            qe_f = [t["qs"] * d["e_q"] for t, d in zip(ts, ds)]
            ke_f = [t["kk"] * d["e_k"] for t, d in zip(ts, ds)]
            q_e, k_e = [_bf(a) for a in qe_f], [_bf(a) for a in ke_f]
            q_m = [_bf(t["qs"] * d["e_qm"]) for t, d in zip(ts, ds)]
            k_m = [_bf(t["kk"] * d["e_km"]) for t, d in zip(ts, ds)]
            a = [_bf(jnp.where(causal, _dot(q_m[j], k_m[j], 1, 1), 0.0)) for j in span]
            da = [_bf(jnp.where(causal, _dot(dos[j], vs[j], 1, 1), 0.0)) for j in span]
            dqm = [_dot(da[j], k_m[j], 1, 0) for j in span]
            dkm = [_dot(da[j], q_m[j], 0, 0) for j in span]
            dv_in = [_dot(a[j], dos[j], 0, 0) for j in span]
            dqe = [_dot(dos[j], _bf(sts[j]), 1, 0) for j in span]
            grow = [_dot(dos[j], q_e[j], 0, 0) for j in span]
            dke, carried = [None] * NB, [None] * NB
            for j in reversed(span):
                dst_b = _bf(dst)
                dke[j] = _dot(vs[j], dst_b, 1, 0)
                dp_ref[2, rows[j], :] = _bf(dv_in[j] + _dot(k_e[j], dst_b, 1, 1))
                carried[j] = ds[j]["total8"] * jnp.sum(dst * sts[j], axis=0, keepdims=True)
                dst = dst * jnp.tile(ds[j]["total8"], (HEAD // 8, 1)) + grow[j]
            kdk = [ke_f[j] * dke[j] for j in span]
            pos = [(q_m[j].astype(jnp.float32) * dqm[j] - k_m[j].astype(jnp.float32) * dkm[j]) + qe_f[j] * dqe[j]
                   for j in span]
            dgs = _exact_sums(rmat, [jnp.concatenate([pos[j], kdk[j]], axis=0) for j in span])
            for j in span:
                t, d = ts[j], ds[j]
                dg = dgs[j] + jnp.tile(carried[j], (CHUNK // 8, 1))
                dqs = dqm[j] * d["e_qm"] + dqe[j] * d["e_q"]
                dkk = dkm[j] * d["e_km"] + dke[j] * d["e_k"]
                df = dg / t["f"] - dkk
                dp_ref[1, rows[j], :] = _bf(df * (1.0 - lb) * (t["sg"] * (1.0 - t["sg"])))
                dp_ref[0, rows[j], :] = _bf(dqs * (t["sq"] * (1.0 + qvs[j] * (1.0 - t["sq"]))))
                dlb = dlb + df * (1.0 - t["sg"])
            return dst, dlb

        _, dlb = lax.fori_loop(0, NGRP, group, (jnp.zeros((HEAD, HEAD), jnp.float32),
                                                jnp.zeros((CHUNK, HEAD), jnp.float32)))
        dlb_ref[...] = jnp.sum(dlb, axis=0, keepdims=True)

    vec = pl.BlockSpec((1, HEAD), lambda h: (0, h))
    return pl.pallas_call(
        body, name="hgrn_bwd", grid=(NHEAD,),
        in_specs=[*_HEAD_SPECS,
                  pl.BlockSpec((2, HEAD), lambda h: (0, h)), vec,
                  pl.BlockSpec((T, HEAD), lambda h: (0, h)),
                  pl.BlockSpec((None, NCHUNK, HEAD, HEAD), lambda h: (h, 0, 0, 0)),
                  pl.BlockSpec((T, HEAD), lambda h: (0, NHEAD + h)), ANY, ANY],
        out_specs=(pl.BlockSpec((4, T, HEAD), lambda h: (0, 0, h)), vec, vec),
        out_shape=(pltpu.HBM((NSEG, T, D), jnp.bfloat16),
                   pltpu.HBM((1, D), jnp.float32), pltpu.HBM((1, D), jnp.float32)),
        scratch_shapes=[pltpu.VMEM((T, HEAD), jnp.float32)],
        input_output_aliases={9: 0},
        compiler_params=_params(("parallel",)),
    )(proj, proj, proj, proj, lb_logits, rec_g, o, states, dymix, dproj_in, token)


def _pool_bwd(proj, pool_w, pool_scale, dymix):
    def body(u_ref, pg_ref, w_ref, sc_ref, dy_ref, dp_ref, gw_ref, gs_ref):
        gidx = pl.program_id(0)
        u, pg = u_ref[...], pg_ref[...]
        w = w_ref[...].reshape(GROUP, GROUP)
        d = _bf(_window_mean(_window_sum(u, gidx, False), gidx) - u)
        mixed = _dot(d, w, 1, 0)
        spg = _sigmoid(pg)
        dyv = dy_ref[...]
        d_p = dyv * (pg * spg)
        dp_ref[1] = _bf(dyv * (mixed * sc_ref[...]) * (spg * (1.0 + pg * (1.0 - spg))))
        gs_ref[...] = jnp.sum(d_p * mixed, axis=0, keepdims=True)
        dmixed = _bf(d_p * sc_ref[...])
        gw_ref[...] = _dot(d, dmixed, 0, 0).reshape(gw_ref.shape)
        dd = _dot(dmixed, w, 1, 1)
        dp_ref[0] = _bf(_window_sum(_window_mean(dd, gidx), gidx, True) - dd)

    return pl.pallas_call(
        body, name="pool_bwd", grid=(NGROUP,),
        in_specs=[*_POOL_SPECS, _POOL_W_SPEC,
                  pl.BlockSpec((1, GROUP), lambda g: (0, g)),
                  pl.BlockSpec((T, GROUP), lambda g: (0, g))],
        out_specs=(pl.BlockSpec((2, T, GROUP), lambda g: (2, 0, g)), _POOL_W_SPEC,
                   pl.BlockSpec((1, GROUP), lambda g: (0, g))),
        out_shape=(pltpu.HBM((NSEG, T, D), jnp.bfloat16),
                   pltpu.HBM((NDEV, NGROUP, GROUP // NDEV, GROUP), jnp.float32),
                   pltpu.HBM((1, D), jnp.float32)),
        compiler_params=_params(("parallel",)),
    )(proj, proj, pool_w, pool_scale, dymix)


HALF = NTILE // 2
AWAY = HALF - 3


def _dproj_tile(chip, side, p):
    j = 6 * chip + 3 * side + p
    return ((j // 4 + 4) % NSEG, 0, j % 4)


def _sibling_copy(sib_out, sib_in, send_sems, recv_sems, slot):
    x, y, c, _ = _place()
    return pltpu.make_async_remote_copy(
        src_ref=sib_out.at[slot], dst_ref=sib_in.at[slot], send_sem=send_sems.at[slot],
        recv_sem=recv_sems.at[slot], device_id=(x, y, 1 - c), device_id_type=MESH)


def _proj_bwd_w_far(place, ht, dproj):
    def body(place_ref, h_ref, dpa_ref, dpb_ref, sib_out, sib_in, send_sems, recv_sems, stage, loc_sems):
        del place_ref
        i = pl.program_id(0)

        def to_hbm(k):
            return pltpu.make_async_copy(stage.at[k], sib_out.at[k], loc_sems.at[k])

        def send(k):
            to_hbm(k).wait()
            _sibling_copy(sib_out, sib_in, send_sems, recv_sems, k).start()

        stage[2 * i] = _bf(_dot(h_ref[...], dpa_ref[...], 1, 0))
        stage[2 * i + 1] = _bf(_dot(h_ref[...], dpb_ref[...], 1, 0))

        @pl.when(i > 0)
        def _():
            send(2 * i - 2)
            send(2 * i - 1)

        to_hbm(2 * i).start()
        to_hbm(2 * i + 1).start()

        @pl.when(i == HALF // 2 - 1)
        def _():
            send(2 * i)
            send(2 * i + 1)

    def tile(k, pr):
        return _dproj_tile(k // 3, 1 - pr[2], k % 3)

    buf = pltpu.HBM((HALF, D, TILE), jnp.bfloat16)
    sems = pltpu.SemaphoreType.DMA((HALF,))
    return pl.pallas_call(
        body, name="proj_bwd_w_far",
        grid_spec=pltpu.PrefetchScalarGridSpec(
            num_scalar_prefetch=1, grid=(HALF // 2,),
            in_specs=[pl.BlockSpec((D, T), lambda i, pr: (0, 0)),
                      pl.BlockSpec((None, T, TILE), lambda i, pr: tile(2 * i, pr)),
                      pl.BlockSpec((None, T, TILE), lambda i, pr: tile(2 * i + 1, pr))],
            out_specs=(HBM, HBM, SEM, SEM),
            scratch_shapes=[pltpu.VMEM((HALF, D, TILE), jnp.bfloat16), pltpu.SemaphoreType.DMA((HALF,))]),
        out_shape=(buf, buf, sems, sems),
        compiler_params=pltpu.CompilerParams(dimension_semantics=("arbitrary",), vmem_limit_bytes=48 * MIB,
                                             has_side_effects=EFFECT),
    )(place, ht, dproj, dproj)


def _proj_bwd_w_near(place, ht, dproj, sib_out, sib_in, sib_send, sib_recv):
    def owner_chip(k, pr):
        return jnp.where(k < AWAY, (pr[1] + 1 + k % 3) % 4, pr[1])

    def tile_p(k):
        return jnp.where(k < AWAY, k // 3, k - AWAY)

    def body(place_ref, h_ref, dp_ref, sib_out, sib_in, sib_send, sib_recv, sums, own_ref, landing, out_send,
             out_recv, recvbuf, outbuf, in_sems, loc_sems):
        i = pl.program_id(0)
        px, py, c, _ = _place()

        def slot_of(k):
            return 3 * owner_chip(k, place_ref) + tile_p(k)

        def load(k):
            return pltpu.make_async_copy(sib_in.at[slot_of(k)], recvbuf.at[k % 2], in_sems.at[k % 2])

        def fetch(k):
            _sibling_copy(sib_out, sib_in, sib_send, sib_recv, slot_of(k)).wait_recv()
            load(k).start()

        def route(k):
            chip = owner_chip(k, place_ref)
            cx, cy = chip // 2, chip % 2
            return cx, cy, (cx ^ px) + 2 * (cy ^ py) - 1, tile_p(k)

        def to_hbm(k):
            _, _, rel, p = route(k)
            return pltpu.make_async_copy(outbuf.at[k], sums.at[rel, p], loc_sems.at[k])

        def to_owner(k):
            cx, cy, rel, p = route(k)
            return pltpu.make_async_remote_copy(
                src_ref=sums.at[rel, p], dst_ref=landing.at[rel, p], send_sem=out_send.at[3 * rel + p],
                recv_sem=out_recv.at[3 * rel + p], device_id=(cx, cy, c), device_id_type=MESH)

        @pl.when(i == 0)
        def _():
            fetch(i)

        @pl.when(i < HALF - 1)
        def _():
            fetch(i + 1)

        load(i).wait()
        total = _dot(h_ref[...], dp_ref[...], 1, 0) + recvbuf[i % 2].astype(jnp.float32)
        own_ref[...] = total
        outbuf[jnp.minimum(i, AWAY)] = _bf(total)

        @pl.when(i < AWAY)
        def _():
            to_hbm(i).start()

        @pl.when(jnp.logical_and(i > 0, i <= AWAY))
        def _():
            to_hbm(i - 1).wait()
            to_owner(i - 1).start()

        @pl.when(i == HALF - 1)
        def _():
            for slot in range(HALF):
                _sibling_copy(sib_out, sib_in, sib_send, sib_recv, slot).wait_send()

    travelling = pltpu.HBM((3, 3, D, TILE), jnp.bfloat16)
    sems = pltpu.SemaphoreType.DMA((AWAY,))
    return pl.pallas_call(
        body, name="proj_bwd_w_near",
        grid_spec=pltpu.PrefetchScalarGridSpec(
            num_scalar_prefetch=1, grid=(HALF,),
            in_specs=[pl.BlockSpec((D, T), lambda i, pr: (0, 0)),
                      pl.BlockSpec((None, T, TILE), lambda i, pr: _dproj_tile(owner_chip(i, pr), pr[2], tile_p(i))),
                      HBM, HBM, SEM, SEM],
            out_specs=(HBM, pl.BlockSpec((None, D, TILE), lambda i, pr: (jnp.where(i < AWAY, 0, i % 3), 0, 0)),
                       HBM, SEM, SEM),
            scratch_shapes=[pltpu.VMEM((2, D, TILE), jnp.bfloat16), pltpu.VMEM((AWAY + 1, D, TILE), jnp.bfloat16),
                            pltpu.SemaphoreType.DMA((2,)), pltpu.SemaphoreType.DMA((AWAY,))]),
        out_shape=(travelling, pltpu.HBM((3, D, TILE), jnp.float32), travelling, sems, sems),
        compiler_params=pltpu.CompilerParams(dimension_semantics=("arbitrary",), vmem_limit_bytes=48 * MIB,
                                             has_side_effects=EFFECT),
    )(place, ht, dproj, sib_out, sib_in, sib_send, sib_recv)


def _proj_bwd_x(dproj, w_t, x, g1, dz, others, token):
    tm = 1024
    pairs = NSEG // 2
    k = len(others)

    def body(dp_ref, w_ref, x_any, g_ref, dz_any, *refs):
        dx_ref, dg_ref, wcat, acc, xbuf, dzbuf, row_sems = refs[k + 1:]
        m, s = pl.program_id(0), pl.program_id(1)
        mine = pl.ds(pl.multiple_of(m * tm, tm), tm)
        fetches = [pltpu.make_async_copy(x_any.at[mine], xbuf, row_sems.at[0]),
                   pltpu.make_async_copy(dz_any.at[mine], dzbuf, row_sems.at[1])]

        @pl.when(m == 0)
        def _():
            for i in range(8):
                wcat[s, :, i * TILE:(i + 1) * TILE] = w_ref[i]

        @pl.when(s == 0)
        def _():
            acc[...] = jnp.zeros((tm, D), jnp.float32)
            for cp in fetches:
                cp.start()

        acc[...] += _dot(jnp.concatenate([dp_ref[0], dp_ref[1]], axis=1), wcat[s], 1, 1)

        @pl.when(s == pairs - 1)
        def _():
            for cp in fetches:
                cp.wait()
            xv = xbuf[...]
            rs = lax.rsqrt(jnp.mean(xv * xv, axis=-1, keepdims=True) + EPS)
            xhat = xv * rs
            dhv = acc[...]
            gdh = dhv * g_ref[...]
            dx_ref[...] = dzbuf[...] + rs * (gdh - xhat * jnp.mean(xhat * gdh, axis=-1, keepdims=True))
            dg = jnp.sum(xhat * dhv, axis=0, keepdims=True)

            @pl.when(m == 0)
            def _():
                dg_ref[0:1, :] = dg
                for r, ref in enumerate(refs[:k]):
                    dg_ref[1 + r:2 + r, :] = ref[...]
                dg_ref[1 + k:, :] = jnp.zeros((7 - k, D), jnp.float32)

            @pl.when(m != 0)
            def _():
                dg_ref[0:1, :] += dg

    rows = pl.BlockSpec((tm, D), lambda m, s: (m, 0))
    vec = pl.BlockSpec((1, D), lambda m, s: (0, 0))
    return pl.pallas_call(
        body, name="proj_bwd_x", grid=(T // tm, pairs),
        in_specs=[pl.BlockSpec((2, tm, D), lambda m, s: (s, m, 0)),
                  pl.BlockSpec((8, D, TILE), lambda m, s: ((jnp.where(m == 0, s, pairs - 1) + 1) % pairs, 0, 0)),
                  HBM, vec, HBM, *[vec] * k, ANY],
        out_specs=(rows, pl.BlockSpec((8, D), lambda m, s: (0, 0))),
        out_shape=(jax.ShapeDtypeStruct((T, D), jnp.float32), pltpu.HBM((8, D), jnp.float32)),
        scratch_shapes=[pltpu.VMEM((pairs, D, 2 * D), jnp.bfloat16), pltpu.VMEM((tm, D), jnp.float32),
                        pltpu.VMEM((tm, D), jnp.float32), pltpu.VMEM((tm, D), jnp.float32),
                        pltpu.SemaphoreType.DMA((2,))],
        compiler_params=_params(("arbitrary", "arbitrary"), vmem_mib=58),
    )(dproj, w_t, _in_hbm(x), g1, _in_hbm(dz), *[_in_hbm(a) for a in others], token)


def _adamw(w, g, m, v):
    m_new = ADAM_B1 * m + (1.0 - ADAM_B1) * g
    v_new = ADAM_B2 * v + (1.0 - ADAM_B2) * (g * g)
    delta = -ADAM_LR * ((m_new / BC1) / (jnp.sqrt(v_new / BC2) + ADAM_EPS) + ADAM_WD * w)
    return delta, m_new, v_new


def _reduce_adam(name, place, parts, w, m, v, grid, w_spec):
    n = len(parts)

    def body(place_ref, *refs):
        del place_ref
        w_ref, m_ref, v_ref, g_ref, d_ref, mo_ref, vo_ref = refs[n:]
        g = None
        for ref, (_, _, stacked) in zip(refs[:n], parts):
            terms = [ref[r] for r in range(ref.shape[0])] if stacked else [ref[...]]
            for t in terms:
                if t.shape[-1] != w_ref.shape[-1]:
                    t = jnp.concatenate([t[p] for p in range(t.shape[0])], axis=1)
                g = t.astype(jnp.float32) if g is None else g + t.astype(jnp.float32)
        delta, m_new, v_new = _adamw(w_ref[...], g, m_ref[...], v_ref[...])
        g_ref[...] = g
        d_ref[...] = delta
        mo_ref[...] = m_new
        vo_ref[...] = v_new

    shape = jax.ShapeDtypeStruct(w.shape, jnp.float32)
    return pl.pallas_call(
        body, name=name,
        grid_spec=pltpu.PrefetchScalarGridSpec(
            num_scalar_prefetch=1, grid=grid,
            in_specs=[spec for _, spec, _ in parts] + [w_spec] * 3, out_specs=(w_spec,) * 4),
        out_shape=(shape,) * 4,
        compiler_params=_params(("parallel",)),
    )(place, *[_in_hbm(a) for a in [a for a, _, _ in parts] + [w, m, v]])


SMALL_ROWS = (1, 1, 2, 1, 1)


def _small_adam(place, own, parts, ws, ms, vs):
    n = len(SMALL_ROWS)

    def body(place_ref, own_ref, p_ref, *refs):
        ins, outs, bufs = refs[:3 * n], refs[3 * n:3 * n + 1 + 4 * n], refs[3 * n + 1 + 4 * n:]

        def stacked(group, buf):
            r0 = 0
            for ref, k in zip(group, SMALL_ROWS):
                buf[r0:r0 + k, :] = ref[...]
                r0 += k
            buf[r0:, :] = jnp.zeros((8 - r0, D), jnp.float32)
            return buf[...]

        wv, mv, vv = (stacked(ins[n * j:n * j + n], bufs[j]) for j in range(3))
        me = place_ref[0]
        g = None
        for s in range(NDEV):
            term = jnp.where(me == s, own_ref[...], p_ref[s])
            g = term if g is None else g + term
        rows = _row_ids(wv.shape)
        other = jnp.where(rows == 2, pltpu.roll(wv, 7, 0), jnp.where(rows == 3, pltpu.roll(wv, 1, 0), 0.0))
        lbv = _sigmoid(wv - other)
        sign = jnp.where(rows == 2, 1.0, -1.0)
        g = jnp.where((rows == 2) | (rows == 3), sign * g * lbv * (1.0 - lbv), g)
        delta, m_new, v_new = _adamw(wv, g, mv, vv)
        outs[0][...] = jnp.sum(g[6:7], axis=1, keepdims=True) * (0.5 / D)
        for j, val in enumerate((g, delta, m_new, v_new)):
            r0 = 0
            for ref, k in zip(outs[1 + n * j:1 + n * j + n], SMALL_ROWS):
                ref[...] = val[r0:r0 + k]
                r0 += k

    vmem = pl.BlockSpec(memory_space=pltpu.VMEM)
    shapes = [jax.ShapeDtypeStruct((k, D), jnp.float32) for k in SMALL_ROWS]
    out = pl.pallas_call(
        body, name="small_adam", out_shape=(jax.ShapeDtypeStruct((1, 1), jnp.float32), *shapes * 4),
        in_specs=[pl.BlockSpec(memory_space=pltpu.SMEM)] + [vmem] * (2 + 3 * n), out_specs=(vmem,) * (1 + 4 * n),
        scratch_shapes=[pltpu.VMEM((8, D), jnp.float32)] * 3,
    )(place, own, parts, *ws, *ms, *vs)
    return out[0], [out[1 + n * j:1 + n * j + n] for j in range(4)]


def kernel(x, norm1_g, w_in, pool_w, pool_scale, lb_logits, rec_norm_g, w_out, final_norm_g, loss_target, m_norm1_g, m_w_in, m_pool_w, m_pool_scale, m_lb_logits, m_rec_norm_g, m_w_out, m_final_norm_g, v_norm1_g, v_w_in, v_pool_w, v_pool_scale, v_lb_logits, v_rec_norm_g, v_w_out, v_final_norm_g):
    xs = x[0]
    target = loss_target[0]
    ix, iy, ic = lax.axis_index("x"), lax.axis_index("y"), lax.axis_index("c")
    place = jnp.stack([4 * ix + 2 * iy + ic, 2 * ix + iy, ic]).astype(jnp.int32)
    gf = final_norm_g.reshape(1, D)

    ht, w_t, w_out_b, w_out_g, pool_g, proj = _gather_proj(xs, norm1_g, w_in, w_out, pool_w)
    wout = [w_out_b, w_out_g]
    wout_send, wout_recv, wout, wout_token = _split_start("gather_wout_start", wout, NDEV - 1, _plan_wout)

    y = _pool_fwd(proj, pool_g, pool_scale, wout_token)
    y, o, states = _hgrn_fwd(proj, lb_logits, rec_norm_g, y)
    _, w_out_g = _split_wait("gather_wout_wait", wout, wout_send, wout_recv, _plan_wout, o)
    w_out_full = _in_hbm(w_out_g.reshape(DMIX, D))
    dz, dzb, sq, dgf = _out_proj_loss(xs, y, w_out_full, target, gf)

    dymix, gwout_f, gwout_b = _out_proj_bwd(dzb, w_out_full, y)
    dproj, gpool, dscale = _pool_bwd(proj, pool_g, pool_scale, dymix)

    blk_out = (NDEV, DMIX // NDEV, D)
    blk_pool = (NDEV, NGROUP, GROUP // NDEV, GROUP)
    rest = [gwout_b.reshape(blk_out), gpool,
            lax.empty((NDEV - 1,) + blk_out[1:], jnp.bfloat16), lax.empty((NDEV - 1,) + blk_pool[1:], jnp.float32)]
    rest_send, rest_recv, rest, rest_token = _split_start("scatter_rest_start", rest, 2 * (NDEV - 1), _plan_rest)

    dproj, drecg, dlb = _hgrn_bwd(proj, lb_logits, rec_norm_g, o, states, dymix, dproj, rest_token)
    chip_sums, own_sum, landing, win_send, win_recv = _proj_bwd_w_near(
        place, ht, dproj, *_proj_bwd_w_far(place, ht, dproj))
    win = [chip_sums, landing]

    others = [a.reshape(1, D) for a in (dscale, dlb, dlb, drecg, dgf, sq)]
    grad_x, small_block = _proj_bwd_x(dproj, w_t, xs, norm1_g, dz, others, chip_sums)

    small = [small_block, lax.empty((NDEV, 8, D), jnp.float32)]
    small_send, small_recv, small, small_token = _split_start("gather_small_start", small, NDEV - 1, _plan_small)

    _, gpool_own, r_out, r_pool = _split_wait("scatter_rest_wait", rest, rest_send, rest_recv, _plan_rest,
                                              small_token)
    g_wout, d_wout, m_wout, v_wout = _reduce_adam(
        "adam_w_out", place,
        [(gwout_f.reshape(blk_out), pl.BlockSpec((None,) + blk_out[1:], lambda i, pr: (pr[0], 0, 0)), False),
         (r_out, pl.BlockSpec((NDEV - 1,) + blk_out[1:], lambda i, pr: (0, 0, 0)), True)],
        w_out, m_w_out, v_w_out, (1,), pl.BlockSpec((None,) + blk_out[1:], lambda i, pr: (0, 0, 0)))
    g_pool, d_pool, m_pool, v_pool = _reduce_adam(
        "adam_pool_w", place,
        [(gpool_own, pl.BlockSpec((None,) + blk_pool[1:], lambda i, pr: (pr[0], 0, 0, 0)), False),
         (r_pool, pl.BlockSpec((NDEV - 1,) + blk_pool[1:], lambda i, pr: (0, 0, 0, 0)), True)],
        pool_w, m_pool_w, v_pool_w, (1,), pl.BlockSpec((None,) + blk_pool[1:], lambda i, pr: (0, 0, 0, 0)))

    _, r_in = _split_wait("scatter_win_wait", win, win_send, win_recv, _plan_in, d_pool)
    g_win, d_win, m_win, v_win = _reduce_adam(
        "adam_w_in", place,
        [(own_sum, pl.BlockSpec((3, D // 2, TILE), lambda i, pr: (0, i, 0)), False),
         (r_in, pl.BlockSpec((3, 3, D // 2, TILE), lambda i, pr: (0, 0, i, 0)), True)],
        w_in, m_w_in, v_w_in, (2,), pl.BlockSpec((None, D // 2, 3 * TILE), lambda i, pr: (0, i, 0)))

    own_small, r_small = _split_wait("gather_small_wait", small, small_send, small_recv, _plan_small, d_win)
    loss, (g_s, d_s, m_s, v_s) = _small_adam(
        place, own_small, r_small,
        (norm1_g, pool_scale, lb_logits, rec_norm_g, gf),
        (m_norm1_g, m_pool_scale, m_lb_logits, m_rec_norm_g, m_final_norm_g.reshape(1, D)),
        (v_norm1_g, v_pool_scale, v_lb_logits, v_rec_norm_g, v_final_norm_g.reshape(1, D)))

    def outs(small, win, pool, wout):
        n1, ps, lbl, rg, fg = small
        return n1, win, pool, ps, lbl, rg, wout, fg.reshape(D)

    return (loss.reshape(()), grad_x[None],
            *outs(g_s, g_win, g_pool, g_wout), *outs(d_s, d_win, d_pool, d_wout),
            *outs(m_s, m_win, m_pool, m_wout), *outs(v_s, v_win, v_pool, v_wout))
```
